```python
import math
import jax, jax.numpy as jnp
from jax import lax
import numpy as np

D_MODEL = 1024
BATCH = 8
SEQ = 4096
DEPTH = 2

PLE_DIM = 256

A_WIDTH = D_MODEL // 4
A_DK = 64
A_DV = 64
A_HEADS = A_WIDTH // A_DV
A_CHUNK = 16

B_WIDTH = D_MODEL // 4
B_CH = 64
B_GROUPS = B_WIDTH // B_CH
B_CHUNK = 128

C_WIDTH = D_MODEL // 2
C_NOPE = 64
C_ROPE = 32
C_V = 64
C_HEADS = C_WIDTH // C_V
C_Q_RANK = 384
C_KV_RANK = 256
Q_BLOCK = 128
ROPE_THETA = 10000.0

MIX_WIDTH = A_WIDTH + B_WIDTH + C_WIDTH
IN_SIZES = (A_WIDTH, A_WIDTH, A_WIDTH, A_WIDTH, B_WIDTH, B_WIDTH, C_Q_RANK, C_KV_RANK, C_ROPE)
IN_COLS = sum(IN_SIZES)
IN_SPLITS = tuple(int(s) for s in np.cumsum(IN_SIZES)[:-1])

D_FF = int(math.ceil(8 * D_MODEL / 3 / 256)) * 256
LN_EPS = 1e-5
RMS_EPS = 1e-6
DEEPNORM_ALPHA = (2 * DEPTH) ** 0.25
DEEPNORM_BETA = (8 * DEPTH) ** -0.25

kernel_name = "hybrid_hgrn2_sgu_mla_deepnorm"


def layer_norm(x, g, b):
    xf = x.astype(jnp.float32)
    mu = jnp.mean(xf, -1, keepdims=True)
    var = jnp.mean(jnp.square(xf - mu), -1, keepdims=True)
    return ((xf - mu) * lax.rsqrt(var + LN_EPS)).astype(x.dtype) * g + b


def rms_norm(x, g):
    xf = x.astype(jnp.float32)
    y = xf * lax.rsqrt(jnp.mean(xf * xf, -1, keepdims=True) + RMS_EPS)
    return y.astype(x.dtype) * g


def rope(x, cos, sin):
    x1, x2 = jnp.split(x, 2, axis=-1)
    return jnp.concatenate([x1 * cos - x2 * sin, x2 * cos + x1 * sin], axis=-1).astype(x.dtype)


def hgrn2_mixer(q, f_logit, i_in, g, lb, norm_g):
    bsz, s, _ = q.shape
    n = s // A_CHUNK
    f32 = jnp.float32

    def heads(t):
        return t.astype(f32).reshape(bsz, n, A_CHUNK, A_HEADS, -1)

    lbf = lb.astype(f32)
    f = lbf + (1.0 - lbf) * jax.nn.sigmoid(f_logit.astype(f32))
    qh = heads(jax.nn.silu(q.astype(f32)))
    kh = heads(1.0 - f)
    vh = heads(i_in)
    bcum = jnp.cumsum(heads(jnp.log(f)), axis=2)

    causal = jnp.tril(jnp.ones((A_CHUNK, A_CHUNK), dtype=bool))[None, None, :, :, None, None]
    diff = bcum[:, :, :, None] - bcum[:, :, None, :]
    decay = jnp.exp(jnp.where(causal, diff, -jnp.inf))
    scores = jnp.einsum('bnthk,bnshk,bntshk->bnhts', qh, kh, decay)
    o_intra = jnp.einsum('bnhts,bnshv->bnthv', scores, vh)

    b_last = bcum[:, :, -1]
    k_to_end = kh * jnp.exp(b_last[:, :, None] - bcum)
    chunk_kv = jnp.einsum('bnshk,bnshv->bnhkv', k_to_end, vh)
    chunk_decay = jnp.exp(b_last)

    def step(state, xs):
        dec, kv = xs
        return dec[..., None] * state + kv, state

    init = jnp.zeros((bsz, A_HEADS, A_DK, A_DV), f32)
    _, prev_states = lax.scan(step, init, (jnp.moveaxis(chunk_decay, 1, 0), jnp.moveaxis(chunk_kv, 1, 0)))
    prev_states = jnp.moveaxis(prev_states, 0, 1)
    o_inter = jnp.einsum('bnthk,bnhkv->bnthv', qh * jnp.exp(bcum), prev_states)

    o = (o_intra + o_inter).reshape(bsz, s, A_HEADS, A_DV)
    o = o * lax.rsqrt(jnp.mean(o * o, -1, keepdims=True) + RMS_EPS)
    o = o.reshape(bsz, s, A_WIDTH) * norm_g.astype(f32) * jax.nn.silu(g.astype(f32))
    return o.astype(g.dtype)


def sgu_mixer(u, v, ln_g, ln_b, w_s, b_s):
    bsz, s, _ = u.shape
    n = s // B_CHUNK
    u = jax.nn.gelu(u, approximate=False)
    v = layer_norm(jax.nn.gelu(v, approximate=False), ln_g, ln_b)
    vh = v.reshape(bsz, n, B_CHUNK, B_GROUPS, B_CH)
    w = w_s * jnp.tril(jnp.ones((B_CHUNK, B_CHUNK), dtype=w_s.dtype))
    z = jnp.einsum('gts,bnsgc->bntgc', w, vh) + b_s.T[None, None, :, :, None]
    return u * z.reshape(bsz, s, B_WIDTH)


def mla_mixer(c_q, c_kv, k_rope_raw, cos, sin, q_norm_g, w_uq, kv_norm_g, w_ukv):
    bsz, s, _ = c_q.shape
    q = (rms_norm(c_q, q_norm_g) @ w_uq).reshape(bsz, s, C_HEADS, C_NOPE + C_ROPE)
    q_nope = q[..., :C_NOPE]
    q_rope = rope(q[..., C_NOPE:], cos[:, :, None], sin[:, :, None])
    kv = (rms_norm(c_kv, kv_norm_g) @ w_ukv).reshape(bsz, s, C_HEADS, C_NOPE + C_V)
    k_nope, v = kv[..., :C_NOPE], kv[..., C_NOPE:]
    k_rope = rope(k_rope_raw, cos, sin)

    nb = s // Q_BLOCK
    scale = (C_NOPE + C_ROPE) ** -0.5
    key_idx = jnp.arange(s)

    def blocks(t):
        return jnp.moveaxis(t.reshape(bsz, nb, Q_BLOCK, *t.shape[2:]), 1, 0)

    def attend(args):
        qn, qr, blk = args
        sc = jnp.einsum('bqhd,bkhd->bhqk', qn, k_nope) + jnp.einsum('bqhr,bkr->bhqk', qr, k_rope)
        sc = sc.astype(jnp.float32) * scale
        q_idx = blk * Q_BLOCK + jnp.arange(Q_BLOCK)
        sc = jnp.where(key_idx[None, :] <= q_idx[:, None], sc, -jnp.inf)
        pr = jax.nn.softmax(sc, axis=-1).astype(v.dtype)
        return jnp.einsum('bhqk,bkhv->bqhv', pr, v)

    out = lax.map(attend, (blocks(q_nope), blocks(q_rope), jnp.arange(nb)))
    return jnp.moveaxis(out, 0, 1).reshape(bsz, s, C_WIDTH)


def _fwd_setup_inputs(seed: int = 0) -> dict:
    key = jax.random.key(seed)
    ks = jax.random.split(key, 32)
    f32 = jnp.float32

    def nrm(k, shape, scale):
        return jax.random.normal(k, shape, f32) * scale

    def gain(k, shape):
        return 1.0 + 0.02 * jax.random.normal(k, shape, f32)

    positions = (jnp.arange(SEQ, dtype=jnp.int32)[None, :]
                 + jax.random.randint(ks[2], (BATCH, 1), 0, 64, dtype=jnp.int32))
    return {
        "x": nrm(ks[0], (BATCH, SEQ, D_MODEL), 1.0),
        "p": nrm(ks[1], (DEPTH, BATCH, SEQ, PLE_DIM), 1.0),
        "positions": positions,
        "ln_in_g": gain(ks[3], (D_MODEL,)),
        "ln_in_b": nrm(ks[4], (D_MODEL,), 0.02),
        "w_in": nrm(ks[5], (DEPTH, D_MODEL, IN_COLS), D_MODEL ** -0.5),
        "hgrn_lb_logits": nrm(ks[6], (DEPTH, A_WIDTH), 0.5),
        "hgrn_norm_g": gain(ks[7], (DEPTH, A_WIDTH)),
        "sgu_ln_g": gain(ks[8], (DEPTH, B_WIDTH)),
        "sgu_ln_b": nrm(ks[9], (DEPTH, B_WIDTH), 0.02),
        "sgu_w_s": nrm(ks[10], (DEPTH, B_GROUPS, B_CHUNK, B_CHUNK), B_CHUNK ** -0.5),
        "sgu_b_s": gain(ks[11], (DEPTH, B_GROUPS, B_CHUNK)),
        "mla_q_norm_g": gain(ks[12], (DEPTH, C_Q_RANK)),
        "mla_w_uq": nrm(ks[13], (DEPTH, C_Q_RANK, C_HEADS * (C_NOPE + C_ROPE)), C_Q_RANK ** -0.5),
        "mla_kv_norm_g": gain(ks[14], (DEPTH, C_KV_RANK)),
        "mla_w_ukv": nrm(ks[15], (DEPTH, C_KV_RANK, C_HEADS * (C_NOPE + C_V)), C_KV_RANK ** -0.5),
        "w_out": nrm(ks[16], (DEPTH, MIX_WIDTH, D_MODEL), DEEPNORM_BETA * MIX_WIDTH ** -0.5),
        "ln1_g": gain(ks[17], (DEPTH, D_MODEL)),
        "ln1_b": nrm(ks[18], (DEPTH, D_MODEL), 0.02),
        "w_gate_up": nrm(ks[19], (DEPTH, D_MODEL, 2 * D_FF), D_MODEL ** -0.5),
        "w_down": nrm(ks[20], (DEPTH, D_FF, D_MODEL), DEEPNORM_BETA * D_FF ** -0.5),
        "ple_w_gate": nrm(ks[21], (DEPTH, D_MODEL, D_MODEL), D_MODEL ** -0.5),
        "ple_w_proj": nrm(ks[22], (DEPTH, PLE_DIM, D_MODEL), DEEPNORM_BETA * PLE_DIM ** -0.5),
        "ln2_g": gain(ks[23], (DEPTH, D_MODEL)),
        "ln2_b": nrm(ks[24], (DEPTH, D_MODEL), 0.02),
    }


def _fwd_reference(x, p, positions, ln_in_g, ln_in_b, w_in, hgrn_lb_logits, hgrn_norm_g,
              sgu_ln_g, sgu_ln_b, sgu_w_s, sgu_b_s, mla_q_norm_g, mla_w_uq,
              mla_kv_norm_g, mla_w_ukv, w_out, ln1_g, ln1_b, w_gate_up, w_down,
              ple_w_gate, ple_w_proj, ln2_g, ln2_b):
    lb_cum = jnp.cumsum(jax.nn.softmax(hgrn_lb_logits.astype(jnp.float32), axis=0), axis=0)
    lower_bounds = lb_cum - lb_cum[0]

    inv_freq = ROPE_THETA ** (-jnp.arange(0, C_ROPE, 2, dtype=jnp.float32) / C_ROPE)
    ang = positions.astype(jnp.float32)[..., None] * inv_freq
    cos, sin = jnp.cos(ang), jnp.sin(ang)

    h = layer_norm(x, ln_in_g, ln_in_b)
    for i in range(DEPTH):
        proj = h @ w_in[i]
        a_q, a_f, a_i, a_g, b_u, b_v, c_q, c_kv, c_kr = jnp.split(proj, IN_SPLITS, axis=-1)
        o_a = hgrn2_mixer(a_q, a_f, a_i, a_g, lower_bounds[i], hgrn_norm_g[i])
        o_b = sgu_mixer(b_u, b_v, sgu_ln_g[i], sgu_ln_b[i], sgu_w_s[i], sgu_b_s[i])
        o_c = mla_mixer(c_q, c_kv, c_kr, cos, sin, mla_q_norm_g[i], mla_w_uq[i],
                        mla_kv_norm_g[i], mla_w_ukv[i])
        mix = jnp.concatenate([o_a, o_b, o_c], axis=-1) @ w_out[i]
        h = layer_norm(DEEPNORM_ALPHA * h + mix, ln1_g[i], ln1_b[i])

        gate, up = jnp.split(h @ w_gate_up[i], 2, axis=-1)
        ffn = (jax.nn.silu(gate) * up) @ w_down[i]
        ple = jax.nn.sigmoid(h @ ple_w_gate[i]) * (p[i] @ ple_w_proj[i])
        h = layer_norm(DEEPNORM_ALPHA * h + ffn + ple, ln2_g[i], ln2_b[i])
    return h


import jax as _jax
import jax.numpy as _jnp

TWIN_FORMAT = 'train_step'
FWD_PARAMS = ['x', 'p', 'positions', 'ln_in_g', 'ln_in_b', 'w_in', 'hgrn_lb_logits', 'hgrn_norm_g', 'sgu_ln_g', 'sgu_ln_b', 'sgu_w_s', 'sgu_b_s', 'mla_q_norm_g', 'mla_w_uq', 'mla_kv_norm_g', 'mla_w_ukv', 'w_out', 'ln1_g', 'ln1_b', 'w_gate_up', 'w_down', 'ple_w_gate', 'ple_w_proj', 'ln2_g', 'ln2_b']
TWIN_WEIGHTS = ['ln_in_g', 'ln_in_b', 'w_in', 'hgrn_lb_logits', 'hgrn_norm_g', 'sgu_ln_g', 'sgu_ln_b', 'sgu_w_s', 'sgu_b_s', 'mla_q_norm_g', 'mla_w_uq', 'mla_kv_norm_g', 'mla_w_ukv', 'w_out', 'ln1_g', 'ln1_b', 'w_gate_up', 'w_down', 'ple_w_gate', 'ple_w_proj', 'ln2_g', 'ln2_b']
TWIN_DIFF_INPUT = 'x'
TWIN_INPUTS = ['x', 'p', 'positions', 'ln_in_g', 'ln_in_b', 'w_in', 'hgrn_lb_logits', 'hgrn_norm_g', 'sgu_ln_g', 'sgu_ln_b', 'sgu_w_s', 'sgu_b_s', 'mla_q_norm_g', 'mla_w_uq', 'mla_kv_norm_g', 'mla_w_ukv', 'w_out', 'ln1_g', 'ln1_b', 'w_gate_up', 'w_down', 'ple_w_gate', 'ple_w_proj', 'ln2_g', 'ln2_b', 'loss_target', 'm_ln_in_g', 'm_ln_in_b', 'm_w_in', 'm_hgrn_lb_logits', 'm_hgrn_norm_g', 'm_sgu_ln_g', 'm_sgu_ln_b', 'm_sgu_w_s', 'm_sgu_b_s', 'm_mla_q_norm_g', 'm_mla_w_uq', 'm_mla_kv_norm_g', 'm_mla_w_ukv', 'm_w_out', 'm_ln1_g', 'm_ln1_b', 'm_w_gate_up', 'm_w_down', 'm_ple_w_gate', 'm_ple_w_proj', 'm_ln2_g', 'm_ln2_b', 'v_ln_in_g', 'v_ln_in_b', 'v_w_in', 'v_hgrn_lb_logits', 'v_hgrn_norm_g', 'v_sgu_ln_g', 'v_sgu_ln_b', 'v_sgu_w_s', 'v_sgu_b_s', 'v_mla_q_norm_g', 'v_mla_w_uq', 'v_mla_kv_norm_g', 'v_mla_w_ukv', 'v_w_out', 'v_ln1_g', 'v_ln1_b', 'v_w_gate_up', 'v_w_down', 'v_ple_w_gate', 'v_ple_w_proj', 'v_ln2_g', 'v_ln2_b']
TWIN_OUTPUTS = ['loss', 'grad_x', 'grad_ln_in_g', 'grad_ln_in_b', 'grad_w_in', 'grad_hgrn_lb_logits', 'grad_hgrn_norm_g', 'grad_sgu_ln_g', 'grad_sgu_ln_b', 'grad_sgu_w_s', 'grad_sgu_b_s', 'grad_mla_q_norm_g', 'grad_mla_w_uq', 'grad_mla_kv_norm_g', 'grad_mla_w_ukv', 'grad_w_out', 'grad_ln1_g', 'grad_ln1_b', 'grad_w_gate_up', 'grad_w_down', 'grad_ple_w_gate', 'grad_ple_w_proj', 'grad_ln2_g', 'grad_ln2_b', 'delta_ln_in_g', 'delta_ln_in_b', 'delta_w_in', 'delta_hgrn_lb_logits', 'delta_hgrn_norm_g', 'delta_sgu_ln_g', 'delta_sgu_ln_b', 'delta_sgu_w_s', 'delta_sgu_b_s', 'delta_mla_q_norm_g', 'delta_mla_w_uq', 'delta_mla_kv_norm_g', 'delta_mla_w_ukv', 'delta_w_out', 'delta_ln1_g', 'delta_ln1_b', 'delta_w_gate_up', 'delta_w_down', 'delta_ple_w_gate', 'delta_ple_w_proj', 'delta_ln2_g', 'delta_ln2_b', 'new_m_ln_in_g', 'new_m_ln_in_b', 'new_m_w_in', 'new_m_hgrn_lb_logits', 'new_m_hgrn_norm_g', 'new_m_sgu_ln_g', 'new_m_sgu_ln_b', 'new_m_sgu_w_s', 'new_m_sgu_b_s', 'new_m_mla_q_norm_g', 'new_m_mla_w_uq', 'new_m_mla_kv_norm_g', 'new_m_mla_w_ukv', 'new_m_w_out', 'new_m_ln1_g', 'new_m_ln1_b', 'new_m_w_gate_up', 'new_m_w_down', 'new_m_ple_w_gate', 'new_m_ple_w_proj', 'new_m_ln2_g', 'new_m_ln2_b', 'new_v_ln_in_g', 'new_v_ln_in_b', 'new_v_w_in', 'new_v_hgrn_lb_logits', 'new_v_hgrn_norm_g', 'new_v_sgu_ln_g', 'new_v_sgu_ln_b', 'new_v_sgu_w_s', 'new_v_sgu_b_s', 'new_v_mla_q_norm_g', 'new_v_mla_w_uq', 'new_v_mla_kv_norm_g', 'new_v_mla_w_ukv', 'new_v_w_out', 'new_v_ln1_g', 'new_v_ln1_b', 'new_v_w_gate_up', 'new_v_w_down', 'new_v_ple_w_gate', 'new_v_ple_w_proj', 'new_v_ln2_g', 'new_v_ln2_b']
TWIN_LEAF_KINDS = {'loss': 'loss', 'grad_x': 'grad_x', 'grad_ln_in_g': 'grad_w', 'grad_ln_in_b': 'grad_w', 'grad_w_in': 'grad_w', 'grad_hgrn_lb_logits': 'grad_w', 'grad_hgrn_norm_g': 'grad_w', 'grad_sgu_ln_g': 'grad_w', 'grad_sgu_ln_b': 'grad_w', 'grad_sgu_w_s': 'grad_w', 'grad_sgu_b_s': 'grad_w', 'grad_mla_q_norm_g': 'grad_w', 'grad_mla_w_uq': 'grad_w', 'grad_mla_kv_norm_g': 'grad_w', 'grad_mla_w_ukv': 'grad_w', 'grad_w_out': 'grad_w', 'grad_ln1_g': 'grad_w', 'grad_ln1_b': 'grad_w', 'grad_w_gate_up': 'grad_w', 'grad_w_down': 'grad_w', 'grad_ple_w_gate': 'grad_w', 'grad_ple_w_proj': 'grad_w', 'grad_ln2_g': 'grad_w', 'grad_ln2_b': 'grad_w', 'delta_ln_in_g': 'delta_w', 'delta_ln_in_b': 'delta_w', 'delta_w_in': 'delta_w', 'delta_hgrn_lb_logits': 'delta_w', 'delta_hgrn_norm_g': 'delta_w', 'delta_sgu_ln_g': 'delta_w', 'delta_sgu_ln_b': 'delta_w', 'delta_sgu_w_s': 'delta_w', 'delta_sgu_b_s': 'delta_w', 'delta_mla_q_norm_g': 'delta_w', 'delta_mla_w_uq': 'delta_w', 'delta_mla_kv_norm_g': 'delta_w', 'delta_mla_w_ukv': 'delta_w', 'delta_w_out': 'delta_w', 'delta_ln1_g': 'delta_w', 'delta_ln1_b': 'delta_w', 'delta_w_gate_up': 'delta_w', 'delta_w_down': 'delta_w', 'delta_ple_w_gate': 'delta_w', 'delta_ple_w_proj': 'delta_w', 'delta_ln2_g': 'delta_w', 'delta_ln2_b': 'delta_w', 'new_m_ln_in_g': 'new_m', 'new_m_ln_in_b': 'new_m', 'new_m_w_in': 'new_m', 'new_m_hgrn_lb_logits': 'new_m', 'new_m_hgrn_norm_g': 'new_m', 'new_m_sgu_ln_g': 'new_m', 'new_m_sgu_ln_b': 'new_m', 'new_m_sgu_w_s': 'new_m', 'new_m_sgu_b_s': 'new_m', 'new_m_mla_q_norm_g': 'new_m', 'new_m_mla_w_uq': 'new_m', 'new_m_mla_kv_norm_g': 'new_m', 'new_m_mla_w_ukv': 'new_m', 'new_m_w_out': 'new_m', 'new_m_ln1_g': 'new_m', 'new_m_ln1_b': 'new_m', 'new_m_w_gate_up': 'new_m', 'new_m_w_down': 'new_m', 'new_m_ple_w_gate': 'new_m', 'new_m_ple_w_proj': 'new_m', 'new_m_ln2_g': 'new_m', 'new_m_ln2_b': 'new_m', 'new_v_ln_in_g': 'new_v', 'new_v_ln_in_b': 'new_v', 'new_v_w_in': 'new_v', 'new_v_hgrn_lb_logits': 'new_v', 'new_v_hgrn_norm_g': 'new_v', 'new_v_sgu_ln_g': 'new_v', 'new_v_sgu_ln_b': 'new_v', 'new_v_sgu_w_s': 'new_v', 'new_v_sgu_b_s': 'new_v', 'new_v_mla_q_norm_g': 'new_v', 'new_v_mla_w_uq': 'new_v', 'new_v_mla_kv_norm_g': 'new_v', 'new_v_mla_w_ukv': 'new_v', 'new_v_w_out': 'new_v', 'new_v_ln1_g': 'new_v', 'new_v_ln1_b': 'new_v', 'new_v_w_gate_up': 'new_v', 'new_v_w_down': 'new_v', 'new_v_ple_w_gate': 'new_v', 'new_v_ple_w_proj': 'new_v', 'new_v_ln2_g': 'new_v', 'new_v_ln2_b': 'new_v'}


def _forward(args):
    return _fwd_reference(*[args[k] for k in FWD_PARAMS])


def _output_shape():
    def fwd():
        inp = _fwd_setup_inputs(0)
        return _fwd_reference(*[inp[k] for k in FWD_PARAMS])
    out = _jax.eval_shape(fwd)
    return out.shape, out.dtype

N_MICROBATCH = 1
ADAM_LR = 0.001
ADAM_B1 = 0.9
ADAM_B2 = 0.999
ADAM_EPS = 1e-08
ADAM_WD = 0.01
ADAM_STEP = 10
PER_EXAMPLE_BATCH_AXIS = {'x': 0, 'p': 1, 'positions': 0, 'loss_target': 0}
SHARED_INPUTS = []
_WEIGHT_DTYPES = {'ln_in_g': _jnp.float32, 'ln_in_b': _jnp.float32, 'w_in': _jnp.float32, 'hgrn_lb_logits': _jnp.float32, 'hgrn_norm_g': _jnp.float32, 'sgu_ln_g': _jnp.float32, 'sgu_ln_b': _jnp.float32, 'sgu_w_s': _jnp.float32, 'sgu_b_s': _jnp.float32, 'mla_q_norm_g': _jnp.float32, 'mla_w_uq': _jnp.float32, 'mla_kv_norm_g': _jnp.float32, 'mla_w_ukv': _jnp.float32, 'w_out': _jnp.float32, 'ln1_g': _jnp.float32, 'ln1_b': _jnp.float32, 'w_gate_up': _jnp.float32, 'w_down': _jnp.float32, 'ple_w_gate': _jnp.float32, 'ple_w_proj': _jnp.float32, 'ln2_g': _jnp.float32, 'ln2_b': _jnp.float32}
MOMENT_SCALE = {'ln_in_g': 9.286425e-01, 'ln_in_b': 5.818284e-01, 'w_in': 2.994559e-02, 'hgrn_lb_logits': 4.693446e-03, 'hgrn_norm_g': 3.809359e-02, 'sgu_ln_g': 2.901861e-02, 'sgu_ln_b': 2.820228e-02, 'sgu_w_s': 2.057604e-02, 'sgu_b_s': 3.049368e-02, 'mla_q_norm_g': 1.353076e-02, 'mla_w_uq': 9.380981e-03, 'mla_kv_norm_g': 2.552082e-02, 'mla_w_ukv': 1.226551e-02, 'w_out': 7.546578e-02, 'ln1_g': 1.036604e+00, 'ln1_b': 5.701470e-01, 'w_gate_up': 2.228177e-02, 'w_down': 7.281564e-02, 'ple_w_gate': 1.289986e-02, 'ple_w_proj': 6.573387e-02, 'ln2_g': 2.266919e+01, 'ln2_b': 1.793861e+00}


def _to_microbatches(a, axis):
    t = _jnp.moveaxis(a, axis, 0)
    t = t.reshape((N_MICROBATCH, t.shape[0] // N_MICROBATCH) + t.shape[1:])
    return _jnp.moveaxis(t, 1, axis + 1)


def setup_inputs(seed: int = 0) -> dict:
    inp = _fwd_setup_inputs(seed)
    key = _jax.random.fold_in(_jax.random.key(seed), 7919)
    shape, _ = _output_shape()
    out = dict(inp)
    out["loss_target"] = _jax.random.normal(_jax.random.fold_in(key, 0), shape, _jnp.float32)
    for i, name in enumerate(TWIN_WEIGHTS):
        w = inp[name].astype(_jnp.float32)
        if MOMENT_SCALE is None:
            s = _jnp.sqrt(_jnp.mean(_jnp.square(w)) + 1e-30)
        else:
            s = MOMENT_SCALE[name]
        km, kv = _jax.random.split(_jax.random.fold_in(key, i + 1))
        out[name] = w
        out["m_" + name] = s * _jax.random.normal(km, w.shape, _jnp.float32)
        out["v_" + name] = (s * s) * _jax.random.uniform(kv, w.shape, _jnp.float32, 0.5, 1.5)
    if N_MICROBATCH > 1:
        for name, axis in PER_EXAMPLE_BATCH_AXIS.items():
            out[name] = _to_microbatches(out[name], axis)
    return {'x': out['x'], 'p': out['p'], 'positions': out['positions'], 'ln_in_g': out['ln_in_g'], 'ln_in_b': out['ln_in_b'], 'w_in': out['w_in'], 'hgrn_lb_logits': out['hgrn_lb_logits'], 'hgrn_norm_g': out['hgrn_norm_g'], 'sgu_ln_g': out['sgu_ln_g'], 'sgu_ln_b': out['sgu_ln_b'], 'sgu_w_s': out['sgu_w_s'], 'sgu_b_s': out['sgu_b_s'], 'mla_q_norm_g': out['mla_q_norm_g'], 'mla_w_uq': out['mla_w_uq'], 'mla_kv_norm_g': out['mla_kv_norm_g'], 'mla_w_ukv': out['mla_w_ukv'], 'w_out': out['w_out'], 'ln1_g': out['ln1_g'], 'ln1_b': out['ln1_b'], 'w_gate_up': out['w_gate_up'], 'w_down': out['w_down'], 'ple_w_gate': out['ple_w_gate'], 'ple_w_proj': out['ple_w_proj'], 'ln2_g': out['ln2_g'], 'ln2_b': out['ln2_b'], 'loss_target': out['loss_target'], 'm_ln_in_g': out['m_ln_in_g'], 'm_ln_in_b': out['m_ln_in_b'], 'm_w_in': out['m_w_in'], 'm_hgrn_lb_logits': out['m_hgrn_lb_logits'], 'm_hgrn_norm_g': out['m_hgrn_norm_g'], 'm_sgu_ln_g': out['m_sgu_ln_g'], 'm_sgu_ln_b': out['m_sgu_ln_b'], 'm_sgu_w_s': out['m_sgu_w_s'], 'm_sgu_b_s': out['m_sgu_b_s'], 'm_mla_q_norm_g': out['m_mla_q_norm_g'], 'm_mla_w_uq': out['m_mla_w_uq'], 'm_mla_kv_norm_g': out['m_mla_kv_norm_g'], 'm_mla_w_ukv': out['m_mla_w_ukv'], 'm_w_out': out['m_w_out'], 'm_ln1_g': out['m_ln1_g'], 'm_ln1_b': out['m_ln1_b'], 'm_w_gate_up': out['m_w_gate_up'], 'm_w_down': out['m_w_down'], 'm_ple_w_gate': out['m_ple_w_gate'], 'm_ple_w_proj': out['m_ple_w_proj'], 'm_ln2_g': out['m_ln2_g'], 'm_ln2_b': out['m_ln2_b'], 'v_ln_in_g': out['v_ln_in_g'], 'v_ln_in_b': out['v_ln_in_b'], 'v_w_in': out['v_w_in'], 'v_hgrn_lb_logits': out['v_hgrn_lb_logits'], 'v_hgrn_norm_g': out['v_hgrn_norm_g'], 'v_sgu_ln_g': out['v_sgu_ln_g'], 'v_sgu_ln_b': out['v_sgu_ln_b'], 'v_sgu_w_s': out['v_sgu_w_s'], 'v_sgu_b_s': out['v_sgu_b_s'], 'v_mla_q_norm_g': out['v_mla_q_norm_g'], 'v_mla_w_uq': out['v_mla_w_uq'], 'v_mla_kv_norm_g': out['v_mla_kv_norm_g'], 'v_mla_w_ukv': out['v_mla_w_ukv'], 'v_w_out': out['v_w_out'], 'v_ln1_g': out['v_ln1_g'], 'v_ln1_b': out['v_ln1_b'], 'v_w_gate_up': out['v_w_gate_up'], 'v_w_down': out['v_w_down'], 'v_ple_w_gate': out['v_ple_w_gate'], 'v_ple_w_proj': out['v_ple_w_proj'], 'v_ln2_g': out['v_ln2_g'], 'v_ln2_b': out['v_ln2_b']}


def _loss(weights, diff, rest, loss_target):
    with _jax.named_scope("forward"):
        args = {**rest, TWIN_DIFF_INPUT: diff, **{k: w.astype(_WEIGHT_DTYPES[k]) for k, w in weights.items()}}
        y = _forward(args)
    with _jax.named_scope("loss_head"):
        err = _jnp.square(y.astype(_jnp.float32) - loss_target)
        return 0.5 * _jnp.sum(_jnp.mean(err, axis=-1)) if err.ndim else 0.5 * err


def _adamw(w, g, m, v):
    m = ADAM_B1 * m + (1.0 - ADAM_B1) * g
    v = ADAM_B2 * v + (1.0 - ADAM_B2) * _jnp.square(g)
    m_hat = m / (1.0 - ADAM_B1 ** ADAM_STEP)
    v_hat = v / (1.0 - ADAM_B2 ** ADAM_STEP)
    delta = -ADAM_LR * (m_hat / (_jnp.sqrt(v_hat) + ADAM_EPS) + ADAM_WD * w)
    return delta, m, v


def reference(x, p, positions, ln_in_g, ln_in_b, w_in, hgrn_lb_logits, hgrn_norm_g, sgu_ln_g, sgu_ln_b, sgu_w_s, sgu_b_s, mla_q_norm_g, mla_w_uq, mla_kv_norm_g, mla_w_ukv, w_out, ln1_g, ln1_b, w_gate_up, w_down, ple_w_gate, ple_w_proj, ln2_g, ln2_b, loss_target, m_ln_in_g, m_ln_in_b, m_w_in, m_hgrn_lb_logits, m_hgrn_norm_g, m_sgu_ln_g, m_sgu_ln_b, m_sgu_w_s, m_sgu_b_s, m_mla_q_norm_g, m_mla_w_uq, m_mla_kv_norm_g, m_mla_w_ukv, m_w_out, m_ln1_g, m_ln1_b, m_w_gate_up, m_w_down, m_ple_w_gate, m_ple_w_proj, m_ln2_g, m_ln2_b, v_ln_in_g, v_ln_in_b, v_w_in, v_hgrn_lb_logits, v_hgrn_norm_g, v_sgu_ln_g, v_sgu_ln_b, v_sgu_w_s, v_sgu_b_s, v_mla_q_norm_g, v_mla_w_uq, v_mla_kv_norm_g, v_mla_w_ukv, v_w_out, v_ln1_g, v_ln1_b, v_w_gate_up, v_w_down, v_ple_w_gate, v_ple_w_proj, v_ln2_g, v_ln2_b):
    given = dict(x=x, p=p, positions=positions, ln_in_g=ln_in_g, ln_in_b=ln_in_b, w_in=w_in, hgrn_lb_logits=hgrn_lb_logits, hgrn_norm_g=hgrn_norm_g, sgu_ln_g=sgu_ln_g, sgu_ln_b=sgu_ln_b, sgu_w_s=sgu_w_s, sgu_b_s=sgu_b_s, mla_q_norm_g=mla_q_norm_g, mla_w_uq=mla_w_uq, mla_kv_norm_g=mla_kv_norm_g, mla_w_ukv=mla_w_ukv, w_out=w_out, ln1_g=ln1_g, ln1_b=ln1_b, w_gate_up=w_gate_up, w_down=w_down, ple_w_gate=ple_w_gate, ple_w_proj=ple_w_proj, ln2_g=ln2_g, ln2_b=ln2_b, loss_target=loss_target, m_ln_in_g=m_ln_in_g, m_ln_in_b=m_ln_in_b, m_w_in=m_w_in, m_hgrn_lb_logits=m_hgrn_lb_logits, m_hgrn_norm_g=m_hgrn_norm_g, m_sgu_ln_g=m_sgu_ln_g, m_sgu_ln_b=m_sgu_ln_b, m_sgu_w_s=m_sgu_w_s, m_sgu_b_s=m_sgu_b_s, m_mla_q_norm_g=m_mla_q_norm_g, m_mla_w_uq=m_mla_w_uq, m_mla_kv_norm_g=m_mla_kv_norm_g, m_mla_w_ukv=m_mla_w_ukv, m_w_out=m_w_out, m_ln1_g=m_ln1_g, m_ln1_b=m_ln1_b, m_w_gate_up=m_w_gate_up, m_w_down=m_w_down, m_ple_w_gate=m_ple_w_gate, m_ple_w_proj=m_ple_w_proj, m_ln2_g=m_ln2_g, m_ln2_b=m_ln2_b, v_ln_in_g=v_ln_in_g, v_ln_in_b=v_ln_in_b, v_w_in=v_w_in, v_hgrn_lb_logits=v_hgrn_lb_logits, v_hgrn_norm_g=v_hgrn_norm_g, v_sgu_ln_g=v_sgu_ln_g, v_sgu_ln_b=v_sgu_ln_b, v_sgu_w_s=v_sgu_w_s, v_sgu_b_s=v_sgu_b_s, v_mla_q_norm_g=v_mla_q_norm_g, v_mla_w_uq=v_mla_w_uq, v_mla_kv_norm_g=v_mla_kv_norm_g, v_mla_w_ukv=v_mla_w_ukv, v_w_out=v_w_out, v_ln1_g=v_ln1_g, v_ln1_b=v_ln1_b, v_w_gate_up=v_w_gate_up, v_w_down=v_w_down, v_ple_w_gate=v_ple_w_gate, v_ple_w_proj=v_ple_w_proj, v_ln2_g=v_ln2_g, v_ln2_b=v_ln2_b)
    weights = {n: given[n] for n in TWIN_WEIGHTS}
    shared = {n: given[n] for n in SHARED_INPUTS}
    per_example = {n: given[n] for n in ['x', 'p', 'positions']}
    grad_fn = _jax.value_and_grad(_loss, argnums=(0, 1))

    def one_microbatch(ex, loss_target):
        ex = dict(ex)
        diff = ex.pop(TWIN_DIFF_INPUT)
        return grad_fn(weights, diff, {**shared, **ex}, loss_target)

    if N_MICROBATCH == 1:
        loss, (grad_w, grad_x) = one_microbatch(per_example, given["loss_target"])
    else:
        def body(carry, xs):
            loss_sum, grad_sum = carry
            l_k, (gw_k, gx_k) = one_microbatch(xs[0], xs[1])
            with _jax.named_scope("update"):
                return (loss_sum + l_k, _jax.tree.map(_jnp.add, grad_sum, gw_k)), gx_k

        init = (_jnp.zeros((), _jnp.float32), _jax.tree.map(_jnp.zeros_like, weights))
        (loss, grad_w), grad_x = _jax.lax.scan(body, init, (per_example, given["loss_target"]))
    with _jax.named_scope("update"):
        delta_w, new_m, new_v = {}, {}, {}
        for n in TWIN_WEIGHTS:
            delta_w[n], new_m[n], new_v[n] = _adamw(weights[n], grad_w[n], given["m_" + n], given["v_" + n])
    return (loss, grad_x, *[grad_w[n] for n in TWIN_WEIGHTS], *[delta_w[n] for n in TWIN_WEIGHTS],
            *[new_m[n] for n in TWIN_WEIGHTS], *[new_v[n] for n in TWIN_WEIGHTS])
```

```python
import functools
import math

import jax
import jax.numpy as jnp
from jax import lax
from jax.experimental import pallas as pl
from jax.experimental.pallas import tpu as pltpu

F32 = jnp.float32
BF16 = jnp.bfloat16
MESH = pl.DeviceIdType.MESH

LN_EPS = 1e-5
RMS_EPS = 1e-6
ROPE_THETA = 10000.0
ADAM_LR, ADAM_B1, ADAM_B2, ADAM_EPS, ADAM_WD, ADAM_STEP = 0.001, 0.9, 0.999, 1e-08, 0.01, 10

N_DEV = 8
LANES = 128
HG_CHUNK = 16
HG_W = 256
HEAD = 64
SGU_CHUNK = 128
N_ATT_HEADS = 8
ATT_D = 96
VMEM_LIMIT = 56 * 1024 * 1024

HG_TILE = 256
ATT_TQ = 512
ROW_TILE = 256

P_CQ, P_KR, P_CKV, P_COLS = 1536, 1920, 2048, 2304


def _cparams(sem):
    return pltpu.CompilerParams(dimension_semantics=sem, vmem_limit_bytes=VMEM_LIMIT)


def _tile(n, pref):
    best = None
    t = LANES
    while t <= min(n, pref):
        if n % t == 0:
            best = t
        t += LANES
    return best if best is not None else n


def _mm(a, b, *, ta=False, tb=False, out_dtype=F32, tm=1024, tn=1024, tk=512, name):
    m, k = (a.shape[1], a.shape[0]) if ta else a.shape
    n = b.shape[0] if tb else b.shape[1]
    assert (b.shape[1] if tb else b.shape[0]) == k, (a.shape, b.shape, ta, tb)
    tm, tn, tk = _tile(m, tm), _tile(n, tn), _tile(k, tk)
    nk = k // tk
    dims = (((0 if ta else 1,), (1 if tb else 0,)), ((), ()))

    def body(a_ref, b_ref, o_ref, acc_ref):
        kk = pl.program_id(2)

        @pl.when(kk == 0)
        def _():
            acc_ref[...] = jnp.zeros_like(acc_ref)

        acc_ref[...] += lax.dot_general(a_ref[...].astype(BF16), b_ref[...].astype(BF16), dims,
                                        preferred_element_type=F32)

        @pl.when(kk == nk - 1)
        def _():
            o_ref[...] = acc_ref[...].astype(o_ref.dtype)

    a_spec = (pl.BlockSpec((tk, tm), lambda i, j, kk: (kk, i)) if ta
              else pl.BlockSpec((tm, tk), lambda i, j, kk: (i, kk)))
    b_spec = (pl.BlockSpec((tn, tk), lambda i, j, kk: (j, kk)) if tb
              else pl.BlockSpec((tk, tn), lambda i, j, kk: (kk, j)))
    return pl.pallas_call(
        body, name=name, grid=(m // tm, n // tn, nk),
        in_specs=[a_spec, b_spec],
        out_specs=pl.BlockSpec((tm, tn), lambda i, j, kk: (i, j)),
        out_shape=jax.ShapeDtypeStruct((m, n), out_dtype),
        scratch_shapes=[pltpu.VMEM((tm, tn), F32)],
        compiler_params=_cparams(("parallel", "parallel", "arbitrary")),
    )(a, b)


def _row_operand(a, tile):
    if isinstance(a, tuple):
        arr, w, j = a
        return arr, pl.BlockSpec((tile, w), lambda i, j=j: (i, j))
    return a, pl.BlockSpec((tile, a.shape[1]), lambda i: (i, 0))


def _const_spec(c):
    nd = c.ndim
    return pl.BlockSpec(c.shape, lambda i, nd=nd: (0,) * nd)


def _rowwise(fn, rows, consts, outs, *, name, accs=(), tile=None):
    t_rows = (rows[0][0] if isinstance(rows[0], tuple) else rows[0]).shape[0]
    tile = min(tile or ROW_TILE, t_rows)
    arrs, specs = zip(*[_row_operand(a, tile) for a in rows])
    nin, no = len(rows) + len(consts), len(outs)

    def body(*refs):
        res = fn(*[r[...] for r in refs[:nin]])
        for r, v in zip(refs[nin:nin + no], res[:no]):
            r[...] = v.astype(r.dtype)
        if accs:
            a_refs = refs[nin + no:]

            @pl.when(pl.program_id(0) == 0)
            def _():
                for r in a_refs:
                    r[...] = jnp.zeros_like(r)

            for r, v in zip(a_refs, res[no:]):
                r[...] += v

    out_shape = [jax.ShapeDtypeStruct((t_rows, w), dt) for w, dt in outs]
    out_shape += [jax.ShapeDtypeStruct(s, F32) for s in accs]
    out_specs = [pl.BlockSpec((tile, w), lambda i: (i, 0)) for w, _ in outs]
    out_specs += [pl.BlockSpec(s, lambda i, nd=len(s): (0,) * nd) for s in accs]
    return pl.pallas_call(
        body, name=name, grid=(t_rows // tile,),
        in_specs=list(specs) + [_const_spec(c) for c in consts],
        out_specs=out_specs, out_shape=out_shape,
        compiler_params=_cparams(("arbitrary",)),
    )(*arrs, *consts)


def _rowwise_vjp(fn, rows, consts, cts, *, name, groups, tile=None):
    t_rows = (rows[0][0] if isinstance(rows[0], tuple) else rows[0]).shape[0]
    tile = min(tile or ROW_TILE, t_rows)
    arrs, specs = zip(*[_row_operand(a, tile) for a in rows])
    flat_cts = [c for group in cts for c in group]
    ct_arrs, ct_specs = zip(*[_row_operand(a, tile) for a in flat_cts])
    nr, nc, nct, ng = len(rows), len(consts), len(flat_cts), len(groups)

    def width(a):
        return a[1] if isinstance(a, tuple) else a.shape[1]

    def body(*refs):
        rv = [r[...] for r in refs[:nr]]
        cv = [r[...] for r in refs[nr:nr + nc]]
        ct_refs = refs[nr + nc:nr + nc + nct]
        ctv, pos = [], 0
        for group in cts:
            s = ct_refs[pos][...]
            for r in ct_refs[pos + 1:pos + len(group)]:
                s = s + r[...]
            ctv.append(s)
            pos += len(group)
        _, pull = jax.vjp(fn, *rv, *cv)
        grads = pull(tuple(ctv))
        g_refs = refs[nr + nc + nct:nr + nc + nct + ng]
        for r, idx in zip(g_refs, groups):
            parts = [grads[i] for i in idx]
            r[...] = parts[0] if len(parts) == 1 else jnp.concatenate(parts, axis=1)
        c_refs = refs[nr + nc + nct + ng:]

        @pl.when(pl.program_id(0) == 0)
        def _():
            for r in c_refs:
                r[...] = jnp.zeros_like(r)

        for r, v in zip(c_refs, grads[nr:]):
            r[...] += v

    gw = [sum(width(rows[i]) for i in idx) for idx in groups]
    out_shape = [jax.ShapeDtypeStruct((t_rows, w), F32) for w in gw]
    out_shape += [jax.ShapeDtypeStruct(c.shape, F32) for c in consts]
    out_specs = [pl.BlockSpec((tile, w), lambda i: (i, 0)) for w in gw]
    out_specs += [_const_spec(c) for c in consts]
    return pl.pallas_call(
        body, name=name, grid=(t_rows // tile,),
        in_specs=list(specs) + [_const_spec(c) for c in consts] + list(ct_specs),
        out_specs=out_specs, out_shape=out_shape,
        compiler_params=_cparams(("arbitrary",)),
    )(*arrs, *consts, *ct_arrs)


def _layer_norm(x, g, b):
    mu = jnp.mean(x, axis=-1, keepdims=True)
    xc = x - mu
    var = jnp.mean(xc * xc, axis=-1, keepdims=True)
    return xc * lax.rsqrt(var + LN_EPS) * g + b


def _sigmoid(x):
    return 1.0 / (1.0 + jnp.exp(-x))


def _fn_ln(x, g, b):
    return (_layer_norm(x, g, b),)


def _fn_rms(x, g):
    return (x * lax.rsqrt(jnp.mean(x * x, axis=-1, keepdims=True) + RMS_EPS) * g,)


def _fn_swiglu(gate, up):
    return (gate * _sigmoid(gate) * up,)


def _make_post_mix(alpha):
    def fn(h, mix, g, b):
        return (_layer_norm(alpha * h + mix, g, b),)
    return fn


def _make_ple_ln(alpha):
    def fn(h1, ffn, pg, pp, g, b):
        return (_layer_norm(alpha * h1 + ffn + _sigmoid(pg) * pp, g, b),)
    return fn


def _fn_lower_bounds(l0, l1):
    m = jnp.maximum(l0, l1)
    e0, e1 = jnp.exp(l0 - m), jnp.exp(l1 - m)
    s = e0 + e1
    p0, p1 = e0 / s, e1 / s
    return (p0 - p0, (p0 + p1) - p0)


def _loss_and_grad(y, target, *, name):
    d = y.shape[1]

    def fn(yv, tv):
        err = yv - tv
        return err * (1.0 / d), 0.5 * jnp.sum(jnp.mean(err * err, axis=-1, keepdims=True), axis=0, keepdims=True)

    return _rowwise(fn, [y, target], [], [(d, F32)], accs=[(1, 1)], name=name)


def _split_dot(x, e_bf16):
    hi = x.astype(BF16)
    lo = (x - hi.astype(F32)).astype(BF16)
    return (jnp.dot(hi, e_bf16, preferred_element_type=F32) + jnp.dot(lo, e_bf16, preferred_element_type=F32))


def _hgrn_common(th):
    rm = lax.broadcasted_iota(jnp.int32, (th, HG_W), 0) % HG_CHUNK

    def seg_cumsum(x):
        for s in (1, 2, 4, 8):
            x = x + jnp.where(rm >= s, pltpu.roll(x, s, 0), 0.0)
        return x

    def seg_rcumsum(x):
        for s in (1, 2, 4, 8):
            x = x + jnp.where(rm < HG_CHUNK - s, pltpu.roll(x, th - s, 0), 0.0)
        return x

    ri = lax.broadcasted_iota(jnp.int32, (HG_W, HG_W), 0) // HEAD
    ci = lax.broadcasted_iota(jnp.int32, (HG_W, HG_W), 1) // HEAD
    head_f32 = (ri == ci).astype(F32)
    head_bf16 = head_f32.astype(BF16)

    def headsum(x):
        return _split_dot(x, head_bf16)

    return rm, seg_cumsum, seg_rcumsum, head_f32, headsum


def _hgrn_gates(qr, fl, lb):
    sg = _sigmoid(fl)
    f = lb + (1.0 - lb) * sg
    sq = _sigmoid(qr)
    return sg, f, jnp.log(f), 1.0 - f, qr * sq, sq


def _shifted(x, d, th):
    return x if d == 0 else pltpu.roll(x, d, 0)


def _unshift(x, d, th):
    return x if d == 0 else pltpu.roll(x, th - d, 0)


def _hgrn_fwd(projp, lb, ng, *, name):
    t_rows = projp.shape[0]
    th = min(HG_TILE, t_rows)
    nct = th // HG_CHUNK

    def body(q_ref, f_ref, i_ref, g_ref, lb_ref, ng_ref, oa_ref, opre_ref, st_out_ref,
             st_ref, vtm_ref, kv_ref, qe_ref, dec_ref, oint_ref):
        rm, seg_cumsum, seg_rcumsum, head_f32, headsum = _hgrn_common(th)

        @pl.when(pl.program_id(0) == 0)
        def _():
            st_ref[...] = jnp.zeros_like(st_ref)

        qr, fl, v, g = q_ref[...], f_ref[...], i_ref[...], g_ref[...]
        _, f, lf, k, q, _ = _hgrn_gates(qr, fl, lb_ref[...])
        b = seg_cumsum(lf)

        o = jnp.zeros((th, HG_W), F32)
        for d in range(HG_CHUNK):
            kd, bd, vd = _shifted(k, d, th), _shifted(b, d, th), _shifted(v, d, th)
            e = jnp.exp(jnp.where(rm >= d, b - bd, -1e30))
            o = o + headsum(q * kd * e) * vd

        blast = seg_rcumsum(jnp.where(rm == HG_CHUNK - 1, b, 0.0))
        kte = (k * jnp.exp(blast - b)).astype(BF16)
        qe_ref[...] = q * jnp.exp(b)
        dec_ref[...] = jnp.exp(blast)
        vt = v.T
        lane_chunk = lax.broadcasted_iota(jnp.int32, (HG_W, th), 1) // HG_CHUNK
        for c in range(nct):
            vtm_ref[c * HG_W:(c + 1) * HG_W, :] = jnp.where(lane_chunk == c, vt, 0.0).astype(BF16)
        kv_ref[...] = jnp.dot(vtm_ref[...], kte, preferred_element_type=F32)

        def step(c, carry):
            r0 = pl.multiple_of(c * HG_CHUNK, HG_CHUNK)
            s = st_ref[...]
            st_out_ref[c] = s
            oint_ref[pl.ds(r0, HG_CHUNK), :] = lax.dot_general(
                qe_ref[pl.ds(r0, HG_CHUNK), :].astype(BF16), s.astype(BF16),
                (((1,), (1,)), ((), ())), preferred_element_type=F32)
            dec = jnp.max(dec_ref[pl.ds(r0, HG_CHUNK), :], axis=0, keepdims=True)
            kv_c = kv_ref[pl.ds(pl.multiple_of(c * HG_W, HG_W), HG_W), :]
            st_ref[...] = s * dec + kv_c * head_f32
            return carry

        lax.fori_loop(0, nct, step, 0)

        o = o + oint_ref[...]
        opre_ref[...] = o
        r = lax.rsqrt(headsum(o * o) * (1.0 / HEAD) + RMS_EPS)
        oa_ref[...] = o * r * ng_ref[...] * (g * _sigmoid(g))

    col = lambda j: pl.BlockSpec((th, HG_W), lambda i, j=j: (i, j))
    vec = pl.BlockSpec((1, HG_W), lambda i: (0, 0))
    row = pl.BlockSpec((th, HG_W), lambda i: (i, 0))
    n_chunks = t_rows // HG_CHUNK
    return pl.pallas_call(
        body, name=name, grid=(t_rows // th,),
        in_specs=[col(0), col(1), col(2), col(3), vec, vec],
        out_specs=[row, row, pl.BlockSpec((nct, HG_W, HG_W), lambda i: (i, 0, 0))],
        out_shape=[jax.ShapeDtypeStruct((t_rows, HG_W), F32), jax.ShapeDtypeStruct((t_rows, HG_W), F32),
                   jax.ShapeDtypeStruct((n_chunks, HG_W, HG_W), F32)],
        scratch_shapes=[pltpu.VMEM((HG_W, HG_W), F32), pltpu.VMEM((nct * HG_W, th), BF16),
                        pltpu.VMEM((nct * HG_W, HG_W), F32), pltpu.VMEM((th, HG_W), F32),
                        pltpu.VMEM((th, HG_W), F32), pltpu.VMEM((th, HG_W), F32)],
        compiler_params=_cparams(("arbitrary",)),
    )(projp, projp, projp, projp, lb, ng)


def _hgrn_bwd(projp, lb, ng, opre, states, dcat, *, name):
    t_rows = projp.shape[0]
    th = min(HG_TILE, t_rows)
    nct = th // HG_CHUNK
    nt = t_rows // th

    def body(q_ref, f_ref, i_ref, g_ref, lb_ref, ng_ref, opre_ref, st_in_ref, do_ref,
             dproj_ref, dng_ref, dlb_ref,
             gst_ref, dotm_ref, qg_ref, v_ref, kte_ref, dop_ref, dec_ref, dkte_ref, dvi_ref, dqe_ref, ddec_ref):
        rm, seg_cumsum, seg_rcumsum, head_f32, headsum = _hgrn_common(th)

        @pl.when(pl.program_id(0) == 0)
        def _():
            gst_ref[...] = jnp.zeros_like(gst_ref)
            dng_ref[...] = jnp.zeros_like(dng_ref)
            dlb_ref[...] = jnp.zeros_like(dlb_ref)

        qr, fl, v, g = q_ref[...], f_ref[...], i_ref[...], g_ref[...]
        lb, ngv = lb_ref[...], ng_ref[...]
        sg, f, lf, k, q, sq = _hgrn_gates(qr, fl, lb)
        b = seg_cumsum(lf)
        blast = seg_rcumsum(jnp.where(rm == HG_CHUNK - 1, b, 0.0))
        eb = jnp.exp(b)
        ekb = jnp.exp(blast - b)
        qe, kte, dec = q * eb, k * ekb, jnp.exp(blast)

        do_out, op = do_ref[...], opre_ref[...]
        sgg = _sigmoid(g)
        sil = g * sgg
        r = lax.rsqrt(headsum(op * op) * (1.0 / HEAD) + RMS_EPS)
        on = op * r
        dng_ref[...] += jnp.sum(do_out * on * sil, axis=0, keepdims=True)
        dg = do_out * on * ngv * (sgg * (1.0 + g * (1.0 - sgg)))
        don = do_out * ngv * sil
        dop = r * (don - on * (headsum(don * on) * (1.0 / HEAD)))

        v_ref[...] = v
        kte_ref[...] = kte
        dop_ref[...] = dop
        dec_ref[...] = dec
        dot_t = dop.T
        lane_chunk = lax.broadcasted_iota(jnp.int32, (HG_W, th), 1) // HG_CHUNK
        for c in range(nct):
            dotm_ref[c * HG_W:(c + 1) * HG_W, :] = jnp.where(lane_chunk == c, dot_t, 0.0).astype(BF16)
        qg_ref[...] = jnp.dot(dotm_ref[...], qe.astype(BF16), preferred_element_type=F32)

        def step(j, carry):
            c = nct - 1 - j
            r0 = pl.multiple_of(c * HG_CHUNK, HG_CHUNK)
            rows = pl.ds(r0, HG_CHUNK)
            gs = gst_ref[...]
            s = st_in_ref[c]
            gm = (gs * head_f32).astype(BF16)
            dkte_ref[rows, :] = jnp.dot(v_ref[rows, :].astype(BF16), gm, preferred_element_type=F32)
            dvi_ref[rows, :] = lax.dot_general(kte_ref[rows, :].astype(BF16), gm, (((1,), (1,)), ((), ())),
                                               preferred_element_type=F32)
            dqe_ref[rows, :] = jnp.dot(dop_ref[rows, :].astype(BF16), s.astype(BF16), preferred_element_type=F32)
            ddec_ref[rows, :] = jnp.broadcast_to(jnp.sum(gs * s, axis=0, keepdims=True), (HG_CHUNK, HG_W))
            dec_c = jnp.max(dec_ref[rows, :], axis=0, keepdims=True)
            qg_c = qg_ref[pl.ds(pl.multiple_of(c * HG_W, HG_W), HG_W), :]
            gst_ref[...] = gs * dec_c + qg_c * head_f32
            return carry

        lax.fori_loop(0, nct, step, 0)

        dkte, dqe = dkte_ref[...], dqe_ref[...]
        dq = dqe * eb
        dk = dkte * ekb
        db = dqe * qe - dkte * kte
        dv = dvi_ref[...]
        dblast = dkte * kte + jnp.where(rm == HG_CHUNK - 1, ddec_ref[...] * dec, 0.0)

        for d in range(HG_CHUNK):
            kd, bd, vd = _shifted(k, d, th), _shifted(b, d, th), _shifted(v, d, th)
            e = jnp.exp(jnp.where(rm >= d, b - bd, -1e30))
            p = q * kd * e
            sc = headsum(p)
            dsc = headsum(dop * vd)
            dv = dv + _unshift(sc * dop, d, th)
            dq = dq + dsc * kd * e
            dk = dk + _unshift(dsc * q * e, d, th)
            darg = dsc * p
            db = db + darg - _unshift(darg, d, th)

        db = db + jnp.where(rm == HG_CHUNK - 1, seg_cumsum(dblast), 0.0)
        dlf = seg_rcumsum(db)
        df = dlf / f - dk
        dlb_ref[...] += jnp.sum(df * (1.0 - sg), axis=0, keepdims=True)
        dfl = df * (1.0 - lb) * sg * (1.0 - sg)
        dqr = dq * (sq * (1.0 + qr * (1.0 - sq)))
        dproj_ref[...] = jnp.concatenate([dqr, dfl, dv, dg], axis=1)

    rev = lambda i: nt - 1 - i
    col = lambda j: pl.BlockSpec((th, HG_W), lambda i, j=j: (rev(i), j))
    vec = pl.BlockSpec((1, HG_W), lambda i: (0, 0))
    row = pl.BlockSpec((th, HG_W), lambda i: (rev(i), 0))
    tile_f32 = pltpu.VMEM((th, HG_W), F32)
    return pl.pallas_call(
        body, name=name, grid=(nt,),
        in_specs=[col(0), col(1), col(2), col(3), vec, vec, row,
                  pl.BlockSpec((nct, HG_W, HG_W), lambda i: (rev(i), 0, 0)), col(0)],
        out_specs=[pl.BlockSpec((th, 4 * HG_W), lambda i: (rev(i), 0)), vec, vec],
        out_shape=[jax.ShapeDtypeStruct((t_rows, 4 * HG_W), F32), jax.ShapeDtypeStruct((1, HG_W), F32),
                   jax.ShapeDtypeStruct((1, HG_W), F32)],
        scratch_shapes=[pltpu.VMEM((HG_W, HG_W), F32), pltpu.VMEM((nct * HG_W, th), BF16),
                        pltpu.VMEM((nct * HG_W, HG_W), F32)] + [tile_f32] * 8,
        compiler_params=_cparams(("arbitrary",)),
    )(projp, projp, projp, projp, lb, ng, opre, states, dcat)


_INV_SQRT2 = 1.0 / math.sqrt(2.0)
_INV_SQRT2PI = 1.0 / math.sqrt(2.0 * math.pi)


def _gelu(x):
    return 0.5 * x * (1.0 + lax.erf(x * _INV_SQRT2))


def _gelu_grad(x):
    return 0.5 * (1.0 + lax.erf(x * _INV_SQRT2)) + x * jnp.exp(-0.5 * x * x) * _INV_SQRT2PI


def _sgu_parts(bu, bv, lg, lbias, w_ref, n_groups):
    c = SGU_CHUNK
    tril = (lax.broadcasted_iota(jnp.int32, (c, c), 0) >= lax.broadcasted_iota(jnp.int32, (c, c), 1)).astype(F32)
    gid = lax.broadcasted_iota(jnp.int32, bu.shape, 1) // HEAD
    u = _gelu(bu)
    gv = _gelu(bv)
    mu = jnp.mean(gv, axis=-1, keepdims=True)
    xc = gv - mu
    rstd = lax.rsqrt(jnp.mean(xc * xc, axis=-1, keepdims=True) + LN_EPS)
    xhat = xc * rstd
    vn = xhat * lg + lbias
    ws = [w_ref[gi] * tril for gi in range(n_groups)]
    return tril, gid, u, rstd, xhat, vn, ws


def _sgu_fwd(projp, lg, lbias, w_s, bias_full, *, name):
    t_rows = projp.shape[0]
    n_groups = w_s.shape[0]
    c = SGU_CHUNK

    def body(u_ref, v_ref, lg_ref, lb_ref, w_ref, bias_ref, o_ref):
        _, gid, u, _, _, vn, ws = _sgu_parts(u_ref[...], v_ref[...], lg_ref[...], lb_ref[...], w_ref, n_groups)
        vnb = vn.astype(BF16)
        z = bias_ref[...]
        for gi in range(n_groups):
            z = z + jnp.where(gid == gi, jnp.dot(ws[gi].astype(BF16), vnb, preferred_element_type=F32), 0.0)
        o_ref[...] = u * z

    col = lambda j: pl.BlockSpec((c, HG_W), lambda i, j=j: (i, j))
    return pl.pallas_call(
        body, name=name, grid=(t_rows // c,),
        in_specs=[col(4), col(5), _const_spec(lg), _const_spec(lbias), _const_spec(w_s), _const_spec(bias_full)],
        out_specs=pl.BlockSpec((c, HG_W), lambda i: (i, 0)),
        out_shape=jax.ShapeDtypeStruct((t_rows, HG_W), F32),
        compiler_params=_cparams(("arbitrary",)),
    )(projp, projp, lg, lbias, w_s, bias_full)


def _sgu_bwd(projp, lg, lbias, w_s, bias_full, dcat, *, name):
    t_rows = projp.shape[0]
    n_groups = w_s.shape[0]
    c = SGU_CHUNK
    n = t_rows // c

    def body(u_ref, v_ref, lg_ref, lb_ref, w_ref, bias_ref, do_ref,
             dproj_ref, dlg_ref, dlb_ref, dw_ref, dbs_ref, dbias_acc):
        i = pl.program_id(0)

        @pl.when(i == 0)
        def _():
            dlg_ref[...] = jnp.zeros_like(dlg_ref)
            dlb_ref[...] = jnp.zeros_like(dlb_ref)
            dw_ref[...] = jnp.zeros_like(dw_ref)
            dbias_acc[...] = jnp.zeros_like(dbias_acc)

        bu, bv, lg_v = u_ref[...], v_ref[...], lg_ref[...]
        tril, gid, u, rstd, xhat, vn, ws = _sgu_parts(bu, bv, lg_v, lb_ref[...], w_ref, n_groups)
        vnb = vn.astype(BF16)
        z = bias_ref[...]
        for gi in range(n_groups):
            z = z + jnp.where(gid == gi, jnp.dot(ws[gi].astype(BF16), vnb, preferred_element_type=F32), 0.0)
        do = do_ref[...]
        dbu = do * z * _gelu_grad(bu)
        dz = do * u
        dbias_acc[...] += dz
        dvn = jnp.zeros_like(dz)
        for gi in range(n_groups):
            dzg = jnp.where(gid == gi, dz, 0.0).astype(BF16)
            dw_ref[gi] += lax.dot_general(dzg, vnb, (((1,), (1,)), ((), ())), preferred_element_type=F32) * tril
            dvn = dvn + jnp.dot(ws[gi].T.astype(BF16), dzg, preferred_element_type=F32)
        dlg_ref[...] += jnp.sum(dvn * xhat, axis=0, keepdims=True)
        dlb_ref[...] += jnp.sum(dvn, axis=0, keepdims=True)
        dxh = dvn * lg_v
        dgv = rstd * (dxh - jnp.mean(dxh, axis=-1, keepdims=True)
                      - xhat * jnp.mean(dxh * xhat, axis=-1, keepdims=True))
        dproj_ref[...] = jnp.concatenate([dbu, dgv * _gelu_grad(bv)], axis=1)

        @pl.when(i == n - 1)
        def _():
            dbs_ref[...] = jnp.sum(dbias_acc[...].T.reshape(n_groups, HEAD, c), axis=1)

    col = lambda j: pl.BlockSpec((c, HG_W), lambda i, j=j: (i, j))
    return pl.pallas_call(
        body, name=name, grid=(n,),
        in_specs=[col(4), col(5), _const_spec(lg), _const_spec(lbias), _const_spec(w_s), _const_spec(bias_full),
                  col(1)],
        out_specs=[pl.BlockSpec((c, 2 * HG_W), lambda i: (i, 0)), _const_spec(lg), _const_spec(lbias),
                   _const_spec(w_s), pl.BlockSpec((n_groups, c), lambda i: (0, 0))],
        out_shape=[jax.ShapeDtypeStruct((t_rows, 2 * HG_W), F32), jax.ShapeDtypeStruct(lg.shape, F32),
                   jax.ShapeDtypeStruct(lbias.shape, F32), jax.ShapeDtypeStruct(w_s.shape, F32),
                   jax.ShapeDtypeStruct((n_groups, c), F32)],
        scratch_shapes=[pltpu.VMEM((c, HG_W), F32)],
        compiler_params=_cparams(("arbitrary",)),
    )(projp, projp, lg, lbias, w_s, bias_full, dcat)


def _rope_tables(positions):
    t = positions.shape[0]
    inv_freq = ROPE_THETA ** (-jnp.arange(0, 32, 2, dtype=F32) / 32)
    ang = positions.astype(F32)[:, None] * inv_freq
    cos, sin = jnp.cos(ang), jnp.sin(ang)
    z = lambda w: jnp.zeros((t, w), F32)
    cos_t = jnp.concatenate([jnp.ones((t, 64), F32), cos, cos, z(32)], axis=1)
    sin_up = jnp.concatenate([z(80), sin, z(32)], axis=1)
    sin_dn = jnp.concatenate([z(64), -sin, z(48)], axis=1)
    return cos_t, sin_up, sin_dn


def _rep(x, n):
    return x if n == 1 else jnp.concatenate([x] * n, axis=1)


def _rope(x, cos_t, sin_up, sin_dn):
    w = x.shape[1]
    return x * cos_t + pltpu.roll(x, 16, 1) * sin_up + pltpu.roll(x, w - 16, 1) * sin_dn


def _rope_t(dy, cos_t, sin_up, sin_dn):
    w = dy.shape[1]
    return dy * cos_t + pltpu.roll(dy * sin_up, w - 16, 1) + pltpu.roll(dy * sin_dn, 16, 1)


def _mla_prep(qpad, kvpad, projp, tables, *, name):
    nh = N_ATT_HEADS

    def fn(q, kv, kr, cos_t, sin_up, sin_dn):
        qr = _rope(q, _rep(cos_t, nh), _rep(sin_up, nh), _rep(sin_dn, nh))
        krr = _rope(kr, cos_t, sin_up, sin_dn)
        add = _rep(jnp.concatenate([krr, jnp.zeros_like(krr)], axis=1), nh)
        return qr, kv + add

    return _rowwise(fn, [qpad, kvpad, (projp, LANES, P_KR // LANES)] + list(tables), [],
                    [(qpad.shape[1], BF16), (kvpad.shape[1], BF16)], name=name)


def _mla_prep_bwd(dqr, dkv, tables, *, name):
    nh = N_ATT_HEADS

    def fn(dq, dk, cos_t, sin_up, sin_dn):
        dqp = _rope_t(dq, _rep(cos_t, nh), _rep(sin_up, nh), _rep(sin_dn, nh))
        dkrr = dk[:, 0:LANES]
        for h in range(1, nh):
            dkrr = dkrr + dk[:, 2 * LANES * h:2 * LANES * h + LANES]
        return dqp, _rope_t(dkrr, cos_t, sin_up, sin_dn)

    return _rowwise(fn, [dqr, dkv] + list(tables), [], [(dqr.shape[1], F32), (LANES, F32)], name=name)


def _att_mask(qi, ki, tq):
    row = qi * tq + lax.broadcasted_iota(jnp.int32, (tq, tq), 0)
    colv = ki * tq + lax.broadcasted_iota(jnp.int32, (tq, tq), 1)
    return colv <= row


def _attn_fwd(qr, kvf, *, name):
    t_rows = qr.shape[0]
    tq = min(ATT_TQ, t_rows)
    nb = t_rows // tq
    scale = ATT_D ** -0.5

    def body(q_ref, kv_ref, o_ref, lse_ref, m_ref, l_ref, acc_ref):
        qi, ki = pl.program_id(1), pl.program_id(2)

        @pl.when(ki == 0)
        def _():
            m_ref[...] = jnp.full_like(m_ref, -1e30)
            l_ref[...] = jnp.zeros_like(l_ref)
            acc_ref[...] = jnp.zeros_like(acc_ref)

        @pl.when(ki <= qi)
        def _():
            kv = kv_ref[...]
            s = lax.dot_general(q_ref[...], kv[:, :LANES], (((1,), (1,)), ((), ())),
                                preferred_element_type=F32) * scale
            s = jnp.where(_att_mask(qi, ki, tq), s, -1e30)
            m_old = m_ref[...]
            m_new = jnp.maximum(m_old, jnp.max(s, axis=-1, keepdims=True))
            p = jnp.exp(s - m_new)
            a = jnp.exp(m_old - m_new)
            l_ref[...] = a * l_ref[...] + jnp.sum(p, axis=-1, keepdims=True)
            acc_ref[...] = a * acc_ref[...] + jnp.dot(p.astype(BF16), kv[:, LANES:], preferred_element_type=F32)
            m_ref[...] = m_new

        @pl.when(ki == nb - 1)
        def _():
            o_ref[...] = acc_ref[...] / l_ref[...]
            lse_ref[...] = m_ref[...] + jnp.log(l_ref[...])

    return pl.pallas_call(
        body, name=name, grid=(N_ATT_HEADS, nb, nb),
        in_specs=[pl.BlockSpec((tq, LANES), lambda h, qi, ki: (qi, h)),
                  pl.BlockSpec((tq, 2 * LANES), lambda h, qi, ki: (jnp.minimum(ki, qi), h))],
        out_specs=[pl.BlockSpec((tq, LANES), lambda h, qi, ki: (qi, h)),
                   pl.BlockSpec((None, tq, 1), lambda h, qi, ki: (h, qi, 0))],
        out_shape=[jax.ShapeDtypeStruct((t_rows, N_ATT_HEADS * LANES), F32),
                   jax.ShapeDtypeStruct((N_ATT_HEADS, t_rows, 1), F32)],
        scratch_shapes=[pltpu.VMEM((tq, 1), F32), pltpu.VMEM((tq, 1), F32), pltpu.VMEM((tq, LANES), F32)],
        compiler_params=_cparams(("parallel", "arbitrary", "arbitrary")),
    )(qr, kvf)


def _attn_bwd(qr, kvf, dcat, opad, lse, *, name):
    t_rows = qr.shape[0]
    tq = min(ATT_TQ, t_rows)
    nb = t_rows // tq
    scale = ATT_D ** -0.5
    do_off = 2 * HG_W // LANES

    def body(q_ref, kv_ref, do_ref, o_ref, lse_ref, dq_ref, dkv_ref, dk_acc, dv_acc):
        ki, qi = pl.program_id(1), pl.program_id(2)

        @pl.when((ki == 0) & (qi == 0))
        def _():
            dq_ref[...] = jnp.zeros_like(dq_ref)

        @pl.when(qi == 0)
        def _():
            dk_acc[...] = jnp.zeros_like(dk_acc)
            dv_acc[...] = jnp.zeros_like(dv_acc)

        @pl.when(qi >= ki)
        def _():
            q, kv, do = q_ref[...], kv_ref[...], do_ref[...]
            k, v = kv[:, :LANES], kv[:, LANES:]
            s = lax.dot_general(q, k, (((1,), (1,)), ((), ())), preferred_element_type=F32) * scale
            s = jnp.where(_att_mask(qi, ki, tq), s, -1e30)
            p = jnp.exp(s - lse_ref[...])
            dob = do.astype(BF16)
            dv_acc[...] += lax.dot_general(p.astype(BF16), dob, (((0,), (0,)), ((), ())),
                                           preferred_element_type=F32)
            dp = lax.dot_general(dob, v, (((1,), (1,)), ((), ())), preferred_element_type=F32)
            delta = jnp.sum(do * o_ref[...], axis=-1, keepdims=True)
            ds = (p * (dp - delta) * scale).astype(BF16)
            dk_acc[...] += lax.dot_general(ds, q, (((0,), (0,)), ((), ())), preferred_element_type=F32)
            rows = pl.ds(pl.multiple_of(qi * tq, tq), tq)
            dq_ref[rows, :] += jnp.dot(ds, k, preferred_element_type=F32)

        @pl.when(qi == nb - 1)
        def _():
            dkv_ref[...] = jnp.concatenate([dk_acc[...], dv_acc[...]], axis=1)

    qblk = lambda h, ki, qi: (jnp.maximum(qi, ki), h)
    return pl.pallas_call(
        body, name=name, grid=(N_ATT_HEADS, nb, nb),
        in_specs=[pl.BlockSpec((tq, LANES), qblk),
                  pl.BlockSpec((tq, 2 * LANES), lambda h, ki, qi: (ki, h)),
                  pl.BlockSpec((tq, LANES), lambda h, ki, qi: (jnp.maximum(qi, ki), do_off + h)),
                  pl.BlockSpec((tq, LANES), qblk),
                  pl.BlockSpec((None, tq, 1), lambda h, ki, qi: (h, jnp.maximum(qi, ki), 0))],
        out_specs=[pl.BlockSpec((t_rows, LANES), lambda h, ki, qi: (0, h)),
                   pl.BlockSpec((tq, 2 * LANES), lambda h, ki, qi: (ki, h))],
        out_shape=[jax.ShapeDtypeStruct((t_rows, N_ATT_HEADS * LANES), F32),
                   jax.ShapeDtypeStruct((t_rows, N_ATT_HEADS * 2 * LANES), F32)],
        scratch_shapes=[pltpu.VMEM((tq, LANES), F32), pltpu.VMEM((tq, LANES), F32)],
        compiler_params=_cparams(("parallel", "arbitrary", "arbitrary")),
    )(qr, kvf, dcat, opad, lse)


def _my_pos():
    return lax.axis_index("x"), lax.axis_index("y"), lax.axis_index("c")


def _all_gather(x, *, name):
    def body(x_ref, out_ref, send_sems, recv_sems, local_sem):
        mx, my, mc = _my_pos()
        me, sibling = (mx, my, mc), (mx, my, 1 - mc)
        chips = [(1 - mx, my), (mx, 1 - my), (1 - mx, 1 - my)]

        def blk(px, py, pc):
            return out_ref.at[4 * px + 2 * py + pc]

        def copy(k, block, to, src=None):
            return pltpu.make_async_remote_copy(
                src_ref=blk(*block) if src is None else src, dst_ref=blk(*block),
                send_sem=send_sems.at[k], recv_sem=recv_sems.at[k], device_id=to, device_id_type=MESH)

        mine = pltpu.make_async_copy(x_ref, blk(*me), local_sem)
        mine.start()
        first = [copy(0, me, sibling, src=x_ref)]
        first += [copy(1 + j, me, (*chip, mc), src=x_ref) for j, chip in enumerate(chips)]
        for cp in first:
            cp.start()
        passed = [copy(4 + j, (*chip, mc), sibling) for j, chip in enumerate(chips)]
        for j, chip in enumerate(chips):
            copy(1 + j, (*chip, mc), me).wait_recv()
            passed[j].start()
        copy(0, sibling, me).wait_recv()
        for j, chip in enumerate(chips):
            copy(4 + j, (*chip, 1 - mc), me).wait_recv()
        for cp in first + passed:
            cp.wait_send()
        mine.wait()

    return pl.pallas_call(
        body, name=name,
        out_shape=jax.ShapeDtypeStruct((N_DEV,) + x.shape, x.dtype),
        in_specs=[pl.BlockSpec(memory_space=pl.ANY)],
        out_specs=pl.BlockSpec(memory_space=pl.ANY),
        scratch_shapes=[pltpu.SemaphoreType.DMA((7,)), pltpu.SemaphoreType.DMA((7,)), pltpu.SemaphoreType.DMA(())],
    )(x)


def _exchange(x, over, *, name):
    def body(x_ref, out_ref, send_sems, recv_sems, local_sem):
        mx, my, mc = _my_pos()
        if over == "c":
            mine, peers = mc, [((mx, my, 1 - mc), 1 - mc)]
        else:
            mine = 2 * mx + my
            peers = [((1 - mx, my, mc), 2 * (1 - mx) + my), ((mx, 1 - my, mc), 2 * mx + 1 - my),
                     ((1 - mx, 1 - my, mc), 2 * (1 - mx) + 1 - my)]

        def copy(k, dev, src_blk, dst_blk):
            return pltpu.make_async_remote_copy(
                src_ref=x_ref.at[src_blk], dst_ref=out_ref.at[dst_blk],
                send_sem=send_sems.at[k], recv_sem=recv_sems.at[k], device_id=dev, device_id_type=MESH)

        local = pltpu.make_async_copy(x_ref.at[mine], out_ref.at[mine], local_sem)
        local.start()
        sends = [copy(k, dev, g, mine) for k, (dev, g) in enumerate(peers)]
        for cp in sends:
            cp.start()
        for k, (dev, g) in enumerate(peers):
            copy(k, dev, g, g).wait_recv()
        for cp in sends:
            cp.wait_send()
        local.wait()

    n = 1 if over == "c" else 3
    return pl.pallas_call(
        body, name=name,
        out_shape=jax.ShapeDtypeStruct(x.shape, x.dtype),
        in_specs=[pl.BlockSpec(memory_space=pl.ANY)],
        out_specs=pl.BlockSpec(memory_space=pl.ANY),
        scratch_shapes=[pltpu.SemaphoreType.DMA((n,)), pltpu.SemaphoreType.DMA((n,)), pltpu.SemaphoreType.DMA(())],
    )(x)


def _sum_groups(x, *, name, tile=512):
    g, r, c = x.shape
    tile = _row_tile(r, tile)

    def body(x_ref, o_ref):
        s = x_ref[0]
        for j in range(1, g):
            s = s + x_ref[j]
        o_ref[...] = s

    return pl.pallas_call(
        body, name=name, grid=(r // tile,),
        in_specs=[pl.BlockSpec((g, tile, c), lambda i: (0, i, 0))],
        out_specs=pl.BlockSpec((tile, c), lambda i: (i, 0)),
        out_shape=jax.ShapeDtypeStruct((r, c), x.dtype),
        compiler_params=_cparams(("parallel",)),
    )(x)


def _row_tile(r, pref):
    t = min(pref, r)
    while r % t or (t % 8 and t != r):
        t -= 1
    return t


def _adamw(parts, w, m, v, *, name, tile=512):
    g, r, c = parts.shape
    tile = _row_tile(r, tile)
    c1 = 1.0 / (1.0 - ADAM_B1 ** ADAM_STEP)
    c2 = 1.0 / (1.0 - ADAM_B2 ** ADAM_STEP)

    def body(p_ref, w_ref, m_ref, v_ref, g_ref, d_ref, mo_ref, vo_ref):
        grad = p_ref[0]
        for j in range(1, g):
            grad = grad + p_ref[j]
        mn = ADAM_B1 * m_ref[...] + (1.0 - ADAM_B1) * grad
        vn = ADAM_B2 * v_ref[...] + (1.0 - ADAM_B2) * (grad * grad)
        g_ref[...] = grad
        mo_ref[...] = mn
        vo_ref[...] = vn
        d_ref[...] = -ADAM_LR * ((mn * c1) / (jnp.sqrt(vn * c2) + ADAM_EPS) + ADAM_WD * w_ref[...])

    slab = pl.BlockSpec((tile, c), lambda i: (i, 0))
    return pl.pallas_call(
        body, name=name, grid=(r // tile,),
        in_specs=[pl.BlockSpec((g, tile, c), lambda i: (0, i, 0)), slab, slab, slab],
        out_specs=[slab] * 4,
        out_shape=[jax.ShapeDtypeStruct((r, c), F32)] * 4,
        compiler_params=_cparams(("parallel",)),
    )(parts, w, m, v)


BIG = (("w_in", 1), ("mla_w_uq", 1), ("mla_w_ukv", 1), ("w_out", 0), ("w_gate_up", 1), ("w_down", 0),
       ("ple_w_gate", 0), ("ple_w_proj", 1))
SMALL = ("ln_in_g", "ln_in_b", "hgrn_lb_logits", "hgrn_norm_g", "sgu_ln_g", "sgu_ln_b", "sgu_w_s", "sgu_b_s",
         "mla_q_norm_g", "mla_kv_norm_g", "ln1_g", "ln1_b", "ln2_g", "ln2_b")
ORDER = ("ln_in_g", "ln_in_b", "w_in", "hgrn_lb_logits", "hgrn_norm_g", "sgu_ln_g", "sgu_ln_b", "sgu_w_s", "sgu_b_s",
         "mla_q_norm_g", "mla_w_uq", "mla_kv_norm_g", "mla_w_ukv", "w_out", "ln1_g", "ln1_b", "w_gate_up", "w_down",
         "ple_w_gate", "ple_w_proj", "ln2_g", "ln2_b")


def _slab(a, align):
    s = a.reshape(-1, LANES)
    pad = -s.shape[0] % align
    return jnp.pad(s, ((0, pad), (0, 0))) if pad else s


def _pack(arrays, align=16, total_align=512):
    s = jnp.concatenate([_slab(a, align) for a in arrays], axis=0)
    pad = -s.shape[0] % total_align
    return jnp.pad(s, ((0, pad), (0, 0))) if pad else s


def _unpack(slab, shapes, align=16):
    out, r0 = [], 0
    for s in shapes:
        nr = math.prod(s) // LANES
        out.append(slab[r0:r0 + nr].reshape(s))
        r0 += nr + (-nr % align)
    return out


def _full_from_gathered(g, local_shape, axis):
    depth, a, b = local_shape
    g = g.reshape(N_DEV, depth, a, b)
    if axis == 1:
        return jnp.transpose(g, (1, 2, 0, 3)).reshape(depth, a, N_DEV * b)
    return jnp.transpose(g, (1, 0, 2, 3)).reshape(depth, N_DEV * a, b)


def _shards_from_full(full, axis):
    depth, a, b = full.shape
    if axis == 1:
        return jnp.transpose(full.reshape(depth, a, N_DEV, b // N_DEV), (2, 0, 1, 3)).reshape(N_DEV, -1)
    return jnp.transpose(full.reshape(depth, N_DEV, a // N_DEV, b), (1, 0, 2, 3)).reshape(N_DEV, -1)


def _pad_w_in(w):
    z = lambda n: jnp.zeros((w.shape[0], n), w.dtype)
    return jnp.concatenate([w[:, :1920], z(64), w[:, 2176:2208], z(32), w[:, 1920:2176]], axis=1)


def _unpad_w_in(g):
    return jnp.concatenate([g[:, :1920], g[:, P_CKV:P_CKV + 256], g[:, P_KR + 64:P_KR + 96]], axis=1)


def _pad_w_uq(w):
    r = w.shape[0]
    return jnp.pad(w.reshape(r, N_ATT_HEADS, ATT_D), ((0, 0), (0, 0), (0, LANES - ATT_D))).reshape(r, -1)


def _unpad_w_uq(g):
    r = g.shape[0]
    return g.reshape(r, N_ATT_HEADS, LANES)[:, :, :ATT_D].reshape(r, -1)


def _pad_w_ukv(w):
    r = w.shape[0]
    w = w.reshape(r, N_ATT_HEADS, 2, HEAD)
    return jnp.pad(w, ((0, 0), (0, 0), (0, 0), (0, HEAD))).reshape(r, -1)


def _unpad_w_ukv(g):
    r = g.shape[0]
    return g.reshape(r, N_ATT_HEADS, 2, LANES)[:, :, :, :HEAD].reshape(r, -1)


def _pad_w_out(w):
    att = jnp.pad(w[2 * HG_W:].reshape(N_ATT_HEADS, HEAD, -1), ((0, 0), (0, HEAD), (0, 0)))
    return jnp.concatenate([w[:2 * HG_W], att.reshape(N_ATT_HEADS * LANES, -1)], axis=0)


def _unpad_w_out(g):
    att = g[2 * HG_W:].reshape(N_ATT_HEADS, LANES, -1)[:, :HEAD].reshape(N_ATT_HEADS * HEAD, -1)
    return jnp.concatenate([g[:2 * HG_W], att], axis=0)


def _layer_forward(li, h, p_i, wts, sm, lbs, tables, alpha):
    n = f"l{li}_"
    row1 = lambda a: a.reshape(1, -1)
    projp = _mm(h, wts["w_in"], name=n + "proj")
    ng = row1(sm["hgrn_norm_g"][li])
    o_a, o_pre, states = _hgrn_fwd(projp, lbs[li], ng, name=n + "hgrn_fwd")
    lg, lbias = row1(sm["sgu_ln_g"][li]), row1(sm["sgu_ln_b"][li])
    w_s = sm["sgu_w_s"][li]
    bias_full = jnp.repeat(sm["sgu_b_s"][li].T, HEAD, axis=1)
    o_b = _sgu_fwd(projp, lg, lbias, w_s, bias_full, name=n + "sgu_fwd")
    qg, kvg = row1(sm["mla_q_norm_g"][li]), row1(sm["mla_kv_norm_g"][li])
    cq_view, ckv_view = (projp, 384, P_CQ // 384), (projp, 256, P_CKV // 256)
    (cqn,) = _rowwise(_fn_rms, [cq_view], [qg], [(384, BF16)], name=n + "q_norm")
    (ckvn,) = _rowwise(_fn_rms, [ckv_view], [kvg], [(256, BF16)], name=n + "kv_norm")
    qpad = _mm(cqn, wts["mla_w_uq"], name=n + "uq")
    kvpad = _mm(ckvn, wts["mla_w_ukv"], name=n + "ukv")
    qr, kvf = _mla_prep(qpad, kvpad, projp, tables, name=n + "mla_prep")
    opad, lse = _attn_fwd(qr, kvf, name=n + "attn_fwd")
    cat = jnp.concatenate([o_a, o_b, opad], axis=1)
    mix = _mm(cat, wts["w_out"], name=n + "out_proj")
    g1, b1 = row1(sm["ln1_g"][li]), row1(sm["ln1_b"][li])
    d = h.shape[1]
    (h1,) = _rowwise(_make_post_mix(alpha), [h, mix], [g1, b1], [(d, F32)], name=n + "ln1")
    gu = _mm(h1, wts["w_gate_up"], tn=512, name=n + "gate_up")
    dff = gu.shape[1] // 2
    (act,) = _rowwise(_fn_swiglu, [(gu, dff, 0), (gu, dff, 1)], [], [(dff, BF16)], name=n + "swiglu")
    ffn = _mm(act, wts["w_down"], tk=1408, name=n + "down")
    pg = _mm(h1, wts["ple_w_gate"], name=n + "ple_gate")
    pp = _mm(p_i, wts["ple_w_proj"], name=n + "ple_proj")
    g2, b2 = row1(sm["ln2_g"][li]), row1(sm["ln2_b"][li])
    (h2,) = _rowwise(_make_ple_ln(alpha), [h1, ffn, pg, pp], [g2, b2], [(d, F32)], name=n + "ln2")
    saved = dict(h=h, projp=projp, o_pre=o_pre, states=states, cqn=cqn, ckvn=ckvn, qr=qr, kvf=kvf, opad=opad,
                 lse=lse, cat=cat, mix=mix, h1=h1, gu=gu, act=act, ffn=ffn, pg=pg, pp=pp, ng=ng, lg=lg,
                 lbias=lbias, w_s=w_s, bias_full=bias_full, qg=qg, kvg=kvg, g1=g1, b1=b1, g2=g2, b2=b2)
    return h2, saved


def _layer_backward(li, dh2_parts, p_i, wts, sv, lbs, tables, alpha):
    n = f"l{li}_b_"
    gr = {}
    dh1_a, dffn, dpg, dpp, gr["ln2_g"], gr["ln2_b"] = _rowwise_vjp(
        _make_ple_ln(alpha), [sv["h1"], sv["ffn"], sv["pg"], sv["pp"]], [sv["g2"], sv["b2"]], [dh2_parts],
        groups=[[0], [1], [2], [3]], name=n + "ln2")
    gr["ple_w_proj"] = _mm(p_i, dpp, ta=True, name=n + "ple_proj_dw")
    gr["ple_w_gate"] = _mm(sv["h1"], dpg, ta=True, name=n + "ple_gate_dw")
    dh1_b = _mm(dpg, wts["ple_w_gate"], tb=True, name=n + "ple_gate_dx")
    gr["w_down"] = _mm(sv["act"], dffn, ta=True, tm=1408, name=n + "down_dw")
    dact = _mm(dffn, wts["w_down"], tb=True, tn=1408, name=n + "down_dx")
    gu = sv["gu"]
    dff = gu.shape[1] // 2
    (dgu,) = _rowwise_vjp(_fn_swiglu, [(gu, dff, 0), (gu, dff, 1)], [], [[dact]], groups=[[0, 1]],
                          name=n + "swiglu")
    gr["w_gate_up"] = _mm(sv["h1"], dgu, ta=True, tn=512, name=n + "gate_up_dw")
    dh1_c = _mm(dgu, wts["w_gate_up"], tb=True, name=n + "gate_up_dx")
    dh_a, dmix, gr["ln1_g"], gr["ln1_b"] = _rowwise_vjp(
        _make_post_mix(alpha), [sv["h"], sv["mix"]], [sv["g1"], sv["b1"]], [[dh1_a, dh1_b, dh1_c]],
        groups=[[0], [1]], name=n + "ln1")
    gr["w_out"] = _mm(sv["cat"], dmix, ta=True, name=n + "out_proj_dw")
    dcat = _mm(dmix, wts["w_out"], tb=True, name=n + "out_proj_dx")

    dqr, dkv = _attn_bwd(sv["qr"], sv["kvf"], dcat, sv["opad"], sv["lse"], name=n + "attn")
    dqpad, dkr = _mla_prep_bwd(dqr, dkv, tables, name=n + "mla_prep")
    gr["mla_w_uq"] = _mm(sv["cqn"], dqpad, ta=True, name=n + "uq_dw")
    dcqn = _mm(dqpad, wts["mla_w_uq"], tb=True, name=n + "uq_dx")
    gr["mla_w_ukv"] = _mm(sv["ckvn"], dkv, ta=True, name=n + "ukv_dw")
    dckvn = _mm(dkv, wts["mla_w_ukv"], tb=True, name=n + "ukv_dx")
    projp = sv["projp"]
    dcq, gr["mla_q_norm_g"] = _rowwise_vjp(_fn_rms, [(projp, 384, P_CQ // 384)], [sv["qg"]], [[dcqn]],
                                           groups=[[0]], name=n + "q_norm")
    dckv, gr["mla_kv_norm_g"] = _rowwise_vjp(_fn_rms, [(projp, 256, P_CKV // 256)], [sv["kvg"]], [[dckvn]],
                                             groups=[[0]], name=n + "kv_norm")
    dsgu, gr["sgu_ln_g"], gr["sgu_ln_b"], gr["sgu_w_s"], gr["sgu_b_s"] = _sgu_bwd(
        projp, sv["lg"], sv["lbias"], sv["w_s"], sv["bias_full"], dcat, name=n + "sgu")
    dhg, gr["hgrn_norm_g"], gr["lower_bound"] = _hgrn_bwd(
        projp, lbs[li], sv["ng"], sv["o_pre"], sv["states"], dcat, name=n + "hgrn")
    dprojp = jnp.concatenate([dhg, dsgu, dcq, dkr, dckv], axis=1)
    gr["w_in"] = _mm(sv["h"], dprojp, ta=True, name=n + "proj_dw")
    dh_b = _mm(dprojp, wts["w_in"], tb=True, name=n + "proj_dx")
    return [dh_a, dh_b], gr


def kernel(x, p, positions, ln_in_g, ln_in_b, w_in, hgrn_lb_logits, hgrn_norm_g, sgu_ln_g, sgu_ln_b, sgu_w_s, sgu_b_s, mla_q_norm_g, mla_w_uq, mla_kv_norm_g, mla_w_ukv, w_out, ln1_g, ln1_b, w_gate_up, w_down, ple_w_gate, ple_w_proj, ln2_g, ln2_b, loss_target, m_ln_in_g, m_ln_in_b, m_w_in, m_hgrn_lb_logits, m_hgrn_norm_g, m_sgu_ln_g, m_sgu_ln_b, m_sgu_w_s, m_sgu_b_s, m_mla_q_norm_g, m_mla_w_uq, m_mla_kv_norm_g, m_mla_w_ukv, m_w_out, m_ln1_g, m_ln1_b, m_w_gate_up, m_w_down, m_ple_w_gate, m_ple_w_proj, m_ln2_g, m_ln2_b, v_ln_in_g, v_ln_in_b, v_w_in, v_hgrn_lb_logits, v_hgrn_norm_g, v_sgu_ln_g, v_sgu_ln_b, v_sgu_w_s, v_sgu_b_s, v_mla_q_norm_g, v_mla_w_uq, v_mla_kv_norm_g, v_mla_w_ukv, v_w_out, v_ln1_g, v_ln1_b, v_w_gate_up, v_w_down, v_ple_w_gate, v_ple_w_proj, v_ln2_g, v_ln2_b):
    args = dict(locals())
    w = {k: args[k] for k in ORDER}
    m = {k: args["m_" + k] for k in ORDER}
    v = {k: args["v_" + k] for k in ORDER}
    depth = w_in.shape[0]
    assert depth == 2, "the lower-bound kernel is written for two layers"
    alpha = (2 * depth) ** 0.25
    xs, tgt = x[0], loss_target[0]
    d_model = xs.shape[1]

    big_names = [k for k, _ in BIG]
    shard_shapes = [w[k].shape for k in big_names]
    gathered = _all_gather(_pack([w[k] for k in big_names]).astype(BF16), name="gather_weights")
    flat = gathered.reshape(N_DEV, -1)
    full, off = {}, 0
    for (k, axis), s in zip(BIG, shard_shapes):
        cnt = math.prod(s)
        full[k] = _full_from_gathered(flat[:, off:off + cnt], s, axis)
        off += cnt
    pad_fns = {"w_in": _pad_w_in, "mla_w_uq": _pad_w_uq, "mla_w_ukv": _pad_w_ukv, "w_out": _pad_w_out}
    unpad_fns = {"w_in": _unpad_w_in, "mla_w_uq": _unpad_w_uq, "mla_w_ukv": _unpad_w_ukv, "w_out": _unpad_w_out}
    layer_w = [{k: (pad_fns[k](full[k][li]) if k in pad_fns else full[k][li]) for k in big_names}
               for li in range(depth)]

    tables = _rope_tables(positions[0])
    row1 = lambda a: a.reshape(1, -1)
    l0, l1 = row1(hgrn_lb_logits[0]), row1(hgrn_lb_logits[1])
    lbs = _rowwise(_fn_lower_bounds, [l0, l1], [], [(HG_W, F32), (HG_W, F32)], name="lower_bounds")

    gin, bin_ = row1(ln_in_g), row1(ln_in_b)
    (h,) = _rowwise(_fn_ln, [xs], [gin, bin_], [(d_model, F32)], name="ln_in")
    saved = []
    for li in range(depth):
        h, sv = _layer_forward(li, h, p[li, 0], layer_w[li], w, lbs, tables, alpha)
        saved.append(sv)
    dy, loss_local = _loss_and_grad(h, tgt, name="loss")
    loss = lax.psum(loss_local[0, 0], ("x", "y", "c"))

    dparts, grads = [dy], [None] * depth
    for li in reversed(range(depth)):
        dparts, grads[li] = _layer_backward(li, dparts, p[li, 0], layer_w[li], saved[li], lbs, tables, alpha)
    dx, d_gin, d_bin = _rowwise_vjp(_fn_ln, [xs], [gin, bin_], [dparts], groups=[[0]], name="ln_in_b")
    dl0, dl1 = _rowwise_vjp(_fn_lower_bounds, [l0, l1], [], [[grads[0]["lower_bound"]], [grads[1]["lower_bound"]]],
                            groups=[[0], [1]], name="lower_bounds_b")

    big_parts = []
    for k, axis in BIG:
        g = jnp.stack([(unpad_fns[k](grads[li][k]) if k in unpad_fns else grads[li][k]) for li in range(depth)])
        big_parts.append(_shards_from_full(g, axis))
    send = jnp.concatenate(big_parts, axis=1)
    n_rows = send.shape[1] // LANES
    send = jnp.transpose(send.reshape(2, 2, 2, n_rows, LANES), (2, 0, 1, 3, 4))
    pair = _exchange(send.reshape(2, 4 * n_rows, LANES), "c", name="rs_pair")
    pair_sum = _sum_groups(pair, name="rs_pair_sum")
    quad = _exchange(pair_sum.reshape(4, n_rows, LANES), "xy", name="rs_quad")
    g_big, d_big, m_big, v_big = _adamw(quad, _pack([w[k] for k in big_names]), _pack([m[k] for k in big_names]),
                                        _pack([v[k] for k in big_names]), name="adamw_big")

    small_g = {"ln_in_g": d_gin.reshape(-1), "ln_in_b": d_bin.reshape(-1),
               "hgrn_lb_logits": jnp.stack([dl0.reshape(-1), dl1.reshape(-1)])}
    for k in SMALL[3:]:
        small_g[k] = jnp.stack([grads[li][k].reshape(w[k].shape[1:]) for li in range(depth)])
    small_parts = _all_gather(_pack([small_g[k] for k in SMALL]), name="gather_small_grads")
    g_sm, d_sm, m_sm, v_sm = _adamw(small_parts, _pack([w[k] for k in SMALL]), _pack([m[k] for k in SMALL]),
                                    _pack([v[k] for k in SMALL]), name="adamw_small")

    out = {}
    for names, slabs in ((big_names, (g_big, d_big, m_big, v_big)), (SMALL, (g_sm, d_sm, m_sm, v_sm))):
        shapes = [w[k].shape for k in names]
        for prefix, slab in zip(("grad_", "delta_", "new_m_", "new_v_"), slabs):
            for k, a in zip(names, _unpack(slab, shapes)):
                out[prefix + k] = a
    res = [loss, dx[None]]
    for prefix in ("grad_", "delta_", "new_m_", "new_v_"):
        res += [out[prefix + k] for k in ORDER]
    return tuple(res)
```

```python
import math

import jax
import jax.numpy as jnp
from jax import lax
from jax.experimental import pallas as pl
from jax.experimental.pallas import tpu as pltpu

F32 = jnp.float32
BF16 = jnp.bfloat16
MESH = pl.DeviceIdType.MESH

LN_EPS = 1e-5
RMS_EPS = 1e-6
ROPE_THETA = 10000.0
ADAM_LR, ADAM_B1, ADAM_B2, ADAM_EPS, ADAM_WD, ADAM_STEP = 0.001, 0.9, 0.999, 1e-08, 0.01, 10

N_DEV = 8
LANES = 128
HG_CHUNK = 16
HG_W = 256
HEAD = 64
SGU_CHUNK = 128
N_ATT_HEADS = 8
ATT_D = 96
VMEM_LIMIT = 56 * 1024 * 1024

HG_TILE = 256
ATT_TQ = 512
ROW_TILE = 256

P_CQ, P_KR, P_CKV, P_COLS = 1536, 1920, 2048, 2304


def _cparams(sem):
    return pltpu.CompilerParams(dimension_semantics=sem, vmem_limit_bytes=VMEM_LIMIT)


def _tile(n, pref):
    if n % pref == 0:
        return pref
    best = None
    t = LANES
    while t <= min(n, pref):
        if n % t == 0:
            best = t
        t += LANES
    return best if best is not None else n


def _mm(a, b, *, am="mk", bm="kn", om="mn", out_dtype=F32, tm=1024, tn=1024, tk=512, name):
    if am == "mk":
        m, k = a.shape
    elif am == "km":
        k, m = a.shape
    elif am == "bmk":
        m, tk = a.shape[1], a.shape[2]
        k = a.shape[0] * tk
    else:
        k, tm = a.shape[1], a.shape[2]
        m = a.shape[0] * tm
    if bm == "kn":
        kb_, n = b.shape
    elif bm == "nk":
        n, kb_ = b.shape
    elif bm == "bkn":
        kb_, tn = b.shape[1], b.shape[2]
        n = b.shape[0] * tn
    else:
        n, tk = b.shape[1], b.shape[2]
        kb_ = b.shape[0] * tk
    assert kb_ == k, (a.shape, b.shape, am, bm)
    tm, tn, tk = _tile(m, tm), _tile(n, tn), _tile(k, tk)
    nk = k // tk
    dims = (((0 if am in ("km", "bkm") else 1,), (1 if bm in ("nk", "bnk") else 0,)), ((), ()))

    a_spec = {"mk": pl.BlockSpec((tm, tk), lambda i, j, kk: (i, kk)),
              "km": pl.BlockSpec((tk, tm), lambda i, j, kk: (kk, i)),
              "bmk": pl.BlockSpec((None, tm, tk), lambda i, j, kk: (kk, i, 0)),
              "bkm": pl.BlockSpec((None, tk, tm), lambda i, j, kk: (i, kk, 0))}[am]
    b_spec = {"kn": pl.BlockSpec((tk, tn), lambda i, j, kk: (kk, j)),
              "nk": pl.BlockSpec((tn, tk), lambda i, j, kk: (j, kk)),
              "bkn": pl.BlockSpec((None, tk, tn), lambda i, j, kk: (j, kk, 0)),
              "bnk": pl.BlockSpec((None, tn, tk), lambda i, j, kk: (kk, j, 0))}[bm]
    if om == "mn":
        o_spec, o_shape = pl.BlockSpec((tm, tn), lambda i, j, kk: (i, j)), (m, n)
    else:
        o_spec, o_shape = pl.BlockSpec((None, tm, tn), lambda i, j, kk: (j, i, 0)), (n // tn, m, tn)

    def body(a_ref, b_ref, o_ref, acc_ref):
        kk = pl.program_id(2)

        @pl.when(kk == 0)
        def _():
            acc_ref[...] = jnp.zeros_like(acc_ref)

        acc_ref[...] += lax.dot_general(a_ref[...].astype(BF16), b_ref[...].astype(BF16), dims,
                                        preferred_element_type=F32)

        @pl.when(kk == nk - 1)
        def _():
            o_ref[...] = acc_ref[...].astype(o_ref.dtype)

    return pl.pallas_call(
        body, name=name, grid=(m // tm, n // tn, nk),
        in_specs=[a_spec, b_spec],
        out_specs=o_spec,
        out_shape=jax.ShapeDtypeStruct(o_shape, out_dtype),
        scratch_shapes=[pltpu.VMEM((tm, tn), F32)],
        compiler_params=_cparams(("parallel", "parallel", "arbitrary")),
    )(a, b)


def _row_operand(a, tile):
    if isinstance(a, tuple):
        arr, w, j = a
        return arr, pl.BlockSpec((tile, w), lambda i, j=j: (i, j))
    return a, pl.BlockSpec((tile, a.shape[1]), lambda i: (i, 0))


def _const_spec(c):
    nd = c.ndim
    return pl.BlockSpec(c.shape, lambda i, nd=nd: (0,) * nd)


def _rowwise(fn, rows, consts, outs, *, name, accs=(), tile=None):
    t_rows = (rows[0][0] if isinstance(rows[0], tuple) else rows[0]).shape[0]
    tile = min(tile or ROW_TILE, t_rows)
    arrs, specs = zip(*[_row_operand(a, tile) for a in rows])
    nin, no = len(rows) + len(consts), len(outs)

    def body(*refs):
        res = fn(*[r[...] for r in refs[:nin]])
        for r, v in zip(refs[nin:nin + no], res[:no]):
            r[...] = v.astype(r.dtype)
        if accs:
            a_refs = refs[nin + no:]

            @pl.when(pl.program_id(0) == 0)
            def _():
                for r in a_refs:
                    r[...] = jnp.zeros_like(r)

            for r, v in zip(a_refs, res[no:]):
                r[...] += v

    out_shape = [jax.ShapeDtypeStruct((t_rows, w), dt) for w, dt in outs]
    out_shape += [jax.ShapeDtypeStruct(s, F32) for s in accs]
    out_specs = [pl.BlockSpec((tile, w), lambda i: (i, 0)) for w, _ in outs]
    out_specs += [pl.BlockSpec(s, lambda i, nd=len(s): (0,) * nd) for s in accs]
    return pl.pallas_call(
        body, name=name, grid=(t_rows // tile,),
        in_specs=list(specs) + [_const_spec(c) for c in consts],
        out_specs=out_specs, out_shape=out_shape,
        compiler_params=_cparams(("arbitrary",)),
    )(*arrs, *consts)


def _rowwise_vjp(fn, rows, consts, cts, *, name, groups, tile=None):
    t_rows = (rows[0][0] if isinstance(rows[0], tuple) else rows[0]).shape[0]
    tile = min(tile or ROW_TILE, t_rows)
    arrs, specs = zip(*[_row_operand(a, tile) for a in rows])
    flat_cts = [c for group in cts for c in group]
    ct_arrs, ct_specs = zip(*[_row_operand(a, tile) for a in flat_cts])
    nr, nc, nct, ng = len(rows), len(consts), len(flat_cts), len(groups)

    def width(a):
        return a[1] if isinstance(a, tuple) else a.shape[1]

    def body(*refs):
        rv = [r[...] for r in refs[:nr]]
        cv = [r[...] for r in refs[nr:nr + nc]]
        ct_refs = refs[nr + nc:nr + nc + nct]
        ctv, pos = [], 0
        for group in cts:
            s = ct_refs[pos][...]
            for r in ct_refs[pos + 1:pos + len(group)]:
                s = s + r[...]
            ctv.append(s)
            pos += len(group)
        _, pull = jax.vjp(fn, *rv, *cv)
        grads = pull(tuple(ctv))
        g_refs = refs[nr + nc + nct:nr + nc + nct + ng]
        for r, idx in zip(g_refs, groups):
            parts = [grads[i] for i in idx]
            r[...] = parts[0] if len(parts) == 1 else jnp.concatenate(parts, axis=1)
        c_refs = refs[nr + nc + nct + ng:]

        @pl.when(pl.program_id(0) == 0)
        def _():
            for r in c_refs:
                r[...] = jnp.zeros_like(r)

        for r, v in zip(c_refs, grads[nr:]):
            r[...] += v

    gw = [sum(width(rows[i]) for i in idx) for idx in groups]
    out_shape = [jax.ShapeDtypeStruct((t_rows, w), F32) for w in gw]
    out_shape += [jax.ShapeDtypeStruct(c.shape, F32) for c in consts]
    out_specs = [pl.BlockSpec((tile, w), lambda i: (i, 0)) for w in gw]
    out_specs += [_const_spec(c) for c in consts]
    return pl.pallas_call(
        body, name=name, grid=(t_rows // tile,),
        in_specs=list(specs) + [_const_spec(c) for c in consts] + list(ct_specs),
        out_specs=out_specs, out_shape=out_shape,
        compiler_params=_cparams(("arbitrary",)),
    )(*arrs, *consts, *ct_arrs)


def _layer_norm(x, g, b):
    mu = jnp.mean(x, axis=-1, keepdims=True)
    xc = x - mu
    var = jnp.mean(xc * xc, axis=-1, keepdims=True)
    return xc * lax.rsqrt(var + LN_EPS) * g + b


def _sigmoid(x):
    return 1.0 / (1.0 + jnp.exp(-x))


def _fn_ln(x, g, b):
    return (_layer_norm(x, g, b),)


def _fn_rms(x, g):
    return (x * lax.rsqrt(jnp.mean(x * x, axis=-1, keepdims=True) + RMS_EPS) * g,)


def _make_post_mix(alpha):
    def fn(h, mix, g, b):
        return (_layer_norm(alpha * h + mix, g, b),)
    return fn


def _make_ple_ln(alpha):
    def fn(h1, ffn, pg, pp, g, b):
        return (_layer_norm(alpha * h1 + ffn + _sigmoid(pg) * pp, g, b),)
    return fn


def _fn_lower_bounds(l0, l1):
    m = jnp.maximum(l0, l1)
    e0, e1 = jnp.exp(l0 - m), jnp.exp(l1 - m)
    s = e0 + e1
    p0, p1 = e0 / s, e1 / s
    return (p0 - p0, (p0 + p1) - p0)


def _loss_and_grad(y, target, *, name):
    d = y.shape[1]

    def fn(yv, tv):
        err = yv - tv
        return err * (1.0 / d), 0.5 * jnp.sum(jnp.mean(err * err, axis=-1, keepdims=True), axis=0, keepdims=True)

    return _rowwise(fn, [y, target], [], [(d, F32)], accs=[(1, 1)], name=name)


def _split_dot(x, e_bf16):
    hi = x.astype(BF16)
    lo = (x - hi.astype(F32)).astype(BF16)
    return (jnp.dot(hi, e_bf16, preferred_element_type=F32) + jnp.dot(lo, e_bf16, preferred_element_type=F32))


def _hgrn_common(th):
    rm = lax.broadcasted_iota(jnp.int32, (th, HG_W), 0) % HG_CHUNK

    def seg_cumsum(x):
        for s in (1, 2, 4, 8):
            x = x + jnp.where(rm >= s, pltpu.roll(x, s, 0), 0.0)
        return x

    def seg_rcumsum(x):
        for s in (1, 2, 4, 8):
            x = x + jnp.where(rm < HG_CHUNK - s, pltpu.roll(x, th - s, 0), 0.0)
        return x

    ri = lax.broadcasted_iota(jnp.int32, (HG_W, HG_W), 0) // HEAD
    ci = lax.broadcasted_iota(jnp.int32, (HG_W, HG_W), 1) // HEAD
    head_f32 = (ri == ci).astype(F32)
    head_bf16 = head_f32.astype(BF16)

    def headsum(x):
        return _split_dot(x, head_bf16)

    return rm, seg_cumsum, seg_rcumsum, head_f32, headsum


def _hgrn_gates(qr, fl, lb):
    sg = _sigmoid(fl)
    f = lb + (1.0 - lb) * sg
    sq = _sigmoid(qr)
    return sg, f, jnp.log(f), 1.0 - f, qr * sq, sq


def _shifted(x, d, th):
    return x if d == 0 else pltpu.roll(x, d, 0)


def _unshift(x, d, th):
    return x if d == 0 else pltpu.roll(x, th - d, 0)


def _hgrn_fwd(projp, lb, ng, *, name):
    t_rows = projp.shape[0]
    th = min(HG_TILE, t_rows)
    nct = th // HG_CHUNK

    def body(q_ref, f_ref, i_ref, g_ref, lb_ref, ng_ref, oa_ref, opre_ref, st_out_ref,
             st_ref, vtm_ref, kv_ref, qe_ref, dec_ref, oint_ref):
        rm, seg_cumsum, seg_rcumsum, head_f32, headsum = _hgrn_common(th)

        @pl.when(pl.program_id(0) == 0)
        def _():
            st_ref[...] = jnp.zeros_like(st_ref)

        qr, fl, v, g = q_ref[...], f_ref[...], i_ref[...], g_ref[...]
        _, f, lf, k, q, _ = _hgrn_gates(qr, fl, lb_ref[...])
        b = seg_cumsum(lf)

        o = jnp.zeros((th, HG_W), F32)
        for d in range(HG_CHUNK):
            kd, bd, vd = _shifted(k, d, th), _shifted(b, d, th), _shifted(v, d, th)
            e = jnp.exp(jnp.where(rm >= d, b - bd, -1e30))
            o = o + headsum(q * kd * e) * vd

        blast = seg_rcumsum(jnp.where(rm == HG_CHUNK - 1, b, 0.0))
        kte = (k * jnp.exp(blast - b)).astype(BF16)
        qe_ref[...] = q * jnp.exp(b)
        dec_ref[...] = jnp.exp(blast)
        vt = v.T
        lane_chunk = lax.broadcasted_iota(jnp.int32, (HG_W, th), 1) // HG_CHUNK
        for c in range(nct):
            vtm_ref[c * HG_W:(c + 1) * HG_W, :] = jnp.where(lane_chunk == c, vt, 0.0).astype(BF16)
        kv_ref[...] = jnp.dot(vtm_ref[...], kte, preferred_element_type=F32)

        def step(c, carry):
            r0 = pl.multiple_of(c * HG_CHUNK, HG_CHUNK)
            s = st_ref[...]
            st_out_ref[c] = s
            oint_ref[pl.ds(r0, HG_CHUNK), :] = lax.dot_general(
                qe_ref[pl.ds(r0, HG_CHUNK), :].astype(BF16), s.astype(BF16),
                (((1,), (1,)), ((), ())), preferred_element_type=F32)
            dec = jnp.max(dec_ref[pl.ds(r0, HG_CHUNK), :], axis=0, keepdims=True)
            kv_c = kv_ref[pl.ds(pl.multiple_of(c * HG_W, HG_W), HG_W), :]
            st_ref[...] = s * dec + kv_c * head_f32
            return carry

        lax.fori_loop(0, nct, step, 0)

        o = o + oint_ref[...]
        opre_ref[...] = o
        r = lax.rsqrt(headsum(o * o) * (1.0 / HEAD) + RMS_EPS)
        oa_ref[...] = o * r * ng_ref[...] * (g * _sigmoid(g))

    col = lambda j: pl.BlockSpec((th, HG_W), lambda i, j=j: (i, j))
    vec = pl.BlockSpec((1, HG_W), lambda i: (0, 0))
    row = pl.BlockSpec((th, HG_W), lambda i: (i, 0))
    n_chunks = t_rows // HG_CHUNK
    return pl.pallas_call(
        body, name=name, grid=(t_rows // th,),
        in_specs=[col(0), col(1), col(2), col(3), vec, vec],
        out_specs=[row, row, pl.BlockSpec((nct, HG_W, HG_W), lambda i: (i, 0, 0))],
        out_shape=[jax.ShapeDtypeStruct((t_rows, HG_W), F32), jax.ShapeDtypeStruct((t_rows, HG_W), F32),
                   jax.ShapeDtypeStruct((n_chunks, HG_W, HG_W), F32)],
        scratch_shapes=[pltpu.VMEM((HG_W, HG_W), F32), pltpu.VMEM((nct * HG_W, th), BF16),
                        pltpu.VMEM((nct * HG_W, HG_W), F32), pltpu.VMEM((th, HG_W), F32),
                        pltpu.VMEM((th, HG_W), F32), pltpu.VMEM((th, HG_W), F32)],
        compiler_params=_cparams(("arbitrary",)),
    )(projp, projp, projp, projp, lb, ng)


def _hgrn_bwd(projp, lb, ng, opre, states, dcat, *, name):
    t_rows = projp.shape[0]
    th = min(HG_TILE, t_rows)
    nct = th // HG_CHUNK
    nt = t_rows // th

    def body(q_ref, f_ref, i_ref, g_ref, lb_ref, ng_ref, opre_ref, st_in_ref, do_ref,
             dproj_ref, dng_ref, dlb_ref,
             gst_ref, dotm_ref, qg_ref, v_ref, kte_ref, dop_ref, dec_ref, dkte_ref, dvi_ref, dqe_ref, ddec_ref):
        rm, seg_cumsum, seg_rcumsum, head_f32, headsum = _hgrn_common(th)

        @pl.when(pl.program_id(0) == 0)
        def _():
            gst_ref[...] = jnp.zeros_like(gst_ref)
            dng_ref[...] = jnp.zeros_like(dng_ref)
            dlb_ref[...] = jnp.zeros_like(dlb_ref)

        qr, fl, v, g = q_ref[...], f_ref[...], i_ref[...], g_ref[...]
        lb, ngv = lb_ref[...], ng_ref[...]
        sg, f, lf, k, q, sq = _hgrn_gates(qr, fl, lb)
        b = seg_cumsum(lf)
        blast = seg_rcumsum(jnp.where(rm == HG_CHUNK - 1, b, 0.0))
        eb = jnp.exp(b)
        ekb = jnp.exp(blast - b)
        qe, kte, dec = q * eb, k * ekb, jnp.exp(blast)

        do_out, op = do_ref[...], opre_ref[...]
        sgg = _sigmoid(g)
        sil = g * sgg
        r = lax.rsqrt(headsum(op * op) * (1.0 / HEAD) + RMS_EPS)
        on = op * r
        dng_ref[...] += jnp.sum(do_out * on * sil, axis=0, keepdims=True)
        dg = do_out * on * ngv * (sgg * (1.0 + g * (1.0 - sgg)))
        don = do_out * ngv * sil
        dop = r * (don - on * (headsum(don * on) * (1.0 / HEAD)))

        v_ref[...] = v
        kte_ref[...] = kte
        dop_ref[...] = dop
        dec_ref[...] = dec
        dot_t = dop.T
        lane_chunk = lax.broadcasted_iota(jnp.int32, (HG_W, th), 1) // HG_CHUNK
        for c in range(nct):
            dotm_ref[c * HG_W:(c + 1) * HG_W, :] = jnp.where(lane_chunk == c, dot_t, 0.0).astype(BF16)
        qg_ref[...] = jnp.dot(dotm_ref[...], qe.astype(BF16), preferred_element_type=F32)

        def step(j, carry):
            c = nct - 1 - j
            r0 = pl.multiple_of(c * HG_CHUNK, HG_CHUNK)
            rows = pl.ds(r0, HG_CHUNK)
            gs = gst_ref[...]
            s = st_in_ref[c]
            gm = (gs * head_f32).astype(BF16)
            dkte_ref[rows, :] = jnp.dot(v_ref[rows, :].astype(BF16), gm, preferred_element_type=F32)
            dvi_ref[rows, :] = lax.dot_general(kte_ref[rows, :].astype(BF16), gm, (((1,), (1,)), ((), ())),
                                               preferred_element_type=F32)
            dqe_ref[rows, :] = jnp.dot(dop_ref[rows, :].astype(BF16), s.astype(BF16), preferred_element_type=F32)
            ddec_ref[rows, :] = jnp.broadcast_to(jnp.sum(gs * s, axis=0, keepdims=True), (HG_CHUNK, HG_W))
            dec_c = jnp.max(dec_ref[rows, :], axis=0, keepdims=True)
            qg_c = qg_ref[pl.ds(pl.multiple_of(c * HG_W, HG_W), HG_W), :]
            gst_ref[...] = gs * dec_c + qg_c * head_f32
            return carry

        lax.fori_loop(0, nct, step, 0)

        dkte, dqe = dkte_ref[...], dqe_ref[...]
        dq = dqe * eb
        dk = dkte * ekb
        db = dqe * qe - dkte * kte
        dv = dvi_ref[...]
        dblast = dkte * kte + jnp.where(rm == HG_CHUNK - 1, ddec_ref[...] * dec, 0.0)

        for d in range(HG_CHUNK):
            kd, bd, vd = _shifted(k, d, th), _shifted(b, d, th), _shifted(v, d, th)
            e = jnp.exp(jnp.where(rm >= d, b - bd, -1e30))
            p = q * kd * e
            sc = headsum(p)
            dsc = headsum(dop * vd)
            dv = dv + _unshift(sc * dop, d, th)
            dq = dq + dsc * kd * e
            dk = dk + _unshift(dsc * q * e, d, th)
            darg = dsc * p
            db = db + darg - _unshift(darg, d, th)

        db = db + jnp.where(rm == HG_CHUNK - 1, seg_cumsum(dblast), 0.0)
        dlf = seg_rcumsum(db)
        df = dlf / f - dk
        dlb_ref[...] += jnp.sum(df * (1.0 - sg), axis=0, keepdims=True)
        dfl = df * (1.0 - lb) * sg * (1.0 - sg)
        dqr = dq * (sq * (1.0 + qr * (1.0 - sq)))
        dproj_ref[...] = jnp.concatenate([dqr, dfl, dv, dg], axis=1)

    rev = lambda i: nt - 1 - i
    col = lambda j: pl.BlockSpec((th, HG_W), lambda i, j=j: (rev(i), j))
    vec = pl.BlockSpec((1, HG_W), lambda i: (0, 0))
    row = pl.BlockSpec((th, HG_W), lambda i: (rev(i), 0))
    tile_f32 = pltpu.VMEM((th, HG_W), F32)
    return pl.pallas_call(
        body, name=name, grid=(nt,),
        in_specs=[col(0), col(1), col(2), col(3), vec, vec, row,
                  pl.BlockSpec((nct, HG_W, HG_W), lambda i: (rev(i), 0, 0)), col(0)],
        out_specs=[pl.BlockSpec((th, 4 * HG_W), lambda i: (rev(i), 0)), vec, vec],
        out_shape=[jax.ShapeDtypeStruct((t_rows, 4 * HG_W), F32), jax.ShapeDtypeStruct((1, HG_W), F32),
                   jax.ShapeDtypeStruct((1, HG_W), F32)],
        scratch_shapes=[pltpu.VMEM((HG_W, HG_W), F32), pltpu.VMEM((nct * HG_W, th), BF16),
                        pltpu.VMEM((nct * HG_W, HG_W), F32)] + [tile_f32] * 8,
        compiler_params=_cparams(("arbitrary",)),
    )(projp, projp, projp, projp, lb, ng, opre, states, dcat)


_INV_SQRT2 = 1.0 / math.sqrt(2.0)
_INV_SQRT2PI = 1.0 / math.sqrt(2.0 * math.pi)


def _gelu(x):
    return 0.5 * x * (1.0 + lax.erf(x * _INV_SQRT2))


def _gelu_grad(x):
    return 0.5 * (1.0 + lax.erf(x * _INV_SQRT2)) + x * jnp.exp(-0.5 * x * x) * _INV_SQRT2PI


def _sgu_parts(bu, bv, lg, lbias, w_ref, n_groups):
    c = SGU_CHUNK
    tril = (lax.broadcasted_iota(jnp.int32, (c, c), 0) >= lax.broadcasted_iota(jnp.int32, (c, c), 1)).astype(F32)
    gid = lax.broadcasted_iota(jnp.int32, bu.shape, 1) // HEAD
    u = _gelu(bu)
    gv = _gelu(bv)
    mu = jnp.mean(gv, axis=-1, keepdims=True)
    xc = gv - mu
    rstd = lax.rsqrt(jnp.mean(xc * xc, axis=-1, keepdims=True) + LN_EPS)
    xhat = xc * rstd
    vn = xhat * lg + lbias
    ws = [w_ref[gi] * tril for gi in range(n_groups)]
    return tril, gid, u, rstd, xhat, vn, ws


def _sgu_fwd(projp, lg, lbias, w_s, bias_full, *, name):
    t_rows = projp.shape[0]
    n_groups = w_s.shape[0]
    c = SGU_CHUNK

    def body(u_ref, v_ref, lg_ref, lb_ref, w_ref, bias_ref, o_ref):
        _, gid, u, _, _, vn, ws = _sgu_parts(u_ref[...], v_ref[...], lg_ref[...], lb_ref[...], w_ref, n_groups)
        vnb = vn.astype(BF16)
        z = bias_ref[...]
        for gi in range(n_groups):
            z = z + jnp.where(gid == gi, jnp.dot(ws[gi].astype(BF16), vnb, preferred_element_type=F32), 0.0)
        o_ref[...] = u * z

    col = lambda j: pl.BlockSpec((c, HG_W), lambda i, j=j: (i, j))
    return pl.pallas_call(
        body, name=name, grid=(t_rows // c,),
        in_specs=[col(4), col(5), _const_spec(lg), _const_spec(lbias), _const_spec(w_s), _const_spec(bias_full)],
        out_specs=pl.BlockSpec((c, HG_W), lambda i: (i, 0)),
        out_shape=jax.ShapeDtypeStruct((t_rows, HG_W), F32),
        compiler_params=_cparams(("arbitrary",)),
    )(projp, projp, lg, lbias, w_s, bias_full)


def _sgu_bwd(projp, lg, lbias, w_s, bias_full, dcat, *, name):
    t_rows = projp.shape[0]
    n_groups = w_s.shape[0]
    c = SGU_CHUNK
    n = t_rows // c

    def body(u_ref, v_ref, lg_ref, lb_ref, w_ref, bias_ref, do_ref,
             dproj_ref, dlg_ref, dlb_ref, dw_ref, dbs_ref, dbias_acc):
        i = pl.program_id(0)

        @pl.when(i == 0)
        def _():
            dlg_ref[...] = jnp.zeros_like(dlg_ref)
            dlb_ref[...] = jnp.zeros_like(dlb_ref)
            dw_ref[...] = jnp.zeros_like(dw_ref)
            dbias_acc[...] = jnp.zeros_like(dbias_acc)

        bu, bv, lg_v = u_ref[...], v_ref[...], lg_ref[...]
        tril, gid, u, rstd, xhat, vn, ws = _sgu_parts(bu, bv, lg_v, lb_ref[...], w_ref, n_groups)
        vnb = vn.astype(BF16)
        z = bias_ref[...]
        for gi in range(n_groups):
            z = z + jnp.where(gid == gi, jnp.dot(ws[gi].astype(BF16), vnb, preferred_element_type=F32), 0.0)
        do = do_ref[...]
        dbu = do * z * _gelu_grad(bu)
        dz = do * u
        dbias_acc[...] += dz
        dvn = jnp.zeros_like(dz)
        for gi in range(n_groups):
            dzg = jnp.where(gid == gi, dz, 0.0).astype(BF16)
            dw_ref[gi] += lax.dot_general(dzg, vnb, (((1,), (1,)), ((), ())), preferred_element_type=F32) * tril
            dvn = dvn + jnp.dot(ws[gi].T.astype(BF16), dzg, preferred_element_type=F32)
        dlg_ref[...] += jnp.sum(dvn * xhat, axis=0, keepdims=True)
        dlb_ref[...] += jnp.sum(dvn, axis=0, keepdims=True)
        dxh = dvn * lg_v
        dgv = rstd * (dxh - jnp.mean(dxh, axis=-1, keepdims=True)
                      - xhat * jnp.mean(dxh * xhat, axis=-1, keepdims=True))
        dproj_ref[...] = jnp.concatenate([dbu, dgv * _gelu_grad(bv)], axis=1)

        @pl.when(i == n - 1)
        def _():
            dbs_ref[...] = jnp.sum(dbias_acc[...].T.reshape(n_groups, HEAD, c), axis=1)

    col = lambda j: pl.BlockSpec((c, HG_W), lambda i, j=j: (i, j))
    return pl.pallas_call(
        body, name=name, grid=(n,),
        in_specs=[col(4), col(5), _const_spec(lg), _const_spec(lbias), _const_spec(w_s), _const_spec(bias_full),
                  col(1)],
        out_specs=[pl.BlockSpec((c, 2 * HG_W), lambda i: (i, 0)), _const_spec(lg), _const_spec(lbias),
                   _const_spec(w_s), pl.BlockSpec((n_groups, c), lambda i: (0, 0))],
        out_shape=[jax.ShapeDtypeStruct((t_rows, 2 * HG_W), F32), jax.ShapeDtypeStruct(lg.shape, F32),
                   jax.ShapeDtypeStruct(lbias.shape, F32), jax.ShapeDtypeStruct(w_s.shape, F32),
                   jax.ShapeDtypeStruct((n_groups, c), F32)],
        scratch_shapes=[pltpu.VMEM((c, HG_W), F32)],
        compiler_params=_cparams(("arbitrary",)),
    )(projp, projp, lg, lbias, w_s, bias_full, dcat)


def _rope_tables(positions):
    t = positions.shape[0]
    inv_freq = ROPE_THETA ** (-jnp.arange(0, 32, 2, dtype=F32) / 32)
    ang = positions.astype(F32)[:, None] * inv_freq
    cos, sin = jnp.cos(ang), jnp.sin(ang)
    z = lambda w: jnp.zeros((t, w), F32)
    cos_t = jnp.concatenate([jnp.ones((t, 64), F32), cos, cos, z(32)], axis=1)
    sin_up = jnp.concatenate([z(80), sin, z(32)], axis=1)
    sin_dn = jnp.concatenate([z(64), -sin, z(48)], axis=1)
    return cos_t, sin_up, sin_dn


def _rep(x, n):
    return x if n == 1 else jnp.concatenate([x] * n, axis=1)


def _rope(x, cos_t, sin_up, sin_dn):
    w = x.shape[1]
    return x * cos_t + pltpu.roll(x, 16, 1) * sin_up + pltpu.roll(x, w - 16, 1) * sin_dn


def _rope_t(dy, cos_t, sin_up, sin_dn):
    w = dy.shape[1]
    return dy * cos_t + pltpu.roll(dy * sin_up, w - 16, 1) + pltpu.roll(dy * sin_dn, 16, 1)


def _mla_prep(q, kv, projp, tables, *, name):
    nh = N_ATT_HEADS

    def fn(qv, kvv, kr, cos_t, sin_up, sin_dn):
        qr = _rope(qv, _rep(cos_t, nh), _rep(sin_up, nh), _rep(sin_dn, nh))
        krr = _rope(kr, cos_t, sin_up, sin_dn)
        lane = lax.broadcasted_iota(jnp.int32, kvv.shape, 1) % LANES
        return qr, jnp.where(lane < HEAD, kvv, 0.0) + _rep(krr, nh), kvv

    w = q.shape[1]
    return _rowwise(fn, [q, kv, (projp, LANES, P_KR // LANES)] + list(tables), [],
                    [(w, BF16), (w, BF16), (w, BF16)], name=name)


def _mla_prep_bwd(dqr, dkf, tables, *, name):
    nh = N_ATT_HEADS

    def fn(dq, dk, cos_t, sin_up, sin_dn):
        dqp = _rope_t(dq, _rep(cos_t, nh), _rep(sin_up, nh), _rep(sin_dn, nh))
        dkrr = dk[:, 0:LANES]
        for h in range(1, nh):
            dkrr = dkrr + dk[:, LANES * h:LANES * (h + 1)]
        return dqp, _rope_t(dkrr, cos_t, sin_up, sin_dn)

    return _rowwise(fn, [dqr, dkf] + list(tables), [], [(dqr.shape[1], F32), (LANES, F32)], name=name)


_NT = (((1,), (1,)), ((), ()))
_TN = (((0,), (0,)), ((), ()))


def _attn_fwd(qr, kf, kvb, *, name):
    t_rows = qr.shape[0]
    tq = min(ATT_TQ, t_rows)
    nb = t_rows // tq
    scale = ATT_D ** -0.5

    def body(q_ref, kf_ref, kvb_ref, o_ref, lse_ref):
        qi = pl.program_id(1)
        lane = lax.broadcasted_iota(jnp.int32, (tq, LANES), 1)
        causal = (lax.broadcasted_iota(jnp.int32, (tq, tq), 1) <= lax.broadcasted_iota(jnp.int32, (tq, tq), 0))
        outs = []
        for hh in range(2):
            cols = slice(hh * LANES, (hh + 1) * LANES)
            q = q_ref[:, cols]

            def block(ki, carry, diagonal, q=q, cols=cols):
                m_old, l_old, acc = carry
                rows = pl.ds(pl.multiple_of(ki * tq, tq), tq)
                s = lax.dot_general(q, kf_ref[rows, cols], _NT, preferred_element_type=F32) * scale
                if diagonal:
                    s = jnp.where(causal, s, -1e30)
                m_new = jnp.maximum(m_old, jnp.max(s, axis=-1, keepdims=True))
                p = jnp.exp(s - m_new)
                a = jnp.exp(m_old - m_new)
                return (m_new, a * l_old + jnp.sum(p, axis=-1, keepdims=True),
                        a * acc + jnp.dot(p.astype(BF16), kvb_ref[rows, cols], preferred_element_type=F32))

            init = (jnp.full((tq, 1), -1e30, F32), jnp.zeros((tq, 1), F32), jnp.zeros((tq, LANES), F32))
            carry = lax.fori_loop(0, qi, lambda ki, c, block=block: block(ki, c, False), init)
            m_fin, l_fin, acc = block(qi, carry, True)
            lse_ref[hh] = m_fin + jnp.log(l_fin)
            outs.append(acc / l_fin)
        o_ref[...] = jnp.where(lane < HEAD, pltpu.roll(outs[0], HEAD, 1), outs[1])

    pair = pl.BlockSpec((t_rows, 2 * LANES), lambda pr, qi: (0, pr))
    return pl.pallas_call(
        body, name=name, grid=(N_ATT_HEADS // 2, nb),
        in_specs=[pl.BlockSpec((tq, 2 * LANES), lambda pr, qi: (qi, pr)), pair, pair],
        out_specs=[pl.BlockSpec((tq, LANES), lambda pr, qi: (qi, pr)),
                   pl.BlockSpec((2, tq, 1), lambda pr, qi: (pr, qi, 0))],
        out_shape=[jax.ShapeDtypeStruct((t_rows, N_ATT_HEADS * HEAD), F32),
                   jax.ShapeDtypeStruct((N_ATT_HEADS, t_rows, 1), F32)],
        compiler_params=_cparams(("parallel", "arbitrary")),
    )(qr, kf, kvb)


def _attn_bwd(qr, kf, kvb, dcat, o, lse, *, name):
    t_rows = qr.shape[0]
    tq = min(ATT_TQ, t_rows)
    nb = t_rows // tq
    scale = ATT_D ** -0.5
    do_off = 2 * HG_W // LANES

    def body(q_ref, kf_ref, kvb_ref, do_ref, o_ref, lse_ref, dq_ref, dkv_ref, dk_ref):
        ki = pl.program_id(1)

        @pl.when(ki == 0)
        def _():
            dq_ref[...] = jnp.zeros_like(dq_ref)

        lane = lax.broadcasted_iota(jnp.int32, (tq, LANES), 1)
        causal = (lax.broadcasted_iota(jnp.int32, (tq, tq), 1) <= lax.broadcasted_iota(jnp.int32, (tq, tq), 0))
        dkvs, dks = [], []
        for hh in range(2):
            cols = slice(hh * LANES, (hh + 1) * LANES)
            k, v = kf_ref[:, cols], kvb_ref[:, cols]

            def block(qi, carry, diagonal, hh=hh, cols=cols, k=k, v=v):
                dk, dv = carry
                rows = pl.ds(pl.multiple_of(qi * tq, tq), tq)
                q = q_ref[rows, cols]
                do, ov = do_ref[rows, :], o_ref[rows, :]
                if hh == 0:
                    do, ov = pltpu.roll(do, HEAD, 1), pltpu.roll(ov, HEAD, 1)
                do = jnp.where(lane >= HEAD, do, 0.0)
                delta = jnp.sum(do * ov, axis=-1, keepdims=True)
                s = lax.dot_general(q, k, _NT, preferred_element_type=F32) * scale
                if diagonal:
                    s = jnp.where(causal, s, -1e30)
                p = jnp.exp(s - lse_ref[hh, rows, :])
                dob = do.astype(BF16)
                dv = dv + lax.dot_general(p.astype(BF16), dob, _TN, preferred_element_type=F32)
                dp = lax.dot_general(dob, v, _NT, preferred_element_type=F32)
                ds = (p * (dp - delta) * scale).astype(BF16)
                dk = dk + lax.dot_general(ds, q, _TN, preferred_element_type=F32)
                dq_ref[rows, cols] += jnp.dot(ds, k, preferred_element_type=F32)
                return dk, dv

            zero = jnp.zeros((tq, LANES), F32)
            carry = block(ki, (zero, zero), True)
            dk, dv = lax.fori_loop(ki + 1, nb, lambda qi, c, block=block: block(qi, c, False), carry)
            dks.append(dk)
            dkvs.append(jnp.where(lane < HEAD, dk, dv))
        dkv_ref[...] = jnp.concatenate(dkvs, axis=1)
        dk_ref[...] = jnp.concatenate(dks, axis=1)

    pair_all = pl.BlockSpec((t_rows, 2 * LANES), lambda pr, ki: (0, pr))
    pair_blk = pl.BlockSpec((tq, 2 * LANES), lambda pr, ki: (ki, pr))
    wide = jax.ShapeDtypeStruct((t_rows, N_ATT_HEADS * LANES), F32)
    return pl.pallas_call(
        body, name=name, grid=(N_ATT_HEADS // 2, nb),
        in_specs=[pair_all, pair_blk, pair_blk,
                  pl.BlockSpec((t_rows, LANES), lambda pr, ki: (0, do_off + pr)),
                  pl.BlockSpec((t_rows, LANES), lambda pr, ki: (0, pr)),
                  pl.BlockSpec((2, t_rows, 1), lambda pr, ki: (pr, 0, 0))],
        out_specs=[pair_all, pair_blk, pair_blk],
        out_shape=[wide, wide, wide],
        compiler_params=_cparams(("parallel", "arbitrary")),
    )(qr, kf, kvb, dcat, o, lse)


def _my_pos():
    return lax.axis_index("x"), lax.axis_index("y"), lax.axis_index("c")


def _all_gather(xs, *, name):
    n = len(xs)

    def body(*refs):
        x_refs, out_refs = refs[:n], refs[n:2 * n]
        send_sems, recv_sems, local_sems = refs[2 * n:]
        mx, my, mc = _my_pos()
        me, sibling = (mx, my, mc), (mx, my, 1 - mc)
        chips = [(1 - mx, my), (mx, 1 - my), (1 - mx, 1 - my)]

        def blk(a, px, py, pc):
            return out_refs[a].at[4 * px + 2 * py + pc]

        def copy(a, k, block, to, src=None):
            return pltpu.make_async_remote_copy(
                src_ref=blk(a, *block) if src is None else src, dst_ref=blk(a, *block),
                send_sem=send_sems.at[7 * a + k], recv_sem=recv_sems.at[7 * a + k],
                device_id=to, device_id_type=MESH)

        mine = [pltpu.make_async_copy(x_refs[a], blk(a, *me), local_sems.at[a]) for a in range(n)]
        first = []
        for a in range(n):
            mine[a].start()
            first.append(copy(a, 0, me, sibling, src=x_refs[a]))
            first += [copy(a, 1 + j, me, (*chip, mc), src=x_refs[a]) for j, chip in enumerate(chips)]
        for cp in first:
            cp.start()
        passed = []
        for j, chip in enumerate(chips):
            for a in range(n):
                copy(a, 1 + j, (*chip, mc), me).wait_recv()
                passed.append(copy(a, 4 + j, (*chip, mc), sibling))
                passed[-1].start()
        for a in range(n):
            copy(a, 0, sibling, me).wait_recv()
            for j, chip in enumerate(chips):
                copy(a, 4 + j, (*chip, 1 - mc), me).wait_recv()
        for cp in first + passed:
            cp.wait_send()
        for cp in mine:
            cp.wait()

    return pl.pallas_call(
        body, name=name,
        out_shape=[jax.ShapeDtypeStruct((N_DEV,) + x.shape, x.dtype) for x in xs],
        in_specs=[pl.BlockSpec(memory_space=pl.ANY)] * n,
        out_specs=[pl.BlockSpec(memory_space=pl.ANY)] * n,
        scratch_shapes=[pltpu.SemaphoreType.DMA((7 * n,)), pltpu.SemaphoreType.DMA((7 * n,)),
                        pltpu.SemaphoreType.DMA((n,))],
    )(*xs)


def _pair_exchange(xs, *, name):
    n = len(xs)

    def body(*refs):
        x_refs, out_refs = refs[:n], refs[n:2 * n]
        send_sems, recv_sems = refs[2 * n:]
        mx, my, mc = _my_pos()
        copies = [pltpu.make_async_remote_copy(
            src_ref=x_refs[a].at[g, 1 - mc], dst_ref=out_refs[a].at[g],
            send_sem=send_sems.at[4 * a + g], recv_sem=recv_sems.at[4 * a + g],
            device_id=(mx, my, 1 - mc), device_id_type=MESH) for a in range(n) for g in range(4)]
        for cp in copies:
            cp.start()
        for cp in copies:
            cp.wait_recv()
        for cp in copies:
            cp.wait_send()

    return pl.pallas_call(
        body, name=name,
        out_shape=[jax.ShapeDtypeStruct((4,) + x.shape[2:], x.dtype) for x in xs],
        in_specs=[pl.BlockSpec(memory_space=pl.ANY)] * n,
        out_specs=[pl.BlockSpec(memory_space=pl.ANY)] * n,
        scratch_shapes=[pltpu.SemaphoreType.DMA((4 * n,)), pltpu.SemaphoreType.DMA((4 * n,))],
    )(*xs)


def _pair_add(x, r, core, *, name):
    _, _, a, b = x.shape
    ta = _row_tile(a, 256)

    def body(c_ref, x_ref, r_ref, o_ref):
        o_ref[...] = x_ref[...] + r_ref[...]

    blk = pl.BlockSpec((None, ta, b), lambda g, i, c_ref: (g, i, 0))
    return pl.pallas_call(
        body, name=name,
        grid_spec=pltpu.PrefetchScalarGridSpec(
            num_scalar_prefetch=1, grid=(4, a // ta),
            in_specs=[pl.BlockSpec((None, None, ta, b), lambda g, i, c_ref: (g, c_ref[0], i, 0)), blk],
            out_specs=blk),
        out_shape=jax.ShapeDtypeStruct((4, a, b), x.dtype),
        compiler_params=_cparams(("parallel", "parallel")),
    )(core, x, r)


def _quad_exchange(xs, *, name):
    n = len(xs)

    def body(*refs):
        x_refs, out_refs = refs[:n], refs[n:2 * n]
        send_sems, recv_sems, local_sems = refs[2 * n:]
        mx, my, mc = _my_pos()
        mine = 2 * mx + my
        peers = [((1 - mx, my, mc), 2 * (1 - mx) + my), ((mx, 1 - my, mc), 2 * mx + 1 - my),
                 ((1 - mx, 1 - my, mc), 2 * (1 - mx) + 1 - my)]

        def copy(a, k, dev, src_blk, dst_blk):
            return pltpu.make_async_remote_copy(
                src_ref=x_refs[a].at[src_blk], dst_ref=out_refs[a].at[dst_blk],
                send_sem=send_sems.at[3 * a + k], recv_sem=recv_sems.at[3 * a + k],
                device_id=dev, device_id_type=MESH)

        local = [pltpu.make_async_copy(x_refs[a].at[mine], out_refs[a].at[mine], local_sems.at[a]) for a in range(n)]
        sends = [copy(a, k, dev, g, mine) for a in range(n) for k, (dev, g) in enumerate(peers)]
        for cp in local + sends:
            cp.start()
        for a in range(n):
            for k, (dev, g) in enumerate(peers):
                copy(a, k, dev, g, g).wait_recv()
        for cp in sends:
            cp.wait_send()
        for cp in local:
            cp.wait()

    return pl.pallas_call(
        body, name=name,
        out_shape=[jax.ShapeDtypeStruct(x.shape, x.dtype) for x in xs],
        in_specs=[pl.BlockSpec(memory_space=pl.ANY)] * n,
        out_specs=[pl.BlockSpec(memory_space=pl.ANY)] * n,
        scratch_shapes=[pltpu.SemaphoreType.DMA((3 * n,)), pltpu.SemaphoreType.DMA((3 * n,)),
                        pltpu.SemaphoreType.DMA((n,))],
    )(*xs)


def _row_tile(r, pref):
    t = min(pref, r)
    while r % t or (t % 8 and t != r):
        t -= 1
    return t


def _adamw(parts, w, m, v, layer, *, name, tile=256):
    g, a, b = parts.shape
    tile = _row_tile(a, tile)
    c1 = 1.0 / (1.0 - ADAM_B1 ** ADAM_STEP)
    c2 = 1.0 / (1.0 - ADAM_B2 ** ADAM_STEP)

    def body(p_ref, w_ref, m_ref, v_ref, g_ref, d_ref, mo_ref, vo_ref):
        grad = p_ref[0]
        for j in range(1, g):
            grad = grad + p_ref[j]
        mn = ADAM_B1 * m_ref[...] + (1.0 - ADAM_B1) * grad
        vn = ADAM_B2 * v_ref[...] + (1.0 - ADAM_B2) * (grad * grad)
        g_ref[...] = grad
        mo_ref[...] = mn
        vo_ref[...] = vn
        d_ref[...] = -ADAM_LR * ((mn * c1) / (jnp.sqrt(vn * c2) + ADAM_EPS) + ADAM_WD * w_ref[...])

    slab = pl.BlockSpec((tile, b), lambda i: (i, 0))
    src = slab if layer is None else pl.BlockSpec((None, tile, b), lambda i: (layer, i, 0))
    return pl.pallas_call(
        body, name=name, grid=(a // tile,),
        in_specs=[pl.BlockSpec((g, tile, b), lambda i: (0, i, 0)), src, src, src],
        out_specs=[slab] * 4,
        out_shape=[jax.ShapeDtypeStruct((a, b), F32)] * 4,
        compiler_params=_cparams(("parallel",)),
    )(parts, w, m, v)


W_IN_SHARD = 276


def _w_in_dest(col):
    return jnp.where(col < P_KR, col, jnp.where(col < P_KR + 256, col + (P_CKV - P_KR), col - 2176 + P_KR + HEAD))


def _place_w_in(g, *, name):
    _, d, sh = g.shape
    tc = 768

    def body(g_ref, o_ref, acc_ref):
        ct, j = pl.program_id(0), pl.program_id(1)

        @pl.when(j == 0)
        def _():
            acc_ref[...] = jnp.zeros_like(acc_ref)

        src = j * sh + lax.broadcasted_iota(jnp.int32, (sh, tc), 0)
        dst = ct * tc + lax.broadcasted_iota(jnp.int32, (sh, tc), 1)
        place = (_w_in_dest(src) == dst).astype(BF16)
        acc_ref[...] += jnp.dot(g_ref[...], place, preferred_element_type=F32)

        @pl.when(j == N_DEV - 1)
        def _():
            o_ref[...] = acc_ref[...].astype(o_ref.dtype)

    return pl.pallas_call(
        body, name=name, grid=(P_COLS // tc, N_DEV),
        in_specs=[pl.BlockSpec((None, d, sh), lambda ct, j: (j, 0, 0))],
        out_specs=pl.BlockSpec((d, tc), lambda ct, j: (0, ct)),
        out_shape=jax.ShapeDtypeStruct((d, P_COLS), BF16),
        scratch_shapes=[pltpu.VMEM((d, tc), F32)],
        compiler_params=_cparams(("parallel", "arbitrary")),
    )(g)


def _unplace_w_in(dw, *, name):
    d = dw.shape[0]
    sh = W_IN_SHARD

    def body(dw_ref, o_ref):
        j = pl.program_id(0)
        src = j * sh + lax.broadcasted_iota(jnp.int32, (P_COLS, sh), 1)
        dst = lax.broadcasted_iota(jnp.int32, (P_COLS, sh), 0)
        pick = (_w_in_dest(src) == dst).astype(BF16)
        x = dw_ref[...]
        hi = x.astype(BF16)
        r1 = x - hi.astype(F32)
        mid = r1.astype(BF16)
        lo = (r1 - mid.astype(F32)).astype(BF16)
        o_ref[...] = (jnp.dot(hi, pick, preferred_element_type=F32) + jnp.dot(mid, pick, preferred_element_type=F32)
                      + jnp.dot(lo, pick, preferred_element_type=F32))

    return pl.pallas_call(
        body, name=name, grid=(N_DEV,),
        in_specs=[pl.BlockSpec((d, P_COLS), lambda j: (0, 0))],
        out_specs=pl.BlockSpec((None, d, sh), lambda j: (j, 0, 0)),
        out_shape=jax.ShapeDtypeStruct((N_DEV, d, sh), F32),
        compiler_params=_cparams(("arbitrary",)),
    )(dw)


def _swiglu3(gu, *, name):
    _, t_rows, w = gu.shape
    tile = min(ROW_TILE, t_rows)
    g2 = gu.reshape(2, 4, t_rows, w)

    def body(g_ref, o_ref):
        gate, up = g_ref[0], g_ref[1]
        o_ref[...] = (gate * _sigmoid(gate) * up).astype(o_ref.dtype)

    return pl.pallas_call(
        body, name=name, grid=(4, t_rows // tile),
        in_specs=[pl.BlockSpec((2, None, tile, w), lambda j, i: (0, j, i, 0))],
        out_specs=pl.BlockSpec((None, tile, w), lambda j, i: (j, i, 0)),
        out_shape=jax.ShapeDtypeStruct((4, t_rows, w), BF16),
        compiler_params=_cparams(("parallel", "parallel")),
    )(g2)


def _swiglu3_bwd(gu, dact, *, name):
    _, t_rows, w = gu.shape
    tile = min(ROW_TILE, t_rows)
    g2 = gu.reshape(2, 4, t_rows, w)

    def body(g_ref, d_ref, o_ref):
        gate, up, d = g_ref[0], g_ref[1], d_ref[...]
        sg = _sigmoid(gate)
        o_ref[0] = d * up * (sg * (1.0 + gate * (1.0 - sg)))
        o_ref[1] = d * gate * sg

    blk = pl.BlockSpec((2, None, tile, w), lambda j, i: (0, j, i, 0))
    out = pl.pallas_call(
        body, name=name, grid=(4, t_rows // tile),
        in_specs=[blk, pl.BlockSpec((None, tile, w), lambda j, i: (j, i, 0))],
        out_specs=blk,
        out_shape=jax.ShapeDtypeStruct((2, 4, t_rows, w), F32),
        compiler_params=_cparams(("parallel", "parallel")),
    )(g2, dact)
    return out.reshape(8, t_rows, w)


BIG = ("w_in", "mla_w_uq", "mla_w_ukv", "w_out", "w_gate_up", "w_down", "ple_w_gate", "ple_w_proj")
SMALL = ("ln_in_g", "ln_in_b", "hgrn_lb_logits", "hgrn_norm_g", "sgu_ln_g", "sgu_ln_b", "sgu_w_s", "sgu_b_s",
         "mla_q_norm_g", "mla_kv_norm_g", "ln1_g", "ln1_b", "ln2_g", "ln2_b")
ORDER = ("ln_in_g", "ln_in_b", "w_in", "hgrn_lb_logits", "hgrn_norm_g", "sgu_ln_g", "sgu_ln_b", "sgu_w_s", "sgu_b_s",
         "mla_q_norm_g", "mla_w_uq", "mla_kv_norm_g", "mla_w_ukv", "w_out", "ln1_g", "ln1_b", "w_gate_up", "w_down",
         "ple_w_gate", "ple_w_proj", "ln2_g", "ln2_b")


def _slab(a, align):
    s = a.reshape(-1, LANES)
    pad = -s.shape[0] % align
    return jnp.pad(s, ((0, pad), (0, 0))) if pad else s


def _pack(arrays, align=16, total_align=512):
    s = jnp.concatenate([_slab(a, align) for a in arrays], axis=0)
    pad = -s.shape[0] % total_align
    return jnp.pad(s, ((0, pad), (0, 0))) if pad else s


def _unpack(slab, shapes, align=16):
    out, r0 = [], 0
    for s in shapes:
        nr = math.prod(s) // LANES
        out.append(slab[r0:r0 + nr].reshape(s))
        r0 += nr + (-nr % align)
    return out


def _gather_layer_weights(w, li, *, name):
    uq_pad = ((0, 0), (0, LANES - ATT_D))
    shards = [w["w_in"][li], jnp.pad(w["mla_w_uq"][li], uq_pad), w["mla_w_ukv"][li], w["w_out"][li],
              w["w_gate_up"][li], w["w_down"][li], w["ple_w_gate"][li], w["ple_w_proj"][li]]
    g = dict(zip(BIG, _all_gather([s.astype(BF16) for s in shards], name=name)))
    rows = lambda a: a.reshape(a.shape[0] * a.shape[1], a.shape[2])
    g["w_in"] = _place_w_in(g["w_in"], name=name + "_place_w_in")
    for k in ("w_out", "w_down", "ple_w_gate"):
        g[k] = rows(g[k])
    return g


def _layer_forward(li, h, p_i, wts, sm, lbs, tables, alpha):
    n = f"l{li}_"
    row1 = lambda a: a.reshape(1, -1)
    projp = _mm(h, wts["w_in"], name=n + "proj")
    ng = row1(sm["hgrn_norm_g"][li])
    o_a, o_pre, states = _hgrn_fwd(projp, lbs[li], ng, name=n + "hgrn_fwd")
    lg, lbias = row1(sm["sgu_ln_g"][li]), row1(sm["sgu_ln_b"][li])
    w_s = sm["sgu_w_s"][li]
    bias_full = jnp.repeat(sm["sgu_b_s"][li].T, HEAD, axis=1)
    o_b = _sgu_fwd(projp, lg, lbias, w_s, bias_full, name=n + "sgu_fwd")
    qg, kvg = row1(sm["mla_q_norm_g"][li]), row1(sm["mla_kv_norm_g"][li])
    cq_view, ckv_view = (projp, 384, P_CQ // 384), (projp, 256, P_CKV // 256)
    (cqn,) = _rowwise(_fn_rms, [cq_view], [qg], [(384, BF16)], name=n + "q_norm")
    (ckvn,) = _rowwise(_fn_rms, [ckv_view], [kvg], [(256, BF16)], name=n + "kv_norm")
    q = _mm(cqn, wts["mla_w_uq"], bm="bkn", name=n + "uq")
    kv = _mm(ckvn, wts["mla_w_ukv"], bm="bkn", name=n + "ukv")
    qr, kf, kvb = _mla_prep(q, kv, projp, tables, name=n + "mla_prep")
    o_c, lse = _attn_fwd(qr, kf, kvb, name=n + "attn_fwd")
    cat = jnp.concatenate([o_a, o_b, o_c], axis=1)
    mix = _mm(cat, wts["w_out"], name=n + "out_proj")
    g1, b1 = row1(sm["ln1_g"][li]), row1(sm["ln1_b"][li])
    d = h.shape[1]
    (h1,) = _rowwise(_make_post_mix(alpha), [h, mix], [g1, b1], [(d, F32)], name=n + "ln1")
    gu = _mm(h1, wts["w_gate_up"], bm="bkn", om="bmn", name=n + "gate_up")
    act = _swiglu3(gu, name=n + "swiglu")
    ffn = _mm(act, wts["w_down"], am="bmk", name=n + "down")
    pg = _mm(h1, wts["ple_w_gate"], name=n + "ple_gate")
    pp = _mm(p_i, wts["ple_w_proj"], bm="bkn", name=n + "ple_proj")
    g2, b2 = row1(sm["ln2_g"][li]), row1(sm["ln2_b"][li])
    (h2,) = _rowwise(_make_ple_ln(alpha), [h1, ffn, pg, pp], [g2, b2], [(d, F32)], name=n + "ln2")
    saved = dict(h=h, projp=projp, o_pre=o_pre, states=states, cqn=cqn, ckvn=ckvn, qr=qr, kf=kf, kvb=kvb, o_c=o_c,
                 lse=lse, cat=cat, mix=mix, h1=h1, gu=gu, act=act, ffn=ffn, pg=pg, pp=pp, ng=ng, lg=lg,
                 lbias=lbias, w_s=w_s, bias_full=bias_full, qg=qg, kvg=kvg, g1=g1, b1=b1, g2=g2, b2=b2)
    return h2, saved


def _layer_backward(li, dh2_parts, p_i, wts, sv, lbs, tables, alpha):
    n = f"l{li}_b_"
    gr = {}
    dh1_a, dffn, dpg, dpp, gr["ln2_g"], gr["ln2_b"] = _rowwise_vjp(
        _make_ple_ln(alpha), [sv["h1"], sv["ffn"], sv["pg"], sv["pp"]], [sv["g2"], sv["b2"]], [dh2_parts],
        groups=[[0], [1], [2], [3]], name=n + "ln2")
    gr["ple_w_proj"] = _mm(p_i, dpp, am="km", tn=LANES, om="bmn", name=n + "ple_proj_dw")
    gr["ple_w_gate"] = _mm(sv["h1"], dpg, am="km", name=n + "ple_gate_dw")
    dh1_b = _mm(dpg, wts["ple_w_gate"], bm="nk", name=n + "ple_gate_dx")
    gr["w_down"] = _mm(sv["act"], dffn, am="bkm", name=n + "down_dw")
    dact = _mm(dffn, wts["w_down"], bm="nk", tn=sv["act"].shape[2], om="bmn", name=n + "down_dx")
    dgu = _swiglu3_bwd(sv["gu"], dact, name=n + "swiglu")
    gr["w_gate_up"] = _mm(sv["h1"], dgu, am="km", bm="bkn", om="bmn", name=n + "gate_up_dw")
    dh1_c = _mm(dgu, wts["w_gate_up"], am="bmk", bm="bnk", name=n + "gate_up_dx")
    dh_a, dmix, gr["ln1_g"], gr["ln1_b"] = _rowwise_vjp(
        _make_post_mix(alpha), [sv["h"], sv["mix"]], [sv["g1"], sv["b1"]], [[dh1_a, dh1_b, dh1_c]],
        groups=[[0], [1]], name=n + "ln1")
    gr["w_out"] = _mm(sv["cat"], dmix, am="km", name=n + "out_proj_dw")
    dcat = _mm(dmix, wts["w_out"], bm="nk", name=n + "out_proj_dx")

    dqr, dkv, dkf = _attn_bwd(sv["qr"], sv["kf"], sv["kvb"], dcat, sv["o_c"], sv["lse"], name=n + "attn")
    dqpad, dkr = _mla_prep_bwd(dqr, dkf, tables, name=n + "mla_prep")
    gr["mla_w_uq"] = _mm(sv["cqn"], dqpad, am="km", tn=LANES, om="bmn", name=n + "uq_dw")
    dcqn = _mm(dqpad, wts["mla_w_uq"], bm="bnk", name=n + "uq_dx")
    gr["mla_w_ukv"] = _mm(sv["ckvn"], dkv, am="km", tn=LANES, om="bmn", name=n + "ukv_dw")
    dckvn = _mm(dkv, wts["mla_w_ukv"], bm="bnk", name=n + "ukv_dx")
    projp = sv["projp"]
    dcq, gr["mla_q_norm_g"] = _rowwise_vjp(_fn_rms, [(projp, 384, P_CQ // 384)], [sv["qg"]], [[dcqn]],
                                           groups=[[0]], name=n + "q_norm")
    dckv, gr["mla_kv_norm_g"] = _rowwise_vjp(_fn_rms, [(projp, 256, P_CKV // 256)], [sv["kvg"]], [[dckvn]],
                                             groups=[[0]], name=n + "kv_norm")
    dsgu, gr["sgu_ln_g"], gr["sgu_ln_b"], gr["sgu_w_s"], gr["sgu_b_s"] = _sgu_bwd(
        projp, sv["lg"], sv["lbias"], sv["w_s"], sv["bias_full"], dcat, name=n + "sgu")
    dhg, gr["hgrn_norm_g"], gr["lower_bound"] = _hgrn_bwd(
        projp, lbs[li], sv["ng"], sv["o_pre"], sv["states"], dcat, name=n + "hgrn")
    dprojp = jnp.concatenate([dhg, dsgu, dcq, dkr, dckv], axis=1)
    gr["w_in"] = _unplace_w_in(_mm(sv["h"], dprojp, am="km", name=n + "proj_dw"), name=n + "proj_dw_shards")
    dh_b = _mm(dprojp, wts["w_in"], bm="nk", name=n + "proj_dx")
    return [dh_a, dh_b], gr


def kernel(x, p, positions, ln_in_g, ln_in_b, w_in, hgrn_lb_logits, hgrn_norm_g, sgu_ln_g, sgu_ln_b, sgu_w_s, sgu_b_s, mla_q_norm_g, mla_w_uq, mla_kv_norm_g, mla_w_ukv, w_out, ln1_g, ln1_b, w_gate_up, w_down, ple_w_gate, ple_w_proj, ln2_g, ln2_b, loss_target, m_ln_in_g, m_ln_in_b, m_w_in, m_hgrn_lb_logits, m_hgrn_norm_g, m_sgu_ln_g, m_sgu_ln_b, m_sgu_w_s, m_sgu_b_s, m_mla_q_norm_g, m_mla_w_uq, m_mla_kv_norm_g, m_mla_w_ukv, m_w_out, m_ln1_g, m_ln1_b, m_w_gate_up, m_w_down, m_ple_w_gate, m_ple_w_proj, m_ln2_g, m_ln2_b, v_ln_in_g, v_ln_in_b, v_w_in, v_hgrn_lb_logits, v_hgrn_norm_g, v_sgu_ln_g, v_sgu_ln_b, v_sgu_w_s, v_sgu_b_s, v_mla_q_norm_g, v_mla_w_uq, v_mla_kv_norm_g, v_mla_w_ukv, v_w_out, v_ln1_g, v_ln1_b, v_w_gate_up, v_w_down, v_ple_w_gate, v_ple_w_proj, v_ln2_g, v_ln2_b):
    args = dict(locals())
    w = {k: args[k] for k in ORDER}
    m = {k: args["m_" + k] for k in ORDER}
    v = {k: args["v_" + k] for k in ORDER}
    depth = w_in.shape[0]
    assert depth == 2, "the lower-bound kernel is written for two layers"
    alpha = (2 * depth) ** 0.25
    xs, tgt = x[0], loss_target[0]
    d_model = xs.shape[1]

    layer_w = [_gather_layer_weights(w, li, name=f"gather_l{li}") for li in range(depth)]

    tables = _rope_tables(positions[0])
    row1 = lambda a: a.reshape(1, -1)
    l0, l1 = row1(hgrn_lb_logits[0]), row1(hgrn_lb_logits[1])
    lbs = _rowwise(_fn_lower_bounds, [l0, l1], [], [(HG_W, F32), (HG_W, F32)], name="lower_bounds")

    gin, bin_ = row1(ln_in_g), row1(ln_in_b)
    (h,) = _rowwise(_fn_ln, [xs], [gin, bin_], [(d_model, F32)], name="ln_in")
    saved = []
    for li in range(depth):
        h, sv = _layer_forward(li, h, p[li, 0], layer_w[li], w, lbs, tables, alpha)
        saved.append(sv)
    dy, loss_local = _loss_and_grad(h, tgt, name="loss")
    loss = lax.psum(loss_local[0, 0], ("x", "y", "c"))

    dparts, grads = [dy], [None] * depth
    for li in reversed(range(depth)):
        dparts, grads[li] = _layer_backward(li, dparts, p[li, 0], layer_w[li], saved[li], lbs, tables, alpha)
    dx, d_gin, d_bin = _rowwise_vjp(_fn_ln, [xs], [gin, bin_], [dparts], groups=[[0]], name="ln_in_b")
    dl0, dl1 = _rowwise_vjp(_fn_lower_bounds, [l0, l1], [], [[grads[0]["lower_bound"]], [grads[1]["lower_bound"]]],
                            groups=[[0], [1]], name="lower_bounds_b")

    core = lax.axis_index("c").astype(jnp.int32).reshape(1)
    prefixes = ("grad_", "delta_", "new_m_", "new_v_")
    per_layer = {pre + k: [] for pre in prefixes for k in BIG}
    uq_pad = ((0, 0), (0, 0), (0, LANES - ATT_D))
    state = {k: ((jnp.pad(w[k], uq_pad), jnp.pad(m[k], uq_pad), jnp.pad(v[k], uq_pad)) if k == "mla_w_uq"
                 else (w[k], m[k], v[k])) for k in BIG}
    for li in reversed(range(depth)):
        n = f"rs_l{li}_"
        parts = [grads[li][k] for k in BIG]
        parts = [g.reshape((4, 2, g.shape[0] // N_DEV) + g.shape[1:]) if g.ndim == 2
                 else g.reshape((4, 2) + g.shape[1:]) for g in parts]
        theirs = _pair_exchange(parts, name=n + "pair")
        sums = [_pair_add(x, r, core, name=n + "pair_add_" + k) for k, x, r in zip(BIG, parts, theirs)]
        quads = _quad_exchange(sums, name=n + "quad")
        for k, q4 in zip(BIG, quads):
            res4 = _adamw(q4, *state[k], li, name=f"adamw_l{li}_{k}")
            for pre, a in zip(prefixes, res4):
                per_layer[pre + k].append(a[:, :ATT_D] if k == "mla_w_uq" else a)
    out = {name: jnp.stack(vals[::-1]) for name, vals in per_layer.items()}

    small_g = {"ln_in_g": d_gin.reshape(-1), "ln_in_b": d_bin.reshape(-1),
               "hgrn_lb_logits": jnp.stack([dl0.reshape(-1), dl1.reshape(-1)])}
    for k in SMALL[3:]:
        small_g[k] = jnp.stack([grads[li][k].reshape(w[k].shape[1:]) for li in range(depth)])
    (small_parts,) = _all_gather([_pack([small_g[k] for k in SMALL])], name="gather_small_grads")
    slabs = _adamw(small_parts, _pack([w[k] for k in SMALL]), _pack([m[k] for k in SMALL]),
                   _pack([v[k] for k in SMALL]), None, name="adamw_small")
    shapes = [w[k].shape for k in SMALL]
    for pre, slab in zip(prefixes, slabs):
        for k, a in zip(SMALL, _unpack(slab, shapes)):
            out[pre + k] = a
    res = [loss, dx[None]]
    for prefix in ("grad_", "delta_", "new_m_", "new_v_"):
        res += [out[prefix + k] for k in ORDER]
    return tuple(res)
```

```python
import functools
import math

import jax
import jax.numpy as jnp
from jax import lax
from jax.experimental import pallas as pl
from jax.experimental.pallas import tpu as pltpu

F32 = jnp.float32
BF16 = jnp.bfloat16
MESH = pl.DeviceIdType.MESH

LN_EPS = 1e-5
RMS_EPS = 1e-6
ROPE_THETA = 10000.0
ADAM_LR, ADAM_B1, ADAM_B2, ADAM_EPS, ADAM_WD, ADAM_STEP = 0.001, 0.9, 0.999, 1e-08, 0.01, 10

N_DEV = 8
LANES = 128
HG_CHUNK = 16
HG_W = 256
HEAD = 64
SGU_CHUNK = 128
N_ATT_HEADS = 8
ATT_D = 96
VMEM_LIMIT = 56 * 1024 * 1024

HG_TILE = 256
ATT_TQ = 512
ROW_TILE = 256

P_CQ, P_KR, P_CKV, P_COLS = 1536, 1920, 2048, 2304


def _cparams(sem):
    return pltpu.CompilerParams(dimension_semantics=sem, vmem_limit_bytes=VMEM_LIMIT)


_ANY = pl.BlockSpec(memory_space=pl.ANY)


def _call(body, operands, *, name, grid, in_specs, out_specs, out_shape, sem, scratch_shapes=(), job=None):
    if job is None:
        return pl.pallas_call(body, name=name, grid=grid, in_specs=in_specs, out_specs=out_specs, out_shape=out_shape,
                              scratch_shapes=list(scratch_shapes), compiler_params=_cparams(sem))(*operands)
    single = not isinstance(out_shape, (list, tuple))
    shapes = [out_shape] if single else list(out_shape)
    ospecs = [out_specs] if single else list(out_specs)
    ni, no, ns = len(operands), len(shapes), len(scratch_shapes)
    ji, jo = len(job.inputs), len(job.out_shapes)

    def hosted(*refs):
        p = 0
        parts = []
        for cnt in (ni, ji, no, jo, ns):
            parts.append(refs[p:p + cnt])
            p += cnt
        ins, jins, outs, jouts, scr = parts
        jsems = refs[p:]
        ids = [pl.program_id(a) for a in range(len(grid))]
        first = functools.reduce(lambda a, b: a & b, [i == 0 for i in ids])
        last = functools.reduce(lambda a, b: a & b, [i == g - 1 for i, g in zip(ids, grid)])

        @pl.when(first)
        def _():
            job.start(jins, jouts, jsems)

        body(*ins, *outs, *scr)

        @pl.when(last)
        def _():
            job.finish(jins, jouts, jsems)

    res = pl.pallas_call(
        hosted, name=name, grid=grid,
        in_specs=list(in_specs) + [_ANY] * ji, out_specs=ospecs + [_ANY] * jo,
        out_shape=shapes + list(job.out_shapes),
        scratch_shapes=list(scratch_shapes) + [pltpu.SemaphoreType.DMA((c,)) for c in job.sem_counts],
        compiler_params=_cparams(("arbitrary",) * len(grid)),
    )(*operands, *job.inputs)
    own = res[0] if single else res[:no]
    return own, res[no:]


class _Job:
    def __init__(self, inputs, out_shapes, sem_counts, start, finish):
        self.inputs, self.out_shapes, self.sem_counts = list(inputs), list(out_shapes), list(sem_counts)
        self.start, self.finish = start, finish


def _copies_job(inputs, out_shapes, n_remote, n_local, make):
    def start(jins, jouts, sems):
        sends, _, local = make(jins, jouts, *sems)
        for cp in local + sends:
            cp.start()

    def finish(jins, jouts, sems):
        sends, recvs, local = make(jins, jouts, *sems)
        for cp in recvs:
            cp.wait_recv()
        for cp in sends:
            cp.wait_send()
        for cp in local:
            cp.wait()

    return _Job(inputs, out_shapes, [n_remote, n_remote, max(n_local, 1)], start, finish)


def _run_job(job, *, name):
    ji, jo = len(job.inputs), len(job.out_shapes)

    def body(*refs):
        jins, jouts, sems = refs[:ji], refs[ji:ji + jo], refs[ji + jo:]
        job.start(jins, jouts, sems)
        job.finish(jins, jouts, sems)

    return pl.pallas_call(
        body, name=name, out_shape=list(job.out_shapes), in_specs=[_ANY] * ji, out_specs=[_ANY] * jo,
        scratch_shapes=[pltpu.SemaphoreType.DMA((c,)) for c in job.sem_counts],
    )(*job.inputs)


def _tile(n, pref):
    if n % pref == 0:
        return pref
    best = None
    t = LANES
    while t <= min(n, pref):
        if n % t == 0:
            best = t
        t += LANES
    return best if best is not None else n


def _mm(a, b, *, am="mk", bm="kn", om="mn", out_dtype=F32, tm=1024, tn=1024, tk=512, name, job=None):
    if am == "mk":
        m, k = a.shape
    elif am == "km":
        k, m = a.shape
    elif am == "bmk":
        m, tk = a.shape[1], a.shape[2]
        k = a.shape[0] * tk
    else:
        k, tm = a.shape[1], a.shape[2]
        m = a.shape[0] * tm
    if bm == "kn":
        kb_, n = b.shape
    elif bm == "nk":
        n, kb_ = b.shape
    elif bm == "bkn":
        kb_, tn = b.shape[1], b.shape[2]
        n = b.shape[0] * tn
    else:
        n, tk = b.shape[1], b.shape[2]
        kb_ = b.shape[0] * tk
    assert kb_ == k, (a.shape, b.shape, am, bm)
    tm, tn, tk = _tile(m, tm), _tile(n, tn), _tile(k, tk)
    nk = k // tk
    dims = (((0 if am in ("km", "bkm") else 1,), (1 if bm in ("nk", "bnk") else 0,)), ((), ()))

    a_spec = {"mk": pl.BlockSpec((tm, tk), lambda i, j, kk: (i, kk)),
              "km": pl.BlockSpec((tk, tm), lambda i, j, kk: (kk, i)),
              "bmk": pl.BlockSpec((None, tm, tk), lambda i, j, kk: (kk, i, 0)),
              "bkm": pl.BlockSpec((None, tk, tm), lambda i, j, kk: (i, kk, 0))}[am]
    b_spec = {"kn": pl.BlockSpec((tk, tn), lambda i, j, kk: (kk, j)),
              "nk": pl.BlockSpec((tn, tk), lambda i, j, kk: (j, kk)),
              "bkn": pl.BlockSpec((None, tk, tn), lambda i, j, kk: (j, kk, 0)),
              "bnk": pl.BlockSpec((None, tn, tk), lambda i, j, kk: (kk, j, 0))}[bm]
    if om == "mn":
        o_spec, o_shape = pl.BlockSpec((tm, tn), lambda i, j, kk: (i, j)), (m, n)
    else:
        o_spec, o_shape = pl.BlockSpec((None, tm, tn), lambda i, j, kk: (j, i, 0)), (n // tn, m, tn)

    def body(a_ref, b_ref, o_ref, acc_ref):
        kk = pl.program_id(2)

        @pl.when(kk == 0)
        def _():
            acc_ref[...] = jnp.zeros_like(acc_ref)

        acc_ref[...] += lax.dot_general(a_ref[...].astype(BF16), b_ref[...].astype(BF16), dims,
                                        preferred_element_type=F32)

        @pl.when(kk == nk - 1)
        def _():
            o_ref[...] = acc_ref[...].astype(o_ref.dtype)

    return _call(body, (a, b), name=name, grid=(m // tm, n // tn, nk), in_specs=[a_spec, b_spec], out_specs=o_spec,
                 out_shape=jax.ShapeDtypeStruct(o_shape, out_dtype), scratch_shapes=[pltpu.VMEM((tm, tn), F32)],
                 sem=("parallel", "parallel", "arbitrary"), job=job)


def _row_operand(a, tile):
    if isinstance(a, tuple):
        arr, w, j = a
        return arr, pl.BlockSpec((tile, w), lambda i, j=j: (i, j))
    return a, pl.BlockSpec((tile, a.shape[1]), lambda i: (i, 0))


def _const_spec(c):
    nd = c.ndim
    return pl.BlockSpec(c.shape, lambda i, nd=nd: (0,) * nd)


def _rowwise(fn, rows, consts, outs, *, name, accs=(), tile=None):
    t_rows = (rows[0][0] if isinstance(rows[0], tuple) else rows[0]).shape[0]
    tile = min(tile or ROW_TILE, t_rows)
    arrs, specs = zip(*[_row_operand(a, tile) for a in rows])
    nin, no = len(rows) + len(consts), len(outs)

    def body(*refs):
        res = fn(*[r[...] for r in refs[:nin]])
        for r, v in zip(refs[nin:nin + no], res[:no]):
            r[...] = v.astype(r.dtype)
        if accs:
            a_refs = refs[nin + no:]

            @pl.when(pl.program_id(0) == 0)
            def _():
                for r in a_refs:
                    r[...] = jnp.zeros_like(r)

            for r, v in zip(a_refs, res[no:]):
                r[...] += v

    out_shape = [jax.ShapeDtypeStruct((t_rows, w), dt) for w, dt in outs]
    out_shape += [jax.ShapeDtypeStruct(s, F32) for s in accs]
    out_specs = [pl.BlockSpec((tile, w), lambda i: (i, 0)) for w, _ in outs]
    out_specs += [pl.BlockSpec(s, lambda i, nd=len(s): (0,) * nd) for s in accs]
    return pl.pallas_call(
        body, name=name, grid=(t_rows // tile,),
        in_specs=list(specs) + [_const_spec(c) for c in consts],
        out_specs=out_specs, out_shape=out_shape,
        compiler_params=_cparams(("arbitrary",)),
    )(*arrs, *consts)


def _rowwise_vjp(fn, rows, consts, cts, *, name, groups, tile=None):
    t_rows = (rows[0][0] if isinstance(rows[0], tuple) else rows[0]).shape[0]
    tile = min(tile or ROW_TILE, t_rows)
    arrs, specs = zip(*[_row_operand(a, tile) for a in rows])
    flat_cts = [c for group in cts for c in group]
    ct_arrs, ct_specs = zip(*[_row_operand(a, tile) for a in flat_cts])
    nr, nc, nct, ng = len(rows), len(consts), len(flat_cts), len(groups)

    def width(a):
        return a[1] if isinstance(a, tuple) else a.shape[1]

    def body(*refs):
        rv = [r[...] for r in refs[:nr]]
        cv = [r[...] for r in refs[nr:nr + nc]]
        ct_refs = refs[nr + nc:nr + nc + nct]
        ctv, pos = [], 0
        for group in cts:
            s = ct_refs[pos][...]
            for r in ct_refs[pos + 1:pos + len(group)]:
                s = s + r[...]
            ctv.append(s)
            pos += len(group)
        _, pull = jax.vjp(fn, *rv, *cv)
        grads = pull(tuple(ctv))
        g_refs = refs[nr + nc + nct:nr + nc + nct + ng]
        for r, idx in zip(g_refs, groups):
            parts = [grads[i] for i in idx]
            r[...] = parts[0] if len(parts) == 1 else jnp.concatenate(parts, axis=1)
        c_refs = refs[nr + nc + nct + ng:]

        @pl.when(pl.program_id(0) == 0)
        def _():
            for r in c_refs:
                r[...] = jnp.zeros_like(r)

        for r, v in zip(c_refs, grads[nr:]):
            r[...] += v

    gw = [sum(width(rows[i]) for i in idx) for idx in groups]
    out_shape = [jax.ShapeDtypeStruct((t_rows, w), F32) for w in gw]
    out_shape += [jax.ShapeDtypeStruct(c.shape, F32) for c in consts]
    out_specs = [pl.BlockSpec((tile, w), lambda i: (i, 0)) for w in gw]
    out_specs += [_const_spec(c) for c in consts]
    return pl.pallas_call(
        body, name=name, grid=(t_rows // tile,),
        in_specs=list(specs) + [_const_spec(c) for c in consts] + list(ct_specs),
        out_specs=out_specs, out_shape=out_shape,
        compiler_params=_cparams(("arbitrary",)),
    )(*arrs, *consts, *ct_arrs)


def _layer_norm(x, g, b):
    mu = jnp.mean(x, axis=-1, keepdims=True)
    xc = x - mu
    var = jnp.mean(xc * xc, axis=-1, keepdims=True)
    return xc * lax.rsqrt(var + LN_EPS) * g + b


def _sigmoid(x):
    return 1.0 / (1.0 + jnp.exp(-x))


def _fn_ln(x, g, b):
    return (_layer_norm(x, g, b),)


def _fn_rms(x, g):
    return (x * lax.rsqrt(jnp.mean(x * x, axis=-1, keepdims=True) + RMS_EPS) * g,)


def _make_post_mix(alpha):
    def fn(h, mix, g, b):
        return (_layer_norm(alpha * h + mix, g, b),)
    return fn


def _make_ple_ln(alpha):
    def fn(h1, ffn, pg, pp, g, b):
        return (_layer_norm(alpha * h1 + ffn + _sigmoid(pg) * pp, g, b),)
    return fn


def _fn_lower_bounds(l0, l1):
    m = jnp.maximum(l0, l1)
    e0, e1 = jnp.exp(l0 - m), jnp.exp(l1 - m)
    s = e0 + e1
    p0, p1 = e0 / s, e1 / s
    return (p0 - p0, (p0 + p1) - p0)


def _loss_and_grad(y, target, *, name):
    d = y.shape[1]

    def fn(yv, tv):
        err = yv - tv
        return err * (1.0 / d), 0.5 * jnp.sum(jnp.mean(err * err, axis=-1, keepdims=True), axis=0, keepdims=True)

    return _rowwise(fn, [y, target], [], [(d, F32)], accs=[(1, 1)], name=name)


def _split_dot(x, e_bf16):
    hi = x.astype(BF16)
    lo = (x - hi.astype(F32)).astype(BF16)
    return (jnp.dot(hi, e_bf16, preferred_element_type=F32) + jnp.dot(lo, e_bf16, preferred_element_type=F32))


def _hgrn_common(th):
    rm = lax.broadcasted_iota(jnp.int32, (th, HG_W), 0) % HG_CHUNK

    def seg_cumsum(x):
        for s in (1, 2, 4, 8):
            x = x + jnp.where(rm >= s, pltpu.roll(x, s, 0), 0.0)
        return x

    def seg_rcumsum(x):
        for s in (1, 2, 4, 8):
            x = x + jnp.where(rm < HG_CHUNK - s, pltpu.roll(x, th - s, 0), 0.0)
        return x

    ri = lax.broadcasted_iota(jnp.int32, (HG_W, HG_W), 0) // HEAD
    ci = lax.broadcasted_iota(jnp.int32, (HG_W, HG_W), 1) // HEAD
    head_f32 = (ri == ci).astype(F32)
    head_bf16 = head_f32.astype(BF16)

    def headsum(x):
        return _split_dot(x, head_bf16)

    return rm, seg_cumsum, seg_rcumsum, head_f32, headsum


def _hgrn_gates(qr, fl, lb):
    sg = _sigmoid(fl)
    f = lb + (1.0 - lb) * sg
    sq = _sigmoid(qr)
    return sg, f, jnp.log(f), 1.0 - f, qr * sq, sq


def _shifted(x, d, th):
    return x if d == 0 else pltpu.roll(x, d, 0)


def _unshift(x, d, th):
    return x if d == 0 else pltpu.roll(x, th - d, 0)


def _hgrn_fwd(projp, lb, ng, *, name, job=None):
    t_rows = projp.shape[0]
    th = min(HG_TILE, t_rows)
    nct = th // HG_CHUNK

    def body(q_ref, f_ref, i_ref, g_ref, lb_ref, ng_ref, oa_ref, opre_ref, st_out_ref,
             st_ref, vtm_ref, kv_ref, qe_ref, dec_ref, oint_ref):
        rm, seg_cumsum, seg_rcumsum, head_f32, headsum = _hgrn_common(th)

        @pl.when(pl.program_id(0) == 0)
        def _():
            st_ref[...] = jnp.zeros_like(st_ref)

        qr, fl, v, g = q_ref[...], f_ref[...], i_ref[...], g_ref[...]
        _, f, lf, k, q, _ = _hgrn_gates(qr, fl, lb_ref[...])
        b = seg_cumsum(lf)

        o = jnp.zeros((th, HG_W), F32)
        for d in range(HG_CHUNK):
            kd, bd, vd = _shifted(k, d, th), _shifted(b, d, th), _shifted(v, d, th)
            e = jnp.exp(jnp.where(rm >= d, b - bd, -1e30))
            o = o + headsum(q * kd * e) * vd

        blast = seg_rcumsum(jnp.where(rm == HG_CHUNK - 1, b, 0.0))
        kte = (k * jnp.exp(blast - b)).astype(BF16)
        qe_ref[...] = q * jnp.exp(b)
        dec_ref[...] = jnp.exp(blast)
        vt = v.T
        lane_chunk = lax.broadcasted_iota(jnp.int32, (HG_W, th), 1) // HG_CHUNK
        for c in range(nct):
            vtm_ref[c * HG_W:(c + 1) * HG_W, :] = jnp.where(lane_chunk == c, vt, 0.0).astype(BF16)
        kv_ref[...] = jnp.dot(vtm_ref[...], kte, preferred_element_type=F32)

        def step(c, carry):
            r0 = pl.multiple_of(c * HG_CHUNK, HG_CHUNK)
            s = st_ref[...]
            st_out_ref[c] = s
            oint_ref[pl.ds(r0, HG_CHUNK), :] = lax.dot_general(
                qe_ref[pl.ds(r0, HG_CHUNK), :].astype(BF16), s.astype(BF16),
                (((1,), (1,)), ((), ())), preferred_element_type=F32)
            dec = jnp.max(dec_ref[pl.ds(r0, HG_CHUNK), :], axis=0, keepdims=True)
            kv_c = kv_ref[pl.ds(pl.multiple_of(c * HG_W, HG_W), HG_W), :]
            st_ref[...] = s * dec + kv_c * head_f32
            return carry

        lax.fori_loop(0, nct, step, 0)

        o = o + oint_ref[...]
        opre_ref[...] = o
        r = lax.rsqrt(headsum(o * o) * (1.0 / HEAD) + RMS_EPS)
        oa_ref[...] = o * r * ng_ref[...] * (g * _sigmoid(g))

    col = lambda j: pl.BlockSpec((th, HG_W), lambda i, j=j: (i, j))
    vec = pl.BlockSpec((1, HG_W), lambda i: (0, 0))
    row = pl.BlockSpec((th, HG_W), lambda i: (i, 0))
    n_chunks = t_rows // HG_CHUNK
    return _call(
        body, (projp, projp, projp, projp, lb, ng), name=name, grid=(t_rows // th,),
        in_specs=[col(0), col(1), col(2), col(3), vec, vec],
        out_specs=[row, row, pl.BlockSpec((nct, HG_W, HG_W), lambda i: (i, 0, 0))],
        out_shape=[jax.ShapeDtypeStruct((t_rows, HG_W), F32), jax.ShapeDtypeStruct((t_rows, HG_W), F32),
                   jax.ShapeDtypeStruct((n_chunks, HG_W, HG_W), F32)],
        scratch_shapes=[pltpu.VMEM((HG_W, HG_W), F32), pltpu.VMEM((nct * HG_W, th), BF16),
                        pltpu.VMEM((nct * HG_W, HG_W), F32), pltpu.VMEM((th, HG_W), F32),
                        pltpu.VMEM((th, HG_W), F32), pltpu.VMEM((th, HG_W), F32)],
        sem=("arbitrary",), job=job)


def _hgrn_bwd(projp, lb, ng, opre, states, dcat, *, name):
    t_rows = projp.shape[0]
    th = min(HG_TILE, t_rows)
    nct = th // HG_CHUNK
    nt = t_rows // th

    def body(q_ref, f_ref, i_ref, g_ref, lb_ref, ng_ref, opre_ref, st_in_ref, do_ref,
             dproj_ref, dng_ref, dlb_ref,
             gst_ref, dotm_ref, qg_ref, v_ref, kte_ref, dop_ref, dec_ref, dkte_ref, dvi_ref, dqe_ref, ddec_ref):
        rm, seg_cumsum, seg_rcumsum, head_f32, headsum = _hgrn_common(th)

        @pl.when(pl.program_id(0) == 0)
        def _():
            gst_ref[...] = jnp.zeros_like(gst_ref)
            dng_ref[...] = jnp.zeros_like(dng_ref)
            dlb_ref[...] = jnp.zeros_like(dlb_ref)

        qr, fl, v, g = q_ref[...], f_ref[...], i_ref[...], g_ref[...]
        lb, ngv = lb_ref[...], ng_ref[...]
        sg, f, lf, k, q, sq = _hgrn_gates(qr, fl, lb)
        b = seg_cumsum(lf)
        blast = seg_rcumsum(jnp.where(rm == HG_CHUNK - 1, b, 0.0))
        eb = jnp.exp(b)
        ekb = jnp.exp(blast - b)
        qe, kte, dec = q * eb, k * ekb, jnp.exp(blast)

        do_out, op = do_ref[...], opre_ref[...]
        sgg = _sigmoid(g)
        sil = g * sgg
        r = lax.rsqrt(headsum(op * op) * (1.0 / HEAD) + RMS_EPS)
        on = op * r
        dng_ref[...] += jnp.sum(do_out * on * sil, axis=0, keepdims=True)
        dg = do_out * on * ngv * (sgg * (1.0 + g * (1.0 - sgg)))
        don = do_out * ngv * sil
        dop = r * (don - on * (headsum(don * on) * (1.0 / HEAD)))

        v_ref[...] = v
        kte_ref[...] = kte
        dop_ref[...] = dop
        dec_ref[...] = dec
        dot_t = dop.T
        lane_chunk = lax.broadcasted_iota(jnp.int32, (HG_W, th), 1) // HG_CHUNK
        for c in range(nct):
            dotm_ref[c * HG_W:(c + 1) * HG_W, :] = jnp.where(lane_chunk == c, dot_t, 0.0).astype(BF16)
        qg_ref[...] = jnp.dot(dotm_ref[...], qe.astype(BF16), preferred_element_type=F32)

        def step(j, carry):
            c = nct - 1 - j
            r0 = pl.multiple_of(c * HG_CHUNK, HG_CHUNK)
            rows = pl.ds(r0, HG_CHUNK)
            gs = gst_ref[...]
            s = st_in_ref[c]
            gm = (gs * head_f32).astype(BF16)
            dkte_ref[rows, :] = jnp.dot(v_ref[rows, :].astype(BF16), gm, preferred_element_type=F32)
            dvi_ref[rows, :] = lax.dot_general(kte_ref[rows, :].astype(BF16), gm, (((1,), (1,)), ((), ())),
                                               preferred_element_type=F32)
            dqe_ref[rows, :] = jnp.dot(dop_ref[rows, :].astype(BF16), s.astype(BF16), preferred_element_type=F32)
            ddec_ref[rows, :] = jnp.broadcast_to(jnp.sum(gs * s, axis=0, keepdims=True), (HG_CHUNK, HG_W))
            dec_c = jnp.max(dec_ref[rows, :], axis=0, keepdims=True)
            qg_c = qg_ref[pl.ds(pl.multiple_of(c * HG_W, HG_W), HG_W), :]
            gst_ref[...] = gs * dec_c + qg_c * head_f32
            return carry

        lax.fori_loop(0, nct, step, 0)

        dkte, dqe = dkte_ref[...], dqe_ref[...]
        dq = dqe * eb
        dk = dkte * ekb
        db = dqe * qe - dkte * kte
        dv = dvi_ref[...]
        dblast = dkte * kte + jnp.where(rm == HG_CHUNK - 1, ddec_ref[...] * dec, 0.0)

        for d in range(HG_CHUNK):
            kd, bd, vd = _shifted(k, d, th), _shifted(b, d, th), _shifted(v, d, th)
            e = jnp.exp(jnp.where(rm >= d, b - bd, -1e30))
            p = q * kd * e
            sc = headsum(p)
            dsc = headsum(dop * vd)
            dv = dv + _unshift(sc * dop, d, th)
            dq = dq + dsc * kd * e
            dk = dk + _unshift(dsc * q * e, d, th)
            darg = dsc * p
            db = db + darg - _unshift(darg, d, th)

        db = db + jnp.where(rm == HG_CHUNK - 1, seg_cumsum(dblast), 0.0)
        dlf = seg_rcumsum(db)
        df = dlf / f - dk
        dlb_ref[...] += jnp.sum(df * (1.0 - sg), axis=0, keepdims=True)
        dfl = df * (1.0 - lb) * sg * (1.0 - sg)
        dqr = dq * (sq * (1.0 + qr * (1.0 - sq)))
        dproj_ref[...] = jnp.concatenate([dqr, dfl, dv, dg], axis=1)

    rev = lambda i: nt - 1 - i
    col = lambda j: pl.BlockSpec((th, HG_W), lambda i, j=j: (rev(i), j))
    vec = pl.BlockSpec((1, HG_W), lambda i: (0, 0))
    row = pl.BlockSpec((th, HG_W), lambda i: (rev(i), 0))
    tile_f32 = pltpu.VMEM((th, HG_W), F32)
    return pl.pallas_call(
        body, name=name, grid=(nt,),
        in_specs=[col(0), col(1), col(2), col(3), vec, vec, row,
                  pl.BlockSpec((nct, HG_W, HG_W), lambda i: (rev(i), 0, 0)), col(0)],
        out_specs=[pl.BlockSpec((th, 4 * HG_W), lambda i: (rev(i), 0)), vec, vec],
        out_shape=[jax.ShapeDtypeStruct((t_rows, 4 * HG_W), F32), jax.ShapeDtypeStruct((1, HG_W), F32),
                   jax.ShapeDtypeStruct((1, HG_W), F32)],
        scratch_shapes=[pltpu.VMEM((HG_W, HG_W), F32), pltpu.VMEM((nct * HG_W, th), BF16),
                        pltpu.VMEM((nct * HG_W, HG_W), F32)] + [tile_f32] * 8,
        compiler_params=_cparams(("arbitrary",)),
    )(projp, projp, projp, projp, lb, ng, opre, states, dcat)


_INV_SQRT2 = 1.0 / math.sqrt(2.0)
_INV_SQRT2PI = 1.0 / math.sqrt(2.0 * math.pi)


def _gelu(x):
    return 0.5 * x * (1.0 + lax.erf(x * _INV_SQRT2))


def _gelu_grad(x):
    return 0.5 * (1.0 + lax.erf(x * _INV_SQRT2)) + x * jnp.exp(-0.5 * x * x) * _INV_SQRT2PI


def _sgu_parts(bu, bv, lg, lbias, w_ref, n_groups):
    c = SGU_CHUNK
    tril = (lax.broadcasted_iota(jnp.int32, (c, c), 0) >= lax.broadcasted_iota(jnp.int32, (c, c), 1)).astype(F32)
    gid = lax.broadcasted_iota(jnp.int32, bu.shape, 1) // HEAD
    u = _gelu(bu)
    gv = _gelu(bv)
    mu = jnp.mean(gv, axis=-1, keepdims=True)
    xc = gv - mu
    rstd = lax.rsqrt(jnp.mean(xc * xc, axis=-1, keepdims=True) + LN_EPS)
    xhat = xc * rstd
    vn = xhat * lg + lbias
    ws = [w_ref[gi] * tril for gi in range(n_groups)]
    return tril, gid, u, rstd, xhat, vn, ws


def _sgu_fwd(projp, lg, lbias, w_s, bias_full, *, name):
    t_rows = projp.shape[0]
    n_groups = w_s.shape[0]
    c = SGU_CHUNK

    def body(u_ref, v_ref, lg_ref, lb_ref, w_ref, bias_ref, o_ref):
        _, gid, u, _, _, vn, ws = _sgu_parts(u_ref[...], v_ref[...], lg_ref[...], lb_ref[...], w_ref, n_groups)
        vnb = vn.astype(BF16)
        z = bias_ref[...]
        for gi in range(n_groups):
            z = z + jnp.where(gid == gi, jnp.dot(ws[gi].astype(BF16), vnb, preferred_element_type=F32), 0.0)
        o_ref[...] = u * z

    col = lambda j: pl.BlockSpec((c, HG_W), lambda i, j=j: (i, j))
    return pl.pallas_call(
        body, name=name, grid=(t_rows // c,),
        in_specs=[col(4), col(5), _const_spec(lg), _const_spec(lbias), _const_spec(w_s), _const_spec(bias_full)],
        out_specs=pl.BlockSpec((c, HG_W), lambda i: (i, 0)),
        out_shape=jax.ShapeDtypeStruct((t_rows, HG_W), F32),
        compiler_params=_cparams(("arbitrary",)),
    )(projp, projp, lg, lbias, w_s, bias_full)


def _sgu_bwd(projp, lg, lbias, w_s, bias_full, dcat, *, name):
    t_rows = projp.shape[0]
    n_groups = w_s.shape[0]
    c = SGU_CHUNK
    n = t_rows // c

    def body(u_ref, v_ref, lg_ref, lb_ref, w_ref, bias_ref, do_ref,
             dproj_ref, dlg_ref, dlb_ref, dw_ref, dbs_ref, dbias_acc):
        i = pl.program_id(0)

        @pl.when(i == 0)
        def _():
            dlg_ref[...] = jnp.zeros_like(dlg_ref)
            dlb_ref[...] = jnp.zeros_like(dlb_ref)
            dw_ref[...] = jnp.zeros_like(dw_ref)
            dbias_acc[...] = jnp.zeros_like(dbias_acc)

        bu, bv, lg_v = u_ref[...], v_ref[...], lg_ref[...]
        tril, gid, u, rstd, xhat, vn, ws = _sgu_parts(bu, bv, lg_v, lb_ref[...], w_ref, n_groups)
        vnb = vn.astype(BF16)
        z = bias_ref[...]
        for gi in range(n_groups):
            z = z + jnp.where(gid == gi, jnp.dot(ws[gi].astype(BF16), vnb, preferred_element_type=F32), 0.0)
        do = do_ref[...]
        dbu = do * z * _gelu_grad(bu)
        dz = do * u
        dbias_acc[...] += dz
        dvn = jnp.zeros_like(dz)
        for gi in range(n_groups):
            dzg = jnp.where(gid == gi, dz, 0.0).astype(BF16)
            dw_ref[gi] += lax.dot_general(dzg, vnb, (((1,), (1,)), ((), ())), preferred_element_type=F32) * tril
            dvn = dvn + jnp.dot(ws[gi].T.astype(BF16), dzg, preferred_element_type=F32)
        dlg_ref[...] += jnp.sum(dvn * xhat, axis=0, keepdims=True)
        dlb_ref[...] += jnp.sum(dvn, axis=0, keepdims=True)
        dxh = dvn * lg_v
        dgv = rstd * (dxh - jnp.mean(dxh, axis=-1, keepdims=True)
                      - xhat * jnp.mean(dxh * xhat, axis=-1, keepdims=True))
        dproj_ref[...] = jnp.concatenate([dbu, dgv * _gelu_grad(bv)], axis=1)

        @pl.when(i == n - 1)
        def _():
            dbs_ref[...] = jnp.sum(dbias_acc[...].T.reshape(n_groups, HEAD, c), axis=1)

    col = lambda j: pl.BlockSpec((c, HG_W), lambda i, j=j: (i, j))
    return pl.pallas_call(
        body, name=name, grid=(n,),
        in_specs=[col(4), col(5), _const_spec(lg), _const_spec(lbias), _const_spec(w_s), _const_spec(bias_full),
                  col(1)],
        out_specs=[pl.BlockSpec((c, 2 * HG_W), lambda i: (i, 0)), _const_spec(lg), _const_spec(lbias),
                   _const_spec(w_s), pl.BlockSpec((n_groups, c), lambda i: (0, 0))],
        out_shape=[jax.ShapeDtypeStruct((t_rows, 2 * HG_W), F32), jax.ShapeDtypeStruct(lg.shape, F32),
                   jax.ShapeDtypeStruct(lbias.shape, F32), jax.ShapeDtypeStruct(w_s.shape, F32),
                   jax.ShapeDtypeStruct((n_groups, c), F32)],
        scratch_shapes=[pltpu.VMEM((c, HG_W), F32)],
        compiler_params=_cparams(("arbitrary",)),
    )(projp, projp, lg, lbias, w_s, bias_full, dcat)


def _rope_tables(positions):
    t = positions.shape[0]
    inv_freq = ROPE_THETA ** (-jnp.arange(0, 32, 2, dtype=F32) / 32)
    ang = positions.astype(F32)[:, None] * inv_freq
    cos, sin = jnp.cos(ang), jnp.sin(ang)
    z = lambda w: jnp.zeros((t, w), F32)
    cos_t = jnp.concatenate([jnp.ones((t, 64), F32), cos, cos, z(32)], axis=1)
    sin_up = jnp.concatenate([z(80), sin, z(32)], axis=1)
    sin_dn = jnp.concatenate([z(64), -sin, z(48)], axis=1)
    return cos_t, sin_up, sin_dn


def _rep(x, n):
    return x if n == 1 else jnp.concatenate([x] * n, axis=1)


def _rope(x, cos_t, sin_up, sin_dn):
    w = x.shape[1]
    return x * cos_t + pltpu.roll(x, 16, 1) * sin_up + pltpu.roll(x, w - 16, 1) * sin_dn


def _rope_t(dy, cos_t, sin_up, sin_dn):
    w = dy.shape[1]
    return dy * cos_t + pltpu.roll(dy * sin_up, w - 16, 1) + pltpu.roll(dy * sin_dn, 16, 1)


def _mla_prep(q, kv, projp, tables, *, name):
    nh = N_ATT_HEADS

    def fn(qv, kvv, kr, cos_t, sin_up, sin_dn):
        qr = _rope(qv, _rep(cos_t, nh), _rep(sin_up, nh), _rep(sin_dn, nh))
        krr = _rope(kr, cos_t, sin_up, sin_dn)
        lane = lax.broadcasted_iota(jnp.int32, kvv.shape, 1) % LANES
        return qr, jnp.where(lane < HEAD, kvv, 0.0) + _rep(krr, nh), kvv

    w = q.shape[1]
    return _rowwise(fn, [q, kv, (projp, LANES, P_KR // LANES)] + list(tables), [],
                    [(w, BF16), (w, BF16), (w, BF16)], name=name)


def _mla_prep_bwd(dqr, dkf, tables, *, name):
    nh = N_ATT_HEADS

    def fn(dq, dk, cos_t, sin_up, sin_dn):
        dqp = _rope_t(dq, _rep(cos_t, nh), _rep(sin_up, nh), _rep(sin_dn, nh))
        dkrr = dk[:, 0:LANES]
        for h in range(1, nh):
            dkrr = dkrr + dk[:, LANES * h:LANES * (h + 1)]
        return dqp, _rope_t(dkrr, cos_t, sin_up, sin_dn)

    return _rowwise(fn, [dqr, dkf] + list(tables), [], [(dqr.shape[1], F32), (LANES, F32)], name=name)


_NT = (((1,), (1,)), ((), ()))
_TN = (((0,), (0,)), ((), ()))


def _attn_fwd(qr, kf, kvb, *, name, job=None):
    t_rows = qr.shape[0]
    tq = min(ATT_TQ, t_rows)
    nb = t_rows // tq
    scale = ATT_D ** -0.5

    def body(q_ref, kf_ref, kvb_ref, o_ref, lse_ref):
        qi = pl.program_id(1)
        lane = lax.broadcasted_iota(jnp.int32, (tq, LANES), 1)
        causal = (lax.broadcasted_iota(jnp.int32, (tq, tq), 1) <= lax.broadcasted_iota(jnp.int32, (tq, tq), 0))
        outs = []
        for hh in range(2):
            cols = slice(hh * LANES, (hh + 1) * LANES)
            q = q_ref[:, cols]

            def block(ki, carry, diagonal, q=q, cols=cols):
                m_old, l_old, acc = carry
                rows = pl.ds(pl.multiple_of(ki * tq, tq), tq)
                s = lax.dot_general(q, kf_ref[rows, cols], _NT, preferred_element_type=F32) * scale
                if diagonal:
                    s = jnp.where(causal, s, -1e30)
                m_new = jnp.maximum(m_old, jnp.max(s, axis=-1, keepdims=True))
                p = jnp.exp(s - m_new)
                a = jnp.exp(m_old - m_new)
                return (m_new, a * l_old + jnp.sum(p, axis=-1, keepdims=True),
                        a * acc + jnp.dot(p.astype(BF16), kvb_ref[rows, cols], preferred_element_type=F32))

            init = (jnp.full((tq, 1), -1e30, F32), jnp.zeros((tq, 1), F32), jnp.zeros((tq, LANES), F32))
            carry = lax.fori_loop(0, qi, lambda ki, c, block=block: block(ki, c, False), init)
            m_fin, l_fin, acc = block(qi, carry, True)
            lse_ref[hh] = m_fin + jnp.log(l_fin)
            outs.append(acc / l_fin)
        o_ref[...] = jnp.where(lane < HEAD, pltpu.roll(outs[0], HEAD, 1), outs[1])

    pair = pl.BlockSpec((t_rows, 2 * LANES), lambda pr, qi: (0, pr))
    return _call(
        body, (qr, kf, kvb), name=name, grid=(N_ATT_HEADS // 2, nb),
        in_specs=[pl.BlockSpec((tq, 2 * LANES), lambda pr, qi: (qi, pr)), pair, pair],
        out_specs=[pl.BlockSpec((tq, LANES), lambda pr, qi: (qi, pr)),
                   pl.BlockSpec((2, tq, 1), lambda pr, qi: (pr, qi, 0))],
        out_shape=[jax.ShapeDtypeStruct((t_rows, N_ATT_HEADS * HEAD), F32),
                   jax.ShapeDtypeStruct((N_ATT_HEADS, t_rows, 1), F32)],
        sem=("parallel", "arbitrary"), job=job)


def _attn_bwd(qr, kf, kvb, dcat, o, lse, *, name, job=None):
    t_rows = qr.shape[0]
    tq = min(ATT_TQ, t_rows)
    nb = t_rows // tq
    scale = ATT_D ** -0.5
    do_off = 2 * HG_W // LANES

    def body(q_ref, kf_ref, kvb_ref, do_ref, o_ref, lse_ref, dq_ref, dkv_ref, dk_ref):
        ki = pl.program_id(1)

        @pl.when(ki == 0)
        def _():
            dq_ref[...] = jnp.zeros_like(dq_ref)

        lane = lax.broadcasted_iota(jnp.int32, (tq, LANES), 1)
        causal = (lax.broadcasted_iota(jnp.int32, (tq, tq), 1) <= lax.broadcasted_iota(jnp.int32, (tq, tq), 0))
        dkvs, dks = [], []
        for hh in range(2):
            cols = slice(hh * LANES, (hh + 1) * LANES)
            k, v = kf_ref[:, cols], kvb_ref[:, cols]

            def block(qi, carry, diagonal, hh=hh, cols=cols, k=k, v=v):
                dk, dv = carry
                rows = pl.ds(pl.multiple_of(qi * tq, tq), tq)
                q = q_ref[rows, cols]
                do, ov = do_ref[rows, :], o_ref[rows, :]
                if hh == 0:
                    do, ov = pltpu.roll(do, HEAD, 1), pltpu.roll(ov, HEAD, 1)
                do = jnp.where(lane >= HEAD, do, 0.0)
                delta = jnp.sum(do * ov, axis=-1, keepdims=True)
                s = lax.dot_general(q, k, _NT, preferred_element_type=F32) * scale
                if diagonal:
                    s = jnp.where(causal, s, -1e30)
                p = jnp.exp(s - lse_ref[hh, rows, :])
                dob = do.astype(BF16)
                dv = dv + lax.dot_general(p.astype(BF16), dob, _TN, preferred_element_type=F32)
                dp = lax.dot_general(dob, v, _NT, preferred_element_type=F32)
                ds = (p * (dp - delta) * scale).astype(BF16)
                dk = dk + lax.dot_general(ds, q, _TN, preferred_element_type=F32)
                dq_ref[rows, cols] += jnp.dot(ds, k, preferred_element_type=F32)
                return dk, dv

            zero = jnp.zeros((tq, LANES), F32)
            carry = block(ki, (zero, zero), True)
            dk, dv = lax.fori_loop(ki + 1, nb, lambda qi, c, block=block: block(qi, c, False), carry)
            dks.append(dk)
            dkvs.append(jnp.where(lane < HEAD, dk, dv))
        dkv_ref[...] = jnp.concatenate(dkvs, axis=1)
        dk_ref[...] = jnp.concatenate(dks, axis=1)

    pair_all = pl.BlockSpec((t_rows, 2 * LANES), lambda pr, ki: (0, pr))
    pair_blk = pl.BlockSpec((tq, 2 * LANES), lambda pr, ki: (ki, pr))
    wide = jax.ShapeDtypeStruct((t_rows, N_ATT_HEADS * LANES), F32)
    return _call(
        body, (qr, kf, kvb, dcat, o, lse), name=name, grid=(N_ATT_HEADS // 2, nb),
        in_specs=[pair_all, pair_blk, pair_blk,
                  pl.BlockSpec((t_rows, LANES), lambda pr, ki: (0, do_off + pr)),
                  pl.BlockSpec((t_rows, LANES), lambda pr, ki: (0, pr)),
                  pl.BlockSpec((2, t_rows, 1), lambda pr, ki: (pr, 0, 0))],
        out_specs=[pair_all, pair_blk, pair_blk],
        out_shape=[wide, wide, wide],
        sem=("parallel", "arbitrary"), job=job)


def _my_pos():
    return lax.axis_index("x"), lax.axis_index("y"), lax.axis_index("c")


def _all_gather(xs, *, name):
    return _gather_forward(_run_job(_gather_job(xs), name=name), name=name + "_forward")


def _remote(src, dst, send_sems, recv_sems, k, dev):
    return pltpu.make_async_remote_copy(src_ref=src, dst_ref=dst, send_sem=send_sems.at[k], recv_sem=recv_sems.at[k],
                                        device_id=dev, device_id_type=MESH)


def _gather_job(xs):
    n = len(xs)

    def make(x_refs, out_refs, send_sems, recv_sems, local_sems):
        mx, my, mc = _my_pos()
        mine = 4 * mx + 2 * my + mc
        peers = [(mx, my, 1 - mc), (1 - mx, my, mc), (mx, 1 - my, mc), (1 - mx, 1 - my, mc)]
        sends, recvs, local = [], [], []
        for a in range(n):
            local.append(pltpu.make_async_copy(x_refs[a], out_refs[a].at[mine], local_sems.at[a]))
            for k, dev in enumerate(peers):
                theirs = 4 * dev[0] + 2 * dev[1] + dev[2]
                sends.append(_remote(x_refs[a], out_refs[a].at[mine], send_sems, recv_sems, 4 * a + k, dev))
                recvs.append(_remote(x_refs[a], out_refs[a].at[theirs], send_sems, recv_sems, 4 * a + k, dev))
        return sends, recvs, local

    shapes = [jax.ShapeDtypeStruct((N_DEV,) + x.shape, x.dtype) for x in xs]
    return _copies_job(xs, shapes, 4 * n, n, make)


def _gather_forward(gs, *, name):
    n = len(gs)

    def body(*refs):
        out_refs = refs[n:2 * n]
        send_sems, recv_sems = refs[2 * n:]
        mx, my, mc = _my_pos()
        chips = [(1 - mx, my), (mx, 1 - my), (1 - mx, 1 - my)]
        sends, recvs = [], []
        for a in range(n):
            for j, (cx, cy) in enumerate(chips):
                here, there = out_refs[a].at[4 * cx + 2 * cy + mc], out_refs[a].at[4 * cx + 2 * cy + 1 - mc]
                sends.append(_remote(here, here, send_sems, recv_sems, 3 * a + j, (mx, my, 1 - mc)))
                recvs.append(_remote(here, there, send_sems, recv_sems, 3 * a + j, (mx, my, 1 - mc)))
        for cp in sends:
            cp.start()
        for cp in recvs:
            cp.wait_recv()
        for cp in sends:
            cp.wait_send()

    return pl.pallas_call(
        body, name=name, out_shape=[jax.ShapeDtypeStruct(g.shape, g.dtype) for g in gs],
        in_specs=[_ANY] * n, out_specs=[_ANY] * n, input_output_aliases={a: a for a in range(n)},
        scratch_shapes=[pltpu.SemaphoreType.DMA((3 * n,)), pltpu.SemaphoreType.DMA((3 * n,))],
    )(*gs)


def _pair_job(xs):
    n = len(xs)

    def make(x_refs, out_refs, send_sems, recv_sems, local_sems):
        mx, my, mc = _my_pos()
        copies = [_remote(x_refs[a].at[g, 1 - mc], out_refs[a].at[g], send_sems, recv_sems, 4 * a + g, (mx, my, 1 - mc))
                  for a in range(n) for g in range(4)]
        return copies, copies, []

    shapes = [jax.ShapeDtypeStruct((4,) + x.shape[2:], x.dtype) for x in xs]
    return _copies_job(xs, shapes, 4 * n, 0, make)


def _pair_add(x, r, core, *, name):
    _, _, a, b = x.shape
    ta = _row_tile(a, 256)

    def body(c_ref, x_ref, r_ref, o_ref):
        o_ref[...] = x_ref[...] + r_ref[...]

    blk = pl.BlockSpec((None, ta, b), lambda g, i, c_ref: (g, i, 0))
    return pl.pallas_call(
        body, name=name,
        grid_spec=pltpu.PrefetchScalarGridSpec(
            num_scalar_prefetch=1, grid=(4, a // ta),
            in_specs=[pl.BlockSpec((None, None, ta, b), lambda g, i, c_ref: (g, c_ref[0], i, 0)), blk],
            out_specs=blk),
        out_shape=jax.ShapeDtypeStruct((4, a, b), x.dtype),
        compiler_params=_cparams(("parallel", "parallel")),
    )(core, x, r)


def _quad_job(xs):
    n = len(xs)

    def make(x_refs, out_refs, send_sems, recv_sems, local_sems):
        mx, my, mc = _my_pos()
        mine = 2 * mx + my
        peers = [((1 - mx, my, mc), 2 * (1 - mx) + my), ((mx, 1 - my, mc), 2 * mx + 1 - my),
                 ((1 - mx, 1 - my, mc), 2 * (1 - mx) + 1 - my)]
        sends, recvs, local = [], [], []
        for a in range(n):
            local.append(pltpu.make_async_copy(x_refs[a].at[mine], out_refs[a].at[mine], local_sems.at[a]))
            for k, (dev, g) in enumerate(peers):
                sends.append(_remote(x_refs[a].at[g], out_refs[a].at[mine], send_sems, recv_sems, 3 * a + k, dev))
                recvs.append(_remote(x_refs[a].at[g], out_refs[a].at[g], send_sems, recv_sems, 3 * a + k, dev))
        return sends, recvs, local

    shapes = [jax.ShapeDtypeStruct(x.shape, x.dtype) for x in xs]
    return _copies_job(xs, shapes, 3 * n, n, make)


def _row_tile(r, pref):
    t = min(pref, r)
    while r % t or (t % 8 and t != r):
        t -= 1
    return t


def _adamw(parts, w, m, v, layer, *, name, tile=256):
    g, a, b = parts.shape
    tile = _row_tile(a, tile)
    c1 = 1.0 / (1.0 - ADAM_B1 ** ADAM_STEP)
    c2 = 1.0 / (1.0 - ADAM_B2 ** ADAM_STEP)

    def body(p_ref, w_ref, m_ref, v_ref, g_ref, d_ref, mo_ref, vo_ref):
        grad = p_ref[0]
        for j in range(1, g):
            grad = grad + p_ref[j]
        mn = ADAM_B1 * m_ref[...] + (1.0 - ADAM_B1) * grad
        vn = ADAM_B2 * v_ref[...] + (1.0 - ADAM_B2) * (grad * grad)
        g_ref[...] = grad
        mo_ref[...] = mn
        vo_ref[...] = vn
        d_ref[...] = -ADAM_LR * ((mn * c1) / (jnp.sqrt(vn * c2) + ADAM_EPS) + ADAM_WD * w_ref[...])

    slab = pl.BlockSpec((tile, b), lambda i: (i, 0))
    src = slab if layer is None else pl.BlockSpec((None, tile, b), lambda i: (layer, i, 0))
    return pl.pallas_call(
        body, name=name, grid=(a // tile,),
        in_specs=[pl.BlockSpec((g, tile, b), lambda i: (0, i, 0)), src, src, src],
        out_specs=[slab] * 4,
        out_shape=[jax.ShapeDtypeStruct((a, b), F32)] * 4,
        compiler_params=_cparams(("parallel",)),
    )(parts, w, m, v)


W_IN_SHARD = 276


def _w_in_dest(col):
    return jnp.where(col < P_KR, col, jnp.where(col < P_KR + 256, col + (P_CKV - P_KR), col - 2176 + P_KR + HEAD))


def _place_w_in(g, *, name):
    _, d, sh = g.shape
    tc = 768

    def body(g_ref, o_ref, acc_ref):
        ct, j = pl.program_id(0), pl.program_id(1)

        @pl.when(j == 0)
        def _():
            acc_ref[...] = jnp.zeros_like(acc_ref)

        src = j * sh + lax.broadcasted_iota(jnp.int32, (sh, tc), 0)
        dst = ct * tc + lax.broadcasted_iota(jnp.int32, (sh, tc), 1)
        place = (_w_in_dest(src) == dst).astype(BF16)
        acc_ref[...] += jnp.dot(g_ref[...], place, preferred_element_type=F32)

        @pl.when(j == N_DEV - 1)
        def _():
            o_ref[...] = acc_ref[...].astype(o_ref.dtype)

    return pl.pallas_call(
        body, name=name, grid=(P_COLS // tc, N_DEV),
        in_specs=[pl.BlockSpec((None, d, sh), lambda ct, j: (j, 0, 0))],
        out_specs=pl.BlockSpec((d, tc), lambda ct, j: (0, ct)),
        out_shape=jax.ShapeDtypeStruct((d, P_COLS), BF16),
        scratch_shapes=[pltpu.VMEM((d, tc), F32)],
        compiler_params=_cparams(("parallel", "arbitrary")),
    )(g)


def _unplace_w_in(dw, *, name):
    d = dw.shape[0]
    sh = W_IN_SHARD

    def body(dw_ref, o_ref):
        j = pl.program_id(0)
        src = j * sh + lax.broadcasted_iota(jnp.int32, (P_COLS, sh), 1)
        dst = lax.broadcasted_iota(jnp.int32, (P_COLS, sh), 0)
        pick = (_w_in_dest(src) == dst).astype(BF16)
        x = dw_ref[...]
        hi = x.astype(BF16)
        r1 = x - hi.astype(F32)
        mid = r1.astype(BF16)
        lo = (r1 - mid.astype(F32)).astype(BF16)
        o_ref[...] = (jnp.dot(hi, pick, preferred_element_type=F32) + jnp.dot(mid, pick, preferred_element_type=F32)
                      + jnp.dot(lo, pick, preferred_element_type=F32))

    return pl.pallas_call(
        body, name=name, grid=(N_DEV,),
        in_specs=[pl.BlockSpec((d, P_COLS), lambda j: (0, 0))],
        out_specs=pl.BlockSpec((None, d, sh), lambda j: (j, 0, 0)),
        out_shape=jax.ShapeDtypeStruct((N_DEV, d, sh), F32),
        compiler_params=_cparams(("arbitrary",)),
    )(dw)


def _swiglu3(gu, *, name):
    _, t_rows, w = gu.shape
    tile = min(ROW_TILE, t_rows)
    g2 = gu.reshape(2, 4, t_rows, w)

    def body(g_ref, o_ref):
        gate, up = g_ref[0], g_ref[1]
        o_ref[...] = (gate * _sigmoid(gate) * up).astype(o_ref.dtype)

    return pl.pallas_call(
        body, name=name, grid=(4, t_rows // tile),
        in_specs=[pl.BlockSpec((2, None, tile, w), lambda j, i: (0, j, i, 0))],
        out_specs=pl.BlockSpec((None, tile, w), lambda j, i: (j, i, 0)),
        out_shape=jax.ShapeDtypeStruct((4, t_rows, w), BF16),
        compiler_params=_cparams(("parallel", "parallel")),
    )(g2)


def _swiglu3_bwd(gu, dact, *, name):
    _, t_rows, w = gu.shape
    tile = min(ROW_TILE, t_rows)
    g2 = gu.reshape(2, 4, t_rows, w)

    def body(g_ref, d_ref, o_ref):
        gate, up, d = g_ref[0], g_ref[1], d_ref[...]
        sg = _sigmoid(gate)
        o_ref[0] = d * up * (sg * (1.0 + gate * (1.0 - sg)))
        o_ref[1] = d * gate * sg

    blk = pl.BlockSpec((2, None, tile, w), lambda j, i: (0, j, i, 0))
    out = pl.pallas_call(
        body, name=name, grid=(4, t_rows // tile),
        in_specs=[blk, pl.BlockSpec((None, tile, w), lambda j, i: (j, i, 0))],
        out_specs=blk,
        out_shape=jax.ShapeDtypeStruct((2, 4, t_rows, w), F32),
        compiler_params=_cparams(("parallel", "parallel")),
    )(g2, dact)
    return out.reshape(8, t_rows, w)


BIG = ("w_in", "mla_w_uq", "mla_w_ukv", "w_out", "w_gate_up", "w_down", "ple_w_gate", "ple_w_proj")
SMALL = ("ln_in_g", "ln_in_b", "hgrn_lb_logits", "hgrn_norm_g", "sgu_ln_g", "sgu_ln_b", "sgu_w_s", "sgu_b_s",
         "mla_q_norm_g", "mla_kv_norm_g", "ln1_g", "ln1_b", "ln2_g", "ln2_b")
ORDER = ("ln_in_g", "ln_in_b", "w_in", "hgrn_lb_logits", "hgrn_norm_g", "sgu_ln_g", "sgu_ln_b", "sgu_w_s", "sgu_b_s",
         "mla_q_norm_g", "mla_w_uq", "mla_kv_norm_g", "mla_w_ukv", "w_out", "ln1_g", "ln1_b", "w_gate_up", "w_down",
         "ple_w_gate", "ple_w_proj", "ln2_g", "ln2_b")


def _slab(a, align):
    s = a.reshape(-1, LANES)
    pad = -s.shape[0] % align
    return jnp.pad(s, ((0, pad), (0, 0))) if pad else s


def _pack(arrays, align=16, total_align=512):
    s = jnp.concatenate([_slab(a, align) for a in arrays], axis=0)
    pad = -s.shape[0] % total_align
    return jnp.pad(s, ((0, pad), (0, 0))) if pad else s


def _unpack(slab, shapes, align=16):
    out, r0 = [], 0
    for s in shapes:
        nr = math.prod(s) // LANES
        out.append(slab[r0:r0 + nr].reshape(s))
        r0 += nr + (-nr % align)
    return out


def _blocks_to_cols(g, *, name):
    nb, a, b = g.shape

    def body(g_ref, o_ref):
        o_ref[...] = g_ref[...]

    return pl.pallas_call(
        body, name=name, grid=(nb,), in_specs=[pl.BlockSpec((None, a, b), lambda j: (j, 0, 0))],
        out_specs=pl.BlockSpec((a, b), lambda j: (0, j)), out_shape=jax.ShapeDtypeStruct((a, nb * b), g.dtype),
        compiler_params=_cparams(("parallel",)),
    )(g)


def _cols_to_blocks(x, *, name):
    a, b = x.shape[0], x.shape[1] // N_DEV

    def body(x_ref, o_ref):
        o_ref[...] = x_ref[...]

    return pl.pallas_call(
        body, name=name, grid=(N_DEV,), in_specs=[pl.BlockSpec((a, b), lambda j: (0, j))],
        out_specs=pl.BlockSpec((None, a, b), lambda j: (j, 0, 0)), out_shape=jax.ShapeDtypeStruct((N_DEV, a, b), x.dtype),
        compiler_params=_cparams(("parallel",)),
    )(x)


def _weight_shards(w, li):
    uq_pad = ((0, 0), (0, LANES - ATT_D))
    shards = {k: w[k][li] for k in BIG}
    shards["mla_w_uq"] = jnp.pad(shards["mla_w_uq"], uq_pad)
    return {k: s.astype(BF16) for k, s in shards.items()}


def _usable_weights(g, *, name):
    out = {}
    for k, a in g.items():
        if k == "w_in":
            out[k] = _place_w_in(a, name=name + "_place_w_in")
        elif k in ("w_out", "w_down", "ple_w_gate"):
            out[k] = a.reshape(a.shape[0] * a.shape[1], a.shape[2])
        elif k == "w_gate_up":
            out[k] = a
        else:
            out[k] = _blocks_to_cols(a, name=name + "_cols_" + k)
    return out


def _as_pairs(g):
    if g.ndim == 2:
        return g.reshape((4, 2, g.shape[0] // N_DEV) + g.shape[1:])
    return g.reshape((4, 2) + g.shape[1:])


def _layer_forward(li, h, p_i, wts, sm, lbs, tables, alpha, hgrn_job=None, more_weights=None, attn_job=None):
    n = f"l{li}_"
    row1 = lambda a: a.reshape(1, -1)
    projp = _mm(h, wts["w_in"], name=n + "proj")
    ng = row1(sm["hgrn_norm_g"][li])
    res = _hgrn_fwd(projp, lbs[li], ng, name=n + "hgrn_fwd", job=hgrn_job)
    if hgrn_job is not None:
        res, got = res
        wts = dict(wts, **more_weights(got))
    o_a, o_pre, states = res
    lg, lbias = row1(sm["sgu_ln_g"][li]), row1(sm["sgu_ln_b"][li])
    w_s = sm["sgu_w_s"][li]
    bias_full = jnp.repeat(sm["sgu_b_s"][li].T, HEAD, axis=1)
    o_b = _sgu_fwd(projp, lg, lbias, w_s, bias_full, name=n + "sgu_fwd")
    qg, kvg = row1(sm["mla_q_norm_g"][li]), row1(sm["mla_kv_norm_g"][li])
    cq_view, ckv_view = (projp, 384, P_CQ // 384), (projp, 256, P_CKV // 256)
    (cqn,) = _rowwise(_fn_rms, [cq_view], [qg], [(384, BF16)], name=n + "q_norm")
    (ckvn,) = _rowwise(_fn_rms, [ckv_view], [kvg], [(256, BF16)], name=n + "kv_norm")
    q = _mm(cqn, wts["mla_w_uq"], name=n + "uq")
    kv = _mm(ckvn, wts["mla_w_ukv"], name=n + "ukv")
    qr, kf, kvb = _mla_prep(q, kv, projp, tables, name=n + "mla_prep")
    res, attn_got = _attn_fwd(qr, kf, kvb, name=n + "attn_fwd", job=attn_job), None
    if attn_job is not None:
        res, attn_got = res
    o_c, lse = res
    cat = jnp.concatenate([o_a, o_b, o_c], axis=1)
    mix = _mm(cat, wts["w_out"], name=n + "out_proj")
    g1, b1 = row1(sm["ln1_g"][li]), row1(sm["ln1_b"][li])
    d = h.shape[1]
    (h1,) = _rowwise(_make_post_mix(alpha), [h, mix], [g1, b1], [(d, F32)], name=n + "ln1")
    gu = _mm(h1, wts["w_gate_up"], bm="bkn", om="bmn", name=n + "gate_up")
    act = _swiglu3(gu, name=n + "swiglu")
    ffn = _mm(act, wts["w_down"], am="bmk", name=n + "down")
    pg = _mm(h1, wts["ple_w_gate"], name=n + "ple_gate")
    pp = _mm(p_i, wts["ple_w_proj"], name=n + "ple_proj")
    g2, b2 = row1(sm["ln2_g"][li]), row1(sm["ln2_b"][li])
    (h2,) = _rowwise(_make_ple_ln(alpha), [h1, ffn, pg, pp], [g2, b2], [(d, F32)], name=n + "ln2")
    saved = dict(h=h, projp=projp, o_pre=o_pre, states=states, cqn=cqn, ckvn=ckvn, qr=qr, kf=kf, kvb=kvb, o_c=o_c,
                 lse=lse, cat=cat, mix=mix, h1=h1, gu=gu, act=act, ffn=ffn, pg=pg, pp=pp, ng=ng, lg=lg, wts=wts,
                 lbias=lbias, w_s=w_s, bias_full=bias_full, qg=qg, kvg=kvg, g1=g1, b1=b1, g2=g2, b2=b2)
    return h2, saved, attn_got


RS_EARLY = ("ple_w_proj", "ple_w_gate", "w_down", "w_gate_up", "w_out")
RS_LATE = ("mla_w_uq", "mla_w_ukv", "w_in")


def _layer_backward(li, dh2_parts, p_i, sv, lbs, tables, alpha, core, carried=None):
    n = f"l{li}_b_"
    wts = sv["wts"]
    gr = {}
    dh1_a, dffn, dpg, dpp, gr["ln2_g"], gr["ln2_b"] = _rowwise_vjp(
        _make_ple_ln(alpha), [sv["h1"], sv["ffn"], sv["pg"], sv["pp"]], [sv["g2"], sv["b2"]], [dh2_parts],
        groups=[[0], [1], [2], [3]], name=n + "ln2")
    big = {}
    big["ple_w_proj"] = _cols_to_blocks(_mm(p_i, dpp, am="km", name=n + "ple_proj_dw"), name=n + "ple_proj_dw_blocks")
    big["ple_w_gate"] = _mm(sv["h1"], dpg, am="km", name=n + "ple_gate_dw")
    dh1_b = _mm(dpg, wts["ple_w_gate"], bm="nk", name=n + "ple_gate_dx")
    big["w_down"] = _mm(sv["act"], dffn, am="bkm", name=n + "down_dw")
    dact = _mm(dffn, wts["w_down"], bm="nk", tn=sv["act"].shape[2], om="bmn", name=n + "down_dx")
    dgu = _swiglu3_bwd(sv["gu"], dact, name=n + "swiglu")
    big["w_gate_up"], carried_got = _mm(sv["h1"], dgu, am="km", bm="bkn", om="bmn", name=n + "gate_up_dw",
                                        job=carried), None
    if carried is not None:
        big["w_gate_up"], carried_got = big["w_gate_up"]
    dh1_c = _mm(dgu, wts["w_gate_up"], am="bmk", bm="bnk", name=n + "gate_up_dx")
    dh_a, dmix, gr["ln1_g"], gr["ln1_b"] = _rowwise_vjp(
        _make_post_mix(alpha), [sv["h"], sv["mix"]], [sv["g1"], sv["b1"]], [[dh1_a, dh1_b, dh1_c]],
        groups=[[0], [1]], name=n + "ln1")
    big["w_out"] = _mm(sv["cat"], dmix, am="km", name=n + "out_proj_dw")
    early = [_as_pairs(big[k]) for k in RS_EARLY]
    dcat, theirs = _mm(dmix, wts["w_out"], bm="nk", name=n + "out_proj_dx", job=_pair_job(early))
    sums = [_pair_add(x, r, core, name=n + "pair_add_" + k) for k, x, r in zip(RS_EARLY, early, theirs)]

    (dqr, dkv, dkf), early_quads = _attn_bwd(sv["qr"], sv["kf"], sv["kvb"], dcat, sv["o_c"], sv["lse"],
                                             name=n + "attn", job=_quad_job(sums))
    dqpad, dkr = _mla_prep_bwd(dqr, dkf, tables, name=n + "mla_prep")
    big["mla_w_uq"] = _cols_to_blocks(_mm(sv["cqn"], dqpad, am="km", name=n + "uq_dw"), name=n + "uq_dw_blocks")
    dcqn = _mm(dqpad, wts["mla_w_uq"], bm="nk", name=n + "uq_dx")
    big["mla_w_ukv"] = _cols_to_blocks(_mm(sv["ckvn"], dkv, am="km", name=n + "ukv_dw"), name=n + "ukv_dw_blocks")
    dckvn = _mm(dkv, wts["mla_w_ukv"], bm="nk", name=n + "ukv_dx")
    projp = sv["projp"]
    dcq, gr["mla_q_norm_g"] = _rowwise_vjp(_fn_rms, [(projp, 384, P_CQ // 384)], [sv["qg"]], [[dcqn]],
                                           groups=[[0]], name=n + "q_norm")
    dckv, gr["mla_kv_norm_g"] = _rowwise_vjp(_fn_rms, [(projp, 256, P_CKV // 256)], [sv["kvg"]], [[dckvn]],
                                             groups=[[0]], name=n + "kv_norm")
    dsgu, gr["sgu_ln_g"], gr["sgu_ln_b"], gr["sgu_w_s"], gr["sgu_b_s"] = _sgu_bwd(
        projp, sv["lg"], sv["lbias"], sv["w_s"], sv["bias_full"], dcat, name=n + "sgu")
    dhg, gr["hgrn_norm_g"], gr["lower_bound"] = _hgrn_bwd(
        projp, lbs[li], sv["ng"], sv["o_pre"], sv["states"], dcat, name=n + "hgrn")
    dprojp = jnp.concatenate([dhg, dsgu, dcq, dkr, dckv], axis=1)
    big["w_in"] = _unplace_w_in(_mm(sv["h"], dprojp, am="km", name=n + "proj_dw"), name=n + "proj_dw_shards")
    late = [_as_pairs(big[k]) for k in RS_LATE]
    dh_b, theirs = _mm(dprojp, wts["w_in"], bm="nk", name=n + "proj_dx", job=_pair_job(late))
    late_sums = [_pair_add(x, r, core, name=n + "pair_add_" + k) for k, x, r in zip(RS_LATE, late, theirs)]
    return [dh_a, dh_b], gr, early_quads, late_sums, carried_got


def kernel(x, p, positions, ln_in_g, ln_in_b, w_in, hgrn_lb_logits, hgrn_norm_g, sgu_ln_g, sgu_ln_b, sgu_w_s, sgu_b_s, mla_q_norm_g, mla_w_uq, mla_kv_norm_g, mla_w_ukv, w_out, ln1_g, ln1_b, w_gate_up, w_down, ple_w_gate, ple_w_proj, ln2_g, ln2_b, loss_target, m_ln_in_g, m_ln_in_b, m_w_in, m_hgrn_lb_logits, m_hgrn_norm_g, m_sgu_ln_g, m_sgu_ln_b, m_sgu_w_s, m_sgu_b_s, m_mla_q_norm_g, m_mla_w_uq, m_mla_kv_norm_g, m_mla_w_ukv, m_w_out, m_ln1_g, m_ln1_b, m_w_gate_up, m_w_down, m_ple_w_gate, m_ple_w_proj, m_ln2_g, m_ln2_b, v_ln_in_g, v_ln_in_b, v_w_in, v_hgrn_lb_logits, v_hgrn_norm_g, v_sgu_ln_g, v_sgu_ln_b, v_sgu_w_s, v_sgu_b_s, v_mla_q_norm_g, v_mla_w_uq, v_mla_kv_norm_g, v_mla_w_ukv, v_w_out, v_ln1_g, v_ln1_b, v_w_gate_up, v_w_down, v_ple_w_gate, v_ple_w_proj, v_ln2_g, v_ln2_b):
    args = dict(locals())
    w = {k: args[k] for k in ORDER}
    m = {k: args["m_" + k] for k in ORDER}
    v = {k: args["v_" + k] for k in ORDER}
    depth = w_in.shape[0]
    assert depth == 2, "the lower-bound kernel is written for two layers"
    alpha = (2 * depth) ** 0.25
    xs, tgt = x[0], loss_target[0]
    d_model = xs.shape[1]

    shards = [_weight_shards(w, li) for li in range(depth)]
    rest = [k for k in BIG if k != "w_in"]
    (g_in,) = _all_gather([shards[0]["w_in"]], name="gather_l0_w_in")
    w_in0 = _usable_weights({"w_in": g_in}, name="l0")

    def rest_of_layer0(got):
        got = _gather_forward(got, name="gather_l0_forward")
        return _usable_weights(dict(zip(rest, got)), name="l0")

    tables = _rope_tables(positions[0])
    row1 = lambda a: a.reshape(1, -1)
    l0, l1 = row1(hgrn_lb_logits[0]), row1(hgrn_lb_logits[1])
    lbs = _rowwise(_fn_lower_bounds, [l0, l1], [], [(HG_W, F32), (HG_W, F32)], name="lower_bounds")

    gin, bin_ = row1(ln_in_g), row1(ln_in_b)
    (h,) = _rowwise(_fn_ln, [xs], [gin, bin_], [(d_model, F32)], name="ln_in")
    h, sv0, got1 = _layer_forward(0, h, p[0, 0], w_in0, w, lbs, tables, alpha,
                                  hgrn_job=_gather_job([shards[0][k] for k in rest]), more_weights=rest_of_layer0,
                                  attn_job=_gather_job([shards[1][k] for k in BIG]))
    wts1 = _usable_weights(dict(zip(BIG, _gather_forward(got1, name="gather_l1_forward"))), name="l1")
    h, sv1, _ = _layer_forward(1, h, p[1, 0], wts1, w, lbs, tables, alpha)
    saved = [sv0, sv1]
    dy, loss_local = _loss_and_grad(h, tgt, name="loss")
    loss = lax.psum(loss_local[0, 0], ("x", "y", "c"))

    core = lax.axis_index("c").astype(jnp.int32).reshape(1)
    dparts, grads, quads, carried = [dy], [None] * depth, [None] * depth, None
    for li in reversed(range(depth)):
        dparts, grads[li], early_quads, late_sums, late_quads = _layer_backward(
            li, dparts, p[li, 0], saved[li], lbs, tables, alpha, core, carried=carried)
        quads[li] = dict(zip(RS_EARLY, early_quads))
        if carried is not None:
            quads[li + 1].update(zip(RS_LATE, late_quads))
        carried = _quad_job(late_sums)
    quads[0].update(zip(RS_LATE, _run_job(carried, name="rs_l0_late_quad")))
    dx, d_gin, d_bin = _rowwise_vjp(_fn_ln, [xs], [gin, bin_], [dparts], groups=[[0]], name="ln_in_b")
    dl0, dl1 = _rowwise_vjp(_fn_lower_bounds, [l0, l1], [], [[grads[0]["lower_bound"]], [grads[1]["lower_bound"]]],
                            groups=[[0], [1]], name="lower_bounds_b")

    prefixes = ("grad_", "delta_", "new_m_", "new_v_")
    per_layer = {pre + k: [] for pre in prefixes for k in BIG}
    uq_pad = ((0, 0), (0, 0), (0, LANES - ATT_D))
    state = {k: ((jnp.pad(w[k], uq_pad), jnp.pad(m[k], uq_pad), jnp.pad(v[k], uq_pad)) if k == "mla_w_uq"
                 else (w[k], m[k], v[k])) for k in BIG}
    for li in range(depth):
        for k in BIG:
            res4 = _adamw(quads[li][k], *state[k], li, name=f"adamw_l{li}_{k}")
            for pre, a in zip(prefixes, res4):
                per_layer[pre + k].append(a[:, :ATT_D] if k == "mla_w_uq" else a)
    out = {name: jnp.stack(vals) for name, vals in per_layer.items()}

    small_g = {"ln_in_g": d_gin.reshape(-1), "ln_in_b": d_bin.reshape(-1),
               "hgrn_lb_logits": jnp.stack([dl0.reshape(-1), dl1.reshape(-1)])}
    for k in SMALL[3:]:
        small_g[k] = jnp.stack([grads[li][k].reshape(w[k].shape[1:]) for li in range(depth)])
    (small_parts,) = _all_gather([_pack([small_g[k] for k in SMALL])], name="gather_small_grads")
    slabs = _adamw(small_parts, _pack([w[k] for k in SMALL]), _pack([m[k] for k in SMALL]),
                   _pack([v[k] for k in SMALL]), None, name="adamw_small")
    shapes = [w[k].shape for k in SMALL]
    for pre, slab in zip(prefixes, slabs):
        for k, a in zip(SMALL, _unpack(slab, shapes)):
            out[pre + k] = a
    res = [loss, dx[None]]
    for prefix in ("grad_", "delta_", "new_m_", "new_v_"):
        res += [out[prefix + k] for k in ORDER]
    return tuple(res)
```

```python
import functools
import math

import jax
import jax.numpy as jnp
from jax import lax
from jax.experimental import pallas as pl
from jax.experimental.pallas import tpu as pltpu

F32 = jnp.float32
BF16 = jnp.bfloat16
MESH = pl.DeviceIdType.MESH

LN_EPS = 1e-5
RMS_EPS = 1e-6
ROPE_THETA = 10000.0
ADAM_LR, ADAM_B1, ADAM_B2, ADAM_EPS, ADAM_WD, ADAM_STEP = 0.001, 0.9, 0.999, 1e-08, 0.01, 10

N_DEV = 8
LANES = 128
HG_CHUNK = 16
HG_W = 256
HEAD = 64
SGU_CHUNK = 128
N_ATT_HEADS = 8
ATT_D = 96
VMEM_LIMIT = 56 * 1024 * 1024

HG_TILE = 256
ATT_TQ = 512
ROW_TILE = 256

P_CQ, P_KR, P_CKV, P_COLS = 1536, 1920, 2048, 2304


def _cparams(sem):
    return pltpu.CompilerParams(dimension_semantics=sem, vmem_limit_bytes=VMEM_LIMIT)


_ANY = pl.BlockSpec(memory_space=pl.ANY)


def _call(body, operands, *, name, grid, in_specs, out_specs, out_shape, sem, scratch_shapes=(), job=None):
    if job is None:
        return pl.pallas_call(body, name=name, grid=grid, in_specs=in_specs, out_specs=out_specs, out_shape=out_shape,
                              scratch_shapes=list(scratch_shapes), compiler_params=_cparams(sem))(*operands)
    single = not isinstance(out_shape, (list, tuple))
    shapes = [out_shape] if single else list(out_shape)
    ospecs = [out_specs] if single else list(out_specs)
    ni, no, ns = len(operands), len(shapes), len(scratch_shapes)
    ji, jo = len(job.inputs), len(job.out_shapes)

    def hosted(*refs):
        p = 0
        parts = []
        for cnt in (ni, ji, no, jo, ns):
            parts.append(refs[p:p + cnt])
            p += cnt
        ins, jins, outs, jouts, scr = parts
        jsems = refs[p:]
        ids = [pl.program_id(a) for a in range(len(grid))]
        first = functools.reduce(lambda a, b: a & b, [i == 0 for i in ids])
        last = functools.reduce(lambda a, b: a & b, [i == g - 1 for i, g in zip(ids, grid)])

        @pl.when(first)
        def _():
            job.start(jins, jouts, jsems)

        body(*ins, *outs, *scr)

        @pl.when(last)
        def _():
            job.finish(jins, jouts, jsems)

    res = pl.pallas_call(
        hosted, name=name, grid=grid,
        in_specs=list(in_specs) + [_ANY] * ji, out_specs=ospecs + [_ANY] * jo,
        out_shape=shapes + list(job.out_shapes),
        scratch_shapes=list(scratch_shapes) + [pltpu.SemaphoreType.DMA((c,)) for c in job.sem_counts],
        compiler_params=_cparams(("arbitrary",) * len(grid)),
    )(*operands, *job.inputs)
    own = res[0] if single else res[:no]
    return own, res[no:]


class _Job:
    def __init__(self, inputs, out_shapes, sem_counts, start, finish):
        self.inputs, self.out_shapes, self.sem_counts = list(inputs), list(out_shapes), list(sem_counts)
        self.start, self.finish = start, finish


def _copies_job(inputs, out_shapes, n_remote, n_local, make):
    def start(jins, jouts, sems):
        sends, _, local = make(jins, jouts, *sems)
        for cp in local + sends:
            cp.start()

    def finish(jins, jouts, sems):
        sends, recvs, local = make(jins, jouts, *sems)
        for cp in recvs:
            cp.wait_recv()
        for cp in sends:
            cp.wait_send()
        for cp in local:
            cp.wait()

    return _Job(inputs, out_shapes, [n_remote, n_remote, max(n_local, 1)], start, finish)


def _run_job(job, *, name):
    ji, jo = len(job.inputs), len(job.out_shapes)

    def body(*refs):
        jins, jouts, sems = refs[:ji], refs[ji:ji + jo], refs[ji + jo:]
        job.start(jins, jouts, sems)
        job.finish(jins, jouts, sems)

    return pl.pallas_call(
        body, name=name, out_shape=list(job.out_shapes), in_specs=[_ANY] * ji, out_specs=[_ANY] * jo,
        scratch_shapes=[pltpu.SemaphoreType.DMA((c,)) for c in job.sem_counts],
    )(*job.inputs)


def _tile(n, pref):
    if n % pref == 0:
        return pref
    best = None
    t = LANES
    while t <= min(n, pref):
        if n % t == 0:
            best = t
        t += LANES
    return best if best is not None else n


def _mm(a, b, *, am="mk", bm="kn", om="mn", out_dtype=F32, tm=1024, tn=1024, tk=1024, name, job=None):
    if am == "mk":
        m, k = a.shape
    elif am == "km":
        k, m = a.shape
    elif am == "bmk":
        m, tk = a.shape[1], a.shape[2]
        k = a.shape[0] * tk
    else:
        k, tm = a.shape[1], a.shape[2]
        m = a.shape[0] * tm
    if bm == "kn":
        kb_, n = b.shape
    elif bm == "nk":
        n, kb_ = b.shape
    elif bm == "bkn":
        kb_, tn = b.shape[1], b.shape[2]
        n = b.shape[0] * tn
    else:
        n, tk = b.shape[1], b.shape[2]
        kb_ = b.shape[0] * tk
    assert kb_ == k, (a.shape, b.shape, am, bm)
    tm, tn, tk = _tile(m, tm), _tile(n, tn), _tile(k, tk)
    nk = k // tk
    dims = (((0 if am in ("km", "bkm") else 1,), (1 if bm in ("nk", "bnk") else 0,)), ((), ()))

    a_spec = {"mk": pl.BlockSpec((tm, tk), lambda i, j, kk: (i, kk)),
              "km": pl.BlockSpec((tk, tm), lambda i, j, kk: (kk, i)),
              "bmk": pl.BlockSpec((None, tm, tk), lambda i, j, kk: (kk, i, 0)),
              "bkm": pl.BlockSpec((None, tk, tm), lambda i, j, kk: (i, kk, 0))}[am]
    b_spec = {"kn": pl.BlockSpec((tk, tn), lambda i, j, kk: (kk, j)),
              "nk": pl.BlockSpec((tn, tk), lambda i, j, kk: (j, kk)),
              "bkn": pl.BlockSpec((None, tk, tn), lambda i, j, kk: (j, kk, 0)),
              "bnk": pl.BlockSpec((None, tn, tk), lambda i, j, kk: (kk, j, 0))}[bm]
    if om == "mn":
        o_spec, o_shape = pl.BlockSpec((tm, tn), lambda i, j, kk: (i, j)), (m, n)
    else:
        o_spec, o_shape = pl.BlockSpec((None, tm, tn), lambda i, j, kk: (j, i, 0)), (n // tn, m, tn)

    def body(a_ref, b_ref, o_ref, *acc):
        kk = pl.program_id(2)
        prod = lax.dot_general(a_ref[...].astype(BF16), b_ref[...].astype(BF16), dims, preferred_element_type=F32)
        if nk == 1:
            o_ref[...] = prod.astype(o_ref.dtype)
            return
        acc_ref, = acc

        @pl.when(kk == 0)
        def _():
            acc_ref[...] = prod

        if nk > 2:
            @pl.when((kk > 0) & (kk < nk - 1))
            def _():
                acc_ref[...] += prod

        @pl.when(kk == nk - 1)
        def _():
            o_ref[...] = (acc_ref[...] + prod).astype(o_ref.dtype)

    return _call(body, (a, b), name=name, grid=(m // tm, n // tn, nk), in_specs=[a_spec, b_spec], out_specs=o_spec,
                 out_shape=jax.ShapeDtypeStruct(o_shape, out_dtype),
                 scratch_shapes=[pltpu.VMEM((tm, tn), F32)] if nk > 1 else [],
                 sem=("parallel", "parallel", "arbitrary"), job=job)


def _row_operand(a, tile):
    if isinstance(a, tuple):
        arr, w, j = a
        return arr, pl.BlockSpec((tile, w), lambda i, j=j: (i, j))
    return a, pl.BlockSpec((tile, a.shape[1]), lambda i: (i, 0))


def _const_spec(c):
    nd = c.ndim
    return pl.BlockSpec(c.shape, lambda i, nd=nd: (0,) * nd)


def _rowwise(fn, rows, consts, outs, *, name, accs=(), tile=None):
    t_rows = (rows[0][0] if isinstance(rows[0], tuple) else rows[0]).shape[0]
    tile = min(tile or ROW_TILE, t_rows)
    arrs, specs = zip(*[_row_operand(a, tile) for a in rows])
    nin, no = len(rows) + len(consts), len(outs)

    def body(*refs):
        res = fn(*[r[...] for r in refs[:nin]])
        for r, v in zip(refs[nin:nin + no], res[:no]):
            r[...] = v.astype(r.dtype)
        if accs:
            a_refs = refs[nin + no:]

            @pl.when(pl.program_id(0) == 0)
            def _():
                for r in a_refs:
                    r[...] = jnp.zeros_like(r)

            for r, v in zip(a_refs, res[no:]):
                r[...] += v

    out_shape = [jax.ShapeDtypeStruct((t_rows, w), dt) for w, dt in outs]
    out_shape += [jax.ShapeDtypeStruct(s, F32) for s in accs]
    out_specs = [pl.BlockSpec((tile, w), lambda i: (i, 0)) for w, _ in outs]
    out_specs += [pl.BlockSpec(s, lambda i, nd=len(s): (0,) * nd) for s in accs]
    return pl.pallas_call(
        body, name=name, grid=(t_rows // tile,),
        in_specs=list(specs) + [_const_spec(c) for c in consts],
        out_specs=out_specs, out_shape=out_shape,
        compiler_params=_cparams(("arbitrary",)),
    )(*arrs, *consts)


def _rowwise_vjp(fn, rows, consts, cts, *, name, groups, tile=None):
    t_rows = (rows[0][0] if isinstance(rows[0], tuple) else rows[0]).shape[0]
    tile = min(tile or ROW_TILE, t_rows)
    arrs, specs = zip(*[_row_operand(a, tile) for a in rows])
    flat_cts = [c for group in cts for c in group]
    ct_arrs, ct_specs = zip(*[_row_operand(a, tile) for a in flat_cts])
    nr, nc, nct, ng = len(rows), len(consts), len(flat_cts), len(groups)

    def width(a):
        return a[1] if isinstance(a, tuple) else a.shape[1]

    def body(*refs):
        rv = [r[...] for r in refs[:nr]]
        cv = [r[...] for r in refs[nr:nr + nc]]
        ct_refs = refs[nr + nc:nr + nc + nct]
        ctv, pos = [], 0
        for group in cts:
            s = ct_refs[pos][...]
            for r in ct_refs[pos + 1:pos + len(group)]:
                s = s + r[...]
            ctv.append(s)
            pos += len(group)
        _, pull = jax.vjp(fn, *rv, *cv)
        grads = pull(tuple(ctv))
        g_refs = refs[nr + nc + nct:nr + nc + nct + ng]
        for r, idx in zip(g_refs, groups):
            parts = [grads[i] for i in idx]
            r[...] = parts[0] if len(parts) == 1 else jnp.concatenate(parts, axis=1)
        c_refs = refs[nr + nc + nct + ng:]

        @pl.when(pl.program_id(0) == 0)
        def _():
            for r in c_refs:
                r[...] = jnp.zeros_like(r)

        for r, v in zip(c_refs, grads[nr:]):
            r[...] += v

    gw = [sum(width(rows[i]) for i in idx) for idx in groups]
    out_shape = [jax.ShapeDtypeStruct((t_rows, w), F32) for w in gw]
    out_shape += [jax.ShapeDtypeStruct(c.shape, F32) for c in consts]
    out_specs = [pl.BlockSpec((tile, w), lambda i: (i, 0)) for w in gw]
    out_specs += [_const_spec(c) for c in consts]
    return pl.pallas_call(
        body, name=name, grid=(t_rows // tile,),
        in_specs=list(specs) + [_const_spec(c) for c in consts] + list(ct_specs),
        out_specs=out_specs, out_shape=out_shape,
        compiler_params=_cparams(("arbitrary",)),
    )(*arrs, *consts, *ct_arrs)


def _layer_norm(x, g, b):
    mu = jnp.mean(x, axis=-1, keepdims=True)
    xc = x - mu
    var = jnp.mean(xc * xc, axis=-1, keepdims=True)
    return xc * lax.rsqrt(var + LN_EPS) * g + b


def _sigmoid(x):
    return 1.0 / (1.0 + jnp.exp(-x))


def _fn_ln(x, g, b):
    return (_layer_norm(x, g, b),)


def _fn_rms(x, g):
    return (x * lax.rsqrt(jnp.mean(x * x, axis=-1, keepdims=True) + RMS_EPS) * g,)


def _make_post_mix(alpha):
    def fn(h, mix, g, b):
        return (_layer_norm(alpha * h + mix, g, b),)
    return fn


def _make_ple_ln(alpha):
    def fn(h1, ffn, pg, pp, g, b):
        return (_layer_norm(alpha * h1 + ffn + _sigmoid(pg) * pp, g, b),)
    return fn


def _fn_lower_bounds(l0, l1):
    m = jnp.maximum(l0, l1)
    e0, e1 = jnp.exp(l0 - m), jnp.exp(l1 - m)
    s = e0 + e1
    p0, p1 = e0 / s, e1 / s
    return (p0 - p0, (p0 + p1) - p0)


def _loss_and_grad(y, target, *, name):
    d = y.shape[1]

    def fn(yv, tv):
        err = yv - tv
        return err * (1.0 / d), 0.5 * jnp.sum(jnp.mean(err * err, axis=-1, keepdims=True), axis=0, keepdims=True)

    return _rowwise(fn, [y, target], [], [(d, F32)], accs=[(1, 1)], name=name)


def _split_dot(x, e_bf16):
    hi = x.astype(BF16)
    lo = (x - hi.astype(F32)).astype(BF16)
    return (jnp.dot(hi, e_bf16, preferred_element_type=F32) + jnp.dot(lo, e_bf16, preferred_element_type=F32))


def _hgrn_common(th):
    rm = lax.broadcasted_iota(jnp.int32, (th, HG_W), 0) % HG_CHUNK

    def seg_cumsum(x):
        for s in (1, 2, 4, 8):
            x = x + jnp.where(rm >= s, pltpu.roll(x, s, 0), 0.0)
        return x

    def seg_rcumsum(x):
        for s in (1, 2, 4, 8):
            x = x + jnp.where(rm < HG_CHUNK - s, pltpu.roll(x, th - s, 0), 0.0)
        return x

    ri = lax.broadcasted_iota(jnp.int32, (HG_W, HG_W), 0) // HEAD
    ci = lax.broadcasted_iota(jnp.int32, (HG_W, HG_W), 1) // HEAD
    head_f32 = (ri == ci).astype(F32)
    head_bf16 = head_f32.astype(BF16)

    def headsum(x):
        return _split_dot(x, head_bf16)

    return rm, seg_cumsum, seg_rcumsum, head_f32, headsum


def _hgrn_gates(qr, fl, lb):
    sg = _sigmoid(fl)
    f = lb + (1.0 - lb) * sg
    sq = _sigmoid(qr)
    return sg, f, jnp.log(f), 1.0 - f, qr * sq, sq


def _shifted(x, d, th):
    return x if d == 0 else pltpu.roll(x, d, 0)


def _unshift(x, d, th):
    return x if d == 0 else pltpu.roll(x, th - d, 0)


def _hgrn_fwd(projp, lb, ng, *, name, job=None):
    t_rows = projp.shape[0]
    th = min(HG_TILE, t_rows)
    nct = th // HG_CHUNK

    def body(q_ref, f_ref, i_ref, g_ref, lb_ref, ng_ref, oa_ref, opre_ref, st_out_ref,
             st_ref, vtm_ref, kv_ref, qe_ref, dec_ref, oint_ref):
        rm, seg_cumsum, seg_rcumsum, head_f32, headsum = _hgrn_common(th)

        @pl.when(pl.program_id(0) == 0)
        def _():
            st_ref[...] = jnp.zeros_like(st_ref)

        qr, fl, v, g = q_ref[...], f_ref[...], i_ref[...], g_ref[...]
        _, f, lf, k, q, _ = _hgrn_gates(qr, fl, lb_ref[...])
        b = seg_cumsum(lf)

        o = jnp.zeros((th, HG_W), F32)
        for d in range(HG_CHUNK):
            kd, bd, vd = _shifted(k, d, th), _shifted(b, d, th), _shifted(v, d, th)
            e = jnp.exp(jnp.where(rm >= d, b - bd, -1e30))
            o = o + headsum(q * kd * e) * vd

        blast = seg_rcumsum(jnp.where(rm == HG_CHUNK - 1, b, 0.0))
        kte = (k * jnp.exp(blast - b)).astype(BF16)
        qe_ref[...] = q * jnp.exp(b)
        dec_ref[...] = jnp.exp(blast)
        vt = v.T
        lane_chunk = lax.broadcasted_iota(jnp.int32, (HG_W, th), 1) // HG_CHUNK
        for c in range(nct):
            vtm_ref[c * HG_W:(c + 1) * HG_W, :] = jnp.where(lane_chunk == c, vt, 0.0).astype(BF16)
        kv_ref[...] = jnp.dot(vtm_ref[...], kte, preferred_element_type=F32)

        def step(c, carry):
            r0 = pl.multiple_of(c * HG_CHUNK, HG_CHUNK)
            s = st_ref[...]
            st_out_ref[c] = s
            oint_ref[pl.ds(r0, HG_CHUNK), :] = lax.dot_general(
                qe_ref[pl.ds(r0, HG_CHUNK), :].astype(BF16), s.astype(BF16),
                (((1,), (1,)), ((), ())), preferred_element_type=F32)
            dec = jnp.max(dec_ref[pl.ds(r0, HG_CHUNK), :], axis=0, keepdims=True)
            kv_c = kv_ref[pl.ds(pl.multiple_of(c * HG_W, HG_W), HG_W), :]
            st_ref[...] = s * dec + kv_c * head_f32
            return carry

        lax.fori_loop(0, nct, step, 0)

        o = o + oint_ref[...]
        opre_ref[...] = o
        r = lax.rsqrt(headsum(o * o) * (1.0 / HEAD) + RMS_EPS)
        oa_ref[...] = o * r * ng_ref[...] * (g * _sigmoid(g))

    col = lambda j: pl.BlockSpec((th, HG_W), lambda i, j=j: (i, j))
    vec = pl.BlockSpec((1, HG_W), lambda i: (0, 0))
    row = pl.BlockSpec((th, HG_W), lambda i: (i, 0))
    n_chunks = t_rows // HG_CHUNK
    return _call(
        body, (projp, projp, projp, projp, lb, ng), name=name, grid=(t_rows // th,),
        in_specs=[col(0), col(1), col(2), col(3), vec, vec],
        out_specs=[row, row, pl.BlockSpec((nct, HG_W, HG_W), lambda i: (i, 0, 0))],
        out_shape=[jax.ShapeDtypeStruct((t_rows, HG_W), F32), jax.ShapeDtypeStruct((t_rows, HG_W), F32),
                   jax.ShapeDtypeStruct((n_chunks, HG_W, HG_W), F32)],
        scratch_shapes=[pltpu.VMEM((HG_W, HG_W), F32), pltpu.VMEM((nct * HG_W, th), BF16),
                        pltpu.VMEM((nct * HG_W, HG_W), F32), pltpu.VMEM((th, HG_W), F32),
                        pltpu.VMEM((th, HG_W), F32), pltpu.VMEM((th, HG_W), F32)],
        sem=("arbitrary",), job=job)


def _hgrn_bwd(projp, lb, ng, opre, states, dcat, *, name):
    t_rows = projp.shape[0]
    th = min(HG_TILE, t_rows)
    nct = th // HG_CHUNK
    nt = t_rows // th

    def body(q_ref, f_ref, i_ref, g_ref, lb_ref, ng_ref, opre_ref, st_in_ref, do_ref,
             dproj_ref, dng_ref, dlb_ref,
             gst_ref, dotm_ref, qg_ref, v_ref, kte_ref, dop_ref, dec_ref, dkte_ref, dvi_ref, dqe_ref, ddec_ref):
        rm, seg_cumsum, seg_rcumsum, head_f32, headsum = _hgrn_common(th)

        @pl.when(pl.program_id(0) == 0)
        def _():
            gst_ref[...] = jnp.zeros_like(gst_ref)
            dng_ref[...] = jnp.zeros_like(dng_ref)
            dlb_ref[...] = jnp.zeros_like(dlb_ref)

        qr, fl, v, g = q_ref[...], f_ref[...], i_ref[...], g_ref[...]
        lb, ngv = lb_ref[...], ng_ref[...]
        sg, f, lf, k, q, sq = _hgrn_gates(qr, fl, lb)
        b = seg_cumsum(lf)
        blast = seg_rcumsum(jnp.where(rm == HG_CHUNK - 1, b, 0.0))
        eb = jnp.exp(b)
        ekb = jnp.exp(blast - b)
        qe, kte, dec = q * eb, k * ekb, jnp.exp(blast)

        do_out, op = do_ref[...], opre_ref[...]
        sgg = _sigmoid(g)
        sil = g * sgg
        r = lax.rsqrt(headsum(op * op) * (1.0 / HEAD) + RMS_EPS)
        on = op * r
        dng_ref[...] += jnp.sum(do_out * on * sil, axis=0, keepdims=True)
        dg = do_out * on * ngv * (sgg * (1.0 + g * (1.0 - sgg)))
        don = do_out * ngv * sil
        dop = r * (don - on * (headsum(don * on) * (1.0 / HEAD)))

        v_ref[...] = v
        kte_ref[...] = kte
        dop_ref[...] = dop
        dec_ref[...] = dec
        dot_t = dop.T
        lane_chunk = lax.broadcasted_iota(jnp.int32, (HG_W, th), 1) // HG_CHUNK
        for c in range(nct):
            dotm_ref[c * HG_W:(c + 1) * HG_W, :] = jnp.where(lane_chunk == c, dot_t, 0.0).astype(BF16)
        qg_ref[...] = jnp.dot(dotm_ref[...], qe.astype(BF16), preferred_element_type=F32)

        def step(j, carry):
            c = nct - 1 - j
            r0 = pl.multiple_of(c * HG_CHUNK, HG_CHUNK)
            rows = pl.ds(r0, HG_CHUNK)
            gs = gst_ref[...]
            s = st_in_ref[c]
            gm = (gs * head_f32).astype(BF16)
            dkte_ref[rows, :] = jnp.dot(v_ref[rows, :].astype(BF16), gm, preferred_element_type=F32)
            dvi_ref[rows, :] = lax.dot_general(kte_ref[rows, :].astype(BF16), gm, (((1,), (1,)), ((), ())),
                                               preferred_element_type=F32)
            dqe_ref[rows, :] = jnp.dot(dop_ref[rows, :].astype(BF16), s.astype(BF16), preferred_element_type=F32)
            ddec_ref[rows, :] = jnp.broadcast_to(jnp.sum(gs * s, axis=0, keepdims=True), (HG_CHUNK, HG_W))
            dec_c = jnp.max(dec_ref[rows, :], axis=0, keepdims=True)
            qg_c = qg_ref[pl.ds(pl.multiple_of(c * HG_W, HG_W), HG_W), :]
            gst_ref[...] = gs * dec_c + qg_c * head_f32
            return carry

        lax.fori_loop(0, nct, step, 0)

        dkte, dqe = dkte_ref[...], dqe_ref[...]
        dq = dqe * eb
        dk = dkte * ekb
        db = dqe * qe - dkte * kte
        dv = dvi_ref[...]
        dblast = dkte * kte + jnp.where(rm == HG_CHUNK - 1, ddec_ref[...] * dec, 0.0)

        for d in range(HG_CHUNK):
            kd, bd, vd = _shifted(k, d, th), _shifted(b, d, th), _shifted(v, d, th)
            e = jnp.exp(jnp.where(rm >= d, b - bd, -1e30))
            p = q * kd * e
            sc = headsum(p)
            dsc = headsum(dop * vd)
            dv = dv + _unshift(sc * dop, d, th)
            dq = dq + dsc * kd * e
            dk = dk + _unshift(dsc * q * e, d, th)
            darg = dsc * p
            db = db + darg - _unshift(darg, d, th)

        db = db + jnp.where(rm == HG_CHUNK - 1, seg_cumsum(dblast), 0.0)
        dlf = seg_rcumsum(db)
        df = dlf / f - dk
        dlb_ref[...] += jnp.sum(df * (1.0 - sg), axis=0, keepdims=True)
        dfl = df * (1.0 - lb) * sg * (1.0 - sg)
        dqr = dq * (sq * (1.0 + qr * (1.0 - sq)))
        dproj_ref[...] = jnp.concatenate([dqr, dfl, dv, dg], axis=1)

    rev = lambda i: nt - 1 - i
    col = lambda j: pl.BlockSpec((th, HG_W), lambda i, j=j: (rev(i), j))
    vec = pl.BlockSpec((1, HG_W), lambda i: (0, 0))
    row = pl.BlockSpec((th, HG_W), lambda i: (rev(i), 0))
    tile_f32 = pltpu.VMEM((th, HG_W), F32)
    return pl.pallas_call(
        body, name=name, grid=(nt,),
        in_specs=[col(0), col(1), col(2), col(3), vec, vec, row,
                  pl.BlockSpec((nct, HG_W, HG_W), lambda i: (rev(i), 0, 0)), col(0)],
        out_specs=[pl.BlockSpec((th, 4 * HG_W), lambda i: (rev(i), 0)), vec, vec],
        out_shape=[jax.ShapeDtypeStruct((t_rows, 4 * HG_W), F32), jax.ShapeDtypeStruct((1, HG_W), F32),
                   jax.ShapeDtypeStruct((1, HG_W), F32)],
        scratch_shapes=[pltpu.VMEM((HG_W, HG_W), F32), pltpu.VMEM((nct * HG_W, th), BF16),
                        pltpu.VMEM((nct * HG_W, HG_W), F32)] + [tile_f32] * 8,
        compiler_params=_cparams(("arbitrary",)),
    )(projp, projp, projp, projp, lb, ng, opre, states, dcat)


_INV_SQRT2 = 1.0 / math.sqrt(2.0)
_INV_SQRT2PI = 1.0 / math.sqrt(2.0 * math.pi)


def _gelu(x):
    return 0.5 * x * (1.0 + lax.erf(x * _INV_SQRT2))


def _gelu_grad(x):
    return 0.5 * (1.0 + lax.erf(x * _INV_SQRT2)) + x * jnp.exp(-0.5 * x * x) * _INV_SQRT2PI


def _sgu_parts(bu, bv, lg, lbias, w_ref, n_groups):
    c = SGU_CHUNK
    tril = (lax.broadcasted_iota(jnp.int32, (c, c), 0) >= lax.broadcasted_iota(jnp.int32, (c, c), 1)).astype(F32)
    gid = lax.broadcasted_iota(jnp.int32, bu.shape, 1) // HEAD
    u = _gelu(bu)
    gv = _gelu(bv)
    mu = jnp.mean(gv, axis=-1, keepdims=True)
    xc = gv - mu
    rstd = lax.rsqrt(jnp.mean(xc * xc, axis=-1, keepdims=True) + LN_EPS)
    xhat = xc * rstd
    vn = xhat * lg + lbias
    ws = [w_ref[gi] * tril for gi in range(n_groups)]
    return tril, gid, u, rstd, xhat, vn, ws


def _sgu_fwd(projp, lg, lbias, w_s, bias_full, *, name):
    t_rows = projp.shape[0]
    n_groups = w_s.shape[0]
    c = SGU_CHUNK

    def body(u_ref, v_ref, lg_ref, lb_ref, w_ref, bias_ref, o_ref):
        _, gid, u, _, _, vn, ws = _sgu_parts(u_ref[...], v_ref[...], lg_ref[...], lb_ref[...], w_ref, n_groups)
        vnb = vn.astype(BF16)
        z = bias_ref[...]
        for gi in range(n_groups):
            z = z + jnp.where(gid == gi, jnp.dot(ws[gi].astype(BF16), vnb, preferred_element_type=F32), 0.0)
        o_ref[...] = u * z

    col = lambda j: pl.BlockSpec((c, HG_W), lambda i, j=j: (i, j))
    return pl.pallas_call(
        body, name=name, grid=(t_rows // c,),
        in_specs=[col(4), col(5), _const_spec(lg), _const_spec(lbias), _const_spec(w_s), _const_spec(bias_full)],
        out_specs=pl.BlockSpec((c, HG_W), lambda i: (i, 0)),
        out_shape=jax.ShapeDtypeStruct((t_rows, HG_W), F32),
        compiler_params=_cparams(("arbitrary",)),
    )(projp, projp, lg, lbias, w_s, bias_full)


def _sgu_bwd(projp, lg, lbias, w_s, bias_full, dcat, *, name):
    t_rows = projp.shape[0]
    n_groups = w_s.shape[0]
    c = SGU_CHUNK
    n = t_rows // c

    def body(u_ref, v_ref, lg_ref, lb_ref, w_ref, bias_ref, do_ref,
             dproj_ref, dlg_ref, dlb_ref, dw_ref, dbs_ref, dbias_acc):
        i = pl.program_id(0)

        @pl.when(i == 0)
        def _():
            dlg_ref[...] = jnp.zeros_like(dlg_ref)
            dlb_ref[...] = jnp.zeros_like(dlb_ref)
            dw_ref[...] = jnp.zeros_like(dw_ref)
            dbias_acc[...] = jnp.zeros_like(dbias_acc)

        bu, bv, lg_v = u_ref[...], v_ref[...], lg_ref[...]
        tril, gid, u, rstd, xhat, vn, ws = _sgu_parts(bu, bv, lg_v, lb_ref[...], w_ref, n_groups)
        vnb = vn.astype(BF16)
        z = bias_ref[...]
        for gi in range(n_groups):
            z = z + jnp.where(gid == gi, jnp.dot(ws[gi].astype(BF16), vnb, preferred_element_type=F32), 0.0)
        do = do_ref[...]
        dbu = do * z * _gelu_grad(bu)
        dz = do * u
        dbias_acc[...] += dz
        dvn = jnp.zeros_like(dz)
        for gi in range(n_groups):
            dzg = jnp.where(gid == gi, dz, 0.0).astype(BF16)
            dw_ref[gi] += lax.dot_general(dzg, vnb, (((1,), (1,)), ((), ())), preferred_element_type=F32) * tril
            dvn = dvn + jnp.dot(ws[gi].T.astype(BF16), dzg, preferred_element_type=F32)
        dlg_ref[...] += jnp.sum(dvn * xhat, axis=0, keepdims=True)
        dlb_ref[...] += jnp.sum(dvn, axis=0, keepdims=True)
        dxh = dvn * lg_v
        dgv = rstd * (dxh - jnp.mean(dxh, axis=-1, keepdims=True)
                      - xhat * jnp.mean(dxh * xhat, axis=-1, keepdims=True))
        dproj_ref[...] = jnp.concatenate([dbu, dgv * _gelu_grad(bv)], axis=1)

        @pl.when(i == n - 1)
        def _():
            dbs_ref[...] = jnp.sum(dbias_acc[...].T.reshape(n_groups, HEAD, c), axis=1)

    col = lambda j: pl.BlockSpec((c, HG_W), lambda i, j=j: (i, j))
    return pl.pallas_call(
        body, name=name, grid=(n,),
        in_specs=[col(4), col(5), _const_spec(lg), _const_spec(lbias), _const_spec(w_s), _const_spec(bias_full),
                  col(1)],
        out_specs=[pl.BlockSpec((c, 2 * HG_W), lambda i: (i, 0)), _const_spec(lg), _const_spec(lbias),
                   _const_spec(w_s), pl.BlockSpec((n_groups, c), lambda i: (0, 0))],
        out_shape=[jax.ShapeDtypeStruct((t_rows, 2 * HG_W), F32), jax.ShapeDtypeStruct(lg.shape, F32),
                   jax.ShapeDtypeStruct(lbias.shape, F32), jax.ShapeDtypeStruct(w_s.shape, F32),
                   jax.ShapeDtypeStruct((n_groups, c), F32)],
        scratch_shapes=[pltpu.VMEM((c, HG_W), F32)],
        compiler_params=_cparams(("arbitrary",)),
    )(projp, projp, lg, lbias, w_s, bias_full, dcat)


def _rope_tables(positions):
    t = positions.shape[0]
    inv_freq = ROPE_THETA ** (-jnp.arange(0, 32, 2, dtype=F32) / 32)
    ang = positions.astype(F32)[:, None] * inv_freq
    cos, sin = jnp.cos(ang), jnp.sin(ang)
    z = lambda w: jnp.zeros((t, w), F32)
    cos_t = jnp.concatenate([jnp.ones((t, 64), F32), cos, cos, z(32)], axis=1)
    sin_up = jnp.concatenate([z(80), sin, z(32)], axis=1)
    sin_dn = jnp.concatenate([z(64), -sin, z(48)], axis=1)
    return cos_t, sin_up, sin_dn


def _rep(x, n):
    return x if n == 1 else jnp.concatenate([x] * n, axis=1)


def _rope(x, cos_t, sin_up, sin_dn):
    w = x.shape[1]
    return x * cos_t + pltpu.roll(x, 16, 1) * sin_up + pltpu.roll(x, w - 16, 1) * sin_dn


def _rope_t(dy, cos_t, sin_up, sin_dn):
    w = dy.shape[1]
    return dy * cos_t + pltpu.roll(dy * sin_up, w - 16, 1) + pltpu.roll(dy * sin_dn, 16, 1)


def _mla_prep(q, kv, projp, tables, *, name):
    nh = N_ATT_HEADS

    def fn(qv, kvv, kr, cos_t, sin_up, sin_dn):
        qr = _rope(qv, _rep(cos_t, nh), _rep(sin_up, nh), _rep(sin_dn, nh))
        krr = _rope(kr, cos_t, sin_up, sin_dn)
        lane = lax.broadcasted_iota(jnp.int32, kvv.shape, 1) % LANES
        return qr, jnp.where(lane < HEAD, kvv, 0.0) + _rep(krr, nh), kvv

    w = q.shape[1]
    return _rowwise(fn, [q, kv, (projp, LANES, P_KR // LANES)] + list(tables), [],
                    [(w, BF16), (w, BF16), (w, BF16)], name=name)


def _mla_prep_bwd(dqr, dkf, tables, *, name):
    nh = N_ATT_HEADS

    def fn(dq, dk, cos_t, sin_up, sin_dn):
        dqp = _rope_t(dq, _rep(cos_t, nh), _rep(sin_up, nh), _rep(sin_dn, nh))
        dkrr = dk[:, 0:LANES]
        for h in range(1, nh):
            dkrr = dkrr + dk[:, LANES * h:LANES * (h + 1)]
        return dqp, _rope_t(dkrr, cos_t, sin_up, sin_dn)

    return _rowwise(fn, [dqr, dkf] + list(tables), [], [(dqr.shape[1], F32), (LANES, F32)], name=name)


_NT = (((1,), (1,)), ((), ()))
_TN = (((0,), (0,)), ((), ()))


def _attn_fwd(qr, kf, kvb, *, name, job=None):
    t_rows = qr.shape[0]
    tq = min(ATT_TQ, t_rows)
    nb = t_rows // tq
    scale = ATT_D ** -0.5

    def body(q_ref, kf_ref, kvb_ref, o_ref, lse_ref):
        qi = pl.program_id(1)
        lane = lax.broadcasted_iota(jnp.int32, (tq, LANES), 1)
        causal = (lax.broadcasted_iota(jnp.int32, (tq, tq), 1) <= lax.broadcasted_iota(jnp.int32, (tq, tq), 0))
        outs = []
        for hh in range(2):
            cols = slice(hh * LANES, (hh + 1) * LANES)
            q = q_ref[:, cols]

            def block(ki, carry, diagonal, q=q, cols=cols):
                m_old, l_old, acc = carry
                rows = pl.ds(pl.multiple_of(ki * tq, tq), tq)
                s = lax.dot_general(q, kf_ref[rows, cols], _NT, preferred_element_type=F32) * scale
                if diagonal:
                    s = jnp.where(causal, s, -1e30)
                m_new = jnp.maximum(m_old, jnp.max(s, axis=-1, keepdims=True))
                p = jnp.exp(s - m_new)
                a = jnp.exp(m_old - m_new)
                return (m_new, a * l_old + jnp.sum(p, axis=-1, keepdims=True),
                        a * acc + jnp.dot(p.astype(BF16), kvb_ref[rows, cols], preferred_element_type=F32))

            init = (jnp.full((tq, 1), -1e30, F32), jnp.zeros((tq, 1), F32), jnp.zeros((tq, LANES), F32))
            carry = lax.fori_loop(0, qi, lambda ki, c, block=block: block(ki, c, False), init)
            m_fin, l_fin, acc = block(qi, carry, True)
            lse_ref[hh] = m_fin + jnp.log(l_fin)
            outs.append(acc / l_fin)
        o_ref[...] = jnp.where(lane < HEAD, pltpu.roll(outs[0], HEAD, 1), outs[1])

    pair = pl.BlockSpec((t_rows, 2 * LANES), lambda pr, qi: (0, pr))
    return _call(
        body, (qr, kf, kvb), name=name, grid=(N_ATT_HEADS // 2, nb),
        in_specs=[pl.BlockSpec((tq, 2 * LANES), lambda pr, qi: (qi, pr)), pair, pair],
        out_specs=[pl.BlockSpec((tq, LANES), lambda pr, qi: (qi, pr)),
                   pl.BlockSpec((2, tq, 1), lambda pr, qi: (pr, qi, 0))],
        out_shape=[jax.ShapeDtypeStruct((t_rows, N_ATT_HEADS * HEAD), F32),
                   jax.ShapeDtypeStruct((N_ATT_HEADS, t_rows, 1), F32)],
        sem=("parallel", "arbitrary"), job=job)


def _attn_bwd(qr, kf, kvb, dcat, o, lse, *, name, job=None):
    t_rows = qr.shape[0]
    tq = min(ATT_TQ, t_rows)
    nb = t_rows // tq
    scale = ATT_D ** -0.5
    do_off = 2 * HG_W // LANES

    def body(q_ref, kf_ref, kvb_ref, do_ref, o_ref, lse_ref, dq_ref, dkv_ref, dk_ref):
        ki = pl.program_id(1)

        @pl.when(ki == 0)
        def _():
            dq_ref[...] = jnp.zeros_like(dq_ref)

        lane = lax.broadcasted_iota(jnp.int32, (tq, LANES), 1)
        causal = (lax.broadcasted_iota(jnp.int32, (tq, tq), 1) <= lax.broadcasted_iota(jnp.int32, (tq, tq), 0))
        dkvs, dks = [], []
        for hh in range(2):
            cols = slice(hh * LANES, (hh + 1) * LANES)
            k, v = kf_ref[:, cols], kvb_ref[:, cols]

            def block(qi, carry, diagonal, hh=hh, cols=cols, k=k, v=v):
                dk, dv = carry
                rows = pl.ds(pl.multiple_of(qi * tq, tq), tq)
                q = q_ref[rows, cols]
                do, ov = do_ref[rows, :], o_ref[rows, :]
                if hh == 0:
                    do, ov = pltpu.roll(do, HEAD, 1), pltpu.roll(ov, HEAD, 1)
                do = jnp.where(lane >= HEAD, do, 0.0)
                delta = jnp.sum(do * ov, axis=-1, keepdims=True)
                s = lax.dot_general(q, k, _NT, preferred_element_type=F32) * scale
                if diagonal:
                    s = jnp.where(causal, s, -1e30)
                p = jnp.exp(s - lse_ref[hh, rows, :])
                dob = do.astype(BF16)
                dv = dv + lax.dot_general(p.astype(BF16), dob, _TN, preferred_element_type=F32)
                dp = lax.dot_general(dob, v, _NT, preferred_element_type=F32)
                ds = (p * (dp - delta) * scale).astype(BF16)
                dk = dk + lax.dot_general(ds, q, _TN, preferred_element_type=F32)
                dq_ref[rows, cols] += jnp.dot(ds, k, preferred_element_type=F32)
                return dk, dv

            zero = jnp.zeros((tq, LANES), F32)
            carry = block(ki, (zero, zero), True)
            dk, dv = lax.fori_loop(ki + 1, nb, lambda qi, c, block=block: block(qi, c, False), carry)
            dks.append(dk)
            dkvs.append(jnp.where(lane < HEAD, dk, dv))
        dkv_ref[...] = jnp.concatenate(dkvs, axis=1)
        dk_ref[...] = jnp.concatenate(dks, axis=1)

    pair_all = pl.BlockSpec((t_rows, 2 * LANES), lambda pr, ki: (0, pr))
    pair_blk = pl.BlockSpec((tq, 2 * LANES), lambda pr, ki: (ki, pr))
    wide = jax.ShapeDtypeStruct((t_rows, N_ATT_HEADS * LANES), F32)
    return _call(
        body, (qr, kf, kvb, dcat, o, lse), name=name, grid=(N_ATT_HEADS // 2, nb),
        in_specs=[pair_all, pair_blk, pair_blk,
                  pl.BlockSpec((t_rows, LANES), lambda pr, ki: (0, do_off + pr)),
                  pl.BlockSpec((t_rows, LANES), lambda pr, ki: (0, pr)),
                  pl.BlockSpec((2, t_rows, 1), lambda pr, ki: (pr, 0, 0))],
        out_specs=[pair_all, pair_blk, pair_blk],
        out_shape=[wide, wide, wide],
        sem=("parallel", "arbitrary"), job=job)


def _my_pos():
    return lax.axis_index("x"), lax.axis_index("y"), lax.axis_index("c")


def _all_gather(xs, *, name):
    return _gather_forward(_run_job(_gather_job(xs), name=name), name=name + "_forward")


def _remote(src, dst, send_sems, recv_sems, k, dev):
    return pltpu.make_async_remote_copy(src_ref=src, dst_ref=dst, send_sem=send_sems.at[k], recv_sem=recv_sems.at[k],
                                        device_id=dev, device_id_type=MESH)


def _gather_job(xs):
    n = len(xs)

    def make(x_refs, out_refs, send_sems, recv_sems, local_sems):
        mx, my, mc = _my_pos()
        mine = 4 * mx + 2 * my + mc
        peers = [(mx, my, 1 - mc), (1 - mx, my, mc), (mx, 1 - my, mc), (1 - mx, 1 - my, mc)]
        sends, recvs, local = [], [], []
        for a in range(n):
            local.append(pltpu.make_async_copy(x_refs[a], out_refs[a].at[mine], local_sems.at[a]))
            for k, dev in enumerate(peers):
                theirs = 4 * dev[0] + 2 * dev[1] + dev[2]
                sends.append(_remote(x_refs[a], out_refs[a].at[mine], send_sems, recv_sems, 4 * a + k, dev))
                recvs.append(_remote(x_refs[a], out_refs[a].at[theirs], send_sems, recv_sems, 4 * a + k, dev))
        return sends, recvs, local

    shapes = [jax.ShapeDtypeStruct((N_DEV,) + x.shape, x.dtype) for x in xs]
    return _copies_job(xs, shapes, 4 * n, n, make)


def _gather_forward(gs, *, name):
    n = len(gs)

    def body(*refs):
        out_refs = refs[n:2 * n]
        send_sems, recv_sems = refs[2 * n:]
        mx, my, mc = _my_pos()
        chips = [(1 - mx, my), (mx, 1 - my), (1 - mx, 1 - my)]
        sends, recvs = [], []
        for a in range(n):
            for j, (cx, cy) in enumerate(chips):
                here, there = out_refs[a].at[4 * cx + 2 * cy + mc], out_refs[a].at[4 * cx + 2 * cy + 1 - mc]
                sends.append(_remote(here, here, send_sems, recv_sems, 3 * a + j, (mx, my, 1 - mc)))
                recvs.append(_remote(here, there, send_sems, recv_sems, 3 * a + j, (mx, my, 1 - mc)))
        for cp in sends:
            cp.start()
        for cp in recvs:
            cp.wait_recv()
        for cp in sends:
            cp.wait_send()

    return pl.pallas_call(
        body, name=name, out_shape=[jax.ShapeDtypeStruct(g.shape, g.dtype) for g in gs],
        in_specs=[_ANY] * n, out_specs=[_ANY] * n, input_output_aliases={a: a for a in range(n)},
        scratch_shapes=[pltpu.SemaphoreType.DMA((3 * n,)), pltpu.SemaphoreType.DMA((3 * n,))],
    )(*gs)


def _pair_job(xs):
    n = len(xs)

    def make(x_refs, out_refs, send_sems, recv_sems, local_sems):
        mx, my, mc = _my_pos()
        copies = [_remote(x_refs[a].at[g, 1 - mc], out_refs[a].at[g], send_sems, recv_sems, 4 * a + g, (mx, my, 1 - mc))
                  for a in range(n) for g in range(4)]
        return copies, copies, []

    shapes = [jax.ShapeDtypeStruct((4,) + x.shape[2:], x.dtype) for x in xs]
    return _copies_job(xs, shapes, 4 * n, 0, make)


def _pair_add(x, r, core, *, name):
    _, _, a, b = x.shape
    ta = _row_tile(a, 256)

    def body(c_ref, x_ref, r_ref, o_ref):
        o_ref[...] = x_ref[...] + r_ref[...]

    blk = pl.BlockSpec((None, ta, b), lambda g, i, c_ref: (g, i, 0))
    return pl.pallas_call(
        body, name=name,
        grid_spec=pltpu.PrefetchScalarGridSpec(
            num_scalar_prefetch=1, grid=(4, a // ta),
            in_specs=[pl.BlockSpec((None, None, ta, b), lambda g, i, c_ref: (g, c_ref[0], i, 0)), blk],
            out_specs=blk),
        out_shape=jax.ShapeDtypeStruct((4, a, b), x.dtype),
        compiler_params=_cparams(("parallel", "parallel")),
    )(core, x, r)


def _quad_job(xs):
    n = len(xs)

    def make(x_refs, out_refs, send_sems, recv_sems, local_sems):
        mx, my, mc = _my_pos()
        mine = 2 * mx + my
        peers = [((1 - mx, my, mc), 2 * (1 - mx) + my), ((mx, 1 - my, mc), 2 * mx + 1 - my),
                 ((1 - mx, 1 - my, mc), 2 * (1 - mx) + 1 - my)]
        sends, recvs, local = [], [], []
        for a in range(n):
            local.append(pltpu.make_async_copy(x_refs[a].at[mine], out_refs[a].at[mine], local_sems.at[a]))
            for k, (dev, g) in enumerate(peers):
                sends.append(_remote(x_refs[a].at[g], out_refs[a].at[mine], send_sems, recv_sems, 3 * a + k, dev))
                recvs.append(_remote(x_refs[a].at[g], out_refs[a].at[g], send_sems, recv_sems, 3 * a + k, dev))
        return sends, recvs, local

    shapes = [jax.ShapeDtypeStruct(x.shape, x.dtype) for x in xs]
    return _copies_job(xs, shapes, 3 * n, n, make)


def _row_tile(r, pref):
    t = min(pref, r)
    while r % t or (t % 8 and t != r):
        t -= 1
    return t


def _adamw(parts, w, m, v, layer, *, name, tile=256):
    g, a, b = parts.shape
    tile = _row_tile(a, tile)
    c1 = 1.0 / (1.0 - ADAM_B1 ** ADAM_STEP)
    c2 = 1.0 / (1.0 - ADAM_B2 ** ADAM_STEP)

    def body(p_ref, w_ref, m_ref, v_ref, g_ref, d_ref, mo_ref, vo_ref):
        grad = p_ref[0]
        for j in range(1, g):
            grad = grad + p_ref[j]
        mn = ADAM_B1 * m_ref[...] + (1.0 - ADAM_B1) * grad
        vn = ADAM_B2 * v_ref[...] + (1.0 - ADAM_B2) * (grad * grad)
        g_ref[...] = grad
        mo_ref[...] = mn
        vo_ref[...] = vn
        d_ref[...] = -ADAM_LR * ((mn * c1) / (jnp.sqrt(vn * c2) + ADAM_EPS) + ADAM_WD * w_ref[...])

    slab = pl.BlockSpec((tile, b), lambda i: (i, 0))
    src = slab if layer is None else pl.BlockSpec((None, tile, b), lambda i: (layer, i, 0))
    return pl.pallas_call(
        body, name=name, grid=(a // tile,),
        in_specs=[pl.BlockSpec((g, tile, b), lambda i: (0, i, 0)), src, src, src],
        out_specs=[slab] * 4,
        out_shape=[jax.ShapeDtypeStruct((a, b), F32)] * 4,
        compiler_params=_cparams(("parallel",)),
    )(parts, w, m, v)


W_IN_SHARD = 276


def _w_in_dest(col):
    return jnp.where(col < P_KR, col, jnp.where(col < P_KR + 256, col + (P_CKV - P_KR), col - 2176 + P_KR + HEAD))


def _place_w_in(g, *, name):
    _, d, sh = g.shape
    tc = 768

    def body(g_ref, o_ref, acc_ref):
        ct, j = pl.program_id(0), pl.program_id(1)

        @pl.when(j == 0)
        def _():
            acc_ref[...] = jnp.zeros_like(acc_ref)

        src = j * sh + lax.broadcasted_iota(jnp.int32, (sh, tc), 0)
        dst = ct * tc + lax.broadcasted_iota(jnp.int32, (sh, tc), 1)
        place = (_w_in_dest(src) == dst).astype(BF16)
        acc_ref[...] += jnp.dot(g_ref[...], place, preferred_element_type=F32)

        @pl.when(j == N_DEV - 1)
        def _():
            o_ref[...] = acc_ref[...].astype(o_ref.dtype)

    return pl.pallas_call(
        body, name=name, grid=(P_COLS // tc, N_DEV),
        in_specs=[pl.BlockSpec((None, d, sh), lambda ct, j: (j, 0, 0))],
        out_specs=pl.BlockSpec((d, tc), lambda ct, j: (0, ct)),
        out_shape=jax.ShapeDtypeStruct((d, P_COLS), BF16),
        scratch_shapes=[pltpu.VMEM((d, tc), F32)],
        compiler_params=_cparams(("parallel", "arbitrary")),
    )(g)


def _unplace_w_in(dw, *, name):
    d = dw.shape[0]
    sh = W_IN_SHARD

    def body(dw_ref, o_ref):
        j = pl.program_id(0)
        src = j * sh + lax.broadcasted_iota(jnp.int32, (P_COLS, sh), 1)
        dst = lax.broadcasted_iota(jnp.int32, (P_COLS, sh), 0)
        pick = (_w_in_dest(src) == dst).astype(BF16)
        x = dw_ref[...]
        hi = x.astype(BF16)
        r1 = x - hi.astype(F32)
        mid = r1.astype(BF16)
        lo = (r1 - mid.astype(F32)).astype(BF16)
        o_ref[...] = (jnp.dot(hi, pick, preferred_element_type=F32) + jnp.dot(mid, pick, preferred_element_type=F32)
                      + jnp.dot(lo, pick, preferred_element_type=F32))

    return pl.pallas_call(
        body, name=name, grid=(N_DEV,),
        in_specs=[pl.BlockSpec((d, P_COLS), lambda j: (0, 0))],
        out_specs=pl.BlockSpec((None, d, sh), lambda j: (j, 0, 0)),
        out_shape=jax.ShapeDtypeStruct((N_DEV, d, sh), F32),
        compiler_params=_cparams(("arbitrary",)),
    )(dw)


def _swiglu3(gu, *, name):
    _, t_rows, w = gu.shape
    tile = min(ROW_TILE, t_rows)
    g2 = gu.reshape(2, 4, t_rows, w)

    def body(g_ref, o_ref):
        gate, up = g_ref[0], g_ref[1]
        o_ref[...] = (gate * _sigmoid(gate) * up).astype(o_ref.dtype)

    return pl.pallas_call(
        body, name=name, grid=(4, t_rows // tile),
        in_specs=[pl.BlockSpec((2, None, tile, w), lambda j, i: (0, j, i, 0))],
        out_specs=pl.BlockSpec((None, tile, w), lambda j, i: (j, i, 0)),
        out_shape=jax.ShapeDtypeStruct((4, t_rows, w), BF16),
        compiler_params=_cparams(("parallel", "parallel")),
    )(g2)


def _swiglu3_bwd(gu, dact, *, name):
    _, t_rows, w = gu.shape
    tile = min(ROW_TILE, t_rows)
    g2 = gu.reshape(2, 4, t_rows, w)

    def body(g_ref, d_ref, o_ref):
        gate, up, d = g_ref[0], g_ref[1], d_ref[...]
        sg = _sigmoid(gate)
        o_ref[0] = d * up * (sg * (1.0 + gate * (1.0 - sg)))
        o_ref[1] = d * gate * sg

    blk = pl.BlockSpec((2, None, tile, w), lambda j, i: (0, j, i, 0))
    out = pl.pallas_call(
        body, name=name, grid=(4, t_rows // tile),
        in_specs=[blk, pl.BlockSpec((None, tile, w), lambda j, i: (j, i, 0))],
        out_specs=blk,
        out_shape=jax.ShapeDtypeStruct((2, 4, t_rows, w), F32),
        compiler_params=_cparams(("parallel", "parallel")),
    )(g2, dact)
    return out.reshape(8, t_rows, w)


BIG = ("w_in", "mla_w_uq", "mla_w_ukv", "w_out", "w_gate_up", "w_down", "ple_w_gate", "ple_w_proj")
SMALL = ("ln_in_g", "ln_in_b", "hgrn_lb_logits", "hgrn_norm_g", "sgu_ln_g", "sgu_ln_b", "sgu_w_s", "sgu_b_s",
         "mla_q_norm_g", "mla_kv_norm_g", "ln1_g", "ln1_b", "ln2_g", "ln2_b")
ORDER = ("ln_in_g", "ln_in_b", "w_in", "hgrn_lb_logits", "hgrn_norm_g", "sgu_ln_g", "sgu_ln_b", "sgu_w_s", "sgu_b_s",
         "mla_q_norm_g", "mla_w_uq", "mla_kv_norm_g", "mla_w_ukv", "w_out", "ln1_g", "ln1_b", "w_gate_up", "w_down",
         "ple_w_gate", "ple_w_proj", "ln2_g", "ln2_b")


def _slab(a, align):
    s = a.reshape(-1, LANES)
    pad = -s.shape[0] % align
    return jnp.pad(s, ((0, pad), (0, 0))) if pad else s


def _pack(arrays, align=16, total_align=512):
    s = jnp.concatenate([_slab(a, align) for a in arrays], axis=0)
    pad = -s.shape[0] % total_align
    return jnp.pad(s, ((0, pad), (0, 0))) if pad else s


def _unpack(slab, shapes, align=16):
    out, r0 = [], 0
    for s in shapes:
        nr = math.prod(s) // LANES
        out.append(slab[r0:r0 + nr].reshape(s))
        r0 += nr + (-nr % align)
    return out


def _blocks_to_cols(g, *, name):
    nb, a, b = g.shape

    def body(g_ref, o_ref):
        o_ref[...] = g_ref[...]

    return pl.pallas_call(
        body, name=name, grid=(nb,), in_specs=[pl.BlockSpec((None, a, b), lambda j: (j, 0, 0))],
        out_specs=pl.BlockSpec((a, b), lambda j: (0, j)), out_shape=jax.ShapeDtypeStruct((a, nb * b), g.dtype),
        compiler_params=_cparams(("parallel",)),
    )(g)


def _cols_to_blocks(x, *, name):
    a, b = x.shape[0], x.shape[1] // N_DEV

    def body(x_ref, o_ref):
        o_ref[...] = x_ref[...]

    return pl.pallas_call(
        body, name=name, grid=(N_DEV,), in_specs=[pl.BlockSpec((a, b), lambda j: (0, j))],
        out_specs=pl.BlockSpec((None, a, b), lambda j: (j, 0, 0)), out_shape=jax.ShapeDtypeStruct((N_DEV, a, b), x.dtype),
        compiler_params=_cparams(("parallel",)),
    )(x)


def _weight_shards(w, li):
    uq_pad = ((0, 0), (0, LANES - ATT_D))
    shards = {k: w[k][li] for k in BIG}
    shards["mla_w_uq"] = jnp.pad(shards["mla_w_uq"], uq_pad)
    return {k: s.astype(BF16) for k, s in shards.items()}


def _usable_weights(g, *, name):
    out = {}
    for k, a in g.items():
        if k == "w_in":
            out[k] = _place_w_in(a, name=name + "_place_w_in")
        elif k in ("w_out", "w_down", "ple_w_gate"):
            out[k] = a.reshape(a.shape[0] * a.shape[1], a.shape[2])
        elif k == "w_gate_up":
            out[k] = a
        else:
            out[k] = _blocks_to_cols(a, name=name + "_cols_" + k)
    return out


def _as_pairs(g):
    if g.ndim == 2:
        return g.reshape((4, 2, g.shape[0] // N_DEV) + g.shape[1:])
    return g.reshape((4, 2) + g.shape[1:])


def _layer_forward(li, h, p_i, wts, sm, lbs, tables, alpha, hgrn_job=None, more_weights=None, attn_job=None):
    n = f"l{li}_"
    row1 = lambda a: a.reshape(1, -1)
    projp = _mm(h, wts["w_in"], name=n + "proj")
    ng = row1(sm["hgrn_norm_g"][li])
    res = _hgrn_fwd(projp, lbs[li], ng, name=n + "hgrn_fwd", job=hgrn_job)
    if hgrn_job is not None:
        res, got = res
        wts = dict(wts, **more_weights(got))
    o_a, o_pre, states = res
    lg, lbias = row1(sm["sgu_ln_g"][li]), row1(sm["sgu_ln_b"][li])
    w_s = sm["sgu_w_s"][li]
    bias_full = jnp.repeat(sm["sgu_b_s"][li].T, HEAD, axis=1)
    o_b = _sgu_fwd(projp, lg, lbias, w_s, bias_full, name=n + "sgu_fwd")
    qg, kvg = row1(sm["mla_q_norm_g"][li]), row1(sm["mla_kv_norm_g"][li])
    cq_view, ckv_view = (projp, 384, P_CQ // 384), (projp, 256, P_CKV // 256)
    (cqn,) = _rowwise(_fn_rms, [cq_view], [qg], [(384, BF16)], name=n + "q_norm")
    (ckvn,) = _rowwise(_fn_rms, [ckv_view], [kvg], [(256, BF16)], name=n + "kv_norm")
    q = _mm(cqn, wts["mla_w_uq"], name=n + "uq")
    kv = _mm(ckvn, wts["mla_w_ukv"], name=n + "ukv")
    qr, kf, kvb = _mla_prep(q, kv, projp, tables, name=n + "mla_prep")
    res, attn_got = _attn_fwd(qr, kf, kvb, name=n + "attn_fwd", job=attn_job), None
    if attn_job is not None:
        res, attn_got = res
    o_c, lse = res
    cat = jnp.concatenate([o_a, o_b, o_c], axis=1)
    mix = _mm(cat, wts["w_out"], name=n + "out_proj")
    g1, b1 = row1(sm["ln1_g"][li]), row1(sm["ln1_b"][li])
    d = h.shape[1]
    (h1,) = _rowwise(_make_post_mix(alpha), [h, mix], [g1, b1], [(d, F32)], name=n + "ln1")
    gu = _mm(h1, wts["w_gate_up"], bm="bkn", om="bmn", name=n + "gate_up")
    act = _swiglu3(gu, name=n + "swiglu")
    ffn = _mm(act, wts["w_down"], am="bmk", name=n + "down")
    pg = _mm(h1, wts["ple_w_gate"], name=n + "ple_gate")
    pp = _mm(p_i, wts["ple_w_proj"], name=n + "ple_proj")
    g2, b2 = row1(sm["ln2_g"][li]), row1(sm["ln2_b"][li])
    (h2,) = _rowwise(_make_ple_ln(alpha), [h1, ffn, pg, pp], [g2, b2], [(d, F32)], name=n + "ln2")
    saved = dict(h=h, projp=projp, o_pre=o_pre, states=states, cqn=cqn, ckvn=ckvn, qr=qr, kf=kf, kvb=kvb, o_c=o_c,
                 lse=lse, cat=cat, mix=mix, h1=h1, gu=gu, act=act, ffn=ffn, pg=pg, pp=pp, ng=ng, lg=lg, wts=wts,
                 lbias=lbias, w_s=w_s, bias_full=bias_full, qg=qg, kvg=kvg, g1=g1, b1=b1, g2=g2, b2=b2)
    return h2, saved, attn_got


RS_EARLY = ("ple_w_proj", "ple_w_gate", "w_down", "w_gate_up", "w_out")
RS_LATE = ("mla_w_uq", "mla_w_ukv", "w_in")


def _layer_backward(li, dh2_parts, p_i, sv, lbs, tables, alpha, core, carried=None):
    n = f"l{li}_b_"
    wts = sv["wts"]
    gr = {}
    dh1_a, dffn, dpg, dpp, gr["ln2_g"], gr["ln2_b"] = _rowwise_vjp(
        _make_ple_ln(alpha), [sv["h1"], sv["ffn"], sv["pg"], sv["pp"]], [sv["g2"], sv["b2"]], [dh2_parts],
        groups=[[0], [1], [2], [3]], name=n + "ln2")
    big = {}
    big["ple_w_proj"] = _cols_to_blocks(_mm(p_i, dpp, am="km", name=n + "ple_proj_dw"), name=n + "ple_proj_dw_blocks")
    big["ple_w_gate"] = _mm(sv["h1"], dpg, am="km", name=n + "ple_gate_dw")
    dh1_b = _mm(dpg, wts["ple_w_gate"], bm="nk", name=n + "ple_gate_dx")
    big["w_down"] = _mm(sv["act"], dffn, am="bkm", name=n + "down_dw")
    dact = _mm(dffn, wts["w_down"], bm="nk", tn=sv["act"].shape[2], om="bmn", name=n + "down_dx")
    dgu = _swiglu3_bwd(sv["gu"], dact, name=n + "swiglu")
    big["w_gate_up"], carried_got = _mm(sv["h1"], dgu, am="km", bm="bkn", om="bmn", name=n + "gate_up_dw",
                                        job=carried), None
    if carried is not None:
        big["w_gate_up"], carried_got = big["w_gate_up"]
    dh1_c = _mm(dgu, wts["w_gate_up"], am="bmk", bm="bnk", name=n + "gate_up_dx")
    dh_a, dmix, gr["ln1_g"], gr["ln1_b"] = _rowwise_vjp(
        _make_post_mix(alpha), [sv["h"], sv["mix"]], [sv["g1"], sv["b1"]], [[dh1_a, dh1_b, dh1_c]],
        groups=[[0], [1]], name=n + "ln1")
    big["w_out"] = _mm(sv["cat"], dmix, am="km", name=n + "out_proj_dw")
    early = [_as_pairs(big[k]) for k in RS_EARLY]
    dcat, theirs = _mm(dmix, wts["w_out"], bm="nk", name=n + "out_proj_dx", job=_pair_job(early))
    sums = [_pair_add(x, r, core, name=n + "pair_add_" + k) for k, x, r in zip(RS_EARLY, early, theirs)]

    (dqr, dkv, dkf), early_quads = _attn_bwd(sv["qr"], sv["kf"], sv["kvb"], dcat, sv["o_c"], sv["lse"],
                                             name=n + "attn", job=_quad_job(sums))
    dqpad, dkr = _mla_prep_bwd(dqr, dkf, tables, name=n + "mla_prep")
    big["mla_w_uq"] = _cols_to_blocks(_mm(sv["cqn"], dqpad, am="km", name=n + "uq_dw"), name=n + "uq_dw_blocks")
    dcqn = _mm(dqpad, wts["mla_w_uq"], bm="nk", name=n + "uq_dx")
    big["mla_w_ukv"] = _cols_to_blocks(_mm(sv["ckvn"], dkv, am="km", name=n + "ukv_dw"), name=n + "ukv_dw_blocks")
    dckvn = _mm(dkv, wts["mla_w_ukv"], bm="nk", name=n + "ukv_dx")
    projp = sv["projp"]
    dcq, gr["mla_q_norm_g"] = _rowwise_vjp(_fn_rms, [(projp, 384, P_CQ // 384)], [sv["qg"]], [[dcqn]],
                                           groups=[[0]], name=n + "q_norm")
    dckv, gr["mla_kv_norm_g"] = _rowwise_vjp(_fn_rms, [(projp, 256, P_CKV // 256)], [sv["kvg"]], [[dckvn]],
                                             groups=[[0]], name=n + "kv_norm")
    dsgu, gr["sgu_ln_g"], gr["sgu_ln_b"], gr["sgu_w_s"], gr["sgu_b_s"] = _sgu_bwd(
        projp, sv["lg"], sv["lbias"], sv["w_s"], sv["bias_full"], dcat, name=n + "sgu")
    dhg, gr["hgrn_norm_g"], gr["lower_bound"] = _hgrn_bwd(
        projp, lbs[li], sv["ng"], sv["o_pre"], sv["states"], dcat, name=n + "hgrn")
    dprojp = jnp.concatenate([dhg, dsgu, dcq, dkr, dckv], axis=1)
    big["w_in"] = _unplace_w_in(_mm(sv["h"], dprojp, am="km", name=n + "proj_dw"), name=n + "proj_dw_shards")
    late = [_as_pairs(big[k]) for k in RS_LATE]
    dh_b, theirs = _mm(dprojp, wts["w_in"], bm="nk", name=n + "proj_dx", job=_pair_job(late))
    late_sums = [_pair_add(x, r, core, name=n + "pair_add_" + k) for k, x, r in zip(RS_LATE, late, theirs)]
    return [dh_a, dh_b], gr, early_quads, late_sums, carried_got


def kernel(x, p, positions, ln_in_g, ln_in_b, w_in, hgrn_lb_logits, hgrn_norm_g, sgu_ln_g, sgu_ln_b, sgu_w_s, sgu_b_s, mla_q_norm_g, mla_w_uq, mla_kv_norm_g, mla_w_ukv, w_out, ln1_g, ln1_b, w_gate_up, w_down, ple_w_gate, ple_w_proj, ln2_g, ln2_b, loss_target, m_ln_in_g, m_ln_in_b, m_w_in, m_hgrn_lb_logits, m_hgrn_norm_g, m_sgu_ln_g, m_sgu_ln_b, m_sgu_w_s, m_sgu_b_s, m_mla_q_norm_g, m_mla_w_uq, m_mla_kv_norm_g, m_mla_w_ukv, m_w_out, m_ln1_g, m_ln1_b, m_w_gate_up, m_w_down, m_ple_w_gate, m_ple_w_proj, m_ln2_g, m_ln2_b, v_ln_in_g, v_ln_in_b, v_w_in, v_hgrn_lb_logits, v_hgrn_norm_g, v_sgu_ln_g, v_sgu_ln_b, v_sgu_w_s, v_sgu_b_s, v_mla_q_norm_g, v_mla_w_uq, v_mla_kv_norm_g, v_mla_w_ukv, v_w_out, v_ln1_g, v_ln1_b, v_w_gate_up, v_w_down, v_ple_w_gate, v_ple_w_proj, v_ln2_g, v_ln2_b):
    args = dict(locals())
    w = {k: args[k] for k in ORDER}
    m = {k: args["m_" + k] for k in ORDER}
    v = {k: args["v_" + k] for k in ORDER}
    depth = w_in.shape[0]
    assert depth == 2, "the lower-bound kernel is written for two layers"
    alpha = (2 * depth) ** 0.25
    xs, tgt = x[0], loss_target[0]
    d_model = xs.shape[1]

    shards = [_weight_shards(w, li) for li in range(depth)]
    rest = [k for k in BIG if k != "w_in"]
    (g_in,) = _all_gather([shards[0]["w_in"]], name="gather_l0_w_in")
    w_in0 = _usable_weights({"w_in": g_in}, name="l0")

    def rest_of_layer0(got):
        got = _gather_forward(got, name="gather_l0_forward")
        return _usable_weights(dict(zip(rest, got)), name="l0")

    tables = _rope_tables(positions[0])
    row1 = lambda a: a.reshape(1, -1)
    l0, l1 = row1(hgrn_lb_logits[0]), row1(hgrn_lb_logits[1])
    lbs = _rowwise(_fn_lower_bounds, [l0, l1], [], [(HG_W, F32), (HG_W, F32)], name="lower_bounds")

    gin, bin_ = row1(ln_in_g), row1(ln_in_b)
    (h,) = _rowwise(_fn_ln, [xs], [gin, bin_], [(d_model, F32)], name="ln_in")
    h, sv0, got1 = _layer_forward(0, h, p[0, 0], w_in0, w, lbs, tables, alpha,
                                  hgrn_job=_gather_job([shards[0][k] for k in rest]), more_weights=rest_of_layer0,
                                  attn_job=_gather_job([shards[1][k] for k in BIG]))
    wts1 = _usable_weights(dict(zip(BIG, _gather_forward(got1, name="gather_l1_forward"))), name="l1")
    h, sv1, _ = _layer_forward(1, h, p[1, 0], wts1, w, lbs, tables, alpha)
    saved = [sv0, sv1]
    dy, loss_local = _loss_and_grad(h, tgt, name="loss")
    loss = lax.psum(loss_local[0, 0], ("x", "y", "c"))

    core = lax.axis_index("c").astype(jnp.int32).reshape(1)
    dparts, grads, quads, carried = [dy], [None] * depth, [None] * depth, None
    for li in reversed(range(depth)):
        dparts, grads[li], early_quads, late_sums, late_quads = _layer_backward(
            li, dparts, p[li, 0], saved[li], lbs, tables, alpha, core, carried=carried)
        quads[li] = dict(zip(RS_EARLY, early_quads))
        if carried is not None:
            quads[li + 1].update(zip(RS_LATE, late_quads))
        carried = _quad_job(late_sums)
    quads[0].update(zip(RS_LATE, _run_job(carried, name="rs_l0_late_quad")))
    dx, d_gin, d_bin = _rowwise_vjp(_fn_ln, [xs], [gin, bin_], [dparts], groups=[[0]], name="ln_in_b")
    dl0, dl1 = _rowwise_vjp(_fn_lower_bounds, [l0, l1], [], [[grads[0]["lower_bound"]], [grads[1]["lower_bound"]]],
                            groups=[[0], [1]], name="lower_bounds_b")

    prefixes = ("grad_", "delta_", "new_m_", "new_v_")
    per_layer = {pre + k: [] for pre in prefixes for k in BIG}
    uq_pad = ((0, 0), (0, 0), (0, LANES - ATT_D))
    state = {k: ((jnp.pad(w[k], uq_pad), jnp.pad(m[k], uq_pad), jnp.pad(v[k], uq_pad)) if k == "mla_w_uq"
                 else (w[k], m[k], v[k])) for k in BIG}
    for li in range(depth):
        for k in BIG:
            res4 = _adamw(quads[li][k], *state[k], li, name=f"adamw_l{li}_{k}")
            for pre, a in zip(prefixes, res4):
                per_layer[pre + k].append(a[:, :ATT_D] if k == "mla_w_uq" else a)
    out = {name: jnp.stack(vals) for name, vals in per_layer.items()}

    small_g = {"ln_in_g": d_gin.reshape(-1), "ln_in_b": d_bin.reshape(-1),
               "hgrn_lb_logits": jnp.stack([dl0.reshape(-1), dl1.reshape(-1)])}
    for k in SMALL[3:]:
        small_g[k] = jnp.stack([grads[li][k].reshape(w[k].shape[1:]) for li in range(depth)])
    (small_parts,) = _all_gather([_pack([small_g[k] for k in SMALL])], name="gather_small_grads")
    slabs = _adamw(small_parts, _pack([w[k] for k in SMALL]), _pack([m[k] for k in SMALL]),
                   _pack([v[k] for k in SMALL]), None, name="adamw_small")
    shapes = [w[k].shape for k in SMALL]
    for pre, slab in zip(prefixes, slabs):
        for k, a in zip(SMALL, _unpack(slab, shapes)):
            out[pre + k] = a
    res = [loss, dx[None]]
    for prefix in ("grad_", "delta_", "new_m_", "new_v_"):
        res += [out[prefix + k] for k in ORDER]
    return tuple(res)
```

```python
import functools
import math

import jax
import jax.numpy as jnp
from jax import lax
from jax.experimental import pallas as pl
from jax.experimental.pallas import tpu as pltpu

F32 = jnp.float32
BF16 = jnp.bfloat16
MESH = pl.DeviceIdType.MESH

LN_EPS = 1e-5
RMS_EPS = 1e-6
ROPE_THETA = 10000.0
ADAM_LR, ADAM_B1, ADAM_B2, ADAM_EPS, ADAM_WD, ADAM_STEP = 0.001, 0.9, 0.999, 1e-08, 0.01, 10

N_DEV = 8
LANES = 128
HG_CHUNK = 16
HG_W = 256
HEAD = 64
SGU_CHUNK = 128
N_ATT_HEADS = 8
ATT_D = 96
VMEM_LIMIT = 56 * 1024 * 1024

HG_TILE = 256
ATT_TQ = 512
ROW_TILE = 256

P_CQ, P_KR, P_CKV, P_COLS = 1536, 1920, 2048, 2304


def _cparams(sem):
    return pltpu.CompilerParams(dimension_semantics=sem, vmem_limit_bytes=VMEM_LIMIT)


_ANY = pl.BlockSpec(memory_space=pl.ANY)


def _call(body, operands, *, name, grid, in_specs, out_specs, out_shape, sem, scratch_shapes=(), job=None):
    if job is None:
        return pl.pallas_call(body, name=name, grid=grid, in_specs=in_specs, out_specs=out_specs, out_shape=out_shape,
                              scratch_shapes=list(scratch_shapes), compiler_params=_cparams(sem))(*operands)
    single = not isinstance(out_shape, (list, tuple))
    shapes = [out_shape] if single else list(out_shape)
    ospecs = [out_specs] if single else list(out_specs)
    ni, no, ns = len(operands), len(shapes), len(scratch_shapes)
    ji, jo = len(job.inputs), len(job.out_shapes)

    def hosted(*refs):
        p = 0
        parts = []
        for cnt in (ni, ji, no, jo, ns):
            parts.append(refs[p:p + cnt])
            p += cnt
        ins, jins, outs, jouts, scr = parts
        jsems = refs[p:]
        ids = [pl.program_id(a) for a in range(len(grid))]
        first = functools.reduce(lambda a, b: a & b, [i == 0 for i in ids])
        last = functools.reduce(lambda a, b: a & b, [i == g - 1 for i, g in zip(ids, grid)])

        @pl.when(first)
        def _():
            job.start(jins, jouts, jsems)

        body(*ins, *outs, *scr)

        @pl.when(last)
        def _():
            job.finish(jins, jouts, jsems)

    res = pl.pallas_call(
        hosted, name=name, grid=grid,
        in_specs=list(in_specs) + [_ANY] * ji, out_specs=ospecs + [_ANY] * jo,
        out_shape=shapes + list(job.out_shapes),
        scratch_shapes=list(scratch_shapes) + [pltpu.SemaphoreType.DMA((c,)) for c in job.sem_counts],
        compiler_params=_cparams(("arbitrary",) * len(grid)),
    )(*operands, *job.inputs)
    own = res[0] if single else res[:no]
    return own, res[no:]


class _Job:
    def __init__(self, inputs, out_shapes, sem_counts, start, finish):
        self.inputs, self.out_shapes, self.sem_counts = list(inputs), list(out_shapes), list(sem_counts)
        self.start, self.finish = start, finish


def _copies_job(inputs, out_shapes, n_remote, n_local, make):
    def start(jins, jouts, sems):
        sends, _, local = make(jins, jouts, *sems)
        for cp in local + sends:
            cp.start()

    def finish(jins, jouts, sems):
        sends, recvs, local = make(jins, jouts, *sems)
        for cp in recvs:
            cp.wait_recv()
        for cp in sends:
            cp.wait_send()
        for cp in local:
            cp.wait()

    return _Job(inputs, out_shapes, [n_remote, n_remote, max(n_local, 1)], start, finish)


def _run_job(job, *, name):
    ji, jo = len(job.inputs), len(job.out_shapes)

    def body(*refs):
        jins, jouts, sems = refs[:ji], refs[ji:ji + jo], refs[ji + jo:]
        job.start(jins, jouts, sems)
        job.finish(jins, jouts, sems)

    return pl.pallas_call(
        body, name=name, out_shape=list(job.out_shapes), in_specs=[_ANY] * ji, out_specs=[_ANY] * jo,
        scratch_shapes=[pltpu.SemaphoreType.DMA((c,)) for c in job.sem_counts],
    )(*job.inputs)


def _tile(n, pref):
    if n % pref == 0:
        return pref
    best = None
    t = LANES
    while t <= min(n, pref):
        if n % t == 0:
            best = t
        t += LANES
    return best if best is not None else n


def _mm(a, b, *, am="mk", bm="kn", om="mn", out_dtype=F32, tm=1024, tn=1024, tk=1024, name, job=None):
    if am == "mk":
        m, k = a.shape
    elif am == "km":
        k, m = a.shape
    elif am == "bmk":
        m, tk = a.shape[1], a.shape[2]
        k = a.shape[0] * tk
    else:
        k, tm = a.shape[1], a.shape[2]
        m = a.shape[0] * tm
    if bm == "kn":
        kb_, n = b.shape
    elif bm == "nk":
        n, kb_ = b.shape
    elif bm == "bkn":
        kb_, tn = b.shape[1], b.shape[2]
        n = b.shape[0] * tn
    else:
        n, tk = b.shape[1], b.shape[2]
        kb_ = b.shape[0] * tk
    assert kb_ == k, (a.shape, b.shape, am, bm)
    tm, tn, tk = _tile(m, tm), _tile(n, tn), _tile(k, tk)
    nk = k // tk
    dims = (((0 if am in ("km", "bkm") else 1,), (1 if bm in ("nk", "bnk") else 0,)), ((), ()))

    a_spec = {"mk": pl.BlockSpec((tm, tk), lambda i, j, kk: (i, kk)),
              "km": pl.BlockSpec((tk, tm), lambda i, j, kk: (kk, i)),
              "bmk": pl.BlockSpec((None, tm, tk), lambda i, j, kk: (kk, i, 0)),
              "bkm": pl.BlockSpec((None, tk, tm), lambda i, j, kk: (i, kk, 0))}[am]
    b_spec = {"kn": pl.BlockSpec((tk, tn), lambda i, j, kk: (kk, j)),
              "nk": pl.BlockSpec((tn, tk), lambda i, j, kk: (j, kk)),
              "bkn": pl.BlockSpec((None, tk, tn), lambda i, j, kk: (j, kk, 0)),
              "bnk": pl.BlockSpec((None, tn, tk), lambda i, j, kk: (kk, j, 0))}[bm]
    if om == "mn":
        o_spec, o_shape = pl.BlockSpec((tm, tn), lambda i, j, kk: (i, j)), (m, n)
    else:
        o_spec, o_shape = pl.BlockSpec((None, tm, tn), lambda i, j, kk: (j, i, 0)), (n // tn, m, tn)

    def body(a_ref, b_ref, o_ref, *acc):
        kk = pl.program_id(2)
        prod = lax.dot_general(a_ref[...].astype(BF16), b_ref[...].astype(BF16), dims, preferred_element_type=F32)
        if nk == 1:
            o_ref[...] = prod.astype(o_ref.dtype)
            return
        acc_ref, = acc

        @pl.when(kk == 0)
        def _():
            acc_ref[...] = prod

        if nk > 2:
            @pl.when((kk > 0) & (kk < nk - 1))
            def _():
                acc_ref[...] += prod

        @pl.when(kk == nk - 1)
        def _():
            o_ref[...] = (acc_ref[...] + prod).astype(o_ref.dtype)

    return _call(body, (a, b), name=name, grid=(m // tm, n // tn, nk), in_specs=[a_spec, b_spec], out_specs=o_spec,
                 out_shape=jax.ShapeDtypeStruct(o_shape, out_dtype),
                 scratch_shapes=[pltpu.VMEM((tm, tn), F32)] if nk > 1 else [],
                 sem=("parallel", "parallel", "arbitrary"), job=job)


def _row_operand(a, tile):
    if isinstance(a, tuple):
        arr, w, j = a
        return arr, pl.BlockSpec((tile, w), lambda i, j=j: (i, j))
    return a, pl.BlockSpec((tile, a.shape[1]), lambda i: (i, 0))


def _const_spec(c):
    nd = c.ndim
    return pl.BlockSpec(c.shape, lambda i, nd=nd: (0,) * nd)


def _rowwise(fn, rows, consts, outs, *, name, accs=(), tile=None):
    t_rows = (rows[0][0] if isinstance(rows[0], tuple) else rows[0]).shape[0]
    tile = min(tile or ROW_TILE, t_rows)
    arrs, specs = zip(*[_row_operand(a, tile) for a in rows])
    nin, no = len(rows) + len(consts), len(outs)

    def body(*refs):
        res = fn(*[r[...] for r in refs[:nin]])
        for r, v in zip(refs[nin:nin + no], res[:no]):
            r[...] = v.astype(r.dtype)
        if accs:
            a_refs = refs[nin + no:]

            @pl.when(pl.program_id(0) == 0)
            def _():
                for r in a_refs:
                    r[...] = jnp.zeros_like(r)

            for r, v in zip(a_refs, res[no:]):
                r[...] += v

    out_shape = [jax.ShapeDtypeStruct((t_rows, w), dt) for w, dt in outs]
    out_shape += [jax.ShapeDtypeStruct(s, F32) for s in accs]
    out_specs = [pl.BlockSpec((tile, w), lambda i: (i, 0)) for w, _ in outs]
    out_specs += [pl.BlockSpec(s, lambda i, nd=len(s): (0,) * nd) for s in accs]
    return pl.pallas_call(
        body, name=name, grid=(t_rows // tile,),
        in_specs=list(specs) + [_const_spec(c) for c in consts],
        out_specs=out_specs, out_shape=out_shape,
        compiler_params=_cparams(("arbitrary",)),
    )(*arrs, *consts)


def _rowwise_vjp(fn, rows, consts, cts, *, name, groups, tile=None, gdtypes=None):
    t_rows = (rows[0][0] if isinstance(rows[0], tuple) else rows[0]).shape[0]
    tile = min(tile or ROW_TILE, t_rows)
    arrs, specs = zip(*[_row_operand(a, tile) for a in rows])
    flat_cts = [c for group in cts for c in group]
    ct_arrs, ct_specs = zip(*[_row_operand(a, tile) for a in flat_cts])
    nr, nc, nct, ng = len(rows), len(consts), len(flat_cts), len(groups)

    def width(a):
        return a[1] if isinstance(a, tuple) else a.shape[1]

    def body(*refs):
        rv = [r[...].astype(F32) for r in refs[:nr]]
        cv = [r[...] for r in refs[nr:nr + nc]]
        ct_refs = refs[nr + nc:nr + nc + nct]
        ctv, pos = [], 0
        for group in cts:
            s = ct_refs[pos][...].astype(F32)
            for r in ct_refs[pos + 1:pos + len(group)]:
                s = s + r[...].astype(F32)
            ctv.append(s)
            pos += len(group)
        _, pull = jax.vjp(fn, *rv, *cv)
        grads = pull(tuple(ctv))
        g_refs = refs[nr + nc + nct:nr + nc + nct + ng]
        for r, idx in zip(g_refs, groups):
            parts = [grads[i] for i in idx]
            r[...] = (parts[0] if len(parts) == 1 else jnp.concatenate(parts, axis=1)).astype(r.dtype)
        c_refs = refs[nr + nc + nct + ng:]

        @pl.when(pl.program_id(0) == 0)
        def _():
            for r in c_refs:
                r[...] = jnp.zeros_like(r)

        for r, v in zip(c_refs, grads[nr:]):
            r[...] += v

    gw = [sum(width(rows[i]) for i in idx) for idx in groups]
    gdtypes = gdtypes or [F32] * ng
    out_shape = [jax.ShapeDtypeStruct((t_rows, w), dt) for w, dt in zip(gw, gdtypes)]
    out_shape += [jax.ShapeDtypeStruct(c.shape, F32) for c in consts]
    out_specs = [pl.BlockSpec((tile, w), lambda i: (i, 0)) for w in gw]
    out_specs += [_const_spec(c) for c in consts]
    return pl.pallas_call(
        body, name=name, grid=(t_rows // tile,),
        in_specs=list(specs) + [_const_spec(c) for c in consts] + list(ct_specs),
        out_specs=out_specs, out_shape=out_shape,
        compiler_params=_cparams(("arbitrary",)),
    )(*arrs, *consts, *ct_arrs)


def _layer_norm(x, g, b):
    mu = jnp.mean(x, axis=-1, keepdims=True)
    xc = x - mu
    var = jnp.mean(xc * xc, axis=-1, keepdims=True)
    return xc * lax.rsqrt(var + LN_EPS) * g + b


def _sigmoid(x):
    return 1.0 / (1.0 + jnp.exp(-x))


def _fn_ln(x, g, b):
    return (_layer_norm(x, g, b),)


def _fn_rms(x, g):
    return (x * lax.rsqrt(jnp.mean(x * x, axis=-1, keepdims=True) + RMS_EPS) * g,)


def _make_post_mix(alpha):
    def fn(h, mix, g, b):
        return (_layer_norm(alpha * h + mix, g, b),)
    return fn


def _make_ple_ln(alpha):
    def fn(h1, ffn, pg, pp, g, b):
        return (_layer_norm(alpha * h1 + ffn + _sigmoid(pg) * pp, g, b),)
    return fn


def _fn_lower_bounds(l0, l1):
    m = jnp.maximum(l0, l1)
    e0, e1 = jnp.exp(l0 - m), jnp.exp(l1 - m)
    s = e0 + e1
    p0, p1 = e0 / s, e1 / s
    return (p0 - p0, (p0 + p1) - p0)


def _loss_and_grad(y, target, *, name):
    d = y.shape[1]

    def fn(yv, tv):
        err = yv - tv
        return err * (1.0 / d), 0.5 * jnp.sum(jnp.mean(err * err, axis=-1, keepdims=True), axis=0, keepdims=True)

    return _rowwise(fn, [y, target], [], [(d, F32)], accs=[(1, 1)], name=name)


def _split_dot(x, e_bf16):
    hi = x.astype(BF16)
    lo = (x - hi.astype(F32)).astype(BF16)
    return (jnp.dot(hi, e_bf16, preferred_element_type=F32) + jnp.dot(lo, e_bf16, preferred_element_type=F32))


def _hgrn_common(th):
    rm = lax.broadcasted_iota(jnp.int32, (th, HG_W), 0) % HG_CHUNK

    def seg_cumsum(x):
        for s in (1, 2, 4, 8):
            x = x + jnp.where(rm >= s, pltpu.roll(x, s, 0), 0.0)
        return x

    def seg_rcumsum(x):
        for s in (1, 2, 4, 8):
            x = x + jnp.where(rm < HG_CHUNK - s, pltpu.roll(x, th - s, 0), 0.0)
        return x

    ri = lax.broadcasted_iota(jnp.int32, (HG_W, HG_W), 0) // HEAD
    ci = lax.broadcasted_iota(jnp.int32, (HG_W, HG_W), 1) // HEAD
    head_f32 = (ri == ci).astype(F32)
    head_bf16 = head_f32.astype(BF16)

    def headsum(x):
        return _split_dot(x, head_bf16)

    return rm, seg_cumsum, seg_rcumsum, head_f32, headsum


def _hgrn_gates(qr, fl, lb):
    sg = _sigmoid(fl)
    f = lb + (1.0 - lb) * sg
    sq = _sigmoid(qr)
    return sg, f, jnp.log(f), 1.0 - f, qr * sq, sq


def _shifted(x, d, th):
    return x if d == 0 else pltpu.roll(x, d, 0)


def _unshift(x, d, th):
    return x if d == 0 else pltpu.roll(x, th - d, 0)


def _hgrn_fwd(projp, lb, ng, *, name, job=None):
    t_rows = projp.shape[0]
    th = min(HG_TILE, t_rows)
    nct = th // HG_CHUNK

    def body(q_ref, f_ref, i_ref, g_ref, lb_ref, ng_ref, oa_ref, opre_ref, st_out_ref,
             st_ref, vtm_ref, kv_ref, qe_ref, dec_ref, oint_ref):
        rm, seg_cumsum, seg_rcumsum, head_f32, headsum = _hgrn_common(th)

        @pl.when(pl.program_id(0) == 0)
        def _():
            st_ref[...] = jnp.zeros_like(st_ref)

        qr, fl, v, g = q_ref[...], f_ref[...], i_ref[...], g_ref[...]
        _, f, lf, k, q, _ = _hgrn_gates(qr, fl, lb_ref[...])
        b = seg_cumsum(lf)

        o = jnp.zeros((th, HG_W), F32)
        for d in range(HG_CHUNK):
            kd, bd, vd = _shifted(k, d, th), _shifted(b, d, th), _shifted(v, d, th)
            e = jnp.exp(jnp.where(rm >= d, b - bd, -1e30))
            o = o + headsum(q * kd * e) * vd

        blast = seg_rcumsum(jnp.where(rm == HG_CHUNK - 1, b, 0.0))
        kte = (k * jnp.exp(blast - b)).astype(BF16)
        qe_ref[...] = q * jnp.exp(b)
        dec_ref[...] = jnp.exp(blast)
        vt = v.T
        lane_chunk = lax.broadcasted_iota(jnp.int32, (HG_W, th), 1) // HG_CHUNK
        for c in range(nct):
            vtm_ref[c * HG_W:(c + 1) * HG_W, :] = jnp.where(lane_chunk == c, vt, 0.0).astype(BF16)
        kv_ref[...] = jnp.dot(vtm_ref[...], kte, preferred_element_type=F32)

        def step(c, carry):
            r0 = pl.multiple_of(c * HG_CHUNK, HG_CHUNK)
            s = st_ref[...]
            st_out_ref[c] = s
            oint_ref[pl.ds(r0, HG_CHUNK), :] = lax.dot_general(
                qe_ref[pl.ds(r0, HG_CHUNK), :].astype(BF16), s.astype(BF16),
                (((1,), (1,)), ((), ())), preferred_element_type=F32)
            dec = jnp.max(dec_ref[pl.ds(r0, HG_CHUNK), :], axis=0, keepdims=True)
            kv_c = kv_ref[pl.ds(pl.multiple_of(c * HG_W, HG_W), HG_W), :]
            st_ref[...] = s * dec + kv_c * head_f32
            return carry

        lax.fori_loop(0, nct, step, 0)

        o = o + oint_ref[...]
        opre_ref[...] = o
        r = lax.rsqrt(headsum(o * o) * (1.0 / HEAD) + RMS_EPS)
        oa_ref[...] = (o * r * ng_ref[...] * (g * _sigmoid(g))).astype(oa_ref.dtype)

    col = lambda j: pl.BlockSpec((th, HG_W), lambda i, j=j: (i, j))
    vec = pl.BlockSpec((1, HG_W), lambda i: (0, 0))
    row = pl.BlockSpec((th, HG_W), lambda i: (i, 0))
    n_chunks = t_rows // HG_CHUNK
    return _call(
        body, (projp, projp, projp, projp, lb, ng), name=name, grid=(t_rows // th,),
        in_specs=[col(0), col(1), col(2), col(3), vec, vec],
        out_specs=[row, row, pl.BlockSpec((nct, HG_W, HG_W), lambda i: (i, 0, 0))],
        out_shape=[jax.ShapeDtypeStruct((t_rows, HG_W), BF16), jax.ShapeDtypeStruct((t_rows, HG_W), F32),
                   jax.ShapeDtypeStruct((n_chunks, HG_W, HG_W), F32)],
        scratch_shapes=[pltpu.VMEM((HG_W, HG_W), F32), pltpu.VMEM((nct * HG_W, th), BF16),
                        pltpu.VMEM((nct * HG_W, HG_W), F32), pltpu.VMEM((th, HG_W), F32),
                        pltpu.VMEM((th, HG_W), F32), pltpu.VMEM((th, HG_W), F32)],
        sem=("arbitrary",), job=job)


def _hgrn_bwd(projp, lb, ng, opre, states, dcat, *, name):
    t_rows = projp.shape[0]
    th = min(HG_TILE, t_rows)
    nct = th // HG_CHUNK
    nt = t_rows // th

    def body(q_ref, f_ref, i_ref, g_ref, lb_ref, ng_ref, opre_ref, st_in_ref, do_ref,
             dproj_ref, dng_ref, dlb_ref,
             gst_ref, dotm_ref, qg_ref, v_ref, kte_ref, dop_ref, dec_ref, dkte_ref, dvi_ref, dqe_ref, ddec_ref):
        rm, seg_cumsum, seg_rcumsum, head_f32, headsum = _hgrn_common(th)

        @pl.when(pl.program_id(0) == 0)
        def _():
            gst_ref[...] = jnp.zeros_like(gst_ref)
            dng_ref[...] = jnp.zeros_like(dng_ref)
            dlb_ref[...] = jnp.zeros_like(dlb_ref)

        qr, fl, v, g = q_ref[...], f_ref[...], i_ref[...], g_ref[...]
        lb, ngv = lb_ref[...], ng_ref[...]
        sg, f, lf, k, q, sq = _hgrn_gates(qr, fl, lb)
        b = seg_cumsum(lf)
        blast = seg_rcumsum(jnp.where(rm == HG_CHUNK - 1, b, 0.0))
        eb = jnp.exp(b)
        ekb = jnp.exp(blast - b)
        qe, kte, dec = q * eb, k * ekb, jnp.exp(blast)

        do_out, op = do_ref[...], opre_ref[...]
        sgg = _sigmoid(g)
        sil = g * sgg
        r = lax.rsqrt(headsum(op * op) * (1.0 / HEAD) + RMS_EPS)
        on = op * r
        dng_ref[...] += jnp.sum(do_out * on * sil, axis=0, keepdims=True)
        dg = do_out * on * ngv * (sgg * (1.0 + g * (1.0 - sgg)))
        don = do_out * ngv * sil
        dop = r * (don - on * (headsum(don * on) * (1.0 / HEAD)))

        v_ref[...] = v
        kte_ref[...] = kte
        dop_ref[...] = dop
        dec_ref[...] = dec
        dot_t = dop.T
        lane_chunk = lax.broadcasted_iota(jnp.int32, (HG_W, th), 1) // HG_CHUNK
        for c in range(nct):
            dotm_ref[c * HG_W:(c + 1) * HG_W, :] = jnp.where(lane_chunk == c, dot_t, 0.0).astype(BF16)
        qg_ref[...] = jnp.dot(dotm_ref[...], qe.astype(BF16), preferred_element_type=F32)

        def step(j, carry):
            c = nct - 1 - j
            r0 = pl.multiple_of(c * HG_CHUNK, HG_CHUNK)
            rows = pl.ds(r0, HG_CHUNK)
            gs = gst_ref[...]
            s = st_in_ref[c]
            gm = (gs * head_f32).astype(BF16)
            dkte_ref[rows, :] = jnp.dot(v_ref[rows, :].astype(BF16), gm, preferred_element_type=F32)
            dvi_ref[rows, :] = lax.dot_general(kte_ref[rows, :].astype(BF16), gm, (((1,), (1,)), ((), ())),
                                               preferred_element_type=F32)
            dqe_ref[rows, :] = jnp.dot(dop_ref[rows, :].astype(BF16), s.astype(BF16), preferred_element_type=F32)
            ddec_ref[rows, :] = jnp.broadcast_to(jnp.sum(gs * s, axis=0, keepdims=True), (HG_CHUNK, HG_W))
            dec_c = jnp.max(dec_ref[rows, :], axis=0, keepdims=True)
            qg_c = qg_ref[pl.ds(pl.multiple_of(c * HG_W, HG_W), HG_W), :]
            gst_ref[...] = gs * dec_c + qg_c * head_f32
            return carry

        lax.fori_loop(0, nct, step, 0)

        dkte, dqe = dkte_ref[...], dqe_ref[...]
        dq = dqe * eb
        dk = dkte * ekb
        db = dqe * qe - dkte * kte
        dv = dvi_ref[...]
        dblast = dkte * kte + jnp.where(rm == HG_CHUNK - 1, ddec_ref[...] * dec, 0.0)

        for d in range(HG_CHUNK):
            kd, bd, vd = _shifted(k, d, th), _shifted(b, d, th), _shifted(v, d, th)
            e = jnp.exp(jnp.where(rm >= d, b - bd, -1e30))
            p = q * kd * e
            sc = headsum(p)
            dsc = headsum(dop * vd)
            dv = dv + _unshift(sc * dop, d, th)
            dq = dq + dsc * kd * e
            dk = dk + _unshift(dsc * q * e, d, th)
            darg = dsc * p
            db = db + darg - _unshift(darg, d, th)

        db = db + jnp.where(rm == HG_CHUNK - 1, seg_cumsum(dblast), 0.0)
        dlf = seg_rcumsum(db)
        df = dlf / f - dk
        dlb_ref[...] += jnp.sum(df * (1.0 - sg), axis=0, keepdims=True)
        dfl = df * (1.0 - lb) * sg * (1.0 - sg)
        dqr = dq * (sq * (1.0 + qr * (1.0 - sq)))
        dproj_ref[...] = jnp.concatenate([dqr, dfl, dv, dg], axis=1).astype(dproj_ref.dtype)

    rev = lambda i: nt - 1 - i
    col = lambda j: pl.BlockSpec((th, HG_W), lambda i, j=j: (rev(i), j))
    vec = pl.BlockSpec((1, HG_W), lambda i: (0, 0))
    row = pl.BlockSpec((th, HG_W), lambda i: (rev(i), 0))
    tile_f32 = pltpu.VMEM((th, HG_W), F32)
    return pl.pallas_call(
        body, name=name, grid=(nt,),
        in_specs=[col(0), col(1), col(2), col(3), vec, vec, row,
                  pl.BlockSpec((nct, HG_W, HG_W), lambda i: (rev(i), 0, 0)), col(0)],
        out_specs=[pl.BlockSpec((th, 4 * HG_W), lambda i: (rev(i), 0)), vec, vec],
        out_shape=[jax.ShapeDtypeStruct((t_rows, 4 * HG_W), BF16), jax.ShapeDtypeStruct((1, HG_W), F32),
                   jax.ShapeDtypeStruct((1, HG_W), F32)],
        scratch_shapes=[pltpu.VMEM((HG_W, HG_W), F32), pltpu.VMEM((nct * HG_W, th), BF16),
                        pltpu.VMEM((nct * HG_W, HG_W), F32)] + [tile_f32] * 8,
        compiler_params=_cparams(("arbitrary",)),
    )(projp, projp, projp, projp, lb, ng, opre, states, dcat)


_INV_SQRT2 = 1.0 / math.sqrt(2.0)
_INV_SQRT2PI = 1.0 / math.sqrt(2.0 * math.pi)


def _gelu(x):
    return 0.5 * x * (1.0 + lax.erf(x * _INV_SQRT2))


def _gelu_grad(x):
    return 0.5 * (1.0 + lax.erf(x * _INV_SQRT2)) + x * jnp.exp(-0.5 * x * x) * _INV_SQRT2PI


def _sgu_parts(bu, bv, lg, lbias, w_ref, n_groups):
    c = SGU_CHUNK
    tril = (lax.broadcasted_iota(jnp.int32, (c, c), 0) >= lax.broadcasted_iota(jnp.int32, (c, c), 1)).astype(F32)
    gid = lax.broadcasted_iota(jnp.int32, bu.shape, 1) // HEAD
    u = _gelu(bu)
    gv = _gelu(bv)
    mu = jnp.mean(gv, axis=-1, keepdims=True)
    xc = gv - mu
    rstd = lax.rsqrt(jnp.mean(xc * xc, axis=-1, keepdims=True) + LN_EPS)
    xhat = xc * rstd
    vn = xhat * lg + lbias
    ws = [w_ref[gi] * tril for gi in range(n_groups)]
    return tril, gid, u, rstd, xhat, vn, ws


def _sgu_fwd(projp, lg, lbias, w_s, bias_full, *, name):
    t_rows = projp.shape[0]
    n_groups = w_s.shape[0]
    c = SGU_CHUNK

    def body(u_ref, v_ref, lg_ref, lb_ref, w_ref, bias_ref, o_ref):
        _, gid, u, _, _, vn, ws = _sgu_parts(u_ref[...], v_ref[...], lg_ref[...], lb_ref[...], w_ref, n_groups)
        vnb = vn.astype(BF16)
        z = bias_ref[...]
        for gi in range(n_groups):
            z = z + jnp.where(gid == gi, jnp.dot(ws[gi].astype(BF16), vnb, preferred_element_type=F32), 0.0)
        o_ref[...] = (u * z).astype(o_ref.dtype)

    col = lambda j: pl.BlockSpec((c, HG_W), lambda i, j=j: (i, j))
    return pl.pallas_call(
        body, name=name, grid=(t_rows // c,),
        in_specs=[col(4), col(5), _const_spec(lg), _const_spec(lbias), _const_spec(w_s), _const_spec(bias_full)],
        out_specs=pl.BlockSpec((c, HG_W), lambda i: (i, 0)),
        out_shape=jax.ShapeDtypeStruct((t_rows, HG_W), BF16),
        compiler_params=_cparams(("arbitrary",)),
    )(projp, projp, lg, lbias, w_s, bias_full)


def _sgu_bwd(projp, lg, lbias, w_s, bias_full, dcat, *, name):
    t_rows = projp.shape[0]
    n_groups = w_s.shape[0]
    c = SGU_CHUNK
    n = t_rows // c

    def body(u_ref, v_ref, lg_ref, lb_ref, w_ref, bias_ref, do_ref,
             dproj_ref, dlg_ref, dlb_ref, dw_ref, dbs_ref, dbias_acc):
        i = pl.program_id(0)

        @pl.when(i == 0)
        def _():
            dlg_ref[...] = jnp.zeros_like(dlg_ref)
            dlb_ref[...] = jnp.zeros_like(dlb_ref)
            dw_ref[...] = jnp.zeros_like(dw_ref)
            dbias_acc[...] = jnp.zeros_like(dbias_acc)

        bu, bv, lg_v = u_ref[...], v_ref[...], lg_ref[...]
        tril, gid, u, rstd, xhat, vn, ws = _sgu_parts(bu, bv, lg_v, lb_ref[...], w_ref, n_groups)
        vnb = vn.astype(BF16)
        z = bias_ref[...]
        for gi in range(n_groups):
            z = z + jnp.where(gid == gi, jnp.dot(ws[gi].astype(BF16), vnb, preferred_element_type=F32), 0.0)
        do = do_ref[...]
        dbu = do * z * _gelu_grad(bu)
        dz = do * u
        dbias_acc[...] += dz
        dvn = jnp.zeros_like(dz)
        for gi in range(n_groups):
            dzg = jnp.where(gid == gi, dz, 0.0).astype(BF16)
            dw_ref[gi] += lax.dot_general(dzg, vnb, (((1,), (1,)), ((), ())), preferred_element_type=F32) * tril
            dvn = dvn + jnp.dot(ws[gi].T.astype(BF16), dzg, preferred_element_type=F32)
        dlg_ref[...] += jnp.sum(dvn * xhat, axis=0, keepdims=True)
        dlb_ref[...] += jnp.sum(dvn, axis=0, keepdims=True)
        dxh = dvn * lg_v
        dgv = rstd * (dxh - jnp.mean(dxh, axis=-1, keepdims=True)
                      - xhat * jnp.mean(dxh * xhat, axis=-1, keepdims=True))
        dproj_ref[...] = jnp.concatenate([dbu, dgv * _gelu_grad(bv)], axis=1).astype(dproj_ref.dtype)

        @pl.when(i == n - 1)
        def _():
            dbs_ref[...] = jnp.sum(dbias_acc[...].T.reshape(n_groups, HEAD, c), axis=1)

    col = lambda j: pl.BlockSpec((c, HG_W), lambda i, j=j: (i, j))
    return pl.pallas_call(
        body, name=name, grid=(n,),
        in_specs=[col(4), col(5), _const_spec(lg), _const_spec(lbias), _const_spec(w_s), _const_spec(bias_full),
                  col(1)],
        out_specs=[pl.BlockSpec((c, 2 * HG_W), lambda i: (i, 0)), _const_spec(lg), _const_spec(lbias),
                   _const_spec(w_s), pl.BlockSpec((n_groups, c), lambda i: (0, 0))],
        out_shape=[jax.ShapeDtypeStruct((t_rows, 2 * HG_W), BF16), jax.ShapeDtypeStruct(lg.shape, F32),
                   jax.ShapeDtypeStruct(lbias.shape, F32), jax.ShapeDtypeStruct(w_s.shape, F32),
                   jax.ShapeDtypeStruct((n_groups, c), F32)],
        scratch_shapes=[pltpu.VMEM((c, HG_W), F32)],
        compiler_params=_cparams(("arbitrary",)),
    )(projp, projp, lg, lbias, w_s, bias_full, dcat)


def _rope_tables(positions):
    t = positions.shape[0]
    inv_freq = ROPE_THETA ** (-jnp.arange(0, 32, 2, dtype=F32) / 32)
    ang = positions.astype(F32)[:, None] * inv_freq
    cos, sin = jnp.cos(ang), jnp.sin(ang)
    z = lambda w: jnp.zeros((t, w), F32)
    cos_t = jnp.concatenate([jnp.ones((t, 64), F32), cos, cos, z(32)], axis=1)
    sin_up = jnp.concatenate([z(80), sin, z(32)], axis=1)
    sin_dn = jnp.concatenate([z(64), -sin, z(48)], axis=1)
    return cos_t, sin_up, sin_dn


def _rep(x, n):
    return x if n == 1 else jnp.concatenate([x] * n, axis=1)


def _rope(x, cos_t, sin_up, sin_dn):
    w = x.shape[1]
    return x * cos_t + pltpu.roll(x, 16, 1) * sin_up + pltpu.roll(x, w - 16, 1) * sin_dn


def _rope_t(dy, cos_t, sin_up, sin_dn):
    w = dy.shape[1]
    return dy * cos_t + pltpu.roll(dy * sin_up, w - 16, 1) + pltpu.roll(dy * sin_dn, 16, 1)


def _mla_prep(q, kv, projp, tables, *, name):
    nh = N_ATT_HEADS

    def fn(qv, kvv, kr, cos_t, sin_up, sin_dn):
        qr = _rope(qv, _rep(cos_t, nh), _rep(sin_up, nh), _rep(sin_dn, nh))
        krr = _rope(kr, cos_t, sin_up, sin_dn)
        lane = lax.broadcasted_iota(jnp.int32, kvv.shape, 1) % LANES
        return qr, jnp.where(lane < HEAD, kvv, 0.0) + _rep(krr, nh), kvv

    w = q.shape[1]
    return _rowwise(fn, [q, kv, (projp, LANES, P_KR // LANES)] + list(tables), [],
                    [(w, BF16), (w, BF16), (w, BF16)], name=name)


def _mla_prep_bwd(dqr, dkf, tables, *, name):
    nh = N_ATT_HEADS

    def fn(dq, dk, cos_t, sin_up, sin_dn):
        dqp = _rope_t(dq, _rep(cos_t, nh), _rep(sin_up, nh), _rep(sin_dn, nh))
        dkrr = dk[:, 0:LANES]
        for h in range(1, nh):
            dkrr = dkrr + dk[:, LANES * h:LANES * (h + 1)]
        return dqp, _rope_t(dkrr, cos_t, sin_up, sin_dn)

    return _rowwise(fn, [dqr, dkf] + list(tables), [], [(dqr.shape[1], BF16), (LANES, BF16)], name=name)


_NT = (((1,), (1,)), ((), ()))
_TN = (((0,), (0,)), ((), ()))


def _attn_fwd(qr, kf, kvb, *, name, job=None):
    t_rows = qr.shape[0]
    tq = min(ATT_TQ, t_rows)
    nb = t_rows // tq
    scale = ATT_D ** -0.5

    def body(q_ref, kf_ref, kvb_ref, o_ref, lse_ref):
        qi = pl.program_id(1)
        lane = lax.broadcasted_iota(jnp.int32, (tq, LANES), 1)
        causal = (lax.broadcasted_iota(jnp.int32, (tq, tq), 1) <= lax.broadcasted_iota(jnp.int32, (tq, tq), 0))
        outs = []
        for hh in range(2):
            cols = slice(hh * LANES, (hh + 1) * LANES)
            q = q_ref[:, cols]

            def block(ki, carry, diagonal, q=q, cols=cols):
                m_old, l_old, acc = carry
                rows = pl.ds(pl.multiple_of(ki * tq, tq), tq)
                s = lax.dot_general(q, kf_ref[rows, cols], _NT, preferred_element_type=F32) * scale
                if diagonal:
                    s = jnp.where(causal, s, -1e30)
                m_new = jnp.maximum(m_old, jnp.max(s, axis=-1, keepdims=True))
                p = jnp.exp(s - m_new)
                a = jnp.exp(m_old - m_new)
                return (m_new, a * l_old + jnp.sum(p, axis=-1, keepdims=True),
                        a * acc + jnp.dot(p.astype(BF16), kvb_ref[rows, cols], preferred_element_type=F32))

            init = (jnp.full((tq, 1), -1e30, F32), jnp.zeros((tq, 1), F32), jnp.zeros((tq, LANES), F32))
            carry = lax.fori_loop(0, qi, lambda ki, c, block=block: block(ki, c, False), init)
            m_fin, l_fin, acc = block(qi, carry, True)
            lse_ref[hh] = m_fin + jnp.log(l_fin)
            outs.append(acc / l_fin)
        o_ref[...] = jnp.where(lane < HEAD, pltpu.roll(outs[0], HEAD, 1), outs[1])

    pair = pl.BlockSpec((t_rows, 2 * LANES), lambda pr, qi: (0, pr))
    return _call(
        body, (qr, kf, kvb), name=name, grid=(N_ATT_HEADS // 2, nb),
        in_specs=[pl.BlockSpec((tq, 2 * LANES), lambda pr, qi: (qi, pr)), pair, pair],
        out_specs=[pl.BlockSpec((tq, LANES), lambda pr, qi: (qi, pr)),
                   pl.BlockSpec((2, tq, 1), lambda pr, qi: (pr, qi, 0))],
        out_shape=[jax.ShapeDtypeStruct((t_rows, N_ATT_HEADS * HEAD), F32),
                   jax.ShapeDtypeStruct((N_ATT_HEADS, t_rows, 1), F32)],
        sem=("parallel", "arbitrary"), job=job)


def _attn_bwd(qr, kf, kvb, dcat, o, lse, *, name, job=None):
    t_rows = qr.shape[0]
    tq = min(ATT_TQ, t_rows)
    nb = t_rows // tq
    scale = ATT_D ** -0.5
    do_off = 2 * HG_W // LANES

    def body(q_ref, kf_ref, kvb_ref, do_ref, o_ref, lse_ref, dq_ref, dkv_ref, dk_ref):
        ki = pl.program_id(1)

        @pl.when(ki == 0)
        def _():
            dq_ref[...] = jnp.zeros_like(dq_ref)

        lane = lax.broadcasted_iota(jnp.int32, (tq, LANES), 1)
        causal = (lax.broadcasted_iota(jnp.int32, (tq, tq), 1) <= lax.broadcasted_iota(jnp.int32, (tq, tq), 0))
        dkvs, dks = [], []
        for hh in range(2):
            cols = slice(hh * LANES, (hh + 1) * LANES)
            k, v = kf_ref[:, cols], kvb_ref[:, cols]

            def block(qi, carry, diagonal, hh=hh, cols=cols, k=k, v=v):
                dk, dv = carry
                rows = pl.ds(pl.multiple_of(qi * tq, tq), tq)
                q = q_ref[rows, cols]
                do, ov = do_ref[rows, :], o_ref[rows, :]
                if hh == 0:
                    do, ov = pltpu.roll(do, HEAD, 1), pltpu.roll(ov, HEAD, 1)
                do = jnp.where(lane >= HEAD, do, 0.0)
                delta = jnp.sum(do * ov, axis=-1, keepdims=True)
                s = lax.dot_general(q, k, _NT, preferred_element_type=F32) * scale
                if diagonal:
                    s = jnp.where(causal, s, -1e30)
                p = jnp.exp(s - lse_ref[hh, rows, :])
                dob = do.astype(BF16)
                dv = dv + lax.dot_general(p.astype(BF16), dob, _TN, preferred_element_type=F32)
                dp = lax.dot_general(dob, v, _NT, preferred_element_type=F32)
                ds = (p * (dp - delta) * scale).astype(BF16)
                dk = dk + lax.dot_general(ds, q, _TN, preferred_element_type=F32)
                dq_ref[rows, cols] += jnp.dot(ds, k, preferred_element_type=F32)
                return dk, dv

            zero = jnp.zeros((tq, LANES), F32)
            carry = block(ki, (zero, zero), True)
            dk, dv = lax.fori_loop(ki + 1, nb, lambda qi, c, block=block: block(qi, c, False), carry)
            dks.append(dk)
            dkvs.append(jnp.where(lane < HEAD, dk, dv))
        dkv_ref[...] = jnp.concatenate(dkvs, axis=1).astype(dkv_ref.dtype)
        dk_ref[...] = jnp.concatenate(dks, axis=1)

    pair_all = pl.BlockSpec((t_rows, 2 * LANES), lambda pr, ki: (0, pr))
    pair_blk = pl.BlockSpec((tq, 2 * LANES), lambda pr, ki: (ki, pr))
    wide = jax.ShapeDtypeStruct((t_rows, N_ATT_HEADS * LANES), F32)
    return _call(
        body, (qr, kf, kvb, dcat, o, lse), name=name, grid=(N_ATT_HEADS // 2, nb),
        in_specs=[pair_all, pair_blk, pair_blk,
                  pl.BlockSpec((t_rows, LANES), lambda pr, ki: (0, do_off + pr)),
                  pl.BlockSpec((t_rows, LANES), lambda pr, ki: (0, pr)),
                  pl.BlockSpec((2, t_rows, 1), lambda pr, ki: (pr, 0, 0))],
        out_specs=[pair_all, pair_blk, pair_blk],
        out_shape=[wide, jax.ShapeDtypeStruct(wide.shape, BF16), wide],
        sem=("parallel", "arbitrary"), job=job)


def _my_pos():
    return lax.axis_index("x"), lax.axis_index("y"), lax.axis_index("c")


def _all_gather(xs, *, name):
    return _gather_forward(_run_job(_gather_job(xs), name=name), name=name + "_forward")


def _remote(src, dst, send_sems, recv_sems, k, dev):
    return pltpu.make_async_remote_copy(src_ref=src, dst_ref=dst, send_sem=send_sems.at[k], recv_sem=recv_sems.at[k],
                                        device_id=dev, device_id_type=MESH)


def _gather_job(xs):
    n = len(xs)

    def make(x_refs, out_refs, send_sems, recv_sems, local_sems):
        mx, my, mc = _my_pos()
        mine = 4 * mx + 2 * my + mc
        peers = [(mx, my, 1 - mc), (1 - mx, my, mc), (mx, 1 - my, mc), (1 - mx, 1 - my, mc)]
        sends, recvs, local = [], [], []
        for a in range(n):
            local.append(pltpu.make_async_copy(x_refs[a], out_refs[a].at[mine], local_sems.at[a]))
            for k, dev in enumerate(peers):
                theirs = 4 * dev[0] + 2 * dev[1] + dev[2]
                sends.append(_remote(x_refs[a], out_refs[a].at[mine], send_sems, recv_sems, 4 * a + k, dev))
                recvs.append(_remote(x_refs[a], out_refs[a].at[theirs], send_sems, recv_sems, 4 * a + k, dev))
        return sends, recvs, local

    shapes = [jax.ShapeDtypeStruct((N_DEV,) + x.shape, x.dtype) for x in xs]
    return _copies_job(xs, shapes, 4 * n, n, make)


def _gather_forward(gs, *, name):
    n = len(gs)

    def body(*refs):
        out_refs = refs[n:2 * n]
        send_sems, recv_sems = refs[2 * n:]
        mx, my, mc = _my_pos()
        chips = [(1 - mx, my), (mx, 1 - my), (1 - mx, 1 - my)]
        sends, recvs = [], []
        for a in range(n):
            for j, (cx, cy) in enumerate(chips):
                here, there = out_refs[a].at[4 * cx + 2 * cy + mc], out_refs[a].at[4 * cx + 2 * cy + 1 - mc]
                sends.append(_remote(here, here, send_sems, recv_sems, 3 * a + j, (mx, my, 1 - mc)))
                recvs.append(_remote(here, there, send_sems, recv_sems, 3 * a + j, (mx, my, 1 - mc)))
        for cp in sends:
            cp.start()
        for cp in recvs:
            cp.wait_recv()
        for cp in sends:
            cp.wait_send()

    return pl.pallas_call(
        body, name=name, out_shape=[jax.ShapeDtypeStruct(g.shape, g.dtype) for g in gs],
        in_specs=[_ANY] * n, out_specs=[_ANY] * n, input_output_aliases={a: a for a in range(n)},
        scratch_shapes=[pltpu.SemaphoreType.DMA((3 * n,)), pltpu.SemaphoreType.DMA((3 * n,))],
    )(*gs)


def _pair_job(xs):
    n = len(xs)

    def make(x_refs, out_refs, send_sems, recv_sems, local_sems):
        mx, my, mc = _my_pos()
        copies = [_remote(x_refs[a].at[g, 1 - mc], out_refs[a].at[g], send_sems, recv_sems, 4 * a + g, (mx, my, 1 - mc))
                  for a in range(n) for g in range(4)]
        return copies, copies, []

    shapes = [jax.ShapeDtypeStruct((4,) + x.shape[2:], x.dtype) for x in xs]
    return _copies_job(xs, shapes, 4 * n, 0, make)


def _pair_add(x, r, core, *, name):
    _, _, a, b = x.shape
    ta = _row_tile(a, 256)

    def body(c_ref, x_ref, r_ref, o_ref):
        o_ref[...] = x_ref[...] + r_ref[...]

    blk = pl.BlockSpec((None, ta, b), lambda g, i, c_ref: (g, i, 0))
    return pl.pallas_call(
        body, name=name,
        grid_spec=pltpu.PrefetchScalarGridSpec(
            num_scalar_prefetch=1, grid=(4, a // ta),
            in_specs=[pl.BlockSpec((None, None, ta, b), lambda g, i, c_ref: (g, c_ref[0], i, 0)), blk],
            out_specs=blk),
        out_shape=jax.ShapeDtypeStruct((4, a, b), x.dtype),
        compiler_params=_cparams(("parallel", "parallel")),
    )(core, x, r)


def _quad_job(xs):
    n = len(xs)

    def make(x_refs, out_refs, send_sems, recv_sems, local_sems):
        mx, my, mc = _my_pos()
        mine = 2 * mx + my
        peers = [((1 - mx, my, mc), 2 * (1 - mx) + my), ((mx, 1 - my, mc), 2 * mx + 1 - my),
                 ((1 - mx, 1 - my, mc), 2 * (1 - mx) + 1 - my)]
        sends, recvs, local = [], [], []
        for a in range(n):
            local.append(pltpu.make_async_copy(x_refs[a].at[mine], out_refs[a].at[mine], local_sems.at[a]))
            for k, (dev, g) in enumerate(peers):
                sends.append(_remote(x_refs[a].at[g], out_refs[a].at[mine], send_sems, recv_sems, 3 * a + k, dev))
                recvs.append(_remote(x_refs[a].at[g], out_refs[a].at[g], send_sems, recv_sems, 3 * a + k, dev))
        return sends, recvs, local

    shapes = [jax.ShapeDtypeStruct(x.shape, x.dtype) for x in xs]
    return _copies_job(xs, shapes, 3 * n, n, make)


def _row_tile(r, pref):
    t = min(pref, r)
    while r % t or (t % 8 and t != r):
        t -= 1
    return t


def _adamw(parts, w, m, v, layer, *, name, tile=256):
    g, a, b = parts.shape
    tile = _row_tile(a, tile)
    c1 = 1.0 / (1.0 - ADAM_B1 ** ADAM_STEP)
    c2 = 1.0 / (1.0 - ADAM_B2 ** ADAM_STEP)

    def body(p_ref, w_ref, m_ref, v_ref, g_ref, d_ref, mo_ref, vo_ref):
        grad = p_ref[0]
        for j in range(1, g):
            grad = grad + p_ref[j]
        mn = ADAM_B1 * m_ref[...] + (1.0 - ADAM_B1) * grad
        vn = ADAM_B2 * v_ref[...] + (1.0 - ADAM_B2) * (grad * grad)
        g_ref[...] = grad
        mo_ref[...] = mn
        vo_ref[...] = vn
        d_ref[...] = -ADAM_LR * ((mn * c1) / (jnp.sqrt(vn * c2) + ADAM_EPS) + ADAM_WD * w_ref[...])

    slab = pl.BlockSpec((tile, b), lambda i: (i, 0))
    src = slab if layer is None else pl.BlockSpec((None, tile, b), lambda i: (layer, i, 0))
    return pl.pallas_call(
        body, name=name, grid=(a // tile,),
        in_specs=[pl.BlockSpec((g, tile, b), lambda i: (0, i, 0)), src, src, src],
        out_specs=[slab] * 4,
        out_shape=[jax.ShapeDtypeStruct((a, b), F32)] * 4,
        compiler_params=_cparams(("parallel",)),
    )(parts, w, m, v)


W_IN_SHARD = 276


def _w_in_dest(col):
    return jnp.where(col < P_KR, col, jnp.where(col < P_KR + 256, col + (P_CKV - P_KR), col - 2176 + P_KR + HEAD))


def _place_w_in(g, *, name):
    _, d, sh = g.shape
    tc = 768

    def body(g_ref, o_ref, acc_ref):
        ct, j = pl.program_id(0), pl.program_id(1)

        @pl.when(j == 0)
        def _():
            acc_ref[...] = jnp.zeros_like(acc_ref)

        src = j * sh + lax.broadcasted_iota(jnp.int32, (sh, tc), 0)
        dst = ct * tc + lax.broadcasted_iota(jnp.int32, (sh, tc), 1)
        place = (_w_in_dest(src) == dst).astype(BF16)
        acc_ref[...] += jnp.dot(g_ref[...], place, preferred_element_type=F32)

        @pl.when(j == N_DEV - 1)
        def _():
            o_ref[...] = acc_ref[...].astype(o_ref.dtype)

    return pl.pallas_call(
        body, name=name, grid=(P_COLS // tc, N_DEV),
        in_specs=[pl.BlockSpec((None, d, sh), lambda ct, j: (j, 0, 0))],
        out_specs=pl.BlockSpec((d, tc), lambda ct, j: (0, ct)),
        out_shape=jax.ShapeDtypeStruct((d, P_COLS), BF16),
        scratch_shapes=[pltpu.VMEM((d, tc), F32)],
        compiler_params=_cparams(("parallel", "arbitrary")),
    )(g)


def _unplace_w_in(dw, *, name):
    d = dw.shape[0]
    sh = W_IN_SHARD

    def body(dw_ref, o_ref):
        j = pl.program_id(0)
        src = j * sh + lax.broadcasted_iota(jnp.int32, (P_COLS, sh), 1)
        dst = lax.broadcasted_iota(jnp.int32, (P_COLS, sh), 0)
        pick = (_w_in_dest(src) == dst).astype(BF16)
        x = dw_ref[...]
        o_ref[...] = _split_dot(x, pick)

    return pl.pallas_call(
        body, name=name, grid=(N_DEV,),
        in_specs=[pl.BlockSpec((d, P_COLS), lambda j: (0, 0))],
        out_specs=pl.BlockSpec((None, d, sh), lambda j: (j, 0, 0)),
        out_shape=jax.ShapeDtypeStruct((N_DEV, d, sh), F32),
        compiler_params=_cparams(("arbitrary",)),
    )(dw)


def _gate_up_swiglu(h1, wgu, *, name):
    t_rows, k = h1.shape
    w = wgu.shape[2]
    tm = _tile(t_rows, 1024)

    def body(a_ref, wg_ref, wu_ref, gu_ref, act_ref):
        a = a_ref[...].astype(BF16)
        gate = jnp.dot(a, wg_ref[...], preferred_element_type=F32)
        up = jnp.dot(a, wu_ref[...], preferred_element_type=F32)
        gu_ref[0] = gate.astype(gu_ref.dtype)
        gu_ref[1] = up.astype(gu_ref.dtype)
        act_ref[...] = (gate * _sigmoid(gate) * up).astype(act_ref.dtype)

    return pl.pallas_call(
        body, name=name, grid=(t_rows // tm, 4),
        in_specs=[pl.BlockSpec((tm, k), lambda i, j: (i, 0)),
                  pl.BlockSpec((None, k, w), lambda i, j: (j, 0, 0)),
                  pl.BlockSpec((None, k, w), lambda i, j: (j + 4, 0, 0))],
        out_specs=[pl.BlockSpec((2, None, tm, w), lambda i, j: (0, j, i, 0)),
                   pl.BlockSpec((None, tm, w), lambda i, j: (j, i, 0))],
        out_shape=[jax.ShapeDtypeStruct((2, 4, t_rows, w), BF16), jax.ShapeDtypeStruct((4, t_rows, w), BF16)],
        compiler_params=_cparams(("parallel", "arbitrary")),
    )(h1, wgu, wgu)


def _down_dx_swiglu(dffn, wdown, gu, *, name):
    t_rows, k = dffn.shape
    w = gu.shape[3]
    tm = _tile(t_rows, 1024)

    def body(d_ref, w_ref, gu_ref, o_ref):
        dact = lax.dot_general(d_ref[...].astype(BF16), w_ref[...], _NT, preferred_element_type=F32)
        gate, up = gu_ref[0].astype(F32), gu_ref[1].astype(F32)
        sg = _sigmoid(gate)
        o_ref[0] = (dact * up * (sg * (1.0 + gate * (1.0 - sg)))).astype(o_ref.dtype)
        o_ref[1] = (dact * gate * sg).astype(o_ref.dtype)

    blk = pl.BlockSpec((2, None, tm, w), lambda i, j: (0, j, i, 0))
    return pl.pallas_call(
        body, name=name, grid=(t_rows // tm, 4),
        in_specs=[pl.BlockSpec((tm, k), lambda i, j: (i, 0)), pl.BlockSpec((w, k), lambda i, j: (j, 0)), blk],
        out_specs=blk, out_shape=jax.ShapeDtypeStruct(gu.shape, BF16),
        compiler_params=_cparams(("parallel", "arbitrary")),
    )(dffn, wdown, gu)


BIG = ("w_in", "mla_w_uq", "mla_w_ukv", "w_out", "w_gate_up", "w_down", "ple_w_gate", "ple_w_proj")
SMALL = ("ln_in_g", "ln_in_b", "hgrn_lb_logits", "hgrn_norm_g", "sgu_ln_g", "sgu_ln_b", "sgu_w_s", "sgu_b_s",
         "mla_q_norm_g", "mla_kv_norm_g", "ln1_g", "ln1_b", "ln2_g", "ln2_b")
ORDER = ("ln_in_g", "ln_in_b", "w_in", "hgrn_lb_logits", "hgrn_norm_g", "sgu_ln_g", "sgu_ln_b", "sgu_w_s", "sgu_b_s",
         "mla_q_norm_g", "mla_w_uq", "mla_kv_norm_g", "mla_w_ukv", "w_out", "ln1_g", "ln1_b", "w_gate_up", "w_down",
         "ple_w_gate", "ple_w_proj", "ln2_g", "ln2_b")


def _slab(a, align):
    s = a.reshape(-1, LANES)
    pad = -s.shape[0] % align
    return jnp.pad(s, ((0, pad), (0, 0))) if pad else s


def _pack(arrays, align=16, total_align=512):
    s = jnp.concatenate([_slab(a, align) for a in arrays], axis=0)
    pad = -s.shape[0] % total_align
    return jnp.pad(s, ((0, pad), (0, 0))) if pad else s


def _unpack(slab, shapes, align=16):
    out, r0 = [], 0
    for s in shapes:
        nr = math.prod(s) // LANES
        out.append(slab[r0:r0 + nr].reshape(s))
        r0 += nr + (-nr % align)
    return out


def _blocks_to_cols(g, *, name):
    nb, a, b = g.shape

    def body(g_ref, o_ref):
        o_ref[...] = g_ref[...]

    return pl.pallas_call(
        body, name=name, grid=(nb,), in_specs=[pl.BlockSpec((None, a, b), lambda j: (j, 0, 0))],
        out_specs=pl.BlockSpec((a, b), lambda j: (0, j)), out_shape=jax.ShapeDtypeStruct((a, nb * b), g.dtype),
        compiler_params=_cparams(("parallel",)),
    )(g)


def _cols_to_blocks(x, *, name):
    a, b = x.shape[0], x.shape[1] // N_DEV

    def body(x_ref, o_ref):
        o_ref[...] = x_ref[...]

    return pl.pallas_call(
        body, name=name, grid=(N_DEV,), in_specs=[pl.BlockSpec((a, b), lambda j: (0, j))],
        out_specs=pl.BlockSpec((None, a, b), lambda j: (j, 0, 0)), out_shape=jax.ShapeDtypeStruct((N_DEV, a, b), x.dtype),
        compiler_params=_cparams(("parallel",)),
    )(x)


def _weight_shards(w, li):
    uq_pad = ((0, 0), (0, LANES - ATT_D))
    shards = {k: w[k][li] for k in BIG}
    shards["mla_w_uq"] = jnp.pad(shards["mla_w_uq"], uq_pad)
    return {k: s.astype(BF16) for k, s in shards.items()}


def _usable_weights(g, *, name):
    out = {}
    for k, a in g.items():
        if k == "w_in":
            out[k] = _place_w_in(a, name=name + "_place_w_in")
        elif k in ("w_out", "w_down", "ple_w_gate"):
            out[k] = a.reshape(a.shape[0] * a.shape[1], a.shape[2])
        elif k == "w_gate_up":
            out[k] = a
        else:
            out[k] = _blocks_to_cols(a, name=name + "_cols_" + k)
    return out


def _as_pairs(g):
    if g.ndim == 2:
        return g.reshape((4, 2, g.shape[0] // N_DEV) + g.shape[1:])
    return g.reshape((4, 2) + g.shape[1:])


def _twice(fn):
    return lambda *a: fn(*a) * 2


def _layer_forward(li, h, hb, p_i, wts, sm, lbs, tables, alpha, hgrn_job=None, more_weights=None, attn_job=None):
    n = f"l{li}_"
    row1 = lambda a: a.reshape(1, -1)
    projp = _mm(hb, wts["w_in"], name=n + "proj")
    ng = row1(sm["hgrn_norm_g"][li])
    res = _hgrn_fwd(projp, lbs[li], ng, name=n + "hgrn_fwd", job=hgrn_job)
    if hgrn_job is not None:
        res, got = res
        wts = dict(wts, **more_weights(got))
    o_a, o_pre, states = res
    lg, lbias = row1(sm["sgu_ln_g"][li]), row1(sm["sgu_ln_b"][li])
    w_s = sm["sgu_w_s"][li]
    bias_full = jnp.repeat(sm["sgu_b_s"][li].T, HEAD, axis=1)
    o_b = _sgu_fwd(projp, lg, lbias, w_s, bias_full, name=n + "sgu_fwd")
    qg, kvg = row1(sm["mla_q_norm_g"][li]), row1(sm["mla_kv_norm_g"][li])
    cq_view, ckv_view = (projp, 384, P_CQ // 384), (projp, 256, P_CKV // 256)
    (cqn,) = _rowwise(_fn_rms, [cq_view], [qg], [(384, BF16)], name=n + "q_norm")
    (ckvn,) = _rowwise(_fn_rms, [ckv_view], [kvg], [(256, BF16)], name=n + "kv_norm")
    q = _mm(cqn, wts["mla_w_uq"], name=n + "uq")
    kv = _mm(ckvn, wts["mla_w_ukv"], name=n + "ukv")
    qr, kf, kvb = _mla_prep(q, kv, projp, tables, name=n + "mla_prep")
    res, attn_got = _attn_fwd(qr, kf, kvb, name=n + "attn_fwd", job=attn_job), None
    if attn_job is not None:
        res, attn_got = res
    o_c, lse = res
    cat = jnp.concatenate([o_a, o_b, o_c.astype(BF16)], axis=1)
    mix = _mm(cat, wts["w_out"], name=n + "out_proj")
    g1, b1 = row1(sm["ln1_g"][li]), row1(sm["ln1_b"][li])
    d = h.shape[1]
    h1, h1b = _rowwise(_twice(_make_post_mix(alpha)), [h, mix], [g1, b1], [(d, F32), (d, BF16)], name=n + "ln1")
    gu, act = _gate_up_swiglu(h1b, wts["w_gate_up"], name=n + "gate_up")
    ffn = _mm(act, wts["w_down"], am="bmk", name=n + "down")
    pg = _mm(h1b, wts["ple_w_gate"], name=n + "ple_gate")
    pp = _mm(p_i, wts["ple_w_proj"], name=n + "ple_proj")
    g2, b2 = row1(sm["ln2_g"][li]), row1(sm["ln2_b"][li])
    h2, h2b = _rowwise(_twice(_make_ple_ln(alpha)), [h1, ffn, pg, pp], [g2, b2], [(d, F32), (d, BF16)],
                       name=n + "ln2")
    saved = dict(h=h, hb=hb, h1b=h1b, projp=projp, o_pre=o_pre, states=states, cqn=cqn, ckvn=ckvn, qr=qr, kf=kf, kvb=kvb, o_c=o_c,
                 lse=lse, cat=cat, mix=mix, h1=h1, gu=gu, act=act, ffn=ffn, pg=pg, pp=pp, ng=ng, lg=lg, wts=wts,
                 lbias=lbias, w_s=w_s, bias_full=bias_full, qg=qg, kvg=kvg, g1=g1, b1=b1, g2=g2, b2=b2)
    return (h2, h2b), saved, attn_got


RS_EARLY = ("ple_w_proj", "ple_w_gate", "w_down", "w_gate_up", "w_out")
RS_LATE = ("mla_w_uq", "mla_w_ukv", "w_in")


def _layer_backward(li, dh2_parts, p_i, sv, lbs, tables, alpha, core, carried=None):
    n = f"l{li}_b_"
    wts = sv["wts"]
    gr = {}
    dh1_a, dffn, dpg, dpp, gr["ln2_g"], gr["ln2_b"] = _rowwise_vjp(
        _make_ple_ln(alpha), [sv["h1"], sv["ffn"], sv["pg"], sv["pp"]], [sv["g2"], sv["b2"]], [dh2_parts],
        groups=[[0], [1], [2], [3]], gdtypes=[F32, BF16, BF16, BF16], name=n + "ln2")
    big = {}
    big["ple_w_proj"] = _cols_to_blocks(_mm(p_i, dpp, am="km", name=n + "ple_proj_dw"), name=n + "ple_proj_dw_blocks")
    big["ple_w_gate"] = _mm(sv["h1b"], dpg, am="km", name=n + "ple_gate_dw")
    dh1_b = _mm(dpg, wts["ple_w_gate"], bm="nk", name=n + "ple_gate_dx")
    big["w_down"] = _mm(sv["act"], dffn, am="bkm", name=n + "down_dw")
    dgu = _down_dx_swiglu(dffn, wts["w_down"], sv["gu"], name=n + "down_dx")
    dgu = dgu.reshape((N_DEV,) + dgu.shape[2:])
    big["w_gate_up"], carried_got = _mm(sv["h1b"], dgu, am="km", bm="bkn", om="bmn", name=n + "gate_up_dw",
                                        job=carried), None
    if carried is not None:
        big["w_gate_up"], carried_got = big["w_gate_up"]
    dh1_c = _mm(dgu, wts["w_gate_up"], am="bmk", bm="bnk", name=n + "gate_up_dx")
    dh_a, dmix, gr["ln1_g"], gr["ln1_b"] = _rowwise_vjp(
        _make_post_mix(alpha), [sv["h"], sv["mix"]], [sv["g1"], sv["b1"]], [[dh1_a, dh1_b, dh1_c]],
        groups=[[0], [1]], gdtypes=[F32, BF16], name=n + "ln1")
    big["w_out"] = _mm(sv["cat"], dmix, am="km", name=n + "out_proj_dw")
    early = [_as_pairs(big[k]) for k in RS_EARLY]
    dcat, theirs = _mm(dmix, wts["w_out"], bm="nk", name=n + "out_proj_dx", job=_pair_job(early))
    sums = [_pair_add(x, r, core, name=n + "pair_add_" + k) for k, x, r in zip(RS_EARLY, early, theirs)]

    (dqr, dkv, dkf), early_quads = _attn_bwd(sv["qr"], sv["kf"], sv["kvb"], dcat, sv["o_c"], sv["lse"],
                                             name=n + "attn", job=_quad_job(sums))
    dqpad, dkr = _mla_prep_bwd(dqr, dkf, tables, name=n + "mla_prep")
    big["mla_w_uq"] = _cols_to_blocks(_mm(sv["cqn"], dqpad, am="km", name=n + "uq_dw"), name=n + "uq_dw_blocks")
    dcqn = _mm(dqpad, wts["mla_w_uq"], bm="nk", name=n + "uq_dx")
    big["mla_w_ukv"] = _cols_to_blocks(_mm(sv["ckvn"], dkv, am="km", name=n + "ukv_dw"), name=n + "ukv_dw_blocks")
    dckvn = _mm(dkv, wts["mla_w_ukv"], bm="nk", name=n + "ukv_dx")
    projp = sv["projp"]
    dcq, gr["mla_q_norm_g"] = _rowwise_vjp(_fn_rms, [(projp, 384, P_CQ // 384)], [sv["qg"]], [[dcqn]],
                                           groups=[[0]], gdtypes=[BF16], name=n + "q_norm")
    dckv, gr["mla_kv_norm_g"] = _rowwise_vjp(_fn_rms, [(projp, 256, P_CKV // 256)], [sv["kvg"]], [[dckvn]],
                                             groups=[[0]], gdtypes=[BF16], name=n + "kv_norm")
    dsgu, gr["sgu_ln_g"], gr["sgu_ln_b"], gr["sgu_w_s"], gr["sgu_b_s"] = _sgu_bwd(
        projp, sv["lg"], sv["lbias"], sv["w_s"], sv["bias_full"], dcat, name=n + "sgu")
    dhg, gr["hgrn_norm_g"], gr["lower_bound"] = _hgrn_bwd(
        projp, lbs[li], sv["ng"], sv["o_pre"], sv["states"], dcat, name=n + "hgrn")
    dprojp = jnp.concatenate([dhg, dsgu, dcq, dkr, dckv], axis=1)
    big["w_in"] = _unplace_w_in(_mm(sv["hb"], dprojp, am="km", name=n + "proj_dw"), name=n + "proj_dw_shards")
    late = [_as_pairs(big[k]) for k in RS_LATE]
    dh_b, theirs = _mm(dprojp, wts["w_in"], bm="nk", name=n + "proj_dx", job=_pair_job(late))
    late_sums = [_pair_add(x, r, core, name=n + "pair_add_" + k) for k, x, r in zip(RS_LATE, late, theirs)]
    return [dh_a, dh_b], gr, early_quads, late_sums, carried_got


def kernel(x, p, positions, ln_in_g, ln_in_b, w_in, hgrn_lb_logits, hgrn_norm_g, sgu_ln_g, sgu_ln_b, sgu_w_s, sgu_b_s, mla_q_norm_g, mla_w_uq, mla_kv_norm_g, mla_w_ukv, w_out, ln1_g, ln1_b, w_gate_up, w_down, ple_w_gate, ple_w_proj, ln2_g, ln2_b, loss_target, m_ln_in_g, m_ln_in_b, m_w_in, m_hgrn_lb_logits, m_hgrn_norm_g, m_sgu_ln_g, m_sgu_ln_b, m_sgu_w_s, m_sgu_b_s, m_mla_q_norm_g, m_mla_w_uq, m_mla_kv_norm_g, m_mla_w_ukv, m_w_out, m_ln1_g, m_ln1_b, m_w_gate_up, m_w_down, m_ple_w_gate, m_ple_w_proj, m_ln2_g, m_ln2_b, v_ln_in_g, v_ln_in_b, v_w_in, v_hgrn_lb_logits, v_hgrn_norm_g, v_sgu_ln_g, v_sgu_ln_b, v_sgu_w_s, v_sgu_b_s, v_mla_q_norm_g, v_mla_w_uq, v_mla_kv_norm_g, v_mla_w_ukv, v_w_out, v_ln1_g, v_ln1_b, v_w_gate_up, v_w_down, v_ple_w_gate, v_ple_w_proj, v_ln2_g, v_ln2_b):
    args = dict(locals())
    w = {k: args[k] for k in ORDER}
    m = {k: args["m_" + k] for k in ORDER}
    v = {k: args["v_" + k] for k in ORDER}
    depth = w_in.shape[0]
    assert depth == 2, "the lower-bound kernel is written for two layers"
    alpha = (2 * depth) ** 0.25
    xs, tgt = x[0], loss_target[0]
    d_model = xs.shape[1]

    shards = [_weight_shards(w, li) for li in range(depth)]
    rest = [k for k in BIG if k != "w_in"]
    (g_in,) = _all_gather([shards[0]["w_in"]], name="gather_l0_w_in")
    w_in0 = _usable_weights({"w_in": g_in}, name="l0")

    def rest_of_layer0(got):
        got = _gather_forward(got, name="gather_l0_forward")
        return _usable_weights(dict(zip(rest, got)), name="l0")

    tables = _rope_tables(positions[0])
    row1 = lambda a: a.reshape(1, -1)
    l0, l1 = row1(hgrn_lb_logits[0]), row1(hgrn_lb_logits[1])
    lbs = _rowwise(_fn_lower_bounds, [l0, l1], [], [(HG_W, F32), (HG_W, F32)], name="lower_bounds")

    gin, bin_ = row1(ln_in_g), row1(ln_in_b)
    h, hb = _rowwise(_twice(_fn_ln), [xs], [gin, bin_], [(d_model, F32), (d_model, BF16)], name="ln_in")
    (h, hb), sv0, got1 = _layer_forward(0, h, hb, p[0, 0], w_in0, w, lbs, tables, alpha,
                                  hgrn_job=_gather_job([shards[0][k] for k in rest]), more_weights=rest_of_layer0,
                                  attn_job=_gather_job([shards[1][k] for k in BIG]))
    wts1 = _usable_weights(dict(zip(BIG, _gather_forward(got1, name="gather_l1_forward"))), name="l1")
    (h, _), sv1, _ = _layer_forward(1, h, hb, p[1, 0], wts1, w, lbs, tables, alpha)
    saved = [sv0, sv1]
    dy, loss_local = _loss_and_grad(h, tgt, name="loss")
    loss = lax.psum(loss_local[0, 0], ("x", "y", "c"))

    core = lax.axis_index("c").astype(jnp.int32).reshape(1)
    dparts, grads, quads, carried = [dy], [None] * depth, [None] * depth, None
    for li in reversed(range(depth)):
        dparts, grads[li], early_quads, late_sums, late_quads = _layer_backward(
            li, dparts, p[li, 0], saved[li], lbs, tables, alpha, core, carried=carried)
        quads[li] = dict(zip(RS_EARLY, early_quads))
        if carried is not None:
            quads[li + 1].update(zip(RS_LATE, late_quads))
        carried = _quad_job(late_sums)
    quads[0].update(zip(RS_LATE, _run_job(carried, name="rs_l0_late_quad")))
    dx, d_gin, d_bin = _rowwise_vjp(_fn_ln, [xs], [gin, bin_], [dparts], groups=[[0]], name="ln_in_b")
    dl0, dl1 = _rowwise_vjp(_fn_lower_bounds, [l0, l1], [], [[grads[0]["lower_bound"]], [grads[1]["lower_bound"]]],
                            groups=[[0], [1]], name="lower_bounds_b")

    prefixes = ("grad_", "delta_", "new_m_", "new_v_")
    per_layer = {pre + k: [] for pre in prefixes for k in BIG}
    uq_pad = ((0, 0), (0, 0), (0, LANES - ATT_D))
    state = {k: ((jnp.pad(w[k], uq_pad), jnp.pad(m[k], uq_pad), jnp.pad(v[k], uq_pad)) if k == "mla_w_uq"
                 else (w[k], m[k], v[k])) for k in BIG}
    for li in range(depth):
        for k in BIG:
            res4 = _adamw(quads[li][k], *state[k], li, name=f"adamw_l{li}_{k}")
            for pre, a in zip(prefixes, res4):
                per_layer[pre + k].append(a[:, :ATT_D] if k == "mla_w_uq" else a)
    out = {name: jnp.stack(vals) for name, vals in per_layer.items()}

    small_g = {"ln_in_g": d_gin.reshape(-1), "ln_in_b": d_bin.reshape(-1),
               "hgrn_lb_logits": jnp.stack([dl0.reshape(-1), dl1.reshape(-1)])}
    for k in SMALL[3:]:
        small_g[k] = jnp.stack([grads[li][k].reshape(w[k].shape[1:]) for li in range(depth)])
    (small_parts,) = _all_gather([_pack([small_g[k] for k in SMALL])], name="gather_small_grads")
    slabs = _adamw(small_parts, _pack([w[k] for k in SMALL]), _pack([m[k] for k in SMALL]),
                   _pack([v[k] for k in SMALL]), None, name="adamw_small")
    shapes = [w[k].shape for k in SMALL]
    for pre, slab in zip(prefixes, slabs):
        for k, a in zip(SMALL, _unpack(slab, shapes)):
            out[pre + k] = a
    res = [loss, dx[None]]
    for prefix in ("grad_", "delta_", "new_m_", "new_v_"):
        res += [out[prefix + k] for k in ORDER]
    return tuple(res)
```

```python
import functools
import math

import jax
import jax.numpy as jnp
from jax import lax
from jax.experimental import pallas as pl
from jax.experimental.pallas import tpu as pltpu

F32 = jnp.float32
BF16 = jnp.bfloat16
MESH = pl.DeviceIdType.MESH

LN_EPS = 1e-5
RMS_EPS = 1e-6
ROPE_THETA = 10000.0
ADAM_LR, ADAM_B1, ADAM_B2, ADAM_EPS, ADAM_WD, ADAM_STEP = 0.001, 0.9, 0.999, 1e-08, 0.01, 10

N_DEV = 8
LANES = 128
HG_CHUNK = 16
HG_W = 256
HEAD = 64
SGU_CHUNK = 128
N_ATT_HEADS = 8
ATT_D = 96
VMEM_LIMIT = 56 * 1024 * 1024

HG_TILE = 256
ATT_TQ = 512
ROW_TILE = 256

P_CQ, P_KR, P_CKV, P_COLS = 1536, 1920, 2048, 2304


def _cparams(sem):
    return pltpu.CompilerParams(dimension_semantics=sem, vmem_limit_bytes=VMEM_LIMIT)


_ANY = pl.BlockSpec(memory_space=pl.ANY)


def _call(body, operands, *, name, grid, in_specs, out_specs, out_shape, sem, scratch_shapes=(), job=None):
    if job is None:
        return pl.pallas_call(body, name=name, grid=grid, in_specs=in_specs, out_specs=out_specs, out_shape=out_shape,
                              scratch_shapes=list(scratch_shapes), compiler_params=_cparams(sem))(*operands)
    single = not isinstance(out_shape, (list, tuple))
    shapes = [out_shape] if single else list(out_shape)
    ospecs = [out_specs] if single else list(out_specs)
    ni, no, ns = len(operands), len(shapes), len(scratch_shapes)
    ji, jo = len(job.inputs), len(job.out_shapes)

    def hosted(*refs):
        p = 0
        parts = []
        for cnt in (ni, ji, no, jo, ns):
            parts.append(refs[p:p + cnt])
            p += cnt
        ins, jins, outs, jouts, scr = parts
        jsems = refs[p:]
        ids = [pl.program_id(a) for a in range(len(grid))]
        first = functools.reduce(lambda a, b: a & b, [i == 0 for i in ids])
        last = functools.reduce(lambda a, b: a & b, [i == g - 1 for i, g in zip(ids, grid)])

        @pl.when(first)
        def _():
            job.start(jins, jouts, jsems)

        body(*ins, *outs, *scr)

        @pl.when(last)
        def _():
            job.finish(jins, jouts, jsems)

    res = pl.pallas_call(
        hosted, name=name, grid=grid,
        in_specs=list(in_specs) + [_ANY] * ji, out_specs=ospecs + [_ANY] * jo,
        out_shape=shapes + list(job.out_shapes),
        scratch_shapes=list(scratch_shapes) + [pltpu.SemaphoreType.DMA((c,)) for c in job.sem_counts],
        compiler_params=_cparams(("arbitrary",) * len(grid)),
    )(*operands, *job.inputs)
    own = res[0] if single else res[:no]
    return own, res[no:]


class _Job:
    def __init__(self, inputs, out_shapes, sem_counts, start, finish):
        self.inputs, self.out_shapes, self.sem_counts = list(inputs), list(out_shapes), list(sem_counts)
        self.start, self.finish = start, finish


def _copies_job(inputs, out_shapes, n_remote, n_local, make):
    def start(jins, jouts, sems):
        sends, _, local = make(jins, jouts, *sems)
        for cp in local + sends:
            cp.start()

    def finish(jins, jouts, sems):
        sends, recvs, local = make(jins, jouts, *sems)
        for cp in recvs:
            cp.wait_recv()
        for cp in sends:
            cp.wait_send()
        for cp in local:
            cp.wait()

    return _Job(inputs, out_shapes, [n_remote, n_remote, max(n_local, 1)], start, finish)


def _run_job(job, *, name):
    ji, jo = len(job.inputs), len(job.out_shapes)

    def body(*refs):
        jins, jouts, sems = refs[:ji], refs[ji:ji + jo], refs[ji + jo:]
        job.start(jins, jouts, sems)
        job.finish(jins, jouts, sems)

    return pl.pallas_call(
        body, name=name, out_shape=list(job.out_shapes), in_specs=[_ANY] * ji, out_specs=[_ANY] * jo,
        scratch_shapes=[pltpu.SemaphoreType.DMA((c,)) for c in job.sem_counts],
    )(*job.inputs)


def _tile(n, pref):
    if n % pref == 0:
        return pref
    best = None
    t = LANES
    while t <= min(n, pref):
        if n % t == 0:
            best = t
        t += LANES
    return best if best is not None else n


def _mm(a, b, *, am="mk", bm="kn", om="mn", out_dtype=F32, tm=1024, tn=1024, tk=1024, name, job=None):
    if am == "mk":
        m, k = a.shape
    elif am == "km":
        k, m = a.shape
    elif am == "bmk":
        m, tk = a.shape[1], a.shape[2]
        k = a.shape[0] * tk
    else:
        k, tm = a.shape[1], a.shape[2]
        m = a.shape[0] * tm
    if bm == "kn":
        kb_, n = b.shape
    elif bm == "nk":
        n, kb_ = b.shape
    elif bm == "bkn":
        kb_, tn = b.shape[1], b.shape[2]
        n = b.shape[0] * tn
    else:
        n, tk = b.shape[1], b.shape[2]
        kb_ = b.shape[0] * tk
    assert kb_ == k, (a.shape, b.shape, am, bm)
    tm, tn, tk = _tile(m, tm), _tile(n, tn), _tile(k, tk)
    nk = k // tk
    dims = (((0 if am in ("km", "bkm") else 1,), (1 if bm in ("nk", "bnk") else 0,)), ((), ()))

    a_spec = {"mk": pl.BlockSpec((tm, tk), lambda i, j, kk: (i, kk)),
              "km": pl.BlockSpec((tk, tm), lambda i, j, kk: (kk, i)),
              "bmk": pl.BlockSpec((None, tm, tk), lambda i, j, kk: (kk, i, 0)),
              "bkm": pl.BlockSpec((None, tk, tm), lambda i, j, kk: (i, kk, 0))}[am]
    b_spec = {"kn": pl.BlockSpec((tk, tn), lambda i, j, kk: (kk, j)),
              "nk": pl.BlockSpec((tn, tk), lambda i, j, kk: (j, kk)),
              "bkn": pl.BlockSpec((None, tk, tn), lambda i, j, kk: (j, kk, 0)),
              "bnk": pl.BlockSpec((None, tn, tk), lambda i, j, kk: (kk, j, 0))}[bm]
    if om == "mn":
        o_spec, o_shape = pl.BlockSpec((tm, tn), lambda i, j, kk: (i, j)), (m, n)
    else:
        o_spec, o_shape = pl.BlockSpec((None, tm, tn), lambda i, j, kk: (j, i, 0)), (n // tn, m, tn)

    def body(a_ref, b_ref, o_ref, *acc):
        kk = pl.program_id(2)
        prod = lax.dot_general(a_ref[...].astype(BF16), b_ref[...].astype(BF16), dims, preferred_element_type=F32)
        if nk == 1:
            o_ref[...] = prod.astype(o_ref.dtype)
            return
        acc_ref, = acc

        @pl.when(kk == 0)
        def _():
            acc_ref[...] = prod

        if nk > 2:
            @pl.when((kk > 0) & (kk < nk - 1))
            def _():
                acc_ref[...] += prod

        @pl.when(kk == nk - 1)
        def _():
            o_ref[...] = (acc_ref[...] + prod).astype(o_ref.dtype)

    return _call(body, (a, b), name=name, grid=(m // tm, n // tn, nk), in_specs=[a_spec, b_spec], out_specs=o_spec,
                 out_shape=jax.ShapeDtypeStruct(o_shape, out_dtype),
                 scratch_shapes=[pltpu.VMEM((tm, tn), F32)] if nk > 1 else [],
                 sem=("parallel", "parallel", "arbitrary"), job=job)


def _row_operand(a, tile):
    if isinstance(a, tuple):
        arr, w, j = a
        return arr, pl.BlockSpec((tile, w), lambda i, j=j: (i, j))
    return a, pl.BlockSpec((tile, a.shape[1]), lambda i: (i, 0))


def _const_spec(c):
    nd = c.ndim
    return pl.BlockSpec(c.shape, lambda i, nd=nd: (0,) * nd)


def _rowwise(fn, rows, consts, outs, *, name, accs=(), tile=None):
    t_rows = (rows[0][0] if isinstance(rows[0], tuple) else rows[0]).shape[0]
    tile = min(tile or ROW_TILE, t_rows)
    arrs, specs = zip(*[_row_operand(a, tile) for a in rows])
    nin, no = len(rows) + len(consts), len(outs)

    def body(*refs):
        res = fn(*[r[...] for r in refs[:nin]])
        for r, v in zip(refs[nin:nin + no], res[:no]):
            r[...] = v.astype(r.dtype)
        if accs:
            a_refs = refs[nin + no:]

            @pl.when(pl.program_id(0) == 0)
            def _():
                for r in a_refs:
                    r[...] = jnp.zeros_like(r)

            for r, v in zip(a_refs, res[no:]):
                r[...] += v

    out_shape = [jax.ShapeDtypeStruct((t_rows, w), dt) for w, dt in outs]
    out_shape += [jax.ShapeDtypeStruct(s, F32) for s in accs]
    out_specs = [pl.BlockSpec((tile, w), lambda i: (i, 0)) for w, _ in outs]
    out_specs += [pl.BlockSpec(s, lambda i, nd=len(s): (0,) * nd) for s in accs]
    return pl.pallas_call(
        body, name=name, grid=(t_rows // tile,),
        in_specs=list(specs) + [_const_spec(c) for c in consts],
        out_specs=out_specs, out_shape=out_shape,
        compiler_params=_cparams(("arbitrary",)),
    )(*arrs, *consts)


def _rowwise_vjp(fn, rows, consts, cts, *, name, groups, tile=None, gdtypes=None):
    t_rows = (rows[0][0] if isinstance(rows[0], tuple) else rows[0]).shape[0]
    tile = min(tile or ROW_TILE, t_rows)
    arrs, specs = zip(*[_row_operand(a, tile) for a in rows])
    flat_cts = [c for group in cts for c in group]
    ct_arrs, ct_specs = zip(*[_row_operand(a, tile) for a in flat_cts])
    nr, nc, nct, ng = len(rows), len(consts), len(flat_cts), len(groups)

    def width(a):
        return a[1] if isinstance(a, tuple) else a.shape[1]

    def body(*refs):
        rv = [r[...].astype(F32) for r in refs[:nr]]
        cv = [r[...] for r in refs[nr:nr + nc]]
        ct_refs = refs[nr + nc:nr + nc + nct]
        ctv, pos = [], 0
        for group in cts:
            s = ct_refs[pos][...].astype(F32)
            for r in ct_refs[pos + 1:pos + len(group)]:
                s = s + r[...].astype(F32)
            ctv.append(s)
            pos += len(group)
        _, pull = jax.vjp(fn, *rv, *cv)
        grads = pull(tuple(ctv))
        g_refs = refs[nr + nc + nct:nr + nc + nct + ng]
        for r, idx in zip(g_refs, groups):
            parts = [grads[i] for i in idx]
            r[...] = (parts[0] if len(parts) == 1 else jnp.concatenate(parts, axis=1)).astype(r.dtype)
        c_refs = refs[nr + nc + nct + ng:]

        @pl.when(pl.program_id(0) == 0)
        def _():
            for r in c_refs:
                r[...] = jnp.zeros_like(r)

        for r, v in zip(c_refs, grads[nr:]):
            r[...] += v

    gw = [sum(width(rows[i]) for i in idx) for idx in groups]
    gdtypes = gdtypes or [F32] * ng
    out_shape = [jax.ShapeDtypeStruct((t_rows, w), dt) for w, dt in zip(gw, gdtypes)]
    out_shape += [jax.ShapeDtypeStruct(c.shape, F32) for c in consts]
    out_specs = [pl.BlockSpec((tile, w), lambda i: (i, 0)) for w in gw]
    out_specs += [_const_spec(c) for c in consts]
    return pl.pallas_call(
        body, name=name, grid=(t_rows // tile,),
        in_specs=list(specs) + [_const_spec(c) for c in consts] + list(ct_specs),
        out_specs=out_specs, out_shape=out_shape,
        compiler_params=_cparams(("arbitrary",)),
    )(*arrs, *consts, *ct_arrs)


def _layer_norm(x, g, b):
    mu = jnp.mean(x, axis=-1, keepdims=True)
    xc = x - mu
    var = jnp.mean(xc * xc, axis=-1, keepdims=True)
    return xc * lax.rsqrt(var + LN_EPS) * g + b


def _sigmoid(x):
    return 1.0 / (1.0 + jnp.exp(-x))


def _fn_ln(x, g, b):
    return (_layer_norm(x, g, b),)


def _fn_rms(x, g):
    return (x * lax.rsqrt(jnp.mean(x * x, axis=-1, keepdims=True) + RMS_EPS) * g,)


def _make_post_mix(alpha):
    def fn(h, mix, g, b):
        return (_layer_norm(alpha * h + mix, g, b),)
    return fn


def _make_ple_ln(alpha):
    def fn(h1, ffn, pg, pp, g, b):
        return (_layer_norm(alpha * h1 + ffn + _sigmoid(pg) * pp, g, b),)
    return fn


def _fn_lower_bounds(l0, l1):
    m = jnp.maximum(l0, l1)
    e0, e1 = jnp.exp(l0 - m), jnp.exp(l1 - m)
    s = e0 + e1
    p0, p1 = e0 / s, e1 / s
    return (p0 - p0, (p0 + p1) - p0)


def _loss_and_grad(y, target, *, name):
    d = y.shape[1]

    def fn(yv, tv):
        err = yv - tv
        return err * (1.0 / d), 0.5 * jnp.sum(jnp.mean(err * err, axis=-1, keepdims=True), axis=0, keepdims=True)

    return _rowwise(fn, [y, target], [], [(d, F32)], accs=[(1, 1)], name=name)


def _split_dot(x, e_bf16):
    hi = x.astype(BF16)
    lo = (x - hi.astype(F32)).astype(BF16)
    return (jnp.dot(hi, e_bf16, preferred_element_type=F32) + jnp.dot(lo, e_bf16, preferred_element_type=F32))


def _hgrn_common(th):
    rm = lax.broadcasted_iota(jnp.int32, (th, HG_W), 0) % HG_CHUNK

    def seg_cumsum(x):
        for s in (1, 2, 4, 8):
            x = x + jnp.where(rm >= s, pltpu.roll(x, s, 0), 0.0)
        return x

    def seg_rcumsum(x):
        for s in (1, 2, 4, 8):
            x = x + jnp.where(rm < HG_CHUNK - s, pltpu.roll(x, th - s, 0), 0.0)
        return x

    ri = lax.broadcasted_iota(jnp.int32, (HG_W, HG_W), 0) // HEAD
    ci = lax.broadcasted_iota(jnp.int32, (HG_W, HG_W), 1) // HEAD
    head_f32 = (ri == ci).astype(F32)
    head_bf16 = head_f32.astype(BF16)

    def headsum(x):
        return _split_dot(x, head_bf16)

    return rm, seg_cumsum, seg_rcumsum, head_f32, headsum


def _hgrn_gates(qr, fl, lb):
    sg = _sigmoid(fl)
    f = lb + (1.0 - lb) * sg
    sq = _sigmoid(qr)
    return sg, f, jnp.log(f), 1.0 - f, qr * sq, sq


def _shifted(x, d, th):
    return x if d == 0 else pltpu.roll(x, d, 0)


def _unshift(x, d, th):
    return x if d == 0 else pltpu.roll(x, th - d, 0)


def _hgrn_fwd(projp, lb, ng, *, name, job=None):
    t_rows = projp.shape[0]
    th = min(HG_TILE, t_rows)
    nct = th // HG_CHUNK

    def body(q_ref, f_ref, i_ref, g_ref, lb_ref, ng_ref, oa_ref, opre_ref, st_out_ref,
             st_ref, vtm_ref, kv_ref, qe_ref, dec_ref, oint_ref):
        rm, seg_cumsum, seg_rcumsum, head_f32, headsum = _hgrn_common(th)

        @pl.when(pl.program_id(0) == 0)
        def _():
            st_ref[...] = jnp.zeros_like(st_ref)

        qr, fl, v, g = q_ref[...], f_ref[...], i_ref[...], g_ref[...]
        _, f, lf, k, q, _ = _hgrn_gates(qr, fl, lb_ref[...])
        b = seg_cumsum(lf)

        o = jnp.zeros((th, HG_W), F32)
        for d in range(HG_CHUNK):
            kd, bd, vd = _shifted(k, d, th), _shifted(b, d, th), _shifted(v, d, th)
            e = jnp.exp(jnp.where(rm >= d, b - bd, -1e30))
            o = o + headsum(q * kd * e) * vd

        blast = seg_rcumsum(jnp.where(rm == HG_CHUNK - 1, b, 0.0))
        kte = (k * jnp.exp(blast - b)).astype(BF16)
        qe_ref[...] = q * jnp.exp(b)
        dec_ref[...] = jnp.exp(blast)
        vt = v.T
        lane_chunk = lax.broadcasted_iota(jnp.int32, (HG_W, th), 1) // HG_CHUNK
        for c in range(nct):
            vtm_ref[c * HG_W:(c + 1) * HG_W, :] = jnp.where(lane_chunk == c, vt, 0.0).astype(BF16)
        kv_ref[...] = jnp.dot(vtm_ref[...], kte, preferred_element_type=F32)

        def step(c, carry):
            r0 = pl.multiple_of(c * HG_CHUNK, HG_CHUNK)
            s = st_ref[...]
            st_out_ref[c] = s
            oint_ref[pl.ds(r0, HG_CHUNK), :] = lax.dot_general(
                qe_ref[pl.ds(r0, HG_CHUNK), :].astype(BF16), s.astype(BF16),
                (((1,), (1,)), ((), ())), preferred_element_type=F32)
            dec = jnp.max(dec_ref[pl.ds(r0, HG_CHUNK), :], axis=0, keepdims=True)
            kv_c = kv_ref[pl.ds(pl.multiple_of(c * HG_W, HG_W), HG_W), :]
            st_ref[...] = s * dec + kv_c * head_f32
            return carry

        lax.fori_loop(0, nct, step, 0)

        o = o + oint_ref[...]
        opre_ref[...] = o
        r = lax.rsqrt(headsum(o * o) * (1.0 / HEAD) + RMS_EPS)
        oa_ref[...] = (o * r * ng_ref[...] * (g * _sigmoid(g))).astype(oa_ref.dtype)

    col = lambda j: pl.BlockSpec((th, HG_W), lambda i, j=j: (i, j))
    vec = pl.BlockSpec((1, HG_W), lambda i: (0, 0))
    row = pl.BlockSpec((th, HG_W), lambda i: (i, 0))
    n_chunks = t_rows // HG_CHUNK
    return _call(
        body, (projp, projp, projp, projp, lb, ng), name=name, grid=(t_rows // th,),
        in_specs=[col(0), col(1), col(2), col(3), vec, vec],
        out_specs=[row, row, pl.BlockSpec((nct, HG_W, HG_W), lambda i: (i, 0, 0))],
        out_shape=[jax.ShapeDtypeStruct((t_rows, HG_W), BF16), jax.ShapeDtypeStruct((t_rows, HG_W), F32),
                   jax.ShapeDtypeStruct((n_chunks, HG_W, HG_W), F32)],
        scratch_shapes=[pltpu.VMEM((HG_W, HG_W), F32), pltpu.VMEM((nct * HG_W, th), BF16),
                        pltpu.VMEM((nct * HG_W, HG_W), F32), pltpu.VMEM((th, HG_W), F32),
                        pltpu.VMEM((th, HG_W), F32), pltpu.VMEM((th, HG_W), F32)],
        sem=("arbitrary",), job=job)


def _hgrn_bwd(projp, lb, ng, opre, states, dcat, *, name):
    t_rows = projp.shape[0]
    th = min(HG_TILE, t_rows)
    nct = th // HG_CHUNK
    nt = t_rows // th

    def body(q_ref, f_ref, i_ref, g_ref, lb_ref, ng_ref, opre_ref, st_in_ref, do_ref,
             dproj_ref, dng_ref, dlb_ref,
             gst_ref, dotm_ref, qg_ref, v_ref, kte_ref, dop_ref, dec_ref, dkte_ref, dvi_ref, dqe_ref, ddec_ref):
        rm, seg_cumsum, seg_rcumsum, head_f32, headsum = _hgrn_common(th)

        @pl.when(pl.program_id(0) == 0)
        def _():
            gst_ref[...] = jnp.zeros_like(gst_ref)
            dng_ref[...] = jnp.zeros_like(dng_ref)
            dlb_ref[...] = jnp.zeros_like(dlb_ref)

        qr, fl, v, g = q_ref[...], f_ref[...], i_ref[...], g_ref[...]
        lb, ngv = lb_ref[...], ng_ref[...]
        sg, f, lf, k, q, sq = _hgrn_gates(qr, fl, lb)
        b = seg_cumsum(lf)
        blast = seg_rcumsum(jnp.where(rm == HG_CHUNK - 1, b, 0.0))
        eb = jnp.exp(b)
        ekb = jnp.exp(blast - b)
        qe, kte, dec = q * eb, k * ekb, jnp.exp(blast)

        do_out, op = do_ref[...], opre_ref[...]
        sgg = _sigmoid(g)
        sil = g * sgg
        r = lax.rsqrt(headsum(op * op) * (1.0 / HEAD) + RMS_EPS)
        on = op * r
        dng_ref[...] += jnp.sum(do_out * on * sil, axis=0, keepdims=True)
        dg = do_out * on * ngv * (sgg * (1.0 + g * (1.0 - sgg)))
        don = do_out * ngv * sil
        dop = r * (don - on * (headsum(don * on) * (1.0 / HEAD)))

        v_ref[...] = v
        kte_ref[...] = kte
        dop_ref[...] = dop
        dec_ref[...] = dec
        dot_t = dop.T
        lane_chunk = lax.broadcasted_iota(jnp.int32, (HG_W, th), 1) // HG_CHUNK
        for c in range(nct):
            dotm_ref[c * HG_W:(c + 1) * HG_W, :] = jnp.where(lane_chunk == c, dot_t, 0.0).astype(BF16)
        qg_ref[...] = jnp.dot(dotm_ref[...], qe.astype(BF16), preferred_element_type=F32)

        def step(j, carry):
            c = nct - 1 - j
            r0 = pl.multiple_of(c * HG_CHUNK, HG_CHUNK)
            rows = pl.ds(r0, HG_CHUNK)
            gs = gst_ref[...]
            s = st_in_ref[c]
            gm = (gs * head_f32).astype(BF16)
            dkte_ref[rows, :] = jnp.dot(v_ref[rows, :].astype(BF16), gm, preferred_element_type=F32)
            dvi_ref[rows, :] = lax.dot_general(kte_ref[rows, :].astype(BF16), gm, (((1,), (1,)), ((), ())),
                                               preferred_element_type=F32)
            dqe_ref[rows, :] = jnp.dot(dop_ref[rows, :].astype(BF16), s.astype(BF16), preferred_element_type=F32)
            ddec_ref[rows, :] = jnp.broadcast_to(jnp.sum(gs * s, axis=0, keepdims=True), (HG_CHUNK, HG_W))
            dec_c = jnp.max(dec_ref[rows, :], axis=0, keepdims=True)
            qg_c = qg_ref[pl.ds(pl.multiple_of(c * HG_W, HG_W), HG_W), :]
            gst_ref[...] = gs * dec_c + qg_c * head_f32
            return carry

        lax.fori_loop(0, nct, step, 0)

        dkte, dqe = dkte_ref[...], dqe_ref[...]
        dq = dqe * eb
        dk = dkte * ekb
        db = dqe * qe - dkte * kte
        dv = dvi_ref[...]
        dblast = dkte * kte + jnp.where(rm == HG_CHUNK - 1, ddec_ref[...] * dec, 0.0)

        for d in range(HG_CHUNK):
            kd, bd, vd = _shifted(k, d, th), _shifted(b, d, th), _shifted(v, d, th)
            e = jnp.exp(jnp.where(rm >= d, b - bd, -1e30))
            p = q * kd * e
            sc = headsum(p)
            dsc = headsum(dop * vd)
            dv = dv + _unshift(sc * dop, d, th)
            dq = dq + dsc * kd * e
            dk = dk + _unshift(dsc * q * e, d, th)
            darg = dsc * p
            db = db + darg - _unshift(darg, d, th)

        db = db + jnp.where(rm == HG_CHUNK - 1, seg_cumsum(dblast), 0.0)
        dlf = seg_rcumsum(db)
        df = dlf / f - dk
        dlb_ref[...] += jnp.sum(df * (1.0 - sg), axis=0, keepdims=True)
        dfl = df * (1.0 - lb) * sg * (1.0 - sg)
        dqr = dq * (sq * (1.0 + qr * (1.0 - sq)))
        dproj_ref[...] = jnp.concatenate([dqr, dfl, dv, dg], axis=1).astype(dproj_ref.dtype)

    rev = lambda i: nt - 1 - i
    col = lambda j: pl.BlockSpec((th, HG_W), lambda i, j=j: (rev(i), j))
    vec = pl.BlockSpec((1, HG_W), lambda i: (0, 0))
    row = pl.BlockSpec((th, HG_W), lambda i: (rev(i), 0))
    tile_f32 = pltpu.VMEM((th, HG_W), F32)
    return pl.pallas_call(
        body, name=name, grid=(nt,),
        in_specs=[col(0), col(1), col(2), col(3), vec, vec, row,
                  pl.BlockSpec((nct, HG_W, HG_W), lambda i: (rev(i), 0, 0)), col(0)],
        out_specs=[pl.BlockSpec((th, 4 * HG_W), lambda i: (rev(i), 0)), vec, vec],
        out_shape=[jax.ShapeDtypeStruct((t_rows, 4 * HG_W), BF16), jax.ShapeDtypeStruct((1, HG_W), F32),
                   jax.ShapeDtypeStruct((1, HG_W), F32)],
        scratch_shapes=[pltpu.VMEM((HG_W, HG_W), F32), pltpu.VMEM((nct * HG_W, th), BF16),
                        pltpu.VMEM((nct * HG_W, HG_W), F32)] + [tile_f32] * 8,
        compiler_params=_cparams(("arbitrary",)),
    )(projp, projp, projp, projp, lb, ng, opre, states, dcat)


_INV_SQRT2 = 1.0 / math.sqrt(2.0)
_INV_SQRT2PI = 1.0 / math.sqrt(2.0 * math.pi)


def _gelu(x):
    return 0.5 * x * (1.0 + lax.erf(x * _INV_SQRT2))


def _gelu_grad(x):
    return 0.5 * (1.0 + lax.erf(x * _INV_SQRT2)) + x * jnp.exp(-0.5 * x * x) * _INV_SQRT2PI


def _sgu_parts(bu, bv, lg, lbias, w_ref, n_groups):
    c = SGU_CHUNK
    tril = (lax.broadcasted_iota(jnp.int32, (c, c), 0) >= lax.broadcasted_iota(jnp.int32, (c, c), 1)).astype(F32)
    gid = lax.broadcasted_iota(jnp.int32, bu.shape, 1) // HEAD
    u = _gelu(bu)
    gv = _gelu(bv)
    mu = jnp.mean(gv, axis=-1, keepdims=True)
    xc = gv - mu
    rstd = lax.rsqrt(jnp.mean(xc * xc, axis=-1, keepdims=True) + LN_EPS)
    xhat = xc * rstd
    vn = xhat * lg + lbias
    ws = [w_ref[gi] * tril for gi in range(n_groups)]
    return tril, gid, u, rstd, xhat, vn, ws


def _sgu_fwd(projp, lg, lbias, w_s, bias_full, *, name):
    t_rows = projp.shape[0]
    n_groups = w_s.shape[0]
    c = SGU_CHUNK

    def body(u_ref, v_ref, lg_ref, lb_ref, w_ref, bias_ref, o_ref):
        _, gid, u, _, _, vn, ws = _sgu_parts(u_ref[...], v_ref[...], lg_ref[...], lb_ref[...], w_ref, n_groups)
        vnb = vn.astype(BF16)
        z = bias_ref[...]
        for gi in range(n_groups):
            z = z + jnp.where(gid == gi, jnp.dot(ws[gi].astype(BF16), vnb, preferred_element_type=F32), 0.0)
        o_ref[...] = (u * z).astype(o_ref.dtype)

    col = lambda j: pl.BlockSpec((c, HG_W), lambda i, j=j: (i, j))
    return pl.pallas_call(
        body, name=name, grid=(t_rows // c,),
        in_specs=[col(4), col(5), _const_spec(lg), _const_spec(lbias), _const_spec(w_s), _const_spec(bias_full)],
        out_specs=pl.BlockSpec((c, HG_W), lambda i: (i, 0)),
        out_shape=jax.ShapeDtypeStruct((t_rows, HG_W), BF16),
        compiler_params=_cparams(("arbitrary",)),
    )(projp, projp, lg, lbias, w_s, bias_full)


def _sgu_bwd(projp, lg, lbias, w_s, bias_full, dcat, *, name):
    t_rows = projp.shape[0]
    n_groups = w_s.shape[0]
    c = SGU_CHUNK
    n = t_rows // c

    def body(u_ref, v_ref, lg_ref, lb_ref, w_ref, bias_ref, do_ref,
             dproj_ref, dlg_ref, dlb_ref, dw_ref, dbs_ref, dbias_acc):
        i = pl.program_id(0)

        @pl.when(i == 0)
        def _():
            dlg_ref[...] = jnp.zeros_like(dlg_ref)
            dlb_ref[...] = jnp.zeros_like(dlb_ref)
            dw_ref[...] = jnp.zeros_like(dw_ref)
            dbias_acc[...] = jnp.zeros_like(dbias_acc)

        bu, bv, lg_v = u_ref[...], v_ref[...], lg_ref[...]
        tril, gid, u, rstd, xhat, vn, ws = _sgu_parts(bu, bv, lg_v, lb_ref[...], w_ref, n_groups)
        vnb = vn.astype(BF16)
        z = bias_ref[...]
        for gi in range(n_groups):
            z = z + jnp.where(gid == gi, jnp.dot(ws[gi].astype(BF16), vnb, preferred_element_type=F32), 0.0)
        do = do_ref[...]
        dbu = do * z * _gelu_grad(bu)
        dz = do * u
        dbias_acc[...] += dz
        dvn = jnp.zeros_like(dz)
        for gi in range(n_groups):
            dzg = jnp.where(gid == gi, dz, 0.0).astype(BF16)
            dw_ref[gi] += lax.dot_general(dzg, vnb, (((1,), (1,)), ((), ())), preferred_element_type=F32) * tril
            dvn = dvn + jnp.dot(ws[gi].T.astype(BF16), dzg, preferred_element_type=F32)
        dlg_ref[...] += jnp.sum(dvn * xhat, axis=0, keepdims=True)
        dlb_ref[...] += jnp.sum(dvn, axis=0, keepdims=True)
        dxh = dvn * lg_v
        dgv = rstd * (dxh - jnp.mean(dxh, axis=-1, keepdims=True)
                      - xhat * jnp.mean(dxh * xhat, axis=-1, keepdims=True))
        dproj_ref[...] = jnp.concatenate([dbu, dgv * _gelu_grad(bv)], axis=1).astype(dproj_ref.dtype)

        @pl.when(i == n - 1)
        def _():
            dbs_ref[...] = jnp.sum(dbias_acc[...].T.reshape(n_groups, HEAD, c), axis=1)

    col = lambda j: pl.BlockSpec((c, HG_W), lambda i, j=j: (i, j))
    return pl.pallas_call(
        body, name=name, grid=(n,),
        in_specs=[col(4), col(5), _const_spec(lg), _const_spec(lbias), _const_spec(w_s), _const_spec(bias_full),
                  col(1)],
        out_specs=[pl.BlockSpec((c, 2 * HG_W), lambda i: (i, 0)), _const_spec(lg), _const_spec(lbias),
                   _const_spec(w_s), pl.BlockSpec((n_groups, c), lambda i: (0, 0))],
        out_shape=[jax.ShapeDtypeStruct((t_rows, 2 * HG_W), BF16), jax.ShapeDtypeStruct(lg.shape, F32),
                   jax.ShapeDtypeStruct(lbias.shape, F32), jax.ShapeDtypeStruct(w_s.shape, F32),
                   jax.ShapeDtypeStruct((n_groups, c), F32)],
        scratch_shapes=[pltpu.VMEM((c, HG_W), F32)],
        compiler_params=_cparams(("arbitrary",)),
    )(projp, projp, lg, lbias, w_s, bias_full, dcat)


def _rope_tables(positions):
    t = positions.shape[0]
    inv_freq = ROPE_THETA ** (-jnp.arange(0, 32, 2, dtype=F32) / 32)
    ang = positions.astype(F32)[:, None] * inv_freq
    cos, sin = jnp.cos(ang), jnp.sin(ang)
    z = lambda w: jnp.zeros((t, w), F32)
    cos_t = jnp.concatenate([jnp.ones((t, 64), F32), cos, cos, z(32)], axis=1)
    sin_up = jnp.concatenate([z(80), sin, z(32)], axis=1)
    sin_dn = jnp.concatenate([z(64), -sin, z(48)], axis=1)
    return cos_t, sin_up, sin_dn


def _rep(x, n):
    return x if n == 1 else jnp.concatenate([x] * n, axis=1)


def _rope(x, cos_t, sin_up, sin_dn):
    w = x.shape[1]
    return x * cos_t + pltpu.roll(x, 16, 1) * sin_up + pltpu.roll(x, w - 16, 1) * sin_dn


def _rope_t(dy, cos_t, sin_up, sin_dn):
    w = dy.shape[1]
    return dy * cos_t + pltpu.roll(dy * sin_up, w - 16, 1) + pltpu.roll(dy * sin_dn, 16, 1)


def _mla_prep(q, kv, projp, tables, *, name):
    nh = N_ATT_HEADS

    def fn(qv, kvv, kr, cos_t, sin_up, sin_dn):
        qr = _rope(qv, _rep(cos_t, nh), _rep(sin_up, nh), _rep(sin_dn, nh))
        krr = _rope(kr, cos_t, sin_up, sin_dn)
        lane = lax.broadcasted_iota(jnp.int32, kvv.shape, 1) % LANES
        return qr, jnp.where(lane < HEAD, kvv, 0.0) + _rep(krr, nh), kvv

    w = q.shape[1]
    return _rowwise(fn, [q, kv, (projp, LANES, P_KR // LANES)] + list(tables), [],
                    [(w, BF16), (w, BF16), (w, BF16)], name=name)


def _mla_prep_bwd(dqr, dkf, tables, *, name):
    nh = N_ATT_HEADS

    def fn(dq, dk, cos_t, sin_up, sin_dn):
        dqp = _rope_t(dq, _rep(cos_t, nh), _rep(sin_up, nh), _rep(sin_dn, nh))
        dkrr = dk[:, 0:LANES]
        for h in range(1, nh):
            dkrr = dkrr + dk[:, LANES * h:LANES * (h + 1)]
        return dqp, _rope_t(dkrr, cos_t, sin_up, sin_dn)

    return _rowwise(fn, [dqr, dkf] + list(tables), [], [(dqr.shape[1], BF16), (LANES, BF16)], name=name)


_LOG2E = 1.0 / math.log(2.0)
_NT = (((1,), (1,)), ((), ()))
_TN = (((0,), (0,)), ((), ()))


def _attn_fwd(qr, kf, kvb, *, name, job=None):
    t_rows = qr.shape[0]
    tq = min(ATT_TQ, t_rows)
    nb = t_rows // tq
    scale = ATT_D ** -0.5

    c2 = scale * _LOG2E

    def body(q_ref, kf_ref, kvb_ref, o_ref, lse_ref):
        qi = pl.program_id(1)
        lane = lax.broadcasted_iota(jnp.int32, (tq, LANES), 1)
        causal_t = (lax.broadcasted_iota(jnp.int32, (tq, tq), 0) <= lax.broadcasted_iota(jnp.int32, (tq, tq), 1))
        outs = []
        for hh in range(2):
            cols = slice(hh * LANES, (hh + 1) * LANES)
            q = q_ref[:, cols]

            def block(ki, carry, diagonal, q=q, cols=cols):
                m_old, l_old, acc_t = carry
                rows = pl.ds(pl.multiple_of(ki * tq, tq), tq)
                s_t = lax.dot_general(kf_ref[rows, cols], q, _NT, preferred_element_type=F32)
                if diagonal:
                    s_t = jnp.where(causal_t, s_t, -1e30)
                m_new = jnp.maximum(m_old, jnp.max(s_t, axis=0, keepdims=True))
                p_t = jnp.exp2((s_t - m_new) * c2)
                a = jnp.exp2((m_old - m_new) * c2)
                pv_t = lax.dot_general(kvb_ref[rows, cols], p_t.astype(BF16), _TN, preferred_element_type=F32)
                return m_new, a * l_old + jnp.sum(p_t, axis=0, keepdims=True), a * acc_t + pv_t

            init = (jnp.full((1, tq), -1e30, F32), jnp.zeros((1, tq), F32), jnp.zeros((LANES, tq), F32))
            carry = lax.fori_loop(0, qi, lambda ki, c, block=block: block(ki, c, False), init)
            m_fin, l_fin, acc_t = block(qi, carry, True)
            lse_ref[hh] = m_fin * scale + jnp.log(l_fin)
            outs.append((acc_t / l_fin).T)
        o_ref[...] = jnp.where(lane < HEAD, pltpu.roll(outs[0], HEAD, 1), outs[1])

    pair = pl.BlockSpec((t_rows, 2 * LANES), lambda pr, qi: (0, pr))
    return _call(
        body, (qr, kf, kvb), name=name, grid=(N_ATT_HEADS // 2, nb),
        in_specs=[pl.BlockSpec((tq, 2 * LANES), lambda pr, qi: (qi, pr)), pair, pair],
        out_specs=[pl.BlockSpec((tq, LANES), lambda pr, qi: (qi, pr)),
                   pl.BlockSpec((2, 1, tq), lambda pr, qi: (pr, 0, qi))],
        out_shape=[jax.ShapeDtypeStruct((t_rows, N_ATT_HEADS * HEAD), F32),
                   jax.ShapeDtypeStruct((N_ATT_HEADS, 1, t_rows), F32)],
        sem=("parallel", "arbitrary"), job=job)


def _attn_bwd(qr, kf, kvb, dcat, o, lse, *, name, job=None):
    t_rows = qr.shape[0]
    tq = min(ATT_TQ, t_rows)
    nb = t_rows // tq
    scale = ATT_D ** -0.5
    c2 = scale * _LOG2E
    do_off = 2 * HG_W // LANES

    def body(q_ref, kf_ref, kvb_ref, do_ref, o_ref, lse_ref, dq_ref, dkv_ref, dk_ref):
        ki = pl.program_id(1)

        @pl.when(ki == 0)
        def _():
            dq_ref[...] = jnp.zeros_like(dq_ref)

        lane = lax.broadcasted_iota(jnp.int32, (tq, LANES), 1)
        causal_t = (lax.broadcasted_iota(jnp.int32, (tq, tq), 0) <= lax.broadcasted_iota(jnp.int32, (tq, tq), 1))
        dkvs, dks = [], []
        for hh in range(2):
            cols = slice(hh * LANES, (hh + 1) * LANES)
            k, v = kf_ref[:, cols], kvb_ref[:, cols]

            def block(qi, carry, diagonal, hh=hh, cols=cols, k=k, v=v):
                dk, dv = carry
                rows = pl.ds(pl.multiple_of(qi * tq, tq), tq)
                q = q_ref[rows, cols]
                do, ov = do_ref[rows, :], o_ref[rows, :]
                if hh == 0:
                    do, ov = pltpu.roll(do, HEAD, 1), pltpu.roll(ov, HEAD, 1)
                do = jnp.where(lane >= HEAD, do, 0.0)
                delta = jnp.sum((do * ov).T, axis=0, keepdims=True)
                s_t = lax.dot_general(k, q, _NT, preferred_element_type=F32)
                if diagonal:
                    s_t = jnp.where(causal_t, s_t, -1e30)
                p_t = jnp.exp2(s_t * c2 - lse_ref[hh, :, rows] * _LOG2E)
                dob = do.astype(BF16)
                dv = dv + jnp.dot(p_t.astype(BF16), dob, preferred_element_type=F32)
                dp_t = lax.dot_general(v, dob, _NT, preferred_element_type=F32)
                ds_t = (p_t * (dp_t - delta) * scale).astype(BF16)
                dk = dk + jnp.dot(ds_t, q, preferred_element_type=F32)
                dq_ref[rows, cols] += lax.dot_general(ds_t, k, _TN, preferred_element_type=F32)
                return dk, dv

            zero = jnp.zeros((tq, LANES), F32)
            carry = block(ki, (zero, zero), True)
            dk, dv = lax.fori_loop(ki + 1, nb, lambda qi, c, block=block: block(qi, c, False), carry)
            dks.append(dk)
            dkvs.append(jnp.where(lane < HEAD, dk, dv))
        dkv_ref[...] = jnp.concatenate(dkvs, axis=1).astype(dkv_ref.dtype)
        dk_ref[...] = jnp.concatenate(dks, axis=1)

    pair_all = pl.BlockSpec((t_rows, 2 * LANES), lambda pr, ki: (0, pr))
    pair_blk = pl.BlockSpec((tq, 2 * LANES), lambda pr, ki: (ki, pr))
    wide = jax.ShapeDtypeStruct((t_rows, N_ATT_HEADS * LANES), F32)
    return _call(
        body, (qr, kf, kvb, dcat, o, lse), name=name, grid=(N_ATT_HEADS // 2, nb),
        in_specs=[pair_all, pair_blk, pair_blk,
                  pl.BlockSpec((t_rows, LANES), lambda pr, ki: (0, do_off + pr)),
                  pl.BlockSpec((t_rows, LANES), lambda pr, ki: (0, pr)),
                  pl.BlockSpec((2, 1, t_rows), lambda pr, ki: (pr, 0, 0))],
        out_specs=[pair_all, pair_blk, pair_blk],
        out_shape=[wide, jax.ShapeDtypeStruct(wide.shape, BF16), wide],
        sem=("parallel", "arbitrary"), job=job)


def _my_pos():
    return lax.axis_index("x"), lax.axis_index("y"), lax.axis_index("c")


def _all_gather(xs, *, name):
    return _gather_forward(_run_job(_gather_job(xs), name=name), name=name + "_forward")


def _remote(src, dst, send_sems, recv_sems, k, dev):
    return pltpu.make_async_remote_copy(src_ref=src, dst_ref=dst, send_sem=send_sems.at[k], recv_sem=recv_sems.at[k],
                                        device_id=dev, device_id_type=MESH)


def _gather_job(xs):
    n = len(xs)

    def make(x_refs, out_refs, send_sems, recv_sems, local_sems):
        mx, my, mc = _my_pos()
        mine = 4 * mx + 2 * my + mc
        peers = [(mx, my, 1 - mc), (1 - mx, my, mc), (mx, 1 - my, mc), (1 - mx, 1 - my, mc)]
        sends, recvs, local = [], [], []
        for a in range(n):
            local.append(pltpu.make_async_copy(x_refs[a], out_refs[a].at[mine], local_sems.at[a]))
            for k, dev in enumerate(peers):
                theirs = 4 * dev[0] + 2 * dev[1] + dev[2]
                sends.append(_remote(x_refs[a], out_refs[a].at[mine], send_sems, recv_sems, 4 * a + k, dev))
                recvs.append(_remote(x_refs[a], out_refs[a].at[theirs], send_sems, recv_sems, 4 * a + k, dev))
        return sends, recvs, local

    shapes = [jax.ShapeDtypeStruct((N_DEV,) + x.shape, x.dtype) for x in xs]
    return _copies_job(xs, shapes, 4 * n, n, make)


def _gather_forward(gs, *, name):
    n = len(gs)

    def body(*refs):
        out_refs = refs[n:2 * n]
        send_sems, recv_sems = refs[2 * n:]
        mx, my, mc = _my_pos()
        chips = [(1 - mx, my), (mx, 1 - my), (1 - mx, 1 - my)]
        sends, recvs = [], []
        for a in range(n):
            for j, (cx, cy) in enumerate(chips):
                here, there = out_refs[a].at[4 * cx + 2 * cy + mc], out_refs[a].at[4 * cx + 2 * cy + 1 - mc]
                sends.append(_remote(here, here, send_sems, recv_sems, 3 * a + j, (mx, my, 1 - mc)))
                recvs.append(_remote(here, there, send_sems, recv_sems, 3 * a + j, (mx, my, 1 - mc)))
        for cp in sends:
            cp.start()
        for cp in recvs:
            cp.wait_recv()
        for cp in sends:
            cp.wait_send()

    return pl.pallas_call(
        body, name=name, out_shape=[jax.ShapeDtypeStruct(g.shape, g.dtype) for g in gs],
        in_specs=[_ANY] * n, out_specs=[_ANY] * n, input_output_aliases={a: a for a in range(n)},
        scratch_shapes=[pltpu.SemaphoreType.DMA((3 * n,)), pltpu.SemaphoreType.DMA((3 * n,))],
    )(*gs)


def _pair_job(xs):
    n = len(xs)

    def make(x_refs, out_refs, send_sems, recv_sems, local_sems):
        mx, my, mc = _my_pos()
        copies = [_remote(x_refs[a].at[g, 1 - mc], out_refs[a].at[g], send_sems, recv_sems, 4 * a + g, (mx, my, 1 - mc))
                  for a in range(n) for g in range(4)]
        return copies, copies, []

    shapes = [jax.ShapeDtypeStruct((4,) + x.shape[2:], x.dtype) for x in xs]
    return _copies_job(xs, shapes, 4 * n, 0, make)


def _pair_add(x, r, core, *, name):
    _, _, a, b = x.shape
    ta = _row_tile(a, 256)

    def body(c_ref, x_ref, r_ref, o_ref):
        o_ref[...] = x_ref[...] + r_ref[...]

    blk = pl.BlockSpec((None, ta, b), lambda g, i, c_ref: (g, i, 0))
    return pl.pallas_call(
        body, name=name,
        grid_spec=pltpu.PrefetchScalarGridSpec(
            num_scalar_prefetch=1, grid=(4, a // ta),
            in_specs=[pl.BlockSpec((None, None, ta, b), lambda g, i, c_ref: (g, c_ref[0], i, 0)), blk],
            out_specs=blk),
        out_shape=jax.ShapeDtypeStruct((4, a, b), x.dtype),
        compiler_params=_cparams(("parallel", "parallel")),
    )(core, x, r)


def _quad_job(xs):
    n = len(xs)

    def make(x_refs, out_refs, send_sems, recv_sems, local_sems):
        mx, my, mc = _my_pos()
        mine = 2 * mx + my
        peers = [((1 - mx, my, mc), 2 * (1 - mx) + my), ((mx, 1 - my, mc), 2 * mx + 1 - my),
                 ((1 - mx, 1 - my, mc), 2 * (1 - mx) + 1 - my)]
        sends, recvs, local = [], [], []
        for a in range(n):
            local.append(pltpu.make_async_copy(x_refs[a].at[mine], out_refs[a].at[mine], local_sems.at[a]))
            for k, (dev, g) in enumerate(peers):
                sends.append(_remote(x_refs[a].at[g], out_refs[a].at[mine], send_sems, recv_sems, 3 * a + k, dev))
                recvs.append(_remote(x_refs[a].at[g], out_refs[a].at[g], send_sems, recv_sems, 3 * a + k, dev))
        return sends, recvs, local

    shapes = [jax.ShapeDtypeStruct(x.shape, x.dtype) for x in xs]
    return _copies_job(xs, shapes, 3 * n, n, make)


def _row_tile(r, pref):
    t = min(pref, r)
    while r % t or (t % 8 and t != r):
        t -= 1
    return t


def _adamw(parts, w, m, v, layer, *, name, tile=256):
    g, a, b = parts.shape
    tile = _row_tile(a, tile)
    c1 = 1.0 / (1.0 - ADAM_B1 ** ADAM_STEP)
    c2 = 1.0 / (1.0 - ADAM_B2 ** ADAM_STEP)

    def body(p_ref, w_ref, m_ref, v_ref, g_ref, d_ref, mo_ref, vo_ref):
        grad = p_ref[0]
        for j in range(1, g):
            grad = grad + p_ref[j]
        mn = ADAM_B1 * m_ref[...] + (1.0 - ADAM_B1) * grad
        vn = ADAM_B2 * v_ref[...] + (1.0 - ADAM_B2) * (grad * grad)
        g_ref[...] = grad
        mo_ref[...] = mn
        vo_ref[...] = vn
        d_ref[...] = -ADAM_LR * ((mn * c1) / (jnp.sqrt(vn * c2) + ADAM_EPS) + ADAM_WD * w_ref[...])

    slab = pl.BlockSpec((tile, b), lambda i: (i, 0))
    src = slab if layer is None else pl.BlockSpec((None, tile, b), lambda i: (layer, i, 0))
    return pl.pallas_call(
        body, name=name, grid=(a // tile,),
        in_specs=[pl.BlockSpec((g, tile, b), lambda i: (0, i, 0)), src, src, src],
        out_specs=[slab] * 4,
        out_shape=[jax.ShapeDtypeStruct((a, b), F32)] * 4,
        compiler_params=_cparams(("parallel",)),
    )(parts, w, m, v)


W_IN_SHARD = 276


def _w_in_dest(col):
    return jnp.where(col < P_KR, col, jnp.where(col < P_KR + 256, col + (P_CKV - P_KR), col - 2176 + P_KR + HEAD))


def _place_w_in(g, *, name):
    _, d, sh = g.shape
    tc = 768

    def body(g_ref, o_ref, acc_ref):
        ct, j = pl.program_id(0), pl.program_id(1)

        @pl.when(j == 0)
        def _():
            acc_ref[...] = jnp.zeros_like(acc_ref)

        src = j * sh + lax.broadcasted_iota(jnp.int32, (sh, tc), 0)
        dst = ct * tc + lax.broadcasted_iota(jnp.int32, (sh, tc), 1)
        place = (_w_in_dest(src) == dst).astype(BF16)
        acc_ref[...] += jnp.dot(g_ref[...], place, preferred_element_type=F32)

        @pl.when(j == N_DEV - 1)
        def _():
            o_ref[...] = acc_ref[...].astype(o_ref.dtype)

    return pl.pallas_call(
        body, name=name, grid=(P_COLS // tc, N_DEV),
        in_specs=[pl.BlockSpec((None, d, sh), lambda ct, j: (j, 0, 0))],
        out_specs=pl.BlockSpec((d, tc), lambda ct, j: (0, ct)),
        out_shape=jax.ShapeDtypeStruct((d, P_COLS), BF16),
        scratch_shapes=[pltpu.VMEM((d, tc), F32)],
        compiler_params=_cparams(("parallel", "arbitrary")),
    )(g)


def _unplace_w_in(dw, *, name):
    d = dw.shape[0]
    sh = W_IN_SHARD

    def body(dw_ref, o_ref):
        j = pl.program_id(0)
        src = j * sh + lax.broadcasted_iota(jnp.int32, (P_COLS, sh), 1)
        dst = lax.broadcasted_iota(jnp.int32, (P_COLS, sh), 0)
        pick = (_w_in_dest(src) == dst).astype(BF16)
        x = dw_ref[...]
        o_ref[...] = _split_dot(x, pick)

    return pl.pallas_call(
        body, name=name, grid=(N_DEV,),
        in_specs=[pl.BlockSpec((d, P_COLS), lambda j: (0, 0))],
        out_specs=pl.BlockSpec((None, d, sh), lambda j: (j, 0, 0)),
        out_shape=jax.ShapeDtypeStruct((N_DEV, d, sh), F32),
        compiler_params=_cparams(("arbitrary",)),
    )(dw)


def _gate_up_swiglu(h1, wgu, *, name):
    t_rows, k = h1.shape
    w = wgu.shape[2]
    tm = _tile(t_rows, 1024)

    def body(a_ref, wg_ref, wu_ref, gu_ref, act_ref):
        a = a_ref[...].astype(BF16)
        gate = jnp.dot(a, wg_ref[...], preferred_element_type=F32)
        up = jnp.dot(a, wu_ref[...], preferred_element_type=F32)
        gu_ref[0] = gate.astype(gu_ref.dtype)
        gu_ref[1] = up.astype(gu_ref.dtype)
        act_ref[...] = (gate * _sigmoid(gate) * up).astype(act_ref.dtype)

    return pl.pallas_call(
        body, name=name, grid=(t_rows // tm, 4),
        in_specs=[pl.BlockSpec((tm, k), lambda i, j: (i, 0)),
                  pl.BlockSpec((None, k, w), lambda i, j: (j, 0, 0)),
                  pl.BlockSpec((None, k, w), lambda i, j: (j + 4, 0, 0))],
        out_specs=[pl.BlockSpec((2, None, tm, w), lambda i, j: (0, j, i, 0)),
                   pl.BlockSpec((None, tm, w), lambda i, j: (j, i, 0))],
        out_shape=[jax.ShapeDtypeStruct((2, 4, t_rows, w), BF16), jax.ShapeDtypeStruct((4, t_rows, w), BF16)],
        compiler_params=_cparams(("parallel", "arbitrary")),
    )(h1, wgu, wgu)


def _down_dx_swiglu(dffn, wdown, gu, *, name):
    t_rows, k = dffn.shape
    w = gu.shape[3]
    tm = _tile(t_rows, 1024)

    def body(d_ref, w_ref, gu_ref, o_ref):
        dact = lax.dot_general(d_ref[...].astype(BF16), w_ref[...], _NT, preferred_element_type=F32)
        gate, up = gu_ref[0].astype(F32), gu_ref[1].astype(F32)
        sg = _sigmoid(gate)
        o_ref[0] = (dact * up * (sg * (1.0 + gate * (1.0 - sg)))).astype(o_ref.dtype)
        o_ref[1] = (dact * gate * sg).astype(o_ref.dtype)

    blk = pl.BlockSpec((2, None, tm, w), lambda i, j: (0, j, i, 0))
    return pl.pallas_call(
        body, name=name, grid=(t_rows // tm, 4),
        in_specs=[pl.BlockSpec((tm, k), lambda i, j: (i, 0)), pl.BlockSpec((w, k), lambda i, j: (j, 0)), blk],
        out_specs=blk, out_shape=jax.ShapeDtypeStruct(gu.shape, BF16),
        compiler_params=_cparams(("parallel", "arbitrary")),
    )(dffn, wdown, gu)


BIG = ("w_in", "mla_w_uq", "mla_w_ukv", "w_out", "w_gate_up", "w_down", "ple_w_gate", "ple_w_proj")
SMALL = ("ln_in_g", "ln_in_b", "hgrn_lb_logits", "hgrn_norm_g", "sgu_ln_g", "sgu_ln_b", "sgu_w_s", "sgu_b_s",
         "mla_q_norm_g", "mla_kv_norm_g", "ln1_g", "ln1_b", "ln2_g", "ln2_b")
ORDER = ("ln_in_g", "ln_in_b", "w_in", "hgrn_lb_logits", "hgrn_norm_g", "sgu_ln_g", "sgu_ln_b", "sgu_w_s", "sgu_b_s",
         "mla_q_norm_g", "mla_w_uq", "mla_kv_norm_g", "mla_w_ukv", "w_out", "ln1_g", "ln1_b", "w_gate_up", "w_down",
         "ple_w_gate", "ple_w_proj", "ln2_g", "ln2_b")


def _slab(a, align):
    s = a.reshape(-1, LANES)
    pad = -s.shape[0] % align
    return jnp.pad(s, ((0, pad), (0, 0))) if pad else s


def _pack(arrays, align=16, total_align=512):
    s = jnp.concatenate([_slab(a, align) for a in arrays], axis=0)
    pad = -s.shape[0] % total_align
    return jnp.pad(s, ((0, pad), (0, 0))) if pad else s


def _unpack(slab, shapes, align=16):
    out, r0 = [], 0
    for s in shapes:
        nr = math.prod(s) // LANES
        out.append(slab[r0:r0 + nr].reshape(s))
        r0 += nr + (-nr % align)
    return out


def _blocks_to_cols(g, *, name):
    nb, a, b = g.shape

    def body(g_ref, o_ref):
        o_ref[...] = g_ref[...]

    return pl.pallas_call(
        body, name=name, grid=(nb,), in_specs=[pl.BlockSpec((None, a, b), lambda j: (j, 0, 0))],
        out_specs=pl.BlockSpec((a, b), lambda j: (0, j)), out_shape=jax.ShapeDtypeStruct((a, nb * b), g.dtype),
        compiler_params=_cparams(("parallel",)),
    )(g)


def _cols_to_blocks(x, *, name):
    a, b = x.shape[0], x.shape[1] // N_DEV

    def body(x_ref, o_ref):
        o_ref[...] = x_ref[...]

    return pl.pallas_call(
        body, name=name, grid=(N_DEV,), in_specs=[pl.BlockSpec((a, b), lambda j: (0, j))],
        out_specs=pl.BlockSpec((None, a, b), lambda j: (j, 0, 0)), out_shape=jax.ShapeDtypeStruct((N_DEV, a, b), x.dtype),
        compiler_params=_cparams(("parallel",)),
    )(x)


def _weight_shards(w, li):
    uq_pad = ((0, 0), (0, LANES - ATT_D))
    shards = {k: w[k][li] for k in BIG}
    shards["mla_w_uq"] = jnp.pad(shards["mla_w_uq"], uq_pad)
    return {k: s.astype(BF16) for k, s in shards.items()}


def _usable_weights(g, *, name):
    out = {}
    for k, a in g.items():
        if k == "w_in":
            out[k] = _place_w_in(a, name=name + "_place_w_in")
        elif k in ("w_out", "w_down", "ple_w_gate"):
            out[k] = a.reshape(a.shape[0] * a.shape[1], a.shape[2])
        elif k == "w_gate_up":
            out[k] = a
        else:
            out[k] = _blocks_to_cols(a, name=name + "_cols_" + k)
    return out


def _as_pairs(g):
    if g.ndim == 2:
        return g.reshape((4, 2, g.shape[0] // N_DEV) + g.shape[1:])
    return g.reshape((4, 2) + g.shape[1:])


def _twice(fn):
    return lambda *a: fn(*a) * 2


def _layer_forward(li, h, hb, p_i, wts, sm, lbs, tables, alpha, hgrn_job=None, more_weights=None, attn_job=None):
    n = f"l{li}_"
    row1 = lambda a: a.reshape(1, -1)
    projp = _mm(hb, wts["w_in"], name=n + "proj")
    ng = row1(sm["hgrn_norm_g"][li])
    res = _hgrn_fwd(projp, lbs[li], ng, name=n + "hgrn_fwd", job=hgrn_job)
    if hgrn_job is not None:
        res, got = res
        wts = dict(wts, **more_weights(got))
    o_a, o_pre, states = res
    lg, lbias = row1(sm["sgu_ln_g"][li]), row1(sm["sgu_ln_b"][li])
    w_s = sm["sgu_w_s"][li]
    bias_full = jnp.repeat(sm["sgu_b_s"][li].T, HEAD, axis=1)
    o_b = _sgu_fwd(projp, lg, lbias, w_s, bias_full, name=n + "sgu_fwd")
    qg, kvg = row1(sm["mla_q_norm_g"][li]), row1(sm["mla_kv_norm_g"][li])
    cq_view, ckv_view = (projp, 384, P_CQ // 384), (projp, 256, P_CKV // 256)
    (cqn,) = _rowwise(_fn_rms, [cq_view], [qg], [(384, BF16)], name=n + "q_norm")
    (ckvn,) = _rowwise(_fn_rms, [ckv_view], [kvg], [(256, BF16)], name=n + "kv_norm")
    q = _mm(cqn, wts["mla_w_uq"], name=n + "uq")
    kv = _mm(ckvn, wts["mla_w_ukv"], name=n + "ukv")
    qr, kf, kvb = _mla_prep(q, kv, projp, tables, name=n + "mla_prep")
    res, attn_got = _attn_fwd(qr, kf, kvb, name=n + "attn_fwd", job=attn_job), None
    if attn_job is not None:
        res, attn_got = res
    o_c, lse = res
    cat = jnp.concatenate([o_a, o_b, o_c.astype(BF16)], axis=1)
    mix = _mm(cat, wts["w_out"], name=n + "out_proj")
    g1, b1 = row1(sm["ln1_g"][li]), row1(sm["ln1_b"][li])
    d = h.shape[1]
    h1, h1b = _rowwise(_twice(_make_post_mix(alpha)), [h, mix], [g1, b1], [(d, F32), (d, BF16)], name=n + "ln1")
    gu, act = _gate_up_swiglu(h1b, wts["w_gate_up"], name=n + "gate_up")
    ffn = _mm(act, wts["w_down"], am="bmk", name=n + "down")
    pg = _mm(h1b, wts["ple_w_gate"], name=n + "ple_gate")
    pp = _mm(p_i, wts["ple_w_proj"], name=n + "ple_proj")
    g2, b2 = row1(sm["ln2_g"][li]), row1(sm["ln2_b"][li])
    h2, h2b = _rowwise(_twice(_make_ple_ln(alpha)), [h1, ffn, pg, pp], [g2, b2], [(d, F32), (d, BF16)],
                       name=n + "ln2")
    saved = dict(h=h, hb=hb, h1b=h1b, projp=projp, o_pre=o_pre, states=states, cqn=cqn, ckvn=ckvn, qr=qr, kf=kf, kvb=kvb, o_c=o_c,
                 lse=lse, cat=cat, mix=mix, h1=h1, gu=gu, act=act, ffn=ffn, pg=pg, pp=pp, ng=ng, lg=lg, wts=wts,
                 lbias=lbias, w_s=w_s, bias_full=bias_full, qg=qg, kvg=kvg, g1=g1, b1=b1, g2=g2, b2=b2)
    return (h2, h2b), saved, attn_got


RS_EARLY = ("ple_w_proj", "ple_w_gate", "w_down", "w_gate_up", "w_out")
RS_LATE = ("mla_w_uq", "mla_w_ukv", "w_in")


def _layer_backward(li, dh2_parts, p_i, sv, lbs, tables, alpha, core, carried=None):
    n = f"l{li}_b_"
    wts = sv["wts"]
    gr = {}
    dh1_a, dffn, dpg, dpp, gr["ln2_g"], gr["ln2_b"] = _rowwise_vjp(
        _make_ple_ln(alpha), [sv["h1"], sv["ffn"], sv["pg"], sv["pp"]], [sv["g2"], sv["b2"]], [dh2_parts],
        groups=[[0], [1], [2], [3]], gdtypes=[F32, BF16, BF16, BF16], name=n + "ln2")
    big = {}
    big["ple_w_proj"] = _cols_to_blocks(_mm(p_i, dpp, am="km", name=n + "ple_proj_dw"), name=n + "ple_proj_dw_blocks")
    big["ple_w_gate"] = _mm(sv["h1b"], dpg, am="km", name=n + "ple_gate_dw")
    dh1_b = _mm(dpg, wts["ple_w_gate"], bm="nk", name=n + "ple_gate_dx")
    big["w_down"] = _mm(sv["act"], dffn, am="bkm", name=n + "down_dw")
    dgu = _down_dx_swiglu(dffn, wts["w_down"], sv["gu"], name=n + "down_dx")
    dgu = dgu.reshape((N_DEV,) + dgu.shape[2:])
    big["w_gate_up"], carried_got = _mm(sv["h1b"], dgu, am="km", bm="bkn", om="bmn", name=n + "gate_up_dw",
                                        job=carried), None
    if carried is not None:
        big["w_gate_up"], carried_got = big["w_gate_up"]
    dh1_c = _mm(dgu, wts["w_gate_up"], am="bmk", bm="bnk", name=n + "gate_up_dx")
    dh_a, dmix, gr["ln1_g"], gr["ln1_b"] = _rowwise_vjp(
        _make_post_mix(alpha), [sv["h"], sv["mix"]], [sv["g1"], sv["b1"]], [[dh1_a, dh1_b, dh1_c]],
        groups=[[0], [1]], gdtypes=[F32, BF16], name=n + "ln1")
    big["w_out"] = _mm(sv["cat"], dmix, am="km", name=n + "out_proj_dw")
    early = [_as_pairs(big[k]) for k in RS_EARLY]
    dcat, theirs = _mm(dmix, wts["w_out"], bm="nk", name=n + "out_proj_dx", job=_pair_job(early))
    sums = [_pair_add(x, r, core, name=n + "pair_add_" + k) for k, x, r in zip(RS_EARLY, early, theirs)]

    (dqr, dkv, dkf), early_quads = _attn_bwd(sv["qr"], sv["kf"], sv["kvb"], dcat, sv["o_c"], sv["lse"],
                                             name=n + "attn", job=_quad_job(sums))
    dqpad, dkr = _mla_prep_bwd(dqr, dkf, tables, name=n + "mla_prep")
    big["mla_w_uq"] = _cols_to_blocks(_mm(sv["cqn"], dqpad, am="km", name=n + "uq_dw"), name=n + "uq_dw_blocks")
    dcqn = _mm(dqpad, wts["mla_w_uq"], bm="nk", name=n + "uq_dx")
    big["mla_w_ukv"] = _cols_to_blocks(_mm(sv["ckvn"], dkv, am="km", name=n + "ukv_dw"), name=n + "ukv_dw_blocks")
    dckvn = _mm(dkv, wts["mla_w_ukv"], bm="nk", name=n + "ukv_dx")
    projp = sv["projp"]
    dcq, gr["mla_q_norm_g"] = _rowwise_vjp(_fn_rms, [(projp, 384, P_CQ // 384)], [sv["qg"]], [[dcqn]],
                                           groups=[[0]], gdtypes=[BF16], name=n + "q_norm")
    dckv, gr["mla_kv_norm_g"] = _rowwise_vjp(_fn_rms, [(projp, 256, P_CKV // 256)], [sv["kvg"]], [[dckvn]],
                                             groups=[[0]], gdtypes=[BF16], name=n + "kv_norm")
    dsgu, gr["sgu_ln_g"], gr["sgu_ln_b"], gr["sgu_w_s"], gr["sgu_b_s"] = _sgu_bwd(
        projp, sv["lg"], sv["lbias"], sv["w_s"], sv["bias_full"], dcat, name=n + "sgu")
    dhg, gr["hgrn_norm_g"], gr["lower_bound"] = _hgrn_bwd(
        projp, lbs[li], sv["ng"], sv["o_pre"], sv["states"], dcat, name=n + "hgrn")
    dprojp = jnp.concatenate([dhg, dsgu, dcq, dkr, dckv], axis=1)
    big["w_in"] = _unplace_w_in(_mm(sv["hb"], dprojp, am="km", name=n + "proj_dw"), name=n + "proj_dw_shards")
    late = [_as_pairs(big[k]) for k in RS_LATE]
    dh_b, theirs = _mm(dprojp, wts["w_in"], bm="nk", name=n + "proj_dx", job=_pair_job(late))
    late_sums = [_pair_add(x, r, core, name=n + "pair_add_" + k) for k, x, r in zip(RS_LATE, late, theirs)]
    return [dh_a, dh_b], gr, early_quads, late_sums, carried_got


def kernel(x, p, positions, ln_in_g, ln_in_b, w_in, hgrn_lb_logits, hgrn_norm_g, sgu_ln_g, sgu_ln_b, sgu_w_s, sgu_b_s, mla_q_norm_g, mla_w_uq, mla_kv_norm_g, mla_w_ukv, w_out, ln1_g, ln1_b, w_gate_up, w_down, ple_w_gate, ple_w_proj, ln2_g, ln2_b, loss_target, m_ln_in_g, m_ln_in_b, m_w_in, m_hgrn_lb_logits, m_hgrn_norm_g, m_sgu_ln_g, m_sgu_ln_b, m_sgu_w_s, m_sgu_b_s, m_mla_q_norm_g, m_mla_w_uq, m_mla_kv_norm_g, m_mla_w_ukv, m_w_out, m_ln1_g, m_ln1_b, m_w_gate_up, m_w_down, m_ple_w_gate, m_ple_w_proj, m_ln2_g, m_ln2_b, v_ln_in_g, v_ln_in_b, v_w_in, v_hgrn_lb_logits, v_hgrn_norm_g, v_sgu_ln_g, v_sgu_ln_b, v_sgu_w_s, v_sgu_b_s, v_mla_q_norm_g, v_mla_w_uq, v_mla_kv_norm_g, v_mla_w_ukv, v_w_out, v_ln1_g, v_ln1_b, v_w_gate_up, v_w_down, v_ple_w_gate, v_ple_w_proj, v_ln2_g, v_ln2_b):
    args = dict(locals())
    w = {k: args[k] for k in ORDER}
    m = {k: args["m_" + k] for k in ORDER}
    v = {k: args["v_" + k] for k in ORDER}
    depth = w_in.shape[0]
    assert depth == 2, "the lower-bound kernel is written for two layers"
    alpha = (2 * depth) ** 0.25
    xs, tgt = x[0], loss_target[0]
    d_model = xs.shape[1]

    shards = [_weight_shards(w, li) for li in range(depth)]
    rest = [k for k in BIG if k != "w_in"]
    (g_in,) = _all_gather([shards[0]["w_in"]], name="gather_l0_w_in")
    w_in0 = _usable_weights({"w_in": g_in}, name="l0")

    def rest_of_layer0(got):
        got = _gather_forward(got, name="gather_l0_forward")
        return _usable_weights(dict(zip(rest, got)), name="l0")

    tables = _rope_tables(positions[0])
    row1 = lambda a: a.reshape(1, -1)
    l0, l1 = row1(hgrn_lb_logits[0]), row1(hgrn_lb_logits[1])
    lbs = _rowwise(_fn_lower_bounds, [l0, l1], [], [(HG_W, F32), (HG_W, F32)], name="lower_bounds")

    gin, bin_ = row1(ln_in_g), row1(ln_in_b)
    h, hb = _rowwise(_twice(_fn_ln), [xs], [gin, bin_], [(d_model, F32), (d_model, BF16)], name="ln_in")
    (h, hb), sv0, got1 = _layer_forward(0, h, hb, p[0, 0], w_in0, w, lbs, tables, alpha,
                                  hgrn_job=_gather_job([shards[0][k] for k in rest]), more_weights=rest_of_layer0,
                                  attn_job=_gather_job([shards[1][k] for k in BIG]))
    wts1 = _usable_weights(dict(zip(BIG, _gather_forward(got1, name="gather_l1_forward"))), name="l1")
    (h, _), sv1, _ = _layer_forward(1, h, hb, p[1, 0], wts1, w, lbs, tables, alpha)
    saved = [sv0, sv1]
    dy, loss_local = _loss_and_grad(h, tgt, name="loss")
    loss = lax.psum(loss_local[0, 0], ("x", "y", "c"))

    core = lax.axis_index("c").astype(jnp.int32).reshape(1)
    dparts, grads, quads, carried = [dy], [None] * depth, [None] * depth, None
    for li in reversed(range(depth)):
        dparts, grads[li], early_quads, late_sums, late_quads = _layer_backward(
            li, dparts, p[li, 0], saved[li], lbs, tables, alpha, core, carried=carried)
        quads[li] = dict(zip(RS_EARLY, early_quads))
        if carried is not None:
            quads[li + 1].update(zip(RS_LATE, late_quads))
        carried = _quad_job(late_sums)
    quads[0].update(zip(RS_LATE, _run_job(carried, name="rs_l0_late_quad")))
    dx, d_gin, d_bin = _rowwise_vjp(_fn_ln, [xs], [gin, bin_], [dparts], groups=[[0]], name="ln_in_b")
    dl0, dl1 = _rowwise_vjp(_fn_lower_bounds, [l0, l1], [], [[grads[0]["lower_bound"]], [grads[1]["lower_bound"]]],
                            groups=[[0], [1]], name="lower_bounds_b")

    prefixes = ("grad_", "delta_", "new_m_", "new_v_")
    per_layer = {pre + k: [] for pre in prefixes for k in BIG}
    uq_pad = ((0, 0), (0, 0), (0, LANES - ATT_D))
    state = {k: ((jnp.pad(w[k], uq_pad), jnp.pad(m[k], uq_pad), jnp.pad(v[k], uq_pad)) if k == "mla_w_uq"
                 else (w[k], m[k], v[k])) for k in BIG}
    for li in range(depth):
        for k in BIG:
            res4 = _adamw(quads[li][k], *state[k], li, name=f"adamw_l{li}_{k}")
            for pre, a in zip(prefixes, res4):
                per_layer[pre + k].append(a[:, :ATT_D] if k == "mla_w_uq" else a)
    out = {name: jnp.stack(vals) for name, vals in per_layer.items()}

    small_g = {"ln_in_g": d_gin.reshape(-1), "ln_in_b": d_bin.reshape(-1),
               "hgrn_lb_logits": jnp.stack([dl0.reshape(-1), dl1.reshape(-1)])}
    for k in SMALL[3:]:
        small_g[k] = jnp.stack([grads[li][k].reshape(w[k].shape[1:]) for li in range(depth)])
    (small_parts,) = _all_gather([_pack([small_g[k] for k in SMALL])], name="gather_small_grads")
    slabs = _adamw(small_parts, _pack([w[k] for k in SMALL]), _pack([m[k] for k in SMALL]),
                   _pack([v[k] for k in SMALL]), None, name="adamw_small")
    shapes = [w[k].shape for k in SMALL]
    for pre, slab in zip(prefixes, slabs):
        for k, a in zip(SMALL, _unpack(slab, shapes)):
            out[pre + k] = a
    res = [loss, dx[None]]
    for prefix in ("grad_", "delta_", "new_m_", "new_v_"):
        res += [out[prefix + k] for k in ORDER]
    return tuple(res)
```

```python
import functools
import math

import jax
import jax.numpy as jnp
from jax import lax
from jax.experimental import pallas as pl
from jax.experimental.pallas import tpu as pltpu

F32 = jnp.float32
BF16 = jnp.bfloat16
MESH = pl.DeviceIdType.MESH

LN_EPS = 1e-5
RMS_EPS = 1e-6
ROPE_THETA = 10000.0
ADAM_LR, ADAM_B1, ADAM_B2, ADAM_EPS, ADAM_WD, ADAM_STEP = 0.001, 0.9, 0.999, 1e-08, 0.01, 10

N_DEV = 8
LANES = 128
HG_CHUNK = 16
HG_W = 256
HEAD = 64
SGU_CHUNK = 128
N_ATT_HEADS = 8
ATT_D = 96
VMEM_LIMIT = 56 * 1024 * 1024

HG_TILE = 256
ATT_TQ = 512
ROW_TILE = 256

P_CQ, P_KR, P_CKV, P_COLS = 1536, 1920, 2048, 2304


def _cparams(sem):
    return pltpu.CompilerParams(dimension_semantics=sem, vmem_limit_bytes=VMEM_LIMIT)


_ANY = pl.BlockSpec(memory_space=pl.ANY)


def _call(body, operands, *, name, grid, in_specs, out_specs, out_shape, sem, scratch_shapes=(), job=None):
    if job is None:
        return pl.pallas_call(body, name=name, grid=grid, in_specs=in_specs, out_specs=out_specs, out_shape=out_shape,
                              scratch_shapes=list(scratch_shapes), compiler_params=_cparams(sem))(*operands)
    single = not isinstance(out_shape, (list, tuple))
    shapes = [out_shape] if single else list(out_shape)
    ospecs = [out_specs] if single else list(out_specs)
    ni, no, ns = len(operands), len(shapes), len(scratch_shapes)
    ji, jo = len(job.inputs), len(job.out_shapes)

    def hosted(*refs):
        p = 0
        parts = []
        for cnt in (ni, ji, no, jo, ns):
            parts.append(refs[p:p + cnt])
            p += cnt
        ins, jins, outs, jouts, scr = parts
        jsems = refs[p:]
        ids = [pl.program_id(a) for a in range(len(grid))]
        first = functools.reduce(lambda a, b: a & b, [i == 0 for i in ids])
        last = functools.reduce(lambda a, b: a & b, [i == g - 1 for i, g in zip(ids, grid)])

        @pl.when(first)
        def _():
            job.start(jins, jouts, jsems)

        body(*ins, *outs, *scr)

        @pl.when(last)
        def _():
            job.finish(jins, jouts, jsems)

    res = pl.pallas_call(
        hosted, name=name, grid=grid,
        in_specs=list(in_specs) + [_ANY] * ji, out_specs=ospecs + [_ANY] * jo,
        out_shape=shapes + list(job.out_shapes),
        scratch_shapes=list(scratch_shapes) + [pltpu.SemaphoreType.DMA((c,)) for c in job.sem_counts],
        compiler_params=_cparams(("arbitrary",) * len(grid)),
    )(*operands, *job.inputs)
    own = res[0] if single else res[:no]
    return own, res[no:]


class _Job:
    def __init__(self, inputs, out_shapes, sem_counts, start, finish):
        self.inputs, self.out_shapes, self.sem_counts = list(inputs), list(out_shapes), list(sem_counts)
        self.start, self.finish = start, finish


def _copies_job(inputs, out_shapes, n_remote, n_local, make):
    def start(jins, jouts, sems):
        sends, _, local = make(jins, jouts, *sems)
        for cp in local + sends:
            cp.start()

    def finish(jins, jouts, sems):
        sends, recvs, local = make(jins, jouts, *sems)
        for cp in recvs:
            cp.wait_recv()
        for cp in sends:
            cp.wait_send()
        for cp in local:
            cp.wait()

    return _Job(inputs, out_shapes, [n_remote, n_remote, max(n_local, 1)], start, finish)


def _run_job(job, *, name):
    ji, jo = len(job.inputs), len(job.out_shapes)

    def body(*refs):
        jins, jouts, sems = refs[:ji], refs[ji:ji + jo], refs[ji + jo:]
        job.start(jins, jouts, sems)
        job.finish(jins, jouts, sems)

    return pl.pallas_call(
        body, name=name, out_shape=list(job.out_shapes), in_specs=[_ANY] * ji, out_specs=[_ANY] * jo,
        scratch_shapes=[pltpu.SemaphoreType.DMA((c,)) for c in job.sem_counts],
    )(*job.inputs)


def _tile(n, pref):
    if n % pref == 0:
        return pref
    best = None
    t = LANES
    while t <= min(n, pref):
        if n % t == 0:
            best = t
        t += LANES
    return best if best is not None else n


def _mm(a, b, *, am="mk", bm="kn", om="mn", out_dtype=F32, tm=1024, tn=1024, tk=1024, name, job=None):
    if am == "mk":
        m, k = a.shape
    elif am == "km":
        k, m = a.shape
    elif am == "bmk":
        m, tk = a.shape[1], a.shape[2]
        k = a.shape[0] * tk
    else:
        k, tm = a.shape[1], a.shape[2]
        m = a.shape[0] * tm
    if bm == "kn":
        kb_, n = b.shape
    elif bm == "nk":
        n, kb_ = b.shape
    elif bm == "bkn":
        kb_, tn = b.shape[1], b.shape[2]
        n = b.shape[0] * tn
    else:
        n, tk = b.shape[1], b.shape[2]
        kb_ = b.shape[0] * tk
    assert kb_ == k, (a.shape, b.shape, am, bm)
    tm, tn, tk = _tile(m, tm), _tile(n, tn), _tile(k, tk)
    nk = k // tk
    dims = (((0 if am in ("km", "bkm") else 1,), (1 if bm in ("nk", "bnk") else 0,)), ((), ()))

    a_spec = {"mk": pl.BlockSpec((tm, tk), lambda i, j, kk: (i, kk)),
              "km": pl.BlockSpec((tk, tm), lambda i, j, kk: (kk, i)),
              "bmk": pl.BlockSpec((None, tm, tk), lambda i, j, kk: (kk, i, 0)),
              "bkm": pl.BlockSpec((None, tk, tm), lambda i, j, kk: (i, kk, 0))}[am]
    b_spec = {"kn": pl.BlockSpec((tk, tn), lambda i, j, kk: (kk, j)),
              "nk": pl.BlockSpec((tn, tk), lambda i, j, kk: (j, kk)),
              "bkn": pl.BlockSpec((None, tk, tn), lambda i, j, kk: (j, kk, 0)),
              "bnk": pl.BlockSpec((None, tn, tk), lambda i, j, kk: (kk, j, 0))}[bm]
    if om == "mn":
        o_spec, o_shape = pl.BlockSpec((tm, tn), lambda i, j, kk: (i, j)), (m, n)
    else:
        o_spec, o_shape = pl.BlockSpec((None, tm, tn), lambda i, j, kk: (j, i, 0)), (n // tn, m, tn)

    def body(a_ref, b_ref, o_ref, *acc):
        kk = pl.program_id(2)
        prod = lax.dot_general(a_ref[...].astype(BF16), b_ref[...].astype(BF16), dims, preferred_element_type=F32)
        if nk == 1:
            o_ref[...] = prod.astype(o_ref.dtype)
            return
        acc_ref, = acc

        @pl.when(kk == 0)
        def _():
            acc_ref[...] = prod

        if nk > 2:
            @pl.when((kk > 0) & (kk < nk - 1))
            def _():
                acc_ref[...] += prod

        @pl.when(kk == nk - 1)
        def _():
            o_ref[...] = (acc_ref[...] + prod).astype(o_ref.dtype)

    return _call(body, (a, b), name=name, grid=(m // tm, n // tn, nk), in_specs=[a_spec, b_spec], out_specs=o_spec,
                 out_shape=jax.ShapeDtypeStruct(o_shape, out_dtype),
                 scratch_shapes=[pltpu.VMEM((tm, tn), F32)] if nk > 1 else [],
                 sem=("parallel", "parallel", "arbitrary"), job=job)


def _row_operand(a, tile):
    if isinstance(a, tuple):
        arr, w, j = a
        return arr, pl.BlockSpec((tile, w), lambda i, j=j: (i, j))
    return a, pl.BlockSpec((tile, a.shape[1]), lambda i: (i, 0))


def _const_spec(c):
    nd = c.ndim
    return pl.BlockSpec(c.shape, lambda i, nd=nd: (0,) * nd)


def _rowwise(fn, rows, consts, outs, *, name, accs=(), tile=None):
    t_rows = (rows[0][0] if isinstance(rows[0], tuple) else rows[0]).shape[0]
    tile = min(tile or ROW_TILE, t_rows)
    arrs, specs = zip(*[_row_operand(a, tile) for a in rows])
    nin, no = len(rows) + len(consts), len(outs)

    def body(*refs):
        res = fn(*[r[...] for r in refs[:nin]])
        for r, v in zip(refs[nin:nin + no], res[:no]):
            r[...] = v.astype(r.dtype)
        if accs:
            a_refs = refs[nin + no:]

            @pl.when(pl.program_id(0) == 0)
            def _():
                for r in a_refs:
                    r[...] = jnp.zeros_like(r)

            for r, v in zip(a_refs, res[no:]):
                r[...] += v

    out_shape = [jax.ShapeDtypeStruct((t_rows, w), dt) for w, dt in outs]
    out_shape += [jax.ShapeDtypeStruct(s, F32) for s in accs]
    out_specs = [pl.BlockSpec((tile, w), lambda i: (i, 0)) for w, _ in outs]
    out_specs += [pl.BlockSpec(s, lambda i, nd=len(s): (0,) * nd) for s in accs]
    return pl.pallas_call(
        body, name=name, grid=(t_rows // tile,),
        in_specs=list(specs) + [_const_spec(c) for c in consts],
        out_specs=out_specs, out_shape=out_shape,
        compiler_params=_cparams(("arbitrary",)),
    )(*arrs, *consts)


def _rowwise_vjp(fn, rows, consts, cts, *, name, groups, tile=None, gdtypes=None):
    t_rows = (rows[0][0] if isinstance(rows[0], tuple) else rows[0]).shape[0]
    tile = min(tile or ROW_TILE, t_rows)
    arrs, specs = zip(*[_row_operand(a, tile) for a in rows])
    flat_cts = [c for group in cts for c in group]
    ct_arrs, ct_specs = zip(*[_row_operand(a, tile) for a in flat_cts])
    nr, nc, nct, ng = len(rows), len(consts), len(flat_cts), len(groups)

    def width(a):
        return a[1] if isinstance(a, tuple) else a.shape[1]

    def body(*refs):
        rv = [r[...].astype(F32) for r in refs[:nr]]
        cv = [r[...] for r in refs[nr:nr + nc]]
        ct_refs = refs[nr + nc:nr + nc + nct]
        ctv, pos = [], 0
        for group in cts:
            s = ct_refs[pos][...].astype(F32)
            for r in ct_refs[pos + 1:pos + len(group)]:
                s = s + r[...].astype(F32)
            ctv.append(s)
            pos += len(group)
        _, pull = jax.vjp(fn, *rv, *cv)
        grads = pull(tuple(ctv))
        g_refs = refs[nr + nc + nct:nr + nc + nct + ng]
        for r, idx in zip(g_refs, groups):
            parts = [grads[i] for i in idx]
            r[...] = (parts[0] if len(parts) == 1 else jnp.concatenate(parts, axis=1)).astype(r.dtype)
        c_refs = refs[nr + nc + nct + ng:]

        @pl.when(pl.program_id(0) == 0)
        def _():
            for r in c_refs:
                r[...] = jnp.zeros_like(r)

        for r, v in zip(c_refs, grads[nr:]):
            r[...] += v

    gw = [sum(width(rows[i]) for i in idx) for idx in groups]
    gdtypes = gdtypes or [F32] * ng
    out_shape = [jax.ShapeDtypeStruct((t_rows, w), dt) for w, dt in zip(gw, gdtypes)]
    out_shape += [jax.ShapeDtypeStruct(c.shape, F32) for c in consts]
    out_specs = [pl.BlockSpec((tile, w), lambda i: (i, 0)) for w in gw]
    out_specs += [_const_spec(c) for c in consts]
    return pl.pallas_call(
        body, name=name, grid=(t_rows // tile,),
        in_specs=list(specs) + [_const_spec(c) for c in consts] + list(ct_specs),
        out_specs=out_specs, out_shape=out_shape,
        compiler_params=_cparams(("arbitrary",)),
    )(*arrs, *consts, *ct_arrs)


def _layer_norm(x, g, b):
    mu = jnp.mean(x, axis=-1, keepdims=True)
    xc = x - mu
    var = jnp.mean(xc * xc, axis=-1, keepdims=True)
    return xc * lax.rsqrt(var + LN_EPS) * g + b


def _sigmoid(x):
    return 1.0 / (1.0 + jnp.exp(-x))


def _fn_ln(x, g, b):
    return (_layer_norm(x, g, b),)


def _fn_rms(x, g):
    return (x * lax.rsqrt(jnp.mean(x * x, axis=-1, keepdims=True) + RMS_EPS) * g,)


def _make_post_mix(alpha):
    def fn(h, mix, g, b):
        return (_layer_norm(alpha * h + mix, g, b),)
    return fn


def _make_ple_ln(alpha):
    def fn(h1, ffn, pg, pp, g, b):
        return (_layer_norm(alpha * h1 + ffn + _sigmoid(pg) * pp, g, b),)
    return fn


def _fn_lower_bounds(l0, l1):
    m = jnp.maximum(l0, l1)
    e0, e1 = jnp.exp(l0 - m), jnp.exp(l1 - m)
    s = e0 + e1
    p0, p1 = e0 / s, e1 / s
    return (p0 - p0, (p0 + p1) - p0)


def _loss_and_grad(y, target, *, name):
    d = y.shape[1]

    def fn(yv, tv):
        err = yv - tv
        return err * (1.0 / d), 0.5 * jnp.sum(jnp.mean(err * err, axis=-1, keepdims=True), axis=0, keepdims=True)

    return _rowwise(fn, [y, target], [], [(d, F32)], accs=[(1, 1)], name=name)


def _split_dot(x, e_bf16):
    hi = x.astype(BF16)
    lo = (x - hi.astype(F32)).astype(BF16)
    return (jnp.dot(hi, e_bf16, preferred_element_type=F32) + jnp.dot(lo, e_bf16, preferred_element_type=F32))


def _hgrn_common(th):
    rm = lax.broadcasted_iota(jnp.int32, (th, HG_W), 0) % HG_CHUNK

    def seg_cumsum(x):
        for s in (1, 2, 4, 8):
            x = x + jnp.where(rm >= s, pltpu.roll(x, s, 0), 0.0)
        return x

    def seg_rcumsum(x):
        for s in (1, 2, 4, 8):
            x = x + jnp.where(rm < HG_CHUNK - s, pltpu.roll(x, th - s, 0), 0.0)
        return x

    ri = lax.broadcasted_iota(jnp.int32, (HG_W, HG_W), 0) // HEAD
    ci = lax.broadcasted_iota(jnp.int32, (HG_W, HG_W), 1) // HEAD
    head_f32 = (ri == ci).astype(F32)
    head_bf16 = head_f32.astype(BF16)

    def headsum(x):
        return _split_dot(x, head_bf16)

    return rm, seg_cumsum, seg_rcumsum, head_f32, headsum


def _hgrn_gates(qr, fl, lb):
    sg = _sigmoid(fl)
    f = lb + (1.0 - lb) * sg
    sq = _sigmoid(qr)
    return sg, f, jnp.log(f), 1.0 - f, qr * sq, sq


def _shifted(x, d, th):
    return x if d == 0 else pltpu.roll(x, d, 0)


def _unshift(x, d, th):
    return x if d == 0 else pltpu.roll(x, th - d, 0)


def _hgrn_fwd(projp, lb, ng, *, name, job=None):
    t_rows = projp.shape[0]
    th = min(HG_TILE, t_rows)
    nct = th // HG_CHUNK

    def body(q_ref, f_ref, i_ref, g_ref, lb_ref, ng_ref, oa_ref, opre_ref, st_out_ref,
             st_ref, vtm_ref, kv_ref, qe_ref, dec_ref, oint_ref):
        rm, seg_cumsum, seg_rcumsum, head_f32, headsum = _hgrn_common(th)

        @pl.when(pl.program_id(0) == 0)
        def _():
            st_ref[...] = jnp.zeros_like(st_ref)

        qr, fl, v, g = q_ref[...], f_ref[...], i_ref[...], g_ref[...]
        _, f, lf, k, q, _ = _hgrn_gates(qr, fl, lb_ref[...])
        b = seg_cumsum(lf)

        o = jnp.zeros((th, HG_W), F32)
        for d in range(HG_CHUNK):
            kd, bd, vd = _shifted(k, d, th), _shifted(b, d, th), _shifted(v, d, th)
            e = jnp.exp(jnp.where(rm >= d, b - bd, -1e30))
            o = o + headsum(q * kd * e) * vd

        blast = seg_rcumsum(jnp.where(rm == HG_CHUNK - 1, b, 0.0))
        kte = (k * jnp.exp(blast - b)).astype(BF16)
        qe_ref[...] = q * jnp.exp(b)
        dec_ref[...] = jnp.exp(blast)
        vt = v.T
        lane_chunk = lax.broadcasted_iota(jnp.int32, (HG_W, th), 1) // HG_CHUNK
        for c in range(nct):
            vtm_ref[c * HG_W:(c + 1) * HG_W, :] = jnp.where(lane_chunk == c, vt, 0.0).astype(BF16)
        kv_ref[...] = jnp.dot(vtm_ref[...], kte, preferred_element_type=F32)

        def step(c, carry):
            r0 = pl.multiple_of(c * HG_CHUNK, HG_CHUNK)
            s = st_ref[...]
            st_out_ref[c] = s
            oint_ref[pl.ds(r0, HG_CHUNK), :] = lax.dot_general(
                qe_ref[pl.ds(r0, HG_CHUNK), :].astype(BF16), s.astype(BF16),
                (((1,), (1,)), ((), ())), preferred_element_type=F32)
            dec = jnp.max(dec_ref[pl.ds(r0, HG_CHUNK), :], axis=0, keepdims=True)
            kv_c = kv_ref[pl.ds(pl.multiple_of(c * HG_W, HG_W), HG_W), :]
            st_ref[...] = s * dec + kv_c * head_f32
            return carry

        lax.fori_loop(0, nct, step, 0)

        o = o + oint_ref[...]
        opre_ref[...] = o
        r = lax.rsqrt(headsum(o * o) * (1.0 / HEAD) + RMS_EPS)
        oa_ref[...] = (o * r * ng_ref[...] * (g * _sigmoid(g))).astype(oa_ref.dtype)

    col = lambda j: pl.BlockSpec((th, HG_W), lambda i, j=j: (i, j))
    vec = pl.BlockSpec((1, HG_W), lambda i: (0, 0))
    row = pl.BlockSpec((th, HG_W), lambda i: (i, 0))
    n_chunks = t_rows // HG_CHUNK
    return _call(
        body, (projp, projp, projp, projp, lb, ng), name=name, grid=(t_rows // th,),
        in_specs=[col(0), col(1), col(2), col(3), vec, vec],
        out_specs=[row, row, pl.BlockSpec((nct, HG_W, HG_W), lambda i: (i, 0, 0))],
        out_shape=[jax.ShapeDtypeStruct((t_rows, HG_W), BF16), jax.ShapeDtypeStruct((t_rows, HG_W), F32),
                   jax.ShapeDtypeStruct((n_chunks, HG_W, HG_W), F32)],
        scratch_shapes=[pltpu.VMEM((HG_W, HG_W), F32), pltpu.VMEM((nct * HG_W, th), BF16),
                        pltpu.VMEM((nct * HG_W, HG_W), F32), pltpu.VMEM((th, HG_W), F32),
                        pltpu.VMEM((th, HG_W), F32), pltpu.VMEM((th, HG_W), F32)],
        sem=("arbitrary",), job=job)


def _hgrn_bwd(projp, lb, ng, opre, states, dcat, *, name):
    t_rows = projp.shape[0]
    th = min(HG_TILE, t_rows)
    nct = th // HG_CHUNK
    nt = t_rows // th

    def body(q_ref, f_ref, i_ref, g_ref, lb_ref, ng_ref, opre_ref, st_in_ref, do_ref,
             dproj_ref, dng_ref, dlb_ref,
             gst_ref, dotm_ref, qg_ref, v_ref, kte_ref, dop_ref, dec_ref, dkte_ref, dvi_ref, dqe_ref, ddec_ref):
        rm, seg_cumsum, seg_rcumsum, head_f32, headsum = _hgrn_common(th)

        @pl.when(pl.program_id(0) == 0)
        def _():
            gst_ref[...] = jnp.zeros_like(gst_ref)
            dng_ref[...] = jnp.zeros_like(dng_ref)
            dlb_ref[...] = jnp.zeros_like(dlb_ref)

        qr, fl, v, g = q_ref[...], f_ref[...], i_ref[...], g_ref[...]
        lb, ngv = lb_ref[...], ng_ref[...]
        sg, f, lf, k, q, sq = _hgrn_gates(qr, fl, lb)
        b = seg_cumsum(lf)
        blast = seg_rcumsum(jnp.where(rm == HG_CHUNK - 1, b, 0.0))
        eb = jnp.exp(b)
        ekb = jnp.exp(blast - b)
        qe, kte, dec = q * eb, k * ekb, jnp.exp(blast)

        do_out, op = do_ref[...], opre_ref[...]
        sgg = _sigmoid(g)
        sil = g * sgg
        r = lax.rsqrt(headsum(op * op) * (1.0 / HEAD) + RMS_EPS)
        on = op * r
        dng_ref[...] += jnp.sum(do_out * on * sil, axis=0, keepdims=True)
        dg = do_out * on * ngv * (sgg * (1.0 + g * (1.0 - sgg)))
        don = do_out * ngv * sil
        dop = r * (don - on * (headsum(don * on) * (1.0 / HEAD)))

        v_ref[...] = v
        kte_ref[...] = kte
        dop_ref[...] = dop
        dec_ref[...] = dec
        dot_t = dop.T
        lane_chunk = lax.broadcasted_iota(jnp.int32, (HG_W, th), 1) // HG_CHUNK
        for c in range(nct):
            dotm_ref[c * HG_W:(c + 1) * HG_W, :] = jnp.where(lane_chunk == c, dot_t, 0.0).astype(BF16)
        qg_ref[...] = jnp.dot(dotm_ref[...], qe.astype(BF16), preferred_element_type=F32)

        def step(j, carry):
            c = nct - 1 - j
            r0 = pl.multiple_of(c * HG_CHUNK, HG_CHUNK)
            rows = pl.ds(r0, HG_CHUNK)
            gs = gst_ref[...]
            s = st_in_ref[c]
            gm = (gs * head_f32).astype(BF16)
            dkte_ref[rows, :] = jnp.dot(v_ref[rows, :].astype(BF16), gm, preferred_element_type=F32)
            dvi_ref[rows, :] = lax.dot_general(kte_ref[rows, :].astype(BF16), gm, (((1,), (1,)), ((), ())),
                                               preferred_element_type=F32)
            dqe_ref[rows, :] = jnp.dot(dop_ref[rows, :].astype(BF16), s.astype(BF16), preferred_element_type=F32)
            ddec_ref[rows, :] = jnp.broadcast_to(jnp.sum(gs * s, axis=0, keepdims=True), (HG_CHUNK, HG_W))
            dec_c = jnp.max(dec_ref[rows, :], axis=0, keepdims=True)
            qg_c = qg_ref[pl.ds(pl.multiple_of(c * HG_W, HG_W), HG_W), :]
            gst_ref[...] = gs * dec_c + qg_c * head_f32
            return carry

        lax.fori_loop(0, nct, step, 0)

        dkte, dqe = dkte_ref[...], dqe_ref[...]
        dq = dqe * eb
        dk = dkte * ekb
        db = dqe * qe - dkte * kte
        dv = dvi_ref[...]
        dblast = dkte * kte + jnp.where(rm == HG_CHUNK - 1, ddec_ref[...] * dec, 0.0)

        for d in range(HG_CHUNK):
            kd, bd, vd = _shifted(k, d, th), _shifted(b, d, th), _shifted(v, d, th)
            e = jnp.exp(jnp.where(rm >= d, b - bd, -1e30))
            p = q * kd * e
            sc = headsum(p)
            dsc = headsum(dop * vd)
            dv = dv + _unshift(sc * dop, d, th)
            dq = dq + dsc * kd * e
            dk = dk + _unshift(dsc * q * e, d, th)
            darg = dsc * p
            db = db + darg - _unshift(darg, d, th)

        db = db + jnp.where(rm == HG_CHUNK - 1, seg_cumsum(dblast), 0.0)
        dlf = seg_rcumsum(db)
        df = dlf / f - dk
        dlb_ref[...] += jnp.sum(df * (1.0 - sg), axis=0, keepdims=True)
        dfl = df * (1.0 - lb) * sg * (1.0 - sg)
        dqr = dq * (sq * (1.0 + qr * (1.0 - sq)))
        dproj_ref[...] = jnp.concatenate([dqr, dfl, dv, dg], axis=1).astype(dproj_ref.dtype)

    rev = lambda i: nt - 1 - i
    col = lambda j: pl.BlockSpec((th, HG_W), lambda i, j=j: (rev(i), j))
    vec = pl.BlockSpec((1, HG_W), lambda i: (0, 0))
    row = pl.BlockSpec((th, HG_W), lambda i: (rev(i), 0))
    tile_f32 = pltpu.VMEM((th, HG_W), F32)
    return pl.pallas_call(
        body, name=name, grid=(nt,),
        in_specs=[col(0), col(1), col(2), col(3), vec, vec, row,
                  pl.BlockSpec((nct, HG_W, HG_W), lambda i: (rev(i), 0, 0)), col(0)],
        out_specs=[pl.BlockSpec((th, 4 * HG_W), lambda i: (rev(i), 0)), vec, vec],
        out_shape=[jax.ShapeDtypeStruct((t_rows, 4 * HG_W), BF16), jax.ShapeDtypeStruct((1, HG_W), F32),
                   jax.ShapeDtypeStruct((1, HG_W), F32)],
        scratch_shapes=[pltpu.VMEM((HG_W, HG_W), F32), pltpu.VMEM((nct * HG_W, th), BF16),
                        pltpu.VMEM((nct * HG_W, HG_W), F32)] + [tile_f32] * 8,
        compiler_params=_cparams(("arbitrary",)),
    )(projp, projp, projp, projp, lb, ng, opre, states, dcat)


_INV_SQRT2 = 1.0 / math.sqrt(2.0)
_INV_SQRT2PI = 1.0 / math.sqrt(2.0 * math.pi)


def _gelu(x):
    return 0.5 * x * (1.0 + lax.erf(x * _INV_SQRT2))


def _gelu_grad(x):
    return 0.5 * (1.0 + lax.erf(x * _INV_SQRT2)) + x * jnp.exp(-0.5 * x * x) * _INV_SQRT2PI


def _sgu_parts(bu, bv, lg, lbias, w_ref, n_groups):
    c = SGU_CHUNK
    tril = (lax.broadcasted_iota(jnp.int32, (c, c), 0) >= lax.broadcasted_iota(jnp.int32, (c, c), 1)).astype(F32)
    gid = lax.broadcasted_iota(jnp.int32, bu.shape, 1) // HEAD
    u = _gelu(bu)
    gv = _gelu(bv)
    mu = jnp.mean(gv, axis=-1, keepdims=True)
    xc = gv - mu
    rstd = lax.rsqrt(jnp.mean(xc * xc, axis=-1, keepdims=True) + LN_EPS)
    xhat = xc * rstd
    vn = xhat * lg + lbias
    ws = [w_ref[gi] * tril for gi in range(n_groups)]
    return tril, gid, u, rstd, xhat, vn, ws


def _sgu_fwd(projp, lg, lbias, w_s, bias_full, *, name):
    t_rows = projp.shape[0]
    n_groups = w_s.shape[0]
    c = SGU_CHUNK

    def body(u_ref, v_ref, lg_ref, lb_ref, w_ref, bias_ref, o_ref):
        _, gid, u, _, _, vn, ws = _sgu_parts(u_ref[...], v_ref[...], lg_ref[...], lb_ref[...], w_ref, n_groups)
        vnb = vn.astype(BF16)
        z = bias_ref[...]
        for gi in range(n_groups):
            z = z + jnp.where(gid == gi, jnp.dot(ws[gi].astype(BF16), vnb, preferred_element_type=F32), 0.0)
        o_ref[...] = (u * z).astype(o_ref.dtype)

    col = lambda j: pl.BlockSpec((c, HG_W), lambda i, j=j: (i, j))
    return pl.pallas_call(
        body, name=name, grid=(t_rows // c,),
        in_specs=[col(4), col(5), _const_spec(lg), _const_spec(lbias), _const_spec(w_s), _const_spec(bias_full)],
        out_specs=pl.BlockSpec((c, HG_W), lambda i: (i, 0)),
        out_shape=jax.ShapeDtypeStruct((t_rows, HG_W), BF16),
        compiler_params=_cparams(("arbitrary",)),
    )(projp, projp, lg, lbias, w_s, bias_full)


def _sgu_bwd(projp, lg, lbias, w_s, bias_full, dcat, *, name):
    t_rows = projp.shape[0]
    n_groups = w_s.shape[0]
    c = SGU_CHUNK
    n = t_rows // c

    def body(u_ref, v_ref, lg_ref, lb_ref, w_ref, bias_ref, do_ref,
             dproj_ref, dlg_ref, dlb_ref, dw_ref, dbs_ref, dbias_acc):
        i = pl.program_id(0)

        @pl.when(i == 0)
        def _():
            dlg_ref[...] = jnp.zeros_like(dlg_ref)
            dlb_ref[...] = jnp.zeros_like(dlb_ref)
            dw_ref[...] = jnp.zeros_like(dw_ref)
            dbias_acc[...] = jnp.zeros_like(dbias_acc)

        bu, bv, lg_v = u_ref[...], v_ref[...], lg_ref[...]
        tril, gid, u, rstd, xhat, vn, ws = _sgu_parts(bu, bv, lg_v, lb_ref[...], w_ref, n_groups)
        vnb = vn.astype(BF16)
        z = bias_ref[...]
        for gi in range(n_groups):
            z = z + jnp.where(gid == gi, jnp.dot(ws[gi].astype(BF16), vnb, preferred_element_type=F32), 0.0)
        do = do_ref[...]
        dbu = do * z * _gelu_grad(bu)
        dz = do * u
        dbias_acc[...] += dz
        dvn = jnp.zeros_like(dz)
        for gi in range(n_groups):
            dzg = jnp.where(gid == gi, dz, 0.0).astype(BF16)
            dw_ref[gi] += lax.dot_general(dzg, vnb, (((1,), (1,)), ((), ())), preferred_element_type=F32) * tril
            dvn = dvn + jnp.dot(ws[gi].T.astype(BF16), dzg, preferred_element_type=F32)
        dlg_ref[...] += jnp.sum(dvn * xhat, axis=0, keepdims=True)
        dlb_ref[...] += jnp.sum(dvn, axis=0, keepdims=True)
        dxh = dvn * lg_v
        dgv = rstd * (dxh - jnp.mean(dxh, axis=-1, keepdims=True)
                      - xhat * jnp.mean(dxh * xhat, axis=-1, keepdims=True))
        dproj_ref[...] = jnp.concatenate([dbu, dgv * _gelu_grad(bv)], axis=1).astype(dproj_ref.dtype)

        @pl.when(i == n - 1)
        def _():
            dbs_ref[...] = jnp.sum(dbias_acc[...].T.reshape(n_groups, HEAD, c), axis=1)

    col = lambda j: pl.BlockSpec((c, HG_W), lambda i, j=j: (i, j))
    return pl.pallas_call(
        body, name=name, grid=(n,),
        in_specs=[col(4), col(5), _const_spec(lg), _const_spec(lbias), _const_spec(w_s), _const_spec(bias_full),
                  col(1)],
        out_specs=[pl.BlockSpec((c, 2 * HG_W), lambda i: (i, 0)), _const_spec(lg), _const_spec(lbias),
                   _const_spec(w_s), pl.BlockSpec((n_groups, c), lambda i: (0, 0))],
        out_shape=[jax.ShapeDtypeStruct((t_rows, 2 * HG_W), BF16), jax.ShapeDtypeStruct(lg.shape, F32),
                   jax.ShapeDtypeStruct(lbias.shape, F32), jax.ShapeDtypeStruct(w_s.shape, F32),
                   jax.ShapeDtypeStruct((n_groups, c), F32)],
        scratch_shapes=[pltpu.VMEM((c, HG_W), F32)],
        compiler_params=_cparams(("arbitrary",)),
    )(projp, projp, lg, lbias, w_s, bias_full, dcat)


def _rope_tables(positions):
    t = positions.shape[0]
    inv_freq = ROPE_THETA ** (-jnp.arange(0, 32, 2, dtype=F32) / 32)
    ang = positions.astype(F32)[:, None] * inv_freq
    cos, sin = jnp.cos(ang), jnp.sin(ang)
    z = lambda w: jnp.zeros((t, w), F32)
    cos_t = jnp.concatenate([jnp.ones((t, 64), F32), cos, cos, z(32)], axis=1)
    sin_up = jnp.concatenate([z(80), sin, z(32)], axis=1)
    sin_dn = jnp.concatenate([z(64), -sin, z(48)], axis=1)
    return cos_t, sin_up, sin_dn


def _rep(x, n):
    return x if n == 1 else jnp.concatenate([x] * n, axis=1)


def _rope(x, cos_t, sin_up, sin_dn):
    w = x.shape[1]
    return x * cos_t + pltpu.roll(x, 16, 1) * sin_up + pltpu.roll(x, w - 16, 1) * sin_dn


def _rope_t(dy, cos_t, sin_up, sin_dn):
    w = dy.shape[1]
    return dy * cos_t + pltpu.roll(dy * sin_up, w - 16, 1) + pltpu.roll(dy * sin_dn, 16, 1)


def _mla_prep(q, kv, projp, tables, *, name):
    nh = N_ATT_HEADS

    def fn(qv, kvv, kr, cos_t, sin_up, sin_dn):
        qr = _rope(qv, _rep(cos_t, nh), _rep(sin_up, nh), _rep(sin_dn, nh))
        krr = _rope(kr, cos_t, sin_up, sin_dn)
        lane = lax.broadcasted_iota(jnp.int32, kvv.shape, 1) % LANES
        return qr, jnp.where(lane < HEAD, kvv, 0.0) + _rep(krr, nh), kvv

    w = q.shape[1]
    return _rowwise(fn, [q, kv, (projp, LANES, P_KR // LANES)] + list(tables), [],
                    [(w, BF16), (w, BF16), (w, BF16)], name=name)


def _mla_prep_bwd(dqr, dkf, tables, *, name):
    nh = N_ATT_HEADS

    def fn(dq, dk, cos_t, sin_up, sin_dn):
        dqp = _rope_t(dq, _rep(cos_t, nh), _rep(sin_up, nh), _rep(sin_dn, nh))
        dkrr = dk[:, 0:LANES]
        for h in range(1, nh):
            dkrr = dkrr + dk[:, LANES * h:LANES * (h + 1)]
        return dqp, _rope_t(dkrr, cos_t, sin_up, sin_dn)

    return _rowwise(fn, [dqr, dkf] + list(tables), [], [(dqr.shape[1], BF16), (LANES, BF16)], name=name)


_LOG2E = 1.0 / math.log(2.0)
_NT = (((1,), (1,)), ((), ()))
_TN = (((0,), (0,)), ((), ()))


def _attn_fwd(qr, kf, kvb, *, name, job=None):
    t_rows = qr.shape[0]
    tq = min(ATT_TQ, t_rows)
    nb = t_rows // tq
    scale = ATT_D ** -0.5

    c2 = scale * _LOG2E

    def body(q_ref, kf_ref, kvb_ref, o_ref, lse_ref):
        qi = pl.program_id(1)
        lane = lax.broadcasted_iota(jnp.int32, (tq, LANES), 1)
        causal_t = (lax.broadcasted_iota(jnp.int32, (tq, tq), 0) <= lax.broadcasted_iota(jnp.int32, (tq, tq), 1))
        heads = [slice(hh * LANES, (hh + 1) * LANES) for hh in range(2)]
        qs = [q_ref[:, cols] for cols in heads]

        def block(ki, carry, diagonal):
            rows = pl.ds(pl.multiple_of(ki * tq, tq), tq)
            new = []
            for q, cols, (m_old, l_old, acc_t) in zip(qs, heads, carry):
                s_t = lax.dot_general(kf_ref[rows, cols], q, _NT, preferred_element_type=F32)
                if diagonal:
                    s_t = jnp.where(causal_t, s_t, -1e30)
                m_new = jnp.maximum(m_old, jnp.max(s_t, axis=0, keepdims=True))
                p_t = jnp.exp2((s_t - m_new) * c2)
                a = jnp.exp2((m_old - m_new) * c2)
                pv_t = lax.dot_general(kvb_ref[rows, cols], p_t.astype(BF16), _TN, preferred_element_type=F32)
                new.append((m_new, a * l_old + jnp.sum(p_t, axis=0, keepdims=True), a * acc_t + pv_t))
            return tuple(new)

        init = (jnp.full((1, tq), -1e30, F32), jnp.zeros((1, tq), F32), jnp.zeros((LANES, tq), F32))
        carry = lax.fori_loop(0, qi, lambda ki, c: block(ki, c, False), (init, init))
        outs = []
        for hh, (m_fin, l_fin, acc_t) in enumerate(block(qi, carry, True)):
            lse_ref[hh] = m_fin * scale + jnp.log(l_fin)
            outs.append((acc_t / l_fin).T)
        o_ref[...] = jnp.where(lane < HEAD, pltpu.roll(outs[0], HEAD, 1), outs[1])

    pair = pl.BlockSpec((t_rows, 2 * LANES), lambda pr, qi: (0, pr))
    return _call(
        body, (qr, kf, kvb), name=name, grid=(N_ATT_HEADS // 2, nb),
        in_specs=[pl.BlockSpec((tq, 2 * LANES), lambda pr, qi: (qi, pr)), pair, pair],
        out_specs=[pl.BlockSpec((tq, LANES), lambda pr, qi: (qi, pr)),
                   pl.BlockSpec((2, 1, tq), lambda pr, qi: (pr, 0, qi))],
        out_shape=[jax.ShapeDtypeStruct((t_rows, N_ATT_HEADS * HEAD), F32),
                   jax.ShapeDtypeStruct((N_ATT_HEADS, 1, t_rows), F32)],
        sem=("parallel", "arbitrary"), job=job)


def _attn_bwd(qr, kf, kvb, dcat, o, lse, *, name, job=None):
    t_rows = qr.shape[0]
    tq = min(ATT_TQ, t_rows)
    nb = t_rows // tq
    scale = ATT_D ** -0.5
    c2 = scale * _LOG2E
    do_off = 2 * HG_W // LANES

    def body(q_ref, kf_ref, kvb_ref, do_ref, o_ref, lse_ref, dq_ref, dkv_ref, dk_ref):
        ki = pl.program_id(1)

        @pl.when(ki == 0)
        def _():
            dq_ref[...] = jnp.zeros_like(dq_ref)

        lane = lax.broadcasted_iota(jnp.int32, (tq, LANES), 1)
        causal_t = (lax.broadcasted_iota(jnp.int32, (tq, tq), 0) <= lax.broadcasted_iota(jnp.int32, (tq, tq), 1))
        heads = [slice(hh * LANES, (hh + 1) * LANES) for hh in range(2)]
        ks = [kf_ref[:, cols] for cols in heads]
        vs = [kvb_ref[:, cols] for cols in heads]

        def block(qi, carry, diagonal):
            rows = pl.ds(pl.multiple_of(qi * tq, tq), tq)
            do_pair, o_pair = do_ref[rows, :], o_ref[rows, :]
            new = []
            for hh, (cols, k, v, (dk, dv)) in enumerate(zip(heads, ks, vs, carry)):
                q = q_ref[rows, cols]
                do, ov = (pltpu.roll(do_pair, HEAD, 1), pltpu.roll(o_pair, HEAD, 1)) if hh == 0 else (do_pair, o_pair)
                do = jnp.where(lane >= HEAD, do, 0.0)
                delta = jnp.sum((do * ov).T, axis=0, keepdims=True)
                s_t = lax.dot_general(k, q, _NT, preferred_element_type=F32)
                if diagonal:
                    s_t = jnp.where(causal_t, s_t, -1e30)
                p_t = jnp.exp2(s_t * c2 - lse_ref[hh, :, rows] * _LOG2E)
                dob = do.astype(BF16)
                dv = dv + jnp.dot(p_t.astype(BF16), dob, preferred_element_type=F32)
                dp_t = lax.dot_general(v, dob, _NT, preferred_element_type=F32)
                ds_t = (p_t * (dp_t - delta) * scale).astype(BF16)
                dk = dk + jnp.dot(ds_t, q, preferred_element_type=F32)
                dq_ref[rows, cols] += lax.dot_general(ds_t, k, _TN, preferred_element_type=F32)
                new.append((dk, dv))
            return tuple(new)

        zero = jnp.zeros((tq, LANES), F32)
        carry = block(ki, ((zero, zero), (zero, zero)), True)
        carry = lax.fori_loop(ki + 1, nb, lambda qi, c: block(qi, c, False), carry)
        dkv_ref[...] = jnp.concatenate([jnp.where(lane < HEAD, dk, dv) for dk, dv in carry],
                                       axis=1).astype(dkv_ref.dtype)
        dk_ref[...] = jnp.concatenate([dk for dk, _ in carry], axis=1)

    pair_all = pl.BlockSpec((t_rows, 2 * LANES), lambda pr, ki: (0, pr))
    pair_blk = pl.BlockSpec((tq, 2 * LANES), lambda pr, ki: (ki, pr))
    wide = jax.ShapeDtypeStruct((t_rows, N_ATT_HEADS * LANES), F32)
    return _call(
        body, (qr, kf, kvb, dcat, o, lse), name=name, grid=(N_ATT_HEADS // 2, nb),
        in_specs=[pair_all, pair_blk, pair_blk,
                  pl.BlockSpec((t_rows, LANES), lambda pr, ki: (0, do_off + pr)),
                  pl.BlockSpec((t_rows, LANES), lambda pr, ki: (0, pr)),
                  pl.BlockSpec((2, 1, t_rows), lambda pr, ki: (pr, 0, 0))],
        out_specs=[pair_all, pair_blk, pair_blk],
        out_shape=[wide, jax.ShapeDtypeStruct(wide.shape, BF16), wide],
        sem=("parallel", "arbitrary"), job=job)


def _my_pos():
    return lax.axis_index("x"), lax.axis_index("y"), lax.axis_index("c")


def _all_gather(xs, *, name):
    return _gather_forward(_run_job(_gather_job(xs), name=name), name=name + "_forward")


def _remote(src, dst, send_sems, recv_sems, k, dev):
    return pltpu.make_async_remote_copy(src_ref=src, dst_ref=dst, send_sem=send_sems.at[k], recv_sem=recv_sems.at[k],
                                        device_id=dev, device_id_type=MESH)


def _gather_job(xs):
    n = len(xs)

    def make(x_refs, out_refs, send_sems, recv_sems, local_sems):
        mx, my, mc = _my_pos()
        mine = 4 * mx + 2 * my + mc
        peers = [(mx, my, 1 - mc), (1 - mx, my, mc), (mx, 1 - my, mc), (1 - mx, 1 - my, mc)]
        sends, recvs, local = [], [], []
        for a in range(n):
            local.append(pltpu.make_async_copy(x_refs[a], out_refs[a].at[mine], local_sems.at[a]))
            for k, dev in enumerate(peers):
                theirs = 4 * dev[0] + 2 * dev[1] + dev[2]
                sends.append(_remote(x_refs[a], out_refs[a].at[mine], send_sems, recv_sems, 4 * a + k, dev))
                recvs.append(_remote(x_refs[a], out_refs[a].at[theirs], send_sems, recv_sems, 4 * a + k, dev))
        return sends, recvs, local

    shapes = [jax.ShapeDtypeStruct((N_DEV,) + x.shape, x.dtype) for x in xs]
    return _copies_job(xs, shapes, 4 * n, n, make)


def _gather_forward(gs, *, name):
    n = len(gs)

    def body(*refs):
        out_refs = refs[n:2 * n]
        send_sems, recv_sems = refs[2 * n:]
        mx, my, mc = _my_pos()
        chips = [(1 - mx, my), (mx, 1 - my), (1 - mx, 1 - my)]
        sends, recvs = [], []
        for a in range(n):
            for j, (cx, cy) in enumerate(chips):
                here, there = out_refs[a].at[4 * cx + 2 * cy + mc], out_refs[a].at[4 * cx + 2 * cy + 1 - mc]
                sends.append(_remote(here, here, send_sems, recv_sems, 3 * a + j, (mx, my, 1 - mc)))
                recvs.append(_remote(here, there, send_sems, recv_sems, 3 * a + j, (mx, my, 1 - mc)))
        for cp in sends:
            cp.start()
        for cp in recvs:
            cp.wait_recv()
        for cp in sends:
            cp.wait_send()

    return pl.pallas_call(
        body, name=name, out_shape=[jax.ShapeDtypeStruct(g.shape, g.dtype) for g in gs],
        in_specs=[_ANY] * n, out_specs=[_ANY] * n, input_output_aliases={a: a for a in range(n)},
        scratch_shapes=[pltpu.SemaphoreType.DMA((3 * n,)), pltpu.SemaphoreType.DMA((3 * n,))],
    )(*gs)


def _pair_job(xs):
    n = len(xs)

    def make(x_refs, out_refs, send_sems, recv_sems, local_sems):
        mx, my, mc = _my_pos()
        copies = [_remote(x_refs[a].at[g, 1 - mc], out_refs[a].at[g], send_sems, recv_sems, 4 * a + g, (mx, my, 1 - mc))
                  for a in range(n) for g in range(4)]
        return copies, copies, []

    shapes = [jax.ShapeDtypeStruct((4,) + x.shape[2:], x.dtype) for x in xs]
    return _copies_job(xs, shapes, 4 * n, 0, make)


def _pair_add(x, r, core, *, name):
    _, _, a, b = x.shape
    ta = _row_tile(a, 256)

    def body(c_ref, x_ref, r_ref, o_ref):
        o_ref[...] = x_ref[...] + r_ref[...]

    blk = pl.BlockSpec((None, ta, b), lambda g, i, c_ref: (g, i, 0))
    return pl.pallas_call(
        body, name=name,
        grid_spec=pltpu.PrefetchScalarGridSpec(
            num_scalar_prefetch=1, grid=(4, a // ta),
            in_specs=[pl.BlockSpec((None, None, ta, b), lambda g, i, c_ref: (g, c_ref[0], i, 0)), blk],
            out_specs=blk),
        out_shape=jax.ShapeDtypeStruct((4, a, b), x.dtype),
        compiler_params=_cparams(("parallel", "parallel")),
    )(core, x, r)


def _quad_job(xs):
    n = len(xs)

    def make(x_refs, out_refs, send_sems, recv_sems, local_sems):
        mx, my, mc = _my_pos()
        mine = 2 * mx + my
        peers = [((1 - mx, my, mc), 2 * (1 - mx) + my), ((mx, 1 - my, mc), 2 * mx + 1 - my),
                 ((1 - mx, 1 - my, mc), 2 * (1 - mx) + 1 - my)]
        sends, recvs, local = [], [], []
        for a in range(n):
            local.append(pltpu.make_async_copy(x_refs[a].at[mine], out_refs[a].at[mine], local_sems.at[a]))
            for k, (dev, g) in enumerate(peers):
                sends.append(_remote(x_refs[a].at[g], out_refs[a].at[mine], send_sems, recv_sems, 3 * a + k, dev))
                recvs.append(_remote(x_refs[a].at[g], out_refs[a].at[g], send_sems, recv_sems, 3 * a + k, dev))
        return sends, recvs, local

    shapes = [jax.ShapeDtypeStruct(x.shape, x.dtype) for x in xs]
    return _copies_job(xs, shapes, 3 * n, n, make)


def _row_tile(r, pref):
    t = min(pref, r)
    while r % t or (t % 8 and t != r):
        t -= 1
    return t


def _adamw(parts, w, m, v, layer, *, name, tile=256):
    g, a, b = parts.shape
    tile = _row_tile(a, tile)
    c1 = 1.0 / (1.0 - ADAM_B1 ** ADAM_STEP)
    c2 = 1.0 / (1.0 - ADAM_B2 ** ADAM_STEP)

    def body(p_ref, w_ref, m_ref, v_ref, g_ref, d_ref, mo_ref, vo_ref):
        grad = p_ref[0]
        for j in range(1, g):
            grad = grad + p_ref[j]
        mn = ADAM_B1 * m_ref[...] + (1.0 - ADAM_B1) * grad
        vn = ADAM_B2 * v_ref[...] + (1.0 - ADAM_B2) * (grad * grad)
        g_ref[...] = grad
        mo_ref[...] = mn
        vo_ref[...] = vn
        d_ref[...] = -ADAM_LR * ((mn * c1) / (jnp.sqrt(vn * c2) + ADAM_EPS) + ADAM_WD * w_ref[...])

    slab = pl.BlockSpec((tile, b), lambda i: (i, 0))
    src = slab if layer is None else pl.BlockSpec((None, tile, b), lambda i: (layer, i, 0))
    return pl.pallas_call(
        body, name=name, grid=(a // tile,),
        in_specs=[pl.BlockSpec((g, tile, b), lambda i: (0, i, 0)), src, src, src],
        out_specs=[slab] * 4,
        out_shape=[jax.ShapeDtypeStruct((a, b), F32)] * 4,
        compiler_params=_cparams(("parallel",)),
    )(parts, w, m, v)


W_IN_SHARD = 276


def _w_in_dest(col):
    return jnp.where(col < P_KR, col, jnp.where(col < P_KR + 256, col + (P_CKV - P_KR), col - 2176 + P_KR + HEAD))


def _place_w_in(g, *, name):
    _, d, sh = g.shape
    tc = 768

    def body(g_ref, o_ref, acc_ref):
        ct, j = pl.program_id(0), pl.program_id(1)

        @pl.when(j == 0)
        def _():
            acc_ref[...] = jnp.zeros_like(acc_ref)

        src = j * sh + lax.broadcasted_iota(jnp.int32, (sh, tc), 0)
        dst = ct * tc + lax.broadcasted_iota(jnp.int32, (sh, tc), 1)
        place = (_w_in_dest(src) == dst).astype(BF16)
        acc_ref[...] += jnp.dot(g_ref[...], place, preferred_element_type=F32)

        @pl.when(j == N_DEV - 1)
        def _():
            o_ref[...] = acc_ref[...].astype(o_ref.dtype)

    return pl.pallas_call(
        body, name=name, grid=(P_COLS // tc, N_DEV),
        in_specs=[pl.BlockSpec((None, d, sh), lambda ct, j: (j, 0, 0))],
        out_specs=pl.BlockSpec((d, tc), lambda ct, j: (0, ct)),
        out_shape=jax.ShapeDtypeStruct((d, P_COLS), BF16),
        scratch_shapes=[pltpu.VMEM((d, tc), F32)],
        compiler_params=_cparams(("parallel", "arbitrary")),
    )(g)


def _unplace_w_in(dw, *, name):
    d = dw.shape[0]
    sh = W_IN_SHARD

    def body(dw_ref, o_ref):
        j = pl.program_id(0)
        src = j * sh + lax.broadcasted_iota(jnp.int32, (P_COLS, sh), 1)
        dst = lax.broadcasted_iota(jnp.int32, (P_COLS, sh), 0)
        pick = (_w_in_dest(src) == dst).astype(BF16)
        x = dw_ref[...]
        o_ref[...] = _split_dot(x, pick)

    return pl.pallas_call(
        body, name=name, grid=(N_DEV,),
        in_specs=[pl.BlockSpec((d, P_COLS), lambda j: (0, 0))],
        out_specs=pl.BlockSpec((None, d, sh), lambda j: (j, 0, 0)),
        out_shape=jax.ShapeDtypeStruct((N_DEV, d, sh), F32),
        compiler_params=_cparams(("arbitrary",)),
    )(dw)


def _gate_up_swiglu(h1, wgu, *, name):
    t_rows, k = h1.shape
    w = wgu.shape[2]
    tm = _tile(t_rows, 1024)

    def body(a_ref, wg_ref, wu_ref, gu_ref, act_ref):
        a = a_ref[...].astype(BF16)
        gate = jnp.dot(a, wg_ref[...], preferred_element_type=F32)
        up = jnp.dot(a, wu_ref[...], preferred_element_type=F32)
        gu_ref[0] = gate.astype(gu_ref.dtype)
        gu_ref[1] = up.astype(gu_ref.dtype)
        act_ref[...] = (gate * _sigmoid(gate) * up).astype(act_ref.dtype)

    return pl.pallas_call(
        body, name=name, grid=(t_rows // tm, 4),
        in_specs=[pl.BlockSpec((tm, k), lambda i, j: (i, 0)),
                  pl.BlockSpec((None, k, w), lambda i, j: (j, 0, 0)),
                  pl.BlockSpec((None, k, w), lambda i, j: (j + 4, 0, 0))],
        out_specs=[pl.BlockSpec((2, None, tm, w), lambda i, j: (0, j, i, 0)),
                   pl.BlockSpec((None, tm, w), lambda i, j: (j, i, 0))],
        out_shape=[jax.ShapeDtypeStruct((2, 4, t_rows, w), BF16), jax.ShapeDtypeStruct((4, t_rows, w), BF16)],
        compiler_params=_cparams(("parallel", "arbitrary")),
    )(h1, wgu, wgu)


def _down_dx_swiglu(dffn, wdown, gu, *, name):
    t_rows, k = dffn.shape
    w = gu.shape[3]
    tm = _tile(t_rows, 1024)

    def body(d_ref, w_ref, gu_ref, o_ref):
        dact = lax.dot_general(d_ref[...].astype(BF16), w_ref[...], _NT, preferred_element_type=F32)
        gate, up = gu_ref[0].astype(F32), gu_ref[1].astype(F32)
        sg = _sigmoid(gate)
        o_ref[0] = (dact * up * (sg * (1.0 + gate * (1.0 - sg)))).astype(o_ref.dtype)
        o_ref[1] = (dact * gate * sg).astype(o_ref.dtype)

    blk = pl.BlockSpec((2, None, tm, w), lambda i, j: (0, j, i, 0))
    return pl.pallas_call(
        body, name=name, grid=(t_rows // tm, 4),
        in_specs=[pl.BlockSpec((tm, k), lambda i, j: (i, 0)), pl.BlockSpec((w, k), lambda i, j: (j, 0)), blk],
        out_specs=blk, out_shape=jax.ShapeDtypeStruct(gu.shape, BF16),
        compiler_params=_cparams(("parallel", "arbitrary")),
    )(dffn, wdown, gu)


BIG = ("w_in", "mla_w_uq", "mla_w_ukv", "w_out", "w_gate_up", "w_down", "ple_w_gate", "ple_w_proj")
SMALL = ("ln_in_g", "ln_in_b", "hgrn_lb_logits", "hgrn_norm_g", "sgu_ln_g", "sgu_ln_b", "sgu_w_s", "sgu_b_s",
         "mla_q_norm_g", "mla_kv_norm_g", "ln1_g", "ln1_b", "ln2_g", "ln2_b")
ORDER = ("ln_in_g", "ln_in_b", "w_in", "hgrn_lb_logits", "hgrn_norm_g", "sgu_ln_g", "sgu_ln_b", "sgu_w_s", "sgu_b_s",
         "mla_q_norm_g", "mla_w_uq", "mla_kv_norm_g", "mla_w_ukv", "w_out", "ln1_g", "ln1_b", "w_gate_up", "w_down",
         "ple_w_gate", "ple_w_proj", "ln2_g", "ln2_b")


def _slab(a, align):
    s = a.reshape(-1, LANES)
    pad = -s.shape[0] % align
    return jnp.pad(s, ((0, pad), (0, 0))) if pad else s


def _pack(arrays, align=16, total_align=512):
    s = jnp.concatenate([_slab(a, align) for a in arrays], axis=0)
    pad = -s.shape[0] % total_align
    return jnp.pad(s, ((0, pad), (0, 0))) if pad else s


def _unpack(slab, shapes, align=16):
    out, r0 = [], 0
    for s in shapes:
        nr = math.prod(s) // LANES
        out.append(slab[r0:r0 + nr].reshape(s))
        r0 += nr + (-nr % align)
    return out


def _blocks_to_cols(g, *, name):
    nb, a, b = g.shape

    def body(g_ref, o_ref):
        o_ref[...] = g_ref[...]

    return pl.pallas_call(
        body, name=name, grid=(nb,), in_specs=[pl.BlockSpec((None, a, b), lambda j: (j, 0, 0))],
        out_specs=pl.BlockSpec((a, b), lambda j: (0, j)), out_shape=jax.ShapeDtypeStruct((a, nb * b), g.dtype),
        compiler_params=_cparams(("parallel",)),
    )(g)


def _cols_to_blocks(x, *, name):
    a, b = x.shape[0], x.shape[1] // N_DEV

    def body(x_ref, o_ref):
        o_ref[...] = x_ref[...]

    return pl.pallas_call(
        body, name=name, grid=(N_DEV,), in_specs=[pl.BlockSpec((a, b), lambda j: (0, j))],
        out_specs=pl.BlockSpec((None, a, b), lambda j: (j, 0, 0)), out_shape=jax.ShapeDtypeStruct((N_DEV, a, b), x.dtype),
        compiler_params=_cparams(("parallel",)),
    )(x)


def _weight_shards(w, li):
    uq_pad = ((0, 0), (0, LANES - ATT_D))
    shards = {k: w[k][li] for k in BIG}
    shards["mla_w_uq"] = jnp.pad(shards["mla_w_uq"], uq_pad)
    return {k: s.astype(BF16) for k, s in shards.items()}


def _usable_weights(g, *, name):
    out = {}
    for k, a in g.items():
        if k == "w_in":
            out[k] = _place_w_in(a, name=name + "_place_w_in")
        elif k in ("w_out", "w_down", "ple_w_gate"):
            out[k] = a.reshape(a.shape[0] * a.shape[1], a.shape[2])
        elif k == "w_gate_up":
            out[k] = a
        else:
            out[k] = _blocks_to_cols(a, name=name + "_cols_" + k)
    return out


def _as_pairs(g):
    if g.ndim == 2:
        return g.reshape((4, 2, g.shape[0] // N_DEV) + g.shape[1:])
    return g.reshape((4, 2) + g.shape[1:])


def _twice(fn):
    return lambda *a: fn(*a) * 2


def _layer_forward(li, h, hb, p_i, wts, sm, lbs, tables, alpha, hgrn_job=None, more_weights=None, attn_job=None):
    n = f"l{li}_"
    row1 = lambda a: a.reshape(1, -1)
    projp = _mm(hb, wts["w_in"], name=n + "proj")
    ng = row1(sm["hgrn_norm_g"][li])
    res = _hgrn_fwd(projp, lbs[li], ng, name=n + "hgrn_fwd", job=hgrn_job)
    if hgrn_job is not None:
        res, got = res
        wts = dict(wts, **more_weights(got))
    o_a, o_pre, states = res
    lg, lbias = row1(sm["sgu_ln_g"][li]), row1(sm["sgu_ln_b"][li])
    w_s = sm["sgu_w_s"][li]
    bias_full = jnp.repeat(sm["sgu_b_s"][li].T, HEAD, axis=1)
    o_b = _sgu_fwd(projp, lg, lbias, w_s, bias_full, name=n + "sgu_fwd")
    qg, kvg = row1(sm["mla_q_norm_g"][li]), row1(sm["mla_kv_norm_g"][li])
    cq_view, ckv_view = (projp, 384, P_CQ // 384), (projp, 256, P_CKV // 256)
    (cqn,) = _rowwise(_fn_rms, [cq_view], [qg], [(384, BF16)], name=n + "q_norm")
    (ckvn,) = _rowwise(_fn_rms, [ckv_view], [kvg], [(256, BF16)], name=n + "kv_norm")
    q = _mm(cqn, wts["mla_w_uq"], name=n + "uq")
    kv = _mm(ckvn, wts["mla_w_ukv"], name=n + "ukv")
    qr, kf, kvb = _mla_prep(q, kv, projp, tables, name=n + "mla_prep")
    res, attn_got = _attn_fwd(qr, kf, kvb, name=n + "attn_fwd", job=attn_job), None
    if attn_job is not None:
        res, attn_got = res
    o_c, lse = res
    cat = jnp.concatenate([o_a, o_b, o_c.astype(BF16)], axis=1)
    mix = _mm(cat, wts["w_out"], name=n + "out_proj")
    g1, b1 = row1(sm["ln1_g"][li]), row1(sm["ln1_b"][li])
    d = h.shape[1]
    h1, h1b = _rowwise(_twice(_make_post_mix(alpha)), [h, mix], [g1, b1], [(d, F32), (d, BF16)], name=n + "ln1")
    gu, act = _gate_up_swiglu(h1b, wts["w_gate_up"], name=n + "gate_up")
    ffn = _mm(act, wts["w_down"], am="bmk", name=n + "down")
    pg = _mm(h1b, wts["ple_w_gate"], name=n + "ple_gate")
    pp = _mm(p_i, wts["ple_w_proj"], name=n + "ple_proj")
    g2, b2 = row1(sm["ln2_g"][li]), row1(sm["ln2_b"][li])
    h2, h2b = _rowwise(_twice(_make_ple_ln(alpha)), [h1, ffn, pg, pp], [g2, b2], [(d, F32), (d, BF16)],
                       name=n + "ln2")
    saved = dict(h=h, hb=hb, h1b=h1b, projp=projp, o_pre=o_pre, states=states, cqn=cqn, ckvn=ckvn, qr=qr, kf=kf, kvb=kvb, o_c=o_c,
                 lse=lse, cat=cat, mix=mix, h1=h1, gu=gu, act=act, ffn=ffn, pg=pg, pp=pp, ng=ng, lg=lg, wts=wts,
                 lbias=lbias, w_s=w_s, bias_full=bias_full, qg=qg, kvg=kvg, g1=g1, b1=b1, g2=g2, b2=b2)
    return (h2, h2b), saved, attn_got


RS_EARLY = ("ple_w_proj", "ple_w_gate", "w_down", "w_gate_up", "w_out")
RS_LATE = ("mla_w_uq", "mla_w_ukv", "w_in")


def _layer_backward(li, dh2_parts, p_i, sv, lbs, tables, alpha, core, carried=None):
    n = f"l{li}_b_"
    wts = sv["wts"]
    gr = {}
    dh1_a, dffn, dpg, dpp, gr["ln2_g"], gr["ln2_b"] = _rowwise_vjp(
        _make_ple_ln(alpha), [sv["h1"], sv["ffn"], sv["pg"], sv["pp"]], [sv["g2"], sv["b2"]], [dh2_parts],
        groups=[[0], [1], [2], [3]], gdtypes=[F32, BF16, BF16, BF16], name=n + "ln2")
    big = {}
    big["ple_w_proj"] = _cols_to_blocks(_mm(p_i, dpp, am="km", name=n + "ple_proj_dw"), name=n + "ple_proj_dw_blocks")
    big["ple_w_gate"] = _mm(sv["h1b"], dpg, am="km", name=n + "ple_gate_dw")
    dh1_b = _mm(dpg, wts["ple_w_gate"], bm="nk", name=n + "ple_gate_dx")
    big["w_down"] = _mm(sv["act"], dffn, am="bkm", name=n + "down_dw")
    dgu = _down_dx_swiglu(dffn, wts["w_down"], sv["gu"], name=n + "down_dx")
    dgu = dgu.reshape((N_DEV,) + dgu.shape[2:])
    big["w_gate_up"], carried_got = _mm(sv["h1b"], dgu, am="km", bm="bkn", om="bmn", name=n + "gate_up_dw",
                                        job=carried), None
    if carried is not None:
        big["w_gate_up"], carried_got = big["w_gate_up"]
    dh1_c = _mm(dgu, wts["w_gate_up"], am="bmk", bm="bnk", name=n + "gate_up_dx")
    dh_a, dmix, gr["ln1_g"], gr["ln1_b"] = _rowwise_vjp(
        _make_post_mix(alpha), [sv["h"], sv["mix"]], [sv["g1"], sv["b1"]], [[dh1_a, dh1_b, dh1_c]],
        groups=[[0], [1]], gdtypes=[F32, BF16], name=n + "ln1")
    big["w_out"] = _mm(sv["cat"], dmix, am="km", name=n + "out_proj_dw")
    early = [_as_pairs(big[k]) for k in RS_EARLY]
    dcat, theirs = _mm(dmix, wts["w_out"], bm="nk", name=n + "out_proj_dx", job=_pair_job(early))
    sums = [_pair_add(x, r, core, name=n + "pair_add_" + k) for k, x, r in zip(RS_EARLY, early, theirs)]

    (dqr, dkv, dkf), early_quads = _attn_bwd(sv["qr"], sv["kf"], sv["kvb"], dcat, sv["o_c"], sv["lse"],
                                             name=n + "attn", job=_quad_job(sums))
    dqpad, dkr = _mla_prep_bwd(dqr, dkf, tables, name=n + "mla_prep")
    big["mla_w_uq"] = _cols_to_blocks(_mm(sv["cqn"], dqpad, am="km", name=n + "uq_dw"), name=n + "uq_dw_blocks")
    dcqn = _mm(dqpad, wts["mla_w_uq"], bm="nk", name=n + "uq_dx")
    big["mla_w_ukv"] = _cols_to_blocks(_mm(sv["ckvn"], dkv, am="km", name=n + "ukv_dw"), name=n + "ukv_dw_blocks")
    dckvn = _mm(dkv, wts["mla_w_ukv"], bm="nk", name=n + "ukv_dx")
    projp = sv["projp"]
    dcq, gr["mla_q_norm_g"] = _rowwise_vjp(_fn_rms, [(projp, 384, P_CQ // 384)], [sv["qg"]], [[dcqn]],
                                           groups=[[0]], gdtypes=[BF16], name=n + "q_norm")
    dckv, gr["mla_kv_norm_g"] = _rowwise_vjp(_fn_rms, [(projp, 256, P_CKV // 256)], [sv["kvg"]], [[dckvn]],
                                             groups=[[0]], gdtypes=[BF16], name=n + "kv_norm")
    dsgu, gr["sgu_ln_g"], gr["sgu_ln_b"], gr["sgu_w_s"], gr["sgu_b_s"] = _sgu_bwd(
        projp, sv["lg"], sv["lbias"], sv["w_s"], sv["bias_full"], dcat, name=n + "sgu")
    dhg, gr["hgrn_norm_g"], gr["lower_bound"] = _hgrn_bwd(
        projp, lbs[li], sv["ng"], sv["o_pre"], sv["states"], dcat, name=n + "hgrn")
    dprojp = jnp.concatenate([dhg, dsgu, dcq, dkr, dckv], axis=1)
    big["w_in"] = _unplace_w_in(_mm(sv["hb"], dprojp, am="km", name=n + "proj_dw"), name=n + "proj_dw_shards")
    late = [_as_pairs(big[k]) for k in RS_LATE]
    dh_b, theirs = _mm(dprojp, wts["w_in"], bm="nk", name=n + "proj_dx", job=_pair_job(late))
    late_sums = [_pair_add(x, r, core, name=n + "pair_add_" + k) for k, x, r in zip(RS_LATE, late, theirs)]
    return [dh_a, dh_b], gr, early_quads, late_sums, carried_got


def kernel(x, p, positions, ln_in_g, ln_in_b, w_in, hgrn_lb_logits, hgrn_norm_g, sgu_ln_g, sgu_ln_b, sgu_w_s, sgu_b_s, mla_q_norm_g, mla_w_uq, mla_kv_norm_g, mla_w_ukv, w_out, ln1_g, ln1_b, w_gate_up, w_down, ple_w_gate, ple_w_proj, ln2_g, ln2_b, loss_target, m_ln_in_g, m_ln_in_b, m_w_in, m_hgrn_lb_logits, m_hgrn_norm_g, m_sgu_ln_g, m_sgu_ln_b, m_sgu_w_s, m_sgu_b_s, m_mla_q_norm_g, m_mla_w_uq, m_mla_kv_norm_g, m_mla_w_ukv, m_w_out, m_ln1_g, m_ln1_b, m_w_gate_up, m_w_down, m_ple_w_gate, m_ple_w_proj, m_ln2_g, m_ln2_b, v_ln_in_g, v_ln_in_b, v_w_in, v_hgrn_lb_logits, v_hgrn_norm_g, v_sgu_ln_g, v_sgu_ln_b, v_sgu_w_s, v_sgu_b_s, v_mla_q_norm_g, v_mla_w_uq, v_mla_kv_norm_g, v_mla_w_ukv, v_w_out, v_ln1_g, v_ln1_b, v_w_gate_up, v_w_down, v_ple_w_gate, v_ple_w_proj, v_ln2_g, v_ln2_b):
    args = dict(locals())
    w = {k: args[k] for k in ORDER}
    m = {k: args["m_" + k] for k in ORDER}
    v = {k: args["v_" + k] for k in ORDER}
    depth = w_in.shape[0]
    assert depth == 2, "the lower-bound kernel is written for two layers"
    alpha = (2 * depth) ** 0.25
    xs, tgt = x[0], loss_target[0]
    d_model = xs.shape[1]

    shards = [_weight_shards(w, li) for li in range(depth)]
    rest = [k for k in BIG if k != "w_in"]
    (g_in,) = _all_gather([shards[0]["w_in"]], name="gather_l0_w_in")
    w_in0 = _usable_weights({"w_in": g_in}, name="l0")

    def rest_of_layer0(got):
        got = _gather_forward(got, name="gather_l0_forward")
        return _usable_weights(dict(zip(rest, got)), name="l0")

    tables = _rope_tables(positions[0])
    row1 = lambda a: a.reshape(1, -1)
    l0, l1 = row1(hgrn_lb_logits[0]), row1(hgrn_lb_logits[1])
    lbs = _rowwise(_fn_lower_bounds, [l0, l1], [], [(HG_W, F32), (HG_W, F32)], name="lower_bounds")

    gin, bin_ = row1(ln_in_g), row1(ln_in_b)
    h, hb = _rowwise(_twice(_fn_ln), [xs], [gin, bin_], [(d_model, F32), (d_model, BF16)], name="ln_in")
    (h, hb), sv0, got1 = _layer_forward(0, h, hb, p[0, 0], w_in0, w, lbs, tables, alpha,
                                  hgrn_job=_gather_job([shards[0][k] for k in rest]), more_weights=rest_of_layer0,
                                  attn_job=_gather_job([shards[1][k] for k in BIG]))
    wts1 = _usable_weights(dict(zip(BIG, _gather_forward(got1, name="gather_l1_forward"))), name="l1")
    (h, _), sv1, _ = _layer_forward(1, h, hb, p[1, 0], wts1, w, lbs, tables, alpha)
    saved = [sv0, sv1]
    dy, loss_local = _loss_and_grad(h, tgt, name="loss")
    loss = lax.psum(loss_local[0, 0], ("x", "y", "c"))

    core = lax.axis_index("c").astype(jnp.int32).reshape(1)
    dparts, grads, quads, carried = [dy], [None] * depth, [None] * depth, None
    for li in reversed(range(depth)):
        dparts, grads[li], early_quads, late_sums, late_quads = _layer_backward(
            li, dparts, p[li, 0], saved[li], lbs, tables, alpha, core, carried=carried)
        quads[li] = dict(zip(RS_EARLY, early_quads))
        if carried is not None:
            quads[li + 1].update(zip(RS_LATE, late_quads))
        carried = _quad_job(late_sums)
    quads[0].update(zip(RS_LATE, _run_job(carried, name="rs_l0_late_quad")))
    dx, d_gin, d_bin = _rowwise_vjp(_fn_ln, [xs], [gin, bin_], [dparts], groups=[[0]], name="ln_in_b")
    dl0, dl1 = _rowwise_vjp(_fn_lower_bounds, [l0, l1], [], [[grads[0]["lower_bound"]], [grads[1]["lower_bound"]]],
                            groups=[[0], [1]], name="lower_bounds_b")

    prefixes = ("grad_", "delta_", "new_m_", "new_v_")
    per_layer = {pre + k: [] for pre in prefixes for k in BIG}
    uq_pad = ((0, 0), (0, 0), (0, LANES - ATT_D))
    state = {k: ((jnp.pad(w[k], uq_pad), jnp.pad(m[k], uq_pad), jnp.pad(v[k], uq_pad)) if k == "mla_w_uq"
                 else (w[k], m[k], v[k])) for k in BIG}
    for li in range(depth):
        for k in BIG:
            res4 = _adamw(quads[li][k], *state[k], li, name=f"adamw_l{li}_{k}")
            for pre, a in zip(prefixes, res4):
                per_layer[pre + k].append(a[:, :ATT_D] if k == "mla_w_uq" else a)
    out = {name: jnp.stack(vals) for name, vals in per_layer.items()}

    small_g = {"ln_in_g": d_gin.reshape(-1), "ln_in_b": d_bin.reshape(-1),
               "hgrn_lb_logits": jnp.stack([dl0.reshape(-1), dl1.reshape(-1)])}
    for k in SMALL[3:]:
        small_g[k] = jnp.stack([grads[li][k].reshape(w[k].shape[1:]) for li in range(depth)])
    (small_parts,) = _all_gather([_pack([small_g[k] for k in SMALL])], name="gather_small_grads")
    slabs = _adamw(small_parts, _pack([w[k] for k in SMALL]), _pack([m[k] for k in SMALL]),
                   _pack([v[k] for k in SMALL]), None, name="adamw_small")
    shapes = [w[k].shape for k in SMALL]
    for pre, slab in zip(prefixes, slabs):
        for k, a in zip(SMALL, _unpack(slab, shapes)):
            out[pre + k] = a
    res = [loss, dx[None]]
    for prefix in ("grad_", "delta_", "new_m_", "new_v_"):
        res += [out[prefix + k] for k in ORDER]
    return tuple(res)
```

```python
import functools
import math

import jax
import jax.numpy as jnp
from jax import lax
from jax.experimental import pallas as pl
from jax.experimental.pallas import tpu as pltpu

F32 = jnp.float32
BF16 = jnp.bfloat16
MESH = pl.DeviceIdType.MESH

LN_EPS = 1e-5
RMS_EPS = 1e-6
ROPE_THETA = 10000.0
ADAM_LR, ADAM_B1, ADAM_B2, ADAM_EPS, ADAM_WD, ADAM_STEP = 0.001, 0.9, 0.999, 1e-08, 0.01, 10

N_DEV = 8
LANES = 128
HG_CHUNK = 16
HG_W = 256
HEAD = 64
SGU_CHUNK = 128
N_ATT_HEADS = 8
ATT_D = 96
VMEM_LIMIT = 56 * 1024 * 1024

HG_TILE = 256
ATT_TQ = 256
ROW_TILE = 256

P_CQ, P_KR, P_CKV, P_COLS = 1536, 1920, 2048, 2304


def _cparams(sem):
    return pltpu.CompilerParams(dimension_semantics=sem, vmem_limit_bytes=VMEM_LIMIT)


_ANY = pl.BlockSpec(memory_space=pl.ANY)


def _call(body, operands, *, name, grid, in_specs, out_specs, out_shape, sem, scratch_shapes=(), job=None):
    if job is None:
        return pl.pallas_call(body, name=name, grid=grid, in_specs=in_specs, out_specs=out_specs, out_shape=out_shape,
                              scratch_shapes=list(scratch_shapes), compiler_params=_cparams(sem))(*operands)
    single = not isinstance(out_shape, (list, tuple))
    shapes = [out_shape] if single else list(out_shape)
    ospecs = [out_specs] if single else list(out_specs)
    ni, no, ns = len(operands), len(shapes), len(scratch_shapes)
    ji, jo = len(job.inputs), len(job.out_shapes)

    def hosted(*refs):
        p = 0
        parts = []
        for cnt in (ni, ji, no, jo, ns):
            parts.append(refs[p:p + cnt])
            p += cnt
        ins, jins, outs, jouts, scr = parts
        jsems = refs[p:]
        ids = [pl.program_id(a) for a in range(len(grid))]
        first = functools.reduce(lambda a, b: a & b, [i == 0 for i in ids])
        last = functools.reduce(lambda a, b: a & b, [i == g - 1 for i, g in zip(ids, grid)])

        @pl.when(first)
        def _():
            job.start(jins, jouts, jsems)

        body(*ins, *outs, *scr)

        @pl.when(last)
        def _():
            job.finish(jins, jouts, jsems)

    res = pl.pallas_call(
        hosted, name=name, grid=grid,
        in_specs=list(in_specs) + [_ANY] * ji, out_specs=ospecs + [_ANY] * jo,
        out_shape=shapes + list(job.out_shapes),
        scratch_shapes=list(scratch_shapes) + [pltpu.SemaphoreType.DMA((c,)) for c in job.sem_counts],
        compiler_params=_cparams(("arbitrary",) * len(grid)),
    )(*operands, *job.inputs)
    own = res[0] if single else res[:no]
    return own, res[no:]


class _Job:
    def __init__(self, inputs, out_shapes, sem_counts, start, finish):
        self.inputs, self.out_shapes, self.sem_counts = list(inputs), list(out_shapes), list(sem_counts)
        self.start, self.finish = start, finish


def _copies_job(inputs, out_shapes, n_remote, n_local, make):
    def start(jins, jouts, sems):
        sends, _, local = make(jins, jouts, *sems)
        for cp in local + sends:
            cp.start()

    def finish(jins, jouts, sems):
        sends, recvs, local = make(jins, jouts, *sems)
        for cp in recvs:
            cp.wait_recv()
        for cp in sends:
            cp.wait_send()
        for cp in local:
            cp.wait()

    return _Job(inputs, out_shapes, [n_remote, n_remote, max(n_local, 1)], start, finish)


def _run_job(job, *, name):
    ji, jo = len(job.inputs), len(job.out_shapes)

    def body(*refs):
        jins, jouts, sems = refs[:ji], refs[ji:ji + jo], refs[ji + jo:]
        job.start(jins, jouts, sems)
        job.finish(jins, jouts, sems)

    return pl.pallas_call(
        body, name=name, out_shape=list(job.out_shapes), in_specs=[_ANY] * ji, out_specs=[_ANY] * jo,
        scratch_shapes=[pltpu.SemaphoreType.DMA((c,)) for c in job.sem_counts],
    )(*job.inputs)


def _tile(n, pref):
    if n % pref == 0:
        return pref
    best = None
    t = LANES
    while t <= min(n, pref):
        if n % t == 0:
            best = t
        t += LANES
    return best if best is not None else n


def _mm(a, b, *, am="mk", bm="kn", om="mn", out_dtype=F32, tm=1024, tn=1024, tk=1024, name, job=None):
    if am == "mk":
        m, k = a.shape
    elif am == "km":
        k, m = a.shape
    elif am == "bmk":
        m, tk = a.shape[1], a.shape[2]
        k = a.shape[0] * tk
    else:
        k, tm = a.shape[1], a.shape[2]
        m = a.shape[0] * tm
    if bm == "kn":
        kb_, n = b.shape
    elif bm == "nk":
        n, kb_ = b.shape
    elif bm == "bkn":
        kb_, tn = b.shape[1], b.shape[2]
        n = b.shape[0] * tn
    else:
        n, tk = b.shape[1], b.shape[2]
        kb_ = b.shape[0] * tk
    assert kb_ == k, (a.shape, b.shape, am, bm)
    tm, tn, tk = _tile(m, tm), _tile(n, tn), _tile(k, tk)
    nk = k // tk
    dims = (((0 if am in ("km", "bkm") else 1,), (1 if bm in ("nk", "bnk") else 0,)), ((), ()))

    a_spec = {"mk": pl.BlockSpec((tm, tk), lambda i, j, kk: (i, kk)),
              "km": pl.BlockSpec((tk, tm), lambda i, j, kk: (kk, i)),
              "bmk": pl.BlockSpec((None, tm, tk), lambda i, j, kk: (kk, i, 0)),
              "bkm": pl.BlockSpec((None, tk, tm), lambda i, j, kk: (i, kk, 0))}[am]
    b_spec = {"kn": pl.BlockSpec((tk, tn), lambda i, j, kk: (kk, j)),
              "nk": pl.BlockSpec((tn, tk), lambda i, j, kk: (j, kk)),
              "bkn": pl.BlockSpec((None, tk, tn), lambda i, j, kk: (j, kk, 0)),
              "bnk": pl.BlockSpec((None, tn, tk), lambda i, j, kk: (kk, j, 0))}[bm]
    if om == "mn":
        o_spec, o_shape = pl.BlockSpec((tm, tn), lambda i, j, kk: (i, j)), (m, n)
    else:
        o_spec, o_shape = pl.BlockSpec((None, tm, tn), lambda i, j, kk: (j, i, 0)), (n // tn, m, tn)

    def body(a_ref, b_ref, o_ref, *acc):
        kk = pl.program_id(2)
        prod = lax.dot_general(a_ref[...].astype(BF16), b_ref[...].astype(BF16), dims, preferred_element_type=F32)
        if nk == 1:
            o_ref[...] = prod.astype(o_ref.dtype)
            return
        acc_ref, = acc

        @pl.when(kk == 0)
        def _():
            acc_ref[...] = prod

        if nk > 2:
            @pl.when((kk > 0) & (kk < nk - 1))
            def _():
                acc_ref[...] += prod

        @pl.when(kk == nk - 1)
        def _():
            o_ref[...] = (acc_ref[...] + prod).astype(o_ref.dtype)

    return _call(body, (a, b), name=name, grid=(m // tm, n // tn, nk), in_specs=[a_spec, b_spec], out_specs=o_spec,
                 out_shape=jax.ShapeDtypeStruct(o_shape, out_dtype),
                 scratch_shapes=[pltpu.VMEM((tm, tn), F32)] if nk > 1 else [],
                 sem=("parallel", "parallel", "arbitrary"), job=job)


def _row_operand(a, tile):
    if isinstance(a, tuple):
        arr, w, j = a
        return arr, pl.BlockSpec((tile, w), lambda i, j=j: (i, j))
    return a, pl.BlockSpec((tile, a.shape[1]), lambda i: (i, 0))


def _const_spec(c):
    nd = c.ndim
    return pl.BlockSpec(c.shape, lambda i, nd=nd: (0,) * nd)


def _rowwise(fn, rows, consts, outs, *, name, accs=(), tile=None):
    t_rows = (rows[0][0] if isinstance(rows[0], tuple) else rows[0]).shape[0]
    tile = min(tile or ROW_TILE, t_rows)
    arrs, specs = zip(*[_row_operand(a, tile) for a in rows])
    nin, no = len(rows) + len(consts), len(outs)

    def body(*refs):
        res = fn(*[r[...] for r in refs[:nin]])
        for r, v in zip(refs[nin:nin + no], res[:no]):
            r[...] = v.astype(r.dtype)
        if accs:
            a_refs = refs[nin + no:]

            @pl.when(pl.program_id(0) == 0)
            def _():
                for r in a_refs:
                    r[...] = jnp.zeros_like(r)

            for r, v in zip(a_refs, res[no:]):
                r[...] += v

    out_shape = [jax.ShapeDtypeStruct((t_rows, w), dt) for w, dt in outs]
    out_shape += [jax.ShapeDtypeStruct(s, F32) for s in accs]
    out_specs = [pl.BlockSpec((tile, w), lambda i: (i, 0)) for w, _ in outs]
    out_specs += [pl.BlockSpec(s, lambda i, nd=len(s): (0,) * nd) for s in accs]
    return pl.pallas_call(
        body, name=name, grid=(t_rows // tile,),
        in_specs=list(specs) + [_const_spec(c) for c in consts],
        out_specs=out_specs, out_shape=out_shape,
        compiler_params=_cparams(("arbitrary",)),
    )(*arrs, *consts)


def _rowwise_vjp(fn, rows, consts, cts, *, name, groups, tile=None, gdtypes=None):
    t_rows = (rows[0][0] if isinstance(rows[0], tuple) else rows[0]).shape[0]
    tile = min(tile or ROW_TILE, t_rows)
    arrs, specs = zip(*[_row_operand(a, tile) for a in rows])
    flat_cts = [c for group in cts for c in group]
    ct_arrs, ct_specs = zip(*[_row_operand(a, tile) for a in flat_cts])
    nr, nc, nct, ng = len(rows), len(consts), len(flat_cts), len(groups)

    def width(a):
        return a[1] if isinstance(a, tuple) else a.shape[1]

    def body(*refs):
        rv = [r[...].astype(F32) for r in refs[:nr]]
        cv = [r[...] for r in refs[nr:nr + nc]]
        ct_refs = refs[nr + nc:nr + nc + nct]
        ctv, pos = [], 0
        for group in cts:
            s = ct_refs[pos][...].astype(F32)
            for r in ct_refs[pos + 1:pos + len(group)]:
                s = s + r[...].astype(F32)
            ctv.append(s)
            pos += len(group)
        _, pull = jax.vjp(fn, *rv, *cv)
        grads = pull(tuple(ctv))
        g_refs = refs[nr + nc + nct:nr + nc + nct + ng]
        for r, idx in zip(g_refs, groups):
            parts = [grads[i] for i in idx]
            r[...] = (parts[0] if len(parts) == 1 else jnp.concatenate(parts, axis=1)).astype(r.dtype)
        c_refs = refs[nr + nc + nct + ng:]

        @pl.when(pl.program_id(0) == 0)
        def _():
            for r in c_refs:
                r[...] = jnp.zeros_like(r)

        for r, v in zip(c_refs, grads[nr:]):
            r[...] += v

    gw = [sum(width(rows[i]) for i in idx) for idx in groups]
    gdtypes = gdtypes or [F32] * ng
    out_shape = [jax.ShapeDtypeStruct((t_rows, w), dt) for w, dt in zip(gw, gdtypes)]
    out_shape += [jax.ShapeDtypeStruct(c.shape, F32) for c in consts]
    out_specs = [pl.BlockSpec((tile, w), lambda i: (i, 0)) for w in gw]
    out_specs += [_const_spec(c) for c in consts]
    return pl.pallas_call(
        body, name=name, grid=(t_rows // tile,),
        in_specs=list(specs) + [_const_spec(c) for c in consts] + list(ct_specs),
        out_specs=out_specs, out_shape=out_shape,
        compiler_params=_cparams(("arbitrary",)),
    )(*arrs, *consts, *ct_arrs)


def _layer_norm(x, g, b):
    mu = jnp.mean(x, axis=-1, keepdims=True)
    xc = x - mu
    var = jnp.mean(xc * xc, axis=-1, keepdims=True)
    return xc * lax.rsqrt(var + LN_EPS) * g + b


def _sigmoid(x):
    return 1.0 / (1.0 + jnp.exp(-x))


def _fn_ln(x, g, b):
    return (_layer_norm(x, g, b),)


def _fn_rms(x, g):
    return (x * lax.rsqrt(jnp.mean(x * x, axis=-1, keepdims=True) + RMS_EPS) * g,)


def _make_post_mix(alpha):
    def fn(h, mix, g, b):
        return (_layer_norm(alpha * h + mix, g, b),)
    return fn


def _make_ple_ln(alpha):
    def fn(h1, ffn, pg, pp, g, b):
        return (_layer_norm(alpha * h1 + ffn + _sigmoid(pg) * pp, g, b),)
    return fn


def _fn_lower_bounds(l0, l1):
    m = jnp.maximum(l0, l1)
    e0, e1 = jnp.exp(l0 - m), jnp.exp(l1 - m)
    s = e0 + e1
    p0, p1 = e0 / s, e1 / s
    return (p0 - p0, (p0 + p1) - p0)


def _loss_and_grad(y, target, *, name):
    d = y.shape[1]

    def fn(yv, tv):
        err = yv - tv
        return err * (1.0 / d), 0.5 * jnp.sum(jnp.mean(err * err, axis=-1, keepdims=True), axis=0, keepdims=True)

    return _rowwise(fn, [y, target], [], [(d, F32)], accs=[(1, 1)], name=name)


def _split_dot(x, e_bf16):
    hi = x.astype(BF16)
    lo = (x - hi.astype(F32)).astype(BF16)
    return (jnp.dot(hi, e_bf16, preferred_element_type=F32) + jnp.dot(lo, e_bf16, preferred_element_type=F32))


def _hgrn_common(th):
    rm = lax.broadcasted_iota(jnp.int32, (th, HG_W), 0) % HG_CHUNK

    def seg_cumsum(x):
        for s in (1, 2, 4, 8):
            x = x + jnp.where(rm >= s, pltpu.roll(x, s, 0), 0.0)
        return x

    def seg_rcumsum(x):
        for s in (1, 2, 4, 8):
            x = x + jnp.where(rm < HG_CHUNK - s, pltpu.roll(x, th - s, 0), 0.0)
        return x

    ri = lax.broadcasted_iota(jnp.int32, (HG_W, HG_W), 0) // HEAD
    ci = lax.broadcasted_iota(jnp.int32, (HG_W, HG_W), 1) // HEAD
    head_f32 = (ri == ci).astype(F32)
    head_bf16 = head_f32.astype(BF16)

    def headsum(x):
        return _split_dot(x, head_bf16)

    return rm, seg_cumsum, seg_rcumsum, head_f32, headsum


def _hgrn_gates(qr, fl, lb):
    sg = _sigmoid(fl)
    f = lb + (1.0 - lb) * sg
    sq = _sigmoid(qr)
    return sg, f, jnp.log(f), 1.0 - f, qr * sq, sq


def _shifted(x, d, th):
    return x if d == 0 else pltpu.roll(x, d, 0)


def _unshift(x, d, th):
    return x if d == 0 else pltpu.roll(x, th - d, 0)


def _hgrn_fwd(projp, lb, ng, *, name, job=None):
    t_rows = projp.shape[0]
    th = min(HG_TILE, t_rows)
    nct = th // HG_CHUNK

    def body(q_ref, f_ref, i_ref, g_ref, lb_ref, ng_ref, oa_ref, opre_ref, st_out_ref,
             st_ref, vtm_ref, kv_ref, qe_ref, dec_ref, oint_ref):
        rm, seg_cumsum, seg_rcumsum, head_f32, headsum = _hgrn_common(th)

        @pl.when(pl.program_id(0) == 0)
        def _():
            st_ref[...] = jnp.zeros_like(st_ref)

        qr, fl, v, g = q_ref[...], f_ref[...], i_ref[...], g_ref[...]
        _, f, lf, k, q, _ = _hgrn_gates(qr, fl, lb_ref[...])
        b = seg_cumsum(lf)

        o = jnp.zeros((th, HG_W), F32)
        for d in range(HG_CHUNK):
            kd, bd, vd = _shifted(k, d, th), _shifted(b, d, th), _shifted(v, d, th)
            e = jnp.exp(jnp.where(rm >= d, b - bd, -1e30))
            o = o + headsum(q * kd * e) * vd

        blast = seg_rcumsum(jnp.where(rm == HG_CHUNK - 1, b, 0.0))
        kte = (k * jnp.exp(blast - b)).astype(BF16)
        qe_ref[...] = q * jnp.exp(b)
        dec_ref[...] = jnp.exp(blast)
        vt = v.T
        lane_chunk = lax.broadcasted_iota(jnp.int32, (HG_W, th), 1) // HG_CHUNK
        for c in range(nct):
            vtm_ref[c * HG_W:(c + 1) * HG_W, :] = jnp.where(lane_chunk == c, vt, 0.0).astype(BF16)
        kv_ref[...] = jnp.dot(vtm_ref[...], kte, preferred_element_type=F32)

        s = st_ref[...]
        for c in range(nct):
            rows = slice(c * HG_CHUNK, (c + 1) * HG_CHUNK)
            st_out_ref[c] = s
            oint_ref[rows, :] = lax.dot_general(qe_ref[rows, :].astype(BF16), s.astype(BF16),
                                                (((1,), (1,)), ((), ())), preferred_element_type=F32)
            dec = jnp.max(dec_ref[rows, :], axis=0, keepdims=True)
            s = s * dec + kv_ref[c * HG_W:(c + 1) * HG_W, :] * head_f32
        st_ref[...] = s

        o = o + oint_ref[...]
        opre_ref[...] = o
        r = lax.rsqrt(headsum(o * o) * (1.0 / HEAD) + RMS_EPS)
        oa_ref[...] = (o * r * ng_ref[...] * (g * _sigmoid(g))).astype(oa_ref.dtype)

    col = lambda j: pl.BlockSpec((th, HG_W), lambda i, j=j: (i, j))
    vec = pl.BlockSpec((1, HG_W), lambda i: (0, 0))
    row = pl.BlockSpec((th, HG_W), lambda i: (i, 0))
    n_chunks = t_rows // HG_CHUNK
    return _call(
        body, (projp, projp, projp, projp, lb, ng), name=name, grid=(t_rows // th,),
        in_specs=[col(0), col(1), col(2), col(3), vec, vec],
        out_specs=[row, row, pl.BlockSpec((nct, HG_W, HG_W), lambda i: (i, 0, 0))],
        out_shape=[jax.ShapeDtypeStruct((t_rows, HG_W), BF16), jax.ShapeDtypeStruct((t_rows, HG_W), F32),
                   jax.ShapeDtypeStruct((n_chunks, HG_W, HG_W), F32)],
        scratch_shapes=[pltpu.VMEM((HG_W, HG_W), F32), pltpu.VMEM((nct * HG_W, th), BF16),
                        pltpu.VMEM((nct * HG_W, HG_W), F32), pltpu.VMEM((th, HG_W), F32),
                        pltpu.VMEM((th, HG_W), F32), pltpu.VMEM((th, HG_W), F32)],
        sem=("arbitrary",), job=job)


def _hgrn_bwd(projp, lb, ng, opre, states, dcat, *, name):
    t_rows = projp.shape[0]
    th = min(HG_TILE, t_rows)
    nct = th // HG_CHUNK
    nt = t_rows // th

    def body(q_ref, f_ref, i_ref, g_ref, lb_ref, ng_ref, opre_ref, st_in_ref, do_ref,
             dproj_ref, dng_ref, dlb_ref,
             gst_ref, dotm_ref, qg_ref, v_ref, kte_ref, dop_ref, dec_ref, dkte_ref, dvi_ref, dqe_ref, ddec_ref):
        rm, seg_cumsum, seg_rcumsum, head_f32, headsum = _hgrn_common(th)

        @pl.when(pl.program_id(0) == 0)
        def _():
            gst_ref[...] = jnp.zeros_like(gst_ref)
            dng_ref[...] = jnp.zeros_like(dng_ref)
            dlb_ref[...] = jnp.zeros_like(dlb_ref)

        qr, fl, v, g = q_ref[...], f_ref[...], i_ref[...], g_ref[...]
        lb, ngv = lb_ref[...], ng_ref[...]
        sg, f, lf, k, q, sq = _hgrn_gates(qr, fl, lb)
        b = seg_cumsum(lf)
        blast = seg_rcumsum(jnp.where(rm == HG_CHUNK - 1, b, 0.0))
        eb = jnp.exp(b)
        ekb = jnp.exp(blast - b)
        qe, kte, dec = q * eb, k * ekb, jnp.exp(blast)

        do_out, op = do_ref[...], opre_ref[...]
        sgg = _sigmoid(g)
        sil = g * sgg
        r = lax.rsqrt(headsum(op * op) * (1.0 / HEAD) + RMS_EPS)
        on = op * r
        dng_ref[...] += jnp.sum(do_out * on * sil, axis=0, keepdims=True)
        dg = do_out * on * ngv * (sgg * (1.0 + g * (1.0 - sgg)))
        don = do_out * ngv * sil
        dop = r * (don - on * (headsum(don * on) * (1.0 / HEAD)))

        v_ref[...] = v
        kte_ref[...] = kte
        dop_ref[...] = dop
        dec_ref[...] = dec
        dot_t = dop.T
        lane_chunk = lax.broadcasted_iota(jnp.int32, (HG_W, th), 1) // HG_CHUNK
        for c in range(nct):
            dotm_ref[c * HG_W:(c + 1) * HG_W, :] = jnp.where(lane_chunk == c, dot_t, 0.0).astype(BF16)
        qg_ref[...] = jnp.dot(dotm_ref[...], qe.astype(BF16), preferred_element_type=F32)

        gs = gst_ref[...]
        for c in reversed(range(nct)):
            rows = slice(c * HG_CHUNK, (c + 1) * HG_CHUNK)
            s = st_in_ref[c]
            gm = (gs * head_f32).astype(BF16)
            dkte_ref[rows, :] = jnp.dot(v_ref[rows, :].astype(BF16), gm, preferred_element_type=F32)
            dvi_ref[rows, :] = lax.dot_general(kte_ref[rows, :].astype(BF16), gm, (((1,), (1,)), ((), ())),
                                               preferred_element_type=F32)
            dqe_ref[rows, :] = jnp.dot(dop_ref[rows, :].astype(BF16), s.astype(BF16), preferred_element_type=F32)
            ddec_ref[rows, :] = jnp.broadcast_to(jnp.sum(gs * s, axis=0, keepdims=True), (HG_CHUNK, HG_W))
            dec_c = jnp.max(dec_ref[rows, :], axis=0, keepdims=True)
            gs = gs * dec_c + qg_ref[c * HG_W:(c + 1) * HG_W, :] * head_f32
        gst_ref[...] = gs

        dkte, dqe = dkte_ref[...], dqe_ref[...]
        dq = dqe * eb
        dk = dkte * ekb
        db = dqe * qe - dkte * kte
        dv = dvi_ref[...]
        dblast = dkte * kte + jnp.where(rm == HG_CHUNK - 1, ddec_ref[...] * dec, 0.0)

        for d in range(HG_CHUNK):
            kd, bd, vd = _shifted(k, d, th), _shifted(b, d, th), _shifted(v, d, th)
            e = jnp.exp(jnp.where(rm >= d, b - bd, -1e30))
            p = q * kd * e
            sc = headsum(p)
            dsc = headsum(dop * vd)
            dv = dv + _unshift(sc * dop, d, th)
            dq = dq + dsc * kd * e
            dk = dk + _unshift(dsc * q * e, d, th)
            darg = dsc * p
            db = db + darg - _unshift(darg, d, th)

        db = db + jnp.where(rm == HG_CHUNK - 1, seg_cumsum(dblast), 0.0)
        dlf = seg_rcumsum(db)
        df = dlf / f - dk
        dlb_ref[...] += jnp.sum(df * (1.0 - sg), axis=0, keepdims=True)
        dfl = df * (1.0 - lb) * sg * (1.0 - sg)
        dqr = dq * (sq * (1.0 + qr * (1.0 - sq)))
        dproj_ref[...] = jnp.concatenate([dqr, dfl, dv, dg], axis=1).astype(dproj_ref.dtype)

    rev = lambda i: nt - 1 - i
    col = lambda j: pl.BlockSpec((th, HG_W), lambda i, j=j: (rev(i), j))
    vec = pl.BlockSpec((1, HG_W), lambda i: (0, 0))
    row = pl.BlockSpec((th, HG_W), lambda i: (rev(i), 0))
    tile_f32 = pltpu.VMEM((th, HG_W), F32)
    return pl.pallas_call(
        body, name=name, grid=(nt,),
        in_specs=[col(0), col(1), col(2), col(3), vec, vec, row,
                  pl.BlockSpec((nct, HG_W, HG_W), lambda i: (rev(i), 0, 0)), col(0)],
        out_specs=[pl.BlockSpec((th, 4 * HG_W), lambda i: (rev(i), 0)), vec, vec],
        out_shape=[jax.ShapeDtypeStruct((t_rows, 4 * HG_W), BF16), jax.ShapeDtypeStruct((1, HG_W), F32),
                   jax.ShapeDtypeStruct((1, HG_W), F32)],
        scratch_shapes=[pltpu.VMEM((HG_W, HG_W), F32), pltpu.VMEM((nct * HG_W, th), BF16),
                        pltpu.VMEM((nct * HG_W, HG_W), F32)] + [tile_f32] * 8,
        compiler_params=_cparams(("arbitrary",)),
    )(projp, projp, projp, projp, lb, ng, opre, states, dcat)


_INV_SQRT2 = 1.0 / math.sqrt(2.0)
_INV_SQRT2PI = 1.0 / math.sqrt(2.0 * math.pi)


def _gelu(x):
    return 0.5 * x * (1.0 + lax.erf(x * _INV_SQRT2))


def _gelu_grad(x):
    return 0.5 * (1.0 + lax.erf(x * _INV_SQRT2)) + x * jnp.exp(-0.5 * x * x) * _INV_SQRT2PI


def _sgu_parts(bu, bv, lg, lbias, w_ref, n_groups):
    c = SGU_CHUNK
    tril = (lax.broadcasted_iota(jnp.int32, (c, c), 0) >= lax.broadcasted_iota(jnp.int32, (c, c), 1)).astype(F32)
    gid = lax.broadcasted_iota(jnp.int32, bu.shape, 1) // HEAD
    u = _gelu(bu)
    gv = _gelu(bv)
    mu = jnp.mean(gv, axis=-1, keepdims=True)
    xc = gv - mu
    rstd = lax.rsqrt(jnp.mean(xc * xc, axis=-1, keepdims=True) + LN_EPS)
    xhat = xc * rstd
    vn = xhat * lg + lbias
    ws = [w_ref[gi] * tril for gi in range(n_groups)]
    return tril, gid, u, rstd, xhat, vn, ws


def _sgu_fwd(projp, lg, lbias, w_s, bias_full, *, name):
    t_rows = projp.shape[0]
    n_groups = w_s.shape[0]
    c = SGU_CHUNK

    def body(u_ref, v_ref, lg_ref, lb_ref, w_ref, bias_ref, o_ref):
        _, gid, u, _, _, vn, ws = _sgu_parts(u_ref[...], v_ref[...], lg_ref[...], lb_ref[...], w_ref, n_groups)
        vnb = vn.astype(BF16)
        z = bias_ref[...]
        for gi in range(n_groups):
            z = z + jnp.where(gid == gi, jnp.dot(ws[gi].astype(BF16), vnb, preferred_element_type=F32), 0.0)
        o_ref[...] = (u * z).astype(o_ref.dtype)

    col = lambda j: pl.BlockSpec((c, HG_W), lambda i, j=j: (i, j))
    return pl.pallas_call(
        body, name=name, grid=(t_rows // c,),
        in_specs=[col(4), col(5), _const_spec(lg), _const_spec(lbias), _const_spec(w_s), _const_spec(bias_full)],
        out_specs=pl.BlockSpec((c, HG_W), lambda i: (i, 0)),
        out_shape=jax.ShapeDtypeStruct((t_rows, HG_W), BF16),
        compiler_params=_cparams(("arbitrary",)),
    )(projp, projp, lg, lbias, w_s, bias_full)


def _sgu_bwd(projp, lg, lbias, w_s, bias_full, dcat, *, name):
    t_rows = projp.shape[0]
    n_groups = w_s.shape[0]
    c = SGU_CHUNK
    n = t_rows // c

    def body(u_ref, v_ref, lg_ref, lb_ref, w_ref, bias_ref, do_ref,
             dproj_ref, dlg_ref, dlb_ref, dw_ref, dbs_ref, dbias_acc):
        i = pl.program_id(0)

        @pl.when(i == 0)
        def _():
            dlg_ref[...] = jnp.zeros_like(dlg_ref)
            dlb_ref[...] = jnp.zeros_like(dlb_ref)
            dw_ref[...] = jnp.zeros_like(dw_ref)
            dbias_acc[...] = jnp.zeros_like(dbias_acc)

        bu, bv, lg_v = u_ref[...], v_ref[...], lg_ref[...]
        tril, gid, u, rstd, xhat, vn, ws = _sgu_parts(bu, bv, lg_v, lb_ref[...], w_ref, n_groups)
        vnb = vn.astype(BF16)
        z = bias_ref[...]
        for gi in range(n_groups):
            z = z + jnp.where(gid == gi, jnp.dot(ws[gi].astype(BF16), vnb, preferred_element_type=F32), 0.0)
        do = do_ref[...]
        dbu = do * z * _gelu_grad(bu)
        dz = do * u
        dbias_acc[...] += dz
        dvn = jnp.zeros_like(dz)
        for gi in range(n_groups):
            dzg = jnp.where(gid == gi, dz, 0.0).astype(BF16)
            dw_ref[gi] += lax.dot_general(dzg, vnb, (((1,), (1,)), ((), ())), preferred_element_type=F32) * tril
            dvn = dvn + jnp.dot(ws[gi].T.astype(BF16), dzg, preferred_element_type=F32)
        dlg_ref[...] += jnp.sum(dvn * xhat, axis=0, keepdims=True)
        dlb_ref[...] += jnp.sum(dvn, axis=0, keepdims=True)
        dxh = dvn * lg_v
        dgv = rstd * (dxh - jnp.mean(dxh, axis=-1, keepdims=True)
                      - xhat * jnp.mean(dxh * xhat, axis=-1, keepdims=True))
        dproj_ref[...] = jnp.concatenate([dbu, dgv * _gelu_grad(bv)], axis=1).astype(dproj_ref.dtype)

        @pl.when(i == n - 1)
        def _():
            dbs_ref[...] = jnp.sum(dbias_acc[...].T.reshape(n_groups, HEAD, c), axis=1)

    col = lambda j: pl.BlockSpec((c, HG_W), lambda i, j=j: (i, j))
    return pl.pallas_call(
        body, name=name, grid=(n,),
        in_specs=[col(4), col(5), _const_spec(lg), _const_spec(lbias), _const_spec(w_s), _const_spec(bias_full),
                  col(1)],
        out_specs=[pl.BlockSpec((c, 2 * HG_W), lambda i: (i, 0)), _const_spec(lg), _const_spec(lbias),
                   _const_spec(w_s), pl.BlockSpec((n_groups, c), lambda i: (0, 0))],
        out_shape=[jax.ShapeDtypeStruct((t_rows, 2 * HG_W), BF16), jax.ShapeDtypeStruct(lg.shape, F32),
                   jax.ShapeDtypeStruct(lbias.shape, F32), jax.ShapeDtypeStruct(w_s.shape, F32),
                   jax.ShapeDtypeStruct((n_groups, c), F32)],
        scratch_shapes=[pltpu.VMEM((c, HG_W), F32)],
        compiler_params=_cparams(("arbitrary",)),
    )(projp, projp, lg, lbias, w_s, bias_full, dcat)


def _rope_tables(positions):
    t = positions.shape[0]
    inv_freq = ROPE_THETA ** (-jnp.arange(0, 32, 2, dtype=F32) / 32)
    ang = positions.astype(F32)[:, None] * inv_freq
    cos, sin = jnp.cos(ang), jnp.sin(ang)
    z = lambda w: jnp.zeros((t, w), F32)
    cos_t = jnp.concatenate([jnp.ones((t, 64), F32), cos, cos, z(32)], axis=1)
    sin_up = jnp.concatenate([z(80), sin, z(32)], axis=1)
    sin_dn = jnp.concatenate([z(64), -sin, z(48)], axis=1)
    return cos_t, sin_up, sin_dn


def _rep(x, n):
    return x if n == 1 else jnp.concatenate([x] * n, axis=1)


def _rope(x, cos_t, sin_up, sin_dn):
    w = x.shape[1]
    return x * cos_t + pltpu.roll(x, 16, 1) * sin_up + pltpu.roll(x, w - 16, 1) * sin_dn


def _rope_t(dy, cos_t, sin_up, sin_dn):
    w = dy.shape[1]
    return dy * cos_t + pltpu.roll(dy * sin_up, w - 16, 1) + pltpu.roll(dy * sin_dn, 16, 1)


def _mla_prep(q, kv, projp, tables, *, name):
    nh = N_ATT_HEADS

    def fn(qv, kvv, kr, cos_t, sin_up, sin_dn):
        qr = _rope(qv, _rep(cos_t, nh), _rep(sin_up, nh), _rep(sin_dn, nh))
        krr = _rope(kr, cos_t, sin_up, sin_dn)
        lane = lax.broadcasted_iota(jnp.int32, kvv.shape, 1) % LANES
        return qr, jnp.where(lane < HEAD, kvv, 0.0) + _rep(krr, nh), kvv

    w = q.shape[1]
    return _rowwise(fn, [q, kv, (projp, LANES, P_KR // LANES)] + list(tables), [],
                    [(w, BF16), (w, BF16), (w, BF16)], name=name)


def _mla_prep_bwd(dqr, dkf, tables, *, name):
    nh = N_ATT_HEADS

    def fn(dq, dk, cos_t, sin_up, sin_dn):
        dqp = _rope_t(dq, _rep(cos_t, nh), _rep(sin_up, nh), _rep(sin_dn, nh))
        dkrr = dk[:, 0:LANES]
        for h in range(1, nh):
            dkrr = dkrr + dk[:, LANES * h:LANES * (h + 1)]
        return dqp, _rope_t(dkrr, cos_t, sin_up, sin_dn)

    return _rowwise(fn, [dqr, dkf] + list(tables), [], [(dqr.shape[1], BF16), (LANES, BF16)], name=name)


_LOG2E = 1.0 / math.log(2.0)
_NT = (((1,), (1,)), ((), ()))
_TN = (((0,), (0,)), ((), ()))


def _attn_fwd(qr, kf, kvb, *, name, job=None):
    t_rows = qr.shape[0]
    tq = min(ATT_TQ, t_rows)
    nb = t_rows // tq
    scale = ATT_D ** -0.5

    c2 = scale * _LOG2E

    def body(q_ref, kf_ref, kvb_ref, o_ref, lse_ref):
        qi = pl.program_id(1)
        lane = lax.broadcasted_iota(jnp.int32, (tq, LANES), 1)
        causal_t = (lax.broadcasted_iota(jnp.int32, (tq, tq), 0) <= lax.broadcasted_iota(jnp.int32, (tq, tq), 1))
        heads = [slice(hh * LANES, (hh + 1) * LANES) for hh in range(2)]
        qs = [q_ref[:, cols] for cols in heads]

        def block(ki, carry, diagonal):
            rows = pl.ds(pl.multiple_of(ki * tq, tq), tq)
            new = []
            for q, cols, (m_old, l_old, acc_t) in zip(qs, heads, carry):
                s_t = lax.dot_general(kf_ref[rows, cols], q, _NT, preferred_element_type=F32)
                if diagonal:
                    s_t = jnp.where(causal_t, s_t, -1e30)
                m_new = jnp.maximum(m_old, jnp.max(s_t, axis=0, keepdims=True))
                p_t = jnp.exp2((s_t - m_new) * c2)
                a = jnp.exp2((m_old - m_new) * c2)
                pv_t = lax.dot_general(kvb_ref[rows, cols], p_t.astype(BF16), _TN, preferred_element_type=F32)
                new.append((m_new, a * l_old + jnp.sum(p_t, axis=0, keepdims=True), a * acc_t + pv_t))
            return tuple(new)

        init = (jnp.full((1, tq), -1e30, F32), jnp.zeros((1, tq), F32), jnp.zeros((LANES, tq), F32))
        carry = lax.fori_loop(0, qi, lambda ki, c: block(ki, c, False), (init, init))
        outs = []
        for hh, (m_fin, l_fin, acc_t) in enumerate(block(qi, carry, True)):
            lse_ref[hh] = m_fin * scale + jnp.log(l_fin)
            outs.append((acc_t / l_fin).T)
        o_ref[...] = jnp.where(lane < HEAD, pltpu.roll(outs[0], HEAD, 1), outs[1])

    pair = pl.BlockSpec((t_rows, 2 * LANES), lambda pr, qi: (0, pr))
    return _call(
        body, (qr, kf, kvb), name=name, grid=(N_ATT_HEADS // 2, nb),
        in_specs=[pl.BlockSpec((tq, 2 * LANES), lambda pr, qi: (qi, pr)), pair, pair],
        out_specs=[pl.BlockSpec((tq, LANES), lambda pr, qi: (qi, pr)),
                   pl.BlockSpec((2, 1, tq), lambda pr, qi: (pr, 0, qi))],
        out_shape=[jax.ShapeDtypeStruct((t_rows, N_ATT_HEADS * HEAD), F32),
                   jax.ShapeDtypeStruct((N_ATT_HEADS, 1, t_rows), F32)],
        sem=("parallel", "arbitrary"), job=job)


def _attn_bwd(qr, kf, kvb, dcat, o, lse, *, name, job=None):
    t_rows = qr.shape[0]
    tq = min(ATT_TQ, t_rows)
    nb = t_rows // tq
    scale = ATT_D ** -0.5
    c2 = scale * _LOG2E
    do_off = 2 * HG_W // LANES

    def body(q_ref, kf_ref, kvb_ref, do_ref, o_ref, lse_ref, dq_ref, dkv_ref, dk_ref):
        ki = pl.program_id(1)

        @pl.when(ki == 0)
        def _():
            dq_ref[...] = jnp.zeros_like(dq_ref)

        lane = lax.broadcasted_iota(jnp.int32, (tq, LANES), 1)
        causal_t = (lax.broadcasted_iota(jnp.int32, (tq, tq), 0) <= lax.broadcasted_iota(jnp.int32, (tq, tq), 1))
        heads = [slice(hh * LANES, (hh + 1) * LANES) for hh in range(2)]
        ks = [kf_ref[:, cols] for cols in heads]
        vs = [kvb_ref[:, cols] for cols in heads]

        def block(qi, carry, diagonal):
            rows = pl.ds(pl.multiple_of(qi * tq, tq), tq)
            do_pair, o_pair = do_ref[rows, :], o_ref[rows, :]
            new = []
            for hh, (cols, k, v, (dk, dv)) in enumerate(zip(heads, ks, vs, carry)):
                q = q_ref[rows, cols]
                do, ov = (pltpu.roll(do_pair, HEAD, 1), pltpu.roll(o_pair, HEAD, 1)) if hh == 0 else (do_pair, o_pair)
                do = jnp.where(lane >= HEAD, do, 0.0)
                delta = jnp.sum((do * ov).T, axis=0, keepdims=True)
                s_t = lax.dot_general(k, q, _NT, preferred_element_type=F32)
                if diagonal:
                    s_t = jnp.where(causal_t, s_t, -1e30)
                p_t = jnp.exp2(s_t * c2 - lse_ref[hh, :, rows] * _LOG2E)
                dob = do.astype(BF16)
                dv = dv + jnp.dot(p_t.astype(BF16), dob, preferred_element_type=F32)
                dp_t = lax.dot_general(v, dob, _NT, preferred_element_type=F32)
                ds_t = (p_t * (dp_t - delta) * scale).astype(BF16)
                dk = dk + jnp.dot(ds_t, q, preferred_element_type=F32)
                dq_ref[rows, cols] += lax.dot_general(ds_t, k, _TN, preferred_element_type=F32)
                new.append((dk, dv))
            return tuple(new)

        zero = jnp.zeros((tq, LANES), F32)
        carry = block(ki, ((zero, zero), (zero, zero)), True)
        carry = lax.fori_loop(ki + 1, nb, lambda qi, c: block(qi, c, False), carry)
        dkv_ref[...] = jnp.concatenate([jnp.where(lane < HEAD, dk, dv) for dk, dv in carry],
                                       axis=1).astype(dkv_ref.dtype)
        dk_ref[...] = jnp.concatenate([dk for dk, _ in carry], axis=1)

    pair_all = pl.BlockSpec((t_rows, 2 * LANES), lambda pr, ki: (0, pr))
    pair_blk = pl.BlockSpec((tq, 2 * LANES), lambda pr, ki: (ki, pr))
    wide = jax.ShapeDtypeStruct((t_rows, N_ATT_HEADS * LANES), F32)
    return _call(
        body, (qr, kf, kvb, dcat, o, lse), name=name, grid=(N_ATT_HEADS // 2, nb),
        in_specs=[pair_all, pair_blk, pair_blk,
                  pl.BlockSpec((t_rows, LANES), lambda pr, ki: (0, do_off + pr)),
                  pl.BlockSpec((t_rows, LANES), lambda pr, ki: (0, pr)),
                  pl.BlockSpec((2, 1, t_rows), lambda pr, ki: (pr, 0, 0))],
        out_specs=[pair_all, pair_blk, pair_blk],
        out_shape=[wide, jax.ShapeDtypeStruct(wide.shape, BF16), wide],
        sem=("parallel", "arbitrary"), job=job)


def _my_pos():
    return lax.axis_index("x"), lax.axis_index("y"), lax.axis_index("c")


def _all_gather(xs, *, name):
    return _gather_forward(_run_job(_gather_job(xs), name=name), name=name + "_forward")


def _remote(src, dst, send_sems, recv_sems, k, dev):
    return pltpu.make_async_remote_copy(src_ref=src, dst_ref=dst, send_sem=send_sems.at[k], recv_sem=recv_sems.at[k],
                                        device_id=dev, device_id_type=MESH)


def _gather_job(xs):
    n = len(xs)

    def make(x_refs, out_refs, send_sems, recv_sems, local_sems):
        mx, my, mc = _my_pos()
        mine = 4 * mx + 2 * my + mc
        peers = [(mx, my, 1 - mc), (1 - mx, my, mc), (mx, 1 - my, mc), (1 - mx, 1 - my, mc)]
        sends, recvs, local = [], [], []
        for a in range(n):
            local.append(pltpu.make_async_copy(x_refs[a], out_refs[a].at[mine], local_sems.at[a]))
            for k, dev in enumerate(peers):
                theirs = 4 * dev[0] + 2 * dev[1] + dev[2]
                sends.append(_remote(x_refs[a], out_refs[a].at[mine], send_sems, recv_sems, 4 * a + k, dev))
                recvs.append(_remote(x_refs[a], out_refs[a].at[theirs], send_sems, recv_sems, 4 * a + k, dev))
        return sends, recvs, local

    shapes = [jax.ShapeDtypeStruct((N_DEV,) + x.shape, x.dtype) for x in xs]
    return _copies_job(xs, shapes, 4 * n, n, make)


def _gather_forward(gs, *, name):
    n = len(gs)

    def body(*refs):
        out_refs = refs[n:2 * n]
        send_sems, recv_sems = refs[2 * n:]
        mx, my, mc = _my_pos()
        chips = [(1 - mx, my), (mx, 1 - my), (1 - mx, 1 - my)]
        sends, recvs = [], []
        for a in range(n):
            for j, (cx, cy) in enumerate(chips):
                here, there = out_refs[a].at[4 * cx + 2 * cy + mc], out_refs[a].at[4 * cx + 2 * cy + 1 - mc]
                sends.append(_remote(here, here, send_sems, recv_sems, 3 * a + j, (mx, my, 1 - mc)))
                recvs.append(_remote(here, there, send_sems, recv_sems, 3 * a + j, (mx, my, 1 - mc)))
        for cp in sends:
            cp.start()
        for cp in recvs:
            cp.wait_recv()
        for cp in sends:
            cp.wait_send()

    return pl.pallas_call(
        body, name=name, out_shape=[jax.ShapeDtypeStruct(g.shape, g.dtype) for g in gs],
        in_specs=[_ANY] * n, out_specs=[_ANY] * n, input_output_aliases={a: a for a in range(n)},
        scratch_shapes=[pltpu.SemaphoreType.DMA((3 * n,)), pltpu.SemaphoreType.DMA((3 * n,))],
    )(*gs)


def _pair_job(xs):
    n = len(xs)

    def make(x_refs, out_refs, send_sems, recv_sems, local_sems):
        mx, my, mc = _my_pos()
        copies = [_remote(x_refs[a].at[g, 1 - mc], out_refs[a].at[g], send_sems, recv_sems, 4 * a + g, (mx, my, 1 - mc))
                  for a in range(n) for g in range(4)]
        return copies, copies, []

    shapes = [jax.ShapeDtypeStruct((4,) + x.shape[2:], x.dtype) for x in xs]
    return _copies_job(xs, shapes, 4 * n, 0, make)


def _pair_add(x, r, core, *, name):
    _, _, a, b = x.shape
    ta = _row_tile(a, 256)

    def body(c_ref, x_ref, r_ref, o_ref):
        o_ref[...] = x_ref[...] + r_ref[...]

    blk = pl.BlockSpec((None, ta, b), lambda g, i, c_ref: (g, i, 0))
    return pl.pallas_call(
        body, name=name,
        grid_spec=pltpu.PrefetchScalarGridSpec(
            num_scalar_prefetch=1, grid=(4, a // ta),
            in_specs=[pl.BlockSpec((None, None, ta, b), lambda g, i, c_ref: (g, c_ref[0], i, 0)), blk],
            out_specs=blk),
        out_shape=jax.ShapeDtypeStruct((4, a, b), x.dtype),
        compiler_params=_cparams(("parallel", "parallel")),
    )(core, x, r)


def _quad_job(xs):
    n = len(xs)

    def make(x_refs, out_refs, send_sems, recv_sems, local_sems):
        mx, my, mc = _my_pos()
        mine = 2 * mx + my
        peers = [((1 - mx, my, mc), 2 * (1 - mx) + my), ((mx, 1 - my, mc), 2 * mx + 1 - my),
                 ((1 - mx, 1 - my, mc), 2 * (1 - mx) + 1 - my)]
        sends, recvs, local = [], [], []
        for a in range(n):
            local.append(pltpu.make_async_copy(x_refs[a].at[mine], out_refs[a].at[mine], local_sems.at[a]))
            for k, (dev, g) in enumerate(peers):
                sends.append(_remote(x_refs[a].at[g], out_refs[a].at[mine], send_sems, recv_sems, 3 * a + k, dev))
                recvs.append(_remote(x_refs[a].at[g], out_refs[a].at[g], send_sems, recv_sems, 3 * a + k, dev))
        return sends, recvs, local

    shapes = [jax.ShapeDtypeStruct(x.shape, x.dtype) for x in xs]
    return _copies_job(xs, shapes, 3 * n, n, make)


def _row_tile(r, pref):
    t = min(pref, r)
    while r % t or (t % 8 and t != r):
        t -= 1
    return t


def _adamw(parts, w, m, v, layer, *, name, tile=256):
    g, a, b = parts.shape
    tile = _row_tile(a, tile)
    c1 = 1.0 / (1.0 - ADAM_B1 ** ADAM_STEP)
    c2 = 1.0 / (1.0 - ADAM_B2 ** ADAM_STEP)

    def body(p_ref, w_ref, m_ref, v_ref, g_ref, d_ref, mo_ref, vo_ref):
        grad = p_ref[0]
        for j in range(1, g):
            grad = grad + p_ref[j]
        mn = ADAM_B1 * m_ref[...] + (1.0 - ADAM_B1) * grad
        vn = ADAM_B2 * v_ref[...] + (1.0 - ADAM_B2) * (grad * grad)
        g_ref[...] = grad
        mo_ref[...] = mn
        vo_ref[...] = vn
        d_ref[...] = -ADAM_LR * ((mn * c1) / (jnp.sqrt(vn * c2) + ADAM_EPS) + ADAM_WD * w_ref[...])

    slab = pl.BlockSpec((tile, b), lambda i: (i, 0))
    src = slab if layer is None else pl.BlockSpec((None, tile, b), lambda i: (layer, i, 0))
    return pl.pallas_call(
        body, name=name, grid=(a // tile,),
        in_specs=[pl.BlockSpec((g, tile, b), lambda i: (0, i, 0)), src, src, src],
        out_specs=[slab] * 4,
        out_shape=[jax.ShapeDtypeStruct((a, b), F32)] * 4,
        compiler_params=_cparams(("parallel",)),
    )(parts, w, m, v)


W_IN_SHARD = 276


def _w_in_dest(col):
    return jnp.where(col < P_KR, col, jnp.where(col < P_KR + 256, col + (P_CKV - P_KR), col - 2176 + P_KR + HEAD))


def _place_w_in(g, *, name):
    _, d, sh = g.shape
    tc = 768

    def body(g_ref, o_ref, acc_ref):
        ct, j = pl.program_id(0), pl.program_id(1)

        @pl.when(j == 0)
        def _():
            acc_ref[...] = jnp.zeros_like(acc_ref)

        src = j * sh + lax.broadcasted_iota(jnp.int32, (sh, tc), 0)
        dst = ct * tc + lax.broadcasted_iota(jnp.int32, (sh, tc), 1)
        place = (_w_in_dest(src) == dst).astype(BF16)
        acc_ref[...] += jnp.dot(g_ref[...], place, preferred_element_type=F32)

        @pl.when(j == N_DEV - 1)
        def _():
            o_ref[...] = acc_ref[...].astype(o_ref.dtype)

    return pl.pallas_call(
        body, name=name, grid=(P_COLS // tc, N_DEV),
        in_specs=[pl.BlockSpec((None, d, sh), lambda ct, j: (j, 0, 0))],
        out_specs=pl.BlockSpec((d, tc), lambda ct, j: (0, ct)),
        out_shape=jax.ShapeDtypeStruct((d, P_COLS), BF16),
        scratch_shapes=[pltpu.VMEM((d, tc), F32)],
        compiler_params=_cparams(("parallel", "arbitrary")),
    )(g)


def _unplace_w_in(dw, *, name):
    d = dw.shape[0]
    sh = W_IN_SHARD

    def body(dw_ref, o_ref):
        j = pl.program_id(0)
        src = j * sh + lax.broadcasted_iota(jnp.int32, (P_COLS, sh), 1)
        dst = lax.broadcasted_iota(jnp.int32, (P_COLS, sh), 0)
        pick = (_w_in_dest(src) == dst).astype(BF16)
        x = dw_ref[...]
        o_ref[...] = _split_dot(x, pick)

    return pl.pallas_call(
        body, name=name, grid=(N_DEV,),
        in_specs=[pl.BlockSpec((d, P_COLS), lambda j: (0, 0))],
        out_specs=pl.BlockSpec((None, d, sh), lambda j: (j, 0, 0)),
        out_shape=jax.ShapeDtypeStruct((N_DEV, d, sh), F32),
        compiler_params=_cparams(("arbitrary",)),
    )(dw)


def _gate_up_swiglu(h1, wgu, *, name):
    t_rows, k = h1.shape
    w = wgu.shape[2]
    tm = _tile(t_rows, 1024)

    def body(a_ref, wg_ref, wu_ref, gu_ref, act_ref):
        a = a_ref[...].astype(BF16)
        gate = jnp.dot(a, wg_ref[...], preferred_element_type=F32)
        up = jnp.dot(a, wu_ref[...], preferred_element_type=F32)
        gu_ref[0] = gate.astype(gu_ref.dtype)
        gu_ref[1] = up.astype(gu_ref.dtype)
        act_ref[...] = (gate * _sigmoid(gate) * up).astype(act_ref.dtype)

    return pl.pallas_call(
        body, name=name, grid=(t_rows // tm, 4),
        in_specs=[pl.BlockSpec((tm, k), lambda i, j: (i, 0)),
                  pl.BlockSpec((None, k, w), lambda i, j: (j, 0, 0)),
                  pl.BlockSpec((None, k, w), lambda i, j: (j + 4, 0, 0))],
        out_specs=[pl.BlockSpec((2, None, tm, w), lambda i, j: (0, j, i, 0)),
                   pl.BlockSpec((None, tm, w), lambda i, j: (j, i, 0))],
        out_shape=[jax.ShapeDtypeStruct((2, 4, t_rows, w), BF16), jax.ShapeDtypeStruct((4, t_rows, w), BF16)],
        compiler_params=_cparams(("parallel", "arbitrary")),
    )(h1, wgu, wgu)


def _down_dx_swiglu(dffn, wdown, gu, *, name):
    t_rows, k = dffn.shape
    w = gu.shape[3]
    tm = _tile(t_rows, 1024)

    def body(d_ref, w_ref, gu_ref, o_ref):
        dact = lax.dot_general(d_ref[...].astype(BF16), w_ref[...], _NT, preferred_element_type=F32)
        gate, up = gu_ref[0].astype(F32), gu_ref[1].astype(F32)
        sg = _sigmoid(gate)
        o_ref[0] = (dact * up * (sg * (1.0 + gate * (1.0 - sg)))).astype(o_ref.dtype)
        o_ref[1] = (dact * gate * sg).astype(o_ref.dtype)

    blk = pl.BlockSpec((2, None, tm, w), lambda i, j: (0, j, i, 0))
    return pl.pallas_call(
        body, name=name, grid=(t_rows // tm, 4),
        in_specs=[pl.BlockSpec((tm, k), lambda i, j: (i, 0)), pl.BlockSpec((w, k), lambda i, j: (j, 0)), blk],
        out_specs=blk, out_shape=jax.ShapeDtypeStruct(gu.shape, BF16),
        compiler_params=_cparams(("parallel", "arbitrary")),
    )(dffn, wdown, gu)


BIG = ("w_in", "mla_w_uq", "mla_w_ukv", "w_out", "w_gate_up", "w_down", "ple_w_gate", "ple_w_proj")
SMALL = ("ln_in_g", "ln_in_b", "hgrn_lb_logits", "hgrn_norm_g", "sgu_ln_g", "sgu_ln_b", "sgu_w_s", "sgu_b_s",
         "mla_q_norm_g", "mla_kv_norm_g", "ln1_g", "ln1_b", "ln2_g", "ln2_b")
ORDER = ("ln_in_g", "ln_in_b", "w_in", "hgrn_lb_logits", "hgrn_norm_g", "sgu_ln_g", "sgu_ln_b", "sgu_w_s", "sgu_b_s",
         "mla_q_norm_g", "mla_w_uq", "mla_kv_norm_g", "mla_w_ukv", "w_out", "ln1_g", "ln1_b", "w_gate_up", "w_down",
         "ple_w_gate", "ple_w_proj", "ln2_g", "ln2_b")


def _slab(a, align):
    s = a.reshape(-1, LANES)
    pad = -s.shape[0] % align
    return jnp.pad(s, ((0, pad), (0, 0))) if pad else s


def _pack(arrays, align=16, total_align=512):
    s = jnp.concatenate([_slab(a, align) for a in arrays], axis=0)
    pad = -s.shape[0] % total_align
    return jnp.pad(s, ((0, pad), (0, 0))) if pad else s


def _unpack(slab, shapes, align=16):
    out, r0 = [], 0
    for s in shapes:
        nr = math.prod(s) // LANES
        out.append(slab[r0:r0 + nr].reshape(s))
        r0 += nr + (-nr % align)
    return out


def _blocks_to_cols(g, *, name):
    nb, a, b = g.shape

    def body(g_ref, o_ref):
        o_ref[...] = g_ref[...]

    return pl.pallas_call(
        body, name=name, grid=(nb,), in_specs=[pl.BlockSpec((None, a, b), lambda j: (j, 0, 0))],
        out_specs=pl.BlockSpec((a, b), lambda j: (0, j)), out_shape=jax.ShapeDtypeStruct((a, nb * b), g.dtype),
        compiler_params=_cparams(("parallel",)),
    )(g)


def _cols_to_blocks(x, *, name):
    a, b = x.shape[0], x.shape[1] // N_DEV

    def body(x_ref, o_ref):
        o_ref[...] = x_ref[...]

    return pl.pallas_call(
        body, name=name, grid=(N_DEV,), in_specs=[pl.BlockSpec((a, b), lambda j: (0, j))],
        out_specs=pl.BlockSpec((None, a, b), lambda j: (j, 0, 0)), out_shape=jax.ShapeDtypeStruct((N_DEV, a, b), x.dtype),
        compiler_params=_cparams(("parallel",)),
    )(x)


def _weight_shards(w, li):
    uq_pad = ((0, 0), (0, LANES - ATT_D))
    shards = {k: w[k][li] for k in BIG}
    shards["mla_w_uq"] = jnp.pad(shards["mla_w_uq"], uq_pad)
    return {k: s.astype(BF16) for k, s in shards.items()}


def _usable_weights(g, *, name):
    out = {}
    for k, a in g.items():
        if k == "w_in":
            out[k] = _place_w_in(a, name=name + "_place_w_in")
        elif k in ("w_out", "w_down", "ple_w_gate"):
            out[k] = a.reshape(a.shape[0] * a.shape[1], a.shape[2])
        elif k == "w_gate_up":
            out[k] = a
        else:
            out[k] = _blocks_to_cols(a, name=name + "_cols_" + k)
    return out


def _as_pairs(g):
    if g.ndim == 2:
        return g.reshape((4, 2, g.shape[0] // N_DEV) + g.shape[1:])
    return g.reshape((4, 2) + g.shape[1:])


def _twice(fn):
    return lambda *a: fn(*a) * 2


def _layer_forward(li, h, hb, p_i, wts, sm, lbs, tables, alpha, hgrn_job=None, more_weights=None, attn_job=None):
    n = f"l{li}_"
    row1 = lambda a: a.reshape(1, -1)
    projp = _mm(hb, wts["w_in"], name=n + "proj")
    ng = row1(sm["hgrn_norm_g"][li])
    res = _hgrn_fwd(projp, lbs[li], ng, name=n + "hgrn_fwd", job=hgrn_job)
    if hgrn_job is not None:
        res, got = res
        wts = dict(wts, **more_weights(got))
    o_a, o_pre, states = res
    lg, lbias = row1(sm["sgu_ln_g"][li]), row1(sm["sgu_ln_b"][li])
    w_s = sm["sgu_w_s"][li]
    bias_full = jnp.repeat(sm["sgu_b_s"][li].T, HEAD, axis=1)
    o_b = _sgu_fwd(projp, lg, lbias, w_s, bias_full, name=n + "sgu_fwd")
    qg, kvg = row1(sm["mla_q_norm_g"][li]), row1(sm["mla_kv_norm_g"][li])
    cq_view, ckv_view = (projp, 384, P_CQ // 384), (projp, 256, P_CKV // 256)
    (cqn,) = _rowwise(_fn_rms, [cq_view], [qg], [(384, BF16)], name=n + "q_norm")
    (ckvn,) = _rowwise(_fn_rms, [ckv_view], [kvg], [(256, BF16)], name=n + "kv_norm")
    q = _mm(cqn, wts["mla_w_uq"], name=n + "uq")
    kv = _mm(ckvn, wts["mla_w_ukv"], name=n + "ukv")
    qr, kf, kvb = _mla_prep(q, kv, projp, tables, name=n + "mla_prep")
    res, attn_got = _attn_fwd(qr, kf, kvb, name=n + "attn_fwd", job=attn_job), None
    if attn_job is not None:
        res, attn_got = res
    o_c, lse = res
    cat = jnp.concatenate([o_a, o_b, o_c.astype(BF16)], axis=1)
    mix = _mm(cat, wts["w_out"], name=n + "out_proj")
    g1, b1 = row1(sm["ln1_g"][li]), row1(sm["ln1_b"][li])
    d = h.shape[1]
    h1, h1b = _rowwise(_twice(_make_post_mix(alpha)), [h, mix], [g1, b1], [(d, F32), (d, BF16)], name=n + "ln1")
    gu, act = _gate_up_swiglu(h1b, wts["w_gate_up"], name=n + "gate_up")
    ffn = _mm(act, wts["w_down"], am="bmk", name=n + "down")
    pg = _mm(h1b, wts["ple_w_gate"], name=n + "ple_gate")
    pp = _mm(p_i, wts["ple_w_proj"], name=n + "ple_proj")
    g2, b2 = row1(sm["ln2_g"][li]), row1(sm["ln2_b"][li])
    h2, h2b = _rowwise(_twice(_make_ple_ln(alpha)), [h1, ffn, pg, pp], [g2, b2], [(d, F32), (d, BF16)],
                       name=n + "ln2")
    saved = dict(h=h, hb=hb, h1b=h1b, projp=projp, o_pre=o_pre, states=states, cqn=cqn, ckvn=ckvn, qr=qr, kf=kf, kvb=kvb, o_c=o_c,
                 lse=lse, cat=cat, mix=mix, h1=h1, gu=gu, act=act, ffn=ffn, pg=pg, pp=pp, ng=ng, lg=lg, wts=wts,
                 lbias=lbias, w_s=w_s, bias_full=bias_full, qg=qg, kvg=kvg, g1=g1, b1=b1, g2=g2, b2=b2)
    return (h2, h2b), saved, attn_got


RS_EARLY = ("ple_w_proj", "ple_w_gate", "w_down", "w_gate_up", "w_out")
RS_LATE = ("mla_w_uq", "mla_w_ukv", "w_in")


def _layer_backward(li, dh2_parts, p_i, sv, lbs, tables, alpha, core, carried=None):
    n = f"l{li}_b_"
    wts = sv["wts"]
    gr = {}
    dh1_a, dffn, dpg, dpp, gr["ln2_g"], gr["ln2_b"] = _rowwise_vjp(
        _make_ple_ln(alpha), [sv["h1"], sv["ffn"], sv["pg"], sv["pp"]], [sv["g2"], sv["b2"]], [dh2_parts],
        groups=[[0], [1], [2], [3]], gdtypes=[F32, BF16, BF16, BF16], name=n + "ln2")
    big = {}
    big["ple_w_proj"] = _cols_to_blocks(_mm(p_i, dpp, am="km", name=n + "ple_proj_dw"), name=n + "ple_proj_dw_blocks")
    big["ple_w_gate"] = _mm(sv["h1b"], dpg, am="km", name=n + "ple_gate_dw")
    dh1_b = _mm(dpg, wts["ple_w_gate"], bm="nk", name=n + "ple_gate_dx")
    big["w_down"] = _mm(sv["act"], dffn, am="bkm", name=n + "down_dw")
    dgu = _down_dx_swiglu(dffn, wts["w_down"], sv["gu"], name=n + "down_dx")
    dgu = dgu.reshape((N_DEV,) + dgu.shape[2:])
    big["w_gate_up"], carried_got = _mm(sv["h1b"], dgu, am="km", bm="bkn", om="bmn", name=n + "gate_up_dw",
                                        job=carried), None
    if carried is not None:
        big["w_gate_up"], carried_got = big["w_gate_up"]
    dh1_c = _mm(dgu, wts["w_gate_up"], am="bmk", bm="bnk", name=n + "gate_up_dx")
    dh_a, dmix, gr["ln1_g"], gr["ln1_b"] = _rowwise_vjp(
        _make_post_mix(alpha), [sv["h"], sv["mix"]], [sv["g1"], sv["b1"]], [[dh1_a, dh1_b, dh1_c]],
        groups=[[0], [1]], gdtypes=[F32, BF16], name=n + "ln1")
    big["w_out"] = _mm(sv["cat"], dmix, am="km", name=n + "out_proj_dw")
    early = [_as_pairs(big[k]) for k in RS_EARLY]
    dcat, theirs = _mm(dmix, wts["w_out"], bm="nk", name=n + "out_proj_dx", job=_pair_job(early))
    sums = [_pair_add(x, r, core, name=n + "pair_add_" + k) for k, x, r in zip(RS_EARLY, early, theirs)]

    (dqr, dkv, dkf), early_quads = _attn_bwd(sv["qr"], sv["kf"], sv["kvb"], dcat, sv["o_c"], sv["lse"],
                                             name=n + "attn", job=_quad_job(sums))
    dqpad, dkr = _mla_prep_bwd(dqr, dkf, tables, name=n + "mla_prep")
    big["mla_w_uq"] = _cols_to_blocks(_mm(sv["cqn"], dqpad, am="km", name=n + "uq_dw"), name=n + "uq_dw_blocks")
    dcqn = _mm(dqpad, wts["mla_w_uq"], bm="nk", name=n + "uq_dx")
    big["mla_w_ukv"] = _cols_to_blocks(_mm(sv["ckvn"], dkv, am="km", name=n + "ukv_dw"), name=n + "ukv_dw_blocks")
    dckvn = _mm(dkv, wts["mla_w_ukv"], bm="nk", name=n + "ukv_dx")
    projp = sv["projp"]
    dcq, gr["mla_q_norm_g"] = _rowwise_vjp(_fn_rms, [(projp, 384, P_CQ // 384)], [sv["qg"]], [[dcqn]],
                                           groups=[[0]], gdtypes=[BF16], name=n + "q_norm")
    dckv, gr["mla_kv_norm_g"] = _rowwise_vjp(_fn_rms, [(projp, 256, P_CKV // 256)], [sv["kvg"]], [[dckvn]],
                                             groups=[[0]], gdtypes=[BF16], name=n + "kv_norm")
    dsgu, gr["sgu_ln_g"], gr["sgu_ln_b"], gr["sgu_w_s"], gr["sgu_b_s"] = _sgu_bwd(
        projp, sv["lg"], sv["lbias"], sv["w_s"], sv["bias_full"], dcat, name=n + "sgu")
    dhg, gr["hgrn_norm_g"], gr["lower_bound"] = _hgrn_bwd(
        projp, lbs[li], sv["ng"], sv["o_pre"], sv["states"], dcat, name=n + "hgrn")
    dprojp = jnp.concatenate([dhg, dsgu, dcq, dkr, dckv], axis=1)
    big["w_in"] = _unplace_w_in(_mm(sv["hb"], dprojp, am="km", name=n + "proj_dw"), name=n + "proj_dw_shards")
    late = [_as_pairs(big[k]) for k in RS_LATE]
    dh_b, theirs = _mm(dprojp, wts["w_in"], bm="nk", name=n + "proj_dx", job=_pair_job(late))
    late_sums = [_pair_add(x, r, core, name=n + "pair_add_" + k) for k, x, r in zip(RS_LATE, late, theirs)]
    return [dh_a, dh_b], gr, early_quads, late_sums, carried_got


def kernel(x, p, positions, ln_in_g, ln_in_b, w_in, hgrn_lb_logits, hgrn_norm_g, sgu_ln_g, sgu_ln_b, sgu_w_s, sgu_b_s, mla_q_norm_g, mla_w_uq, mla_kv_norm_g, mla_w_ukv, w_out, ln1_g, ln1_b, w_gate_up, w_down, ple_w_gate, ple_w_proj, ln2_g, ln2_b, loss_target, m_ln_in_g, m_ln_in_b, m_w_in, m_hgrn_lb_logits, m_hgrn_norm_g, m_sgu_ln_g, m_sgu_ln_b, m_sgu_w_s, m_sgu_b_s, m_mla_q_norm_g, m_mla_w_uq, m_mla_kv_norm_g, m_mla_w_ukv, m_w_out, m_ln1_g, m_ln1_b, m_w_gate_up, m_w_down, m_ple_w_gate, m_ple_w_proj, m_ln2_g, m_ln2_b, v_ln_in_g, v_ln_in_b, v_w_in, v_hgrn_lb_logits, v_hgrn_norm_g, v_sgu_ln_g, v_sgu_ln_b, v_sgu_w_s, v_sgu_b_s, v_mla_q_norm_g, v_mla_w_uq, v_mla_kv_norm_g, v_mla_w_ukv, v_w_out, v_ln1_g, v_ln1_b, v_w_gate_up, v_w_down, v_ple_w_gate, v_ple_w_proj, v_ln2_g, v_ln2_b):
    args = dict(locals())
    w = {k: args[k] for k in ORDER}
    m = {k: args["m_" + k] for k in ORDER}
    v = {k: args["v_" + k] for k in ORDER}
    depth = w_in.shape[0]
    assert depth == 2, "the lower-bound kernel is written for two layers"
    alpha = (2 * depth) ** 0.25
    xs, tgt = x[0], loss_target[0]
    d_model = xs.shape[1]

    shards = [_weight_shards(w, li) for li in range(depth)]
    rest = [k for k in BIG if k != "w_in"]
    (g_in,) = _all_gather([shards[0]["w_in"]], name="gather_l0_w_in")
    w_in0 = _usable_weights({"w_in": g_in}, name="l0")

    def rest_of_layer0(got):
        got = _gather_forward(got, name="gather_l0_forward")
        return _usable_weights(dict(zip(rest, got)), name="l0")

    tables = _rope_tables(positions[0])
    row1 = lambda a: a.reshape(1, -1)
    l0, l1 = row1(hgrn_lb_logits[0]), row1(hgrn_lb_logits[1])
    lbs = _rowwise(_fn_lower_bounds, [l0, l1], [], [(HG_W, F32), (HG_W, F32)], name="lower_bounds")

    gin, bin_ = row1(ln_in_g), row1(ln_in_b)
    h, hb = _rowwise(_twice(_fn_ln), [xs], [gin, bin_], [(d_model, F32), (d_model, BF16)], name="ln_in")
    (h, hb), sv0, got1 = _layer_forward(0, h, hb, p[0, 0], w_in0, w, lbs, tables, alpha,
                                  hgrn_job=_gather_job([shards[0][k] for k in rest]), more_weights=rest_of_layer0,
                                  attn_job=_gather_job([shards[1][k] for k in BIG]))
    wts1 = _usable_weights(dict(zip(BIG, _gather_forward(got1, name="gather_l1_forward"))), name="l1")
    (h, _), sv1, _ = _layer_forward(1, h, hb, p[1, 0], wts1, w, lbs, tables, alpha)
    saved = [sv0, sv1]
    dy, loss_local = _loss_and_grad(h, tgt, name="loss")
    loss = lax.psum(loss_local[0, 0], ("x", "y", "c"))

    core = lax.axis_index("c").astype(jnp.int32).reshape(1)
    dparts, grads, quads, carried = [dy], [None] * depth, [None] * depth, None
    for li in reversed(range(depth)):
        dparts, grads[li], early_quads, late_sums, late_quads = _layer_backward(
            li, dparts, p[li, 0], saved[li], lbs, tables, alpha, core, carried=carried)
        quads[li] = dict(zip(RS_EARLY, early_quads))
        if carried is not None:
            quads[li + 1].update(zip(RS_LATE, late_quads))
        carried = _quad_job(late_sums)
    quads[0].update(zip(RS_LATE, _run_job(carried, name="rs_l0_late_quad")))
    dx, d_gin, d_bin = _rowwise_vjp(_fn_ln, [xs], [gin, bin_], [dparts], groups=[[0]], name="ln_in_b")
    dl0, dl1 = _rowwise_vjp(_fn_lower_bounds, [l0, l1], [], [[grads[0]["lower_bound"]], [grads[1]["lower_bound"]]],
                            groups=[[0], [1]], name="lower_bounds_b")

    prefixes = ("grad_", "delta_", "new_m_", "new_v_")
    per_layer = {pre + k: [] for pre in prefixes for k in BIG}
    uq_pad = ((0, 0), (0, 0), (0, LANES - ATT_D))
    state = {k: ((jnp.pad(w[k], uq_pad), jnp.pad(m[k], uq_pad), jnp.pad(v[k], uq_pad)) if k == "mla_w_uq"
                 else (w[k], m[k], v[k])) for k in BIG}
    for li in range(depth):
        for k in BIG:
            res4 = _adamw(quads[li][k], *state[k], li, name=f"adamw_l{li}_{k}")
            for pre, a in zip(prefixes, res4):
                per_layer[pre + k].append(a[:, :ATT_D] if k == "mla_w_uq" else a)
    out = {name: jnp.stack(vals) for name, vals in per_layer.items()}

    small_g = {"ln_in_g": d_gin.reshape(-1), "ln_in_b": d_bin.reshape(-1),
               "hgrn_lb_logits": jnp.stack([dl0.reshape(-1), dl1.reshape(-1)])}
    for k in SMALL[3:]:
        small_g[k] = jnp.stack([grads[li][k].reshape(w[k].shape[1:]) for li in range(depth)])
    (small_parts,) = _all_gather([_pack([small_g[k] for k in SMALL])], name="gather_small_grads")
    slabs = _adamw(small_parts, _pack([w[k] for k in SMALL]), _pack([m[k] for k in SMALL]),
                   _pack([v[k] for k in SMALL]), None, name="adamw_small")
    shapes = [w[k].shape for k in SMALL]
    for pre, slab in zip(prefixes, slabs):
        for k, a in zip(SMALL, _unpack(slab, shapes)):
            out[pre + k] = a
    res = [loss, dx[None]]
    for prefix in ("grad_", "delta_", "new_m_", "new_v_"):
        res += [out[prefix + k] for k in ORDER]
    return tuple(res)
```

```python
import functools
import math

import jax
import jax.numpy as jnp
from jax import lax
from jax.experimental import pallas as pl
from jax.experimental.pallas import tpu as pltpu

F32 = jnp.float32
BF16 = jnp.bfloat16
MESH = pl.DeviceIdType.MESH

LN_EPS = 1e-5
RMS_EPS = 1e-6
ROPE_THETA = 10000.0
ADAM_LR, ADAM_B1, ADAM_B2, ADAM_EPS, ADAM_WD, ADAM_STEP = 0.001, 0.9, 0.999, 1e-08, 0.01, 10

N_DEV = 8
LANES = 128
HG_CHUNK = 16
HG_W = 256
HEAD = 64
SGU_CHUNK = 128
N_ATT_HEADS = 8
ATT_D = 96
VMEM_LIMIT = 56 * 1024 * 1024

HG_TILE = 256
ATT_TQ = 512
ROW_TILE = 256

P_CQ, P_KR, P_CKV, P_COLS = 1536, 1920, 2048, 2304


def _cparams(sem):
    return pltpu.CompilerParams(dimension_semantics=sem, vmem_limit_bytes=VMEM_LIMIT)


_ANY = pl.BlockSpec(memory_space=pl.ANY)


def _call(body, operands, *, name, grid, in_specs, out_specs, out_shape, sem, scratch_shapes=(), job=None):
    if job is None:
        return pl.pallas_call(body, name=name, grid=grid, in_specs=in_specs, out_specs=out_specs, out_shape=out_shape,
                              scratch_shapes=list(scratch_shapes), compiler_params=_cparams(sem))(*operands)
    single = not isinstance(out_shape, (list, tuple))
    shapes = [out_shape] if single else list(out_shape)
    ospecs = [out_specs] if single else list(out_specs)
    ni, no, ns = len(operands), len(shapes), len(scratch_shapes)
    ji, jo = len(job.inputs), len(job.out_shapes)

    def hosted(*refs):
        p = 0
        parts = []
        for cnt in (ni, ji, no, jo, ns):
            parts.append(refs[p:p + cnt])
            p += cnt
        ins, jins, outs, jouts, scr = parts
        jsems = refs[p:]
        ids = [pl.program_id(a) for a in range(len(grid))]
        first = functools.reduce(lambda a, b: a & b, [i == 0 for i in ids])
        last = functools.reduce(lambda a, b: a & b, [i == g - 1 for i, g in zip(ids, grid)])

        @pl.when(first)
        def _():
            job.start(jins, jouts, jsems)

        body(*ins, *outs, *scr)

        @pl.when(last)
        def _():
            job.finish(jins, jouts, jsems)

    res = pl.pallas_call(
        hosted, name=name, grid=grid,
        in_specs=list(in_specs) + [_ANY] * ji, out_specs=ospecs + [_ANY] * jo,
        out_shape=shapes + list(job.out_shapes),
        scratch_shapes=list(scratch_shapes) + [pltpu.SemaphoreType.DMA((c,)) for c in job.sem_counts],
        compiler_params=_cparams(("arbitrary",) * len(grid)),
    )(*operands, *job.inputs)
    own = res[0] if single else res[:no]
    return own, res[no:]


class _Job:
    def __init__(self, inputs, out_shapes, sem_counts, start, finish):
        self.inputs, self.out_shapes, self.sem_counts = list(inputs), list(out_shapes), list(sem_counts)
        self.start, self.finish = start, finish


def _copies_job(inputs, out_shapes, n_remote, n_local, make):
    def start(jins, jouts, sems):
        sends, _, local = make(jins, jouts, *sems)
        for cp in local + sends:
            cp.start()

    def finish(jins, jouts, sems):
        sends, recvs, local = make(jins, jouts, *sems)
        for cp in recvs:
            cp.wait_recv()
        for cp in sends:
            cp.wait_send()
        for cp in local:
            cp.wait()

    return _Job(inputs, out_shapes, [n_remote, n_remote, max(n_local, 1)], start, finish)


def _run_job(job, *, name):
    ji, jo = len(job.inputs), len(job.out_shapes)

    def body(*refs):
        jins, jouts, sems = refs[:ji], refs[ji:ji + jo], refs[ji + jo:]
        job.start(jins, jouts, sems)
        job.finish(jins, jouts, sems)

    return pl.pallas_call(
        body, name=name, out_shape=list(job.out_shapes), in_specs=[_ANY] * ji, out_specs=[_ANY] * jo,
        scratch_shapes=[pltpu.SemaphoreType.DMA((c,)) for c in job.sem_counts],
    )(*job.inputs)


def _tile(n, pref):
    if n % pref == 0:
        return pref
    best = None
    t = LANES
    while t <= min(n, pref):
        if n % t == 0:
            best = t
        t += LANES
    return best if best is not None else n


def _mm(a, b, *, am="mk", bm="kn", om="mn", out_dtype=F32, tm=1024, tn=1024, tk=1024, name, job=None):
    if am == "mk":
        m, k = a.shape
    elif am == "km":
        k, m = a.shape
    elif am == "bmk":
        m, tk = a.shape[1], a.shape[2]
        k = a.shape[0] * tk
    else:
        k, tm = a.shape[1], a.shape[2]
        m = a.shape[0] * tm
    if bm == "kn":
        kb_, n = b.shape
    elif bm == "nk":
        n, kb_ = b.shape
    elif bm == "bkn":
        kb_, tn = b.shape[1], b.shape[2]
        n = b.shape[0] * tn
    else:
        n, tk = b.shape[1], b.shape[2]
        kb_ = b.shape[0] * tk
    assert kb_ == k, (a.shape, b.shape, am, bm)
    tm, tn, tk = _tile(m, tm), _tile(n, tn), _tile(k, tk)
    nk = k // tk
    dims = (((0 if am in ("km", "bkm") else 1,), (1 if bm in ("nk", "bnk") else 0,)), ((), ()))

    a_spec = {"mk": pl.BlockSpec((tm, tk), lambda i, j, kk: (i, kk)),
              "km": pl.BlockSpec((tk, tm), lambda i, j, kk: (kk, i)),
              "bmk": pl.BlockSpec((None, tm, tk), lambda i, j, kk: (kk, i, 0)),
              "bkm": pl.BlockSpec((None, tk, tm), lambda i, j, kk: (i, kk, 0))}[am]
    b_spec = {"kn": pl.BlockSpec((tk, tn), lambda i, j, kk: (kk, j)),
              "nk": pl.BlockSpec((tn, tk), lambda i, j, kk: (j, kk)),
              "bkn": pl.BlockSpec((None, tk, tn), lambda i, j, kk: (j, kk, 0)),
              "bnk": pl.BlockSpec((None, tn, tk), lambda i, j, kk: (kk, j, 0))}[bm]
    if om == "mn":
        o_spec, o_shape = pl.BlockSpec((tm, tn), lambda i, j, kk: (i, j)), (m, n)
    else:
        o_spec, o_shape = pl.BlockSpec((None, tm, tn), lambda i, j, kk: (j, i, 0)), (n // tn, m, tn)

    def body(a_ref, b_ref, o_ref, *acc):
        kk = pl.program_id(2)
        prod = lax.dot_general(a_ref[...].astype(BF16), b_ref[...].astype(BF16), dims, preferred_element_type=F32)
        if nk == 1:
            o_ref[...] = prod.astype(o_ref.dtype)
            return
        acc_ref, = acc

        @pl.when(kk == 0)
        def _():
            acc_ref[...] = prod

        if nk > 2:
            @pl.when((kk > 0) & (kk < nk - 1))
            def _():
                acc_ref[...] += prod

        @pl.when(kk == nk - 1)
        def _():
            o_ref[...] = (acc_ref[...] + prod).astype(o_ref.dtype)

    return _call(body, (a, b), name=name, grid=(m // tm, n // tn, nk), in_specs=[a_spec, b_spec], out_specs=o_spec,
                 out_shape=jax.ShapeDtypeStruct(o_shape, out_dtype),
                 scratch_shapes=[pltpu.VMEM((tm, tn), F32)] if nk > 1 else [],
                 sem=("parallel", "parallel", "arbitrary"), job=job)


def _row_operand(a, tile):
    if isinstance(a, tuple):
        arr, w, j = a
        return arr, pl.BlockSpec((tile, w), lambda i, j=j: (i, j))
    return a, pl.BlockSpec((tile, a.shape[1]), lambda i: (i, 0))


def _const_spec(c):
    nd = c.ndim
    return pl.BlockSpec(c.shape, lambda i, nd=nd: (0,) * nd)


def _rowwise(fn, rows, consts, outs, *, name, accs=(), tile=None):
    t_rows = (rows[0][0] if isinstance(rows[0], tuple) else rows[0]).shape[0]
    tile = min(tile or ROW_TILE, t_rows)
    arrs, specs = zip(*[_row_operand(a, tile) for a in rows])
    nin, no = len(rows) + len(consts), len(outs)

    def body(*refs):
        res = fn(*[r[...] for r in refs[:nin]])
        for r, v in zip(refs[nin:nin + no], res[:no]):
            r[...] = v.astype(r.dtype)
        if accs:
            a_refs = refs[nin + no:]

            @pl.when(pl.program_id(0) == 0)
            def _():
                for r in a_refs:
                    r[...] = jnp.zeros_like(r)

            for r, v in zip(a_refs, res[no:]):
                r[...] += v

    out_shape = [jax.ShapeDtypeStruct((t_rows, w), dt) for w, dt in outs]
    out_shape += [jax.ShapeDtypeStruct(s, F32) for s in accs]
    out_specs = [pl.BlockSpec((tile, w), lambda i: (i, 0)) for w, _ in outs]
    out_specs += [pl.BlockSpec(s, lambda i, nd=len(s): (0,) * nd) for s in accs]
    return pl.pallas_call(
        body, name=name, grid=(t_rows // tile,),
        in_specs=list(specs) + [_const_spec(c) for c in consts],
        out_specs=out_specs, out_shape=out_shape,
        compiler_params=_cparams(("arbitrary",)),
    )(*arrs, *consts)


def _rowwise_vjp(fn, rows, consts, cts, *, name, groups, tile=None, gdtypes=None, job=None):
    t_rows = (rows[0][0] if isinstance(rows[0], tuple) else rows[0]).shape[0]
    tile = min(tile or ROW_TILE, t_rows)
    arrs, specs = zip(*[_row_operand(a, tile) for a in rows])
    flat_cts = [c for group in cts for c in group]
    ct_arrs, ct_specs = zip(*[_row_operand(a, tile) for a in flat_cts])
    nr, nc, nct, ng = len(rows), len(consts), len(flat_cts), len(groups)

    def width(a):
        return a[1] if isinstance(a, tuple) else a.shape[1]

    def body(*refs):
        rv = [r[...].astype(F32) for r in refs[:nr]]
        cv = [r[...] for r in refs[nr:nr + nc]]
        ct_refs = refs[nr + nc:nr + nc + nct]
        ctv, pos = [], 0
        for group in cts:
            s = ct_refs[pos][...].astype(F32)
            for r in ct_refs[pos + 1:pos + len(group)]:
                s = s + r[...].astype(F32)
            ctv.append(s)
            pos += len(group)
        _, pull = jax.vjp(fn, *rv, *cv)
        grads = pull(tuple(ctv))
        g_refs = refs[nr + nc + nct:nr + nc + nct + ng]
        for r, idx in zip(g_refs, groups):
            parts = [grads[i] for i in idx]
            r[...] = (parts[0] if len(parts) == 1 else jnp.concatenate(parts, axis=1)).astype(r.dtype)
        c_refs = refs[nr + nc + nct + ng:]

        @pl.when(pl.program_id(0) == 0)
        def _():
            for r in c_refs:
                r[...] = jnp.zeros_like(r)

        for r, v in zip(c_refs, grads[nr:]):
            r[...] += v

    gw = [sum(width(rows[i]) for i in idx) for idx in groups]
    gdtypes = gdtypes or [F32] * ng
    out_shape = [jax.ShapeDtypeStruct((t_rows, w), dt) for w, dt in zip(gw, gdtypes)]
    out_shape += [jax.ShapeDtypeStruct(c.shape, F32) for c in consts]
    out_specs = [pl.BlockSpec((tile, w), lambda i: (i, 0)) for w in gw]
    out_specs += [_const_spec(c) for c in consts]
    return _call(body, (*arrs, *consts, *ct_arrs), name=name, grid=(t_rows // tile,),
                 in_specs=list(specs) + [_const_spec(c) for c in consts] + list(ct_specs),
                 out_specs=out_specs, out_shape=out_shape, sem=("arbitrary",), job=job)


def _layer_norm(x, g, b):
    mu = jnp.mean(x, axis=-1, keepdims=True)
    xc = x - mu
    var = jnp.mean(xc * xc, axis=-1, keepdims=True)
    return xc * lax.rsqrt(var + LN_EPS) * g + b


def _sigmoid(x):
    return 1.0 / (1.0 + jnp.exp(-x))


def _fn_ln(x, g, b):
    return (_layer_norm(x, g, b),)


def _fn_rms(x, g):
    return (x * lax.rsqrt(jnp.mean(x * x, axis=-1, keepdims=True) + RMS_EPS) * g,)


def _make_post_mix(alpha):
    def fn(h, mix, g, b):
        return (_layer_norm(alpha * h + mix, g, b),)
    return fn


def _make_ple_ln(alpha):
    def fn(h1, ffn, pg, pp, g, b):
        return (_layer_norm(alpha * h1 + ffn + _sigmoid(pg) * pp, g, b),)
    return fn


def _fn_lower_bounds(l0, l1):
    m = jnp.maximum(l0, l1)
    e0, e1 = jnp.exp(l0 - m), jnp.exp(l1 - m)
    s = e0 + e1
    p0, p1 = e0 / s, e1 / s
    return (p0 - p0, (p0 + p1) - p0)


def _loss_and_grad(y, target, *, name):
    d = y.shape[1]

    def fn(yv, tv):
        err = yv - tv
        return err * (1.0 / d), 0.5 * jnp.sum(jnp.mean(err * err, axis=-1, keepdims=True), axis=0, keepdims=True)

    return _rowwise(fn, [y, target], [], [(d, F32)], accs=[(1, 1)], name=name)


def _split_dot(x, e_bf16):
    hi = x.astype(BF16)
    lo = (x - hi.astype(F32)).astype(BF16)
    return (jnp.dot(hi, e_bf16, preferred_element_type=F32) + jnp.dot(lo, e_bf16, preferred_element_type=F32))


def _hgrn_common(th):
    rm = lax.broadcasted_iota(jnp.int32, (th, HG_W), 0) % HG_CHUNK

    def seg_cumsum(x):
        for s in (1, 2, 4, 8):
            x = x + jnp.where(rm >= s, pltpu.roll(x, s, 0), 0.0)
        return x

    def seg_rcumsum(x):
        for s in (1, 2, 4, 8):
            x = x + jnp.where(rm < HG_CHUNK - s, pltpu.roll(x, th - s, 0), 0.0)
        return x

    ri = lax.broadcasted_iota(jnp.int32, (HG_W, HG_W), 0) // HEAD
    ci = lax.broadcasted_iota(jnp.int32, (HG_W, HG_W), 1) // HEAD
    head_f32 = (ri == ci).astype(F32)
    head_bf16 = head_f32.astype(BF16)

    def headsum(x):
        return _split_dot(x, head_bf16)

    return rm, seg_cumsum, seg_rcumsum, head_f32, headsum


def _hgrn_gates(qr, fl, lb):
    sg = _sigmoid(fl)
    f = lb + (1.0 - lb) * sg
    sq = _sigmoid(qr)
    return sg, f, jnp.log(f), 1.0 - f, qr * sq, sq


def _shifted(x, d, th):
    return x if d == 0 else pltpu.roll(x, d, 0)


def _unshift(x, d, th):
    return x if d == 0 else pltpu.roll(x, th - d, 0)


def _hgrn_fwd(projp, lb, ng, *, name, job=None):
    t_rows = projp.shape[0]
    th = min(HG_TILE, t_rows)
    nct = th // HG_CHUNK

    def body(q_ref, f_ref, i_ref, g_ref, lb_ref, ng_ref, oa_ref, opre_ref, st_out_ref,
             st_ref, vtm_ref, kv_ref, qe_ref, dec_ref, oint_ref):
        rm, seg_cumsum, seg_rcumsum, head_f32, headsum = _hgrn_common(th)

        @pl.when(pl.program_id(0) == 0)
        def _():
            st_ref[...] = jnp.zeros_like(st_ref)

        qr, fl, v, g = q_ref[...], f_ref[...], i_ref[...], g_ref[...]
        _, f, lf, k, q, _ = _hgrn_gates(qr, fl, lb_ref[...])
        b = seg_cumsum(lf)

        o = jnp.zeros((th, HG_W), F32)
        for d in range(HG_CHUNK):
            kd, bd, vd = _shifted(k, d, th), _shifted(b, d, th), _shifted(v, d, th)
            e = jnp.exp(jnp.where(rm >= d, b - bd, -1e30))
            o = o + headsum(q * kd * e) * vd

        blast = seg_rcumsum(jnp.where(rm == HG_CHUNK - 1, b, 0.0))
        kte = (k * jnp.exp(blast - b)).astype(BF16)
        qe_ref[...] = q * jnp.exp(b)
        dec_ref[...] = jnp.exp(blast)
        vt = v.T
        lane_chunk = lax.broadcasted_iota(jnp.int32, (HG_W, th), 1) // HG_CHUNK
        for c in range(nct):
            vtm_ref[c * HG_W:(c + 1) * HG_W, :] = jnp.where(lane_chunk == c, vt, 0.0).astype(BF16)
        kv_ref[...] = jnp.dot(vtm_ref[...], kte, preferred_element_type=F32)

        s = st_ref[...]
        for c in range(nct):
            rows = slice(c * HG_CHUNK, (c + 1) * HG_CHUNK)
            st_out_ref[c] = s
            oint_ref[rows, :] = lax.dot_general(qe_ref[rows, :].astype(BF16), s.astype(BF16),
                                                (((1,), (1,)), ((), ())), preferred_element_type=F32)
            dec = jnp.max(dec_ref[rows, :], axis=0, keepdims=True)
            s = s * dec + kv_ref[c * HG_W:(c + 1) * HG_W, :] * head_f32
        st_ref[...] = s

        o = o + oint_ref[...]
        opre_ref[...] = o
        r = lax.rsqrt(headsum(o * o) * (1.0 / HEAD) + RMS_EPS)
        oa_ref[...] = (o * r * ng_ref[...] * (g * _sigmoid(g))).astype(oa_ref.dtype)

    col = lambda j: pl.BlockSpec((th, HG_W), lambda i, j=j: (i, j))
    vec = pl.BlockSpec((1, HG_W), lambda i: (0, 0))
    row = pl.BlockSpec((th, HG_W), lambda i: (i, 0))
    n_chunks = t_rows // HG_CHUNK
    return _call(
        body, (projp, projp, projp, projp, lb, ng), name=name, grid=(t_rows // th,),
        in_specs=[col(0), col(1), col(2), col(3), vec, vec],
        out_specs=[row, row, pl.BlockSpec((nct, HG_W, HG_W), lambda i: (i, 0, 0))],
        out_shape=[jax.ShapeDtypeStruct((t_rows, HG_W), BF16), jax.ShapeDtypeStruct((t_rows, HG_W), F32),
                   jax.ShapeDtypeStruct((n_chunks, HG_W, HG_W), F32)],
        scratch_shapes=[pltpu.VMEM((HG_W, HG_W), F32), pltpu.VMEM((nct * HG_W, th), BF16),
                        pltpu.VMEM((nct * HG_W, HG_W), F32), pltpu.VMEM((th, HG_W), F32),
                        pltpu.VMEM((th, HG_W), F32), pltpu.VMEM((th, HG_W), F32)],
        sem=("arbitrary",), job=job)


def _hgrn_bwd(projp, lb, ng, opre, states, dcat, *, name):
    t_rows = projp.shape[0]
    th = min(HG_TILE, t_rows)
    nct = th // HG_CHUNK
    nt = t_rows // th

    def body(q_ref, f_ref, i_ref, g_ref, lb_ref, ng_ref, opre_ref, st_in_ref, do_ref,
             dproj_ref, dng_ref, dlb_ref,
             gst_ref, dotm_ref, qg_ref, v_ref, kte_ref, dop_ref, dec_ref, dkte_ref, dvi_ref, dqe_ref, ddec_ref):
        rm, seg_cumsum, seg_rcumsum, head_f32, headsum = _hgrn_common(th)

        @pl.when(pl.program_id(0) == 0)
        def _():
            gst_ref[...] = jnp.zeros_like(gst_ref)
            dng_ref[...] = jnp.zeros_like(dng_ref)
            dlb_ref[...] = jnp.zeros_like(dlb_ref)

        qr, fl, v, g = q_ref[...], f_ref[...], i_ref[...], g_ref[...]
        lb, ngv = lb_ref[...], ng_ref[...]
        sg, f, lf, k, q, sq = _hgrn_gates(qr, fl, lb)
        b = seg_cumsum(lf)
        blast = seg_rcumsum(jnp.where(rm == HG_CHUNK - 1, b, 0.0))
        eb = jnp.exp(b)
        ekb = jnp.exp(blast - b)
        qe, kte, dec = q * eb, k * ekb, jnp.exp(blast)

        do_out, op = do_ref[...], opre_ref[...]
        sgg = _sigmoid(g)
        sil = g * sgg
        r = lax.rsqrt(headsum(op * op) * (1.0 / HEAD) + RMS_EPS)
        on = op * r
        dng_ref[...] += jnp.sum(do_out * on * sil, axis=0, keepdims=True)
        dg = do_out * on * ngv * (sgg * (1.0 + g * (1.0 - sgg)))
        don = do_out * ngv * sil
        dop = r * (don - on * (headsum(don * on) * (1.0 / HEAD)))

        v_ref[...] = v
        kte_ref[...] = kte
        dop_ref[...] = dop
        dec_ref[...] = dec
        dot_t = dop.T
        lane_chunk = lax.broadcasted_iota(jnp.int32, (HG_W, th), 1) // HG_CHUNK
        for c in range(nct):
            dotm_ref[c * HG_W:(c + 1) * HG_W, :] = jnp.where(lane_chunk == c, dot_t, 0.0).astype(BF16)
        qg_ref[...] = jnp.dot(dotm_ref[...], qe.astype(BF16), preferred_element_type=F32)

        gs = gst_ref[...]
        for c in reversed(range(nct)):
            rows = slice(c * HG_CHUNK, (c + 1) * HG_CHUNK)
            s = st_in_ref[c]
            gm = (gs * head_f32).astype(BF16)
            dkte_ref[rows, :] = jnp.dot(v_ref[rows, :].astype(BF16), gm, preferred_element_type=F32)
            dvi_ref[rows, :] = lax.dot_general(kte_ref[rows, :].astype(BF16), gm, (((1,), (1,)), ((), ())),
                                               preferred_element_type=F32)
            dqe_ref[rows, :] = jnp.dot(dop_ref[rows, :].astype(BF16), s.astype(BF16), preferred_element_type=F32)
            ddec_ref[rows, :] = jnp.broadcast_to(jnp.sum(gs * s, axis=0, keepdims=True), (HG_CHUNK, HG_W))
            dec_c = jnp.max(dec_ref[rows, :], axis=0, keepdims=True)
            gs = gs * dec_c + qg_ref[c * HG_W:(c + 1) * HG_W, :] * head_f32
        gst_ref[...] = gs

        dkte, dqe = dkte_ref[...], dqe_ref[...]
        dq = dqe * eb
        dk = dkte * ekb
        db = dqe * qe - dkte * kte
        dv = dvi_ref[...]
        dblast = dkte * kte + jnp.where(rm == HG_CHUNK - 1, ddec_ref[...] * dec, 0.0)

        for d in range(HG_CHUNK):
            kd, bd, vd = _shifted(k, d, th), _shifted(b, d, th), _shifted(v, d, th)
            e = jnp.exp(jnp.where(rm >= d, b - bd, -1e30))
            p = q * kd * e
            sc = headsum(p)
            dsc = headsum(dop * vd)
            dv = dv + _unshift(sc * dop, d, th)
            dq = dq + dsc * kd * e
            dk = dk + _unshift(dsc * q * e, d, th)
            darg = dsc * p
            db = db + darg - _unshift(darg, d, th)

        db = db + jnp.where(rm == HG_CHUNK - 1, seg_cumsum(dblast), 0.0)
        dlf = seg_rcumsum(db)
        df = dlf / f - dk
        dlb_ref[...] += jnp.sum(df * (1.0 - sg), axis=0, keepdims=True)
        dfl = df * (1.0 - lb) * sg * (1.0 - sg)
        dqr = dq * (sq * (1.0 + qr * (1.0 - sq)))
        dproj_ref[...] = jnp.concatenate([dqr, dfl, dv, dg], axis=1).astype(dproj_ref.dtype)

    rev = lambda i: nt - 1 - i
    col = lambda j: pl.BlockSpec((th, HG_W), lambda i, j=j: (rev(i), j))
    vec = pl.BlockSpec((1, HG_W), lambda i: (0, 0))
    row = pl.BlockSpec((th, HG_W), lambda i: (rev(i), 0))
    tile_f32 = pltpu.VMEM((th, HG_W), F32)
    return pl.pallas_call(
        body, name=name, grid=(nt,),
        in_specs=[col(0), col(1), col(2), col(3), vec, vec, row,
                  pl.BlockSpec((nct, HG_W, HG_W), lambda i: (rev(i), 0, 0)), col(0)],
        out_specs=[pl.BlockSpec((th, 4 * HG_W), lambda i: (rev(i), 0)), vec, vec],
        out_shape=[jax.ShapeDtypeStruct((t_rows, 4 * HG_W), BF16), jax.ShapeDtypeStruct((1, HG_W), F32),
                   jax.ShapeDtypeStruct((1, HG_W), F32)],
        scratch_shapes=[pltpu.VMEM((HG_W, HG_W), F32), pltpu.VMEM((nct * HG_W, th), BF16),
                        pltpu.VMEM((nct * HG_W, HG_W), F32)] + [tile_f32] * 8,
        compiler_params=_cparams(("arbitrary",)),
    )(projp, projp, projp, projp, lb, ng, opre, states, dcat)


_INV_SQRT2 = 1.0 / math.sqrt(2.0)
_INV_SQRT2PI = 1.0 / math.sqrt(2.0 * math.pi)


def _gelu(x):
    return 0.5 * x * (1.0 + lax.erf(x * _INV_SQRT2))


def _gelu_grad(x):
    return 0.5 * (1.0 + lax.erf(x * _INV_SQRT2)) + x * jnp.exp(-0.5 * x * x) * _INV_SQRT2PI


def _sgu_parts(bu, bv, lg, lbias, w_ref, n_groups):
    c = SGU_CHUNK
    tril = (lax.broadcasted_iota(jnp.int32, (c, c), 0) >= lax.broadcasted_iota(jnp.int32, (c, c), 1)).astype(F32)
    gid = lax.broadcasted_iota(jnp.int32, bu.shape, 1) // HEAD
    u = _gelu(bu)
    gv = _gelu(bv)
    mu = jnp.mean(gv, axis=-1, keepdims=True)
    xc = gv - mu
    rstd = lax.rsqrt(jnp.mean(xc * xc, axis=-1, keepdims=True) + LN_EPS)
    xhat = xc * rstd
    vn = xhat * lg + lbias
    ws = [w_ref[gi] * tril for gi in range(n_groups)]
    return tril, gid, u, rstd, xhat, vn, ws


def _sgu_fwd(projp, lg, lbias, w_s, bias_full, *, name):
    t_rows = projp.shape[0]
    n_groups = w_s.shape[0]
    c = SGU_CHUNK

    def body(u_ref, v_ref, lg_ref, lb_ref, w_ref, bias_ref, o_ref):
        _, gid, u, _, _, vn, ws = _sgu_parts(u_ref[...], v_ref[...], lg_ref[...], lb_ref[...], w_ref, n_groups)
        vnb = vn.astype(BF16)
        z = bias_ref[...]
        for gi in range(n_groups):
            z = z + jnp.where(gid == gi, jnp.dot(ws[gi].astype(BF16), vnb, preferred_element_type=F32), 0.0)
        o_ref[...] = (u * z).astype(o_ref.dtype)

    col = lambda j: pl.BlockSpec((c, HG_W), lambda i, j=j: (i, j))
    return pl.pallas_call(
        body, name=name, grid=(t_rows // c,),
        in_specs=[col(4), col(5), _const_spec(lg), _const_spec(lbias), _const_spec(w_s), _const_spec(bias_full)],
        out_specs=pl.BlockSpec((c, HG_W), lambda i: (i, 0)),
        out_shape=jax.ShapeDtypeStruct((t_rows, HG_W), BF16),
        compiler_params=_cparams(("arbitrary",)),
    )(projp, projp, lg, lbias, w_s, bias_full)


def _sgu_bwd(projp, lg, lbias, w_s, bias_full, dcat, *, name):
    t_rows = projp.shape[0]
    n_groups = w_s.shape[0]
    c = SGU_CHUNK
    n = t_rows // c

    def body(u_ref, v_ref, lg_ref, lb_ref, w_ref, bias_ref, do_ref,
             dproj_ref, dlg_ref, dlb_ref, dw_ref, dbs_ref, dbias_acc):
        i = pl.program_id(0)

        @pl.when(i == 0)
        def _():
            dlg_ref[...] = jnp.zeros_like(dlg_ref)
            dlb_ref[...] = jnp.zeros_like(dlb_ref)
            dw_ref[...] = jnp.zeros_like(dw_ref)
            dbias_acc[...] = jnp.zeros_like(dbias_acc)

        bu, bv, lg_v = u_ref[...], v_ref[...], lg_ref[...]
        tril, gid, u, rstd, xhat, vn, ws = _sgu_parts(bu, bv, lg_v, lb_ref[...], w_ref, n_groups)
        vnb = vn.astype(BF16)
        z = bias_ref[...]
        for gi in range(n_groups):
            z = z + jnp.where(gid == gi, jnp.dot(ws[gi].astype(BF16), vnb, preferred_element_type=F32), 0.0)
        do = do_ref[...]
        dbu = do * z * _gelu_grad(bu)
        dz = do * u
        dbias_acc[...] += dz
        dvn = jnp.zeros_like(dz)
        for gi in range(n_groups):
            dzg = jnp.where(gid == gi, dz, 0.0).astype(BF16)
            dw_ref[gi] += lax.dot_general(dzg, vnb, (((1,), (1,)), ((), ())), preferred_element_type=F32) * tril
            dvn = dvn + jnp.dot(ws[gi].T.astype(BF16), dzg, preferred_element_type=F32)
        dlg_ref[...] += jnp.sum(dvn * xhat, axis=0, keepdims=True)
        dlb_ref[...] += jnp.sum(dvn, axis=0, keepdims=True)
        dxh = dvn * lg_v
        dgv = rstd * (dxh - jnp.mean(dxh, axis=-1, keepdims=True)
                      - xhat * jnp.mean(dxh * xhat, axis=-1, keepdims=True))
        dproj_ref[...] = jnp.concatenate([dbu, dgv * _gelu_grad(bv)], axis=1).astype(dproj_ref.dtype)

        @pl.when(i == n - 1)
        def _():
            dbs_ref[...] = jnp.sum(dbias_acc[...].T.reshape(n_groups, HEAD, c), axis=1)

    col = lambda j: pl.BlockSpec((c, HG_W), lambda i, j=j: (i, j))
    return pl.pallas_call(
        body, name=name, grid=(n,),
        in_specs=[col(4), col(5), _const_spec(lg), _const_spec(lbias), _const_spec(w_s), _const_spec(bias_full),
                  col(1)],
        out_specs=[pl.BlockSpec((c, 2 * HG_W), lambda i: (i, 0)), _const_spec(lg), _const_spec(lbias),
                   _const_spec(w_s), pl.BlockSpec((n_groups, c), lambda i: (0, 0))],
        out_shape=[jax.ShapeDtypeStruct((t_rows, 2 * HG_W), BF16), jax.ShapeDtypeStruct(lg.shape, F32),
                   jax.ShapeDtypeStruct(lbias.shape, F32), jax.ShapeDtypeStruct(w_s.shape, F32),
                   jax.ShapeDtypeStruct((n_groups, c), F32)],
        scratch_shapes=[pltpu.VMEM((c, HG_W), F32)],
        compiler_params=_cparams(("arbitrary",)),
    )(projp, projp, lg, lbias, w_s, bias_full, dcat)


def _rope_tables(positions):
    t = positions.shape[0]
    inv_freq = ROPE_THETA ** (-jnp.arange(0, 32, 2, dtype=F32) / 32)
    ang = positions.astype(F32)[:, None] * inv_freq
    cos, sin = jnp.cos(ang), jnp.sin(ang)
    z = lambda w: jnp.zeros((t, w), F32)
    cos_t = jnp.concatenate([jnp.ones((t, 64), F32), cos, cos, z(32)], axis=1)
    sin_up = jnp.concatenate([z(80), sin, z(32)], axis=1)
    sin_dn = jnp.concatenate([z(64), -sin, z(48)], axis=1)
    return cos_t, sin_up, sin_dn


def _rep(x, n):
    return x if n == 1 else jnp.concatenate([x] * n, axis=1)


def _rope(x, cos_t, sin_up, sin_dn):
    w = x.shape[1]
    return x * cos_t + pltpu.roll(x, 16, 1) * sin_up + pltpu.roll(x, w - 16, 1) * sin_dn


def _rope_t(dy, cos_t, sin_up, sin_dn):
    w = dy.shape[1]
    return dy * cos_t + pltpu.roll(dy * sin_up, w - 16, 1) + pltpu.roll(dy * sin_dn, 16, 1)


def _mla_prep(q, kv, projp, tables, *, name):
    nh = N_ATT_HEADS

    def fn(qv, kvv, kr, cos_t, sin_up, sin_dn):
        qr = _rope(qv, _rep(cos_t, nh), _rep(sin_up, nh), _rep(sin_dn, nh))
        krr = _rope(kr, cos_t, sin_up, sin_dn)
        lane = lax.broadcasted_iota(jnp.int32, kvv.shape, 1) % LANES
        return qr, jnp.where(lane < HEAD, kvv, 0.0) + _rep(krr, nh), kvv

    w = q.shape[1]
    return _rowwise(fn, [q, kv, (projp, LANES, P_KR // LANES)] + list(tables), [],
                    [(w, BF16), (w, BF16), (w, BF16)], name=name)


def _mla_prep_bwd(dqr, dkf, tables, *, name):
    nh = N_ATT_HEADS

    def fn(dq, dk, cos_t, sin_up, sin_dn):
        dqp = _rope_t(dq, _rep(cos_t, nh), _rep(sin_up, nh), _rep(sin_dn, nh))
        dkrr = dk[:, 0:LANES]
        for h in range(1, nh):
            dkrr = dkrr + dk[:, LANES * h:LANES * (h + 1)]
        return dqp, _rope_t(dkrr, cos_t, sin_up, sin_dn)

    return _rowwise(fn, [dqr, dkf] + list(tables), [], [(dqr.shape[1], BF16), (LANES, BF16)], name=name)


_LOG2E = 1.0 / math.log(2.0)
_NT = (((1,), (1,)), ((), ()))
_TN = (((0,), (0,)), ((), ()))


def _attn_fwd(qr, kf, kvb, *, name, job=None):
    t_rows = qr.shape[0]
    tq = min(ATT_TQ, t_rows)
    nb = t_rows // tq
    scale = ATT_D ** -0.5

    c2 = scale * _LOG2E

    def body(q_ref, kf_ref, kvb_ref, o_ref, lse_ref):
        qi = pl.program_id(1)
        lane = lax.broadcasted_iota(jnp.int32, (tq, LANES), 1)
        causal_t = (lax.broadcasted_iota(jnp.int32, (tq, tq), 0) <= lax.broadcasted_iota(jnp.int32, (tq, tq), 1))
        heads = [slice(hh * LANES, (hh + 1) * LANES) for hh in range(2)]
        qs = [q_ref[:, cols] for cols in heads]

        def block(ki, carry, diagonal):
            rows = pl.ds(pl.multiple_of(ki * tq, tq), tq)
            new = []
            for q, cols, (m_old, l_old, acc_t) in zip(qs, heads, carry):
                s_t = lax.dot_general(kf_ref[rows, cols], q, _NT, preferred_element_type=F32)
                if diagonal:
                    s_t = jnp.where(causal_t, s_t, -1e30)
                m_new = jnp.maximum(m_old, jnp.max(s_t, axis=0, keepdims=True))
                p_t = jnp.exp2((s_t - m_new) * c2)
                a = jnp.exp2((m_old - m_new) * c2)
                pv_t = lax.dot_general(kvb_ref[rows, cols], p_t.astype(BF16), _TN, preferred_element_type=F32)
                new.append((m_new, a * l_old + jnp.sum(p_t, axis=0, keepdims=True), a * acc_t + pv_t))
            return tuple(new)

        init = (jnp.full((1, tq), -1e30, F32), jnp.zeros((1, tq), F32), jnp.zeros((LANES, tq), F32))
        carry = lax.fori_loop(0, qi, lambda ki, c: block(ki, c, False), (init, init))
        outs = []
        for hh, (m_fin, l_fin, acc_t) in enumerate(block(qi, carry, True)):
            lse_ref[hh] = m_fin * scale + jnp.log(l_fin)
            outs.append((acc_t / l_fin).T)
        o_ref[...] = jnp.where(lane < HEAD, pltpu.roll(outs[0], HEAD, 1), outs[1])

    pair = pl.BlockSpec((t_rows, 2 * LANES), lambda pr, qi: (0, pr))
    return _call(
        body, (qr, kf, kvb), name=name, grid=(N_ATT_HEADS // 2, nb),
        in_specs=[pl.BlockSpec((tq, 2 * LANES), lambda pr, qi: (qi, pr)), pair, pair],
        out_specs=[pl.BlockSpec((tq, LANES), lambda pr, qi: (qi, pr)),
                   pl.BlockSpec((2, 1, tq), lambda pr, qi: (pr, 0, qi))],
        out_shape=[jax.ShapeDtypeStruct((t_rows, N_ATT_HEADS * HEAD), F32),
                   jax.ShapeDtypeStruct((N_ATT_HEADS, 1, t_rows), F32)],
        sem=("parallel", "arbitrary"), job=job)


def _attn_bwd(qr, kf, kvb, dcat, o, lse, *, name, job=None):
    t_rows = qr.shape[0]
    tq = min(ATT_TQ, t_rows)
    nb = t_rows // tq
    scale = ATT_D ** -0.5
    c2 = scale * _LOG2E
    do_off = 2 * HG_W // LANES

    def body(q_ref, kf_ref, kvb_ref, do_ref, o_ref, lse_ref, dq_ref, dkv_ref, dk_ref):
        ki = pl.program_id(1)

        @pl.when(ki == 0)
        def _():
            dq_ref[...] = jnp.zeros_like(dq_ref)

        lane = lax.broadcasted_iota(jnp.int32, (tq, LANES), 1)
        causal_t = (lax.broadcasted_iota(jnp.int32, (tq, tq), 0) <= lax.broadcasted_iota(jnp.int32, (tq, tq), 1))
        heads = [slice(hh * LANES, (hh + 1) * LANES) for hh in range(2)]
        ks = [kf_ref[:, cols] for cols in heads]
        vs = [kvb_ref[:, cols] for cols in heads]

        def block(qi, carry, diagonal):
            rows = pl.ds(pl.multiple_of(qi * tq, tq), tq)
            do_pair, o_pair = do_ref[rows, :], o_ref[rows, :]
            new = []
            for hh, (cols, k, v, (dk, dv)) in enumerate(zip(heads, ks, vs, carry)):
                q = q_ref[rows, cols]
                do, ov = (pltpu.roll(do_pair, HEAD, 1), pltpu.roll(o_pair, HEAD, 1)) if hh == 0 else (do_pair, o_pair)
                do = jnp.where(lane >= HEAD, do, 0.0)
                delta = jnp.sum((do * ov).T, axis=0, keepdims=True)
                s_t = lax.dot_general(k, q, _NT, preferred_element_type=F32)
                if diagonal:
                    s_t = jnp.where(causal_t, s_t, -1e30)
                p_t = jnp.exp2(s_t * c2 - lse_ref[hh, :, rows] * _LOG2E)
                dob = do.astype(BF16)
                dv = dv + jnp.dot(p_t.astype(BF16), dob, preferred_element_type=F32)
                dp_t = lax.dot_general(v, dob, _NT, preferred_element_type=F32)
                ds_t = (p_t * (dp_t - delta) * scale).astype(BF16)
                dk = dk + jnp.dot(ds_t, q, preferred_element_type=F32)
                dq_ref[rows, cols] += lax.dot_general(ds_t, k, _TN, preferred_element_type=F32)
                new.append((dk, dv))
            return tuple(new)

        zero = jnp.zeros((tq, LANES), F32)
        carry = block(ki, ((zero, zero), (zero, zero)), True)
        carry = lax.fori_loop(ki + 1, nb, lambda qi, c: block(qi, c, False), carry)
        dkv_ref[...] = jnp.concatenate([jnp.where(lane < HEAD, dk, dv) for dk, dv in carry],
                                       axis=1).astype(dkv_ref.dtype)
        dk_ref[...] = jnp.concatenate([dk for dk, _ in carry], axis=1)

    pair_all = pl.BlockSpec((t_rows, 2 * LANES), lambda pr, ki: (0, pr))
    pair_blk = pl.BlockSpec((tq, 2 * LANES), lambda pr, ki: (ki, pr))
    wide = jax.ShapeDtypeStruct((t_rows, N_ATT_HEADS * LANES), F32)
    return _call(
        body, (qr, kf, kvb, dcat, o, lse), name=name, grid=(N_ATT_HEADS // 2, nb),
        in_specs=[pair_all, pair_blk, pair_blk,
                  pl.BlockSpec((t_rows, LANES), lambda pr, ki: (0, do_off + pr)),
                  pl.BlockSpec((t_rows, LANES), lambda pr, ki: (0, pr)),
                  pl.BlockSpec((2, 1, t_rows), lambda pr, ki: (pr, 0, 0))],
        out_specs=[pair_all, pair_blk, pair_blk],
        out_shape=[wide, jax.ShapeDtypeStruct(wide.shape, BF16), wide],
        sem=("parallel", "arbitrary"), job=job)


def _my_pos():
    return lax.axis_index("x"), lax.axis_index("y"), lax.axis_index("c")


def _all_gather(xs, *, name):
    return _gather_forward(_run_job(_gather_job(xs), name=name), name=name + "_forward")


def _remote(src, dst, send_sems, recv_sems, k, dev):
    return pltpu.make_async_remote_copy(src_ref=src, dst_ref=dst, send_sem=send_sems.at[k], recv_sem=recv_sems.at[k],
                                        device_id=dev, device_id_type=MESH)


def _gather_job(xs):
    n = len(xs)

    def make(x_refs, out_refs, send_sems, recv_sems, local_sems):
        mx, my, mc = _my_pos()
        mine = 4 * mx + 2 * my + mc
        peers = [(mx, my, 1 - mc), (1 - mx, my, mc), (mx, 1 - my, mc), (1 - mx, 1 - my, mc)]
        sends, recvs, local = [], [], []
        for a in range(n):
            local.append(pltpu.make_async_copy(x_refs[a], out_refs[a].at[mine], local_sems.at[a]))
            for k, dev in enumerate(peers):
                theirs = 4 * dev[0] + 2 * dev[1] + dev[2]
                sends.append(_remote(x_refs[a], out_refs[a].at[mine], send_sems, recv_sems, 4 * a + k, dev))
                recvs.append(_remote(x_refs[a], out_refs[a].at[theirs], send_sems, recv_sems, 4 * a + k, dev))
        return sends, recvs, local

    shapes = [jax.ShapeDtypeStruct((N_DEV,) + x.shape, x.dtype) for x in xs]
    return _copies_job(xs, shapes, 4 * n, n, make)


def _gather_forward(gs, *, name):
    n = len(gs)

    def body(*refs):
        out_refs = refs[n:2 * n]
        send_sems, recv_sems = refs[2 * n:]
        mx, my, mc = _my_pos()
        chips = [(1 - mx, my), (mx, 1 - my), (1 - mx, 1 - my)]
        sends, recvs = [], []
        for a in range(n):
            for j, (cx, cy) in enumerate(chips):
                here, there = out_refs[a].at[4 * cx + 2 * cy + mc], out_refs[a].at[4 * cx + 2 * cy + 1 - mc]
                sends.append(_remote(here, here, send_sems, recv_sems, 3 * a + j, (mx, my, 1 - mc)))
                recvs.append(_remote(here, there, send_sems, recv_sems, 3 * a + j, (mx, my, 1 - mc)))
        for cp in sends:
            cp.start()
        for cp in recvs:
            cp.wait_recv()
        for cp in sends:
            cp.wait_send()

    return pl.pallas_call(
        body, name=name, out_shape=[jax.ShapeDtypeStruct(g.shape, g.dtype) for g in gs],
        in_specs=[_ANY] * n, out_specs=[_ANY] * n, input_output_aliases={a: a for a in range(n)},
        scratch_shapes=[pltpu.SemaphoreType.DMA((3 * n,)), pltpu.SemaphoreType.DMA((3 * n,))],
    )(*gs)


def _pair_job(xs):
    n = len(xs)

    def make(x_refs, out_refs, send_sems, recv_sems, local_sems):
        mx, my, mc = _my_pos()
        copies = [_remote(x_refs[a].at[g, 1 - mc], out_refs[a].at[g], send_sems, recv_sems, 4 * a + g, (mx, my, 1 - mc))
                  for a in range(n) for g in range(4)]
        return copies, copies, []

    shapes = [jax.ShapeDtypeStruct((4,) + x.shape[2:], x.dtype) for x in xs]
    return _copies_job(xs, shapes, 4 * n, 0, make)


def _pair_add(x, r, core, *, name):
    _, _, a, b = x.shape
    ta = _row_tile(a, 256)

    def body(c_ref, x_ref, r_ref, o_ref):
        o_ref[...] = x_ref[...] + r_ref[...]

    blk = pl.BlockSpec((None, ta, b), lambda g, i, c_ref: (g, i, 0))
    return pl.pallas_call(
        body, name=name,
        grid_spec=pltpu.PrefetchScalarGridSpec(
            num_scalar_prefetch=1, grid=(4, a // ta),
            in_specs=[pl.BlockSpec((None, None, ta, b), lambda g, i, c_ref: (g, c_ref[0], i, 0)), blk],
            out_specs=blk),
        out_shape=jax.ShapeDtypeStruct((4, a, b), x.dtype),
        compiler_params=_cparams(("parallel", "parallel")),
    )(core, x, r)


def _quad_job(xs):
    n = len(xs)

    def make(x_refs, out_refs, send_sems, recv_sems, local_sems):
        mx, my, mc = _my_pos()
        mine = 2 * mx + my
        peers = [((1 - mx, my, mc), 2 * (1 - mx) + my), ((mx, 1 - my, mc), 2 * mx + 1 - my),
                 ((1 - mx, 1 - my, mc), 2 * (1 - mx) + 1 - my)]
        sends, recvs, local = [], [], []
        for a in range(n):
            local.append(pltpu.make_async_copy(x_refs[a].at[mine], out_refs[a].at[mine], local_sems.at[a]))
            for k, (dev, g) in enumerate(peers):
                sends.append(_remote(x_refs[a].at[g], out_refs[a].at[mine], send_sems, recv_sems, 3 * a + k, dev))
                recvs.append(_remote(x_refs[a].at[g], out_refs[a].at[g], send_sems, recv_sems, 3 * a + k, dev))
        return sends, recvs, local

    shapes = [jax.ShapeDtypeStruct(x.shape, x.dtype) for x in xs]
    return _copies_job(xs, shapes, 3 * n, n, make)


def _row_tile(r, pref):
    t = min(pref, r)
    while r % t or (t % 8 and t != r):
        t -= 1
    return t


def _adamw(parts, w, m, v, layer, *, name, tile=256):
    g, a, b = parts.shape
    tile = _row_tile(a, tile)
    c1 = 1.0 / (1.0 - ADAM_B1 ** ADAM_STEP)
    c2 = 1.0 / (1.0 - ADAM_B2 ** ADAM_STEP)

    def body(p_ref, w_ref, m_ref, v_ref, g_ref, d_ref, mo_ref, vo_ref):
        grad = p_ref[0]
        for j in range(1, g):
            grad = grad + p_ref[j]
        mn = ADAM_B1 * m_ref[...] + (1.0 - ADAM_B1) * grad
        vn = ADAM_B2 * v_ref[...] + (1.0 - ADAM_B2) * (grad * grad)
        g_ref[...] = grad
        mo_ref[...] = mn
        vo_ref[...] = vn
        d_ref[...] = -ADAM_LR * ((mn * c1) / (jnp.sqrt(vn * c2) + ADAM_EPS) + ADAM_WD * w_ref[...])

    slab = pl.BlockSpec((tile, b), lambda i: (i, 0))
    src = slab if layer is None else pl.BlockSpec((None, tile, b), lambda i: (layer, i, 0))
    return pl.pallas_call(
        body, name=name, grid=(a // tile,),
        in_specs=[pl.BlockSpec((g, tile, b), lambda i: (0, i, 0)), src, src, src],
        out_specs=[slab] * 4,
        out_shape=[jax.ShapeDtypeStruct((a, b), F32)] * 4,
        compiler_params=_cparams(("parallel",)),
    )(parts, w, m, v)


W_IN_SHARD = 276


def _w_in_dest(col):
    return jnp.where(col < P_KR, col, jnp.where(col < P_KR + 256, col + (P_CKV - P_KR), col - 2176 + P_KR + HEAD))


def _place_w_in(g, *, name):
    _, d, sh = g.shape
    tc = 768

    def body(g_ref, o_ref, acc_ref):
        ct, j = pl.program_id(0), pl.program_id(1)

        @pl.when(j == 0)
        def _():
            acc_ref[...] = jnp.zeros_like(acc_ref)

        src = j * sh + lax.broadcasted_iota(jnp.int32, (sh, tc), 0)
        dst = ct * tc + lax.broadcasted_iota(jnp.int32, (sh, tc), 1)
        place = (_w_in_dest(src) == dst).astype(BF16)
        acc_ref[...] += jnp.dot(g_ref[...], place, preferred_element_type=F32)

        @pl.when(j == N_DEV - 1)
        def _():
            o_ref[...] = acc_ref[...].astype(o_ref.dtype)

    return pl.pallas_call(
        body, name=name, grid=(P_COLS // tc, N_DEV),
        in_specs=[pl.BlockSpec((None, d, sh), lambda ct, j: (j, 0, 0))],
        out_specs=pl.BlockSpec((d, tc), lambda ct, j: (0, ct)),
        out_shape=jax.ShapeDtypeStruct((d, P_COLS), BF16),
        scratch_shapes=[pltpu.VMEM((d, tc), F32)],
        compiler_params=_cparams(("parallel", "arbitrary")),
    )(g)


def _unplace_w_in(dw, *, name):
    d = dw.shape[0]
    sh = W_IN_SHARD

    def body(dw_ref, o_ref):
        j = pl.program_id(0)
        src = j * sh + lax.broadcasted_iota(jnp.int32, (P_COLS, sh), 1)
        dst = lax.broadcasted_iota(jnp.int32, (P_COLS, sh), 0)
        pick = (_w_in_dest(src) == dst).astype(BF16)
        x = dw_ref[...]
        o_ref[...] = _split_dot(x, pick)

    return pl.pallas_call(
        body, name=name, grid=(N_DEV,),
        in_specs=[pl.BlockSpec((d, P_COLS), lambda j: (0, 0))],
        out_specs=pl.BlockSpec((None, d, sh), lambda j: (j, 0, 0)),
        out_shape=jax.ShapeDtypeStruct((N_DEV, d, sh), F32),
        compiler_params=_cparams(("arbitrary",)),
    )(dw)


def _gate_up_swiglu(h1, wgu, *, name):
    t_rows, k = h1.shape
    w = wgu.shape[2]
    tm = _tile(t_rows, 1024)

    def body(a_ref, wg_ref, wu_ref, gu_ref, act_ref):
        a = a_ref[...].astype(BF16)
        gate = jnp.dot(a, wg_ref[...], preferred_element_type=F32)
        up = jnp.dot(a, wu_ref[...], preferred_element_type=F32)
        gu_ref[0] = gate.astype(gu_ref.dtype)
        gu_ref[1] = up.astype(gu_ref.dtype)
        act_ref[...] = (gate * _sigmoid(gate) * up).astype(act_ref.dtype)

    return pl.pallas_call(
        body, name=name, grid=(t_rows // tm, 4),
        in_specs=[pl.BlockSpec((tm, k), lambda i, j: (i, 0)),
                  pl.BlockSpec((None, k, w), lambda i, j: (j, 0, 0)),
                  pl.BlockSpec((None, k, w), lambda i, j: (j + 4, 0, 0))],
        out_specs=[pl.BlockSpec((2, None, tm, w), lambda i, j: (0, j, i, 0)),
                   pl.BlockSpec((None, tm, w), lambda i, j: (j, i, 0))],
        out_shape=[jax.ShapeDtypeStruct((2, 4, t_rows, w), BF16), jax.ShapeDtypeStruct((4, t_rows, w), BF16)],
        compiler_params=_cparams(("parallel", "arbitrary")),
    )(h1, wgu, wgu)


def _down_dx_swiglu(dffn, wdown, gu, *, name):
    t_rows, k = dffn.shape
    w = gu.shape[3]
    tm = _tile(t_rows, 1024)

    def body(d_ref, w_ref, gu_ref, o_ref):
        dact = lax.dot_general(d_ref[...].astype(BF16), w_ref[...], _NT, preferred_element_type=F32)
        gate, up = gu_ref[0].astype(F32), gu_ref[1].astype(F32)
        sg = _sigmoid(gate)
        o_ref[0] = (dact * up * (sg * (1.0 + gate * (1.0 - sg)))).astype(o_ref.dtype)
        o_ref[1] = (dact * gate * sg).astype(o_ref.dtype)

    blk = pl.BlockSpec((2, None, tm, w), lambda i, j: (0, j, i, 0))
    return pl.pallas_call(
        body, name=name, grid=(t_rows // tm, 4),
        in_specs=[pl.BlockSpec((tm, k), lambda i, j: (i, 0)), pl.BlockSpec((w, k), lambda i, j: (j, 0)), blk],
        out_specs=blk, out_shape=jax.ShapeDtypeStruct(gu.shape, BF16),
        compiler_params=_cparams(("parallel", "arbitrary")),
    )(dffn, wdown, gu)


BIG = ("w_in", "mla_w_uq", "mla_w_ukv", "w_out", "w_gate_up", "w_down", "ple_w_gate", "ple_w_proj")
SMALL = ("ln_in_g", "ln_in_b", "hgrn_lb_logits", "hgrn_norm_g", "sgu_ln_g", "sgu_ln_b", "sgu_w_s", "sgu_b_s",
         "mla_q_norm_g", "mla_kv_norm_g", "ln1_g", "ln1_b", "ln2_g", "ln2_b")
ORDER = ("ln_in_g", "ln_in_b", "w_in", "hgrn_lb_logits", "hgrn_norm_g", "sgu_ln_g", "sgu_ln_b", "sgu_w_s", "sgu_b_s",
         "mla_q_norm_g", "mla_w_uq", "mla_kv_norm_g", "mla_w_ukv", "w_out", "ln1_g", "ln1_b", "w_gate_up", "w_down",
         "ple_w_gate", "ple_w_proj", "ln2_g", "ln2_b")


def _slab(a, align):
    s = a.reshape(-1, LANES)
    pad = -s.shape[0] % align
    return jnp.pad(s, ((0, pad), (0, 0))) if pad else s


def _pack(arrays, align=16, total_align=512):
    s = jnp.concatenate([_slab(a, align) for a in arrays], axis=0)
    pad = -s.shape[0] % total_align
    return jnp.pad(s, ((0, pad), (0, 0))) if pad else s


def _unpack(slab, shapes, align=16):
    out, r0 = [], 0
    for s in shapes:
        nr = math.prod(s) // LANES
        out.append(slab[r0:r0 + nr].reshape(s))
        r0 += nr + (-nr % align)
    return out


def _blocks_to_cols(g, *, name):
    nb, a, b = g.shape

    def body(g_ref, o_ref):
        o_ref[...] = g_ref[...]

    return pl.pallas_call(
        body, name=name, grid=(nb,), in_specs=[pl.BlockSpec((None, a, b), lambda j: (j, 0, 0))],
        out_specs=pl.BlockSpec((a, b), lambda j: (0, j)), out_shape=jax.ShapeDtypeStruct((a, nb * b), g.dtype),
        compiler_params=_cparams(("parallel",)),
    )(g)


def _cols_to_blocks(x, *, name):
    a, b = x.shape[0], x.shape[1] // N_DEV

    def body(x_ref, o_ref):
        o_ref[...] = x_ref[...]

    return pl.pallas_call(
        body, name=name, grid=(N_DEV,), in_specs=[pl.BlockSpec((a, b), lambda j: (0, j))],
        out_specs=pl.BlockSpec((None, a, b), lambda j: (j, 0, 0)), out_shape=jax.ShapeDtypeStruct((N_DEV, a, b), x.dtype),
        compiler_params=_cparams(("parallel",)),
    )(x)


def _weight_shards(w, li):
    uq_pad = ((0, 0), (0, LANES - ATT_D))
    shards = {k: w[k][li] for k in BIG}
    shards["mla_w_uq"] = jnp.pad(shards["mla_w_uq"], uq_pad)
    return {k: s.astype(BF16) for k, s in shards.items()}


def _usable_weights(g, *, name):
    out = {}
    for k, a in g.items():
        if k == "w_in":
            out[k] = _place_w_in(a, name=name + "_place_w_in")
        elif k in ("w_out", "w_down", "ple_w_gate"):
            out[k] = a.reshape(a.shape[0] * a.shape[1], a.shape[2])
        elif k == "w_gate_up":
            out[k] = a
        else:
            out[k] = _blocks_to_cols(a, name=name + "_cols_" + k)
    return out


def _as_pairs(g):
    if g.ndim == 2:
        return g.reshape((4, 2, g.shape[0] // N_DEV) + g.shape[1:])
    return g.reshape((4, 2) + g.shape[1:])


def _twice(fn):
    return lambda *a: fn(*a) * 2


def _layer_forward(li, h, hb, p_i, wts, sm, lbs, tables, alpha, hgrn_job=None, after_hgrn=None, attn_job=None,
                   after_attn=None):
    n = f"l{li}_"
    row1 = lambda a: a.reshape(1, -1)
    projp = _mm(hb, wts["w_in"], name=n + "proj")
    ng = row1(sm["hgrn_norm_g"][li])
    res = _hgrn_fwd(projp, lbs[li], ng, name=n + "hgrn_fwd", job=hgrn_job)
    if hgrn_job is not None:
        res, got = res
        wts = dict(wts, **after_hgrn(got))
    o_a, o_pre, states = res
    lg, lbias = row1(sm["sgu_ln_g"][li]), row1(sm["sgu_ln_b"][li])
    w_s = sm["sgu_w_s"][li]
    bias_full = jnp.repeat(sm["sgu_b_s"][li].T, HEAD, axis=1)
    o_b = _sgu_fwd(projp, lg, lbias, w_s, bias_full, name=n + "sgu_fwd")
    qg, kvg = row1(sm["mla_q_norm_g"][li]), row1(sm["mla_kv_norm_g"][li])
    cq_view, ckv_view = (projp, 384, P_CQ // 384), (projp, 256, P_CKV // 256)
    (cqn,) = _rowwise(_fn_rms, [cq_view], [qg], [(384, BF16)], name=n + "q_norm")
    (ckvn,) = _rowwise(_fn_rms, [ckv_view], [kvg], [(256, BF16)], name=n + "kv_norm")
    q = _mm(cqn, wts["mla_w_uq"], name=n + "uq")
    kv = _mm(ckvn, wts["mla_w_ukv"], name=n + "ukv")
    qr, kf, kvb = _mla_prep(q, kv, projp, tables, name=n + "mla_prep")
    res = _attn_fwd(qr, kf, kvb, name=n + "attn_fwd", job=attn_job)
    if attn_job is not None:
        res, got = res
        wts = dict(wts, **after_attn(got))
    o_c, lse = res
    cat = jnp.concatenate([o_a, o_b, o_c.astype(BF16)], axis=1)
    mix = _mm(cat, wts["w_out"], name=n + "out_proj")
    g1, b1 = row1(sm["ln1_g"][li]), row1(sm["ln1_b"][li])
    d = h.shape[1]
    h1, h1b = _rowwise(_twice(_make_post_mix(alpha)), [h, mix], [g1, b1], [(d, F32), (d, BF16)], name=n + "ln1")
    gu, act = _gate_up_swiglu(h1b, wts["w_gate_up"], name=n + "gate_up")
    ffn = _mm(act, wts["w_down"], am="bmk", name=n + "down")
    pg = _mm(h1b, wts["ple_w_gate"], name=n + "ple_gate")
    pp = _mm(p_i, wts["ple_w_proj"], name=n + "ple_proj")
    g2, b2 = row1(sm["ln2_g"][li]), row1(sm["ln2_b"][li])
    h2, h2b = _rowwise(_twice(_make_ple_ln(alpha)), [h1, ffn, pg, pp], [g2, b2], [(d, F32), (d, BF16)],
                       name=n + "ln2")
    saved = dict(h=h, hb=hb, h1b=h1b, projp=projp, o_pre=o_pre, states=states, cqn=cqn, ckvn=ckvn, qr=qr, kf=kf, kvb=kvb, o_c=o_c,
                 lse=lse, cat=cat, mix=mix, h1=h1, gu=gu, act=act, ffn=ffn, pg=pg, pp=pp, ng=ng, lg=lg, wts=wts,
                 lbias=lbias, w_s=w_s, bias_full=bias_full, qg=qg, kvg=kvg, g1=g1, b1=b1, g2=g2, b2=b2)
    return (h2, h2b), saved


RS_EARLY = ("ple_w_proj", "ple_w_gate", "w_down", "w_gate_up", "w_out")
RS_LATE = ("mla_w_uq", "mla_w_ukv", "w_in")


def _layer_backward(li, dh2_parts, p_i, sv, lbs, tables, alpha, core, carried=None):
    n = f"l{li}_b_"
    wts = sv["wts"]
    gr = {}
    dh1_a, dffn, dpg, dpp, gr["ln2_g"], gr["ln2_b"] = _rowwise_vjp(
        _make_ple_ln(alpha), [sv["h1"], sv["ffn"], sv["pg"], sv["pp"]], [sv["g2"], sv["b2"]], [dh2_parts],
        groups=[[0], [1], [2], [3]], gdtypes=[F32, BF16, BF16, BF16], name=n + "ln2")
    big = {}
    big["ple_w_proj"] = _cols_to_blocks(_mm(p_i, dpp, am="km", name=n + "ple_proj_dw"), name=n + "ple_proj_dw_blocks")
    big["ple_w_gate"] = _mm(sv["h1b"], dpg, am="km", name=n + "ple_gate_dw")
    dh1_b = _mm(dpg, wts["ple_w_gate"], bm="nk", name=n + "ple_gate_dx")
    big["w_down"] = _mm(sv["act"], dffn, am="bkm", name=n + "down_dw")
    dgu = _down_dx_swiglu(dffn, wts["w_down"], sv["gu"], name=n + "down_dx")
    dgu = dgu.reshape((N_DEV,) + dgu.shape[2:])
    big["w_gate_up"], carried_got = _mm(sv["h1b"], dgu, am="km", bm="bkn", om="bmn", name=n + "gate_up_dw",
                                        job=carried), None
    if carried is not None:
        big["w_gate_up"], carried_got = big["w_gate_up"]
    dh1_c = _mm(dgu, wts["w_gate_up"], am="bmk", bm="bnk", name=n + "gate_up_dx")
    dh_a, dmix, gr["ln1_g"], gr["ln1_b"] = _rowwise_vjp(
        _make_post_mix(alpha), [sv["h"], sv["mix"]], [sv["g1"], sv["b1"]], [[dh1_a, dh1_b, dh1_c]],
        groups=[[0], [1]], gdtypes=[F32, BF16], name=n + "ln1")
    big["w_out"] = _mm(sv["cat"], dmix, am="km", name=n + "out_proj_dw")
    early = [_as_pairs(big[k]) for k in RS_EARLY]
    dcat, theirs = _mm(dmix, wts["w_out"], bm="nk", name=n + "out_proj_dx", job=_pair_job(early))
    sums = [_pair_add(x, r, core, name=n + "pair_add_" + k) for k, x, r in zip(RS_EARLY, early, theirs)]

    (dqr, dkv, dkf), early_quads = _attn_bwd(sv["qr"], sv["kf"], sv["kvb"], dcat, sv["o_c"], sv["lse"],
                                             name=n + "attn", job=_quad_job(sums))
    dqpad, dkr = _mla_prep_bwd(dqr, dkf, tables, name=n + "mla_prep")
    big["mla_w_uq"] = _cols_to_blocks(_mm(sv["cqn"], dqpad, am="km", name=n + "uq_dw"), name=n + "uq_dw_blocks")
    dcqn = _mm(dqpad, wts["mla_w_uq"], bm="nk", name=n + "uq_dx")
    big["mla_w_ukv"] = _cols_to_blocks(_mm(sv["ckvn"], dkv, am="km", name=n + "ukv_dw"), name=n + "ukv_dw_blocks")
    dckvn = _mm(dkv, wts["mla_w_ukv"], bm="nk", name=n + "ukv_dx")
    projp = sv["projp"]
    dcq, gr["mla_q_norm_g"] = _rowwise_vjp(_fn_rms, [(projp, 384, P_CQ // 384)], [sv["qg"]], [[dcqn]],
                                           groups=[[0]], gdtypes=[BF16], name=n + "q_norm")
    dckv, gr["mla_kv_norm_g"] = _rowwise_vjp(_fn_rms, [(projp, 256, P_CKV // 256)], [sv["kvg"]], [[dckvn]],
                                             groups=[[0]], gdtypes=[BF16], name=n + "kv_norm")
    dsgu, gr["sgu_ln_g"], gr["sgu_ln_b"], gr["sgu_w_s"], gr["sgu_b_s"] = _sgu_bwd(
        projp, sv["lg"], sv["lbias"], sv["w_s"], sv["bias_full"], dcat, name=n + "sgu")
    dhg, gr["hgrn_norm_g"], gr["lower_bound"] = _hgrn_bwd(
        projp, lbs[li], sv["ng"], sv["o_pre"], sv["states"], dcat, name=n + "hgrn")
    dprojp = jnp.concatenate([dhg, dsgu, dcq, dkr, dckv], axis=1)
    big["w_in"] = _unplace_w_in(_mm(sv["hb"], dprojp, am="km", name=n + "proj_dw"), name=n + "proj_dw_shards")
    late = [_as_pairs(big[k]) for k in RS_LATE]
    dh_b, theirs = _mm(dprojp, wts["w_in"], bm="nk", name=n + "proj_dx", job=_pair_job(late))
    late_sums = [_pair_add(x, r, core, name=n + "pair_add_" + k) for k, x, r in zip(RS_LATE, late, theirs)]
    return [dh_a, dh_b], gr, early_quads, late_sums, carried_got


def kernel(x, p, positions, ln_in_g, ln_in_b, w_in, hgrn_lb_logits, hgrn_norm_g, sgu_ln_g, sgu_ln_b, sgu_w_s, sgu_b_s, mla_q_norm_g, mla_w_uq, mla_kv_norm_g, mla_w_ukv, w_out, ln1_g, ln1_b, w_gate_up, w_down, ple_w_gate, ple_w_proj, ln2_g, ln2_b, loss_target, m_ln_in_g, m_ln_in_b, m_w_in, m_hgrn_lb_logits, m_hgrn_norm_g, m_sgu_ln_g, m_sgu_ln_b, m_sgu_w_s, m_sgu_b_s, m_mla_q_norm_g, m_mla_w_uq, m_mla_kv_norm_g, m_mla_w_ukv, m_w_out, m_ln1_g, m_ln1_b, m_w_gate_up, m_w_down, m_ple_w_gate, m_ple_w_proj, m_ln2_g, m_ln2_b, v_ln_in_g, v_ln_in_b, v_w_in, v_hgrn_lb_logits, v_hgrn_norm_g, v_sgu_ln_g, v_sgu_ln_b, v_sgu_w_s, v_sgu_b_s, v_mla_q_norm_g, v_mla_w_uq, v_mla_kv_norm_g, v_mla_w_ukv, v_w_out, v_ln1_g, v_ln1_b, v_w_gate_up, v_w_down, v_ple_w_gate, v_ple_w_proj, v_ln2_g, v_ln2_b):
    args = dict(locals())
    w = {k: args[k] for k in ORDER}
    m = {k: args["m_" + k] for k in ORDER}
    v = {k: args["v_" + k] for k in ORDER}
    depth = w_in.shape[0]
    assert depth == 2, "the lower-bound kernel is written for two layers"
    alpha = (2 * depth) ** 0.25
    xs, tgt = x[0], loss_target[0]
    d_model = xs.shape[1]

    shards = [_weight_shards(w, li) for li in range(depth)]
    (g_in,) = _all_gather([shards[0]["w_in"]], name="gather_l0_w_in")
    w_in0 = _usable_weights({"w_in": g_in}, name="l0")
    on_hgrn0 = ("mla_w_uq", "mla_w_ukv", "w_out", "ple_w_gate", "ple_w_proj")
    ffn0 = ("w_gate_up", "w_down")
    first1 = ("w_in", "mla_w_uq", "mla_w_ukv", "w_out")
    on_attn1 = ("w_gate_up", "w_down", "ple_w_gate", "ple_w_proj")
    layer1_first = {}

    def after_hgrn0(got):
        got = _gather_forward(got, name="gather_l0a_forward")
        return _usable_weights(dict(zip(on_hgrn0, got)), name="l0")

    def after_attn0(got):
        got = _gather_forward(got, name="gather_l0b_forward")
        layer1_first.update(_usable_weights(dict(zip(first1, got[len(ffn0):])), name="l1"))
        return _usable_weights(dict(zip(ffn0, got[:len(ffn0)])), name="l0")

    def after_attn1(got):
        got = _gather_forward(got, name="gather_l1_forward")
        return _usable_weights(dict(zip(on_attn1, got)), name="l1")

    tables = _rope_tables(positions[0])
    row1 = lambda a: a.reshape(1, -1)
    l0, l1 = row1(hgrn_lb_logits[0]), row1(hgrn_lb_logits[1])
    lbs = _rowwise(_fn_lower_bounds, [l0, l1], [], [(HG_W, F32), (HG_W, F32)], name="lower_bounds")

    gin, bin_ = row1(ln_in_g), row1(ln_in_b)
    h, hb = _rowwise(_twice(_fn_ln), [xs], [gin, bin_], [(d_model, F32), (d_model, BF16)], name="ln_in")
    (h, hb), sv0 = _layer_forward(
        0, h, hb, p[0, 0], w_in0, w, lbs, tables, alpha,
        hgrn_job=_gather_job([shards[0][k] for k in on_hgrn0]), after_hgrn=after_hgrn0,
        attn_job=_gather_job([shards[0][k] for k in ffn0] + [shards[1][k] for k in first1]), after_attn=after_attn0)
    (h, _), sv1 = _layer_forward(
        1, h, hb, p[1, 0], layer1_first, w, lbs, tables, alpha,
        attn_job=_gather_job([shards[1][k] for k in on_attn1]), after_attn=after_attn1)
    saved = [sv0, sv1]
    dy, loss_local = _loss_and_grad(h, tgt, name="loss")
    loss = lax.psum(loss_local[0, 0], ("x", "y", "c"))

    core = lax.axis_index("c").astype(jnp.int32).reshape(1)
    dparts, grads, quads, carried = [dy], [None] * depth, [None] * depth, None
    for li in reversed(range(depth)):
        dparts, grads[li], early_quads, late_sums, late_quads = _layer_backward(
            li, dparts, p[li, 0], saved[li], lbs, tables, alpha, core, carried=carried)
        quads[li] = dict(zip(RS_EARLY, early_quads))
        if carried is not None:
            quads[li + 1].update(zip(RS_LATE, late_quads))
        carried = _quad_job(late_sums)
    (dx, d_gin, d_bin), late_quads = _rowwise_vjp(_fn_ln, [xs], [gin, bin_], [dparts], groups=[[0]], name="ln_in_b",
                                                   job=carried)
    quads[0].update(zip(RS_LATE, late_quads))
    dl0, dl1 = _rowwise_vjp(_fn_lower_bounds, [l0, l1], [], [[grads[0]["lower_bound"]], [grads[1]["lower_bound"]]],
                            groups=[[0], [1]], name="lower_bounds_b")

    prefixes = ("grad_", "delta_", "new_m_", "new_v_")
    per_layer = {pre + k: [] for pre in prefixes for k in BIG}
    uq_pad = ((0, 0), (0, 0), (0, LANES - ATT_D))
    state = {k: ((jnp.pad(w[k], uq_pad), jnp.pad(m[k], uq_pad), jnp.pad(v[k], uq_pad)) if k == "mla_w_uq"
                 else (w[k], m[k], v[k])) for k in BIG}
    for li in range(depth):
        for k in BIG:
            res4 = _adamw(quads[li][k], *state[k], li, name=f"adamw_l{li}_{k}")
            for pre, a in zip(prefixes, res4):
                per_layer[pre + k].append(a[:, :ATT_D] if k == "mla_w_uq" else a)
    out = {name: jnp.stack(vals) for name, vals in per_layer.items()}

    small_g = {"ln_in_g": d_gin.reshape(-1), "ln_in_b": d_bin.reshape(-1),
               "hgrn_lb_logits": jnp.stack([dl0.reshape(-1), dl1.reshape(-1)])}
    for k in SMALL[3:]:
        small_g[k] = jnp.stack([grads[li][k].reshape(w[k].shape[1:]) for li in range(depth)])
    (small_parts,) = _all_gather([_pack([small_g[k] for k in SMALL])], name="gather_small_grads")
    slabs = _adamw(small_parts, _pack([w[k] for k in SMALL]), _pack([m[k] for k in SMALL]),
                   _pack([v[k] for k in SMALL]), None, name="adamw_small")
    shapes = [w[k].shape for k in SMALL]
    for pre, slab in zip(prefixes, slabs):
        for k, a in zip(SMALL, _unpack(slab, shapes)):
            out[pre + k] = a
    res = [loss, dx[None]]
    for prefix in ("grad_", "delta_", "new_m_", "new_v_"):
        res += [out[prefix + k] for k in ORDER]
    return tuple(res)
```

```python
import functools
import math

import jax
import jax.numpy as jnp
from jax import lax
from jax.experimental import pallas as pl
from jax.experimental.pallas import tpu as pltpu

F32 = jnp.float32
BF16 = jnp.bfloat16
MESH = pl.DeviceIdType.MESH

LN_EPS = 1e-5
RMS_EPS = 1e-6
ROPE_THETA = 10000.0
ADAM_LR, ADAM_B1, ADAM_B2, ADAM_EPS, ADAM_WD, ADAM_STEP = 0.001, 0.9, 0.999, 1e-08, 0.01, 10

N_DEV = 8
LANES = 128
HG_CHUNK = 16
HG_W = 256
HEAD = 64
SGU_CHUNK = 128
N_ATT_HEADS = 8
ATT_D = 96
VMEM_LIMIT = 56 * 1024 * 1024

HG_TILE = 256
ATT_TQ = 512
ROW_TILE = 256

P_CQ, P_KR, P_CKV, P_COLS = 1536, 1920, 2048, 2304


def _cparams(sem):
    return pltpu.CompilerParams(dimension_semantics=sem, vmem_limit_bytes=VMEM_LIMIT)


_ANY = pl.BlockSpec(memory_space=pl.ANY)


def _call(body, operands, *, name, grid, in_specs, out_specs, out_shape, sem, scratch_shapes=(), job=None):
    if job is None:
        return pl.pallas_call(body, name=name, grid=grid, in_specs=in_specs, out_specs=out_specs, out_shape=out_shape,
                              scratch_shapes=list(scratch_shapes), compiler_params=_cparams(sem))(*operands)
    single = not isinstance(out_shape, (list, tuple))
    shapes = [out_shape] if single else list(out_shape)
    ospecs = [out_specs] if single else list(out_specs)
    ni, no, ns = len(operands), len(shapes), len(scratch_shapes)
    ji, jo = len(job.inputs), len(job.out_shapes)

    def hosted(*refs):
        p = 0
        parts = []
        for cnt in (ni, ji, no, jo, ns):
            parts.append(refs[p:p + cnt])
            p += cnt
        ins, jins, outs, jouts, scr = parts
        jsems = refs[p:]
        ids = [pl.program_id(a) for a in range(len(grid))]
        first = functools.reduce(lambda a, b: a & b, [i == 0 for i in ids])
        last = functools.reduce(lambda a, b: a & b, [i == g - 1 for i, g in zip(ids, grid)])

        @pl.when(first)
        def _():
            job.start(jins, jouts, jsems)

        body(*ins, *outs, *scr)

        @pl.when(last)
        def _():
            job.finish(jins, jouts, jsems)

    res = pl.pallas_call(
        hosted, name=name, grid=grid,
        in_specs=list(in_specs) + [_ANY] * ji, out_specs=ospecs + [_ANY] * jo,
        out_shape=shapes + list(job.out_shapes),
        scratch_shapes=list(scratch_shapes) + [pltpu.SemaphoreType.DMA((c,)) for c in job.sem_counts],
        compiler_params=_cparams(("arbitrary",) * len(grid)),
    )(*operands, *job.inputs)
    own = res[0] if single else res[:no]
    return own, res[no:]


class _Job:
    def __init__(self, inputs, out_shapes, sem_counts, start, finish):
        self.inputs, self.out_shapes, self.sem_counts = list(inputs), list(out_shapes), list(sem_counts)
        self.start, self.finish = start, finish


def _copies_job(inputs, out_shapes, n_remote, n_local, make):
    def start(jins, jouts, sems):
        sends, _, local = make(jins, jouts, *sems)
        for cp in local + sends:
            cp.start()

    def finish(jins, jouts, sems):
        sends, recvs, local = make(jins, jouts, *sems)
        for cp in recvs:
            cp.wait_recv()
        for cp in sends:
            cp.wait_send()
        for cp in local:
            cp.wait()

    return _Job(inputs, out_shapes, [n_remote, n_remote, max(n_local, 1)], start, finish)


def _run_job(job, *, name):
    ji, jo = len(job.inputs), len(job.out_shapes)

    def body(*refs):
        jins, jouts, sems = refs[:ji], refs[ji:ji + jo], refs[ji + jo:]
        job.start(jins, jouts, sems)
        job.finish(jins, jouts, sems)

    return pl.pallas_call(
        body, name=name, out_shape=list(job.out_shapes), in_specs=[_ANY] * ji, out_specs=[_ANY] * jo,
        scratch_shapes=[pltpu.SemaphoreType.DMA((c,)) for c in job.sem_counts],
    )(*job.inputs)


def _tile(n, pref):
    if n % pref == 0:
        return pref
    best = None
    t = LANES
    while t <= min(n, pref):
        if n % t == 0:
            best = t
        t += LANES
    return best if best is not None else n


def _mm(a, b, *, am="mk", bm="kn", om="mn", out_dtype=F32, tm=1024, tn=1024, tk=1024, name, job=None):
    if am == "mk":
        m, k = a.shape
    elif am == "km":
        k, m = a.shape
    elif am == "bmk":
        m, tk = a.shape[1], a.shape[2]
        k = a.shape[0] * tk
    else:
        k, tm = a.shape[1], a.shape[2]
        m = a.shape[0] * tm
    if bm == "kn":
        kb_, n = b.shape
    elif bm == "nk":
        n, kb_ = b.shape
    elif bm == "bkn":
        kb_, tn = b.shape[1], b.shape[2]
        n = b.shape[0] * tn
    else:
        n, tk = b.shape[1], b.shape[2]
        kb_ = b.shape[0] * tk
    assert kb_ == k, (a.shape, b.shape, am, bm)
    tm, tn, tk = _tile(m, tm), _tile(n, tn), _tile(k, tk)
    nk = k // tk
    dims = (((0 if am in ("km", "bkm") else 1,), (1 if bm in ("nk", "bnk") else 0,)), ((), ()))

    a_spec = {"mk": pl.BlockSpec((tm, tk), lambda i, j, kk: (i, kk)),
              "km": pl.BlockSpec((tk, tm), lambda i, j, kk: (kk, i)),
              "bmk": pl.BlockSpec((None, tm, tk), lambda i, j, kk: (kk, i, 0)),
              "bkm": pl.BlockSpec((None, tk, tm), lambda i, j, kk: (i, kk, 0))}[am]
    b_spec = {"kn": pl.BlockSpec((tk, tn), lambda i, j, kk: (kk, j)),
              "nk": pl.BlockSpec((tn, tk), lambda i, j, kk: (j, kk)),
              "bkn": pl.BlockSpec((None, tk, tn), lambda i, j, kk: (j, kk, 0)),
              "bnk": pl.BlockSpec((None, tn, tk), lambda i, j, kk: (kk, j, 0))}[bm]
    if om == "mn":
        o_spec, o_shape = pl.BlockSpec((tm, tn), lambda i, j, kk: (i, j)), (m, n)
    else:
        o_spec, o_shape = pl.BlockSpec((None, tm, tn), lambda i, j, kk: (j, i, 0)), (n // tn, m, tn)

    def body(a_ref, b_ref, o_ref, *acc):
        kk = pl.program_id(2)
        prod = lax.dot_general(a_ref[...].astype(BF16), b_ref[...].astype(BF16), dims, preferred_element_type=F32)
        if nk == 1:
            o_ref[...] = prod.astype(o_ref.dtype)
            return
        acc_ref, = acc

        @pl.when(kk == 0)
        def _():
            acc_ref[...] = prod

        if nk > 2:
            @pl.when((kk > 0) & (kk < nk - 1))
            def _():
                acc_ref[...] += prod

        @pl.when(kk == nk - 1)
        def _():
            o_ref[...] = (acc_ref[...] + prod).astype(o_ref.dtype)

    return _call(body, (a, b), name=name, grid=(m // tm, n // tn, nk), in_specs=[a_spec, b_spec], out_specs=o_spec,
                 out_shape=jax.ShapeDtypeStruct(o_shape, out_dtype),
                 scratch_shapes=[pltpu.VMEM((tm, tn), F32)] if nk > 1 else [],
                 sem=("parallel", "parallel", "arbitrary"), job=job)


def _row_operand(a, tile):
    if isinstance(a, tuple):
        arr, w, j = a
        return arr, pl.BlockSpec((tile, w), lambda i, j=j: (i, j))
    return a, pl.BlockSpec((tile, a.shape[1]), lambda i: (i, 0))


def _const_spec(c):
    nd = c.ndim
    return pl.BlockSpec(c.shape, lambda i, nd=nd: (0,) * nd)


def _rowwise(fn, rows, consts, outs, *, name, accs=(), tile=None, job=None):
    t_rows = (rows[0][0] if isinstance(rows[0], tuple) else rows[0]).shape[0]
    tile = min(tile or ROW_TILE, t_rows)
    arrs, specs = zip(*[_row_operand(a, tile) for a in rows])
    nin, no = len(rows) + len(consts), len(outs)

    def body(*refs):
        res = fn(*[r[...] for r in refs[:nin]])
        for r, v in zip(refs[nin:nin + no], res[:no]):
            r[...] = v.astype(r.dtype)
        if accs:
            a_refs = refs[nin + no:]

            @pl.when(pl.program_id(0) == 0)
            def _():
                for r in a_refs:
                    r[...] = jnp.zeros_like(r)

            for r, v in zip(a_refs, res[no:]):
                r[...] += v

    out_shape = [jax.ShapeDtypeStruct((t_rows, w), dt) for w, dt in outs]
    out_shape += [jax.ShapeDtypeStruct(s, F32) for s in accs]
    out_specs = [pl.BlockSpec((tile, w), lambda i: (i, 0)) for w, _ in outs]
    out_specs += [pl.BlockSpec(s, lambda i, nd=len(s): (0,) * nd) for s in accs]
    return _call(body, (*arrs, *consts), name=name, grid=(t_rows // tile,),
                 in_specs=list(specs) + [_const_spec(c) for c in consts],
                 out_specs=out_specs, out_shape=out_shape, sem=("arbitrary",), job=job)


def _rowwise_vjp(fn, rows, consts, cts, *, name, groups, tile=None, gdtypes=None, job=None):
    t_rows = (rows[0][0] if isinstance(rows[0], tuple) else rows[0]).shape[0]
    tile = min(tile or ROW_TILE, t_rows)
    arrs, specs = zip(*[_row_operand(a, tile) for a in rows])
    flat_cts = [c for group in cts for c in group]
    ct_arrs, ct_specs = zip(*[_row_operand(a, tile) for a in flat_cts])
    nr, nc, nct, ng = len(rows), len(consts), len(flat_cts), len(groups)

    def width(a):
        return a[1] if isinstance(a, tuple) else a.shape[1]

    def body(*refs):
        rv = [r[...].astype(F32) for r in refs[:nr]]
        cv = [r[...] for r in refs[nr:nr + nc]]
        ct_refs = refs[nr + nc:nr + nc + nct]
        ctv, pos = [], 0
        for group in cts:
            s = ct_refs[pos][...].astype(F32)
            for r in ct_refs[pos + 1:pos + len(group)]:
                s = s + r[...].astype(F32)
            ctv.append(s)
            pos += len(group)
        _, pull = jax.vjp(fn, *rv, *cv)
        grads = pull(tuple(ctv))
        g_refs = refs[nr + nc + nct:nr + nc + nct + ng]
        for r, idx in zip(g_refs, groups):
            parts = [grads[i] for i in idx]
            r[...] = (parts[0] if len(parts) == 1 else jnp.concatenate(parts, axis=1)).astype(r.dtype)
        c_refs = refs[nr + nc + nct + ng:]

        @pl.when(pl.program_id(0) == 0)
        def _():
            for r in c_refs:
                r[...] = jnp.zeros_like(r)

        for r, v in zip(c_refs, grads[nr:]):
            r[...] += v

    gw = [sum(width(rows[i]) for i in idx) for idx in groups]
    gdtypes = gdtypes or [F32] * ng
    out_shape = [jax.ShapeDtypeStruct((t_rows, w), dt) for w, dt in zip(gw, gdtypes)]
    out_shape += [jax.ShapeDtypeStruct(c.shape, F32) for c in consts]
    out_specs = [pl.BlockSpec((tile, w), lambda i: (i, 0)) for w in gw]
    out_specs += [_const_spec(c) for c in consts]
    return _call(body, (*arrs, *consts, *ct_arrs), name=name, grid=(t_rows // tile,),
                 in_specs=list(specs) + [_const_spec(c) for c in consts] + list(ct_specs),
                 out_specs=out_specs, out_shape=out_shape, sem=("arbitrary",), job=job)


def _layer_norm(x, g, b):
    mu = jnp.mean(x, axis=-1, keepdims=True)
    xc = x - mu
    var = jnp.mean(xc * xc, axis=-1, keepdims=True)
    return xc * lax.rsqrt(var + LN_EPS) * g + b


def _sigmoid(x):
    return 1.0 / (1.0 + jnp.exp(-x))


def _fn_ln(x, g, b):
    return (_layer_norm(x, g, b),)


def _fn_rms(x, g):
    return (x * lax.rsqrt(jnp.mean(x * x, axis=-1, keepdims=True) + RMS_EPS) * g,)


def _make_post_mix(alpha):
    def fn(h, mix, g, b):
        return (_layer_norm(alpha * h + mix, g, b),)
    return fn


def _make_ple_ln(alpha):
    def fn(h1, ffn, pg, pp, g, b):
        return (_layer_norm(alpha * h1 + ffn + _sigmoid(pg) * pp, g, b),)
    return fn


def _fn_lower_bounds(l0, l1):
    m = jnp.maximum(l0, l1)
    e0, e1 = jnp.exp(l0 - m), jnp.exp(l1 - m)
    s = e0 + e1
    p0, p1 = e0 / s, e1 / s
    return (p0 - p0, (p0 + p1) - p0)


def _loss_and_grad(y, target, *, name):
    d = y.shape[1]

    def fn(yv, tv):
        err = yv - tv
        return err * (1.0 / d), 0.5 * jnp.sum(jnp.mean(err * err, axis=-1, keepdims=True), axis=0, keepdims=True)

    return _rowwise(fn, [y, target], [], [(d, F32)], accs=[(1, 1)], name=name)


def _split_dot(x, e_bf16):
    hi = x.astype(BF16)
    lo = (x - hi.astype(F32)).astype(BF16)
    return (jnp.dot(hi, e_bf16, preferred_element_type=F32) + jnp.dot(lo, e_bf16, preferred_element_type=F32))


def _hgrn_common(th):
    rm = lax.broadcasted_iota(jnp.int32, (th, HG_W), 0) % HG_CHUNK

    def seg_cumsum(x):
        for s in (1, 2, 4, 8):
            x = x + jnp.where(rm >= s, pltpu.roll(x, s, 0), 0.0)
        return x

    def seg_rcumsum(x):
        for s in (1, 2, 4, 8):
            x = x + jnp.where(rm < HG_CHUNK - s, pltpu.roll(x, th - s, 0), 0.0)
        return x

    ri = lax.broadcasted_iota(jnp.int32, (HG_W, HG_W), 0) // HEAD
    ci = lax.broadcasted_iota(jnp.int32, (HG_W, HG_W), 1) // HEAD
    head_f32 = (ri == ci).astype(F32)
    head_bf16 = head_f32.astype(BF16)

    def headsum(x):
        return _split_dot(x, head_bf16)

    return rm, seg_cumsum, seg_rcumsum, head_f32, headsum


def _hgrn_gates(qr, fl, lb):
    sg = _sigmoid(fl)
    f = lb + (1.0 - lb) * sg
    sq = _sigmoid(qr)
    return sg, f, jnp.log(f), 1.0 - f, qr * sq, sq


def _shifted(x, d, th):
    return x if d == 0 else pltpu.roll(x, d, 0)


def _unshift(x, d, th):
    return x if d == 0 else pltpu.roll(x, th - d, 0)


def _hgrn_fwd(projp, lb, ng, *, name, job=None):
    t_rows = projp.shape[0]
    th = min(HG_TILE, t_rows)
    nct = th // HG_CHUNK

    def body(q_ref, f_ref, i_ref, g_ref, lb_ref, ng_ref, oa_ref, opre_ref, st_out_ref,
             st_ref, vtm_ref, kv_ref, qe_ref, dec_ref, oint_ref):
        rm, seg_cumsum, seg_rcumsum, head_f32, headsum = _hgrn_common(th)

        @pl.when(pl.program_id(0) == 0)
        def _():
            st_ref[...] = jnp.zeros_like(st_ref)

        qr, fl, v, g = q_ref[...], f_ref[...], i_ref[...], g_ref[...]
        _, f, lf, k, q, _ = _hgrn_gates(qr, fl, lb_ref[...])
        b = seg_cumsum(lf)

        o = jnp.zeros((th, HG_W), F32)
        for d in range(HG_CHUNK):
            kd, bd, vd = _shifted(k, d, th), _shifted(b, d, th), _shifted(v, d, th)
            e = jnp.exp(jnp.where(rm >= d, b - bd, -1e30))
            o = o + headsum(q * kd * e) * vd

        blast = seg_rcumsum(jnp.where(rm == HG_CHUNK - 1, b, 0.0))
        kte = (k * jnp.exp(blast - b)).astype(BF16)
        qe_ref[...] = q * jnp.exp(b)
        dec_ref[...] = jnp.exp(blast)
        vt = v.T
        lane_chunk = lax.broadcasted_iota(jnp.int32, (HG_W, th), 1) // HG_CHUNK
        for c in range(nct):
            vtm_ref[c * HG_W:(c + 1) * HG_W, :] = jnp.where(lane_chunk == c, vt, 0.0).astype(BF16)
        kv_ref[...] = jnp.dot(vtm_ref[...], kte, preferred_element_type=F32)

        s = st_ref[...]
        for c in range(nct):
            rows = slice(c * HG_CHUNK, (c + 1) * HG_CHUNK)
            st_out_ref[c] = s
            oint_ref[rows, :] = lax.dot_general(qe_ref[rows, :].astype(BF16), s.astype(BF16),
                                                (((1,), (1,)), ((), ())), preferred_element_type=F32)
            dec = jnp.max(dec_ref[rows, :], axis=0, keepdims=True)
            s = s * dec + kv_ref[c * HG_W:(c + 1) * HG_W, :] * head_f32
        st_ref[...] = s

        o = o + oint_ref[...]
        opre_ref[...] = o
        r = lax.rsqrt(headsum(o * o) * (1.0 / HEAD) + RMS_EPS)
        oa_ref[...] = (o * r * ng_ref[...] * (g * _sigmoid(g))).astype(oa_ref.dtype)

    col = lambda j: pl.BlockSpec((th, HG_W), lambda i, j=j: (i, j))
    vec = pl.BlockSpec((1, HG_W), lambda i: (0, 0))
    row = pl.BlockSpec((th, HG_W), lambda i: (i, 0))
    n_chunks = t_rows // HG_CHUNK
    return _call(
        body, (projp, projp, projp, projp, lb, ng), name=name, grid=(t_rows // th,),
        in_specs=[col(0), col(1), col(2), col(3), vec, vec],
        out_specs=[row, row, pl.BlockSpec((nct, HG_W, HG_W), lambda i: (i, 0, 0))],
        out_shape=[jax.ShapeDtypeStruct((t_rows, HG_W), BF16), jax.ShapeDtypeStruct((t_rows, HG_W), F32),
                   jax.ShapeDtypeStruct((n_chunks, HG_W, HG_W), F32)],
        scratch_shapes=[pltpu.VMEM((HG_W, HG_W), F32), pltpu.VMEM((nct * HG_W, th), BF16),
                        pltpu.VMEM((nct * HG_W, HG_W), F32), pltpu.VMEM((th, HG_W), F32),
                        pltpu.VMEM((th, HG_W), F32), pltpu.VMEM((th, HG_W), F32)],
        sem=("arbitrary",), job=job)


def _hgrn_bwd(projp, lb, ng, opre, states, dcat, *, name):
    t_rows = projp.shape[0]
    th = min(HG_TILE, t_rows)
    nct = th // HG_CHUNK
    nt = t_rows // th

    def body(q_ref, f_ref, i_ref, g_ref, lb_ref, ng_ref, opre_ref, st_in_ref, do_ref,
             dproj_ref, dng_ref, dlb_ref,
             gst_ref, dotm_ref, qg_ref, v_ref, kte_ref, dop_ref, dec_ref, dkte_ref, dvi_ref, dqe_ref, ddec_ref):
        rm, seg_cumsum, seg_rcumsum, head_f32, headsum = _hgrn_common(th)

        @pl.when(pl.program_id(0) == 0)
        def _():
            gst_ref[...] = jnp.zeros_like(gst_ref)
            dng_ref[...] = jnp.zeros_like(dng_ref)
            dlb_ref[...] = jnp.zeros_like(dlb_ref)

        qr, fl, v, g = q_ref[...], f_ref[...], i_ref[...], g_ref[...]
        lb, ngv = lb_ref[...], ng_ref[...]
        sg, f, lf, k, q, sq = _hgrn_gates(qr, fl, lb)
        b = seg_cumsum(lf)
        blast = seg_rcumsum(jnp.where(rm == HG_CHUNK - 1, b, 0.0))
        eb = jnp.exp(b)
        ekb = jnp.exp(blast - b)
        qe, kte, dec = q * eb, k * ekb, jnp.exp(blast)

        do_out, op = do_ref[...], opre_ref[...]
        sgg = _sigmoid(g)
        sil = g * sgg
        r = lax.rsqrt(headsum(op * op) * (1.0 / HEAD) + RMS_EPS)
        on = op * r
        dng_ref[...] += jnp.sum(do_out * on * sil, axis=0, keepdims=True)
        dg = do_out * on * ngv * (sgg * (1.0 + g * (1.0 - sgg)))
        don = do_out * ngv * sil
        dop = r * (don - on * (headsum(don * on) * (1.0 / HEAD)))

        v_ref[...] = v
        kte_ref[...] = kte
        dop_ref[...] = dop
        dec_ref[...] = dec
        dot_t = dop.T
        lane_chunk = lax.broadcasted_iota(jnp.int32, (HG_W, th), 1) // HG_CHUNK
        for c in range(nct):
            dotm_ref[c * HG_W:(c + 1) * HG_W, :] = jnp.where(lane_chunk == c, dot_t, 0.0).astype(BF16)
        qg_ref[...] = jnp.dot(dotm_ref[...], qe.astype(BF16), preferred_element_type=F32)

        gs = gst_ref[...]
        for c in reversed(range(nct)):
            rows = slice(c * HG_CHUNK, (c + 1) * HG_CHUNK)
            s = st_in_ref[c]
            gm = (gs * head_f32).astype(BF16)
            dkte_ref[rows, :] = jnp.dot(v_ref[rows, :].astype(BF16), gm, preferred_element_type=F32)
            dvi_ref[rows, :] = lax.dot_general(kte_ref[rows, :].astype(BF16), gm, (((1,), (1,)), ((), ())),
                                               preferred_element_type=F32)
            dqe_ref[rows, :] = jnp.dot(dop_ref[rows, :].astype(BF16), s.astype(BF16), preferred_element_type=F32)
            ddec_ref[rows, :] = jnp.broadcast_to(jnp.sum(gs * s, axis=0, keepdims=True), (HG_CHUNK, HG_W))
            dec_c = jnp.max(dec_ref[rows, :], axis=0, keepdims=True)
            gs = gs * dec_c + qg_ref[c * HG_W:(c + 1) * HG_W, :] * head_f32
        gst_ref[...] = gs

        dkte, dqe = dkte_ref[...], dqe_ref[...]
        dq = dqe * eb
        dk = dkte * ekb
        db = dqe * qe - dkte * kte
        dv = dvi_ref[...]
        dblast = dkte * kte + jnp.where(rm == HG_CHUNK - 1, ddec_ref[...] * dec, 0.0)

        for d in range(HG_CHUNK):
            kd, bd, vd = _shifted(k, d, th), _shifted(b, d, th), _shifted(v, d, th)
            e = jnp.exp(jnp.where(rm >= d, b - bd, -1e30))
            p = q * kd * e
            sc = headsum(p)
            dsc = headsum(dop * vd)
            dv = dv + _unshift(sc * dop, d, th)
            dq = dq + dsc * kd * e
            dk = dk + _unshift(dsc * q * e, d, th)
            darg = dsc * p
            db = db + darg - _unshift(darg, d, th)

        db = db + jnp.where(rm == HG_CHUNK - 1, seg_cumsum(dblast), 0.0)
        dlf = seg_rcumsum(db)
        df = dlf / f - dk
        dlb_ref[...] += jnp.sum(df * (1.0 - sg), axis=0, keepdims=True)
        dfl = df * (1.0 - lb) * sg * (1.0 - sg)
        dqr = dq * (sq * (1.0 + qr * (1.0 - sq)))
        dproj_ref[...] = jnp.concatenate([dqr, dfl, dv, dg], axis=1).astype(dproj_ref.dtype)

    rev = lambda i: nt - 1 - i
    col = lambda j: pl.BlockSpec((th, HG_W), lambda i, j=j: (rev(i), j))
    vec = pl.BlockSpec((1, HG_W), lambda i: (0, 0))
    row = pl.BlockSpec((th, HG_W), lambda i: (rev(i), 0))
    tile_f32 = pltpu.VMEM((th, HG_W), F32)
    return pl.pallas_call(
        body, name=name, grid=(nt,),
        in_specs=[col(0), col(1), col(2), col(3), vec, vec, row,
                  pl.BlockSpec((nct, HG_W, HG_W), lambda i: (rev(i), 0, 0)), col(0)],
        out_specs=[pl.BlockSpec((th, 4 * HG_W), lambda i: (rev(i), 0)), vec, vec],
        out_shape=[jax.ShapeDtypeStruct((t_rows, 4 * HG_W), BF16), jax.ShapeDtypeStruct((1, HG_W), F32),
                   jax.ShapeDtypeStruct((1, HG_W), F32)],
        scratch_shapes=[pltpu.VMEM((HG_W, HG_W), F32), pltpu.VMEM((nct * HG_W, th), BF16),
                        pltpu.VMEM((nct * HG_W, HG_W), F32)] + [tile_f32] * 8,
        compiler_params=_cparams(("arbitrary",)),
    )(projp, projp, projp, projp, lb, ng, opre, states, dcat)


_INV_SQRT2 = 1.0 / math.sqrt(2.0)
_INV_SQRT2PI = 1.0 / math.sqrt(2.0 * math.pi)


def _gelu(x):
    return 0.5 * x * (1.0 + lax.erf(x * _INV_SQRT2))


def _gelu_grad(x):
    return 0.5 * (1.0 + lax.erf(x * _INV_SQRT2)) + x * jnp.exp(-0.5 * x * x) * _INV_SQRT2PI


def _sgu_parts(bu, bv, lg, lbias, w_ref, n_groups):
    c = SGU_CHUNK
    tril = (lax.broadcasted_iota(jnp.int32, (c, c), 0) >= lax.broadcasted_iota(jnp.int32, (c, c), 1)).astype(F32)
    gid = lax.broadcasted_iota(jnp.int32, bu.shape, 1) // HEAD
    u = _gelu(bu)
    gv = _gelu(bv)
    mu = jnp.mean(gv, axis=-1, keepdims=True)
    xc = gv - mu
    rstd = lax.rsqrt(jnp.mean(xc * xc, axis=-1, keepdims=True) + LN_EPS)
    xhat = xc * rstd
    vn = xhat * lg + lbias
    ws = [w_ref[gi] * tril for gi in range(n_groups)]
    return tril, gid, u, rstd, xhat, vn, ws


def _sgu_fwd(projp, lg, lbias, w_s, bias_full, *, name):
    t_rows = projp.shape[0]
    n_groups = w_s.shape[0]
    c = SGU_CHUNK

    def body(u_ref, v_ref, lg_ref, lb_ref, w_ref, bias_ref, o_ref):
        _, gid, u, _, _, vn, ws = _sgu_parts(u_ref[...], v_ref[...], lg_ref[...], lb_ref[...], w_ref, n_groups)
        vnb = vn.astype(BF16)
        z = bias_ref[...]
        for gi in range(n_groups):
            z = z + jnp.where(gid == gi, jnp.dot(ws[gi].astype(BF16), vnb, preferred_element_type=F32), 0.0)
        o_ref[...] = (u * z).astype(o_ref.dtype)

    col = lambda j: pl.BlockSpec((c, HG_W), lambda i, j=j: (i, j))
    return pl.pallas_call(
        body, name=name, grid=(t_rows // c,),
        in_specs=[col(4), col(5), _const_spec(lg), _const_spec(lbias), _const_spec(w_s), _const_spec(bias_full)],
        out_specs=pl.BlockSpec((c, HG_W), lambda i: (i, 0)),
        out_shape=jax.ShapeDtypeStruct((t_rows, HG_W), BF16),
        compiler_params=_cparams(("arbitrary",)),
    )(projp, projp, lg, lbias, w_s, bias_full)


def _sgu_bwd(projp, lg, lbias, w_s, bias_full, dcat, *, name):
    t_rows = projp.shape[0]
    n_groups = w_s.shape[0]
    c = SGU_CHUNK
    n = t_rows // c

    def body(u_ref, v_ref, lg_ref, lb_ref, w_ref, bias_ref, do_ref,
             dproj_ref, dlg_ref, dlb_ref, dw_ref, dbs_ref, dbias_acc):
        i = pl.program_id(0)

        @pl.when(i == 0)
        def _():
            dlg_ref[...] = jnp.zeros_like(dlg_ref)
            dlb_ref[...] = jnp.zeros_like(dlb_ref)
            dw_ref[...] = jnp.zeros_like(dw_ref)
            dbias_acc[...] = jnp.zeros_like(dbias_acc)

        bu, bv, lg_v = u_ref[...], v_ref[...], lg_ref[...]
        tril, gid, u, rstd, xhat, vn, ws = _sgu_parts(bu, bv, lg_v, lb_ref[...], w_ref, n_groups)
        vnb = vn.astype(BF16)
        z = bias_ref[...]
        for gi in range(n_groups):
            z = z + jnp.where(gid == gi, jnp.dot(ws[gi].astype(BF16), vnb, preferred_element_type=F32), 0.0)
        do = do_ref[...]
        dbu = do * z * _gelu_grad(bu)
        dz = do * u
        dbias_acc[...] += dz
        dvn = jnp.zeros_like(dz)
        for gi in range(n_groups):
            dzg = jnp.where(gid == gi, dz, 0.0).astype(BF16)
            dw_ref[gi] += lax.dot_general(dzg, vnb, (((1,), (1,)), ((), ())), preferred_element_type=F32) * tril
            dvn = dvn + jnp.dot(ws[gi].T.astype(BF16), dzg, preferred_element_type=F32)
        dlg_ref[...] += jnp.sum(dvn * xhat, axis=0, keepdims=True)
        dlb_ref[...] += jnp.sum(dvn, axis=0, keepdims=True)
        dxh = dvn * lg_v
        dgv = rstd * (dxh - jnp.mean(dxh, axis=-1, keepdims=True)
                      - xhat * jnp.mean(dxh * xhat, axis=-1, keepdims=True))
        dproj_ref[...] = jnp.concatenate([dbu, dgv * _gelu_grad(bv)], axis=1).astype(dproj_ref.dtype)

        @pl.when(i == n - 1)
        def _():
            dbs_ref[...] = jnp.sum(dbias_acc[...].T.reshape(n_groups, HEAD, c), axis=1)

    col = lambda j: pl.BlockSpec((c, HG_W), lambda i, j=j: (i, j))
    return pl.pallas_call(
        body, name=name, grid=(n,),
        in_specs=[col(4), col(5), _const_spec(lg), _const_spec(lbias), _const_spec(w_s), _const_spec(bias_full),
                  col(1)],
        out_specs=[pl.BlockSpec((c, 2 * HG_W), lambda i: (i, 0)), _const_spec(lg), _const_spec(lbias),
                   _const_spec(w_s), pl.BlockSpec((n_groups, c), lambda i: (0, 0))],
        out_shape=[jax.ShapeDtypeStruct((t_rows, 2 * HG_W), BF16), jax.ShapeDtypeStruct(lg.shape, F32),
                   jax.ShapeDtypeStruct(lbias.shape, F32), jax.ShapeDtypeStruct(w_s.shape, F32),
                   jax.ShapeDtypeStruct((n_groups, c), F32)],
        scratch_shapes=[pltpu.VMEM((c, HG_W), F32)],
        compiler_params=_cparams(("arbitrary",)),
    )(projp, projp, lg, lbias, w_s, bias_full, dcat)


def _rope_tables(positions):
    t = positions.shape[0]
    inv_freq = ROPE_THETA ** (-jnp.arange(0, 32, 2, dtype=F32) / 32)
    ang = positions.astype(F32)[:, None] * inv_freq
    cos, sin = jnp.cos(ang), jnp.sin(ang)
    z = lambda w: jnp.zeros((t, w), F32)
    cos_t = jnp.concatenate([jnp.ones((t, 64), F32), cos, cos, z(32)], axis=1)
    sin_up = jnp.concatenate([z(80), sin, z(32)], axis=1)
    sin_dn = jnp.concatenate([z(64), -sin, z(48)], axis=1)
    return cos_t, sin_up, sin_dn


def _rep(x, n):
    return x if n == 1 else jnp.concatenate([x] * n, axis=1)


def _rope(x, cos_t, sin_up, sin_dn):
    w = x.shape[1]
    return x * cos_t + pltpu.roll(x, 16, 1) * sin_up + pltpu.roll(x, w - 16, 1) * sin_dn


def _rope_t(dy, cos_t, sin_up, sin_dn):
    w = dy.shape[1]
    return dy * cos_t + pltpu.roll(dy * sin_up, w - 16, 1) + pltpu.roll(dy * sin_dn, 16, 1)


def _mla_prep(q, kv, projp, tables, *, name):
    nh = N_ATT_HEADS

    def fn(qv, kvv, kr, cos_t, sin_up, sin_dn):
        qr = _rope(qv, _rep(cos_t, nh), _rep(sin_up, nh), _rep(sin_dn, nh))
        krr = _rope(kr, cos_t, sin_up, sin_dn)
        lane = lax.broadcasted_iota(jnp.int32, kvv.shape, 1) % LANES
        return qr, jnp.where(lane < HEAD, kvv, 0.0) + _rep(krr, nh), kvv

    w = q.shape[1]
    return _rowwise(fn, [q, kv, (projp, LANES, P_KR // LANES)] + list(tables), [],
                    [(w, BF16), (w, BF16), (w, BF16)], name=name)


def _mla_prep_bwd(dqr, dkf, tables, *, name):
    nh = N_ATT_HEADS

    def fn(dq, dk, cos_t, sin_up, sin_dn):
        dqp = _rope_t(dq, _rep(cos_t, nh), _rep(sin_up, nh), _rep(sin_dn, nh))
        dkrr = dk[:, 0:LANES]
        for h in range(1, nh):
            dkrr = dkrr + dk[:, LANES * h:LANES * (h + 1)]
        return dqp, _rope_t(dkrr, cos_t, sin_up, sin_dn)

    return _rowwise(fn, [dqr, dkf] + list(tables), [], [(dqr.shape[1], BF16), (LANES, BF16)], name=name)


_LOG2E = 1.0 / math.log(2.0)
_NT = (((1,), (1,)), ((), ()))
_TN = (((0,), (0,)), ((), ()))


def _attn_fwd(qr, kf, kvb, *, name, job=None):
    t_rows = qr.shape[0]
    tq = min(ATT_TQ, t_rows)
    nb = t_rows // tq
    scale = ATT_D ** -0.5

    c2 = scale * _LOG2E

    def body(q_ref, kf_ref, kvb_ref, o_ref, lse_ref):
        qi = pl.program_id(1)
        lane = lax.broadcasted_iota(jnp.int32, (tq, LANES), 1)
        causal_t = (lax.broadcasted_iota(jnp.int32, (tq, tq), 0) <= lax.broadcasted_iota(jnp.int32, (tq, tq), 1))
        heads = [slice(hh * LANES, (hh + 1) * LANES) for hh in range(2)]
        qs = [q_ref[:, cols] for cols in heads]

        def block(ki, carry, diagonal):
            rows = pl.ds(pl.multiple_of(ki * tq, tq), tq)
            new = []
            for q, cols, (m_old, l_old, acc_t) in zip(qs, heads, carry):
                s_t = lax.dot_general(kf_ref[rows, cols], q, _NT, preferred_element_type=F32)
                if diagonal:
                    s_t = jnp.where(causal_t, s_t, -1e30)
                m_new = jnp.maximum(m_old, jnp.max(s_t, axis=0, keepdims=True))
                p_t = jnp.exp2((s_t - m_new) * c2)
                a = jnp.exp2((m_old - m_new) * c2)
                pv_t = lax.dot_general(kvb_ref[rows, cols], p_t.astype(BF16), _TN, preferred_element_type=F32)
                new.append((m_new, a * l_old + jnp.sum(p_t, axis=0, keepdims=True), a * acc_t + pv_t))
            return tuple(new)

        init = (jnp.full((1, tq), -1e30, F32), jnp.zeros((1, tq), F32), jnp.zeros((LANES, tq), F32))
        carry = lax.fori_loop(0, qi, lambda ki, c: block(ki, c, False), (init, init))
        outs = []
        for hh, (m_fin, l_fin, acc_t) in enumerate(block(qi, carry, True)):
            lse_ref[hh] = m_fin * scale + jnp.log(l_fin)
            outs.append((acc_t / l_fin).T)
        o_ref[...] = jnp.where(lane < HEAD, pltpu.roll(outs[0], HEAD, 1), outs[1])

    pair = pl.BlockSpec((t_rows, 2 * LANES), lambda pr, qi: (0, pr))
    return _call(
        body, (qr, kf, kvb), name=name, grid=(N_ATT_HEADS // 2, nb),
        in_specs=[pl.BlockSpec((tq, 2 * LANES), lambda pr, qi: (qi, pr)), pair, pair],
        out_specs=[pl.BlockSpec((tq, LANES), lambda pr, qi: (qi, pr)),
                   pl.BlockSpec((2, 1, tq), lambda pr, qi: (pr, 0, qi))],
        out_shape=[jax.ShapeDtypeStruct((t_rows, N_ATT_HEADS * HEAD), F32),
                   jax.ShapeDtypeStruct((N_ATT_HEADS, 1, t_rows), F32)],
        sem=("parallel", "arbitrary"), job=job)


def _attn_bwd(qr, kf, kvb, dcat, o, lse, *, name, job=None):
    t_rows = qr.shape[0]
    tq = min(ATT_TQ, t_rows)
    nb = t_rows // tq
    scale = ATT_D ** -0.5
    c2 = scale * _LOG2E
    do_off = 2 * HG_W // LANES

    def body(q_ref, kf_ref, kvb_ref, do_ref, o_ref, lse_ref, dq_ref, dkv_ref, dk_ref):
        ki = pl.program_id(1)

        @pl.when(ki == 0)
        def _():
            dq_ref[...] = jnp.zeros_like(dq_ref)

        lane = lax.broadcasted_iota(jnp.int32, (tq, LANES), 1)
        causal_t = (lax.broadcasted_iota(jnp.int32, (tq, tq), 0) <= lax.broadcasted_iota(jnp.int32, (tq, tq), 1))
        heads = [slice(hh * LANES, (hh + 1) * LANES) for hh in range(2)]
        ks = [kf_ref[:, cols] for cols in heads]
        vs = [kvb_ref[:, cols] for cols in heads]

        def block(qi, carry, diagonal):
            rows = pl.ds(pl.multiple_of(qi * tq, tq), tq)
            do_pair, o_pair = do_ref[rows, :], o_ref[rows, :]
            new = []
            for hh, (cols, k, v, (dk, dv)) in enumerate(zip(heads, ks, vs, carry)):
                q = q_ref[rows, cols]
                do, ov = (pltpu.roll(do_pair, HEAD, 1), pltpu.roll(o_pair, HEAD, 1)) if hh == 0 else (do_pair, o_pair)
                do = jnp.where(lane >= HEAD, do, 0.0)
                delta = jnp.sum((do * ov).T, axis=0, keepdims=True)
                s_t = lax.dot_general(k, q, _NT, preferred_element_type=F32)
                if diagonal:
                    s_t = jnp.where(causal_t, s_t, -1e30)
                p_t = jnp.exp2(s_t * c2 - lse_ref[hh, :, rows] * _LOG2E)
                dob = do.astype(BF16)
                dv = dv + jnp.dot(p_t.astype(BF16), dob, preferred_element_type=F32)
                dp_t = lax.dot_general(v, dob, _NT, preferred_element_type=F32)
                ds_t = (p_t * (dp_t - delta) * scale).astype(BF16)
                dk = dk + jnp.dot(ds_t, q, preferred_element_type=F32)
                dq_ref[rows, cols] += lax.dot_general(ds_t, k, _TN, preferred_element_type=F32)
                new.append((dk, dv))
            return tuple(new)

        zero = jnp.zeros((tq, LANES), F32)
        carry = block(ki, ((zero, zero), (zero, zero)), True)
        carry = lax.fori_loop(ki + 1, nb, lambda qi, c: block(qi, c, False), carry)
        dkv_ref[...] = jnp.concatenate([jnp.where(lane < HEAD, dk, dv) for dk, dv in carry],
                                       axis=1).astype(dkv_ref.dtype)
        dk_ref[...] = jnp.concatenate([dk for dk, _ in carry], axis=1)

    pair_all = pl.BlockSpec((t_rows, 2 * LANES), lambda pr, ki: (0, pr))
    pair_blk = pl.BlockSpec((tq, 2 * LANES), lambda pr, ki: (ki, pr))
    wide = jax.ShapeDtypeStruct((t_rows, N_ATT_HEADS * LANES), F32)
    return _call(
        body, (qr, kf, kvb, dcat, o, lse), name=name, grid=(N_ATT_HEADS // 2, nb),
        in_specs=[pair_all, pair_blk, pair_blk,
                  pl.BlockSpec((t_rows, LANES), lambda pr, ki: (0, do_off + pr)),
                  pl.BlockSpec((t_rows, LANES), lambda pr, ki: (0, pr)),
                  pl.BlockSpec((2, 1, t_rows), lambda pr, ki: (pr, 0, 0))],
        out_specs=[pair_all, pair_blk, pair_blk],
        out_shape=[wide, jax.ShapeDtypeStruct(wide.shape, BF16), wide],
        sem=("parallel", "arbitrary"), job=job)


def _my_pos():
    return lax.axis_index("x"), lax.axis_index("y"), lax.axis_index("c")


def _all_gather(xs, *, name):
    return _gather_forward(_run_job(_gather_job(xs), name=name), name=name + "_forward")


def _remote(src, dst, send_sems, recv_sems, k, dev):
    return pltpu.make_async_remote_copy(src_ref=src, dst_ref=dst, send_sem=send_sems.at[k], recv_sem=recv_sems.at[k],
                                        device_id=dev, device_id_type=MESH)


def _gather_job(xs):
    n = len(xs)

    def make(x_refs, out_refs, send_sems, recv_sems, local_sems):
        mx, my, mc = _my_pos()
        mine = 4 * mx + 2 * my + mc
        peers = [(mx, my, 1 - mc), (1 - mx, my, mc), (mx, 1 - my, mc), (1 - mx, 1 - my, mc)]
        sends, recvs, local = [], [], []
        for a in range(n):
            local.append(pltpu.make_async_copy(x_refs[a], out_refs[a].at[mine], local_sems.at[a]))
            for k, dev in enumerate(peers):
                theirs = 4 * dev[0] + 2 * dev[1] + dev[2]
                sends.append(_remote(x_refs[a], out_refs[a].at[mine], send_sems, recv_sems, 4 * a + k, dev))
                recvs.append(_remote(x_refs[a], out_refs[a].at[theirs], send_sems, recv_sems, 4 * a + k, dev))
        return sends, recvs, local

    shapes = [jax.ShapeDtypeStruct((N_DEV,) + x.shape, x.dtype) for x in xs]
    return _copies_job(xs, shapes, 4 * n, n, make)


def _gather_forward(gs, *, name):
    n = len(gs)

    def body(*refs):
        out_refs = refs[n:2 * n]
        send_sems, recv_sems = refs[2 * n:]
        mx, my, mc = _my_pos()
        chips = [(1 - mx, my), (mx, 1 - my), (1 - mx, 1 - my)]
        sends, recvs = [], []
        for a in range(n):
            for j, (cx, cy) in enumerate(chips):
                here, there = out_refs[a].at[4 * cx + 2 * cy + mc], out_refs[a].at[4 * cx + 2 * cy + 1 - mc]
                sends.append(_remote(here, here, send_sems, recv_sems, 3 * a + j, (mx, my, 1 - mc)))
                recvs.append(_remote(here, there, send_sems, recv_sems, 3 * a + j, (mx, my, 1 - mc)))
        for cp in sends:
            cp.start()
        for cp in recvs:
            cp.wait_recv()
        for cp in sends:
            cp.wait_send()

    return pl.pallas_call(
        body, name=name, out_shape=[jax.ShapeDtypeStruct(g.shape, g.dtype) for g in gs],
        in_specs=[_ANY] * n, out_specs=[_ANY] * n, input_output_aliases={a: a for a in range(n)},
        scratch_shapes=[pltpu.SemaphoreType.DMA((3 * n,)), pltpu.SemaphoreType.DMA((3 * n,))],
    )(*gs)


def _pair_job(xs):
    n = len(xs)

    def make(x_refs, out_refs, send_sems, recv_sems, local_sems):
        mx, my, mc = _my_pos()
        copies = [_remote(x_refs[a].at[g, 1 - mc], out_refs[a].at[g], send_sems, recv_sems, 4 * a + g, (mx, my, 1 - mc))
                  for a in range(n) for g in range(4)]
        return copies, copies, []

    shapes = [jax.ShapeDtypeStruct((4,) + x.shape[2:], x.dtype) for x in xs]
    return _copies_job(xs, shapes, 4 * n, 0, make)


def _pair_add(x, r, core, *, name):
    _, _, a, b = x.shape
    ta = _row_tile(a, 256)

    def body(c_ref, x_ref, r_ref, o_ref):
        o_ref[...] = (x_ref[...] + r_ref[...]).astype(o_ref.dtype)

    blk = pl.BlockSpec((None, ta, b), lambda g, i, c_ref: (g, i, 0))
    return pl.pallas_call(
        body, name=name,
        grid_spec=pltpu.PrefetchScalarGridSpec(
            num_scalar_prefetch=1, grid=(4, a // ta),
            in_specs=[pl.BlockSpec((None, None, ta, b), lambda g, i, c_ref: (g, c_ref[0], i, 0)), blk],
            out_specs=blk),
        out_shape=jax.ShapeDtypeStruct((4, a, b), BF16),
        compiler_params=_cparams(("parallel", "parallel")),
    )(core, x, r)


def _quad_job(xs):
    n = len(xs)

    def make(x_refs, out_refs, send_sems, recv_sems, local_sems):
        mx, my, mc = _my_pos()
        mine = 2 * mx + my
        peers = [((1 - mx, my, mc), 2 * (1 - mx) + my), ((mx, 1 - my, mc), 2 * mx + 1 - my),
                 ((1 - mx, 1 - my, mc), 2 * (1 - mx) + 1 - my)]
        sends, recvs, local = [], [], []
        for a in range(n):
            local.append(pltpu.make_async_copy(x_refs[a].at[mine], out_refs[a].at[mine], local_sems.at[a]))
            for k, (dev, g) in enumerate(peers):
                sends.append(_remote(x_refs[a].at[g], out_refs[a].at[mine], send_sems, recv_sems, 3 * a + k, dev))
                recvs.append(_remote(x_refs[a].at[g], out_refs[a].at[g], send_sems, recv_sems, 3 * a + k, dev))
        return sends, recvs, local

    shapes = [jax.ShapeDtypeStruct(x.shape, x.dtype) for x in xs]
    return _copies_job(xs, shapes, 3 * n, n, make)


def _row_tile(r, pref):
    t = min(pref, r)
    while r % t or (t % 8 and t != r):
        t -= 1
    return t


def _adamw(parts, w, m, v, layer, *, name, tile=256):
    g, a, b = parts.shape
    tile = _row_tile(a, tile)
    c1 = 1.0 / (1.0 - ADAM_B1 ** ADAM_STEP)
    c2 = 1.0 / (1.0 - ADAM_B2 ** ADAM_STEP)

    def body(p_ref, w_ref, m_ref, v_ref, g_ref, d_ref, mo_ref, vo_ref):
        grad = p_ref[0].astype(F32)
        for j in range(1, g):
            grad = grad + p_ref[j].astype(F32)
        mn = ADAM_B1 * m_ref[...] + (1.0 - ADAM_B1) * grad
        vn = ADAM_B2 * v_ref[...] + (1.0 - ADAM_B2) * (grad * grad)
        g_ref[...] = grad
        mo_ref[...] = mn
        vo_ref[...] = vn
        d_ref[...] = -ADAM_LR * ((mn * c1) / (jnp.sqrt(vn * c2) + ADAM_EPS) + ADAM_WD * w_ref[...])

    slab = pl.BlockSpec((tile, b), lambda i: (i, 0))
    src = slab if layer is None else pl.BlockSpec((None, tile, b), lambda i: (layer, i, 0))
    return pl.pallas_call(
        body, name=name, grid=(a // tile,),
        in_specs=[pl.BlockSpec((g, tile, b), lambda i: (0, i, 0)), src, src, src],
        out_specs=[slab] * 4,
        out_shape=[jax.ShapeDtypeStruct((a, b), F32)] * 4,
        compiler_params=_cparams(("parallel",)),
    )(parts, w, m, v)


W_IN_SHARD = 276


def _w_in_dest(col):
    return jnp.where(col < P_KR, col, jnp.where(col < P_KR + 256, col + (P_CKV - P_KR), col - 2176 + P_KR + HEAD))


def _place_w_in(g, *, name):
    _, d, sh = g.shape
    tc = 768

    def body(g_ref, o_ref, acc_ref):
        ct, j = pl.program_id(0), pl.program_id(1)

        @pl.when(j == 0)
        def _():
            acc_ref[...] = jnp.zeros_like(acc_ref)

        src = j * sh + lax.broadcasted_iota(jnp.int32, (sh, tc), 0)
        dst = ct * tc + lax.broadcasted_iota(jnp.int32, (sh, tc), 1)
        place = (_w_in_dest(src) == dst).astype(BF16)
        acc_ref[...] += jnp.dot(g_ref[...], place, preferred_element_type=F32)

        @pl.when(j == N_DEV - 1)
        def _():
            o_ref[...] = acc_ref[...].astype(o_ref.dtype)

    return pl.pallas_call(
        body, name=name, grid=(P_COLS // tc, N_DEV),
        in_specs=[pl.BlockSpec((None, d, sh), lambda ct, j: (j, 0, 0))],
        out_specs=pl.BlockSpec((d, tc), lambda ct, j: (0, ct)),
        out_shape=jax.ShapeDtypeStruct((d, P_COLS), BF16),
        scratch_shapes=[pltpu.VMEM((d, tc), F32)],
        compiler_params=_cparams(("parallel", "arbitrary")),
    )(g)


def _unplace_w_in(dw, *, name):
    d = dw.shape[0]
    sh = W_IN_SHARD

    def body(dw_ref, o_ref):
        j = pl.program_id(0)
        src = j * sh + lax.broadcasted_iota(jnp.int32, (P_COLS, sh), 1)
        dst = lax.broadcasted_iota(jnp.int32, (P_COLS, sh), 0)
        pick = (_w_in_dest(src) == dst).astype(BF16)
        x = dw_ref[...]
        o_ref[...] = _split_dot(x, pick)

    return pl.pallas_call(
        body, name=name, grid=(N_DEV,),
        in_specs=[pl.BlockSpec((d, P_COLS), lambda j: (0, 0))],
        out_specs=pl.BlockSpec((None, d, sh), lambda j: (j, 0, 0)),
        out_shape=jax.ShapeDtypeStruct((N_DEV, d, sh), F32),
        compiler_params=_cparams(("arbitrary",)),
    )(dw)


def _gate_up_swiglu(h1, wgu, *, name):
    t_rows, k = h1.shape
    w = wgu.shape[2]
    tm = _tile(t_rows, 1024)

    def body(a_ref, wg_ref, wu_ref, gu_ref, act_ref):
        a = a_ref[...].astype(BF16)
        gate = jnp.dot(a, wg_ref[...], preferred_element_type=F32)
        up = jnp.dot(a, wu_ref[...], preferred_element_type=F32)
        gu_ref[0] = gate.astype(gu_ref.dtype)
        gu_ref[1] = up.astype(gu_ref.dtype)
        act_ref[...] = (gate * _sigmoid(gate) * up).astype(act_ref.dtype)

    return pl.pallas_call(
        body, name=name, grid=(t_rows // tm, 4),
        in_specs=[pl.BlockSpec((tm, k), lambda i, j: (i, 0)),
                  pl.BlockSpec((None, k, w), lambda i, j: (j, 0, 0)),
                  pl.BlockSpec((None, k, w), lambda i, j: (j + 4, 0, 0))],
        out_specs=[pl.BlockSpec((2, None, tm, w), lambda i, j: (0, j, i, 0)),
                   pl.BlockSpec((None, tm, w), lambda i, j: (j, i, 0))],
        out_shape=[jax.ShapeDtypeStruct((2, 4, t_rows, w), BF16), jax.ShapeDtypeStruct((4, t_rows, w), BF16)],
        compiler_params=_cparams(("parallel", "arbitrary")),
    )(h1, wgu, wgu)


def _down_dx_swiglu(dffn, wdown, gu, *, name):
    t_rows, k = dffn.shape
    w = gu.shape[3]
    tm = _tile(t_rows, 1024)

    def body(d_ref, w_ref, gu_ref, o_ref):
        dact = lax.dot_general(d_ref[...].astype(BF16), w_ref[...], _NT, preferred_element_type=F32)
        gate, up = gu_ref[0].astype(F32), gu_ref[1].astype(F32)
        sg = _sigmoid(gate)
        o_ref[0] = (dact * up * (sg * (1.0 + gate * (1.0 - sg)))).astype(o_ref.dtype)
        o_ref[1] = (dact * gate * sg).astype(o_ref.dtype)

    blk = pl.BlockSpec((2, None, tm, w), lambda i, j: (0, j, i, 0))
    return pl.pallas_call(
        body, name=name, grid=(t_rows // tm, 4),
        in_specs=[pl.BlockSpec((tm, k), lambda i, j: (i, 0)), pl.BlockSpec((w, k), lambda i, j: (j, 0)), blk],
        out_specs=blk, out_shape=jax.ShapeDtypeStruct(gu.shape, BF16),
        compiler_params=_cparams(("parallel", "arbitrary")),
    )(dffn, wdown, gu)


BIG = ("w_in", "mla_w_uq", "mla_w_ukv", "w_out", "w_gate_up", "w_down", "ple_w_gate", "ple_w_proj")
SMALL = ("ln_in_g", "ln_in_b", "hgrn_lb_logits", "hgrn_norm_g", "sgu_ln_g", "sgu_ln_b", "sgu_w_s", "sgu_b_s",
         "mla_q_norm_g", "mla_kv_norm_g", "ln1_g", "ln1_b", "ln2_g", "ln2_b")
ORDER = ("ln_in_g", "ln_in_b", "w_in", "hgrn_lb_logits", "hgrn_norm_g", "sgu_ln_g", "sgu_ln_b", "sgu_w_s", "sgu_b_s",
         "mla_q_norm_g", "mla_w_uq", "mla_kv_norm_g", "mla_w_ukv", "w_out", "ln1_g", "ln1_b", "w_gate_up", "w_down",
         "ple_w_gate", "ple_w_proj", "ln2_g", "ln2_b")


def _slab(a, align):
    s = a.reshape(-1, LANES)
    pad = -s.shape[0] % align
    return jnp.pad(s, ((0, pad), (0, 0))) if pad else s


def _pack(arrays, align=16, total_align=512):
    s = jnp.concatenate([_slab(a, align) for a in arrays], axis=0)
    pad = -s.shape[0] % total_align
    return jnp.pad(s, ((0, pad), (0, 0))) if pad else s


def _unpack(slab, shapes, align=16):
    out, r0 = [], 0
    for s in shapes:
        nr = math.prod(s) // LANES
        out.append(slab[r0:r0 + nr].reshape(s))
        r0 += nr + (-nr % align)
    return out


def _blocks_to_cols(g, *, name):
    nb, a, b = g.shape

    def body(g_ref, o_ref):
        o_ref[...] = g_ref[...]

    return pl.pallas_call(
        body, name=name, grid=(nb,), in_specs=[pl.BlockSpec((None, a, b), lambda j: (j, 0, 0))],
        out_specs=pl.BlockSpec((a, b), lambda j: (0, j)), out_shape=jax.ShapeDtypeStruct((a, nb * b), g.dtype),
        compiler_params=_cparams(("parallel",)),
    )(g)


def _cols_to_blocks(x, *, name):
    a, b = x.shape[0], x.shape[1] // N_DEV

    def body(x_ref, o_ref):
        o_ref[...] = x_ref[...]

    return pl.pallas_call(
        body, name=name, grid=(N_DEV,), in_specs=[pl.BlockSpec((a, b), lambda j: (0, j))],
        out_specs=pl.BlockSpec((None, a, b), lambda j: (j, 0, 0)), out_shape=jax.ShapeDtypeStruct((N_DEV, a, b), x.dtype),
        compiler_params=_cparams(("parallel",)),
    )(x)


def _weight_shards(w, li):
    uq_pad = ((0, 0), (0, LANES - ATT_D))
    shards = {k: w[k][li] for k in BIG}
    shards["mla_w_uq"] = jnp.pad(shards["mla_w_uq"], uq_pad)
    return {k: s.astype(BF16) for k, s in shards.items()}


def _usable_weights(g, *, name):
    out = {}
    for k, a in g.items():
        if k == "w_in":
            out[k] = _place_w_in(a, name=name + "_place_w_in")
        elif k in ("w_out", "w_down", "ple_w_gate"):
            out[k] = a.reshape(a.shape[0] * a.shape[1], a.shape[2])
        elif k == "w_gate_up":
            out[k] = a
        else:
            out[k] = _blocks_to_cols(a, name=name + "_cols_" + k)
    return out


def _as_pairs(g):
    if g.ndim == 2:
        return g.reshape((4, 2, g.shape[0] // N_DEV) + g.shape[1:])
    return g.reshape((4, 2) + g.shape[1:])


def _twice(fn):
    return lambda *a: fn(*a) * 2


def _layer_forward(li, h, hb, p_i, wts, sm, lbs, tables, alpha, hgrn_job=None, after_hgrn=None, attn_job=None,
                   after_attn=None):
    n = f"l{li}_"
    row1 = lambda a: a.reshape(1, -1)
    projp = _mm(hb, wts["w_in"], name=n + "proj")
    ng = row1(sm["hgrn_norm_g"][li])
    res = _hgrn_fwd(projp, lbs[li], ng, name=n + "hgrn_fwd", job=hgrn_job)
    if hgrn_job is not None:
        res, got = res
        wts = dict(wts, **after_hgrn(got))
    o_a, o_pre, states = res
    lg, lbias = row1(sm["sgu_ln_g"][li]), row1(sm["sgu_ln_b"][li])
    w_s = sm["sgu_w_s"][li]
    bias_full = jnp.repeat(sm["sgu_b_s"][li].T, HEAD, axis=1)
    o_b = _sgu_fwd(projp, lg, lbias, w_s, bias_full, name=n + "sgu_fwd")
    qg, kvg = row1(sm["mla_q_norm_g"][li]), row1(sm["mla_kv_norm_g"][li])
    cq_view, ckv_view = (projp, 384, P_CQ // 384), (projp, 256, P_CKV // 256)
    (cqn,) = _rowwise(_fn_rms, [cq_view], [qg], [(384, BF16)], name=n + "q_norm")
    (ckvn,) = _rowwise(_fn_rms, [ckv_view], [kvg], [(256, BF16)], name=n + "kv_norm")
    q = _mm(cqn, wts["mla_w_uq"], name=n + "uq")
    kv = _mm(ckvn, wts["mla_w_ukv"], name=n + "ukv")
    qr, kf, kvb = _mla_prep(q, kv, projp, tables, name=n + "mla_prep")
    res = _attn_fwd(qr, kf, kvb, name=n + "attn_fwd", job=attn_job)
    if attn_job is not None:
        res, got = res
        wts = dict(wts, **after_attn(got))
    o_c, lse = res
    cat = jnp.concatenate([o_a, o_b, o_c.astype(BF16)], axis=1)
    mix = _mm(cat, wts["w_out"], name=n + "out_proj")
    g1, b1 = row1(sm["ln1_g"][li]), row1(sm["ln1_b"][li])
    d = h.shape[1]
    h1, h1b = _rowwise(_twice(_make_post_mix(alpha)), [h, mix], [g1, b1], [(d, F32), (d, BF16)], name=n + "ln1")
    gu, act = _gate_up_swiglu(h1b, wts["w_gate_up"], name=n + "gate_up")
    ffn = _mm(act, wts["w_down"], am="bmk", name=n + "down")
    pg = _mm(h1b, wts["ple_w_gate"], name=n + "ple_gate")
    pp = _mm(p_i, wts["ple_w_proj"], name=n + "ple_proj")
    g2, b2 = row1(sm["ln2_g"][li]), row1(sm["ln2_b"][li])
    h2, h2b = _rowwise(_twice(_make_ple_ln(alpha)), [h1, ffn, pg, pp], [g2, b2], [(d, F32), (d, BF16)],
                       name=n + "ln2")
    saved = dict(h=h, hb=hb, h1b=h1b, projp=projp, o_pre=o_pre, states=states, cqn=cqn, ckvn=ckvn, qr=qr, kf=kf, kvb=kvb, o_c=o_c,
                 lse=lse, cat=cat, mix=mix, h1=h1, gu=gu, act=act, ffn=ffn, pg=pg, pp=pp, ng=ng, lg=lg, wts=wts,
                 lbias=lbias, w_s=w_s, bias_full=bias_full, qg=qg, kvg=kvg, g1=g1, b1=b1, g2=g2, b2=b2)
    return (h2, h2b), saved


RS_EARLY = ("ple_w_proj", "ple_w_gate", "w_down", "w_gate_up", "w_out")
RS_LATE = ("mla_w_uq", "mla_w_ukv", "w_in")


def _layer_backward(li, dh2_parts, p_i, sv, lbs, tables, alpha, core, carried=None):
    n = f"l{li}_b_"
    wts = sv["wts"]
    gr = {}
    dh1_a, dffn, dpg, dpp, gr["ln2_g"], gr["ln2_b"] = _rowwise_vjp(
        _make_ple_ln(alpha), [sv["h1"], sv["ffn"], sv["pg"], sv["pp"]], [sv["g2"], sv["b2"]], [dh2_parts],
        groups=[[0], [1], [2], [3]], gdtypes=[F32, BF16, BF16, BF16], name=n + "ln2")
    big = {}
    big["ple_w_proj"] = _cols_to_blocks(_mm(p_i, dpp, am="km", name=n + "ple_proj_dw"), name=n + "ple_proj_dw_blocks")
    big["ple_w_gate"] = _mm(sv["h1b"], dpg, am="km", name=n + "ple_gate_dw")
    dh1_b = _mm(dpg, wts["ple_w_gate"], bm="nk", name=n + "ple_gate_dx")
    big["w_down"] = _mm(sv["act"], dffn, am="bkm", name=n + "down_dw")
    dgu = _down_dx_swiglu(dffn, wts["w_down"], sv["gu"], name=n + "down_dx")
    dgu = dgu.reshape((N_DEV,) + dgu.shape[2:])
    big["w_gate_up"], carried_got = _mm(sv["h1b"], dgu, am="km", bm="bkn", om="bmn", name=n + "gate_up_dw",
                                        job=carried), None
    if carried is not None:
        big["w_gate_up"], carried_got = big["w_gate_up"]
    early = [_as_pairs(big[k]) for k in RS_EARLY[:-1]]
    dh1_c, theirs = _mm(dgu, wts["w_gate_up"], am="bmk", bm="bnk", name=n + "gate_up_dx", job=_pair_job(early))
    dh_a, dmix, gr["ln1_g"], gr["ln1_b"] = _rowwise_vjp(
        _make_post_mix(alpha), [sv["h"], sv["mix"]], [sv["g1"], sv["b1"]], [[dh1_a, dh1_b, dh1_c]],
        groups=[[0], [1]], gdtypes=[F32, BF16], name=n + "ln1")
    big["w_out"] = _mm(sv["cat"], dmix, am="km", name=n + "out_proj_dw")
    early.append(_as_pairs(big["w_out"]))
    dcat, their_w_out = _mm(dmix, wts["w_out"], bm="nk", name=n + "out_proj_dx", job=_pair_job(early[-1:]))
    sums = [_pair_add(x, r, core, name=n + "pair_add_" + k)
            for k, x, r in zip(RS_EARLY, early, list(theirs) + list(their_w_out))]

    (dqr, dkv, dkf), early_quads = _attn_bwd(sv["qr"], sv["kf"], sv["kvb"], dcat, sv["o_c"], sv["lse"],
                                             name=n + "attn", job=_quad_job(sums))
    dqpad, dkr = _mla_prep_bwd(dqr, dkf, tables, name=n + "mla_prep")
    big["mla_w_uq"] = _cols_to_blocks(_mm(sv["cqn"], dqpad, am="km", name=n + "uq_dw"), name=n + "uq_dw_blocks")
    dcqn = _mm(dqpad, wts["mla_w_uq"], bm="nk", name=n + "uq_dx")
    big["mla_w_ukv"] = _cols_to_blocks(_mm(sv["ckvn"], dkv, am="km", name=n + "ukv_dw"), name=n + "ukv_dw_blocks")
    dckvn = _mm(dkv, wts["mla_w_ukv"], bm="nk", name=n + "ukv_dx")
    projp = sv["projp"]
    dcq, gr["mla_q_norm_g"] = _rowwise_vjp(_fn_rms, [(projp, 384, P_CQ // 384)], [sv["qg"]], [[dcqn]],
                                           groups=[[0]], gdtypes=[BF16], name=n + "q_norm")
    dckv, gr["mla_kv_norm_g"] = _rowwise_vjp(_fn_rms, [(projp, 256, P_CKV // 256)], [sv["kvg"]], [[dckvn]],
                                             groups=[[0]], gdtypes=[BF16], name=n + "kv_norm")
    dsgu, gr["sgu_ln_g"], gr["sgu_ln_b"], gr["sgu_w_s"], gr["sgu_b_s"] = _sgu_bwd(
        projp, sv["lg"], sv["lbias"], sv["w_s"], sv["bias_full"], dcat, name=n + "sgu")
    dhg, gr["hgrn_norm_g"], gr["lower_bound"] = _hgrn_bwd(
        projp, lbs[li], sv["ng"], sv["o_pre"], sv["states"], dcat, name=n + "hgrn")
    dprojp = jnp.concatenate([dhg, dsgu, dcq, dkr, dckv], axis=1)
    big["w_in"] = _unplace_w_in(_mm(sv["hb"], dprojp, am="km", name=n + "proj_dw"), name=n + "proj_dw_shards")
    late = [_as_pairs(big[k]) for k in RS_LATE]
    dh_b, theirs = _mm(dprojp, wts["w_in"], bm="nk", name=n + "proj_dx", job=_pair_job(late))
    late_sums = [_pair_add(x, r, core, name=n + "pair_add_" + k) for k, x, r in zip(RS_LATE, late, theirs)]
    return [dh_a, dh_b], gr, early_quads, late_sums, carried_got


def kernel(x, p, positions, ln_in_g, ln_in_b, w_in, hgrn_lb_logits, hgrn_norm_g, sgu_ln_g, sgu_ln_b, sgu_w_s, sgu_b_s, mla_q_norm_g, mla_w_uq, mla_kv_norm_g, mla_w_ukv, w_out, ln1_g, ln1_b, w_gate_up, w_down, ple_w_gate, ple_w_proj, ln2_g, ln2_b, loss_target, m_ln_in_g, m_ln_in_b, m_w_in, m_hgrn_lb_logits, m_hgrn_norm_g, m_sgu_ln_g, m_sgu_ln_b, m_sgu_w_s, m_sgu_b_s, m_mla_q_norm_g, m_mla_w_uq, m_mla_kv_norm_g, m_mla_w_ukv, m_w_out, m_ln1_g, m_ln1_b, m_w_gate_up, m_w_down, m_ple_w_gate, m_ple_w_proj, m_ln2_g, m_ln2_b, v_ln_in_g, v_ln_in_b, v_w_in, v_hgrn_lb_logits, v_hgrn_norm_g, v_sgu_ln_g, v_sgu_ln_b, v_sgu_w_s, v_sgu_b_s, v_mla_q_norm_g, v_mla_w_uq, v_mla_kv_norm_g, v_mla_w_ukv, v_w_out, v_ln1_g, v_ln1_b, v_w_gate_up, v_w_down, v_ple_w_gate, v_ple_w_proj, v_ln2_g, v_ln2_b):
    args = dict(locals())
    w = {k: args[k] for k in ORDER}
    m = {k: args["m_" + k] for k in ORDER}
    v = {k: args["v_" + k] for k in ORDER}
    depth = w_in.shape[0]
    assert depth == 2, "the lower-bound kernel is written for two layers"
    alpha = (2 * depth) ** 0.25
    xs, tgt = x[0], loss_target[0]
    d_model = xs.shape[1]

    shards = [_weight_shards(w, li) for li in range(depth)]
    on_hgrn0 = ("mla_w_uq", "mla_w_ukv", "w_out", "ple_w_gate", "ple_w_proj")
    ffn0 = ("w_gate_up", "w_down")
    first1 = ("w_in", "mla_w_uq", "mla_w_ukv", "w_out")
    on_attn1 = ("w_gate_up", "w_down", "ple_w_gate", "ple_w_proj")
    layer1_first = {}

    def after_hgrn0(got):
        got = _gather_forward(got, name="gather_l0a_forward")
        return _usable_weights(dict(zip(on_hgrn0, got)), name="l0")

    def after_attn0(got):
        got = _gather_forward(got, name="gather_l0b_forward")
        layer1_first.update(_usable_weights(dict(zip(first1, got[len(ffn0):])), name="l1"))
        return _usable_weights(dict(zip(ffn0, got[:len(ffn0)])), name="l0")

    def after_attn1(got):
        got = _gather_forward(got, name="gather_l1_forward")
        return _usable_weights(dict(zip(on_attn1, got)), name="l1")

    tables = _rope_tables(positions[0])
    row1 = lambda a: a.reshape(1, -1)
    l0, l1 = row1(hgrn_lb_logits[0]), row1(hgrn_lb_logits[1])
    lbs = _rowwise(_fn_lower_bounds, [l0, l1], [], [(HG_W, F32), (HG_W, F32)], name="lower_bounds")

    gin, bin_ = row1(ln_in_g), row1(ln_in_b)
    (h, hb), g_in = _rowwise(_twice(_fn_ln), [xs], [gin, bin_], [(d_model, F32), (d_model, BF16)], name="ln_in",
                             job=_gather_job([shards[0]["w_in"]]))
    w_in0 = _usable_weights({"w_in": _gather_forward(g_in, name="gather_l0_w_in_forward")[0]}, name="l0")
    (h, hb), sv0 = _layer_forward(
        0, h, hb, p[0, 0], w_in0, w, lbs, tables, alpha,
        hgrn_job=_gather_job([shards[0][k] for k in on_hgrn0]), after_hgrn=after_hgrn0,
        attn_job=_gather_job([shards[0][k] for k in ffn0] + [shards[1][k] for k in first1]), after_attn=after_attn0)
    (h, _), sv1 = _layer_forward(
        1, h, hb, p[1, 0], layer1_first, w, lbs, tables, alpha,
        attn_job=_gather_job([shards[1][k] for k in on_attn1]), after_attn=after_attn1)
    saved = [sv0, sv1]
    dy, loss_local = _loss_and_grad(h, tgt, name="loss")
    loss = lax.psum(loss_local[0, 0], ("x", "y", "c"))

    core = lax.axis_index("c").astype(jnp.int32).reshape(1)
    dparts, grads, quads, carried = [dy], [None] * depth, [None] * depth, None
    for li in reversed(range(depth)):
        dparts, grads[li], early_quads, late_sums, late_quads = _layer_backward(
            li, dparts, p[li, 0], saved[li], lbs, tables, alpha, core, carried=carried)
        quads[li] = dict(zip(RS_EARLY, early_quads))
        if carried is not None:
            quads[li + 1].update(zip(RS_LATE, late_quads))
        carried = _quad_job(late_sums)
    (dx, d_gin, d_bin), late_quads = _rowwise_vjp(_fn_ln, [xs], [gin, bin_], [dparts], groups=[[0]], name="ln_in_b",
                                                   job=carried)
    quads[0].update(zip(RS_LATE, late_quads))
    dl0, dl1 = _rowwise_vjp(_fn_lower_bounds, [l0, l1], [], [[grads[0]["lower_bound"]], [grads[1]["lower_bound"]]],
                            groups=[[0], [1]], name="lower_bounds_b")

    prefixes = ("grad_", "delta_", "new_m_", "new_v_")
    per_layer = {pre + k: [] for pre in prefixes for k in BIG}
    uq_pad = ((0, 0), (0, 0), (0, LANES - ATT_D))
    state = {k: ((jnp.pad(w[k], uq_pad), jnp.pad(m[k], uq_pad), jnp.pad(v[k], uq_pad)) if k == "mla_w_uq"
                 else (w[k], m[k], v[k])) for k in BIG}
    for li in range(depth):
        for k in BIG:
            res4 = _adamw(quads[li][k], *state[k], li, name=f"adamw_l{li}_{k}")
            for pre, a in zip(prefixes, res4):
                per_layer[pre + k].append(a[:, :ATT_D] if k == "mla_w_uq" else a)
    out = {name: jnp.stack(vals) for name, vals in per_layer.items()}

    small_g = {"ln_in_g": d_gin.reshape(-1), "ln_in_b": d_bin.reshape(-1),
               "hgrn_lb_logits": jnp.stack([dl0.reshape(-1), dl1.reshape(-1)])}
    for k in SMALL[3:]:
        small_g[k] = jnp.stack([grads[li][k].reshape(w[k].shape[1:]) for li in range(depth)])
    (small_parts,) = _all_gather([_pack([small_g[k] for k in SMALL])], name="gather_small_grads")
    slabs = _adamw(small_parts, _pack([w[k] for k in SMALL]), _pack([m[k] for k in SMALL]),
                   _pack([v[k] for k in SMALL]), None, name="adamw_small")
    shapes = [w[k].shape for k in SMALL]
    for pre, slab in zip(prefixes, slabs):
        for k, a in zip(SMALL, _unpack(slab, shapes)):
            out[pre + k] = a
    res = [loss, dx[None]]
    for prefix in ("grad_", "delta_", "new_m_", "new_v_"):
        res += [out[prefix + k] for k in ORDER]
    return tuple(res)
```

```python
import functools
import math

import jax
import jax.numpy as jnp
from jax import lax
from jax.experimental import pallas as pl
from jax.experimental.pallas import tpu as pltpu

F32 = jnp.float32
BF16 = jnp.bfloat16
MESH = pl.DeviceIdType.MESH

LN_EPS = 1e-5
RMS_EPS = 1e-6
ROPE_THETA = 10000.0
ADAM_LR, ADAM_B1, ADAM_B2, ADAM_EPS, ADAM_WD, ADAM_STEP = 0.001, 0.9, 0.999, 1e-08, 0.01, 10

N_DEV = 8
LANES = 128
HG_CHUNK = 16
HG_W = 256
HEAD = 64
SGU_CHUNK = 128
N_ATT_HEADS = 8
ATT_D = 96
VMEM_LIMIT = 56 * 1024 * 1024

HG_TILE = 256
ATT_TQ = 512
ROW_TILE = 256

P_CQ, P_KR, P_CKV, P_COLS = 1536, 1920, 2048, 2304


def _cparams(sem):
    return pltpu.CompilerParams(dimension_semantics=sem, vmem_limit_bytes=VMEM_LIMIT)


_ANY = pl.BlockSpec(memory_space=pl.ANY)


def _call(body, operands, *, name, grid, in_specs, out_specs, out_shape, sem, scratch_shapes=(), job=None):
    if job is None:
        return pl.pallas_call(body, name=name, grid=grid, in_specs=in_specs, out_specs=out_specs, out_shape=out_shape,
                              scratch_shapes=list(scratch_shapes), compiler_params=_cparams(sem))(*operands)
    single = not isinstance(out_shape, (list, tuple))
    shapes = [out_shape] if single else list(out_shape)
    ospecs = [out_specs] if single else list(out_specs)
    ni, no, ns = len(operands), len(shapes), len(scratch_shapes)
    ji, jo = len(job.inputs), len(job.out_shapes)

    def hosted(*refs):
        p = 0
        parts = []
        for cnt in (ni, ji, no, jo, ns):
            parts.append(refs[p:p + cnt])
            p += cnt
        ins, jins, outs, jouts, scr = parts
        jsems = refs[p:]
        ids = [pl.program_id(a) for a in range(len(grid))]
        first = functools.reduce(lambda a, b: a & b, [i == 0 for i in ids])
        last = functools.reduce(lambda a, b: a & b, [i == g - 1 for i, g in zip(ids, grid)])

        @pl.when(first)
        def _():
            job.start(jins, jouts, jsems)

        body(*ins, *outs, *scr)

        @pl.when(last)
        def _():
            job.finish(jins, jouts, jsems)

    res = pl.pallas_call(
        hosted, name=name, grid=grid,
        in_specs=list(in_specs) + [_ANY] * ji, out_specs=ospecs + [_ANY] * jo,
        out_shape=shapes + list(job.out_shapes),
        scratch_shapes=list(scratch_shapes) + [pltpu.SemaphoreType.DMA((c,)) for c in job.sem_counts],
        compiler_params=_cparams(("arbitrary",) * len(grid)),
    )(*operands, *job.inputs)
    own = res[0] if single else res[:no]
    return own, res[no:]


class _Job:
    def __init__(self, inputs, out_shapes, sem_counts, start, finish):
        self.inputs, self.out_shapes, self.sem_counts = list(inputs), list(out_shapes), list(sem_counts)
        self.start, self.finish = start, finish


def _copies_job(inputs, out_shapes, n_remote, n_local, make):
    def start(jins, jouts, sems):
        sends, _, local = make(jins, jouts, *sems)
        for cp in local + sends:
            cp.start()

    def finish(jins, jouts, sems):
        sends, recvs, local = make(jins, jouts, *sems)
        for cp in recvs:
            cp.wait_recv()
        for cp in sends:
            cp.wait_send()
        for cp in local:
            cp.wait()

    return _Job(inputs, out_shapes, [n_remote, n_remote, max(n_local, 1)], start, finish)


def _run_job(job, *, name):
    ji, jo = len(job.inputs), len(job.out_shapes)

    def body(*refs):
        jins, jouts, sems = refs[:ji], refs[ji:ji + jo], refs[ji + jo:]
        job.start(jins, jouts, sems)
        job.finish(jins, jouts, sems)

    return pl.pallas_call(
        body, name=name, out_shape=list(job.out_shapes), in_specs=[_ANY] * ji, out_specs=[_ANY] * jo,
        scratch_shapes=[pltpu.SemaphoreType.DMA((c,)) for c in job.sem_counts],
    )(*job.inputs)


def _tile(n, pref):
    if n % pref == 0:
        return pref
    best = None
    t = LANES
    while t <= min(n, pref):
        if n % t == 0:
            best = t
        t += LANES
    return best if best is not None else n


def _mm(a, b, *, am="mk", bm="kn", om="mn", out_dtype=F32, tm=1024, tn=1024, tk=1024, name, job=None):
    if am == "mk":
        m, k = a.shape
    elif am == "km":
        k, m = a.shape
    elif am == "bmk":
        m, tk = a.shape[1], a.shape[2]
        k = a.shape[0] * tk
    else:
        k, tm = a.shape[1], a.shape[2]
        m = a.shape[0] * tm
    if bm == "kn":
        kb_, n = b.shape
    elif bm == "nk":
        n, kb_ = b.shape
    elif bm == "bkn":
        kb_, tn = b.shape[1], b.shape[2]
        n = b.shape[0] * tn
    else:
        n, tk = b.shape[1], b.shape[2]
        kb_ = b.shape[0] * tk
    assert kb_ == k, (a.shape, b.shape, am, bm)
    tm, tn, tk = _tile(m, tm), _tile(n, tn), _tile(k, tk)
    nk = k // tk
    dims = (((0 if am in ("km", "bkm") else 1,), (1 if bm in ("nk", "bnk") else 0,)), ((), ()))

    a_spec = {"mk": pl.BlockSpec((tm, tk), lambda i, j, kk: (i, kk)),
              "km": pl.BlockSpec((tk, tm), lambda i, j, kk: (kk, i)),
              "bmk": pl.BlockSpec((None, tm, tk), lambda i, j, kk: (kk, i, 0)),
              "bkm": pl.BlockSpec((None, tk, tm), lambda i, j, kk: (i, kk, 0))}[am]
    b_spec = {"kn": pl.BlockSpec((tk, tn), lambda i, j, kk: (kk, j)),
              "nk": pl.BlockSpec((tn, tk), lambda i, j, kk: (j, kk)),
              "bkn": pl.BlockSpec((None, tk, tn), lambda i, j, kk: (j, kk, 0)),
              "bnk": pl.BlockSpec((None, tn, tk), lambda i, j, kk: (kk, j, 0))}[bm]
    if om == "mn":
        o_spec, o_shape = pl.BlockSpec((tm, tn), lambda i, j, kk: (i, j)), (m, n)
    else:
        o_spec, o_shape = pl.BlockSpec((None, tm, tn), lambda i, j, kk: (j, i, 0)), (n // tn, m, tn)

    def body(a_ref, b_ref, o_ref, *acc):
        kk = pl.program_id(2)
        prod = lax.dot_general(a_ref[...].astype(BF16), b_ref[...].astype(BF16), dims, preferred_element_type=F32)
        if nk == 1:
            o_ref[...] = prod.astype(o_ref.dtype)
            return
        acc_ref, = acc

        @pl.when(kk == 0)
        def _():
            acc_ref[...] = prod

        if nk > 2:
            @pl.when((kk > 0) & (kk < nk - 1))
            def _():
                acc_ref[...] += prod

        @pl.when(kk == nk - 1)
        def _():
            o_ref[...] = (acc_ref[...] + prod).astype(o_ref.dtype)

    return _call(body, (a, b), name=name, grid=(m // tm, n // tn, nk), in_specs=[a_spec, b_spec], out_specs=o_spec,
                 out_shape=jax.ShapeDtypeStruct(o_shape, out_dtype),
                 scratch_shapes=[pltpu.VMEM((tm, tn), F32)] if nk > 1 else [],
                 sem=("parallel", "parallel", "arbitrary"), job=job)


def _row_operand(a, tile):
    if isinstance(a, tuple):
        arr, w, j = a
        return arr, pl.BlockSpec((tile, w), lambda i, j=j: (i, j))
    return a, pl.BlockSpec((tile, a.shape[1]), lambda i: (i, 0))


def _const_spec(c):
    nd = c.ndim
    return pl.BlockSpec(c.shape, lambda i, nd=nd: (0,) * nd)


def _rowwise(fn, rows, consts, outs, *, name, accs=(), tile=None, job=None):
    t_rows = (rows[0][0] if isinstance(rows[0], tuple) else rows[0]).shape[0]
    tile = min(tile or ROW_TILE, t_rows)
    arrs, specs = zip(*[_row_operand(a, tile) for a in rows])
    nin, no = len(rows) + len(consts), len(outs)

    def body(*refs):
        res = fn(*[r[...] for r in refs[:nin]])
        for r, v in zip(refs[nin:nin + no], res[:no]):
            r[...] = v.astype(r.dtype)
        if accs:
            a_refs = refs[nin + no:]

            @pl.when(pl.program_id(0) == 0)
            def _():
                for r in a_refs:
                    r[...] = jnp.zeros_like(r)

            for r, v in zip(a_refs, res[no:]):
                r[...] += v

    out_shape = [jax.ShapeDtypeStruct((t_rows, w), dt) for w, dt in outs]
    out_shape += [jax.ShapeDtypeStruct(s, F32) for s in accs]
    out_specs = [pl.BlockSpec((tile, w), lambda i: (i, 0)) for w, _ in outs]
    out_specs += [pl.BlockSpec(s, lambda i, nd=len(s): (0,) * nd) for s in accs]
    return _call(body, (*arrs, *consts), name=name, grid=(t_rows // tile,),
                 in_specs=list(specs) + [_const_spec(c) for c in consts],
                 out_specs=out_specs, out_shape=out_shape, sem=("arbitrary",), job=job)


def _rowwise_vjp(fn, rows, consts, cts, *, name, groups, tile=None, gdtypes=None, job=None):
    t_rows = (rows[0][0] if isinstance(rows[0], tuple) else rows[0]).shape[0]
    tile = min(tile or ROW_TILE, t_rows)
    arrs, specs = zip(*[_row_operand(a, tile) for a in rows])
    flat_cts = [c for group in cts for c in group]
    ct_arrs, ct_specs = zip(*[_row_operand(a, tile) for a in flat_cts])
    nr, nc, nct, ng = len(rows), len(consts), len(flat_cts), len(groups)

    def width(a):
        return a[1] if isinstance(a, tuple) else a.shape[1]

    def body(*refs):
        rv = [r[...].astype(F32) for r in refs[:nr]]
        cv = [r[...] for r in refs[nr:nr + nc]]
        ct_refs = refs[nr + nc:nr + nc + nct]
        ctv, pos = [], 0
        for group in cts:
            s = ct_refs[pos][...].astype(F32)
            for r in ct_refs[pos + 1:pos + len(group)]:
                s = s + r[...].astype(F32)
            ctv.append(s)
            pos += len(group)
        _, pull = jax.vjp(fn, *rv, *cv)
        grads = pull(tuple(ctv))
        g_refs = refs[nr + nc + nct:nr + nc + nct + ng]
        for r, idx in zip(g_refs, groups):
            parts = [grads[i] for i in idx]
            r[...] = (parts[0] if len(parts) == 1 else jnp.concatenate(parts, axis=1)).astype(r.dtype)
        c_refs = refs[nr + nc + nct + ng:]

        @pl.when(pl.program_id(0) == 0)
        def _():
            for r in c_refs:
                r[...] = jnp.zeros_like(r)

        for r, v in zip(c_refs, grads[nr:]):
            r[...] += v

    gw = [sum(width(rows[i]) for i in idx) for idx in groups]
    gdtypes = gdtypes or [F32] * ng
    out_shape = [jax.ShapeDtypeStruct((t_rows, w), dt) for w, dt in zip(gw, gdtypes)]
    out_shape += [jax.ShapeDtypeStruct(c.shape, F32) for c in consts]
    out_specs = [pl.BlockSpec((tile, w), lambda i: (i, 0)) for w in gw]
    out_specs += [_const_spec(c) for c in consts]
    return _call(body, (*arrs, *consts, *ct_arrs), name=name, grid=(t_rows // tile,),
                 in_specs=list(specs) + [_const_spec(c) for c in consts] + list(ct_specs),
                 out_specs=out_specs, out_shape=out_shape, sem=("arbitrary",), job=job)


def _layer_norm(x, g, b):
    mu = jnp.mean(x, axis=-1, keepdims=True)
    xc = x - mu
    var = jnp.mean(xc * xc, axis=-1, keepdims=True)
    return xc * lax.rsqrt(var + LN_EPS) * g + b


def _sigmoid(x):
    return 1.0 / (1.0 + jnp.exp(-x))


def _fn_ln(x, g, b):
    return (_layer_norm(x, g, b),)


def _fn_rms(x, g):
    return (x * lax.rsqrt(jnp.mean(x * x, axis=-1, keepdims=True) + RMS_EPS) * g,)


def _make_post_mix(alpha):
    def fn(h, mix, g, b):
        return (_layer_norm(alpha * h + mix, g, b),)
    return fn


def _make_ple_ln(alpha):
    def fn(h1, ffn, pg, pp, g, b):
        return (_layer_norm(alpha * h1 + ffn + _sigmoid(pg) * pp, g, b),)
    return fn


def _fn_lower_bounds(l0, l1):
    m = jnp.maximum(l0, l1)
    e0, e1 = jnp.exp(l0 - m), jnp.exp(l1 - m)
    s = e0 + e1
    p0, p1 = e0 / s, e1 / s
    return (p0 - p0, (p0 + p1) - p0)


def _loss_and_grad(y, target, *, name):
    d = y.shape[1]

    def fn(yv, tv):
        err = yv - tv
        return err * (1.0 / d), 0.5 * jnp.sum(jnp.mean(err * err, axis=-1, keepdims=True), axis=0, keepdims=True)

    return _rowwise(fn, [y, target], [], [(d, F32)], accs=[(1, 1)], name=name)


def _split_dot(x, e_bf16):
    hi = x.astype(BF16)
    lo = (x - hi.astype(F32)).astype(BF16)
    return (jnp.dot(hi, e_bf16, preferred_element_type=F32) + jnp.dot(lo, e_bf16, preferred_element_type=F32))


def _hgrn_common(th):
    rm = lax.broadcasted_iota(jnp.int32, (th, HG_W), 0) % HG_CHUNK

    def seg_cumsum(x):
        for s in (1, 2, 4, 8):
            x = x + jnp.where(rm >= s, pltpu.roll(x, s, 0), 0.0)
        return x

    def seg_rcumsum(x):
        for s in (1, 2, 4, 8):
            x = x + jnp.where(rm < HG_CHUNK - s, pltpu.roll(x, th - s, 0), 0.0)
        return x

    ri = lax.broadcasted_iota(jnp.int32, (HG_W, HG_W), 0) // HEAD
    ci = lax.broadcasted_iota(jnp.int32, (HG_W, HG_W), 1) // HEAD
    head_f32 = (ri == ci).astype(F32)
    head_bf16 = head_f32.astype(BF16)

    def headsum(x):
        return _split_dot(x, head_bf16)

    return rm, seg_cumsum, seg_rcumsum, head_f32, headsum


def _hgrn_gates(qr, fl, lb):
    sg = _sigmoid(fl)
    f = lb + (1.0 - lb) * sg
    sq = _sigmoid(qr)
    return sg, f, jnp.log(f), 1.0 - f, qr * sq, sq


def _shifted(x, d, th):
    return x if d == 0 else pltpu.roll(x, d, 0)


def _unshift(x, d, th):
    return x if d == 0 else pltpu.roll(x, th - d, 0)


def _hgrn_fwd(projp, lb, ng, *, name, job=None):
    t_rows = projp.shape[0]
    th = min(HG_TILE, t_rows)
    nct = th // HG_CHUNK

    def body(q_ref, f_ref, i_ref, g_ref, lb_ref, ng_ref, oa_ref, opre_ref, st_out_ref,
             st_ref, vtm_ref, kv_ref, qe_ref, dec_ref, oint_ref):
        rm, seg_cumsum, seg_rcumsum, head_f32, headsum = _hgrn_common(th)

        @pl.when(pl.program_id(0) == 0)
        def _():
            st_ref[...] = jnp.zeros_like(st_ref)

        qr, fl, v, g = q_ref[...], f_ref[...], i_ref[...], g_ref[...]
        _, f, lf, k, q, _ = _hgrn_gates(qr, fl, lb_ref[...])
        b = seg_cumsum(lf)

        o = jnp.zeros((th, HG_W), F32)
        for d in range(HG_CHUNK):
            kd, bd, vd = _shifted(k, d, th), _shifted(b, d, th), _shifted(v, d, th)
            e = jnp.exp(jnp.where(rm >= d, b - bd, -1e30))
            o = o + headsum(q * kd * e) * vd

        blast = seg_rcumsum(jnp.where(rm == HG_CHUNK - 1, b, 0.0))
        kte = (k * jnp.exp(blast - b)).astype(BF16)
        qe_ref[...] = q * jnp.exp(b)
        dec_ref[...] = jnp.exp(blast)
        vt = v.T
        lane_chunk = lax.broadcasted_iota(jnp.int32, (HG_W, th), 1) // HG_CHUNK
        for c in range(nct):
            vtm_ref[c * HG_W:(c + 1) * HG_W, :] = jnp.where(lane_chunk == c, vt, 0.0).astype(BF16)
        kv_ref[...] = jnp.dot(vtm_ref[...], kte, preferred_element_type=F32)

        s = st_ref[...]
        for c in range(nct):
            rows = slice(c * HG_CHUNK, (c + 1) * HG_CHUNK)
            st_out_ref[c] = s
            oint_ref[rows, :] = lax.dot_general(qe_ref[rows, :].astype(BF16), s.astype(BF16),
                                                (((1,), (1,)), ((), ())), preferred_element_type=F32)
            dec = jnp.max(dec_ref[rows, :], axis=0, keepdims=True)
            s = s * dec + kv_ref[c * HG_W:(c + 1) * HG_W, :] * head_f32
        st_ref[...] = s

        o = o + oint_ref[...]
        opre_ref[...] = o
        r = lax.rsqrt(headsum(o * o) * (1.0 / HEAD) + RMS_EPS)
        oa_ref[...] = (o * r * ng_ref[...] * (g * _sigmoid(g))).astype(oa_ref.dtype)

    col = lambda j: pl.BlockSpec((th, HG_W), lambda i, j=j: (i, j))
    vec = pl.BlockSpec((1, HG_W), lambda i: (0, 0))
    row = pl.BlockSpec((th, HG_W), lambda i: (i, 0))
    n_chunks = t_rows // HG_CHUNK
    return _call(
        body, (projp, projp, projp, projp, lb, ng), name=name, grid=(t_rows // th,),
        in_specs=[col(0), col(1), col(2), col(3), vec, vec],
        out_specs=[row, row, pl.BlockSpec((nct, HG_W, HG_W), lambda i: (i, 0, 0))],
        out_shape=[jax.ShapeDtypeStruct((t_rows, HG_W), BF16), jax.ShapeDtypeStruct((t_rows, HG_W), F32),
                   jax.ShapeDtypeStruct((n_chunks, HG_W, HG_W), F32)],
        scratch_shapes=[pltpu.VMEM((HG_W, HG_W), F32), pltpu.VMEM((nct * HG_W, th), BF16),
                        pltpu.VMEM((nct * HG_W, HG_W), F32), pltpu.VMEM((th, HG_W), F32),
                        pltpu.VMEM((th, HG_W), F32), pltpu.VMEM((th, HG_W), F32)],
        sem=("arbitrary",), job=job)


def _hgrn_bwd(projp, lb, ng, opre, states, dcat, *, name):
    t_rows = projp.shape[0]
    th = min(HG_TILE, t_rows)
    nct = th // HG_CHUNK
    nt = t_rows // th

    def body(q_ref, f_ref, i_ref, g_ref, lb_ref, ng_ref, opre_ref, st_in_ref, do_ref,
             dproj_ref, dng_ref, dlb_ref,
             gst_ref, dotm_ref, qg_ref, v_ref, kte_ref, dop_ref, dec_ref, dkte_ref, dvi_ref, dqe_ref, ddec_ref):
        rm, seg_cumsum, seg_rcumsum, head_f32, headsum = _hgrn_common(th)

        @pl.when(pl.program_id(0) == 0)
        def _():
            gst_ref[...] = jnp.zeros_like(gst_ref)
            dng_ref[...] = jnp.zeros_like(dng_ref)
            dlb_ref[...] = jnp.zeros_like(dlb_ref)

        qr, fl, v, g = q_ref[...], f_ref[...], i_ref[...], g_ref[...]
        lb, ngv = lb_ref[...], ng_ref[...]
        sg, f, lf, k, q, sq = _hgrn_gates(qr, fl, lb)
        b = seg_cumsum(lf)
        blast = seg_rcumsum(jnp.where(rm == HG_CHUNK - 1, b, 0.0))
        eb = jnp.exp(b)
        ekb = jnp.exp(blast - b)
        qe, kte, dec = q * eb, k * ekb, jnp.exp(blast)

        do_out, op = do_ref[...], opre_ref[...]
        sgg = _sigmoid(g)
        sil = g * sgg
        r = lax.rsqrt(headsum(op * op) * (1.0 / HEAD) + RMS_EPS)
        on = op * r
        dng_ref[...] += jnp.sum(do_out * on * sil, axis=0, keepdims=True)
        dg = do_out * on * ngv * (sgg * (1.0 + g * (1.0 - sgg)))
        don = do_out * ngv * sil
        dop = r * (don - on * (headsum(don * on) * (1.0 / HEAD)))

        v_ref[...] = v
        kte_ref[...] = kte
        dop_ref[...] = dop
        dec_ref[...] = dec
        dot_t = dop.T
        lane_chunk = lax.broadcasted_iota(jnp.int32, (HG_W, th), 1) // HG_CHUNK
        for c in range(nct):
            dotm_ref[c * HG_W:(c + 1) * HG_W, :] = jnp.where(lane_chunk == c, dot_t, 0.0).astype(BF16)
        qg_ref[...] = jnp.dot(dotm_ref[...], qe.astype(BF16), preferred_element_type=F32)

        gs = gst_ref[...]
        for c in reversed(range(nct)):
            rows = slice(c * HG_CHUNK, (c + 1) * HG_CHUNK)
            s = st_in_ref[c]
            gm = (gs * head_f32).astype(BF16)
            dkte_ref[rows, :] = jnp.dot(v_ref[rows, :].astype(BF16), gm, preferred_element_type=F32)
            dvi_ref[rows, :] = lax.dot_general(kte_ref[rows, :].astype(BF16), gm, (((1,), (1,)), ((), ())),
                                               preferred_element_type=F32)
            dqe_ref[rows, :] = jnp.dot(dop_ref[rows, :].astype(BF16), s.astype(BF16), preferred_element_type=F32)
            ddec_ref[rows, :] = jnp.broadcast_to(jnp.sum(gs * s, axis=0, keepdims=True), (HG_CHUNK, HG_W))
            dec_c = jnp.max(dec_ref[rows, :], axis=0, keepdims=True)
            gs = gs * dec_c + qg_ref[c * HG_W:(c + 1) * HG_W, :] * head_f32
        gst_ref[...] = gs

        dkte, dqe = dkte_ref[...], dqe_ref[...]
        dq = dqe * eb
        dk = dkte * ekb
        db = dqe * qe - dkte * kte
        dv = dvi_ref[...]
        dblast = dkte * kte + jnp.where(rm == HG_CHUNK - 1, ddec_ref[...] * dec, 0.0)

        for d in range(HG_CHUNK):
            kd, bd, vd = _shifted(k, d, th), _shifted(b, d, th), _shifted(v, d, th)
            e = jnp.exp(jnp.where(rm >= d, b - bd, -1e30))
            p = q * kd * e
            sc = headsum(p)
            dsc = headsum(dop * vd)
            dv = dv + _unshift(sc * dop, d, th)
            dq = dq + dsc * kd * e
            dk = dk + _unshift(dsc * q * e, d, th)
            darg = dsc * p
            db = db + darg - _unshift(darg, d, th)

        db = db + jnp.where(rm == HG_CHUNK - 1, seg_cumsum(dblast), 0.0)
        dlf = seg_rcumsum(db)
        df = dlf / f - dk
        dlb_ref[...] += jnp.sum(df * (1.0 - sg), axis=0, keepdims=True)
        dfl = df * (1.0 - lb) * sg * (1.0 - sg)
        dqr = dq * (sq * (1.0 + qr * (1.0 - sq)))
        dproj_ref[...] = jnp.concatenate([dqr, dfl, dv, dg], axis=1).astype(dproj_ref.dtype)

    rev = lambda i: nt - 1 - i
    col = lambda j: pl.BlockSpec((th, HG_W), lambda i, j=j: (rev(i), j))
    vec = pl.BlockSpec((1, HG_W), lambda i: (0, 0))
    row = pl.BlockSpec((th, HG_W), lambda i: (rev(i), 0))
    tile_f32 = pltpu.VMEM((th, HG_W), F32)
    return pl.pallas_call(
        body, name=name, grid=(nt,),
        in_specs=[col(0), col(1), col(2), col(3), vec, vec, row,
                  pl.BlockSpec((nct, HG_W, HG_W), lambda i: (rev(i), 0, 0)), col(0)],
        out_specs=[pl.BlockSpec((th, 4 * HG_W), lambda i: (rev(i), 0)), vec, vec],
        out_shape=[jax.ShapeDtypeStruct((t_rows, 4 * HG_W), BF16), jax.ShapeDtypeStruct((1, HG_W), F32),
                   jax.ShapeDtypeStruct((1, HG_W), F32)],
        scratch_shapes=[pltpu.VMEM((HG_W, HG_W), F32), pltpu.VMEM((nct * HG_W, th), BF16),
                        pltpu.VMEM((nct * HG_W, HG_W), F32)] + [tile_f32] * 8,
        compiler_params=_cparams(("arbitrary",)),
    )(projp, projp, projp, projp, lb, ng, opre, states, dcat)


_INV_SQRT2 = 1.0 / math.sqrt(2.0)
_INV_SQRT2PI = 1.0 / math.sqrt(2.0 * math.pi)


def _gelu(x):
    return 0.5 * x * (1.0 + lax.erf(x * _INV_SQRT2))


def _gelu_grad(x):
    return 0.5 * (1.0 + lax.erf(x * _INV_SQRT2)) + x * jnp.exp(-0.5 * x * x) * _INV_SQRT2PI


def _sgu_parts(bu, bv, lg, lbias, w_ref, n_groups):
    c = SGU_CHUNK
    tril = (lax.broadcasted_iota(jnp.int32, (c, c), 0) >= lax.broadcasted_iota(jnp.int32, (c, c), 1)).astype(F32)
    gid = lax.broadcasted_iota(jnp.int32, bu.shape, 1) // HEAD
    u = _gelu(bu)
    gv = _gelu(bv)
    mu = jnp.mean(gv, axis=-1, keepdims=True)
    xc = gv - mu
    rstd = lax.rsqrt(jnp.mean(xc * xc, axis=-1, keepdims=True) + LN_EPS)
    xhat = xc * rstd
    vn = xhat * lg + lbias
    ws = [w_ref[gi] * tril for gi in range(n_groups)]
    return tril, gid, u, rstd, xhat, vn, ws


def _sgu_fwd(projp, lg, lbias, w_s, bias_full, *, name):
    t_rows = projp.shape[0]
    n_groups = w_s.shape[0]
    c = SGU_CHUNK

    def body(u_ref, v_ref, lg_ref, lb_ref, w_ref, bias_ref, o_ref):
        _, gid, u, _, _, vn, ws = _sgu_parts(u_ref[...], v_ref[...], lg_ref[...], lb_ref[...], w_ref, n_groups)
        vnb = vn.astype(BF16)
        z = bias_ref[...]
        for gi in range(n_groups):
            z = z + jnp.where(gid == gi, jnp.dot(ws[gi].astype(BF16), vnb, preferred_element_type=F32), 0.0)
        o_ref[...] = (u * z).astype(o_ref.dtype)

    col = lambda j: pl.BlockSpec((c, HG_W), lambda i, j=j: (i, j))
    return pl.pallas_call(
        body, name=name, grid=(t_rows // c,),
        in_specs=[col(4), col(5), _const_spec(lg), _const_spec(lbias), _const_spec(w_s), _const_spec(bias_full)],
        out_specs=pl.BlockSpec((c, HG_W), lambda i: (i, 0)),
        out_shape=jax.ShapeDtypeStruct((t_rows, HG_W), BF16),
        compiler_params=_cparams(("arbitrary",)),
    )(projp, projp, lg, lbias, w_s, bias_full)


def _sgu_bwd(projp, lg, lbias, w_s, bias_full, dcat, *, name):
    t_rows = projp.shape[0]
    n_groups = w_s.shape[0]
    c = SGU_CHUNK
    n = t_rows // c

    def body(u_ref, v_ref, lg_ref, lb_ref, w_ref, bias_ref, do_ref,
             dproj_ref, dlg_ref, dlb_ref, dw_ref, dbs_ref, dbias_acc):
        i = pl.program_id(0)

        @pl.when(i == 0)
        def _():
            dlg_ref[...] = jnp.zeros_like(dlg_ref)
            dlb_ref[...] = jnp.zeros_like(dlb_ref)
            dw_ref[...] = jnp.zeros_like(dw_ref)
            dbias_acc[...] = jnp.zeros_like(dbias_acc)

        bu, bv, lg_v = u_ref[...], v_ref[...], lg_ref[...]
        tril, gid, u, rstd, xhat, vn, ws = _sgu_parts(bu, bv, lg_v, lb_ref[...], w_ref, n_groups)
        vnb = vn.astype(BF16)
        z = bias_ref[...]
        for gi in range(n_groups):
            z = z + jnp.where(gid == gi, jnp.dot(ws[gi].astype(BF16), vnb, preferred_element_type=F32), 0.0)
        do = do_ref[...]
        dbu = do * z * _gelu_grad(bu)
        dz = do * u
        dbias_acc[...] += dz
        dvn = jnp.zeros_like(dz)
        for gi in range(n_groups):
            dzg = jnp.where(gid == gi, dz, 0.0).astype(BF16)
            dw_ref[gi] += lax.dot_general(dzg, vnb, (((1,), (1,)), ((), ())), preferred_element_type=F32) * tril
            dvn = dvn + jnp.dot(ws[gi].T.astype(BF16), dzg, preferred_element_type=F32)
        dlg_ref[...] += jnp.sum(dvn * xhat, axis=0, keepdims=True)
        dlb_ref[...] += jnp.sum(dvn, axis=0, keepdims=True)
        dxh = dvn * lg_v
        dgv = rstd * (dxh - jnp.mean(dxh, axis=-1, keepdims=True)
                      - xhat * jnp.mean(dxh * xhat, axis=-1, keepdims=True))
        dproj_ref[...] = jnp.concatenate([dbu, dgv * _gelu_grad(bv)], axis=1).astype(dproj_ref.dtype)

        @pl.when(i == n - 1)
        def _():
            dbs_ref[...] = jnp.sum(dbias_acc[...].T.reshape(n_groups, HEAD, c), axis=1)

    col = lambda j: pl.BlockSpec((c, HG_W), lambda i, j=j: (i, j))
    return pl.pallas_call(
        body, name=name, grid=(n,),
        in_specs=[col(4), col(5), _const_spec(lg), _const_spec(lbias), _const_spec(w_s), _const_spec(bias_full),
                  col(1)],
        out_specs=[pl.BlockSpec((c, 2 * HG_W), lambda i: (i, 0)), _const_spec(lg), _const_spec(lbias),
                   _const_spec(w_s), pl.BlockSpec((n_groups, c), lambda i: (0, 0))],
        out_shape=[jax.ShapeDtypeStruct((t_rows, 2 * HG_W), BF16), jax.ShapeDtypeStruct(lg.shape, F32),
                   jax.ShapeDtypeStruct(lbias.shape, F32), jax.ShapeDtypeStruct(w_s.shape, F32),
                   jax.ShapeDtypeStruct((n_groups, c), F32)],
        scratch_shapes=[pltpu.VMEM((c, HG_W), F32)],
        compiler_params=_cparams(("arbitrary",)),
    )(projp, projp, lg, lbias, w_s, bias_full, dcat)


def _rope_tables(positions):
    t = positions.shape[0]
    inv_freq = ROPE_THETA ** (-jnp.arange(0, 32, 2, dtype=F32) / 32)
    ang = positions.astype(F32)[:, None] * inv_freq
    cos, sin = jnp.cos(ang), jnp.sin(ang)
    z = lambda w: jnp.zeros((t, w), F32)
    cos_t = jnp.concatenate([jnp.ones((t, 64), F32), cos, cos, z(32)], axis=1)
    sin_up = jnp.concatenate([z(80), sin, z(32)], axis=1)
    sin_dn = jnp.concatenate([z(64), -sin, z(48)], axis=1)
    return cos_t, sin_up, sin_dn


def _rep(x, n):
    return x if n == 1 else jnp.concatenate([x] * n, axis=1)


def _rope(x, cos_t, sin_up, sin_dn):
    w = x.shape[1]
    return x * cos_t + pltpu.roll(x, 16, 1) * sin_up + pltpu.roll(x, w - 16, 1) * sin_dn


def _rope_t(dy, cos_t, sin_up, sin_dn):
    w = dy.shape[1]
    return dy * cos_t + pltpu.roll(dy * sin_up, w - 16, 1) + pltpu.roll(dy * sin_dn, 16, 1)


def _mla_prep(q, kv, projp, tables, *, name):
    nh = N_ATT_HEADS

    def fn(qv, kvv, kr, cos_t, sin_up, sin_dn):
        qr = _rope(qv, _rep(cos_t, nh), _rep(sin_up, nh), _rep(sin_dn, nh))
        krr = _rope(kr, cos_t, sin_up, sin_dn)
        lane = lax.broadcasted_iota(jnp.int32, kvv.shape, 1) % LANES
        return qr, jnp.where(lane < HEAD, kvv, 0.0) + _rep(krr, nh), kvv

    w = q.shape[1]
    return _rowwise(fn, [q, kv, (projp, LANES, P_KR // LANES)] + list(tables), [],
                    [(w, BF16), (w, BF16), (w, BF16)], name=name)


def _mla_prep_bwd(dqr, dkf, tables, *, name):
    nh = N_ATT_HEADS

    def fn(dq, dk, cos_t, sin_up, sin_dn):
        dqp = _rope_t(dq, _rep(cos_t, nh), _rep(sin_up, nh), _rep(sin_dn, nh))
        dkrr = dk[:, 0:LANES]
        for h in range(1, nh):
            dkrr = dkrr + dk[:, LANES * h:LANES * (h + 1)]
        return dqp, _rope_t(dkrr, cos_t, sin_up, sin_dn)

    return _rowwise(fn, [dqr, dkf] + list(tables), [], [(dqr.shape[1], BF16), (LANES, BF16)], name=name)


_LOG2E = 1.0 / math.log(2.0)
_NT = (((1,), (1,)), ((), ()))
_TN = (((0,), (0,)), ((), ()))


def _attn_fwd(qr, kf, kvb, *, name, job=None):
    t_rows = qr.shape[0]
    tq = min(ATT_TQ, t_rows)
    nb = t_rows // tq
    scale = ATT_D ** -0.5

    c2 = scale * _LOG2E

    def body(q_ref, kf_ref, kvb_ref, o_ref, lse_ref):
        qi = pl.program_id(1)
        lane = lax.broadcasted_iota(jnp.int32, (tq, LANES), 1)
        causal_t = (lax.broadcasted_iota(jnp.int32, (tq, tq), 0) <= lax.broadcasted_iota(jnp.int32, (tq, tq), 1))
        heads = [slice(hh * LANES, (hh + 1) * LANES) for hh in range(2)]
        qs = [q_ref[:, cols] for cols in heads]

        def scores(ki):
            rows = pl.ds(pl.multiple_of(ki * tq, tq), tq)
            return tuple(lax.dot_general(kf_ref[rows, cols], q, _NT, preferred_element_type=F32)
                         for q, cols in zip(qs, heads))

        def fold(op, red, x):
            a = x.reshape(4, tq // 4, tq)
            return red(op(op(a[0], a[1]), op(a[2], a[3])), axis=0, keepdims=True)

        def block(ki, s_pair, carry, diagonal):
            rows = pl.ds(pl.multiple_of(ki * tq, tq), tq)
            new = []
            for s_t, cols, (m_old, l_old, acc_t) in zip(s_pair, heads, carry):
                if diagonal:
                    s_t = jnp.where(causal_t, s_t, -1e30)
                m_new = jnp.maximum(m_old, fold(jnp.maximum, jnp.max, s_t))
                p_t = jnp.exp2((s_t - m_new) * c2)
                a = jnp.exp2((m_old - m_new) * c2)
                pv_t = lax.dot_general(kvb_ref[rows, cols], p_t.astype(BF16), _TN, preferred_element_type=F32)
                new.append((m_new, a * l_old + fold(jnp.add, jnp.sum, p_t), a * acc_t + pv_t))
            return tuple(new)

        def step(ki, state):
            s_pair, carry = state
            s_next = scores(ki + 1)
            return s_next, block(ki, s_pair, carry, False)

        init = (jnp.full((1, tq), -1e30, F32), jnp.zeros((1, tq), F32), jnp.zeros((LANES, tq), F32))
        s_pair, carry = lax.fori_loop(0, qi, step, (scores(0), (init, init)))
        outs = []
        for hh, (m_fin, l_fin, acc_t) in enumerate(block(qi, s_pair, carry, True)):
            lse_ref[hh] = m_fin * scale + jnp.log(l_fin)
            outs.append((acc_t / l_fin).T)
        o_ref[...] = jnp.where(lane < HEAD, pltpu.roll(outs[0], HEAD, 1), outs[1])

    pair = pl.BlockSpec((t_rows, 2 * LANES), lambda pr, qi: (0, pr))
    return _call(
        body, (qr, kf, kvb), name=name, grid=(N_ATT_HEADS // 2, nb),
        in_specs=[pl.BlockSpec((tq, 2 * LANES), lambda pr, qi: (qi, pr)), pair, pair],
        out_specs=[pl.BlockSpec((tq, LANES), lambda pr, qi: (qi, pr)),
                   pl.BlockSpec((2, 1, tq), lambda pr, qi: (pr, 0, qi))],
        out_shape=[jax.ShapeDtypeStruct((t_rows, N_ATT_HEADS * HEAD), F32),
                   jax.ShapeDtypeStruct((N_ATT_HEADS, 1, t_rows), F32)],
        sem=("parallel", "arbitrary"), job=job)


def _attn_bwd(qr, kf, kvb, dcat, o, lse, *, name, job=None):
    t_rows = qr.shape[0]
    tq = min(ATT_TQ, t_rows)
    nb = t_rows // tq
    scale = ATT_D ** -0.5
    c2 = scale * _LOG2E
    do_off = 2 * HG_W // LANES

    def body(q_ref, kf_ref, kvb_ref, do_ref, o_ref, lse_ref, dq_ref, dkv_ref, dk_ref):
        ki = pl.program_id(1)

        @pl.when(ki == 0)
        def _():
            dq_ref[...] = jnp.zeros_like(dq_ref)

        lane = lax.broadcasted_iota(jnp.int32, (tq, LANES), 1)
        causal_t = (lax.broadcasted_iota(jnp.int32, (tq, tq), 0) <= lax.broadcasted_iota(jnp.int32, (tq, tq), 1))
        heads = [slice(hh * LANES, (hh + 1) * LANES) for hh in range(2)]
        ks = [kf_ref[:, cols] for cols in heads]
        vs = [kvb_ref[:, cols] for cols in heads]

        def block(qi, carry, diagonal):
            rows = pl.ds(pl.multiple_of(qi * tq, tq), tq)
            do_pair, o_pair = do_ref[rows, :], o_ref[rows, :]
            new = []
            for hh, (cols, k, v, (dk, dv)) in enumerate(zip(heads, ks, vs, carry)):
                q = q_ref[rows, cols]
                do, ov = (pltpu.roll(do_pair, HEAD, 1), pltpu.roll(o_pair, HEAD, 1)) if hh == 0 else (do_pair, o_pair)
                do = jnp.where(lane >= HEAD, do, 0.0)
                delta = jnp.sum((do * ov).T, axis=0, keepdims=True)
                s_t = lax.dot_general(k, q, _NT, preferred_element_type=F32)
                if diagonal:
                    s_t = jnp.where(causal_t, s_t, -1e30)
                p_t = jnp.exp2(s_t * c2 - lse_ref[hh, :, rows] * _LOG2E)
                dob = do.astype(BF16)
                dv = dv + jnp.dot(p_t.astype(BF16), dob, preferred_element_type=F32)
                dp_t = lax.dot_general(v, dob, _NT, preferred_element_type=F32)
                ds_t = (p_t * (dp_t - delta) * scale).astype(BF16)
                dk = dk + jnp.dot(ds_t, q, preferred_element_type=F32)
                dq_ref[rows, cols] += lax.dot_general(ds_t, k, _TN, preferred_element_type=F32)
                new.append((dk, dv))
            return tuple(new)

        zero = jnp.zeros((tq, LANES), F32)
        carry = block(ki, ((zero, zero), (zero, zero)), True)
        carry = lax.fori_loop(ki + 1, nb, lambda qi, c: block(qi, c, False), carry)
        dkv_ref[...] = jnp.concatenate([jnp.where(lane < HEAD, dk, dv) for dk, dv in carry],
                                       axis=1).astype(dkv_ref.dtype)
        dk_ref[...] = jnp.concatenate([dk for dk, _ in carry], axis=1)

    pair_all = pl.BlockSpec((t_rows, 2 * LANES), lambda pr, ki: (0, pr))
    pair_blk = pl.BlockSpec((tq, 2 * LANES), lambda pr, ki: (ki, pr))
    wide = jax.ShapeDtypeStruct((t_rows, N_ATT_HEADS * LANES), F32)
    return _call(
        body, (qr, kf, kvb, dcat, o, lse), name=name, grid=(N_ATT_HEADS // 2, nb),
        in_specs=[pair_all, pair_blk, pair_blk,
                  pl.BlockSpec((t_rows, LANES), lambda pr, ki: (0, do_off + pr)),
                  pl.BlockSpec((t_rows, LANES), lambda pr, ki: (0, pr)),
                  pl.BlockSpec((2, 1, t_rows), lambda pr, ki: (pr, 0, 0))],
        out_specs=[pair_all, pair_blk, pair_blk],
        out_shape=[wide, jax.ShapeDtypeStruct(wide.shape, BF16), wide],
        sem=("parallel", "arbitrary"), job=job)


def _my_pos():
    return lax.axis_index("x"), lax.axis_index("y"), lax.axis_index("c")


def _all_gather(xs, *, name):
    return _gather_forward(_run_job(_gather_job(xs), name=name), name=name + "_forward")


def _remote(src, dst, send_sems, recv_sems, k, dev):
    return pltpu.make_async_remote_copy(src_ref=src, dst_ref=dst, send_sem=send_sems.at[k], recv_sem=recv_sems.at[k],
                                        device_id=dev, device_id_type=MESH)


def _gather_job(xs):
    n = len(xs)

    def make(x_refs, out_refs, send_sems, recv_sems, local_sems):
        mx, my, mc = _my_pos()
        mine = 4 * mx + 2 * my + mc
        peers = [(mx, my, 1 - mc), (1 - mx, my, mc), (mx, 1 - my, mc), (1 - mx, 1 - my, mc)]
        sends, recvs, local = [], [], []
        for a in range(n):
            local.append(pltpu.make_async_copy(x_refs[a], out_refs[a].at[mine], local_sems.at[a]))
            for k, dev in enumerate(peers):
                theirs = 4 * dev[0] + 2 * dev[1] + dev[2]
                sends.append(_remote(x_refs[a], out_refs[a].at[mine], send_sems, recv_sems, 4 * a + k, dev))
                recvs.append(_remote(x_refs[a], out_refs[a].at[theirs], send_sems, recv_sems, 4 * a + k, dev))
        return sends, recvs, local

    shapes = [jax.ShapeDtypeStruct((N_DEV,) + x.shape, x.dtype) for x in xs]
    return _copies_job(xs, shapes, 4 * n, n, make)


def _gather_forward(gs, *, name):
    n = len(gs)

    def body(*refs):
        out_refs = refs[n:2 * n]
        send_sems, recv_sems = refs[2 * n:]
        mx, my, mc = _my_pos()
        chips = [(1 - mx, my), (mx, 1 - my), (1 - mx, 1 - my)]
        sends, recvs = [], []
        for a in range(n):
            for j, (cx, cy) in enumerate(chips):
                here, there = out_refs[a].at[4 * cx + 2 * cy + mc], out_refs[a].at[4 * cx + 2 * cy + 1 - mc]
                sends.append(_remote(here, here, send_sems, recv_sems, 3 * a + j, (mx, my, 1 - mc)))
                recvs.append(_remote(here, there, send_sems, recv_sems, 3 * a + j, (mx, my, 1 - mc)))
        for cp in sends:
            cp.start()
        for cp in recvs:
            cp.wait_recv()
        for cp in sends:
            cp.wait_send()

    return pl.pallas_call(
        body, name=name, out_shape=[jax.ShapeDtypeStruct(g.shape, g.dtype) for g in gs],
        in_specs=[_ANY] * n, out_specs=[_ANY] * n, input_output_aliases={a: a for a in range(n)},
        scratch_shapes=[pltpu.SemaphoreType.DMA((3 * n,)), pltpu.SemaphoreType.DMA((3 * n,))],
    )(*gs)


def _pair_job(xs):
    n = len(xs)

    def make(x_refs, out_refs, send_sems, recv_sems, local_sems):
        mx, my, mc = _my_pos()
        copies = [_remote(x_refs[a].at[g, 1 - mc], out_refs[a].at[g], send_sems, recv_sems, 4 * a + g, (mx, my, 1 - mc))
                  for a in range(n) for g in range(4)]
        return copies, copies, []

    shapes = [jax.ShapeDtypeStruct((4,) + x.shape[2:], x.dtype) for x in xs]
    return _copies_job(xs, shapes, 4 * n, 0, make)


def _pair_add(x, r, core, *, name):
    _, _, a, b = x.shape
    ta = _row_tile(a, 256)

    def body(c_ref, x_ref, r_ref, o_ref):
        o_ref[...] = (x_ref[...] + r_ref[...]).astype(o_ref.dtype)

    blk = pl.BlockSpec((None, ta, b), lambda g, i, c_ref: (g, i, 0))
    return pl.pallas_call(
        body, name=name,
        grid_spec=pltpu.PrefetchScalarGridSpec(
            num_scalar_prefetch=1, grid=(4, a // ta),
            in_specs=[pl.BlockSpec((None, None, ta, b), lambda g, i, c_ref: (g, c_ref[0], i, 0)), blk],
            out_specs=blk),
        out_shape=jax.ShapeDtypeStruct((4, a, b), BF16),
        compiler_params=_cparams(("parallel", "parallel")),
    )(core, x, r)


def _quad_job(xs):
    n = len(xs)

    def make(x_refs, out_refs, send_sems, recv_sems, local_sems):
        mx, my, mc = _my_pos()
        mine = 2 * mx + my
        peers = [((1 - mx, my, mc), 2 * (1 - mx) + my), ((mx, 1 - my, mc), 2 * mx + 1 - my),
                 ((1 - mx, 1 - my, mc), 2 * (1 - mx) + 1 - my)]
        sends, recvs, local = [], [], []
        for a in range(n):
            local.append(pltpu.make_async_copy(x_refs[a].at[mine], out_refs[a].at[mine], local_sems.at[a]))
            for k, (dev, g) in enumerate(peers):
                sends.append(_remote(x_refs[a].at[g], out_refs[a].at[mine], send_sems, recv_sems, 3 * a + k, dev))
                recvs.append(_remote(x_refs[a].at[g], out_refs[a].at[g], send_sems, recv_sems, 3 * a + k, dev))
        return sends, recvs, local

    shapes = [jax.ShapeDtypeStruct(x.shape, x.dtype) for x in xs]
    return _copies_job(xs, shapes, 3 * n, n, make)


def _row_tile(r, pref):
    t = min(pref, r)
    while r % t or (t % 8 and t != r):
        t -= 1
    return t


def _adamw(parts, w, m, v, layer, *, name, tile=256):
    g, a, b = parts.shape
    tile = _row_tile(a, tile)
    c1 = 1.0 / (1.0 - ADAM_B1 ** ADAM_STEP)
    c2 = 1.0 / (1.0 - ADAM_B2 ** ADAM_STEP)

    def body(p_ref, w_ref, m_ref, v_ref, g_ref, d_ref, mo_ref, vo_ref):
        grad = p_ref[0].astype(F32)
        for j in range(1, g):
            grad = grad + p_ref[j].astype(F32)
        mn = ADAM_B1 * m_ref[...] + (1.0 - ADAM_B1) * grad
        vn = ADAM_B2 * v_ref[...] + (1.0 - ADAM_B2) * (grad * grad)
        g_ref[...] = grad
        mo_ref[...] = mn
        vo_ref[...] = vn
        d_ref[...] = -ADAM_LR * ((mn * c1) / (jnp.sqrt(vn * c2) + ADAM_EPS) + ADAM_WD * w_ref[...])

    slab = pl.BlockSpec((tile, b), lambda i: (i, 0))
    src = slab if layer is None else pl.BlockSpec((None, tile, b), lambda i: (layer, i, 0))
    return pl.pallas_call(
        body, name=name, grid=(a // tile,),
        in_specs=[pl.BlockSpec((g, tile, b), lambda i: (0, i, 0)), src, src, src],
        out_specs=[slab] * 4,
        out_shape=[jax.ShapeDtypeStruct((a, b), F32)] * 4,
        compiler_params=_cparams(("parallel",)),
    )(parts, w, m, v)


W_IN_SHARD = 276


def _w_in_dest(col):
    return jnp.where(col < P_KR, col, jnp.where(col < P_KR + 256, col + (P_CKV - P_KR), col - 2176 + P_KR + HEAD))


def _place_w_in(g, *, name):
    _, d, sh = g.shape
    tc = 768

    def body(g_ref, o_ref, acc_ref):
        ct, j = pl.program_id(0), pl.program_id(1)

        @pl.when(j == 0)
        def _():
            acc_ref[...] = jnp.zeros_like(acc_ref)

        src = j * sh + lax.broadcasted_iota(jnp.int32, (sh, tc), 0)
        dst = ct * tc + lax.broadcasted_iota(jnp.int32, (sh, tc), 1)
        place = (_w_in_dest(src) == dst).astype(BF16)
        acc_ref[...] += jnp.dot(g_ref[...], place, preferred_element_type=F32)

        @pl.when(j == N_DEV - 1)
        def _():
            o_ref[...] = acc_ref[...].astype(o_ref.dtype)

    return pl.pallas_call(
        body, name=name, grid=(P_COLS // tc, N_DEV),
        in_specs=[pl.BlockSpec((None, d, sh), lambda ct, j: (j, 0, 0))],
        out_specs=pl.BlockSpec((d, tc), lambda ct, j: (0, ct)),
        out_shape=jax.ShapeDtypeStruct((d, P_COLS), BF16),
        scratch_shapes=[pltpu.VMEM((d, tc), F32)],
        compiler_params=_cparams(("parallel", "arbitrary")),
    )(g)


def _unplace_w_in(dw, *, name):
    d = dw.shape[0]
    sh = W_IN_SHARD

    def body(dw_ref, o_ref):
        j = pl.program_id(0)
        src = j * sh + lax.broadcasted_iota(jnp.int32, (P_COLS, sh), 1)
        dst = lax.broadcasted_iota(jnp.int32, (P_COLS, sh), 0)
        pick = (_w_in_dest(src) == dst).astype(BF16)
        x = dw_ref[...]
        o_ref[...] = _split_dot(x, pick)

    return pl.pallas_call(
        body, name=name, grid=(N_DEV,),
        in_specs=[pl.BlockSpec((d, P_COLS), lambda j: (0, 0))],
        out_specs=pl.BlockSpec((None, d, sh), lambda j: (j, 0, 0)),
        out_shape=jax.ShapeDtypeStruct((N_DEV, d, sh), F32),
        compiler_params=_cparams(("arbitrary",)),
    )(dw)


def _gate_up_swiglu(h1, wgu, *, name):
    t_rows, k = h1.shape
    w = wgu.shape[2]
    tm = _tile(t_rows, 1024)

    def body(a_ref, wg_ref, wu_ref, gu_ref, act_ref):
        a = a_ref[...].astype(BF16)
        gate = jnp.dot(a, wg_ref[...], preferred_element_type=F32)
        up = jnp.dot(a, wu_ref[...], preferred_element_type=F32)
        gu_ref[0] = gate.astype(gu_ref.dtype)
        gu_ref[1] = up.astype(gu_ref.dtype)
        act_ref[...] = (gate * _sigmoid(gate) * up).astype(act_ref.dtype)

    return pl.pallas_call(
        body, name=name, grid=(t_rows // tm, 4),
        in_specs=[pl.BlockSpec((tm, k), lambda i, j: (i, 0)),
                  pl.BlockSpec((None, k, w), lambda i, j: (j, 0, 0)),
                  pl.BlockSpec((None, k, w), lambda i, j: (j + 4, 0, 0))],
        out_specs=[pl.BlockSpec((2, None, tm, w), lambda i, j: (0, j, i, 0)),
                   pl.BlockSpec((None, tm, w), lambda i, j: (j, i, 0))],
        out_shape=[jax.ShapeDtypeStruct((2, 4, t_rows, w), BF16), jax.ShapeDtypeStruct((4, t_rows, w), BF16)],
        compiler_params=_cparams(("parallel", "arbitrary")),
    )(h1, wgu, wgu)


def _down_dx_swiglu(dffn, wdown, gu, *, name):
    t_rows, k = dffn.shape
    w = gu.shape[3]
    tm = _tile(t_rows, 1024)

    def body(d_ref, w_ref, gu_ref, o_ref):
        dact = lax.dot_general(d_ref[...].astype(BF16), w_ref[...], _NT, preferred_element_type=F32)
        gate, up = gu_ref[0].astype(F32), gu_ref[1].astype(F32)
        sg = _sigmoid(gate)
        o_ref[0] = (dact * up * (sg * (1.0 + gate * (1.0 - sg)))).astype(o_ref.dtype)
        o_ref[1] = (dact * gate * sg).astype(o_ref.dtype)

    blk = pl.BlockSpec((2, None, tm, w), lambda i, j: (0, j, i, 0))
    return pl.pallas_call(
        body, name=name, grid=(t_rows // tm, 4),
        in_specs=[pl.BlockSpec((tm, k), lambda i, j: (i, 0)), pl.BlockSpec((w, k), lambda i, j: (j, 0)), blk],
        out_specs=blk, out_shape=jax.ShapeDtypeStruct(gu.shape, BF16),
        compiler_params=_cparams(("parallel", "arbitrary")),
    )(dffn, wdown, gu)


BIG = ("w_in", "mla_w_uq", "mla_w_ukv", "w_out", "w_gate_up", "w_down", "ple_w_gate", "ple_w_proj")
SMALL = ("ln_in_g", "ln_in_b", "hgrn_lb_logits", "hgrn_norm_g", "sgu_ln_g", "sgu_ln_b", "sgu_w_s", "sgu_b_s",
         "mla_q_norm_g", "mla_kv_norm_g", "ln1_g", "ln1_b", "ln2_g", "ln2_b")
ORDER = ("ln_in_g", "ln_in_b", "w_in", "hgrn_lb_logits", "hgrn_norm_g", "sgu_ln_g", "sgu_ln_b", "sgu_w_s", "sgu_b_s",
         "mla_q_norm_g", "mla_w_uq", "mla_kv_norm_g", "mla_w_ukv", "w_out", "ln1_g", "ln1_b", "w_gate_up", "w_down",
         "ple_w_gate", "ple_w_proj", "ln2_g", "ln2_b")


def _slab(a, align):
    s = a.reshape(-1, LANES)
    pad = -s.shape[0] % align
    return jnp.pad(s, ((0, pad), (0, 0))) if pad else s


def _pack(arrays, align=16, total_align=512):
    s = jnp.concatenate([_slab(a, align) for a in arrays], axis=0)
    pad = -s.shape[0] % total_align
    return jnp.pad(s, ((0, pad), (0, 0))) if pad else s


def _unpack(slab, shapes, align=16):
    out, r0 = [], 0
    for s in shapes:
        nr = math.prod(s) // LANES
        out.append(slab[r0:r0 + nr].reshape(s))
        r0 += nr + (-nr % align)
    return out


def _blocks_to_cols(g, *, name):
    nb, a, b = g.shape

    def body(g_ref, o_ref):
        o_ref[...] = g_ref[...]

    return pl.pallas_call(
        body, name=name, grid=(nb,), in_specs=[pl.BlockSpec((None, a, b), lambda j: (j, 0, 0))],
        out_specs=pl.BlockSpec((a, b), lambda j: (0, j)), out_shape=jax.ShapeDtypeStruct((a, nb * b), g.dtype),
        compiler_params=_cparams(("parallel",)),
    )(g)


def _cols_to_blocks(x, *, name):
    a, b = x.shape[0], x.shape[1] // N_DEV

    def body(x_ref, o_ref):
        o_ref[...] = x_ref[...]

    return pl.pallas_call(
        body, name=name, grid=(N_DEV,), in_specs=[pl.BlockSpec((a, b), lambda j: (0, j))],
        out_specs=pl.BlockSpec((None, a, b), lambda j: (j, 0, 0)), out_shape=jax.ShapeDtypeStruct((N_DEV, a, b), x.dtype),
        compiler_params=_cparams(("parallel",)),
    )(x)


def _weight_shards(w, li):
    uq_pad = ((0, 0), (0, LANES - ATT_D))
    shards = {k: w[k][li] for k in BIG}
    shards["mla_w_uq"] = jnp.pad(shards["mla_w_uq"], uq_pad)
    return {k: s.astype(BF16) for k, s in shards.items()}


def _usable_weights(g, *, name):
    out = {}
    for k, a in g.items():
        if k == "w_in":
            out[k] = _place_w_in(a, name=name + "_place_w_in")
        elif k in ("w_out", "w_down", "ple_w_gate"):
            out[k] = a.reshape(a.shape[0] * a.shape[1], a.shape[2])
        elif k == "w_gate_up":
            out[k] = a
        else:
            out[k] = _blocks_to_cols(a, name=name + "_cols_" + k)
    return out


def _as_pairs(g):
    if g.ndim == 2:
        return g.reshape((4, 2, g.shape[0] // N_DEV) + g.shape[1:])
    return g.reshape((4, 2) + g.shape[1:])


def _twice(fn):
    return lambda *a: fn(*a) * 2


def _layer_forward(li, h, hb, p_i, wts, sm, lbs, tables, alpha, hgrn_job=None, after_hgrn=None, attn_job=None,
                   after_attn=None):
    n = f"l{li}_"
    row1 = lambda a: a.reshape(1, -1)
    projp = _mm(hb, wts["w_in"], name=n + "proj")
    ng = row1(sm["hgrn_norm_g"][li])
    res = _hgrn_fwd(projp, lbs[li], ng, name=n + "hgrn_fwd", job=hgrn_job)
    if hgrn_job is not None:
        res, got = res
        wts = dict(wts, **after_hgrn(got))
    o_a, o_pre, states = res
    lg, lbias = row1(sm["sgu_ln_g"][li]), row1(sm["sgu_ln_b"][li])
    w_s = sm["sgu_w_s"][li]
    bias_full = jnp.repeat(sm["sgu_b_s"][li].T, HEAD, axis=1)
    o_b = _sgu_fwd(projp, lg, lbias, w_s, bias_full, name=n + "sgu_fwd")
    qg, kvg = row1(sm["mla_q_norm_g"][li]), row1(sm["mla_kv_norm_g"][li])
    cq_view, ckv_view = (projp, 384, P_CQ // 384), (projp, 256, P_CKV // 256)
    (cqn,) = _rowwise(_fn_rms, [cq_view], [qg], [(384, BF16)], name=n + "q_norm")
    (ckvn,) = _rowwise(_fn_rms, [ckv_view], [kvg], [(256, BF16)], name=n + "kv_norm")
    q = _mm(cqn, wts["mla_w_uq"], name=n + "uq")
    kv = _mm(ckvn, wts["mla_w_ukv"], name=n + "ukv")
    qr, kf, kvb = _mla_prep(q, kv, projp, tables, name=n + "mla_prep")
    res = _attn_fwd(qr, kf, kvb, name=n + "attn_fwd", job=attn_job)
    if attn_job is not None:
        res, got = res
        wts = dict(wts, **after_attn(got))
    o_c, lse = res
    cat = jnp.concatenate([o_a, o_b, o_c.astype(BF16)], axis=1)
    mix = _mm(cat, wts["w_out"], name=n + "out_proj")
    g1, b1 = row1(sm["ln1_g"][li]), row1(sm["ln1_b"][li])
    d = h.shape[1]
    h1, h1b = _rowwise(_twice(_make_post_mix(alpha)), [h, mix], [g1, b1], [(d, F32), (d, BF16)], name=n + "ln1")
    gu, act = _gate_up_swiglu(h1b, wts["w_gate_up"], name=n + "gate_up")
    ffn = _mm(act, wts["w_down"], am="bmk", name=n + "down")
    pg = _mm(h1b, wts["ple_w_gate"], name=n + "ple_gate")
    pp = _mm(p_i, wts["ple_w_proj"], name=n + "ple_proj")
    g2, b2 = row1(sm["ln2_g"][li]), row1(sm["ln2_b"][li])
    h2, h2b = _rowwise(_twice(_make_ple_ln(alpha)), [h1, ffn, pg, pp], [g2, b2], [(d, F32), (d, BF16)],
                       name=n + "ln2")
    saved = dict(h=h, hb=hb, h1b=h1b, projp=projp, o_pre=o_pre, states=states, cqn=cqn, ckvn=ckvn, qr=qr, kf=kf, kvb=kvb, o_c=o_c,
                 lse=lse, cat=cat, mix=mix, h1=h1, gu=gu, act=act, ffn=ffn, pg=pg, pp=pp, ng=ng, lg=lg, wts=wts,
                 lbias=lbias, w_s=w_s, bias_full=bias_full, qg=qg, kvg=kvg, g1=g1, b1=b1, g2=g2, b2=b2)
    return (h2, h2b), saved


RS_EARLY = ("ple_w_proj", "ple_w_gate", "w_down", "w_gate_up", "w_out")
RS_LATE = ("mla_w_uq", "mla_w_ukv", "w_in")


def _layer_backward(li, dh2_parts, p_i, sv, lbs, tables, alpha, core, carried=None):
    n = f"l{li}_b_"
    wts = sv["wts"]
    gr = {}
    dh1_a, dffn, dpg, dpp, gr["ln2_g"], gr["ln2_b"] = _rowwise_vjp(
        _make_ple_ln(alpha), [sv["h1"], sv["ffn"], sv["pg"], sv["pp"]], [sv["g2"], sv["b2"]], [dh2_parts],
        groups=[[0], [1], [2], [3]], gdtypes=[F32, BF16, BF16, BF16], name=n + "ln2")
    big = {}
    big["ple_w_proj"] = _cols_to_blocks(_mm(p_i, dpp, am="km", name=n + "ple_proj_dw"), name=n + "ple_proj_dw_blocks")
    big["ple_w_gate"] = _mm(sv["h1b"], dpg, am="km", name=n + "ple_gate_dw")
    dh1_b = _mm(dpg, wts["ple_w_gate"], bm="nk", name=n + "ple_gate_dx")
    big["w_down"] = _mm(sv["act"], dffn, am="bkm", name=n + "down_dw")
    dgu = _down_dx_swiglu(dffn, wts["w_down"], sv["gu"], name=n + "down_dx")
    dgu = dgu.reshape((N_DEV,) + dgu.shape[2:])
    big["w_gate_up"], carried_got = _mm(sv["h1b"], dgu, am="km", bm="bkn", om="bmn", name=n + "gate_up_dw",
                                        job=carried), None
    if carried is not None:
        big["w_gate_up"], carried_got = big["w_gate_up"]
    early = [_as_pairs(big[k]) for k in RS_EARLY[:-1]]
    dh1_c, theirs = _mm(dgu, wts["w_gate_up"], am="bmk", bm="bnk", name=n + "gate_up_dx", job=_pair_job(early))
    dh_a, dmix, gr["ln1_g"], gr["ln1_b"] = _rowwise_vjp(
        _make_post_mix(alpha), [sv["h"], sv["mix"]], [sv["g1"], sv["b1"]], [[dh1_a, dh1_b, dh1_c]],
        groups=[[0], [1]], gdtypes=[F32, BF16], name=n + "ln1")
    big["w_out"] = _mm(sv["cat"], dmix, am="km", name=n + "out_proj_dw")
    early.append(_as_pairs(big["w_out"]))
    dcat, their_w_out = _mm(dmix, wts["w_out"], bm="nk", name=n + "out_proj_dx", job=_pair_job(early[-1:]))
    sums = [_pair_add(x, r, core, name=n + "pair_add_" + k)
            for k, x, r in zip(RS_EARLY, early, list(theirs) + list(their_w_out))]

    (dqr, dkv, dkf), early_quads = _attn_bwd(sv["qr"], sv["kf"], sv["kvb"], dcat, sv["o_c"], sv["lse"],
                                             name=n + "attn", job=_quad_job(sums))
    dqpad, dkr = _mla_prep_bwd(dqr, dkf, tables, name=n + "mla_prep")
    big["mla_w_uq"] = _cols_to_blocks(_mm(sv["cqn"], dqpad, am="km", name=n + "uq_dw"), name=n + "uq_dw_blocks")
    dcqn = _mm(dqpad, wts["mla_w_uq"], bm="nk", name=n + "uq_dx")
    big["mla_w_ukv"] = _cols_to_blocks(_mm(sv["ckvn"], dkv, am="km", name=n + "ukv_dw"), name=n + "ukv_dw_blocks")
    dckvn = _mm(dkv, wts["mla_w_ukv"], bm="nk", name=n + "ukv_dx")
    projp = sv["projp"]
    dcq, gr["mla_q_norm_g"] = _rowwise_vjp(_fn_rms, [(projp, 384, P_CQ // 384)], [sv["qg"]], [[dcqn]],
                                           groups=[[0]], gdtypes=[BF16], name=n + "q_norm")
    dckv, gr["mla_kv_norm_g"] = _rowwise_vjp(_fn_rms, [(projp, 256, P_CKV // 256)], [sv["kvg"]], [[dckvn]],
                                             groups=[[0]], gdtypes=[BF16], name=n + "kv_norm")
    dsgu, gr["sgu_ln_g"], gr["sgu_ln_b"], gr["sgu_w_s"], gr["sgu_b_s"] = _sgu_bwd(
        projp, sv["lg"], sv["lbias"], sv["w_s"], sv["bias_full"], dcat, name=n + "sgu")
    dhg, gr["hgrn_norm_g"], gr["lower_bound"] = _hgrn_bwd(
        projp, lbs[li], sv["ng"], sv["o_pre"], sv["states"], dcat, name=n + "hgrn")
    dprojp = jnp.concatenate([dhg, dsgu, dcq, dkr, dckv], axis=1)
    big["w_in"] = _unplace_w_in(_mm(sv["hb"], dprojp, am="km", name=n + "proj_dw"), name=n + "proj_dw_shards")
    late = [_as_pairs(big[k]) for k in RS_LATE]
    dh_b, theirs = _mm(dprojp, wts["w_in"], bm="nk", name=n + "proj_dx", job=_pair_job(late))
    late_sums = [_pair_add(x, r, core, name=n + "pair_add_" + k) for k, x, r in zip(RS_LATE, late, theirs)]
    return [dh_a, dh_b], gr, early_quads, late_sums, carried_got


def kernel(x, p, positions, ln_in_g, ln_in_b, w_in, hgrn_lb_logits, hgrn_norm_g, sgu_ln_g, sgu_ln_b, sgu_w_s, sgu_b_s, mla_q_norm_g, mla_w_uq, mla_kv_norm_g, mla_w_ukv, w_out, ln1_g, ln1_b, w_gate_up, w_down, ple_w_gate, ple_w_proj, ln2_g, ln2_b, loss_target, m_ln_in_g, m_ln_in_b, m_w_in, m_hgrn_lb_logits, m_hgrn_norm_g, m_sgu_ln_g, m_sgu_ln_b, m_sgu_w_s, m_sgu_b_s, m_mla_q_norm_g, m_mla_w_uq, m_mla_kv_norm_g, m_mla_w_ukv, m_w_out, m_ln1_g, m_ln1_b, m_w_gate_up, m_w_down, m_ple_w_gate, m_ple_w_proj, m_ln2_g, m_ln2_b, v_ln_in_g, v_ln_in_b, v_w_in, v_hgrn_lb_logits, v_hgrn_norm_g, v_sgu_ln_g, v_sgu_ln_b, v_sgu_w_s, v_sgu_b_s, v_mla_q_norm_g, v_mla_w_uq, v_mla_kv_norm_g, v_mla_w_ukv, v_w_out, v_ln1_g, v_ln1_b, v_w_gate_up, v_w_down, v_ple_w_gate, v_ple_w_proj, v_ln2_g, v_ln2_b):
    args = dict(locals())
    w = {k: args[k] for k in ORDER}
    m = {k: args["m_" + k] for k in ORDER}
    v = {k: args["v_" + k] for k in ORDER}
    depth = w_in.shape[0]
    assert depth == 2, "the lower-bound kernel is written for two layers"
    alpha = (2 * depth) ** 0.25
    xs, tgt = x[0], loss_target[0]
    d_model = xs.shape[1]

    shards = [_weight_shards(w, li) for li in range(depth)]
    on_hgrn0 = ("mla_w_uq", "mla_w_ukv", "w_out", "ple_w_gate", "ple_w_proj")
    ffn0 = ("w_gate_up", "w_down")
    first1 = ("w_in", "mla_w_uq", "mla_w_ukv", "w_out")
    on_attn1 = ("w_gate_up", "w_down", "ple_w_gate", "ple_w_proj")
    layer1_first = {}

    def after_hgrn0(got):
        got = _gather_forward(got, name="gather_l0a_forward")
        return _usable_weights(dict(zip(on_hgrn0, got)), name="l0")

    def after_attn0(got):
        got = _gather_forward(got, name="gather_l0b_forward")
        layer1_first.update(_usable_weights(dict(zip(first1, got[len(ffn0):])), name="l1"))
        return _usable_weights(dict(zip(ffn0, got[:len(ffn0)])), name="l0")

    def after_attn1(got):
        got = _gather_forward(got, name="gather_l1_forward")
        return _usable_weights(dict(zip(on_attn1, got)), name="l1")

    tables = _rope_tables(positions[0])
    row1 = lambda a: a.reshape(1, -1)
    l0, l1 = row1(hgrn_lb_logits[0]), row1(hgrn_lb_logits[1])
    lbs = _rowwise(_fn_lower_bounds, [l0, l1], [], [(HG_W, F32), (HG_W, F32)], name="lower_bounds")

    gin, bin_ = row1(ln_in_g), row1(ln_in_b)
    (h, hb), g_in = _rowwise(_twice(_fn_ln), [xs], [gin, bin_], [(d_model, F32), (d_model, BF16)], name="ln_in",
                             job=_gather_job([shards[0]["w_in"]]))
    w_in0 = _usable_weights({"w_in": _gather_forward(g_in, name="gather_l0_w_in_forward")[0]}, name="l0")
    (h, hb), sv0 = _layer_forward(
        0, h, hb, p[0, 0], w_in0, w, lbs, tables, alpha,
        hgrn_job=_gather_job([shards[0][k] for k in on_hgrn0]), after_hgrn=after_hgrn0,
        attn_job=_gather_job([shards[0][k] for k in ffn0] + [shards[1][k] for k in first1]), after_attn=after_attn0)
    (h, _), sv1 = _layer_forward(
        1, h, hb, p[1, 0], layer1_first, w, lbs, tables, alpha,
        attn_job=_gather_job([shards[1][k] for k in on_attn1]), after_attn=after_attn1)
    saved = [sv0, sv1]
    dy, loss_local = _loss_and_grad(h, tgt, name="loss")
    loss = lax.psum(loss_local[0, 0], ("x", "y", "c"))

    core = lax.axis_index("c").astype(jnp.int32).reshape(1)
    dparts, grads, quads, carried = [dy], [None] * depth, [None] * depth, None
    for li in reversed(range(depth)):
        dparts, grads[li], early_quads, late_sums, late_quads = _layer_backward(
            li, dparts, p[li, 0], saved[li], lbs, tables, alpha, core, carried=carried)
        quads[li] = dict(zip(RS_EARLY, early_quads))
        if carried is not None:
            quads[li + 1].update(zip(RS_LATE, late_quads))
        carried = _quad_job(late_sums)
    (dx, d_gin, d_bin), late_quads = _rowwise_vjp(_fn_ln, [xs], [gin, bin_], [dparts], groups=[[0]], name="ln_in_b",
                                                   job=carried)
    quads[0].update(zip(RS_LATE, late_quads))
    dl0, dl1 = _rowwise_vjp(_fn_lower_bounds, [l0, l1], [], [[grads[0]["lower_bound"]], [grads[1]["lower_bound"]]],
                            groups=[[0], [1]], name="lower_bounds_b")

    prefixes = ("grad_", "delta_", "new_m_", "new_v_")
    per_layer = {pre + k: [] for pre in prefixes for k in BIG}
    uq_pad = ((0, 0), (0, 0), (0, LANES - ATT_D))
    state = {k: ((jnp.pad(w[k], uq_pad), jnp.pad(m[k], uq_pad), jnp.pad(v[k], uq_pad)) if k == "mla_w_uq"
                 else (w[k], m[k], v[k])) for k in BIG}
    for li in range(depth):
        for k in BIG:
            res4 = _adamw(quads[li][k], *state[k], li, name=f"adamw_l{li}_{k}")
            for pre, a in zip(prefixes, res4):
                per_layer[pre + k].append(a[:, :ATT_D] if k == "mla_w_uq" else a)
    out = {name: jnp.stack(vals) for name, vals in per_layer.items()}

    small_g = {"ln_in_g": d_gin.reshape(-1), "ln_in_b": d_bin.reshape(-1),
               "hgrn_lb_logits": jnp.stack([dl0.reshape(-1), dl1.reshape(-1)])}
    for k in SMALL[3:]:
        small_g[k] = jnp.stack([grads[li][k].reshape(w[k].shape[1:]) for li in range(depth)])
    (small_parts,) = _all_gather([_pack([small_g[k] for k in SMALL])], name="gather_small_grads")
    slabs = _adamw(small_parts, _pack([w[k] for k in SMALL]), _pack([m[k] for k in SMALL]),
                   _pack([v[k] for k in SMALL]), None, name="adamw_small")
    shapes = [w[k].shape for k in SMALL]
    for pre, slab in zip(prefixes, slabs):
        for k, a in zip(SMALL, _unpack(slab, shapes)):
            out[pre + k] = a
    res = [loss, dx[None]]
    for prefix in ("grad_", "delta_", "new_m_", "new_v_"):
        res += [out[prefix + k] for k in ORDER]
    return tuple(res)
```

```python
import functools
import math

import jax
import jax.numpy as jnp
from jax import lax
from jax.experimental import pallas as pl
from jax.experimental.pallas import tpu as pltpu

F32 = jnp.float32
BF16 = jnp.bfloat16
MESH = pl.DeviceIdType.MESH

LN_EPS = 1e-5
RMS_EPS = 1e-6
ROPE_THETA = 10000.0
ADAM_LR, ADAM_B1, ADAM_B2, ADAM_EPS, ADAM_WD, ADAM_STEP = 0.001, 0.9, 0.999, 1e-08, 0.01, 10

N_DEV = 8
LANES = 128
HG_CHUNK = 16
HG_W = 256
HEAD = 64
SGU_CHUNK = 128
N_ATT_HEADS = 8
ATT_D = 96
VMEM_LIMIT = 56 * 1024 * 1024

HG_TILE = 128
ATT_TQ = 512
ROW_TILE = 256

P_CQ, P_KR, P_CKV, P_COLS = 1536, 1920, 2048, 2304


def _cparams(sem):
    return pltpu.CompilerParams(dimension_semantics=sem, vmem_limit_bytes=VMEM_LIMIT)


_ANY = pl.BlockSpec(memory_space=pl.ANY)


def _call(body, operands, *, name, grid, in_specs, out_specs, out_shape, sem, scratch_shapes=(), job=None):
    if job is None:
        return pl.pallas_call(body, name=name, grid=grid, in_specs=in_specs, out_specs=out_specs, out_shape=out_shape,
                              scratch_shapes=list(scratch_shapes), compiler_params=_cparams(sem))(*operands)
    single = not isinstance(out_shape, (list, tuple))
    shapes = [out_shape] if single else list(out_shape)
    ospecs = [out_specs] if single else list(out_specs)
    ni, no, ns = len(operands), len(shapes), len(scratch_shapes)
    ji, jo = len(job.inputs), len(job.out_shapes)

    def hosted(*refs):
        p = 0
        parts = []
        for cnt in (ni, ji, no, jo, ns):
            parts.append(refs[p:p + cnt])
            p += cnt
        ins, jins, outs, jouts, scr = parts
        jsems = refs[p:]
        ids = [pl.program_id(a) for a in range(len(grid))]
        first = functools.reduce(lambda a, b: a & b, [i == 0 for i in ids])
        last = functools.reduce(lambda a, b: a & b, [i == g - 1 for i, g in zip(ids, grid)])

        @pl.when(first)
        def _():
            job.start(jins, jouts, jsems)

        body(*ins, *outs, *scr)

        @pl.when(last)
        def _():
            job.finish(jins, jouts, jsems)

    res = pl.pallas_call(
        hosted, name=name, grid=grid,
        in_specs=list(in_specs) + [_ANY] * ji, out_specs=ospecs + [_ANY] * jo,
        out_shape=shapes + list(job.out_shapes),
        scratch_shapes=list(scratch_shapes) + [pltpu.SemaphoreType.DMA((c,)) for c in job.sem_counts],
        compiler_params=_cparams(("arbitrary",) * len(grid)),
    )(*operands, *job.inputs)
    own = res[0] if single else res[:no]
    return own, res[no:]


class _Job:
    def __init__(self, inputs, out_shapes, sem_counts, start, finish):
        self.inputs, self.out_shapes, self.sem_counts = list(inputs), list(out_shapes), list(sem_counts)
        self.start, self.finish = start, finish


def _copies_job(inputs, out_shapes, n_remote, n_local, make):
    def start(jins, jouts, sems):
        sends, _, local = make(jins, jouts, *sems)
        for cp in local + sends:
            cp.start()

    def finish(jins, jouts, sems):
        sends, recvs, local = make(jins, jouts, *sems)
        for cp in recvs:
            cp.wait_recv()
        for cp in sends:
            cp.wait_send()
        for cp in local:
            cp.wait()

    return _Job(inputs, out_shapes, [n_remote, n_remote, max(n_local, 1)], start, finish)


def _run_job(job, *, name):
    ji, jo = len(job.inputs), len(job.out_shapes)

    def body(*refs):
        jins, jouts, sems = refs[:ji], refs[ji:ji + jo], refs[ji + jo:]
        job.start(jins, jouts, sems)
        job.finish(jins, jouts, sems)

    return pl.pallas_call(
        body, name=name, out_shape=list(job.out_shapes), in_specs=[_ANY] * ji, out_specs=[_ANY] * jo,
        scratch_shapes=[pltpu.SemaphoreType.DMA((c,)) for c in job.sem_counts],
    )(*job.inputs)


def _tile(n, pref):
    if n % pref == 0:
        return pref
    best = None
    t = LANES
    while t <= min(n, pref):
        if n % t == 0:
            best = t
        t += LANES
    return best if best is not None else n


def _mm(a, b, *, am="mk", bm="kn", om="mn", out_dtype=F32, tm=1024, tn=1024, tk=1024, name, job=None):
    if am == "mk":
        m, k = a.shape
    elif am == "km":
        k, m = a.shape
    elif am == "bmk":
        m, tk = a.shape[1], a.shape[2]
        k = a.shape[0] * tk
    else:
        k, tm = a.shape[1], a.shape[2]
        m = a.shape[0] * tm
    if bm == "kn":
        kb_, n = b.shape
    elif bm == "nk":
        n, kb_ = b.shape
    elif bm == "bkn":
        kb_, tn = b.shape[1], b.shape[2]
        n = b.shape[0] * tn
    else:
        n, tk = b.shape[1], b.shape[2]
        kb_ = b.shape[0] * tk
    assert kb_ == k, (a.shape, b.shape, am, bm)
    tm, tn, tk = _tile(m, tm), _tile(n, tn), _tile(k, tk)
    nk = k // tk
    dims = (((0 if am in ("km", "bkm") else 1,), (1 if bm in ("nk", "bnk") else 0,)), ((), ()))

    a_spec = {"mk": pl.BlockSpec((tm, tk), lambda i, j, kk: (i, kk)),
              "km": pl.BlockSpec((tk, tm), lambda i, j, kk: (kk, i)),
              "bmk": pl.BlockSpec((None, tm, tk), lambda i, j, kk: (kk, i, 0)),
              "bkm": pl.BlockSpec((None, tk, tm), lambda i, j, kk: (i, kk, 0))}[am]
    b_spec = {"kn": pl.BlockSpec((tk, tn), lambda i, j, kk: (kk, j)),
              "nk": pl.BlockSpec((tn, tk), lambda i, j, kk: (j, kk)),
              "bkn": pl.BlockSpec((None, tk, tn), lambda i, j, kk: (j, kk, 0)),
              "bnk": pl.BlockSpec((None, tn, tk), lambda i, j, kk: (kk, j, 0))}[bm]
    if om == "mn":
        o_spec, o_shape = pl.BlockSpec((tm, tn), lambda i, j, kk: (i, j)), (m, n)
    else:
        o_spec, o_shape = pl.BlockSpec((None, tm, tn), lambda i, j, kk: (j, i, 0)), (n // tn, m, tn)

    def body(a_ref, b_ref, o_ref, *acc):
        kk = pl.program_id(2)
        prod = lax.dot_general(a_ref[...].astype(BF16), b_ref[...].astype(BF16), dims, preferred_element_type=F32)
        if nk == 1:
            o_ref[...] = prod.astype(o_ref.dtype)
            return
        acc_ref, = acc

        @pl.when(kk == 0)
        def _():
            acc_ref[...] = prod

        if nk > 2:
            @pl.when((kk > 0) & (kk < nk - 1))
            def _():
                acc_ref[...] += prod

        @pl.when(kk == nk - 1)
        def _():
            o_ref[...] = (acc_ref[...] + prod).astype(o_ref.dtype)

    return _call(body, (a, b), name=name, grid=(m // tm, n // tn, nk), in_specs=[a_spec, b_spec], out_specs=o_spec,
                 out_shape=jax.ShapeDtypeStruct(o_shape, out_dtype),
                 scratch_shapes=[pltpu.VMEM((tm, tn), F32)] if nk > 1 else [],
                 sem=("parallel", "parallel", "arbitrary"), job=job)


def _row_operand(a, tile):
    if isinstance(a, tuple):
        arr, w, j = a
        return arr, pl.BlockSpec((tile, w), lambda i, j=j: (i, j))
    return a, pl.BlockSpec((tile, a.shape[1]), lambda i: (i, 0))


def _const_spec(c):
    nd = c.ndim
    return pl.BlockSpec(c.shape, lambda i, nd=nd: (0,) * nd)


def _rowwise(fn, rows, consts, outs, *, name, accs=(), tile=None, job=None):
    t_rows = (rows[0][0] if isinstance(rows[0], tuple) else rows[0]).shape[0]
    tile = min(tile or ROW_TILE, t_rows)
    arrs, specs = zip(*[_row_operand(a, tile) for a in rows])
    nin, no = len(rows) + len(consts), len(outs)

    def body(*refs):
        res = fn(*[r[...] for r in refs[:nin]])
        for r, v in zip(refs[nin:nin + no], res[:no]):
            r[...] = v.astype(r.dtype)
        if accs:
            a_refs = refs[nin + no:]

            @pl.when(pl.program_id(0) == 0)
            def _():
                for r in a_refs:
                    r[...] = jnp.zeros_like(r)

            for r, v in zip(a_refs, res[no:]):
                r[...] += v

    out_shape = [jax.ShapeDtypeStruct((t_rows, w), dt) for w, dt in outs]
    out_shape += [jax.ShapeDtypeStruct(s, F32) for s in accs]
    out_specs = [pl.BlockSpec((tile, w), lambda i: (i, 0)) for w, _ in outs]
    out_specs += [pl.BlockSpec(s, lambda i, nd=len(s): (0,) * nd) for s in accs]
    return _call(body, (*arrs, *consts), name=name, grid=(t_rows // tile,),
                 in_specs=list(specs) + [_const_spec(c) for c in consts],
                 out_specs=out_specs, out_shape=out_shape, sem=("arbitrary",), job=job)


def _rowwise_vjp(fn, rows, consts, cts, *, name, groups, tile=None, gdtypes=None, job=None):
    t_rows = (rows[0][0] if isinstance(rows[0], tuple) else rows[0]).shape[0]
    tile = min(tile or ROW_TILE, t_rows)
    arrs, specs = zip(*[_row_operand(a, tile) for a in rows])
    flat_cts = [c for group in cts for c in group]
    ct_arrs, ct_specs = zip(*[_row_operand(a, tile) for a in flat_cts])
    nr, nc, nct, ng = len(rows), len(consts), len(flat_cts), len(groups)

    def width(a):
        return a[1] if isinstance(a, tuple) else a.shape[1]

    def body(*refs):
        rv = [r[...].astype(F32) for r in refs[:nr]]
        cv = [r[...] for r in refs[nr:nr + nc]]
        ct_refs = refs[nr + nc:nr + nc + nct]
        ctv, pos = [], 0
        for group in cts:
            s = ct_refs[pos][...].astype(F32)
            for r in ct_refs[pos + 1:pos + len(group)]:
                s = s + r[...].astype(F32)
            ctv.append(s)
            pos += len(group)
        _, pull = jax.vjp(fn, *rv, *cv)
        grads = pull(tuple(ctv))
        g_refs = refs[nr + nc + nct:nr + nc + nct + ng]
        for r, idx in zip(g_refs, groups):
            parts = [grads[i] for i in idx]
            r[...] = (parts[0] if len(parts) == 1 else jnp.concatenate(parts, axis=1)).astype(r.dtype)
        c_refs = refs[nr + nc + nct + ng:]

        @pl.when(pl.program_id(0) == 0)
        def _():
            for r in c_refs:
                r[...] = jnp.zeros_like(r)

        for r, v in zip(c_refs, grads[nr:]):
            r[...] += v

    gw = [sum(width(rows[i]) for i in idx) for idx in groups]
    gdtypes = gdtypes or [F32] * ng
    out_shape = [jax.ShapeDtypeStruct((t_rows, w), dt) for w, dt in zip(gw, gdtypes)]
    out_shape += [jax.ShapeDtypeStruct(c.shape, F32) for c in consts]
    out_specs = [pl.BlockSpec((tile, w), lambda i: (i, 0)) for w in gw]
    out_specs += [_const_spec(c) for c in consts]
    return _call(body, (*arrs, *consts, *ct_arrs), name=name, grid=(t_rows // tile,),
                 in_specs=list(specs) + [_const_spec(c) for c in consts] + list(ct_specs),
                 out_specs=out_specs, out_shape=out_shape, sem=("arbitrary",), job=job)


def _layer_norm(x, g, b):
    mu = jnp.mean(x, axis=-1, keepdims=True)
    xc = x - mu
    var = jnp.mean(xc * xc, axis=-1, keepdims=True)
    return xc * lax.rsqrt(var + LN_EPS) * g + b


def _sigmoid(x):
    return 1.0 / (1.0 + jnp.exp(-x))


def _fn_ln(x, g, b):
    return (_layer_norm(x, g, b),)


def _fn_rms(x, g):
    return (x * lax.rsqrt(jnp.mean(x * x, axis=-1, keepdims=True) + RMS_EPS) * g,)


def _make_post_mix(alpha):
    def fn(h, mix, g, b):
        return (_layer_norm(alpha * h + mix, g, b),)
    return fn


def _make_ple_ln(alpha):
    def fn(h1, ffn, pg, pp, g, b):
        return (_layer_norm(alpha * h1 + ffn + _sigmoid(pg) * pp, g, b),)
    return fn


def _fn_lower_bounds(l0, l1):
    m = jnp.maximum(l0, l1)
    e0, e1 = jnp.exp(l0 - m), jnp.exp(l1 - m)
    s = e0 + e1
    p0, p1 = e0 / s, e1 / s
    return (p0 - p0, (p0 + p1) - p0)


def _loss_and_grad(y, target, *, name):
    d = y.shape[1]

    def fn(yv, tv):
        err = yv - tv
        return err * (1.0 / d), 0.5 * jnp.sum(jnp.mean(err * err, axis=-1, keepdims=True), axis=0, keepdims=True)

    return _rowwise(fn, [y, target], [], [(d, F32)], accs=[(1, 1)], name=name)


def _split_dot(x, e_bf16):
    hi = x.astype(BF16)
    lo = (x - hi.astype(F32)).astype(BF16)
    return (jnp.dot(hi, e_bf16, preferred_element_type=F32) + jnp.dot(lo, e_bf16, preferred_element_type=F32))


def _hgrn_common(th):
    rm = lax.broadcasted_iota(jnp.int32, (th, HG_W), 0) % HG_CHUNK

    def seg_cumsum(x):
        for s in (1, 2, 4, 8):
            x = x + jnp.where(rm >= s, pltpu.roll(x, s, 0), 0.0)
        return x

    def seg_rcumsum(x):
        for s in (1, 2, 4, 8):
            x = x + jnp.where(rm < HG_CHUNK - s, pltpu.roll(x, th - s, 0), 0.0)
        return x

    ri = lax.broadcasted_iota(jnp.int32, (HG_W, HG_W), 0) // HEAD
    ci = lax.broadcasted_iota(jnp.int32, (HG_W, HG_W), 1) // HEAD
    head_f32 = (ri == ci).astype(F32)
    head_bf16 = head_f32.astype(BF16)

    def headsum(x):
        return _split_dot(x, head_bf16)

    return rm, seg_cumsum, seg_rcumsum, head_f32, headsum


def _hgrn_gates(qr, fl, lb):
    sg = _sigmoid(fl)
    f = lb + (1.0 - lb) * sg
    sq = _sigmoid(qr)
    return sg, f, jnp.log(f), 1.0 - f, qr * sq, sq


def _shifted(x, d, th):
    return x if d == 0 else pltpu.roll(x, d, 0)


def _unshift(x, d, th):
    return x if d == 0 else pltpu.roll(x, th - d, 0)


def _hgrn_fwd(projp, lb, ng, *, name, job=None):
    t_rows = projp.shape[0]
    th = min(HG_TILE, t_rows)
    nct = th // HG_CHUNK

    def body(q_ref, f_ref, i_ref, g_ref, lb_ref, ng_ref, oa_ref, opre_ref, st_out_ref,
             st_ref, vtm_ref, kv_ref, qe_ref, dec_ref, oint_ref):
        rm, seg_cumsum, seg_rcumsum, head_f32, headsum = _hgrn_common(th)

        @pl.when(pl.program_id(0) == 0)
        def _():
            st_ref[...] = jnp.zeros_like(st_ref)

        qr, fl, v, g = q_ref[...], f_ref[...], i_ref[...], g_ref[...]
        _, f, lf, k, q, _ = _hgrn_gates(qr, fl, lb_ref[...])
        b = seg_cumsum(lf)

        o = jnp.zeros((th, HG_W), F32)
        for d in range(HG_CHUNK):
            kd, bd, vd = _shifted(k, d, th), _shifted(b, d, th), _shifted(v, d, th)
            e = jnp.exp(jnp.where(rm >= d, b - bd, -1e30))
            o = o + headsum(q * kd * e) * vd

        blast = seg_rcumsum(jnp.where(rm == HG_CHUNK - 1, b, 0.0))
        kte = (k * jnp.exp(blast - b)).astype(BF16)
        qe_ref[...] = q * jnp.exp(b)
        dec_ref[...] = jnp.exp(blast)
        vt = v.T
        lane_chunk = lax.broadcasted_iota(jnp.int32, (HG_W, th), 1) // HG_CHUNK
        for c in range(nct):
            vtm_ref[c * HG_W:(c + 1) * HG_W, :] = jnp.where(lane_chunk == c, vt, 0.0).astype(BF16)
        kv_ref[...] = jnp.dot(vtm_ref[...], kte, preferred_element_type=F32)

        s = st_ref[...]
        for c in range(nct):
            rows = slice(c * HG_CHUNK, (c + 1) * HG_CHUNK)
            st_out_ref[c] = s
            oint_ref[rows, :] = lax.dot_general(qe_ref[rows, :].astype(BF16), s.astype(BF16),
                                                (((1,), (1,)), ((), ())), preferred_element_type=F32)
            dec = jnp.max(dec_ref[rows, :], axis=0, keepdims=True)
            s = s * dec + kv_ref[c * HG_W:(c + 1) * HG_W, :] * head_f32
        st_ref[...] = s

        o = o + oint_ref[...]
        opre_ref[...] = o
        r = lax.rsqrt(headsum(o * o) * (1.0 / HEAD) + RMS_EPS)
        oa_ref[...] = (o * r * ng_ref[...] * (g * _sigmoid(g))).astype(oa_ref.dtype)

    col = lambda j: pl.BlockSpec((th, HG_W), lambda i, j=j: (i, j))
    vec = pl.BlockSpec((1, HG_W), lambda i: (0, 0))
    row = pl.BlockSpec((th, HG_W), lambda i: (i, 0))
    n_chunks = t_rows // HG_CHUNK
    return _call(
        body, (projp, projp, projp, projp, lb, ng), name=name, grid=(t_rows // th,),
        in_specs=[col(0), col(1), col(2), col(3), vec, vec],
        out_specs=[row, row, pl.BlockSpec((nct, HG_W, HG_W), lambda i: (i, 0, 0))],
        out_shape=[jax.ShapeDtypeStruct((t_rows, HG_W), BF16), jax.ShapeDtypeStruct((t_rows, HG_W), F32),
                   jax.ShapeDtypeStruct((n_chunks, HG_W, HG_W), F32)],
        scratch_shapes=[pltpu.VMEM((HG_W, HG_W), F32), pltpu.VMEM((nct * HG_W, th), BF16),
                        pltpu.VMEM((nct * HG_W, HG_W), F32), pltpu.VMEM((th, HG_W), F32),
                        pltpu.VMEM((th, HG_W), F32), pltpu.VMEM((th, HG_W), F32)],
        sem=("arbitrary",), job=job)


def _hgrn_bwd(projp, lb, ng, opre, states, dcat, *, name):
    t_rows = projp.shape[0]
    th = min(HG_TILE, t_rows)
    nct = th // HG_CHUNK
    nt = t_rows // th

    def body(q_ref, f_ref, i_ref, g_ref, lb_ref, ng_ref, opre_ref, st_in_ref, do_ref,
             dproj_ref, dng_ref, dlb_ref,
             gst_ref, dotm_ref, qg_ref, v_ref, kte_ref, dop_ref, dec_ref, dkte_ref, dvi_ref, dqe_ref, ddec_ref):
        rm, seg_cumsum, seg_rcumsum, head_f32, headsum = _hgrn_common(th)

        @pl.when(pl.program_id(0) == 0)
        def _():
            gst_ref[...] = jnp.zeros_like(gst_ref)
            dng_ref[...] = jnp.zeros_like(dng_ref)
            dlb_ref[...] = jnp.zeros_like(dlb_ref)

        qr, fl, v, g = q_ref[...], f_ref[...], i_ref[...], g_ref[...]
        lb, ngv = lb_ref[...], ng_ref[...]
        sg, f, lf, k, q, sq = _hgrn_gates(qr, fl, lb)
        b = seg_cumsum(lf)
        blast = seg_rcumsum(jnp.where(rm == HG_CHUNK - 1, b, 0.0))
        eb = jnp.exp(b)
        ekb = jnp.exp(blast - b)
        qe, kte, dec = q * eb, k * ekb, jnp.exp(blast)

        do_out, op = do_ref[...], opre_ref[...]
        sgg = _sigmoid(g)
        sil = g * sgg
        r = lax.rsqrt(headsum(op * op) * (1.0 / HEAD) + RMS_EPS)
        on = op * r
        dng_ref[...] += jnp.sum(do_out * on * sil, axis=0, keepdims=True)
        dg = do_out * on * ngv * (sgg * (1.0 + g * (1.0 - sgg)))
        don = do_out * ngv * sil
        dop = r * (don - on * (headsum(don * on) * (1.0 / HEAD)))

        v_ref[...] = v
        kte_ref[...] = kte
        dop_ref[...] = dop
        dec_ref[...] = dec
        dot_t = dop.T
        lane_chunk = lax.broadcasted_iota(jnp.int32, (HG_W, th), 1) // HG_CHUNK
        for c in range(nct):
            dotm_ref[c * HG_W:(c + 1) * HG_W, :] = jnp.where(lane_chunk == c, dot_t, 0.0).astype(BF16)
        qg_ref[...] = jnp.dot(dotm_ref[...], qe.astype(BF16), preferred_element_type=F32)

        gs = gst_ref[...]
        for c in reversed(range(nct)):
            rows = slice(c * HG_CHUNK, (c + 1) * HG_CHUNK)
            s = st_in_ref[c]
            gm = (gs * head_f32).astype(BF16)
            dkte_ref[rows, :] = jnp.dot(v_ref[rows, :].astype(BF16), gm, preferred_element_type=F32)
            dvi_ref[rows, :] = lax.dot_general(kte_ref[rows, :].astype(BF16), gm, (((1,), (1,)), ((), ())),
                                               preferred_element_type=F32)
            dqe_ref[rows, :] = jnp.dot(dop_ref[rows, :].astype(BF16), s.astype(BF16), preferred_element_type=F32)
            ddec_ref[rows, :] = jnp.broadcast_to(jnp.sum(gs * s, axis=0, keepdims=True), (HG_CHUNK, HG_W))
            dec_c = jnp.max(dec_ref[rows, :], axis=0, keepdims=True)
            gs = gs * dec_c + qg_ref[c * HG_W:(c + 1) * HG_W, :] * head_f32
        gst_ref[...] = gs

        dkte, dqe = dkte_ref[...], dqe_ref[...]
        dq = dqe * eb
        dk = dkte * ekb
        db = dqe * qe - dkte * kte
        dv = dvi_ref[...]
        dblast = dkte * kte + jnp.where(rm == HG_CHUNK - 1, ddec_ref[...] * dec, 0.0)

        for d in range(HG_CHUNK):
            kd, bd, vd = _shifted(k, d, th), _shifted(b, d, th), _shifted(v, d, th)
            e = jnp.exp(jnp.where(rm >= d, b - bd, -1e30))
            p = q * kd * e
            sc = headsum(p)
            dsc = headsum(dop * vd)
            dv = dv + _unshift(sc * dop, d, th)
            dq = dq + dsc * kd * e
            dk = dk + _unshift(dsc * q * e, d, th)
            darg = dsc * p
            db = db + darg - _unshift(darg, d, th)

        db = db + jnp.where(rm == HG_CHUNK - 1, seg_cumsum(dblast), 0.0)
        dlf = seg_rcumsum(db)
        df = dlf / f - dk
        dlb_ref[...] += jnp.sum(df * (1.0 - sg), axis=0, keepdims=True)
        dfl = df * (1.0 - lb) * sg * (1.0 - sg)
        dqr = dq * (sq * (1.0 + qr * (1.0 - sq)))
        dproj_ref[...] = jnp.concatenate([dqr, dfl, dv, dg], axis=1).astype(dproj_ref.dtype)

    rev = lambda i: nt - 1 - i
    col = lambda j: pl.BlockSpec((th, HG_W), lambda i, j=j: (rev(i), j))
    vec = pl.BlockSpec((1, HG_W), lambda i: (0, 0))
    row = pl.BlockSpec((th, HG_W), lambda i: (rev(i), 0))
    tile_f32 = pltpu.VMEM((th, HG_W), F32)
    return pl.pallas_call(
        body, name=name, grid=(nt,),
        in_specs=[col(0), col(1), col(2), col(3), vec, vec, row,
                  pl.BlockSpec((nct, HG_W, HG_W), lambda i: (rev(i), 0, 0)), col(0)],
        out_specs=[pl.BlockSpec((th, 4 * HG_W), lambda i: (rev(i), 0)), vec, vec],
        out_shape=[jax.ShapeDtypeStruct((t_rows, 4 * HG_W), BF16), jax.ShapeDtypeStruct((1, HG_W), F32),
                   jax.ShapeDtypeStruct((1, HG_W), F32)],
        scratch_shapes=[pltpu.VMEM((HG_W, HG_W), F32), pltpu.VMEM((nct * HG_W, th), BF16),
                        pltpu.VMEM((nct * HG_W, HG_W), F32)] + [tile_f32] * 8,
        compiler_params=_cparams(("arbitrary",)),
    )(projp, projp, projp, projp, lb, ng, opre, states, dcat)


_INV_SQRT2 = 1.0 / math.sqrt(2.0)
_INV_SQRT2PI = 1.0 / math.sqrt(2.0 * math.pi)


def _gelu(x):
    return 0.5 * x * (1.0 + lax.erf(x * _INV_SQRT2))


def _gelu_grad(x):
    return 0.5 * (1.0 + lax.erf(x * _INV_SQRT2)) + x * jnp.exp(-0.5 * x * x) * _INV_SQRT2PI


def _sgu_parts(bu, bv, lg, lbias, w_ref, n_groups):
    c = SGU_CHUNK
    tril = (lax.broadcasted_iota(jnp.int32, (c, c), 0) >= lax.broadcasted_iota(jnp.int32, (c, c), 1)).astype(F32)
    gid = lax.broadcasted_iota(jnp.int32, bu.shape, 1) // HEAD
    u = _gelu(bu)
    gv = _gelu(bv)
    mu = jnp.mean(gv, axis=-1, keepdims=True)
    xc = gv - mu
    rstd = lax.rsqrt(jnp.mean(xc * xc, axis=-1, keepdims=True) + LN_EPS)
    xhat = xc * rstd
    vn = xhat * lg + lbias
    ws = [w_ref[gi] * tril for gi in range(n_groups)]
    return tril, gid, u, rstd, xhat, vn, ws


def _sgu_fwd(projp, lg, lbias, w_s, bias_full, *, name):
    t_rows = projp.shape[0]
    n_groups = w_s.shape[0]
    c = SGU_CHUNK

    def body(u_ref, v_ref, lg_ref, lb_ref, w_ref, bias_ref, o_ref):
        _, gid, u, _, _, vn, ws = _sgu_parts(u_ref[...], v_ref[...], lg_ref[...], lb_ref[...], w_ref, n_groups)
        vnb = vn.astype(BF16)
        z = bias_ref[...]
        for gi in range(n_groups):
            z = z + jnp.where(gid == gi, jnp.dot(ws[gi].astype(BF16), vnb, preferred_element_type=F32), 0.0)
        o_ref[...] = (u * z).astype(o_ref.dtype)

    col = lambda j: pl.BlockSpec((c, HG_W), lambda i, j=j: (i, j))
    return pl.pallas_call(
        body, name=name, grid=(t_rows // c,),
        in_specs=[col(4), col(5), _const_spec(lg), _const_spec(lbias), _const_spec(w_s), _const_spec(bias_full)],
        out_specs=pl.BlockSpec((c, HG_W), lambda i: (i, 0)),
        out_shape=jax.ShapeDtypeStruct((t_rows, HG_W), BF16),
        compiler_params=_cparams(("arbitrary",)),
    )(projp, projp, lg, lbias, w_s, bias_full)


def _sgu_bwd(projp, lg, lbias, w_s, bias_full, dcat, *, name):
    t_rows = projp.shape[0]
    n_groups = w_s.shape[0]
    c = SGU_CHUNK
    n = t_rows // c

    def body(u_ref, v_ref, lg_ref, lb_ref, w_ref, bias_ref, do_ref,
             dproj_ref, dlg_ref, dlb_ref, dw_ref, dbs_ref, dbias_acc):
        i = pl.program_id(0)

        @pl.when(i == 0)
        def _():
            dlg_ref[...] = jnp.zeros_like(dlg_ref)
            dlb_ref[...] = jnp.zeros_like(dlb_ref)
            dw_ref[...] = jnp.zeros_like(dw_ref)
            dbias_acc[...] = jnp.zeros_like(dbias_acc)

        bu, bv, lg_v = u_ref[...], v_ref[...], lg_ref[...]
        tril, gid, u, rstd, xhat, vn, ws = _sgu_parts(bu, bv, lg_v, lb_ref[...], w_ref, n_groups)
        vnb = vn.astype(BF16)
        z = bias_ref[...]
        for gi in range(n_groups):
            z = z + jnp.where(gid == gi, jnp.dot(ws[gi].astype(BF16), vnb, preferred_element_type=F32), 0.0)
        do = do_ref[...]
        dbu = do * z * _gelu_grad(bu)
        dz = do * u
        dbias_acc[...] += dz
        dvn = jnp.zeros_like(dz)
        for gi in range(n_groups):
            dzg = jnp.where(gid == gi, dz, 0.0).astype(BF16)
            dw_ref[gi] += lax.dot_general(dzg, vnb, (((1,), (1,)), ((), ())), preferred_element_type=F32) * tril
            dvn = dvn + jnp.dot(ws[gi].T.astype(BF16), dzg, preferred_element_type=F32)
        dlg_ref[...] += jnp.sum(dvn * xhat, axis=0, keepdims=True)
        dlb_ref[...] += jnp.sum(dvn, axis=0, keepdims=True)
        dxh = dvn * lg_v
        dgv = rstd * (dxh - jnp.mean(dxh, axis=-1, keepdims=True)
                      - xhat * jnp.mean(dxh * xhat, axis=-1, keepdims=True))
        dproj_ref[...] = jnp.concatenate([dbu, dgv * _gelu_grad(bv)], axis=1).astype(dproj_ref.dtype)

        @pl.when(i == n - 1)
        def _():
            dbs_ref[...] = jnp.sum(dbias_acc[...].T.reshape(n_groups, HEAD, c), axis=1)

    col = lambda j: pl.BlockSpec((c, HG_W), lambda i, j=j: (i, j))
    return pl.pallas_call(
        body, name=name, grid=(n,),
        in_specs=[col(4), col(5), _const_spec(lg), _const_spec(lbias), _const_spec(w_s), _const_spec(bias_full),
                  col(1)],
        out_specs=[pl.BlockSpec((c, 2 * HG_W), lambda i: (i, 0)), _const_spec(lg), _const_spec(lbias),
                   _const_spec(w_s), pl.BlockSpec((n_groups, c), lambda i: (0, 0))],
        out_shape=[jax.ShapeDtypeStruct((t_rows, 2 * HG_W), BF16), jax.ShapeDtypeStruct(lg.shape, F32),
                   jax.ShapeDtypeStruct(lbias.shape, F32), jax.ShapeDtypeStruct(w_s.shape, F32),
                   jax.ShapeDtypeStruct((n_groups, c), F32)],
        scratch_shapes=[pltpu.VMEM((c, HG_W), F32)],
        compiler_params=_cparams(("arbitrary",)),
    )(projp, projp, lg, lbias, w_s, bias_full, dcat)


def _rope_tables(positions):
    t = positions.shape[0]
    inv_freq = ROPE_THETA ** (-jnp.arange(0, 32, 2, dtype=F32) / 32)
    ang = positions.astype(F32)[:, None] * inv_freq
    cos, sin = jnp.cos(ang), jnp.sin(ang)
    z = lambda w: jnp.zeros((t, w), F32)
    cos_t = jnp.concatenate([jnp.ones((t, 64), F32), cos, cos, z(32)], axis=1)
    sin_up = jnp.concatenate([z(80), sin, z(32)], axis=1)
    sin_dn = jnp.concatenate([z(64), -sin, z(48)], axis=1)
    return cos_t, sin_up, sin_dn


def _rep(x, n):
    return x if n == 1 else jnp.concatenate([x] * n, axis=1)


def _rope(x, cos_t, sin_up, sin_dn):
    w = x.shape[1]
    return x * cos_t + pltpu.roll(x, 16, 1) * sin_up + pltpu.roll(x, w - 16, 1) * sin_dn


def _rope_t(dy, cos_t, sin_up, sin_dn):
    w = dy.shape[1]
    return dy * cos_t + pltpu.roll(dy * sin_up, w - 16, 1) + pltpu.roll(dy * sin_dn, 16, 1)


def _mla_prep(q, kv, projp, tables, *, name):
    nh = N_ATT_HEADS

    def fn(qv, kvv, kr, cos_t, sin_up, sin_dn):
        qr = _rope(qv, _rep(cos_t, nh), _rep(sin_up, nh), _rep(sin_dn, nh))
        krr = _rope(kr, cos_t, sin_up, sin_dn)
        lane = lax.broadcasted_iota(jnp.int32, kvv.shape, 1) % LANES
        return qr, jnp.where(lane < HEAD, kvv, 0.0) + _rep(krr, nh), kvv

    w = q.shape[1]
    return _rowwise(fn, [q, kv, (projp, LANES, P_KR // LANES)] + list(tables), [],
                    [(w, BF16), (w, BF16), (w, BF16)], name=name)


def _mla_prep_bwd(dqr, dkf, tables, *, name):
    nh = N_ATT_HEADS

    def fn(dq, dk, cos_t, sin_up, sin_dn):
        dqp = _rope_t(dq, _rep(cos_t, nh), _rep(sin_up, nh), _rep(sin_dn, nh))
        dkrr = dk[:, 0:LANES]
        for h in range(1, nh):
            dkrr = dkrr + dk[:, LANES * h:LANES * (h + 1)]
        return dqp, _rope_t(dkrr, cos_t, sin_up, sin_dn)

    return _rowwise(fn, [dqr, dkf] + list(tables), [], [(dqr.shape[1], BF16), (LANES, BF16)], name=name)


_LOG2E = 1.0 / math.log(2.0)
_NT = (((1,), (1,)), ((), ()))
_TN = (((0,), (0,)), ((), ()))


def _attn_fwd(qr, kf, kvb, *, name, job=None):
    t_rows = qr.shape[0]
    tq = min(ATT_TQ, t_rows)
    nb = t_rows // tq
    scale = ATT_D ** -0.5

    c2 = scale * _LOG2E

    def body(q_ref, kf_ref, kvb_ref, o_ref, lse_ref):
        qi = pl.program_id(1)
        lane = lax.broadcasted_iota(jnp.int32, (tq, LANES), 1)
        causal_t = (lax.broadcasted_iota(jnp.int32, (tq, tq), 0) <= lax.broadcasted_iota(jnp.int32, (tq, tq), 1))
        heads = [slice(hh * LANES, (hh + 1) * LANES) for hh in range(2)]
        qs = [q_ref[:, cols] for cols in heads]

        def block(ki, carry, diagonal):
            rows = pl.ds(pl.multiple_of(ki * tq, tq), tq)
            new = []
            for q, cols, (m_old, l_old, acc_t) in zip(qs, heads, carry):
                s_t = lax.dot_general(kf_ref[rows, cols], q, _NT, preferred_element_type=F32)
                if diagonal:
                    s_t = jnp.where(causal_t, s_t, -1e30)
                m_new = jnp.maximum(m_old, jnp.max(s_t, axis=0, keepdims=True))
                p_t = jnp.exp2((s_t - m_new) * c2)
                a = jnp.exp2((m_old - m_new) * c2)
                pv_t = lax.dot_general(kvb_ref[rows, cols], p_t.astype(BF16), _TN, preferred_element_type=F32)
                new.append((m_new, a * l_old + jnp.sum(p_t, axis=0, keepdims=True), a * acc_t + pv_t))
            return tuple(new)

        init = (jnp.full((1, tq), -1e30, F32), jnp.zeros((1, tq), F32), jnp.zeros((LANES, tq), F32))
        carry = lax.fori_loop(0, qi, lambda ki, c: block(ki, c, False), (init, init))
        outs = []
        for hh, (m_fin, l_fin, acc_t) in enumerate(block(qi, carry, True)):
            lse_ref[hh] = m_fin * scale + jnp.log(l_fin)
            outs.append((acc_t / l_fin).T)
        o_ref[...] = jnp.where(lane < HEAD, pltpu.roll(outs[0], HEAD, 1), outs[1])

    pair = pl.BlockSpec((t_rows, 2 * LANES), lambda pr, qi: (0, pr))
    return _call(
        body, (qr, kf, kvb), name=name, grid=(N_ATT_HEADS // 2, nb),
        in_specs=[pl.BlockSpec((tq, 2 * LANES), lambda pr, qi: (qi, pr)), pair, pair],
        out_specs=[pl.BlockSpec((tq, LANES), lambda pr, qi: (qi, pr)),
                   pl.BlockSpec((2, 1, tq), lambda pr, qi: (pr, 0, qi))],
        out_shape=[jax.ShapeDtypeStruct((t_rows, N_ATT_HEADS * HEAD), F32),
                   jax.ShapeDtypeStruct((N_ATT_HEADS, 1, t_rows), F32)],
        sem=("parallel", "arbitrary"), job=job)


def _attn_bwd(qr, kf, kvb, dcat, o, lse, *, name, job=None):
    t_rows = qr.shape[0]
    tq = min(ATT_TQ, t_rows)
    nb = t_rows // tq
    scale = ATT_D ** -0.5
    c2 = scale * _LOG2E
    do_off = 2 * HG_W // LANES

    def body(q_ref, kf_ref, kvb_ref, do_ref, o_ref, lse_ref, dq_ref, dkv_ref, dk_ref):
        ki = pl.program_id(1)

        @pl.when(ki == 0)
        def _():
            dq_ref[...] = jnp.zeros_like(dq_ref)

        lane = lax.broadcasted_iota(jnp.int32, (tq, LANES), 1)
        causal_t = (lax.broadcasted_iota(jnp.int32, (tq, tq), 0) <= lax.broadcasted_iota(jnp.int32, (tq, tq), 1))
        heads = [slice(hh * LANES, (hh + 1) * LANES) for hh in range(2)]
        ks = [kf_ref[:, cols] for cols in heads]
        vs = [kvb_ref[:, cols] for cols in heads]

        def block(qi, carry, diagonal):
            rows = pl.ds(pl.multiple_of(qi * tq, tq), tq)
            do_pair, o_pair = do_ref[rows, :], o_ref[rows, :]
            new = []
            for hh, (cols, k, v, (dk, dv)) in enumerate(zip(heads, ks, vs, carry)):
                q = q_ref[rows, cols]
                do, ov = (pltpu.roll(do_pair, HEAD, 1), pltpu.roll(o_pair, HEAD, 1)) if hh == 0 else (do_pair, o_pair)
                do = jnp.where(lane >= HEAD, do, 0.0)
                delta = jnp.sum((do * ov).T, axis=0, keepdims=True)
                s_t = lax.dot_general(k, q, _NT, preferred_element_type=F32)
                if diagonal:
                    s_t = jnp.where(causal_t, s_t, -1e30)
                p_t = jnp.exp2(s_t * c2 - lse_ref[hh, :, rows] * _LOG2E)
                dob = do.astype(BF16)
                dv = dv + jnp.dot(p_t.astype(BF16), dob, preferred_element_type=F32)
                dp_t = lax.dot_general(v, dob, _NT, preferred_element_type=F32)
                ds_t = (p_t * (dp_t - delta) * scale).astype(BF16)
                dk = dk + jnp.dot(ds_t, q, preferred_element_type=F32)
                dq_ref[rows, cols] += lax.dot_general(ds_t, k, _TN, preferred_element_type=F32)
                new.append((dk, dv))
            return tuple(new)

        zero = jnp.zeros((tq, LANES), F32)
        carry = block(ki, ((zero, zero), (zero, zero)), True)
        carry = lax.fori_loop(ki + 1, nb, lambda qi, c: block(qi, c, False), carry)
        dkv_ref[...] = jnp.concatenate([jnp.where(lane < HEAD, dk, dv) for dk, dv in carry],
                                       axis=1).astype(dkv_ref.dtype)
        dk_ref[...] = jnp.concatenate([dk for dk, _ in carry], axis=1)

    pair_all = pl.BlockSpec((t_rows, 2 * LANES), lambda pr, ki: (0, pr))
    pair_blk = pl.BlockSpec((tq, 2 * LANES), lambda pr, ki: (ki, pr))
    wide = jax.ShapeDtypeStruct((t_rows, N_ATT_HEADS * LANES), F32)
    return _call(
        body, (qr, kf, kvb, dcat, o, lse), name=name, grid=(N_ATT_HEADS // 2, nb),
        in_specs=[pair_all, pair_blk, pair_blk,
                  pl.BlockSpec((t_rows, LANES), lambda pr, ki: (0, do_off + pr)),
                  pl.BlockSpec((t_rows, LANES), lambda pr, ki: (0, pr)),
                  pl.BlockSpec((2, 1, t_rows), lambda pr, ki: (pr, 0, 0))],
        out_specs=[pair_all, pair_blk, pair_blk],
        out_shape=[wide, jax.ShapeDtypeStruct(wide.shape, BF16), wide],
        sem=("parallel", "arbitrary"), job=job)


def _my_pos():
    return lax.axis_index("x"), lax.axis_index("y"), lax.axis_index("c")


def _all_gather(xs, *, name):
    return _gather_forward(_run_job(_gather_job(xs), name=name), name=name + "_forward")


def _remote(src, dst, send_sems, recv_sems, k, dev):
    return pltpu.make_async_remote_copy(src_ref=src, dst_ref=dst, send_sem=send_sems.at[k], recv_sem=recv_sems.at[k],
                                        device_id=dev, device_id_type=MESH)


def _gather_job(xs):
    n = len(xs)

    def make(x_refs, out_refs, send_sems, recv_sems, local_sems):
        mx, my, mc = _my_pos()
        mine = 4 * mx + 2 * my + mc
        peers = [(mx, my, 1 - mc), (1 - mx, my, mc), (mx, 1 - my, mc), (1 - mx, 1 - my, mc)]
        sends, recvs, local = [], [], []
        for a in range(n):
            local.append(pltpu.make_async_copy(x_refs[a], out_refs[a].at[mine], local_sems.at[a]))
            for k, dev in enumerate(peers):
                theirs = 4 * dev[0] + 2 * dev[1] + dev[2]
                sends.append(_remote(x_refs[a], out_refs[a].at[mine], send_sems, recv_sems, 4 * a + k, dev))
                recvs.append(_remote(x_refs[a], out_refs[a].at[theirs], send_sems, recv_sems, 4 * a + k, dev))
        return sends, recvs, local

    shapes = [jax.ShapeDtypeStruct((N_DEV,) + x.shape, x.dtype) for x in xs]
    return _copies_job(xs, shapes, 4 * n, n, make)


def _gather_forward(gs, *, name):
    n = len(gs)

    def body(*refs):
        out_refs = refs[n:2 * n]
        send_sems, recv_sems = refs[2 * n:]
        mx, my, mc = _my_pos()
        chips = [(1 - mx, my), (mx, 1 - my), (1 - mx, 1 - my)]
        sends, recvs = [], []
        for a in range(n):
            for j, (cx, cy) in enumerate(chips):
                here, there = out_refs[a].at[4 * cx + 2 * cy + mc], out_refs[a].at[4 * cx + 2 * cy + 1 - mc]
                sends.append(_remote(here, here, send_sems, recv_sems, 3 * a + j, (mx, my, 1 - mc)))
                recvs.append(_remote(here, there, send_sems, recv_sems, 3 * a + j, (mx, my, 1 - mc)))
        for cp in sends:
            cp.start()
        for cp in recvs:
            cp.wait_recv()
        for cp in sends:
            cp.wait_send()

    return pl.pallas_call(
        body, name=name, out_shape=[jax.ShapeDtypeStruct(g.shape, g.dtype) for g in gs],
        in_specs=[_ANY] * n, out_specs=[_ANY] * n, input_output_aliases={a: a for a in range(n)},
        scratch_shapes=[pltpu.SemaphoreType.DMA((3 * n,)), pltpu.SemaphoreType.DMA((3 * n,))],
    )(*gs)


def _pair_job(xs):
    n = len(xs)

    def make(x_refs, out_refs, send_sems, recv_sems, local_sems):
        mx, my, mc = _my_pos()
        copies = [_remote(x_refs[a].at[g, 1 - mc], out_refs[a].at[g], send_sems, recv_sems, 4 * a + g, (mx, my, 1 - mc))
                  for a in range(n) for g in range(4)]
        return copies, copies, []

    shapes = [jax.ShapeDtypeStruct((4,) + x.shape[2:], x.dtype) for x in xs]
    return _copies_job(xs, shapes, 4 * n, 0, make)


def _pair_add(x, r, core, *, name):
    _, _, a, b = x.shape
    ta = _row_tile(a, 256)

    def body(c_ref, x_ref, r_ref, o_ref):
        o_ref[...] = (x_ref[...] + r_ref[...]).astype(o_ref.dtype)

    blk = pl.BlockSpec((None, ta, b), lambda g, i, c_ref: (g, i, 0))
    return pl.pallas_call(
        body, name=name,
        grid_spec=pltpu.PrefetchScalarGridSpec(
            num_scalar_prefetch=1, grid=(4, a // ta),
            in_specs=[pl.BlockSpec((None, None, ta, b), lambda g, i, c_ref: (g, c_ref[0], i, 0)), blk],
            out_specs=blk),
        out_shape=jax.ShapeDtypeStruct((4, a, b), BF16),
        compiler_params=_cparams(("parallel", "parallel")),
    )(core, x, r)


def _quad_job(xs):
    n = len(xs)

    def make(x_refs, out_refs, send_sems, recv_sems, local_sems):
        mx, my, mc = _my_pos()
        mine = 2 * mx + my
        peers = [((1 - mx, my, mc), 2 * (1 - mx) + my), ((mx, 1 - my, mc), 2 * mx + 1 - my),
                 ((1 - mx, 1 - my, mc), 2 * (1 - mx) + 1 - my)]
        sends, recvs, local = [], [], []
        for a in range(n):
            local.append(pltpu.make_async_copy(x_refs[a].at[mine], out_refs[a].at[mine], local_sems.at[a]))
            for k, (dev, g) in enumerate(peers):
                sends.append(_remote(x_refs[a].at[g], out_refs[a].at[mine], send_sems, recv_sems, 3 * a + k, dev))
                recvs.append(_remote(x_refs[a].at[g], out_refs[a].at[g], send_sems, recv_sems, 3 * a + k, dev))
        return sends, recvs, local

    shapes = [jax.ShapeDtypeStruct(x.shape, x.dtype) for x in xs]
    return _copies_job(xs, shapes, 3 * n, n, make)


def _row_tile(r, pref):
    t = min(pref, r)
    while r % t or (t % 8 and t != r):
        t -= 1
    return t


def _adamw(parts, w, m, v, layer, *, name, tile=256):
    g, a, b = parts.shape
    tile = _row_tile(a, tile)
    c1 = 1.0 / (1.0 - ADAM_B1 ** ADAM_STEP)
    c2 = 1.0 / (1.0 - ADAM_B2 ** ADAM_STEP)

    def body(p_ref, w_ref, m_ref, v_ref, g_ref, d_ref, mo_ref, vo_ref):
        grad = p_ref[0].astype(F32)
        for j in range(1, g):
            grad = grad + p_ref[j].astype(F32)
        mn = ADAM_B1 * m_ref[...] + (1.0 - ADAM_B1) * grad
        vn = ADAM_B2 * v_ref[...] + (1.0 - ADAM_B2) * (grad * grad)
        g_ref[...] = grad
        mo_ref[...] = mn
        vo_ref[...] = vn
        d_ref[...] = -ADAM_LR * ((mn * c1) / (jnp.sqrt(vn * c2) + ADAM_EPS) + ADAM_WD * w_ref[...])

    slab = pl.BlockSpec((tile, b), lambda i: (i, 0))
    src = slab if layer is None else pl.BlockSpec((None, tile, b), lambda i: (layer, i, 0))
    return pl.pallas_call(
        body, name=name, grid=(a // tile,),
        in_specs=[pl.BlockSpec((g, tile, b), lambda i: (0, i, 0)), src, src, src],
        out_specs=[slab] * 4,
        out_shape=[jax.ShapeDtypeStruct((a, b), F32)] * 4,
        compiler_params=_cparams(("parallel",)),
    )(parts, w, m, v)


W_IN_SHARD = 276


def _w_in_dest(col):
    return jnp.where(col < P_KR, col, jnp.where(col < P_KR + 256, col + (P_CKV - P_KR), col - 2176 + P_KR + HEAD))


def _place_w_in(g, *, name):
    _, d, sh = g.shape
    tc = 768

    def body(g_ref, o_ref, acc_ref):
        ct, j = pl.program_id(0), pl.program_id(1)

        @pl.when(j == 0)
        def _():
            acc_ref[...] = jnp.zeros_like(acc_ref)

        src = j * sh + lax.broadcasted_iota(jnp.int32, (sh, tc), 0)
        dst = ct * tc + lax.broadcasted_iota(jnp.int32, (sh, tc), 1)
        place = (_w_in_dest(src) == dst).astype(BF16)
        acc_ref[...] += jnp.dot(g_ref[...], place, preferred_element_type=F32)

        @pl.when(j == N_DEV - 1)
        def _():
            o_ref[...] = acc_ref[...].astype(o_ref.dtype)

    return pl.pallas_call(
        body, name=name, grid=(P_COLS // tc, N_DEV),
        in_specs=[pl.BlockSpec((None, d, sh), lambda ct, j: (j, 0, 0))],
        out_specs=pl.BlockSpec((d, tc), lambda ct, j: (0, ct)),
        out_shape=jax.ShapeDtypeStruct((d, P_COLS), BF16),
        scratch_shapes=[pltpu.VMEM((d, tc), F32)],
        compiler_params=_cparams(("parallel", "arbitrary")),
    )(g)


def _unplace_w_in(dw, *, name):
    d = dw.shape[0]
    sh = W_IN_SHARD

    def body(dw_ref, o_ref):
        j = pl.program_id(0)
        src = j * sh + lax.broadcasted_iota(jnp.int32, (P_COLS, sh), 1)
        dst = lax.broadcasted_iota(jnp.int32, (P_COLS, sh), 0)
        pick = (_w_in_dest(src) == dst).astype(BF16)
        x = dw_ref[...]
        o_ref[...] = _split_dot(x, pick)

    return pl.pallas_call(
        body, name=name, grid=(N_DEV,),
        in_specs=[pl.BlockSpec((d, P_COLS), lambda j: (0, 0))],
        out_specs=pl.BlockSpec((None, d, sh), lambda j: (j, 0, 0)),
        out_shape=jax.ShapeDtypeStruct((N_DEV, d, sh), F32),
        compiler_params=_cparams(("arbitrary",)),
    )(dw)


def _gate_up_swiglu(h1, wgu, *, name):
    t_rows, k = h1.shape
    w = wgu.shape[2]
    tm = _tile(t_rows, 1024)

    def body(a_ref, wg_ref, wu_ref, gu_ref, act_ref):
        a = a_ref[...].astype(BF16)
        gate = jnp.dot(a, wg_ref[...], preferred_element_type=F32)
        up = jnp.dot(a, wu_ref[...], preferred_element_type=F32)
        gu_ref[0] = gate.astype(gu_ref.dtype)
        gu_ref[1] = up.astype(gu_ref.dtype)
        act_ref[...] = (gate * _sigmoid(gate) * up).astype(act_ref.dtype)

    return pl.pallas_call(
        body, name=name, grid=(t_rows // tm, 4),
        in_specs=[pl.BlockSpec((tm, k), lambda i, j: (i, 0)),
                  pl.BlockSpec((None, k, w), lambda i, j: (j, 0, 0)),
                  pl.BlockSpec((None, k, w), lambda i, j: (j + 4, 0, 0))],
        out_specs=[pl.BlockSpec((2, None, tm, w), lambda i, j: (0, j, i, 0)),
                   pl.BlockSpec((None, tm, w), lambda i, j: (j, i, 0))],
        out_shape=[jax.ShapeDtypeStruct((2, 4, t_rows, w), BF16), jax.ShapeDtypeStruct((4, t_rows, w), BF16)],
        compiler_params=_cparams(("parallel", "arbitrary")),
    )(h1, wgu, wgu)


def _down_dx_swiglu(dffn, wdown, gu, *, name):
    t_rows, k = dffn.shape
    w = gu.shape[3]
    tm = _tile(t_rows, 1024)

    def body(d_ref, w_ref, gu_ref, o_ref):
        dact = lax.dot_general(d_ref[...].astype(BF16), w_ref[...], _NT, preferred_element_type=F32)
        gate, up = gu_ref[0].astype(F32), gu_ref[1].astype(F32)
        sg = _sigmoid(gate)
        o_ref[0] = (dact * up * (sg * (1.0 + gate * (1.0 - sg)))).astype(o_ref.dtype)
        o_ref[1] = (dact * gate * sg).astype(o_ref.dtype)

    blk = pl.BlockSpec((2, None, tm, w), lambda i, j: (0, j, i, 0))
    return pl.pallas_call(
        body, name=name, grid=(t_rows // tm, 4),
        in_specs=[pl.BlockSpec((tm, k), lambda i, j: (i, 0)), pl.BlockSpec((w, k), lambda i, j: (j, 0)), blk],
        out_specs=blk, out_shape=jax.ShapeDtypeStruct(gu.shape, BF16),
        compiler_params=_cparams(("parallel", "arbitrary")),
    )(dffn, wdown, gu)


BIG = ("w_in", "mla_w_uq", "mla_w_ukv", "w_out", "w_gate_up", "w_down", "ple_w_gate", "ple_w_proj")
SMALL = ("ln_in_g", "ln_in_b", "hgrn_lb_logits", "hgrn_norm_g", "sgu_ln_g", "sgu_ln_b", "sgu_w_s", "sgu_b_s",
         "mla_q_norm_g", "mla_kv_norm_g", "ln1_g", "ln1_b", "ln2_g", "ln2_b")
ORDER = ("ln_in_g", "ln_in_b", "w_in", "hgrn_lb_logits", "hgrn_norm_g", "sgu_ln_g", "sgu_ln_b", "sgu_w_s", "sgu_b_s",
         "mla_q_norm_g", "mla_w_uq", "mla_kv_norm_g", "mla_w_ukv", "w_out", "ln1_g", "ln1_b", "w_gate_up", "w_down",
         "ple_w_gate", "ple_w_proj", "ln2_g", "ln2_b")


def _slab(a, align):
    s = a.reshape(-1, LANES)
    pad = -s.shape[0] % align
    return jnp.pad(s, ((0, pad), (0, 0))) if pad else s


def _pack(arrays, align=16, total_align=512):
    s = jnp.concatenate([_slab(a, align) for a in arrays], axis=0)
    pad = -s.shape[0] % total_align
    return jnp.pad(s, ((0, pad), (0, 0))) if pad else s


def _unpack(slab, shapes, align=16):
    out, r0 = [], 0
    for s in shapes:
        nr = math.prod(s) // LANES
        out.append(slab[r0:r0 + nr].reshape(s))
        r0 += nr + (-nr % align)
    return out


def _blocks_to_cols(g, *, name):
    nb, a, b = g.shape

    def body(g_ref, o_ref):
        o_ref[...] = g_ref[...]

    return pl.pallas_call(
        body, name=name, grid=(nb,), in_specs=[pl.BlockSpec((None, a, b), lambda j: (j, 0, 0))],
        out_specs=pl.BlockSpec((a, b), lambda j: (0, j)), out_shape=jax.ShapeDtypeStruct((a, nb * b), g.dtype),
        compiler_params=_cparams(("parallel",)),
    )(g)


def _cols_to_blocks(x, *, name):
    a, b = x.shape[0], x.shape[1] // N_DEV

    def body(x_ref, o_ref):
        o_ref[...] = x_ref[...]

    return pl.pallas_call(
        body, name=name, grid=(N_DEV,), in_specs=[pl.BlockSpec((a, b), lambda j: (0, j))],
        out_specs=pl.BlockSpec((None, a, b), lambda j: (j, 0, 0)), out_shape=jax.ShapeDtypeStruct((N_DEV, a, b), x.dtype),
        compiler_params=_cparams(("parallel",)),
    )(x)


def _weight_shards(w, li):
    uq_pad = ((0, 0), (0, LANES - ATT_D))
    shards = {k: w[k][li] for k in BIG}
    shards["mla_w_uq"] = jnp.pad(shards["mla_w_uq"], uq_pad)
    return {k: s.astype(BF16) for k, s in shards.items()}


def _usable_weights(g, *, name):
    out = {}
    for k, a in g.items():
        if k == "w_in":
            out[k] = _place_w_in(a, name=name + "_place_w_in")
        elif k in ("w_out", "w_down", "ple_w_gate"):
            out[k] = a.reshape(a.shape[0] * a.shape[1], a.shape[2])
        elif k == "w_gate_up":
            out[k] = a
        else:
            out[k] = _blocks_to_cols(a, name=name + "_cols_" + k)
    return out


def _as_pairs(g):
    if g.ndim == 2:
        return g.reshape((4, 2, g.shape[0] // N_DEV) + g.shape[1:])
    return g.reshape((4, 2) + g.shape[1:])


def _twice(fn):
    return lambda *a: fn(*a) * 2


def _layer_forward(li, h, hb, p_i, wts, sm, lbs, tables, alpha, hgrn_job=None, after_hgrn=None, attn_job=None,
                   after_attn=None):
    n = f"l{li}_"
    row1 = lambda a: a.reshape(1, -1)
    projp = _mm(hb, wts["w_in"], name=n + "proj")
    ng = row1(sm["hgrn_norm_g"][li])
    res = _hgrn_fwd(projp, lbs[li], ng, name=n + "hgrn_fwd", job=hgrn_job)
    if hgrn_job is not None:
        res, got = res
        wts = dict(wts, **after_hgrn(got))
    o_a, o_pre, states = res
    lg, lbias = row1(sm["sgu_ln_g"][li]), row1(sm["sgu_ln_b"][li])
    w_s = sm["sgu_w_s"][li]
    bias_full = jnp.repeat(sm["sgu_b_s"][li].T, HEAD, axis=1)
    o_b = _sgu_fwd(projp, lg, lbias, w_s, bias_full, name=n + "sgu_fwd")
    qg, kvg = row1(sm["mla_q_norm_g"][li]), row1(sm["mla_kv_norm_g"][li])
    cq_view, ckv_view = (projp, 384, P_CQ // 384), (projp, 256, P_CKV // 256)
    (cqn,) = _rowwise(_fn_rms, [cq_view], [qg], [(384, BF16)], name=n + "q_norm")
    (ckvn,) = _rowwise(_fn_rms, [ckv_view], [kvg], [(256, BF16)], name=n + "kv_norm")
    q = _mm(cqn, wts["mla_w_uq"], name=n + "uq")
    kv = _mm(ckvn, wts["mla_w_ukv"], name=n + "ukv")
    qr, kf, kvb = _mla_prep(q, kv, projp, tables, name=n + "mla_prep")
    res = _attn_fwd(qr, kf, kvb, name=n + "attn_fwd", job=attn_job)
    if attn_job is not None:
        res, got = res
        wts = dict(wts, **after_attn(got))
    o_c, lse = res
    cat = jnp.concatenate([o_a, o_b, o_c.astype(BF16)], axis=1)
    mix = _mm(cat, wts["w_out"], name=n + "out_proj")
    g1, b1 = row1(sm["ln1_g"][li]), row1(sm["ln1_b"][li])
    d = h.shape[1]
    h1, h1b = _rowwise(_twice(_make_post_mix(alpha)), [h, mix], [g1, b1], [(d, F32), (d, BF16)], name=n + "ln1")
    gu, act = _gate_up_swiglu(h1b, wts["w_gate_up"], name=n + "gate_up")
    ffn = _mm(act, wts["w_down"], am="bmk", name=n + "down")
    pg = _mm(h1b, wts["ple_w_gate"], name=n + "ple_gate")
    pp = _mm(p_i, wts["ple_w_proj"], name=n + "ple_proj")
    g2, b2 = row1(sm["ln2_g"][li]), row1(sm["ln2_b"][li])
    h2, h2b = _rowwise(_twice(_make_ple_ln(alpha)), [h1, ffn, pg, pp], [g2, b2], [(d, F32), (d, BF16)],
                       name=n + "ln2")
    saved = dict(h=h, hb=hb, h1b=h1b, projp=projp, o_pre=o_pre, states=states, cqn=cqn, ckvn=ckvn, qr=qr, kf=kf, kvb=kvb, o_c=o_c,
                 lse=lse, cat=cat, mix=mix, h1=h1, gu=gu, act=act, ffn=ffn, pg=pg, pp=pp, ng=ng, lg=lg, wts=wts,
                 lbias=lbias, w_s=w_s, bias_full=bias_full, qg=qg, kvg=kvg, g1=g1, b1=b1, g2=g2, b2=b2)
    return (h2, h2b), saved


RS_EARLY = ("ple_w_proj", "ple_w_gate", "w_down", "w_gate_up", "w_out")
RS_LATE = ("mla_w_uq", "mla_w_ukv", "w_in")


def _layer_backward(li, dh2_parts, p_i, sv, lbs, tables, alpha, core, carried=None):
    n = f"l{li}_b_"
    wts = sv["wts"]
    gr = {}
    dh1_a, dffn, dpg, dpp, gr["ln2_g"], gr["ln2_b"] = _rowwise_vjp(
        _make_ple_ln(alpha), [sv["h1"], sv["ffn"], sv["pg"], sv["pp"]], [sv["g2"], sv["b2"]], [dh2_parts],
        groups=[[0], [1], [2], [3]], gdtypes=[F32, BF16, BF16, BF16], name=n + "ln2")
    big = {}
    big["ple_w_proj"] = _cols_to_blocks(_mm(p_i, dpp, am="km", name=n + "ple_proj_dw"), name=n + "ple_proj_dw_blocks")
    big["ple_w_gate"] = _mm(sv["h1b"], dpg, am="km", name=n + "ple_gate_dw")
    dh1_b = _mm(dpg, wts["ple_w_gate"], bm="nk", name=n + "ple_gate_dx")
    big["w_down"] = _mm(sv["act"], dffn, am="bkm", name=n + "down_dw")
    dgu = _down_dx_swiglu(dffn, wts["w_down"], sv["gu"], name=n + "down_dx")
    dgu = dgu.reshape((N_DEV,) + dgu.shape[2:])
    big["w_gate_up"], carried_got = _mm(sv["h1b"], dgu, am="km", bm="bkn", om="bmn", name=n + "gate_up_dw",
                                        job=carried), None
    if carried is not None:
        big["w_gate_up"], carried_got = big["w_gate_up"]
    early = [_as_pairs(big[k]) for k in RS_EARLY[:-1]]
    dh1_c, theirs = _mm(dgu, wts["w_gate_up"], am="bmk", bm="bnk", name=n + "gate_up_dx", job=_pair_job(early))
    dh_a, dmix, gr["ln1_g"], gr["ln1_b"] = _rowwise_vjp(
        _make_post_mix(alpha), [sv["h"], sv["mix"]], [sv["g1"], sv["b1"]], [[dh1_a, dh1_b, dh1_c]],
        groups=[[0], [1]], gdtypes=[F32, BF16], name=n + "ln1")
    big["w_out"] = _mm(sv["cat"], dmix, am="km", name=n + "out_proj_dw")
    early.append(_as_pairs(big["w_out"]))
    dcat, their_w_out = _mm(dmix, wts["w_out"], bm="nk", name=n + "out_proj_dx", job=_pair_job(early[-1:]))
    sums = [_pair_add(x, r, core, name=n + "pair_add_" + k)
            for k, x, r in zip(RS_EARLY, early, list(theirs) + list(their_w_out))]

    (dqr, dkv, dkf), early_quads = _attn_bwd(sv["qr"], sv["kf"], sv["kvb"], dcat, sv["o_c"], sv["lse"],
                                             name=n + "attn", job=_quad_job(sums))
    dqpad, dkr = _mla_prep_bwd(dqr, dkf, tables, name=n + "mla_prep")
    big["mla_w_uq"] = _cols_to_blocks(_mm(sv["cqn"], dqpad, am="km", name=n + "uq_dw"), name=n + "uq_dw_blocks")
    dcqn = _mm(dqpad, wts["mla_w_uq"], bm="nk", name=n + "uq_dx")
    big["mla_w_ukv"] = _cols_to_blocks(_mm(sv["ckvn"], dkv, am="km", name=n + "ukv_dw"), name=n + "ukv_dw_blocks")
    dckvn = _mm(dkv, wts["mla_w_ukv"], bm="nk", name=n + "ukv_dx")
    projp = sv["projp"]
    dcq, gr["mla_q_norm_g"] = _rowwise_vjp(_fn_rms, [(projp, 384, P_CQ // 384)], [sv["qg"]], [[dcqn]],
                                           groups=[[0]], gdtypes=[BF16], name=n + "q_norm")
    dckv, gr["mla_kv_norm_g"] = _rowwise_vjp(_fn_rms, [(projp, 256, P_CKV // 256)], [sv["kvg"]], [[dckvn]],
                                             groups=[[0]], gdtypes=[BF16], name=n + "kv_norm")
    dsgu, gr["sgu_ln_g"], gr["sgu_ln_b"], gr["sgu_w_s"], gr["sgu_b_s"] = _sgu_bwd(
        projp, sv["lg"], sv["lbias"], sv["w_s"], sv["bias_full"], dcat, name=n + "sgu")
    dhg, gr["hgrn_norm_g"], gr["lower_bound"] = _hgrn_bwd(
        projp, lbs[li], sv["ng"], sv["o_pre"], sv["states"], dcat, name=n + "hgrn")
    dprojp = jnp.concatenate([dhg, dsgu, dcq, dkr, dckv], axis=1)
    big["w_in"] = _unplace_w_in(_mm(sv["hb"], dprojp, am="km", name=n + "proj_dw"), name=n + "proj_dw_shards")
    late = [_as_pairs(big[k]) for k in RS_LATE]
    dh_b, theirs = _mm(dprojp, wts["w_in"], bm="nk", name=n + "proj_dx", job=_pair_job(late))
    late_sums = [_pair_add(x, r, core, name=n + "pair_add_" + k) for k, x, r in zip(RS_LATE, late, theirs)]
    return [dh_a, dh_b], gr, early_quads, late_sums, carried_got


def kernel(x, p, positions, ln_in_g, ln_in_b, w_in, hgrn_lb_logits, hgrn_norm_g, sgu_ln_g, sgu_ln_b, sgu_w_s, sgu_b_s, mla_q_norm_g, mla_w_uq, mla_kv_norm_g, mla_w_ukv, w_out, ln1_g, ln1_b, w_gate_up, w_down, ple_w_gate, ple_w_proj, ln2_g, ln2_b, loss_target, m_ln_in_g, m_ln_in_b, m_w_in, m_hgrn_lb_logits, m_hgrn_norm_g, m_sgu_ln_g, m_sgu_ln_b, m_sgu_w_s, m_sgu_b_s, m_mla_q_norm_g, m_mla_w_uq, m_mla_kv_norm_g, m_mla_w_ukv, m_w_out, m_ln1_g, m_ln1_b, m_w_gate_up, m_w_down, m_ple_w_gate, m_ple_w_proj, m_ln2_g, m_ln2_b, v_ln_in_g, v_ln_in_b, v_w_in, v_hgrn_lb_logits, v_hgrn_norm_g, v_sgu_ln_g, v_sgu_ln_b, v_sgu_w_s, v_sgu_b_s, v_mla_q_norm_g, v_mla_w_uq, v_mla_kv_norm_g, v_mla_w_ukv, v_w_out, v_ln1_g, v_ln1_b, v_w_gate_up, v_w_down, v_ple_w_gate, v_ple_w_proj, v_ln2_g, v_ln2_b):
    args = dict(locals())
    w = {k: args[k] for k in ORDER}
    m = {k: args["m_" + k] for k in ORDER}
    v = {k: args["v_" + k] for k in ORDER}
    depth = w_in.shape[0]
    assert depth == 2, "the lower-bound kernel is written for two layers"
    alpha = (2 * depth) ** 0.25
    xs, tgt = x[0], loss_target[0]
    d_model = xs.shape[1]

    shards = [_weight_shards(w, li) for li in range(depth)]
    on_hgrn0 = ("mla_w_uq", "mla_w_ukv", "w_out", "ple_w_gate", "ple_w_proj")
    ffn0 = ("w_gate_up", "w_down")
    first1 = ("w_in", "mla_w_uq", "mla_w_ukv", "w_out")
    on_attn1 = ("w_gate_up", "w_down", "ple_w_gate", "ple_w_proj")
    layer1_first = {}

    def after_hgrn0(got):
        got = _gather_forward(got, name="gather_l0a_forward")
        return _usable_weights(dict(zip(on_hgrn0, got)), name="l0")

    def after_attn0(got):
        got = _gather_forward(got, name="gather_l0b_forward")
        layer1_first.update(_usable_weights(dict(zip(first1, got[len(ffn0):])), name="l1"))
        return _usable_weights(dict(zip(ffn0, got[:len(ffn0)])), name="l0")

    def after_attn1(got):
        got = _gather_forward(got, name="gather_l1_forward")
        return _usable_weights(dict(zip(on_attn1, got)), name="l1")

    tables = _rope_tables(positions[0])
    row1 = lambda a: a.reshape(1, -1)
    l0, l1 = row1(hgrn_lb_logits[0]), row1(hgrn_lb_logits[1])
    lbs = _rowwise(_fn_lower_bounds, [l0, l1], [], [(HG_W, F32), (HG_W, F32)], name="lower_bounds")

    gin, bin_ = row1(ln_in_g), row1(ln_in_b)
    (h, hb), g_in = _rowwise(_twice(_fn_ln), [xs], [gin, bin_], [(d_model, F32), (d_model, BF16)], name="ln_in",
                             job=_gather_job([shards[0]["w_in"]]))
    w_in0 = _usable_weights({"w_in": _gather_forward(g_in, name="gather_l0_w_in_forward")[0]}, name="l0")
    (h, hb), sv0 = _layer_forward(
        0, h, hb, p[0, 0], w_in0, w, lbs, tables, alpha,
        hgrn_job=_gather_job([shards[0][k] for k in on_hgrn0]), after_hgrn=after_hgrn0,
        attn_job=_gather_job([shards[0][k] for k in ffn0] + [shards[1][k] for k in first1]), after_attn=after_attn0)
    (h, _), sv1 = _layer_forward(
        1, h, hb, p[1, 0], layer1_first, w, lbs, tables, alpha,
        attn_job=_gather_job([shards[1][k] for k in on_attn1]), after_attn=after_attn1)
    saved = [sv0, sv1]
    dy, loss_local = _loss_and_grad(h, tgt, name="loss")
    loss = lax.psum(loss_local[0, 0], ("x", "y", "c"))

    core = lax.axis_index("c").astype(jnp.int32).reshape(1)
    dparts, grads, quads, carried = [dy], [None] * depth, [None] * depth, None
    for li in reversed(range(depth)):
        dparts, grads[li], early_quads, late_sums, late_quads = _layer_backward(
            li, dparts, p[li, 0], saved[li], lbs, tables, alpha, core, carried=carried)
        quads[li] = dict(zip(RS_EARLY, early_quads))
        if carried is not None:
            quads[li + 1].update(zip(RS_LATE, late_quads))
        carried = _quad_job(late_sums)
    (dx, d_gin, d_bin), late_quads = _rowwise_vjp(_fn_ln, [xs], [gin, bin_], [dparts], groups=[[0]], name="ln_in_b",
                                                   job=carried)
    quads[0].update(zip(RS_LATE, late_quads))
    dl0, dl1 = _rowwise_vjp(_fn_lower_bounds, [l0, l1], [], [[grads[0]["lower_bound"]], [grads[1]["lower_bound"]]],
                            groups=[[0], [1]], name="lower_bounds_b")

    prefixes = ("grad_", "delta_", "new_m_", "new_v_")
    per_layer = {pre + k: [] for pre in prefixes for k in BIG}
    uq_pad = ((0, 0), (0, 0), (0, LANES - ATT_D))
    state = {k: ((jnp.pad(w[k], uq_pad), jnp.pad(m[k], uq_pad), jnp.pad(v[k], uq_pad)) if k == "mla_w_uq"
                 else (w[k], m[k], v[k])) for k in BIG}
    for li in range(depth):
        for k in BIG:
            res4 = _adamw(quads[li][k], *state[k], li, name=f"adamw_l{li}_{k}")
            for pre, a in zip(prefixes, res4):
                per_layer[pre + k].append(a[:, :ATT_D] if k == "mla_w_uq" else a)
    out = {name: jnp.stack(vals) for name, vals in per_layer.items()}

    small_g = {"ln_in_g": d_gin.reshape(-1), "ln_in_b": d_bin.reshape(-1),
               "hgrn_lb_logits": jnp.stack([dl0.reshape(-1), dl1.reshape(-1)])}
    for k in SMALL[3:]:
        small_g[k] = jnp.stack([grads[li][k].reshape(w[k].shape[1:]) for li in range(depth)])
    (small_parts,) = _all_gather([_pack([small_g[k] for k in SMALL])], name="gather_small_grads")
    slabs = _adamw(small_parts, _pack([w[k] for k in SMALL]), _pack([m[k] for k in SMALL]),
                   _pack([v[k] for k in SMALL]), None, name="adamw_small")
    shapes = [w[k].shape for k in SMALL]
    for pre, slab in zip(prefixes, slabs):
        for k, a in zip(SMALL, _unpack(slab, shapes)):
            out[pre + k] = a
    res = [loss, dx[None]]
    for prefix in ("grad_", "delta_", "new_m_", "new_v_"):
        res += [out[prefix + k] for k in ORDER]
    return tuple(res)
```

```python
import functools
import math

import jax
import jax.numpy as jnp
import numpy as np
from jax import lax
from jax.experimental import pallas as pl
from jax.experimental.pallas import tpu as pltpu

F32 = jnp.float32
BF16 = jnp.bfloat16
MESH = pl.DeviceIdType.MESH

LN_EPS = 1e-5
RMS_EPS = 1e-6
ROPE_THETA = 10000.0
ADAM_LR, ADAM_B1, ADAM_B2, ADAM_EPS, ADAM_WD, ADAM_STEP = 0.001, 0.9, 0.999, 1e-08, 0.01, 10

N_DEV = 8
LANES = 128
HG_CHUNK = 16
HG_W = 256
HEAD = 64
SGU_CHUNK = 128
N_ATT_HEADS = 8
ATT_D = 96
VMEM_LIMIT = 56 * 1024 * 1024

HG_TILE = 256
ATT_TQ = 512
ROW_TILE = 256

P_CQ, P_KR, P_CKV, P_COLS = 1536, 1920, 2048, 2304


def _cparams(sem):
    return pltpu.CompilerParams(dimension_semantics=sem, vmem_limit_bytes=VMEM_LIMIT)


_ANY = pl.BlockSpec(memory_space=pl.ANY)


def _call(body, operands, *, name, grid, in_specs, out_specs, out_shape, sem, scratch_shapes=(), job=None):
    if job is None:
        return pl.pallas_call(body, name=name, grid=grid, in_specs=in_specs, out_specs=out_specs, out_shape=out_shape,
                              scratch_shapes=list(scratch_shapes), compiler_params=_cparams(sem))(*operands)
    single = not isinstance(out_shape, (list, tuple))
    shapes = [out_shape] if single else list(out_shape)
    ospecs = [out_specs] if single else list(out_specs)
    ni, no, ns = len(operands), len(shapes), len(scratch_shapes)
    ji, jo = len(job.inputs), len(job.out_shapes)

    def hosted(*refs):
        p = 0
        parts = []
        for cnt in (ni, ji, no, jo, ns):
            parts.append(refs[p:p + cnt])
            p += cnt
        ins, jins, outs, jouts, scr = parts
        jsems = refs[p:]
        ids = [pl.program_id(a) for a in range(len(grid))]
        first = functools.reduce(lambda a, b: a & b, [i == 0 for i in ids])
        last = functools.reduce(lambda a, b: a & b, [i == g - 1 for i, g in zip(ids, grid)])

        @pl.when(first)
        def _():
            job.start(jins, jouts, jsems)

        body(*ins, *outs, *scr)

        @pl.when(last)
        def _():
            job.finish(jins, jouts, jsems)

    res = pl.pallas_call(
        hosted, name=name, grid=grid,
        in_specs=list(in_specs) + [_ANY] * ji, out_specs=ospecs + [_ANY] * jo,
        out_shape=shapes + list(job.out_shapes),
        scratch_shapes=list(scratch_shapes) + [pltpu.SemaphoreType.DMA((c,)) for c in job.sem_counts],
        compiler_params=_cparams(("arbitrary",) * len(grid)),
    )(*operands, *job.inputs)
    own = res[0] if single else res[:no]
    return own, res[no:]


class _Job:
    def __init__(self, inputs, out_shapes, sem_counts, start, finish):
        self.inputs, self.out_shapes, self.sem_counts = list(inputs), list(out_shapes), list(sem_counts)
        self.start, self.finish = start, finish


def _copies_job(inputs, out_shapes, n_remote, n_local, make):
    def start(jins, jouts, sems):
        sends, _, local = make(jins, jouts, *sems)
        for cp in local + sends:
            cp.start()

    def finish(jins, jouts, sems):
        sends, recvs, local = make(jins, jouts, *sems)
        for cp in recvs:
            cp.wait_recv()
        for cp in sends:
            cp.wait_send()
        for cp in local:
            cp.wait()

    return _Job(inputs, out_shapes, [n_remote, n_remote, max(n_local, 1)], start, finish)


def _run_job(job, *, name):
    ji, jo = len(job.inputs), len(job.out_shapes)

    def body(*refs):
        jins, jouts, sems = refs[:ji], refs[ji:ji + jo], refs[ji + jo:]
        job.start(jins, jouts, sems)
        job.finish(jins, jouts, sems)

    return pl.pallas_call(
        body, name=name, out_shape=list(job.out_shapes), in_specs=[_ANY] * ji, out_specs=[_ANY] * jo,
        scratch_shapes=[pltpu.SemaphoreType.DMA((c,)) for c in job.sem_counts],
    )(*job.inputs)


def _tile(n, pref):
    if n % pref == 0:
        return pref
    best = None
    t = LANES
    while t <= min(n, pref):
        if n % t == 0:
            best = t
        t += LANES
    return best if best is not None else n


def _mm(a, b, *, am="mk", bm="kn", om="mn", out_dtype=F32, tm=1024, tn=1024, tk=1024, name, job=None):
    if am == "mk":
        m, k = a.shape
    elif am == "km":
        k, m = a.shape
    elif am == "bmk":
        m, tk = a.shape[1], a.shape[2]
        k = a.shape[0] * tk
    else:
        k, tm = a.shape[1], a.shape[2]
        m = a.shape[0] * tm
    if bm == "kn":
        kb_, n = b.shape
    elif bm == "nk":
        n, kb_ = b.shape
    elif bm == "bkn":
        kb_, tn = b.shape[1], b.shape[2]
        n = b.shape[0] * tn
    else:
        n, tk = b.shape[1], b.shape[2]
        kb_ = b.shape[0] * tk
    assert kb_ == k, (a.shape, b.shape, am, bm)
    tm, tn, tk = _tile(m, tm), _tile(n, tn), _tile(k, tk)
    nk = k // tk
    dims = (((0 if am in ("km", "bkm") else 1,), (1 if bm in ("nk", "bnk") else 0,)), ((), ()))

    a_spec = {"mk": pl.BlockSpec((tm, tk), lambda i, j, kk: (i, kk)),
              "km": pl.BlockSpec((tk, tm), lambda i, j, kk: (kk, i)),
              "bmk": pl.BlockSpec((None, tm, tk), lambda i, j, kk: (kk, i, 0)),
              "bkm": pl.BlockSpec((None, tk, tm), lambda i, j, kk: (i, kk, 0))}[am]
    b_spec = {"kn": pl.BlockSpec((tk, tn), lambda i, j, kk: (kk, j)),
              "nk": pl.BlockSpec((tn, tk), lambda i, j, kk: (j, kk)),
              "bkn": pl.BlockSpec((None, tk, tn), lambda i, j, kk: (j, kk, 0)),
              "bnk": pl.BlockSpec((None, tn, tk), lambda i, j, kk: (kk, j, 0))}[bm]
    if om == "mn":
        o_spec, o_shape = pl.BlockSpec((tm, tn), lambda i, j, kk: (i, j)), (m, n)
    else:
        o_spec, o_shape = pl.BlockSpec((None, tm, tn), lambda i, j, kk: (j, i, 0)), (n // tn, m, tn)

    def body(a_ref, b_ref, o_ref, *acc):
        kk = pl.program_id(2)
        prod = lax.dot_general(a_ref[...].astype(BF16), b_ref[...].astype(BF16), dims, preferred_element_type=F32)
        if nk == 1:
            o_ref[...] = prod.astype(o_ref.dtype)
            return
        acc_ref, = acc

        @pl.when(kk == 0)
        def _():
            acc_ref[...] = prod

        if nk > 2:
            @pl.when((kk > 0) & (kk < nk - 1))
            def _():
                acc_ref[...] += prod

        @pl.when(kk == nk - 1)
        def _():
            o_ref[...] = (acc_ref[...] + prod).astype(o_ref.dtype)

    return _call(body, (a, b), name=name, grid=(m // tm, n // tn, nk), in_specs=[a_spec, b_spec], out_specs=o_spec,
                 out_shape=jax.ShapeDtypeStruct(o_shape, out_dtype),
                 scratch_shapes=[pltpu.VMEM((tm, tn), F32)] if nk > 1 else [],
                 sem=("parallel", "parallel", "arbitrary"), job=job)


def _row_operand(a, tile):
    if isinstance(a, tuple):
        arr, w, j = a
        return arr, pl.BlockSpec((tile, w), lambda i, j=j: (i, j))
    return a, pl.BlockSpec((tile, a.shape[1]), lambda i: (i, 0))


def _const_spec(c):
    nd = c.ndim
    return pl.BlockSpec(c.shape, lambda i, nd=nd: (0,) * nd)


def _rowwise(fn, rows, consts, outs, *, name, accs=(), tile=None, job=None):
    t_rows = (rows[0][0] if isinstance(rows[0], tuple) else rows[0]).shape[0]
    tile = min(tile or ROW_TILE, t_rows)
    arrs, specs = zip(*[_row_operand(a, tile) for a in rows])
    nin, no = len(rows) + len(consts), len(outs)

    def body(*refs):
        res = fn(*[r[...] for r in refs[:nin]])
        for r, v in zip(refs[nin:nin + no], res[:no]):
            r[...] = v.astype(r.dtype)
        if accs:
            a_refs = refs[nin + no:]

            @pl.when(pl.program_id(0) == 0)
            def _():
                for r in a_refs:
                    r[...] = jnp.zeros_like(r)

            for r, v in zip(a_refs, res[no:]):
                r[...] += v

    out_shape = [jax.ShapeDtypeStruct((t_rows, w), dt) for w, dt in outs]
    out_shape += [jax.ShapeDtypeStruct(s, F32) for s in accs]
    out_specs = [pl.BlockSpec((tile, w), lambda i: (i, 0)) for w, _ in outs]
    out_specs += [pl.BlockSpec(s, lambda i, nd=len(s): (0,) * nd) for s in accs]
    return _call(body, (*arrs, *consts), name=name, grid=(t_rows // tile,),
                 in_specs=list(specs) + [_const_spec(c) for c in consts],
                 out_specs=out_specs, out_shape=out_shape, sem=("arbitrary",), job=job)


def _rowwise_vjp(fn, rows, consts, cts, *, name, groups, tile=None, gdtypes=None, job=None):
    t_rows = (rows[0][0] if isinstance(rows[0], tuple) else rows[0]).shape[0]
    tile = min(tile or ROW_TILE, t_rows)
    arrs, specs = zip(*[_row_operand(a, tile) for a in rows])
    flat_cts = [c for group in cts for c in group]
    ct_arrs, ct_specs = zip(*[_row_operand(a, tile) for a in flat_cts])
    nr, nc, nct, ng = len(rows), len(consts), len(flat_cts), len(groups)

    def width(a):
        return a[1] if isinstance(a, tuple) else a.shape[1]

    def body(*refs):
        rv = [r[...].astype(F32) for r in refs[:nr]]
        cv = [r[...] for r in refs[nr:nr + nc]]
        ct_refs = refs[nr + nc:nr + nc + nct]
        ctv, pos = [], 0
        for group in cts:
            s = ct_refs[pos][...].astype(F32)
            for r in ct_refs[pos + 1:pos + len(group)]:
                s = s + r[...].astype(F32)
            ctv.append(s)
            pos += len(group)
        _, pull = jax.vjp(fn, *rv, *cv)
        grads = pull(tuple(ctv))
        g_refs = refs[nr + nc + nct:nr + nc + nct + ng]
        for r, idx in zip(g_refs, groups):
            parts = [grads[i] for i in idx]
            r[...] = (parts[0] if len(parts) == 1 else jnp.concatenate(parts, axis=1)).astype(r.dtype)
        c_refs = refs[nr + nc + nct + ng:]

        @pl.when(pl.program_id(0) == 0)
        def _():
            for r in c_refs:
                r[...] = jnp.zeros_like(r)

        for r, v in zip(c_refs, grads[nr:]):
            r[...] += v

    gw = [sum(width(rows[i]) for i in idx) for idx in groups]
    gdtypes = gdtypes or [F32] * ng
    out_shape = [jax.ShapeDtypeStruct((t_rows, w), dt) for w, dt in zip(gw, gdtypes)]
    out_shape += [jax.ShapeDtypeStruct(c.shape, F32) for c in consts]
    out_specs = [pl.BlockSpec((tile, w), lambda i: (i, 0)) for w in gw]
    out_specs += [_const_spec(c) for c in consts]
    return _call(body, (*arrs, *consts, *ct_arrs), name=name, grid=(t_rows // tile,),
                 in_specs=list(specs) + [_const_spec(c) for c in consts] + list(ct_specs),
                 out_specs=out_specs, out_shape=out_shape, sem=("arbitrary",), job=job)


def _layer_norm(x, g, b):
    mu = jnp.mean(x, axis=-1, keepdims=True)
    xc = x - mu
    var = jnp.mean(xc * xc, axis=-1, keepdims=True)
    return xc * lax.rsqrt(var + LN_EPS) * g + b


def _sigmoid(x):
    return 1.0 / (1.0 + jnp.exp(-x))


def _fn_ln(x, g, b):
    return (_layer_norm(x, g, b),)


def _fn_rms(x, g):
    return (x * lax.rsqrt(jnp.mean(x * x, axis=-1, keepdims=True) + RMS_EPS) * g,)


def _make_post_mix(alpha):
    def fn(h, mix, g, b):
        return (_layer_norm(alpha * h + mix, g, b),)
    return fn


def _make_ple_ln(alpha):
    def fn(h1, ffn, pg, pp, g, b):
        return (_layer_norm(alpha * h1 + ffn + _sigmoid(pg) * pp, g, b),)
    return fn


def _fn_lower_bounds(l0, l1):
    m = jnp.maximum(l0, l1)
    e0, e1 = jnp.exp(l0 - m), jnp.exp(l1 - m)
    s = e0 + e1
    p0, p1 = e0 / s, e1 / s
    return (p0 - p0, (p0 + p1) - p0)


def _loss_and_grad(y, target, *, name):
    d = y.shape[1]

    def fn(yv, tv):
        err = yv - tv
        return err * (1.0 / d), 0.5 * jnp.sum(jnp.mean(err * err, axis=-1, keepdims=True), axis=0, keepdims=True)

    return _rowwise(fn, [y, target], [], [(d, F32)], accs=[(1, 1)], name=name)


def _split_dot(x, e_bf16):
    hi = x.astype(BF16)
    lo = (x - hi.astype(F32)).astype(BF16)
    return (jnp.dot(hi, e_bf16, preferred_element_type=F32) + jnp.dot(lo, e_bf16, preferred_element_type=F32))


def _hgrn_common(th):
    rm = lax.broadcasted_iota(jnp.int32, (th, HG_W), 0) % HG_CHUNK

    def seg_cumsum(x):
        for s in (1, 2, 4, 8):
            x = x + jnp.where(rm >= s, pltpu.roll(x, s, 0), 0.0)
        return x

    def seg_rcumsum(x):
        for s in (1, 2, 4, 8):
            x = x + jnp.where(rm < HG_CHUNK - s, pltpu.roll(x, th - s, 0), 0.0)
        return x

    ri = lax.broadcasted_iota(jnp.int32, (HG_W, HG_W), 0) // HEAD
    ci = lax.broadcasted_iota(jnp.int32, (HG_W, HG_W), 1) // HEAD
    head_f32 = (ri == ci).astype(F32)
    head_bf16 = head_f32.astype(BF16)

    def headsum(x, pieces=2):
        if pieces == 1:
            return jnp.dot(x.astype(BF16), head_bf16, preferred_element_type=F32)
        return _split_dot(x, head_bf16)

    return rm, seg_cumsum, seg_rcumsum, head_f32, headsum


def _hgrn_gates(qr, fl, lb):
    sg = _sigmoid(fl)
    f = lb + (1.0 - lb) * sg
    sq = _sigmoid(qr)
    return sg, f, jnp.log(f), 1.0 - f, qr * sq, sq


def _shifted(x, d, th):
    return x if d == 0 else pltpu.roll(x, d, 0)


def _unshift(x, d, th):
    return x if d == 0 else pltpu.roll(x, th - d, 0)


def _hgrn_fwd(projp, lb, ng, *, name, job=None):
    t_rows = projp.shape[0]
    th = min(HG_TILE, t_rows)
    nct = th // HG_CHUNK

    def body(q_ref, f_ref, i_ref, g_ref, lb_ref, ng_ref, oa_ref, opre_ref, st_out_ref,
             st_ref, vtm_ref, kv_ref, qe_ref, dec_ref, oint_ref):
        rm, seg_cumsum, seg_rcumsum, head_f32, headsum = _hgrn_common(th)

        @pl.when(pl.program_id(0) == 0)
        def _():
            st_ref[...] = jnp.zeros_like(st_ref)

        qr, fl, v, g = q_ref[...], f_ref[...], i_ref[...], g_ref[...]
        _, f, lf, k, q, _ = _hgrn_gates(qr, fl, lb_ref[...])
        b = seg_cumsum(lf)

        o = jnp.zeros((th, HG_W), F32)
        for d in range(HG_CHUNK):
            kd, bd, vd = _shifted(k, d, th), _shifted(b, d, th), _shifted(v, d, th)
            e = jnp.exp(jnp.where(rm >= d, b - bd, -1e30))
            o = o + headsum(q * kd * e, 1) * vd

        blast = seg_rcumsum(jnp.where(rm == HG_CHUNK - 1, b, 0.0))
        kte = (k * jnp.exp(blast - b)).astype(BF16)
        qe_ref[...] = q * jnp.exp(b)
        dec_ref[...] = jnp.exp(blast)
        vt = v.T
        lane_chunk = lax.broadcasted_iota(jnp.int32, (HG_W, th), 1) // HG_CHUNK
        for c in range(nct):
            vtm_ref[c * HG_W:(c + 1) * HG_W, :] = jnp.where(lane_chunk == c, vt, 0.0).astype(BF16)
        kv_ref[...] = jnp.dot(vtm_ref[...], kte, preferred_element_type=F32)

        s = st_ref[...]
        for c in range(nct):
            rows = slice(c * HG_CHUNK, (c + 1) * HG_CHUNK)
            st_out_ref[c] = s
            oint_ref[rows, :] = lax.dot_general(qe_ref[rows, :].astype(BF16), s.astype(BF16),
                                                (((1,), (1,)), ((), ())), preferred_element_type=F32)
            dec = jnp.max(dec_ref[rows, :], axis=0, keepdims=True)
            s = s * dec + kv_ref[c * HG_W:(c + 1) * HG_W, :] * head_f32
        st_ref[...] = s

        o = o + oint_ref[...]
        opre_ref[...] = o
        r = lax.rsqrt(headsum(o * o) * (1.0 / HEAD) + RMS_EPS)
        oa_ref[...] = (o * r * ng_ref[...] * (g * _sigmoid(g))).astype(oa_ref.dtype)

    col = lambda j: pl.BlockSpec((th, HG_W), lambda i, j=j: (i, j))
    vec = pl.BlockSpec((1, HG_W), lambda i: (0, 0))
    row = pl.BlockSpec((th, HG_W), lambda i: (i, 0))
    n_chunks = t_rows // HG_CHUNK
    return _call(
        body, (projp, projp, projp, projp, lb, ng), name=name, grid=(t_rows // th,),
        in_specs=[col(0), col(1), col(2), col(3), vec, vec],
        out_specs=[row, row, pl.BlockSpec((nct, HG_W, HG_W), lambda i: (i, 0, 0))],
        out_shape=[jax.ShapeDtypeStruct((t_rows, HG_W), BF16), jax.ShapeDtypeStruct((t_rows, HG_W), F32),
                   jax.ShapeDtypeStruct((n_chunks, HG_W, HG_W), F32)],
        scratch_shapes=[pltpu.VMEM((HG_W, HG_W), F32), pltpu.VMEM((nct * HG_W, th), BF16),
                        pltpu.VMEM((nct * HG_W, HG_W), F32), pltpu.VMEM((th, HG_W), F32),
                        pltpu.VMEM((th, HG_W), F32), pltpu.VMEM((th, HG_W), F32)],
        sem=("arbitrary",), job=job)


def _hgrn_bwd(projp, lb, ng, opre, states, dcat, *, name):
    t_rows = projp.shape[0]
    th = min(HG_TILE, t_rows)
    nct = th // HG_CHUNK
    nt = t_rows // th

    def body(q_ref, f_ref, i_ref, g_ref, lb_ref, ng_ref, opre_ref, st_in_ref, do_ref,
             dproj_ref, dng_ref, dlb_ref,
             gst_ref, dotm_ref, qg_ref, v_ref, kte_ref, dop_ref, dec_ref, dkte_ref, dvi_ref, dqe_ref, ddec_ref):
        rm, seg_cumsum, seg_rcumsum, head_f32, headsum = _hgrn_common(th)

        @pl.when(pl.program_id(0) == 0)
        def _():
            gst_ref[...] = jnp.zeros_like(gst_ref)
            dng_ref[...] = jnp.zeros_like(dng_ref)
            dlb_ref[...] = jnp.zeros_like(dlb_ref)

        qr, fl, v, g = q_ref[...], f_ref[...], i_ref[...], g_ref[...]
        lb, ngv = lb_ref[...], ng_ref[...]
        sg, f, lf, k, q, sq = _hgrn_gates(qr, fl, lb)
        b = seg_cumsum(lf)
        blast = seg_rcumsum(jnp.where(rm == HG_CHUNK - 1, b, 0.0))
        eb = jnp.exp(b)
        ekb = jnp.exp(blast - b)
        qe, kte, dec = q * eb, k * ekb, jnp.exp(blast)

        do_out, op = do_ref[...], opre_ref[...]
        sgg = _sigmoid(g)
        sil = g * sgg
        r = lax.rsqrt(headsum(op * op) * (1.0 / HEAD) + RMS_EPS)
        on = op * r
        dng_ref[...] += jnp.sum(do_out * on * sil, axis=0, keepdims=True)
        dg = do_out * on * ngv * (sgg * (1.0 + g * (1.0 - sgg)))
        don = do_out * ngv * sil
        dop = r * (don - on * (headsum(don * on) * (1.0 / HEAD)))

        v_ref[...] = v
        kte_ref[...] = kte
        dop_ref[...] = dop
        dec_ref[...] = dec
        dot_t = dop.T
        lane_chunk = lax.broadcasted_iota(jnp.int32, (HG_W, th), 1) // HG_CHUNK
        for c in range(nct):
            dotm_ref[c * HG_W:(c + 1) * HG_W, :] = jnp.where(lane_chunk == c, dot_t, 0.0).astype(BF16)
        qg_ref[...] = jnp.dot(dotm_ref[...], qe.astype(BF16), preferred_element_type=F32)

        gs = gst_ref[...]
        for c in reversed(range(nct)):
            rows = slice(c * HG_CHUNK, (c + 1) * HG_CHUNK)
            s = st_in_ref[c]
            gm = (gs * head_f32).astype(BF16)
            dkte_ref[rows, :] = jnp.dot(v_ref[rows, :].astype(BF16), gm, preferred_element_type=F32)
            dvi_ref[rows, :] = lax.dot_general(kte_ref[rows, :].astype(BF16), gm, (((1,), (1,)), ((), ())),
                                               preferred_element_type=F32)
            dqe_ref[rows, :] = jnp.dot(dop_ref[rows, :].astype(BF16), s.astype(BF16), preferred_element_type=F32)
            ddec_ref[rows, :] = jnp.broadcast_to(jnp.sum(gs * s, axis=0, keepdims=True), (HG_CHUNK, HG_W))
            dec_c = jnp.max(dec_ref[rows, :], axis=0, keepdims=True)
            gs = gs * dec_c + qg_ref[c * HG_W:(c + 1) * HG_W, :] * head_f32
        gst_ref[...] = gs

        dkte, dqe = dkte_ref[...], dqe_ref[...]
        dq = dqe * eb
        dk = dkte * ekb
        db = dqe * qe - dkte * kte
        dv = dvi_ref[...]
        dblast = dkte * kte + jnp.where(rm == HG_CHUNK - 1, ddec_ref[...] * dec, 0.0)

        for d in range(HG_CHUNK):
            kd, bd, vd = _shifted(k, d, th), _shifted(b, d, th), _shifted(v, d, th)
            e = jnp.exp(jnp.where(rm >= d, b - bd, -1e30))
            p = q * kd * e
            sc = headsum(p, 1)
            dsc = headsum(dop * vd, 1)
            dv = dv + _unshift(sc * dop, d, th)
            dq = dq + dsc * kd * e
            dk = dk + _unshift(dsc * q * e, d, th)
            darg = dsc * p
            db = db + darg - _unshift(darg, d, th)

        db = db + jnp.where(rm == HG_CHUNK - 1, seg_cumsum(dblast), 0.0)
        dlf = seg_rcumsum(db)
        df = dlf / f - dk
        dlb_ref[...] += jnp.sum(df * (1.0 - sg), axis=0, keepdims=True)
        dfl = df * (1.0 - lb) * sg * (1.0 - sg)
        dqr = dq * (sq * (1.0 + qr * (1.0 - sq)))
        dproj_ref[...] = jnp.concatenate([dqr, dfl, dv, dg], axis=1).astype(dproj_ref.dtype)

    rev = lambda i: nt - 1 - i
    col = lambda j: pl.BlockSpec((th, HG_W), lambda i, j=j: (rev(i), j))
    vec = pl.BlockSpec((1, HG_W), lambda i: (0, 0))
    row = pl.BlockSpec((th, HG_W), lambda i: (rev(i), 0))
    tile_f32 = pltpu.VMEM((th, HG_W), F32)
    return pl.pallas_call(
        body, name=name, grid=(nt,),
        in_specs=[col(0), col(1), col(2), col(3), vec, vec, row,
                  pl.BlockSpec((nct, HG_W, HG_W), lambda i: (rev(i), 0, 0)), col(0)],
        out_specs=[pl.BlockSpec((th, 4 * HG_W), lambda i: (rev(i), 0)), vec, vec],
        out_shape=[jax.ShapeDtypeStruct((t_rows, 4 * HG_W), BF16), jax.ShapeDtypeStruct((1, HG_W), F32),
                   jax.ShapeDtypeStruct((1, HG_W), F32)],
        scratch_shapes=[pltpu.VMEM((HG_W, HG_W), F32), pltpu.VMEM((nct * HG_W, th), BF16),
                        pltpu.VMEM((nct * HG_W, HG_W), F32)] + [tile_f32] * 8,
        compiler_params=_cparams(("arbitrary",)),
    )(projp, projp, projp, projp, lb, ng, opre, states, dcat)


_INV_SQRT2 = 1.0 / math.sqrt(2.0)
_INV_SQRT2PI = 1.0 / math.sqrt(2.0 * math.pi)


def _gelu(x):
    return 0.5 * x * (1.0 + lax.erf(x * _INV_SQRT2))


def _gelu_grad(x):
    return 0.5 * (1.0 + lax.erf(x * _INV_SQRT2)) + x * jnp.exp(-0.5 * x * x) * _INV_SQRT2PI


def _sgu_parts(bu, bv, lg, lbias, w_ref, n_groups):
    c = SGU_CHUNK
    tril = (lax.broadcasted_iota(jnp.int32, (c, c), 0) >= lax.broadcasted_iota(jnp.int32, (c, c), 1)).astype(F32)
    gid = lax.broadcasted_iota(jnp.int32, bu.shape, 1) // HEAD
    u = _gelu(bu)
    gv = _gelu(bv)
    mu = jnp.mean(gv, axis=-1, keepdims=True)
    xc = gv - mu
    rstd = lax.rsqrt(jnp.mean(xc * xc, axis=-1, keepdims=True) + LN_EPS)
    xhat = xc * rstd
    vn = xhat * lg + lbias
    ws = [w_ref[gi] * tril for gi in range(n_groups)]
    return tril, gid, u, rstd, xhat, vn, ws


def _sgu_fwd(projp, lg, lbias, w_s, bias_full, *, name):
    t_rows = projp.shape[0]
    n_groups = w_s.shape[0]
    c = SGU_CHUNK

    def body(u_ref, v_ref, lg_ref, lb_ref, w_ref, bias_ref, o_ref):
        _, gid, u, _, _, vn, ws = _sgu_parts(u_ref[...], v_ref[...], lg_ref[...], lb_ref[...], w_ref, n_groups)
        vnb = vn.astype(BF16)
        z = bias_ref[...]
        for gi in range(n_groups):
            z = z + jnp.where(gid == gi, jnp.dot(ws[gi].astype(BF16), vnb, preferred_element_type=F32), 0.0)
        o_ref[...] = (u * z).astype(o_ref.dtype)

    col = lambda j: pl.BlockSpec((c, HG_W), lambda i, j=j: (i, j))
    return pl.pallas_call(
        body, name=name, grid=(t_rows // c,),
        in_specs=[col(4), col(5), _const_spec(lg), _const_spec(lbias), _const_spec(w_s), _const_spec(bias_full)],
        out_specs=pl.BlockSpec((c, HG_W), lambda i: (i, 0)),
        out_shape=jax.ShapeDtypeStruct((t_rows, HG_W), BF16),
        compiler_params=_cparams(("arbitrary",)),
    )(projp, projp, lg, lbias, w_s, bias_full)


def _sgu_bwd(projp, lg, lbias, w_s, bias_full, dcat, *, name):
    t_rows = projp.shape[0]
    n_groups = w_s.shape[0]
    c = SGU_CHUNK
    n = t_rows // c

    def body(u_ref, v_ref, lg_ref, lb_ref, w_ref, bias_ref, do_ref,
             dproj_ref, dlg_ref, dlb_ref, dw_ref, dbs_ref, dbias_acc):
        i = pl.program_id(0)

        @pl.when(i == 0)
        def _():
            dlg_ref[...] = jnp.zeros_like(dlg_ref)
            dlb_ref[...] = jnp.zeros_like(dlb_ref)
            dw_ref[...] = jnp.zeros_like(dw_ref)
            dbias_acc[...] = jnp.zeros_like(dbias_acc)

        bu, bv, lg_v = u_ref[...], v_ref[...], lg_ref[...]
        tril, gid, u, rstd, xhat, vn, ws = _sgu_parts(bu, bv, lg_v, lb_ref[...], w_ref, n_groups)
        vnb = vn.astype(BF16)
        z = bias_ref[...]
        for gi in range(n_groups):
            z = z + jnp.where(gid == gi, jnp.dot(ws[gi].astype(BF16), vnb, preferred_element_type=F32), 0.0)
        do = do_ref[...]
        dbu = do * z * _gelu_grad(bu)
        dz = do * u
        dbias_acc[...] += dz
        dvn = jnp.zeros_like(dz)
        for gi in range(n_groups):
            dzg = jnp.where(gid == gi, dz, 0.0).astype(BF16)
            dw_ref[gi] += lax.dot_general(dzg, vnb, (((1,), (1,)), ((), ())), preferred_element_type=F32) * tril
            dvn = dvn + jnp.dot(ws[gi].T.astype(BF16), dzg, preferred_element_type=F32)
        dlg_ref[...] += jnp.sum(dvn * xhat, axis=0, keepdims=True)
        dlb_ref[...] += jnp.sum(dvn, axis=0, keepdims=True)
        dxh = dvn * lg_v
        dgv = rstd * (dxh - jnp.mean(dxh, axis=-1, keepdims=True)
                      - xhat * jnp.mean(dxh * xhat, axis=-1, keepdims=True))
        dproj_ref[...] = jnp.concatenate([dbu, dgv * _gelu_grad(bv)], axis=1).astype(dproj_ref.dtype)

        @pl.when(i == n - 1)
        def _():
            dbs_ref[...] = jnp.sum(dbias_acc[...].T.reshape(n_groups, HEAD, c), axis=1)

    col = lambda j: pl.BlockSpec((c, HG_W), lambda i, j=j: (i, j))
    return pl.pallas_call(
        body, name=name, grid=(n,),
        in_specs=[col(4), col(5), _const_spec(lg), _const_spec(lbias), _const_spec(w_s), _const_spec(bias_full),
                  col(1)],
        out_specs=[pl.BlockSpec((c, 2 * HG_W), lambda i: (i, 0)), _const_spec(lg), _const_spec(lbias),
                   _const_spec(w_s), pl.BlockSpec((n_groups, c), lambda i: (0, 0))],
        out_shape=[jax.ShapeDtypeStruct((t_rows, 2 * HG_W), BF16), jax.ShapeDtypeStruct(lg.shape, F32),
                   jax.ShapeDtypeStruct(lbias.shape, F32), jax.ShapeDtypeStruct(w_s.shape, F32),
                   jax.ShapeDtypeStruct((n_groups, c), F32)],
        scratch_shapes=[pltpu.VMEM((c, HG_W), F32)],
        compiler_params=_cparams(("arbitrary",)),
    )(projp, projp, lg, lbias, w_s, bias_full, dcat)


def _rope_tables(positions):
    t = positions.shape[0]
    inv_freq = ROPE_THETA ** (-jnp.arange(0, 32, 2, dtype=F32) / 32)
    ang = positions.astype(F32)[:, None] * inv_freq
    cos, sin = jnp.cos(ang), jnp.sin(ang)
    z = lambda w: jnp.zeros((t, w), F32)
    cos_t = jnp.concatenate([jnp.ones((t, 64), F32), cos, cos, z(32)], axis=1)
    sin_up = jnp.concatenate([z(80), sin, z(32)], axis=1)
    sin_dn = jnp.concatenate([z(64), -sin, z(48)], axis=1)
    return cos_t, sin_up, sin_dn


def _rep(x, n):
    return x if n == 1 else jnp.concatenate([x] * n, axis=1)


def _rope(x, cos_t, sin_up, sin_dn):
    w = x.shape[1]
    return x * cos_t + pltpu.roll(x, 16, 1) * sin_up + pltpu.roll(x, w - 16, 1) * sin_dn


def _rope_t(dy, cos_t, sin_up, sin_dn):
    w = dy.shape[1]
    return dy * cos_t + pltpu.roll(dy * sin_up, w - 16, 1) + pltpu.roll(dy * sin_dn, 16, 1)


def _mla_prep(q, kv, projp, tables, *, name):
    nh = N_ATT_HEADS

    def fn(qv, kvv, kr, cos_t, sin_up, sin_dn):
        qr = _rope(qv, _rep(cos_t, nh), _rep(sin_up, nh), _rep(sin_dn, nh))
        krr = _rope(kr, cos_t, sin_up, sin_dn)
        lane = lax.broadcasted_iota(jnp.int32, kvv.shape, 1) % LANES
        return qr, jnp.where(lane < HEAD, kvv, 0.0) + _rep(krr, nh), kvv

    w = q.shape[1]
    return _rowwise(fn, [q, kv, (projp, LANES, P_KR // LANES)] + list(tables), [],
                    [(w, BF16), (w, BF16), (w, BF16)], name=name)


def _mla_prep_bwd(dqr, dkf, tables, *, name):
    nh = N_ATT_HEADS

    def fn(dq, dk, cos_t, sin_up, sin_dn):
        dqp = _rope_t(dq, _rep(cos_t, nh), _rep(sin_up, nh), _rep(sin_dn, nh))
        dkrr = dk[:, 0:LANES]
        for h in range(1, nh):
            dkrr = dkrr + dk[:, LANES * h:LANES * (h + 1)]
        return dqp, _rope_t(dkrr, cos_t, sin_up, sin_dn)

    return _rowwise(fn, [dqr, dkf] + list(tables), [], [(dqr.shape[1], BF16), (LANES, BF16)], name=name)


_LOG2E = 1.0 / math.log(2.0)
_NT = (((1,), (1,)), ((), ()))
_TN = (((0,), (0,)), ((), ()))


def _attn_fwd(qr, kf, kvb, *, name, job=None):
    t_rows = qr.shape[0]
    tq = min(ATT_TQ, t_rows)
    nb = t_rows // tq
    scale = ATT_D ** -0.5

    c2 = scale * _LOG2E

    def body(q_ref, kf_ref, kvb_ref, o_ref, lse_ref):
        qi = pl.program_id(1)
        lane = lax.broadcasted_iota(jnp.int32, (tq, LANES), 1)
        causal_t = (lax.broadcasted_iota(jnp.int32, (tq, tq), 0) <= lax.broadcasted_iota(jnp.int32, (tq, tq), 1))
        heads = [slice(hh * LANES, (hh + 1) * LANES) for hh in range(2)]
        qs = [q_ref[:, cols] for cols in heads]

        def block(ki, carry, diagonal):
            rows = pl.ds(pl.multiple_of(ki * tq, tq), tq)
            new = []
            for q, cols, (m_old, l_old, acc_t) in zip(qs, heads, carry):
                s_t = lax.dot_general(kf_ref[rows, cols], q, _NT, preferred_element_type=F32)
                if diagonal:
                    s_t = jnp.where(causal_t, s_t, -1e30)
                m_new = jnp.maximum(m_old, jnp.max(s_t, axis=0, keepdims=True))
                p_t = jnp.exp2((s_t - m_new) * c2)
                a = jnp.exp2((m_old - m_new) * c2)
                pv_t = lax.dot_general(kvb_ref[rows, cols], p_t.astype(BF16), _TN, preferred_element_type=F32)
                new.append((m_new, a * l_old + jnp.sum(p_t, axis=0, keepdims=True), a * acc_t + pv_t))
            return tuple(new)

        init = (jnp.full((1, tq), -1e30, F32), jnp.zeros((1, tq), F32), jnp.zeros((LANES, tq), F32))
        carry = lax.fori_loop(0, qi, lambda ki, c: block(ki, c, False), (init, init))
        outs = []
        for hh, (m_fin, l_fin, acc_t) in enumerate(block(qi, carry, True)):
            lse_ref[hh] = m_fin * scale + jnp.log(l_fin)
            outs.append((acc_t / l_fin).T)
        o_ref[...] = jnp.where(lane < HEAD, pltpu.roll(outs[0], HEAD, 1), outs[1])

    pair = pl.BlockSpec((t_rows, 2 * LANES), lambda pr, qi: (0, pr))
    return _call(
        body, (qr, kf, kvb), name=name, grid=(N_ATT_HEADS // 2, nb),
        in_specs=[pl.BlockSpec((tq, 2 * LANES), lambda pr, qi: (qi, pr)), pair, pair],
        out_specs=[pl.BlockSpec((tq, LANES), lambda pr, qi: (qi, pr)),
                   pl.BlockSpec((2, 1, tq), lambda pr, qi: (pr, 0, qi))],
        out_shape=[jax.ShapeDtypeStruct((t_rows, N_ATT_HEADS * HEAD), F32),
                   jax.ShapeDtypeStruct((N_ATT_HEADS, 1, t_rows), F32)],
        sem=("parallel", "arbitrary"), job=job)


def _attn_bwd(qr, kf, kvb, dcat, o, lse, *, name, job=None):
    t_rows = qr.shape[0]
    tq = min(ATT_TQ, t_rows)
    nb = t_rows // tq
    scale = ATT_D ** -0.5
    c2 = scale * _LOG2E
    do_off = 2 * HG_W // LANES

    def body(q_ref, kf_ref, kvb_ref, do_ref, o_ref, lse_ref, dq_ref, dkv_ref, dk_ref):
        ki = pl.program_id(1)

        @pl.when(ki == 0)
        def _():
            dq_ref[...] = jnp.zeros_like(dq_ref)

        lane = lax.broadcasted_iota(jnp.int32, (tq, LANES), 1)
        causal_t = (lax.broadcasted_iota(jnp.int32, (tq, tq), 0) <= lax.broadcasted_iota(jnp.int32, (tq, tq), 1))
        heads = [slice(hh * LANES, (hh + 1) * LANES) for hh in range(2)]
        ks = [kf_ref[:, cols] for cols in heads]
        vs = [kvb_ref[:, cols] for cols in heads]

        def block(qi, carry, diagonal):
            rows = pl.ds(pl.multiple_of(qi * tq, tq), tq)
            do_pair, o_pair = do_ref[rows, :], o_ref[rows, :]
            new = []
            for hh, (cols, k, v, (dk, dv)) in enumerate(zip(heads, ks, vs, carry)):
                q = q_ref[rows, cols]
                do, ov = (pltpu.roll(do_pair, HEAD, 1), pltpu.roll(o_pair, HEAD, 1)) if hh == 0 else (do_pair, o_pair)
                do = jnp.where(lane >= HEAD, do, 0.0)
                delta = jnp.sum((do * ov).T, axis=0, keepdims=True)
                s_t = lax.dot_general(k, q, _NT, preferred_element_type=F32)
                if diagonal:
                    s_t = jnp.where(causal_t, s_t, -1e30)
                p_t = jnp.exp2(s_t * c2 - lse_ref[hh, :, rows] * _LOG2E)
                dob = do.astype(BF16)
                dv = dv + jnp.dot(p_t.astype(BF16), dob, preferred_element_type=F32)
                dp_t = lax.dot_general(v, dob, _NT, preferred_element_type=F32)
                ds_t = (p_t * (dp_t - delta) * scale).astype(BF16)
                dk = dk + jnp.dot(ds_t, q, preferred_element_type=F32)
                dq_ref[rows, cols] += lax.dot_general(ds_t, k, _TN, preferred_element_type=F32)
                new.append((dk, dv))
            return tuple(new)

        zero = jnp.zeros((tq, LANES), F32)
        carry = block(ki, ((zero, zero), (zero, zero)), True)
        carry = lax.fori_loop(ki + 1, nb, lambda qi, c: block(qi, c, False), carry)
        dkv_ref[...] = jnp.concatenate([jnp.where(lane < HEAD, dk, dv) for dk, dv in carry],
                                       axis=1).astype(dkv_ref.dtype)
        dk_ref[...] = jnp.concatenate([dk for dk, _ in carry], axis=1)

    pair_all = pl.BlockSpec((t_rows, 2 * LANES), lambda pr, ki: (0, pr))
    pair_blk = pl.BlockSpec((tq, 2 * LANES), lambda pr, ki: (ki, pr))
    wide = jax.ShapeDtypeStruct((t_rows, N_ATT_HEADS * LANES), F32)
    return _call(
        body, (qr, kf, kvb, dcat, o, lse), name=name, grid=(N_ATT_HEADS // 2, nb),
        in_specs=[pair_all, pair_blk, pair_blk,
                  pl.BlockSpec((t_rows, LANES), lambda pr, ki: (0, do_off + pr)),
                  pl.BlockSpec((t_rows, LANES), lambda pr, ki: (0, pr)),
                  pl.BlockSpec((2, 1, t_rows), lambda pr, ki: (pr, 0, 0))],
        out_specs=[pair_all, pair_blk, pair_blk],
        out_shape=[wide, jax.ShapeDtypeStruct(wide.shape, BF16), wide],
        sem=("parallel", "arbitrary"), job=job)


def _my_pos():
    return lax.axis_index("x"), lax.axis_index("y"), lax.axis_index("c")


def _all_gather(xs, *, name):
    return _gather_forward(_run_job(_gather_job(xs), name=name), name=name + "_forward")


def _remote(src, dst, send_sems, recv_sems, k, dev):
    return pltpu.make_async_remote_copy(src_ref=src, dst_ref=dst, send_sem=send_sems.at[k], recv_sem=recv_sems.at[k],
                                        device_id=dev, device_id_type=MESH)


def _gather_job(xs):
    n = len(xs)

    def make(x_refs, out_refs, send_sems, recv_sems, local_sems):
        mx, my, mc = _my_pos()
        mine = 4 * mx + 2 * my + mc
        peers = [(mx, my, 1 - mc), (1 - mx, my, mc), (mx, 1 - my, mc), (1 - mx, 1 - my, mc)]
        sends, recvs, local = [], [], []
        for a in range(n):
            local.append(pltpu.make_async_copy(x_refs[a], out_refs[a].at[mine], local_sems.at[a]))
            for k, dev in enumerate(peers):
                theirs = 4 * dev[0] + 2 * dev[1] + dev[2]
                sends.append(_remote(x_refs[a], out_refs[a].at[mine], send_sems, recv_sems, 4 * a + k, dev))
                recvs.append(_remote(x_refs[a], out_refs[a].at[theirs], send_sems, recv_sems, 4 * a + k, dev))
        return sends, recvs, local

    shapes = [jax.ShapeDtypeStruct((N_DEV,) + x.shape, x.dtype) for x in xs]
    return _copies_job(xs, shapes, 4 * n, n, make)


def _gather_forward(gs, *, name):
    n = len(gs)

    def body(*refs):
        out_refs = refs[n:2 * n]
        send_sems, recv_sems = refs[2 * n:]
        mx, my, mc = _my_pos()
        chips = [(1 - mx, my), (mx, 1 - my), (1 - mx, 1 - my)]
        sends, recvs = [], []
        for a in range(n):
            for j, (cx, cy) in enumerate(chips):
                here, there = out_refs[a].at[4 * cx + 2 * cy + mc], out_refs[a].at[4 * cx + 2 * cy + 1 - mc]
                sends.append(_remote(here, here, send_sems, recv_sems, 3 * a + j, (mx, my, 1 - mc)))
                recvs.append(_remote(here, there, send_sems, recv_sems, 3 * a + j, (mx, my, 1 - mc)))
        for cp in sends:
            cp.start()
        for cp in recvs:
            cp.wait_recv()
        for cp in sends:
            cp.wait_send()

    return pl.pallas_call(
        body, name=name, out_shape=[jax.ShapeDtypeStruct(g.shape, g.dtype) for g in gs],
        in_specs=[_ANY] * n, out_specs=[_ANY] * n, input_output_aliases={a: a for a in range(n)},
        scratch_shapes=[pltpu.SemaphoreType.DMA((3 * n,)), pltpu.SemaphoreType.DMA((3 * n,))],
    )(*gs)


def _pair_job(xs):
    n = len(xs)

    def make(x_refs, out_refs, send_sems, recv_sems, local_sems):
        mx, my, mc = _my_pos()
        copies = [_remote(x_refs[a].at[g, 1 - mc], out_refs[a].at[g], send_sems, recv_sems, 4 * a + g, (mx, my, 1 - mc))
                  for a in range(n) for g in range(4)]
        return copies, copies, []

    shapes = [jax.ShapeDtypeStruct((4,) + x.shape[2:], x.dtype) for x in xs]
    return _copies_job(xs, shapes, 4 * n, 0, make)


def _pair_add(x, r, core, *, name):
    _, _, a, b = x.shape
    ta = _row_tile(a, 256)

    def body(c_ref, x_ref, r_ref, o_ref):
        o_ref[...] = (x_ref[...] + r_ref[...]).astype(o_ref.dtype)

    blk = pl.BlockSpec((None, ta, b), lambda g, i, c_ref: (g, i, 0))
    return pl.pallas_call(
        body, name=name,
        grid_spec=pltpu.PrefetchScalarGridSpec(
            num_scalar_prefetch=1, grid=(4, a // ta),
            in_specs=[pl.BlockSpec((None, None, ta, b), lambda g, i, c_ref: (g, c_ref[0], i, 0)), blk],
            out_specs=blk),
        out_shape=jax.ShapeDtypeStruct((4, a, b), BF16),
        compiler_params=_cparams(("parallel", "parallel")),
    )(core, x, r)


def _quad_job(xs):
    n = len(xs)

    def make(x_refs, out_refs, send_sems, recv_sems, local_sems):
        mx, my, mc = _my_pos()
        mine = 2 * mx + my
        peers = [((1 - mx, my, mc), 2 * (1 - mx) + my), ((mx, 1 - my, mc), 2 * mx + 1 - my),
                 ((1 - mx, 1 - my, mc), 2 * (1 - mx) + 1 - my)]
        sends, recvs, local = [], [], []
        for a in range(n):
            local.append(pltpu.make_async_copy(x_refs[a].at[mine], out_refs[a].at[mine], local_sems.at[a]))
            for k, (dev, g) in enumerate(peers):
                sends.append(_remote(x_refs[a].at[g], out_refs[a].at[mine], send_sems, recv_sems, 3 * a + k, dev))
                recvs.append(_remote(x_refs[a].at[g], out_refs[a].at[g], send_sems, recv_sems, 3 * a + k, dev))
        return sends, recvs, local

    shapes = [jax.ShapeDtypeStruct(x.shape, x.dtype) for x in xs]
    return _copies_job(xs, shapes, 3 * n, n, make)


def _row_tile(r, pref):
    t = min(pref, r)
    while r % t or (t % 8 and t != r):
        t -= 1
    return t


def _adamw(parts, w, m, v, layer, *, name, tile=256, into=None):
    g, a, b = parts.shape
    tile = _row_tile(a, tile)
    c1 = 1.0 / (1.0 - ADAM_B1 ** ADAM_STEP)
    c2 = 1.0 / (1.0 - ADAM_B2 ** ADAM_STEP)
    into = tuple(into or ())

    def body(p_ref, w_ref, m_ref, v_ref, *refs):
        g_ref, d_ref, mo_ref, vo_ref = refs[len(into):]
        grad = p_ref[0].astype(F32)
        for j in range(1, g):
            grad = grad + p_ref[j].astype(F32)
        mn = ADAM_B1 * m_ref[...] + (1.0 - ADAM_B1) * grad
        vn = ADAM_B2 * v_ref[...] + (1.0 - ADAM_B2) * (grad * grad)
        g_ref[...] = grad
        mo_ref[...] = mn
        vo_ref[...] = vn
        d_ref[...] = -ADAM_LR * ((mn * c1) / (jnp.sqrt(vn * c2) + ADAM_EPS) + ADAM_WD * w_ref[...])

    if layer is None:
        src, shape = pl.BlockSpec((tile, b), lambda i: (i, 0)), (a, b)
    else:
        src, shape = pl.BlockSpec((None, tile, b), lambda i: (layer, i, 0)), w.shape
    return pl.pallas_call(
        body, name=name, grid=(a // tile,),
        in_specs=[pl.BlockSpec((g, tile, b), lambda i: (0, i, 0)), src, src, src] + [_ANY] * len(into),
        out_specs=[src] * 4,
        out_shape=[jax.ShapeDtypeStruct(shape, F32)] * 4,
        input_output_aliases={4 + i: i for i in range(len(into))},
        compiler_params=_cparams(("parallel",)),
    )(parts, w, m, v, *into)


W_IN_SHARD = 276


def _w_in_dest(col):
    return jnp.where(col < P_KR, col, jnp.where(col < P_KR + 256, col + (P_CKV - P_KR), col - 2176 + P_KR + HEAD))


PLACE_TILE = 384
PLACE_SHARDS = 3
PICK_TILE = 128
PICK_TILES = 4


def _w_in_tables():
    col = np.arange(N_DEV * W_IN_SHARD)
    dest = np.where(col < P_KR, col, np.where(col < P_KR + 256, col + (P_CKV - P_KR), col - 2176 + P_KR + HEAD))
    shard = col // W_IN_SHARD

    def filled(used, universe, n):
        used = sorted(set(int(u) for u in used))
        assert len(used) <= n, used
        return used + [u for u in universe if u not in used][:n - len(used)]

    place = [filled(shard[dest // PLACE_TILE == c], range(N_DEV), PLACE_SHARDS) for c in range(P_COLS // PLACE_TILE)]
    pick = [filled(dest[shard == j] // PICK_TILE, range(P_COLS // PICK_TILE), PICK_TILES) for j in range(N_DEV)]
    return np.asarray(place, np.int32).reshape(-1), np.asarray(pick, np.int32).reshape(-1)


def _place_w_in(g, *, name):
    _, d, sh = g.shape
    tc, ns = PLACE_TILE, PLACE_SHARDS
    table = jnp.asarray(_w_in_tables()[0])

    def body(tab_ref, g_ref, o_ref, acc_ref):
        ct, s = pl.program_id(0), pl.program_id(1)
        j = tab_ref[ct * ns + s]

        @pl.when(s == 0)
        def _():
            acc_ref[...] = jnp.zeros_like(acc_ref)

        src = j * sh + lax.broadcasted_iota(jnp.int32, (sh, tc), 0)
        dst = ct * tc + lax.broadcasted_iota(jnp.int32, (sh, tc), 1)
        place = (_w_in_dest(src) == dst).astype(BF16)
        acc_ref[...] += jnp.dot(g_ref[...], place, preferred_element_type=F32)

        @pl.when(s == ns - 1)
        def _():
            o_ref[...] = acc_ref[...].astype(o_ref.dtype)

    return pl.pallas_call(
        body, name=name,
        grid_spec=pltpu.PrefetchScalarGridSpec(
            num_scalar_prefetch=1, grid=(P_COLS // tc, ns),
            in_specs=[pl.BlockSpec((None, d, sh), lambda ct, s, tab: (tab[ct * ns + s], 0, 0))],
            out_specs=pl.BlockSpec((d, tc), lambda ct, s, tab: (0, ct)),
            scratch_shapes=[pltpu.VMEM((d, tc), F32)]),
        out_shape=jax.ShapeDtypeStruct((d, P_COLS), BF16),
        compiler_params=_cparams(("parallel", "arbitrary")),
    )(table, g)


def _unplace_w_in(dw, *, name):
    d = dw.shape[0]
    sh, tk, nt = W_IN_SHARD, PICK_TILE, PICK_TILES
    table = jnp.asarray(_w_in_tables()[1])

    def body(tab_ref, dw_ref, o_ref):
        j, kk = pl.program_id(0), pl.program_id(1)
        tile = tab_ref[j * nt + kk]
        src = j * sh + lax.broadcasted_iota(jnp.int32, (tk, sh), 1)
        dst = tile * tk + lax.broadcasted_iota(jnp.int32, (tk, sh), 0)
        pick = (_w_in_dest(src) == dst).astype(BF16)
        part = _split_dot(dw_ref[...], pick)

        @pl.when(kk == 0)
        def _():
            o_ref[...] = part

        @pl.when(kk > 0)
        def _():
            o_ref[...] += part

    return pl.pallas_call(
        body, name=name,
        grid_spec=pltpu.PrefetchScalarGridSpec(
            num_scalar_prefetch=1, grid=(N_DEV, nt),
            in_specs=[pl.BlockSpec((d, tk), lambda j, kk, tab: (0, tab[j * nt + kk]))],
            out_specs=pl.BlockSpec((None, d, sh), lambda j, kk, tab: (j, 0, 0))),
        out_shape=jax.ShapeDtypeStruct((N_DEV, d, sh), F32),
        compiler_params=_cparams(("parallel", "arbitrary")),
    )(table, dw)


def _gate_up_swiglu(h1, wgu, *, name):
    t_rows, k = h1.shape
    w = wgu.shape[2]
    tm = _tile(t_rows, 1024)

    def body(a_ref, wg_ref, wu_ref, gu_ref, act_ref):
        a = a_ref[...].astype(BF16)
        gate = jnp.dot(a, wg_ref[...], preferred_element_type=F32)
        up = jnp.dot(a, wu_ref[...], preferred_element_type=F32)
        gu_ref[0] = gate.astype(gu_ref.dtype)
        gu_ref[1] = up.astype(gu_ref.dtype)
        act_ref[...] = (gate * _sigmoid(gate) * up).astype(act_ref.dtype)

    return pl.pallas_call(
        body, name=name, grid=(t_rows // tm, 4),
        in_specs=[pl.BlockSpec((tm, k), lambda i, j: (i, 0)),
                  pl.BlockSpec((None, k, w), lambda i, j: (j, 0, 0)),
                  pl.BlockSpec((None, k, w), lambda i, j: (j + 4, 0, 0))],
        out_specs=[pl.BlockSpec((2, None, tm, w), lambda i, j: (0, j, i, 0)),
                   pl.BlockSpec((None, tm, w), lambda i, j: (j, i, 0))],
        out_shape=[jax.ShapeDtypeStruct((2, 4, t_rows, w), BF16), jax.ShapeDtypeStruct((4, t_rows, w), BF16)],
        compiler_params=_cparams(("parallel", "arbitrary")),
    )(h1, wgu, wgu)


def _down_dx_swiglu(dffn, wdown, gu, *, name):
    t_rows, k = dffn.shape
    w = gu.shape[3]
    tm = _tile(t_rows, 1024)

    def body(d_ref, w_ref, gu_ref, o_ref):
        dact = lax.dot_general(d_ref[...].astype(BF16), w_ref[...], _NT, preferred_element_type=F32)
        gate, up = gu_ref[0].astype(F32), gu_ref[1].astype(F32)
        sg = _sigmoid(gate)
        o_ref[0] = (dact * up * (sg * (1.0 + gate * (1.0 - sg)))).astype(o_ref.dtype)
        o_ref[1] = (dact * gate * sg).astype(o_ref.dtype)

    blk = pl.BlockSpec((2, None, tm, w), lambda i, j: (0, j, i, 0))
    return pl.pallas_call(
        body, name=name, grid=(t_rows // tm, 4),
        in_specs=[pl.BlockSpec((tm, k), lambda i, j: (i, 0)), pl.BlockSpec((w, k), lambda i, j: (j, 0)), blk],
        out_specs=blk, out_shape=jax.ShapeDtypeStruct(gu.shape, BF16),
        compiler_params=_cparams(("parallel", "arbitrary")),
    )(dffn, wdown, gu)


BIG = ("w_in", "mla_w_uq", "mla_w_ukv", "w_out", "w_gate_up", "w_down", "ple_w_gate", "ple_w_proj")
SMALL = ("ln_in_g", "ln_in_b", "hgrn_lb_logits", "hgrn_norm_g", "sgu_ln_g", "sgu_ln_b", "sgu_w_s", "sgu_b_s",
         "mla_q_norm_g", "mla_kv_norm_g", "ln1_g", "ln1_b", "ln2_g", "ln2_b")
ORDER = ("ln_in_g", "ln_in_b", "w_in", "hgrn_lb_logits", "hgrn_norm_g", "sgu_ln_g", "sgu_ln_b", "sgu_w_s", "sgu_b_s",
         "mla_q_norm_g", "mla_w_uq", "mla_kv_norm_g", "mla_w_ukv", "w_out", "ln1_g", "ln1_b", "w_gate_up", "w_down",
         "ple_w_gate", "ple_w_proj", "ln2_g", "ln2_b")


def _slab(a, align):
    s = a.reshape(-1, LANES)
    pad = -s.shape[0] % align
    return jnp.pad(s, ((0, pad), (0, 0))) if pad else s


def _pack(arrays, align=16, total_align=512):
    s = jnp.concatenate([_slab(a, align) for a in arrays], axis=0)
    pad = -s.shape[0] % total_align
    return jnp.pad(s, ((0, pad), (0, 0))) if pad else s


def _unpack(slab, shapes, align=16):
    out, r0 = [], 0
    for s in shapes:
        nr = math.prod(s) // LANES
        out.append(slab[r0:r0 + nr].reshape(s))
        r0 += nr + (-nr % align)
    return out


def _blocks_to_cols(g, *, name):
    nb, a, b = g.shape

    def body(g_ref, o_ref):
        o_ref[...] = g_ref[...]

    return pl.pallas_call(
        body, name=name, grid=(nb,), in_specs=[pl.BlockSpec((None, a, b), lambda j: (j, 0, 0))],
        out_specs=pl.BlockSpec((a, b), lambda j: (0, j)), out_shape=jax.ShapeDtypeStruct((a, nb * b), g.dtype),
        compiler_params=_cparams(("parallel",)),
    )(g)


def _cols_to_blocks(x, *, name):
    a, b = x.shape[0], x.shape[1] // N_DEV

    def body(x_ref, o_ref):
        o_ref[...] = x_ref[...]

    return pl.pallas_call(
        body, name=name, grid=(N_DEV,), in_specs=[pl.BlockSpec((a, b), lambda j: (0, j))],
        out_specs=pl.BlockSpec((None, a, b), lambda j: (j, 0, 0)), out_shape=jax.ShapeDtypeStruct((N_DEV, a, b), x.dtype),
        compiler_params=_cparams(("parallel",)),
    )(x)


def _weight_shards(w, li):
    uq_pad = ((0, 0), (0, LANES - ATT_D))
    shards = {k: w[k][li] for k in BIG}
    shards["mla_w_uq"] = jnp.pad(shards["mla_w_uq"], uq_pad)
    return {k: s.astype(BF16) for k, s in shards.items()}


def _usable_weights(g, *, name):
    out = {}
    for k, a in g.items():
        if k == "w_in":
            out[k] = _place_w_in(a, name=name + "_place_w_in")
        elif k in ("w_out", "w_down", "ple_w_gate"):
            out[k] = a.reshape(a.shape[0] * a.shape[1], a.shape[2])
        elif k == "w_gate_up":
            out[k] = a
        else:
            out[k] = _blocks_to_cols(a, name=name + "_cols_" + k)
    return out


def _as_pairs(g):
    if g.ndim == 2:
        return g.reshape((4, 2, g.shape[0] // N_DEV) + g.shape[1:])
    return g.reshape((4, 2) + g.shape[1:])


def _twice(fn):
    return lambda *a: fn(*a) * 2


def _layer_forward(li, h, hb, p_i, wts, sm, lbs, tables, alpha, hgrn_job=None, after_hgrn=None, attn_job=None,
                   after_attn=None):
    n = f"l{li}_"
    row1 = lambda a: a.reshape(1, -1)
    projp = _mm(hb, wts["w_in"], name=n + "proj")
    ng = row1(sm["hgrn_norm_g"][li])
    res = _hgrn_fwd(projp, lbs[li], ng, name=n + "hgrn_fwd", job=hgrn_job)
    if hgrn_job is not None:
        res, got = res
        wts = dict(wts, **after_hgrn(got))
    o_a, o_pre, states = res
    lg, lbias = row1(sm["sgu_ln_g"][li]), row1(sm["sgu_ln_b"][li])
    w_s = sm["sgu_w_s"][li]
    bias_full = jnp.repeat(sm["sgu_b_s"][li].T, HEAD, axis=1)
    o_b = _sgu_fwd(projp, lg, lbias, w_s, bias_full, name=n + "sgu_fwd")
    qg, kvg = row1(sm["mla_q_norm_g"][li]), row1(sm["mla_kv_norm_g"][li])
    cq_view, ckv_view = (projp, 384, P_CQ // 384), (projp, 256, P_CKV // 256)
    (cqn,) = _rowwise(_fn_rms, [cq_view], [qg], [(384, BF16)], name=n + "q_norm")
    (ckvn,) = _rowwise(_fn_rms, [ckv_view], [kvg], [(256, BF16)], name=n + "kv_norm")
    q = _mm(cqn, wts["mla_w_uq"], name=n + "uq")
    kv = _mm(ckvn, wts["mla_w_ukv"], name=n + "ukv")
    qr, kf, kvb = _mla_prep(q, kv, projp, tables, name=n + "mla_prep")
    res = _attn_fwd(qr, kf, kvb, name=n + "attn_fwd", job=attn_job)
    if attn_job is not None:
        res, got = res
        wts = dict(wts, **after_attn(got))
    o_c, lse = res
    cat = jnp.concatenate([o_a, o_b, o_c.astype(BF16)], axis=1)
    mix = _mm(cat, wts["w_out"], name=n + "out_proj")
    g1, b1 = row1(sm["ln1_g"][li]), row1(sm["ln1_b"][li])
    d = h.shape[1]
    h1, h1b = _rowwise(_twice(_make_post_mix(alpha)), [h, mix], [g1, b1], [(d, F32), (d, BF16)], name=n + "ln1")
    gu, act = _gate_up_swiglu(h1b, wts["w_gate_up"], name=n + "gate_up")
    ffn = _mm(act, wts["w_down"], am="bmk", name=n + "down")
    pg = _mm(h1b, wts["ple_w_gate"], name=n + "ple_gate")
    pp = _mm(p_i, wts["ple_w_proj"], name=n + "ple_proj")
    g2, b2 = row1(sm["ln2_g"][li]), row1(sm["ln2_b"][li])
    h2, h2b = _rowwise(_twice(_make_ple_ln(alpha)), [h1, ffn, pg, pp], [g2, b2], [(d, F32), (d, BF16)],
                       name=n + "ln2")
    saved = dict(h=h, hb=hb, h1b=h1b, projp=projp, o_pre=o_pre, states=states, cqn=cqn, ckvn=ckvn, qr=qr, kf=kf, kvb=kvb, o_c=o_c,
                 lse=lse, cat=cat, mix=mix, h1=h1, gu=gu, act=act, ffn=ffn, pg=pg, pp=pp, ng=ng, lg=lg, wts=wts,
                 lbias=lbias, w_s=w_s, bias_full=bias_full, qg=qg, kvg=kvg, g1=g1, b1=b1, g2=g2, b2=b2)
    return (h2, h2b), saved


RS_EARLY = ("ple_w_proj", "ple_w_gate", "w_down", "w_gate_up", "w_out")
RS_LATE = ("mla_w_uq", "mla_w_ukv", "w_in")


def _layer_backward(li, dh2_parts, p_i, sv, lbs, tables, alpha, core, carried=None):
    n = f"l{li}_b_"
    wts = sv["wts"]
    gr = {}
    dh1_a, dffn, dpg, dpp, gr["ln2_g"], gr["ln2_b"] = _rowwise_vjp(
        _make_ple_ln(alpha), [sv["h1"], sv["ffn"], sv["pg"], sv["pp"]], [sv["g2"], sv["b2"]], [dh2_parts],
        groups=[[0], [1], [2], [3]], gdtypes=[F32, BF16, BF16, BF16], name=n + "ln2")
    big = {}
    big["ple_w_proj"] = _cols_to_blocks(_mm(p_i, dpp, am="km", name=n + "ple_proj_dw"), name=n + "ple_proj_dw_blocks")
    big["ple_w_gate"] = _mm(sv["h1b"], dpg, am="km", name=n + "ple_gate_dw")
    dh1_b = _mm(dpg, wts["ple_w_gate"], bm="nk", name=n + "ple_gate_dx")
    big["w_down"] = _mm(sv["act"], dffn, am="bkm", name=n + "down_dw")
    dgu = _down_dx_swiglu(dffn, wts["w_down"], sv["gu"], name=n + "down_dx")
    dgu = dgu.reshape((N_DEV,) + dgu.shape[2:])
    big["w_gate_up"], carried_got = _mm(sv["h1b"], dgu, am="km", bm="bkn", om="bmn", name=n + "gate_up_dw",
                                        job=carried), None
    if carried is not None:
        big["w_gate_up"], carried_got = big["w_gate_up"]
    early = [_as_pairs(big[k]) for k in RS_EARLY[:-1]]
    dh1_c, theirs = _mm(dgu, wts["w_gate_up"], am="bmk", bm="bnk", name=n + "gate_up_dx", job=_pair_job(early))
    dh_a, dmix, gr["ln1_g"], gr["ln1_b"] = _rowwise_vjp(
        _make_post_mix(alpha), [sv["h"], sv["mix"]], [sv["g1"], sv["b1"]], [[dh1_a, dh1_b, dh1_c]],
        groups=[[0], [1]], gdtypes=[F32, BF16], name=n + "ln1")
    big["w_out"] = _mm(sv["cat"], dmix, am="km", name=n + "out_proj_dw")
    early.append(_as_pairs(big["w_out"]))
    dcat, their_w_out = _mm(dmix, wts["w_out"], bm="nk", name=n + "out_proj_dx", job=_pair_job(early[-1:]))
    sums = [_pair_add(x, r, core, name=n + "pair_add_" + k)
            for k, x, r in zip(RS_EARLY, early, list(theirs) + list(their_w_out))]

    (dqr, dkv, dkf), early_quads = _attn_bwd(sv["qr"], sv["kf"], sv["kvb"], dcat, sv["o_c"], sv["lse"],
                                             name=n + "attn", job=_quad_job(sums))
    dqpad, dkr = _mla_prep_bwd(dqr, dkf, tables, name=n + "mla_prep")
    big["mla_w_uq"] = _cols_to_blocks(_mm(sv["cqn"], dqpad, am="km", name=n + "uq_dw"), name=n + "uq_dw_blocks")
    dcqn = _mm(dqpad, wts["mla_w_uq"], bm="nk", name=n + "uq_dx")
    big["mla_w_ukv"] = _cols_to_blocks(_mm(sv["ckvn"], dkv, am="km", name=n + "ukv_dw"), name=n + "ukv_dw_blocks")
    dckvn = _mm(dkv, wts["mla_w_ukv"], bm="nk", name=n + "ukv_dx")
    projp = sv["projp"]
    dcq, gr["mla_q_norm_g"] = _rowwise_vjp(_fn_rms, [(projp, 384, P_CQ // 384)], [sv["qg"]], [[dcqn]],
                                           groups=[[0]], gdtypes=[BF16], name=n + "q_norm")
    dckv, gr["mla_kv_norm_g"] = _rowwise_vjp(_fn_rms, [(projp, 256, P_CKV // 256)], [sv["kvg"]], [[dckvn]],
                                             groups=[[0]], gdtypes=[BF16], name=n + "kv_norm")
    dsgu, gr["sgu_ln_g"], gr["sgu_ln_b"], gr["sgu_w_s"], gr["sgu_b_s"] = _sgu_bwd(
        projp, sv["lg"], sv["lbias"], sv["w_s"], sv["bias_full"], dcat, name=n + "sgu")
    dhg, gr["hgrn_norm_g"], gr["lower_bound"] = _hgrn_bwd(
        projp, lbs[li], sv["ng"], sv["o_pre"], sv["states"], dcat, name=n + "hgrn")
    dprojp = jnp.concatenate([dhg, dsgu, dcq, dkr, dckv], axis=1)
    big["w_in"] = _unplace_w_in(_mm(sv["hb"], dprojp, am="km", name=n + "proj_dw"), name=n + "proj_dw_shards")
    late = [_as_pairs(big[k]) for k in RS_LATE]
    dh_b, theirs = _mm(dprojp, wts["w_in"], bm="nk", name=n + "proj_dx", job=_pair_job(late))
    late_sums = [_pair_add(x, r, core, name=n + "pair_add_" + k) for k, x, r in zip(RS_LATE, late, theirs)]
    return [dh_a, dh_b], gr, early_quads, late_sums, carried_got


def kernel(x, p, positions, ln_in_g, ln_in_b, w_in, hgrn_lb_logits, hgrn_norm_g, sgu_ln_g, sgu_ln_b, sgu_w_s, sgu_b_s, mla_q_norm_g, mla_w_uq, mla_kv_norm_g, mla_w_ukv, w_out, ln1_g, ln1_b, w_gate_up, w_down, ple_w_gate, ple_w_proj, ln2_g, ln2_b, loss_target, m_ln_in_g, m_ln_in_b, m_w_in, m_hgrn_lb_logits, m_hgrn_norm_g, m_sgu_ln_g, m_sgu_ln_b, m_sgu_w_s, m_sgu_b_s, m_mla_q_norm_g, m_mla_w_uq, m_mla_kv_norm_g, m_mla_w_ukv, m_w_out, m_ln1_g, m_ln1_b, m_w_gate_up, m_w_down, m_ple_w_gate, m_ple_w_proj, m_ln2_g, m_ln2_b, v_ln_in_g, v_ln_in_b, v_w_in, v_hgrn_lb_logits, v_hgrn_norm_g, v_sgu_ln_g, v_sgu_ln_b, v_sgu_w_s, v_sgu_b_s, v_mla_q_norm_g, v_mla_w_uq, v_mla_kv_norm_g, v_mla_w_ukv, v_w_out, v_ln1_g, v_ln1_b, v_w_gate_up, v_w_down, v_ple_w_gate, v_ple_w_proj, v_ln2_g, v_ln2_b):
    args = dict(locals())
    w = {k: args[k] for k in ORDER}
    m = {k: args["m_" + k] for k in ORDER}
    v = {k: args["v_" + k] for k in ORDER}
    depth = w_in.shape[0]
    assert depth == 2, "the lower-bound kernel is written for two layers"
    alpha = (2 * depth) ** 0.25
    xs, tgt = x[0], loss_target[0]
    d_model = xs.shape[1]

    shards = [_weight_shards(w, li) for li in range(depth)]
    on_hgrn0 = ("mla_w_uq", "mla_w_ukv", "w_out", "ple_w_gate", "ple_w_proj")
    ffn0 = ("w_gate_up", "w_down")
    first1 = ("w_in", "mla_w_uq", "mla_w_ukv", "w_out")
    on_attn1 = ("w_gate_up", "w_down", "ple_w_gate", "ple_w_proj")
    layer1_first = {}

    def after_hgrn0(got):
        got = _gather_forward(got, name="gather_l0a_forward")
        return _usable_weights(dict(zip(on_hgrn0, got)), name="l0")

    def after_attn0(got):
        got = _gather_forward(got, name="gather_l0b_forward")
        layer1_first.update(_usable_weights(dict(zip(first1, got[len(ffn0):])), name="l1"))
        return _usable_weights(dict(zip(ffn0, got[:len(ffn0)])), name="l0")

    def after_attn1(got):
        got = _gather_forward(got, name="gather_l1_forward")
        return _usable_weights(dict(zip(on_attn1, got)), name="l1")

    tables = _rope_tables(positions[0])
    row1 = lambda a: a.reshape(1, -1)
    l0, l1 = row1(hgrn_lb_logits[0]), row1(hgrn_lb_logits[1])
    lbs = _rowwise(_fn_lower_bounds, [l0, l1], [], [(HG_W, F32), (HG_W, F32)], name="lower_bounds")

    gin, bin_ = row1(ln_in_g), row1(ln_in_b)
    (h, hb), g_in = _rowwise(_twice(_fn_ln), [xs], [gin, bin_], [(d_model, F32), (d_model, BF16)], name="ln_in",
                             job=_gather_job([shards[0]["w_in"]]))
    w_in0 = _usable_weights({"w_in": _gather_forward(g_in, name="gather_l0_w_in_forward")[0]}, name="l0")
    (h, hb), sv0 = _layer_forward(
        0, h, hb, p[0, 0], w_in0, w, lbs, tables, alpha,
        hgrn_job=_gather_job([shards[0][k] for k in on_hgrn0]), after_hgrn=after_hgrn0,
        attn_job=_gather_job([shards[0][k] for k in ffn0] + [shards[1][k] for k in first1]), after_attn=after_attn0)
    (h, _), sv1 = _layer_forward(
        1, h, hb, p[1, 0], layer1_first, w, lbs, tables, alpha,
        attn_job=_gather_job([shards[1][k] for k in on_attn1]), after_attn=after_attn1)
    saved = [sv0, sv1]
    dy, loss_local = _loss_and_grad(h, tgt, name="loss")
    loss = lax.psum(loss_local[0, 0], ("x", "y", "c"))

    core = lax.axis_index("c").astype(jnp.int32).reshape(1)
    dparts, grads, quads, carried = [dy], [None] * depth, [None] * depth, None
    for li in reversed(range(depth)):
        dparts, grads[li], early_quads, late_sums, late_quads = _layer_backward(
            li, dparts, p[li, 0], saved[li], lbs, tables, alpha, core, carried=carried)
        quads[li] = dict(zip(RS_EARLY, early_quads))
        if carried is not None:
            quads[li + 1].update(zip(RS_LATE, late_quads))
        carried = _quad_job(late_sums)
    (dx, d_gin, d_bin), late_quads = _rowwise_vjp(_fn_ln, [xs], [gin, bin_], [dparts], groups=[[0]], name="ln_in_b",
                                                   job=carried)
    quads[0].update(zip(RS_LATE, late_quads))
    dl0, dl1 = _rowwise_vjp(_fn_lower_bounds, [l0, l1], [], [[grads[0]["lower_bound"]], [grads[1]["lower_bound"]]],
                            groups=[[0], [1]], name="lower_bounds_b")

    prefixes = ("grad_", "delta_", "new_m_", "new_v_")
    uq_pad = ((0, 0), (0, 0), (0, LANES - ATT_D))
    state = {k: ((jnp.pad(w[k], uq_pad), jnp.pad(m[k], uq_pad), jnp.pad(v[k], uq_pad)) if k == "mla_w_uq"
                 else (w[k], m[k], v[k])) for k in BIG}
    out = {}
    for k in BIG:
        res4 = None
        for li in range(depth):
            res4 = _adamw(quads[li][k], *state[k], li, name=f"adamw_l{li}_{k}", into=res4)
        for pre, a in zip(prefixes, res4):
            out[pre + k] = a[:, :, :ATT_D] if k == "mla_w_uq" else a

    small_g = {"ln_in_g": d_gin.reshape(-1), "ln_in_b": d_bin.reshape(-1),
               "hgrn_lb_logits": jnp.stack([dl0.reshape(-1), dl1.reshape(-1)])}
    for k in SMALL[3:]:
        small_g[k] = jnp.stack([grads[li][k].reshape(w[k].shape[1:]) for li in range(depth)])
    (small_parts,) = _all_gather([_pack([small_g[k] for k in SMALL])], name="gather_small_grads")
    slabs = _adamw(small_parts, _pack([w[k] for k in SMALL]), _pack([m[k] for k in SMALL]),
                   _pack([v[k] for k in SMALL]), None, name="adamw_small")
    shapes = [w[k].shape for k in SMALL]
    for pre, slab in zip(prefixes, slabs):
        for k, a in zip(SMALL, _unpack(slab, shapes)):
            out[pre + k] = a
    res = [loss, dx[None]]
    for prefix in ("grad_", "delta_", "new_m_", "new_v_"):
        res += [out[prefix + k] for k in ORDER]
    return tuple(res)
```

```python
import functools
import math

import jax
import jax.numpy as jnp
import numpy as np
from jax import lax
from jax.experimental import pallas as pl
from jax.experimental.pallas import tpu as pltpu

F32 = jnp.float32
BF16 = jnp.bfloat16
MESH = pl.DeviceIdType.MESH

LN_EPS = 1e-5
RMS_EPS = 1e-6
ROPE_THETA = 10000.0
ADAM_LR, ADAM_B1, ADAM_B2, ADAM_EPS, ADAM_WD, ADAM_STEP = 0.001, 0.9, 0.999, 1e-08, 0.01, 10

N_DEV = 8
LANES = 128
HG_CHUNK = 16
HG_W = 256
HEAD = 64
SGU_CHUNK = 128
N_ATT_HEADS = 8
ATT_D = 96
VMEM_LIMIT = 56 * 1024 * 1024

HG_TILE = 256
ATT_TQ = 512
ROW_TILE = 256

P_CQ, P_KR, P_CKV, P_COLS = 1536, 1920, 2048, 2304


def _cparams(sem):
    return pltpu.CompilerParams(dimension_semantics=sem, vmem_limit_bytes=VMEM_LIMIT)


_ANY = pl.BlockSpec(memory_space=pl.ANY)


def _call(body, operands, *, name, grid, in_specs, out_specs, out_shape, sem, scratch_shapes=(), job=None):
    if job is None:
        return pl.pallas_call(body, name=name, grid=grid, in_specs=in_specs, out_specs=out_specs, out_shape=out_shape,
                              scratch_shapes=list(scratch_shapes), compiler_params=_cparams(sem))(*operands)
    single = not isinstance(out_shape, (list, tuple))
    shapes = [out_shape] if single else list(out_shape)
    ospecs = [out_specs] if single else list(out_specs)
    ni, no, ns = len(operands), len(shapes), len(scratch_shapes)
    ji, jo = len(job.inputs), len(job.out_shapes)

    def hosted(*refs):
        p = 0
        parts = []
        for cnt in (ni, ji, no, jo, ns):
            parts.append(refs[p:p + cnt])
            p += cnt
        ins, jins, outs, jouts, scr = parts
        jsems = refs[p:]
        ids = [pl.program_id(a) for a in range(len(grid))]
        first = functools.reduce(lambda a, b: a & b, [i == 0 for i in ids])
        last = functools.reduce(lambda a, b: a & b, [i == g - 1 for i, g in zip(ids, grid)])

        @pl.when(first)
        def _():
            job.start(jins, jouts, jsems)

        body(*ins, *outs, *scr)

        @pl.when(last)
        def _():
            job.finish(jins, jouts, jsems)

    res = pl.pallas_call(
        hosted, name=name, grid=grid,
        in_specs=list(in_specs) + [_ANY] * ji, out_specs=ospecs + [_ANY] * jo,
        out_shape=shapes + list(job.out_shapes),
        scratch_shapes=list(scratch_shapes) + [pltpu.SemaphoreType.DMA((c,)) for c in job.sem_counts],
        compiler_params=_cparams(("arbitrary",) * len(grid)),
    )(*operands, *job.inputs)
    own = res[0] if single else res[:no]
    return own, res[no:]


class _Job:
    def __init__(self, inputs, out_shapes, sem_counts, start, finish):
        self.inputs, self.out_shapes, self.sem_counts = list(inputs), list(out_shapes), list(sem_counts)
        self.start, self.finish = start, finish


def _copies_job(inputs, out_shapes, n_remote, n_local, make):
    def start(jins, jouts, sems):
        sends, _, local = make(jins, jouts, *sems)
        for cp in local + sends:
            cp.start()

    def finish(jins, jouts, sems):
        sends, recvs, local = make(jins, jouts, *sems)
        for cp in recvs:
            cp.wait_recv()
        for cp in sends:
            cp.wait_send()
        for cp in local:
            cp.wait()

    return _Job(inputs, out_shapes, [n_remote, n_remote, max(n_local, 1)], start, finish)


def _run_job(job, *, name):
    ji, jo = len(job.inputs), len(job.out_shapes)

    def body(*refs):
        jins, jouts, sems = refs[:ji], refs[ji:ji + jo], refs[ji + jo:]
        job.start(jins, jouts, sems)
        job.finish(jins, jouts, sems)

    return pl.pallas_call(
        body, name=name, out_shape=list(job.out_shapes), in_specs=[_ANY] * ji, out_specs=[_ANY] * jo,
        scratch_shapes=[pltpu.SemaphoreType.DMA((c,)) for c in job.sem_counts],
    )(*job.inputs)


def _tile(n, pref):
    if n % pref == 0:
        return pref
    best = None
    t = LANES
    while t <= min(n, pref):
        if n % t == 0:
            best = t
        t += LANES
    return best if best is not None else n


def _mm(a, b, *, am="mk", bm="kn", om="mn", out_dtype=F32, tm=1024, tn=1024, tk=1024, name, job=None):
    if am == "mk":
        m, k = a.shape
    elif am == "km":
        k, m = a.shape
    elif am == "bmk":
        m, tk = a.shape[1], a.shape[2]
        k = a.shape[0] * tk
    else:
        k, tm = a.shape[1], a.shape[2]
        m = a.shape[0] * tm
    if bm == "kn":
        kb_, n = b.shape
    elif bm == "nk":
        n, kb_ = b.shape
    elif bm == "bkn":
        kb_, tn = b.shape[1], b.shape[2]
        n = b.shape[0] * tn
    else:
        n, tk = b.shape[1], b.shape[2]
        kb_ = b.shape[0] * tk
    assert kb_ == k, (a.shape, b.shape, am, bm)
    tm, tn, tk = _tile(m, tm), _tile(n, tn), _tile(k, tk)
    nk = k // tk
    dims = (((0 if am in ("km", "bkm") else 1,), (1 if bm in ("nk", "bnk") else 0,)), ((), ()))

    a_spec = {"mk": pl.BlockSpec((tm, tk), lambda i, j, kk: (i, kk)),
              "km": pl.BlockSpec((tk, tm), lambda i, j, kk: (kk, i)),
              "bmk": pl.BlockSpec((None, tm, tk), lambda i, j, kk: (kk, i, 0)),
              "bkm": pl.BlockSpec((None, tk, tm), lambda i, j, kk: (i, kk, 0))}[am]
    b_spec = {"kn": pl.BlockSpec((tk, tn), lambda i, j, kk: (kk, j)),
              "nk": pl.BlockSpec((tn, tk), lambda i, j, kk: (j, kk)),
              "bkn": pl.BlockSpec((None, tk, tn), lambda i, j, kk: (j, kk, 0)),
              "bnk": pl.BlockSpec((None, tn, tk), lambda i, j, kk: (kk, j, 0))}[bm]
    if om == "mn":
        o_spec, o_shape = pl.BlockSpec((tm, tn), lambda i, j, kk: (i, j)), (m, n)
    else:
        o_spec, o_shape = pl.BlockSpec((None, tm, tn), lambda i, j, kk: (j, i, 0)), (n // tn, m, tn)

    def body(a_ref, b_ref, o_ref, *acc):
        kk = pl.program_id(2)
        prod = lax.dot_general(a_ref[...].astype(BF16), b_ref[...].astype(BF16), dims, preferred_element_type=F32)
        if nk == 1:
            o_ref[...] = prod.astype(o_ref.dtype)
            return
        acc_ref, = acc

        @pl.when(kk == 0)
        def _():
            acc_ref[...] = prod

        if nk > 2:
            @pl.when((kk > 0) & (kk < nk - 1))
            def _():
                acc_ref[...] += prod

        @pl.when(kk == nk - 1)
        def _():
            o_ref[...] = (acc_ref[...] + prod).astype(o_ref.dtype)

    return _call(body, (a, b), name=name, grid=(m // tm, n // tn, nk), in_specs=[a_spec, b_spec], out_specs=o_spec,
                 out_shape=jax.ShapeDtypeStruct(o_shape, out_dtype),
                 scratch_shapes=[pltpu.VMEM((tm, tn), F32)] if nk > 1 else [],
                 sem=("parallel", "parallel", "arbitrary"), job=job)


def _row_operand(a, tile):
    if isinstance(a, tuple):
        arr, w, j = a
        return arr, pl.BlockSpec((tile, w), lambda i, j=j: (i, j))
    return a, pl.BlockSpec((tile, a.shape[1]), lambda i: (i, 0))


def _const_spec(c):
    nd = c.ndim
    return pl.BlockSpec(c.shape, lambda i, nd=nd: (0,) * nd)


def _rowwise(fn, rows, consts, outs, *, name, accs=(), tile=None, job=None):
    t_rows = (rows[0][0] if isinstance(rows[0], tuple) else rows[0]).shape[0]
    tile = min(tile or ROW_TILE, t_rows)
    arrs, specs = zip(*[_row_operand(a, tile) for a in rows])
    nin, no = len(rows) + len(consts), len(outs)

    def body(*refs):
        res = fn(*[r[...] for r in refs[:nin]])
        for r, v in zip(refs[nin:nin + no], res[:no]):
            r[...] = v.astype(r.dtype)
        if accs:
            a_refs = refs[nin + no:]

            @pl.when(pl.program_id(0) == 0)
            def _():
                for r in a_refs:
                    r[...] = jnp.zeros_like(r)

            for r, v in zip(a_refs, res[no:]):
                r[...] += v

    out_shape = [jax.ShapeDtypeStruct((t_rows, w), dt) for w, dt in outs]
    out_shape += [jax.ShapeDtypeStruct(s, F32) for s in accs]
    out_specs = [pl.BlockSpec((tile, w), lambda i: (i, 0)) for w, _ in outs]
    out_specs += [pl.BlockSpec(s, lambda i, nd=len(s): (0,) * nd) for s in accs]
    return _call(body, (*arrs, *consts), name=name, grid=(t_rows // tile,),
                 in_specs=list(specs) + [_const_spec(c) for c in consts],
                 out_specs=out_specs, out_shape=out_shape, sem=("arbitrary",), job=job)


def _rowwise_vjp(fn, rows, consts, cts, *, name, groups, tile=None, gdtypes=None, job=None):
    t_rows = (rows[0][0] if isinstance(rows[0], tuple) else rows[0]).shape[0]
    tile = min(tile or ROW_TILE, t_rows)
    arrs, specs = zip(*[_row_operand(a, tile) for a in rows])
    flat_cts = [c for group in cts for c in group]
    ct_arrs, ct_specs = zip(*[_row_operand(a, tile) for a in flat_cts])
    nr, nc, nct, ng = len(rows), len(consts), len(flat_cts), len(groups)

    def width(a):
        return a[1] if isinstance(a, tuple) else a.shape[1]

    def body(*refs):
        rv = [r[...].astype(F32) for r in refs[:nr]]
        cv = [r[...] for r in refs[nr:nr + nc]]
        ct_refs = refs[nr + nc:nr + nc + nct]
        ctv, pos = [], 0
        for group in cts:
            s = ct_refs[pos][...].astype(F32)
            for r in ct_refs[pos + 1:pos + len(group)]:
                s = s + r[...].astype(F32)
            ctv.append(s)
            pos += len(group)
        _, pull = jax.vjp(fn, *rv, *cv)
        grads = pull(tuple(ctv))
        g_refs = refs[nr + nc + nct:nr + nc + nct + ng]
        for r, idx in zip(g_refs, groups):
            parts = [grads[i] for i in idx]
            r[...] = (parts[0] if len(parts) == 1 else jnp.concatenate(parts, axis=1)).astype(r.dtype)
        c_refs = refs[nr + nc + nct + ng:]

        @pl.when(pl.program_id(0) == 0)
        def _():
            for r in c_refs:
                r[...] = jnp.zeros_like(r)

        for r, v in zip(c_refs, grads[nr:]):
            r[...] += v

    gw = [sum(width(rows[i]) for i in idx) for idx in groups]
    gdtypes = gdtypes or [F32] * ng
    out_shape = [jax.ShapeDtypeStruct((t_rows, w), dt) for w, dt in zip(gw, gdtypes)]
    out_shape += [jax.ShapeDtypeStruct(c.shape, F32) for c in consts]
    out_specs = [pl.BlockSpec((tile, w), lambda i: (i, 0)) for w in gw]
    out_specs += [_const_spec(c) for c in consts]
    return _call(body, (*arrs, *consts, *ct_arrs), name=name, grid=(t_rows // tile,),
                 in_specs=list(specs) + [_const_spec(c) for c in consts] + list(ct_specs),
                 out_specs=out_specs, out_shape=out_shape, sem=("arbitrary",), job=job)


def _layer_norm(x, g, b):
    mu = jnp.mean(x, axis=-1, keepdims=True)
    xc = x - mu
    var = jnp.mean(xc * xc, axis=-1, keepdims=True)
    return xc * lax.rsqrt(var + LN_EPS) * g + b


def _sigmoid(x):
    return 1.0 / (1.0 + jnp.exp(-x))


def _fn_ln(x, g, b):
    return (_layer_norm(x, g, b),)


def _fn_rms(x, g):
    return (x * lax.rsqrt(jnp.mean(x * x, axis=-1, keepdims=True) + RMS_EPS) * g,)


def _make_post_mix(alpha):
    def fn(h, mix, g, b):
        return (_layer_norm(alpha * h + mix, g, b),)
    return fn


def _make_ple_ln(alpha):
    def fn(h1, ffn, pg, pp, g, b):
        return (_layer_norm(alpha * h1 + ffn + _sigmoid(pg) * pp, g, b),)
    return fn


def _fn_lower_bounds(l0, l1):
    m = jnp.maximum(l0, l1)
    e0, e1 = jnp.exp(l0 - m), jnp.exp(l1 - m)
    s = e0 + e1
    p0, p1 = e0 / s, e1 / s
    return (p0 - p0, (p0 + p1) - p0)


def _loss_and_grad(y, target, *, name):
    d = y.shape[1]

    def fn(yv, tv):
        err = yv - tv
        return err * (1.0 / d), 0.5 * jnp.sum(jnp.mean(err * err, axis=-1, keepdims=True), axis=0, keepdims=True)

    return _rowwise(fn, [y, target], [], [(d, F32)], accs=[(1, 1)], name=name)


def _split_dot(x, e_bf16):
    hi = x.astype(BF16)
    lo = (x - hi.astype(F32)).astype(BF16)
    return (jnp.dot(hi, e_bf16, preferred_element_type=F32) + jnp.dot(lo, e_bf16, preferred_element_type=F32))


def _hgrn_common(th):
    rm = lax.broadcasted_iota(jnp.int32, (th, HG_W), 0) % HG_CHUNK

    def seg_cumsum(x):
        for s in (1, 2, 4, 8):
            x = x + jnp.where(rm >= s, pltpu.roll(x, s, 0), 0.0)
        return x

    def seg_rcumsum(x):
        for s in (1, 2, 4, 8):
            x = x + jnp.where(rm < HG_CHUNK - s, pltpu.roll(x, th - s, 0), 0.0)
        return x

    ri = lax.broadcasted_iota(jnp.int32, (HG_W, HG_W), 0) // HEAD
    ci = lax.broadcasted_iota(jnp.int32, (HG_W, HG_W), 1) // HEAD
    head_f32 = (ri == ci).astype(F32)
    head_bf16 = head_f32.astype(BF16)

    def headsum(x, pieces=2):
        if pieces == 1:
            return jnp.dot(x.astype(BF16), head_bf16, preferred_element_type=F32)
        return _split_dot(x, head_bf16)

    return rm, seg_cumsum, seg_rcumsum, head_f32, headsum


def _hgrn_gates(qr, fl, lb):
    sg = _sigmoid(fl)
    f = lb + (1.0 - lb) * sg
    sq = _sigmoid(qr)
    return sg, f, jnp.log(f), 1.0 - f, qr * sq, sq


def _shifted(x, d, th):
    return x if d == 0 else pltpu.roll(x, d, 0)


def _unshift(x, d, th):
    return x if d == 0 else pltpu.roll(x, th - d, 0)


def _hgrn_fwd(projp, lb, ng, *, name, job=None):
    t_rows = projp.shape[0]
    th = min(HG_TILE, t_rows)
    nct = th // HG_CHUNK

    def body(q_ref, f_ref, i_ref, g_ref, lb_ref, ng_ref, oa_ref, opre_ref, st_out_ref,
             st_ref, vtm_ref, kv_ref, qe_ref, dec_ref, oint_ref):
        rm, seg_cumsum, seg_rcumsum, head_f32, headsum = _hgrn_common(th)

        @pl.when(pl.program_id(0) == 0)
        def _():
            st_ref[...] = jnp.zeros_like(st_ref)

        qr, fl, v, g = q_ref[...], f_ref[...], i_ref[...], g_ref[...]
        _, f, lf, k, q, _ = _hgrn_gates(qr, fl, lb_ref[...])
        b = seg_cumsum(lf)

        o = jnp.zeros((th, HG_W), F32)
        for d in range(HG_CHUNK):
            kd, bd, vd = _shifted(k, d, th), _shifted(b, d, th), _shifted(v, d, th)
            e = jnp.exp(jnp.where(rm >= d, b - bd, -1e30))
            o = o + headsum(q * kd * e, 1) * vd

        blast = seg_rcumsum(jnp.where(rm == HG_CHUNK - 1, b, 0.0))
        kte = (k * jnp.exp(blast - b)).astype(BF16)
        qe_ref[...] = q * jnp.exp(b)
        dec_ref[...] = jnp.exp(blast)
        vt = v.T
        lane_chunk = lax.broadcasted_iota(jnp.int32, (HG_W, th), 1) // HG_CHUNK
        for c in range(nct):
            vtm_ref[c * HG_W:(c + 1) * HG_W, :] = jnp.where(lane_chunk == c, vt, 0.0).astype(BF16)
        kv_ref[...] = jnp.dot(vtm_ref[...], kte, preferred_element_type=F32)

        s = st_ref[...]
        for c in range(nct):
            rows = slice(c * HG_CHUNK, (c + 1) * HG_CHUNK)
            st_out_ref[c] = s
            oint_ref[rows, :] = lax.dot_general(qe_ref[rows, :].astype(BF16), s.astype(BF16),
                                                (((1,), (1,)), ((), ())), preferred_element_type=F32)
            dec = jnp.max(dec_ref[rows, :], axis=0, keepdims=True)
            s = s * dec + kv_ref[c * HG_W:(c + 1) * HG_W, :] * head_f32
        st_ref[...] = s

        o = o + oint_ref[...]
        opre_ref[...] = o
        r = lax.rsqrt(headsum(o * o) * (1.0 / HEAD) + RMS_EPS)
        oa_ref[...] = (o * r * ng_ref[...] * (g * _sigmoid(g))).astype(oa_ref.dtype)

    col = lambda j: pl.BlockSpec((th, HG_W), lambda i, j=j: (i, j))
    vec = pl.BlockSpec((1, HG_W), lambda i: (0, 0))
    row = pl.BlockSpec((th, HG_W), lambda i: (i, 0))
    n_chunks = t_rows // HG_CHUNK
    return _call(
        body, (projp, projp, projp, projp, lb, ng), name=name, grid=(t_rows // th,),
        in_specs=[col(0), col(1), col(2), col(3), vec, vec],
        out_specs=[row, row, pl.BlockSpec((nct, HG_W, HG_W), lambda i: (i, 0, 0))],
        out_shape=[jax.ShapeDtypeStruct((t_rows, HG_W), BF16), jax.ShapeDtypeStruct((t_rows, HG_W), F32),
                   jax.ShapeDtypeStruct((n_chunks, HG_W, HG_W), F32)],
        scratch_shapes=[pltpu.VMEM((HG_W, HG_W), F32), pltpu.VMEM((nct * HG_W, th), BF16),
                        pltpu.VMEM((nct * HG_W, HG_W), F32), pltpu.VMEM((th, HG_W), F32),
                        pltpu.VMEM((th, HG_W), F32), pltpu.VMEM((th, HG_W), F32)],
        sem=("arbitrary",), job=job)


def _hgrn_bwd(projp, lb, ng, opre, states, dcat, *, name):
    t_rows = projp.shape[0]
    th = min(HG_TILE, t_rows)
    nct = th // HG_CHUNK
    nt = t_rows // th

    def body(q_ref, f_ref, i_ref, g_ref, lb_ref, ng_ref, opre_ref, st_in_ref, do_ref,
             dproj_ref, dng_ref, dlb_ref,
             gst_ref, dotm_ref, qg_ref, v_ref, kte_ref, dop_ref, dec_ref, dkte_ref, dvi_ref, dqe_ref, ddec_ref):
        rm, seg_cumsum, seg_rcumsum, head_f32, headsum = _hgrn_common(th)

        @pl.when(pl.program_id(0) == 0)
        def _():
            gst_ref[...] = jnp.zeros_like(gst_ref)
            dng_ref[...] = jnp.zeros_like(dng_ref)
            dlb_ref[...] = jnp.zeros_like(dlb_ref)

        qr, fl, v, g = q_ref[...], f_ref[...], i_ref[...], g_ref[...]
        lb, ngv = lb_ref[...], ng_ref[...]
        sg, f, lf, k, q, sq = _hgrn_gates(qr, fl, lb)
        b = seg_cumsum(lf)
        blast = seg_rcumsum(jnp.where(rm == HG_CHUNK - 1, b, 0.0))
        eb = jnp.exp(b)
        ekb = jnp.exp(blast - b)
        qe, kte, dec = q * eb, k * ekb, jnp.exp(blast)

        do_out, op = do_ref[...], opre_ref[...]
        sgg = _sigmoid(g)
        sil = g * sgg
        r = lax.rsqrt(headsum(op * op) * (1.0 / HEAD) + RMS_EPS)
        on = op * r
        dng_ref[...] += jnp.sum(do_out * on * sil, axis=0, keepdims=True)
        dg = do_out * on * ngv * (sgg * (1.0 + g * (1.0 - sgg)))
        don = do_out * ngv * sil
        dop = r * (don - on * (headsum(don * on) * (1.0 / HEAD)))

        v_ref[...] = v
        kte_ref[...] = kte
        dop_ref[...] = dop
        dec_ref[...] = dec
        dot_t = dop.T
        lane_chunk = lax.broadcasted_iota(jnp.int32, (HG_W, th), 1) // HG_CHUNK
        for c in range(nct):
            dotm_ref[c * HG_W:(c + 1) * HG_W, :] = jnp.where(lane_chunk == c, dot_t, 0.0).astype(BF16)
        qg_ref[...] = jnp.dot(dotm_ref[...], qe.astype(BF16), preferred_element_type=F32)

        gs = gst_ref[...]
        for c in reversed(range(nct)):
            rows = slice(c * HG_CHUNK, (c + 1) * HG_CHUNK)
            s = st_in_ref[c]
            gm = (gs * head_f32).astype(BF16)
            dkte_ref[rows, :] = jnp.dot(v_ref[rows, :].astype(BF16), gm, preferred_element_type=F32)
            dvi_ref[rows, :] = lax.dot_general(kte_ref[rows, :].astype(BF16), gm, (((1,), (1,)), ((), ())),
                                               preferred_element_type=F32)
            dqe_ref[rows, :] = jnp.dot(dop_ref[rows, :].astype(BF16), s.astype(BF16), preferred_element_type=F32)
            ddec_ref[rows, :] = jnp.broadcast_to(jnp.sum(gs * s, axis=0, keepdims=True), (HG_CHUNK, HG_W))
            dec_c = jnp.max(dec_ref[rows, :], axis=0, keepdims=True)
            gs = gs * dec_c + qg_ref[c * HG_W:(c + 1) * HG_W, :] * head_f32
        gst_ref[...] = gs

        dkte, dqe = dkte_ref[...], dqe_ref[...]
        dq = dqe * eb
        dk = dkte * ekb
        db = dqe * qe - dkte * kte
        dv = dvi_ref[...]
        dblast = dkte * kte + jnp.where(rm == HG_CHUNK - 1, ddec_ref[...] * dec, 0.0)

        for d in range(HG_CHUNK):
            kd, bd, vd = _shifted(k, d, th), _shifted(b, d, th), _shifted(v, d, th)
            e = jnp.exp(jnp.where(rm >= d, b - bd, -1e30))
            p = q * kd * e
            sc = headsum(p, 1)
            dsc = headsum(dop * vd, 1)
            dv = dv + _unshift(sc * dop, d, th)
            dq = dq + dsc * kd * e
            dk = dk + _unshift(dsc * q * e, d, th)
            darg = dsc * p
            db = db + darg - _unshift(darg, d, th)

        db = db + jnp.where(rm == HG_CHUNK - 1, seg_cumsum(dblast), 0.0)
        dlf = seg_rcumsum(db)
        df = dlf / f - dk
        dlb_ref[...] += jnp.sum(df * (1.0 - sg), axis=0, keepdims=True)
        dfl = df * (1.0 - lb) * sg * (1.0 - sg)
        dqr = dq * (sq * (1.0 + qr * (1.0 - sq)))
        dproj_ref[...] = jnp.concatenate([dqr, dfl, dv, dg], axis=1).astype(dproj_ref.dtype)

    rev = lambda i: nt - 1 - i
    col = lambda j: pl.BlockSpec((th, HG_W), lambda i, j=j: (rev(i), j))
    vec = pl.BlockSpec((1, HG_W), lambda i: (0, 0))
    row = pl.BlockSpec((th, HG_W), lambda i: (rev(i), 0))
    tile_f32 = pltpu.VMEM((th, HG_W), F32)
    return pl.pallas_call(
        body, name=name, grid=(nt,),
        in_specs=[col(0), col(1), col(2), col(3), vec, vec, row,
                  pl.BlockSpec((nct, HG_W, HG_W), lambda i: (rev(i), 0, 0)), col(0)],
        out_specs=[pl.BlockSpec((th, 4 * HG_W), lambda i: (rev(i), 0)), vec, vec],
        out_shape=[jax.ShapeDtypeStruct((t_rows, 4 * HG_W), BF16), jax.ShapeDtypeStruct((1, HG_W), F32),
                   jax.ShapeDtypeStruct((1, HG_W), F32)],
        scratch_shapes=[pltpu.VMEM((HG_W, HG_W), F32), pltpu.VMEM((nct * HG_W, th), BF16),
                        pltpu.VMEM((nct * HG_W, HG_W), F32)] + [tile_f32] * 8,
        compiler_params=_cparams(("arbitrary",)),
    )(projp, projp, projp, projp, lb, ng, opre, states, dcat)


_INV_SQRT2 = 1.0 / math.sqrt(2.0)
_INV_SQRT2PI = 1.0 / math.sqrt(2.0 * math.pi)


def _gelu(x):
    return 0.5 * x * (1.0 + lax.erf(x * _INV_SQRT2))


def _gelu_grad(x):
    return 0.5 * (1.0 + lax.erf(x * _INV_SQRT2)) + x * jnp.exp(-0.5 * x * x) * _INV_SQRT2PI


def _sgu_parts(bu, bv, lg, lbias, w_ref, n_groups):
    c = SGU_CHUNK
    tril = (lax.broadcasted_iota(jnp.int32, (c, c), 0) >= lax.broadcasted_iota(jnp.int32, (c, c), 1)).astype(F32)
    gid = lax.broadcasted_iota(jnp.int32, bu.shape, 1) // HEAD
    u = _gelu(bu)
    gv = _gelu(bv)
    mu = jnp.mean(gv, axis=-1, keepdims=True)
    xc = gv - mu
    rstd = lax.rsqrt(jnp.mean(xc * xc, axis=-1, keepdims=True) + LN_EPS)
    xhat = xc * rstd
    vn = xhat * lg + lbias
    ws = [w_ref[gi] * tril for gi in range(n_groups)]
    return tril, gid, u, rstd, xhat, vn, ws


def _sgu_fwd(projp, lg, lbias, w_s, bias_full, *, name):
    t_rows = projp.shape[0]
    n_groups = w_s.shape[0]
    c = SGU_CHUNK

    def body(u_ref, v_ref, lg_ref, lb_ref, w_ref, bias_ref, o_ref):
        _, gid, u, _, _, vn, ws = _sgu_parts(u_ref[...], v_ref[...], lg_ref[...], lb_ref[...], w_ref, n_groups)
        vnb = vn.astype(BF16)
        z = bias_ref[...]
        for gi in range(n_groups):
            z = z + jnp.where(gid == gi, jnp.dot(ws[gi].astype(BF16), vnb, preferred_element_type=F32), 0.0)
        o_ref[...] = (u * z).astype(o_ref.dtype)

    col = lambda j: pl.BlockSpec((c, HG_W), lambda i, j=j: (i, j))
    return pl.pallas_call(
        body, name=name, grid=(t_rows // c,),
        in_specs=[col(4), col(5), _const_spec(lg), _const_spec(lbias), _const_spec(w_s), _const_spec(bias_full)],
        out_specs=pl.BlockSpec((c, HG_W), lambda i: (i, 0)),
        out_shape=jax.ShapeDtypeStruct((t_rows, HG_W), BF16),
        compiler_params=_cparams(("arbitrary",)),
    )(projp, projp, lg, lbias, w_s, bias_full)


def _sgu_bwd(projp, lg, lbias, w_s, bias_full, dcat, *, name):
    t_rows = projp.shape[0]
    n_groups = w_s.shape[0]
    c = SGU_CHUNK
    n = t_rows // c

    def body(u_ref, v_ref, lg_ref, lb_ref, w_ref, bias_ref, do_ref,
             dproj_ref, dlg_ref, dlb_ref, dw_ref, dbs_ref, dbias_acc):
        i = pl.program_id(0)

        @pl.when(i == 0)
        def _():
            dlg_ref[...] = jnp.zeros_like(dlg_ref)
            dlb_ref[...] = jnp.zeros_like(dlb_ref)
            dw_ref[...] = jnp.zeros_like(dw_ref)
            dbias_acc[...] = jnp.zeros_like(dbias_acc)

        bu, bv, lg_v = u_ref[...], v_ref[...], lg_ref[...]
        tril, gid, u, rstd, xhat, vn, ws = _sgu_parts(bu, bv, lg_v, lb_ref[...], w_ref, n_groups)
        vnb = vn.astype(BF16)
        z = bias_ref[...]
        for gi in range(n_groups):
            z = z + jnp.where(gid == gi, jnp.dot(ws[gi].astype(BF16), vnb, preferred_element_type=F32), 0.0)
        do = do_ref[...]
        dbu = do * z * _gelu_grad(bu)
        dz = do * u
        dbias_acc[...] += dz
        dvn = jnp.zeros_like(dz)
        for gi in range(n_groups):
            dzg = jnp.where(gid == gi, dz, 0.0).astype(BF16)
            dw_ref[gi] += lax.dot_general(dzg, vnb, (((1,), (1,)), ((), ())), preferred_element_type=F32) * tril
            dvn = dvn + jnp.dot(ws[gi].T.astype(BF16), dzg, preferred_element_type=F32)
        dlg_ref[...] += jnp.sum(dvn * xhat, axis=0, keepdims=True)
        dlb_ref[...] += jnp.sum(dvn, axis=0, keepdims=True)
        dxh = dvn * lg_v
        dgv = rstd * (dxh - jnp.mean(dxh, axis=-1, keepdims=True)
                      - xhat * jnp.mean(dxh * xhat, axis=-1, keepdims=True))
        dproj_ref[...] = jnp.concatenate([dbu, dgv * _gelu_grad(bv)], axis=1).astype(dproj_ref.dtype)

        @pl.when(i == n - 1)
        def _():
            dbs_ref[...] = jnp.sum(dbias_acc[...].T.reshape(n_groups, HEAD, c), axis=1)

    col = lambda j: pl.BlockSpec((c, HG_W), lambda i, j=j: (i, j))
    return pl.pallas_call(
        body, name=name, grid=(n,),
        in_specs=[col(4), col(5), _const_spec(lg), _const_spec(lbias), _const_spec(w_s), _const_spec(bias_full),
                  col(1)],
        out_specs=[pl.BlockSpec((c, 2 * HG_W), lambda i: (i, 0)), _const_spec(lg), _const_spec(lbias),
                   _const_spec(w_s), pl.BlockSpec((n_groups, c), lambda i: (0, 0))],
        out_shape=[jax.ShapeDtypeStruct((t_rows, 2 * HG_W), BF16), jax.ShapeDtypeStruct(lg.shape, F32),
                   jax.ShapeDtypeStruct(lbias.shape, F32), jax.ShapeDtypeStruct(w_s.shape, F32),
                   jax.ShapeDtypeStruct((n_groups, c), F32)],
        scratch_shapes=[pltpu.VMEM((c, HG_W), F32)],
        compiler_params=_cparams(("arbitrary",)),
    )(projp, projp, lg, lbias, w_s, bias_full, dcat)


def _rope_tables(positions):
    t = positions.shape[0]
    inv_freq = ROPE_THETA ** (-jnp.arange(0, 32, 2, dtype=F32) / 32)
    ang = positions.astype(F32)[:, None] * inv_freq
    cos, sin = jnp.cos(ang), jnp.sin(ang)
    z = lambda w: jnp.zeros((t, w), F32)
    cos_t = jnp.concatenate([jnp.ones((t, 64), F32), cos, cos, z(32)], axis=1)
    sin_up = jnp.concatenate([z(80), sin, z(32)], axis=1)
    sin_dn = jnp.concatenate([z(64), -sin, z(48)], axis=1)
    return cos_t, sin_up, sin_dn


def _rep(x, n):
    return x if n == 1 else jnp.concatenate([x] * n, axis=1)


def _rope(x, cos_t, sin_up, sin_dn):
    w = x.shape[1]
    return x * cos_t + pltpu.roll(x, 16, 1) * sin_up + pltpu.roll(x, w - 16, 1) * sin_dn


def _rope_t(dy, cos_t, sin_up, sin_dn):
    w = dy.shape[1]
    return dy * cos_t + pltpu.roll(dy * sin_up, w - 16, 1) + pltpu.roll(dy * sin_dn, 16, 1)


def _mla_prep(q, kv, projp, tables, *, name):
    nh = N_ATT_HEADS

    def fn(qv, kvv, kr, cos_t, sin_up, sin_dn):
        qr = _rope(qv, _rep(cos_t, nh), _rep(sin_up, nh), _rep(sin_dn, nh))
        krr = _rope(kr, cos_t, sin_up, sin_dn)
        lane = lax.broadcasted_iota(jnp.int32, kvv.shape, 1) % LANES
        return qr, jnp.where(lane < HEAD, kvv, 0.0) + _rep(krr, nh), kvv

    w = q.shape[1]
    return _rowwise(fn, [q, kv, (projp, LANES, P_KR // LANES)] + list(tables), [],
                    [(w, BF16), (w, BF16), (w, BF16)], name=name)


def _mla_prep_bwd(dqr, dkf, tables, *, name):
    nh = N_ATT_HEADS

    def fn(dq, dk, cos_t, sin_up, sin_dn):
        dqp = _rope_t(dq, _rep(cos_t, nh), _rep(sin_up, nh), _rep(sin_dn, nh))
        dkrr = dk[:, 0:LANES]
        for h in range(1, nh):
            dkrr = dkrr + dk[:, LANES * h:LANES * (h + 1)]
        return dqp, _rope_t(dkrr, cos_t, sin_up, sin_dn)

    return _rowwise(fn, [dqr, dkf] + list(tables), [], [(dqr.shape[1], BF16), (LANES, BF16)], name=name)


_LOG2E = 1.0 / math.log(2.0)
_NT = (((1,), (1,)), ((), ()))
_TN = (((0,), (0,)), ((), ()))


def _attn_fwd(qr, kf, kvb, *, name, job=None):
    t_rows = qr.shape[0]
    tq = min(ATT_TQ, t_rows)
    nb = t_rows // tq
    scale = ATT_D ** -0.5

    c2 = scale * _LOG2E

    def body(q_ref, kf_ref, kvb_ref, o_ref, lse_ref):
        qi = pl.program_id(1)
        lane = lax.broadcasted_iota(jnp.int32, (tq, LANES), 1)
        causal_t = (lax.broadcasted_iota(jnp.int32, (tq, tq), 0) <= lax.broadcasted_iota(jnp.int32, (tq, tq), 1))
        heads = [slice(hh * LANES, (hh + 1) * LANES) for hh in range(2)]
        qs = [q_ref[:, cols] for cols in heads]

        def block(first, n_keys, carry, diagonal):
            rows = pl.ds(pl.multiple_of(first * tq, tq), n_keys)
            new = []
            for q, cols, (m_old, l_old, acc_t) in zip(qs, heads, carry):
                s_t = lax.dot_general(kf_ref[rows, cols], q, _NT, preferred_element_type=F32)
                if diagonal:
                    s_t = jnp.where(causal_t, s_t, -1e30)
                m_new = jnp.maximum(m_old, jnp.max(s_t, axis=0, keepdims=True))
                p_t = jnp.exp2((s_t - m_new) * c2)
                a = jnp.exp2((m_old - m_new) * c2)
                pv_t = lax.dot_general(kvb_ref[rows, cols], p_t.astype(BF16), _TN, preferred_element_type=F32)
                new.append((m_new, a * l_old + jnp.sum(p_t, axis=0, keepdims=True), a * acc_t + pv_t))
            return tuple(new)

        init = (jnp.full((1, tq), -1e30, F32), jnp.zeros((1, tq), F32), jnp.zeros((LANES, tq), F32))
        carry = lax.fori_loop(0, qi // 2, lambda g, c: block(2 * g, 2 * tq, c, False), (init, init))
        carry = lax.cond(qi % 2 == 1, lambda c: block(qi - 1, tq, c, False), lambda c: c, carry)
        outs = []
        for hh, (m_fin, l_fin, acc_t) in enumerate(block(qi, tq, carry, True)):
            lse_ref[hh] = m_fin * scale + jnp.log(l_fin)
            outs.append((acc_t / l_fin).T)
        o_ref[...] = jnp.where(lane < HEAD, pltpu.roll(outs[0], HEAD, 1), outs[1])

    pair = pl.BlockSpec((t_rows, 2 * LANES), lambda pr, qi: (0, pr))
    return _call(
        body, (qr, kf, kvb), name=name, grid=(N_ATT_HEADS // 2, nb),
        in_specs=[pl.BlockSpec((tq, 2 * LANES), lambda pr, qi: (qi, pr)), pair, pair],
        out_specs=[pl.BlockSpec((tq, LANES), lambda pr, qi: (qi, pr)),
                   pl.BlockSpec((2, 1, tq), lambda pr, qi: (pr, 0, qi))],
        out_shape=[jax.ShapeDtypeStruct((t_rows, N_ATT_HEADS * HEAD), F32),
                   jax.ShapeDtypeStruct((N_ATT_HEADS, 1, t_rows), F32)],
        sem=("parallel", "arbitrary"), job=job)


def _attn_bwd(qr, kf, kvb, dcat, o, lse, *, name, job=None):
    t_rows = qr.shape[0]
    tq = min(ATT_TQ, t_rows)
    nb = t_rows // tq
    scale = ATT_D ** -0.5
    c2 = scale * _LOG2E
    do_off = 2 * HG_W // LANES

    def body(q_ref, kf_ref, kvb_ref, do_ref, o_ref, lse_ref, dq_ref, dkv_ref, dk_ref):
        ki = pl.program_id(1)

        @pl.when(ki == 0)
        def _():
            dq_ref[...] = jnp.zeros_like(dq_ref)

        lane = lax.broadcasted_iota(jnp.int32, (tq, LANES), 1)
        causal_t = (lax.broadcasted_iota(jnp.int32, (tq, tq), 0) <= lax.broadcasted_iota(jnp.int32, (tq, tq), 1))
        heads = [slice(hh * LANES, (hh + 1) * LANES) for hh in range(2)]
        ks = [kf_ref[:, cols] for cols in heads]
        vs = [kvb_ref[:, cols] for cols in heads]

        def block(qi, n_q, carry, diagonal):
            rows = pl.ds(pl.multiple_of(qi * tq, tq), n_q)
            do_pair, o_pair = do_ref[rows, :], o_ref[rows, :]
            upper = lax.broadcasted_iota(jnp.int32, do_pair.shape, 1) >= HEAD
            new = []
            for hh, (cols, k, v, (dk, dv)) in enumerate(zip(heads, ks, vs, carry)):
                q = q_ref[rows, cols]
                do, ov = (pltpu.roll(do_pair, HEAD, 1), pltpu.roll(o_pair, HEAD, 1)) if hh == 0 else (do_pair, o_pair)
                do = jnp.where(upper, do, 0.0)
                delta = jnp.sum((do * ov).T, axis=0, keepdims=True)
                s_t = lax.dot_general(k, q, _NT, preferred_element_type=F32)
                if diagonal:
                    s_t = jnp.where(causal_t, s_t, -1e30)
                p_t = jnp.exp2(s_t * c2 - lse_ref[hh, :, rows] * _LOG2E)
                dob = do.astype(BF16)
                dv = dv + jnp.dot(p_t.astype(BF16), dob, preferred_element_type=F32)
                dp_t = lax.dot_general(v, dob, _NT, preferred_element_type=F32)
                ds_t = (p_t * (dp_t - delta) * scale).astype(BF16)
                dk = dk + jnp.dot(ds_t, q, preferred_element_type=F32)
                dq_ref[rows, cols] += lax.dot_general(ds_t, k, _TN, preferred_element_type=F32)
                new.append((dk, dv))
            return tuple(new)

        zero = jnp.zeros((tq, LANES), F32)
        carry = block(ki, tq, ((zero, zero), (zero, zero)), True)
        rest = nb - 1 - ki
        carry = lax.fori_loop(0, rest // 2, lambda g, c: block(ki + 1 + 2 * g, 2 * tq, c, False), carry)
        carry = lax.cond(rest % 2 == 1, lambda c: block(nb - 1, tq, c, False), lambda c: c, carry)
        dkv_ref[...] = jnp.concatenate([jnp.where(lane < HEAD, dk, dv) for dk, dv in carry],
                                       axis=1).astype(dkv_ref.dtype)
        dk_ref[...] = jnp.concatenate([dk for dk, _ in carry], axis=1)

    pair_all = pl.BlockSpec((t_rows, 2 * LANES), lambda pr, ki: (0, pr))
    pair_blk = pl.BlockSpec((tq, 2 * LANES), lambda pr, ki: (ki, pr))
    wide = jax.ShapeDtypeStruct((t_rows, N_ATT_HEADS * LANES), F32)
    return _call(
        body, (qr, kf, kvb, dcat, o, lse), name=name, grid=(N_ATT_HEADS // 2, nb),
        in_specs=[pair_all, pair_blk, pair_blk,
                  pl.BlockSpec((t_rows, LANES), lambda pr, ki: (0, do_off + pr)),
                  pl.BlockSpec((t_rows, LANES), lambda pr, ki: (0, pr)),
                  pl.BlockSpec((2, 1, t_rows), lambda pr, ki: (pr, 0, 0))],
        out_specs=[pair_all, pair_blk, pair_blk],
        out_shape=[wide, jax.ShapeDtypeStruct(wide.shape, BF16), wide],
        sem=("parallel", "arbitrary"), job=job)


def _my_pos():
    return lax.axis_index("x"), lax.axis_index("y"), lax.axis_index("c")


def _all_gather(xs, *, name):
    return _gather_forward(_run_job(_gather_job(xs), name=name), name=name + "_forward")


def _remote(src, dst, send_sems, recv_sems, k, dev):
    return pltpu.make_async_remote_copy(src_ref=src, dst_ref=dst, send_sem=send_sems.at[k], recv_sem=recv_sems.at[k],
                                        device_id=dev, device_id_type=MESH)


def _gather_job(xs):
    n = len(xs)

    def make(x_refs, out_refs, send_sems, recv_sems, local_sems):
        mx, my, mc = _my_pos()
        mine = 4 * mx + 2 * my + mc
        peers = [(mx, my, 1 - mc), (1 - mx, my, mc), (mx, 1 - my, mc), (1 - mx, 1 - my, mc)]
        sends, recvs, local = [], [], []
        for a in range(n):
            local.append(pltpu.make_async_copy(x_refs[a], out_refs[a].at[mine], local_sems.at[a]))
            for k, dev in enumerate(peers):
                theirs = 4 * dev[0] + 2 * dev[1] + dev[2]
                sends.append(_remote(x_refs[a], out_refs[a].at[mine], send_sems, recv_sems, 4 * a + k, dev))
                recvs.append(_remote(x_refs[a], out_refs[a].at[theirs], send_sems, recv_sems, 4 * a + k, dev))
        return sends, recvs, local

    shapes = [jax.ShapeDtypeStruct((N_DEV,) + x.shape, x.dtype) for x in xs]
    return _copies_job(xs, shapes, 4 * n, n, make)


def _gather_forward(gs, *, name):
    n = len(gs)

    def body(*refs):
        out_refs = refs[n:2 * n]
        send_sems, recv_sems = refs[2 * n:]
        mx, my, mc = _my_pos()
        chips = [(1 - mx, my), (mx, 1 - my), (1 - mx, 1 - my)]
        sends, recvs = [], []
        for a in range(n):
            for j, (cx, cy) in enumerate(chips):
                here, there = out_refs[a].at[4 * cx + 2 * cy + mc], out_refs[a].at[4 * cx + 2 * cy + 1 - mc]
                sends.append(_remote(here, here, send_sems, recv_sems, 3 * a + j, (mx, my, 1 - mc)))
                recvs.append(_remote(here, there, send_sems, recv_sems, 3 * a + j, (mx, my, 1 - mc)))
        for cp in sends:
            cp.start()
        for cp in recvs:
            cp.wait_recv()
        for cp in sends:
            cp.wait_send()

    return pl.pallas_call(
        body, name=name, out_shape=[jax.ShapeDtypeStruct(g.shape, g.dtype) for g in gs],
        in_specs=[_ANY] * n, out_specs=[_ANY] * n, input_output_aliases={a: a for a in range(n)},
        scratch_shapes=[pltpu.SemaphoreType.DMA((3 * n,)), pltpu.SemaphoreType.DMA((3 * n,))],
    )(*gs)


def _pair_job(xs):
    n = len(xs)

    def make(x_refs, out_refs, send_sems, recv_sems, local_sems):
        mx, my, mc = _my_pos()
        copies = [_remote(x_refs[a].at[g, 1 - mc], out_refs[a].at[g], send_sems, recv_sems, 4 * a + g, (mx, my, 1 - mc))
                  for a in range(n) for g in range(4)]
        return copies, copies, []

    shapes = [jax.ShapeDtypeStruct((4,) + x.shape[2:], x.dtype) for x in xs]
    return _copies_job(xs, shapes, 4 * n, 0, make)


def _pair_add(x, r, core, *, name):
    _, _, a, b = x.shape
    ta = _row_tile(a, 256)

    def body(c_ref, x_ref, r_ref, o_ref):
        o_ref[...] = (x_ref[...] + r_ref[...]).astype(o_ref.dtype)

    blk = pl.BlockSpec((None, ta, b), lambda g, i, c_ref: (g, i, 0))
    return pl.pallas_call(
        body, name=name,
        grid_spec=pltpu.PrefetchScalarGridSpec(
            num_scalar_prefetch=1, grid=(4, a // ta),
            in_specs=[pl.BlockSpec((None, None, ta, b), lambda g, i, c_ref: (g, c_ref[0], i, 0)), blk],
            out_specs=blk),
        out_shape=jax.ShapeDtypeStruct((4, a, b), BF16),
        compiler_params=_cparams(("parallel", "parallel")),
    )(core, x, r)


def _quad_job(xs):
    n = len(xs)

    def make(x_refs, out_refs, send_sems, recv_sems, local_sems):
        mx, my, mc = _my_pos()
        mine = 2 * mx + my
        peers = [((1 - mx, my, mc), 2 * (1 - mx) + my), ((mx, 1 - my, mc), 2 * mx + 1 - my),
                 ((1 - mx, 1 - my, mc), 2 * (1 - mx) + 1 - my)]
        sends, recvs, local = [], [], []
        for a in range(n):
            local.append(pltpu.make_async_copy(x_refs[a].at[mine], out_refs[a].at[mine], local_sems.at[a]))
            for k, (dev, g) in enumerate(peers):
                sends.append(_remote(x_refs[a].at[g], out_refs[a].at[mine], send_sems, recv_sems, 3 * a + k, dev))
                recvs.append(_remote(x_refs[a].at[g], out_refs[a].at[g], send_sems, recv_sems, 3 * a + k, dev))
        return sends, recvs, local

    shapes = [jax.ShapeDtypeStruct(x.shape, x.dtype) for x in xs]
    return _copies_job(xs, shapes, 3 * n, n, make)


def _row_tile(r, pref):
    t = min(pref, r)
    while r % t or (t % 8 and t != r):
        t -= 1
    return t


def _adamw(parts, w, m, v, layer, *, name, tile=256, into=None):
    g, a, b = parts.shape
    tile = _row_tile(a, tile)
    c1 = 1.0 / (1.0 - ADAM_B1 ** ADAM_STEP)
    c2 = 1.0 / (1.0 - ADAM_B2 ** ADAM_STEP)
    into = tuple(into or ())

    def body(p_ref, w_ref, m_ref, v_ref, *refs):
        g_ref, d_ref, mo_ref, vo_ref = refs[len(into):]
        grad = p_ref[0].astype(F32)
        for j in range(1, g):
            grad = grad + p_ref[j].astype(F32)
        mn = ADAM_B1 * m_ref[...] + (1.0 - ADAM_B1) * grad
        vn = ADAM_B2 * v_ref[...] + (1.0 - ADAM_B2) * (grad * grad)
        g_ref[...] = grad
        mo_ref[...] = mn
        vo_ref[...] = vn
        d_ref[...] = -ADAM_LR * ((mn * c1) / (jnp.sqrt(vn * c2) + ADAM_EPS) + ADAM_WD * w_ref[...])

    if layer is None:
        src, shape = pl.BlockSpec((tile, b), lambda i: (i, 0)), (a, b)
    else:
        src, shape = pl.BlockSpec((None, tile, b), lambda i: (layer, i, 0)), w.shape
    return pl.pallas_call(
        body, name=name, grid=(a // tile,),
        in_specs=[pl.BlockSpec((g, tile, b), lambda i: (0, i, 0)), src, src, src] + [_ANY] * len(into),
        out_specs=[src] * 4,
        out_shape=[jax.ShapeDtypeStruct(shape, F32)] * 4,
        input_output_aliases={4 + i: i for i in range(len(into))},
        compiler_params=_cparams(("parallel",)),
    )(parts, w, m, v, *into)


W_IN_SHARD = 276


def _w_in_dest(col):
    return jnp.where(col < P_KR, col, jnp.where(col < P_KR + 256, col + (P_CKV - P_KR), col - 2176 + P_KR + HEAD))


PLACE_TILE = 384
PLACE_SHARDS = 3
PICK_TILE = 128
PICK_TILES = 4


def _w_in_tables():
    col = np.arange(N_DEV * W_IN_SHARD)
    dest = np.where(col < P_KR, col, np.where(col < P_KR + 256, col + (P_CKV - P_KR), col - 2176 + P_KR + HEAD))
    shard = col // W_IN_SHARD

    def filled(used, universe, n):
        used = sorted(set(int(u) for u in used))
        assert len(used) <= n, used
        return used + [u for u in universe if u not in used][:n - len(used)]

    place = [filled(shard[dest // PLACE_TILE == c], range(N_DEV), PLACE_SHARDS) for c in range(P_COLS // PLACE_TILE)]
    pick = [filled(dest[shard == j] // PICK_TILE, range(P_COLS // PICK_TILE), PICK_TILES) for j in range(N_DEV)]
    return np.asarray(place, np.int32).reshape(-1), np.asarray(pick, np.int32).reshape(-1)


def _place_w_in(g, *, name):
    _, d, sh = g.shape
    tc, ns = PLACE_TILE, PLACE_SHARDS
    table = jnp.asarray(_w_in_tables()[0])

    def body(tab_ref, g_ref, o_ref, acc_ref):
        ct, s = pl.program_id(0), pl.program_id(1)
        j = tab_ref[ct * ns + s]

        @pl.when(s == 0)
        def _():
            acc_ref[...] = jnp.zeros_like(acc_ref)

        src = j * sh + lax.broadcasted_iota(jnp.int32, (sh, tc), 0)
        dst = ct * tc + lax.broadcasted_iota(jnp.int32, (sh, tc), 1)
        place = (_w_in_dest(src) == dst).astype(BF16)
        acc_ref[...] += jnp.dot(g_ref[...], place, preferred_element_type=F32)

        @pl.when(s == ns - 1)
        def _():
            o_ref[...] = acc_ref[...].astype(o_ref.dtype)

    return pl.pallas_call(
        body, name=name,
        grid_spec=pltpu.PrefetchScalarGridSpec(
            num_scalar_prefetch=1, grid=(P_COLS // tc, ns),
            in_specs=[pl.BlockSpec((None, d, sh), lambda ct, s, tab: (tab[ct * ns + s], 0, 0))],
            out_specs=pl.BlockSpec((d, tc), lambda ct, s, tab: (0, ct)),
            scratch_shapes=[pltpu.VMEM((d, tc), F32)]),
        out_shape=jax.ShapeDtypeStruct((d, P_COLS), BF16),
        compiler_params=_cparams(("parallel", "arbitrary")),
    )(table, g)


def _unplace_w_in(dw, *, name):
    d = dw.shape[0]
    sh, tk, nt = W_IN_SHARD, PICK_TILE, PICK_TILES
    table = jnp.asarray(_w_in_tables()[1])

    def body(tab_ref, dw_ref, o_ref):
        j, kk = pl.program_id(0), pl.program_id(1)
        tile = tab_ref[j * nt + kk]
        src = j * sh + lax.broadcasted_iota(jnp.int32, (tk, sh), 1)
        dst = tile * tk + lax.broadcasted_iota(jnp.int32, (tk, sh), 0)
        pick = (_w_in_dest(src) == dst).astype(BF16)
        part = _split_dot(dw_ref[...], pick)

        @pl.when(kk == 0)
        def _():
            o_ref[...] = part

        @pl.when(kk > 0)
        def _():
            o_ref[...] += part

    return pl.pallas_call(
        body, name=name,
        grid_spec=pltpu.PrefetchScalarGridSpec(
            num_scalar_prefetch=1, grid=(N_DEV, nt),
            in_specs=[pl.BlockSpec((d, tk), lambda j, kk, tab: (0, tab[j * nt + kk]))],
            out_specs=pl.BlockSpec((None, d, sh), lambda j, kk, tab: (j, 0, 0))),
        out_shape=jax.ShapeDtypeStruct((N_DEV, d, sh), F32),
        compiler_params=_cparams(("parallel", "arbitrary")),
    )(table, dw)


def _gate_up_swiglu(h1, wgu, *, name):
    t_rows, k = h1.shape
    w = wgu.shape[2]
    tm = _tile(t_rows, 1024)

    def body(a_ref, wg_ref, wu_ref, gu_ref, act_ref):
        a = a_ref[...].astype(BF16)
        gate = jnp.dot(a, wg_ref[...], preferred_element_type=F32)
        up = jnp.dot(a, wu_ref[...], preferred_element_type=F32)
        gu_ref[0] = gate.astype(gu_ref.dtype)
        gu_ref[1] = up.astype(gu_ref.dtype)
        act_ref[...] = (gate * _sigmoid(gate) * up).astype(act_ref.dtype)

    return pl.pallas_call(
        body, name=name, grid=(t_rows // tm, 4),
        in_specs=[pl.BlockSpec((tm, k), lambda i, j: (i, 0)),
                  pl.BlockSpec((None, k, w), lambda i, j: (j, 0, 0)),
                  pl.BlockSpec((None, k, w), lambda i, j: (j + 4, 0, 0))],
        out_specs=[pl.BlockSpec((2, None, tm, w), lambda i, j: (0, j, i, 0)),
                   pl.BlockSpec((None, tm, w), lambda i, j: (j, i, 0))],
        out_shape=[jax.ShapeDtypeStruct((2, 4, t_rows, w), BF16), jax.ShapeDtypeStruct((4, t_rows, w), BF16)],
        compiler_params=_cparams(("parallel", "arbitrary")),
    )(h1, wgu, wgu)


def _down_dx_swiglu(dffn, wdown, gu, *, name):
    t_rows, k = dffn.shape
    w = gu.shape[3]
    tm = _tile(t_rows, 1024)

    def body(d_ref, w_ref, gu_ref, o_ref):
        dact = lax.dot_general(d_ref[...].astype(BF16), w_ref[...], _NT, preferred_element_type=F32)
        gate, up = gu_ref[0].astype(F32), gu_ref[1].astype(F32)
        sg = _sigmoid(gate)
        o_ref[0] = (dact * up * (sg * (1.0 + gate * (1.0 - sg)))).astype(o_ref.dtype)
        o_ref[1] = (dact * gate * sg).astype(o_ref.dtype)

    blk = pl.BlockSpec((2, None, tm, w), lambda i, j: (0, j, i, 0))
    return pl.pallas_call(
        body, name=name, grid=(t_rows // tm, 4),
        in_specs=[pl.BlockSpec((tm, k), lambda i, j: (i, 0)), pl.BlockSpec((w, k), lambda i, j: (j, 0)), blk],
        out_specs=blk, out_shape=jax.ShapeDtypeStruct(gu.shape, BF16),
        compiler_params=_cparams(("parallel", "arbitrary")),
    )(dffn, wdown, gu)


BIG = ("w_in", "mla_w_uq", "mla_w_ukv", "w_out", "w_gate_up", "w_down", "ple_w_gate", "ple_w_proj")
SMALL = ("ln_in_g", "ln_in_b", "hgrn_lb_logits", "hgrn_norm_g", "sgu_ln_g", "sgu_ln_b", "sgu_w_s", "sgu_b_s",
         "mla_q_norm_g", "mla_kv_norm_g", "ln1_g", "ln1_b", "ln2_g", "ln2_b")
ORDER = ("ln_in_g", "ln_in_b", "w_in", "hgrn_lb_logits", "hgrn_norm_g", "sgu_ln_g", "sgu_ln_b", "sgu_w_s", "sgu_b_s",
         "mla_q_norm_g", "mla_w_uq", "mla_kv_norm_g", "mla_w_ukv", "w_out", "ln1_g", "ln1_b", "w_gate_up", "w_down",
         "ple_w_gate", "ple_w_proj", "ln2_g", "ln2_b")


def _slab(a, align):
    s = a.reshape(-1, LANES)
    pad = -s.shape[0] % align
    return jnp.pad(s, ((0, pad), (0, 0))) if pad else s


def _pack(arrays, align=16, total_align=512):
    s = jnp.concatenate([_slab(a, align) for a in arrays], axis=0)
    pad = -s.shape[0] % total_align
    return jnp.pad(s, ((0, pad), (0, 0))) if pad else s


def _unpack(slab, shapes, align=16):
    out, r0 = [], 0
    for s in shapes:
        nr = math.prod(s) // LANES
        out.append(slab[r0:r0 + nr].reshape(s))
        r0 += nr + (-nr % align)
    return out


def _blocks_to_cols(g, *, name):
    nb, a, b = g.shape

    def body(g_ref, o_ref):
        o_ref[...] = g_ref[...]

    return pl.pallas_call(
        body, name=name, grid=(nb,), in_specs=[pl.BlockSpec((None, a, b), lambda j: (j, 0, 0))],
        out_specs=pl.BlockSpec((a, b), lambda j: (0, j)), out_shape=jax.ShapeDtypeStruct((a, nb * b), g.dtype),
        compiler_params=_cparams(("parallel",)),
    )(g)


def _cols_to_blocks(x, *, name):
    a, b = x.shape[0], x.shape[1] // N_DEV

    def body(x_ref, o_ref):
        o_ref[...] = x_ref[...]

    return pl.pallas_call(
        body, name=name, grid=(N_DEV,), in_specs=[pl.BlockSpec((a, b), lambda j: (0, j))],
        out_specs=pl.BlockSpec((None, a, b), lambda j: (j, 0, 0)), out_shape=jax.ShapeDtypeStruct((N_DEV, a, b), x.dtype),
        compiler_params=_cparams(("parallel",)),
    )(x)


def _weight_shards(w, li):
    uq_pad = ((0, 0), (0, LANES - ATT_D))
    shards = {k: w[k][li] for k in BIG}
    shards["mla_w_uq"] = jnp.pad(shards["mla_w_uq"], uq_pad)
    return {k: s.astype(BF16) for k, s in shards.items()}


def _usable_weights(g, *, name):
    out = {}
    for k, a in g.items():
        if k == "w_in":
            out[k] = _place_w_in(a, name=name + "_place_w_in")
        elif k in ("w_out", "w_down", "ple_w_gate"):
            out[k] = a.reshape(a.shape[0] * a.shape[1], a.shape[2])
        elif k == "w_gate_up":
            out[k] = a
        else:
            out[k] = _blocks_to_cols(a, name=name + "_cols_" + k)
    return out


def _as_pairs(g):
    if g.ndim == 2:
        return g.reshape((4, 2, g.shape[0] // N_DEV) + g.shape[1:])
    return g.reshape((4, 2) + g.shape[1:])


def _twice(fn):
    return lambda *a: fn(*a) * 2


def _layer_forward(li, h, hb, p_i, wts, sm, lbs, tables, alpha, hgrn_job=None, after_hgrn=None, attn_job=None,
                   after_attn=None):
    n = f"l{li}_"
    row1 = lambda a: a.reshape(1, -1)
    projp = _mm(hb, wts["w_in"], name=n + "proj")
    ng = row1(sm["hgrn_norm_g"][li])
    res = _hgrn_fwd(projp, lbs[li], ng, name=n + "hgrn_fwd", job=hgrn_job)
    if hgrn_job is not None:
        res, got = res
        wts = dict(wts, **after_hgrn(got))
    o_a, o_pre, states = res
    lg, lbias = row1(sm["sgu_ln_g"][li]), row1(sm["sgu_ln_b"][li])
    w_s = sm["sgu_w_s"][li]
    bias_full = jnp.repeat(sm["sgu_b_s"][li].T, HEAD, axis=1)
    o_b = _sgu_fwd(projp, lg, lbias, w_s, bias_full, name=n + "sgu_fwd")
    qg, kvg = row1(sm["mla_q_norm_g"][li]), row1(sm["mla_kv_norm_g"][li])
    cq_view, ckv_view = (projp, 384, P_CQ // 384), (projp, 256, P_CKV // 256)
    (cqn,) = _rowwise(_fn_rms, [cq_view], [qg], [(384, BF16)], name=n + "q_norm")
    (ckvn,) = _rowwise(_fn_rms, [ckv_view], [kvg], [(256, BF16)], name=n + "kv_norm")
    q = _mm(cqn, wts["mla_w_uq"], name=n + "uq")
    kv = _mm(ckvn, wts["mla_w_ukv"], name=n + "ukv")
    qr, kf, kvb = _mla_prep(q, kv, projp, tables, name=n + "mla_prep")
    res = _attn_fwd(qr, kf, kvb, name=n + "attn_fwd", job=attn_job)
    if attn_job is not None:
        res, got = res
        wts = dict(wts, **after_attn(got))
    o_c, lse = res
    cat = jnp.concatenate([o_a, o_b, o_c.astype(BF16)], axis=1)
    mix = _mm(cat, wts["w_out"], name=n + "out_proj")
    g1, b1 = row1(sm["ln1_g"][li]), row1(sm["ln1_b"][li])
    d = h.shape[1]
    h1, h1b = _rowwise(_twice(_make_post_mix(alpha)), [h, mix], [g1, b1], [(d, F32), (d, BF16)], name=n + "ln1")
    gu, act = _gate_up_swiglu(h1b, wts["w_gate_up"], name=n + "gate_up")
    ffn = _mm(act, wts["w_down"], am="bmk", name=n + "down")
    pg = _mm(h1b, wts["ple_w_gate"], name=n + "ple_gate")
    pp = _mm(p_i, wts["ple_w_proj"], name=n + "ple_proj")
    g2, b2 = row1(sm["ln2_g"][li]), row1(sm["ln2_b"][li])
    h2, h2b = _rowwise(_twice(_make_ple_ln(alpha)), [h1, ffn, pg, pp], [g2, b2], [(d, F32), (d, BF16)],
                       name=n + "ln2")
    saved = dict(h=h, hb=hb, h1b=h1b, projp=projp, o_pre=o_pre, states=states, cqn=cqn, ckvn=ckvn, qr=qr, kf=kf, kvb=kvb, o_c=o_c,
                 lse=lse, cat=cat, mix=mix, h1=h1, gu=gu, act=act, ffn=ffn, pg=pg, pp=pp, ng=ng, lg=lg, wts=wts,
                 lbias=lbias, w_s=w_s, bias_full=bias_full, qg=qg, kvg=kvg, g1=g1, b1=b1, g2=g2, b2=b2)
    return (h2, h2b), saved


RS_EARLY = ("ple_w_proj", "ple_w_gate", "w_down", "w_gate_up", "w_out")
RS_LATE = ("mla_w_uq", "mla_w_ukv", "w_in")


def _layer_backward(li, dh2_parts, p_i, sv, lbs, tables, alpha, core, carried=None):
    n = f"l{li}_b_"
    wts = sv["wts"]
    gr = {}
    dh1_a, dffn, dpg, dpp, gr["ln2_g"], gr["ln2_b"] = _rowwise_vjp(
        _make_ple_ln(alpha), [sv["h1"], sv["ffn"], sv["pg"], sv["pp"]], [sv["g2"], sv["b2"]], [dh2_parts],
        groups=[[0], [1], [2], [3]], gdtypes=[F32, BF16, BF16, BF16], name=n + "ln2")
    big = {}
    big["ple_w_proj"] = _cols_to_blocks(_mm(p_i, dpp, am="km", name=n + "ple_proj_dw"), name=n + "ple_proj_dw_blocks")
    big["ple_w_gate"] = _mm(sv["h1b"], dpg, am="km", name=n + "ple_gate_dw")
    dh1_b = _mm(dpg, wts["ple_w_gate"], bm="nk", name=n + "ple_gate_dx")
    big["w_down"] = _mm(sv["act"], dffn, am="bkm", name=n + "down_dw")
    dgu = _down_dx_swiglu(dffn, wts["w_down"], sv["gu"], name=n + "down_dx")
    dgu = dgu.reshape((N_DEV,) + dgu.shape[2:])
    big["w_gate_up"], carried_got = _mm(sv["h1b"], dgu, am="km", bm="bkn", om="bmn", name=n + "gate_up_dw",
                                        job=carried), None
    if carried is not None:
        big["w_gate_up"], carried_got = big["w_gate_up"]
    early = [_as_pairs(big[k]) for k in RS_EARLY[:-1]]
    dh1_c, theirs = _mm(dgu, wts["w_gate_up"], am="bmk", bm="bnk", name=n + "gate_up_dx", job=_pair_job(early))
    dh_a, dmix, gr["ln1_g"], gr["ln1_b"] = _rowwise_vjp(
        _make_post_mix(alpha), [sv["h"], sv["mix"]], [sv["g1"], sv["b1"]], [[dh1_a, dh1_b, dh1_c]],
        groups=[[0], [1]], gdtypes=[F32, BF16], name=n + "ln1")
    big["w_out"] = _mm(sv["cat"], dmix, am="km", name=n + "out_proj_dw")
    early.append(_as_pairs(big["w_out"]))
    dcat, their_w_out = _mm(dmix, wts["w_out"], bm="nk", name=n + "out_proj_dx", job=_pair_job(early[-1:]))
    sums = [_pair_add(x, r, core, name=n + "pair_add_" + k)
            for k, x, r in zip(RS_EARLY, early, list(theirs) + list(their_w_out))]

    (dqr, dkv, dkf), early_quads = _attn_bwd(sv["qr"], sv["kf"], sv["kvb"], dcat, sv["o_c"], sv["lse"],
                                             name=n + "attn", job=_quad_job(sums))
    dqpad, dkr = _mla_prep_bwd(dqr, dkf, tables, name=n + "mla_prep")
    big["mla_w_uq"] = _cols_to_blocks(_mm(sv["cqn"], dqpad, am="km", name=n + "uq_dw"), name=n + "uq_dw_blocks")
    dcqn = _mm(dqpad, wts["mla_w_uq"], bm="nk", name=n + "uq_dx")
    big["mla_w_ukv"] = _cols_to_blocks(_mm(sv["ckvn"], dkv, am="km", name=n + "ukv_dw"), name=n + "ukv_dw_blocks")
    dckvn = _mm(dkv, wts["mla_w_ukv"], bm="nk", name=n + "ukv_dx")
    projp = sv["projp"]
    dcq, gr["mla_q_norm_g"] = _rowwise_vjp(_fn_rms, [(projp, 384, P_CQ // 384)], [sv["qg"]], [[dcqn]],
                                           groups=[[0]], gdtypes=[BF16], name=n + "q_norm")
    dckv, gr["mla_kv_norm_g"] = _rowwise_vjp(_fn_rms, [(projp, 256, P_CKV // 256)], [sv["kvg"]], [[dckvn]],
                                             groups=[[0]], gdtypes=[BF16], name=n + "kv_norm")
    dsgu, gr["sgu_ln_g"], gr["sgu_ln_b"], gr["sgu_w_s"], gr["sgu_b_s"] = _sgu_bwd(
        projp, sv["lg"], sv["lbias"], sv["w_s"], sv["bias_full"], dcat, name=n + "sgu")
    dhg, gr["hgrn_norm_g"], gr["lower_bound"] = _hgrn_bwd(
        projp, lbs[li], sv["ng"], sv["o_pre"], sv["states"], dcat, name=n + "hgrn")
    dprojp = jnp.concatenate([dhg, dsgu, dcq, dkr, dckv], axis=1)
    big["w_in"] = _unplace_w_in(_mm(sv["hb"], dprojp, am="km", name=n + "proj_dw"), name=n + "proj_dw_shards")
    late = [_as_pairs(big[k]) for k in RS_LATE]
    dh_b, theirs = _mm(dprojp, wts["w_in"], bm="nk", name=n + "proj_dx", job=_pair_job(late))
    late_sums = [_pair_add(x, r, core, name=n + "pair_add_" + k) for k, x, r in zip(RS_LATE, late, theirs)]
    return [dh_a, dh_b], gr, early_quads, late_sums, carried_got


def kernel(x, p, positions, ln_in_g, ln_in_b, w_in, hgrn_lb_logits, hgrn_norm_g, sgu_ln_g, sgu_ln_b, sgu_w_s, sgu_b_s, mla_q_norm_g, mla_w_uq, mla_kv_norm_g, mla_w_ukv, w_out, ln1_g, ln1_b, w_gate_up, w_down, ple_w_gate, ple_w_proj, ln2_g, ln2_b, loss_target, m_ln_in_g, m_ln_in_b, m_w_in, m_hgrn_lb_logits, m_hgrn_norm_g, m_sgu_ln_g, m_sgu_ln_b, m_sgu_w_s, m_sgu_b_s, m_mla_q_norm_g, m_mla_w_uq, m_mla_kv_norm_g, m_mla_w_ukv, m_w_out, m_ln1_g, m_ln1_b, m_w_gate_up, m_w_down, m_ple_w_gate, m_ple_w_proj, m_ln2_g, m_ln2_b, v_ln_in_g, v_ln_in_b, v_w_in, v_hgrn_lb_logits, v_hgrn_norm_g, v_sgu_ln_g, v_sgu_ln_b, v_sgu_w_s, v_sgu_b_s, v_mla_q_norm_g, v_mla_w_uq, v_mla_kv_norm_g, v_mla_w_ukv, v_w_out, v_ln1_g, v_ln1_b, v_w_gate_up, v_w_down, v_ple_w_gate, v_ple_w_proj, v_ln2_g, v_ln2_b):
    args = dict(locals())
    w = {k: args[k] for k in ORDER}
    m = {k: args["m_" + k] for k in ORDER}
    v = {k: args["v_" + k] for k in ORDER}
    depth = w_in.shape[0]
    assert depth == 2, "the lower-bound kernel is written for two layers"
    alpha = (2 * depth) ** 0.25
    xs, tgt = x[0], loss_target[0]
    d_model = xs.shape[1]

    shards = [_weight_shards(w, li) for li in range(depth)]
    on_hgrn0 = ("mla_w_uq", "mla_w_ukv", "w_out", "ple_w_gate", "ple_w_proj")
    ffn0 = ("w_gate_up", "w_down")
    first1 = ("w_in", "mla_w_uq", "mla_w_ukv", "w_out")
    on_attn1 = ("w_gate_up", "w_down", "ple_w_gate", "ple_w_proj")
    layer1_first = {}

    def after_hgrn0(got):
        got = _gather_forward(got, name="gather_l0a_forward")
        return _usable_weights(dict(zip(on_hgrn0, got)), name="l0")

    def after_attn0(got):
        got = _gather_forward(got, name="gather_l0b_forward")
        layer1_first.update(_usable_weights(dict(zip(first1, got[len(ffn0):])), name="l1"))
        return _usable_weights(dict(zip(ffn0, got[:len(ffn0)])), name="l0")

    def after_attn1(got):
        got = _gather_forward(got, name="gather_l1_forward")
        return _usable_weights(dict(zip(on_attn1, got)), name="l1")

    tables = _rope_tables(positions[0])
    row1 = lambda a: a.reshape(1, -1)
    l0, l1 = row1(hgrn_lb_logits[0]), row1(hgrn_lb_logits[1])
    lbs = _rowwise(_fn_lower_bounds, [l0, l1], [], [(HG_W, F32), (HG_W, F32)], name="lower_bounds")

    gin, bin_ = row1(ln_in_g), row1(ln_in_b)
    (h, hb), g_in = _rowwise(_twice(_fn_ln), [xs], [gin, bin_], [(d_model, F32), (d_model, BF16)], name="ln_in",
                             job=_gather_job([shards[0]["w_in"]]))
    w_in0 = _usable_weights({"w_in": _gather_forward(g_in, name="gather_l0_w_in_forward")[0]}, name="l0")
    (h, hb), sv0 = _layer_forward(
        0, h, hb, p[0, 0], w_in0, w, lbs, tables, alpha,
        hgrn_job=_gather_job([shards[0][k] for k in on_hgrn0]), after_hgrn=after_hgrn0,
        attn_job=_gather_job([shards[0][k] for k in ffn0] + [shards[1][k] for k in first1]), after_attn=after_attn0)
    (h, _), sv1 = _layer_forward(
        1, h, hb, p[1, 0], layer1_first, w, lbs, tables, alpha,
        attn_job=_gather_job([shards[1][k] for k in on_attn1]), after_attn=after_attn1)
    saved = [sv0, sv1]
    dy, loss_local = _loss_and_grad(h, tgt, name="loss")
    loss = lax.psum(loss_local[0, 0], ("x", "y", "c"))

    core = lax.axis_index("c").astype(jnp.int32).reshape(1)
    dparts, grads, quads, carried = [dy], [None] * depth, [None] * depth, None
    for li in reversed(range(depth)):
        dparts, grads[li], early_quads, late_sums, late_quads = _layer_backward(
            li, dparts, p[li, 0], saved[li], lbs, tables, alpha, core, carried=carried)
        quads[li] = dict(zip(RS_EARLY, early_quads))
        if carried is not None:
            quads[li + 1].update(zip(RS_LATE, late_quads))
        carried = _quad_job(late_sums)
    (dx, d_gin, d_bin), late_quads = _rowwise_vjp(_fn_ln, [xs], [gin, bin_], [dparts], groups=[[0]], name="ln_in_b",
                                                   job=carried)
    quads[0].update(zip(RS_LATE, late_quads))
    dl0, dl1 = _rowwise_vjp(_fn_lower_bounds, [l0, l1], [], [[grads[0]["lower_bound"]], [grads[1]["lower_bound"]]],
                            groups=[[0], [1]], name="lower_bounds_b")

    prefixes = ("grad_", "delta_", "new_m_", "new_v_")
    uq_pad = ((0, 0), (0, 0), (0, LANES - ATT_D))
    state = {k: ((jnp.pad(w[k], uq_pad), jnp.pad(m[k], uq_pad), jnp.pad(v[k], uq_pad)) if k == "mla_w_uq"
                 else (w[k], m[k], v[k])) for k in BIG}
    out = {}
    for k in BIG:
        res4 = None
        for li in range(depth):
            res4 = _adamw(quads[li][k], *state[k], li, name=f"adamw_l{li}_{k}", into=res4)
        for pre, a in zip(prefixes, res4):
            out[pre + k] = a[:, :, :ATT_D] if k == "mla_w_uq" else a

    small_g = {"ln_in_g": d_gin.reshape(-1), "ln_in_b": d_bin.reshape(-1),
               "hgrn_lb_logits": jnp.stack([dl0.reshape(-1), dl1.reshape(-1)])}
    for k in SMALL[3:]:
        small_g[k] = jnp.stack([grads[li][k].reshape(w[k].shape[1:]) for li in range(depth)])
    (small_parts,) = _all_gather([_pack([small_g[k] for k in SMALL])], name="gather_small_grads")
    slabs = _adamw(small_parts, _pack([w[k] for k in SMALL]), _pack([m[k] for k in SMALL]),
                   _pack([v[k] for k in SMALL]), None, name="adamw_small")
    shapes = [w[k].shape for k in SMALL]
    for pre, slab in zip(prefixes, slabs):
        for k, a in zip(SMALL, _unpack(slab, shapes)):
            out[pre + k] = a
    res = [loss, dx[None]]
    for prefix in ("grad_", "delta_", "new_m_", "new_v_"):
        res += [out[prefix + k] for k in ORDER]
    return tuple(res)
```

```python
import functools
import math

import jax
import jax.numpy as jnp
import numpy as np
from jax import lax
from jax.experimental import pallas as pl
from jax.experimental.pallas import tpu as pltpu

F32 = jnp.float32
BF16 = jnp.bfloat16
MESH = pl.DeviceIdType.MESH

LN_EPS = 1e-5
RMS_EPS = 1e-6
ROPE_THETA = 10000.0
ADAM_LR, ADAM_B1, ADAM_B2, ADAM_EPS, ADAM_WD, ADAM_STEP = 0.001, 0.9, 0.999, 1e-08, 0.01, 10

N_DEV = 8
LANES = 128
HG_CHUNK = 16
HG_W = 256
HEAD = 64
SGU_CHUNK = 128
N_ATT_HEADS = 8
ATT_D = 96
VMEM_LIMIT = 56 * 1024 * 1024

HG_TILE = 256
ATT_TQ = 512
ROW_TILE = 256

P_CQ, P_KR, P_CKV, P_COLS = 1536, 1920, 2048, 2304


def _cparams(sem):
    return pltpu.CompilerParams(dimension_semantics=sem, vmem_limit_bytes=VMEM_LIMIT)


_ANY = pl.BlockSpec(memory_space=pl.ANY)


def _call(body, operands, *, name, grid, in_specs, out_specs, out_shape, sem, scratch_shapes=(), job=None):
    if job is None:
        return pl.pallas_call(body, name=name, grid=grid, in_specs=in_specs, out_specs=out_specs, out_shape=out_shape,
                              scratch_shapes=list(scratch_shapes), compiler_params=_cparams(sem))(*operands)
    single = not isinstance(out_shape, (list, tuple))
    shapes = [out_shape] if single else list(out_shape)
    ospecs = [out_specs] if single else list(out_specs)
    ni, no, ns = len(operands), len(shapes), len(scratch_shapes)
    ji, jo = len(job.inputs), len(job.out_shapes)

    def hosted(*refs):
        p = 0
        parts = []
        for cnt in (ni, ji, no, jo, ns):
            parts.append(refs[p:p + cnt])
            p += cnt
        ins, jins, outs, jouts, scr = parts
        jsems = refs[p:]
        ids = [pl.program_id(a) for a in range(len(grid))]
        first = functools.reduce(lambda a, b: a & b, [i == 0 for i in ids])
        last = functools.reduce(lambda a, b: a & b, [i == g - 1 for i, g in zip(ids, grid)])

        @pl.when(first)
        def _():
            job.start(jins, jouts, jsems)

        body(*ins, *outs, *scr)

        @pl.when(last)
        def _():
            job.finish(jins, jouts, jsems)

    res = pl.pallas_call(
        hosted, name=name, grid=grid,
        in_specs=list(in_specs) + [_ANY] * ji, out_specs=ospecs + [_ANY] * jo,
        out_shape=shapes + list(job.out_shapes),
        scratch_shapes=list(scratch_shapes) + [pltpu.SemaphoreType.DMA((c,)) for c in job.sem_counts],
        compiler_params=_cparams(("arbitrary",) * len(grid)),
    )(*operands, *job.inputs)
    own = res[0] if single else res[:no]
    return own, res[no:]


class _Job:
    def __init__(self, inputs, out_shapes, sem_counts, start, finish):
        self.inputs, self.out_shapes, self.sem_counts = list(inputs), list(out_shapes), list(sem_counts)
        self.start, self.finish = start, finish


def _copies_job(inputs, out_shapes, n_remote, n_local, make):
    def start(jins, jouts, sems):
        sends, _, local = make(jins, jouts, *sems)
        for cp in local + sends:
            cp.start()

    def finish(jins, jouts, sems):
        sends, recvs, local = make(jins, jouts, *sems)
        for cp in recvs:
            cp.wait_recv()
        for cp in sends:
            cp.wait_send()
        for cp in local:
            cp.wait()

    return _Job(inputs, out_shapes, [n_remote, n_remote, max(n_local, 1)], start, finish)


def _run_job(job, *, name):
    ji, jo = len(job.inputs), len(job.out_shapes)

    def body(*refs):
        jins, jouts, sems = refs[:ji], refs[ji:ji + jo], refs[ji + jo:]
        job.start(jins, jouts, sems)
        job.finish(jins, jouts, sems)

    return pl.pallas_call(
        body, name=name, out_shape=list(job.out_shapes), in_specs=[_ANY] * ji, out_specs=[_ANY] * jo,
        scratch_shapes=[pltpu.SemaphoreType.DMA((c,)) for c in job.sem_counts],
    )(*job.inputs)


def _tile(n, pref):
    if n % pref == 0:
        return pref
    best = None
    t = LANES
    while t <= min(n, pref):
        if n % t == 0:
            best = t
        t += LANES
    return best if best is not None else n


def _mm(a, b, *, am="mk", bm="kn", om="mn", out_dtype=F32, tm=1024, tn=1024, tk=1024, name, job=None):
    if am == "mk":
        m, k = a.shape
    elif am == "km":
        k, m = a.shape
    elif am == "bmk":
        m, tk = a.shape[1], a.shape[2]
        k = a.shape[0] * tk
    else:
        k, tm = a.shape[1], a.shape[2]
        m = a.shape[0] * tm
    if bm == "kn":
        kb_, n = b.shape
    elif bm == "nk":
        n, kb_ = b.shape
    elif bm == "bkn":
        kb_, tn = b.shape[1], b.shape[2]
        n = b.shape[0] * tn
    else:
        n, tk = b.shape[1], b.shape[2]
        kb_ = b.shape[0] * tk
    assert kb_ == k, (a.shape, b.shape, am, bm)
    tm, tn, tk = _tile(m, tm), _tile(n, tn), _tile(k, tk)
    nk = k // tk
    dims = (((0 if am in ("km", "bkm") else 1,), (1 if bm in ("nk", "bnk") else 0,)), ((), ()))

    a_spec = {"mk": pl.BlockSpec((tm, tk), lambda i, j, kk: (i, kk)),
              "km": pl.BlockSpec((tk, tm), lambda i, j, kk: (kk, i)),
              "bmk": pl.BlockSpec((None, tm, tk), lambda i, j, kk: (kk, i, 0)),
              "bkm": pl.BlockSpec((None, tk, tm), lambda i, j, kk: (i, kk, 0))}[am]
    b_spec = {"kn": pl.BlockSpec((tk, tn), lambda i, j, kk: (kk, j)),
              "nk": pl.BlockSpec((tn, tk), lambda i, j, kk: (j, kk)),
              "bkn": pl.BlockSpec((None, tk, tn), lambda i, j, kk: (j, kk, 0)),
              "bnk": pl.BlockSpec((None, tn, tk), lambda i, j, kk: (kk, j, 0))}[bm]
    if om == "mn":
        o_spec, o_shape = pl.BlockSpec((tm, tn), lambda i, j, kk: (i, j)), (m, n)
    else:
        o_spec, o_shape = pl.BlockSpec((None, tm, tn), lambda i, j, kk: (j, i, 0)), (n // tn, m, tn)

    def body(a_ref, b_ref, o_ref, *acc):
        kk = pl.program_id(2)
        prod = lax.dot_general(a_ref[...].astype(BF16), b_ref[...].astype(BF16), dims, preferred_element_type=F32)
        if nk == 1:
            o_ref[...] = prod.astype(o_ref.dtype)
            return
        acc_ref, = acc

        @pl.when(kk == 0)
        def _():
            acc_ref[...] = prod

        if nk > 2:
            @pl.when((kk > 0) & (kk < nk - 1))
            def _():
                acc_ref[...] += prod

        @pl.when(kk == nk - 1)
        def _():
            o_ref[...] = (acc_ref[...] + prod).astype(o_ref.dtype)

    return _call(body, (a, b), name=name, grid=(m // tm, n // tn, nk), in_specs=[a_spec, b_spec], out_specs=o_spec,
                 out_shape=jax.ShapeDtypeStruct(o_shape, out_dtype),
                 scratch_shapes=[pltpu.VMEM((tm, tn), F32)] if nk > 1 else [],
                 sem=("parallel", "parallel", "arbitrary"), job=job)


def _row_operand(a, tile):
    if isinstance(a, tuple):
        arr, w, j = a
        return arr, pl.BlockSpec((tile, w), lambda i, j=j: (i, j))
    return a, pl.BlockSpec((tile, a.shape[1]), lambda i: (i, 0))


def _const_spec(c):
    nd = c.ndim
    return pl.BlockSpec(c.shape, lambda i, nd=nd: (0,) * nd)


def _rowwise(fn, rows, consts, outs, *, name, accs=(), tile=None, job=None):
    t_rows = (rows[0][0] if isinstance(rows[0], tuple) else rows[0]).shape[0]
    tile = min(tile or ROW_TILE, t_rows)
    arrs, specs = zip(*[_row_operand(a, tile) for a in rows])
    nin, no = len(rows) + len(consts), len(outs)

    def body(*refs):
        res = fn(*[r[...] for r in refs[:nin]])
        for r, v in zip(refs[nin:nin + no], res[:no]):
            r[...] = v.astype(r.dtype)
        if accs:
            a_refs = refs[nin + no:]

            @pl.when(pl.program_id(0) == 0)
            def _():
                for r in a_refs:
                    r[...] = jnp.zeros_like(r)

            for r, v in zip(a_refs, res[no:]):
                r[...] += v

    out_shape = [jax.ShapeDtypeStruct((t_rows, w), dt) for w, dt in outs]
    out_shape += [jax.ShapeDtypeStruct(s, F32) for s in accs]
    out_specs = [pl.BlockSpec((tile, w), lambda i: (i, 0)) for w, _ in outs]
    out_specs += [pl.BlockSpec(s, lambda i, nd=len(s): (0,) * nd) for s in accs]
    return _call(body, (*arrs, *consts), name=name, grid=(t_rows // tile,),
                 in_specs=list(specs) + [_const_spec(c) for c in consts],
                 out_specs=out_specs, out_shape=out_shape, sem=("arbitrary",), job=job)


def _rowwise_vjp(fn, rows, consts, cts, *, name, groups, tile=None, gdtypes=None, job=None):
    t_rows = (rows[0][0] if isinstance(rows[0], tuple) else rows[0]).shape[0]
    tile = min(tile or ROW_TILE, t_rows)
    arrs, specs = zip(*[_row_operand(a, tile) for a in rows])
    flat_cts = [c for group in cts for c in group]
    ct_arrs, ct_specs = zip(*[_row_operand(a, tile) for a in flat_cts])
    nr, nc, nct, ng = len(rows), len(consts), len(flat_cts), len(groups)

    def width(a):
        return a[1] if isinstance(a, tuple) else a.shape[1]

    def body(*refs):
        rv = [r[...].astype(F32) for r in refs[:nr]]
        cv = [r[...] for r in refs[nr:nr + nc]]
        ct_refs = refs[nr + nc:nr + nc + nct]
        ctv, pos = [], 0
        for group in cts:
            s = ct_refs[pos][...].astype(F32)
            for r in ct_refs[pos + 1:pos + len(group)]:
                s = s + r[...].astype(F32)
            ctv.append(s)
            pos += len(group)
        _, pull = jax.vjp(fn, *rv, *cv)
        grads = pull(tuple(ctv))
        g_refs = refs[nr + nc + nct:nr + nc + nct + ng]
        for r, idx in zip(g_refs, groups):
            parts = [grads[i] for i in idx]
            r[...] = (parts[0] if len(parts) == 1 else jnp.concatenate(parts, axis=1)).astype(r.dtype)
        c_refs = refs[nr + nc + nct + ng:]

        @pl.when(pl.program_id(0) == 0)
        def _():
            for r in c_refs:
                r[...] = jnp.zeros_like(r)

        for r, v in zip(c_refs, grads[nr:]):
            r[...] += v

    gw = [sum(width(rows[i]) for i in idx) for idx in groups]
    gdtypes = gdtypes or [F32] * ng
    out_shape = [jax.ShapeDtypeStruct((t_rows, w), dt) for w, dt in zip(gw, gdtypes)]
    out_shape += [jax.ShapeDtypeStruct(c.shape, F32) for c in consts]
    out_specs = [pl.BlockSpec((tile, w), lambda i: (i, 0)) for w in gw]
    out_specs += [_const_spec(c) for c in consts]
    return _call(body, (*arrs, *consts, *ct_arrs), name=name, grid=(t_rows // tile,),
                 in_specs=list(specs) + [_const_spec(c) for c in consts] + list(ct_specs),
                 out_specs=out_specs, out_shape=out_shape, sem=("arbitrary",), job=job)


def _layer_norm(x, g, b):
    mu = jnp.mean(x, axis=-1, keepdims=True)
    xc = x - mu
    var = jnp.mean(xc * xc, axis=-1, keepdims=True)
    return xc * lax.rsqrt(var + LN_EPS) * g + b


def _sigmoid(x):
    return 1.0 / (1.0 + jnp.exp(-x))


def _fn_ln(x, g, b):
    return (_layer_norm(x, g, b),)


def _fn_rms(x, g):
    return (x * lax.rsqrt(jnp.mean(x * x, axis=-1, keepdims=True) + RMS_EPS) * g,)


def _make_post_mix(alpha):
    def fn(h, mix, g, b):
        return (_layer_norm(alpha * h + mix, g, b),)
    return fn


def _make_ple_ln(alpha):
    def fn(h1, ffn, pg, pp, g, b):
        return (_layer_norm(alpha * h1 + ffn + _sigmoid(pg) * pp, g, b),)
    return fn


def _fn_lower_bounds(l0, l1):
    m = jnp.maximum(l0, l1)
    e0, e1 = jnp.exp(l0 - m), jnp.exp(l1 - m)
    s = e0 + e1
    p0, p1 = e0 / s, e1 / s
    return (p0 - p0, (p0 + p1) - p0)


def _loss_and_grad(y, target, *, name):
    d = y.shape[1]

    def fn(yv, tv):
        err = yv - tv
        return err * (1.0 / d), 0.5 * jnp.sum(jnp.mean(err * err, axis=-1, keepdims=True), axis=0, keepdims=True)

    return _rowwise(fn, [y, target], [], [(d, F32)], accs=[(1, 1)], name=name)


def _split_dot(x, e_bf16):
    hi = x.astype(BF16)
    lo = (x - hi.astype(F32)).astype(BF16)
    return (jnp.dot(hi, e_bf16, preferred_element_type=F32) + jnp.dot(lo, e_bf16, preferred_element_type=F32))


def _hgrn_common(th):
    rm = lax.broadcasted_iota(jnp.int32, (th, HG_W), 0) % HG_CHUNK

    def seg_cumsum(x):
        for s in (1, 2, 4, 8):
            x = x + jnp.where(rm >= s, pltpu.roll(x, s, 0), 0.0)
        return x

    def seg_rcumsum(x):
        for s in (1, 2, 4, 8):
            x = x + jnp.where(rm < HG_CHUNK - s, pltpu.roll(x, th - s, 0), 0.0)
        return x

    ri = lax.broadcasted_iota(jnp.int32, (HG_W, HG_W), 0) // HEAD
    ci = lax.broadcasted_iota(jnp.int32, (HG_W, HG_W), 1) // HEAD
    head_f32 = (ri == ci).astype(F32)
    head_bf16 = head_f32.astype(BF16)

    def headsum(x, pieces=2):
        if pieces == 1:
            return jnp.dot(x.astype(BF16), head_bf16, preferred_element_type=F32)
        return _split_dot(x, head_bf16)

    return rm, seg_cumsum, seg_rcumsum, head_f32, headsum


def _hgrn_gates(qr, fl, lb):
    sg = _sigmoid(fl)
    f = lb + (1.0 - lb) * sg
    sq = _sigmoid(qr)
    return sg, f, jnp.log(f), 1.0 - f, qr * sq, sq


def _shifted(x, d, th):
    return x if d == 0 else pltpu.roll(x, d, 0)


def _unshift(x, d, th):
    return x if d == 0 else pltpu.roll(x, th - d, 0)


def _hgrn_fwd(projp, lb, ng, *, name, job=None):
    t_rows = projp.shape[0]
    th = min(HG_TILE, t_rows)
    nct = th // HG_CHUNK

    def body(q_ref, f_ref, i_ref, g_ref, lb_ref, ng_ref, oa_ref, opre_ref, st_out_ref,
             st_ref, vtm_ref, kv_ref, qe_ref, dec_ref, oint_ref):
        rm, seg_cumsum, seg_rcumsum, head_f32, headsum = _hgrn_common(th)

        @pl.when(pl.program_id(0) == 0)
        def _():
            st_ref[...] = jnp.zeros_like(st_ref)

        qr, fl, v, g = q_ref[...], f_ref[...], i_ref[...], g_ref[...]
        _, f, lf, k, q, _ = _hgrn_gates(qr, fl, lb_ref[...])
        b = seg_cumsum(lf)

        o = jnp.zeros((th, HG_W), F32)
        for d in range(HG_CHUNK):
            kd, bd, vd = _shifted(k, d, th), _shifted(b, d, th), _shifted(v, d, th)
            e = jnp.exp(jnp.where(rm >= d, b - bd, -1e30))
            o = o + headsum(q * kd * e, 1) * vd

        blast = seg_rcumsum(jnp.where(rm == HG_CHUNK - 1, b, 0.0))
        kte = (k * jnp.exp(blast - b)).astype(BF16)
        qe_ref[...] = q * jnp.exp(b)
        dec_ref[...] = jnp.exp(blast)
        vt = v.T
        lane_chunk = lax.broadcasted_iota(jnp.int32, (HG_W, th), 1) // HG_CHUNK
        for c in range(nct):
            vtm_ref[c * HG_W:(c + 1) * HG_W, :] = jnp.where(lane_chunk == c, vt, 0.0).astype(BF16)
        kv_ref[...] = jnp.dot(vtm_ref[...], kte, preferred_element_type=F32)

        s = st_ref[...]
        for c in range(nct):
            rows = slice(c * HG_CHUNK, (c + 1) * HG_CHUNK)
            st_out_ref[c] = s
            oint_ref[rows, :] = lax.dot_general(qe_ref[rows, :].astype(BF16), s.astype(BF16),
                                                (((1,), (1,)), ((), ())), preferred_element_type=F32)
            dec = jnp.max(dec_ref[rows, :], axis=0, keepdims=True)
            s = s * dec + kv_ref[c * HG_W:(c + 1) * HG_W, :] * head_f32
        st_ref[...] = s

        o = o + oint_ref[...]
        opre_ref[...] = o
        r = lax.rsqrt(headsum(o * o) * (1.0 / HEAD) + RMS_EPS)
        oa_ref[...] = (o * r * ng_ref[...] * (g * _sigmoid(g))).astype(oa_ref.dtype)

    col = lambda j: pl.BlockSpec((th, HG_W), lambda i, j=j: (i, j))
    vec = pl.BlockSpec((1, HG_W), lambda i: (0, 0))
    row = pl.BlockSpec((th, HG_W), lambda i: (i, 0))
    n_chunks = t_rows // HG_CHUNK
    return _call(
        body, (projp, projp, projp, projp, lb, ng), name=name, grid=(t_rows // th,),
        in_specs=[col(0), col(1), col(2), col(3), vec, vec],
        out_specs=[row, row, pl.BlockSpec((nct, HG_W, HG_W), lambda i: (i, 0, 0))],
        out_shape=[jax.ShapeDtypeStruct((t_rows, HG_W), BF16), jax.ShapeDtypeStruct((t_rows, HG_W), F32),
                   jax.ShapeDtypeStruct((n_chunks, HG_W, HG_W), F32)],
        scratch_shapes=[pltpu.VMEM((HG_W, HG_W), F32), pltpu.VMEM((nct * HG_W, th), BF16),
                        pltpu.VMEM((nct * HG_W, HG_W), F32), pltpu.VMEM((th, HG_W), F32),
                        pltpu.VMEM((th, HG_W), F32), pltpu.VMEM((th, HG_W), F32)],
        sem=("arbitrary",), job=job)


def _hgrn_bwd(projp, lb, ng, opre, states, dcat, *, name):
    t_rows = projp.shape[0]
    th = min(HG_TILE, t_rows)
    nct = th // HG_CHUNK
    nt = t_rows // th

    def body(q_ref, f_ref, i_ref, g_ref, lb_ref, ng_ref, opre_ref, st_in_ref, do_ref,
             dproj_ref, dng_ref, dlb_ref,
             gst_ref, dotm_ref, qg_ref, v_ref, kte_ref, dop_ref, dec_ref, dkte_ref, dvi_ref, dqe_ref, ddec_ref):
        rm, seg_cumsum, seg_rcumsum, head_f32, headsum = _hgrn_common(th)

        @pl.when(pl.program_id(0) == 0)
        def _():
            gst_ref[...] = jnp.zeros_like(gst_ref)
            dng_ref[...] = jnp.zeros_like(dng_ref)
            dlb_ref[...] = jnp.zeros_like(dlb_ref)

        qr, fl, v, g = q_ref[...], f_ref[...], i_ref[...], g_ref[...]
        lb, ngv = lb_ref[...], ng_ref[...]
        sg, f, lf, k, q, sq = _hgrn_gates(qr, fl, lb)
        b = seg_cumsum(lf)
        blast = seg_rcumsum(jnp.where(rm == HG_CHUNK - 1, b, 0.0))
        eb = jnp.exp(b)
        ekb = jnp.exp(blast - b)
        qe, kte, dec = q * eb, k * ekb, jnp.exp(blast)

        do_out, op = do_ref[...], opre_ref[...]
        sgg = _sigmoid(g)
        sil = g * sgg
        r = lax.rsqrt(headsum(op * op) * (1.0 / HEAD) + RMS_EPS)
        on = op * r
        dng_ref[...] += jnp.sum(do_out * on * sil, axis=0, keepdims=True)
        dg = do_out * on * ngv * (sgg * (1.0 + g * (1.0 - sgg)))
        don = do_out * ngv * sil
        dop = r * (don - on * (headsum(don * on) * (1.0 / HEAD)))

        v_ref[...] = v
        kte_ref[...] = kte
        dop_ref[...] = dop
        dec_ref[...] = dec
        dot_t = dop.T
        lane_chunk = lax.broadcasted_iota(jnp.int32, (HG_W, th), 1) // HG_CHUNK
        for c in range(nct):
            dotm_ref[c * HG_W:(c + 1) * HG_W, :] = jnp.where(lane_chunk == c, dot_t, 0.0).astype(BF16)
        qg_ref[...] = jnp.dot(dotm_ref[...], qe.astype(BF16), preferred_element_type=F32)

        gs = gst_ref[...]
        for c in reversed(range(nct)):
            rows = slice(c * HG_CHUNK, (c + 1) * HG_CHUNK)
            s = st_in_ref[c]
            gm = (gs * head_f32).astype(BF16)
            dkte_ref[rows, :] = jnp.dot(v_ref[rows, :].astype(BF16), gm, preferred_element_type=F32)
            dvi_ref[rows, :] = lax.dot_general(kte_ref[rows, :].astype(BF16), gm, (((1,), (1,)), ((), ())),
                                               preferred_element_type=F32)
            dqe_ref[rows, :] = jnp.dot(dop_ref[rows, :].astype(BF16), s.astype(BF16), preferred_element_type=F32)
            ddec_ref[rows, :] = jnp.broadcast_to(jnp.sum(gs * s, axis=0, keepdims=True), (HG_CHUNK, HG_W))
            dec_c = jnp.max(dec_ref[rows, :], axis=0, keepdims=True)
            gs = gs * dec_c + qg_ref[c * HG_W:(c + 1) * HG_W, :] * head_f32
        gst_ref[...] = gs

        dkte, dqe = dkte_ref[...], dqe_ref[...]
        dq = dqe * eb
        dk = dkte * ekb
        db = dqe * qe - dkte * kte
        dv = dvi_ref[...]
        dblast = dkte * kte + jnp.where(rm == HG_CHUNK - 1, ddec_ref[...] * dec, 0.0)

        for d in range(HG_CHUNK):
            kd, bd, vd = _shifted(k, d, th), _shifted(b, d, th), _shifted(v, d, th)
            e = jnp.exp(jnp.where(rm >= d, b - bd, -1e30))
            p = q * kd * e
            sc = headsum(p, 1)
            dsc = headsum(dop * vd, 1)
            dv = dv + _unshift(sc * dop, d, th)
            dq = dq + dsc * kd * e
            dk = dk + _unshift(dsc * q * e, d, th)
            darg = dsc * p
            db = db + darg - _unshift(darg, d, th)

        db = db + jnp.where(rm == HG_CHUNK - 1, seg_cumsum(dblast), 0.0)
        dlf = seg_rcumsum(db)
        df = dlf / f - dk
        dlb_ref[...] += jnp.sum(df * (1.0 - sg), axis=0, keepdims=True)
        dfl = df * (1.0 - lb) * sg * (1.0 - sg)
        dqr = dq * (sq * (1.0 + qr * (1.0 - sq)))
        dproj_ref[...] = jnp.concatenate([dqr, dfl, dv, dg], axis=1).astype(dproj_ref.dtype)

    rev = lambda i: nt - 1 - i
    col = lambda j: pl.BlockSpec((th, HG_W), lambda i, j=j: (rev(i), j))
    vec = pl.BlockSpec((1, HG_W), lambda i: (0, 0))
    row = pl.BlockSpec((th, HG_W), lambda i: (rev(i), 0))
    tile_f32 = pltpu.VMEM((th, HG_W), F32)
    return pl.pallas_call(
        body, name=name, grid=(nt,),
        in_specs=[col(0), col(1), col(2), col(3), vec, vec, row,
                  pl.BlockSpec((nct, HG_W, HG_W), lambda i: (rev(i), 0, 0)), col(0)],
        out_specs=[pl.BlockSpec((th, 4 * HG_W), lambda i: (rev(i), 0)), vec, vec],
        out_shape=[jax.ShapeDtypeStruct((t_rows, 4 * HG_W), BF16), jax.ShapeDtypeStruct((1, HG_W), F32),
                   jax.ShapeDtypeStruct((1, HG_W), F32)],
        scratch_shapes=[pltpu.VMEM((HG_W, HG_W), F32), pltpu.VMEM((nct * HG_W, th), BF16),
                        pltpu.VMEM((nct * HG_W, HG_W), F32)] + [tile_f32] * 8,
        compiler_params=_cparams(("arbitrary",)),
    )(projp, projp, projp, projp, lb, ng, opre, states, dcat)


_INV_SQRT2 = 1.0 / math.sqrt(2.0)
_INV_SQRT2PI = 1.0 / math.sqrt(2.0 * math.pi)


def _gelu(x):
    return 0.5 * x * (1.0 + lax.erf(x * _INV_SQRT2))


def _gelu_grad(x):
    return 0.5 * (1.0 + lax.erf(x * _INV_SQRT2)) + x * jnp.exp(-0.5 * x * x) * _INV_SQRT2PI


def _sgu_parts(bu, bv, lg, lbias, w_ref, n_groups):
    c = SGU_CHUNK
    tril = (lax.broadcasted_iota(jnp.int32, (c, c), 0) >= lax.broadcasted_iota(jnp.int32, (c, c), 1)).astype(F32)
    gid = lax.broadcasted_iota(jnp.int32, bu.shape, 1) // HEAD
    u = _gelu(bu)
    gv = _gelu(bv)
    mu = jnp.mean(gv, axis=-1, keepdims=True)
    xc = gv - mu
    rstd = lax.rsqrt(jnp.mean(xc * xc, axis=-1, keepdims=True) + LN_EPS)
    xhat = xc * rstd
    vn = xhat * lg + lbias
    ws = [w_ref[gi] * tril for gi in range(n_groups)]
    return tril, gid, u, rstd, xhat, vn, ws


def _sgu_fwd(projp, lg, lbias, w_s, bias_full, *, name):
    t_rows = projp.shape[0]
    n_groups = w_s.shape[0]
    c = SGU_CHUNK

    def body(u_ref, v_ref, lg_ref, lb_ref, w_ref, bias_ref, o_ref):
        _, gid, u, _, _, vn, ws = _sgu_parts(u_ref[...], v_ref[...], lg_ref[...], lb_ref[...], w_ref, n_groups)
        vnb = vn.astype(BF16)
        z = bias_ref[...]
        for gi in range(n_groups):
            z = z + jnp.where(gid == gi, jnp.dot(ws[gi].astype(BF16), vnb, preferred_element_type=F32), 0.0)
        o_ref[...] = (u * z).astype(o_ref.dtype)

    col = lambda j: pl.BlockSpec((c, HG_W), lambda i, j=j: (i, j))
    return pl.pallas_call(
        body, name=name, grid=(t_rows // c,),
        in_specs=[col(4), col(5), _const_spec(lg), _const_spec(lbias), _const_spec(w_s), _const_spec(bias_full)],
        out_specs=pl.BlockSpec((c, HG_W), lambda i: (i, 0)),
        out_shape=jax.ShapeDtypeStruct((t_rows, HG_W), BF16),
        compiler_params=_cparams(("arbitrary",)),
    )(projp, projp, lg, lbias, w_s, bias_full)


def _sgu_bwd(projp, lg, lbias, w_s, bias_full, dcat, *, name):
    t_rows = projp.shape[0]
    n_groups = w_s.shape[0]
    c = SGU_CHUNK
    n = t_rows // c

    def body(u_ref, v_ref, lg_ref, lb_ref, w_ref, bias_ref, do_ref,
             dproj_ref, dlg_ref, dlb_ref, dw_ref, dbs_ref, dbias_acc):
        i = pl.program_id(0)

        @pl.when(i == 0)
        def _():
            dlg_ref[...] = jnp.zeros_like(dlg_ref)
            dlb_ref[...] = jnp.zeros_like(dlb_ref)
            dw_ref[...] = jnp.zeros_like(dw_ref)
            dbias_acc[...] = jnp.zeros_like(dbias_acc)

        bu, bv, lg_v = u_ref[...], v_ref[...], lg_ref[...]
        tril, gid, u, rstd, xhat, vn, ws = _sgu_parts(bu, bv, lg_v, lb_ref[...], w_ref, n_groups)
        vnb = vn.astype(BF16)
        z = bias_ref[...]
        for gi in range(n_groups):
            z = z + jnp.where(gid == gi, jnp.dot(ws[gi].astype(BF16), vnb, preferred_element_type=F32), 0.0)
        do = do_ref[...]
        dbu = do * z * _gelu_grad(bu)
        dz = do * u
        dbias_acc[...] += dz
        dvn = jnp.zeros_like(dz)
        for gi in range(n_groups):
            dzg = jnp.where(gid == gi, dz, 0.0).astype(BF16)
            dw_ref[gi] += lax.dot_general(dzg, vnb, (((1,), (1,)), ((), ())), preferred_element_type=F32) * tril
            dvn = dvn + jnp.dot(ws[gi].T.astype(BF16), dzg, preferred_element_type=F32)
        dlg_ref[...] += jnp.sum(dvn * xhat, axis=0, keepdims=True)
        dlb_ref[...] += jnp.sum(dvn, axis=0, keepdims=True)
        dxh = dvn * lg_v
        dgv = rstd * (dxh - jnp.mean(dxh, axis=-1, keepdims=True)
                      - xhat * jnp.mean(dxh * xhat, axis=-1, keepdims=True))
        dproj_ref[...] = jnp.concatenate([dbu, dgv * _gelu_grad(bv)], axis=1).astype(dproj_ref.dtype)

        @pl.when(i == n - 1)
        def _():
            dbs_ref[...] = jnp.sum(dbias_acc[...].T.reshape(n_groups, HEAD, c), axis=1)

    col = lambda j: pl.BlockSpec((c, HG_W), lambda i, j=j: (i, j))
    return pl.pallas_call(
        body, name=name, grid=(n,),
        in_specs=[col(4), col(5), _const_spec(lg), _const_spec(lbias), _const_spec(w_s), _const_spec(bias_full),
                  col(1)],
        out_specs=[pl.BlockSpec((c, 2 * HG_W), lambda i: (i, 0)), _const_spec(lg), _const_spec(lbias),
                   _const_spec(w_s), pl.BlockSpec((n_groups, c), lambda i: (0, 0))],
        out_shape=[jax.ShapeDtypeStruct((t_rows, 2 * HG_W), BF16), jax.ShapeDtypeStruct(lg.shape, F32),
                   jax.ShapeDtypeStruct(lbias.shape, F32), jax.ShapeDtypeStruct(w_s.shape, F32),
                   jax.ShapeDtypeStruct((n_groups, c), F32)],
        scratch_shapes=[pltpu.VMEM((c, HG_W), F32)],
        compiler_params=_cparams(("arbitrary",)),
    )(projp, projp, lg, lbias, w_s, bias_full, dcat)


def _rope_tables(positions):
    t = positions.shape[0]
    inv_freq = ROPE_THETA ** (-jnp.arange(0, 32, 2, dtype=F32) / 32)
    ang = positions.astype(F32)[:, None] * inv_freq
    cos, sin = jnp.cos(ang), jnp.sin(ang)
    z = lambda w: jnp.zeros((t, w), F32)
    cos_t = jnp.concatenate([jnp.ones((t, 64), F32), cos, cos, z(32)], axis=1)
    sin_up = jnp.concatenate([z(80), sin, z(32)], axis=1)
    sin_dn = jnp.concatenate([z(64), -sin, z(48)], axis=1)
    return cos_t, sin_up, sin_dn


def _rep(x, n):
    return x if n == 1 else jnp.concatenate([x] * n, axis=1)


def _rope(x, cos_t, sin_up, sin_dn):
    w = x.shape[1]
    return x * cos_t + pltpu.roll(x, 16, 1) * sin_up + pltpu.roll(x, w - 16, 1) * sin_dn


def _rope_t(dy, cos_t, sin_up, sin_dn):
    w = dy.shape[1]
    return dy * cos_t + pltpu.roll(dy * sin_up, w - 16, 1) + pltpu.roll(dy * sin_dn, 16, 1)


def _mla_prep(q, kv, projp, tables, *, name):
    nh = N_ATT_HEADS

    def fn(qv, kvv, kr, cos_t, sin_up, sin_dn):
        qr = _rope(qv, _rep(cos_t, nh), _rep(sin_up, nh), _rep(sin_dn, nh))
        krr = _rope(kr, cos_t, sin_up, sin_dn)
        lane = lax.broadcasted_iota(jnp.int32, kvv.shape, 1) % LANES
        return qr, jnp.where(lane < HEAD, kvv, 0.0) + _rep(krr, nh), kvv

    w = q.shape[1]
    return _rowwise(fn, [q, kv, (projp, LANES, P_KR // LANES)] + list(tables), [],
                    [(w, BF16), (w, BF16), (w, BF16)], name=name)


def _mla_prep_bwd(dqr, dkf, tables, *, name):
    nh = N_ATT_HEADS

    def fn(dq, dk, cos_t, sin_up, sin_dn):
        dqp = _rope_t(dq, _rep(cos_t, nh), _rep(sin_up, nh), _rep(sin_dn, nh))
        dkrr = dk[:, 0:LANES]
        for h in range(1, nh):
            dkrr = dkrr + dk[:, LANES * h:LANES * (h + 1)]
        return dqp, _rope_t(dkrr, cos_t, sin_up, sin_dn)

    return _rowwise(fn, [dqr, dkf] + list(tables), [], [(dqr.shape[1], BF16), (LANES, BF16)], name=name)


_LOG2E = 1.0 / math.log(2.0)
_NT = (((1,), (1,)), ((), ()))
_TN = (((0,), (0,)), ((), ()))


def _attn_fwd(qr, kf, kvb, *, name, job=None):
    t_rows = qr.shape[0]
    tq = min(ATT_TQ, t_rows)
    nb = t_rows // tq
    scale = ATT_D ** -0.5

    c2 = scale * _LOG2E

    def body(q_ref, kf_ref, kvb_ref, o_ref, lse_ref):
        qi = pl.program_id(1)
        lane = lax.broadcasted_iota(jnp.int32, (tq, LANES), 1)
        causal_t = (lax.broadcasted_iota(jnp.int32, (tq, tq), 0) <= lax.broadcasted_iota(jnp.int32, (tq, tq), 1))
        heads = [slice(hh * LANES, (hh + 1) * LANES) for hh in range(2)]
        qs = [q_ref[:, cols] for cols in heads]

        def block(first, n_keys, carry, diagonal):
            rows = pl.ds(pl.multiple_of(first * tq, tq), n_keys)
            new = []
            for q, cols, (m_old, l_old, acc_t) in zip(qs, heads, carry):
                s_t = lax.dot_general(kf_ref[rows, cols], q, _NT, preferred_element_type=F32)
                if diagonal:
                    s_t = jnp.where(causal_t, s_t, -1e30)
                m_new = jnp.maximum(m_old, jnp.max(s_t, axis=0, keepdims=True))
                p_t = jnp.exp2((s_t - m_new) * c2)
                a = jnp.exp2((m_old - m_new) * c2)
                pv_t = lax.dot_general(kvb_ref[rows, cols], p_t.astype(BF16), _TN, preferred_element_type=F32)
                new.append((m_new, a * l_old + jnp.sum(p_t, axis=0, keepdims=True), a * acc_t + pv_t))
            return tuple(new)

        init = (jnp.full((1, tq), -1e30, F32), jnp.zeros((1, tq), F32), jnp.zeros((LANES, tq), F32))
        carry = lax.fori_loop(0, qi // 2, lambda g, c: block(2 * g, 2 * tq, c, False), (init, init))
        carry = lax.cond(qi % 2 == 1, lambda c: block(qi - 1, tq, c, False), lambda c: c, carry)
        outs = []
        for hh, (m_fin, l_fin, acc_t) in enumerate(block(qi, tq, carry, True)):
            lse_ref[hh] = m_fin * scale + jnp.log(l_fin)
            outs.append((acc_t / l_fin).T)
        o_ref[...] = jnp.where(lane < HEAD, pltpu.roll(outs[0], HEAD, 1), outs[1])

    pair = pl.BlockSpec((t_rows, 2 * LANES), lambda pr, qi: (0, pr))
    return _call(
        body, (qr, kf, kvb), name=name, grid=(N_ATT_HEADS // 2, nb),
        in_specs=[pl.BlockSpec((tq, 2 * LANES), lambda pr, qi: (qi, pr)), pair, pair],
        out_specs=[pl.BlockSpec((tq, LANES), lambda pr, qi: (qi, pr)),
                   pl.BlockSpec((2, 1, tq), lambda pr, qi: (pr, 0, qi))],
        out_shape=[jax.ShapeDtypeStruct((t_rows, N_ATT_HEADS * HEAD), F32),
                   jax.ShapeDtypeStruct((N_ATT_HEADS, 1, t_rows), F32)],
        sem=("parallel", "arbitrary"), job=job)


def _attn_bwd(qr, kf, kvb, dcat, o, lse, *, name, job=None):
    t_rows = qr.shape[0]
    tq = min(ATT_TQ, t_rows)
    nb = t_rows // tq
    scale = ATT_D ** -0.5
    c2 = scale * _LOG2E
    do_off = 2 * HG_W // LANES

    def body(q_ref, kf_ref, kvb_ref, do_ref, o_ref, lse_ref, dq_ref, dkv_ref, dk_ref):
        ki = pl.program_id(1)

        @pl.when(ki == 0)
        def _():
            dq_ref[...] = jnp.zeros_like(dq_ref)

        lane = lax.broadcasted_iota(jnp.int32, (tq, LANES), 1)
        causal_t = (lax.broadcasted_iota(jnp.int32, (tq, tq), 0) <= lax.broadcasted_iota(jnp.int32, (tq, tq), 1))
        heads = [slice(hh * LANES, (hh + 1) * LANES) for hh in range(2)]
        ks = [kf_ref[:, cols] for cols in heads]
        vs = [kvb_ref[:, cols] for cols in heads]

        def block(qi, n_q, carry, diagonal):
            rows = pl.ds(pl.multiple_of(qi * tq, tq), n_q)
            do_pair, o_pair = do_ref[rows, :], o_ref[rows, :]
            upper = lax.broadcasted_iota(jnp.int32, do_pair.shape, 1) >= HEAD
            new = []
            for hh, (cols, k, v, (dk, dv)) in enumerate(zip(heads, ks, vs, carry)):
                q = q_ref[rows, cols]
                do, ov = (pltpu.roll(do_pair, HEAD, 1), pltpu.roll(o_pair, HEAD, 1)) if hh == 0 else (do_pair, o_pair)
                do = jnp.where(upper, do, 0.0)
                delta = jnp.sum((do * ov).T, axis=0, keepdims=True)
                s_t = lax.dot_general(k, q, _NT, preferred_element_type=F32)
                if diagonal:
                    s_t = jnp.where(causal_t, s_t, -1e30)
                p_t = jnp.exp2(s_t * c2 - lse_ref[hh, :, rows] * _LOG2E)
                dob = do.astype(BF16)
                dv = dv + jnp.dot(p_t.astype(BF16), dob, preferred_element_type=F32)
                dp_t = lax.dot_general(v, dob, _NT, preferred_element_type=F32)
                ds_t = (p_t * (dp_t - delta) * scale).astype(BF16)
                dk = dk + jnp.dot(ds_t, q, preferred_element_type=F32)
                dq_ref[rows, cols] += lax.dot_general(ds_t, k, _TN, preferred_element_type=F32)
                new.append((dk, dv))
            return tuple(new)

        zero = jnp.zeros((tq, LANES), F32)
        carry = block(ki, tq, ((zero, zero), (zero, zero)), True)
        rest = nb - 1 - ki
        carry = lax.fori_loop(0, rest // 2, lambda g, c: block(ki + 1 + 2 * g, 2 * tq, c, False), carry)
        carry = lax.cond(rest % 2 == 1, lambda c: block(nb - 1, tq, c, False), lambda c: c, carry)
        dkv_ref[...] = jnp.concatenate([jnp.where(lane < HEAD, dk, dv) for dk, dv in carry],
                                       axis=1).astype(dkv_ref.dtype)
        dk_ref[...] = jnp.concatenate([dk for dk, _ in carry], axis=1)

    pair_all = pl.BlockSpec((t_rows, 2 * LANES), lambda pr, ki: (0, pr))
    pair_blk = pl.BlockSpec((tq, 2 * LANES), lambda pr, ki: (ki, pr))
    wide = jax.ShapeDtypeStruct((t_rows, N_ATT_HEADS * LANES), F32)
    return _call(
        body, (qr, kf, kvb, dcat, o, lse), name=name, grid=(N_ATT_HEADS // 2, nb),
        in_specs=[pair_all, pair_blk, pair_blk,
                  pl.BlockSpec((t_rows, LANES), lambda pr, ki: (0, do_off + pr)),
                  pl.BlockSpec((t_rows, LANES), lambda pr, ki: (0, pr)),
                  pl.BlockSpec((2, 1, t_rows), lambda pr, ki: (pr, 0, 0))],
        out_specs=[pair_all, pair_blk, pair_blk],
        out_shape=[wide, jax.ShapeDtypeStruct(wide.shape, BF16), wide],
        sem=("parallel", "arbitrary"), job=job)


def _my_pos():
    return lax.axis_index("x"), lax.axis_index("y"), lax.axis_index("c")


def _all_gather(xs, *, name, columns=True):
    return _gather_forward(_run_job(_gather_job(xs, columns), name=name), name=name + "_forward")


def _remote(src, dst, send_sems, recv_sems, k, dev):
    return pltpu.make_async_remote_copy(src_ref=src, dst_ref=dst, send_sem=send_sems.at[k], recv_sem=recv_sems.at[k],
                                        device_id=dev, device_id_type=MESH)


def _block(ref, idx):
    if len(ref.shape) == 2:
        return ref.at[:, pl.ds(pl.multiple_of(idx * LANES, LANES), LANES)]
    return ref.at[idx]


def _gather_job(xs, columns=True):
    n = len(xs)

    def make(x_refs, out_refs, send_sems, recv_sems, local_sems):
        mx, my, mc = _my_pos()
        mine = 4 * mx + 2 * my + mc
        peers = [(mx, my, 1 - mc), (1 - mx, my, mc), (mx, 1 - my, mc), (1 - mx, 1 - my, mc)]
        sends, recvs, local = [], [], []
        for a in range(n):
            local.append(pltpu.make_async_copy(x_refs[a], _block(out_refs[a], mine), local_sems.at[a]))
            for k, dev in enumerate(peers):
                theirs = 4 * dev[0] + 2 * dev[1] + dev[2]
                sends.append(_remote(x_refs[a], _block(out_refs[a], mine), send_sems, recv_sems, 4 * a + k, dev))
                recvs.append(_remote(x_refs[a], _block(out_refs[a], theirs), send_sems, recv_sems, 4 * a + k, dev))
        return sends, recvs, local

    def gathered(x):
        if columns and x.ndim == 2 and x.shape[1] == LANES:
            return jax.ShapeDtypeStruct((x.shape[0], N_DEV * LANES), x.dtype)
        return jax.ShapeDtypeStruct((N_DEV,) + x.shape, x.dtype)

    return _copies_job(xs, [gathered(x) for x in xs], 4 * n, n, make)


def _gather_forward(gs, *, name):
    n = len(gs)

    def body(*refs):
        out_refs = refs[n:2 * n]
        send_sems, recv_sems = refs[2 * n:]
        mx, my, mc = _my_pos()
        chips = [(1 - mx, my), (mx, 1 - my), (1 - mx, 1 - my)]
        sends, recvs = [], []
        for a in range(n):
            for j, (cx, cy) in enumerate(chips):
                here = _block(out_refs[a], 4 * cx + 2 * cy + mc)
                there = _block(out_refs[a], 4 * cx + 2 * cy + 1 - mc)
                sends.append(_remote(here, here, send_sems, recv_sems, 3 * a + j, (mx, my, 1 - mc)))
                recvs.append(_remote(here, there, send_sems, recv_sems, 3 * a + j, (mx, my, 1 - mc)))
        for cp in sends:
            cp.start()
        for cp in recvs:
            cp.wait_recv()
        for cp in sends:
            cp.wait_send()

    return pl.pallas_call(
        body, name=name, out_shape=[jax.ShapeDtypeStruct(g.shape, g.dtype) for g in gs],
        in_specs=[_ANY] * n, out_specs=[_ANY] * n, input_output_aliases={a: a for a in range(n)},
        scratch_shapes=[pltpu.SemaphoreType.DMA((3 * n,)), pltpu.SemaphoreType.DMA((3 * n,))],
    )(*gs)


def _pair_job(xs):
    n = len(xs)

    def make(x_refs, out_refs, send_sems, recv_sems, local_sems):
        mx, my, mc = _my_pos()

        def src(ref, g):
            return _block(ref, 2 * g + 1 - mc) if len(ref.shape) == 2 else ref.at[g, 1 - mc]

        copies = [_remote(src(x_refs[a], g), out_refs[a].at[g], send_sems, recv_sems, 4 * a + g, (mx, my, 1 - mc))
                  for a in range(n) for g in range(4)]
        return copies, copies, []

    shapes = [jax.ShapeDtypeStruct((4, x.shape[0], LANES) if x.ndim == 2 else (4,) + x.shape[2:], x.dtype)
              for x in xs]
    return _copies_job(xs, shapes, 4 * n, 0, make)


def _pair_add(x, r, core, *, name):
    _, a, b = r.shape
    ta = _row_tile(a, 256)

    def body(c_ref, x_ref, r_ref, o_ref):
        o_ref[...] = (x_ref[...] + r_ref[...]).astype(o_ref.dtype)

    blk = pl.BlockSpec((None, ta, b), lambda g, i, c_ref: (g, i, 0))
    own = (pl.BlockSpec((ta, b), lambda g, i, c_ref: (i, 2 * g + c_ref[0])) if x.ndim == 2
           else pl.BlockSpec((None, None, ta, b), lambda g, i, c_ref: (g, c_ref[0], i, 0)))
    return pl.pallas_call(
        body, name=name,
        grid_spec=pltpu.PrefetchScalarGridSpec(
            num_scalar_prefetch=1, grid=(4, a // ta), in_specs=[own, blk], out_specs=blk),
        out_shape=jax.ShapeDtypeStruct((4, a, b), BF16),
        compiler_params=_cparams(("parallel", "parallel")),
    )(core, x, r)


def _quad_job(xs):
    n = len(xs)

    def make(x_refs, out_refs, send_sems, recv_sems, local_sems):
        mx, my, mc = _my_pos()
        mine = 2 * mx + my
        peers = [((1 - mx, my, mc), 2 * (1 - mx) + my), ((mx, 1 - my, mc), 2 * mx + 1 - my),
                 ((1 - mx, 1 - my, mc), 2 * (1 - mx) + 1 - my)]
        sends, recvs, local = [], [], []
        for a in range(n):
            local.append(pltpu.make_async_copy(x_refs[a].at[mine], out_refs[a].at[mine], local_sems.at[a]))
            for k, (dev, g) in enumerate(peers):
                sends.append(_remote(x_refs[a].at[g], out_refs[a].at[mine], send_sems, recv_sems, 3 * a + k, dev))
                recvs.append(_remote(x_refs[a].at[g], out_refs[a].at[g], send_sems, recv_sems, 3 * a + k, dev))
        return sends, recvs, local

    shapes = [jax.ShapeDtypeStruct(x.shape, x.dtype) for x in xs]
    return _copies_job(xs, shapes, 3 * n, n, make)


def _row_tile(r, pref):
    t = min(pref, r)
    while r % t or (t % 8 and t != r):
        t -= 1
    return t


def _adamw(parts, w, m, v, layer, *, name, tile=256, into=None):
    g, a, b = parts.shape
    tile = _row_tile(a, tile)
    c1 = 1.0 / (1.0 - ADAM_B1 ** ADAM_STEP)
    c2 = 1.0 / (1.0 - ADAM_B2 ** ADAM_STEP)
    into = tuple(into or ())

    def body(p_ref, w_ref, m_ref, v_ref, *refs):
        g_ref, d_ref, mo_ref, vo_ref = refs[len(into):]
        grad = p_ref[0].astype(F32)
        for j in range(1, g):
            grad = grad + p_ref[j].astype(F32)
        mn = ADAM_B1 * m_ref[...] + (1.0 - ADAM_B1) * grad
        vn = ADAM_B2 * v_ref[...] + (1.0 - ADAM_B2) * (grad * grad)
        g_ref[...] = grad
        mo_ref[...] = mn
        vo_ref[...] = vn
        d_ref[...] = -ADAM_LR * ((mn * c1) / (jnp.sqrt(vn * c2) + ADAM_EPS) + ADAM_WD * w_ref[...])

    if layer is None:
        src, shape = pl.BlockSpec((tile, b), lambda i: (i, 0)), (a, b)
    else:
        src, shape = pl.BlockSpec((None, tile, b), lambda i: (layer, i, 0)), w.shape
    return pl.pallas_call(
        body, name=name, grid=(a // tile,),
        in_specs=[pl.BlockSpec((g, tile, b), lambda i: (0, i, 0)), src, src, src] + [_ANY] * len(into),
        out_specs=[src] * 4,
        out_shape=[jax.ShapeDtypeStruct(shape, F32)] * 4,
        input_output_aliases={4 + i: i for i in range(len(into))},
        compiler_params=_cparams(("parallel",)),
    )(parts, w, m, v, *into)


W_IN_SHARD = 276


def _w_in_dest(col):
    return jnp.where(col < P_KR, col, jnp.where(col < P_KR + 256, col + (P_CKV - P_KR), col - 2176 + P_KR + HEAD))


PLACE_TILE = 384
PLACE_SHARDS = 3
PICK_TILE = 128
PICK_TILES = 4


def _w_in_tables():
    col = np.arange(N_DEV * W_IN_SHARD)
    dest = np.where(col < P_KR, col, np.where(col < P_KR + 256, col + (P_CKV - P_KR), col - 2176 + P_KR + HEAD))
    shard = col // W_IN_SHARD

    def filled(used, universe, n):
        used = sorted(set(int(u) for u in used))
        assert len(used) <= n, used
        return used + [u for u in universe if u not in used][:n - len(used)]

    place = [filled(shard[dest // PLACE_TILE == c], range(N_DEV), PLACE_SHARDS) for c in range(P_COLS // PLACE_TILE)]
    pick = [filled(dest[shard == j] // PICK_TILE, range(P_COLS // PICK_TILE), PICK_TILES) for j in range(N_DEV)]
    return np.asarray(place, np.int32).reshape(-1), np.asarray(pick, np.int32).reshape(-1)


def _place_w_in(g, *, name):
    _, d, sh = g.shape
    tc, ns = PLACE_TILE, PLACE_SHARDS
    table = jnp.asarray(_w_in_tables()[0])

    def body(tab_ref, g_ref, o_ref, acc_ref):
        ct, s = pl.program_id(0), pl.program_id(1)
        j = tab_ref[ct * ns + s]

        @pl.when(s == 0)
        def _():
            acc_ref[...] = jnp.zeros_like(acc_ref)

        src = j * sh + lax.broadcasted_iota(jnp.int32, (sh, tc), 0)
        dst = ct * tc + lax.broadcasted_iota(jnp.int32, (sh, tc), 1)
        place = (_w_in_dest(src) == dst).astype(BF16)
        acc_ref[...] += jnp.dot(g_ref[...], place, preferred_element_type=F32)

        @pl.when(s == ns - 1)
        def _():
            o_ref[...] = acc_ref[...].astype(o_ref.dtype)

    return pl.pallas_call(
        body, name=name,
        grid_spec=pltpu.PrefetchScalarGridSpec(
            num_scalar_prefetch=1, grid=(P_COLS // tc, ns),
            in_specs=[pl.BlockSpec((None, d, sh), lambda ct, s, tab: (tab[ct * ns + s], 0, 0))],
            out_specs=pl.BlockSpec((d, tc), lambda ct, s, tab: (0, ct)),
            scratch_shapes=[pltpu.VMEM((d, tc), F32)]),
        out_shape=jax.ShapeDtypeStruct((d, P_COLS), BF16),
        compiler_params=_cparams(("parallel", "arbitrary")),
    )(table, g)


def _unplace_w_in(dw, *, name):
    d = dw.shape[0]
    sh, tk, nt = W_IN_SHARD, PICK_TILE, PICK_TILES
    table = jnp.asarray(_w_in_tables()[1])

    def body(tab_ref, dw_ref, o_ref):
        j, kk = pl.program_id(0), pl.program_id(1)
        tile = tab_ref[j * nt + kk]
        src = j * sh + lax.broadcasted_iota(jnp.int32, (tk, sh), 1)
        dst = tile * tk + lax.broadcasted_iota(jnp.int32, (tk, sh), 0)
        pick = (_w_in_dest(src) == dst).astype(BF16)
        part = _split_dot(dw_ref[...], pick)

        @pl.when(kk == 0)
        def _():
            o_ref[...] = part

        @pl.when(kk > 0)
        def _():
            o_ref[...] += part

    return pl.pallas_call(
        body, name=name,
        grid_spec=pltpu.PrefetchScalarGridSpec(
            num_scalar_prefetch=1, grid=(N_DEV, nt),
            in_specs=[pl.BlockSpec((d, tk), lambda j, kk, tab: (0, tab[j * nt + kk]))],
            out_specs=pl.BlockSpec((None, d, sh), lambda j, kk, tab: (j, 0, 0))),
        out_shape=jax.ShapeDtypeStruct((N_DEV, d, sh), F32),
        compiler_params=_cparams(("parallel", "arbitrary")),
    )(table, dw)


def _gate_up_swiglu(h1, wgu, *, name):
    t_rows, k = h1.shape
    w = wgu.shape[2]
    tm = _tile(t_rows, 1024)

    def body(a_ref, wg_ref, wu_ref, gu_ref, act_ref):
        a = a_ref[...].astype(BF16)
        gate = jnp.dot(a, wg_ref[...], preferred_element_type=F32)
        up = jnp.dot(a, wu_ref[...], preferred_element_type=F32)
        gu_ref[0] = gate.astype(gu_ref.dtype)
        gu_ref[1] = up.astype(gu_ref.dtype)
        act_ref[...] = (gate * _sigmoid(gate) * up).astype(act_ref.dtype)

    return pl.pallas_call(
        body, name=name, grid=(t_rows // tm, 4),
        in_specs=[pl.BlockSpec((tm, k), lambda i, j: (i, 0)),
                  pl.BlockSpec((None, k, w), lambda i, j: (j, 0, 0)),
                  pl.BlockSpec((None, k, w), lambda i, j: (j + 4, 0, 0))],
        out_specs=[pl.BlockSpec((2, None, tm, w), lambda i, j: (0, j, i, 0)),
                   pl.BlockSpec((None, tm, w), lambda i, j: (j, i, 0))],
        out_shape=[jax.ShapeDtypeStruct((2, 4, t_rows, w), BF16), jax.ShapeDtypeStruct((4, t_rows, w), BF16)],
        compiler_params=_cparams(("parallel", "arbitrary")),
    )(h1, wgu, wgu)


def _down_dx_swiglu(dffn, wdown, gu, *, name):
    t_rows, k = dffn.shape
    w = gu.shape[3]
    tm = _tile(t_rows, 1024)

    def body(d_ref, w_ref, gu_ref, o_ref):
        dact = lax.dot_general(d_ref[...].astype(BF16), w_ref[...], _NT, preferred_element_type=F32)
        gate, up = gu_ref[0].astype(F32), gu_ref[1].astype(F32)
        sg = _sigmoid(gate)
        o_ref[0] = (dact * up * (sg * (1.0 + gate * (1.0 - sg)))).astype(o_ref.dtype)
        o_ref[1] = (dact * gate * sg).astype(o_ref.dtype)

    blk = pl.BlockSpec((2, None, tm, w), lambda i, j: (0, j, i, 0))
    return pl.pallas_call(
        body, name=name, grid=(t_rows // tm, 4),
        in_specs=[pl.BlockSpec((tm, k), lambda i, j: (i, 0)), pl.BlockSpec((w, k), lambda i, j: (j, 0)), blk],
        out_specs=blk, out_shape=jax.ShapeDtypeStruct(gu.shape, BF16),
        compiler_params=_cparams(("parallel", "arbitrary")),
    )(dffn, wdown, gu)


BIG = ("w_in", "mla_w_uq", "mla_w_ukv", "w_out", "w_gate_up", "w_down", "ple_w_gate", "ple_w_proj")
SMALL = ("ln_in_g", "ln_in_b", "hgrn_lb_logits", "hgrn_norm_g", "sgu_ln_g", "sgu_ln_b", "sgu_w_s", "sgu_b_s",
         "mla_q_norm_g", "mla_kv_norm_g", "ln1_g", "ln1_b", "ln2_g", "ln2_b")
ORDER = ("ln_in_g", "ln_in_b", "w_in", "hgrn_lb_logits", "hgrn_norm_g", "sgu_ln_g", "sgu_ln_b", "sgu_w_s", "sgu_b_s",
         "mla_q_norm_g", "mla_w_uq", "mla_kv_norm_g", "mla_w_ukv", "w_out", "ln1_g", "ln1_b", "w_gate_up", "w_down",
         "ple_w_gate", "ple_w_proj", "ln2_g", "ln2_b")


def _slab(a, align):
    s = a.reshape(-1, LANES)
    pad = -s.shape[0] % align
    return jnp.pad(s, ((0, pad), (0, 0))) if pad else s


def _pack(arrays, align=16, total_align=512):
    s = jnp.concatenate([_slab(a, align) for a in arrays], axis=0)
    pad = -s.shape[0] % total_align
    return jnp.pad(s, ((0, pad), (0, 0))) if pad else s


def _unpack(slab, shapes, align=16):
    out, r0 = [], 0
    for s in shapes:
        nr = math.prod(s) // LANES
        out.append(slab[r0:r0 + nr].reshape(s))
        r0 += nr + (-nr % align)
    return out


def _weight_shards(w, li):
    uq_pad = ((0, 0), (0, LANES - ATT_D))
    shards = {k: w[k][li] for k in BIG}
    shards["mla_w_uq"] = jnp.pad(shards["mla_w_uq"], uq_pad)
    return {k: s.astype(BF16) for k, s in shards.items()}


def _usable_weights(g, *, name):
    out = {}
    for k, a in g.items():
        if k == "w_in":
            out[k] = _place_w_in(a, name=name + "_place_w_in")
        elif k in ("w_out", "w_down", "ple_w_gate"):
            out[k] = a.reshape(a.shape[0] * a.shape[1], a.shape[2])
        else:
            out[k] = a
    return out


BY_COLUMNS = ("mla_w_uq", "mla_w_ukv", "ple_w_proj")


def _as_pairs(k, g):
    if k in BY_COLUMNS:
        return g
    if g.ndim == 2:
        return g.reshape((4, 2, g.shape[0] // N_DEV) + g.shape[1:])
    return g.reshape((4, 2) + g.shape[1:])


def _twice(fn):
    return lambda *a: fn(*a) * 2


def _layer_forward(li, h, hb, p_i, wts, sm, lbs, tables, alpha, hgrn_job=None, after_hgrn=None, attn_job=None,
                   after_attn=None):
    n = f"l{li}_"
    row1 = lambda a: a.reshape(1, -1)
    projp = _mm(hb, wts["w_in"], name=n + "proj")
    ng = row1(sm["hgrn_norm_g"][li])
    res = _hgrn_fwd(projp, lbs[li], ng, name=n + "hgrn_fwd", job=hgrn_job)
    if hgrn_job is not None:
        res, got = res
        wts = dict(wts, **after_hgrn(got))
    o_a, o_pre, states = res
    lg, lbias = row1(sm["sgu_ln_g"][li]), row1(sm["sgu_ln_b"][li])
    w_s = sm["sgu_w_s"][li]
    bias_full = jnp.repeat(sm["sgu_b_s"][li].T, HEAD, axis=1)
    o_b = _sgu_fwd(projp, lg, lbias, w_s, bias_full, name=n + "sgu_fwd")
    qg, kvg = row1(sm["mla_q_norm_g"][li]), row1(sm["mla_kv_norm_g"][li])
    cq_view, ckv_view = (projp, 384, P_CQ // 384), (projp, 256, P_CKV // 256)
    (cqn,) = _rowwise(_fn_rms, [cq_view], [qg], [(384, BF16)], name=n + "q_norm")
    (ckvn,) = _rowwise(_fn_rms, [ckv_view], [kvg], [(256, BF16)], name=n + "kv_norm")
    q = _mm(cqn, wts["mla_w_uq"], name=n + "uq")
    kv = _mm(ckvn, wts["mla_w_ukv"], name=n + "ukv")
    qr, kf, kvb = _mla_prep(q, kv, projp, tables, name=n + "mla_prep")
    res = _attn_fwd(qr, kf, kvb, name=n + "attn_fwd", job=attn_job)
    if attn_job is not None:
        res, got = res
        wts = dict(wts, **after_attn(got))
    o_c, lse = res
    cat = jnp.concatenate([o_a, o_b, o_c.astype(BF16)], axis=1)
    mix = _mm(cat, wts["w_out"], name=n + "out_proj")
    g1, b1 = row1(sm["ln1_g"][li]), row1(sm["ln1_b"][li])
    d = h.shape[1]
    h1, h1b = _rowwise(_twice(_make_post_mix(alpha)), [h, mix], [g1, b1], [(d, F32), (d, BF16)], name=n + "ln1")
    gu, act = _gate_up_swiglu(h1b, wts["w_gate_up"], name=n + "gate_up")
    ffn = _mm(act, wts["w_down"], am="bmk", tm=2048, name=n + "down")
    pg = _mm(h1b, wts["ple_w_gate"], name=n + "ple_gate")
    pp = _mm(p_i, wts["ple_w_proj"], name=n + "ple_proj")
    g2, b2 = row1(sm["ln2_g"][li]), row1(sm["ln2_b"][li])
    h2, h2b = _rowwise(_twice(_make_ple_ln(alpha)), [h1, ffn, pg, pp], [g2, b2], [(d, F32), (d, BF16)],
                       name=n + "ln2")
    saved = dict(h=h, hb=hb, h1b=h1b, projp=projp, o_pre=o_pre, states=states, cqn=cqn, ckvn=ckvn, qr=qr, kf=kf, kvb=kvb, o_c=o_c,
                 lse=lse, cat=cat, mix=mix, h1=h1, gu=gu, act=act, ffn=ffn, pg=pg, pp=pp, ng=ng, lg=lg, wts=wts,
                 lbias=lbias, w_s=w_s, bias_full=bias_full, qg=qg, kvg=kvg, g1=g1, b1=b1, g2=g2, b2=b2)
    return (h2, h2b), saved


RS_EARLY = ("ple_w_proj", "ple_w_gate", "w_down", "w_gate_up", "w_out")
RS_LATE = ("mla_w_uq", "mla_w_ukv", "w_in")


def _layer_backward(li, dh2_parts, p_i, sv, lbs, tables, alpha, core, carried=None):
    n = f"l{li}_b_"
    wts = sv["wts"]
    gr = {}
    dh1_a, dffn, dpg, dpp, gr["ln2_g"], gr["ln2_b"] = _rowwise_vjp(
        _make_ple_ln(alpha), [sv["h1"], sv["ffn"], sv["pg"], sv["pp"]], [sv["g2"], sv["b2"]], [dh2_parts],
        groups=[[0], [1], [2], [3]], gdtypes=[F32, BF16, BF16, BF16], name=n + "ln2")
    big = {}
    big["ple_w_proj"] = _mm(p_i, dpp, am="km", tk=2048, name=n + "ple_proj_dw")
    big["ple_w_gate"] = _mm(sv["h1b"], dpg, am="km", name=n + "ple_gate_dw")
    dh1_b = _mm(dpg, wts["ple_w_gate"], bm="nk", name=n + "ple_gate_dx")
    big["w_down"] = _mm(sv["act"], dffn, am="bkm", tk=2048, name=n + "down_dw")
    dgu = _down_dx_swiglu(dffn, wts["w_down"], sv["gu"], name=n + "down_dx")
    dgu = dgu.reshape((N_DEV,) + dgu.shape[2:])
    big["w_gate_up"], carried_got = _mm(sv["h1b"], dgu, am="km", bm="bkn", om="bmn", tk=2048, name=n + "gate_up_dw",
                                        job=carried), None
    if carried is not None:
        big["w_gate_up"], carried_got = big["w_gate_up"]
    early = [_as_pairs(k, big[k]) for k in RS_EARLY[:-1]]
    dh1_c, theirs = _mm(dgu, wts["w_gate_up"], am="bmk", bm="bnk", tm=2048, name=n + "gate_up_dx",
                        job=_pair_job(early))
    dh_a, dmix, gr["ln1_g"], gr["ln1_b"] = _rowwise_vjp(
        _make_post_mix(alpha), [sv["h"], sv["mix"]], [sv["g1"], sv["b1"]], [[dh1_a, dh1_b, dh1_c]],
        groups=[[0], [1]], gdtypes=[F32, BF16], name=n + "ln1")
    big["w_out"] = _mm(sv["cat"], dmix, am="km", name=n + "out_proj_dw")
    early.append(_as_pairs("w_out", big["w_out"]))
    dcat, their_w_out = _mm(dmix, wts["w_out"], bm="nk", name=n + "out_proj_dx", job=_pair_job(early[-1:]))
    sums = [_pair_add(x, r, core, name=n + "pair_add_" + k)
            for k, x, r in zip(RS_EARLY, early, list(theirs) + list(their_w_out))]

    (dqr, dkv, dkf), early_quads = _attn_bwd(sv["qr"], sv["kf"], sv["kvb"], dcat, sv["o_c"], sv["lse"],
                                             name=n + "attn", job=_quad_job(sums))
    dqpad, dkr = _mla_prep_bwd(dqr, dkf, tables, name=n + "mla_prep")
    big["mla_w_uq"] = _mm(sv["cqn"], dqpad, am="km", tk=2048, name=n + "uq_dw")
    dcqn = _mm(dqpad, wts["mla_w_uq"], bm="nk", name=n + "uq_dx")
    big["mla_w_ukv"] = _mm(sv["ckvn"], dkv, am="km", tk=2048, name=n + "ukv_dw")
    dckvn = _mm(dkv, wts["mla_w_ukv"], bm="nk", name=n + "ukv_dx")
    projp = sv["projp"]
    dcq, gr["mla_q_norm_g"] = _rowwise_vjp(_fn_rms, [(projp, 384, P_CQ // 384)], [sv["qg"]], [[dcqn]],
                                           groups=[[0]], gdtypes=[BF16], name=n + "q_norm")
    dckv, gr["mla_kv_norm_g"] = _rowwise_vjp(_fn_rms, [(projp, 256, P_CKV // 256)], [sv["kvg"]], [[dckvn]],
                                             groups=[[0]], gdtypes=[BF16], name=n + "kv_norm")
    dsgu, gr["sgu_ln_g"], gr["sgu_ln_b"], gr["sgu_w_s"], gr["sgu_b_s"] = _sgu_bwd(
        projp, sv["lg"], sv["lbias"], sv["w_s"], sv["bias_full"], dcat, name=n + "sgu")
    dhg, gr["hgrn_norm_g"], gr["lower_bound"] = _hgrn_bwd(
        projp, lbs[li], sv["ng"], sv["o_pre"], sv["states"], dcat, name=n + "hgrn")
    dprojp = jnp.concatenate([dhg, dsgu, dcq, dkr, dckv], axis=1)
    big["w_in"] = _unplace_w_in(_mm(sv["hb"], dprojp, am="km", tk=2048, name=n + "proj_dw"),
                                name=n + "proj_dw_shards")
    late = [_as_pairs(k, big[k]) for k in RS_LATE]
    dh_b, theirs = _mm(dprojp, wts["w_in"], bm="nk", name=n + "proj_dx", job=_pair_job(late))
    late_sums = [_pair_add(x, r, core, name=n + "pair_add_" + k) for k, x, r in zip(RS_LATE, late, theirs)]
    return [dh_a, dh_b], gr, early_quads, late_sums, carried_got


def kernel(x, p, positions, ln_in_g, ln_in_b, w_in, hgrn_lb_logits, hgrn_norm_g, sgu_ln_g, sgu_ln_b, sgu_w_s, sgu_b_s, mla_q_norm_g, mla_w_uq, mla_kv_norm_g, mla_w_ukv, w_out, ln1_g, ln1_b, w_gate_up, w_down, ple_w_gate, ple_w_proj, ln2_g, ln2_b, loss_target, m_ln_in_g, m_ln_in_b, m_w_in, m_hgrn_lb_logits, m_hgrn_norm_g, m_sgu_ln_g, m_sgu_ln_b, m_sgu_w_s, m_sgu_b_s, m_mla_q_norm_g, m_mla_w_uq, m_mla_kv_norm_g, m_mla_w_ukv, m_w_out, m_ln1_g, m_ln1_b, m_w_gate_up, m_w_down, m_ple_w_gate, m_ple_w_proj, m_ln2_g, m_ln2_b, v_ln_in_g, v_ln_in_b, v_w_in, v_hgrn_lb_logits, v_hgrn_norm_g, v_sgu_ln_g, v_sgu_ln_b, v_sgu_w_s, v_sgu_b_s, v_mla_q_norm_g, v_mla_w_uq, v_mla_kv_norm_g, v_mla_w_ukv, v_w_out, v_ln1_g, v_ln1_b, v_w_gate_up, v_w_down, v_ple_w_gate, v_ple_w_proj, v_ln2_g, v_ln2_b):
    args = dict(locals())
    w = {k: args[k] for k in ORDER}
    m = {k: args["m_" + k] for k in ORDER}
    v = {k: args["v_" + k] for k in ORDER}
    depth = w_in.shape[0]
    assert depth == 2, "the lower-bound kernel is written for two layers"
    alpha = (2 * depth) ** 0.25
    xs, tgt = x[0], loss_target[0]
    d_model = xs.shape[1]

    shards = [_weight_shards(w, li) for li in range(depth)]
    on_hgrn0 = ("mla_w_uq", "mla_w_ukv", "w_out", "ple_w_gate", "ple_w_proj")
    ffn0 = ("w_gate_up", "w_down")
    first1 = ("w_in", "mla_w_uq", "mla_w_ukv", "w_out")
    on_attn1 = ("w_gate_up", "w_down", "ple_w_gate", "ple_w_proj")
    layer1_first = {}

    def after_hgrn0(got):
        got = _gather_forward(got, name="gather_l0a_forward")
        return _usable_weights(dict(zip(on_hgrn0, got)), name="l0")

    def after_attn0(got):
        got = _gather_forward(got, name="gather_l0b_forward")
        layer1_first.update(_usable_weights(dict(zip(first1, got[len(ffn0):])), name="l1"))
        return _usable_weights(dict(zip(ffn0, got[:len(ffn0)])), name="l0")

    def after_attn1(got):
        got = _gather_forward(got, name="gather_l1_forward")
        return _usable_weights(dict(zip(on_attn1, got)), name="l1")

    tables = _rope_tables(positions[0])
    row1 = lambda a: a.reshape(1, -1)
    l0, l1 = row1(hgrn_lb_logits[0]), row1(hgrn_lb_logits[1])
    lbs = _rowwise(_fn_lower_bounds, [l0, l1], [], [(HG_W, F32), (HG_W, F32)], name="lower_bounds")

    gin, bin_ = row1(ln_in_g), row1(ln_in_b)
    (h, hb), g_in = _rowwise(_twice(_fn_ln), [xs], [gin, bin_], [(d_model, F32), (d_model, BF16)], name="ln_in",
                             job=_gather_job([shards[0]["w_in"]]))
    w_in0 = _usable_weights({"w_in": _gather_forward(g_in, name="gather_l0_w_in_forward")[0]}, name="l0")
    (h, hb), sv0 = _layer_forward(
        0, h, hb, p[0, 0], w_in0, w, lbs, tables, alpha,
        hgrn_job=_gather_job([shards[0][k] for k in on_hgrn0]), after_hgrn=after_hgrn0,
        attn_job=_gather_job([shards[0][k] for k in ffn0] + [shards[1][k] for k in first1]), after_attn=after_attn0)
    (h, _), sv1 = _layer_forward(
        1, h, hb, p[1, 0], layer1_first, w, lbs, tables, alpha,
        attn_job=_gather_job([shards[1][k] for k in on_attn1]), after_attn=after_attn1)
    saved = [sv0, sv1]
    dy, loss_local = _loss_and_grad(h, tgt, name="loss")
    loss = lax.psum(loss_local[0, 0], ("x", "y", "c"))

    core = lax.axis_index("c").astype(jnp.int32).reshape(1)
    dparts, grads, quads, carried = [dy], [None] * depth, [None] * depth, None
    for li in reversed(range(depth)):
        dparts, grads[li], early_quads, late_sums, late_quads = _layer_backward(
            li, dparts, p[li, 0], saved[li], lbs, tables, alpha, core, carried=carried)
        quads[li] = dict(zip(RS_EARLY, early_quads))
        if carried is not None:
            quads[li + 1].update(zip(RS_LATE, late_quads))
        carried = _quad_job(late_sums)
    (dx, d_gin, d_bin), late_quads = _rowwise_vjp(_fn_ln, [xs], [gin, bin_], [dparts], groups=[[0]], name="ln_in_b",
                                                   job=carried)
    quads[0].update(zip(RS_LATE, late_quads))
    dl0, dl1 = _rowwise_vjp(_fn_lower_bounds, [l0, l1], [], [[grads[0]["lower_bound"]], [grads[1]["lower_bound"]]],
                            groups=[[0], [1]], name="lower_bounds_b")

    prefixes = ("grad_", "delta_", "new_m_", "new_v_")
    uq_pad = ((0, 0), (0, 0), (0, LANES - ATT_D))
    state = {k: ((jnp.pad(w[k], uq_pad), jnp.pad(m[k], uq_pad), jnp.pad(v[k], uq_pad)) if k == "mla_w_uq"
                 else (w[k], m[k], v[k])) for k in BIG}
    out = {}
    for k in BIG:
        res4 = None
        for li in range(depth):
            res4 = _adamw(quads[li][k], *state[k], li, name=f"adamw_l{li}_{k}", into=res4)
        for pre, a in zip(prefixes, res4):
            out[pre + k] = a[:, :, :ATT_D] if k == "mla_w_uq" else a

    small_g = {"ln_in_g": d_gin.reshape(-1), "ln_in_b": d_bin.reshape(-1),
               "hgrn_lb_logits": jnp.stack([dl0.reshape(-1), dl1.reshape(-1)])}
    for k in SMALL[3:]:
        small_g[k] = jnp.stack([grads[li][k].reshape(w[k].shape[1:]) for li in range(depth)])
    (small_parts,) = _all_gather([_pack([small_g[k] for k in SMALL])], name="gather_small_grads", columns=False)
    slabs = _adamw(small_parts, _pack([w[k] for k in SMALL]), _pack([m[k] for k in SMALL]),
                   _pack([v[k] for k in SMALL]), None, name="adamw_small")
    shapes = [w[k].shape for k in SMALL]
    for pre, slab in zip(prefixes, slabs):
        for k, a in zip(SMALL, _unpack(slab, shapes)):
            out[pre + k] = a
    res = [loss, dx[None]]
    for prefix in ("grad_", "delta_", "new_m_", "new_v_"):
        res += [out[prefix + k] for k in ORDER]
    return tuple(res)
```

```python
import functools
import math

import jax
import jax.numpy as jnp
import numpy as np
from jax import lax
from jax.experimental import pallas as pl
from jax.experimental.pallas import tpu as pltpu

F32 = jnp.float32
BF16 = jnp.bfloat16
MESH = pl.DeviceIdType.MESH

LN_EPS = 1e-5
RMS_EPS = 1e-6
ROPE_THETA = 10000.0
ADAM_LR, ADAM_B1, ADAM_B2, ADAM_EPS, ADAM_WD, ADAM_STEP = 0.001, 0.9, 0.999, 1e-08, 0.01, 10

N_DEV = 8
LANES = 128
HG_CHUNK = 16
HG_W = 256
HEAD = 64
SGU_CHUNK = 128
N_ATT_HEADS = 8
ATT_D = 96
VMEM_LIMIT = 56 * 1024 * 1024

HG_TILE = 256
ATT_TQ = 512
ROW_TILE = 256

P_CQ, P_KR, P_CKV, P_COLS = 1536, 1920, 2048, 2304


def _cparams(sem):
    return pltpu.CompilerParams(dimension_semantics=sem, vmem_limit_bytes=VMEM_LIMIT)


_ANY = pl.BlockSpec(memory_space=pl.ANY)


def _call(body, operands, *, name, grid, in_specs, out_specs, out_shape, sem, scratch_shapes=(), job=None):
    if job is None:
        return pl.pallas_call(body, name=name, grid=grid, in_specs=in_specs, out_specs=out_specs, out_shape=out_shape,
                              scratch_shapes=list(scratch_shapes), compiler_params=_cparams(sem))(*operands)
    single = not isinstance(out_shape, (list, tuple))
    shapes = [out_shape] if single else list(out_shape)
    ospecs = [out_specs] if single else list(out_specs)
    ni, no, ns = len(operands), len(shapes), len(scratch_shapes)
    ji, jo = len(job.inputs), len(job.out_shapes)

    def hosted(*refs):
        p = 0
        parts = []
        for cnt in (ni, ji, no, jo, ns):
            parts.append(refs[p:p + cnt])
            p += cnt
        ins, jins, outs, jouts, scr = parts
        jsems = refs[p:]
        ids = [pl.program_id(a) for a in range(len(grid))]
        first = functools.reduce(lambda a, b: a & b, [i == 0 for i in ids])
        last = functools.reduce(lambda a, b: a & b, [i == g - 1 for i, g in zip(ids, grid)])

        @pl.when(first)
        def _():
            job.start(jins, jouts, jsems)

        body(*ins, *outs, *scr)

        @pl.when(last)
        def _():
            job.finish(jins, jouts, jsems)

    res = pl.pallas_call(
        hosted, name=name, grid=grid,
        in_specs=list(in_specs) + [_ANY] * ji, out_specs=ospecs + [_ANY] * jo,
        out_shape=shapes + list(job.out_shapes),
        scratch_shapes=list(scratch_shapes) + [pltpu.SemaphoreType.DMA((c,)) for c in job.sem_counts],
        input_output_aliases=job.aliases(ni, no),
        compiler_params=_cparams(("arbitrary",) * len(grid)),
    )(*operands, *job.inputs)
    own = res[0] if single else res[:no]
    return own, res[no:]


class _Job:
    def __init__(self, inputs, out_shapes, sem_counts, start, finish, in_place=False):
        self.inputs, self.out_shapes, self.sem_counts = list(inputs), list(out_shapes), list(sem_counts)
        self.start, self.finish, self.in_place = start, finish, in_place

    def aliases(self, first_in, first_out):
        return {first_in + i: first_out + i for i in range(len(self.inputs))} if self.in_place else {}


def _copies_job(inputs, out_shapes, n_remote, n_local, make, in_place=False):
    def start(jins, jouts, sems):
        sends, _, local = make(jins, jouts, *sems)
        for cp in local + sends:
            cp.start()

    def finish(jins, jouts, sems):
        sends, recvs, local = make(jins, jouts, *sems)
        for cp in recvs:
            cp.wait_recv()
        for cp in sends:
            cp.wait_send()
        for cp in local:
            cp.wait()

    return _Job(inputs, out_shapes, [n_remote, n_remote, max(n_local, 1)], start, finish, in_place)


def _run_job(job, *, name):
    ji, jo = len(job.inputs), len(job.out_shapes)

    def body(*refs):
        jins, jouts, sems = refs[:ji], refs[ji:ji + jo], refs[ji + jo:]
        job.start(jins, jouts, sems)
        job.finish(jins, jouts, sems)

    return pl.pallas_call(
        body, name=name, out_shape=list(job.out_shapes), in_specs=[_ANY] * ji, out_specs=[_ANY] * jo,
        scratch_shapes=[pltpu.SemaphoreType.DMA((c,)) for c in job.sem_counts],
        input_output_aliases=job.aliases(0, 0),
    )(*job.inputs)


def _tile(n, pref):
    if n % pref == 0:
        return pref
    best = None
    t = LANES
    while t <= min(n, pref):
        if n % t == 0:
            best = t
        t += LANES
    return best if best is not None else n


def _mm(a, b, *, am="mk", bm="kn", om="mn", out_dtype=F32, tm=1024, tn=1024, tk=1024, name, job=None):
    if am == "mk":
        m, k = a.shape
    elif am == "km":
        k, m = a.shape
    elif am == "bmk":
        m, tk = a.shape[1], a.shape[2]
        k = a.shape[0] * tk
    else:
        k, tm = a.shape[1], a.shape[2]
        m = a.shape[0] * tm
    if bm == "kn":
        kb_, n = b.shape
    elif bm == "nk":
        n, kb_ = b.shape
    elif bm == "bkn":
        kb_, tn = b.shape[1], b.shape[2]
        n = b.shape[0] * tn
    else:
        n, tk = b.shape[1], b.shape[2]
        kb_ = b.shape[0] * tk
    assert kb_ == k, (a.shape, b.shape, am, bm)
    tm, tn, tk = _tile(m, tm), _tile(n, tn), _tile(k, tk)
    nk = k // tk
    dims = (((0 if am in ("km", "bkm") else 1,), (1 if bm in ("nk", "bnk") else 0,)), ((), ()))

    a_spec = {"mk": pl.BlockSpec((tm, tk), lambda i, j, kk: (i, kk)),
              "km": pl.BlockSpec((tk, tm), lambda i, j, kk: (kk, i)),
              "bmk": pl.BlockSpec((None, tm, tk), lambda i, j, kk: (kk, i, 0)),
              "bkm": pl.BlockSpec((None, tk, tm), lambda i, j, kk: (i, kk, 0))}[am]
    b_spec = {"kn": pl.BlockSpec((tk, tn), lambda i, j, kk: (kk, j)),
              "nk": pl.BlockSpec((tn, tk), lambda i, j, kk: (j, kk)),
              "bkn": pl.BlockSpec((None, tk, tn), lambda i, j, kk: (j, kk, 0)),
              "bnk": pl.BlockSpec((None, tn, tk), lambda i, j, kk: (kk, j, 0))}[bm]
    if om == "mn":
        o_spec, o_shape = pl.BlockSpec((tm, tn), lambda i, j, kk: (i, j)), (m, n)
    else:
        o_spec, o_shape = pl.BlockSpec((None, tm, tn), lambda i, j, kk: (j, i, 0)), (n // tn, m, tn)

    def body(a_ref, b_ref, o_ref, *acc):
        kk = pl.program_id(2)
        prod = lax.dot_general(a_ref[...].astype(BF16), b_ref[...].astype(BF16), dims, preferred_element_type=F32)
        if nk == 1:
            o_ref[...] = prod.astype(o_ref.dtype)
            return
        acc_ref, = acc

        @pl.when(kk == 0)
        def _():
            acc_ref[...] = prod

        if nk > 2:
            @pl.when((kk > 0) & (kk < nk - 1))
            def _():
                acc_ref[...] += prod

        @pl.when(kk == nk - 1)
        def _():
            o_ref[...] = (acc_ref[...] + prod).astype(o_ref.dtype)

    return _call(body, (a, b), name=name, grid=(m // tm, n // tn, nk), in_specs=[a_spec, b_spec], out_specs=o_spec,
                 out_shape=jax.ShapeDtypeStruct(o_shape, out_dtype),
                 scratch_shapes=[pltpu.VMEM((tm, tn), F32)] if nk > 1 else [],
                 sem=("parallel", "parallel", "arbitrary"), job=job)


def _row_operand(a, tile):
    if isinstance(a, tuple):
        arr, w, j = a
        return arr, pl.BlockSpec((tile, w), lambda i, j=j: (i, j))
    return a, pl.BlockSpec((tile, a.shape[1]), lambda i: (i, 0))


def _const_spec(c):
    nd = c.ndim
    return pl.BlockSpec(c.shape, lambda i, nd=nd: (0,) * nd)


def _rowwise(fn, rows, consts, outs, *, name, accs=(), tile=None, job=None):
    t_rows = (rows[0][0] if isinstance(rows[0], tuple) else rows[0]).shape[0]
    tile = min(tile or ROW_TILE, t_rows)
    arrs, specs = zip(*[_row_operand(a, tile) for a in rows])
    nin, no = len(rows) + len(consts), len(outs)

    def body(*refs):
        res = fn(*[r[...] for r in refs[:nin]])
        for r, v in zip(refs[nin:nin + no], res[:no]):
            r[...] = v.astype(r.dtype)
        if accs:
            a_refs = refs[nin + no:]

            @pl.when(pl.program_id(0) == 0)
            def _():
                for r in a_refs:
                    r[...] = jnp.zeros_like(r)

            for r, v in zip(a_refs, res[no:]):
                r[...] += v

    out_shape = [jax.ShapeDtypeStruct((t_rows, w), dt) for w, dt in outs]
    out_shape += [jax.ShapeDtypeStruct(s, F32) for s in accs]
    out_specs = [pl.BlockSpec((tile, w), lambda i: (i, 0)) for w, _ in outs]
    out_specs += [pl.BlockSpec(s, lambda i, nd=len(s): (0,) * nd) for s in accs]
    return _call(body, (*arrs, *consts), name=name, grid=(t_rows // tile,),
                 in_specs=list(specs) + [_const_spec(c) for c in consts],
                 out_specs=out_specs, out_shape=out_shape, sem=("arbitrary",), job=job)


def _rowwise_vjp(fn, rows, consts, cts, *, name, groups, tile=None, gdtypes=None, job=None):
    t_rows = (rows[0][0] if isinstance(rows[0], tuple) else rows[0]).shape[0]
    tile = min(tile or ROW_TILE, t_rows)
    arrs, specs = zip(*[_row_operand(a, tile) for a in rows])
    flat_cts = [c for group in cts for c in group]
    ct_arrs, ct_specs = zip(*[_row_operand(a, tile) for a in flat_cts])
    nr, nc, nct, ng = len(rows), len(consts), len(flat_cts), len(groups)

    def width(a):
        return a[1] if isinstance(a, tuple) else a.shape[1]

    def body(*refs):
        rv = [r[...].astype(F32) for r in refs[:nr]]
        cv = [r[...] for r in refs[nr:nr + nc]]
        ct_refs = refs[nr + nc:nr + nc + nct]
        ctv, pos = [], 0
        for group in cts:
            s = ct_refs[pos][...].astype(F32)
            for r in ct_refs[pos + 1:pos + len(group)]:
                s = s + r[...].astype(F32)
            ctv.append(s)
            pos += len(group)
        _, pull = jax.vjp(fn, *rv, *cv)
        grads = pull(tuple(ctv))
        g_refs = refs[nr + nc + nct:nr + nc + nct + ng]
        for r, idx in zip(g_refs, groups):
            parts = [grads[i] for i in idx]
            r[...] = (parts[0] if len(parts) == 1 else jnp.concatenate(parts, axis=1)).astype(r.dtype)
        c_refs = refs[nr + nc + nct + ng:]

        @pl.when(pl.program_id(0) == 0)
        def _():
            for r in c_refs:
                r[...] = jnp.zeros_like(r)

        for r, v in zip(c_refs, grads[nr:]):
            r[...] += v

    gw = [sum(width(rows[i]) for i in idx) for idx in groups]
    gdtypes = gdtypes or [F32] * ng
    out_shape = [jax.ShapeDtypeStruct((t_rows, w), dt) for w, dt in zip(gw, gdtypes)]
    out_shape += [jax.ShapeDtypeStruct(c.shape, F32) for c in consts]
    out_specs = [pl.BlockSpec((tile, w), lambda i: (i, 0)) for w in gw]
    out_specs += [_const_spec(c) for c in consts]
    return _call(body, (*arrs, *consts, *ct_arrs), name=name, grid=(t_rows // tile,),
                 in_specs=list(specs) + [_const_spec(c) for c in consts] + list(ct_specs),
                 out_specs=out_specs, out_shape=out_shape, sem=("arbitrary",), job=job)


def _layer_norm(x, g, b):
    mu = jnp.mean(x, axis=-1, keepdims=True)
    xc = x - mu
    var = jnp.mean(xc * xc, axis=-1, keepdims=True)
    return xc * lax.rsqrt(var + LN_EPS) * g + b


def _sigmoid(x):
    return 1.0 / (1.0 + jnp.exp(-x))


def _fn_ln(x, g, b):
    return (_layer_norm(x, g, b),)


def _fn_rms(x, g):
    return (x * lax.rsqrt(jnp.mean(x * x, axis=-1, keepdims=True) + RMS_EPS) * g,)


def _make_post_mix(alpha):
    def fn(h, mix, g, b):
        return (_layer_norm(alpha * h + mix, g, b),)
    return fn


def _make_ple_ln(alpha):
    def fn(h1, ffn, pg, pp, g, b):
        return (_layer_norm(alpha * h1 + ffn + _sigmoid(pg) * pp, g, b),)
    return fn


def _fn_lower_bounds(l0, l1):
    m = jnp.maximum(l0, l1)
    e0, e1 = jnp.exp(l0 - m), jnp.exp(l1 - m)
    s = e0 + e1
    p0, p1 = e0 / s, e1 / s
    return (p0 - p0, (p0 + p1) - p0)


def _split_dot(x, e_bf16):
    hi = x.astype(BF16)
    lo = (x - hi.astype(F32)).astype(BF16)
    return (jnp.dot(hi, e_bf16, preferred_element_type=F32) + jnp.dot(lo, e_bf16, preferred_element_type=F32))


def _hgrn_common(th):
    rm = lax.broadcasted_iota(jnp.int32, (th, HG_W), 0) % HG_CHUNK

    def seg_cumsum(x):
        for s in (1, 2, 4, 8):
            x = x + jnp.where(rm >= s, pltpu.roll(x, s, 0), 0.0)
        return x

    def seg_rcumsum(x):
        for s in (1, 2, 4, 8):
            x = x + jnp.where(rm < HG_CHUNK - s, pltpu.roll(x, th - s, 0), 0.0)
        return x

    ri = lax.broadcasted_iota(jnp.int32, (HG_W, HG_W), 0) // HEAD
    ci = lax.broadcasted_iota(jnp.int32, (HG_W, HG_W), 1) // HEAD
    head_f32 = (ri == ci).astype(F32)
    head_bf16 = head_f32.astype(BF16)

    def headsum(x, pieces=2):
        if pieces == 1:
            return jnp.dot(x.astype(BF16), head_bf16, preferred_element_type=F32)
        return _split_dot(x, head_bf16)

    return rm, seg_cumsum, seg_rcumsum, head_f32, headsum


def _hgrn_gates(qr, fl, lb):
    sg = _sigmoid(fl)
    f = lb + (1.0 - lb) * sg
    sq = _sigmoid(qr)
    return sg, f, jnp.log(f), 1.0 - f, qr * sq, sq


def _shifted(x, d, th):
    return x if d == 0 else pltpu.roll(x, d, 0)


def _unshift(x, d, th):
    return x if d == 0 else pltpu.roll(x, th - d, 0)


def _hgrn_fwd(projp, lb, ng, *, name, job=None):
    t_rows = projp.shape[0]
    th = min(HG_TILE, t_rows)
    nct = th // HG_CHUNK

    def body(q_ref, f_ref, i_ref, g_ref, lb_ref, ng_ref, oa_ref, opre_ref, st_out_ref,
             st_ref, vtm_ref, kv_ref, qe_ref, dec_ref, oint_ref):
        rm, seg_cumsum, seg_rcumsum, head_f32, headsum = _hgrn_common(th)

        @pl.when(pl.program_id(0) == 0)
        def _():
            st_ref[...] = jnp.zeros_like(st_ref)

        qr, fl, v, g = q_ref[...], f_ref[...], i_ref[...], g_ref[...]
        _, f, lf, k, q, _ = _hgrn_gates(qr, fl, lb_ref[...])
        b = seg_cumsum(lf)

        o = jnp.zeros((th, HG_W), F32)
        for d in range(HG_CHUNK):
            kd, bd, vd = _shifted(k, d, th), _shifted(b, d, th), _shifted(v, d, th)
            e = jnp.exp(jnp.where(rm >= d, b - bd, -1e30))
            o = o + headsum(q * kd * e, 1) * vd

        blast = seg_rcumsum(jnp.where(rm == HG_CHUNK - 1, b, 0.0))
        kte = (k * jnp.exp(blast - b)).astype(BF16)
        qe_ref[...] = q * jnp.exp(b)
        dec_ref[...] = jnp.exp(blast)
        vt = v.T
        lane_chunk = lax.broadcasted_iota(jnp.int32, (HG_W, th), 1) // HG_CHUNK
        for c in range(nct):
            vtm_ref[c * HG_W:(c + 1) * HG_W, :] = jnp.where(lane_chunk == c, vt, 0.0).astype(BF16)
        kv_ref[...] = jnp.dot(vtm_ref[...], kte, preferred_element_type=F32)

        s = st_ref[...]
        for c in range(nct):
            rows = slice(c * HG_CHUNK, (c + 1) * HG_CHUNK)
            st_out_ref[c] = s
            oint_ref[rows, :] = lax.dot_general(qe_ref[rows, :].astype(BF16), s.astype(BF16),
                                                (((1,), (1,)), ((), ())), preferred_element_type=F32)
            dec = jnp.max(dec_ref[rows, :], axis=0, keepdims=True)
            s = s * dec + kv_ref[c * HG_W:(c + 1) * HG_W, :] * head_f32
        st_ref[...] = s

        o = o + oint_ref[...]
        opre_ref[...] = o
        r = lax.rsqrt(headsum(o * o) * (1.0 / HEAD) + RMS_EPS)
        oa_ref[...] = (o * r * ng_ref[...] * (g * _sigmoid(g))).astype(oa_ref.dtype)

    col = lambda j: pl.BlockSpec((th, HG_W), lambda i, j=j: (i, j))
    vec = pl.BlockSpec((1, HG_W), lambda i: (0, 0))
    row = pl.BlockSpec((th, HG_W), lambda i: (i, 0))
    n_chunks = t_rows // HG_CHUNK
    return _call(
        body, (projp, projp, projp, projp, lb, ng), name=name, grid=(t_rows // th,),
        in_specs=[col(0), col(1), col(2), col(3), vec, vec],
        out_specs=[row, row, pl.BlockSpec((nct, HG_W, HG_W), lambda i: (i, 0, 0))],
        out_shape=[jax.ShapeDtypeStruct((t_rows, HG_W), BF16), jax.ShapeDtypeStruct((t_rows, HG_W), F32),
                   jax.ShapeDtypeStruct((n_chunks, HG_W, HG_W), F32)],
        scratch_shapes=[pltpu.VMEM((HG_W, HG_W), F32), pltpu.VMEM((nct * HG_W, th), BF16),
                        pltpu.VMEM((nct * HG_W, HG_W), F32), pltpu.VMEM((th, HG_W), F32),
                        pltpu.VMEM((th, HG_W), F32), pltpu.VMEM((th, HG_W), F32)],
        sem=("arbitrary",), job=job)


def _hgrn_bwd(projp, lb, ng, opre, states, dcat, *, name):
    t_rows = projp.shape[0]
    th = min(HG_TILE, t_rows)
    nct = th // HG_CHUNK
    nt = t_rows // th

    def body(q_ref, f_ref, i_ref, g_ref, lb_ref, ng_ref, opre_ref, st_in_ref, do_ref,
             dproj_ref, dng_ref, dlb_ref,
             gst_ref, dotm_ref, qg_ref, v_ref, kte_ref, dop_ref, dec_ref, dkte_ref, dvi_ref, dqe_ref, ddec_ref):
        rm, seg_cumsum, seg_rcumsum, head_f32, headsum = _hgrn_common(th)

        @pl.when(pl.program_id(0) == 0)
        def _():
            gst_ref[...] = jnp.zeros_like(gst_ref)
            dng_ref[...] = jnp.zeros_like(dng_ref)
            dlb_ref[...] = jnp.zeros_like(dlb_ref)

        qr, fl, v, g = q_ref[...], f_ref[...], i_ref[...], g_ref[...]
        lb, ngv = lb_ref[...], ng_ref[...]
        sg, f, lf, k, q, sq = _hgrn_gates(qr, fl, lb)
        b = seg_cumsum(lf)
        blast = seg_rcumsum(jnp.where(rm == HG_CHUNK - 1, b, 0.0))
        eb = jnp.exp(b)
        ekb = jnp.exp(blast - b)
        qe, kte, dec = q * eb, k * ekb, jnp.exp(blast)

        do_out, op = do_ref[...], opre_ref[...]
        sgg = _sigmoid(g)
        sil = g * sgg
        r = lax.rsqrt(headsum(op * op) * (1.0 / HEAD) + RMS_EPS)
        on = op * r
        dng_ref[...] += jnp.sum(do_out * on * sil, axis=0, keepdims=True)
        dg = do_out * on * ngv * (sgg * (1.0 + g * (1.0 - sgg)))
        don = do_out * ngv * sil
        dop = r * (don - on * (headsum(don * on) * (1.0 / HEAD)))

        v_ref[...] = v
        kte_ref[...] = kte
        dop_ref[...] = dop
        dec_ref[...] = dec
        dot_t = dop.T
        lane_chunk = lax.broadcasted_iota(jnp.int32, (HG_W, th), 1) // HG_CHUNK
        for c in range(nct):
            dotm_ref[c * HG_W:(c + 1) * HG_W, :] = jnp.where(lane_chunk == c, dot_t, 0.0).astype(BF16)
        qg_ref[...] = jnp.dot(dotm_ref[...], qe.astype(BF16), preferred_element_type=F32)

        gs = gst_ref[...]
        for c in reversed(range(nct)):
            rows = slice(c * HG_CHUNK, (c + 1) * HG_CHUNK)
            s = st_in_ref[c]
            gm = (gs * head_f32).astype(BF16)
            dkte_ref[rows, :] = jnp.dot(v_ref[rows, :].astype(BF16), gm, preferred_element_type=F32)
            dvi_ref[rows, :] = lax.dot_general(kte_ref[rows, :].astype(BF16), gm, (((1,), (1,)), ((), ())),
                                               preferred_element_type=F32)
            dqe_ref[rows, :] = jnp.dot(dop_ref[rows, :].astype(BF16), s.astype(BF16), preferred_element_type=F32)
            ddec_ref[rows, :] = jnp.broadcast_to(jnp.sum(gs * s, axis=0, keepdims=True), (HG_CHUNK, HG_W))
            dec_c = jnp.max(dec_ref[rows, :], axis=0, keepdims=True)
            gs = gs * dec_c + qg_ref[c * HG_W:(c + 1) * HG_W, :] * head_f32
        gst_ref[...] = gs

        dkte, dqe = dkte_ref[...], dqe_ref[...]
        dq = dqe * eb
        dk = dkte * ekb
        db = dqe * qe - dkte * kte
        dv = dvi_ref[...]
        dblast = dkte * kte + jnp.where(rm == HG_CHUNK - 1, ddec_ref[...] * dec, 0.0)

        for d in range(HG_CHUNK):
            kd, bd, vd = _shifted(k, d, th), _shifted(b, d, th), _shifted(v, d, th)
            e = jnp.exp(jnp.where(rm >= d, b - bd, -1e30))
            p = q * kd * e
            sc = headsum(p, 1)
            dsc = headsum(dop * vd, 1)
            dv = dv + _unshift(sc * dop, d, th)
            dq = dq + dsc * kd * e
            dk = dk + _unshift(dsc * q * e, d, th)
            darg = dsc * p
            db = db + darg - _unshift(darg, d, th)

        db = db + jnp.where(rm == HG_CHUNK - 1, seg_cumsum(dblast), 0.0)
        dlf = seg_rcumsum(db)
        df = dlf / f - dk
        dlb_ref[...] += jnp.sum(df * (1.0 - sg), axis=0, keepdims=True)
        dfl = df * (1.0 - lb) * sg * (1.0 - sg)
        dqr = dq * (sq * (1.0 + qr * (1.0 - sq)))
        dproj_ref[...] = jnp.concatenate([dqr, dfl, dv, dg], axis=1).astype(dproj_ref.dtype)

    rev = lambda i: nt - 1 - i
    col = lambda j: pl.BlockSpec((th, HG_W), lambda i, j=j: (rev(i), j))
    vec = pl.BlockSpec((1, HG_W), lambda i: (0, 0))
    row = pl.BlockSpec((th, HG_W), lambda i: (rev(i), 0))
    tile_f32 = pltpu.VMEM((th, HG_W), F32)
    return pl.pallas_call(
        body, name=name, grid=(nt,),
        in_specs=[col(0), col(1), col(2), col(3), vec, vec, row,
                  pl.BlockSpec((nct, HG_W, HG_W), lambda i: (rev(i), 0, 0)), col(0)],
        out_specs=[pl.BlockSpec((th, 4 * HG_W), lambda i: (rev(i), 0)), vec, vec],
        out_shape=[jax.ShapeDtypeStruct((t_rows, 4 * HG_W), BF16), jax.ShapeDtypeStruct((1, HG_W), F32),
                   jax.ShapeDtypeStruct((1, HG_W), F32)],
        scratch_shapes=[pltpu.VMEM((HG_W, HG_W), F32), pltpu.VMEM((nct * HG_W, th), BF16),
                        pltpu.VMEM((nct * HG_W, HG_W), F32)] + [tile_f32] * 8,
        compiler_params=_cparams(("arbitrary",)),
    )(projp, projp, projp, projp, lb, ng, opre, states, dcat)


_INV_SQRT2 = 1.0 / math.sqrt(2.0)
_INV_SQRT2PI = 1.0 / math.sqrt(2.0 * math.pi)


def _gelu(x):
    return 0.5 * x * (1.0 + lax.erf(x * _INV_SQRT2))


def _gelu_grad(x):
    return 0.5 * (1.0 + lax.erf(x * _INV_SQRT2)) + x * jnp.exp(-0.5 * x * x) * _INV_SQRT2PI


def _sgu_parts(bu, bv, lg, lbias, w_ref, n_groups):
    c = SGU_CHUNK
    tril = (lax.broadcasted_iota(jnp.int32, (c, c), 0) >= lax.broadcasted_iota(jnp.int32, (c, c), 1)).astype(F32)
    gid = lax.broadcasted_iota(jnp.int32, bu.shape, 1) // HEAD
    u = _gelu(bu)
    gv = _gelu(bv)
    mu = jnp.mean(gv, axis=-1, keepdims=True)
    xc = gv - mu
    rstd = lax.rsqrt(jnp.mean(xc * xc, axis=-1, keepdims=True) + LN_EPS)
    xhat = xc * rstd
    vn = xhat * lg + lbias
    ws = [w_ref[gi] * tril for gi in range(n_groups)]
    return tril, gid, u, rstd, xhat, vn, ws


def _sgu_fwd(projp, lg, lbias, w_s, bias_full, *, name, job=None):
    t_rows = projp.shape[0]
    n_groups = w_s.shape[0]
    c = SGU_CHUNK

    def body(u_ref, v_ref, lg_ref, lb_ref, w_ref, bias_ref, o_ref):
        _, gid, u, _, _, vn, ws = _sgu_parts(u_ref[...], v_ref[...], lg_ref[...], lb_ref[...], w_ref, n_groups)
        vnb = vn.astype(BF16)
        z = bias_ref[...]
        for gi in range(n_groups):
            z = z + jnp.where(gid == gi, jnp.dot(ws[gi].astype(BF16), vnb, preferred_element_type=F32), 0.0)
        o_ref[...] = (u * z).astype(o_ref.dtype)

    col = lambda j: pl.BlockSpec((c, HG_W), lambda i, j=j: (i, j))
    return _call(
        body, (projp, projp, lg, lbias, w_s, bias_full), name=name, grid=(t_rows // c,),
        in_specs=[col(4), col(5), _const_spec(lg), _const_spec(lbias), _const_spec(w_s), _const_spec(bias_full)],
        out_specs=pl.BlockSpec((c, HG_W), lambda i: (i, 0)),
        out_shape=jax.ShapeDtypeStruct((t_rows, HG_W), BF16), sem=("arbitrary",), job=job)


def _sgu_bwd(projp, lg, lbias, w_s, bias_full, dcat, *, name):
    t_rows = projp.shape[0]
    n_groups = w_s.shape[0]
    c = SGU_CHUNK
    n = t_rows // c

    def body(u_ref, v_ref, lg_ref, lb_ref, w_ref, bias_ref, do_ref,
             dproj_ref, dlg_ref, dlb_ref, dw_ref, dbs_ref, dbias_acc):
        i = pl.program_id(0)

        @pl.when(i == 0)
        def _():
            dlg_ref[...] = jnp.zeros_like(dlg_ref)
            dlb_ref[...] = jnp.zeros_like(dlb_ref)
            dw_ref[...] = jnp.zeros_like(dw_ref)
            dbias_acc[...] = jnp.zeros_like(dbias_acc)

        bu, bv, lg_v = u_ref[...], v_ref[...], lg_ref[...]
        tril, gid, u, rstd, xhat, vn, ws = _sgu_parts(bu, bv, lg_v, lb_ref[...], w_ref, n_groups)
        vnb = vn.astype(BF16)
        z = bias_ref[...]
        for gi in range(n_groups):
            z = z + jnp.where(gid == gi, jnp.dot(ws[gi].astype(BF16), vnb, preferred_element_type=F32), 0.0)
        do = do_ref[...]
        dbu = do * z * _gelu_grad(bu)
        dz = do * u
        dbias_acc[...] += dz
        dvn = jnp.zeros_like(dz)
        for gi in range(n_groups):
            dzg = jnp.where(gid == gi, dz, 0.0).astype(BF16)
            dw_ref[gi] += lax.dot_general(dzg, vnb, (((1,), (1,)), ((), ())), preferred_element_type=F32) * tril
            dvn = dvn + jnp.dot(ws[gi].T.astype(BF16), dzg, preferred_element_type=F32)
        dlg_ref[...] += jnp.sum(dvn * xhat, axis=0, keepdims=True)
        dlb_ref[...] += jnp.sum(dvn, axis=0, keepdims=True)
        dxh = dvn * lg_v
        dgv = rstd * (dxh - jnp.mean(dxh, axis=-1, keepdims=True)
                      - xhat * jnp.mean(dxh * xhat, axis=-1, keepdims=True))
        dproj_ref[...] = jnp.concatenate([dbu, dgv * _gelu_grad(bv)], axis=1).astype(dproj_ref.dtype)

        @pl.when(i == n - 1)
        def _():
            dbs_ref[...] = jnp.sum(dbias_acc[...].T.reshape(n_groups, HEAD, c), axis=1)

    col = lambda j: pl.BlockSpec((c, HG_W), lambda i, j=j: (i, j))
    return pl.pallas_call(
        body, name=name, grid=(n,),
        in_specs=[col(4), col(5), _const_spec(lg), _const_spec(lbias), _const_spec(w_s), _const_spec(bias_full),
                  col(1)],
        out_specs=[pl.BlockSpec((c, 2 * HG_W), lambda i: (i, 0)), _const_spec(lg), _const_spec(lbias),
                   _const_spec(w_s), pl.BlockSpec((n_groups, c), lambda i: (0, 0))],
        out_shape=[jax.ShapeDtypeStruct((t_rows, 2 * HG_W), BF16), jax.ShapeDtypeStruct(lg.shape, F32),
                   jax.ShapeDtypeStruct(lbias.shape, F32), jax.ShapeDtypeStruct(w_s.shape, F32),
                   jax.ShapeDtypeStruct((n_groups, c), F32)],
        scratch_shapes=[pltpu.VMEM((c, HG_W), F32)],
        compiler_params=_cparams(("arbitrary",)),
    )(projp, projp, lg, lbias, w_s, bias_full, dcat)


def _rope_tables(positions):
    t = positions.shape[0]
    inv_freq = ROPE_THETA ** (-jnp.arange(0, 32, 2, dtype=F32) / 32)
    ang = positions.astype(F32)[:, None] * inv_freq
    cos, sin = jnp.cos(ang), jnp.sin(ang)
    z = lambda w: jnp.zeros((t, w), F32)
    cos_t = jnp.concatenate([jnp.ones((t, 64), F32), cos, cos, z(32)], axis=1)
    sin_up = jnp.concatenate([z(80), sin, z(32)], axis=1)
    sin_dn = jnp.concatenate([z(64), -sin, z(48)], axis=1)
    return cos_t, sin_up, sin_dn


def _rep(x, n):
    return x if n == 1 else jnp.concatenate([x] * n, axis=1)


def _rope(x, cos_t, sin_up, sin_dn):
    w = x.shape[1]
    return x * cos_t + pltpu.roll(x, 16, 1) * sin_up + pltpu.roll(x, w - 16, 1) * sin_dn


def _rope_t(dy, cos_t, sin_up, sin_dn):
    w = dy.shape[1]
    return dy * cos_t + pltpu.roll(dy * sin_up, w - 16, 1) + pltpu.roll(dy * sin_dn, 16, 1)


def _mla_prep(q, kv, projp, tables, *, name):
    nh = N_ATT_HEADS

    def fn(qv, kvv, kr, cos_t, sin_up, sin_dn):
        qr = _rope(qv, _rep(cos_t, nh), _rep(sin_up, nh), _rep(sin_dn, nh))
        krr = _rope(kr, cos_t, sin_up, sin_dn)
        lane = lax.broadcasted_iota(jnp.int32, kvv.shape, 1) % LANES
        return qr, jnp.where(lane < HEAD, kvv, 0.0) + _rep(krr, nh), kvv

    w = q.shape[1]
    return _rowwise(fn, [q, kv, (projp, LANES, P_KR // LANES)] + list(tables), [],
                    [(w, BF16), (w, BF16), (w, BF16)], name=name)


def _mla_prep_bwd(dqr, dkf, tables, *, name):
    nh = N_ATT_HEADS

    def fn(dq, dk, cos_t, sin_up, sin_dn):
        dqp = _rope_t(dq, _rep(cos_t, nh), _rep(sin_up, nh), _rep(sin_dn, nh))
        dkrr = dk[:, 0:LANES]
        for h in range(1, nh):
            dkrr = dkrr + dk[:, LANES * h:LANES * (h + 1)]
        return dqp, _rope_t(dkrr, cos_t, sin_up, sin_dn)

    return _rowwise(fn, [dqr, dkf] + list(tables), [], [(dqr.shape[1], BF16), (LANES, BF16)], name=name)


_LOG2E = 1.0 / math.log(2.0)
_NT = (((1,), (1,)), ((), ()))
_TN = (((0,), (0,)), ((), ()))


def _attn_fwd(qr, kf, kvb, *, name, job=None):
    t_rows = qr.shape[0]
    tq = min(ATT_TQ, t_rows)
    nb = t_rows // tq
    scale = ATT_D ** -0.5

    c2 = scale * _LOG2E

    def body(q_ref, kf_ref, kvb_ref, o_ref, lse_ref):
        qi = pl.program_id(1)
        lane = lax.broadcasted_iota(jnp.int32, (tq, LANES), 1)
        causal_t = (lax.broadcasted_iota(jnp.int32, (tq, tq), 0) <= lax.broadcasted_iota(jnp.int32, (tq, tq), 1))
        heads = [slice(hh * LANES, (hh + 1) * LANES) for hh in range(2)]
        qs = [q_ref[:, cols] for cols in heads]

        def block(first, n_keys, carry, diagonal):
            rows = pl.ds(pl.multiple_of(first * tq, tq), n_keys)
            new = []
            for q, cols, (m_old, l_old, acc_t) in zip(qs, heads, carry):
                s_t = lax.dot_general(kf_ref[rows, cols], q, _NT, preferred_element_type=F32)
                if diagonal:
                    s_t = jnp.where(causal_t, s_t, -1e30)
                m_new = jnp.maximum(m_old, jnp.max(s_t, axis=0, keepdims=True))
                p_t = jnp.exp2((s_t - m_new) * c2)
                a = jnp.exp2((m_old - m_new) * c2)
                pv_t = lax.dot_general(kvb_ref[rows, cols], p_t.astype(BF16), _TN, preferred_element_type=F32)
                new.append((m_new, a * l_old + jnp.sum(p_t, axis=0, keepdims=True), a * acc_t + pv_t))
            return tuple(new)

        init = (jnp.full((1, tq), -1e30, F32), jnp.zeros((1, tq), F32), jnp.zeros((LANES, tq), F32))
        carry = lax.fori_loop(0, qi // 4, lambda g, c: block(4 * g, 4 * tq, c, False), (init, init))
        carry = lax.cond((qi // 2) % 2 == 1, lambda c: block(4 * (qi // 4), 2 * tq, c, False), lambda c: c, carry)
        carry = lax.cond(qi % 2 == 1, lambda c: block(qi - 1, tq, c, False), lambda c: c, carry)
        outs = []
        for hh, (m_fin, l_fin, acc_t) in enumerate(block(qi, tq, carry, True)):
            lse_ref[hh] = m_fin * scale + jnp.log(l_fin)
            outs.append((acc_t / l_fin).T)
        o_ref[...] = jnp.where(lane < HEAD, pltpu.roll(outs[0], HEAD, 1), outs[1])

    pair = pl.BlockSpec((t_rows, 2 * LANES), lambda pr, qi: (0, pr))
    return _call(
        body, (qr, kf, kvb), name=name, grid=(N_ATT_HEADS // 2, nb),
        in_specs=[pl.BlockSpec((tq, 2 * LANES), lambda pr, qi: (qi, pr)), pair, pair],
        out_specs=[pl.BlockSpec((tq, LANES), lambda pr, qi: (qi, pr)),
                   pl.BlockSpec((2, 1, tq), lambda pr, qi: (pr, 0, qi))],
        out_shape=[jax.ShapeDtypeStruct((t_rows, N_ATT_HEADS * HEAD), F32),
                   jax.ShapeDtypeStruct((N_ATT_HEADS, 1, t_rows), F32)],
        sem=("parallel", "arbitrary"), job=job)


def _attn_bwd(qr, kf, kvb, dcat, o, lse, *, name, job=None):
    t_rows = qr.shape[0]
    tq = min(ATT_TQ, t_rows)
    nb = t_rows // tq
    scale = ATT_D ** -0.5
    c2 = scale * _LOG2E
    do_off = 2 * HG_W // LANES

    def body(q_ref, kf_ref, kvb_ref, do_ref, o_ref, lse_ref, dq_ref, dkv_ref, dk_ref):
        ki = pl.program_id(1)

        @pl.when(ki == 0)
        def _():
            dq_ref[...] = jnp.zeros_like(dq_ref)

        lane = lax.broadcasted_iota(jnp.int32, (tq, LANES), 1)
        causal_t = (lax.broadcasted_iota(jnp.int32, (tq, tq), 0) <= lax.broadcasted_iota(jnp.int32, (tq, tq), 1))
        heads = [slice(hh * LANES, (hh + 1) * LANES) for hh in range(2)]
        ks = [kf_ref[:, cols] for cols in heads]
        vs = [kvb_ref[:, cols] for cols in heads]

        def block(qi, n_q, carry, diagonal):
            rows = pl.ds(pl.multiple_of(qi * tq, tq), n_q)
            do_pair, o_pair = do_ref[rows, :], o_ref[rows, :]
            upper = lax.broadcasted_iota(jnp.int32, do_pair.shape, 1) >= HEAD
            new = []
            for hh, (cols, k, v, (dk, dv)) in enumerate(zip(heads, ks, vs, carry)):
                q = q_ref[rows, cols]
                do, ov = (pltpu.roll(do_pair, HEAD, 1), pltpu.roll(o_pair, HEAD, 1)) if hh == 0 else (do_pair, o_pair)
                do = jnp.where(upper, do, 0.0)
                delta = jnp.sum((do * ov).T, axis=0, keepdims=True)
                s_t = lax.dot_general(k, q, _NT, preferred_element_type=F32)
                if diagonal:
                    s_t = jnp.where(causal_t, s_t, -1e30)
                p_t = jnp.exp2(s_t * c2 - lse_ref[hh, :, rows] * _LOG2E)
                dob = do.astype(BF16)
                dv = dv + jnp.dot(p_t.astype(BF16), dob, preferred_element_type=F32)
                dp_t = lax.dot_general(v, dob, _NT, preferred_element_type=F32)
                ds_t = (p_t * (dp_t - delta) * scale).astype(BF16)
                dk = dk + jnp.dot(ds_t, q, preferred_element_type=F32)
                dq_ref[rows, cols] += lax.dot_general(ds_t, k, _TN, preferred_element_type=F32)
                new.append((dk, dv))
            return tuple(new)

        zero = jnp.zeros((tq, LANES), F32)
        carry = block(ki, tq, ((zero, zero), (zero, zero)), True)
        rest = nb - 1 - ki
        carry = lax.fori_loop(0, rest // 2, lambda g, c: block(ki + 1 + 2 * g, 2 * tq, c, False), carry)
        carry = lax.cond(rest % 2 == 1, lambda c: block(nb - 1, tq, c, False), lambda c: c, carry)
        dkv_ref[...] = jnp.concatenate([jnp.where(lane < HEAD, dk, dv) for dk, dv in carry],
                                       axis=1).astype(dkv_ref.dtype)
        dk_ref[...] = jnp.concatenate([dk for dk, _ in carry], axis=1)

    pair_all = pl.BlockSpec((t_rows, 2 * LANES), lambda pr, ki: (0, pr))
    pair_blk = pl.BlockSpec((tq, 2 * LANES), lambda pr, ki: (ki, pr))
    wide = jax.ShapeDtypeStruct((t_rows, N_ATT_HEADS * LANES), F32)
    return _call(
        body, (qr, kf, kvb, dcat, o, lse), name=name, grid=(N_ATT_HEADS // 2, nb),
        in_specs=[pair_all, pair_blk, pair_blk,
                  pl.BlockSpec((t_rows, LANES), lambda pr, ki: (0, do_off + pr)),
                  pl.BlockSpec((t_rows, LANES), lambda pr, ki: (0, pr)),
                  pl.BlockSpec((2, 1, t_rows), lambda pr, ki: (pr, 0, 0))],
        out_specs=[pair_all, pair_blk, pair_blk],
        out_shape=[wide, jax.ShapeDtypeStruct(wide.shape, BF16), wide],
        sem=("parallel", "arbitrary"), job=job)


def _my_pos():
    return lax.axis_index("x"), lax.axis_index("y"), lax.axis_index("c")


def _all_gather(xs, *, name, columns=True):
    return _gather_forward(_run_job(_gather_job(xs, columns), name=name), name=name + "_forward")


def _remote(src, dst, send_sems, recv_sems, k, dev):
    return pltpu.make_async_remote_copy(src_ref=src, dst_ref=dst, send_sem=send_sems.at[k], recv_sem=recv_sems.at[k],
                                        device_id=dev, device_id_type=MESH)


def _block(ref, idx):
    if len(ref.shape) == 2:
        return ref.at[:, pl.ds(pl.multiple_of(idx * LANES, LANES), LANES)]
    return ref.at[idx]


def _gather_job(xs, columns=True):
    n = len(xs)

    def make(x_refs, out_refs, send_sems, recv_sems, local_sems):
        mx, my, mc = _my_pos()
        mine = 4 * mx + 2 * my + mc
        peers = [(mx, my, 1 - mc), (1 - mx, my, mc), (mx, 1 - my, mc), (1 - mx, 1 - my, mc)]
        sends, recvs, local = [], [], []
        for a in range(n):
            local.append(pltpu.make_async_copy(x_refs[a], _block(out_refs[a], mine), local_sems.at[a]))
            for k, dev in enumerate(peers):
                theirs = 4 * dev[0] + 2 * dev[1] + dev[2]
                sends.append(_remote(x_refs[a], _block(out_refs[a], mine), send_sems, recv_sems, 4 * a + k, dev))
                recvs.append(_remote(x_refs[a], _block(out_refs[a], theirs), send_sems, recv_sems, 4 * a + k, dev))
        return sends, recvs, local

    def gathered(x):
        if columns and x.ndim == 2 and x.shape[1] == LANES:
            return jax.ShapeDtypeStruct((x.shape[0], N_DEV * LANES), x.dtype)
        return jax.ShapeDtypeStruct((N_DEV,) + x.shape, x.dtype)

    return _copies_job(xs, [gathered(x) for x in xs], 4 * n, n, make)


def _forward_job(gs):
    n = len(gs)

    def make(in_refs, out_refs, send_sems, recv_sems, local_sems):
        mx, my, mc = _my_pos()
        chips = [(1 - mx, my), (mx, 1 - my), (1 - mx, 1 - my)]
        sends, recvs = [], []
        for a in range(n):
            for j, (cx, cy) in enumerate(chips):
                here = _block(out_refs[a], 4 * cx + 2 * cy + mc)
                there = _block(out_refs[a], 4 * cx + 2 * cy + 1 - mc)
                sends.append(_remote(here, here, send_sems, recv_sems, 3 * a + j, (mx, my, 1 - mc)))
                recvs.append(_remote(here, there, send_sems, recv_sems, 3 * a + j, (mx, my, 1 - mc)))
        return sends, recvs, []

    shapes = [jax.ShapeDtypeStruct(g.shape, g.dtype) for g in gs]
    return _copies_job(gs, shapes, 3 * n, 0, make, in_place=True)


def _gather_forward(gs, *, name):
    return _run_job(_forward_job(gs), name=name)


def _pair_job(xs):
    n = len(xs)

    def make(x_refs, out_refs, send_sems, recv_sems, local_sems):
        mx, my, mc = _my_pos()

        def src(ref, g):
            return _block(ref, 2 * g + 1 - mc) if len(ref.shape) == 2 else ref.at[g, 1 - mc]

        copies = [_remote(src(x_refs[a], g), out_refs[a].at[g], send_sems, recv_sems, 4 * a + g, (mx, my, 1 - mc))
                  for a in range(n) for g in range(4)]
        return copies, copies, []

    shapes = [jax.ShapeDtypeStruct((4, x.shape[0], LANES) if x.ndim == 2 else (4,) + x.shape[2:], x.dtype)
              for x in xs]
    return _copies_job(xs, shapes, 4 * n, 0, make)


def _pair_add(x, r, core, *, name):
    _, a, b = r.shape
    ta = _row_tile(a, 256)

    def body(c_ref, x_ref, r_ref, o_ref):
        o_ref[...] = (x_ref[...] + r_ref[...]).astype(o_ref.dtype)

    blk = pl.BlockSpec((None, ta, b), lambda g, i, c_ref: (g, i, 0))
    own = (pl.BlockSpec((ta, b), lambda g, i, c_ref: (i, 2 * g + c_ref[0])) if x.ndim == 2
           else pl.BlockSpec((None, None, ta, b), lambda g, i, c_ref: (g, c_ref[0], i, 0)))
    return pl.pallas_call(
        body, name=name,
        grid_spec=pltpu.PrefetchScalarGridSpec(
            num_scalar_prefetch=1, grid=(4, a // ta), in_specs=[own, blk], out_specs=blk),
        out_shape=jax.ShapeDtypeStruct((4, a, b), BF16),
        compiler_params=_cparams(("parallel", "parallel")),
    )(core, x, r)


def _quad_job(xs):
    n = len(xs)

    def make(x_refs, out_refs, send_sems, recv_sems, local_sems):
        mx, my, mc = _my_pos()
        mine = 2 * mx + my
        peers = [((1 - mx, my, mc), 2 * (1 - mx) + my), ((mx, 1 - my, mc), 2 * mx + 1 - my),
                 ((1 - mx, 1 - my, mc), 2 * (1 - mx) + 1 - my)]
        sends, recvs, local = [], [], []
        for a in range(n):
            local.append(pltpu.make_async_copy(x_refs[a].at[mine], out_refs[a].at[mine], local_sems.at[a]))
            for k, (dev, g) in enumerate(peers):
                sends.append(_remote(x_refs[a].at[g], out_refs[a].at[mine], send_sems, recv_sems, 3 * a + k, dev))
                recvs.append(_remote(x_refs[a].at[g], out_refs[a].at[g], send_sems, recv_sems, 3 * a + k, dev))
        return sends, recvs, local

    shapes = [jax.ShapeDtypeStruct(x.shape, x.dtype) for x in xs]
    return _copies_job(xs, shapes, 3 * n, n, make)


def _row_tile(r, pref):
    t = min(pref, r)
    while r % t or (t % 8 and t != r):
        t -= 1
    return t


def _adamw(parts, w, m, v, layer, *, name, tile=256, into=None):
    g, a, b = parts.shape
    tile = _row_tile(a, tile)
    c1 = 1.0 / (1.0 - ADAM_B1 ** ADAM_STEP)
    c2 = 1.0 / (1.0 - ADAM_B2 ** ADAM_STEP)
    into = tuple(into or ())

    def body(p_ref, w_ref, m_ref, v_ref, *refs):
        g_ref, d_ref, mo_ref, vo_ref = refs[len(into):]
        grad = p_ref[0].astype(F32)
        for j in range(1, g):
            grad = grad + p_ref[j].astype(F32)
        mn = ADAM_B1 * m_ref[...] + (1.0 - ADAM_B1) * grad
        vn = ADAM_B2 * v_ref[...] + (1.0 - ADAM_B2) * (grad * grad)
        g_ref[...] = grad
        mo_ref[...] = mn
        vo_ref[...] = vn
        d_ref[...] = -ADAM_LR * ((mn * c1) / (jnp.sqrt(vn * c2) + ADAM_EPS) + ADAM_WD * w_ref[...])

    if layer is None:
        src, shape = pl.BlockSpec((tile, b), lambda i: (i, 0)), (a, b)
    else:
        src, shape = pl.BlockSpec((None, tile, b), lambda i: (layer, i, 0)), w.shape
    return pl.pallas_call(
        body, name=name, grid=(a // tile,),
        in_specs=[pl.BlockSpec((g, tile, b), lambda i: (0, i, 0)), src, src, src] + [_ANY] * len(into),
        out_specs=[src] * 4,
        out_shape=[jax.ShapeDtypeStruct(shape, F32)] * 4,
        input_output_aliases={4 + i: i for i in range(len(into))},
        compiler_params=_cparams(("parallel",)),
    )(parts, w, m, v, *into)


W_IN_SHARD = 276


def _w_in_dest(col):
    return jnp.where(col < P_KR, col, jnp.where(col < P_KR + 256, col + (P_CKV - P_KR), col - 2176 + P_KR + HEAD))


PLACE_TILE = 384
PLACE_SHARDS = 3
PICK_TILE = 128
PICK_TILES = 4


def _w_in_tables():
    col = np.arange(N_DEV * W_IN_SHARD)
    dest = np.where(col < P_KR, col, np.where(col < P_KR + 256, col + (P_CKV - P_KR), col - 2176 + P_KR + HEAD))
    shard = col // W_IN_SHARD

    def filled(used, universe, n):
        used = sorted(set(int(u) for u in used))
        assert len(used) <= n, used
        return used + [u for u in universe if u not in used][:n - len(used)]

    place = [filled(shard[dest // PLACE_TILE == c], range(N_DEV), PLACE_SHARDS) for c in range(P_COLS // PLACE_TILE)]
    pick = [filled(dest[shard == j] // PICK_TILE, range(P_COLS // PICK_TILE), PICK_TILES) for j in range(N_DEV)]
    return np.asarray(place, np.int32).reshape(-1), np.asarray(pick, np.int32).reshape(-1)


def _place_w_in(g, *, name):
    _, d, sh = g.shape
    tc, ns = PLACE_TILE, PLACE_SHARDS
    table = jnp.asarray(_w_in_tables()[0])

    def body(tab_ref, g_ref, o_ref, acc_ref):
        ct, s = pl.program_id(0), pl.program_id(1)
        j = tab_ref[ct * ns + s]

        @pl.when(s == 0)
        def _():
            acc_ref[...] = jnp.zeros_like(acc_ref)

        src = j * sh + lax.broadcasted_iota(jnp.int32, (sh, tc), 0)
        dst = ct * tc + lax.broadcasted_iota(jnp.int32, (sh, tc), 1)
        place = (_w_in_dest(src) == dst).astype(BF16)
        acc_ref[...] += jnp.dot(g_ref[...], place, preferred_element_type=F32)

        @pl.when(s == ns - 1)
        def _():
            o_ref[...] = acc_ref[...].astype(o_ref.dtype)

    return pl.pallas_call(
        body, name=name,
        grid_spec=pltpu.PrefetchScalarGridSpec(
            num_scalar_prefetch=1, grid=(P_COLS // tc, ns),
            in_specs=[pl.BlockSpec((None, d, sh), lambda ct, s, tab: (tab[ct * ns + s], 0, 0))],
            out_specs=pl.BlockSpec((d, tc), lambda ct, s, tab: (0, ct)),
            scratch_shapes=[pltpu.VMEM((d, tc), F32)]),
        out_shape=jax.ShapeDtypeStruct((d, P_COLS), BF16),
        compiler_params=_cparams(("parallel", "arbitrary")),
    )(table, g)


def _unplace_w_in(dw, *, name):
    d = dw.shape[0]
    sh, tk, nt = W_IN_SHARD, PICK_TILE, PICK_TILES
    table = jnp.asarray(_w_in_tables()[1])

    def body(tab_ref, dw_ref, o_ref):
        j, kk = pl.program_id(0), pl.program_id(1)
        tile = tab_ref[j * nt + kk]
        src = j * sh + lax.broadcasted_iota(jnp.int32, (tk, sh), 1)
        dst = tile * tk + lax.broadcasted_iota(jnp.int32, (tk, sh), 0)
        pick = (_w_in_dest(src) == dst).astype(BF16)
        part = _split_dot(dw_ref[...], pick)

        @pl.when(kk == 0)
        def _():
            o_ref[...] = part

        @pl.when(kk > 0)
        def _():
            o_ref[...] += part

    return pl.pallas_call(
        body, name=name,
        grid_spec=pltpu.PrefetchScalarGridSpec(
            num_scalar_prefetch=1, grid=(N_DEV, nt),
            in_specs=[pl.BlockSpec((d, tk), lambda j, kk, tab: (0, tab[j * nt + kk]))],
            out_specs=pl.BlockSpec((None, d, sh), lambda j, kk, tab: (j, 0, 0))),
        out_shape=jax.ShapeDtypeStruct((N_DEV, d, sh), F32),
        compiler_params=_cparams(("parallel", "arbitrary")),
    )(table, dw)


def _gate_up_swiglu(h1, wgu, *, name):
    t_rows, k = h1.shape
    w = wgu.shape[2]
    tm = _tile(t_rows, 1024)

    def body(a_ref, wg_ref, wu_ref, gu_ref, act_ref):
        a = a_ref[...].astype(BF16)
        gate = jnp.dot(a, wg_ref[...], preferred_element_type=F32)
        up = jnp.dot(a, wu_ref[...], preferred_element_type=F32)
        gu_ref[0] = gate.astype(gu_ref.dtype)
        gu_ref[1] = up.astype(gu_ref.dtype)
        act_ref[...] = (gate * _sigmoid(gate) * up).astype(act_ref.dtype)

    return pl.pallas_call(
        body, name=name, grid=(t_rows // tm, 4),
        in_specs=[pl.BlockSpec((tm, k), lambda i, j: (i, 0)),
                  pl.BlockSpec((None, k, w), lambda i, j: (j, 0, 0)),
                  pl.BlockSpec((None, k, w), lambda i, j: (j + 4, 0, 0))],
        out_specs=[pl.BlockSpec((2, None, tm, w), lambda i, j: (0, j, i, 0)),
                   pl.BlockSpec((None, tm, w), lambda i, j: (j, i, 0))],
        out_shape=[jax.ShapeDtypeStruct((2, 4, t_rows, w), BF16), jax.ShapeDtypeStruct((4, t_rows, w), BF16)],
        compiler_params=_cparams(("parallel", "arbitrary")),
    )(h1, wgu, wgu)


def _down_dx_swiglu(dffn, wdown, gu, *, name):
    t_rows, k = dffn.shape
    w = gu.shape[3]
    tm = _tile(t_rows, 1024)

    def body(d_ref, w_ref, gu_ref, o_ref):
        dact = lax.dot_general(d_ref[...].astype(BF16), w_ref[...], _NT, preferred_element_type=F32)
        gate, up = gu_ref[0].astype(F32), gu_ref[1].astype(F32)
        sg = _sigmoid(gate)
        o_ref[0] = (dact * up * (sg * (1.0 + gate * (1.0 - sg)))).astype(o_ref.dtype)
        o_ref[1] = (dact * gate * sg).astype(o_ref.dtype)

    blk = pl.BlockSpec((2, None, tm, w), lambda i, j: (0, j, i, 0))
    return pl.pallas_call(
        body, name=name, grid=(t_rows // tm, 4),
        in_specs=[pl.BlockSpec((tm, k), lambda i, j: (i, 0)), pl.BlockSpec((w, k), lambda i, j: (j, 0)), blk],
        out_specs=blk, out_shape=jax.ShapeDtypeStruct(gu.shape, BF16),
        compiler_params=_cparams(("parallel", "arbitrary")),
    )(dffn, wdown, gu)


BIG = ("w_in", "mla_w_uq", "mla_w_ukv", "w_out", "w_gate_up", "w_down", "ple_w_gate", "ple_w_proj")
SMALL = ("ln_in_g", "ln_in_b", "hgrn_lb_logits", "hgrn_norm_g", "sgu_ln_g", "sgu_ln_b", "sgu_w_s", "sgu_b_s",
         "mla_q_norm_g", "mla_kv_norm_g", "ln1_g", "ln1_b", "ln2_g", "ln2_b")
ORDER = ("ln_in_g", "ln_in_b", "w_in", "hgrn_lb_logits", "hgrn_norm_g", "sgu_ln_g", "sgu_ln_b", "sgu_w_s", "sgu_b_s",
         "mla_q_norm_g", "mla_w_uq", "mla_kv_norm_g", "mla_w_ukv", "w_out", "ln1_g", "ln1_b", "w_gate_up", "w_down",
         "ple_w_gate", "ple_w_proj", "ln2_g", "ln2_b")


def _slab(a, align):
    s = a.reshape(-1, LANES)
    pad = -s.shape[0] % align
    return jnp.pad(s, ((0, pad), (0, 0))) if pad else s


def _pack(arrays, align=16, total_align=512):
    s = jnp.concatenate([_slab(a, align) for a in arrays], axis=0)
    pad = -s.shape[0] % total_align
    return jnp.pad(s, ((0, pad), (0, 0))) if pad else s


def _unpack(slab, shapes, align=16):
    out, r0 = [], 0
    for s in shapes:
        nr = math.prod(s) // LANES
        out.append(slab[r0:r0 + nr].reshape(s))
        r0 += nr + (-nr % align)
    return out


def _weight_shards(w, li):
    uq_pad = ((0, 0), (0, LANES - ATT_D))
    shards = {k: w[k][li] for k in BIG}
    shards["mla_w_uq"] = jnp.pad(shards["mla_w_uq"], uq_pad)
    return {k: s.astype(BF16) for k, s in shards.items()}


def _usable_weights(g, *, name):
    out = {}
    for k, a in g.items():
        if k == "w_in":
            out[k] = _place_w_in(a, name=name + "_place_w_in")
        elif k in ("w_out", "w_down", "ple_w_gate"):
            out[k] = a.reshape(a.shape[0] * a.shape[1], a.shape[2])
        else:
            out[k] = a
    return out


BY_COLUMNS = ("mla_w_uq", "mla_w_ukv", "ple_w_proj")


def _as_pairs(k, g):
    if k in BY_COLUMNS:
        return g
    if g.ndim == 2:
        return g.reshape((4, 2, g.shape[0] // N_DEV) + g.shape[1:])
    return g.reshape((4, 2) + g.shape[1:])


def _twice(fn):
    return lambda *a: fn(*a) * 2


def _layer_forward(li, h, hb, p_i, wts, sm, lbs, tables, alpha, hgrn_job=None, after_hgrn=None, attn_job=None,
                   after_attn=None, loss_target=None):
    n = f"l{li}_"
    row1 = lambda a: a.reshape(1, -1)
    projp = _mm(hb, wts["w_in"], name=n + "proj")
    ng = row1(sm["hgrn_norm_g"][li])
    res = _hgrn_fwd(projp, lbs[li], ng, name=n + "hgrn_fwd", job=hgrn_job)
    if hgrn_job is not None:
        res, got = res
    o_a, o_pre, states = res
    lg, lbias = row1(sm["sgu_ln_g"][li]), row1(sm["sgu_ln_b"][li])
    w_s = sm["sgu_w_s"][li]
    bias_full = jnp.repeat(sm["sgu_b_s"][li].T, HEAD, axis=1)
    o_b = _sgu_fwd(projp, lg, lbias, w_s, bias_full, name=n + "sgu_fwd",
                   job=None if hgrn_job is None else _forward_job(got))
    if hgrn_job is not None:
        o_b, got = o_b
        wts = dict(wts, **after_hgrn(got))
    qg, kvg = row1(sm["mla_q_norm_g"][li]), row1(sm["mla_kv_norm_g"][li])
    cq_view, ckv_view = (projp, 384, P_CQ // 384), (projp, 256, P_CKV // 256)
    (cqn,) = _rowwise(_fn_rms, [cq_view], [qg], [(384, BF16)], name=n + "q_norm")
    (ckvn,) = _rowwise(_fn_rms, [ckv_view], [kvg], [(256, BF16)], name=n + "kv_norm")
    q = _mm(cqn, wts["mla_w_uq"], name=n + "uq")
    kv = _mm(ckvn, wts["mla_w_ukv"], name=n + "ukv")
    qr, kf, kvb = _mla_prep(q, kv, projp, tables, name=n + "mla_prep")
    res = _attn_fwd(qr, kf, kvb, name=n + "attn_fwd", job=attn_job)
    if attn_job is not None:
        res, got = res
    o_c, lse = res
    cat = jnp.concatenate([o_a, o_b, o_c.astype(BF16)], axis=1)
    mix = _mm(cat, wts["w_out"], name=n + "out_proj", job=None if attn_job is None else _forward_job(got))
    if attn_job is not None:
        mix, got = mix
        wts = dict(wts, **after_attn(got))
    g1, b1 = row1(sm["ln1_g"][li]), row1(sm["ln1_b"][li])
    d = h.shape[1]
    h1, h1b = _rowwise(_twice(_make_post_mix(alpha)), [h, mix], [g1, b1], [(d, F32), (d, BF16)], name=n + "ln1")
    gu, act = _gate_up_swiglu(h1b, wts["w_gate_up"], name=n + "gate_up")
    ffn = _mm(act, wts["w_down"], am="bmk", tm=2048, name=n + "down")
    pg = _mm(h1b, wts["ple_w_gate"], name=n + "ple_gate")
    pp = _mm(p_i, wts["ple_w_proj"], name=n + "ple_proj")
    g2, b2 = row1(sm["ln2_g"][li]), row1(sm["ln2_b"][li])
    if loss_target is None:
        out = _rowwise(_twice(_make_ple_ln(alpha)), [h1, ffn, pg, pp], [g2, b2], [(d, F32), (d, BF16)],
                       name=n + "ln2")
    else:
        def ln_and_loss(h1v, ffnv, pgv, ppv, tv, gv, bv):
            err = _make_ple_ln(alpha)(h1v, ffnv, pgv, ppv, gv, bv)[0] - tv
            return err * (1.0 / d), 0.5 * jnp.sum(jnp.mean(err * err, axis=-1, keepdims=True), axis=0, keepdims=True)

        out = _rowwise(ln_and_loss, [h1, ffn, pg, pp, loss_target], [g2, b2], [(d, F32)], accs=[(1, 1)],
                       name=n + "ln2_loss")
    saved = dict(h=h, hb=hb, h1b=h1b, projp=projp, o_pre=o_pre, states=states, cqn=cqn, ckvn=ckvn, qr=qr, kf=kf, kvb=kvb, o_c=o_c,
                 lse=lse, cat=cat, mix=mix, h1=h1, gu=gu, act=act, ffn=ffn, pg=pg, pp=pp, ng=ng, lg=lg, wts=wts,
                 lbias=lbias, w_s=w_s, bias_full=bias_full, qg=qg, kvg=kvg, g1=g1, b1=b1, g2=g2, b2=b2)
    return tuple(out), saved


RS_EARLY = ("ple_w_proj", "ple_w_gate", "w_down", "w_gate_up", "w_out")
RS_LATE = ("mla_w_uq", "mla_w_ukv", "w_in")


def _layer_backward(li, dh2_parts, p_i, sv, lbs, tables, alpha, core, carried=None):
    n = f"l{li}_b_"
    wts = sv["wts"]
    gr = {}
    dh1_a, dffn, dpg, dpp, gr["ln2_g"], gr["ln2_b"] = _rowwise_vjp(
        _make_ple_ln(alpha), [sv["h1"], sv["ffn"], sv["pg"], sv["pp"]], [sv["g2"], sv["b2"]], [dh2_parts],
        groups=[[0], [1], [2], [3]], gdtypes=[F32, BF16, BF16, BF16], name=n + "ln2")
    big = {}
    big["ple_w_proj"] = _mm(p_i, dpp, am="km", tk=2048, name=n + "ple_proj_dw")
    big["ple_w_gate"] = _mm(sv["h1b"], dpg, am="km", name=n + "ple_gate_dw")
    dh1_b = _mm(dpg, wts["ple_w_gate"], bm="nk", name=n + "ple_gate_dx")
    big["w_down"] = _mm(sv["act"], dffn, am="bkm", tk=2048, name=n + "down_dw")
    dgu = _down_dx_swiglu(dffn, wts["w_down"], sv["gu"], name=n + "down_dx")
    dgu = dgu.reshape((N_DEV,) + dgu.shape[2:])
    big["w_gate_up"], carried_got = _mm(sv["h1b"], dgu, am="km", bm="bkn", om="bmn", tk=2048, name=n + "gate_up_dw",
                                        job=carried), None
    if carried is not None:
        big["w_gate_up"], carried_got = big["w_gate_up"]
    early = [_as_pairs(k, big[k]) for k in RS_EARLY[:-1]]
    dh1_c, theirs = _mm(dgu, wts["w_gate_up"], am="bmk", bm="bnk", tm=2048, name=n + "gate_up_dx",
                        job=_pair_job(early))
    dh_a, dmix, gr["ln1_g"], gr["ln1_b"] = _rowwise_vjp(
        _make_post_mix(alpha), [sv["h"], sv["mix"]], [sv["g1"], sv["b1"]], [[dh1_a, dh1_b, dh1_c]],
        groups=[[0], [1]], gdtypes=[F32, BF16], name=n + "ln1")
    big["w_out"] = _mm(sv["cat"], dmix, am="km", name=n + "out_proj_dw")
    early.append(_as_pairs("w_out", big["w_out"]))
    dcat, their_w_out = _mm(dmix, wts["w_out"], bm="nk", name=n + "out_proj_dx", job=_pair_job(early[-1:]))
    sums = [_pair_add(x, r, core, name=n + "pair_add_" + k)
            for k, x, r in zip(RS_EARLY, early, list(theirs) + list(their_w_out))]

    (dqr, dkv, dkf), early_quads = _attn_bwd(sv["qr"], sv["kf"], sv["kvb"], dcat, sv["o_c"], sv["lse"],
                                             name=n + "attn", job=_quad_job(sums))
    dqpad, dkr = _mla_prep_bwd(dqr, dkf, tables, name=n + "mla_prep")
    big["mla_w_uq"] = _mm(sv["cqn"], dqpad, am="km", tk=2048, name=n + "uq_dw")
    dcqn = _mm(dqpad, wts["mla_w_uq"], bm="nk", name=n + "uq_dx")
    big["mla_w_ukv"] = _mm(sv["ckvn"], dkv, am="km", tk=2048, name=n + "ukv_dw")
    dckvn = _mm(dkv, wts["mla_w_ukv"], bm="nk", name=n + "ukv_dx")
    projp = sv["projp"]
    dcq, gr["mla_q_norm_g"] = _rowwise_vjp(_fn_rms, [(projp, 384, P_CQ // 384)], [sv["qg"]], [[dcqn]],
                                           groups=[[0]], gdtypes=[BF16], name=n + "q_norm")
    dckv, gr["mla_kv_norm_g"] = _rowwise_vjp(_fn_rms, [(projp, 256, P_CKV // 256)], [sv["kvg"]], [[dckvn]],
                                             groups=[[0]], gdtypes=[BF16], name=n + "kv_norm")
    dsgu, gr["sgu_ln_g"], gr["sgu_ln_b"], gr["sgu_w_s"], gr["sgu_b_s"] = _sgu_bwd(
        projp, sv["lg"], sv["lbias"], sv["w_s"], sv["bias_full"], dcat, name=n + "sgu")
    dhg, gr["hgrn_norm_g"], gr["lower_bound"] = _hgrn_bwd(
        projp, lbs[li], sv["ng"], sv["o_pre"], sv["states"], dcat, name=n + "hgrn")
    dprojp = jnp.concatenate([dhg, dsgu, dcq, dkr, dckv], axis=1)
    big["w_in"] = _unplace_w_in(_mm(sv["hb"], dprojp, am="km", tk=2048, name=n + "proj_dw"),
                                name=n + "proj_dw_shards")
    late = [_as_pairs(k, big[k]) for k in RS_LATE]
    dh_b, theirs = _mm(dprojp, wts["w_in"], bm="nk", name=n + "proj_dx", job=_pair_job(late))
    late_sums = [_pair_add(x, r, core, name=n + "pair_add_" + k) for k, x, r in zip(RS_LATE, late, theirs)]
    return [dh_a, dh_b], gr, early_quads, late_sums, carried_got


def kernel(x, p, positions, ln_in_g, ln_in_b, w_in, hgrn_lb_logits, hgrn_norm_g, sgu_ln_g, sgu_ln_b, sgu_w_s, sgu_b_s, mla_q_norm_g, mla_w_uq, mla_kv_norm_g, mla_w_ukv, w_out, ln1_g, ln1_b, w_gate_up, w_down, ple_w_gate, ple_w_proj, ln2_g, ln2_b, loss_target, m_ln_in_g, m_ln_in_b, m_w_in, m_hgrn_lb_logits, m_hgrn_norm_g, m_sgu_ln_g, m_sgu_ln_b, m_sgu_w_s, m_sgu_b_s, m_mla_q_norm_g, m_mla_w_uq, m_mla_kv_norm_g, m_mla_w_ukv, m_w_out, m_ln1_g, m_ln1_b, m_w_gate_up, m_w_down, m_ple_w_gate, m_ple_w_proj, m_ln2_g, m_ln2_b, v_ln_in_g, v_ln_in_b, v_w_in, v_hgrn_lb_logits, v_hgrn_norm_g, v_sgu_ln_g, v_sgu_ln_b, v_sgu_w_s, v_sgu_b_s, v_mla_q_norm_g, v_mla_w_uq, v_mla_kv_norm_g, v_mla_w_ukv, v_w_out, v_ln1_g, v_ln1_b, v_w_gate_up, v_w_down, v_ple_w_gate, v_ple_w_proj, v_ln2_g, v_ln2_b):
    args = dict(locals())
    w = {k: args[k] for k in ORDER}
    m = {k: args["m_" + k] for k in ORDER}
    v = {k: args["v_" + k] for k in ORDER}
    depth = w_in.shape[0]
    assert depth == 2, "the lower-bound kernel is written for two layers"
    alpha = (2 * depth) ** 0.25
    xs, tgt = x[0], loss_target[0]
    d_model = xs.shape[1]

    shards = [_weight_shards(w, li) for li in range(depth)]
    on_hgrn0 = ("mla_w_uq", "mla_w_ukv", "w_out", "ple_w_gate", "ple_w_proj")
    ffn0 = ("w_gate_up", "w_down")
    first1 = ("w_in", "mla_w_uq", "mla_w_ukv", "w_out")
    on_attn1 = ("w_gate_up", "w_down", "ple_w_gate", "ple_w_proj")
    layer1_first = {}

    def after_hgrn0(got):
        return _usable_weights(dict(zip(on_hgrn0, got)), name="l0")

    def after_attn0(got):
        layer1_first.update(_usable_weights(dict(zip(first1, got[len(ffn0):])), name="l1"))
        return _usable_weights(dict(zip(ffn0, got[:len(ffn0)])), name="l0")

    def after_attn1(got):
        return _usable_weights(dict(zip(on_attn1, got)), name="l1")

    tables = _rope_tables(positions[0])
    row1 = lambda a: a.reshape(1, -1)
    l0, l1 = row1(hgrn_lb_logits[0]), row1(hgrn_lb_logits[1])
    lbs = _rowwise(_fn_lower_bounds, [l0, l1], [], [(HG_W, F32), (HG_W, F32)], name="lower_bounds")

    gin, bin_ = row1(ln_in_g), row1(ln_in_b)
    (h, hb), g_in = _rowwise(_twice(_fn_ln), [xs], [gin, bin_], [(d_model, F32), (d_model, BF16)], name="ln_in",
                             job=_gather_job([shards[0]["w_in"]]))
    w_in0 = _usable_weights({"w_in": _gather_forward(g_in, name="gather_l0_w_in_forward")[0]}, name="l0")
    (h, hb), sv0 = _layer_forward(
        0, h, hb, p[0, 0], w_in0, w, lbs, tables, alpha,
        hgrn_job=_gather_job([shards[0][k] for k in on_hgrn0]), after_hgrn=after_hgrn0,
        attn_job=_gather_job([shards[0][k] for k in ffn0] + [shards[1][k] for k in first1]), after_attn=after_attn0)
    (dy, loss_local), sv1 = _layer_forward(
        1, h, hb, p[1, 0], layer1_first, w, lbs, tables, alpha,
        attn_job=_gather_job([shards[1][k] for k in on_attn1]), after_attn=after_attn1, loss_target=tgt)
    saved = [sv0, sv1]
    loss = lax.psum(loss_local[0, 0], ("x", "y", "c"))

    core = lax.axis_index("c").astype(jnp.int32).reshape(1)
    dparts, grads, quads, carried = [dy], [None] * depth, [None] * depth, None
    for li in reversed(range(depth)):
        dparts, grads[li], early_quads, late_sums, late_quads = _layer_backward(
            li, dparts, p[li, 0], saved[li], lbs, tables, alpha, core, carried=carried)
        quads[li] = dict(zip(RS_EARLY, early_quads))
        if carried is not None:
            quads[li + 1].update(zip(RS_LATE, late_quads))
        carried = _quad_job(late_sums)
    (dx, d_gin, d_bin), late_quads = _rowwise_vjp(_fn_ln, [xs], [gin, bin_], [dparts], groups=[[0]], name="ln_in_b",
                                                   job=carried)
    quads[0].update(zip(RS_LATE, late_quads))
    dl0, dl1 = _rowwise_vjp(_fn_lower_bounds, [l0, l1], [], [[grads[0]["lower_bound"]], [grads[1]["lower_bound"]]],
                            groups=[[0], [1]], name="lower_bounds_b")

    prefixes = ("grad_", "delta_", "new_m_", "new_v_")
    uq_pad = ((0, 0), (0, 0), (0, LANES - ATT_D))
    state = {k: ((jnp.pad(w[k], uq_pad), jnp.pad(m[k], uq_pad), jnp.pad(v[k], uq_pad)) if k == "mla_w_uq"
                 else (w[k], m[k], v[k])) for k in BIG}
    out = {}
    for k in BIG:
        res4 = None
        for li in range(depth):
            res4 = _adamw(quads[li][k], *state[k], li, name=f"adamw_l{li}_{k}", into=res4)
        for pre, a in zip(prefixes, res4):
            out[pre + k] = a[:, :, :ATT_D] if k == "mla_w_uq" else a

    small_g = {"ln_in_g": d_gin.reshape(-1), "ln_in_b": d_bin.reshape(-1),
               "hgrn_lb_logits": jnp.stack([dl0.reshape(-1), dl1.reshape(-1)])}
    for k in SMALL[3:]:
        small_g[k] = jnp.stack([grads[li][k].reshape(w[k].shape[1:]) for li in range(depth)])
    (small_parts,) = _all_gather([_pack([small_g[k] for k in SMALL])], name="gather_small_grads", columns=False)
    slabs = _adamw(small_parts, _pack([w[k] for k in SMALL]), _pack([m[k] for k in SMALL]),
                   _pack([v[k] for k in SMALL]), None, name="adamw_small")
    shapes = [w[k].shape for k in SMALL]
    for pre, slab in zip(prefixes, slabs):
        for k, a in zip(SMALL, _unpack(slab, shapes)):
            out[pre + k] = a
    res = [loss, dx[None]]
    for prefix in ("grad_", "delta_", "new_m_", "new_v_"):
        res += [out[prefix + k] for k in ORDER]
    return tuple(res)
```

```python
import functools
import math

import jax
import jax.numpy as jnp
import numpy as np
from jax import lax
from jax.experimental import pallas as pl
from jax.experimental.pallas import tpu as pltpu

F32 = jnp.float32
BF16 = jnp.bfloat16
MESH = pl.DeviceIdType.MESH

LN_EPS = 1e-5
RMS_EPS = 1e-6
ROPE_THETA = 10000.0
ADAM_LR, ADAM_B1, ADAM_B2, ADAM_EPS, ADAM_WD, ADAM_STEP = 0.001, 0.9, 0.999, 1e-08, 0.01, 10

N_DEV = 8
LANES = 128
HG_CHUNK = 16
HG_W = 256
HEAD = 64
SGU_CHUNK = 128
N_ATT_HEADS = 8
ATT_D = 96
VMEM_LIMIT = 56 * 1024 * 1024

HG_TILE = 256
ATT_TQ = 512
ROW_TILE = 256

P_CQ, P_KR, P_CKV, P_COLS = 1536, 1920, 2048, 2304


def _cparams(sem):
    return pltpu.CompilerParams(dimension_semantics=sem, vmem_limit_bytes=VMEM_LIMIT)


_ANY = pl.BlockSpec(memory_space=pl.ANY)


def _call(body, operands, *, name, grid, in_specs, out_specs, out_shape, sem, scratch_shapes=(), job=None):
    if job is None:
        return pl.pallas_call(body, name=name, grid=grid, in_specs=in_specs, out_specs=out_specs, out_shape=out_shape,
                              scratch_shapes=list(scratch_shapes), compiler_params=_cparams(sem))(*operands)
    single = not isinstance(out_shape, (list, tuple))
    shapes = [out_shape] if single else list(out_shape)
    ospecs = [out_specs] if single else list(out_specs)
    ni, no, ns = len(operands), len(shapes), len(scratch_shapes)
    ji, jo = len(job.inputs), len(job.out_shapes)

    def hosted(*refs):
        p = 0
        parts = []
        for cnt in (ni, ji, no, jo, ns):
            parts.append(refs[p:p + cnt])
            p += cnt
        ins, jins, outs, jouts, scr = parts
        jsems = refs[p:]
        ids = [pl.program_id(a) for a in range(len(grid))]
        first = functools.reduce(lambda a, b: a & b, [i == 0 for i in ids])
        last = functools.reduce(lambda a, b: a & b, [i == g - 1 for i, g in zip(ids, grid)])

        @pl.when(first)
        def _():
            job.start(jins, jouts, jsems)

        body(*ins, *outs, *scr)

        @pl.when(last)
        def _():
            job.finish(jins, jouts, jsems)

    res = pl.pallas_call(
        hosted, name=name, grid=grid,
        in_specs=list(in_specs) + [_ANY] * ji, out_specs=ospecs + [_ANY] * jo,
        out_shape=shapes + list(job.out_shapes),
        scratch_shapes=list(scratch_shapes) + [pltpu.SemaphoreType.DMA((c,)) for c in job.sem_counts],
        input_output_aliases=job.aliases(ni, no),
        compiler_params=_cparams(("arbitrary",) * len(grid)),
    )(*operands, *job.inputs)
    own = res[0] if single else res[:no]
    return own, res[no:]


class _Job:
    def __init__(self, inputs, out_shapes, sem_counts, start, finish, in_place=False):
        self.inputs, self.out_shapes, self.sem_counts = list(inputs), list(out_shapes), list(sem_counts)
        self.start, self.finish, self.in_place = start, finish, in_place

    def aliases(self, first_in, first_out):
        return {first_in + i: first_out + i for i in range(len(self.inputs))} if self.in_place else {}


def _copies_job(inputs, out_shapes, n_remote, n_local, make, in_place=False):
    def start(jins, jouts, sems):
        sends, _, local = make(jins, jouts, *sems)
        for cp in local + sends:
            cp.start()

    def finish(jins, jouts, sems):
        sends, recvs, local = make(jins, jouts, *sems)
        for cp in recvs:
            cp.wait_recv()
        for cp in sends:
            cp.wait_send()
        for cp in local:
            cp.wait()

    return _Job(inputs, out_shapes, [n_remote, n_remote, max(n_local, 1)], start, finish, in_place)


def _run_job(job, *, name):
    ji, jo = len(job.inputs), len(job.out_shapes)

    def body(*refs):
        jins, jouts, sems = refs[:ji], refs[ji:ji + jo], refs[ji + jo:]
        job.start(jins, jouts, sems)
        job.finish(jins, jouts, sems)

    return pl.pallas_call(
        body, name=name, out_shape=list(job.out_shapes), in_specs=[_ANY] * ji, out_specs=[_ANY] * jo,
        scratch_shapes=[pltpu.SemaphoreType.DMA((c,)) for c in job.sem_counts],
        input_output_aliases=job.aliases(0, 0),
    )(*job.inputs)


def _tile(n, pref):
    if n % pref == 0:
        return pref
    best = None
    t = LANES
    while t <= min(n, pref):
        if n % t == 0:
            best = t
        t += LANES
    return best if best is not None else n


def _mm(a, b, *, am="mk", bm="kn", om="mn", out_dtype=F32, tm=1024, tn=1024, tk=1024, name, job=None):
    if am == "mk":
        m, k = a.shape
    elif am == "km":
        k, m = a.shape
    elif am == "bmk":
        m, tk = a.shape[1], a.shape[2]
        k = a.shape[0] * tk
    else:
        k, tm = a.shape[1], a.shape[2]
        m = a.shape[0] * tm
    if bm == "kn":
        kb_, n = b.shape
    elif bm == "nk":
        n, kb_ = b.shape
    elif bm == "bkn":
        kb_, tn = b.shape[1], b.shape[2]
        n = b.shape[0] * tn
    else:
        n, tk = b.shape[1], b.shape[2]
        kb_ = b.shape[0] * tk
    assert kb_ == k, (a.shape, b.shape, am, bm)
    tm, tn, tk = _tile(m, tm), _tile(n, tn), _tile(k, tk)
    nk = k // tk
    dims = (((0 if am in ("km", "bkm") else 1,), (1 if bm in ("nk", "bnk") else 0,)), ((), ()))

    a_spec = {"mk": pl.BlockSpec((tm, tk), lambda i, j, kk: (i, kk)),
              "km": pl.BlockSpec((tk, tm), lambda i, j, kk: (kk, i)),
              "bmk": pl.BlockSpec((None, tm, tk), lambda i, j, kk: (kk, i, 0)),
              "bkm": pl.BlockSpec((None, tk, tm), lambda i, j, kk: (i, kk, 0))}[am]
    b_spec = {"kn": pl.BlockSpec((tk, tn), lambda i, j, kk: (kk, j)),
              "nk": pl.BlockSpec((tn, tk), lambda i, j, kk: (j, kk)),
              "bkn": pl.BlockSpec((None, tk, tn), lambda i, j, kk: (j, kk, 0)),
              "bnk": pl.BlockSpec((None, tn, tk), lambda i, j, kk: (kk, j, 0))}[bm]
    if om == "mn":
        o_spec, o_shape = pl.BlockSpec((tm, tn), lambda i, j, kk: (i, j)), (m, n)
    else:
        o_spec, o_shape = pl.BlockSpec((None, tm, tn), lambda i, j, kk: (j, i, 0)), (n // tn, m, tn)

    def body(a_ref, b_ref, o_ref, *acc):
        kk = pl.program_id(2)
        prod = lax.dot_general(a_ref[...].astype(BF16), b_ref[...].astype(BF16), dims, preferred_element_type=F32)
        if nk == 1:
            o_ref[...] = prod.astype(o_ref.dtype)
            return
        acc_ref, = acc

        @pl.when(kk == 0)
        def _():
            acc_ref[...] = prod

        if nk > 2:
            @pl.when((kk > 0) & (kk < nk - 1))
            def _():
                acc_ref[...] += prod

        @pl.when(kk == nk - 1)
        def _():
            o_ref[...] = (acc_ref[...] + prod).astype(o_ref.dtype)

    return _call(body, (a, b), name=name, grid=(m // tm, n // tn, nk), in_specs=[a_spec, b_spec], out_specs=o_spec,
                 out_shape=jax.ShapeDtypeStruct(o_shape, out_dtype),
                 scratch_shapes=[pltpu.VMEM((tm, tn), F32)] if nk > 1 else [],
                 sem=("parallel", "parallel", "arbitrary"), job=job)


def _row_operand(a, tile):
    if isinstance(a, tuple):
        arr, w, j = a
        return arr, pl.BlockSpec((tile, w), lambda i, j=j: (i, j))
    return a, pl.BlockSpec((tile, a.shape[1]), lambda i: (i, 0))


def _const_spec(c):
    nd = c.ndim
    return pl.BlockSpec(c.shape, lambda i, nd=nd: (0,) * nd)


def _rowwise(fn, rows, consts, outs, *, name, accs=(), tile=None, job=None):
    t_rows = (rows[0][0] if isinstance(rows[0], tuple) else rows[0]).shape[0]
    tile = min(tile or ROW_TILE, t_rows)
    arrs, specs = zip(*[_row_operand(a, tile) for a in rows])
    nin, no = len(rows) + len(consts), len(outs)

    def body(*refs):
        res = fn(*[r[...] for r in refs[:nin]])
        for r, v in zip(refs[nin:nin + no], res[:no]):
            r[...] = v.astype(r.dtype)
        if accs:
            a_refs = refs[nin + no:]

            @pl.when(pl.program_id(0) == 0)
            def _():
                for r in a_refs:
                    r[...] = jnp.zeros_like(r)

            for r, v in zip(a_refs, res[no:]):
                r[...] += v

    out_shape = [jax.ShapeDtypeStruct((t_rows, w), dt) for w, dt in outs]
    out_shape += [jax.ShapeDtypeStruct(s, F32) for s in accs]
    out_specs = [pl.BlockSpec((tile, w), lambda i: (i, 0)) for w, _ in outs]
    out_specs += [pl.BlockSpec(s, lambda i, nd=len(s): (0,) * nd) for s in accs]
    return _call(body, (*arrs, *consts), name=name, grid=(t_rows // tile,),
                 in_specs=list(specs) + [_const_spec(c) for c in consts],
                 out_specs=out_specs, out_shape=out_shape, sem=("arbitrary",), job=job)


def _rowwise_vjp(fn, rows, consts, cts, *, name, groups, tile=None, gdtypes=None, job=None):
    t_rows = (rows[0][0] if isinstance(rows[0], tuple) else rows[0]).shape[0]
    tile = min(tile or ROW_TILE, t_rows)
    arrs, specs = zip(*[_row_operand(a, tile) for a in rows])
    flat_cts = [c for group in cts for c in group]
    ct_arrs, ct_specs = zip(*[_row_operand(a, tile) for a in flat_cts])
    nr, nc, nct, ng = len(rows), len(consts), len(flat_cts), len(groups)

    def width(a):
        return a[1] if isinstance(a, tuple) else a.shape[1]

    def body(*refs):
        rv = [r[...].astype(F32) for r in refs[:nr]]
        cv = [r[...] for r in refs[nr:nr + nc]]
        ct_refs = refs[nr + nc:nr + nc + nct]
        ctv, pos = [], 0
        for group in cts:
            s = ct_refs[pos][...].astype(F32)
            for r in ct_refs[pos + 1:pos + len(group)]:
                s = s + r[...].astype(F32)
            ctv.append(s)
            pos += len(group)
        _, pull = jax.vjp(fn, *rv, *cv)
        grads = pull(tuple(ctv))
        g_refs = refs[nr + nc + nct:nr + nc + nct + ng]
        for r, idx in zip(g_refs, groups):
            parts = [grads[i] for i in idx]
            r[...] = (parts[0] if len(parts) == 1 else jnp.concatenate(parts, axis=1)).astype(r.dtype)
        c_refs = refs[nr + nc + nct + ng:]

        @pl.when(pl.program_id(0) == 0)
        def _():
            for r in c_refs:
                r[...] = jnp.zeros_like(r)

        for r, v in zip(c_refs, grads[nr:]):
            r[...] += v

    gw = [sum(width(rows[i]) for i in idx) for idx in groups]
    gdtypes = gdtypes or [F32] * ng
    out_shape = [jax.ShapeDtypeStruct((t_rows, w), dt) for w, dt in zip(gw, gdtypes)]
    out_shape += [jax.ShapeDtypeStruct(c.shape, F32) for c in consts]
    out_specs = [pl.BlockSpec((tile, w), lambda i: (i, 0)) for w in gw]
    out_specs += [_const_spec(c) for c in consts]
    return _call(body, (*arrs, *consts, *ct_arrs), name=name, grid=(t_rows // tile,),
                 in_specs=list(specs) + [_const_spec(c) for c in consts] + list(ct_specs),
                 out_specs=out_specs, out_shape=out_shape, sem=("arbitrary",), job=job)


def _layer_norm(x, g, b):
    mu = jnp.mean(x, axis=-1, keepdims=True)
    xc = x - mu
    var = jnp.mean(xc * xc, axis=-1, keepdims=True)
    return xc * lax.rsqrt(var + LN_EPS) * g + b


def _sigmoid(x):
    return 1.0 / (1.0 + jnp.exp(-x))


def _fn_ln(x, g, b):
    return (_layer_norm(x, g, b),)


def _fn_rms(x, g):
    return (x * lax.rsqrt(jnp.mean(x * x, axis=-1, keepdims=True) + RMS_EPS) * g,)


def _make_post_mix(alpha):
    def fn(h, mix, g, b):
        return (_layer_norm(alpha * h + mix, g, b),)
    return fn


def _make_ple_ln(alpha):
    def fn(h1, ffn, pg, pp, g, b):
        return (_layer_norm(alpha * h1 + ffn + _sigmoid(pg) * pp, g, b),)
    return fn


def _fn_lower_bounds(l0, l1):
    m = jnp.maximum(l0, l1)
    e0, e1 = jnp.exp(l0 - m), jnp.exp(l1 - m)
    s = e0 + e1
    p0, p1 = e0 / s, e1 / s
    return (p0 - p0, (p0 + p1) - p0)


def _split_dot(x, e_bf16):
    hi = x.astype(BF16)
    lo = (x - hi.astype(F32)).astype(BF16)
    return (jnp.dot(hi, e_bf16, preferred_element_type=F32) + jnp.dot(lo, e_bf16, preferred_element_type=F32))


def _hgrn_common(th):
    rm = lax.broadcasted_iota(jnp.int32, (th, HG_W), 0) % HG_CHUNK

    def seg_cumsum(x):
        for s in (1, 2, 4, 8):
            x = x + jnp.where(rm >= s, pltpu.roll(x, s, 0), 0.0)
        return x

    def seg_rcumsum(x):
        for s in (1, 2, 4, 8):
            x = x + jnp.where(rm < HG_CHUNK - s, pltpu.roll(x, th - s, 0), 0.0)
        return x

    ri = lax.broadcasted_iota(jnp.int32, (HG_W, HG_W), 0) // HEAD
    ci = lax.broadcasted_iota(jnp.int32, (HG_W, HG_W), 1) // HEAD
    head_f32 = (ri == ci).astype(F32)
    head_bf16 = head_f32.astype(BF16)

    def headsum(x, pieces=2):
        if pieces == 1:
            return jnp.dot(x.astype(BF16), head_bf16, preferred_element_type=F32)
        return _split_dot(x, head_bf16)

    return rm, seg_cumsum, seg_rcumsum, head_f32, headsum


def _hgrn_gates(qr, fl, lb):
    sg = _sigmoid(fl)
    f = lb + (1.0 - lb) * sg
    sq = _sigmoid(qr)
    return sg, f, jnp.log(f), 1.0 - f, qr * sq, sq


def _shifted(x, d, th):
    return x if d == 0 else pltpu.roll(x, d, 0)


def _unshift(x, d, th):
    return x if d == 0 else pltpu.roll(x, th - d, 0)


def _hgrn_fwd(projp, lb, ng, *, name, job=None):
    t_rows = projp.shape[0]
    th = min(HG_TILE, t_rows)
    nct = th // HG_CHUNK

    def body(q_ref, f_ref, i_ref, g_ref, lb_ref, ng_ref, oa_ref, opre_ref, st_out_ref,
             st_ref, vtm_ref, kv_ref, qe_ref, dec_ref, oint_ref):
        rm, seg_cumsum, seg_rcumsum, head_f32, headsum = _hgrn_common(th)

        @pl.when(pl.program_id(0) == 0)
        def _():
            st_ref[...] = jnp.zeros_like(st_ref)

        qr, fl, v, g = q_ref[...], f_ref[...], i_ref[...], g_ref[...]
        _, f, lf, k, q, _ = _hgrn_gates(qr, fl, lb_ref[...])
        b = seg_cumsum(lf)

        o = jnp.zeros((th, HG_W), F32)
        for d in range(HG_CHUNK):
            kd, bd, vd = _shifted(k, d, th), _shifted(b, d, th), _shifted(v, d, th)
            e = jnp.exp(jnp.where(rm >= d, b - bd, -1e30))
            o = o + headsum(q * kd * e, 1) * vd

        blast = seg_rcumsum(jnp.where(rm == HG_CHUNK - 1, b, 0.0))
        kte = (k * jnp.exp(blast - b)).astype(BF16)
        qe_ref[...] = q * jnp.exp(b)
        dec_ref[...] = jnp.exp(blast)
        vt = v.T
        lane_chunk = lax.broadcasted_iota(jnp.int32, (HG_W, th), 1) // HG_CHUNK
        for c in range(nct):
            vtm_ref[c * HG_W:(c + 1) * HG_W, :] = jnp.where(lane_chunk == c, vt, 0.0).astype(BF16)
        kv_ref[...] = jnp.dot(vtm_ref[...], kte, preferred_element_type=F32)

        s = st_ref[...]
        for c in range(nct):
            rows = slice(c * HG_CHUNK, (c + 1) * HG_CHUNK)
            st_out_ref[c] = s
            oint_ref[rows, :] = lax.dot_general(qe_ref[rows, :].astype(BF16), s.astype(BF16),
                                                (((1,), (1,)), ((), ())), preferred_element_type=F32)
            dec = jnp.max(dec_ref[rows, :], axis=0, keepdims=True)
            s = s * dec + kv_ref[c * HG_W:(c + 1) * HG_W, :] * head_f32
        st_ref[...] = s

        o = o + oint_ref[...]
        opre_ref[...] = o
        r = lax.rsqrt(headsum(o * o) * (1.0 / HEAD) + RMS_EPS)
        oa_ref[...] = (o * r * ng_ref[...] * (g * _sigmoid(g))).astype(oa_ref.dtype)

    col = lambda j: pl.BlockSpec((th, HG_W), lambda i, j=j: (i, j))
    vec = pl.BlockSpec((1, HG_W), lambda i: (0, 0))
    row = pl.BlockSpec((th, HG_W), lambda i: (i, 0))
    n_chunks = t_rows // HG_CHUNK
    return _call(
        body, (projp, projp, projp, projp, lb, ng), name=name, grid=(t_rows // th,),
        in_specs=[col(0), col(1), col(2), col(3), vec, vec],
        out_specs=[row, row, pl.BlockSpec((nct, HG_W, HG_W), lambda i: (i, 0, 0))],
        out_shape=[jax.ShapeDtypeStruct((t_rows, HG_W), BF16), jax.ShapeDtypeStruct((t_rows, HG_W), F32),
                   jax.ShapeDtypeStruct((n_chunks, HG_W, HG_W), F32)],
        scratch_shapes=[pltpu.VMEM((HG_W, HG_W), F32), pltpu.VMEM((nct * HG_W, th), BF16),
                        pltpu.VMEM((nct * HG_W, HG_W), F32), pltpu.VMEM((th, HG_W), F32),
                        pltpu.VMEM((th, HG_W), F32), pltpu.VMEM((th, HG_W), F32)],
        sem=("arbitrary",), job=job)


def _hgrn_bwd(projp, lb, ng, opre, states, dcat, *, name):
    t_rows = projp.shape[0]
    th = min(HG_TILE, t_rows)
    nct = th // HG_CHUNK
    nt = t_rows // th

    def body(q_ref, f_ref, i_ref, g_ref, lb_ref, ng_ref, opre_ref, st_in_ref, do_ref,
             dproj_ref, dng_ref, dlb_ref,
             gst_ref, dotm_ref, qg_ref, v_ref, kte_ref, dop_ref, dec_ref, dkte_ref, dvi_ref, dqe_ref, ddec_ref):
        rm, seg_cumsum, seg_rcumsum, head_f32, headsum = _hgrn_common(th)

        @pl.when(pl.program_id(0) == 0)
        def _():
            gst_ref[...] = jnp.zeros_like(gst_ref)
            dng_ref[...] = jnp.zeros_like(dng_ref)
            dlb_ref[...] = jnp.zeros_like(dlb_ref)

        qr, fl, v, g = q_ref[...], f_ref[...], i_ref[...], g_ref[...]
        lb, ngv = lb_ref[...], ng_ref[...]
        sg, f, lf, k, q, sq = _hgrn_gates(qr, fl, lb)
        b = seg_cumsum(lf)
        blast = seg_rcumsum(jnp.where(rm == HG_CHUNK - 1, b, 0.0))
        eb = jnp.exp(b)
        ekb = jnp.exp(blast - b)
        qe, kte, dec = q * eb, k * ekb, jnp.exp(blast)

        do_out, op = do_ref[...], opre_ref[...]
        sgg = _sigmoid(g)
        sil = g * sgg
        r = lax.rsqrt(headsum(op * op) * (1.0 / HEAD) + RMS_EPS)
        on = op * r
        dng_ref[...] += jnp.sum(do_out * on * sil, axis=0, keepdims=True)
        dg = do_out * on * ngv * (sgg * (1.0 + g * (1.0 - sgg)))
        don = do_out * ngv * sil
        dop = r * (don - on * (headsum(don * on) * (1.0 / HEAD)))

        v_ref[...] = v
        kte_ref[...] = kte
        dop_ref[...] = dop
        dec_ref[...] = dec
        dot_t = dop.T
        lane_chunk = lax.broadcasted_iota(jnp.int32, (HG_W, th), 1) // HG_CHUNK
        for c in range(nct):
            dotm_ref[c * HG_W:(c + 1) * HG_W, :] = jnp.where(lane_chunk == c, dot_t, 0.0).astype(BF16)
        qg_ref[...] = jnp.dot(dotm_ref[...], qe.astype(BF16), preferred_element_type=F32)

        gs = gst_ref[...]
        for c in reversed(range(nct)):
            rows = slice(c * HG_CHUNK, (c + 1) * HG_CHUNK)
            s = st_in_ref[c]
            gm = (gs * head_f32).astype(BF16)
            dkte_ref[rows, :] = jnp.dot(v_ref[rows, :].astype(BF16), gm, preferred_element_type=F32)
            dvi_ref[rows, :] = lax.dot_general(kte_ref[rows, :].astype(BF16), gm, (((1,), (1,)), ((), ())),
                                               preferred_element_type=F32)
            dqe_ref[rows, :] = jnp.dot(dop_ref[rows, :].astype(BF16), s.astype(BF16), preferred_element_type=F32)
            ddec_ref[rows, :] = jnp.broadcast_to(jnp.sum(gs * s, axis=0, keepdims=True), (HG_CHUNK, HG_W))
            dec_c = jnp.max(dec_ref[rows, :], axis=0, keepdims=True)
            gs = gs * dec_c + qg_ref[c * HG_W:(c + 1) * HG_W, :] * head_f32
        gst_ref[...] = gs

        dkte, dqe = dkte_ref[...], dqe_ref[...]
        dq = dqe * eb
        dk = dkte * ekb
        db = dqe * qe - dkte * kte
        dv = dvi_ref[...]
        dblast = dkte * kte + jnp.where(rm == HG_CHUNK - 1, ddec_ref[...] * dec, 0.0)

        for d in range(HG_CHUNK):
            kd, bd, vd = _shifted(k, d, th), _shifted(b, d, th), _shifted(v, d, th)
            e = jnp.exp(jnp.where(rm >= d, b - bd, -1e30))
            p = q * kd * e
            sc = headsum(p, 1)
            dsc = headsum(dop * vd, 1)
            dv = dv + _unshift(sc * dop, d, th)
            dq = dq + dsc * kd * e
            dk = dk + _unshift(dsc * q * e, d, th)
            darg = dsc * p
            db = db + darg - _unshift(darg, d, th)

        db = db + jnp.where(rm == HG_CHUNK - 1, seg_cumsum(dblast), 0.0)
        dlf = seg_rcumsum(db)
        df = dlf / f - dk
        dlb_ref[...] += jnp.sum(df * (1.0 - sg), axis=0, keepdims=True)
        dfl = df * (1.0 - lb) * sg * (1.0 - sg)
        dqr = dq * (sq * (1.0 + qr * (1.0 - sq)))
        dproj_ref[...] = jnp.concatenate([dqr, dfl, dv, dg], axis=1).astype(dproj_ref.dtype)

    rev = lambda i: nt - 1 - i
    col = lambda j: pl.BlockSpec((th, HG_W), lambda i, j=j: (rev(i), j))
    vec = pl.BlockSpec((1, HG_W), lambda i: (0, 0))
    row = pl.BlockSpec((th, HG_W), lambda i: (rev(i), 0))
    tile_f32 = pltpu.VMEM((th, HG_W), F32)
    return pl.pallas_call(
        body, name=name, grid=(nt,),
        in_specs=[col(0), col(1), col(2), col(3), vec, vec, row,
                  pl.BlockSpec((nct, HG_W, HG_W), lambda i: (rev(i), 0, 0)), col(0)],
        out_specs=[pl.BlockSpec((th, 4 * HG_W), lambda i: (rev(i), 0)), vec, vec],
        out_shape=[jax.ShapeDtypeStruct((t_rows, 4 * HG_W), BF16), jax.ShapeDtypeStruct((1, HG_W), F32),
                   jax.ShapeDtypeStruct((1, HG_W), F32)],
        scratch_shapes=[pltpu.VMEM((HG_W, HG_W), F32), pltpu.VMEM((nct * HG_W, th), BF16),
                        pltpu.VMEM((nct * HG_W, HG_W), F32)] + [tile_f32] * 8,
        compiler_params=_cparams(("arbitrary",)),
    )(projp, projp, projp, projp, lb, ng, opre, states, dcat)


_INV_SQRT2 = 1.0 / math.sqrt(2.0)
_INV_SQRT2PI = 1.0 / math.sqrt(2.0 * math.pi)


def _gelu(x):
    return 0.5 * x * (1.0 + lax.erf(x * _INV_SQRT2))


def _gelu_grad(x):
    return 0.5 * (1.0 + lax.erf(x * _INV_SQRT2)) + x * jnp.exp(-0.5 * x * x) * _INV_SQRT2PI


def _sgu_parts(bu, bv, lg, lbias, w_ref, n_groups):
    c = SGU_CHUNK
    tril = (lax.broadcasted_iota(jnp.int32, (c, c), 0) >= lax.broadcasted_iota(jnp.int32, (c, c), 1)).astype(F32)
    gid = lax.broadcasted_iota(jnp.int32, bu.shape, 1) // HEAD
    u = _gelu(bu)
    gv = _gelu(bv)
    mu = jnp.mean(gv, axis=-1, keepdims=True)
    xc = gv - mu
    rstd = lax.rsqrt(jnp.mean(xc * xc, axis=-1, keepdims=True) + LN_EPS)
    xhat = xc * rstd
    vn = xhat * lg + lbias
    ws = [w_ref[gi] * tril for gi in range(n_groups)]
    return tril, gid, u, rstd, xhat, vn, ws


def _sgu_fwd(projp, lg, lbias, w_s, bias_full, *, name, job=None):
    t_rows = projp.shape[0]
    n_groups = w_s.shape[0]
    c = SGU_CHUNK

    def body(u_ref, v_ref, lg_ref, lb_ref, w_ref, bias_ref, o_ref):
        _, gid, u, _, _, vn, ws = _sgu_parts(u_ref[...], v_ref[...], lg_ref[...], lb_ref[...], w_ref, n_groups)
        vnb = vn.astype(BF16)
        z = bias_ref[...]
        for gi in range(n_groups):
            z = z + jnp.where(gid == gi, jnp.dot(ws[gi].astype(BF16), vnb, preferred_element_type=F32), 0.0)
        o_ref[...] = (u * z).astype(o_ref.dtype)

    col = lambda j: pl.BlockSpec((c, HG_W), lambda i, j=j: (i, j))
    return _call(
        body, (projp, projp, lg, lbias, w_s, bias_full), name=name, grid=(t_rows // c,),
        in_specs=[col(4), col(5), _const_spec(lg), _const_spec(lbias), _const_spec(w_s), _const_spec(bias_full)],
        out_specs=pl.BlockSpec((c, HG_W), lambda i: (i, 0)),
        out_shape=jax.ShapeDtypeStruct((t_rows, HG_W), BF16), sem=("arbitrary",), job=job)


def _sgu_bwd(projp, lg, lbias, w_s, bias_full, dcat, *, name):
    t_rows = projp.shape[0]
    n_groups = w_s.shape[0]
    c = SGU_CHUNK
    n = t_rows // c

    def body(u_ref, v_ref, lg_ref, lb_ref, w_ref, bias_ref, do_ref,
             dproj_ref, dlg_ref, dlb_ref, dw_ref, dbs_ref, dbias_acc):
        i = pl.program_id(0)

        @pl.when(i == 0)
        def _():
            dlg_ref[...] = jnp.zeros_like(dlg_ref)
            dlb_ref[...] = jnp.zeros_like(dlb_ref)
            dw_ref[...] = jnp.zeros_like(dw_ref)
            dbias_acc[...] = jnp.zeros_like(dbias_acc)

        bu, bv, lg_v = u_ref[...], v_ref[...], lg_ref[...]
        tril, gid, u, rstd, xhat, vn, ws = _sgu_parts(bu, bv, lg_v, lb_ref[...], w_ref, n_groups)
        vnb = vn.astype(BF16)
        z = bias_ref[...]
        for gi in range(n_groups):
            z = z + jnp.where(gid == gi, jnp.dot(ws[gi].astype(BF16), vnb, preferred_element_type=F32), 0.0)
        do = do_ref[...]
        dbu = do * z * _gelu_grad(bu)
        dz = do * u
        dbias_acc[...] += dz
        dvn = jnp.zeros_like(dz)
        for gi in range(n_groups):
            dzg = jnp.where(gid == gi, dz, 0.0).astype(BF16)
            dw_ref[gi] += lax.dot_general(dzg, vnb, (((1,), (1,)), ((), ())), preferred_element_type=F32) * tril
            dvn = dvn + jnp.dot(ws[gi].T.astype(BF16), dzg, preferred_element_type=F32)
        dlg_ref[...] += jnp.sum(dvn * xhat, axis=0, keepdims=True)
        dlb_ref[...] += jnp.sum(dvn, axis=0, keepdims=True)
        dxh = dvn * lg_v
        dgv = rstd * (dxh - jnp.mean(dxh, axis=-1, keepdims=True)
                      - xhat * jnp.mean(dxh * xhat, axis=-1, keepdims=True))
        dproj_ref[...] = jnp.concatenate([dbu, dgv * _gelu_grad(bv)], axis=1).astype(dproj_ref.dtype)

        @pl.when(i == n - 1)
        def _():
            dbs_ref[...] = jnp.sum(dbias_acc[...].T.reshape(n_groups, HEAD, c), axis=1)

    col = lambda j: pl.BlockSpec((c, HG_W), lambda i, j=j: (i, j))
    return pl.pallas_call(
        body, name=name, grid=(n,),
        in_specs=[col(4), col(5), _const_spec(lg), _const_spec(lbias), _const_spec(w_s), _const_spec(bias_full),
                  col(1)],
        out_specs=[pl.BlockSpec((c, 2 * HG_W), lambda i: (i, 0)), _const_spec(lg), _const_spec(lbias),
                   _const_spec(w_s), pl.BlockSpec((n_groups, c), lambda i: (0, 0))],
        out_shape=[jax.ShapeDtypeStruct((t_rows, 2 * HG_W), BF16), jax.ShapeDtypeStruct(lg.shape, F32),
                   jax.ShapeDtypeStruct(lbias.shape, F32), jax.ShapeDtypeStruct(w_s.shape, F32),
                   jax.ShapeDtypeStruct((n_groups, c), F32)],
        scratch_shapes=[pltpu.VMEM((c, HG_W), F32)],
        compiler_params=_cparams(("arbitrary",)),
    )(projp, projp, lg, lbias, w_s, bias_full, dcat)


def _rope_tables(positions):
    t = positions.shape[0]
    inv_freq = ROPE_THETA ** (-jnp.arange(0, 32, 2, dtype=F32) / 32)
    ang = positions.astype(F32)[:, None] * inv_freq
    cos, sin = jnp.cos(ang), jnp.sin(ang)
    z = lambda w: jnp.zeros((t, w), F32)
    cos_t = jnp.concatenate([jnp.ones((t, 64), F32), cos, cos, z(32)], axis=1)
    sin_up = jnp.concatenate([z(80), sin, z(32)], axis=1)
    sin_dn = jnp.concatenate([z(64), -sin, z(48)], axis=1)
    return cos_t, sin_up, sin_dn


def _rep(x, n):
    return x if n == 1 else jnp.concatenate([x] * n, axis=1)


def _rope(x, cos_t, sin_up, sin_dn):
    w = x.shape[1]
    return x * cos_t + pltpu.roll(x, 16, 1) * sin_up + pltpu.roll(x, w - 16, 1) * sin_dn


def _rope_t(dy, cos_t, sin_up, sin_dn):
    w = dy.shape[1]
    return dy * cos_t + pltpu.roll(dy * sin_up, w - 16, 1) + pltpu.roll(dy * sin_dn, 16, 1)


def _mla_prep(q, kv, projp, tables, *, name):
    nh = N_ATT_HEADS

    def fn(qv, kvv, kr, cos_t, sin_up, sin_dn):
        qr = _rope(qv, _rep(cos_t, nh), _rep(sin_up, nh), _rep(sin_dn, nh))
        krr = _rope(kr, cos_t, sin_up, sin_dn)
        lane = lax.broadcasted_iota(jnp.int32, kvv.shape, 1) % LANES
        return qr, jnp.where(lane < HEAD, kvv, 0.0) + _rep(krr, nh), kvv

    w = q.shape[1]
    return _rowwise(fn, [q, kv, (projp, LANES, P_KR // LANES)] + list(tables), [],
                    [(w, BF16), (w, BF16), (w, BF16)], name=name)


def _mla_prep_bwd(dqr, dkf, tables, *, name):
    nh = N_ATT_HEADS

    def fn(dq, dk, cos_t, sin_up, sin_dn):
        dqp = _rope_t(dq, _rep(cos_t, nh), _rep(sin_up, nh), _rep(sin_dn, nh))
        dkrr = dk[:, 0:LANES]
        for h in range(1, nh):
            dkrr = dkrr + dk[:, LANES * h:LANES * (h + 1)]
        return dqp, _rope_t(dkrr, cos_t, sin_up, sin_dn)

    return _rowwise(fn, [dqr, dkf] + list(tables), [], [(dqr.shape[1], BF16), (LANES, BF16)], name=name)


_LOG2E = 1.0 / math.log(2.0)
_NT = (((1,), (1,)), ((), ()))
_TN = (((0,), (0,)), ((), ()))


def _attn_fwd(qr, kf, kvb, *, name, job=None):
    t_rows = qr.shape[0]
    tq = min(ATT_TQ, t_rows)
    nb = t_rows // tq
    scale = ATT_D ** -0.5

    c2 = scale * _LOG2E

    def body(q_ref, kf_ref, kvb_ref, o_ref, lse_ref):
        qi = pl.program_id(1)
        lane = lax.broadcasted_iota(jnp.int32, (tq, LANES), 1)
        causal_t = (lax.broadcasted_iota(jnp.int32, (tq, tq), 0) <= lax.broadcasted_iota(jnp.int32, (tq, tq), 1))
        heads = [slice(hh * LANES, (hh + 1) * LANES) for hh in range(2)]
        qs = [q_ref[:, cols] for cols in heads]

        def block(first, n_keys, carry, diagonal):
            rows = pl.ds(pl.multiple_of(first * tq, tq), n_keys)
            new = []
            for q, cols, (m_old, l_old, acc_t) in zip(qs, heads, carry):
                s_t = lax.dot_general(kf_ref[rows, cols], q, _NT, preferred_element_type=F32)
                if diagonal:
                    s_t = jnp.where(causal_t, s_t, -1e30)
                m_new = jnp.maximum(m_old, jnp.max(s_t, axis=0, keepdims=True))
                p_t = jnp.exp2((s_t - m_new) * c2)
                a = jnp.exp2((m_old - m_new) * c2)
                pv_t = lax.dot_general(kvb_ref[rows, cols], p_t.astype(BF16), _TN, preferred_element_type=F32)
                new.append((m_new, a * l_old + jnp.sum(p_t, axis=0, keepdims=True), a * acc_t + pv_t))
            return tuple(new)

        init = (jnp.full((1, tq), -1e30, F32), jnp.zeros((1, tq), F32), jnp.zeros((LANES, tq), F32))
        carry = lax.fori_loop(0, qi // 4, lambda g, c: block(4 * g, 4 * tq, c, False), (init, init))
        carry = lax.cond((qi // 2) % 2 == 1, lambda c: block(4 * (qi // 4), 2 * tq, c, False), lambda c: c, carry)
        carry = lax.cond(qi % 2 == 1, lambda c: block(qi - 1, tq, c, False), lambda c: c, carry)
        outs = []
        for hh, (m_fin, l_fin, acc_t) in enumerate(block(qi, tq, carry, True)):
            lse_ref[hh] = m_fin * scale + jnp.log(l_fin)
            outs.append((acc_t / l_fin).T)
        o_ref[...] = jnp.where(lane < HEAD, pltpu.roll(outs[0], HEAD, 1), outs[1])

    pair = pl.BlockSpec((t_rows, 2 * LANES), lambda pr, qi: (0, pr))
    return _call(
        body, (qr, kf, kvb), name=name, grid=(N_ATT_HEADS // 2, nb),
        in_specs=[pl.BlockSpec((tq, 2 * LANES), lambda pr, qi: (qi, pr)), pair, pair],
        out_specs=[pl.BlockSpec((tq, LANES), lambda pr, qi: (qi, pr)),
                   pl.BlockSpec((2, 1, tq), lambda pr, qi: (pr, 0, qi))],
        out_shape=[jax.ShapeDtypeStruct((t_rows, N_ATT_HEADS * HEAD), F32),
                   jax.ShapeDtypeStruct((N_ATT_HEADS, 1, t_rows), F32)],
        sem=("parallel", "arbitrary"), job=job)


def _attn_bwd(qr, kf, kvb, dcat, o, lse, *, name, job=None):
    t_rows = qr.shape[0]
    tq = min(ATT_TQ, t_rows)
    nb = t_rows // tq
    scale = ATT_D ** -0.5
    c2 = scale * _LOG2E
    do_off = 2 * HG_W // LANES

    def body(q_ref, kf_ref, kvb_ref, do_ref, o_ref, lse_ref, dq_ref, dkv_ref, dk_ref):
        ki = pl.program_id(1)

        @pl.when(ki == 0)
        def _():
            dq_ref[...] = jnp.zeros_like(dq_ref)

        lane = lax.broadcasted_iota(jnp.int32, (tq, LANES), 1)
        causal_t = (lax.broadcasted_iota(jnp.int32, (tq, tq), 0) <= lax.broadcasted_iota(jnp.int32, (tq, tq), 1))
        heads = [slice(hh * LANES, (hh + 1) * LANES) for hh in range(2)]
        ks = [kf_ref[:, cols] for cols in heads]
        vs = [kvb_ref[:, cols] for cols in heads]

        def block(qi, n_q, carry, diagonal):
            rows = pl.ds(pl.multiple_of(qi * tq, tq), n_q)
            do_pair, o_pair = do_ref[rows, :], o_ref[rows, :]
            upper = lax.broadcasted_iota(jnp.int32, do_pair.shape, 1) >= HEAD
            new = []
            for hh, (cols, k, v, (dk, dv)) in enumerate(zip(heads, ks, vs, carry)):
                q = q_ref[rows, cols]
                do, ov = (pltpu.roll(do_pair, HEAD, 1), pltpu.roll(o_pair, HEAD, 1)) if hh == 0 else (do_pair, o_pair)
                do = jnp.where(upper, do, 0.0)
                delta = jnp.sum((do * ov).T, axis=0, keepdims=True)
                s_t = lax.dot_general(k, q, _NT, preferred_element_type=F32)
                if diagonal:
                    s_t = jnp.where(causal_t, s_t, -1e30)
                p_t = jnp.exp2(s_t * c2 - lse_ref[hh, :, rows] * _LOG2E)
                dob = do.astype(BF16)
                dv = dv + jnp.dot(p_t.astype(BF16), dob, preferred_element_type=F32)
                dp_t = lax.dot_general(v, dob, _NT, preferred_element_type=F32)
                ds_t = (p_t * (dp_t - delta) * scale).astype(BF16)
                dk = dk + jnp.dot(ds_t, q, preferred_element_type=F32)
                dq_ref[rows, cols] += lax.dot_general(ds_t, k, _TN, preferred_element_type=F32)
                new.append((dk, dv))
            return tuple(new)

        zero = jnp.zeros((tq, LANES), F32)
        carry = block(ki, tq, ((zero, zero), (zero, zero)), True)
        rest = nb - 1 - ki
        carry = lax.fori_loop(0, rest // 2, lambda g, c: block(ki + 1 + 2 * g, 2 * tq, c, False), carry)
        carry = lax.cond(rest % 2 == 1, lambda c: block(nb - 1, tq, c, False), lambda c: c, carry)
        dkv_ref[...] = jnp.concatenate([jnp.where(lane < HEAD, dk, dv) for dk, dv in carry],
                                       axis=1).astype(dkv_ref.dtype)
        dk_ref[...] = jnp.concatenate([dk for dk, _ in carry], axis=1)

    pair_all = pl.BlockSpec((t_rows, 2 * LANES), lambda pr, ki: (0, pr))
    pair_blk = pl.BlockSpec((tq, 2 * LANES), lambda pr, ki: (ki, pr))
    wide = jax.ShapeDtypeStruct((t_rows, N_ATT_HEADS * LANES), F32)
    return _call(
        body, (qr, kf, kvb, dcat, o, lse), name=name, grid=(N_ATT_HEADS // 2, nb),
        in_specs=[pair_all, pair_blk, pair_blk,
                  pl.BlockSpec((t_rows, LANES), lambda pr, ki: (0, do_off + pr)),
                  pl.BlockSpec((t_rows, LANES), lambda pr, ki: (0, pr)),
                  pl.BlockSpec((2, 1, t_rows), lambda pr, ki: (pr, 0, 0))],
        out_specs=[pair_all, pair_blk, pair_blk],
        out_shape=[wide, jax.ShapeDtypeStruct(wide.shape, BF16), wide],
        sem=("parallel", "arbitrary"), job=job)


def _my_pos():
    return lax.axis_index("x"), lax.axis_index("y"), lax.axis_index("c")


def _all_gather(xs, *, name, columns=True):
    return _gather_forward(_run_job(_gather_job(xs, columns), name=name), name=name + "_forward")


def _remote(src, dst, send_sems, recv_sems, k, dev):
    return pltpu.make_async_remote_copy(src_ref=src, dst_ref=dst, send_sem=send_sems.at[k], recv_sem=recv_sems.at[k],
                                        device_id=dev, device_id_type=MESH)


def _block(ref, idx):
    if len(ref.shape) == 2:
        return ref.at[:, pl.ds(pl.multiple_of(idx * LANES, LANES), LANES)]
    return ref.at[idx]


def _gather_job(xs, columns=True):
    n = len(xs)

    def make(x_refs, out_refs, send_sems, recv_sems, local_sems):
        mx, my, mc = _my_pos()
        mine = 4 * mx + 2 * my + mc
        peers = [(mx, my, 1 - mc), (1 - mx, my, mc), (mx, 1 - my, mc), (1 - mx, 1 - my, mc)]
        sends, recvs, local = [], [], []
        for a in range(n):
            local.append(pltpu.make_async_copy(x_refs[a], _block(out_refs[a], mine), local_sems.at[a]))
            for k, dev in enumerate(peers):
                theirs = 4 * dev[0] + 2 * dev[1] + dev[2]
                sends.append(_remote(x_refs[a], _block(out_refs[a], mine), send_sems, recv_sems, 4 * a + k, dev))
                recvs.append(_remote(x_refs[a], _block(out_refs[a], theirs), send_sems, recv_sems, 4 * a + k, dev))
        return sends, recvs, local

    def gathered(x):
        if columns and x.ndim == 2 and x.shape[1] == LANES:
            return jax.ShapeDtypeStruct((x.shape[0], N_DEV * LANES), x.dtype)
        return jax.ShapeDtypeStruct((N_DEV,) + x.shape, x.dtype)

    return _copies_job(xs, [gathered(x) for x in xs], 4 * n, n, make)


def _forward_job(gs):
    n = len(gs)

    def make(in_refs, out_refs, send_sems, recv_sems, local_sems):
        mx, my, mc = _my_pos()
        chips = [(1 - mx, my), (mx, 1 - my), (1 - mx, 1 - my)]
        sends, recvs = [], []
        for a in range(n):
            for j, (cx, cy) in enumerate(chips):
                here = _block(out_refs[a], 4 * cx + 2 * cy + mc)
                there = _block(out_refs[a], 4 * cx + 2 * cy + 1 - mc)
                sends.append(_remote(here, here, send_sems, recv_sems, 3 * a + j, (mx, my, 1 - mc)))
                recvs.append(_remote(here, there, send_sems, recv_sems, 3 * a + j, (mx, my, 1 - mc)))
        return sends, recvs, []

    shapes = [jax.ShapeDtypeStruct(g.shape, g.dtype) for g in gs]
    return _copies_job(gs, shapes, 3 * n, 0, make, in_place=True)


def _gather_forward(gs, *, name):
    return _run_job(_forward_job(gs), name=name)


def _pair_job(xs):
    n = len(xs)

    def make(x_refs, out_refs, send_sems, recv_sems, local_sems):
        mx, my, mc = _my_pos()

        def src(ref, g):
            return _block(ref, 2 * g + 1 - mc) if len(ref.shape) == 2 else ref.at[g, 1 - mc]

        copies = [_remote(src(x_refs[a], g), out_refs[a].at[g], send_sems, recv_sems, 4 * a + g, (mx, my, 1 - mc))
                  for a in range(n) for g in range(4)]
        return copies, copies, []

    shapes = [jax.ShapeDtypeStruct((4, x.shape[0], LANES) if x.ndim == 2 else (4,) + x.shape[2:], x.dtype)
              for x in xs]
    return _copies_job(xs, shapes, 4 * n, 0, make)


def _pair_add(x, r, core, *, name):
    _, a, b = r.shape
    ta = _row_tile(a, 256)

    def body(c_ref, x_ref, r_ref, o_ref):
        o_ref[...] = (x_ref[...] + r_ref[...]).astype(o_ref.dtype)

    blk = pl.BlockSpec((None, ta, b), lambda g, i, c_ref: (g, i, 0))
    own = (pl.BlockSpec((ta, b), lambda g, i, c_ref: (i, 2 * g + c_ref[0])) if x.ndim == 2
           else pl.BlockSpec((None, None, ta, b), lambda g, i, c_ref: (g, c_ref[0], i, 0)))
    return pl.pallas_call(
        body, name=name,
        grid_spec=pltpu.PrefetchScalarGridSpec(
            num_scalar_prefetch=1, grid=(4, a // ta), in_specs=[own, blk], out_specs=blk),
        out_shape=jax.ShapeDtypeStruct((4, a, b), BF16),
        compiler_params=_cparams(("parallel", "parallel")),
    )(core, x, r)


def _quad_job(xs):
    n = len(xs)

    def make(x_refs, out_refs, send_sems, recv_sems, local_sems):
        mx, my, mc = _my_pos()
        mine = 2 * mx + my
        peers = [((1 - mx, my, mc), 2 * (1 - mx) + my), ((mx, 1 - my, mc), 2 * mx + 1 - my),
                 ((1 - mx, 1 - my, mc), 2 * (1 - mx) + 1 - my)]
        sends, recvs, local = [], [], []
        for a in range(n):
            local.append(pltpu.make_async_copy(x_refs[a].at[mine], out_refs[a].at[mine], local_sems.at[a]))
            for k, (dev, g) in enumerate(peers):
                sends.append(_remote(x_refs[a].at[g], out_refs[a].at[mine], send_sems, recv_sems, 3 * a + k, dev))
                recvs.append(_remote(x_refs[a].at[g], out_refs[a].at[g], send_sems, recv_sems, 3 * a + k, dev))
        return sends, recvs, local

    shapes = [jax.ShapeDtypeStruct(x.shape, x.dtype) for x in xs]
    return _copies_job(xs, shapes, 3 * n, n, make)


def _row_tile(r, pref):
    t = min(pref, r)
    while r % t or (t % 8 and t != r):
        t -= 1
    return t


def _adamw(parts, w, m, v, layer, *, name, tile=256, into=None):
    g, a, b = parts.shape
    tile = _row_tile(a, tile)
    c1 = 1.0 / (1.0 - ADAM_B1 ** ADAM_STEP)
    c2 = 1.0 / (1.0 - ADAM_B2 ** ADAM_STEP)
    into = tuple(into or ())

    def body(p_ref, w_ref, m_ref, v_ref, *refs):
        g_ref, d_ref, mo_ref, vo_ref = refs[len(into):]
        grad = p_ref[0].astype(F32)
        for j in range(1, g):
            grad = grad + p_ref[j].astype(F32)
        mn = ADAM_B1 * m_ref[...] + (1.0 - ADAM_B1) * grad
        vn = ADAM_B2 * v_ref[...] + (1.0 - ADAM_B2) * (grad * grad)
        g_ref[...] = grad
        mo_ref[...] = mn
        vo_ref[...] = vn
        d_ref[...] = -ADAM_LR * ((mn * c1) / (jnp.sqrt(vn * c2) + ADAM_EPS) + ADAM_WD * w_ref[...])

    if layer is None:
        src, shape = pl.BlockSpec((tile, b), lambda i: (i, 0)), (a, b)
    else:
        src, shape = pl.BlockSpec((None, tile, b), lambda i: (layer, i, 0)), w.shape
    return pl.pallas_call(
        body, name=name, grid=(a // tile,),
        in_specs=[pl.BlockSpec((g, tile, b), lambda i: (0, i, 0)), src, src, src] + [_ANY] * len(into),
        out_specs=[src] * 4,
        out_shape=[jax.ShapeDtypeStruct(shape, F32)] * 4,
        input_output_aliases={4 + i: i for i in range(len(into))},
        compiler_params=_cparams(("parallel",)),
    )(parts, w, m, v, *into)


W_IN_SHARD = 276


def _w_in_dest(col):
    return jnp.where(col < P_KR, col, jnp.where(col < P_KR + 256, col + (P_CKV - P_KR), col - 2176 + P_KR + HEAD))


PLACE_TILE = 384
PLACE_SHARDS = 3
PICK_TILE = 128
PICK_TILES = 4


def _w_in_tables():
    col = np.arange(N_DEV * W_IN_SHARD)
    dest = np.where(col < P_KR, col, np.where(col < P_KR + 256, col + (P_CKV - P_KR), col - 2176 + P_KR + HEAD))
    shard = col // W_IN_SHARD

    def filled(used, universe, n):
        used = sorted(set(int(u) for u in used))
        assert len(used) <= n, used
        return used + [u for u in universe if u not in used][:n - len(used)]

    place = [filled(shard[dest // PLACE_TILE == c], range(N_DEV), PLACE_SHARDS) for c in range(P_COLS // PLACE_TILE)]
    pick = [filled(dest[shard == j] // PICK_TILE, range(P_COLS // PICK_TILE), PICK_TILES) for j in range(N_DEV)]
    return np.asarray(place, np.int32).reshape(-1), np.asarray(pick, np.int32).reshape(-1)


def _place_w_in(g, *, name):
    _, d, sh = g.shape
    tc, ns = PLACE_TILE, PLACE_SHARDS
    table = jnp.asarray(_w_in_tables()[0])

    def body(tab_ref, g_ref, o_ref, acc_ref):
        ct, s = pl.program_id(0), pl.program_id(1)
        j = tab_ref[ct * ns + s]

        @pl.when(s == 0)
        def _():
            acc_ref[...] = jnp.zeros_like(acc_ref)

        src = j * sh + lax.broadcasted_iota(jnp.int32, (sh, tc), 0)
        dst = ct * tc + lax.broadcasted_iota(jnp.int32, (sh, tc), 1)
        place = (_w_in_dest(src) == dst).astype(BF16)
        acc_ref[...] += jnp.dot(g_ref[...], place, preferred_element_type=F32)

        @pl.when(s == ns - 1)
        def _():
            o_ref[...] = acc_ref[...].astype(o_ref.dtype)

    return pl.pallas_call(
        body, name=name,
        grid_spec=pltpu.PrefetchScalarGridSpec(
            num_scalar_prefetch=1, grid=(P_COLS // tc, ns),
            in_specs=[pl.BlockSpec((None, d, sh), lambda ct, s, tab: (tab[ct * ns + s], 0, 0))],
            out_specs=pl.BlockSpec((d, tc), lambda ct, s, tab: (0, ct)),
            scratch_shapes=[pltpu.VMEM((d, tc), F32)]),
        out_shape=jax.ShapeDtypeStruct((d, P_COLS), BF16),
        compiler_params=_cparams(("parallel", "arbitrary")),
    )(table, g)


def _unplace_w_in(dw, *, name):
    d = dw.shape[0]
    sh, tk, nt = W_IN_SHARD, PICK_TILE, PICK_TILES
    table = jnp.asarray(_w_in_tables()[1])

    def body(tab_ref, dw_ref, o_ref):
        j, kk = pl.program_id(0), pl.program_id(1)
        tile = tab_ref[j * nt + kk]
        src = j * sh + lax.broadcasted_iota(jnp.int32, (tk, sh), 1)
        dst = tile * tk + lax.broadcasted_iota(jnp.int32, (tk, sh), 0)
        pick = (_w_in_dest(src) == dst).astype(BF16)
        part = _split_dot(dw_ref[...], pick)

        @pl.when(kk == 0)
        def _():
            o_ref[...] = part

        @pl.when(kk > 0)
        def _():
            o_ref[...] += part

    return pl.pallas_call(
        body, name=name,
        grid_spec=pltpu.PrefetchScalarGridSpec(
            num_scalar_prefetch=1, grid=(N_DEV, nt),
            in_specs=[pl.BlockSpec((d, tk), lambda j, kk, tab: (0, tab[j * nt + kk]))],
            out_specs=pl.BlockSpec((None, d, sh), lambda j, kk, tab: (j, 0, 0))),
        out_shape=jax.ShapeDtypeStruct((N_DEV, d, sh), F32),
        compiler_params=_cparams(("parallel", "arbitrary")),
    )(table, dw)


def _gate_up_swiglu(h1, wgu, *, name):
    t_rows, k = h1.shape
    w = wgu.shape[2]
    tm = _tile(t_rows, 1024)

    def body(a_ref, wg_ref, wu_ref, gu_ref, act_ref):
        a = a_ref[...].astype(BF16)
        gate = jnp.dot(a, wg_ref[...], preferred_element_type=F32)
        up = jnp.dot(a, wu_ref[...], preferred_element_type=F32)
        gu_ref[0] = gate.astype(gu_ref.dtype)
        gu_ref[1] = up.astype(gu_ref.dtype)
        act_ref[...] = (gate * _sigmoid(gate) * up).astype(act_ref.dtype)

    return pl.pallas_call(
        body, name=name, grid=(t_rows // tm, 4),
        in_specs=[pl.BlockSpec((tm, k), lambda i, j: (i, 0)),
                  pl.BlockSpec((None, k, w), lambda i, j: (j, 0, 0)),
                  pl.BlockSpec((None, k, w), lambda i, j: (j + 4, 0, 0))],
        out_specs=[pl.BlockSpec((2, None, tm, w), lambda i, j: (0, j, i, 0)),
                   pl.BlockSpec((None, tm, w), lambda i, j: (j, i, 0))],
        out_shape=[jax.ShapeDtypeStruct((2, 4, t_rows, w), BF16), jax.ShapeDtypeStruct((4, t_rows, w), BF16)],
        compiler_params=_cparams(("parallel", "arbitrary")),
    )(h1, wgu, wgu)


def _down_dx_swiglu(dffn, wdown, gu, *, name):
    t_rows, k = dffn.shape
    w = gu.shape[3]
    tm = _tile(t_rows, 1024)

    def body(d_ref, w_ref, gu_ref, o_ref):
        dact = lax.dot_general(d_ref[...].astype(BF16), w_ref[...], _NT, preferred_element_type=F32)
        gate, up = gu_ref[0].astype(F32), gu_ref[1].astype(F32)
        sg = _sigmoid(gate)
        o_ref[0] = (dact * up * (sg * (1.0 + gate * (1.0 - sg)))).astype(o_ref.dtype)
        o_ref[1] = (dact * gate * sg).astype(o_ref.dtype)

    blk = pl.BlockSpec((2, None, tm, w), lambda i, j: (0, j, i, 0))
    return pl.pallas_call(
        body, name=name, grid=(t_rows // tm, 4),
        in_specs=[pl.BlockSpec((tm, k), lambda i, j: (i, 0)), pl.BlockSpec((w, k), lambda i, j: (j, 0)), blk],
        out_specs=blk, out_shape=jax.ShapeDtypeStruct(gu.shape, BF16),
        compiler_params=_cparams(("parallel", "arbitrary")),
    )(dffn, wdown, gu)


BIG = ("w_in", "mla_w_uq", "mla_w_ukv", "w_out", "w_gate_up", "w_down", "ple_w_gate", "ple_w_proj")
SMALL = ("ln_in_g", "ln_in_b", "hgrn_lb_logits", "hgrn_norm_g", "sgu_ln_g", "sgu_ln_b", "sgu_w_s", "sgu_b_s",
         "mla_q_norm_g", "mla_kv_norm_g", "ln1_g", "ln1_b", "ln2_g", "ln2_b")
ORDER = ("ln_in_g", "ln_in_b", "w_in", "hgrn_lb_logits", "hgrn_norm_g", "sgu_ln_g", "sgu_ln_b", "sgu_w_s", "sgu_b_s",
         "mla_q_norm_g", "mla_w_uq", "mla_kv_norm_g", "mla_w_ukv", "w_out", "ln1_g", "ln1_b", "w_gate_up", "w_down",
         "ple_w_gate", "ple_w_proj", "ln2_g", "ln2_b")


def _slab(a, align):
    s = a.reshape(-1, LANES)
    pad = -s.shape[0] % align
    return jnp.pad(s, ((0, pad), (0, 0))) if pad else s


def _pack(arrays, align=16, total_align=512):
    s = jnp.concatenate([_slab(a, align) for a in arrays], axis=0)
    pad = -s.shape[0] % total_align
    return jnp.pad(s, ((0, pad), (0, 0))) if pad else s


def _unpack(slab, shapes, align=16):
    out, r0 = [], 0
    for s in shapes:
        nr = math.prod(s) // LANES
        out.append(slab[r0:r0 + nr].reshape(s))
        r0 += nr + (-nr % align)
    return out


def _weight_shards(w, li):
    uq_pad = ((0, 0), (0, LANES - ATT_D))
    shards = {k: w[k][li] for k in BIG}
    shards["mla_w_uq"] = jnp.pad(shards["mla_w_uq"], uq_pad)
    return {k: s.astype(BF16) for k, s in shards.items()}


def _usable_weights(g, *, name):
    out = {}
    for k, a in g.items():
        if k == "w_in":
            out[k] = _place_w_in(a, name=name + "_place_w_in")
        elif k in ("w_out", "w_down", "ple_w_gate"):
            out[k] = a.reshape(a.shape[0] * a.shape[1], a.shape[2])
        else:
            out[k] = a
    return out


BY_COLUMNS = ("mla_w_uq", "mla_w_ukv", "ple_w_proj")


def _as_pairs(k, g):
    if k in BY_COLUMNS:
        return g
    if g.ndim == 2:
        return g.reshape((4, 2, g.shape[0] // N_DEV) + g.shape[1:])
    return g.reshape((4, 2) + g.shape[1:])


def _twice(fn):
    return lambda *a: fn(*a) * 2


def _layer_forward(li, h, hb, p_i, wts, sm, lbs, tables, alpha, hgrn_job=None, after_hgrn=None, attn_job=None,
                   after_attn=None, loss_target=None):
    n = f"l{li}_"
    row1 = lambda a: a.reshape(1, -1)
    projp = _mm(hb, wts["w_in"], name=n + "proj")
    ng = row1(sm["hgrn_norm_g"][li])
    res = _hgrn_fwd(projp, lbs[li], ng, name=n + "hgrn_fwd", job=hgrn_job)
    if hgrn_job is not None:
        res, got = res
    o_a, o_pre, states = res
    lg, lbias = row1(sm["sgu_ln_g"][li]), row1(sm["sgu_ln_b"][li])
    w_s = sm["sgu_w_s"][li]
    bias_full = jnp.repeat(sm["sgu_b_s"][li].T, HEAD, axis=1)
    o_b = _sgu_fwd(projp, lg, lbias, w_s, bias_full, name=n + "sgu_fwd",
                   job=None if hgrn_job is None else _forward_job(got))
    if hgrn_job is not None:
        o_b, got = o_b
        wts = dict(wts, **after_hgrn(got))
    qg, kvg = row1(sm["mla_q_norm_g"][li]), row1(sm["mla_kv_norm_g"][li])
    cq_view, ckv_view = (projp, 384, P_CQ // 384), (projp, 256, P_CKV // 256)
    (cqn,) = _rowwise(_fn_rms, [cq_view], [qg], [(384, BF16)], name=n + "q_norm")
    (ckvn,) = _rowwise(_fn_rms, [ckv_view], [kvg], [(256, BF16)], name=n + "kv_norm")
    q = _mm(cqn, wts["mla_w_uq"], name=n + "uq")
    kv = _mm(ckvn, wts["mla_w_ukv"], name=n + "ukv")
    qr, kf, kvb = _mla_prep(q, kv, projp, tables, name=n + "mla_prep")
    res = _attn_fwd(qr, kf, kvb, name=n + "attn_fwd", job=attn_job)
    if attn_job is not None:
        res, got = res
    o_c, lse = res
    cat = jnp.concatenate([o_a, o_b, o_c.astype(BF16)], axis=1)
    mix = _mm(cat, wts["w_out"], name=n + "out_proj", job=None if attn_job is None else _forward_job(got))
    if attn_job is not None:
        mix, got = mix
        wts = dict(wts, **after_attn(got))
    g1, b1 = row1(sm["ln1_g"][li]), row1(sm["ln1_b"][li])
    d = h.shape[1]
    h1, h1b = _rowwise(_twice(_make_post_mix(alpha)), [h, mix], [g1, b1], [(d, F32), (d, BF16)], name=n + "ln1")
    gu, act = _gate_up_swiglu(h1b, wts["w_gate_up"], name=n + "gate_up")
    ffn = _mm(act, wts["w_down"], am="bmk", tm=2048, name=n + "down")
    pg = _mm(h1b, wts["ple_w_gate"], name=n + "ple_gate")
    pp = _mm(p_i, wts["ple_w_proj"], name=n + "ple_proj")
    g2, b2 = row1(sm["ln2_g"][li]), row1(sm["ln2_b"][li])
    if loss_target is None:
        out = _rowwise(_twice(_make_ple_ln(alpha)), [h1, ffn, pg, pp], [g2, b2], [(d, F32), (d, BF16)],
                       name=n + "ln2")
    else:
        def ln_and_loss(h1v, ffnv, pgv, ppv, tv, gv, bv):
            err = _make_ple_ln(alpha)(h1v, ffnv, pgv, ppv, gv, bv)[0] - tv
            return err * (1.0 / d), 0.5 * jnp.sum(jnp.mean(err * err, axis=-1, keepdims=True), axis=0, keepdims=True)

        out = _rowwise(ln_and_loss, [h1, ffn, pg, pp, loss_target], [g2, b2], [(d, F32)], accs=[(1, 1)],
                       name=n + "ln2_loss")
    saved = dict(h=h, hb=hb, h1b=h1b, projp=projp, o_pre=o_pre, states=states, cqn=cqn, ckvn=ckvn, qr=qr, kf=kf, kvb=kvb, o_c=o_c,
                 lse=lse, cat=cat, mix=mix, h1=h1, gu=gu, act=act, ffn=ffn, pg=pg, pp=pp, ng=ng, lg=lg, wts=wts,
                 lbias=lbias, w_s=w_s, bias_full=bias_full, qg=qg, kvg=kvg, g1=g1, b1=b1, g2=g2, b2=b2)
    return tuple(out), saved


RS_EARLY = ("ple_w_proj", "ple_w_gate", "w_down", "w_gate_up", "w_out")
RS_LATE = ("mla_w_uq", "mla_w_ukv", "w_in")


def _layer_backward(li, dh2_parts, p_i, sv, lbs, tables, alpha, core, carried=None):
    n = f"l{li}_b_"
    wts = sv["wts"]
    gr = {}
    dh1_a, dffn, dpg, dpp, gr["ln2_g"], gr["ln2_b"] = _rowwise_vjp(
        _make_ple_ln(alpha), [sv["h1"], sv["ffn"], sv["pg"], sv["pp"]], [sv["g2"], sv["b2"]], [dh2_parts],
        groups=[[0], [1], [2], [3]], gdtypes=[F32, BF16, BF16, BF16], name=n + "ln2")
    big = {}
    big["ple_w_proj"] = _mm(p_i, dpp, am="km", tk=2048, name=n + "ple_proj_dw")
    big["ple_w_gate"] = _mm(sv["h1b"], dpg, am="km", name=n + "ple_gate_dw")
    dh1_b = _mm(dpg, wts["ple_w_gate"], bm="nk", name=n + "ple_gate_dx")
    big["w_down"] = _mm(sv["act"], dffn, am="bkm", tk=4096, name=n + "down_dw")
    dgu = _down_dx_swiglu(dffn, wts["w_down"], sv["gu"], name=n + "down_dx")
    dgu = dgu.reshape((N_DEV,) + dgu.shape[2:])
    big["w_gate_up"], carried_got = _mm(sv["h1b"], dgu, am="km", bm="bkn", om="bmn", tk=4096, name=n + "gate_up_dw",
                                        job=carried), None
    if carried is not None:
        big["w_gate_up"], carried_got = big["w_gate_up"]
    early = [_as_pairs(k, big[k]) for k in RS_EARLY[:-1]]
    dh1_c, theirs = _mm(dgu, wts["w_gate_up"], am="bmk", bm="bnk", tm=2048, name=n + "gate_up_dx",
                        job=_pair_job(early))
    dh_a, dmix, gr["ln1_g"], gr["ln1_b"] = _rowwise_vjp(
        _make_post_mix(alpha), [sv["h"], sv["mix"]], [sv["g1"], sv["b1"]], [[dh1_a, dh1_b, dh1_c]],
        groups=[[0], [1]], gdtypes=[F32, BF16], name=n + "ln1")
    big["w_out"] = _mm(sv["cat"], dmix, am="km", name=n + "out_proj_dw")
    early.append(_as_pairs("w_out", big["w_out"]))
    dcat, their_w_out = _mm(dmix, wts["w_out"], bm="nk", name=n + "out_proj_dx", job=_pair_job(early[-1:]))
    sums = [_pair_add(x, r, core, name=n + "pair_add_" + k)
            for k, x, r in zip(RS_EARLY, early, list(theirs) + list(their_w_out))]

    (dqr, dkv, dkf), early_quads = _attn_bwd(sv["qr"], sv["kf"], sv["kvb"], dcat, sv["o_c"], sv["lse"],
                                             name=n + "attn", job=_quad_job(sums))
    dqpad, dkr = _mla_prep_bwd(dqr, dkf, tables, name=n + "mla_prep")
    big["mla_w_uq"] = _mm(sv["cqn"], dqpad, am="km", tk=2048, name=n + "uq_dw")
    dcqn = _mm(dqpad, wts["mla_w_uq"], bm="nk", name=n + "uq_dx")
    big["mla_w_ukv"] = _mm(sv["ckvn"], dkv, am="km", tk=2048, name=n + "ukv_dw")
    dckvn = _mm(dkv, wts["mla_w_ukv"], bm="nk", name=n + "ukv_dx")
    projp = sv["projp"]
    dcq, gr["mla_q_norm_g"] = _rowwise_vjp(_fn_rms, [(projp, 384, P_CQ // 384)], [sv["qg"]], [[dcqn]],
                                           groups=[[0]], gdtypes=[BF16], name=n + "q_norm")
    dckv, gr["mla_kv_norm_g"] = _rowwise_vjp(_fn_rms, [(projp, 256, P_CKV // 256)], [sv["kvg"]], [[dckvn]],
                                             groups=[[0]], gdtypes=[BF16], name=n + "kv_norm")
    dsgu, gr["sgu_ln_g"], gr["sgu_ln_b"], gr["sgu_w_s"], gr["sgu_b_s"] = _sgu_bwd(
        projp, sv["lg"], sv["lbias"], sv["w_s"], sv["bias_full"], dcat, name=n + "sgu")
    dhg, gr["hgrn_norm_g"], gr["lower_bound"] = _hgrn_bwd(
        projp, lbs[li], sv["ng"], sv["o_pre"], sv["states"], dcat, name=n + "hgrn")
    dprojp = jnp.concatenate([dhg, dsgu, dcq, dkr, dckv], axis=1)
    big["w_in"] = _unplace_w_in(_mm(sv["hb"], dprojp, am="km", tk=4096, name=n + "proj_dw"),
                                name=n + "proj_dw_shards")
    late = [_as_pairs(k, big[k]) for k in RS_LATE]
    dh_b, theirs = _mm(dprojp, wts["w_in"], bm="nk", name=n + "proj_dx", job=_pair_job(late))
    late_sums = [_pair_add(x, r, core, name=n + "pair_add_" + k) for k, x, r in zip(RS_LATE, late, theirs)]
    return [dh_a, dh_b], gr, early_quads, late_sums, carried_got


def kernel(x, p, positions, ln_in_g, ln_in_b, w_in, hgrn_lb_logits, hgrn_norm_g, sgu_ln_g, sgu_ln_b, sgu_w_s, sgu_b_s, mla_q_norm_g, mla_w_uq, mla_kv_norm_g, mla_w_ukv, w_out, ln1_g, ln1_b, w_gate_up, w_down, ple_w_gate, ple_w_proj, ln2_g, ln2_b, loss_target, m_ln_in_g, m_ln_in_b, m_w_in, m_hgrn_lb_logits, m_hgrn_norm_g, m_sgu_ln_g, m_sgu_ln_b, m_sgu_w_s, m_sgu_b_s, m_mla_q_norm_g, m_mla_w_uq, m_mla_kv_norm_g, m_mla_w_ukv, m_w_out, m_ln1_g, m_ln1_b, m_w_gate_up, m_w_down, m_ple_w_gate, m_ple_w_proj, m_ln2_g, m_ln2_b, v_ln_in_g, v_ln_in_b, v_w_in, v_hgrn_lb_logits, v_hgrn_norm_g, v_sgu_ln_g, v_sgu_ln_b, v_sgu_w_s, v_sgu_b_s, v_mla_q_norm_g, v_mla_w_uq, v_mla_kv_norm_g, v_mla_w_ukv, v_w_out, v_ln1_g, v_ln1_b, v_w_gate_up, v_w_down, v_ple_w_gate, v_ple_w_proj, v_ln2_g, v_ln2_b):
    args = dict(locals())
    w = {k: args[k] for k in ORDER}
    m = {k: args["m_" + k] for k in ORDER}
    v = {k: args["v_" + k] for k in ORDER}
    depth = w_in.shape[0]
    assert depth == 2, "the lower-bound kernel is written for two layers"
    alpha = (2 * depth) ** 0.25
    xs, tgt = x[0], loss_target[0]
    d_model = xs.shape[1]

    shards = [_weight_shards(w, li) for li in range(depth)]
    on_hgrn0 = ("mla_w_uq", "mla_w_ukv", "w_out", "ple_w_gate", "ple_w_proj")
    ffn0 = ("w_gate_up", "w_down")
    first1 = ("w_in", "mla_w_uq", "mla_w_ukv", "w_out")
    on_attn1 = ("w_gate_up", "w_down", "ple_w_gate", "ple_w_proj")
    layer1_first = {}

    def after_hgrn0(got):
        return _usable_weights(dict(zip(on_hgrn0, got)), name="l0")

    def after_attn0(got):
        layer1_first.update(_usable_weights(dict(zip(first1, got[len(ffn0):])), name="l1"))
        return _usable_weights(dict(zip(ffn0, got[:len(ffn0)])), name="l0")

    def after_attn1(got):
        return _usable_weights(dict(zip(on_attn1, got)), name="l1")

    tables = _rope_tables(positions[0])
    row1 = lambda a: a.reshape(1, -1)
    l0, l1 = row1(hgrn_lb_logits[0]), row1(hgrn_lb_logits[1])
    lbs = _rowwise(_fn_lower_bounds, [l0, l1], [], [(HG_W, F32), (HG_W, F32)], name="lower_bounds")

    gin, bin_ = row1(ln_in_g), row1(ln_in_b)
    (h, hb), g_in = _rowwise(_twice(_fn_ln), [xs], [gin, bin_], [(d_model, F32), (d_model, BF16)], name="ln_in",
                             job=_gather_job([shards[0]["w_in"]]))
    w_in0 = _usable_weights({"w_in": _gather_forward(g_in, name="gather_l0_w_in_forward")[0]}, name="l0")
    (h, hb), sv0 = _layer_forward(
        0, h, hb, p[0, 0], w_in0, w, lbs, tables, alpha,
        hgrn_job=_gather_job([shards[0][k] for k in on_hgrn0]), after_hgrn=after_hgrn0,
        attn_job=_gather_job([shards[0][k] for k in ffn0] + [shards[1][k] for k in first1]), after_attn=after_attn0)
    (dy, loss_local), sv1 = _layer_forward(
        1, h, hb, p[1, 0], layer1_first, w, lbs, tables, alpha,
        attn_job=_gather_job([shards[1][k] for k in on_attn1]), after_attn=after_attn1, loss_target=tgt)
    saved = [sv0, sv1]
    loss = lax.psum(loss_local[0, 0], ("x", "y", "c"))

    core = lax.axis_index("c").astype(jnp.int32).reshape(1)
    dparts, grads, quads, carried = [dy], [None] * depth, [None] * depth, None
    for li in reversed(range(depth)):
        dparts, grads[li], early_quads, late_sums, late_quads = _layer_backward(
            li, dparts, p[li, 0], saved[li], lbs, tables, alpha, core, carried=carried)
        quads[li] = dict(zip(RS_EARLY, early_quads))
        if carried is not None:
            quads[li + 1].update(zip(RS_LATE, late_quads))
        carried = _quad_job(late_sums)
    (dx, d_gin, d_bin), late_quads = _rowwise_vjp(_fn_ln, [xs], [gin, bin_], [dparts], groups=[[0]], name="ln_in_b",
                                                   job=carried)
    quads[0].update(zip(RS_LATE, late_quads))
    dl0, dl1 = _rowwise_vjp(_fn_lower_bounds, [l0, l1], [], [[grads[0]["lower_bound"]], [grads[1]["lower_bound"]]],
                            groups=[[0], [1]], name="lower_bounds_b")

    prefixes = ("grad_", "delta_", "new_m_", "new_v_")
    uq_pad = ((0, 0), (0, 0), (0, LANES - ATT_D))
    state = {k: ((jnp.pad(w[k], uq_pad), jnp.pad(m[k], uq_pad), jnp.pad(v[k], uq_pad)) if k == "mla_w_uq"
                 else (w[k], m[k], v[k])) for k in BIG}
    out = {}
    for k in BIG:
        res4 = None
        for li in range(depth):
            res4 = _adamw(quads[li][k], *state[k], li, name=f"adamw_l{li}_{k}", into=res4)
        for pre, a in zip(prefixes, res4):
            out[pre + k] = a[:, :, :ATT_D] if k == "mla_w_uq" else a

    small_g = {"ln_in_g": d_gin.reshape(-1), "ln_in_b": d_bin.reshape(-1),
               "hgrn_lb_logits": jnp.stack([dl0.reshape(-1), dl1.reshape(-1)])}
    for k in SMALL[3:]:
        small_g[k] = jnp.stack([grads[li][k].reshape(w[k].shape[1:]) for li in range(depth)])
    (small_parts,) = _all_gather([_pack([small_g[k] for k in SMALL])], name="gather_small_grads", columns=False)
    slabs = _adamw(small_parts, _pack([w[k] for k in SMALL]), _pack([m[k] for k in SMALL]),
                   _pack([v[k] for k in SMALL]), None, name="adamw_small")
    shapes = [w[k].shape for k in SMALL]
    for pre, slab in zip(prefixes, slabs):
        for k, a in zip(SMALL, _unpack(slab, shapes)):
            out[pre + k] = a
    res = [loss, dx[None]]
    for prefix in ("grad_", "delta_", "new_m_", "new_v_"):
        res += [out[prefix + k] for k in ORDER]
    return tuple(res)
```

```python
import functools
import math

import jax
import jax.numpy as jnp
import numpy as np
from jax import lax
from jax.experimental import pallas as pl
from jax.experimental.pallas import tpu as pltpu

F32 = jnp.float32
BF16 = jnp.bfloat16
MESH = pl.DeviceIdType.MESH

LN_EPS = 1e-5
RMS_EPS = 1e-6
ROPE_THETA = 10000.0
ADAM_LR, ADAM_B1, ADAM_B2, ADAM_EPS, ADAM_WD, ADAM_STEP = 0.001, 0.9, 0.999, 1e-08, 0.01, 10

N_DEV = 8
LANES = 128
HG_CHUNK = 16
HG_W = 256
HEAD = 64
SGU_CHUNK = 128
N_ATT_HEADS = 8
ATT_D = 96
VMEM_LIMIT = 56 * 1024 * 1024

HG_TILE = 256
ATT_TQ = 512
ROW_TILE = 256

P_CQ, P_KR, P_CKV, P_COLS = 1536, 1920, 2048, 2304


def _cparams(sem):
    return pltpu.CompilerParams(dimension_semantics=sem, vmem_limit_bytes=VMEM_LIMIT)


_ANY = pl.BlockSpec(memory_space=pl.ANY)


def _call(body, operands, *, name, grid, in_specs, out_specs, out_shape, sem, scratch_shapes=(), job=None):
    if job is None:
        return pl.pallas_call(body, name=name, grid=grid, in_specs=in_specs, out_specs=out_specs, out_shape=out_shape,
                              scratch_shapes=list(scratch_shapes), compiler_params=_cparams(sem))(*operands)
    single = not isinstance(out_shape, (list, tuple))
    shapes = [out_shape] if single else list(out_shape)
    ospecs = [out_specs] if single else list(out_specs)
    ni, no, ns = len(operands), len(shapes), len(scratch_shapes)
    ji, jo = len(job.inputs), len(job.out_shapes)

    def hosted(*refs):
        p = 0
        parts = []
        for cnt in (ni, ji, no, jo, ns):
            parts.append(refs[p:p + cnt])
            p += cnt
        ins, jins, outs, jouts, scr = parts
        jsems = refs[p:]
        ids = [pl.program_id(a) for a in range(len(grid))]
        first = functools.reduce(lambda a, b: a & b, [i == 0 for i in ids])
        last = functools.reduce(lambda a, b: a & b, [i == g - 1 for i, g in zip(ids, grid)])

        @pl.when(first)
        def _():
            job.start(jins, jouts, jsems)

        body(*ins, *outs, *scr)

        @pl.when(last)
        def _():
            job.finish(jins, jouts, jsems)

    res = pl.pallas_call(
        hosted, name=name, grid=grid,
        in_specs=list(in_specs) + [_ANY] * ji, out_specs=ospecs + [_ANY] * jo,
        out_shape=shapes + list(job.out_shapes),
        scratch_shapes=list(scratch_shapes) + [pltpu.SemaphoreType.DMA((c,)) for c in job.sem_counts],
        input_output_aliases=job.aliases(ni, no),
        compiler_params=_cparams(("arbitrary",) * len(grid)),
    )(*operands, *job.inputs)
    own = res[0] if single else res[:no]
    return own, res[no:]


class _Job:
    def __init__(self, inputs, out_shapes, sem_counts, start, finish, in_place=False):
        self.inputs, self.out_shapes, self.sem_counts = list(inputs), list(out_shapes), list(sem_counts)
        self.start, self.finish, self.in_place = start, finish, in_place

    def aliases(self, first_in, first_out):
        return {first_in + i: first_out + i for i in range(len(self.inputs))} if self.in_place else {}


def _copies_job(inputs, out_shapes, n_remote, n_local, make, in_place=False):
    def start(jins, jouts, sems):
        sends, _, local = make(jins, jouts, *sems)
        for cp in local + sends:
            cp.start()

    def finish(jins, jouts, sems):
        sends, recvs, local = make(jins, jouts, *sems)
        for cp in recvs:
            cp.wait_recv()
        for cp in sends:
            cp.wait_send()
        for cp in local:
            cp.wait()

    return _Job(inputs, out_shapes, [n_remote, n_remote, max(n_local, 1)], start, finish, in_place)


def _run_job(job, *, name):
    ji, jo = len(job.inputs), len(job.out_shapes)

    def body(*refs):
        jins, jouts, sems = refs[:ji], refs[ji:ji + jo], refs[ji + jo:]
        job.start(jins, jouts, sems)
        job.finish(jins, jouts, sems)

    return pl.pallas_call(
        body, name=name, out_shape=list(job.out_shapes), in_specs=[_ANY] * ji, out_specs=[_ANY] * jo,
        scratch_shapes=[pltpu.SemaphoreType.DMA((c,)) for c in job.sem_counts],
        input_output_aliases=job.aliases(0, 0),
    )(*job.inputs)


def _tile(n, pref):
    if n % pref == 0:
        return pref
    best = None
    t = LANES
    while t <= min(n, pref):
        if n % t == 0:
            best = t
        t += LANES
    return best if best is not None else n


def _mm(a, b, *, am="mk", bm="kn", om="mn", out_dtype=F32, tm=1024, tn=1024, tk=1024, name, job=None):
    if am == "mk":
        m, k = a.shape
    elif am == "km":
        k, m = a.shape
    elif am == "bmk":
        m, tk = a.shape[1], a.shape[2]
        k = a.shape[0] * tk
    else:
        k, tm = a.shape[1], a.shape[2]
        m = a.shape[0] * tm
    if bm == "kn":
        kb_, n = b.shape
    elif bm == "nk":
        n, kb_ = b.shape
    elif bm == "bkn":
        kb_, tn = b.shape[1], b.shape[2]
        n = b.shape[0] * tn
    else:
        n, tk = b.shape[1], b.shape[2]
        kb_ = b.shape[0] * tk
    assert kb_ == k, (a.shape, b.shape, am, bm)
    tm, tn, tk = _tile(m, tm), _tile(n, tn), _tile(k, tk)
    nk = k // tk
    dims = (((0 if am in ("km", "bkm") else 1,), (1 if bm in ("nk", "bnk") else 0,)), ((), ()))

    a_spec = {"mk": pl.BlockSpec((tm, tk), lambda i, j, kk: (i, kk)),
              "km": pl.BlockSpec((tk, tm), lambda i, j, kk: (kk, i)),
              "bmk": pl.BlockSpec((None, tm, tk), lambda i, j, kk: (kk, i, 0)),
              "bkm": pl.BlockSpec((None, tk, tm), lambda i, j, kk: (i, kk, 0))}[am]
    b_spec = {"kn": pl.BlockSpec((tk, tn), lambda i, j, kk: (kk, j)),
              "nk": pl.BlockSpec((tn, tk), lambda i, j, kk: (j, kk)),
              "bkn": pl.BlockSpec((None, tk, tn), lambda i, j, kk: (j, kk, 0)),
              "bnk": pl.BlockSpec((None, tn, tk), lambda i, j, kk: (kk, j, 0))}[bm]
    if om == "mn":
        o_spec, o_shape = pl.BlockSpec((tm, tn), lambda i, j, kk: (i, j)), (m, n)
    else:
        o_spec, o_shape = pl.BlockSpec((None, tm, tn), lambda i, j, kk: (j, i, 0)), (n // tn, m, tn)

    def body(a_ref, b_ref, o_ref, *acc):
        kk = pl.program_id(2)

        def prod():
            return lax.dot_general(a_ref[...].astype(BF16), b_ref[...].astype(BF16), dims, preferred_element_type=F32)

        if nk == 1:
            o_ref[...] = prod().astype(o_ref.dtype)
            return
        acc_ref, = acc

        @pl.when(kk == 0)
        def _():
            acc_ref[...] = prod()

        if nk > 2:
            @pl.when((kk > 0) & (kk < nk - 1))
            def _():
                acc_ref[...] += prod()

        @pl.when(kk == nk - 1)
        def _():
            o_ref[...] = (acc_ref[...] + prod()).astype(o_ref.dtype)

    return _call(body, (a, b), name=name, grid=(m // tm, n // tn, nk), in_specs=[a_spec, b_spec], out_specs=o_spec,
                 out_shape=jax.ShapeDtypeStruct(o_shape, out_dtype),
                 scratch_shapes=[pltpu.VMEM((tm, tn), F32)] if nk > 1 else [],
                 sem=("parallel", "parallel", "arbitrary"), job=job)


def _row_operand(a, tile):
    if isinstance(a, tuple):
        arr, w, j = a
        return arr, pl.BlockSpec((tile, w), lambda i, j=j: (i, j))
    return a, pl.BlockSpec((tile, a.shape[1]), lambda i: (i, 0))


def _const_spec(c):
    nd = c.ndim
    return pl.BlockSpec(c.shape, lambda i, nd=nd: (0,) * nd)


def _rowwise(fn, rows, consts, outs, *, name, accs=(), tile=None, job=None):
    t_rows = (rows[0][0] if isinstance(rows[0], tuple) else rows[0]).shape[0]
    tile = min(tile or ROW_TILE, t_rows)
    arrs, specs = zip(*[_row_operand(a, tile) for a in rows])
    nin, no = len(rows) + len(consts), len(outs)

    def body(*refs):
        res = fn(*[r[...] for r in refs[:nin]])
        for r, v in zip(refs[nin:nin + no], res[:no]):
            r[...] = v.astype(r.dtype)
        if accs:
            a_refs = refs[nin + no:]

            @pl.when(pl.program_id(0) == 0)
            def _():
                for r in a_refs:
                    r[...] = jnp.zeros_like(r)

            for r, v in zip(a_refs, res[no:]):
                r[...] += v

    out_shape = [jax.ShapeDtypeStruct((t_rows, w), dt) for w, dt in outs]
    out_shape += [jax.ShapeDtypeStruct(s, F32) for s in accs]
    out_specs = [pl.BlockSpec((tile, w), lambda i: (i, 0)) for w, _ in outs]
    out_specs += [pl.BlockSpec(s, lambda i, nd=len(s): (0,) * nd) for s in accs]
    return _call(body, (*arrs, *consts), name=name, grid=(t_rows // tile,),
                 in_specs=list(specs) + [_const_spec(c) for c in consts],
                 out_specs=out_specs, out_shape=out_shape, sem=("arbitrary",), job=job)


def _rowwise_vjp(fn, rows, consts, cts, *, name, groups, tile=None, gdtypes=None, job=None):
    t_rows = (rows[0][0] if isinstance(rows[0], tuple) else rows[0]).shape[0]
    tile = min(tile or ROW_TILE, t_rows)
    arrs, specs = zip(*[_row_operand(a, tile) for a in rows])
    flat_cts = [c for group in cts for c in group]
    ct_arrs, ct_specs = zip(*[_row_operand(a, tile) for a in flat_cts])
    nr, nc, nct, ng = len(rows), len(consts), len(flat_cts), len(groups)

    def width(a):
        return a[1] if isinstance(a, tuple) else a.shape[1]

    def body(*refs):
        rv = [r[...].astype(F32) for r in refs[:nr]]
        cv = [r[...] for r in refs[nr:nr + nc]]
        ct_refs = refs[nr + nc:nr + nc + nct]
        ctv, pos = [], 0
        for group in cts:
            s = ct_refs[pos][...].astype(F32)
            for r in ct_refs[pos + 1:pos + len(group)]:
                s = s + r[...].astype(F32)
            ctv.append(s)
            pos += len(group)
        _, pull = jax.vjp(fn, *rv, *cv)
        grads = pull(tuple(ctv))
        g_refs = refs[nr + nc + nct:nr + nc + nct + ng]
        for r, idx in zip(g_refs, groups):
            parts = [grads[i] for i in idx]
            r[...] = (parts[0] if len(parts) == 1 else jnp.concatenate(parts, axis=1)).astype(r.dtype)
        c_refs = refs[nr + nc + nct + ng:]

        @pl.when(pl.program_id(0) == 0)
        def _():
            for r in c_refs:
                r[...] = jnp.zeros_like(r)

        for r, v in zip(c_refs, grads[nr:]):
            r[...] += v

    gw = [sum(width(rows[i]) for i in idx) for idx in groups]
    gdtypes = gdtypes or [F32] * ng
    out_shape = [jax.ShapeDtypeStruct((t_rows, w), dt) for w, dt in zip(gw, gdtypes)]
    out_shape += [jax.ShapeDtypeStruct(c.shape, F32) for c in consts]
    out_specs = [pl.BlockSpec((tile, w), lambda i: (i, 0)) for w in gw]
    out_specs += [_const_spec(c) for c in consts]
    return _call(body, (*arrs, *consts, *ct_arrs), name=name, grid=(t_rows // tile,),
                 in_specs=list(specs) + [_const_spec(c) for c in consts] + list(ct_specs),
                 out_specs=out_specs, out_shape=out_shape, sem=("arbitrary",), job=job)


def _layer_norm(x, g, b):
    mu = jnp.mean(x, axis=-1, keepdims=True)
    xc = x - mu
    var = jnp.mean(xc * xc, axis=-1, keepdims=True)
    return xc * lax.rsqrt(var + LN_EPS) * g + b


def _sigmoid(x):
    return 1.0 / (1.0 + jnp.exp(-x))


def _fn_ln(x, g, b):
    return (_layer_norm(x, g, b),)


def _fn_rms(x, g):
    return (x * lax.rsqrt(jnp.mean(x * x, axis=-1, keepdims=True) + RMS_EPS) * g,)


def _make_post_mix(alpha):
    def fn(h, mix, g, b):
        return (_layer_norm(alpha * h + mix, g, b),)
    return fn


def _make_ple_ln(alpha):
    def fn(h1, ffn, pg, pp, g, b):
        return (_layer_norm(alpha * h1 + ffn + _sigmoid(pg) * pp, g, b),)
    return fn


def _fn_lower_bounds(l0, l1):
    m = jnp.maximum(l0, l1)
    e0, e1 = jnp.exp(l0 - m), jnp.exp(l1 - m)
    s = e0 + e1
    p0, p1 = e0 / s, e1 / s
    return (p0 - p0, (p0 + p1) - p0)


def _split_dot(x, e_bf16):
    hi = x.astype(BF16)
    lo = (x - hi.astype(F32)).astype(BF16)
    return (jnp.dot(hi, e_bf16, preferred_element_type=F32) + jnp.dot(lo, e_bf16, preferred_element_type=F32))


def _hgrn_common(th):
    rm = lax.broadcasted_iota(jnp.int32, (th, HG_W), 0) % HG_CHUNK

    def seg_cumsum(x):
        for s in (1, 2, 4, 8):
            x = x + jnp.where(rm >= s, pltpu.roll(x, s, 0), 0.0)
        return x

    def seg_rcumsum(x):
        for s in (1, 2, 4, 8):
            x = x + jnp.where(rm < HG_CHUNK - s, pltpu.roll(x, th - s, 0), 0.0)
        return x

    ri = lax.broadcasted_iota(jnp.int32, (HG_W, HG_W), 0) // HEAD
    ci = lax.broadcasted_iota(jnp.int32, (HG_W, HG_W), 1) // HEAD
    head_f32 = (ri == ci).astype(F32)
    head_bf16 = head_f32.astype(BF16)

    def headsum(x, pieces=2):
        if pieces == 1:
            return jnp.dot(x.astype(BF16), head_bf16, preferred_element_type=F32)
        return _split_dot(x, head_bf16)

    return rm, seg_cumsum, seg_rcumsum, head_f32, headsum


def _hgrn_gates(qr, fl, lb):
    sg = _sigmoid(fl)
    f = lb + (1.0 - lb) * sg
    sq = _sigmoid(qr)
    return sg, f, jnp.log(f), 1.0 - f, qr * sq, sq


def _shifted(x, d, th):
    return x if d == 0 else pltpu.roll(x, d, 0)


def _unshift(x, d, th):
    return x if d == 0 else pltpu.roll(x, th - d, 0)


def _hgrn_fwd(projp, lb, ng, *, name, job=None):
    t_rows = projp.shape[0]
    th = min(HG_TILE, t_rows)
    nct = th // HG_CHUNK

    def body(q_ref, f_ref, i_ref, g_ref, lb_ref, ng_ref, oa_ref, opre_ref, st_out_ref,
             st_ref, vtm_ref, kv_ref, qe_ref, dec_ref, oint_ref):
        rm, seg_cumsum, seg_rcumsum, head_f32, headsum = _hgrn_common(th)

        @pl.when(pl.program_id(0) == 0)
        def _():
            st_ref[...] = jnp.zeros_like(st_ref)

        qr, fl, v, g = q_ref[...], f_ref[...], i_ref[...], g_ref[...]
        _, f, lf, k, q, _ = _hgrn_gates(qr, fl, lb_ref[...])
        b = seg_cumsum(lf)

        o = jnp.zeros((th, HG_W), F32)
        for d in range(HG_CHUNK):
            kd, bd, vd = _shifted(k, d, th), _shifted(b, d, th), _shifted(v, d, th)
            e = jnp.exp(jnp.where(rm >= d, b - bd, -1e30))
            o = o + headsum(q * kd * e, 1) * vd

        blast = seg_rcumsum(jnp.where(rm == HG_CHUNK - 1, b, 0.0))
        kte = (k * jnp.exp(blast - b)).astype(BF16)
        qe_ref[...] = q * jnp.exp(b)
        dec_ref[...] = jnp.exp(blast)
        vt = v.T
        lane_chunk = lax.broadcasted_iota(jnp.int32, (HG_W, th), 1) // HG_CHUNK
        for c in range(nct):
            vtm_ref[c * HG_W:(c + 1) * HG_W, :] = jnp.where(lane_chunk == c, vt, 0.0).astype(BF16)
        kv_ref[...] = jnp.dot(vtm_ref[...], kte, preferred_element_type=F32)

        s = st_ref[...]
        for c in range(nct):
            rows = slice(c * HG_CHUNK, (c + 1) * HG_CHUNK)
            st_out_ref[c] = s
            oint_ref[rows, :] = lax.dot_general(qe_ref[rows, :].astype(BF16), s.astype(BF16),
                                                (((1,), (1,)), ((), ())), preferred_element_type=F32)
            dec = jnp.max(dec_ref[rows, :], axis=0, keepdims=True)
            s = s * dec + kv_ref[c * HG_W:(c + 1) * HG_W, :] * head_f32
        st_ref[...] = s

        o = o + oint_ref[...]
        opre_ref[...] = o
        r = lax.rsqrt(headsum(o * o) * (1.0 / HEAD) + RMS_EPS)
        oa_ref[...] = (o * r * ng_ref[...] * (g * _sigmoid(g))).astype(oa_ref.dtype)

    col = lambda j: pl.BlockSpec((th, HG_W), lambda i, j=j: (i, j))
    vec = pl.BlockSpec((1, HG_W), lambda i: (0, 0))
    row = pl.BlockSpec((th, HG_W), lambda i: (i, 0))
    n_chunks = t_rows // HG_CHUNK
    return _call(
        body, (projp, projp, projp, projp, lb, ng), name=name, grid=(t_rows // th,),
        in_specs=[col(0), col(1), col(2), col(3), vec, vec],
        out_specs=[row, row, pl.BlockSpec((nct, HG_W, HG_W), lambda i: (i, 0, 0))],
        out_shape=[jax.ShapeDtypeStruct((t_rows, HG_W), BF16), jax.ShapeDtypeStruct((t_rows, HG_W), F32),
                   jax.ShapeDtypeStruct((n_chunks, HG_W, HG_W), F32)],
        scratch_shapes=[pltpu.VMEM((HG_W, HG_W), F32), pltpu.VMEM((nct * HG_W, th), BF16),
                        pltpu.VMEM((nct * HG_W, HG_W), F32), pltpu.VMEM((th, HG_W), F32),
                        pltpu.VMEM((th, HG_W), F32), pltpu.VMEM((th, HG_W), F32)],
        sem=("arbitrary",), job=job)


def _hgrn_bwd(projp, lb, ng, opre, states, dcat, *, name):
    t_rows = projp.shape[0]
    th = min(HG_TILE, t_rows)
    nct = th // HG_CHUNK
    nt = t_rows // th

    def body(q_ref, f_ref, i_ref, g_ref, lb_ref, ng_ref, opre_ref, st_in_ref, do_ref,
             dproj_ref, dng_ref, dlb_ref,
             gst_ref, dotm_ref, qg_ref, v_ref, kte_ref, dop_ref, dec_ref, dkte_ref, dvi_ref, dqe_ref, ddec_ref):
        rm, seg_cumsum, seg_rcumsum, head_f32, headsum = _hgrn_common(th)

        @pl.when(pl.program_id(0) == 0)
        def _():
            gst_ref[...] = jnp.zeros_like(gst_ref)
            dng_ref[...] = jnp.zeros_like(dng_ref)
            dlb_ref[...] = jnp.zeros_like(dlb_ref)

        qr, fl, v, g = q_ref[...], f_ref[...], i_ref[...], g_ref[...]
        lb, ngv = lb_ref[...], ng_ref[...]
        sg, f, lf, k, q, sq = _hgrn_gates(qr, fl, lb)
        b = seg_cumsum(lf)
        blast = seg_rcumsum(jnp.where(rm == HG_CHUNK - 1, b, 0.0))
        eb = jnp.exp(b)
        ekb = jnp.exp(blast - b)
        qe, kte, dec = q * eb, k * ekb, jnp.exp(blast)

        do_out, op = do_ref[...], opre_ref[...]
        sgg = _sigmoid(g)
        sil = g * sgg
        r = lax.rsqrt(headsum(op * op) * (1.0 / HEAD) + RMS_EPS)
        on = op * r
        dng_ref[...] += jnp.sum(do_out * on * sil, axis=0, keepdims=True)
        dg = do_out * on * ngv * (sgg * (1.0 + g * (1.0 - sgg)))
        don = do_out * ngv * sil
        dop = r * (don - on * (headsum(don * on) * (1.0 / HEAD)))

        v_ref[...] = v
        kte_ref[...] = kte
        dop_ref[...] = dop
        dec_ref[...] = dec
        dot_t = dop.T
        lane_chunk = lax.broadcasted_iota(jnp.int32, (HG_W, th), 1) // HG_CHUNK
        for c in range(nct):
            dotm_ref[c * HG_W:(c + 1) * HG_W, :] = jnp.where(lane_chunk == c, dot_t, 0.0).astype(BF16)
        qg_ref[...] = jnp.dot(dotm_ref[...], qe.astype(BF16), preferred_element_type=F32)

        gs = gst_ref[...]
        for c in reversed(range(nct)):
            rows = slice(c * HG_CHUNK, (c + 1) * HG_CHUNK)
            s = st_in_ref[c]
            gm = (gs * head_f32).astype(BF16)
            dkte_ref[rows, :] = jnp.dot(v_ref[rows, :].astype(BF16), gm, preferred_element_type=F32)
            dvi_ref[rows, :] = lax.dot_general(kte_ref[rows, :].astype(BF16), gm, (((1,), (1,)), ((), ())),
                                               preferred_element_type=F32)
            dqe_ref[rows, :] = jnp.dot(dop_ref[rows, :].astype(BF16), s.astype(BF16), preferred_element_type=F32)
            ddec_ref[rows, :] = jnp.broadcast_to(jnp.sum(gs * s, axis=0, keepdims=True), (HG_CHUNK, HG_W))
            dec_c = jnp.max(dec_ref[rows, :], axis=0, keepdims=True)
            gs = gs * dec_c + qg_ref[c * HG_W:(c + 1) * HG_W, :] * head_f32
        gst_ref[...] = gs

        dkte, dqe = dkte_ref[...], dqe_ref[...]
        dq = dqe * eb
        dk = dkte * ekb
        db = dqe * qe - dkte * kte
        dv = dvi_ref[...]
        dblast = dkte * kte + jnp.where(rm == HG_CHUNK - 1, ddec_ref[...] * dec, 0.0)

        for d in range(HG_CHUNK):
            kd, bd, vd = _shifted(k, d, th), _shifted(b, d, th), _shifted(v, d, th)
            e = jnp.exp(jnp.where(rm >= d, b - bd, -1e30))
            p = q * kd * e
            sc = headsum(p, 1)
            dsc = headsum(dop * vd, 1)
            dv = dv + _unshift(sc * dop, d, th)
            dq = dq + dsc * kd * e
            dk = dk + _unshift(dsc * q * e, d, th)
            darg = dsc * p
            db = db + darg - _unshift(darg, d, th)

        db = db + jnp.where(rm == HG_CHUNK - 1, seg_cumsum(dblast), 0.0)
        dlf = seg_rcumsum(db)
        df = dlf / f - dk
        dlb_ref[...] += jnp.sum(df * (1.0 - sg), axis=0, keepdims=True)
        dfl = df * (1.0 - lb) * sg * (1.0 - sg)
        dqr = dq * (sq * (1.0 + qr * (1.0 - sq)))
        dproj_ref[...] = jnp.concatenate([dqr, dfl, dv, dg], axis=1).astype(dproj_ref.dtype)

    rev = lambda i: nt - 1 - i
    col = lambda j: pl.BlockSpec((th, HG_W), lambda i, j=j: (rev(i), j))
    vec = pl.BlockSpec((1, HG_W), lambda i: (0, 0))
    row = pl.BlockSpec((th, HG_W), lambda i: (rev(i), 0))
    tile_f32 = pltpu.VMEM((th, HG_W), F32)
    return pl.pallas_call(
        body, name=name, grid=(nt,),
        in_specs=[col(0), col(1), col(2), col(3), vec, vec, row,
                  pl.BlockSpec((nct, HG_W, HG_W), lambda i: (rev(i), 0, 0)), col(0)],
        out_specs=[pl.BlockSpec((th, 4 * HG_W), lambda i: (rev(i), 0)), vec, vec],
        out_shape=[jax.ShapeDtypeStruct((t_rows, 4 * HG_W), BF16), jax.ShapeDtypeStruct((1, HG_W), F32),
                   jax.ShapeDtypeStruct((1, HG_W), F32)],
        scratch_shapes=[pltpu.VMEM((HG_W, HG_W), F32), pltpu.VMEM((nct * HG_W, th), BF16),
                        pltpu.VMEM((nct * HG_W, HG_W), F32)] + [tile_f32] * 8,
        compiler_params=_cparams(("arbitrary",)),
    )(projp, projp, projp, projp, lb, ng, opre, states, dcat)


_INV_SQRT2 = 1.0 / math.sqrt(2.0)
_INV_SQRT2PI = 1.0 / math.sqrt(2.0 * math.pi)


def _gelu(x):
    return 0.5 * x * (1.0 + lax.erf(x * _INV_SQRT2))


def _gelu_grad(x):
    return 0.5 * (1.0 + lax.erf(x * _INV_SQRT2)) + x * jnp.exp(-0.5 * x * x) * _INV_SQRT2PI


def _sgu_parts(bu, bv, lg, lbias, w_ref, n_groups):
    c = SGU_CHUNK
    tril = (lax.broadcasted_iota(jnp.int32, (c, c), 0) >= lax.broadcasted_iota(jnp.int32, (c, c), 1)).astype(F32)
    gid = lax.broadcasted_iota(jnp.int32, bu.shape, 1) // HEAD
    u = _gelu(bu)
    gv = _gelu(bv)
    mu = jnp.mean(gv, axis=-1, keepdims=True)
    xc = gv - mu
    rstd = lax.rsqrt(jnp.mean(xc * xc, axis=-1, keepdims=True) + LN_EPS)
    xhat = xc * rstd
    vn = xhat * lg + lbias
    ws = [w_ref[gi] * tril for gi in range(n_groups)]
    return tril, gid, u, rstd, xhat, vn, ws


def _sgu_fwd(projp, lg, lbias, w_s, bias_full, *, name, job=None):
    t_rows = projp.shape[0]
    n_groups = w_s.shape[0]
    c = SGU_CHUNK

    def body(u_ref, v_ref, lg_ref, lb_ref, w_ref, bias_ref, o_ref):
        _, gid, u, _, _, vn, ws = _sgu_parts(u_ref[...], v_ref[...], lg_ref[...], lb_ref[...], w_ref, n_groups)
        vnb = vn.astype(BF16)
        z = bias_ref[...]
        for gi in range(n_groups):
            z = z + jnp.where(gid == gi, jnp.dot(ws[gi].astype(BF16), vnb, preferred_element_type=F32), 0.0)
        o_ref[...] = (u * z).astype(o_ref.dtype)

    col = lambda j: pl.BlockSpec((c, HG_W), lambda i, j=j: (i, j))
    return _call(
        body, (projp, projp, lg, lbias, w_s, bias_full), name=name, grid=(t_rows // c,),
        in_specs=[col(4), col(5), _const_spec(lg), _const_spec(lbias), _const_spec(w_s), _const_spec(bias_full)],
        out_specs=pl.BlockSpec((c, HG_W), lambda i: (i, 0)),
        out_shape=jax.ShapeDtypeStruct((t_rows, HG_W), BF16), sem=("arbitrary",), job=job)


def _sgu_bwd(projp, lg, lbias, w_s, bias_full, dcat, *, name):
    t_rows = projp.shape[0]
    n_groups = w_s.shape[0]
    c = SGU_CHUNK
    n = t_rows // c

    def body(u_ref, v_ref, lg_ref, lb_ref, w_ref, bias_ref, do_ref,
             dproj_ref, dlg_ref, dlb_ref, dw_ref, dbs_ref, dbias_acc):
        i = pl.program_id(0)

        @pl.when(i == 0)
        def _():
            dlg_ref[...] = jnp.zeros_like(dlg_ref)
            dlb_ref[...] = jnp.zeros_like(dlb_ref)
            dw_ref[...] = jnp.zeros_like(dw_ref)
            dbias_acc[...] = jnp.zeros_like(dbias_acc)

        bu, bv, lg_v = u_ref[...], v_ref[...], lg_ref[...]
        tril, gid, u, rstd, xhat, vn, ws = _sgu_parts(bu, bv, lg_v, lb_ref[...], w_ref, n_groups)
        vnb = vn.astype(BF16)
        z = bias_ref[...]
        for gi in range(n_groups):
            z = z + jnp.where(gid == gi, jnp.dot(ws[gi].astype(BF16), vnb, preferred_element_type=F32), 0.0)
        do = do_ref[...]
        dbu = do * z * _gelu_grad(bu)
        dz = do * u
        dbias_acc[...] += dz
        dvn = jnp.zeros_like(dz)
        for gi in range(n_groups):
            dzg = jnp.where(gid == gi, dz, 0.0).astype(BF16)
            dw_ref[gi] += lax.dot_general(dzg, vnb, (((1,), (1,)), ((), ())), preferred_element_type=F32) * tril
            dvn = dvn + jnp.dot(ws[gi].T.astype(BF16), dzg, preferred_element_type=F32)
        dlg_ref[...] += jnp.sum(dvn * xhat, axis=0, keepdims=True)
        dlb_ref[...] += jnp.sum(dvn, axis=0, keepdims=True)
        dxh = dvn * lg_v
        dgv = rstd * (dxh - jnp.mean(dxh, axis=-1, keepdims=True)
                      - xhat * jnp.mean(dxh * xhat, axis=-1, keepdims=True))
        dproj_ref[...] = jnp.concatenate([dbu, dgv * _gelu_grad(bv)], axis=1).astype(dproj_ref.dtype)

        @pl.when(i == n - 1)
        def _():
            dbs_ref[...] = jnp.sum(dbias_acc[...].T.reshape(n_groups, HEAD, c), axis=1)

    col = lambda j: pl.BlockSpec((c, HG_W), lambda i, j=j: (i, j))
    return pl.pallas_call(
        body, name=name, grid=(n,),
        in_specs=[col(4), col(5), _const_spec(lg), _const_spec(lbias), _const_spec(w_s), _const_spec(bias_full),
                  col(1)],
        out_specs=[pl.BlockSpec((c, 2 * HG_W), lambda i: (i, 0)), _const_spec(lg), _const_spec(lbias),
                   _const_spec(w_s), pl.BlockSpec((n_groups, c), lambda i: (0, 0))],
        out_shape=[jax.ShapeDtypeStruct((t_rows, 2 * HG_W), BF16), jax.ShapeDtypeStruct(lg.shape, F32),
                   jax.ShapeDtypeStruct(lbias.shape, F32), jax.ShapeDtypeStruct(w_s.shape, F32),
                   jax.ShapeDtypeStruct((n_groups, c), F32)],
        scratch_shapes=[pltpu.VMEM((c, HG_W), F32)],
        compiler_params=_cparams(("arbitrary",)),
    )(projp, projp, lg, lbias, w_s, bias_full, dcat)


def _rope_tables(positions):
    t = positions.shape[0]
    inv_freq = ROPE_THETA ** (-jnp.arange(0, 32, 2, dtype=F32) / 32)
    ang = positions.astype(F32)[:, None] * inv_freq
    cos, sin = jnp.cos(ang), jnp.sin(ang)
    z = lambda w: jnp.zeros((t, w), F32)
    cos_t = jnp.concatenate([jnp.ones((t, 64), F32), cos, cos, z(32)], axis=1)
    sin_up = jnp.concatenate([z(80), sin, z(32)], axis=1)
    sin_dn = jnp.concatenate([z(64), -sin, z(48)], axis=1)
    return cos_t, sin_up, sin_dn


def _rep(x, n):
    return x if n == 1 else jnp.concatenate([x] * n, axis=1)


def _rope(x, cos_t, sin_up, sin_dn):
    w = x.shape[1]
    return x * cos_t + pltpu.roll(x, 16, 1) * sin_up + pltpu.roll(x, w - 16, 1) * sin_dn


def _rope_t(dy, cos_t, sin_up, sin_dn):
    w = dy.shape[1]
    return dy * cos_t + pltpu.roll(dy * sin_up, w - 16, 1) + pltpu.roll(dy * sin_dn, 16, 1)


def _mla_prep(q, kv, projp, tables, *, name):
    nh = N_ATT_HEADS

    def fn(qv, kvv, kr, cos_t, sin_up, sin_dn):
        qr = _rope(qv, _rep(cos_t, nh), _rep(sin_up, nh), _rep(sin_dn, nh))
        krr = _rope(kr, cos_t, sin_up, sin_dn)
        lane = lax.broadcasted_iota(jnp.int32, kvv.shape, 1) % LANES
        return qr, jnp.where(lane < HEAD, kvv, 0.0) + _rep(krr, nh), kvv

    w = q.shape[1]
    return _rowwise(fn, [q, kv, (projp, LANES, P_KR // LANES)] + list(tables), [],
                    [(w, BF16), (w, BF16), (w, BF16)], name=name)


def _mla_prep_bwd(dqr, dkf, tables, *, name):
    nh = N_ATT_HEADS

    def fn(dq, dk, cos_t, sin_up, sin_dn):
        dqp = _rope_t(dq, _rep(cos_t, nh), _rep(sin_up, nh), _rep(sin_dn, nh))
        dkrr = dk[:, 0:LANES]
        for h in range(1, nh):
            dkrr = dkrr + dk[:, LANES * h:LANES * (h + 1)]
        return dqp, _rope_t(dkrr, cos_t, sin_up, sin_dn)

    return _rowwise(fn, [dqr, dkf] + list(tables), [], [(dqr.shape[1], BF16), (LANES, BF16)], name=name)


_LOG2E = 1.0 / math.log(2.0)
_NT = (((1,), (1,)), ((), ()))
_TN = (((0,), (0,)), ((), ()))


def _attn_fwd(qr, kf, kvb, *, name, job=None):
    t_rows = qr.shape[0]
    tq = min(ATT_TQ, t_rows)
    nb = t_rows // tq
    scale = ATT_D ** -0.5

    c2 = scale * _LOG2E

    def body(q_ref, kf_ref, kvb_ref, o_ref, lse_ref):
        qi = pl.program_id(1)
        lane = lax.broadcasted_iota(jnp.int32, (tq, LANES), 1)
        causal_t = (lax.broadcasted_iota(jnp.int32, (tq, tq), 0) <= lax.broadcasted_iota(jnp.int32, (tq, tq), 1))
        heads = [slice(hh * LANES, (hh + 1) * LANES) for hh in range(2)]
        qs = [q_ref[:, cols] for cols in heads]

        def block(first, n_keys, carry, diagonal):
            rows = pl.ds(pl.multiple_of(first * tq, tq), n_keys)
            new = []
            for q, cols, (m_old, l_old, acc_t) in zip(qs, heads, carry):
                s_t = lax.dot_general(kf_ref[rows, cols], q, _NT, preferred_element_type=F32)
                if diagonal:
                    s_t = jnp.where(causal_t, s_t, -1e30)
                m_new = jnp.maximum(m_old, jnp.max(s_t, axis=0, keepdims=True))
                p_t = jnp.exp2((s_t - m_new) * c2)
                a = jnp.exp2((m_old - m_new) * c2)
                pv_t = lax.dot_general(kvb_ref[rows, cols], p_t.astype(BF16), _TN, preferred_element_type=F32)
                new.append((m_new, a * l_old + jnp.sum(p_t, axis=0, keepdims=True), a * acc_t + pv_t))
            return tuple(new)

        init = (jnp.full((1, tq), -1e30, F32), jnp.zeros((1, tq), F32), jnp.zeros((LANES, tq), F32))
        carry = lax.fori_loop(0, qi // 4, lambda g, c: block(4 * g, 4 * tq, c, False), (init, init))
        carry = lax.cond((qi // 2) % 2 == 1, lambda c: block(4 * (qi // 4), 2 * tq, c, False), lambda c: c, carry)
        carry = lax.cond(qi % 2 == 1, lambda c: block(qi - 1, tq, c, False), lambda c: c, carry)
        outs = []
        for hh, (m_fin, l_fin, acc_t) in enumerate(block(qi, tq, carry, True)):
            lse_ref[hh] = m_fin * scale + jnp.log(l_fin)
            outs.append((acc_t / l_fin).T)
        o_ref[...] = jnp.where(lane < HEAD, pltpu.roll(outs[0], HEAD, 1), outs[1])

    pair = pl.BlockSpec((t_rows, 2 * LANES), lambda pr, qi: (0, pr))
    return _call(
        body, (qr, kf, kvb), name=name, grid=(N_ATT_HEADS // 2, nb),
        in_specs=[pl.BlockSpec((tq, 2 * LANES), lambda pr, qi: (qi, pr)), pair, pair],
        out_specs=[pl.BlockSpec((tq, LANES), lambda pr, qi: (qi, pr)),
                   pl.BlockSpec((2, 1, tq), lambda pr, qi: (pr, 0, qi))],
        out_shape=[jax.ShapeDtypeStruct((t_rows, N_ATT_HEADS * HEAD), F32),
                   jax.ShapeDtypeStruct((N_ATT_HEADS, 1, t_rows), F32)],
        sem=("parallel", "arbitrary"), job=job)


def _attn_bwd(qr, kf, kvb, dcat, o, lse, *, name, job=None):
    t_rows = qr.shape[0]
    tq = min(ATT_TQ, t_rows)
    nb = t_rows // tq
    scale = ATT_D ** -0.5
    c2 = scale * _LOG2E
    do_off = 2 * HG_W // LANES

    def body(q_ref, kf_ref, kvb_ref, do_ref, o_ref, lse_ref, dq_ref, dkv_ref, dk_ref):
        ki = pl.program_id(1)

        @pl.when(ki == 0)
        def _():
            dq_ref[...] = jnp.zeros_like(dq_ref)

        lane = lax.broadcasted_iota(jnp.int32, (tq, LANES), 1)
        causal_t = (lax.broadcasted_iota(jnp.int32, (tq, tq), 0) <= lax.broadcasted_iota(jnp.int32, (tq, tq), 1))
        heads = [slice(hh * LANES, (hh + 1) * LANES) for hh in range(2)]
        ks = [kf_ref[:, cols] for cols in heads]
        vs = [kvb_ref[:, cols] for cols in heads]

        def block(qi, n_q, carry, diagonal):
            rows = pl.ds(pl.multiple_of(qi * tq, tq), n_q)
            do_pair, o_pair = do_ref[rows, :], o_ref[rows, :]
            upper = lax.broadcasted_iota(jnp.int32, do_pair.shape, 1) >= HEAD
            new = []
            for hh, (cols, k, v, (dk, dv)) in enumerate(zip(heads, ks, vs, carry)):
                q = q_ref[rows, cols]
                do, ov = (pltpu.roll(do_pair, HEAD, 1), pltpu.roll(o_pair, HEAD, 1)) if hh == 0 else (do_pair, o_pair)
                do = jnp.where(upper, do, 0.0)
                delta = jnp.sum((do * ov).T, axis=0, keepdims=True)
                s_t = lax.dot_general(k, q, _NT, preferred_element_type=F32)
                if diagonal:
                    s_t = jnp.where(causal_t, s_t, -1e30)
                p_t = jnp.exp2(s_t * c2 - lse_ref[hh, :, rows] * _LOG2E)
                dob = do.astype(BF16)
                dv = dv + jnp.dot(p_t.astype(BF16), dob, preferred_element_type=F32)
                dp_t = lax.dot_general(v, dob, _NT, preferred_element_type=F32)
                ds_t = (p_t * (dp_t - delta) * scale).astype(BF16)
                dk = dk + jnp.dot(ds_t, q, preferred_element_type=F32)
                dq_ref[rows, cols] += lax.dot_general(ds_t, k, _TN, preferred_element_type=F32)
                new.append((dk, dv))
            return tuple(new)

        zero = jnp.zeros((tq, LANES), F32)
        carry = block(ki, tq, ((zero, zero), (zero, zero)), True)
        rest = nb - 1 - ki
        carry = lax.fori_loop(0, rest // 2, lambda g, c: block(ki + 1 + 2 * g, 2 * tq, c, False), carry)
        carry = lax.cond(rest % 2 == 1, lambda c: block(nb - 1, tq, c, False), lambda c: c, carry)
        dkv_ref[...] = jnp.concatenate([jnp.where(lane < HEAD, dk, dv) for dk, dv in carry],
                                       axis=1).astype(dkv_ref.dtype)
        dk_ref[...] = jnp.concatenate([dk for dk, _ in carry], axis=1)

    pair_all = pl.BlockSpec((t_rows, 2 * LANES), lambda pr, ki: (0, pr))
    pair_blk = pl.BlockSpec((tq, 2 * LANES), lambda pr, ki: (ki, pr))
    wide = jax.ShapeDtypeStruct((t_rows, N_ATT_HEADS * LANES), F32)
    return _call(
        body, (qr, kf, kvb, dcat, o, lse), name=name, grid=(N_ATT_HEADS // 2, nb),
        in_specs=[pair_all, pair_blk, pair_blk,
                  pl.BlockSpec((t_rows, LANES), lambda pr, ki: (0, do_off + pr)),
                  pl.BlockSpec((t_rows, LANES), lambda pr, ki: (0, pr)),
                  pl.BlockSpec((2, 1, t_rows), lambda pr, ki: (pr, 0, 0))],
        out_specs=[pair_all, pair_blk, pair_blk],
        out_shape=[wide, jax.ShapeDtypeStruct(wide.shape, BF16), wide],
        sem=("parallel", "arbitrary"), job=job)


def _my_pos():
    return lax.axis_index("x"), lax.axis_index("y"), lax.axis_index("c")


def _all_gather(xs, *, name, columns=True):
    return _gather_forward(_run_job(_gather_job(xs, columns), name=name), name=name + "_forward")


def _remote(src, dst, send_sems, recv_sems, k, dev):
    return pltpu.make_async_remote_copy(src_ref=src, dst_ref=dst, send_sem=send_sems.at[k], recv_sem=recv_sems.at[k],
                                        device_id=dev, device_id_type=MESH)


def _block(ref, idx):
    if len(ref.shape) == 2:
        return ref.at[:, pl.ds(pl.multiple_of(idx * LANES, LANES), LANES)]
    return ref.at[idx]


def _gather_job(xs, columns=True):
    n = len(xs)

    def make(x_refs, out_refs, send_sems, recv_sems, local_sems):
        mx, my, mc = _my_pos()
        mine = 4 * mx + 2 * my + mc
        peers = [(mx, my, 1 - mc), (1 - mx, my, mc), (mx, 1 - my, mc), (1 - mx, 1 - my, mc)]
        sends, recvs, local = [], [], []
        for a in range(n):
            local.append(pltpu.make_async_copy(x_refs[a], _block(out_refs[a], mine), local_sems.at[a]))
            for k, dev in enumerate(peers):
                theirs = 4 * dev[0] + 2 * dev[1] + dev[2]
                sends.append(_remote(x_refs[a], _block(out_refs[a], mine), send_sems, recv_sems, 4 * a + k, dev))
                recvs.append(_remote(x_refs[a], _block(out_refs[a], theirs), send_sems, recv_sems, 4 * a + k, dev))
        return sends, recvs, local

    def gathered(x):
        if columns and x.ndim == 2 and x.shape[1] == LANES:
            return jax.ShapeDtypeStruct((x.shape[0], N_DEV * LANES), x.dtype)
        return jax.ShapeDtypeStruct((N_DEV,) + x.shape, x.dtype)

    return _copies_job(xs, [gathered(x) for x in xs], 4 * n, n, make)


def _forward_job(gs):
    n = len(gs)

    def make(in_refs, out_refs, send_sems, recv_sems, local_sems):
        mx, my, mc = _my_pos()
        chips = [(1 - mx, my), (mx, 1 - my), (1 - mx, 1 - my)]
        sends, recvs = [], []
        for a in range(n):
            for j, (cx, cy) in enumerate(chips):
                here = _block(out_refs[a], 4 * cx + 2 * cy + mc)
                there = _block(out_refs[a], 4 * cx + 2 * cy + 1 - mc)
                sends.append(_remote(here, here, send_sems, recv_sems, 3 * a + j, (mx, my, 1 - mc)))
                recvs.append(_remote(here, there, send_sems, recv_sems, 3 * a + j, (mx, my, 1 - mc)))
        return sends, recvs, []

    shapes = [jax.ShapeDtypeStruct(g.shape, g.dtype) for g in gs]
    return _copies_job(gs, shapes, 3 * n, 0, make, in_place=True)


def _gather_forward(gs, *, name):
    return _run_job(_forward_job(gs), name=name)


def _pair_job(xs):
    n = len(xs)

    def make(x_refs, out_refs, send_sems, recv_sems, local_sems):
        mx, my, mc = _my_pos()

        def src(ref, g):
            return _block(ref, 2 * g + 1 - mc) if len(ref.shape) == 2 else ref.at[g, 1 - mc]

        copies = [_remote(src(x_refs[a], g), out_refs[a].at[g], send_sems, recv_sems, 4 * a + g, (mx, my, 1 - mc))
                  for a in range(n) for g in range(4)]
        return copies, copies, []

    shapes = [jax.ShapeDtypeStruct((4, x.shape[0], LANES) if x.ndim == 2 else (4,) + x.shape[2:], x.dtype)
              for x in xs]
    return _copies_job(xs, shapes, 4 * n, 0, make)


def _pair_add(x, r, core, *, name):
    _, a, b = r.shape
    ta = _row_tile(a, 256)

    def body(c_ref, x_ref, r_ref, o_ref):
        o_ref[...] = (x_ref[...] + r_ref[...]).astype(o_ref.dtype)

    blk = pl.BlockSpec((None, ta, b), lambda g, i, c_ref: (g, i, 0))
    own = (pl.BlockSpec((ta, b), lambda g, i, c_ref: (i, 2 * g + c_ref[0])) if x.ndim == 2
           else pl.BlockSpec((None, None, ta, b), lambda g, i, c_ref: (g, c_ref[0], i, 0)))
    return pl.pallas_call(
        body, name=name,
        grid_spec=pltpu.PrefetchScalarGridSpec(
            num_scalar_prefetch=1, grid=(4, a // ta), in_specs=[own, blk], out_specs=blk),
        out_shape=jax.ShapeDtypeStruct((4, a, b), BF16),
        compiler_params=_cparams(("parallel", "parallel")),
    )(core, x, r)


def _quad_job(xs):
    n = len(xs)

    def make(x_refs, out_refs, send_sems, recv_sems, local_sems):
        mx, my, mc = _my_pos()
        mine = 2 * mx + my
        peers = [((1 - mx, my, mc), 2 * (1 - mx) + my), ((mx, 1 - my, mc), 2 * mx + 1 - my),
                 ((1 - mx, 1 - my, mc), 2 * (1 - mx) + 1 - my)]
        sends, recvs, local = [], [], []
        for a in range(n):
            local.append(pltpu.make_async_copy(x_refs[a].at[mine], out_refs[a].at[mine], local_sems.at[a]))
            for k, (dev, g) in enumerate(peers):
                sends.append(_remote(x_refs[a].at[g], out_refs[a].at[mine], send_sems, recv_sems, 3 * a + k, dev))
                recvs.append(_remote(x_refs[a].at[g], out_refs[a].at[g], send_sems, recv_sems, 3 * a + k, dev))
        return sends, recvs, local

    shapes = [jax.ShapeDtypeStruct(x.shape, x.dtype) for x in xs]
    return _copies_job(xs, shapes, 3 * n, n, make)


def _row_tile(r, pref):
    t = min(pref, r)
    while r % t or (t % 8 and t != r):
        t -= 1
    return t


def _adamw(parts, w, m, v, layer, *, name, tile=256, into=None):
    g, a, b = parts.shape
    tile = _row_tile(a, tile)
    c1 = 1.0 / (1.0 - ADAM_B1 ** ADAM_STEP)
    c2 = 1.0 / (1.0 - ADAM_B2 ** ADAM_STEP)
    into = tuple(into or ())

    def body(p_ref, w_ref, m_ref, v_ref, *refs):
        g_ref, d_ref, mo_ref, vo_ref = refs[len(into):]
        grad = p_ref[0].astype(F32)
        for j in range(1, g):
            grad = grad + p_ref[j].astype(F32)
        mn = ADAM_B1 * m_ref[...] + (1.0 - ADAM_B1) * grad
        vn = ADAM_B2 * v_ref[...] + (1.0 - ADAM_B2) * (grad * grad)
        g_ref[...] = grad
        mo_ref[...] = mn
        vo_ref[...] = vn
        d_ref[...] = -ADAM_LR * ((mn * c1) / (jnp.sqrt(vn * c2) + ADAM_EPS) + ADAM_WD * w_ref[...])

    if layer is None:
        src, shape = pl.BlockSpec((tile, b), lambda i: (i, 0)), (a, b)
    else:
        src, shape = pl.BlockSpec((None, tile, b), lambda i: (layer, i, 0)), w.shape
    return pl.pallas_call(
        body, name=name, grid=(a // tile,),
        in_specs=[pl.BlockSpec((g, tile, b), lambda i: (0, i, 0)), src, src, src] + [_ANY] * len(into),
        out_specs=[src] * 4,
        out_shape=[jax.ShapeDtypeStruct(shape, F32)] * 4,
        input_output_aliases={4 + i: i for i in range(len(into))},
        compiler_params=_cparams(("parallel",)),
    )(parts, w, m, v, *into)


W_IN_SHARD = 276


def _w_in_dest(col):
    return jnp.where(col < P_KR, col, jnp.where(col < P_KR + 256, col + (P_CKV - P_KR), col - 2176 + P_KR + HEAD))


PLACE_TILE = 384
PLACE_SHARDS = 3
PICK_TILE = 128
PICK_TILES = 4


def _w_in_tables():
    col = np.arange(N_DEV * W_IN_SHARD)
    dest = np.where(col < P_KR, col, np.where(col < P_KR + 256, col + (P_CKV - P_KR), col - 2176 + P_KR + HEAD))
    shard = col // W_IN_SHARD

    def filled(used, universe, n):
        used = sorted(set(int(u) for u in used))
        assert len(used) <= n, used
        return used + [u for u in universe if u not in used][:n - len(used)]

    place = [filled(shard[dest // PLACE_TILE == c], range(N_DEV), PLACE_SHARDS) for c in range(P_COLS // PLACE_TILE)]
    pick = [filled(dest[shard == j] // PICK_TILE, range(P_COLS // PICK_TILE), PICK_TILES) for j in range(N_DEV)]
    return np.asarray(place, np.int32).reshape(-1), np.asarray(pick, np.int32).reshape(-1)


def _place_w_in(g, *, name):
    _, d, sh = g.shape
    tc, ns = PLACE_TILE, PLACE_SHARDS
    table = jnp.asarray(_w_in_tables()[0])

    def body(tab_ref, g_ref, o_ref, acc_ref):
        ct, s = pl.program_id(0), pl.program_id(1)
        j = tab_ref[ct * ns + s]

        @pl.when(s == 0)
        def _():
            acc_ref[...] = jnp.zeros_like(acc_ref)

        src = j * sh + lax.broadcasted_iota(jnp.int32, (sh, tc), 0)
        dst = ct * tc + lax.broadcasted_iota(jnp.int32, (sh, tc), 1)
        place = (_w_in_dest(src) == dst).astype(BF16)
        acc_ref[...] += jnp.dot(g_ref[...], place, preferred_element_type=F32)

        @pl.when(s == ns - 1)
        def _():
            o_ref[...] = acc_ref[...].astype(o_ref.dtype)

    return pl.pallas_call(
        body, name=name,
        grid_spec=pltpu.PrefetchScalarGridSpec(
            num_scalar_prefetch=1, grid=(P_COLS // tc, ns),
            in_specs=[pl.BlockSpec((None, d, sh), lambda ct, s, tab: (tab[ct * ns + s], 0, 0))],
            out_specs=pl.BlockSpec((d, tc), lambda ct, s, tab: (0, ct)),
            scratch_shapes=[pltpu.VMEM((d, tc), F32)]),
        out_shape=jax.ShapeDtypeStruct((d, P_COLS), BF16),
        compiler_params=_cparams(("parallel", "arbitrary")),
    )(table, g)


def _unplace_w_in(dw, *, name):
    d = dw.shape[0]
    sh, tk, nt = W_IN_SHARD, PICK_TILE, PICK_TILES
    table = jnp.asarray(_w_in_tables()[1])

    def body(tab_ref, dw_ref, o_ref):
        j, kk = pl.program_id(0), pl.program_id(1)
        tile = tab_ref[j * nt + kk]
        src = j * sh + lax.broadcasted_iota(jnp.int32, (tk, sh), 1)
        dst = tile * tk + lax.broadcasted_iota(jnp.int32, (tk, sh), 0)
        pick = (_w_in_dest(src) == dst).astype(BF16)
        part = _split_dot(dw_ref[...], pick)

        @pl.when(kk == 0)
        def _():
            o_ref[...] = part

        @pl.when(kk > 0)
        def _():
            o_ref[...] += part

    return pl.pallas_call(
        body, name=name,
        grid_spec=pltpu.PrefetchScalarGridSpec(
            num_scalar_prefetch=1, grid=(N_DEV, nt),
            in_specs=[pl.BlockSpec((d, tk), lambda j, kk, tab: (0, tab[j * nt + kk]))],
            out_specs=pl.BlockSpec((None, d, sh), lambda j, kk, tab: (j, 0, 0))),
        out_shape=jax.ShapeDtypeStruct((N_DEV, d, sh), F32),
        compiler_params=_cparams(("parallel", "arbitrary")),
    )(table, dw)


def _gate_up_swiglu(h1, wgu, *, name):
    t_rows, k = h1.shape
    w = wgu.shape[2]
    tm = _tile(t_rows, 1024)

    def body(a_ref, wg_ref, wu_ref, gu_ref, act_ref):
        a = a_ref[...].astype(BF16)
        gate = jnp.dot(a, wg_ref[...], preferred_element_type=F32)
        up = jnp.dot(a, wu_ref[...], preferred_element_type=F32)
        gu_ref[0] = gate.astype(gu_ref.dtype)
        gu_ref[1] = up.astype(gu_ref.dtype)
        act_ref[...] = (gate * _sigmoid(gate) * up).astype(act_ref.dtype)

    return pl.pallas_call(
        body, name=name, grid=(t_rows // tm, 4),
        in_specs=[pl.BlockSpec((tm, k), lambda i, j: (i, 0)),
                  pl.BlockSpec((None, k, w), lambda i, j: (j, 0, 0)),
                  pl.BlockSpec((None, k, w), lambda i, j: (j + 4, 0, 0))],
        out_specs=[pl.BlockSpec((2, None, tm, w), lambda i, j: (0, j, i, 0)),
                   pl.BlockSpec((None, tm, w), lambda i, j: (j, i, 0))],
        out_shape=[jax.ShapeDtypeStruct((2, 4, t_rows, w), BF16), jax.ShapeDtypeStruct((4, t_rows, w), BF16)],
        compiler_params=_cparams(("parallel", "arbitrary")),
    )(h1, wgu, wgu)


def _down_dx_swiglu(dffn, wdown, gu, *, name):
    t_rows, k = dffn.shape
    w = gu.shape[3]
    tm = _tile(t_rows, 1024)

    def body(d_ref, w_ref, gu_ref, o_ref):
        dact = lax.dot_general(d_ref[...].astype(BF16), w_ref[...], _NT, preferred_element_type=F32)
        gate, up = gu_ref[0].astype(F32), gu_ref[1].astype(F32)
        sg = _sigmoid(gate)
        o_ref[0] = (dact * up * (sg * (1.0 + gate * (1.0 - sg)))).astype(o_ref.dtype)
        o_ref[1] = (dact * gate * sg).astype(o_ref.dtype)

    blk = pl.BlockSpec((2, None, tm, w), lambda i, j: (0, j, i, 0))
    return pl.pallas_call(
        body, name=name, grid=(t_rows // tm, 4),
        in_specs=[pl.BlockSpec((tm, k), lambda i, j: (i, 0)), pl.BlockSpec((w, k), lambda i, j: (j, 0)), blk],
        out_specs=blk, out_shape=jax.ShapeDtypeStruct(gu.shape, BF16),
        compiler_params=_cparams(("parallel", "arbitrary")),
    )(dffn, wdown, gu)


BIG = ("w_in", "mla_w_uq", "mla_w_ukv", "w_out", "w_gate_up", "w_down", "ple_w_gate", "ple_w_proj")
SMALL = ("ln_in_g", "ln_in_b", "hgrn_lb_logits", "hgrn_norm_g", "sgu_ln_g", "sgu_ln_b", "sgu_w_s", "sgu_b_s",
         "mla_q_norm_g", "mla_kv_norm_g", "ln1_g", "ln1_b", "ln2_g", "ln2_b")
ORDER = ("ln_in_g", "ln_in_b", "w_in", "hgrn_lb_logits", "hgrn_norm_g", "sgu_ln_g", "sgu_ln_b", "sgu_w_s", "sgu_b_s",
         "mla_q_norm_g", "mla_w_uq", "mla_kv_norm_g", "mla_w_ukv", "w_out", "ln1_g", "ln1_b", "w_gate_up", "w_down",
         "ple_w_gate", "ple_w_proj", "ln2_g", "ln2_b")


def _slab(a, align):
    s = a.reshape(-1, LANES)
    pad = -s.shape[0] % align
    return jnp.pad(s, ((0, pad), (0, 0))) if pad else s


def _pack(arrays, align=16, total_align=512):
    s = jnp.concatenate([_slab(a, align) for a in arrays], axis=0)
    pad = -s.shape[0] % total_align
    return jnp.pad(s, ((0, pad), (0, 0))) if pad else s


def _unpack(slab, shapes, align=16):
    out, r0 = [], 0
    for s in shapes:
        nr = math.prod(s) // LANES
        out.append(slab[r0:r0 + nr].reshape(s))
        r0 += nr + (-nr % align)
    return out


def _weight_shards(w, li):
    uq_pad = ((0, 0), (0, LANES - ATT_D))
    shards = {k: w[k][li] for k in BIG}
    shards["mla_w_uq"] = jnp.pad(shards["mla_w_uq"], uq_pad)
    return {k: s.astype(BF16) for k, s in shards.items()}


def _usable_weights(g, *, name):
    out = {}
    for k, a in g.items():
        if k == "w_in":
            out[k] = _place_w_in(a, name=name + "_place_w_in")
        elif k in ("w_out", "w_down", "ple_w_gate"):
            out[k] = a.reshape(a.shape[0] * a.shape[1], a.shape[2])
        else:
            out[k] = a
    return out


BY_COLUMNS = ("mla_w_uq", "mla_w_ukv", "ple_w_proj")


def _as_pairs(k, g):
    if k in BY_COLUMNS:
        return g
    if g.ndim == 2:
        return g.reshape((4, 2, g.shape[0] // N_DEV) + g.shape[1:])
    return g.reshape((4, 2) + g.shape[1:])


def _twice(fn):
    return lambda *a: fn(*a) * 2


def _layer_forward(li, h, hb, p_i, wts, sm, lbs, tables, alpha, hgrn_job=None, after_hgrn=None, attn_job=None,
                   after_attn=None, loss_target=None):
    n = f"l{li}_"
    row1 = lambda a: a.reshape(1, -1)
    projp = _mm(hb, wts["w_in"], name=n + "proj")
    ng = row1(sm["hgrn_norm_g"][li])
    res = _hgrn_fwd(projp, lbs[li], ng, name=n + "hgrn_fwd", job=hgrn_job)
    if hgrn_job is not None:
        res, got = res
    o_a, o_pre, states = res
    lg, lbias = row1(sm["sgu_ln_g"][li]), row1(sm["sgu_ln_b"][li])
    w_s = sm["sgu_w_s"][li]
    bias_full = jnp.repeat(sm["sgu_b_s"][li].T, HEAD, axis=1)
    o_b = _sgu_fwd(projp, lg, lbias, w_s, bias_full, name=n + "sgu_fwd",
                   job=None if hgrn_job is None else _forward_job(got))
    if hgrn_job is not None:
        o_b, got = o_b
        wts = dict(wts, **after_hgrn(got))
    qg, kvg = row1(sm["mla_q_norm_g"][li]), row1(sm["mla_kv_norm_g"][li])
    cq_view, ckv_view = (projp, 384, P_CQ // 384), (projp, 256, P_CKV // 256)
    (cqn,) = _rowwise(_fn_rms, [cq_view], [qg], [(384, BF16)], name=n + "q_norm")
    (ckvn,) = _rowwise(_fn_rms, [ckv_view], [kvg], [(256, BF16)], name=n + "kv_norm")
    q = _mm(cqn, wts["mla_w_uq"], name=n + "uq")
    kv = _mm(ckvn, wts["mla_w_ukv"], name=n + "ukv")
    qr, kf, kvb = _mla_prep(q, kv, projp, tables, name=n + "mla_prep")
    res = _attn_fwd(qr, kf, kvb, name=n + "attn_fwd", job=attn_job)
    if attn_job is not None:
        res, got = res
    o_c, lse = res
    cat = jnp.concatenate([o_a, o_b, o_c.astype(BF16)], axis=1)
    mix = _mm(cat, wts["w_out"], name=n + "out_proj", job=None if attn_job is None else _forward_job(got))
    if attn_job is not None:
        mix, got = mix
        wts = dict(wts, **after_attn(got))
    g1, b1 = row1(sm["ln1_g"][li]), row1(sm["ln1_b"][li])
    d = h.shape[1]
    h1, h1b = _rowwise(_twice(_make_post_mix(alpha)), [h, mix], [g1, b1], [(d, F32), (d, BF16)], name=n + "ln1")
    gu, act = _gate_up_swiglu(h1b, wts["w_gate_up"], name=n + "gate_up")
    ffn = _mm(act, wts["w_down"], am="bmk", tm=2048, name=n + "down")
    pg = _mm(h1b, wts["ple_w_gate"], name=n + "ple_gate")
    pp = _mm(p_i, wts["ple_w_proj"], name=n + "ple_proj")
    g2, b2 = row1(sm["ln2_g"][li]), row1(sm["ln2_b"][li])
    if loss_target is None:
        out = _rowwise(_twice(_make_ple_ln(alpha)), [h1, ffn, pg, pp], [g2, b2], [(d, F32), (d, BF16)],
                       name=n + "ln2")
    else:
        def ln_and_loss(h1v, ffnv, pgv, ppv, tv, gv, bv):
            err = _make_ple_ln(alpha)(h1v, ffnv, pgv, ppv, gv, bv)[0] - tv
            return err * (1.0 / d), 0.5 * jnp.sum(jnp.mean(err * err, axis=-1, keepdims=True), axis=0, keepdims=True)

        out = _rowwise(ln_and_loss, [h1, ffn, pg, pp, loss_target], [g2, b2], [(d, F32)], accs=[(1, 1)],
                       name=n + "ln2_loss")
    saved = dict(h=h, hb=hb, h1b=h1b, projp=projp, o_pre=o_pre, states=states, cqn=cqn, ckvn=ckvn, qr=qr, kf=kf, kvb=kvb, o_c=o_c,
                 lse=lse, cat=cat, mix=mix, h1=h1, gu=gu, act=act, ffn=ffn, pg=pg, pp=pp, ng=ng, lg=lg, wts=wts,
                 lbias=lbias, w_s=w_s, bias_full=bias_full, qg=qg, kvg=kvg, g1=g1, b1=b1, g2=g2, b2=b2)
    return tuple(out), saved


RS_EARLY = ("ple_w_proj", "ple_w_gate", "w_down", "w_gate_up", "w_out")
RS_LATE = ("mla_w_uq", "mla_w_ukv", "w_in")


def _layer_backward(li, dh2_parts, p_i, sv, lbs, tables, alpha, core, carried=None):
    n = f"l{li}_b_"
    wts = sv["wts"]
    gr = {}
    dh1_a, dffn, dpg, dpp, gr["ln2_g"], gr["ln2_b"] = _rowwise_vjp(
        _make_ple_ln(alpha), [sv["h1"], sv["ffn"], sv["pg"], sv["pp"]], [sv["g2"], sv["b2"]], [dh2_parts],
        groups=[[0], [1], [2], [3]], gdtypes=[F32, BF16, BF16, BF16], name=n + "ln2")
    big = {}
    big["ple_w_proj"] = _mm(p_i, dpp, am="km", tk=2048, name=n + "ple_proj_dw")
    big["ple_w_gate"] = _mm(sv["h1b"], dpg, am="km", name=n + "ple_gate_dw")
    dh1_b = _mm(dpg, wts["ple_w_gate"], bm="nk", name=n + "ple_gate_dx")
    big["w_down"] = _mm(sv["act"], dffn, am="bkm", tk=4096, name=n + "down_dw")
    dgu = _down_dx_swiglu(dffn, wts["w_down"], sv["gu"], name=n + "down_dx")
    dgu = dgu.reshape((N_DEV,) + dgu.shape[2:])
    big["w_gate_up"], carried_got = _mm(sv["h1b"], dgu, am="km", bm="bkn", om="bmn", tk=4096, name=n + "gate_up_dw",
                                        job=carried), None
    if carried is not None:
        big["w_gate_up"], carried_got = big["w_gate_up"]
    early = [_as_pairs(k, big[k]) for k in RS_EARLY[:-1]]
    dh1_c, theirs = _mm(dgu, wts["w_gate_up"], am="bmk", bm="bnk", tm=2048, name=n + "gate_up_dx",
                        job=_pair_job(early))
    dh_a, dmix, gr["ln1_g"], gr["ln1_b"] = _rowwise_vjp(
        _make_post_mix(alpha), [sv["h"], sv["mix"]], [sv["g1"], sv["b1"]], [[dh1_a, dh1_b, dh1_c]],
        groups=[[0], [1]], gdtypes=[F32, BF16], name=n + "ln1")
    big["w_out"] = _mm(sv["cat"], dmix, am="km", name=n + "out_proj_dw")
    early.append(_as_pairs("w_out", big["w_out"]))
    dcat, their_w_out = _mm(dmix, wts["w_out"], bm="nk", name=n + "out_proj_dx", job=_pair_job(early[-1:]))
    sums = [_pair_add(x, r, core, name=n + "pair_add_" + k)
            for k, x, r in zip(RS_EARLY, early, list(theirs) + list(their_w_out))]

    (dqr, dkv, dkf), early_quads = _attn_bwd(sv["qr"], sv["kf"], sv["kvb"], dcat, sv["o_c"], sv["lse"],
                                             name=n + "attn", job=_quad_job(sums))
    dqpad, dkr = _mla_prep_bwd(dqr, dkf, tables, name=n + "mla_prep")
    big["mla_w_uq"] = _mm(sv["cqn"], dqpad, am="km", tk=2048, name=n + "uq_dw")
    dcqn = _mm(dqpad, wts["mla_w_uq"], bm="nk", name=n + "uq_dx")
    big["mla_w_ukv"] = _mm(sv["ckvn"], dkv, am="km", tk=2048, name=n + "ukv_dw")
    dckvn = _mm(dkv, wts["mla_w_ukv"], bm="nk", name=n + "ukv_dx")
    projp = sv["projp"]
    dcq, gr["mla_q_norm_g"] = _rowwise_vjp(_fn_rms, [(projp, 384, P_CQ // 384)], [sv["qg"]], [[dcqn]],
                                           groups=[[0]], gdtypes=[BF16], name=n + "q_norm")
    dckv, gr["mla_kv_norm_g"] = _rowwise_vjp(_fn_rms, [(projp, 256, P_CKV // 256)], [sv["kvg"]], [[dckvn]],
                                             groups=[[0]], gdtypes=[BF16], name=n + "kv_norm")
    dsgu, gr["sgu_ln_g"], gr["sgu_ln_b"], gr["sgu_w_s"], gr["sgu_b_s"] = _sgu_bwd(
        projp, sv["lg"], sv["lbias"], sv["w_s"], sv["bias_full"], dcat, name=n + "sgu")
    dhg, gr["hgrn_norm_g"], gr["lower_bound"] = _hgrn_bwd(
        projp, lbs[li], sv["ng"], sv["o_pre"], sv["states"], dcat, name=n + "hgrn")
    dprojp = jnp.concatenate([dhg, dsgu, dcq, dkr, dckv], axis=1)
    big["w_in"] = _unplace_w_in(_mm(sv["hb"], dprojp, am="km", tk=4096, name=n + "proj_dw"),
                                name=n + "proj_dw_shards")
    late = [_as_pairs(k, big[k]) for k in RS_LATE]
    dh_b, theirs = _mm(dprojp, wts["w_in"], bm="nk", name=n + "proj_dx", job=_pair_job(late))
    late_sums = [_pair_add(x, r, core, name=n + "pair_add_" + k) for k, x, r in zip(RS_LATE, late, theirs)]
    return [dh_a, dh_b], gr, early_quads, late_sums, carried_got


def kernel(x, p, positions, ln_in_g, ln_in_b, w_in, hgrn_lb_logits, hgrn_norm_g, sgu_ln_g, sgu_ln_b, sgu_w_s, sgu_b_s, mla_q_norm_g, mla_w_uq, mla_kv_norm_g, mla_w_ukv, w_out, ln1_g, ln1_b, w_gate_up, w_down, ple_w_gate, ple_w_proj, ln2_g, ln2_b, loss_target, m_ln_in_g, m_ln_in_b, m_w_in, m_hgrn_lb_logits, m_hgrn_norm_g, m_sgu_ln_g, m_sgu_ln_b, m_sgu_w_s, m_sgu_b_s, m_mla_q_norm_g, m_mla_w_uq, m_mla_kv_norm_g, m_mla_w_ukv, m_w_out, m_ln1_g, m_ln1_b, m_w_gate_up, m_w_down, m_ple_w_gate, m_ple_w_proj, m_ln2_g, m_ln2_b, v_ln_in_g, v_ln_in_b, v_w_in, v_hgrn_lb_logits, v_hgrn_norm_g, v_sgu_ln_g, v_sgu_ln_b, v_sgu_w_s, v_sgu_b_s, v_mla_q_norm_g, v_mla_w_uq, v_mla_kv_norm_g, v_mla_w_ukv, v_w_out, v_ln1_g, v_ln1_b, v_w_gate_up, v_w_down, v_ple_w_gate, v_ple_w_proj, v_ln2_g, v_ln2_b):
    args = dict(locals())
    w = {k: args[k] for k in ORDER}
    m = {k: args["m_" + k] for k in ORDER}
    v = {k: args["v_" + k] for k in ORDER}
    depth = w_in.shape[0]
    assert depth == 2, "the lower-bound kernel is written for two layers"
    alpha = (2 * depth) ** 0.25
    xs, tgt = x[0], loss_target[0]
    d_model = xs.shape[1]

    shards = [_weight_shards(w, li) for li in range(depth)]
    on_hgrn0 = ("mla_w_uq", "mla_w_ukv", "w_out", "ple_w_gate", "ple_w_proj")
    ffn0 = ("w_gate_up", "w_down")
    first1 = ("w_in", "mla_w_uq", "mla_w_ukv", "w_out")
    on_attn1 = ("w_gate_up", "w_down", "ple_w_gate", "ple_w_proj")
    layer1_first = {}

    def after_hgrn0(got):
        return _usable_weights(dict(zip(on_hgrn0, got)), name="l0")

    def after_attn0(got):
        layer1_first.update(_usable_weights(dict(zip(first1, got[len(ffn0):])), name="l1"))
        return _usable_weights(dict(zip(ffn0, got[:len(ffn0)])), name="l0")

    def after_attn1(got):
        return _usable_weights(dict(zip(on_attn1, got)), name="l1")

    tables = _rope_tables(positions[0])
    row1 = lambda a: a.reshape(1, -1)
    l0, l1 = row1(hgrn_lb_logits[0]), row1(hgrn_lb_logits[1])
    lbs = _rowwise(_fn_lower_bounds, [l0, l1], [], [(HG_W, F32), (HG_W, F32)], name="lower_bounds")

    gin, bin_ = row1(ln_in_g), row1(ln_in_b)
    (h, hb), g_in = _rowwise(_twice(_fn_ln), [xs], [gin, bin_], [(d_model, F32), (d_model, BF16)], name="ln_in",
                             job=_gather_job([shards[0]["w_in"]]))
    w_in0 = _usable_weights({"w_in": _gather_forward(g_in, name="gather_l0_w_in_forward")[0]}, name="l0")
    (h, hb), sv0 = _layer_forward(
        0, h, hb, p[0, 0], w_in0, w, lbs, tables, alpha,
        hgrn_job=_gather_job([shards[0][k] for k in on_hgrn0]), after_hgrn=after_hgrn0,
        attn_job=_gather_job([shards[0][k] for k in ffn0] + [shards[1][k] for k in first1]), after_attn=after_attn0)
    (dy, loss_local), sv1 = _layer_forward(
        1, h, hb, p[1, 0], layer1_first, w, lbs, tables, alpha,
        attn_job=_gather_job([shards[1][k] for k in on_attn1]), after_attn=after_attn1, loss_target=tgt)
    saved = [sv0, sv1]
    loss = lax.psum(loss_local[0, 0], ("x", "y", "c"))

    core = lax.axis_index("c").astype(jnp.int32).reshape(1)
    dparts, grads, quads, carried = [dy], [None] * depth, [None] * depth, None
    for li in reversed(range(depth)):
        dparts, grads[li], early_quads, late_sums, late_quads = _layer_backward(
            li, dparts, p[li, 0], saved[li], lbs, tables, alpha, core, carried=carried)
        quads[li] = dict(zip(RS_EARLY, early_quads))
        if carried is not None:
            quads[li + 1].update(zip(RS_LATE, late_quads))
        carried = _quad_job(late_sums)
    (dx, d_gin, d_bin), late_quads = _rowwise_vjp(_fn_ln, [xs], [gin, bin_], [dparts], groups=[[0]], name="ln_in_b",
                                                   job=carried)
    quads[0].update(zip(RS_LATE, late_quads))
    dl0, dl1 = _rowwise_vjp(_fn_lower_bounds, [l0, l1], [], [[grads[0]["lower_bound"]], [grads[1]["lower_bound"]]],
                            groups=[[0], [1]], name="lower_bounds_b")

    prefixes = ("grad_", "delta_", "new_m_", "new_v_")
    uq_pad = ((0, 0), (0, 0), (0, LANES - ATT_D))
    state = {k: ((jnp.pad(w[k], uq_pad), jnp.pad(m[k], uq_pad), jnp.pad(v[k], uq_pad)) if k == "mla_w_uq"
                 else (w[k], m[k], v[k])) for k in BIG}
    out = {}
    for k in BIG:
        res4 = None
        for li in range(depth):
            res4 = _adamw(quads[li][k], *state[k], li, name=f"adamw_l{li}_{k}", into=res4)
        for pre, a in zip(prefixes, res4):
            out[pre + k] = a[:, :, :ATT_D] if k == "mla_w_uq" else a

    small_g = {"ln_in_g": d_gin.reshape(-1), "ln_in_b": d_bin.reshape(-1),
               "hgrn_lb_logits": jnp.stack([dl0.reshape(-1), dl1.reshape(-1)])}
    for k in SMALL[3:]:
        small_g[k] = jnp.stack([grads[li][k].reshape(w[k].shape[1:]) for li in range(depth)])
    (small_parts,) = _all_gather([_pack([small_g[k] for k in SMALL])], name="gather_small_grads", columns=False)
    slabs = _adamw(small_parts, _pack([w[k] for k in SMALL]), _pack([m[k] for k in SMALL]),
                   _pack([v[k] for k in SMALL]), None, name="adamw_small")
    shapes = [w[k].shape for k in SMALL]
    for pre, slab in zip(prefixes, slabs):
        for k, a in zip(SMALL, _unpack(slab, shapes)):
            out[pre + k] = a
    res = [loss, dx[None]]
    for prefix in ("grad_", "delta_", "new_m_", "new_v_"):
        res += [out[prefix + k] for k in ORDER]
    return tuple(res)
```

```python
import functools
import math

import jax
import jax.numpy as jnp
import numpy as np
from jax import lax
from jax.experimental import pallas as pl
from jax.experimental.pallas import tpu as pltpu

F32 = jnp.float32
BF16 = jnp.bfloat16
MESH = pl.DeviceIdType.MESH

LN_EPS = 1e-5
RMS_EPS = 1e-6
ROPE_THETA = 10000.0
ADAM_LR, ADAM_B1, ADAM_B2, ADAM_EPS, ADAM_WD, ADAM_STEP = 0.001, 0.9, 0.999, 1e-08, 0.01, 10

N_DEV = 8
LANES = 128
HG_CHUNK = 16
HG_W = 256
HEAD = 64
SGU_CHUNK = 128
N_ATT_HEADS = 8
ATT_D = 96
VMEM_LIMIT = 56 * 1024 * 1024

HG_TILE = 256
ATT_TQ = 512
ROW_TILE = 256

P_CQ, P_KR, P_CKV, P_COLS = 1536, 1920, 2048, 2304


def _cparams(sem):
    return pltpu.CompilerParams(dimension_semantics=sem, vmem_limit_bytes=VMEM_LIMIT)


_ANY = pl.BlockSpec(memory_space=pl.ANY)


def _call(body, operands, *, name, grid, in_specs, out_specs, out_shape, sem, scratch_shapes=(), job=None):
    if job is None:
        return pl.pallas_call(body, name=name, grid=grid, in_specs=in_specs, out_specs=out_specs, out_shape=out_shape,
                              scratch_shapes=list(scratch_shapes), compiler_params=_cparams(sem))(*operands)
    single = not isinstance(out_shape, (list, tuple))
    shapes = [out_shape] if single else list(out_shape)
    ospecs = [out_specs] if single else list(out_specs)
    ni, no, ns = len(operands), len(shapes), len(scratch_shapes)
    ji, jo = len(job.inputs), len(job.out_shapes)

    def hosted(*refs):
        p = 0
        parts = []
        for cnt in (ni, ji, no, jo, ns):
            parts.append(refs[p:p + cnt])
            p += cnt
        ins, jins, outs, jouts, scr = parts
        jsems = refs[p:]
        ids = [pl.program_id(a) for a in range(len(grid))]
        first = functools.reduce(lambda a, b: a & b, [i == 0 for i in ids])
        last = functools.reduce(lambda a, b: a & b, [i == g - 1 for i, g in zip(ids, grid)])

        @pl.when(first)
        def _():
            job.start(jins, jouts, jsems)

        body(*ins, *outs, *scr)

        @pl.when(last)
        def _():
            job.finish(jins, jouts, jsems)

    res = pl.pallas_call(
        hosted, name=name, grid=grid,
        in_specs=list(in_specs) + [_ANY] * ji, out_specs=ospecs + [_ANY] * jo,
        out_shape=shapes + list(job.out_shapes),
        scratch_shapes=list(scratch_shapes) + [pltpu.SemaphoreType.DMA((c,)) for c in job.sem_counts],
        input_output_aliases=job.aliases(ni, no),
        compiler_params=_cparams(("arbitrary",) * len(grid)),
    )(*operands, *job.inputs)
    own = res[0] if single else res[:no]
    return own, res[no:]


class _Job:
    def __init__(self, inputs, out_shapes, sem_counts, start, finish, in_place=False):
        self.inputs, self.out_shapes, self.sem_counts = list(inputs), list(out_shapes), list(sem_counts)
        self.start, self.finish, self.in_place = start, finish, in_place

    def aliases(self, first_in, first_out):
        return {first_in + i: first_out + i for i in range(len(self.inputs))} if self.in_place else {}


def _copies_job(inputs, out_shapes, n_remote, n_local, make, in_place=False):
    def start(jins, jouts, sems):
        sends, _, local = make(jins, jouts, *sems)
        for cp in local + sends:
            cp.start()

    def finish(jins, jouts, sems):
        sends, recvs, local = make(jins, jouts, *sems)
        for cp in recvs:
            cp.wait_recv()
        for cp in sends:
            cp.wait_send()
        for cp in local:
            cp.wait()

    return _Job(inputs, out_shapes, [n_remote, n_remote, max(n_local, 1)], start, finish, in_place)


def _run_job(job, *, name):
    ji, jo = len(job.inputs), len(job.out_shapes)

    def body(*refs):
        jins, jouts, sems = refs[:ji], refs[ji:ji + jo], refs[ji + jo:]
        job.start(jins, jouts, sems)
        job.finish(jins, jouts, sems)

    return pl.pallas_call(
        body, name=name, out_shape=list(job.out_shapes), in_specs=[_ANY] * ji, out_specs=[_ANY] * jo,
        scratch_shapes=[pltpu.SemaphoreType.DMA((c,)) for c in job.sem_counts],
        input_output_aliases=job.aliases(0, 0),
    )(*job.inputs)


def _tile(n, pref):
    if n % pref == 0:
        return pref
    best = None
    t = LANES
    while t <= min(n, pref):
        if n % t == 0:
            best = t
        t += LANES
    return best if best is not None else n


def _mm(a, b, *, am="mk", bm="kn", om="mn", out_dtype=F32, tm=1024, tn=1024, tk=1024, name, job=None):
    if am == "mk":
        m, k = a.shape
    elif am == "km":
        k, m = a.shape
    elif am == "bmk":
        m, tk = a.shape[1], a.shape[2]
        k = a.shape[0] * tk
    else:
        k, tm = a.shape[1], a.shape[2]
        m = a.shape[0] * tm
    if bm == "kn":
        kb_, n = b.shape
    elif bm == "nk":
        n, kb_ = b.shape
    elif bm == "bkn":
        kb_, tn = b.shape[1], b.shape[2]
        n = b.shape[0] * tn
    else:
        n, tk = b.shape[1], b.shape[2]
        kb_ = b.shape[0] * tk
    assert kb_ == k, (a.shape, b.shape, am, bm)
    tm, tn, tk = _tile(m, tm), _tile(n, tn), _tile(k, tk)
    nk = k // tk
    dims = (((0 if am in ("km", "bkm") else 1,), (1 if bm in ("nk", "bnk") else 0,)), ((), ()))

    a_spec = {"mk": pl.BlockSpec((tm, tk), lambda i, j, kk: (i, kk)),
              "km": pl.BlockSpec((tk, tm), lambda i, j, kk: (kk, i)),
              "bmk": pl.BlockSpec((None, tm, tk), lambda i, j, kk: (kk, i, 0)),
              "bkm": pl.BlockSpec((None, tk, tm), lambda i, j, kk: (i, kk, 0))}[am]
    b_spec = {"kn": pl.BlockSpec((tk, tn), lambda i, j, kk: (kk, j)),
              "nk": pl.BlockSpec((tn, tk), lambda i, j, kk: (j, kk)),
              "bkn": pl.BlockSpec((None, tk, tn), lambda i, j, kk: (j, kk, 0)),
              "bnk": pl.BlockSpec((None, tn, tk), lambda i, j, kk: (kk, j, 0))}[bm]
    if om == "mn":
        o_spec, o_shape = pl.BlockSpec((tm, tn), lambda i, j, kk: (i, j)), (m, n)
    else:
        o_spec, o_shape = pl.BlockSpec((None, tm, tn), lambda i, j, kk: (j, i, 0)), (n // tn, m, tn)

    def body(a_ref, b_ref, o_ref, *acc):
        kk = pl.program_id(2)

        def prod():
            return lax.dot_general(a_ref[...].astype(BF16), b_ref[...].astype(BF16), dims, preferred_element_type=F32)

        if nk == 1:
            o_ref[...] = prod().astype(o_ref.dtype)
            return
        acc_ref, = acc

        @pl.when(kk == 0)
        def _():
            acc_ref[...] = prod()

        if nk > 2:
            @pl.when((kk > 0) & (kk < nk - 1))
            def _():
                acc_ref[...] += prod()

        @pl.when(kk == nk - 1)
        def _():
            o_ref[...] = (acc_ref[...] + prod()).astype(o_ref.dtype)

    return _call(body, (a, b), name=name, grid=(m // tm, n // tn, nk), in_specs=[a_spec, b_spec], out_specs=o_spec,
                 out_shape=jax.ShapeDtypeStruct(o_shape, out_dtype),
                 scratch_shapes=[pltpu.VMEM((tm, tn), F32)] if nk > 1 else [],
                 sem=("parallel", "parallel", "arbitrary"), job=job)


def _mm_kblocks(a, b, *, bm, tm, name, job=None):
    nkb, m, kb = a.shape
    n = b.shape[1]
    tm = _tile(m, tm)

    def body(a_ref, b_ref, o_ref):
        acc = None
        for j in range(nkb):
            if bm == "kn":
                part = jnp.dot(a_ref[j], b_ref[j * kb:(j + 1) * kb, :], preferred_element_type=F32)
            else:
                part = lax.dot_general(a_ref[j], b_ref[j], _NT, preferred_element_type=F32)
            acc = part if acc is None else acc + part
        o_ref[...] = acc

    b_spec = (pl.BlockSpec(b.shape, lambda i: (0, 0)) if bm == "kn" else pl.BlockSpec(b.shape, lambda i: (0, 0, 0)))
    return _call(body, (a, b), name=name, grid=(m // tm,),
                 in_specs=[pl.BlockSpec((nkb, tm, kb), lambda i: (0, i, 0)), b_spec],
                 out_specs=pl.BlockSpec((tm, n), lambda i: (i, 0)), out_shape=jax.ShapeDtypeStruct((m, n), F32),
                 sem=("parallel",), job=job)


def _row_operand(a, tile):
    if isinstance(a, tuple):
        arr, w, j = a
        return arr, pl.BlockSpec((tile, w), lambda i, j=j: (i, j))
    return a, pl.BlockSpec((tile, a.shape[1]), lambda i: (i, 0))


def _const_spec(c):
    nd = c.ndim
    return pl.BlockSpec(c.shape, lambda i, nd=nd: (0,) * nd)


def _rowwise(fn, rows, consts, outs, *, name, accs=(), tile=None, job=None):
    t_rows = (rows[0][0] if isinstance(rows[0], tuple) else rows[0]).shape[0]
    tile = min(tile or ROW_TILE, t_rows)
    arrs, specs = zip(*[_row_operand(a, tile) for a in rows])
    nin, no = len(rows) + len(consts), len(outs)

    def body(*refs):
        res = fn(*[r[...] for r in refs[:nin]])
        for r, v in zip(refs[nin:nin + no], res[:no]):
            r[...] = v.astype(r.dtype)
        if accs:
            a_refs = refs[nin + no:]

            @pl.when(pl.program_id(0) == 0)
            def _():
                for r in a_refs:
                    r[...] = jnp.zeros_like(r)

            for r, v in zip(a_refs, res[no:]):
                r[...] += v

    out_shape = [jax.ShapeDtypeStruct((t_rows, w), dt) for w, dt in outs]
    out_shape += [jax.ShapeDtypeStruct(s, F32) for s in accs]
    out_specs = [pl.BlockSpec((tile, w), lambda i: (i, 0)) for w, _ in outs]
    out_specs += [pl.BlockSpec(s, lambda i, nd=len(s): (0,) * nd) for s in accs]
    return _call(body, (*arrs, *consts), name=name, grid=(t_rows // tile,),
                 in_specs=list(specs) + [_const_spec(c) for c in consts],
                 out_specs=out_specs, out_shape=out_shape, sem=("arbitrary",), job=job)


def _rowwise_vjp(fn, rows, consts, cts, *, name, groups, tile=None, gdtypes=None, job=None):
    t_rows = (rows[0][0] if isinstance(rows[0], tuple) else rows[0]).shape[0]
    tile = min(tile or ROW_TILE, t_rows)
    arrs, specs = zip(*[_row_operand(a, tile) for a in rows])
    flat_cts = [c for group in cts for c in group]
    ct_arrs, ct_specs = zip(*[_row_operand(a, tile) for a in flat_cts])
    nr, nc, nct, ng = len(rows), len(consts), len(flat_cts), len(groups)

    def width(a):
        return a[1] if isinstance(a, tuple) else a.shape[1]

    def body(*refs):
        rv = [r[...].astype(F32) for r in refs[:nr]]
        cv = [r[...] for r in refs[nr:nr + nc]]
        ct_refs = refs[nr + nc:nr + nc + nct]
        ctv, pos = [], 0
        for group in cts:
            s = ct_refs[pos][...].astype(F32)
            for r in ct_refs[pos + 1:pos + len(group)]:
                s = s + r[...].astype(F32)
            ctv.append(s)
            pos += len(group)
        _, pull = jax.vjp(fn, *rv, *cv)
        grads = pull(tuple(ctv))
        g_refs = refs[nr + nc + nct:nr + nc + nct + ng]
        for r, idx in zip(g_refs, groups):
            parts = [grads[i] for i in idx]
            r[...] = (parts[0] if len(parts) == 1 else jnp.concatenate(parts, axis=1)).astype(r.dtype)
        c_refs = refs[nr + nc + nct + ng:]

        @pl.when(pl.program_id(0) == 0)
        def _():
            for r in c_refs:
                r[...] = jnp.zeros_like(r)

        for r, v in zip(c_refs, grads[nr:]):
            r[...] += v

    gw = [sum(width(rows[i]) for i in idx) for idx in groups]
    gdtypes = gdtypes or [F32] * ng
    out_shape = [jax.ShapeDtypeStruct((t_rows, w), dt) for w, dt in zip(gw, gdtypes)]
    out_shape += [jax.ShapeDtypeStruct(c.shape, F32) for c in consts]
    out_specs = [pl.BlockSpec((tile, w), lambda i: (i, 0)) for w in gw]
    out_specs += [_const_spec(c) for c in consts]
    return _call(body, (*arrs, *consts, *ct_arrs), name=name, grid=(t_rows // tile,),
                 in_specs=list(specs) + [_const_spec(c) for c in consts] + list(ct_specs),
                 out_specs=out_specs, out_shape=out_shape, sem=("arbitrary",), job=job)


def _layer_norm(x, g, b):
    mu = jnp.mean(x, axis=-1, keepdims=True)
    xc = x - mu
    var = jnp.mean(xc * xc, axis=-1, keepdims=True)
    return xc * lax.rsqrt(var + LN_EPS) * g + b


def _sigmoid(x):
    return 1.0 / (1.0 + jnp.exp(-x))


def _fn_ln(x, g, b):
    return (_layer_norm(x, g, b),)


def _fn_rms(x, g):
    return (x * lax.rsqrt(jnp.mean(x * x, axis=-1, keepdims=True) + RMS_EPS) * g,)


def _make_post_mix(alpha):
    def fn(h, mix, g, b):
        return (_layer_norm(alpha * h + mix, g, b),)
    return fn


def _make_ple_ln(alpha):
    def fn(h1, ffn, pg, pp, g, b):
        return (_layer_norm(alpha * h1 + ffn + _sigmoid(pg) * pp, g, b),)
    return fn


def _fn_lower_bounds(l0, l1):
    m = jnp.maximum(l0, l1)
    e0, e1 = jnp.exp(l0 - m), jnp.exp(l1 - m)
    s = e0 + e1
    p0, p1 = e0 / s, e1 / s
    return (p0 - p0, (p0 + p1) - p0)


def _split_dot(x, e_bf16):
    hi = x.astype(BF16)
    lo = (x - hi.astype(F32)).astype(BF16)
    return (jnp.dot(hi, e_bf16, preferred_element_type=F32) + jnp.dot(lo, e_bf16, preferred_element_type=F32))


def _hgrn_common(th):
    rm = lax.broadcasted_iota(jnp.int32, (th, HG_W), 0) % HG_CHUNK

    def seg_cumsum(x):
        for s in (1, 2, 4, 8):
            x = x + jnp.where(rm >= s, pltpu.roll(x, s, 0), 0.0)
        return x

    def seg_rcumsum(x):
        for s in (1, 2, 4, 8):
            x = x + jnp.where(rm < HG_CHUNK - s, pltpu.roll(x, th - s, 0), 0.0)
        return x

    ri = lax.broadcasted_iota(jnp.int32, (HG_W, HG_W), 0) // HEAD
    ci = lax.broadcasted_iota(jnp.int32, (HG_W, HG_W), 1) // HEAD
    head_f32 = (ri == ci).astype(F32)
    head_bf16 = head_f32.astype(BF16)

    def headsum(x, pieces=2):
        if pieces == 1:
            return jnp.dot(x.astype(BF16), head_bf16, preferred_element_type=F32)
        return _split_dot(x, head_bf16)

    return rm, seg_cumsum, seg_rcumsum, head_f32, headsum


def _hgrn_gates(qr, fl, lb):
    sg = _sigmoid(fl)
    f = lb + (1.0 - lb) * sg
    sq = _sigmoid(qr)
    return sg, f, jnp.log(f), 1.0 - f, qr * sq, sq


def _shifted(x, d, th):
    return x if d == 0 else pltpu.roll(x, d, 0)


def _unshift(x, d, th):
    return x if d == 0 else pltpu.roll(x, th - d, 0)


def _hgrn_fwd(projp, lb, ng, *, name, job=None):
    t_rows = projp.shape[0]
    th = min(HG_TILE, t_rows)
    nct = th // HG_CHUNK

    def body(q_ref, f_ref, i_ref, g_ref, lb_ref, ng_ref, oa_ref, opre_ref, st_out_ref,
             st_ref, vtm_ref, kv_ref, qe_ref, dec_ref, oint_ref):
        rm, seg_cumsum, seg_rcumsum, head_f32, headsum = _hgrn_common(th)

        @pl.when(pl.program_id(0) == 0)
        def _():
            st_ref[...] = jnp.zeros_like(st_ref)

        qr, fl, v, g = q_ref[...], f_ref[...], i_ref[...], g_ref[...]
        _, f, lf, k, q, _ = _hgrn_gates(qr, fl, lb_ref[...])
        b = seg_cumsum(lf)

        o = jnp.zeros((th, HG_W), F32)
        for d in range(HG_CHUNK):
            kd, bd, vd = _shifted(k, d, th), _shifted(b, d, th), _shifted(v, d, th)
            e = jnp.exp(jnp.where(rm >= d, b - bd, -1e30))
            o = o + headsum(q * kd * e, 1) * vd

        blast = seg_rcumsum(jnp.where(rm == HG_CHUNK - 1, b, 0.0))
        kte = (k * jnp.exp(blast - b)).astype(BF16)
        qe_ref[...] = q * jnp.exp(b)
        dec_ref[...] = jnp.exp(blast)
        vt = v.T
        lane_chunk = lax.broadcasted_iota(jnp.int32, (HG_W, th), 1) // HG_CHUNK
        for c in range(nct):
            vtm_ref[c * HG_W:(c + 1) * HG_W, :] = jnp.where(lane_chunk == c, vt, 0.0).astype(BF16)
        kv_ref[...] = jnp.dot(vtm_ref[...], kte, preferred_element_type=F32)

        s = st_ref[...]
        for c in range(nct):
            rows = slice(c * HG_CHUNK, (c + 1) * HG_CHUNK)
            st_out_ref[c] = s
            oint_ref[rows, :] = lax.dot_general(qe_ref[rows, :].astype(BF16), s.astype(BF16),
                                                (((1,), (1,)), ((), ())), preferred_element_type=F32)
            dec = jnp.max(dec_ref[rows, :], axis=0, keepdims=True)
            s = s * dec + kv_ref[c * HG_W:(c + 1) * HG_W, :] * head_f32
        st_ref[...] = s

        o = o + oint_ref[...]
        opre_ref[...] = o
        r = lax.rsqrt(headsum(o * o) * (1.0 / HEAD) + RMS_EPS)
        oa_ref[...] = (o * r * ng_ref[...] * (g * _sigmoid(g))).astype(oa_ref.dtype)

    col = lambda j: pl.BlockSpec((th, HG_W), lambda i, j=j: (i, j))
    vec = pl.BlockSpec((1, HG_W), lambda i: (0, 0))
    row = pl.BlockSpec((th, HG_W), lambda i: (i, 0))
    n_chunks = t_rows // HG_CHUNK
    return _call(
        body, (projp, projp, projp, projp, lb, ng), name=name, grid=(t_rows // th,),
        in_specs=[col(0), col(1), col(2), col(3), vec, vec],
        out_specs=[row, row, pl.BlockSpec((nct, HG_W, HG_W), lambda i: (i, 0, 0))],
        out_shape=[jax.ShapeDtypeStruct((t_rows, HG_W), BF16), jax.ShapeDtypeStruct((t_rows, HG_W), F32),
                   jax.ShapeDtypeStruct((n_chunks, HG_W, HG_W), F32)],
        scratch_shapes=[pltpu.VMEM((HG_W, HG_W), F32), pltpu.VMEM((nct * HG_W, th), BF16),
                        pltpu.VMEM((nct * HG_W, HG_W), F32), pltpu.VMEM((th, HG_W), F32),
                        pltpu.VMEM((th, HG_W), F32), pltpu.VMEM((th, HG_W), F32)],
        sem=("arbitrary",), job=job)


def _hgrn_bwd(projp, lb, ng, opre, states, dcat, *, name):
    t_rows = projp.shape[0]
    th = min(HG_TILE, t_rows)
    nct = th // HG_CHUNK
    nt = t_rows // th

    def body(q_ref, f_ref, i_ref, g_ref, lb_ref, ng_ref, opre_ref, st_in_ref, do_ref,
             dproj_ref, dng_ref, dlb_ref,
             gst_ref, dotm_ref, qg_ref, v_ref, kte_ref, dop_ref, dec_ref, dkte_ref, dvi_ref, dqe_ref, ddec_ref):
        rm, seg_cumsum, seg_rcumsum, head_f32, headsum = _hgrn_common(th)

        @pl.when(pl.program_id(0) == 0)
        def _():
            gst_ref[...] = jnp.zeros_like(gst_ref)
            dng_ref[...] = jnp.zeros_like(dng_ref)
            dlb_ref[...] = jnp.zeros_like(dlb_ref)

        qr, fl, v, g = q_ref[...], f_ref[...], i_ref[...], g_ref[...]
        lb, ngv = lb_ref[...], ng_ref[...]
        sg, f, lf, k, q, sq = _hgrn_gates(qr, fl, lb)
        b = seg_cumsum(lf)
        blast = seg_rcumsum(jnp.where(rm == HG_CHUNK - 1, b, 0.0))
        eb = jnp.exp(b)
        ekb = jnp.exp(blast - b)
        qe, kte, dec = q * eb, k * ekb, jnp.exp(blast)

        do_out, op = do_ref[...], opre_ref[...]
        sgg = _sigmoid(g)
        sil = g * sgg
        r = lax.rsqrt(headsum(op * op) * (1.0 / HEAD) + RMS_EPS)
        on = op * r
        dng_ref[...] += jnp.sum(do_out * on * sil, axis=0, keepdims=True)
        dg = do_out * on * ngv * (sgg * (1.0 + g * (1.0 - sgg)))
        don = do_out * ngv * sil
        dop = r * (don - on * (headsum(don * on) * (1.0 / HEAD)))

        v_ref[...] = v
        kte_ref[...] = kte
        dop_ref[...] = dop
        dec_ref[...] = dec
        dot_t = dop.T
        lane_chunk = lax.broadcasted_iota(jnp.int32, (HG_W, th), 1) // HG_CHUNK
        for c in range(nct):
            dotm_ref[c * HG_W:(c + 1) * HG_W, :] = jnp.where(lane_chunk == c, dot_t, 0.0).astype(BF16)
        qg_ref[...] = jnp.dot(dotm_ref[...], qe.astype(BF16), preferred_element_type=F32)

        gs = gst_ref[...]
        for c in reversed(range(nct)):
            rows = slice(c * HG_CHUNK, (c + 1) * HG_CHUNK)
            s = st_in_ref[c]
            gm = (gs * head_f32).astype(BF16)
            dkte_ref[rows, :] = jnp.dot(v_ref[rows, :].astype(BF16), gm, preferred_element_type=F32)
            dvi_ref[rows, :] = lax.dot_general(kte_ref[rows, :].astype(BF16), gm, (((1,), (1,)), ((), ())),
                                               preferred_element_type=F32)
            dqe_ref[rows, :] = jnp.dot(dop_ref[rows, :].astype(BF16), s.astype(BF16), preferred_element_type=F32)
            ddec_ref[rows, :] = jnp.broadcast_to(jnp.sum(gs * s, axis=0, keepdims=True), (HG_CHUNK, HG_W))
            dec_c = jnp.max(dec_ref[rows, :], axis=0, keepdims=True)
            gs = gs * dec_c + qg_ref[c * HG_W:(c + 1) * HG_W, :] * head_f32
        gst_ref[...] = gs

        dkte, dqe = dkte_ref[...], dqe_ref[...]
        dq = dqe * eb
        dk = dkte * ekb
        db = dqe * qe - dkte * kte
        dv = dvi_ref[...]
        dblast = dkte * kte + jnp.where(rm == HG_CHUNK - 1, ddec_ref[...] * dec, 0.0)

        for d in range(HG_CHUNK):
            kd, bd, vd = _shifted(k, d, th), _shifted(b, d, th), _shifted(v, d, th)
            e = jnp.exp(jnp.where(rm >= d, b - bd, -1e30))
            p = q * kd * e
            sc = headsum(p, 1)
            dsc = headsum(dop * vd, 1)
            dv = dv + _unshift(sc * dop, d, th)
            dq = dq + dsc * kd * e
            dk = dk + _unshift(dsc * q * e, d, th)
            darg = dsc * p
            db = db + darg - _unshift(darg, d, th)

        db = db + jnp.where(rm == HG_CHUNK - 1, seg_cumsum(dblast), 0.0)
        dlf = seg_rcumsum(db)
        df = dlf / f - dk
        dlb_ref[...] += jnp.sum(df * (1.0 - sg), axis=0, keepdims=True)
        dfl = df * (1.0 - lb) * sg * (1.0 - sg)
        dqr = dq * (sq * (1.0 + qr * (1.0 - sq)))
        dproj_ref[...] = jnp.concatenate([dqr, dfl, dv, dg], axis=1).astype(dproj_ref.dtype)

    rev = lambda i: nt - 1 - i
    col = lambda j: pl.BlockSpec((th, HG_W), lambda i, j=j: (rev(i), j))
    vec = pl.BlockSpec((1, HG_W), lambda i: (0, 0))
    row = pl.BlockSpec((th, HG_W), lambda i: (rev(i), 0))
    tile_f32 = pltpu.VMEM((th, HG_W), F32)
    return pl.pallas_call(
        body, name=name, grid=(nt,),
        in_specs=[col(0), col(1), col(2), col(3), vec, vec, row,
                  pl.BlockSpec((nct, HG_W, HG_W), lambda i: (rev(i), 0, 0)), col(0)],
        out_specs=[pl.BlockSpec((th, 4 * HG_W), lambda i: (rev(i), 0)), vec, vec],
        out_shape=[jax.ShapeDtypeStruct((t_rows, 4 * HG_W), BF16), jax.ShapeDtypeStruct((1, HG_W), F32),
                   jax.ShapeDtypeStruct((1, HG_W), F32)],
        scratch_shapes=[pltpu.VMEM((HG_W, HG_W), F32), pltpu.VMEM((nct * HG_W, th), BF16),
                        pltpu.VMEM((nct * HG_W, HG_W), F32)] + [tile_f32] * 8,
        compiler_params=_cparams(("arbitrary",)),
    )(projp, projp, projp, projp, lb, ng, opre, states, dcat)


_INV_SQRT2 = 1.0 / math.sqrt(2.0)
_INV_SQRT2PI = 1.0 / math.sqrt(2.0 * math.pi)


def _gelu(x):
    return 0.5 * x * (1.0 + lax.erf(x * _INV_SQRT2))


def _gelu_grad(x):
    return 0.5 * (1.0 + lax.erf(x * _INV_SQRT2)) + x * jnp.exp(-0.5 * x * x) * _INV_SQRT2PI


def _sgu_parts(bu, bv, lg, lbias, w_ref, n_groups):
    c = SGU_CHUNK
    tril = (lax.broadcasted_iota(jnp.int32, (c, c), 0) >= lax.broadcasted_iota(jnp.int32, (c, c), 1)).astype(F32)
    gid = lax.broadcasted_iota(jnp.int32, bu.shape, 1) // HEAD
    u = _gelu(bu)
    gv = _gelu(bv)
    mu = jnp.mean(gv, axis=-1, keepdims=True)
    xc = gv - mu
    rstd = lax.rsqrt(jnp.mean(xc * xc, axis=-1, keepdims=True) + LN_EPS)
    xhat = xc * rstd
    vn = xhat * lg + lbias
    ws = [w_ref[gi] * tril for gi in range(n_groups)]
    return tril, gid, u, rstd, xhat, vn, ws


def _sgu_fwd(projp, lg, lbias, w_s, bias_full, *, name, job=None):
    t_rows = projp.shape[0]
    n_groups = w_s.shape[0]
    c = SGU_CHUNK

    def body(u_ref, v_ref, lg_ref, lb_ref, w_ref, bias_ref, o_ref):
        _, gid, u, _, _, vn, ws = _sgu_parts(u_ref[...], v_ref[...], lg_ref[...], lb_ref[...], w_ref, n_groups)
        vnb = vn.astype(BF16)
        z = bias_ref[...]
        for gi in range(n_groups):
            z = z + jnp.where(gid == gi, jnp.dot(ws[gi].astype(BF16), vnb, preferred_element_type=F32), 0.0)
        o_ref[...] = (u * z).astype(o_ref.dtype)

    col = lambda j: pl.BlockSpec((c, HG_W), lambda i, j=j: (i, j))
    return _call(
        body, (projp, projp, lg, lbias, w_s, bias_full), name=name, grid=(t_rows // c,),
        in_specs=[col(4), col(5), _const_spec(lg), _const_spec(lbias), _const_spec(w_s), _const_spec(bias_full)],
        out_specs=pl.BlockSpec((c, HG_W), lambda i: (i, 0)),
        out_shape=jax.ShapeDtypeStruct((t_rows, HG_W), BF16), sem=("arbitrary",), job=job)


def _sgu_bwd(projp, lg, lbias, w_s, bias_full, dcat, *, name):
    t_rows = projp.shape[0]
    n_groups = w_s.shape[0]
    c = SGU_CHUNK
    n = t_rows // c

    def body(u_ref, v_ref, lg_ref, lb_ref, w_ref, bias_ref, do_ref,
             dproj_ref, dlg_ref, dlb_ref, dw_ref, dbs_ref, dbias_acc):
        i = pl.program_id(0)

        @pl.when(i == 0)
        def _():
            dlg_ref[...] = jnp.zeros_like(dlg_ref)
            dlb_ref[...] = jnp.zeros_like(dlb_ref)
            dw_ref[...] = jnp.zeros_like(dw_ref)
            dbias_acc[...] = jnp.zeros_like(dbias_acc)

        bu, bv, lg_v = u_ref[...], v_ref[...], lg_ref[...]
        tril, gid, u, rstd, xhat, vn, ws = _sgu_parts(bu, bv, lg_v, lb_ref[...], w_ref, n_groups)
        vnb = vn.astype(BF16)
        z = bias_ref[...]
        for gi in range(n_groups):
            z = z + jnp.where(gid == gi, jnp.dot(ws[gi].astype(BF16), vnb, preferred_element_type=F32), 0.0)
        do = do_ref[...]
        dbu = do * z * _gelu_grad(bu)
        dz = do * u
        dbias_acc[...] += dz
        dvn = jnp.zeros_like(dz)
        for gi in range(n_groups):
            dzg = jnp.where(gid == gi, dz, 0.0).astype(BF16)
            dw_ref[gi] += lax.dot_general(dzg, vnb, (((1,), (1,)), ((), ())), preferred_element_type=F32) * tril
            dvn = dvn + jnp.dot(ws[gi].T.astype(BF16), dzg, preferred_element_type=F32)
        dlg_ref[...] += jnp.sum(dvn * xhat, axis=0, keepdims=True)
        dlb_ref[...] += jnp.sum(dvn, axis=0, keepdims=True)
        dxh = dvn * lg_v
        dgv = rstd * (dxh - jnp.mean(dxh, axis=-1, keepdims=True)
                      - xhat * jnp.mean(dxh * xhat, axis=-1, keepdims=True))
        dproj_ref[...] = jnp.concatenate([dbu, dgv * _gelu_grad(bv)], axis=1).astype(dproj_ref.dtype)

        @pl.when(i == n - 1)
        def _():
            dbs_ref[...] = jnp.sum(dbias_acc[...].T.reshape(n_groups, HEAD, c), axis=1)

    col = lambda j: pl.BlockSpec((c, HG_W), lambda i, j=j: (i, j))
    return pl.pallas_call(
        body, name=name, grid=(n,),
        in_specs=[col(4), col(5), _const_spec(lg), _const_spec(lbias), _const_spec(w_s), _const_spec(bias_full),
                  col(1)],
        out_specs=[pl.BlockSpec((c, 2 * HG_W), lambda i: (i, 0)), _const_spec(lg), _const_spec(lbias),
                   _const_spec(w_s), pl.BlockSpec((n_groups, c), lambda i: (0, 0))],
        out_shape=[jax.ShapeDtypeStruct((t_rows, 2 * HG_W), BF16), jax.ShapeDtypeStruct(lg.shape, F32),
                   jax.ShapeDtypeStruct(lbias.shape, F32), jax.ShapeDtypeStruct(w_s.shape, F32),
                   jax.ShapeDtypeStruct((n_groups, c), F32)],
        scratch_shapes=[pltpu.VMEM((c, HG_W), F32)],
        compiler_params=_cparams(("arbitrary",)),
    )(projp, projp, lg, lbias, w_s, bias_full, dcat)


def _rope_tables(positions):
    t = positions.shape[0]
    inv_freq = ROPE_THETA ** (-jnp.arange(0, 32, 2, dtype=F32) / 32)
    ang = positions.astype(F32)[:, None] * inv_freq
    cos, sin = jnp.cos(ang), jnp.sin(ang)
    z = lambda w: jnp.zeros((t, w), F32)
    cos_t = jnp.concatenate([jnp.ones((t, 64), F32), cos, cos, z(32)], axis=1)
    sin_up = jnp.concatenate([z(80), sin, z(32)], axis=1)
    sin_dn = jnp.concatenate([z(64), -sin, z(48)], axis=1)
    return cos_t, sin_up, sin_dn


def _rep(x, n):
    return x if n == 1 else jnp.concatenate([x] * n, axis=1)


def _rope(x, cos_t, sin_up, sin_dn):
    w = x.shape[1]
    return x * cos_t + pltpu.roll(x, 16, 1) * sin_up + pltpu.roll(x, w - 16, 1) * sin_dn


def _rope_t(dy, cos_t, sin_up, sin_dn):
    w = dy.shape[1]
    return dy * cos_t + pltpu.roll(dy * sin_up, w - 16, 1) + pltpu.roll(dy * sin_dn, 16, 1)


def _mla_prep(q, kv, projp, tables, *, name):
    nh = N_ATT_HEADS

    def fn(qv, kvv, kr, cos_t, sin_up, sin_dn):
        qr = _rope(qv, _rep(cos_t, nh), _rep(sin_up, nh), _rep(sin_dn, nh))
        krr = _rope(kr, cos_t, sin_up, sin_dn)
        lane = lax.broadcasted_iota(jnp.int32, kvv.shape, 1) % LANES
        return qr, jnp.where(lane < HEAD, kvv, 0.0) + _rep(krr, nh), kvv

    w = q.shape[1]
    return _rowwise(fn, [q, kv, (projp, LANES, P_KR // LANES)] + list(tables), [],
                    [(w, BF16), (w, BF16), (w, BF16)], name=name)


def _mla_prep_bwd(dqr, dkf, tables, *, name):
    nh = N_ATT_HEADS

    def fn(dq, dk, cos_t, sin_up, sin_dn):
        dqp = _rope_t(dq, _rep(cos_t, nh), _rep(sin_up, nh), _rep(sin_dn, nh))
        dkrr = dk[:, 0:LANES]
        for h in range(1, nh):
            dkrr = dkrr + dk[:, LANES * h:LANES * (h + 1)]
        return dqp, _rope_t(dkrr, cos_t, sin_up, sin_dn)

    return _rowwise(fn, [dqr, dkf] + list(tables), [], [(dqr.shape[1], BF16), (LANES, BF16)], name=name)


_LOG2E = 1.0 / math.log(2.0)
_NT = (((1,), (1,)), ((), ()))
_TN = (((0,), (0,)), ((), ()))


def _attn_fwd(qr, kf, kvb, *, name, job=None):
    t_rows = qr.shape[0]
    tq = min(ATT_TQ, t_rows)
    nb = t_rows // tq
    scale = ATT_D ** -0.5

    c2 = scale * _LOG2E

    def body(q_ref, kf_ref, kvb_ref, o_ref, lse_ref):
        qi = pl.program_id(1)
        lane = lax.broadcasted_iota(jnp.int32, (tq, LANES), 1)
        causal_t = (lax.broadcasted_iota(jnp.int32, (tq, tq), 0) <= lax.broadcasted_iota(jnp.int32, (tq, tq), 1))
        heads = [slice(hh * LANES, (hh + 1) * LANES) for hh in range(2)]
        qs = [q_ref[:, cols] for cols in heads]

        def block(first, n_keys, carry, diagonal):
            rows = pl.ds(pl.multiple_of(first * tq, tq), n_keys)
            new = []
            for q, cols, (m_old, l_old, acc_t) in zip(qs, heads, carry):
                s_t = lax.dot_general(kf_ref[rows, cols], q, _NT, preferred_element_type=F32)
                if diagonal:
                    s_t = jnp.where(causal_t, s_t, -1e30)
                m_new = jnp.maximum(m_old, jnp.max(s_t, axis=0, keepdims=True))
                p_t = jnp.exp2((s_t - m_new) * c2)
                a = jnp.exp2((m_old - m_new) * c2)
                pv_t = lax.dot_general(kvb_ref[rows, cols], p_t.astype(BF16), _TN, preferred_element_type=F32)
                new.append((m_new, a * l_old + jnp.sum(p_t, axis=0, keepdims=True), a * acc_t + pv_t))
            return tuple(new)

        init = (jnp.full((1, tq), -1e30, F32), jnp.zeros((1, tq), F32), jnp.zeros((LANES, tq), F32))
        carry = lax.fori_loop(0, qi // 4, lambda g, c: block(4 * g, 4 * tq, c, False), (init, init))
        carry = lax.cond((qi // 2) % 2 == 1, lambda c: block(4 * (qi // 4), 2 * tq, c, False), lambda c: c, carry)
        carry = lax.cond(qi % 2 == 1, lambda c: block(qi - 1, tq, c, False), lambda c: c, carry)
        outs = []
        for hh, (m_fin, l_fin, acc_t) in enumerate(block(qi, tq, carry, True)):
            lse_ref[hh] = m_fin * scale + jnp.log(l_fin)
            outs.append((acc_t / l_fin).T)
        o_ref[...] = jnp.where(lane < HEAD, pltpu.roll(outs[0], HEAD, 1), outs[1])

    pair = pl.BlockSpec((t_rows, 2 * LANES), lambda pr, qi: (0, pr))
    return _call(
        body, (qr, kf, kvb), name=name, grid=(N_ATT_HEADS // 2, nb),
        in_specs=[pl.BlockSpec((tq, 2 * LANES), lambda pr, qi: (qi, pr)), pair, pair],
        out_specs=[pl.BlockSpec((tq, LANES), lambda pr, qi: (qi, pr)),
                   pl.BlockSpec((2, 1, tq), lambda pr, qi: (pr, 0, qi))],
        out_shape=[jax.ShapeDtypeStruct((t_rows, N_ATT_HEADS * HEAD), F32),
                   jax.ShapeDtypeStruct((N_ATT_HEADS, 1, t_rows), F32)],
        sem=("parallel", "arbitrary"), job=job)


def _attn_bwd(qr, kf, kvb, dcat, o, lse, *, name, job=None):
    t_rows = qr.shape[0]
    tq = min(ATT_TQ, t_rows)
    nb = t_rows // tq
    scale = ATT_D ** -0.5
    c2 = scale * _LOG2E
    do_off = 2 * HG_W // LANES

    def body(q_ref, kf_ref, kvb_ref, do_ref, o_ref, lse_ref, dq_ref, dkv_ref, dk_ref):
        ki = pl.program_id(1)

        @pl.when(ki == 0)
        def _():
            dq_ref[...] = jnp.zeros_like(dq_ref)

        lane = lax.broadcasted_iota(jnp.int32, (tq, LANES), 1)
        causal_t = (lax.broadcasted_iota(jnp.int32, (tq, tq), 0) <= lax.broadcasted_iota(jnp.int32, (tq, tq), 1))
        heads = [slice(hh * LANES, (hh + 1) * LANES) for hh in range(2)]
        ks = [kf_ref[:, cols] for cols in heads]
        vs = [kvb_ref[:, cols] for cols in heads]

        def block(qi, n_q, carry, diagonal):
            rows = pl.ds(pl.multiple_of(qi * tq, tq), n_q)
            do_pair, o_pair = do_ref[rows, :], o_ref[rows, :]
            upper = lax.broadcasted_iota(jnp.int32, do_pair.shape, 1) >= HEAD
            new = []
            for hh, (cols, k, v, (dk, dv)) in enumerate(zip(heads, ks, vs, carry)):
                q = q_ref[rows, cols]
                do, ov = (pltpu.roll(do_pair, HEAD, 1), pltpu.roll(o_pair, HEAD, 1)) if hh == 0 else (do_pair, o_pair)
                do = jnp.where(upper, do, 0.0)
                delta = jnp.sum((do * ov).T, axis=0, keepdims=True)
                s_t = lax.dot_general(k, q, _NT, preferred_element_type=F32)
                if diagonal:
                    s_t = jnp.where(causal_t, s_t, -1e30)
                p_t = jnp.exp2(s_t * c2 - lse_ref[hh, :, rows] * _LOG2E)
                dob = do.astype(BF16)
                dv = dv + jnp.dot(p_t.astype(BF16), dob, preferred_element_type=F32)
                dp_t = lax.dot_general(v, dob, _NT, preferred_element_type=F32)
                ds_t = (p_t * (dp_t - delta) * scale).astype(BF16)
                dk = dk + jnp.dot(ds_t, q, preferred_element_type=F32)
                dq_ref[rows, cols] += lax.dot_general(ds_t, k, _TN, preferred_element_type=F32)
                new.append((dk, dv))
            return tuple(new)

        zero = jnp.zeros((tq, LANES), F32)
        carry = block(ki, tq, ((zero, zero), (zero, zero)), True)
        rest = nb - 1 - ki
        carry = lax.fori_loop(0, rest // 2, lambda g, c: block(ki + 1 + 2 * g, 2 * tq, c, False), carry)
        carry = lax.cond(rest % 2 == 1, lambda c: block(nb - 1, tq, c, False), lambda c: c, carry)
        dkv_ref[...] = jnp.concatenate([jnp.where(lane < HEAD, dk, dv) for dk, dv in carry],
                                       axis=1).astype(dkv_ref.dtype)
        dk_ref[...] = jnp.concatenate([dk for dk, _ in carry], axis=1)

    pair_all = pl.BlockSpec((t_rows, 2 * LANES), lambda pr, ki: (0, pr))
    pair_blk = pl.BlockSpec((tq, 2 * LANES), lambda pr, ki: (ki, pr))
    wide = jax.ShapeDtypeStruct((t_rows, N_ATT_HEADS * LANES), F32)
    return _call(
        body, (qr, kf, kvb, dcat, o, lse), name=name, grid=(N_ATT_HEADS // 2, nb),
        in_specs=[pair_all, pair_blk, pair_blk,
                  pl.BlockSpec((t_rows, LANES), lambda pr, ki: (0, do_off + pr)),
                  pl.BlockSpec((t_rows, LANES), lambda pr, ki: (0, pr)),
                  pl.BlockSpec((2, 1, t_rows), lambda pr, ki: (pr, 0, 0))],
        out_specs=[pair_all, pair_blk, pair_blk],
        out_shape=[wide, jax.ShapeDtypeStruct(wide.shape, BF16), wide],
        sem=("parallel", "arbitrary"), job=job)


def _my_pos():
    return lax.axis_index("x"), lax.axis_index("y"), lax.axis_index("c")


def _all_gather(xs, *, name, columns=True):
    return _gather_forward(_run_job(_gather_job(xs, columns), name=name), name=name + "_forward")


def _remote(src, dst, send_sems, recv_sems, k, dev):
    return pltpu.make_async_remote_copy(src_ref=src, dst_ref=dst, send_sem=send_sems.at[k], recv_sem=recv_sems.at[k],
                                        device_id=dev, device_id_type=MESH)


def _block(ref, idx):
    if len(ref.shape) == 2:
        return ref.at[:, pl.ds(pl.multiple_of(idx * LANES, LANES), LANES)]
    return ref.at[idx]


def _gather_job(xs, columns=True):
    n = len(xs)

    def make(x_refs, out_refs, send_sems, recv_sems, local_sems):
        mx, my, mc = _my_pos()
        mine = 4 * mx + 2 * my + mc
        peers = [(mx, my, 1 - mc), (1 - mx, my, mc), (mx, 1 - my, mc), (1 - mx, 1 - my, mc)]
        sends, recvs, local = [], [], []
        for a in range(n):
            local.append(pltpu.make_async_copy(x_refs[a], _block(out_refs[a], mine), local_sems.at[a]))
            for k, dev in enumerate(peers):
                theirs = 4 * dev[0] + 2 * dev[1] + dev[2]
                sends.append(_remote(x_refs[a], _block(out_refs[a], mine), send_sems, recv_sems, 4 * a + k, dev))
                recvs.append(_remote(x_refs[a], _block(out_refs[a], theirs), send_sems, recv_sems, 4 * a + k, dev))
        return sends, recvs, local

    def gathered(x):
        if columns and x.ndim == 2 and x.shape[1] == LANES:
            return jax.ShapeDtypeStruct((x.shape[0], N_DEV * LANES), x.dtype)
        return jax.ShapeDtypeStruct((N_DEV,) + x.shape, x.dtype)

    return _copies_job(xs, [gathered(x) for x in xs], 4 * n, n, make)


def _forward_job(gs):
    n = len(gs)

    def make(in_refs, out_refs, send_sems, recv_sems, local_sems):
        mx, my, mc = _my_pos()
        chips = [(1 - mx, my), (mx, 1 - my), (1 - mx, 1 - my)]
        sends, recvs = [], []
        for a in range(n):
            for j, (cx, cy) in enumerate(chips):
                here = _block(out_refs[a], 4 * cx + 2 * cy + mc)
                there = _block(out_refs[a], 4 * cx + 2 * cy + 1 - mc)
                sends.append(_remote(here, here, send_sems, recv_sems, 3 * a + j, (mx, my, 1 - mc)))
                recvs.append(_remote(here, there, send_sems, recv_sems, 3 * a + j, (mx, my, 1 - mc)))
        return sends, recvs, []

    shapes = [jax.ShapeDtypeStruct(g.shape, g.dtype) for g in gs]
    return _copies_job(gs, shapes, 3 * n, 0, make, in_place=True)


def _gather_forward(gs, *, name):
    return _run_job(_forward_job(gs), name=name)


def _pair_job(xs):
    n = len(xs)

    def make(x_refs, out_refs, send_sems, recv_sems, local_sems):
        mx, my, mc = _my_pos()

        def src(ref, g):
            return _block(ref, 2 * g + 1 - mc) if len(ref.shape) == 2 else ref.at[g, 1 - mc]

        copies = [_remote(src(x_refs[a], g), out_refs[a].at[g], send_sems, recv_sems, 4 * a + g, (mx, my, 1 - mc))
                  for a in range(n) for g in range(4)]
        return copies, copies, []

    shapes = [jax.ShapeDtypeStruct((4, x.shape[0], LANES) if x.ndim == 2 else (4,) + x.shape[2:], x.dtype)
              for x in xs]
    return _copies_job(xs, shapes, 4 * n, 0, make)


def _pair_add(x, r, core, *, name):
    _, a, b = r.shape
    ta = _row_tile(a, 256)

    def body(c_ref, x_ref, r_ref, o_ref):
        o_ref[...] = (x_ref[...] + r_ref[...]).astype(o_ref.dtype)

    blk = pl.BlockSpec((None, ta, b), lambda g, i, c_ref: (g, i, 0))
    own = (pl.BlockSpec((ta, b), lambda g, i, c_ref: (i, 2 * g + c_ref[0])) if x.ndim == 2
           else pl.BlockSpec((None, None, ta, b), lambda g, i, c_ref: (g, c_ref[0], i, 0)))
    return pl.pallas_call(
        body, name=name,
        grid_spec=pltpu.PrefetchScalarGridSpec(
            num_scalar_prefetch=1, grid=(4, a // ta), in_specs=[own, blk], out_specs=blk),
        out_shape=jax.ShapeDtypeStruct((4, a, b), BF16),
        compiler_params=_cparams(("parallel", "parallel")),
    )(core, x, r)


def _quad_job(xs):
    n = len(xs)

    def make(x_refs, out_refs, send_sems, recv_sems, local_sems):
        mx, my, mc = _my_pos()
        mine = 2 * mx + my
        peers = [((1 - mx, my, mc), 2 * (1 - mx) + my), ((mx, 1 - my, mc), 2 * mx + 1 - my),
                 ((1 - mx, 1 - my, mc), 2 * (1 - mx) + 1 - my)]
        sends, recvs, local = [], [], []
        for a in range(n):
            local.append(pltpu.make_async_copy(x_refs[a].at[mine], out_refs[a].at[mine], local_sems.at[a]))
            for k, (dev, g) in enumerate(peers):
                sends.append(_remote(x_refs[a].at[g], out_refs[a].at[mine], send_sems, recv_sems, 3 * a + k, dev))
                recvs.append(_remote(x_refs[a].at[g], out_refs[a].at[g], send_sems, recv_sems, 3 * a + k, dev))
        return sends, recvs, local

    shapes = [jax.ShapeDtypeStruct(x.shape, x.dtype) for x in xs]
    return _copies_job(xs, shapes, 3 * n, n, make)


def _row_tile(r, pref):
    t = min(pref, r)
    while r % t or (t % 8 and t != r):
        t -= 1
    return t


def _adamw(parts, w, m, v, layer, *, name, tile=256, into=None):
    g, a, b = parts.shape
    tile = _row_tile(a, tile)
    c1 = 1.0 / (1.0 - ADAM_B1 ** ADAM_STEP)
    c2 = 1.0 / (1.0 - ADAM_B2 ** ADAM_STEP)
    into = tuple(into or ())

    def body(p_ref, w_ref, m_ref, v_ref, *refs):
        g_ref, d_ref, mo_ref, vo_ref = refs[len(into):]
        grad = p_ref[0].astype(F32)
        for j in range(1, g):
            grad = grad + p_ref[j].astype(F32)
        mn = ADAM_B1 * m_ref[...] + (1.0 - ADAM_B1) * grad
        vn = ADAM_B2 * v_ref[...] + (1.0 - ADAM_B2) * (grad * grad)
        g_ref[...] = grad
        mo_ref[...] = mn
        vo_ref[...] = vn
        d_ref[...] = -ADAM_LR * ((mn * c1) / (jnp.sqrt(vn * c2) + ADAM_EPS) + ADAM_WD * w_ref[...])

    if layer is None:
        src, shape = pl.BlockSpec((tile, b), lambda i: (i, 0)), (a, b)
    else:
        src, shape = pl.BlockSpec((None, tile, b), lambda i: (layer, i, 0)), w.shape
    return pl.pallas_call(
        body, name=name, grid=(a // tile,),
        in_specs=[pl.BlockSpec((g, tile, b), lambda i: (0, i, 0)), src, src, src] + [_ANY] * len(into),
        out_specs=[src] * 4,
        out_shape=[jax.ShapeDtypeStruct(shape, F32)] * 4,
        input_output_aliases={4 + i: i for i in range(len(into))},
        compiler_params=_cparams(("parallel",)),
    )(parts, w, m, v, *into)


W_IN_SHARD = 276


def _w_in_dest(col):
    return jnp.where(col < P_KR, col, jnp.where(col < P_KR + 256, col + (P_CKV - P_KR), col - 2176 + P_KR + HEAD))


PLACE_TILE = 384
PLACE_SHARDS = 3
PICK_TILE = 128
PICK_TILES = 4


def _w_in_tables():
    col = np.arange(N_DEV * W_IN_SHARD)
    dest = np.where(col < P_KR, col, np.where(col < P_KR + 256, col + (P_CKV - P_KR), col - 2176 + P_KR + HEAD))
    shard = col // W_IN_SHARD

    def filled(used, universe, n):
        used = sorted(set(int(u) for u in used))
        assert len(used) <= n, used
        return used + [u for u in universe if u not in used][:n - len(used)]

    place = [filled(shard[dest // PLACE_TILE == c], range(N_DEV), PLACE_SHARDS) for c in range(P_COLS // PLACE_TILE)]
    pick = [filled(dest[shard == j] // PICK_TILE, range(P_COLS // PICK_TILE), PICK_TILES) for j in range(N_DEV)]
    return np.asarray(place, np.int32).reshape(-1), np.asarray(pick, np.int32).reshape(-1)


def _place_w_in(g, *, name):
    _, d, sh = g.shape
    tc, ns = PLACE_TILE, PLACE_SHARDS
    table = jnp.asarray(_w_in_tables()[0])

    def body(tab_ref, g_ref, o_ref, acc_ref):
        ct, s = pl.program_id(0), pl.program_id(1)
        j = tab_ref[ct * ns + s]

        @pl.when(s == 0)
        def _():
            acc_ref[...] = jnp.zeros_like(acc_ref)

        src = j * sh + lax.broadcasted_iota(jnp.int32, (sh, tc), 0)
        dst = ct * tc + lax.broadcasted_iota(jnp.int32, (sh, tc), 1)
        place = (_w_in_dest(src) == dst).astype(BF16)
        acc_ref[...] += jnp.dot(g_ref[...], place, preferred_element_type=F32)

        @pl.when(s == ns - 1)
        def _():
            o_ref[...] = acc_ref[...].astype(o_ref.dtype)

    return pl.pallas_call(
        body, name=name,
        grid_spec=pltpu.PrefetchScalarGridSpec(
            num_scalar_prefetch=1, grid=(P_COLS // tc, ns),
            in_specs=[pl.BlockSpec((None, d, sh), lambda ct, s, tab: (tab[ct * ns + s], 0, 0))],
            out_specs=pl.BlockSpec((d, tc), lambda ct, s, tab: (0, ct)),
            scratch_shapes=[pltpu.VMEM((d, tc), F32)]),
        out_shape=jax.ShapeDtypeStruct((d, P_COLS), BF16),
        compiler_params=_cparams(("parallel", "arbitrary")),
    )(table, g)


def _unplace_w_in(dw, *, name):
    d = dw.shape[0]
    sh, tk, nt = W_IN_SHARD, PICK_TILE, PICK_TILES
    table = jnp.asarray(_w_in_tables()[1])

    def body(tab_ref, dw_ref, o_ref):
        j, kk = pl.program_id(0), pl.program_id(1)
        tile = tab_ref[j * nt + kk]
        src = j * sh + lax.broadcasted_iota(jnp.int32, (tk, sh), 1)
        dst = tile * tk + lax.broadcasted_iota(jnp.int32, (tk, sh), 0)
        pick = (_w_in_dest(src) == dst).astype(BF16)
        part = _split_dot(dw_ref[...], pick)

        @pl.when(kk == 0)
        def _():
            o_ref[...] = part

        @pl.when(kk > 0)
        def _():
            o_ref[...] += part

    return pl.pallas_call(
        body, name=name,
        grid_spec=pltpu.PrefetchScalarGridSpec(
            num_scalar_prefetch=1, grid=(N_DEV, nt),
            in_specs=[pl.BlockSpec((d, tk), lambda j, kk, tab: (0, tab[j * nt + kk]))],
            out_specs=pl.BlockSpec((None, d, sh), lambda j, kk, tab: (j, 0, 0))),
        out_shape=jax.ShapeDtypeStruct((N_DEV, d, sh), F32),
        compiler_params=_cparams(("parallel", "arbitrary")),
    )(table, dw)


def _gate_up_swiglu(h1, wgu, *, name):
    t_rows, k = h1.shape
    w = wgu.shape[2]
    tm = _tile(t_rows, 1024)

    def body(a_ref, wg_ref, wu_ref, gu_ref, act_ref):
        a = a_ref[...].astype(BF16)
        gate = jnp.dot(a, wg_ref[...], preferred_element_type=F32)
        up = jnp.dot(a, wu_ref[...], preferred_element_type=F32)
        gu_ref[0] = gate.astype(gu_ref.dtype)
        gu_ref[1] = up.astype(gu_ref.dtype)
        act_ref[...] = (gate * _sigmoid(gate) * up).astype(act_ref.dtype)

    return pl.pallas_call(
        body, name=name, grid=(t_rows // tm, 4),
        in_specs=[pl.BlockSpec((tm, k), lambda i, j: (i, 0)),
                  pl.BlockSpec((None, k, w), lambda i, j: (j, 0, 0)),
                  pl.BlockSpec((None, k, w), lambda i, j: (j + 4, 0, 0))],
        out_specs=[pl.BlockSpec((2, None, tm, w), lambda i, j: (0, j, i, 0)),
                   pl.BlockSpec((None, tm, w), lambda i, j: (j, i, 0))],
        out_shape=[jax.ShapeDtypeStruct((2, 4, t_rows, w), BF16), jax.ShapeDtypeStruct((4, t_rows, w), BF16)],
        compiler_params=_cparams(("parallel", "arbitrary")),
    )(h1, wgu, wgu)


def _down_dx_swiglu(dffn, wdown, gu, *, name):
    t_rows, k = dffn.shape
    w = gu.shape[3]
    tm = _tile(t_rows, 1024)

    def body(d_ref, w_ref, gu_ref, o_ref):
        dact = lax.dot_general(d_ref[...].astype(BF16), w_ref[...], _NT, preferred_element_type=F32)
        gate, up = gu_ref[0].astype(F32), gu_ref[1].astype(F32)
        sg = _sigmoid(gate)
        o_ref[0] = (dact * up * (sg * (1.0 + gate * (1.0 - sg)))).astype(o_ref.dtype)
        o_ref[1] = (dact * gate * sg).astype(o_ref.dtype)

    blk = pl.BlockSpec((2, None, tm, w), lambda i, j: (0, j, i, 0))
    return pl.pallas_call(
        body, name=name, grid=(t_rows // tm, 4),
        in_specs=[pl.BlockSpec((tm, k), lambda i, j: (i, 0)), pl.BlockSpec((w, k), lambda i, j: (j, 0)), blk],
        out_specs=blk, out_shape=jax.ShapeDtypeStruct(gu.shape, BF16),
        compiler_params=_cparams(("parallel", "arbitrary")),
    )(dffn, wdown, gu)


BIG = ("w_in", "mla_w_uq", "mla_w_ukv", "w_out", "w_gate_up", "w_down", "ple_w_gate", "ple_w_proj")
SMALL = ("ln_in_g", "ln_in_b", "hgrn_lb_logits", "hgrn_norm_g", "sgu_ln_g", "sgu_ln_b", "sgu_w_s", "sgu_b_s",
         "mla_q_norm_g", "mla_kv_norm_g", "ln1_g", "ln1_b", "ln2_g", "ln2_b")
ORDER = ("ln_in_g", "ln_in_b", "w_in", "hgrn_lb_logits", "hgrn_norm_g", "sgu_ln_g", "sgu_ln_b", "sgu_w_s", "sgu_b_s",
         "mla_q_norm_g", "mla_w_uq", "mla_kv_norm_g", "mla_w_ukv", "w_out", "ln1_g", "ln1_b", "w_gate_up", "w_down",
         "ple_w_gate", "ple_w_proj", "ln2_g", "ln2_b")


def _slab(a, align):
    s = a.reshape(-1, LANES)
    pad = -s.shape[0] % align
    return jnp.pad(s, ((0, pad), (0, 0))) if pad else s


def _pack(arrays, align=16, total_align=512):
    s = jnp.concatenate([_slab(a, align) for a in arrays], axis=0)
    pad = -s.shape[0] % total_align
    return jnp.pad(s, ((0, pad), (0, 0))) if pad else s


def _unpack(slab, shapes, align=16):
    out, r0 = [], 0
    for s in shapes:
        nr = math.prod(s) // LANES
        out.append(slab[r0:r0 + nr].reshape(s))
        r0 += nr + (-nr % align)
    return out


def _weight_shards(w, li):
    uq_pad = ((0, 0), (0, LANES - ATT_D))
    shards = {k: w[k][li] for k in BIG}
    shards["mla_w_uq"] = jnp.pad(shards["mla_w_uq"], uq_pad)
    return {k: s.astype(BF16) for k, s in shards.items()}


def _usable_weights(g, *, name):
    out = {}
    for k, a in g.items():
        if k == "w_in":
            out[k] = _place_w_in(a, name=name + "_place_w_in")
        elif k in ("w_out", "w_down", "ple_w_gate"):
            out[k] = a.reshape(a.shape[0] * a.shape[1], a.shape[2])
        else:
            out[k] = a
    return out


BY_COLUMNS = ("mla_w_uq", "mla_w_ukv", "ple_w_proj")


def _as_pairs(k, g):
    if k in BY_COLUMNS:
        return g
    if g.ndim == 2:
        return g.reshape((4, 2, g.shape[0] // N_DEV) + g.shape[1:])
    return g.reshape((4, 2) + g.shape[1:])


def _twice(fn):
    return lambda *a: fn(*a) * 2


def _layer_forward(li, h, hb, p_i, wts, sm, lbs, tables, alpha, hgrn_job=None, after_hgrn=None, attn_job=None,
                   after_attn=None, loss_target=None):
    n = f"l{li}_"
    row1 = lambda a: a.reshape(1, -1)
    projp = _mm(hb, wts["w_in"], name=n + "proj")
    ng = row1(sm["hgrn_norm_g"][li])
    res = _hgrn_fwd(projp, lbs[li], ng, name=n + "hgrn_fwd", job=hgrn_job)
    if hgrn_job is not None:
        res, got = res
    o_a, o_pre, states = res
    lg, lbias = row1(sm["sgu_ln_g"][li]), row1(sm["sgu_ln_b"][li])
    w_s = sm["sgu_w_s"][li]
    bias_full = jnp.repeat(sm["sgu_b_s"][li].T, HEAD, axis=1)
    o_b = _sgu_fwd(projp, lg, lbias, w_s, bias_full, name=n + "sgu_fwd",
                   job=None if hgrn_job is None else _forward_job(got))
    if hgrn_job is not None:
        o_b, got = o_b
        wts = dict(wts, **after_hgrn(got))
    qg, kvg = row1(sm["mla_q_norm_g"][li]), row1(sm["mla_kv_norm_g"][li])
    cq_view, ckv_view = (projp, 384, P_CQ // 384), (projp, 256, P_CKV // 256)
    (cqn,) = _rowwise(_fn_rms, [cq_view], [qg], [(384, BF16)], name=n + "q_norm")
    (ckvn,) = _rowwise(_fn_rms, [ckv_view], [kvg], [(256, BF16)], name=n + "kv_norm")
    q = _mm(cqn, wts["mla_w_uq"], name=n + "uq")
    kv = _mm(ckvn, wts["mla_w_ukv"], name=n + "ukv")
    qr, kf, kvb = _mla_prep(q, kv, projp, tables, name=n + "mla_prep")
    res = _attn_fwd(qr, kf, kvb, name=n + "attn_fwd", job=attn_job)
    if attn_job is not None:
        res, got = res
    o_c, lse = res
    cat = jnp.concatenate([o_a, o_b, o_c.astype(BF16)], axis=1)
    mix = _mm(cat, wts["w_out"], name=n + "out_proj", job=None if attn_job is None else _forward_job(got))
    if attn_job is not None:
        mix, got = mix
        wts = dict(wts, **after_attn(got))
    g1, b1 = row1(sm["ln1_g"][li]), row1(sm["ln1_b"][li])
    d = h.shape[1]
    h1, h1b = _rowwise(_twice(_make_post_mix(alpha)), [h, mix], [g1, b1], [(d, F32), (d, BF16)], name=n + "ln1")
    gu, act = _gate_up_swiglu(h1b, wts["w_gate_up"], name=n + "gate_up")
    ffn = _mm_kblocks(act, wts["w_down"], bm="kn", tm=1024, name=n + "down")
    pg = _mm(h1b, wts["ple_w_gate"], name=n + "ple_gate")
    pp = _mm(p_i, wts["ple_w_proj"], name=n + "ple_proj")
    g2, b2 = row1(sm["ln2_g"][li]), row1(sm["ln2_b"][li])
    if loss_target is None:
        out = _rowwise(_twice(_make_ple_ln(alpha)), [h1, ffn, pg, pp], [g2, b2], [(d, F32), (d, BF16)],
                       name=n + "ln2")
    else:
        def ln_and_loss(h1v, ffnv, pgv, ppv, tv, gv, bv):
            err = _make_ple_ln(alpha)(h1v, ffnv, pgv, ppv, gv, bv)[0] - tv
            return err * (1.0 / d), 0.5 * jnp.sum(jnp.mean(err * err, axis=-1, keepdims=True), axis=0, keepdims=True)

        out = _rowwise(ln_and_loss, [h1, ffn, pg, pp, loss_target], [g2, b2], [(d, F32)], accs=[(1, 1)],
                       name=n + "ln2_loss")
    saved = dict(h=h, hb=hb, h1b=h1b, projp=projp, o_pre=o_pre, states=states, cqn=cqn, ckvn=ckvn, qr=qr, kf=kf, kvb=kvb, o_c=o_c,
                 lse=lse, cat=cat, mix=mix, h1=h1, gu=gu, act=act, ffn=ffn, pg=pg, pp=pp, ng=ng, lg=lg, wts=wts,
                 lbias=lbias, w_s=w_s, bias_full=bias_full, qg=qg, kvg=kvg, g1=g1, b1=b1, g2=g2, b2=b2)
    return tuple(out), saved


RS_EARLY = ("ple_w_proj", "ple_w_gate", "w_down", "w_gate_up", "w_out")
RS_LATE = ("mla_w_uq", "mla_w_ukv", "w_in")


def _layer_backward(li, dh2_parts, p_i, sv, lbs, tables, alpha, core, carried=None):
    n = f"l{li}_b_"
    wts = sv["wts"]
    gr = {}
    dh1_a, dffn, dpg, dpp, gr["ln2_g"], gr["ln2_b"] = _rowwise_vjp(
        _make_ple_ln(alpha), [sv["h1"], sv["ffn"], sv["pg"], sv["pp"]], [sv["g2"], sv["b2"]], [dh2_parts],
        groups=[[0], [1], [2], [3]], gdtypes=[F32, BF16, BF16, BF16], name=n + "ln2")
    big = {}
    big["ple_w_proj"] = _mm(p_i, dpp, am="km", tk=2048, name=n + "ple_proj_dw")
    big["ple_w_gate"] = _mm(sv["h1b"], dpg, am="km", name=n + "ple_gate_dw")
    dh1_b = _mm(dpg, wts["ple_w_gate"], bm="nk", name=n + "ple_gate_dx")
    big["w_down"] = _mm(sv["act"], dffn, am="bkm", tk=4096, name=n + "down_dw")
    dgu = _down_dx_swiglu(dffn, wts["w_down"], sv["gu"], name=n + "down_dx")
    dgu = dgu.reshape((N_DEV,) + dgu.shape[2:])
    big["w_gate_up"], carried_got = _mm(sv["h1b"], dgu, am="km", bm="bkn", om="bmn", tk=4096, name=n + "gate_up_dw",
                                        job=carried), None
    if carried is not None:
        big["w_gate_up"], carried_got = big["w_gate_up"]
    early = [_as_pairs(k, big[k]) for k in RS_EARLY[:-1]]
    dh1_c, theirs = _mm_kblocks(dgu, wts["w_gate_up"], bm="bnk", tm=512, name=n + "gate_up_dx",
                                job=_pair_job(early))
    dh_a, dmix, gr["ln1_g"], gr["ln1_b"] = _rowwise_vjp(
        _make_post_mix(alpha), [sv["h"], sv["mix"]], [sv["g1"], sv["b1"]], [[dh1_a, dh1_b, dh1_c]],
        groups=[[0], [1]], gdtypes=[F32, BF16], name=n + "ln1")
    big["w_out"] = _mm(sv["cat"], dmix, am="km", name=n + "out_proj_dw")
    early.append(_as_pairs("w_out", big["w_out"]))
    dcat, their_w_out = _mm(dmix, wts["w_out"], bm="nk", name=n + "out_proj_dx", job=_pair_job(early[-1:]))
    sums = [_pair_add(x, r, core, name=n + "pair_add_" + k)
            for k, x, r in zip(RS_EARLY, early, list(theirs) + list(their_w_out))]

    (dqr, dkv, dkf), early_quads = _attn_bwd(sv["qr"], sv["kf"], sv["kvb"], dcat, sv["o_c"], sv["lse"],
                                             name=n + "attn", job=_quad_job(sums))
    dqpad, dkr = _mla_prep_bwd(dqr, dkf, tables, name=n + "mla_prep")
    big["mla_w_uq"] = _mm(sv["cqn"], dqpad, am="km", tk=2048, name=n + "uq_dw")
    dcqn = _mm(dqpad, wts["mla_w_uq"], bm="nk", name=n + "uq_dx")
    big["mla_w_ukv"] = _mm(sv["ckvn"], dkv, am="km", tk=2048, name=n + "ukv_dw")
    dckvn = _mm(dkv, wts["mla_w_ukv"], bm="nk", name=n + "ukv_dx")
    projp = sv["projp"]
    dcq, gr["mla_q_norm_g"] = _rowwise_vjp(_fn_rms, [(projp, 384, P_CQ // 384)], [sv["qg"]], [[dcqn]],
                                           groups=[[0]], gdtypes=[BF16], name=n + "q_norm")
    dckv, gr["mla_kv_norm_g"] = _rowwise_vjp(_fn_rms, [(projp, 256, P_CKV // 256)], [sv["kvg"]], [[dckvn]],
                                             groups=[[0]], gdtypes=[BF16], name=n + "kv_norm")
    dsgu, gr["sgu_ln_g"], gr["sgu_ln_b"], gr["sgu_w_s"], gr["sgu_b_s"] = _sgu_bwd(
        projp, sv["lg"], sv["lbias"], sv["w_s"], sv["bias_full"], dcat, name=n + "sgu")
    dhg, gr["hgrn_norm_g"], gr["lower_bound"] = _hgrn_bwd(
        projp, lbs[li], sv["ng"], sv["o_pre"], sv["states"], dcat, name=n + "hgrn")
    dprojp = jnp.concatenate([dhg, dsgu, dcq, dkr, dckv], axis=1)
    big["w_in"] = _unplace_w_in(_mm(sv["hb"], dprojp, am="km", tk=4096, name=n + "proj_dw"),
                                name=n + "proj_dw_shards")
    late = [_as_pairs(k, big[k]) for k in RS_LATE]
    dh_b, theirs = _mm(dprojp, wts["w_in"], bm="nk", tk=P_COLS, name=n + "proj_dx", job=_pair_job(late))
    late_sums = [_pair_add(x, r, core, name=n + "pair_add_" + k) for k, x, r in zip(RS_LATE, late, theirs)]
    return [dh_a, dh_b], gr, early_quads, late_sums, carried_got


def kernel(x, p, positions, ln_in_g, ln_in_b, w_in, hgrn_lb_logits, hgrn_norm_g, sgu_ln_g, sgu_ln_b, sgu_w_s, sgu_b_s, mla_q_norm_g, mla_w_uq, mla_kv_norm_g, mla_w_ukv, w_out, ln1_g, ln1_b, w_gate_up, w_down, ple_w_gate, ple_w_proj, ln2_g, ln2_b, loss_target, m_ln_in_g, m_ln_in_b, m_w_in, m_hgrn_lb_logits, m_hgrn_norm_g, m_sgu_ln_g, m_sgu_ln_b, m_sgu_w_s, m_sgu_b_s, m_mla_q_norm_g, m_mla_w_uq, m_mla_kv_norm_g, m_mla_w_ukv, m_w_out, m_ln1_g, m_ln1_b, m_w_gate_up, m_w_down, m_ple_w_gate, m_ple_w_proj, m_ln2_g, m_ln2_b, v_ln_in_g, v_ln_in_b, v_w_in, v_hgrn_lb_logits, v_hgrn_norm_g, v_sgu_ln_g, v_sgu_ln_b, v_sgu_w_s, v_sgu_b_s, v_mla_q_norm_g, v_mla_w_uq, v_mla_kv_norm_g, v_mla_w_ukv, v_w_out, v_ln1_g, v_ln1_b, v_w_gate_up, v_w_down, v_ple_w_gate, v_ple_w_proj, v_ln2_g, v_ln2_b):
    args = dict(locals())
    w = {k: args[k] for k in ORDER}
    m = {k: args["m_" + k] for k in ORDER}
    v = {k: args["v_" + k] for k in ORDER}
    depth = w_in.shape[0]
    assert depth == 2, "the lower-bound kernel is written for two layers"
    alpha = (2 * depth) ** 0.25
    xs, tgt = x[0], loss_target[0]
    d_model = xs.shape[1]

    shards = [_weight_shards(w, li) for li in range(depth)]
    on_hgrn0 = ("mla_w_uq", "mla_w_ukv", "w_out", "ple_w_gate", "ple_w_proj")
    ffn0 = ("w_gate_up", "w_down")
    first1 = ("w_in", "mla_w_uq", "mla_w_ukv", "w_out")
    on_attn1 = ("w_gate_up", "w_down", "ple_w_gate", "ple_w_proj")
    layer1_first = {}

    def after_hgrn0(got):
        return _usable_weights(dict(zip(on_hgrn0, got)), name="l0")

    def after_attn0(got):
        layer1_first.update(_usable_weights(dict(zip(first1, got[len(ffn0):])), name="l1"))
        return _usable_weights(dict(zip(ffn0, got[:len(ffn0)])), name="l0")

    def after_attn1(got):
        return _usable_weights(dict(zip(on_attn1, got)), name="l1")

    tables = _rope_tables(positions[0])
    row1 = lambda a: a.reshape(1, -1)
    l0, l1 = row1(hgrn_lb_logits[0]), row1(hgrn_lb_logits[1])
    lbs = _rowwise(_fn_lower_bounds, [l0, l1], [], [(HG_W, F32), (HG_W, F32)], name="lower_bounds")

    gin, bin_ = row1(ln_in_g), row1(ln_in_b)
    (h, hb), g_in = _rowwise(_twice(_fn_ln), [xs], [gin, bin_], [(d_model, F32), (d_model, BF16)], name="ln_in",
                             job=_gather_job([shards[0]["w_in"]]))
    w_in0 = _usable_weights({"w_in": _gather_forward(g_in, name="gather_l0_w_in_forward")[0]}, name="l0")
    (h, hb), sv0 = _layer_forward(
        0, h, hb, p[0, 0], w_in0, w, lbs, tables, alpha,
        hgrn_job=_gather_job([shards[0][k] for k in on_hgrn0]), after_hgrn=after_hgrn0,
        attn_job=_gather_job([shards[0][k] for k in ffn0] + [shards[1][k] for k in first1]), after_attn=after_attn0)
    (dy, loss_local), sv1 = _layer_forward(
        1, h, hb, p[1, 0], layer1_first, w, lbs, tables, alpha,
        attn_job=_gather_job([shards[1][k] for k in on_attn1]), after_attn=after_attn1, loss_target=tgt)
    saved = [sv0, sv1]
    loss = lax.psum(loss_local[0, 0], ("x", "y", "c"))

    core = lax.axis_index("c").astype(jnp.int32).reshape(1)
    dparts, grads, quads, carried = [dy], [None] * depth, [None] * depth, None
    for li in reversed(range(depth)):
        dparts, grads[li], early_quads, late_sums, late_quads = _layer_backward(
            li, dparts, p[li, 0], saved[li], lbs, tables, alpha, core, carried=carried)
        quads[li] = dict(zip(RS_EARLY, early_quads))
        if carried is not None:
            quads[li + 1].update(zip(RS_LATE, late_quads))
        carried = _quad_job(late_sums)
    (dx, d_gin, d_bin), late_quads = _rowwise_vjp(_fn_ln, [xs], [gin, bin_], [dparts], groups=[[0]], name="ln_in_b",
                                                   job=carried)
    quads[0].update(zip(RS_LATE, late_quads))
    dl0, dl1 = _rowwise_vjp(_fn_lower_bounds, [l0, l1], [], [[grads[0]["lower_bound"]], [grads[1]["lower_bound"]]],
                            groups=[[0], [1]], name="lower_bounds_b")

    prefixes = ("grad_", "delta_", "new_m_", "new_v_")
    uq_pad = ((0, 0), (0, 0), (0, LANES - ATT_D))
    state = {k: ((jnp.pad(w[k], uq_pad), jnp.pad(m[k], uq_pad), jnp.pad(v[k], uq_pad)) if k == "mla_w_uq"
                 else (w[k], m[k], v[k])) for k in BIG}
    out = {}
    for k in BIG:
        res4 = None
        for li in range(depth):
            res4 = _adamw(quads[li][k], *state[k], li, name=f"adamw_l{li}_{k}", into=res4)
        for pre, a in zip(prefixes, res4):
            out[pre + k] = a[:, :, :ATT_D] if k == "mla_w_uq" else a

    small_g = {"ln_in_g": d_gin.reshape(-1), "ln_in_b": d_bin.reshape(-1),
               "hgrn_lb_logits": jnp.stack([dl0.reshape(-1), dl1.reshape(-1)])}
    for k in SMALL[3:]:
        small_g[k] = jnp.stack([grads[li][k].reshape(w[k].shape[1:]) for li in range(depth)])
    (small_parts,) = _all_gather([_pack([small_g[k] for k in SMALL])], name="gather_small_grads", columns=False)
    slabs = _adamw(small_parts, _pack([w[k] for k in SMALL]), _pack([m[k] for k in SMALL]),
                   _pack([v[k] for k in SMALL]), None, name="adamw_small")
    shapes = [w[k].shape for k in SMALL]
    for pre, slab in zip(prefixes, slabs):
        for k, a in zip(SMALL, _unpack(slab, shapes)):
            out[pre + k] = a
    res = [loss, dx[None]]
    for prefix in ("grad_", "delta_", "new_m_", "new_v_"):
        res += [out[prefix + k] for k in ORDER]
    return tuple(res)
```

```python
import functools
import math

import jax
import jax.numpy as jnp
import numpy as np
from jax import lax
from jax.experimental import pallas as pl
from jax.experimental.pallas import tpu as pltpu

F32 = jnp.float32
BF16 = jnp.bfloat16
MESH = pl.DeviceIdType.MESH

LN_EPS = 1e-5
RMS_EPS = 1e-6
ROPE_THETA = 10000.0
ADAM_LR, ADAM_B1, ADAM_B2, ADAM_EPS, ADAM_WD, ADAM_STEP = 0.001, 0.9, 0.999, 1e-08, 0.01, 10

N_DEV = 8
LANES = 128
HG_CHUNK = 16
HG_W = 256
HEAD = 64
SGU_CHUNK = 128
N_ATT_HEADS = 8
ATT_D = 96
VMEM_LIMIT = 56 * 1024 * 1024

HG_TILE = 256
ATT_TQ = 512
ROW_TILE = 256

P_CQ, P_KR, P_CKV, P_COLS = 1536, 1920, 2048, 2304


def _cparams(sem):
    return pltpu.CompilerParams(dimension_semantics=sem, vmem_limit_bytes=VMEM_LIMIT)


_ANY = pl.BlockSpec(memory_space=pl.ANY)


def _call(body, operands, *, name, grid, in_specs, out_specs, out_shape, sem, scratch_shapes=(), job=None):
    if job is None:
        return pl.pallas_call(body, name=name, grid=grid, in_specs=in_specs, out_specs=out_specs, out_shape=out_shape,
                              scratch_shapes=list(scratch_shapes), compiler_params=_cparams(sem))(*operands)
    single = not isinstance(out_shape, (list, tuple))
    shapes = [out_shape] if single else list(out_shape)
    ospecs = [out_specs] if single else list(out_specs)
    ni, no, ns = len(operands), len(shapes), len(scratch_shapes)
    ji, jo = len(job.inputs), len(job.out_shapes)

    def hosted(*refs):
        p = 0
        parts = []
        for cnt in (ni, ji, no, jo, ns):
            parts.append(refs[p:p + cnt])
            p += cnt
        ins, jins, outs, jouts, scr = parts
        jsems = refs[p:]
        ids = [pl.program_id(a) for a in range(len(grid))]
        first = functools.reduce(lambda a, b: a & b, [i == 0 for i in ids])
        last = functools.reduce(lambda a, b: a & b, [i == g - 1 for i, g in zip(ids, grid)])

        @pl.when(first)
        def _():
            job.start(jins, jouts, jsems)

        body(*ins, *outs, *scr)

        @pl.when(last)
        def _():
            job.finish(jins, jouts, jsems)

    res = pl.pallas_call(
        hosted, name=name, grid=grid,
        in_specs=list(in_specs) + [_ANY] * ji, out_specs=ospecs + [_ANY] * jo,
        out_shape=shapes + list(job.out_shapes),
        scratch_shapes=list(scratch_shapes) + [pltpu.SemaphoreType.DMA((c,)) for c in job.sem_counts],
        input_output_aliases=job.aliases(ni, no),
        compiler_params=_cparams(("arbitrary",) * len(grid)),
    )(*operands, *job.inputs)
    own = res[0] if single else res[:no]
    return own, res[no:]


class _Job:
    def __init__(self, inputs, out_shapes, sem_counts, start, finish, in_place=False):
        self.inputs, self.out_shapes, self.sem_counts = list(inputs), list(out_shapes), list(sem_counts)
        self.start, self.finish, self.in_place = start, finish, in_place

    def aliases(self, first_in, first_out):
        return {first_in + i: first_out + i for i in range(len(self.inputs))} if self.in_place else {}


def _copies_job(inputs, out_shapes, n_remote, n_local, make, in_place=False):
    def start(jins, jouts, sems):
        sends, _, local = make(jins, jouts, *sems)
        for cp in local + sends:
            cp.start()

    def finish(jins, jouts, sems):
        sends, recvs, local = make(jins, jouts, *sems)
        for cp in recvs:
            cp.wait_recv()
        for cp in sends:
            cp.wait_send()
        for cp in local:
            cp.wait()

    return _Job(inputs, out_shapes, [n_remote, n_remote, max(n_local, 1)], start, finish, in_place)


def _run_job(job, *, name):
    ji, jo = len(job.inputs), len(job.out_shapes)

    def body(*refs):
        jins, jouts, sems = refs[:ji], refs[ji:ji + jo], refs[ji + jo:]
        job.start(jins, jouts, sems)
        job.finish(jins, jouts, sems)

    return pl.pallas_call(
        body, name=name, out_shape=list(job.out_shapes), in_specs=[_ANY] * ji, out_specs=[_ANY] * jo,
        scratch_shapes=[pltpu.SemaphoreType.DMA((c,)) for c in job.sem_counts],
        input_output_aliases=job.aliases(0, 0),
    )(*job.inputs)


def _tile(n, pref):
    if n % pref == 0:
        return pref
    best = None
    t = LANES
    while t <= min(n, pref):
        if n % t == 0:
            best = t
        t += LANES
    return best if best is not None else n


def _mm(a, b, *, am="mk", bm="kn", om="mn", out_dtype=F32, tm=1024, tn=1024, tk=1024, name, job=None):
    if am == "mk":
        m, k = a.shape
    elif am == "km":
        k, m = a.shape
    elif am == "bmk":
        m, tk = a.shape[1], a.shape[2]
        k = a.shape[0] * tk
    else:
        k, tm = a.shape[1], a.shape[2]
        m = a.shape[0] * tm
    if bm == "kn":
        kb_, n = b.shape
    elif bm == "nk":
        n, kb_ = b.shape
    elif bm == "bkn":
        kb_, tn = b.shape[1], b.shape[2]
        n = b.shape[0] * tn
    else:
        n, tk = b.shape[1], b.shape[2]
        kb_ = b.shape[0] * tk
    assert kb_ == k, (a.shape, b.shape, am, bm)
    tm, tn, tk = _tile(m, tm), _tile(n, tn), _tile(k, tk)
    nk = k // tk
    dims = (((0 if am in ("km", "bkm") else 1,), (1 if bm in ("nk", "bnk") else 0,)), ((), ()))

    a_spec = {"mk": pl.BlockSpec((tm, tk), lambda i, j, kk: (i, kk)),
              "km": pl.BlockSpec((tk, tm), lambda i, j, kk: (kk, i)),
              "bmk": pl.BlockSpec((None, tm, tk), lambda i, j, kk: (kk, i, 0)),
              "bkm": pl.BlockSpec((None, tk, tm), lambda i, j, kk: (i, kk, 0))}[am]
    b_spec = {"kn": pl.BlockSpec((tk, tn), lambda i, j, kk: (kk, j)),
              "nk": pl.BlockSpec((tn, tk), lambda i, j, kk: (j, kk)),
              "bkn": pl.BlockSpec((None, tk, tn), lambda i, j, kk: (j, kk, 0)),
              "bnk": pl.BlockSpec((None, tn, tk), lambda i, j, kk: (kk, j, 0))}[bm]
    if om == "mn":
        o_spec, o_shape = pl.BlockSpec((tm, tn), lambda i, j, kk: (i, j)), (m, n)
    else:
        o_spec, o_shape = pl.BlockSpec((None, tm, tn), lambda i, j, kk: (j, i, 0)), (n // tn, m, tn)

    def body(a_ref, b_ref, o_ref, *acc):
        kk = pl.program_id(2)

        def prod():
            return lax.dot_general(a_ref[...].astype(BF16), b_ref[...].astype(BF16), dims, preferred_element_type=F32)

        if nk == 1:
            o_ref[...] = prod().astype(o_ref.dtype)
            return
        acc_ref, = acc

        @pl.when(kk == 0)
        def _():
            acc_ref[...] = prod()

        if nk > 2:
            @pl.when((kk > 0) & (kk < nk - 1))
            def _():
                acc_ref[...] += prod()

        @pl.when(kk == nk - 1)
        def _():
            o_ref[...] = (acc_ref[...] + prod()).astype(o_ref.dtype)

    return _call(body, (a, b), name=name, grid=(m // tm, n // tn, nk), in_specs=[a_spec, b_spec], out_specs=o_spec,
                 out_shape=jax.ShapeDtypeStruct(o_shape, out_dtype),
                 scratch_shapes=[pltpu.VMEM((tm, tn), F32)] if nk > 1 else [],
                 sem=("parallel", "parallel", "arbitrary"), job=job)


def _mm_kblocks(a, b, *, bm, tm, name, job=None):
    nkb, m, kb = a.shape
    n = b.shape[1]
    tm = _tile(m, tm)

    def body(a_ref, b_ref, o_ref):
        acc = None
        for j in range(nkb):
            if bm == "kn":
                part = jnp.dot(a_ref[j], b_ref[j * kb:(j + 1) * kb, :], preferred_element_type=F32)
            else:
                part = lax.dot_general(a_ref[j], b_ref[j], _NT, preferred_element_type=F32)
            acc = part if acc is None else acc + part
        o_ref[...] = acc

    b_spec = (pl.BlockSpec(b.shape, lambda i: (0, 0)) if bm == "kn" else pl.BlockSpec(b.shape, lambda i: (0, 0, 0)))
    return _call(body, (a, b), name=name, grid=(m // tm,),
                 in_specs=[pl.BlockSpec((nkb, tm, kb), lambda i: (0, i, 0)), b_spec],
                 out_specs=pl.BlockSpec((tm, n), lambda i: (i, 0)), out_shape=jax.ShapeDtypeStruct((m, n), F32),
                 sem=("parallel",), job=job)


def _row_operand(a, tile):
    if isinstance(a, tuple):
        arr, w, j = a
        return arr, pl.BlockSpec((tile, w), lambda i, j=j: (i, j))
    return a, pl.BlockSpec((tile, a.shape[1]), lambda i: (i, 0))


def _const_spec(c):
    nd = c.ndim
    return pl.BlockSpec(c.shape, lambda i, nd=nd: (0,) * nd)


def _rowwise(fn, rows, consts, outs, *, name, accs=(), tile=None, job=None):
    t_rows = (rows[0][0] if isinstance(rows[0], tuple) else rows[0]).shape[0]
    tile = min(tile or ROW_TILE, t_rows)
    arrs, specs = zip(*[_row_operand(a, tile) for a in rows])
    nin, no = len(rows) + len(consts), len(outs)

    def body(*refs):
        res = fn(*[r[...] for r in refs[:nin]])
        for r, v in zip(refs[nin:nin + no], res[:no]):
            r[...] = v.astype(r.dtype)
        if accs:
            a_refs = refs[nin + no:]

            @pl.when(pl.program_id(0) == 0)
            def _():
                for r in a_refs:
                    r[...] = jnp.zeros_like(r)

            for r, v in zip(a_refs, res[no:]):
                r[...] += v

    out_shape = [jax.ShapeDtypeStruct((t_rows, w), dt) for w, dt in outs]
    out_shape += [jax.ShapeDtypeStruct(s, F32) for s in accs]
    out_specs = [pl.BlockSpec((tile, w), lambda i: (i, 0)) for w, _ in outs]
    out_specs += [pl.BlockSpec(s, lambda i, nd=len(s): (0,) * nd) for s in accs]
    return _call(body, (*arrs, *consts), name=name, grid=(t_rows // tile,),
                 in_specs=list(specs) + [_const_spec(c) for c in consts],
                 out_specs=out_specs, out_shape=out_shape, sem=("arbitrary",), job=job)


def _rowwise_vjp(fn, rows, consts, cts, *, name, groups, tile=None, gdtypes=None, job=None):
    t_rows = (rows[0][0] if isinstance(rows[0], tuple) else rows[0]).shape[0]
    tile = min(tile or ROW_TILE, t_rows)
    arrs, specs = zip(*[_row_operand(a, tile) for a in rows])
    flat_cts = [c for group in cts for c in group]
    ct_arrs, ct_specs = zip(*[_row_operand(a, tile) for a in flat_cts])
    nr, nc, nct, ng = len(rows), len(consts), len(flat_cts), len(groups)

    def width(a):
        return a[1] if isinstance(a, tuple) else a.shape[1]

    def body(*refs):
        rv = [r[...].astype(F32) for r in refs[:nr]]
        cv = [r[...] for r in refs[nr:nr + nc]]
        ct_refs = refs[nr + nc:nr + nc + nct]
        ctv, pos = [], 0
        for group in cts:
            s = ct_refs[pos][...].astype(F32)
            for r in ct_refs[pos + 1:pos + len(group)]:
                s = s + r[...].astype(F32)
            ctv.append(s)
            pos += len(group)
        _, pull = jax.vjp(fn, *rv, *cv)
        grads = pull(tuple(ctv))
        g_refs = refs[nr + nc + nct:nr + nc + nct + ng]
        for r, idx in zip(g_refs, groups):
            parts = [grads[i] for i in idx]
            r[...] = (parts[0] if len(parts) == 1 else jnp.concatenate(parts, axis=1)).astype(r.dtype)
        c_refs = refs[nr + nc + nct + ng:]

        @pl.when(pl.program_id(0) == 0)
        def _():
            for r in c_refs:
                r[...] = jnp.zeros_like(r)

        for r, v in zip(c_refs, grads[nr:]):
            r[...] += v

    gw = [sum(width(rows[i]) for i in idx) for idx in groups]
    gdtypes = gdtypes or [F32] * ng
    out_shape = [jax.ShapeDtypeStruct((t_rows, w), dt) for w, dt in zip(gw, gdtypes)]
    out_shape += [jax.ShapeDtypeStruct(c.shape, F32) for c in consts]
    out_specs = [pl.BlockSpec((tile, w), lambda i: (i, 0)) for w in gw]
    out_specs += [_const_spec(c) for c in consts]
    return _call(body, (*arrs, *consts, *ct_arrs), name=name, grid=(t_rows // tile,),
                 in_specs=list(specs) + [_const_spec(c) for c in consts] + list(ct_specs),
                 out_specs=out_specs, out_shape=out_shape, sem=("arbitrary",), job=job)


def _layer_norm(x, g, b):
    mu = jnp.mean(x, axis=-1, keepdims=True)
    xc = x - mu
    var = jnp.mean(xc * xc, axis=-1, keepdims=True)
    return xc * lax.rsqrt(var + LN_EPS) * g + b


def _sigmoid(x):
    return 1.0 / (1.0 + jnp.exp(-x))


def _fn_ln(x, g, b):
    return (_layer_norm(x, g, b),)


def _fn_rms(x, g):
    return (x * lax.rsqrt(jnp.mean(x * x, axis=-1, keepdims=True) + RMS_EPS) * g,)


def _make_post_mix(alpha):
    def fn(h, mix, g, b):
        return (_layer_norm(alpha * h + mix, g, b),)
    return fn


def _make_ple_ln(alpha):
    def fn(h1, ffn, pg, pp, g, b):
        return (_layer_norm(alpha * h1 + ffn + _sigmoid(pg) * pp, g, b),)
    return fn


def _fn_lower_bounds(l0, l1):
    m = jnp.maximum(l0, l1)
    e0, e1 = jnp.exp(l0 - m), jnp.exp(l1 - m)
    s = e0 + e1
    p0, p1 = e0 / s, e1 / s
    return (p0 - p0, (p0 + p1) - p0)


def _split_dot(x, e_bf16):
    hi = x.astype(BF16)
    lo = (x - hi.astype(F32)).astype(BF16)
    return (jnp.dot(hi, e_bf16, preferred_element_type=F32) + jnp.dot(lo, e_bf16, preferred_element_type=F32))


def _hgrn_common(th):
    rm = lax.broadcasted_iota(jnp.int32, (th, HG_W), 0) % HG_CHUNK

    def seg_cumsum(x):
        for s in (1, 2, 4, 8):
            x = x + jnp.where(rm >= s, pltpu.roll(x, s, 0), 0.0)
        return x

    def seg_rcumsum(x):
        for s in (1, 2, 4, 8):
            x = x + jnp.where(rm < HG_CHUNK - s, pltpu.roll(x, th - s, 0), 0.0)
        return x

    ri = lax.broadcasted_iota(jnp.int32, (HG_W, HG_W), 0) // HEAD
    ci = lax.broadcasted_iota(jnp.int32, (HG_W, HG_W), 1) // HEAD
    head_f32 = (ri == ci).astype(F32)
    head_bf16 = head_f32.astype(BF16)

    def headsum(x, pieces=2):
        if pieces == 1:
            return jnp.dot(x.astype(BF16), head_bf16, preferred_element_type=F32)
        return _split_dot(x, head_bf16)

    return rm, seg_cumsum, seg_rcumsum, head_f32, headsum


def _hgrn_gates(qr, fl, lb):
    sg = _sigmoid(fl)
    f = lb + (1.0 - lb) * sg
    sq = _sigmoid(qr)
    return sg, f, jnp.log(f), 1.0 - f, qr * sq, sq


def _shifted(x, d, th):
    return x if d == 0 else pltpu.roll(x, d, 0)


def _unshift(x, d, th):
    return x if d == 0 else pltpu.roll(x, th - d, 0)


def _hgrn_fwd(projp, lb, ng, *, name, job=None):
    t_rows = projp.shape[0]
    th = min(HG_TILE, t_rows)
    nct = th // HG_CHUNK

    def body(q_ref, f_ref, i_ref, g_ref, lb_ref, ng_ref, oa_ref, opre_ref, st_out_ref,
             st_ref, vtm_ref, kv_ref, qe_ref, dec_ref, oint_ref):
        rm, seg_cumsum, seg_rcumsum, head_f32, headsum = _hgrn_common(th)

        @pl.when(pl.program_id(0) == 0)
        def _():
            st_ref[...] = jnp.zeros_like(st_ref)

        qr, fl, v, g = q_ref[...], f_ref[...], i_ref[...], g_ref[...]
        _, f, lf, k, q, _ = _hgrn_gates(qr, fl, lb_ref[...])
        b = seg_cumsum(lf)

        o = jnp.zeros((th, HG_W), F32)
        for d in range(HG_CHUNK):
            kd, bd, vd = _shifted(k, d, th), _shifted(b, d, th), _shifted(v, d, th)
            e = jnp.exp(jnp.where(rm >= d, b - bd, -1e30))
            o = o + headsum(q * kd * e, 1) * vd

        blast = seg_rcumsum(jnp.where(rm == HG_CHUNK - 1, b, 0.0))
        kte = (k * jnp.exp(blast - b)).astype(BF16)
        qe_ref[...] = q * jnp.exp(b)
        dec_ref[...] = jnp.exp(blast)
        vt = v.T
        lane_chunk = lax.broadcasted_iota(jnp.int32, (HG_W, th), 1) // HG_CHUNK
        for c in range(nct):
            vtm_ref[c * HG_W:(c + 1) * HG_W, :] = jnp.where(lane_chunk == c, vt, 0.0).astype(BF16)
        kv_ref[...] = jnp.dot(vtm_ref[...], kte, preferred_element_type=F32)

        s = st_ref[...]
        for c in range(nct):
            rows = slice(c * HG_CHUNK, (c + 1) * HG_CHUNK)
            st_out_ref[c] = s
            oint_ref[rows, :] = lax.dot_general(qe_ref[rows, :].astype(BF16), s.astype(BF16),
                                                (((1,), (1,)), ((), ())), preferred_element_type=F32)
            dec = jnp.max(dec_ref[rows, :], axis=0, keepdims=True)
            s = s * dec + kv_ref[c * HG_W:(c + 1) * HG_W, :] * head_f32
        st_ref[...] = s

        o = o + oint_ref[...]
        opre_ref[...] = o
        r = lax.rsqrt(headsum(o * o) * (1.0 / HEAD) + RMS_EPS)
        oa_ref[...] = (o * r * ng_ref[...] * (g * _sigmoid(g))).astype(oa_ref.dtype)

    col = lambda j: pl.BlockSpec((th, HG_W), lambda i, j=j: (i, j))
    vec = pl.BlockSpec((1, HG_W), lambda i: (0, 0))
    row = pl.BlockSpec((th, HG_W), lambda i: (i, 0))
    n_chunks = t_rows // HG_CHUNK
    return _call(
        body, (projp, projp, projp, projp, lb, ng), name=name, grid=(t_rows // th,),
        in_specs=[col(0), col(1), col(2), col(3), vec, vec],
        out_specs=[row, row, pl.BlockSpec((nct, HG_W, HG_W), lambda i: (i, 0, 0))],
        out_shape=[jax.ShapeDtypeStruct((t_rows, HG_W), BF16), jax.ShapeDtypeStruct((t_rows, HG_W), F32),
                   jax.ShapeDtypeStruct((n_chunks, HG_W, HG_W), F32)],
        scratch_shapes=[pltpu.VMEM((HG_W, HG_W), F32), pltpu.VMEM((nct * HG_W, th), BF16),
                        pltpu.VMEM((nct * HG_W, HG_W), F32), pltpu.VMEM((th, HG_W), F32),
                        pltpu.VMEM((th, HG_W), F32), pltpu.VMEM((th, HG_W), F32)],
        sem=("arbitrary",), job=job)


def _hgrn_bwd(projp, lb, ng, opre, states, dcat, *, name):
    t_rows = projp.shape[0]
    th = min(HG_TILE, t_rows)
    nct = th // HG_CHUNK
    nt = t_rows // th

    def body(q_ref, f_ref, i_ref, g_ref, lb_ref, ng_ref, opre_ref, st_in_ref, do_ref,
             dproj_ref, dng_ref, dlb_ref,
             gst_ref, dotm_ref, qg_ref, v_ref, kte_ref, dop_ref, dec_ref, dkte_ref, dvi_ref, dqe_ref, ddec_ref):
        rm, seg_cumsum, seg_rcumsum, head_f32, headsum = _hgrn_common(th)

        @pl.when(pl.program_id(0) == 0)
        def _():
            gst_ref[...] = jnp.zeros_like(gst_ref)
            dng_ref[...] = jnp.zeros_like(dng_ref)
            dlb_ref[...] = jnp.zeros_like(dlb_ref)

        qr, fl, v, g = q_ref[...], f_ref[...], i_ref[...], g_ref[...]
        lb, ngv = lb_ref[...], ng_ref[...]
        sg, f, lf, k, q, sq = _hgrn_gates(qr, fl, lb)
        b = seg_cumsum(lf)
        blast = seg_rcumsum(jnp.where(rm == HG_CHUNK - 1, b, 0.0))
        eb = jnp.exp(b)
        ekb = jnp.exp(blast - b)
        qe, kte, dec = q * eb, k * ekb, jnp.exp(blast)

        do_out, op = do_ref[...], opre_ref[...]
        sgg = _sigmoid(g)
        sil = g * sgg
        r = lax.rsqrt(headsum(op * op) * (1.0 / HEAD) + RMS_EPS)
        on = op * r
        dng_ref[...] += jnp.sum(do_out * on * sil, axis=0, keepdims=True)
        dg = do_out * on * ngv * (sgg * (1.0 + g * (1.0 - sgg)))
        don = do_out * ngv * sil
        dop = r * (don - on * (headsum(don * on) * (1.0 / HEAD)))

        v_ref[...] = v
        kte_ref[...] = kte
        dop_ref[...] = dop
        dec_ref[...] = dec
        dot_t = dop.T
        lane_chunk = lax.broadcasted_iota(jnp.int32, (HG_W, th), 1) // HG_CHUNK
        for c in range(nct):
            dotm_ref[c * HG_W:(c + 1) * HG_W, :] = jnp.where(lane_chunk == c, dot_t, 0.0).astype(BF16)
        qg_ref[...] = jnp.dot(dotm_ref[...], qe.astype(BF16), preferred_element_type=F32)

        gs = gst_ref[...]
        for c in reversed(range(nct)):
            rows = slice(c * HG_CHUNK, (c + 1) * HG_CHUNK)
            s = st_in_ref[c]
            gm = (gs * head_f32).astype(BF16)
            dkte_ref[rows, :] = jnp.dot(v_ref[rows, :].astype(BF16), gm, preferred_element_type=F32)
            dvi_ref[rows, :] = lax.dot_general(kte_ref[rows, :].astype(BF16), gm, (((1,), (1,)), ((), ())),
                                               preferred_element_type=F32)
            dqe_ref[rows, :] = jnp.dot(dop_ref[rows, :].astype(BF16), s.astype(BF16), preferred_element_type=F32)
            ddec_ref[rows, :] = jnp.broadcast_to(jnp.sum(gs * s, axis=0, keepdims=True), (HG_CHUNK, HG_W))
            dec_c = jnp.max(dec_ref[rows, :], axis=0, keepdims=True)
            gs = gs * dec_c + qg_ref[c * HG_W:(c + 1) * HG_W, :] * head_f32
        gst_ref[...] = gs

        dkte, dqe = dkte_ref[...], dqe_ref[...]
        dq = dqe * eb
        dk = dkte * ekb
        db = dqe * qe - dkte * kte
        dv = dvi_ref[...]
        dblast = dkte * kte + jnp.where(rm == HG_CHUNK - 1, ddec_ref[...] * dec, 0.0)

        for d in range(HG_CHUNK):
            kd, bd, vd = _shifted(k, d, th), _shifted(b, d, th), _shifted(v, d, th)
            e = jnp.exp(jnp.where(rm >= d, b - bd, -1e30))
            p = q * kd * e
            sc = headsum(p, 1)
            dsc = headsum(dop * vd, 1)
            dv = dv + _unshift(sc * dop, d, th)
            dq = dq + dsc * kd * e
            dk = dk + _unshift(dsc * q * e, d, th)
            darg = dsc * p
            db = db + darg - _unshift(darg, d, th)

        db = db + jnp.where(rm == HG_CHUNK - 1, seg_cumsum(dblast), 0.0)
        dlf = seg_rcumsum(db)
        df = dlf / f - dk
        dlb_ref[...] += jnp.sum(df * (1.0 - sg), axis=0, keepdims=True)
        dfl = df * (1.0 - lb) * sg * (1.0 - sg)
        dqr = dq * (sq * (1.0 + qr * (1.0 - sq)))
        dproj_ref[...] = jnp.concatenate([dqr, dfl, dv, dg], axis=1).astype(dproj_ref.dtype)

    rev = lambda i: nt - 1 - i
    col = lambda j: pl.BlockSpec((th, HG_W), lambda i, j=j: (rev(i), j))
    vec = pl.BlockSpec((1, HG_W), lambda i: (0, 0))
    row = pl.BlockSpec((th, HG_W), lambda i: (rev(i), 0))
    tile_f32 = pltpu.VMEM((th, HG_W), F32)
    return pl.pallas_call(
        body, name=name, grid=(nt,),
        in_specs=[col(0), col(1), col(2), col(3), vec, vec, row,
                  pl.BlockSpec((nct, HG_W, HG_W), lambda i: (rev(i), 0, 0)), col(0)],
        out_specs=[pl.BlockSpec((th, 4 * HG_W), lambda i: (rev(i), 0)), vec, vec],
        out_shape=[jax.ShapeDtypeStruct((t_rows, 4 * HG_W), BF16), jax.ShapeDtypeStruct((1, HG_W), F32),
                   jax.ShapeDtypeStruct((1, HG_W), F32)],
        scratch_shapes=[pltpu.VMEM((HG_W, HG_W), F32), pltpu.VMEM((nct * HG_W, th), BF16),
                        pltpu.VMEM((nct * HG_W, HG_W), F32)] + [tile_f32] * 8,
        compiler_params=_cparams(("arbitrary",)),
    )(projp, projp, projp, projp, lb, ng, opre, states, dcat)


_INV_SQRT2 = 1.0 / math.sqrt(2.0)
_INV_SQRT2PI = 1.0 / math.sqrt(2.0 * math.pi)


def _gelu(x):
    return 0.5 * x * (1.0 + lax.erf(x * _INV_SQRT2))


def _gelu_grad(x):
    return 0.5 * (1.0 + lax.erf(x * _INV_SQRT2)) + x * jnp.exp(-0.5 * x * x) * _INV_SQRT2PI


def _sgu_parts(bu, bv, lg, lbias, w_ref, n_groups):
    c = SGU_CHUNK
    tril = (lax.broadcasted_iota(jnp.int32, (c, c), 0) >= lax.broadcasted_iota(jnp.int32, (c, c), 1)).astype(F32)
    gid = lax.broadcasted_iota(jnp.int32, bu.shape, 1) // HEAD
    u = _gelu(bu)
    gv = _gelu(bv)
    mu = jnp.mean(gv, axis=-1, keepdims=True)
    xc = gv - mu
    rstd = lax.rsqrt(jnp.mean(xc * xc, axis=-1, keepdims=True) + LN_EPS)
    xhat = xc * rstd
    vn = xhat * lg + lbias
    ws = [w_ref[gi] * tril for gi in range(n_groups)]
    return tril, gid, u, rstd, xhat, vn, ws


def _sgu_fwd(projp, lg, lbias, w_s, bias_full, *, name, job=None):
    t_rows = projp.shape[0]
    n_groups = w_s.shape[0]
    c = SGU_CHUNK

    def body(u_ref, v_ref, lg_ref, lb_ref, w_ref, bias_ref, o_ref):
        _, gid, u, _, _, vn, ws = _sgu_parts(u_ref[...], v_ref[...], lg_ref[...], lb_ref[...], w_ref, n_groups)
        vnb = vn.astype(BF16)
        z = bias_ref[...]
        for gi in range(n_groups):
            z = z + jnp.where(gid == gi, jnp.dot(ws[gi].astype(BF16), vnb, preferred_element_type=F32), 0.0)
        o_ref[...] = (u * z).astype(o_ref.dtype)

    col = lambda j: pl.BlockSpec((c, HG_W), lambda i, j=j: (i, j))
    return _call(
        body, (projp, projp, lg, lbias, w_s, bias_full), name=name, grid=(t_rows // c,),
        in_specs=[col(4), col(5), _const_spec(lg), _const_spec(lbias), _const_spec(w_s), _const_spec(bias_full)],
        out_specs=pl.BlockSpec((c, HG_W), lambda i: (i, 0)),
        out_shape=jax.ShapeDtypeStruct((t_rows, HG_W), BF16), sem=("arbitrary",), job=job)


def _sgu_bwd(projp, lg, lbias, w_s, bias_full, dcat, *, name):
    t_rows = projp.shape[0]
    n_groups = w_s.shape[0]
    c = SGU_CHUNK
    n = t_rows // c

    def body(u_ref, v_ref, lg_ref, lb_ref, w_ref, bias_ref, do_ref,
             dproj_ref, dlg_ref, dlb_ref, dw_ref, dbs_ref, dbias_acc):
        i = pl.program_id(0)

        @pl.when(i == 0)
        def _():
            dlg_ref[...] = jnp.zeros_like(dlg_ref)
            dlb_ref[...] = jnp.zeros_like(dlb_ref)
            dw_ref[...] = jnp.zeros_like(dw_ref)
            dbias_acc[...] = jnp.zeros_like(dbias_acc)

        bu, bv, lg_v = u_ref[...], v_ref[...], lg_ref[...]
        tril, gid, u, rstd, xhat, vn, ws = _sgu_parts(bu, bv, lg_v, lb_ref[...], w_ref, n_groups)
        vnb = vn.astype(BF16)
        z = bias_ref[...]
        for gi in range(n_groups):
            z = z + jnp.where(gid == gi, jnp.dot(ws[gi].astype(BF16), vnb, preferred_element_type=F32), 0.0)
        do = do_ref[...]
        dbu = do * z * _gelu_grad(bu)
        dz = do * u
        dbias_acc[...] += dz
        dvn = jnp.zeros_like(dz)
        for gi in range(n_groups):
            dzg = jnp.where(gid == gi, dz, 0.0).astype(BF16)
            dw_ref[gi] += lax.dot_general(dzg, vnb, (((1,), (1,)), ((), ())), preferred_element_type=F32) * tril
            dvn = dvn + jnp.dot(ws[gi].T.astype(BF16), dzg, preferred_element_type=F32)
        dlg_ref[...] += jnp.sum(dvn * xhat, axis=0, keepdims=True)
        dlb_ref[...] += jnp.sum(dvn, axis=0, keepdims=True)
        dxh = dvn * lg_v
        dgv = rstd * (dxh - jnp.mean(dxh, axis=-1, keepdims=True)
                      - xhat * jnp.mean(dxh * xhat, axis=-1, keepdims=True))
        dproj_ref[...] = jnp.concatenate([dbu, dgv * _gelu_grad(bv)], axis=1).astype(dproj_ref.dtype)

        @pl.when(i == n - 1)
        def _():
            dbs_ref[...] = jnp.sum(dbias_acc[...].T.reshape(n_groups, HEAD, c), axis=1)

    col = lambda j: pl.BlockSpec((c, HG_W), lambda i, j=j: (i, j))
    return pl.pallas_call(
        body, name=name, grid=(n,),
        in_specs=[col(4), col(5), _const_spec(lg), _const_spec(lbias), _const_spec(w_s), _const_spec(bias_full),
                  col(1)],
        out_specs=[pl.BlockSpec((c, 2 * HG_W), lambda i: (i, 0)), _const_spec(lg), _const_spec(lbias),
                   _const_spec(w_s), pl.BlockSpec((n_groups, c), lambda i: (0, 0))],
        out_shape=[jax.ShapeDtypeStruct((t_rows, 2 * HG_W), BF16), jax.ShapeDtypeStruct(lg.shape, F32),
                   jax.ShapeDtypeStruct(lbias.shape, F32), jax.ShapeDtypeStruct(w_s.shape, F32),
                   jax.ShapeDtypeStruct((n_groups, c), F32)],
        scratch_shapes=[pltpu.VMEM((c, HG_W), F32)],
        compiler_params=_cparams(("arbitrary",)),
    )(projp, projp, lg, lbias, w_s, bias_full, dcat)


def _rope_tables(positions):
    t = positions.shape[0]
    inv_freq = ROPE_THETA ** (-jnp.arange(0, 32, 2, dtype=F32) / 32)
    ang = positions.astype(F32)[:, None] * inv_freq
    cos, sin = jnp.cos(ang), jnp.sin(ang)
    z = lambda w: jnp.zeros((t, w), F32)
    cos_t = jnp.concatenate([jnp.ones((t, 64), F32), cos, cos, z(32)], axis=1)
    sin_up = jnp.concatenate([z(80), sin, z(32)], axis=1)
    sin_dn = jnp.concatenate([z(64), -sin, z(48)], axis=1)
    return cos_t, sin_up, sin_dn


def _rep(x, n):
    return x if n == 1 else jnp.concatenate([x] * n, axis=1)


def _rope(x, cos_t, sin_up, sin_dn):
    w = x.shape[1]
    return x * cos_t + pltpu.roll(x, 16, 1) * sin_up + pltpu.roll(x, w - 16, 1) * sin_dn


def _rope_t(dy, cos_t, sin_up, sin_dn):
    w = dy.shape[1]
    return dy * cos_t + pltpu.roll(dy * sin_up, w - 16, 1) + pltpu.roll(dy * sin_dn, 16, 1)


def _mla_prep(q, kv, projp, tables, *, name):
    nh = N_ATT_HEADS

    def fn(qv, kvv, kr, cos_t, sin_up, sin_dn):
        qr = _rope(qv, _rep(cos_t, nh), _rep(sin_up, nh), _rep(sin_dn, nh))
        krr = _rope(kr, cos_t, sin_up, sin_dn)
        lane = lax.broadcasted_iota(jnp.int32, kvv.shape, 1) % LANES
        return qr, jnp.where(lane < HEAD, kvv, 0.0) + _rep(krr, nh), kvv

    w = q.shape[1]
    return _rowwise(fn, [q, kv, (projp, LANES, P_KR // LANES)] + list(tables), [],
                    [(w, BF16), (w, BF16), (w, BF16)], name=name)


def _mla_prep_bwd(dqr, dkf, tables, *, name):
    nh = N_ATT_HEADS

    def fn(dq, dk, cos_t, sin_up, sin_dn):
        dqp = _rope_t(dq, _rep(cos_t, nh), _rep(sin_up, nh), _rep(sin_dn, nh))
        dkrr = dk[:, 0:LANES]
        for h in range(1, nh):
            dkrr = dkrr + dk[:, LANES * h:LANES * (h + 1)]
        return dqp, _rope_t(dkrr, cos_t, sin_up, sin_dn)

    return _rowwise(fn, [dqr, dkf] + list(tables), [], [(dqr.shape[1], BF16), (LANES, BF16)], name=name)


_LOG2E = 1.0 / math.log(2.0)
_NT = (((1,), (1,)), ((), ()))
_TN = (((0,), (0,)), ((), ()))


def _attn_fwd(qr, kf, kvb, *, name, job=None):
    t_rows = qr.shape[0]
    tq = min(ATT_TQ, t_rows)
    nb = t_rows // tq
    scale = ATT_D ** -0.5

    c2 = scale * _LOG2E

    def body(q_ref, kf_ref, kvb_ref, o_ref, lse_ref):
        qi = pl.program_id(1)
        lane = lax.broadcasted_iota(jnp.int32, (tq, LANES), 1)
        causal_t = (lax.broadcasted_iota(jnp.int32, (tq, tq), 0) <= lax.broadcasted_iota(jnp.int32, (tq, tq), 1))
        heads = [slice(hh * LANES, (hh + 1) * LANES) for hh in range(2)]
        qs = [q_ref[:, cols] for cols in heads]

        def block(first, n_keys, carry, diagonal):
            rows = pl.ds(pl.multiple_of(first * tq, tq), n_keys)
            new = []
            for q, cols, (m_old, l_old, acc_t) in zip(qs, heads, carry):
                s_t = lax.dot_general(kf_ref[rows, cols], q, _NT, preferred_element_type=F32)
                if diagonal:
                    s_t = jnp.where(causal_t, s_t, -1e30)
                m_new = jnp.maximum(m_old, jnp.max(s_t, axis=0, keepdims=True))
                p_t = jnp.exp2((s_t - m_new) * c2)
                a = jnp.exp2((m_old - m_new) * c2)
                pv_t = lax.dot_general(kvb_ref[rows, cols], p_t.astype(BF16), _TN, preferred_element_type=F32)
                new.append((m_new, a * l_old + jnp.sum(p_t, axis=0, keepdims=True), a * acc_t + pv_t))
            return tuple(new)

        init = (jnp.full((1, tq), -1e30, F32), jnp.zeros((1, tq), F32), jnp.zeros((LANES, tq), F32))
        carry = lax.fori_loop(0, qi // 4, lambda g, c: block(4 * g, 4 * tq, c, False), (init, init))
        carry = lax.cond((qi // 2) % 2 == 1, lambda c: block(4 * (qi // 4), 2 * tq, c, False), lambda c: c, carry)
        carry = lax.cond(qi % 2 == 1, lambda c: block(qi - 1, tq, c, False), lambda c: c, carry)
        outs = []
        for hh, (m_fin, l_fin, acc_t) in enumerate(block(qi, tq, carry, True)):
            lse_ref[hh] = m_fin * scale + jnp.log(l_fin)
            outs.append((acc_t / l_fin).T)
        o_ref[...] = jnp.where(lane < HEAD, pltpu.roll(outs[0], HEAD, 1), outs[1])

    pair = pl.BlockSpec((t_rows, 2 * LANES), lambda pr, qi: (0, pr))
    return _call(
        body, (qr, kf, kvb), name=name, grid=(N_ATT_HEADS // 2, nb),
        in_specs=[pl.BlockSpec((tq, 2 * LANES), lambda pr, qi: (qi, pr)), pair, pair],
        out_specs=[pl.BlockSpec((tq, LANES), lambda pr, qi: (qi, pr)),
                   pl.BlockSpec((2, 1, tq), lambda pr, qi: (pr, 0, qi))],
        out_shape=[jax.ShapeDtypeStruct((t_rows, N_ATT_HEADS * HEAD), F32),
                   jax.ShapeDtypeStruct((N_ATT_HEADS, 1, t_rows), F32)],
        sem=("parallel", "arbitrary"), job=job)


def _attn_bwd(qr, kf, kvb, dcat, o, lse, *, name, job=None):
    t_rows = qr.shape[0]
    tq = min(ATT_TQ, t_rows)
    nb = t_rows // tq
    scale = ATT_D ** -0.5
    c2 = scale * _LOG2E
    do_off = 2 * HG_W // LANES

    def body(q_ref, kf_ref, kvb_ref, do_ref, o_ref, lse_ref, dq_ref, dkv_ref, dk_ref):
        ki = pl.program_id(1)

        @pl.when(ki == 0)
        def _():
            dq_ref[...] = jnp.zeros_like(dq_ref)

        lane = lax.broadcasted_iota(jnp.int32, (tq, LANES), 1)
        causal_t = (lax.broadcasted_iota(jnp.int32, (tq, tq), 0) <= lax.broadcasted_iota(jnp.int32, (tq, tq), 1))
        heads = [slice(hh * LANES, (hh + 1) * LANES) for hh in range(2)]
        ks = [kf_ref[:, cols] for cols in heads]
        vs = [kvb_ref[:, cols] for cols in heads]

        def block(qi, n_q, carry, diagonal):
            rows = pl.ds(pl.multiple_of(qi * tq, tq), n_q)
            do_pair, o_pair = do_ref[rows, :], o_ref[rows, :]
            upper = lax.broadcasted_iota(jnp.int32, do_pair.shape, 1) >= HEAD
            new = []
            for hh, (cols, k, v, (dk, dv)) in enumerate(zip(heads, ks, vs, carry)):
                q = q_ref[rows, cols]
                do, ov = (pltpu.roll(do_pair, HEAD, 1), pltpu.roll(o_pair, HEAD, 1)) if hh == 0 else (do_pair, o_pair)
                do = jnp.where(upper, do, 0.0)
                delta = jnp.sum((do * ov).T, axis=0, keepdims=True)
                s_t = lax.dot_general(k, q, _NT, preferred_element_type=F32)
                if diagonal:
                    s_t = jnp.where(causal_t, s_t, -1e30)
                p_t = jnp.exp2(s_t * c2 - lse_ref[hh, :, rows] * _LOG2E)
                dob = do.astype(BF16)
                dv = dv + jnp.dot(p_t.astype(BF16), dob, preferred_element_type=F32)
                dp_t = lax.dot_general(v, dob, _NT, preferred_element_type=F32)
                ds_t = (p_t * (dp_t - delta) * scale).astype(BF16)
                dk = dk + jnp.dot(ds_t, q, preferred_element_type=F32)
                dq_ref[rows, cols] += lax.dot_general(ds_t, k, _TN, preferred_element_type=F32)
                new.append((dk, dv))
            return tuple(new)

        zero = jnp.zeros((tq, LANES), F32)
        carry = block(ki, tq, ((zero, zero), (zero, zero)), True)
        rest = nb - 1 - ki
        carry = lax.fori_loop(0, rest // 2, lambda g, c: block(ki + 1 + 2 * g, 2 * tq, c, False), carry)
        carry = lax.cond(rest % 2 == 1, lambda c: block(nb - 1, tq, c, False), lambda c: c, carry)
        dkv_ref[...] = jnp.concatenate([jnp.where(lane < HEAD, dk, dv) for dk, dv in carry],
                                       axis=1).astype(dkv_ref.dtype)
        dk_ref[...] = jnp.concatenate([dk for dk, _ in carry], axis=1)

    pair_all = pl.BlockSpec((t_rows, 2 * LANES), lambda pr, ki: (0, pr))
    pair_blk = pl.BlockSpec((tq, 2 * LANES), lambda pr, ki: (ki, pr))
    wide = jax.ShapeDtypeStruct((t_rows, N_ATT_HEADS * LANES), F32)
    return _call(
        body, (qr, kf, kvb, dcat, o, lse), name=name, grid=(N_ATT_HEADS // 2, nb),
        in_specs=[pair_all, pair_blk, pair_blk,
                  pl.BlockSpec((t_rows, LANES), lambda pr, ki: (0, do_off + pr)),
                  pl.BlockSpec((t_rows, LANES), lambda pr, ki: (0, pr)),
                  pl.BlockSpec((2, 1, t_rows), lambda pr, ki: (pr, 0, 0))],
        out_specs=[pair_all, pair_blk, pair_blk],
        out_shape=[wide, jax.ShapeDtypeStruct(wide.shape, BF16), wide],
        sem=("parallel", "arbitrary"), job=job)


def _my_pos():
    return lax.axis_index("x"), lax.axis_index("y"), lax.axis_index("c")


def _all_gather(xs, *, name, columns=True):
    return _gather_forward(_run_job(_gather_job(xs, columns), name=name), name=name + "_forward")


def _remote(src, dst, send_sems, recv_sems, k, dev):
    return pltpu.make_async_remote_copy(src_ref=src, dst_ref=dst, send_sem=send_sems.at[k], recv_sem=recv_sems.at[k],
                                        device_id=dev, device_id_type=MESH)


def _block(ref, idx):
    if len(ref.shape) == 2:
        return ref.at[:, pl.ds(pl.multiple_of(idx * LANES, LANES), LANES)]
    return ref.at[idx]


def _gather_job(xs, columns=True):
    n = len(xs)

    def make(x_refs, out_refs, send_sems, recv_sems, local_sems):
        mx, my, mc = _my_pos()
        mine = 4 * mx + 2 * my + mc
        peers = [(mx, my, 1 - mc), (1 - mx, my, mc), (mx, 1 - my, mc), (1 - mx, 1 - my, mc)]
        sends, recvs, local = [], [], []
        for a in range(n):
            local.append(pltpu.make_async_copy(x_refs[a], _block(out_refs[a], mine), local_sems.at[a]))
            for k, dev in enumerate(peers):
                theirs = 4 * dev[0] + 2 * dev[1] + dev[2]
                sends.append(_remote(x_refs[a], _block(out_refs[a], mine), send_sems, recv_sems, 4 * a + k, dev))
                recvs.append(_remote(x_refs[a], _block(out_refs[a], theirs), send_sems, recv_sems, 4 * a + k, dev))
        return sends, recvs, local

    def gathered(x):
        if columns and x.ndim == 2 and x.shape[1] == LANES:
            return jax.ShapeDtypeStruct((x.shape[0], N_DEV * LANES), x.dtype)
        return jax.ShapeDtypeStruct((N_DEV,) + x.shape, x.dtype)

    return _copies_job(xs, [gathered(x) for x in xs], 4 * n, n, make)


def _forward_job(gs):
    n = len(gs)

    def make(in_refs, out_refs, send_sems, recv_sems, local_sems):
        mx, my, mc = _my_pos()
        chips = [(1 - mx, my), (mx, 1 - my), (1 - mx, 1 - my)]
        sends, recvs = [], []
        for a in range(n):
            for j, (cx, cy) in enumerate(chips):
                here = _block(out_refs[a], 4 * cx + 2 * cy + mc)
                there = _block(out_refs[a], 4 * cx + 2 * cy + 1 - mc)
                sends.append(_remote(here, here, send_sems, recv_sems, 3 * a + j, (mx, my, 1 - mc)))
                recvs.append(_remote(here, there, send_sems, recv_sems, 3 * a + j, (mx, my, 1 - mc)))
        return sends, recvs, []

    shapes = [jax.ShapeDtypeStruct(g.shape, g.dtype) for g in gs]
    return _copies_job(gs, shapes, 3 * n, 0, make, in_place=True)


def _gather_forward(gs, *, name):
    return _run_job(_forward_job(gs), name=name)


def _pair_job(xs):
    n = len(xs)

    def make(x_refs, out_refs, send_sems, recv_sems, local_sems):
        mx, my, mc = _my_pos()

        def src(ref, g):
            return _block(ref, 2 * g + 1 - mc) if len(ref.shape) == 2 else ref.at[g, 1 - mc]

        copies = [_remote(src(x_refs[a], g), out_refs[a].at[g], send_sems, recv_sems, 4 * a + g, (mx, my, 1 - mc))
                  for a in range(n) for g in range(4)]
        return copies, copies, []

    shapes = [jax.ShapeDtypeStruct((4, x.shape[0], LANES) if x.ndim == 2 else (4,) + x.shape[2:], x.dtype)
              for x in xs]
    return _copies_job(xs, shapes, 4 * n, 0, make)


def _pair_add(x, r, core, *, name):
    _, a, b = r.shape
    ta = _row_tile(a, 512)

    def body(c_ref, x_ref, r_ref, o_ref):
        o_ref[...] = (x_ref[...] + r_ref[...]).astype(o_ref.dtype)

    blk = pl.BlockSpec((None, ta, b), lambda g, i, c_ref: (g, i, 0))
    own = (pl.BlockSpec((ta, b), lambda g, i, c_ref: (i, 2 * g + c_ref[0])) if x.ndim == 2
           else pl.BlockSpec((None, None, ta, b), lambda g, i, c_ref: (g, c_ref[0], i, 0)))
    return pl.pallas_call(
        body, name=name,
        grid_spec=pltpu.PrefetchScalarGridSpec(
            num_scalar_prefetch=1, grid=(4, a // ta), in_specs=[own, blk], out_specs=blk),
        out_shape=jax.ShapeDtypeStruct((4, a, b), BF16),
        compiler_params=_cparams(("parallel", "parallel")),
    )(core, x, r)


def _quad_job(xs):
    n = len(xs)

    def make(x_refs, out_refs, send_sems, recv_sems, local_sems):
        mx, my, mc = _my_pos()
        mine = 2 * mx + my
        peers = [((1 - mx, my, mc), 2 * (1 - mx) + my), ((mx, 1 - my, mc), 2 * mx + 1 - my),
                 ((1 - mx, 1 - my, mc), 2 * (1 - mx) + 1 - my)]
        sends, recvs, local = [], [], []
        for a in range(n):
            local.append(pltpu.make_async_copy(x_refs[a].at[mine], out_refs[a].at[mine], local_sems.at[a]))
            for k, (dev, g) in enumerate(peers):
                sends.append(_remote(x_refs[a].at[g], out_refs[a].at[mine], send_sems, recv_sems, 3 * a + k, dev))
                recvs.append(_remote(x_refs[a].at[g], out_refs[a].at[g], send_sems, recv_sems, 3 * a + k, dev))
        return sends, recvs, local

    shapes = [jax.ShapeDtypeStruct(x.shape, x.dtype) for x in xs]
    return _copies_job(xs, shapes, 3 * n, n, make)


def _row_tile(r, pref):
    t = min(pref, r)
    while r % t or (t % 8 and t != r):
        t -= 1
    return t


def _adamw(parts, w, m, v, layer, *, name, tile=256, into=None):
    g, a, b = parts.shape
    tile = _row_tile(a, tile)
    c1 = 1.0 / (1.0 - ADAM_B1 ** ADAM_STEP)
    c2 = 1.0 / (1.0 - ADAM_B2 ** ADAM_STEP)
    into = tuple(into or ())

    def body(p_ref, w_ref, m_ref, v_ref, *refs):
        g_ref, d_ref, mo_ref, vo_ref = refs[len(into):]
        grad = p_ref[0].astype(F32)
        for j in range(1, g):
            grad = grad + p_ref[j].astype(F32)
        mn = ADAM_B1 * m_ref[...] + (1.0 - ADAM_B1) * grad
        vn = ADAM_B2 * v_ref[...] + (1.0 - ADAM_B2) * (grad * grad)
        g_ref[...] = grad
        mo_ref[...] = mn
        vo_ref[...] = vn
        d_ref[...] = -ADAM_LR * ((mn * c1) / (jnp.sqrt(vn * c2) + ADAM_EPS) + ADAM_WD * w_ref[...])

    if layer is None:
        src, shape = pl.BlockSpec((tile, b), lambda i: (i, 0)), (a, b)
    else:
        src, shape = pl.BlockSpec((None, tile, b), lambda i: (layer, i, 0)), w.shape
    return pl.pallas_call(
        body, name=name, grid=(a // tile,),
        in_specs=[pl.BlockSpec((g, tile, b), lambda i: (0, i, 0)), src, src, src] + [_ANY] * len(into),
        out_specs=[src] * 4,
        out_shape=[jax.ShapeDtypeStruct(shape, F32)] * 4,
        input_output_aliases={4 + i: i for i in range(len(into))},
        compiler_params=_cparams(("parallel",)),
    )(parts, w, m, v, *into)


W_IN_SHARD = 276


def _w_in_dest(col):
    return jnp.where(col < P_KR, col, jnp.where(col < P_KR + 256, col + (P_CKV - P_KR), col - 2176 + P_KR + HEAD))


PLACE_TILE = 384
PLACE_SHARDS = 3
PICK_TILE = 128
PICK_TILES = 4


def _w_in_tables():
    col = np.arange(N_DEV * W_IN_SHARD)
    dest = np.where(col < P_KR, col, np.where(col < P_KR + 256, col + (P_CKV - P_KR), col - 2176 + P_KR + HEAD))
    shard = col // W_IN_SHARD

    def filled(used, universe, n):
        used = sorted(set(int(u) for u in used))
        assert len(used) <= n, used
        return used + [u for u in universe if u not in used][:n - len(used)]

    place = [filled(shard[dest // PLACE_TILE == c], range(N_DEV), PLACE_SHARDS) for c in range(P_COLS // PLACE_TILE)]
    pick = [filled(dest[shard == j] // PICK_TILE, range(P_COLS // PICK_TILE), PICK_TILES) for j in range(N_DEV)]
    return np.asarray(place, np.int32).reshape(-1), np.asarray(pick, np.int32).reshape(-1)


def _place_w_in(g, *, name):
    _, d, sh = g.shape
    tc, ns = PLACE_TILE, PLACE_SHARDS
    table = jnp.asarray(_w_in_tables()[0])

    def body(tab_ref, g_ref, o_ref, acc_ref):
        ct, s = pl.program_id(0), pl.program_id(1)
        j = tab_ref[ct * ns + s]

        @pl.when(s == 0)
        def _():
            acc_ref[...] = jnp.zeros_like(acc_ref)

        src = j * sh + lax.broadcasted_iota(jnp.int32, (sh, tc), 0)
        dst = ct * tc + lax.broadcasted_iota(jnp.int32, (sh, tc), 1)
        place = (_w_in_dest(src) == dst).astype(BF16)
        acc_ref[...] += jnp.dot(g_ref[...], place, preferred_element_type=F32)

        @pl.when(s == ns - 1)
        def _():
            o_ref[...] = acc_ref[...].astype(o_ref.dtype)

    return pl.pallas_call(
        body, name=name,
        grid_spec=pltpu.PrefetchScalarGridSpec(
            num_scalar_prefetch=1, grid=(P_COLS // tc, ns),
            in_specs=[pl.BlockSpec((None, d, sh), lambda ct, s, tab: (tab[ct * ns + s], 0, 0))],
            out_specs=pl.BlockSpec((d, tc), lambda ct, s, tab: (0, ct)),
            scratch_shapes=[pltpu.VMEM((d, tc), F32)]),
        out_shape=jax.ShapeDtypeStruct((d, P_COLS), BF16),
        compiler_params=_cparams(("parallel", "arbitrary")),
    )(table, g)


def _unplace_w_in(dw, *, name):
    d = dw.shape[0]
    sh, tk, nt = W_IN_SHARD, PICK_TILE, PICK_TILES
    table = jnp.asarray(_w_in_tables()[1])

    def body(tab_ref, dw_ref, o_ref):
        j, kk = pl.program_id(0), pl.program_id(1)
        tile = tab_ref[j * nt + kk]
        src = j * sh + lax.broadcasted_iota(jnp.int32, (tk, sh), 1)
        dst = tile * tk + lax.broadcasted_iota(jnp.int32, (tk, sh), 0)
        pick = (_w_in_dest(src) == dst).astype(BF16)
        part = _split_dot(dw_ref[...], pick)

        @pl.when(kk == 0)
        def _():
            o_ref[...] = part

        @pl.when(kk > 0)
        def _():
            o_ref[...] += part

    return pl.pallas_call(
        body, name=name,
        grid_spec=pltpu.PrefetchScalarGridSpec(
            num_scalar_prefetch=1, grid=(N_DEV, nt),
            in_specs=[pl.BlockSpec((d, tk), lambda j, kk, tab: (0, tab[j * nt + kk]))],
            out_specs=pl.BlockSpec((None, d, sh), lambda j, kk, tab: (j, 0, 0))),
        out_shape=jax.ShapeDtypeStruct((N_DEV, d, sh), F32),
        compiler_params=_cparams(("parallel", "arbitrary")),
    )(table, dw)


def _gate_up_swiglu(h1, wgu, *, name):
    t_rows, k = h1.shape
    w = wgu.shape[2]
    tm = _tile(t_rows, 1024)

    def body(a_ref, wg_ref, wu_ref, gu_ref, act_ref):
        a = a_ref[...].astype(BF16)
        gate = jnp.dot(a, wg_ref[...], preferred_element_type=F32)
        up = jnp.dot(a, wu_ref[...], preferred_element_type=F32)
        gu_ref[0] = gate.astype(gu_ref.dtype)
        gu_ref[1] = up.astype(gu_ref.dtype)
        act_ref[...] = (gate * _sigmoid(gate) * up).astype(act_ref.dtype)

    return pl.pallas_call(
        body, name=name, grid=(t_rows // tm, 4),
        in_specs=[pl.BlockSpec((tm, k), lambda i, j: (i, 0)),
                  pl.BlockSpec((None, k, w), lambda i, j: (j, 0, 0)),
                  pl.BlockSpec((None, k, w), lambda i, j: (j + 4, 0, 0))],
        out_specs=[pl.BlockSpec((2, None, tm, w), lambda i, j: (0, j, i, 0)),
                   pl.BlockSpec((None, tm, w), lambda i, j: (j, i, 0))],
        out_shape=[jax.ShapeDtypeStruct((2, 4, t_rows, w), BF16), jax.ShapeDtypeStruct((4, t_rows, w), BF16)],
        compiler_params=_cparams(("parallel", "arbitrary")),
    )(h1, wgu, wgu)


def _down_dx_swiglu(dffn, wdown, gu, *, name):
    t_rows, k = dffn.shape
    w = gu.shape[3]
    tm = _tile(t_rows, 1024)

    def body(d_ref, w_ref, gu_ref, o_ref):
        dact = lax.dot_general(d_ref[...].astype(BF16), w_ref[...], _NT, preferred_element_type=F32)
        gate, up = gu_ref[0].astype(F32), gu_ref[1].astype(F32)
        sg = _sigmoid(gate)
        silu = gate * sg
        o_ref[0] = (dact * up * (sg + silu - silu * sg)).astype(o_ref.dtype)
        o_ref[1] = (dact * silu).astype(o_ref.dtype)

    blk = pl.BlockSpec((2, None, tm, w), lambda i, j: (0, j, i, 0))
    return pl.pallas_call(
        body, name=name, grid=(t_rows // tm, 4),
        in_specs=[pl.BlockSpec((tm, k), lambda i, j: (i, 0)), pl.BlockSpec((w, k), lambda i, j: (j, 0)), blk],
        out_specs=blk, out_shape=jax.ShapeDtypeStruct(gu.shape, BF16),
        compiler_params=_cparams(("parallel", "arbitrary")),
    )(dffn, wdown, gu)


BIG = ("w_in", "mla_w_uq", "mla_w_ukv", "w_out", "w_gate_up", "w_down", "ple_w_gate", "ple_w_proj")
SMALL = ("ln_in_g", "ln_in_b", "hgrn_lb_logits", "hgrn_norm_g", "sgu_ln_g", "sgu_ln_b", "sgu_w_s", "sgu_b_s",
         "mla_q_norm_g", "mla_kv_norm_g", "ln1_g", "ln1_b", "ln2_g", "ln2_b")
ORDER = ("ln_in_g", "ln_in_b", "w_in", "hgrn_lb_logits", "hgrn_norm_g", "sgu_ln_g", "sgu_ln_b", "sgu_w_s", "sgu_b_s",
         "mla_q_norm_g", "mla_w_uq", "mla_kv_norm_g", "mla_w_ukv", "w_out", "ln1_g", "ln1_b", "w_gate_up", "w_down",
         "ple_w_gate", "ple_w_proj", "ln2_g", "ln2_b")


def _slab(a, align):
    s = a.reshape(-1, LANES)
    pad = -s.shape[0] % align
    return jnp.pad(s, ((0, pad), (0, 0))) if pad else s


def _pack(arrays, align=16, total_align=512):
    s = jnp.concatenate([_slab(a, align) for a in arrays], axis=0)
    pad = -s.shape[0] % total_align
    return jnp.pad(s, ((0, pad), (0, 0))) if pad else s


def _unpack(slab, shapes, align=16):
    out, r0 = [], 0
    for s in shapes:
        nr = math.prod(s) // LANES
        out.append(slab[r0:r0 + nr].reshape(s))
        r0 += nr + (-nr % align)
    return out


def _weight_shards(w, li):
    uq_pad = ((0, 0), (0, LANES - ATT_D))
    shards = {k: w[k][li] for k in BIG}
    shards["mla_w_uq"] = jnp.pad(shards["mla_w_uq"], uq_pad)
    return {k: s.astype(BF16) for k, s in shards.items()}


def _usable_weights(g, *, name):
    out = {}
    for k, a in g.items():
        if k == "w_in":
            out[k] = _place_w_in(a, name=name + "_place_w_in")
        elif k in ("w_out", "w_down", "ple_w_gate"):
            out[k] = a.reshape(a.shape[0] * a.shape[1], a.shape[2])
        else:
            out[k] = a
    return out


BY_COLUMNS = ("mla_w_uq", "mla_w_ukv", "ple_w_proj")


def _as_pairs(k, g):
    if k in BY_COLUMNS:
        return g
    if g.ndim == 2:
        return g.reshape((4, 2, g.shape[0] // N_DEV) + g.shape[1:])
    return g.reshape((4, 2) + g.shape[1:])


def _twice(fn):
    return lambda *a: fn(*a) * 2


def _layer_forward(li, h, hb, p_i, wts, sm, lbs, tables, alpha, hgrn_job=None, after_hgrn=None, attn_job=None,
                   after_attn=None, loss_target=None):
    n = f"l{li}_"
    row1 = lambda a: a.reshape(1, -1)
    projp = _mm(hb, wts["w_in"], name=n + "proj")
    ng = row1(sm["hgrn_norm_g"][li])
    res = _hgrn_fwd(projp, lbs[li], ng, name=n + "hgrn_fwd", job=hgrn_job)
    if hgrn_job is not None:
        res, got = res
    o_a, o_pre, states = res
    lg, lbias = row1(sm["sgu_ln_g"][li]), row1(sm["sgu_ln_b"][li])
    w_s = sm["sgu_w_s"][li]
    bias_full = jnp.repeat(sm["sgu_b_s"][li].T, HEAD, axis=1)
    o_b = _sgu_fwd(projp, lg, lbias, w_s, bias_full, name=n + "sgu_fwd",
                   job=None if hgrn_job is None else _forward_job(got))
    if hgrn_job is not None:
        o_b, got = o_b
        wts = dict(wts, **after_hgrn(got))
    qg, kvg = row1(sm["mla_q_norm_g"][li]), row1(sm["mla_kv_norm_g"][li])
    cq_view, ckv_view = (projp, 384, P_CQ // 384), (projp, 256, P_CKV // 256)
    (cqn,) = _rowwise(_fn_rms, [cq_view], [qg], [(384, BF16)], name=n + "q_norm")
    (ckvn,) = _rowwise(_fn_rms, [ckv_view], [kvg], [(256, BF16)], name=n + "kv_norm")
    q = _mm(cqn, wts["mla_w_uq"], name=n + "uq")
    kv = _mm(ckvn, wts["mla_w_ukv"], name=n + "ukv")
    qr, kf, kvb = _mla_prep(q, kv, projp, tables, name=n + "mla_prep")
    res = _attn_fwd(qr, kf, kvb, name=n + "attn_fwd", job=attn_job)
    if attn_job is not None:
        res, got = res
    o_c, lse = res
    cat = jnp.concatenate([o_a, o_b, o_c.astype(BF16)], axis=1)
    mix = _mm(cat, wts["w_out"], name=n + "out_proj", job=None if attn_job is None else _forward_job(got))
    if attn_job is not None:
        mix, got = mix
        wts = dict(wts, **after_attn(got))
    g1, b1 = row1(sm["ln1_g"][li]), row1(sm["ln1_b"][li])
    d = h.shape[1]
    h1, h1b = _rowwise(_twice(_make_post_mix(alpha)), [h, mix], [g1, b1], [(d, F32), (d, BF16)], name=n + "ln1")
    gu, act = _gate_up_swiglu(h1b, wts["w_gate_up"], name=n + "gate_up")
    ffn = _mm_kblocks(act, wts["w_down"], bm="kn", tm=1024, name=n + "down")
    pg = _mm(h1b, wts["ple_w_gate"], name=n + "ple_gate")
    pp = _mm(p_i, wts["ple_w_proj"], name=n + "ple_proj")
    g2, b2 = row1(sm["ln2_g"][li]), row1(sm["ln2_b"][li])
    if loss_target is None:
        out = _rowwise(_twice(_make_ple_ln(alpha)), [h1, ffn, pg, pp], [g2, b2], [(d, F32), (d, BF16)],
                       name=n + "ln2")
    else:
        def ln_and_loss(h1v, ffnv, pgv, ppv, tv, gv, bv):
            err = _make_ple_ln(alpha)(h1v, ffnv, pgv, ppv, gv, bv)[0] - tv
            return err * (1.0 / d), 0.5 * jnp.sum(jnp.mean(err * err, axis=-1, keepdims=True), axis=0, keepdims=True)

        out = _rowwise(ln_and_loss, [h1, ffn, pg, pp, loss_target], [g2, b2], [(d, F32)], accs=[(1, 1)],
                       name=n + "ln2_loss")
    saved = dict(h=h, hb=hb, h1b=h1b, projp=projp, o_pre=o_pre, states=states, cqn=cqn, ckvn=ckvn, qr=qr, kf=kf, kvb=kvb, o_c=o_c,
                 lse=lse, cat=cat, mix=mix, h1=h1, gu=gu, act=act, ffn=ffn, pg=pg, pp=pp, ng=ng, lg=lg, wts=wts,
                 lbias=lbias, w_s=w_s, bias_full=bias_full, qg=qg, kvg=kvg, g1=g1, b1=b1, g2=g2, b2=b2)
    return tuple(out), saved


RS_EARLY = ("ple_w_proj", "ple_w_gate", "w_down", "w_gate_up", "w_out")
RS_LATE = ("mla_w_uq", "mla_w_ukv", "w_in")


def _layer_backward(li, dh2_parts, p_i, sv, lbs, tables, alpha, core, carried=None):
    n = f"l{li}_b_"
    wts = sv["wts"]
    gr = {}
    dh1_a, dffn, dpg, dpp, gr["ln2_g"], gr["ln2_b"] = _rowwise_vjp(
        _make_ple_ln(alpha), [sv["h1"], sv["ffn"], sv["pg"], sv["pp"]], [sv["g2"], sv["b2"]], [dh2_parts],
        groups=[[0], [1], [2], [3]], gdtypes=[F32, BF16, BF16, BF16], name=n + "ln2")
    big = {}
    big["ple_w_proj"] = _mm(p_i, dpp, am="km", tk=2048, name=n + "ple_proj_dw")
    big["ple_w_gate"] = _mm(sv["h1b"], dpg, am="km", name=n + "ple_gate_dw")
    dh1_b = _mm(dpg, wts["ple_w_gate"], bm="nk", name=n + "ple_gate_dx")
    big["w_down"] = _mm(sv["act"], dffn, am="bkm", tk=4096, name=n + "down_dw")
    dgu = _down_dx_swiglu(dffn, wts["w_down"], sv["gu"], name=n + "down_dx")
    dgu = dgu.reshape((N_DEV,) + dgu.shape[2:])
    big["w_gate_up"], carried_got = _mm(sv["h1b"], dgu, am="km", bm="bkn", om="bmn", tk=4096, name=n + "gate_up_dw",
                                        job=carried), None
    if carried is not None:
        big["w_gate_up"], carried_got = big["w_gate_up"]
    early = [_as_pairs(k, big[k]) for k in RS_EARLY[:-1]]
    dh1_c, theirs = _mm_kblocks(dgu, wts["w_gate_up"], bm="bnk", tm=512, name=n + "gate_up_dx",
                                job=_pair_job(early))
    dh_a, dmix, gr["ln1_g"], gr["ln1_b"] = _rowwise_vjp(
        _make_post_mix(alpha), [sv["h"], sv["mix"]], [sv["g1"], sv["b1"]], [[dh1_a, dh1_b, dh1_c]],
        groups=[[0], [1]], gdtypes=[F32, BF16], name=n + "ln1")
    big["w_out"] = _mm(sv["cat"], dmix, am="km", name=n + "out_proj_dw")
    early.append(_as_pairs("w_out", big["w_out"]))
    dcat, their_w_out = _mm(dmix, wts["w_out"], bm="nk", name=n + "out_proj_dx", job=_pair_job(early[-1:]))
    sums = [_pair_add(x, r, core, name=n + "pair_add_" + k)
            for k, x, r in zip(RS_EARLY, early, list(theirs) + list(their_w_out))]

    (dqr, dkv, dkf), early_quads = _attn_bwd(sv["qr"], sv["kf"], sv["kvb"], dcat, sv["o_c"], sv["lse"],
                                             name=n + "attn", job=_quad_job(sums))
    dqpad, dkr = _mla_prep_bwd(dqr, dkf, tables, name=n + "mla_prep")
    big["mla_w_uq"] = _mm(sv["cqn"], dqpad, am="km", tk=2048, name=n + "uq_dw")
    dcqn = _mm(dqpad, wts["mla_w_uq"], bm="nk", name=n + "uq_dx")
    big["mla_w_ukv"] = _mm(sv["ckvn"], dkv, am="km", tk=2048, name=n + "ukv_dw")
    dckvn = _mm(dkv, wts["mla_w_ukv"], bm="nk", name=n + "ukv_dx")
    projp = sv["projp"]
    dcq, gr["mla_q_norm_g"] = _rowwise_vjp(_fn_rms, [(projp, 384, P_CQ // 384)], [sv["qg"]], [[dcqn]],
                                           groups=[[0]], gdtypes=[BF16], name=n + "q_norm")
    dckv, gr["mla_kv_norm_g"] = _rowwise_vjp(_fn_rms, [(projp, 256, P_CKV // 256)], [sv["kvg"]], [[dckvn]],
                                             groups=[[0]], gdtypes=[BF16], name=n + "kv_norm")
    dsgu, gr["sgu_ln_g"], gr["sgu_ln_b"], gr["sgu_w_s"], gr["sgu_b_s"] = _sgu_bwd(
        projp, sv["lg"], sv["lbias"], sv["w_s"], sv["bias_full"], dcat, name=n + "sgu")
    dhg, gr["hgrn_norm_g"], gr["lower_bound"] = _hgrn_bwd(
        projp, lbs[li], sv["ng"], sv["o_pre"], sv["states"], dcat, name=n + "hgrn")
    dprojp = jnp.concatenate([dhg, dsgu, dcq, dkr, dckv], axis=1)
    big["w_in"] = _unplace_w_in(_mm(sv["hb"], dprojp, am="km", tk=4096, name=n + "proj_dw"),
                                name=n + "proj_dw_shards")
    late = [_as_pairs(k, big[k]) for k in RS_LATE]
    dh_b, theirs = _mm(dprojp, wts["w_in"], bm="nk", tk=P_COLS, name=n + "proj_dx", job=_pair_job(late))
    late_sums = [_pair_add(x, r, core, name=n + "pair_add_" + k) for k, x, r in zip(RS_LATE, late, theirs)]
    return [dh_a, dh_b], gr, early_quads, late_sums, carried_got


def kernel(x, p, positions, ln_in_g, ln_in_b, w_in, hgrn_lb_logits, hgrn_norm_g, sgu_ln_g, sgu_ln_b, sgu_w_s, sgu_b_s, mla_q_norm_g, mla_w_uq, mla_kv_norm_g, mla_w_ukv, w_out, ln1_g, ln1_b, w_gate_up, w_down, ple_w_gate, ple_w_proj, ln2_g, ln2_b, loss_target, m_ln_in_g, m_ln_in_b, m_w_in, m_hgrn_lb_logits, m_hgrn_norm_g, m_sgu_ln_g, m_sgu_ln_b, m_sgu_w_s, m_sgu_b_s, m_mla_q_norm_g, m_mla_w_uq, m_mla_kv_norm_g, m_mla_w_ukv, m_w_out, m_ln1_g, m_ln1_b, m_w_gate_up, m_w_down, m_ple_w_gate, m_ple_w_proj, m_ln2_g, m_ln2_b, v_ln_in_g, v_ln_in_b, v_w_in, v_hgrn_lb_logits, v_hgrn_norm_g, v_sgu_ln_g, v_sgu_ln_b, v_sgu_w_s, v_sgu_b_s, v_mla_q_norm_g, v_mla_w_uq, v_mla_kv_norm_g, v_mla_w_ukv, v_w_out, v_ln1_g, v_ln1_b, v_w_gate_up, v_w_down, v_ple_w_gate, v_ple_w_proj, v_ln2_g, v_ln2_b):
    args = dict(locals())
    w = {k: args[k] for k in ORDER}
    m = {k: args["m_" + k] for k in ORDER}
    v = {k: args["v_" + k] for k in ORDER}
    depth = w_in.shape[0]
    assert depth == 2, "the lower-bound kernel is written for two layers"
    alpha = (2 * depth) ** 0.25
    xs, tgt = x[0], loss_target[0]
    d_model = xs.shape[1]

    shards = [_weight_shards(w, li) for li in range(depth)]
    on_hgrn0 = ("mla_w_uq", "mla_w_ukv", "w_out", "ple_w_gate", "ple_w_proj")
    ffn0 = ("w_gate_up", "w_down")
    first1 = ("w_in", "mla_w_uq", "mla_w_ukv", "w_out")
    on_attn1 = ("w_gate_up", "w_down", "ple_w_gate", "ple_w_proj")
    layer1_first = {}

    def after_hgrn0(got):
        return _usable_weights(dict(zip(on_hgrn0, got)), name="l0")

    def after_attn0(got):
        layer1_first.update(_usable_weights(dict(zip(first1, got[len(ffn0):])), name="l1"))
        return _usable_weights(dict(zip(ffn0, got[:len(ffn0)])), name="l0")

    def after_attn1(got):
        return _usable_weights(dict(zip(on_attn1, got)), name="l1")

    tables = _rope_tables(positions[0])
    row1 = lambda a: a.reshape(1, -1)
    l0, l1 = row1(hgrn_lb_logits[0]), row1(hgrn_lb_logits[1])
    lbs = _rowwise(_fn_lower_bounds, [l0, l1], [], [(HG_W, F32), (HG_W, F32)], name="lower_bounds")

    gin, bin_ = row1(ln_in_g), row1(ln_in_b)
    (h, hb), g_in = _rowwise(_twice(_fn_ln), [xs], [gin, bin_], [(d_model, F32), (d_model, BF16)], name="ln_in",
                             job=_gather_job([shards[0]["w_in"]]))
    w_in0 = _usable_weights({"w_in": _gather_forward(g_in, name="gather_l0_w_in_forward")[0]}, name="l0")
    (h, hb), sv0 = _layer_forward(
        0, h, hb, p[0, 0], w_in0, w, lbs, tables, alpha,
        hgrn_job=_gather_job([shards[0][k] for k in on_hgrn0]), after_hgrn=after_hgrn0,
        attn_job=_gather_job([shards[0][k] for k in ffn0] + [shards[1][k] for k in first1]), after_attn=after_attn0)
    (dy, loss_local), sv1 = _layer_forward(
        1, h, hb, p[1, 0], layer1_first, w, lbs, tables, alpha,
        attn_job=_gather_job([shards[1][k] for k in on_attn1]), after_attn=after_attn1, loss_target=tgt)
    saved = [sv0, sv1]
    loss = lax.psum(loss_local[0, 0], ("x", "y", "c"))

    core = lax.axis_index("c").astype(jnp.int32).reshape(1)
    dparts, grads, quads, carried = [dy], [None] * depth, [None] * depth, None
    for li in reversed(range(depth)):
        dparts, grads[li], early_quads, late_sums, late_quads = _layer_backward(
            li, dparts, p[li, 0], saved[li], lbs, tables, alpha, core, carried=carried)
        quads[li] = dict(zip(RS_EARLY, early_quads))
        if carried is not None:
            quads[li + 1].update(zip(RS_LATE, late_quads))
        carried = _quad_job(late_sums)
    (dx, d_gin, d_bin), late_quads = _rowwise_vjp(_fn_ln, [xs], [gin, bin_], [dparts], groups=[[0]], name="ln_in_b",
                                                   job=carried)
    quads[0].update(zip(RS_LATE, late_quads))
    dl0, dl1 = _rowwise_vjp(_fn_lower_bounds, [l0, l1], [], [[grads[0]["lower_bound"]], [grads[1]["lower_bound"]]],
                            groups=[[0], [1]], name="lower_bounds_b")

    prefixes = ("grad_", "delta_", "new_m_", "new_v_")
    uq_pad = ((0, 0), (0, 0), (0, LANES - ATT_D))
    state = {k: ((jnp.pad(w[k], uq_pad), jnp.pad(m[k], uq_pad), jnp.pad(v[k], uq_pad)) if k == "mla_w_uq"
                 else (w[k], m[k], v[k])) for k in BIG}
    out = {}
    for k in BIG:
        res4 = None
        for li in range(depth):
            res4 = _adamw(quads[li][k], *state[k], li, name=f"adamw_l{li}_{k}", into=res4)
        for pre, a in zip(prefixes, res4):
            out[pre + k] = a[:, :, :ATT_D] if k == "mla_w_uq" else a

    small_g = {"ln_in_g": d_gin.reshape(-1), "ln_in_b": d_bin.reshape(-1),
               "hgrn_lb_logits": jnp.stack([dl0.reshape(-1), dl1.reshape(-1)])}
    for k in SMALL[3:]:
        small_g[k] = jnp.stack([grads[li][k].reshape(w[k].shape[1:]) for li in range(depth)])
    (small_parts,) = _all_gather([_pack([small_g[k] for k in SMALL])], name="gather_small_grads", columns=False)
    slabs = _adamw(small_parts, _pack([w[k] for k in SMALL]), _pack([m[k] for k in SMALL]),
                   _pack([v[k] for k in SMALL]), None, name="adamw_small")
    shapes = [w[k].shape for k in SMALL]
    for pre, slab in zip(prefixes, slabs):
        for k, a in zip(SMALL, _unpack(slab, shapes)):
            out[pre + k] = a
    res = [loss, dx[None]]
    for prefix in ("grad_", "delta_", "new_m_", "new_v_"):
        res += [out[prefix + k] for k in ORDER]
    return tuple(res)
```

```python
import functools
import math

import jax
import jax.numpy as jnp
import numpy as np
from jax import lax
from jax.experimental import pallas as pl
from jax.experimental.pallas import tpu as pltpu

F32 = jnp.float32
BF16 = jnp.bfloat16
MESH = pl.DeviceIdType.MESH

LN_EPS = 1e-5
RMS_EPS = 1e-6
ROPE_THETA = 10000.0
ADAM_LR, ADAM_B1, ADAM_B2, ADAM_EPS, ADAM_WD, ADAM_STEP = 0.001, 0.9, 0.999, 1e-08, 0.01, 10

N_DEV = 8
LANES = 128
HG_CHUNK = 16
HG_W = 256
HEAD = 64
SGU_CHUNK = 128
N_ATT_HEADS = 8
ATT_D = 96
VMEM_LIMIT = 56 * 1024 * 1024

HG_TILE = 256
ATT_TQ = 512
ROW_TILE = 256

P_CQ, P_KR, P_CKV, P_COLS = 1536, 1920, 2048, 2304


def _cparams(sem):
    return pltpu.CompilerParams(dimension_semantics=sem, vmem_limit_bytes=VMEM_LIMIT)


_ANY = pl.BlockSpec(memory_space=pl.ANY)


def _call(body, operands, *, name, grid, in_specs, out_specs, out_shape, sem, scratch_shapes=(), job=None):
    if job is None:
        return pl.pallas_call(body, name=name, grid=grid, in_specs=in_specs, out_specs=out_specs, out_shape=out_shape,
                              scratch_shapes=list(scratch_shapes), compiler_params=_cparams(sem))(*operands)
    single = not isinstance(out_shape, (list, tuple))
    shapes = [out_shape] if single else list(out_shape)
    ospecs = [out_specs] if single else list(out_specs)
    ni, no, ns = len(operands), len(shapes), len(scratch_shapes)
    ji, jo = len(job.inputs), len(job.out_shapes)

    def hosted(*refs):
        p = 0
        parts = []
        for cnt in (ni, ji, no, jo, ns):
            parts.append(refs[p:p + cnt])
            p += cnt
        ins, jins, outs, jouts, scr = parts
        jsems = refs[p:]
        ids = [pl.program_id(a) for a in range(len(grid))]
        first = functools.reduce(lambda a, b: a & b, [i == 0 for i in ids])
        last = functools.reduce(lambda a, b: a & b, [i == g - 1 for i, g in zip(ids, grid)])

        @pl.when(first)
        def _():
            job.start(jins, jouts, jsems)

        body(*ins, *outs, *scr)

        @pl.when(last)
        def _():
            job.finish(jins, jouts, jsems)

    res = pl.pallas_call(
        hosted, name=name, grid=grid,
        in_specs=list(in_specs) + [_ANY] * ji, out_specs=ospecs + [_ANY] * jo,
        out_shape=shapes + list(job.out_shapes),
        scratch_shapes=list(scratch_shapes) + [pltpu.SemaphoreType.DMA((c,)) for c in job.sem_counts],
        input_output_aliases=job.aliases(ni, no),
        compiler_params=_cparams(("arbitrary",) * len(grid)),
    )(*operands, *job.inputs)
    own = res[0] if single else res[:no]
    return own, res[no:]


class _Job:
    def __init__(self, inputs, out_shapes, sem_counts, start, finish, in_place=False):
        self.inputs, self.out_shapes, self.sem_counts = list(inputs), list(out_shapes), list(sem_counts)
        self.start, self.finish, self.in_place = start, finish, in_place

    def aliases(self, first_in, first_out):
        return {first_in + i: first_out + i for i in range(len(self.inputs))} if self.in_place else {}


def _copies_job(inputs, out_shapes, n_remote, n_local, make, in_place=False):
    def start(jins, jouts, sems):
        sends, _, local = make(jins, jouts, *sems)
        for cp in local + sends:
            cp.start()

    def finish(jins, jouts, sems):
        sends, recvs, local = make(jins, jouts, *sems)
        for cp in recvs:
            cp.wait_recv()
        for cp in sends:
            cp.wait_send()
        for cp in local:
            cp.wait()

    return _Job(inputs, out_shapes, [n_remote, n_remote, max(n_local, 1)], start, finish, in_place)


def _run_job(job, *, name):
    ji, jo = len(job.inputs), len(job.out_shapes)

    def body(*refs):
        jins, jouts, sems = refs[:ji], refs[ji:ji + jo], refs[ji + jo:]
        job.start(jins, jouts, sems)
        job.finish(jins, jouts, sems)

    return pl.pallas_call(
        body, name=name, out_shape=list(job.out_shapes), in_specs=[_ANY] * ji, out_specs=[_ANY] * jo,
        scratch_shapes=[pltpu.SemaphoreType.DMA((c,)) for c in job.sem_counts],
        input_output_aliases=job.aliases(0, 0),
    )(*job.inputs)


def _tile(n, pref):
    if n % pref == 0:
        return pref
    best = None
    t = LANES
    while t <= min(n, pref):
        if n % t == 0:
            best = t
        t += LANES
    return best if best is not None else n


def _mm(a, b, *, am="mk", bm="kn", om="mn", out_dtype=F32, tm=1024, tn=1024, tk=1024, name, job=None):
    if am == "mk":
        m, k = a.shape
    elif am == "km":
        k, m = a.shape
    elif am == "bmk":
        m, tk = a.shape[1], a.shape[2]
        k = a.shape[0] * tk
    else:
        k, tm = a.shape[1], a.shape[2]
        m = a.shape[0] * tm
    if bm == "kn":
        kb_, n = b.shape
    elif bm == "nk":
        n, kb_ = b.shape
    elif bm == "bkn":
        kb_, tn = b.shape[1], b.shape[2]
        n = b.shape[0] * tn
    else:
        n, tk = b.shape[1], b.shape[2]
        kb_ = b.shape[0] * tk
    assert kb_ == k, (a.shape, b.shape, am, bm)
    tm, tn, tk = _tile(m, tm), _tile(n, tn), _tile(k, tk)
    nk = k // tk
    dims = (((0 if am in ("km", "bkm") else 1,), (1 if bm in ("nk", "bnk") else 0,)), ((), ()))

    a_spec = {"mk": pl.BlockSpec((tm, tk), lambda i, j, kk: (i, kk)),
              "km": pl.BlockSpec((tk, tm), lambda i, j, kk: (kk, i)),
              "bmk": pl.BlockSpec((None, tm, tk), lambda i, j, kk: (kk, i, 0)),
              "bkm": pl.BlockSpec((None, tk, tm), lambda i, j, kk: (i, kk, 0))}[am]
    b_spec = {"kn": pl.BlockSpec((tk, tn), lambda i, j, kk: (kk, j)),
              "nk": pl.BlockSpec((tn, tk), lambda i, j, kk: (j, kk)),
              "bkn": pl.BlockSpec((None, tk, tn), lambda i, j, kk: (j, kk, 0)),
              "bnk": pl.BlockSpec((None, tn, tk), lambda i, j, kk: (kk, j, 0))}[bm]
    if om == "mn":
        o_spec, o_shape = pl.BlockSpec((tm, tn), lambda i, j, kk: (i, j)), (m, n)
    else:
        o_spec, o_shape = pl.BlockSpec((None, tm, tn), lambda i, j, kk: (j, i, 0)), (n // tn, m, tn)

    def body(a_ref, b_ref, o_ref, *acc):
        kk = pl.program_id(2)

        def prod():
            return lax.dot_general(a_ref[...].astype(BF16), b_ref[...].astype(BF16), dims, preferred_element_type=F32)

        if nk == 1:
            o_ref[...] = prod().astype(o_ref.dtype)
            return
        acc_ref, = acc

        @pl.when(kk == 0)
        def _():
            acc_ref[...] = prod()

        if nk > 2:
            @pl.when((kk > 0) & (kk < nk - 1))
            def _():
                acc_ref[...] += prod()

        @pl.when(kk == nk - 1)
        def _():
            o_ref[...] = (acc_ref[...] + prod()).astype(o_ref.dtype)

    return _call(body, (a, b), name=name, grid=(m // tm, n // tn, nk), in_specs=[a_spec, b_spec], out_specs=o_spec,
                 out_shape=jax.ShapeDtypeStruct(o_shape, out_dtype),
                 scratch_shapes=[pltpu.VMEM((tm, tn), F32)] if nk > 1 else [],
                 sem=("parallel", "parallel", "arbitrary"), job=job)


def _mm_kblocks(a, b, *, bm, tm, name, job=None):
    nkb, m, kb = a.shape
    n = b.shape[1]
    tm = _tile(m, tm)

    def body(a_ref, b_ref, o_ref):
        acc = None
        for j in range(nkb):
            if bm == "kn":
                part = jnp.dot(a_ref[j], b_ref[j * kb:(j + 1) * kb, :], preferred_element_type=F32)
            else:
                part = lax.dot_general(a_ref[j], b_ref[j], _NT, preferred_element_type=F32)
            acc = part if acc is None else acc + part
        o_ref[...] = acc

    b_spec = (pl.BlockSpec(b.shape, lambda i: (0, 0)) if bm == "kn" else pl.BlockSpec(b.shape, lambda i: (0, 0, 0)))
    return _call(body, (a, b), name=name, grid=(m // tm,),
                 in_specs=[pl.BlockSpec((nkb, tm, kb), lambda i: (0, i, 0)), b_spec],
                 out_specs=pl.BlockSpec((tm, n), lambda i: (i, 0)), out_shape=jax.ShapeDtypeStruct((m, n), F32),
                 sem=("parallel",), job=job)


def _row_operand(a, tile):
    if isinstance(a, tuple):
        arr, w, j = a
        return arr, pl.BlockSpec((tile, w), lambda i, j=j: (i, j))
    return a, pl.BlockSpec((tile, a.shape[1]), lambda i: (i, 0))


def _const_spec(c):
    nd = c.ndim
    return pl.BlockSpec(c.shape, lambda i, nd=nd: (0,) * nd)


def _rowwise(fn, rows, consts, outs, *, name, accs=(), tile=None, job=None):
    t_rows = (rows[0][0] if isinstance(rows[0], tuple) else rows[0]).shape[0]
    tile = min(tile or ROW_TILE, t_rows)
    arrs, specs = zip(*[_row_operand(a, tile) for a in rows])
    nin, no = len(rows) + len(consts), len(outs)

    def body(*refs):
        res = fn(*[r[...] for r in refs[:nin]])
        for r, v in zip(refs[nin:nin + no], res[:no]):
            r[...] = v.astype(r.dtype)
        if accs:
            a_refs = refs[nin + no:]

            @pl.when(pl.program_id(0) == 0)
            def _():
                for r in a_refs:
                    r[...] = jnp.zeros_like(r)

            for r, v in zip(a_refs, res[no:]):
                r[...] += v

    out_shape = [jax.ShapeDtypeStruct((t_rows, w), dt) for w, dt in outs]
    out_shape += [jax.ShapeDtypeStruct(s, F32) for s in accs]
    out_specs = [pl.BlockSpec((tile, w), lambda i: (i, 0)) for w, _ in outs]
    out_specs += [pl.BlockSpec(s, lambda i, nd=len(s): (0,) * nd) for s in accs]
    return _call(body, (*arrs, *consts), name=name, grid=(t_rows // tile,),
                 in_specs=list(specs) + [_const_spec(c) for c in consts],
                 out_specs=out_specs, out_shape=out_shape, sem=("arbitrary",), job=job)


def _rowwise_vjp(fn, rows, consts, cts, *, name, groups, tile=None, gdtypes=None, job=None):
    t_rows = (rows[0][0] if isinstance(rows[0], tuple) else rows[0]).shape[0]
    tile = min(tile or ROW_TILE, t_rows)
    arrs, specs = zip(*[_row_operand(a, tile) for a in rows])
    flat_cts = [c for group in cts for c in group]
    ct_arrs, ct_specs = zip(*[_row_operand(a, tile) for a in flat_cts])
    nr, nc, nct, ng = len(rows), len(consts), len(flat_cts), len(groups)

    def width(a):
        return a[1] if isinstance(a, tuple) else a.shape[1]

    def body(*refs):
        rv = [r[...].astype(F32) for r in refs[:nr]]
        cv = [r[...] for r in refs[nr:nr + nc]]
        ct_refs = refs[nr + nc:nr + nc + nct]
        ctv, pos = [], 0
        for group in cts:
            s = ct_refs[pos][...].astype(F32)
            for r in ct_refs[pos + 1:pos + len(group)]:
                s = s + r[...].astype(F32)
            ctv.append(s)
            pos += len(group)
        _, pull = jax.vjp(fn, *rv, *cv)
        grads = pull(tuple(ctv))
        g_refs = refs[nr + nc + nct:nr + nc + nct + ng]
        for r, idx in zip(g_refs, groups):
            parts = [grads[i] for i in idx]
            r[...] = (parts[0] if len(parts) == 1 else jnp.concatenate(parts, axis=1)).astype(r.dtype)
        c_refs = refs[nr + nc + nct + ng:]

        @pl.when(pl.program_id(0) == 0)
        def _():
            for r in c_refs:
                r[...] = jnp.zeros_like(r)

        for r, v in zip(c_refs, grads[nr:]):
            r[...] += v

    gw = [sum(width(rows[i]) for i in idx) for idx in groups]
    gdtypes = gdtypes or [F32] * ng
    out_shape = [jax.ShapeDtypeStruct((t_rows, w), dt) for w, dt in zip(gw, gdtypes)]
    out_shape += [jax.ShapeDtypeStruct(c.shape, F32) for c in consts]
    out_specs = [pl.BlockSpec((tile, w), lambda i: (i, 0)) for w in gw]
    out_specs += [_const_spec(c) for c in consts]
    return _call(body, (*arrs, *consts, *ct_arrs), name=name, grid=(t_rows // tile,),
                 in_specs=list(specs) + [_const_spec(c) for c in consts] + list(ct_specs),
                 out_specs=out_specs, out_shape=out_shape, sem=("arbitrary",), job=job)


def _layer_norm(x, g, b):
    mu = jnp.mean(x, axis=-1, keepdims=True)
    xc = x - mu
    var = jnp.mean(xc * xc, axis=-1, keepdims=True)
    return xc * lax.rsqrt(var + LN_EPS) * g + b


def _sigmoid(x):
    return 1.0 / (1.0 + jnp.exp(-x))


def _fn_ln(x, g, b):
    return (_layer_norm(x, g, b),)


def _fn_rms(x, g):
    return (x * lax.rsqrt(jnp.mean(x * x, axis=-1, keepdims=True) + RMS_EPS) * g,)


def _make_post_mix(alpha):
    def fn(h, mix, g, b):
        return (_layer_norm(alpha * h + mix, g, b),)
    return fn


def _make_ple_ln(alpha):
    def fn(h1, ffn, pg, pp, g, b):
        return (_layer_norm(alpha * h1 + ffn + _sigmoid(pg) * pp, g, b),)
    return fn


def _fn_lower_bounds(l0, l1):
    m = jnp.maximum(l0, l1)
    e0, e1 = jnp.exp(l0 - m), jnp.exp(l1 - m)
    s = e0 + e1
    p0, p1 = e0 / s, e1 / s
    return (p0 - p0, (p0 + p1) - p0)


def _split_dot(x, e_bf16):
    hi = x.astype(BF16)
    lo = (x - hi.astype(F32)).astype(BF16)
    return (jnp.dot(hi, e_bf16, preferred_element_type=F32) + jnp.dot(lo, e_bf16, preferred_element_type=F32))


def _hgrn_common(th):
    rm = lax.broadcasted_iota(jnp.int32, (th, HG_W), 0) % HG_CHUNK

    def seg_cumsum(x):
        for s in (1, 2, 4, 8):
            x = x + jnp.where(rm >= s, pltpu.roll(x, s, 0), 0.0)
        return x

    def seg_rcumsum(x):
        for s in (1, 2, 4, 8):
            x = x + jnp.where(rm < HG_CHUNK - s, pltpu.roll(x, th - s, 0), 0.0)
        return x

    ri = lax.broadcasted_iota(jnp.int32, (HG_W, HG_W), 0) // HEAD
    ci = lax.broadcasted_iota(jnp.int32, (HG_W, HG_W), 1) // HEAD
    head_f32 = (ri == ci).astype(F32)
    head_bf16 = head_f32.astype(BF16)

    def headsum(x, pieces=2):
        if pieces == 1:
            return jnp.dot(x.astype(BF16), head_bf16, preferred_element_type=F32)
        return _split_dot(x, head_bf16)

    return rm, seg_cumsum, seg_rcumsum, head_f32, headsum


def _hgrn_gates(qr, fl, lb):
    sg = _sigmoid(fl)
    f = lb + (1.0 - lb) * sg
    sq = _sigmoid(qr)
    return sg, f, jnp.log(f), 1.0 - f, qr * sq, sq


def _shifted(x, d, th):
    return x if d == 0 else pltpu.roll(x, d, 0)


def _unshift(x, d, th):
    return x if d == 0 else pltpu.roll(x, th - d, 0)


def _hgrn_fwd(projp, lb, ng, *, name, job=None):
    t_rows = projp.shape[0]
    th = min(HG_TILE, t_rows)
    nct = th // HG_CHUNK

    def body(q_ref, f_ref, i_ref, g_ref, lb_ref, ng_ref, oa_ref, opre_ref, st_out_ref,
             st_ref, vtm_ref, kv_ref, qe_ref, dec_ref, oint_ref):
        rm, seg_cumsum, seg_rcumsum, head_f32, headsum = _hgrn_common(th)

        @pl.when(pl.program_id(0) == 0)
        def _():
            st_ref[...] = jnp.zeros_like(st_ref)

        qr, fl, v, g = q_ref[...], f_ref[...], i_ref[...], g_ref[...]
        _, f, lf, k, q, _ = _hgrn_gates(qr, fl, lb_ref[...])
        b = seg_cumsum(lf)

        o = jnp.zeros((th, HG_W), F32)
        for d in range(HG_CHUNK):
            kd, bd, vd = _shifted(k, d, th), _shifted(b, d, th), _shifted(v, d, th)
            e = jnp.exp(jnp.where(rm >= d, b - bd, -1e30))
            o = o + headsum(q * kd * e, 1) * vd

        blast = seg_rcumsum(jnp.where(rm == HG_CHUNK - 1, b, 0.0))
        kte = (k * jnp.exp(blast - b)).astype(BF16)
        qe_ref[...] = q * jnp.exp(b)
        dec_ref[...] = jnp.exp(blast)
        vt = v.T
        lane_chunk = lax.broadcasted_iota(jnp.int32, (HG_W, th), 1) // HG_CHUNK
        for c in range(nct):
            vtm_ref[c * HG_W:(c + 1) * HG_W, :] = jnp.where(lane_chunk == c, vt, 0.0).astype(BF16)
        kv_ref[...] = jnp.dot(vtm_ref[...], kte, preferred_element_type=F32)

        s = st_ref[...]
        for c in range(nct):
            rows = slice(c * HG_CHUNK, (c + 1) * HG_CHUNK)
            st_out_ref[c] = s
            oint_ref[rows, :] = lax.dot_general(qe_ref[rows, :].astype(BF16), s.astype(BF16),
                                                (((1,), (1,)), ((), ())), preferred_element_type=F32)
            dec = jnp.max(dec_ref[rows, :], axis=0, keepdims=True)
            s = s * dec + kv_ref[c * HG_W:(c + 1) * HG_W, :] * head_f32
        st_ref[...] = s

        o = o + oint_ref[...]
        opre_ref[...] = o
        r = lax.rsqrt(headsum(o * o) * (1.0 / HEAD) + RMS_EPS)
        oa_ref[...] = (o * r * ng_ref[...] * (g * _sigmoid(g))).astype(oa_ref.dtype)

    col = lambda j: pl.BlockSpec((th, HG_W), lambda i, j=j: (i, j))
    vec = pl.BlockSpec((1, HG_W), lambda i: (0, 0))
    row = pl.BlockSpec((th, HG_W), lambda i: (i, 0))
    n_chunks = t_rows // HG_CHUNK
    return _call(
        body, (projp, projp, projp, projp, lb, ng), name=name, grid=(t_rows // th,),
        in_specs=[col(0), col(1), col(2), col(3), vec, vec],
        out_specs=[row, row, pl.BlockSpec((nct, HG_W, HG_W), lambda i: (i, 0, 0))],
        out_shape=[jax.ShapeDtypeStruct((t_rows, HG_W), BF16), jax.ShapeDtypeStruct((t_rows, HG_W), F32),
                   jax.ShapeDtypeStruct((n_chunks, HG_W, HG_W), F32)],
        scratch_shapes=[pltpu.VMEM((HG_W, HG_W), F32), pltpu.VMEM((nct * HG_W, th), BF16),
                        pltpu.VMEM((nct * HG_W, HG_W), F32), pltpu.VMEM((th, HG_W), F32),
                        pltpu.VMEM((th, HG_W), F32), pltpu.VMEM((th, HG_W), F32)],
        sem=("arbitrary",), job=job)


def _hgrn_bwd(projp, lb, ng, opre, states, dcat, *, name):
    t_rows = projp.shape[0]
    th = min(HG_TILE, t_rows)
    nct = th // HG_CHUNK
    nt = t_rows // th

    def body(q_ref, f_ref, i_ref, g_ref, lb_ref, ng_ref, opre_ref, st_in_ref, do_ref,
             dproj_ref, dng_ref, dlb_ref,
             gst_ref, dotm_ref, qg_ref, v_ref, kte_ref, dop_ref, dec_ref, dkte_ref, dvi_ref, dqe_ref, ddec_ref):
        rm, seg_cumsum, seg_rcumsum, head_f32, headsum = _hgrn_common(th)

        @pl.when(pl.program_id(0) == 0)
        def _():
            gst_ref[...] = jnp.zeros_like(gst_ref)
            dng_ref[...] = jnp.zeros_like(dng_ref)
            dlb_ref[...] = jnp.zeros_like(dlb_ref)

        qr, fl, v, g = q_ref[...], f_ref[...], i_ref[...], g_ref[...]
        lb, ngv = lb_ref[...], ng_ref[...]
        sg, f, lf, k, q, sq = _hgrn_gates(qr, fl, lb)
        b = seg_cumsum(lf)
        blast = seg_rcumsum(jnp.where(rm == HG_CHUNK - 1, b, 0.0))
        eb = jnp.exp(b)
        ekb = jnp.exp(blast - b)
        qe, kte, dec = q * eb, k * ekb, jnp.exp(blast)

        do_out, op = do_ref[...], opre_ref[...]
        sgg = _sigmoid(g)
        sil = g * sgg
        r = lax.rsqrt(headsum(op * op) * (1.0 / HEAD) + RMS_EPS)
        on = op * r
        dng_ref[...] += jnp.sum(do_out * on * sil, axis=0, keepdims=True)
        dg = do_out * on * ngv * (sgg * (1.0 + g * (1.0 - sgg)))
        don = do_out * ngv * sil
        dop = r * (don - on * (headsum(don * on) * (1.0 / HEAD)))

        v_ref[...] = v
        kte_ref[...] = kte
        dop_ref[...] = dop
        dec_ref[...] = dec
        dot_t = dop.T
        lane_chunk = lax.broadcasted_iota(jnp.int32, (HG_W, th), 1) // HG_CHUNK
        for c in range(nct):
            dotm_ref[c * HG_W:(c + 1) * HG_W, :] = jnp.where(lane_chunk == c, dot_t, 0.0).astype(BF16)
        qg_ref[...] = jnp.dot(dotm_ref[...], qe.astype(BF16), preferred_element_type=F32)

        gs = gst_ref[...]
        for c in reversed(range(nct)):
            rows = slice(c * HG_CHUNK, (c + 1) * HG_CHUNK)
            s = st_in_ref[c]
            gm = (gs * head_f32).astype(BF16)
            dkte_ref[rows, :] = jnp.dot(v_ref[rows, :].astype(BF16), gm, preferred_element_type=F32)
            dvi_ref[rows, :] = lax.dot_general(kte_ref[rows, :].astype(BF16), gm, (((1,), (1,)), ((), ())),
                                               preferred_element_type=F32)
            dqe_ref[rows, :] = jnp.dot(dop_ref[rows, :].astype(BF16), s.astype(BF16), preferred_element_type=F32)
            ddec_ref[rows, :] = jnp.broadcast_to(jnp.sum(gs * s, axis=0, keepdims=True), (HG_CHUNK, HG_W))
            dec_c = jnp.max(dec_ref[rows, :], axis=0, keepdims=True)
            gs = gs * dec_c + qg_ref[c * HG_W:(c + 1) * HG_W, :] * head_f32
        gst_ref[...] = gs

        dkte, dqe = dkte_ref[...], dqe_ref[...]
        dq = dqe * eb
        dk = dkte * ekb
        db = dqe * qe - dkte * kte
        dv = dvi_ref[...]
        dblast = dkte * kte + jnp.where(rm == HG_CHUNK - 1, ddec_ref[...] * dec, 0.0)

        for d in range(HG_CHUNK):
            kd, bd, vd = _shifted(k, d, th), _shifted(b, d, th), _shifted(v, d, th)
            e = jnp.exp(jnp.where(rm >= d, b - bd, -1e30))
            p = q * kd * e
            sc = headsum(p, 1)
            dsc = headsum(dop * vd, 1)
            dv = dv + _unshift(sc * dop, d, th)
            dq = dq + dsc * kd * e
            dk = dk + _unshift(dsc * q * e, d, th)
            darg = dsc * p
            db = db + darg - _unshift(darg, d, th)

        db = db + jnp.where(rm == HG_CHUNK - 1, seg_cumsum(dblast), 0.0)
        dlf = seg_rcumsum(db)
        df = dlf / f - dk
        dlb_ref[...] += jnp.sum(df * (1.0 - sg), axis=0, keepdims=True)
        dfl = df * (1.0 - lb) * sg * (1.0 - sg)
        dqr = dq * (sq * (1.0 + qr * (1.0 - sq)))
        dproj_ref[...] = jnp.concatenate([dqr, dfl, dv, dg], axis=1).astype(dproj_ref.dtype)

    rev = lambda i: nt - 1 - i
    col = lambda j: pl.BlockSpec((th, HG_W), lambda i, j=j: (rev(i), j))
    vec = pl.BlockSpec((1, HG_W), lambda i: (0, 0))
    row = pl.BlockSpec((th, HG_W), lambda i: (rev(i), 0))
    tile_f32 = pltpu.VMEM((th, HG_W), F32)
    return pl.pallas_call(
        body, name=name, grid=(nt,),
        in_specs=[col(0), col(1), col(2), col(3), vec, vec, row,
                  pl.BlockSpec((nct, HG_W, HG_W), lambda i: (rev(i), 0, 0)), col(0)],
        out_specs=[pl.BlockSpec((th, 4 * HG_W), lambda i: (rev(i), 0)), vec, vec],
        out_shape=[jax.ShapeDtypeStruct((t_rows, 4 * HG_W), BF16), jax.ShapeDtypeStruct((1, HG_W), F32),
                   jax.ShapeDtypeStruct((1, HG_W), F32)],
        scratch_shapes=[pltpu.VMEM((HG_W, HG_W), F32), pltpu.VMEM((nct * HG_W, th), BF16),
                        pltpu.VMEM((nct * HG_W, HG_W), F32)] + [tile_f32] * 8,
        compiler_params=_cparams(("arbitrary",)),
    )(projp, projp, projp, projp, lb, ng, opre, states, dcat)


_INV_SQRT2 = 1.0 / math.sqrt(2.0)
_INV_SQRT2PI = 1.0 / math.sqrt(2.0 * math.pi)


def _gelu(x):
    return 0.5 * x * (1.0 + lax.erf(x * _INV_SQRT2))


def _gelu_grad(x):
    return 0.5 * (1.0 + lax.erf(x * _INV_SQRT2)) + x * jnp.exp(-0.5 * x * x) * _INV_SQRT2PI


def _sgu_parts(bu, bv, lg, lbias, w_ref, n_groups):
    c = SGU_CHUNK
    tril = (lax.broadcasted_iota(jnp.int32, (c, c), 0) >= lax.broadcasted_iota(jnp.int32, (c, c), 1)).astype(F32)
    gid = lax.broadcasted_iota(jnp.int32, bu.shape, 1) // HEAD
    u = _gelu(bu)
    gv = _gelu(bv)
    mu = jnp.mean(gv, axis=-1, keepdims=True)
    xc = gv - mu
    rstd = lax.rsqrt(jnp.mean(xc * xc, axis=-1, keepdims=True) + LN_EPS)
    xhat = xc * rstd
    vn = xhat * lg + lbias
    ws = [w_ref[gi] * tril for gi in range(n_groups)]
    return tril, gid, u, rstd, xhat, vn, ws


def _sgu_fwd(projp, lg, lbias, w_s, bias_full, *, name, job=None):
    t_rows = projp.shape[0]
    n_groups = w_s.shape[0]
    c = SGU_CHUNK

    def body(u_ref, v_ref, lg_ref, lb_ref, w_ref, bias_ref, o_ref):
        _, gid, u, _, _, vn, ws = _sgu_parts(u_ref[...], v_ref[...], lg_ref[...], lb_ref[...], w_ref, n_groups)
        vnb = vn.astype(BF16)
        z = bias_ref[...]
        for gi in range(n_groups):
            z = z + jnp.where(gid == gi, jnp.dot(ws[gi].astype(BF16), vnb, preferred_element_type=F32), 0.0)
        o_ref[...] = (u * z).astype(o_ref.dtype)

    col = lambda j: pl.BlockSpec((c, HG_W), lambda i, j=j: (i, j))
    return _call(
        body, (projp, projp, lg, lbias, w_s, bias_full), name=name, grid=(t_rows // c,),
        in_specs=[col(4), col(5), _const_spec(lg), _const_spec(lbias), _const_spec(w_s), _const_spec(bias_full)],
        out_specs=pl.BlockSpec((c, HG_W), lambda i: (i, 0)),
        out_shape=jax.ShapeDtypeStruct((t_rows, HG_W), BF16), sem=("arbitrary",), job=job)


def _sgu_bwd(projp, lg, lbias, w_s, bias_full, dcat, *, name):
    t_rows = projp.shape[0]
    n_groups = w_s.shape[0]
    c = SGU_CHUNK
    n = t_rows // c

    def body(u_ref, v_ref, lg_ref, lb_ref, w_ref, bias_ref, do_ref,
             dproj_ref, dlg_ref, dlb_ref, dw_ref, dbs_ref, dbias_acc):
        i = pl.program_id(0)

        @pl.when(i == 0)
        def _():
            dlg_ref[...] = jnp.zeros_like(dlg_ref)
            dlb_ref[...] = jnp.zeros_like(dlb_ref)
            dw_ref[...] = jnp.zeros_like(dw_ref)
            dbias_acc[...] = jnp.zeros_like(dbias_acc)

        bu, bv, lg_v = u_ref[...], v_ref[...], lg_ref[...]
        tril, gid, u, rstd, xhat, vn, ws = _sgu_parts(bu, bv, lg_v, lb_ref[...], w_ref, n_groups)
        vnb = vn.astype(BF16)
        z = bias_ref[...]
        for gi in range(n_groups):
            z = z + jnp.where(gid == gi, jnp.dot(ws[gi].astype(BF16), vnb, preferred_element_type=F32), 0.0)
        do = do_ref[...]
        dbu = do * z * _gelu_grad(bu)
        dz = do * u
        dbias_acc[...] += dz
        dvn = jnp.zeros_like(dz)
        for gi in range(n_groups):
            dzg = jnp.where(gid == gi, dz, 0.0).astype(BF16)
            dw_ref[gi] += lax.dot_general(dzg, vnb, (((1,), (1,)), ((), ())), preferred_element_type=F32) * tril
            dvn = dvn + jnp.dot(ws[gi].T.astype(BF16), dzg, preferred_element_type=F32)
        dlg_ref[...] += jnp.sum(dvn * xhat, axis=0, keepdims=True)
        dlb_ref[...] += jnp.sum(dvn, axis=0, keepdims=True)
        dxh = dvn * lg_v
        dgv = rstd * (dxh - jnp.mean(dxh, axis=-1, keepdims=True)
                      - xhat * jnp.mean(dxh * xhat, axis=-1, keepdims=True))
        dproj_ref[...] = jnp.concatenate([dbu, dgv * _gelu_grad(bv)], axis=1).astype(dproj_ref.dtype)

        @pl.when(i == n - 1)
        def _():
            dbs_ref[...] = jnp.sum(dbias_acc[...].T.reshape(n_groups, HEAD, c), axis=1)

    col = lambda j: pl.BlockSpec((c, HG_W), lambda i, j=j: (i, j))
    return pl.pallas_call(
        body, name=name, grid=(n,),
        in_specs=[col(4), col(5), _const_spec(lg), _const_spec(lbias), _const_spec(w_s), _const_spec(bias_full),
                  col(1)],
        out_specs=[pl.BlockSpec((c, 2 * HG_W), lambda i: (i, 0)), _const_spec(lg), _const_spec(lbias),
                   _const_spec(w_s), pl.BlockSpec((n_groups, c), lambda i: (0, 0))],
        out_shape=[jax.ShapeDtypeStruct((t_rows, 2 * HG_W), BF16), jax.ShapeDtypeStruct(lg.shape, F32),
                   jax.ShapeDtypeStruct(lbias.shape, F32), jax.ShapeDtypeStruct(w_s.shape, F32),
                   jax.ShapeDtypeStruct((n_groups, c), F32)],
        scratch_shapes=[pltpu.VMEM((c, HG_W), F32)],
        compiler_params=_cparams(("arbitrary",)),
    )(projp, projp, lg, lbias, w_s, bias_full, dcat)


def _rope_tables(positions):
    t = positions.shape[0]
    inv_freq = ROPE_THETA ** (-jnp.arange(0, 32, 2, dtype=F32) / 32)
    ang = positions.astype(F32)[:, None] * inv_freq
    cos, sin = jnp.cos(ang), jnp.sin(ang)
    z = lambda w: jnp.zeros((t, w), F32)
    cos_t = jnp.concatenate([jnp.ones((t, 64), F32), cos, cos, z(32)], axis=1)
    sin_up = jnp.concatenate([z(80), sin, z(32)], axis=1)
    sin_dn = jnp.concatenate([z(64), -sin, z(48)], axis=1)
    return cos_t, sin_up, sin_dn


def _rep(x, n):
    return x if n == 1 else jnp.concatenate([x] * n, axis=1)


def _rope(x, cos_t, sin_up, sin_dn):
    w = x.shape[1]
    return x * cos_t + pltpu.roll(x, 16, 1) * sin_up + pltpu.roll(x, w - 16, 1) * sin_dn


def _rope_t(dy, cos_t, sin_up, sin_dn):
    w = dy.shape[1]
    return dy * cos_t + pltpu.roll(dy * sin_up, w - 16, 1) + pltpu.roll(dy * sin_dn, 16, 1)


def _mla_prep(q, kv, projp, tables, *, name):
    nh = N_ATT_HEADS

    def fn(qv, kvv, kr, cos_t, sin_up, sin_dn):
        qr = _rope(qv, _rep(cos_t, nh), _rep(sin_up, nh), _rep(sin_dn, nh))
        krr = _rope(kr, cos_t, sin_up, sin_dn)
        lane = lax.broadcasted_iota(jnp.int32, kvv.shape, 1) % LANES
        return qr, jnp.where(lane < HEAD, kvv, 0.0) + _rep(krr, nh), kvv

    w = q.shape[1]
    return _rowwise(fn, [q, kv, (projp, LANES, P_KR // LANES)] + list(tables), [],
                    [(w, BF16), (w, BF16), (w, BF16)], name=name)


def _mla_prep_bwd(dqr, dkf, tables, *, name):
    nh = N_ATT_HEADS

    def fn(dq, dk, cos_t, sin_up, sin_dn):
        dqp = _rope_t(dq, _rep(cos_t, nh), _rep(sin_up, nh), _rep(sin_dn, nh))
        dkrr = dk[:, 0:LANES]
        for h in range(1, nh):
            dkrr = dkrr + dk[:, LANES * h:LANES * (h + 1)]
        return dqp, _rope_t(dkrr, cos_t, sin_up, sin_dn)

    return _rowwise(fn, [dqr, dkf] + list(tables), [], [(dqr.shape[1], BF16), (LANES, BF16)], name=name)


_LOG2E = 1.0 / math.log(2.0)
_NT = (((1,), (1,)), ((), ()))
_TN = (((0,), (0,)), ((), ()))


def _attn_fwd(qr, kf, kvb, *, name, job=None):
    t_rows = qr.shape[0]
    tq = min(ATT_TQ, t_rows)
    nb = t_rows // tq
    scale = ATT_D ** -0.5

    c2 = scale * _LOG2E

    def body(q_ref, kf_ref, kvb_ref, o_ref, lse_ref):
        qi = pl.program_id(1)
        lane = lax.broadcasted_iota(jnp.int32, (tq, LANES), 1)
        causal_t = (lax.broadcasted_iota(jnp.int32, (tq, tq), 0) <= lax.broadcasted_iota(jnp.int32, (tq, tq), 1))
        heads = [slice(hh * LANES, (hh + 1) * LANES) for hh in range(2)]
        qs = [q_ref[:, cols] for cols in heads]

        def block(first, n_keys, carry, diagonal):
            rows = pl.ds(pl.multiple_of(first * tq, tq), n_keys)
            new = []
            for q, cols, (m_old, l_old, acc_t) in zip(qs, heads, carry):
                s_t = lax.dot_general(kf_ref[rows, cols], q, _NT, preferred_element_type=F32)
                if diagonal:
                    s_t = jnp.where(causal_t, s_t, -1e30)
                m_new = jnp.maximum(m_old, jnp.max(s_t, axis=0, keepdims=True))
                p_t = jnp.exp2((s_t - m_new) * c2)
                a = jnp.exp2((m_old - m_new) * c2)
                pv_t = lax.dot_general(kvb_ref[rows, cols], p_t.astype(BF16), _TN, preferred_element_type=F32)
                new.append((m_new, a * l_old + jnp.sum(p_t, axis=0, keepdims=True), a * acc_t + pv_t))
            return tuple(new)

        init = (jnp.full((1, tq), -1e30, F32), jnp.zeros((1, tq), F32), jnp.zeros((LANES, tq), F32))
        carry = lax.fori_loop(0, qi // 4, lambda g, c: block(4 * g, 4 * tq, c, False), (init, init))
        carry = lax.cond((qi // 2) % 2 == 1, lambda c: block(4 * (qi // 4), 2 * tq, c, False), lambda c: c, carry)
        carry = lax.cond(qi % 2 == 1, lambda c: block(qi - 1, tq, c, False), lambda c: c, carry)
        outs = []
        for hh, (m_fin, l_fin, acc_t) in enumerate(block(qi, tq, carry, True)):
            lse_ref[hh] = m_fin * scale + jnp.log(l_fin)
            outs.append((acc_t / l_fin).T)
        o_ref[...] = jnp.where(lane < HEAD, pltpu.roll(outs[0], HEAD, 1), outs[1])

    pair = pl.BlockSpec((t_rows, 2 * LANES), lambda pr, qi: (0, pr))
    return _call(
        body, (qr, kf, kvb), name=name, grid=(N_ATT_HEADS // 2, nb),
        in_specs=[pl.BlockSpec((tq, 2 * LANES), lambda pr, qi: (qi, pr)), pair, pair],
        out_specs=[pl.BlockSpec((tq, LANES), lambda pr, qi: (qi, pr)),
                   pl.BlockSpec((2, 1, tq), lambda pr, qi: (pr, 0, qi))],
        out_shape=[jax.ShapeDtypeStruct((t_rows, N_ATT_HEADS * HEAD), F32),
                   jax.ShapeDtypeStruct((N_ATT_HEADS, 1, t_rows), F32)],
        sem=("parallel", "arbitrary"), job=job)


def _attn_bwd(qr, kf, kvb, dcat, o, lse, *, name, job=None):
    t_rows = qr.shape[0]
    tq = min(ATT_TQ, t_rows)
    nb = t_rows // tq
    scale = ATT_D ** -0.5
    c2 = scale * _LOG2E
    do_off = 2 * HG_W // LANES

    def body(q_ref, kf_ref, kvb_ref, do_ref, o_ref, lse_ref, dq_ref, dkv_ref, dk_ref):
        ki = pl.program_id(1)

        @pl.when(ki == 0)
        def _():
            dq_ref[...] = jnp.zeros_like(dq_ref)

        lane = lax.broadcasted_iota(jnp.int32, (tq, LANES), 1)
        causal_t = (lax.broadcasted_iota(jnp.int32, (tq, tq), 0) <= lax.broadcasted_iota(jnp.int32, (tq, tq), 1))
        heads = [slice(hh * LANES, (hh + 1) * LANES) for hh in range(2)]
        ks = [kf_ref[:, cols] for cols in heads]
        vs = [kvb_ref[:, cols] for cols in heads]

        def block(qi, n_q, carry, diagonal):
            rows = pl.ds(pl.multiple_of(qi * tq, tq), n_q)
            do_pair, o_pair = do_ref[rows, :], o_ref[rows, :]
            upper = lax.broadcasted_iota(jnp.int32, do_pair.shape, 1) >= HEAD
            new = []
            for hh, (cols, k, v, (dk, dv)) in enumerate(zip(heads, ks, vs, carry)):
                q = q_ref[rows, cols]
                do, ov = (pltpu.roll(do_pair, HEAD, 1), pltpu.roll(o_pair, HEAD, 1)) if hh == 0 else (do_pair, o_pair)
                do = jnp.where(upper, do, 0.0)
                delta = jnp.sum((do * ov).T, axis=0, keepdims=True)
                s_t = lax.dot_general(k, q, _NT, preferred_element_type=F32)
                if diagonal:
                    s_t = jnp.where(causal_t, s_t, -1e30)
                p_t = jnp.exp2(s_t * c2 - lse_ref[hh, :, rows] * _LOG2E)
                dob = do.astype(BF16)
                dv = dv + jnp.dot(p_t.astype(BF16), dob, preferred_element_type=F32)
                dp_t = lax.dot_general(v, dob, _NT, preferred_element_type=F32)
                ds_t = (p_t * (dp_t - delta) * scale).astype(BF16)
                dk = dk + jnp.dot(ds_t, q, preferred_element_type=F32)
                dq_ref[rows, cols] += lax.dot_general(ds_t, k, _TN, preferred_element_type=F32)
                new.append((dk, dv))
            return tuple(new)

        zero = jnp.zeros((tq, LANES), F32)
        carry = block(ki, tq, ((zero, zero), (zero, zero)), True)
        rest = nb - 1 - ki
        carry = lax.fori_loop(0, rest // 2, lambda g, c: block(ki + 1 + 2 * g, 2 * tq, c, False), carry)
        carry = lax.cond(rest % 2 == 1, lambda c: block(nb - 1, tq, c, False), lambda c: c, carry)
        dkv_ref[...] = jnp.concatenate([jnp.where(lane < HEAD, dk, dv) for dk, dv in carry],
                                       axis=1).astype(dkv_ref.dtype)
        dk_ref[...] = jnp.concatenate([dk for dk, _ in carry], axis=1)

    pair_all = pl.BlockSpec((t_rows, 2 * LANES), lambda pr, ki: (0, pr))
    pair_blk = pl.BlockSpec((tq, 2 * LANES), lambda pr, ki: (ki, pr))
    wide = jax.ShapeDtypeStruct((t_rows, N_ATT_HEADS * LANES), F32)
    return _call(
        body, (qr, kf, kvb, dcat, o, lse), name=name, grid=(N_ATT_HEADS // 2, nb),
        in_specs=[pair_all, pair_blk, pair_blk,
                  pl.BlockSpec((t_rows, LANES), lambda pr, ki: (0, do_off + pr)),
                  pl.BlockSpec((t_rows, LANES), lambda pr, ki: (0, pr)),
                  pl.BlockSpec((2, 1, t_rows), lambda pr, ki: (pr, 0, 0))],
        out_specs=[pair_all, pair_blk, pair_blk],
        out_shape=[wide, jax.ShapeDtypeStruct(wide.shape, BF16), wide],
        sem=("parallel", "arbitrary"), job=job)


def _my_pos():
    return lax.axis_index("x"), lax.axis_index("y"), lax.axis_index("c")


def _all_gather(xs, *, name, columns=True):
    return _gather_forward(_run_job(_gather_job(xs, columns), name=name), name=name + "_forward")


def _remote(src, dst, send_sems, recv_sems, k, dev):
    return pltpu.make_async_remote_copy(src_ref=src, dst_ref=dst, send_sem=send_sems.at[k], recv_sem=recv_sems.at[k],
                                        device_id=dev, device_id_type=MESH)


def _block(ref, idx):
    if len(ref.shape) == 2:
        return ref.at[:, pl.ds(pl.multiple_of(idx * LANES, LANES), LANES)]
    return ref.at[idx]


def _gather_job(xs, columns=True):
    n = len(xs)

    def make(x_refs, out_refs, send_sems, recv_sems, local_sems):
        mx, my, mc = _my_pos()
        mine = 4 * mx + 2 * my + mc
        peers = [(mx, my, 1 - mc), (1 - mx, my, mc), (mx, 1 - my, mc), (1 - mx, 1 - my, mc)]
        sends, recvs, local = [], [], []
        for a in range(n):
            local.append(pltpu.make_async_copy(x_refs[a], _block(out_refs[a], mine), local_sems.at[a]))
            for k, dev in enumerate(peers):
                theirs = 4 * dev[0] + 2 * dev[1] + dev[2]
                sends.append(_remote(x_refs[a], _block(out_refs[a], mine), send_sems, recv_sems, 4 * a + k, dev))
                recvs.append(_remote(x_refs[a], _block(out_refs[a], theirs), send_sems, recv_sems, 4 * a + k, dev))
        return sends, recvs, local

    def gathered(x):
        if columns and x.ndim == 2 and x.shape[1] == LANES:
            return jax.ShapeDtypeStruct((x.shape[0], N_DEV * LANES), x.dtype)
        return jax.ShapeDtypeStruct((N_DEV,) + x.shape, x.dtype)

    return _copies_job(xs, [gathered(x) for x in xs], 4 * n, n, make)


def _forward_job(gs):
    n = len(gs)

    def make(in_refs, out_refs, send_sems, recv_sems, local_sems):
        mx, my, mc = _my_pos()
        chips = [(1 - mx, my), (mx, 1 - my), (1 - mx, 1 - my)]
        sends, recvs = [], []
        for a in range(n):
            for j, (cx, cy) in enumerate(chips):
                here = _block(out_refs[a], 4 * cx + 2 * cy + mc)
                there = _block(out_refs[a], 4 * cx + 2 * cy + 1 - mc)
                sends.append(_remote(here, here, send_sems, recv_sems, 3 * a + j, (mx, my, 1 - mc)))
                recvs.append(_remote(here, there, send_sems, recv_sems, 3 * a + j, (mx, my, 1 - mc)))
        return sends, recvs, []

    shapes = [jax.ShapeDtypeStruct(g.shape, g.dtype) for g in gs]
    return _copies_job(gs, shapes, 3 * n, 0, make, in_place=True)


def _gather_forward(gs, *, name):
    return _run_job(_forward_job(gs), name=name)


def _pair_job(xs):
    n = len(xs)

    def make(x_refs, out_refs, send_sems, recv_sems, local_sems):
        mx, my, mc = _my_pos()

        def src(ref, g):
            return _block(ref, 2 * g + 1 - mc) if len(ref.shape) == 2 else ref.at[g, 1 - mc]

        copies = [_remote(src(x_refs[a], g), out_refs[a].at[g], send_sems, recv_sems, 4 * a + g, (mx, my, 1 - mc))
                  for a in range(n) for g in range(4)]
        return copies, copies, []

    shapes = [jax.ShapeDtypeStruct((4, x.shape[0], LANES) if x.ndim == 2 else (4,) + x.shape[2:], x.dtype)
              for x in xs]
    return _copies_job(xs, shapes, 4 * n, 0, make)


def _pair_add(x, r, core, *, name):
    _, a, b = r.shape
    ta = _row_tile(a, 512)

    def body(c_ref, x_ref, r_ref, o_ref):
        o_ref[...] = (x_ref[...] + r_ref[...]).astype(o_ref.dtype)

    blk = pl.BlockSpec((None, ta, b), lambda g, i, c_ref: (g, i, 0))
    own = (pl.BlockSpec((ta, b), lambda g, i, c_ref: (i, 2 * g + c_ref[0])) if x.ndim == 2
           else pl.BlockSpec((None, None, ta, b), lambda g, i, c_ref: (g, c_ref[0], i, 0)))
    return pl.pallas_call(
        body, name=name,
        grid_spec=pltpu.PrefetchScalarGridSpec(
            num_scalar_prefetch=1, grid=(4, a // ta), in_specs=[own, blk], out_specs=blk),
        out_shape=jax.ShapeDtypeStruct((4, a, b), BF16),
        compiler_params=_cparams(("parallel", "parallel")),
    )(core, x, r)


def _quad_job(xs):
    n = len(xs)

    def make(x_refs, out_refs, send_sems, recv_sems, local_sems):
        mx, my, mc = _my_pos()
        mine = 2 * mx + my
        peers = [((1 - mx, my, mc), 2 * (1 - mx) + my), ((mx, 1 - my, mc), 2 * mx + 1 - my),
                 ((1 - mx, 1 - my, mc), 2 * (1 - mx) + 1 - my)]
        sends, recvs, local = [], [], []
        for a in range(n):
            local.append(pltpu.make_async_copy(x_refs[a].at[mine], out_refs[a].at[mine], local_sems.at[a]))
            for k, (dev, g) in enumerate(peers):
                sends.append(_remote(x_refs[a].at[g], out_refs[a].at[mine], send_sems, recv_sems, 3 * a + k, dev))
                recvs.append(_remote(x_refs[a].at[g], out_refs[a].at[g], send_sems, recv_sems, 3 * a + k, dev))
        return sends, recvs, local

    shapes = [jax.ShapeDtypeStruct(x.shape, x.dtype) for x in xs]
    return _copies_job(xs, shapes, 3 * n, n, make)


def _row_tile(r, pref):
    t = min(pref, r)
    while r % t or (t % 8 and t != r):
        t -= 1
    return t


def _adamw(parts, w, m, v, layer, *, name, tile=256, into=None):
    g, a, b = parts.shape
    tile = _row_tile(a, tile)
    c1 = 1.0 / (1.0 - ADAM_B1 ** ADAM_STEP)
    c2 = 1.0 / (1.0 - ADAM_B2 ** ADAM_STEP)
    into = tuple(into or ())

    def body(p_ref, w_ref, m_ref, v_ref, *refs):
        g_ref, d_ref, mo_ref, vo_ref = refs[len(into):]
        grad = p_ref[0].astype(F32)
        for j in range(1, g):
            grad = grad + p_ref[j].astype(F32)
        mn = ADAM_B1 * m_ref[...] + (1.0 - ADAM_B1) * grad
        vn = ADAM_B2 * v_ref[...] + (1.0 - ADAM_B2) * (grad * grad)
        g_ref[...] = grad
        mo_ref[...] = mn
        vo_ref[...] = vn
        d_ref[...] = -ADAM_LR * ((mn * c1) / (jnp.sqrt(vn * c2) + ADAM_EPS) + ADAM_WD * w_ref[...])

    if layer is None:
        src, shape = pl.BlockSpec((tile, b), lambda i: (i, 0)), (a, b)
    else:
        src, shape = pl.BlockSpec((None, tile, b), lambda i: (layer, i, 0)), w.shape
    return pl.pallas_call(
        body, name=name, grid=(a // tile,),
        in_specs=[pl.BlockSpec((g, tile, b), lambda i: (0, i, 0)), src, src, src] + [_ANY] * len(into),
        out_specs=[src] * 4,
        out_shape=[jax.ShapeDtypeStruct(shape, F32)] * 4,
        input_output_aliases={4 + i: i for i in range(len(into))},
        compiler_params=_cparams(("parallel",)),
    )(parts, w, m, v, *into)


W_IN_SHARD = 276


def _w_in_dest(col):
    return jnp.where(col < P_KR, col, jnp.where(col < P_KR + 256, col + (P_CKV - P_KR), col - 2176 + P_KR + HEAD))


PLACE_TILE = 384
PLACE_SHARDS = 3
PICK_TILE = 128
PICK_TILES = 4


def _w_in_tables():
    col = np.arange(N_DEV * W_IN_SHARD)
    dest = np.where(col < P_KR, col, np.where(col < P_KR + 256, col + (P_CKV - P_KR), col - 2176 + P_KR + HEAD))
    shard = col // W_IN_SHARD

    def filled(used, universe, n):
        used = sorted(set(int(u) for u in used))
        assert len(used) <= n, used
        return used + [u for u in universe if u not in used][:n - len(used)]

    place = [filled(shard[dest // PLACE_TILE == c], range(N_DEV), PLACE_SHARDS) for c in range(P_COLS // PLACE_TILE)]
    pick = [filled(dest[shard == j] // PICK_TILE, range(P_COLS // PICK_TILE), PICK_TILES) for j in range(N_DEV)]
    return np.asarray(place, np.int32).reshape(-1), np.asarray(pick, np.int32).reshape(-1)


def _place_w_in(g, *, name):
    _, d, sh = g.shape
    tc, ns = PLACE_TILE, PLACE_SHARDS
    table = jnp.asarray(_w_in_tables()[0])

    def body(tab_ref, g_ref, o_ref, acc_ref):
        ct, s = pl.program_id(0), pl.program_id(1)
        j = tab_ref[ct * ns + s]

        @pl.when(s == 0)
        def _():
            acc_ref[...] = jnp.zeros_like(acc_ref)

        src = j * sh + lax.broadcasted_iota(jnp.int32, (sh, tc), 0)
        dst = ct * tc + lax.broadcasted_iota(jnp.int32, (sh, tc), 1)
        place = (_w_in_dest(src) == dst).astype(BF16)
        acc_ref[...] += jnp.dot(g_ref[...], place, preferred_element_type=F32)

        @pl.when(s == ns - 1)
        def _():
            o_ref[...] = acc_ref[...].astype(o_ref.dtype)

    return pl.pallas_call(
        body, name=name,
        grid_spec=pltpu.PrefetchScalarGridSpec(
            num_scalar_prefetch=1, grid=(P_COLS // tc, ns),
            in_specs=[pl.BlockSpec((None, d, sh), lambda ct, s, tab: (tab[ct * ns + s], 0, 0))],
            out_specs=pl.BlockSpec((d, tc), lambda ct, s, tab: (0, ct)),
            scratch_shapes=[pltpu.VMEM((d, tc), F32)]),
        out_shape=jax.ShapeDtypeStruct((d, P_COLS), BF16),
        compiler_params=_cparams(("parallel", "arbitrary")),
    )(table, g)


def _unplace_w_in(dw, *, name):
    d = dw.shape[0]
    sh, tk, nt = W_IN_SHARD, PICK_TILE, PICK_TILES
    table = jnp.asarray(_w_in_tables()[1])

    def body(tab_ref, dw_ref, o_ref):
        j, kk = pl.program_id(0), pl.program_id(1)
        tile = tab_ref[j * nt + kk]
        src = j * sh + lax.broadcasted_iota(jnp.int32, (tk, sh), 1)
        dst = tile * tk + lax.broadcasted_iota(jnp.int32, (tk, sh), 0)
        pick = (_w_in_dest(src) == dst).astype(BF16)
        part = _split_dot(dw_ref[...], pick)

        @pl.when(kk == 0)
        def _():
            o_ref[...] = part

        @pl.when(kk > 0)
        def _():
            o_ref[...] += part

    return pl.pallas_call(
        body, name=name,
        grid_spec=pltpu.PrefetchScalarGridSpec(
            num_scalar_prefetch=1, grid=(N_DEV, nt),
            in_specs=[pl.BlockSpec((d, tk), lambda j, kk, tab: (0, tab[j * nt + kk]))],
            out_specs=pl.BlockSpec((None, d, sh), lambda j, kk, tab: (j, 0, 0))),
        out_shape=jax.ShapeDtypeStruct((N_DEV, d, sh), F32),
        compiler_params=_cparams(("parallel", "arbitrary")),
    )(table, dw)


def _gate_up_swiglu(h1, wgu, *, name):
    t_rows, k = h1.shape
    w = wgu.shape[2]
    tm = _tile(t_rows, 1024)

    def body(a_ref, wg_ref, wu_ref, gu_ref, act_ref):
        a = a_ref[...].astype(BF16)
        gate = jnp.dot(a, wg_ref[...], preferred_element_type=F32)
        up = jnp.dot(a, wu_ref[...], preferred_element_type=F32)
        gu_ref[0] = gate.astype(gu_ref.dtype)
        gu_ref[1] = up.astype(gu_ref.dtype)
        act_ref[...] = (gate * _sigmoid(gate) * up).astype(act_ref.dtype)

    return pl.pallas_call(
        body, name=name, grid=(t_rows // tm, 4),
        in_specs=[pl.BlockSpec((tm, k), lambda i, j: (i, 0)),
                  pl.BlockSpec((None, k, w), lambda i, j: (j, 0, 0)),
                  pl.BlockSpec((None, k, w), lambda i, j: (j + 4, 0, 0))],
        out_specs=[pl.BlockSpec((2, None, tm, w), lambda i, j: (0, j, i, 0)),
                   pl.BlockSpec((None, tm, w), lambda i, j: (j, i, 0))],
        out_shape=[jax.ShapeDtypeStruct((2, 4, t_rows, w), BF16), jax.ShapeDtypeStruct((4, t_rows, w), BF16)],
        compiler_params=_cparams(("parallel", "arbitrary")),
    )(h1, wgu, wgu)


def _down_dx_swiglu(dffn, wdown, gu, *, name):
    t_rows, k = dffn.shape
    w = gu.shape[3]
    tm = _tile(t_rows, 1024)

    def body(d_ref, w_ref, gu_ref, o_ref):
        for r0 in range(0, tm, tm // 4):
            rows = slice(r0, r0 + tm // 4)
            dact = lax.dot_general(d_ref[rows, :].astype(BF16), w_ref[...], _NT, preferred_element_type=F32)
            gate, up = gu_ref[0, rows, :].astype(F32), gu_ref[1, rows, :].astype(F32)
            sg = _sigmoid(gate)
            silu = gate * sg
            o_ref[0, rows, :] = (dact * up * (sg + silu - silu * sg)).astype(o_ref.dtype)
            o_ref[1, rows, :] = (dact * silu).astype(o_ref.dtype)

    blk = pl.BlockSpec((2, None, tm, w), lambda i, j: (0, j, i, 0))
    return pl.pallas_call(
        body, name=name, grid=(t_rows // tm, 4),
        in_specs=[pl.BlockSpec((tm, k), lambda i, j: (i, 0)), pl.BlockSpec((w, k), lambda i, j: (j, 0)), blk],
        out_specs=blk, out_shape=jax.ShapeDtypeStruct(gu.shape, BF16),
        compiler_params=_cparams(("parallel", "arbitrary")),
    )(dffn, wdown, gu)


BIG = ("w_in", "mla_w_uq", "mla_w_ukv", "w_out", "w_gate_up", "w_down", "ple_w_gate", "ple_w_proj")
SMALL = ("ln_in_g", "ln_in_b", "hgrn_lb_logits", "hgrn_norm_g", "sgu_ln_g", "sgu_ln_b", "sgu_w_s", "sgu_b_s",
         "mla_q_norm_g", "mla_kv_norm_g", "ln1_g", "ln1_b", "ln2_g", "ln2_b")
ORDER = ("ln_in_g", "ln_in_b", "w_in", "hgrn_lb_logits", "hgrn_norm_g", "sgu_ln_g", "sgu_ln_b", "sgu_w_s", "sgu_b_s",
         "mla_q_norm_g", "mla_w_uq", "mla_kv_norm_g", "mla_w_ukv", "w_out", "ln1_g", "ln1_b", "w_gate_up", "w_down",
         "ple_w_gate", "ple_w_proj", "ln2_g", "ln2_b")


def _slab(a, align):
    s = a.reshape(-1, LANES)
    pad = -s.shape[0] % align
    return jnp.pad(s, ((0, pad), (0, 0))) if pad else s


def _pack(arrays, align=16, total_align=512):
    s = jnp.concatenate([_slab(a, align) for a in arrays], axis=0)
    pad = -s.shape[0] % total_align
    return jnp.pad(s, ((0, pad), (0, 0))) if pad else s


def _unpack(slab, shapes, align=16):
    out, r0 = [], 0
    for s in shapes:
        nr = math.prod(s) // LANES
        out.append(slab[r0:r0 + nr].reshape(s))
        r0 += nr + (-nr % align)
    return out


def _weight_shards(w, li):
    uq_pad = ((0, 0), (0, LANES - ATT_D))
    shards = {k: w[k][li] for k in BIG}
    shards["mla_w_uq"] = jnp.pad(shards["mla_w_uq"], uq_pad)
    return {k: s.astype(BF16) for k, s in shards.items()}


def _usable_weights(g, *, name):
    out = {}
    for k, a in g.items():
        if k == "w_in":
            out[k] = _place_w_in(a, name=name + "_place_w_in")
        elif k in ("w_out", "w_down", "ple_w_gate"):
            out[k] = a.reshape(a.shape[0] * a.shape[1], a.shape[2])
        else:
            out[k] = a
    return out


BY_COLUMNS = ("mla_w_uq", "mla_w_ukv", "ple_w_proj")


def _as_pairs(k, g):
    if k in BY_COLUMNS:
        return g
    if g.ndim == 2:
        return g.reshape((4, 2, g.shape[0] // N_DEV) + g.shape[1:])
    return g.reshape((4, 2) + g.shape[1:])


def _twice(fn):
    return lambda *a: fn(*a) * 2


def _layer_forward(li, h, hb, p_i, wts, sm, lbs, tables, alpha, hgrn_job=None, after_hgrn=None, attn_job=None,
                   after_attn=None, loss_target=None):
    n = f"l{li}_"
    row1 = lambda a: a.reshape(1, -1)
    projp = _mm(hb, wts["w_in"], name=n + "proj")
    ng = row1(sm["hgrn_norm_g"][li])
    res = _hgrn_fwd(projp, lbs[li], ng, name=n + "hgrn_fwd", job=hgrn_job)
    if hgrn_job is not None:
        res, got = res
    o_a, o_pre, states = res
    lg, lbias = row1(sm["sgu_ln_g"][li]), row1(sm["sgu_ln_b"][li])
    w_s = sm["sgu_w_s"][li]
    bias_full = jnp.repeat(sm["sgu_b_s"][li].T, HEAD, axis=1)
    o_b = _sgu_fwd(projp, lg, lbias, w_s, bias_full, name=n + "sgu_fwd",
                   job=None if hgrn_job is None else _forward_job(got))
    if hgrn_job is not None:
        o_b, got = o_b
        wts = dict(wts, **after_hgrn(got))
    qg, kvg = row1(sm["mla_q_norm_g"][li]), row1(sm["mla_kv_norm_g"][li])
    cq_view, ckv_view = (projp, 384, P_CQ // 384), (projp, 256, P_CKV // 256)
    (cqn,) = _rowwise(_fn_rms, [cq_view], [qg], [(384, BF16)], name=n + "q_norm")
    (ckvn,) = _rowwise(_fn_rms, [ckv_view], [kvg], [(256, BF16)], name=n + "kv_norm")
    q = _mm(cqn, wts["mla_w_uq"], name=n + "uq")
    kv = _mm(ckvn, wts["mla_w_ukv"], name=n + "ukv")
    qr, kf, kvb = _mla_prep(q, kv, projp, tables, name=n + "mla_prep")
    res = _attn_fwd(qr, kf, kvb, name=n + "attn_fwd", job=attn_job)
    if attn_job is not None:
        res, got = res
    o_c, lse = res
    cat = jnp.concatenate([o_a, o_b, o_c.astype(BF16)], axis=1)
    mix = _mm(cat, wts["w_out"], name=n + "out_proj", job=None if attn_job is None else _forward_job(got))
    if attn_job is not None:
        mix, got = mix
        wts = dict(wts, **after_attn(got))
    g1, b1 = row1(sm["ln1_g"][li]), row1(sm["ln1_b"][li])
    d = h.shape[1]
    h1, h1b = _rowwise(_twice(_make_post_mix(alpha)), [h, mix], [g1, b1], [(d, F32), (d, BF16)], name=n + "ln1")
    gu, act = _gate_up_swiglu(h1b, wts["w_gate_up"], name=n + "gate_up")
    ffn = _mm_kblocks(act, wts["w_down"], bm="kn", tm=1024, name=n + "down")
    pg = _mm(h1b, wts["ple_w_gate"], name=n + "ple_gate")
    pp = _mm(p_i, wts["ple_w_proj"], name=n + "ple_proj")
    g2, b2 = row1(sm["ln2_g"][li]), row1(sm["ln2_b"][li])
    if loss_target is None:
        out = _rowwise(_twice(_make_ple_ln(alpha)), [h1, ffn, pg, pp], [g2, b2], [(d, F32), (d, BF16)],
                       name=n + "ln2")
    else:
        def ln_and_loss(h1v, ffnv, pgv, ppv, tv, gv, bv):
            err = _make_ple_ln(alpha)(h1v, ffnv, pgv, ppv, gv, bv)[0] - tv
            return err * (1.0 / d), 0.5 * jnp.sum(jnp.mean(err * err, axis=-1, keepdims=True), axis=0, keepdims=True)

        out = _rowwise(ln_and_loss, [h1, ffn, pg, pp, loss_target], [g2, b2], [(d, F32)], accs=[(1, 1)],
                       name=n + "ln2_loss")
    saved = dict(h=h, hb=hb, h1b=h1b, projp=projp, o_pre=o_pre, states=states, cqn=cqn, ckvn=ckvn, qr=qr, kf=kf, kvb=kvb, o_c=o_c,
                 lse=lse, cat=cat, mix=mix, h1=h1, gu=gu, act=act, ffn=ffn, pg=pg, pp=pp, ng=ng, lg=lg, wts=wts,
                 lbias=lbias, w_s=w_s, bias_full=bias_full, qg=qg, kvg=kvg, g1=g1, b1=b1, g2=g2, b2=b2)
    return tuple(out), saved


RS_EARLY = ("ple_w_proj", "ple_w_gate", "w_down", "w_gate_up", "w_out")
RS_LATE = ("mla_w_uq", "mla_w_ukv", "w_in")


def _layer_backward(li, dh2_parts, p_i, sv, lbs, tables, alpha, core, carried=None):
    n = f"l{li}_b_"
    wts = sv["wts"]
    gr = {}
    dh1_a, dffn, dpg, dpp, gr["ln2_g"], gr["ln2_b"] = _rowwise_vjp(
        _make_ple_ln(alpha), [sv["h1"], sv["ffn"], sv["pg"], sv["pp"]], [sv["g2"], sv["b2"]], [dh2_parts],
        groups=[[0], [1], [2], [3]], gdtypes=[F32, BF16, BF16, BF16], name=n + "ln2")
    big = {}
    big["ple_w_proj"] = _mm(p_i, dpp, am="km", tk=2048, name=n + "ple_proj_dw")
    big["ple_w_gate"] = _mm(sv["h1b"], dpg, am="km", name=n + "ple_gate_dw")
    dh1_b = _mm(dpg, wts["ple_w_gate"], bm="nk", name=n + "ple_gate_dx")
    big["w_down"] = _mm(sv["act"], dffn, am="bkm", tk=4096, name=n + "down_dw")
    dgu = _down_dx_swiglu(dffn, wts["w_down"], sv["gu"], name=n + "down_dx")
    dgu = dgu.reshape((N_DEV,) + dgu.shape[2:])
    big["w_gate_up"], carried_got = _mm(sv["h1b"], dgu, am="km", bm="bkn", om="bmn", tk=4096, name=n + "gate_up_dw",
                                        job=carried), None
    if carried is not None:
        big["w_gate_up"], carried_got = big["w_gate_up"]
    early = [_as_pairs(k, big[k]) for k in RS_EARLY[:-1]]
    dh1_c, theirs = _mm_kblocks(dgu, wts["w_gate_up"], bm="bnk", tm=512, name=n + "gate_up_dx",
                                job=_pair_job(early))
    dh_a, dmix, gr["ln1_g"], gr["ln1_b"] = _rowwise_vjp(
        _make_post_mix(alpha), [sv["h"], sv["mix"]], [sv["g1"], sv["b1"]], [[dh1_a, dh1_b, dh1_c]],
        groups=[[0], [1]], gdtypes=[F32, BF16], name=n + "ln1")
    big["w_out"] = _mm(sv["cat"], dmix, am="km", name=n + "out_proj_dw")
    early.append(_as_pairs("w_out", big["w_out"]))
    dcat, their_w_out = _mm(dmix, wts["w_out"], bm="nk", name=n + "out_proj_dx", job=_pair_job(early[-1:]))
    sums = [_pair_add(x, r, core, name=n + "pair_add_" + k)
            for k, x, r in zip(RS_EARLY, early, list(theirs) + list(their_w_out))]

    (dqr, dkv, dkf), early_quads = _attn_bwd(sv["qr"], sv["kf"], sv["kvb"], dcat, sv["o_c"], sv["lse"],
                                             name=n + "attn", job=_quad_job(sums))
    dqpad, dkr = _mla_prep_bwd(dqr, dkf, tables, name=n + "mla_prep")
    big["mla_w_uq"] = _mm(sv["cqn"], dqpad, am="km", tk=2048, name=n + "uq_dw")
    dcqn = _mm(dqpad, wts["mla_w_uq"], bm="nk", name=n + "uq_dx")
    big["mla_w_ukv"] = _mm(sv["ckvn"], dkv, am="km", tk=2048, name=n + "ukv_dw")
    dckvn = _mm(dkv, wts["mla_w_ukv"], bm="nk", name=n + "ukv_dx")
    projp = sv["projp"]
    dcq, gr["mla_q_norm_g"] = _rowwise_vjp(_fn_rms, [(projp, 384, P_CQ // 384)], [sv["qg"]], [[dcqn]],
                                           groups=[[0]], gdtypes=[BF16], name=n + "q_norm")
    dckv, gr["mla_kv_norm_g"] = _rowwise_vjp(_fn_rms, [(projp, 256, P_CKV // 256)], [sv["kvg"]], [[dckvn]],
                                             groups=[[0]], gdtypes=[BF16], name=n + "kv_norm")
    dsgu, gr["sgu_ln_g"], gr["sgu_ln_b"], gr["sgu_w_s"], gr["sgu_b_s"] = _sgu_bwd(
        projp, sv["lg"], sv["lbias"], sv["w_s"], sv["bias_full"], dcat, name=n + "sgu")
    dhg, gr["hgrn_norm_g"], gr["lower_bound"] = _hgrn_bwd(
        projp, lbs[li], sv["ng"], sv["o_pre"], sv["states"], dcat, name=n + "hgrn")
    dprojp = jnp.concatenate([dhg, dsgu, dcq, dkr, dckv], axis=1)
    big["w_in"] = _unplace_w_in(_mm(sv["hb"], dprojp, am="km", tk=4096, name=n + "proj_dw"),
                                name=n + "proj_dw_shards")
    late = [_as_pairs(k, big[k]) for k in RS_LATE]
    dh_b, theirs = _mm(dprojp, wts["w_in"], bm="nk", tk=P_COLS, name=n + "proj_dx", job=_pair_job(late))
    late_sums = [_pair_add(x, r, core, name=n + "pair_add_" + k) for k, x, r in zip(RS_LATE, late, theirs)]
    return [dh_a, dh_b], gr, early_quads, late_sums, carried_got


def kernel(x, p, positions, ln_in_g, ln_in_b, w_in, hgrn_lb_logits, hgrn_norm_g, sgu_ln_g, sgu_ln_b, sgu_w_s, sgu_b_s, mla_q_norm_g, mla_w_uq, mla_kv_norm_g, mla_w_ukv, w_out, ln1_g, ln1_b, w_gate_up, w_down, ple_w_gate, ple_w_proj, ln2_g, ln2_b, loss_target, m_ln_in_g, m_ln_in_b, m_w_in, m_hgrn_lb_logits, m_hgrn_norm_g, m_sgu_ln_g, m_sgu_ln_b, m_sgu_w_s, m_sgu_b_s, m_mla_q_norm_g, m_mla_w_uq, m_mla_kv_norm_g, m_mla_w_ukv, m_w_out, m_ln1_g, m_ln1_b, m_w_gate_up, m_w_down, m_ple_w_gate, m_ple_w_proj, m_ln2_g, m_ln2_b, v_ln_in_g, v_ln_in_b, v_w_in, v_hgrn_lb_logits, v_hgrn_norm_g, v_sgu_ln_g, v_sgu_ln_b, v_sgu_w_s, v_sgu_b_s, v_mla_q_norm_g, v_mla_w_uq, v_mla_kv_norm_g, v_mla_w_ukv, v_w_out, v_ln1_g, v_ln1_b, v_w_gate_up, v_w_down, v_ple_w_gate, v_ple_w_proj, v_ln2_g, v_ln2_b):
    args = dict(locals())
    w = {k: args[k] for k in ORDER}
    m = {k: args["m_" + k] for k in ORDER}
    v = {k: args["v_" + k] for k in ORDER}
    depth = w_in.shape[0]
    assert depth == 2, "the lower-bound kernel is written for two layers"
    alpha = (2 * depth) ** 0.25
    xs, tgt = x[0], loss_target[0]
    d_model = xs.shape[1]

    shards = [_weight_shards(w, li) for li in range(depth)]
    on_hgrn0 = ("mla_w_uq", "mla_w_ukv", "w_out", "ple_w_gate", "ple_w_proj")
    ffn0 = ("w_gate_up", "w_down")
    first1 = ("w_in", "mla_w_uq", "mla_w_ukv", "w_out")
    on_attn1 = ("w_gate_up", "w_down", "ple_w_gate", "ple_w_proj")
    layer1_first = {}

    def after_hgrn0(got):
        return _usable_weights(dict(zip(on_hgrn0, got)), name="l0")

    def after_attn0(got):
        layer1_first.update(_usable_weights(dict(zip(first1, got[len(ffn0):])), name="l1"))
        return _usable_weights(dict(zip(ffn0, got[:len(ffn0)])), name="l0")

    def after_attn1(got):
        return _usable_weights(dict(zip(on_attn1, got)), name="l1")

    tables = _rope_tables(positions[0])
    row1 = lambda a: a.reshape(1, -1)
    l0, l1 = row1(hgrn_lb_logits[0]), row1(hgrn_lb_logits[1])
    lbs = _rowwise(_fn_lower_bounds, [l0, l1], [], [(HG_W, F32), (HG_W, F32)], name="lower_bounds")

    gin, bin_ = row1(ln_in_g), row1(ln_in_b)
    (h, hb), g_in = _rowwise(_twice(_fn_ln), [xs], [gin, bin_], [(d_model, F32), (d_model, BF16)], name="ln_in",
                             job=_gather_job([shards[0]["w_in"]]))
    w_in0 = _usable_weights({"w_in": _gather_forward(g_in, name="gather_l0_w_in_forward")[0]}, name="l0")
    (h, hb), sv0 = _layer_forward(
        0, h, hb, p[0, 0], w_in0, w, lbs, tables, alpha,
        hgrn_job=_gather_job([shards[0][k] for k in on_hgrn0]), after_hgrn=after_hgrn0,
        attn_job=_gather_job([shards[0][k] for k in ffn0] + [shards[1][k] for k in first1]), after_attn=after_attn0)
    (dy, loss_local), sv1 = _layer_forward(
        1, h, hb, p[1, 0], layer1_first, w, lbs, tables, alpha,
        attn_job=_gather_job([shards[1][k] for k in on_attn1]), after_attn=after_attn1, loss_target=tgt)
    saved = [sv0, sv1]
    loss = lax.psum(loss_local[0, 0], ("x", "y", "c"))

    core = lax.axis_index("c").astype(jnp.int32).reshape(1)
    dparts, grads, quads, carried = [dy], [None] * depth, [None] * depth, None
    for li in reversed(range(depth)):
        dparts, grads[li], early_quads, late_sums, late_quads = _layer_backward(
            li, dparts, p[li, 0], saved[li], lbs, tables, alpha, core, carried=carried)
        quads[li] = dict(zip(RS_EARLY, early_quads))
        if carried is not None:
            quads[li + 1].update(zip(RS_LATE, late_quads))
        carried = _quad_job(late_sums)
    (dx, d_gin, d_bin), late_quads = _rowwise_vjp(_fn_ln, [xs], [gin, bin_], [dparts], groups=[[0]], name="ln_in_b",
                                                   job=carried)
    quads[0].update(zip(RS_LATE, late_quads))
    dl0, dl1 = _rowwise_vjp(_fn_lower_bounds, [l0, l1], [], [[grads[0]["lower_bound"]], [grads[1]["lower_bound"]]],
                            groups=[[0], [1]], name="lower_bounds_b")

    prefixes = ("grad_", "delta_", "new_m_", "new_v_")
    uq_pad = ((0, 0), (0, 0), (0, LANES - ATT_D))
    state = {k: ((jnp.pad(w[k], uq_pad), jnp.pad(m[k], uq_pad), jnp.pad(v[k], uq_pad)) if k == "mla_w_uq"
                 else (w[k], m[k], v[k])) for k in BIG}
    out = {}
    for k in BIG:
        res4 = None
        for li in range(depth):
            res4 = _adamw(quads[li][k], *state[k], li, name=f"adamw_l{li}_{k}", into=res4)
        for pre, a in zip(prefixes, res4):
            out[pre + k] = a[:, :, :ATT_D] if k == "mla_w_uq" else a

    small_g = {"ln_in_g": d_gin.reshape(-1), "ln_in_b": d_bin.reshape(-1),
               "hgrn_lb_logits": jnp.stack([dl0.reshape(-1), dl1.reshape(-1)])}
    for k in SMALL[3:]:
        small_g[k] = jnp.stack([grads[li][k].reshape(w[k].shape[1:]) for li in range(depth)])
    (small_parts,) = _all_gather([_pack([small_g[k] for k in SMALL])], name="gather_small_grads", columns=False)
    slabs = _adamw(small_parts, _pack([w[k] for k in SMALL]), _pack([m[k] for k in SMALL]),
                   _pack([v[k] for k in SMALL]), None, name="adamw_small")
    shapes = [w[k].shape for k in SMALL]
    for pre, slab in zip(prefixes, slabs):
        for k, a in zip(SMALL, _unpack(slab, shapes)):
            out[pre + k] = a
    res = [loss, dx[None]]
    for prefix in ("grad_", "delta_", "new_m_", "new_v_"):
        res += [out[prefix + k] for k in ORDER]
    return tuple(res)
```

```python
import functools
import math

import jax
import jax.numpy as jnp
import numpy as np
from jax import lax
from jax.experimental import pallas as pl
from jax.experimental.pallas import tpu as pltpu

F32 = jnp.float32
BF16 = jnp.bfloat16
MESH = pl.DeviceIdType.MESH

LN_EPS = 1e-5
RMS_EPS = 1e-6
ROPE_THETA = 10000.0
ADAM_LR, ADAM_B1, ADAM_B2, ADAM_EPS, ADAM_WD, ADAM_STEP = 0.001, 0.9, 0.999, 1e-08, 0.01, 10

N_DEV = 8
LANES = 128
HG_CHUNK = 16
HG_W = 256
HEAD = 64
SGU_CHUNK = 128
N_ATT_HEADS = 8
ATT_D = 96
VMEM_LIMIT = 56 * 1024 * 1024

HG_TILE = 256
ATT_TQ = 512
ROW_TILE = 512

P_CQ, P_KR, P_CKV, P_COLS = 1536, 1920, 2048, 2304


def _cparams(sem):
    return pltpu.CompilerParams(dimension_semantics=sem, vmem_limit_bytes=VMEM_LIMIT)


_ANY = pl.BlockSpec(memory_space=pl.ANY)


def _call(body, operands, *, name, grid, in_specs, out_specs, out_shape, sem, scratch_shapes=(), job=None):
    if job is None:
        return pl.pallas_call(body, name=name, grid=grid, in_specs=in_specs, out_specs=out_specs, out_shape=out_shape,
                              scratch_shapes=list(scratch_shapes), compiler_params=_cparams(sem))(*operands)
    single = not isinstance(out_shape, (list, tuple))
    shapes = [out_shape] if single else list(out_shape)
    ospecs = [out_specs] if single else list(out_specs)
    ni, no, ns = len(operands), len(shapes), len(scratch_shapes)
    ji, jo = len(job.inputs), len(job.out_shapes)

    def hosted(*refs):
        p = 0
        parts = []
        for cnt in (ni, ji, no, jo, ns):
            parts.append(refs[p:p + cnt])
            p += cnt
        ins, jins, outs, jouts, scr = parts
        jsems = refs[p:]
        ids = [pl.program_id(a) for a in range(len(grid))]
        first = functools.reduce(lambda a, b: a & b, [i == 0 for i in ids])
        last = functools.reduce(lambda a, b: a & b, [i == g - 1 for i, g in zip(ids, grid)])

        @pl.when(first)
        def _():
            job.start(jins, jouts, jsems)

        body(*ins, *outs, *scr)

        @pl.when(last)
        def _():
            job.finish(jins, jouts, jsems)

    res = pl.pallas_call(
        hosted, name=name, grid=grid,
        in_specs=list(in_specs) + [_ANY] * ji, out_specs=ospecs + [_ANY] * jo,
        out_shape=shapes + list(job.out_shapes),
        scratch_shapes=list(scratch_shapes) + [pltpu.SemaphoreType.DMA((c,)) for c in job.sem_counts],
        input_output_aliases=job.aliases(ni, no),
        compiler_params=_cparams(("arbitrary",) * len(grid)),
    )(*operands, *job.inputs)
    own = res[0] if single else res[:no]
    return own, res[no:]


class _Job:
    def __init__(self, inputs, out_shapes, sem_counts, start, finish, in_place=False):
        self.inputs, self.out_shapes, self.sem_counts = list(inputs), list(out_shapes), list(sem_counts)
        self.start, self.finish, self.in_place = start, finish, in_place

    def aliases(self, first_in, first_out):
        return {first_in + i: first_out + i for i in range(len(self.inputs))} if self.in_place else {}


def _copies_job(inputs, out_shapes, n_remote, n_local, make, in_place=False):
    def start(jins, jouts, sems):
        sends, _, local = make(jins, jouts, *sems)
        for cp in local + sends:
            cp.start()

    def finish(jins, jouts, sems):
        sends, recvs, local = make(jins, jouts, *sems)
        for cp in recvs:
            cp.wait_recv()
        for cp in sends:
            cp.wait_send()
        for cp in local:
            cp.wait()

    return _Job(inputs, out_shapes, [n_remote, n_remote, max(n_local, 1)], start, finish, in_place)


def _run_job(job, *, name):
    ji, jo = len(job.inputs), len(job.out_shapes)

    def body(*refs):
        jins, jouts, sems = refs[:ji], refs[ji:ji + jo], refs[ji + jo:]
        job.start(jins, jouts, sems)
        job.finish(jins, jouts, sems)

    return pl.pallas_call(
        body, name=name, out_shape=list(job.out_shapes), in_specs=[_ANY] * ji, out_specs=[_ANY] * jo,
        scratch_shapes=[pltpu.SemaphoreType.DMA((c,)) for c in job.sem_counts],
        input_output_aliases=job.aliases(0, 0),
    )(*job.inputs)


def _tile(n, pref):
    if n % pref == 0:
        return pref
    best = None
    t = LANES
    while t <= min(n, pref):
        if n % t == 0:
            best = t
        t += LANES
    return best if best is not None else n


def _mm(a, b, *, am="mk", bm="kn", om="mn", out_dtype=F32, tm=1024, tn=1024, tk=1024, name, job=None):
    if am == "mk":
        m, k = a.shape
    elif am == "km":
        k, m = a.shape
    elif am == "bmk":
        m, tk = a.shape[1], a.shape[2]
        k = a.shape[0] * tk
    else:
        k, tm = a.shape[1], a.shape[2]
        m = a.shape[0] * tm
    if bm == "kn":
        kb_, n = b.shape
    elif bm == "nk":
        n, kb_ = b.shape
    elif bm == "bkn":
        kb_, tn = b.shape[1], b.shape[2]
        n = b.shape[0] * tn
    else:
        n, tk = b.shape[1], b.shape[2]
        kb_ = b.shape[0] * tk
    assert kb_ == k, (a.shape, b.shape, am, bm)
    tm, tn, tk = _tile(m, tm), _tile(n, tn), _tile(k, tk)
    nk = k // tk
    dims = (((0 if am in ("km", "bkm") else 1,), (1 if bm in ("nk", "bnk") else 0,)), ((), ()))

    a_spec = {"mk": pl.BlockSpec((tm, tk), lambda i, j, kk: (i, kk)),
              "km": pl.BlockSpec((tk, tm), lambda i, j, kk: (kk, i)),
              "bmk": pl.BlockSpec((None, tm, tk), lambda i, j, kk: (kk, i, 0)),
              "bkm": pl.BlockSpec((None, tk, tm), lambda i, j, kk: (i, kk, 0))}[am]
    b_spec = {"kn": pl.BlockSpec((tk, tn), lambda i, j, kk: (kk, j)),
              "nk": pl.BlockSpec((tn, tk), lambda i, j, kk: (j, kk)),
              "bkn": pl.BlockSpec((None, tk, tn), lambda i, j, kk: (j, kk, 0)),
              "bnk": pl.BlockSpec((None, tn, tk), lambda i, j, kk: (kk, j, 0))}[bm]
    if om == "mn":
        o_spec, o_shape = pl.BlockSpec((tm, tn), lambda i, j, kk: (i, j)), (m, n)
    else:
        o_spec, o_shape = pl.BlockSpec((None, tm, tn), lambda i, j, kk: (j, i, 0)), (n // tn, m, tn)

    def body(a_ref, b_ref, o_ref, *acc):
        kk = pl.program_id(2)

        def prod():
            return lax.dot_general(a_ref[...].astype(BF16), b_ref[...].astype(BF16), dims, preferred_element_type=F32)

        if nk == 1:
            o_ref[...] = prod().astype(o_ref.dtype)
            return
        acc_ref, = acc

        @pl.when(kk == 0)
        def _():
            acc_ref[...] = prod()

        if nk > 2:
            @pl.when((kk > 0) & (kk < nk - 1))
            def _():
                acc_ref[...] += prod()

        @pl.when(kk == nk - 1)
        def _():
            o_ref[...] = (acc_ref[...] + prod()).astype(o_ref.dtype)

    return _call(body, (a, b), name=name, grid=(m // tm, n // tn, nk), in_specs=[a_spec, b_spec], out_specs=o_spec,
                 out_shape=jax.ShapeDtypeStruct(o_shape, out_dtype),
                 scratch_shapes=[pltpu.VMEM((tm, tn), F32)] if nk > 1 else [],
                 sem=("parallel", "parallel", "arbitrary"), job=job)


def _mm_kblocks(a, b, *, bm, tm, name, job=None):
    nkb, m, kb = a.shape
    n = b.shape[1]
    tm = _tile(m, tm)

    def body(a_ref, b_ref, o_ref):
        acc = None
        for j in range(nkb):
            if bm == "kn":
                part = jnp.dot(a_ref[j], b_ref[j * kb:(j + 1) * kb, :], preferred_element_type=F32)
            else:
                part = lax.dot_general(a_ref[j], b_ref[j], _NT, preferred_element_type=F32)
            acc = part if acc is None else acc + part
        o_ref[...] = acc

    b_spec = (pl.BlockSpec(b.shape, lambda i: (0, 0)) if bm == "kn" else pl.BlockSpec(b.shape, lambda i: (0, 0, 0)))
    return _call(body, (a, b), name=name, grid=(m // tm,),
                 in_specs=[pl.BlockSpec((nkb, tm, kb), lambda i: (0, i, 0)), b_spec],
                 out_specs=pl.BlockSpec((tm, n), lambda i: (i, 0)), out_shape=jax.ShapeDtypeStruct((m, n), F32),
                 sem=("parallel",), job=job)


def _row_operand(a, tile):
    if isinstance(a, tuple):
        arr, w, j = a
        return arr, pl.BlockSpec((tile, w), lambda i, j=j: (i, j))
    return a, pl.BlockSpec((tile, a.shape[1]), lambda i: (i, 0))


def _const_spec(c):
    nd = c.ndim
    return pl.BlockSpec(c.shape, lambda i, nd=nd: (0,) * nd)


def _rowwise(fn, rows, consts, outs, *, name, accs=(), tile=None, job=None):
    t_rows = (rows[0][0] if isinstance(rows[0], tuple) else rows[0]).shape[0]
    tile = min(tile or ROW_TILE, t_rows)
    arrs, specs = zip(*[_row_operand(a, tile) for a in rows])
    nin, no = len(rows) + len(consts), len(outs)

    def body(*refs):
        res = fn(*[r[...] for r in refs[:nin]])
        for r, v in zip(refs[nin:nin + no], res[:no]):
            r[...] = v.astype(r.dtype)
        if accs:
            a_refs = refs[nin + no:]

            @pl.when(pl.program_id(0) == 0)
            def _():
                for r in a_refs:
                    r[...] = jnp.zeros_like(r)

            for r, v in zip(a_refs, res[no:]):
                r[...] += v

    out_shape = [jax.ShapeDtypeStruct((t_rows, w), dt) for w, dt in outs]
    out_shape += [jax.ShapeDtypeStruct(s, F32) for s in accs]
    out_specs = [pl.BlockSpec((tile, w), lambda i: (i, 0)) for w, _ in outs]
    out_specs += [pl.BlockSpec(s, lambda i, nd=len(s): (0,) * nd) for s in accs]
    return _call(body, (*arrs, *consts), name=name, grid=(t_rows // tile,),
                 in_specs=list(specs) + [_const_spec(c) for c in consts],
                 out_specs=out_specs, out_shape=out_shape, sem=("arbitrary",), job=job)


def _rowwise_vjp(fn, rows, consts, cts, *, name, groups, tile=None, gdtypes=None, job=None):
    t_rows = (rows[0][0] if isinstance(rows[0], tuple) else rows[0]).shape[0]
    tile = min(tile or ROW_TILE, t_rows)
    arrs, specs = zip(*[_row_operand(a, tile) for a in rows])
    flat_cts = [c for group in cts for c in group]
    ct_arrs, ct_specs = zip(*[_row_operand(a, tile) for a in flat_cts])
    nr, nc, nct, ng = len(rows), len(consts), len(flat_cts), len(groups)

    def width(a):
        return a[1] if isinstance(a, tuple) else a.shape[1]

    def body(*refs):
        rv = [r[...].astype(F32) for r in refs[:nr]]
        cv = [r[...] for r in refs[nr:nr + nc]]
        ct_refs = refs[nr + nc:nr + nc + nct]
        ctv, pos = [], 0
        for group in cts:
            s = ct_refs[pos][...].astype(F32)
            for r in ct_refs[pos + 1:pos + len(group)]:
                s = s + r[...].astype(F32)
            ctv.append(s)
            pos += len(group)
        _, pull = jax.vjp(fn, *rv, *cv)
        grads = pull(tuple(ctv))
        g_refs = refs[nr + nc + nct:nr + nc + nct + ng]
        for r, idx in zip(g_refs, groups):
            parts = [grads[i] for i in idx]
            r[...] = (parts[0] if len(parts) == 1 else jnp.concatenate(parts, axis=1)).astype(r.dtype)
        c_refs = refs[nr + nc + nct + ng:]

        @pl.when(pl.program_id(0) == 0)
        def _():
            for r in c_refs:
                r[...] = jnp.zeros_like(r)

        for r, v in zip(c_refs, grads[nr:]):
            r[...] += v

    gw = [sum(width(rows[i]) for i in idx) for idx in groups]
    gdtypes = gdtypes or [F32] * ng
    out_shape = [jax.ShapeDtypeStruct((t_rows, w), dt) for w, dt in zip(gw, gdtypes)]
    out_shape += [jax.ShapeDtypeStruct(c.shape, F32) for c in consts]
    out_specs = [pl.BlockSpec((tile, w), lambda i: (i, 0)) for w in gw]
    out_specs += [_const_spec(c) for c in consts]
    return _call(body, (*arrs, *consts, *ct_arrs), name=name, grid=(t_rows // tile,),
                 in_specs=list(specs) + [_const_spec(c) for c in consts] + list(ct_specs),
                 out_specs=out_specs, out_shape=out_shape, sem=("arbitrary",), job=job)


def _layer_norm(x, g, b):
    mu = jnp.mean(x, axis=-1, keepdims=True)
    xc = x - mu
    var = jnp.mean(xc * xc, axis=-1, keepdims=True)
    return xc * lax.rsqrt(var + LN_EPS) * g + b


def _sigmoid(x):
    return 1.0 / (1.0 + jnp.exp(-x))


def _fn_ln(x, g, b):
    return (_layer_norm(x, g, b),)


def _fn_rms(x, g):
    return (x * lax.rsqrt(jnp.mean(x * x, axis=-1, keepdims=True) + RMS_EPS) * g,)


def _make_post_mix(alpha):
    def fn(h, mix, g, b):
        return (_layer_norm(alpha * h + mix, g, b),)
    return fn


def _make_ple_ln(alpha):
    def fn(h1, ffn, pg, pp, g, b):
        return (_layer_norm(alpha * h1 + ffn + _sigmoid(pg) * pp, g, b),)
    return fn


def _fn_lower_bounds(l0, l1):
    m = jnp.maximum(l0, l1)
    e0, e1 = jnp.exp(l0 - m), jnp.exp(l1 - m)
    s = e0 + e1
    p0, p1 = e0 / s, e1 / s
    return (p0 - p0, (p0 + p1) - p0)


def _split_dot(x, e_bf16):
    hi = x.astype(BF16)
    lo = (x - hi.astype(F32)).astype(BF16)
    return (jnp.dot(hi, e_bf16, preferred_element_type=F32) + jnp.dot(lo, e_bf16, preferred_element_type=F32))


def _hgrn_common(th):
    rm = lax.broadcasted_iota(jnp.int32, (th, HG_W), 0) % HG_CHUNK

    def seg_cumsum(x):
        for s in (1, 2, 4, 8):
            x = x + jnp.where(rm >= s, pltpu.roll(x, s, 0), 0.0)
        return x

    def seg_rcumsum(x):
        for s in (1, 2, 4, 8):
            x = x + jnp.where(rm < HG_CHUNK - s, pltpu.roll(x, th - s, 0), 0.0)
        return x

    ri = lax.broadcasted_iota(jnp.int32, (HG_W, HG_W), 0) // HEAD
    ci = lax.broadcasted_iota(jnp.int32, (HG_W, HG_W), 1) // HEAD
    head_f32 = (ri == ci).astype(F32)
    head_bf16 = head_f32.astype(BF16)

    def headsum(x, pieces=2):
        if pieces == 1:
            return jnp.dot(x.astype(BF16), head_bf16, preferred_element_type=F32)
        return _split_dot(x, head_bf16)

    return rm, seg_cumsum, seg_rcumsum, head_f32, headsum


def _hgrn_gates(qr, fl, lb):
    sg = _sigmoid(fl)
    f = lb + (1.0 - lb) * sg
    sq = _sigmoid(qr)
    return sg, f, jnp.log(f), 1.0 - f, qr * sq, sq


def _shifted(x, d, th):
    return x if d == 0 else pltpu.roll(x, d, 0)


def _unshift(x, d, th):
    return x if d == 0 else pltpu.roll(x, th - d, 0)


def _hgrn_fwd(projp, lb, ng, *, name, job=None):
    t_rows = projp.shape[0]
    th = min(HG_TILE, t_rows)
    nct = th // HG_CHUNK

    def body(q_ref, f_ref, i_ref, g_ref, lb_ref, ng_ref, oa_ref, opre_ref, st_out_ref,
             st_ref, vtm_ref, kv_ref, qe_ref, dec_ref, oint_ref):
        rm, seg_cumsum, seg_rcumsum, head_f32, headsum = _hgrn_common(th)

        @pl.when(pl.program_id(0) == 0)
        def _():
            st_ref[...] = jnp.zeros_like(st_ref)

        qr, fl, v, g = q_ref[...], f_ref[...], i_ref[...], g_ref[...]
        _, f, lf, k, q, _ = _hgrn_gates(qr, fl, lb_ref[...])
        b = seg_cumsum(lf)

        o = jnp.zeros((th, HG_W), F32)
        for d in range(HG_CHUNK):
            kd, bd, vd = _shifted(k, d, th), _shifted(b, d, th), _shifted(v, d, th)
            e = jnp.exp(jnp.where(rm >= d, b - bd, -1e30))
            o = o + headsum(q * kd * e, 1) * vd

        blast = seg_rcumsum(jnp.where(rm == HG_CHUNK - 1, b, 0.0))
        kte = (k * jnp.exp(blast - b)).astype(BF16)
        qe_ref[...] = q * jnp.exp(b)
        dec_ref[...] = jnp.exp(blast)
        vt = v.T
        lane_chunk = lax.broadcasted_iota(jnp.int32, (HG_W, th), 1) // HG_CHUNK
        for c in range(nct):
            vtm_ref[c * HG_W:(c + 1) * HG_W, :] = jnp.where(lane_chunk == c, vt, 0.0).astype(BF16)
        kv_ref[...] = jnp.dot(vtm_ref[...], kte, preferred_element_type=F32)

        s = st_ref[...]
        for c in range(nct):
            rows = slice(c * HG_CHUNK, (c + 1) * HG_CHUNK)
            st_out_ref[c] = s
            oint_ref[rows, :] = lax.dot_general(qe_ref[rows, :].astype(BF16), s.astype(BF16),
                                                (((1,), (1,)), ((), ())), preferred_element_type=F32)
            dec = jnp.max(dec_ref[rows, :], axis=0, keepdims=True)
            s = s * dec + kv_ref[c * HG_W:(c + 1) * HG_W, :] * head_f32
        st_ref[...] = s

        o = o + oint_ref[...]
        opre_ref[...] = o
        r = lax.rsqrt(headsum(o * o) * (1.0 / HEAD) + RMS_EPS)
        oa_ref[...] = (o * r * ng_ref[...] * (g * _sigmoid(g))).astype(oa_ref.dtype)

    col = lambda j: pl.BlockSpec((th, HG_W), lambda i, j=j: (i, j))
    vec = pl.BlockSpec((1, HG_W), lambda i: (0, 0))
    row = pl.BlockSpec((th, HG_W), lambda i: (i, 0))
    n_chunks = t_rows // HG_CHUNK
    return _call(
        body, (projp, projp, projp, projp, lb, ng), name=name, grid=(t_rows // th,),
        in_specs=[col(0), col(1), col(2), col(3), vec, vec],
        out_specs=[row, row, pl.BlockSpec((nct, HG_W, HG_W), lambda i: (i, 0, 0))],
        out_shape=[jax.ShapeDtypeStruct((t_rows, HG_W), BF16), jax.ShapeDtypeStruct((t_rows, HG_W), F32),
                   jax.ShapeDtypeStruct((n_chunks, HG_W, HG_W), F32)],
        scratch_shapes=[pltpu.VMEM((HG_W, HG_W), F32), pltpu.VMEM((nct * HG_W, th), BF16),
                        pltpu.VMEM((nct * HG_W, HG_W), F32), pltpu.VMEM((th, HG_W), F32),
                        pltpu.VMEM((th, HG_W), F32), pltpu.VMEM((th, HG_W), F32)],
        sem=("arbitrary",), job=job)


def _hgrn_bwd(projp, lb, ng, opre, states, dcat, *, name):
    t_rows = projp.shape[0]
    th = min(HG_TILE, t_rows)
    nct = th // HG_CHUNK
    nt = t_rows // th

    def body(q_ref, f_ref, i_ref, g_ref, lb_ref, ng_ref, opre_ref, st_in_ref, do_ref,
             dproj_ref, dng_ref, dlb_ref,
             gst_ref, dotm_ref, qg_ref, v_ref, kte_ref, dop_ref, dec_ref, dkte_ref, dvi_ref, dqe_ref, ddec_ref):
        rm, seg_cumsum, seg_rcumsum, head_f32, headsum = _hgrn_common(th)

        @pl.when(pl.program_id(0) == 0)
        def _():
            gst_ref[...] = jnp.zeros_like(gst_ref)
            dng_ref[...] = jnp.zeros_like(dng_ref)
            dlb_ref[...] = jnp.zeros_like(dlb_ref)

        qr, fl, v, g = q_ref[...], f_ref[...], i_ref[...], g_ref[...]
        lb, ngv = lb_ref[...], ng_ref[...]
        sg, f, lf, k, q, sq = _hgrn_gates(qr, fl, lb)
        b = seg_cumsum(lf)
        blast = seg_rcumsum(jnp.where(rm == HG_CHUNK - 1, b, 0.0))
        eb = jnp.exp(b)
        ekb = jnp.exp(blast - b)
        qe, kte, dec = q * eb, k * ekb, jnp.exp(blast)

        do_out, op = do_ref[...], opre_ref[...]
        sgg = _sigmoid(g)
        sil = g * sgg
        r = lax.rsqrt(headsum(op * op) * (1.0 / HEAD) + RMS_EPS)
        on = op * r
        dng_ref[...] += jnp.sum(do_out * on * sil, axis=0, keepdims=True)
        dg = do_out * on * ngv * (sgg * (1.0 + g * (1.0 - sgg)))
        don = do_out * ngv * sil
        dop = r * (don - on * (headsum(don * on) * (1.0 / HEAD)))

        v_ref[...] = v
        kte_ref[...] = kte
        dop_ref[...] = dop
        dec_ref[...] = dec
        dot_t = dop.T
        lane_chunk = lax.broadcasted_iota(jnp.int32, (HG_W, th), 1) // HG_CHUNK
        for c in range(nct):
            dotm_ref[c * HG_W:(c + 1) * HG_W, :] = jnp.where(lane_chunk == c, dot_t, 0.0).astype(BF16)
        qg_ref[...] = jnp.dot(dotm_ref[...], qe.astype(BF16), preferred_element_type=F32)

        gs = gst_ref[...]
        for c in reversed(range(nct)):
            rows = slice(c * HG_CHUNK, (c + 1) * HG_CHUNK)
            s = st_in_ref[c]
            gm = (gs * head_f32).astype(BF16)
            dkte_ref[rows, :] = jnp.dot(v_ref[rows, :].astype(BF16), gm, preferred_element_type=F32)
            dvi_ref[rows, :] = lax.dot_general(kte_ref[rows, :].astype(BF16), gm, (((1,), (1,)), ((), ())),
                                               preferred_element_type=F32)
            dqe_ref[rows, :] = jnp.dot(dop_ref[rows, :].astype(BF16), s.astype(BF16), preferred_element_type=F32)
            ddec_ref[rows, :] = jnp.broadcast_to(jnp.sum(gs * s, axis=0, keepdims=True), (HG_CHUNK, HG_W))
            dec_c = jnp.max(dec_ref[rows, :], axis=0, keepdims=True)
            gs = gs * dec_c + qg_ref[c * HG_W:(c + 1) * HG_W, :] * head_f32
        gst_ref[...] = gs

        dkte, dqe = dkte_ref[...], dqe_ref[...]
        dq = dqe * eb
        dk = dkte * ekb
        db = dqe * qe - dkte * kte
        dv = dvi_ref[...]
        dblast = dkte * kte + jnp.where(rm == HG_CHUNK - 1, ddec_ref[...] * dec, 0.0)

        for d in range(HG_CHUNK):
            kd, bd, vd = _shifted(k, d, th), _shifted(b, d, th), _shifted(v, d, th)
            e = jnp.exp(jnp.where(rm >= d, b - bd, -1e30))
            p = q * kd * e
            sc = headsum(p, 1)
            dsc = headsum(dop * vd, 1)
            dv = dv + _unshift(sc * dop, d, th)
            dq = dq + dsc * kd * e
            dk = dk + _unshift(dsc * q * e, d, th)
            darg = dsc * p
            db = db + darg - _unshift(darg, d, th)

        db = db + jnp.where(rm == HG_CHUNK - 1, seg_cumsum(dblast), 0.0)
        dlf = seg_rcumsum(db)
        df = dlf / f - dk
        dlb_ref[...] += jnp.sum(df * (1.0 - sg), axis=0, keepdims=True)
        dfl = df * (1.0 - lb) * sg * (1.0 - sg)
        dqr = dq * (sq * (1.0 + qr * (1.0 - sq)))
        dproj_ref[...] = jnp.concatenate([dqr, dfl, dv, dg], axis=1).astype(dproj_ref.dtype)

    rev = lambda i: nt - 1 - i
    col = lambda j: pl.BlockSpec((th, HG_W), lambda i, j=j: (rev(i), j))
    vec = pl.BlockSpec((1, HG_W), lambda i: (0, 0))
    row = pl.BlockSpec((th, HG_W), lambda i: (rev(i), 0))
    tile_f32 = pltpu.VMEM((th, HG_W), F32)
    return pl.pallas_call(
        body, name=name, grid=(nt,),
        in_specs=[col(0), col(1), col(2), col(3), vec, vec, row,
                  pl.BlockSpec((nct, HG_W, HG_W), lambda i: (rev(i), 0, 0)), col(0)],
        out_specs=[pl.BlockSpec((th, 4 * HG_W), lambda i: (rev(i), 0)), vec, vec],
        out_shape=[jax.ShapeDtypeStruct((t_rows, 4 * HG_W), BF16), jax.ShapeDtypeStruct((1, HG_W), F32),
                   jax.ShapeDtypeStruct((1, HG_W), F32)],
        scratch_shapes=[pltpu.VMEM((HG_W, HG_W), F32), pltpu.VMEM((nct * HG_W, th), BF16),
                        pltpu.VMEM((nct * HG_W, HG_W), F32)] + [tile_f32] * 8,
        compiler_params=_cparams(("arbitrary",)),
    )(projp, projp, projp, projp, lb, ng, opre, states, dcat)


_INV_SQRT2 = 1.0 / math.sqrt(2.0)
_INV_SQRT2PI = 1.0 / math.sqrt(2.0 * math.pi)


def _gelu(x):
    return 0.5 * x * (1.0 + lax.erf(x * _INV_SQRT2))


def _gelu_grad(x):
    return 0.5 * (1.0 + lax.erf(x * _INV_SQRT2)) + x * jnp.exp(-0.5 * x * x) * _INV_SQRT2PI


def _sgu_parts(bu, bv, lg, lbias, w_ref, n_groups):
    c = SGU_CHUNK
    tril = (lax.broadcasted_iota(jnp.int32, (c, c), 0) >= lax.broadcasted_iota(jnp.int32, (c, c), 1)).astype(F32)
    gid = lax.broadcasted_iota(jnp.int32, bu.shape, 1) // HEAD
    u = _gelu(bu)
    gv = _gelu(bv)
    mu = jnp.mean(gv, axis=-1, keepdims=True)
    xc = gv - mu
    rstd = lax.rsqrt(jnp.mean(xc * xc, axis=-1, keepdims=True) + LN_EPS)
    xhat = xc * rstd
    vn = xhat * lg + lbias
    ws = [w_ref[gi] * tril for gi in range(n_groups)]
    return tril, gid, u, rstd, xhat, vn, ws


def _sgu_fwd(projp, lg, lbias, w_s, bias_full, *, name, job=None):
    t_rows = projp.shape[0]
    n_groups = w_s.shape[0]
    c = SGU_CHUNK

    def body(u_ref, v_ref, lg_ref, lb_ref, w_ref, bias_ref, o_ref):
        _, gid, u, _, _, vn, ws = _sgu_parts(u_ref[...], v_ref[...], lg_ref[...], lb_ref[...], w_ref, n_groups)
        vnb = vn.astype(BF16)
        z = bias_ref[...]
        for gi in range(n_groups):
            z = z + jnp.where(gid == gi, jnp.dot(ws[gi].astype(BF16), vnb, preferred_element_type=F32), 0.0)
        o_ref[...] = (u * z).astype(o_ref.dtype)

    col = lambda j: pl.BlockSpec((c, HG_W), lambda i, j=j: (i, j))
    return _call(
        body, (projp, projp, lg, lbias, w_s, bias_full), name=name, grid=(t_rows // c,),
        in_specs=[col(4), col(5), _const_spec(lg), _const_spec(lbias), _const_spec(w_s), _const_spec(bias_full)],
        out_specs=pl.BlockSpec((c, HG_W), lambda i: (i, 0)),
        out_shape=jax.ShapeDtypeStruct((t_rows, HG_W), BF16), sem=("arbitrary",), job=job)


def _sgu_bwd(projp, lg, lbias, w_s, bias_full, dcat, *, name):
    t_rows = projp.shape[0]
    n_groups = w_s.shape[0]
    c = SGU_CHUNK
    n = t_rows // c

    def body(u_ref, v_ref, lg_ref, lb_ref, w_ref, bias_ref, do_ref,
             dproj_ref, dlg_ref, dlb_ref, dw_ref, dbs_ref, dbias_acc):
        i = pl.program_id(0)

        @pl.when(i == 0)
        def _():
            dlg_ref[...] = jnp.zeros_like(dlg_ref)
            dlb_ref[...] = jnp.zeros_like(dlb_ref)
            dw_ref[...] = jnp.zeros_like(dw_ref)
            dbias_acc[...] = jnp.zeros_like(dbias_acc)

        bu, bv, lg_v = u_ref[...], v_ref[...], lg_ref[...]
        tril, gid, u, rstd, xhat, vn, ws = _sgu_parts(bu, bv, lg_v, lb_ref[...], w_ref, n_groups)
        vnb = vn.astype(BF16)
        z = bias_ref[...]
        for gi in range(n_groups):
            z = z + jnp.where(gid == gi, jnp.dot(ws[gi].astype(BF16), vnb, preferred_element_type=F32), 0.0)
        do = do_ref[...]
        dbu = do * z * _gelu_grad(bu)
        dz = do * u
        dbias_acc[...] += dz
        dvn = jnp.zeros_like(dz)
        for gi in range(n_groups):
            dzg = jnp.where(gid == gi, dz, 0.0).astype(BF16)
            dw_ref[gi] += lax.dot_general(dzg, vnb, (((1,), (1,)), ((), ())), preferred_element_type=F32) * tril
            dvn = dvn + jnp.dot(ws[gi].T.astype(BF16), dzg, preferred_element_type=F32)
        dlg_ref[...] += jnp.sum(dvn * xhat, axis=0, keepdims=True)
        dlb_ref[...] += jnp.sum(dvn, axis=0, keepdims=True)
        dxh = dvn * lg_v
        dgv = rstd * (dxh - jnp.mean(dxh, axis=-1, keepdims=True)
                      - xhat * jnp.mean(dxh * xhat, axis=-1, keepdims=True))
        dproj_ref[...] = jnp.concatenate([dbu, dgv * _gelu_grad(bv)], axis=1).astype(dproj_ref.dtype)

        @pl.when(i == n - 1)
        def _():
            dbs_ref[...] = jnp.sum(dbias_acc[...].T.reshape(n_groups, HEAD, c), axis=1)

    col = lambda j: pl.BlockSpec((c, HG_W), lambda i, j=j: (i, j))
    return pl.pallas_call(
        body, name=name, grid=(n,),
        in_specs=[col(4), col(5), _const_spec(lg), _const_spec(lbias), _const_spec(w_s), _const_spec(bias_full),
                  col(1)],
        out_specs=[pl.BlockSpec((c, 2 * HG_W), lambda i: (i, 0)), _const_spec(lg), _const_spec(lbias),
                   _const_spec(w_s), pl.BlockSpec((n_groups, c), lambda i: (0, 0))],
        out_shape=[jax.ShapeDtypeStruct((t_rows, 2 * HG_W), BF16), jax.ShapeDtypeStruct(lg.shape, F32),
                   jax.ShapeDtypeStruct(lbias.shape, F32), jax.ShapeDtypeStruct(w_s.shape, F32),
                   jax.ShapeDtypeStruct((n_groups, c), F32)],
        scratch_shapes=[pltpu.VMEM((c, HG_W), F32)],
        compiler_params=_cparams(("arbitrary",)),
    )(projp, projp, lg, lbias, w_s, bias_full, dcat)


def _rope_tables(positions):
    t = positions.shape[0]
    inv_freq = ROPE_THETA ** (-jnp.arange(0, 32, 2, dtype=F32) / 32)
    ang = positions.astype(F32)[:, None] * inv_freq
    cos, sin = jnp.cos(ang), jnp.sin(ang)
    z = lambda w: jnp.zeros((t, w), F32)
    cos_t = jnp.concatenate([jnp.ones((t, 64), F32), cos, cos, z(32)], axis=1)
    sin_up = jnp.concatenate([z(80), sin, z(32)], axis=1)
    sin_dn = jnp.concatenate([z(64), -sin, z(48)], axis=1)
    return cos_t, sin_up, sin_dn


def _rep(x, n):
    return x if n == 1 else jnp.concatenate([x] * n, axis=1)


def _rope(x, cos_t, sin_up, sin_dn):
    w = x.shape[1]
    return x * cos_t + pltpu.roll(x, 16, 1) * sin_up + pltpu.roll(x, w - 16, 1) * sin_dn


def _rope_t(dy, cos_t, sin_up, sin_dn):
    w = dy.shape[1]
    return dy * cos_t + pltpu.roll(dy * sin_up, w - 16, 1) + pltpu.roll(dy * sin_dn, 16, 1)


def _mla_prep(q, kv, projp, tables, *, name):
    nh = N_ATT_HEADS

    def fn(qv, kvv, kr, cos_t, sin_up, sin_dn):
        qr = _rope(qv, _rep(cos_t, nh), _rep(sin_up, nh), _rep(sin_dn, nh))
        krr = _rope(kr, cos_t, sin_up, sin_dn)
        lane = lax.broadcasted_iota(jnp.int32, kvv.shape, 1) % LANES
        return qr, jnp.where(lane < HEAD, kvv, 0.0) + _rep(krr, nh), kvv

    w = q.shape[1]
    return _rowwise(fn, [q, kv, (projp, LANES, P_KR // LANES)] + list(tables), [],
                    [(w, BF16), (w, BF16), (w, BF16)], name=name)


def _mla_prep_bwd(dqr, dkf, tables, *, name):
    nh = N_ATT_HEADS

    def fn(dq, dk, cos_t, sin_up, sin_dn):
        dqp = _rope_t(dq, _rep(cos_t, nh), _rep(sin_up, nh), _rep(sin_dn, nh))
        dkrr = dk[:, 0:LANES]
        for h in range(1, nh):
            dkrr = dkrr + dk[:, LANES * h:LANES * (h + 1)]
        return dqp, _rope_t(dkrr, cos_t, sin_up, sin_dn)

    return _rowwise(fn, [dqr, dkf] + list(tables), [], [(dqr.shape[1], BF16), (LANES, BF16)], name=name)


_LOG2E = 1.0 / math.log(2.0)
_NT = (((1,), (1,)), ((), ()))
_TN = (((0,), (0,)), ((), ()))


def _attn_fwd(qr, kf, kvb, *, name, job=None):
    t_rows = qr.shape[0]
    tq = min(ATT_TQ, t_rows)
    nb = t_rows // tq
    scale = ATT_D ** -0.5

    c2 = scale * _LOG2E

    def body(q_ref, kf_ref, kvb_ref, o_ref, lse_ref):
        qi = pl.program_id(1)
        lane = lax.broadcasted_iota(jnp.int32, (tq, LANES), 1)
        causal_t = (lax.broadcasted_iota(jnp.int32, (tq, tq), 0) <= lax.broadcasted_iota(jnp.int32, (tq, tq), 1))
        heads = [slice(hh * LANES, (hh + 1) * LANES) for hh in range(2)]
        qs = [q_ref[:, cols] for cols in heads]

        def block(first, n_keys, carry, diagonal):
            rows = pl.ds(pl.multiple_of(first * tq, tq), n_keys)
            new = []
            for q, cols, (m_old, l_old, acc_t) in zip(qs, heads, carry):
                s_t = lax.dot_general(kf_ref[rows, cols], q, _NT, preferred_element_type=F32)
                if diagonal:
                    s_t = jnp.where(causal_t, s_t, -1e30)
                m_new = jnp.maximum(m_old, jnp.max(s_t, axis=0, keepdims=True))
                p_t = jnp.exp2((s_t - m_new) * c2)
                a = jnp.exp2((m_old - m_new) * c2)
                pv_t = lax.dot_general(kvb_ref[rows, cols], p_t.astype(BF16), _TN, preferred_element_type=F32)
                new.append((m_new, a * l_old + jnp.sum(p_t, axis=0, keepdims=True), a * acc_t + pv_t))
            return tuple(new)

        init = (jnp.full((1, tq), -1e30, F32), jnp.zeros((1, tq), F32), jnp.zeros((LANES, tq), F32))
        carry = lax.fori_loop(0, qi // 4, lambda g, c: block(4 * g, 4 * tq, c, False), (init, init))
        carry = lax.cond((qi // 2) % 2 == 1, lambda c: block(4 * (qi // 4), 2 * tq, c, False), lambda c: c, carry)
        carry = lax.cond(qi % 2 == 1, lambda c: block(qi - 1, tq, c, False), lambda c: c, carry)
        outs = []
        for hh, (m_fin, l_fin, acc_t) in enumerate(block(qi, tq, carry, True)):
            lse_ref[hh] = m_fin * scale + jnp.log(l_fin)
            outs.append((acc_t / l_fin).T)
        o_ref[...] = jnp.where(lane < HEAD, pltpu.roll(outs[0], HEAD, 1), outs[1])

    pair = pl.BlockSpec((t_rows, 2 * LANES), lambda pr, qi: (0, pr))
    return _call(
        body, (qr, kf, kvb), name=name, grid=(N_ATT_HEADS // 2, nb),
        in_specs=[pl.BlockSpec((tq, 2 * LANES), lambda pr, qi: (qi, pr)), pair, pair],
        out_specs=[pl.BlockSpec((tq, LANES), lambda pr, qi: (qi, pr)),
                   pl.BlockSpec((2, 1, tq), lambda pr, qi: (pr, 0, qi))],
        out_shape=[jax.ShapeDtypeStruct((t_rows, N_ATT_HEADS * HEAD), F32),
                   jax.ShapeDtypeStruct((N_ATT_HEADS, 1, t_rows), F32)],
        sem=("parallel", "arbitrary"), job=job)


def _attn_bwd(qr, kf, kvb, dcat, o, lse, *, name, job=None):
    t_rows = qr.shape[0]
    tq = min(ATT_TQ, t_rows)
    nb = t_rows // tq
    scale = ATT_D ** -0.5
    c2 = scale * _LOG2E
    do_off = 2 * HG_W // LANES

    def body(q_ref, kf_ref, kvb_ref, do_ref, o_ref, lse_ref, dq_ref, dkv_ref, dk_ref):
        ki = pl.program_id(1)

        @pl.when(ki == 0)
        def _():
            dq_ref[...] = jnp.zeros_like(dq_ref)

        lane = lax.broadcasted_iota(jnp.int32, (tq, LANES), 1)
        causal_t = (lax.broadcasted_iota(jnp.int32, (tq, tq), 0) <= lax.broadcasted_iota(jnp.int32, (tq, tq), 1))
        heads = [slice(hh * LANES, (hh + 1) * LANES) for hh in range(2)]
        ks = [kf_ref[:, cols] for cols in heads]
        vs = [kvb_ref[:, cols] for cols in heads]

        def block(qi, n_q, carry, diagonal):
            rows = pl.ds(pl.multiple_of(qi * tq, tq), n_q)
            do_pair, o_pair = do_ref[rows, :], o_ref[rows, :]
            upper = lax.broadcasted_iota(jnp.int32, do_pair.shape, 1) >= HEAD
            new = []
            for hh, (cols, k, v, (dk, dv)) in enumerate(zip(heads, ks, vs, carry)):
                q = q_ref[rows, cols]
                do, ov = (pltpu.roll(do_pair, HEAD, 1), pltpu.roll(o_pair, HEAD, 1)) if hh == 0 else (do_pair, o_pair)
                do = jnp.where(upper, do, 0.0)
                delta = jnp.sum((do * ov).T, axis=0, keepdims=True)
                s_t = lax.dot_general(k, q, _NT, preferred_element_type=F32)
                if diagonal:
                    s_t = jnp.where(causal_t, s_t, -1e30)
                p_t = jnp.exp2(s_t * c2 - lse_ref[hh, :, rows] * _LOG2E)
                dob = do.astype(BF16)
                dv = dv + jnp.dot(p_t.astype(BF16), dob, preferred_element_type=F32)
                dp_t = lax.dot_general(v, dob, _NT, preferred_element_type=F32)
                ds_t = (p_t * (dp_t - delta) * scale).astype(BF16)
                dk = dk + jnp.dot(ds_t, q, preferred_element_type=F32)
                dq_ref[rows, cols] += lax.dot_general(ds_t, k, _TN, preferred_element_type=F32)
                new.append((dk, dv))
            return tuple(new)

        zero = jnp.zeros((tq, LANES), F32)
        carry = block(ki, tq, ((zero, zero), (zero, zero)), True)
        rest = nb - 1 - ki
        carry = lax.fori_loop(0, rest // 2, lambda g, c: block(ki + 1 + 2 * g, 2 * tq, c, False), carry)
        carry = lax.cond(rest % 2 == 1, lambda c: block(nb - 1, tq, c, False), lambda c: c, carry)
        dkv_ref[...] = jnp.concatenate([jnp.where(lane < HEAD, dk, dv) for dk, dv in carry],
                                       axis=1).astype(dkv_ref.dtype)
        dk_ref[...] = jnp.concatenate([dk for dk, _ in carry], axis=1)

    pair_all = pl.BlockSpec((t_rows, 2 * LANES), lambda pr, ki: (0, pr))
    pair_blk = pl.BlockSpec((tq, 2 * LANES), lambda pr, ki: (ki, pr))
    wide = jax.ShapeDtypeStruct((t_rows, N_ATT_HEADS * LANES), F32)
    return _call(
        body, (qr, kf, kvb, dcat, o, lse), name=name, grid=(N_ATT_HEADS // 2, nb),
        in_specs=[pair_all, pair_blk, pair_blk,
                  pl.BlockSpec((t_rows, LANES), lambda pr, ki: (0, do_off + pr)),
                  pl.BlockSpec((t_rows, LANES), lambda pr, ki: (0, pr)),
                  pl.BlockSpec((2, 1, t_rows), lambda pr, ki: (pr, 0, 0))],
        out_specs=[pair_all, pair_blk, pair_blk],
        out_shape=[wide, jax.ShapeDtypeStruct(wide.shape, BF16), wide],
        sem=("parallel", "arbitrary"), job=job)


def _my_pos():
    return lax.axis_index("x"), lax.axis_index("y"), lax.axis_index("c")


def _all_gather(xs, *, name, columns=True):
    return _gather_forward(_run_job(_gather_job(xs, columns), name=name), name=name + "_forward")


def _remote(src, dst, send_sems, recv_sems, k, dev):
    return pltpu.make_async_remote_copy(src_ref=src, dst_ref=dst, send_sem=send_sems.at[k], recv_sem=recv_sems.at[k],
                                        device_id=dev, device_id_type=MESH)


def _block(ref, idx):
    if len(ref.shape) == 2:
        return ref.at[:, pl.ds(pl.multiple_of(idx * LANES, LANES), LANES)]
    return ref.at[idx]


def _gather_job(xs, columns=True):
    n = len(xs)

    def make(x_refs, out_refs, send_sems, recv_sems, local_sems):
        mx, my, mc = _my_pos()
        mine = 4 * mx + 2 * my + mc
        peers = [(mx, my, 1 - mc), (1 - mx, my, mc), (mx, 1 - my, mc), (1 - mx, 1 - my, mc)]
        sends, recvs, local = [], [], []
        for a in range(n):
            local.append(pltpu.make_async_copy(x_refs[a], _block(out_refs[a], mine), local_sems.at[a]))
            for k, dev in enumerate(peers):
                theirs = 4 * dev[0] + 2 * dev[1] + dev[2]
                sends.append(_remote(x_refs[a], _block(out_refs[a], mine), send_sems, recv_sems, 4 * a + k, dev))
                recvs.append(_remote(x_refs[a], _block(out_refs[a], theirs), send_sems, recv_sems, 4 * a + k, dev))
        return sends, recvs, local

    def gathered(x):
        if columns and x.ndim == 2 and x.shape[1] == LANES:
            return jax.ShapeDtypeStruct((x.shape[0], N_DEV * LANES), x.dtype)
        return jax.ShapeDtypeStruct((N_DEV,) + x.shape, x.dtype)

    return _copies_job(xs, [gathered(x) for x in xs], 4 * n, n, make)


def _forward_job(gs):
    n = len(gs)

    def make(in_refs, out_refs, send_sems, recv_sems, local_sems):
        mx, my, mc = _my_pos()
        chips = [(1 - mx, my), (mx, 1 - my), (1 - mx, 1 - my)]
        sends, recvs = [], []
        for a in range(n):
            for j, (cx, cy) in enumerate(chips):
                here = _block(out_refs[a], 4 * cx + 2 * cy + mc)
                there = _block(out_refs[a], 4 * cx + 2 * cy + 1 - mc)
                sends.append(_remote(here, here, send_sems, recv_sems, 3 * a + j, (mx, my, 1 - mc)))
                recvs.append(_remote(here, there, send_sems, recv_sems, 3 * a + j, (mx, my, 1 - mc)))
        return sends, recvs, []

    shapes = [jax.ShapeDtypeStruct(g.shape, g.dtype) for g in gs]
    return _copies_job(gs, shapes, 3 * n, 0, make, in_place=True)


def _gather_forward(gs, *, name):
    return _run_job(_forward_job(gs), name=name)


def _pair_job(xs):
    n = len(xs)

    def make(x_refs, out_refs, send_sems, recv_sems, local_sems):
        mx, my, mc = _my_pos()

        def src(ref, g):
            return _block(ref, 2 * g + 1 - mc) if len(ref.shape) == 2 else ref.at[g, 1 - mc]

        copies = [_remote(src(x_refs[a], g), out_refs[a].at[g], send_sems, recv_sems, 4 * a + g, (mx, my, 1 - mc))
                  for a in range(n) for g in range(4)]
        return copies, copies, []

    shapes = [jax.ShapeDtypeStruct((4, x.shape[0], LANES) if x.ndim == 2 else (4,) + x.shape[2:], x.dtype)
              for x in xs]
    return _copies_job(xs, shapes, 4 * n, 0, make)


def _pair_add(x, r, core, *, name):
    _, a, b = r.shape
    ta = _row_tile(a, 512)

    def body(c_ref, x_ref, r_ref, o_ref):
        o_ref[...] = (x_ref[...] + r_ref[...]).astype(o_ref.dtype)

    blk = pl.BlockSpec((None, ta, b), lambda g, i, c_ref: (g, i, 0))
    own = (pl.BlockSpec((ta, b), lambda g, i, c_ref: (i, 2 * g + c_ref[0])) if x.ndim == 2
           else pl.BlockSpec((None, None, ta, b), lambda g, i, c_ref: (g, c_ref[0], i, 0)))
    return pl.pallas_call(
        body, name=name,
        grid_spec=pltpu.PrefetchScalarGridSpec(
            num_scalar_prefetch=1, grid=(4, a // ta), in_specs=[own, blk], out_specs=blk),
        out_shape=jax.ShapeDtypeStruct((4, a, b), BF16),
        compiler_params=_cparams(("parallel", "parallel")),
    )(core, x, r)


def _quad_job(xs):
    n = len(xs)

    def make(x_refs, out_refs, send_sems, recv_sems, local_sems):
        mx, my, mc = _my_pos()
        mine = 2 * mx + my
        peers = [((1 - mx, my, mc), 2 * (1 - mx) + my), ((mx, 1 - my, mc), 2 * mx + 1 - my),
                 ((1 - mx, 1 - my, mc), 2 * (1 - mx) + 1 - my)]
        sends, recvs, local = [], [], []
        for a in range(n):
            local.append(pltpu.make_async_copy(x_refs[a].at[mine], out_refs[a].at[mine], local_sems.at[a]))
            for k, (dev, g) in enumerate(peers):
                sends.append(_remote(x_refs[a].at[g], out_refs[a].at[mine], send_sems, recv_sems, 3 * a + k, dev))
                recvs.append(_remote(x_refs[a].at[g], out_refs[a].at[g], send_sems, recv_sems, 3 * a + k, dev))
        return sends, recvs, local

    shapes = [jax.ShapeDtypeStruct(x.shape, x.dtype) for x in xs]
    return _copies_job(xs, shapes, 3 * n, n, make)


def _row_tile(r, pref):
    t = min(pref, r)
    while r % t or (t % 8 and t != r):
        t -= 1
    return t


def _adamw(parts, w, m, v, layer, *, name, tile=256, into=None):
    g, a, b = parts.shape
    tile = _row_tile(a, tile)
    c1 = 1.0 / (1.0 - ADAM_B1 ** ADAM_STEP)
    c2 = 1.0 / (1.0 - ADAM_B2 ** ADAM_STEP)
    into = tuple(into or ())

    def body(p_ref, w_ref, m_ref, v_ref, *refs):
        g_ref, d_ref, mo_ref, vo_ref = refs[len(into):]
        grad = p_ref[0].astype(F32)
        for j in range(1, g):
            grad = grad + p_ref[j].astype(F32)
        mn = ADAM_B1 * m_ref[...] + (1.0 - ADAM_B1) * grad
        vn = ADAM_B2 * v_ref[...] + (1.0 - ADAM_B2) * (grad * grad)
        g_ref[...] = grad
        mo_ref[...] = mn
        vo_ref[...] = vn
        d_ref[...] = -ADAM_LR * ((mn * c1) / (jnp.sqrt(vn * c2) + ADAM_EPS) + ADAM_WD * w_ref[...])

    if layer is None:
        src, shape = pl.BlockSpec((tile, b), lambda i: (i, 0)), (a, b)
    else:
        src, shape = pl.BlockSpec((None, tile, b), lambda i: (layer, i, 0)), w.shape
    return pl.pallas_call(
        body, name=name, grid=(a // tile,),
        in_specs=[pl.BlockSpec((g, tile, b), lambda i: (0, i, 0)), src, src, src] + [_ANY] * len(into),
        out_specs=[src] * 4,
        out_shape=[jax.ShapeDtypeStruct(shape, F32)] * 4,
        input_output_aliases={4 + i: i for i in range(len(into))},
        compiler_params=_cparams(("parallel",)),
    )(parts, w, m, v, *into)


W_IN_SHARD = 276


def _w_in_dest(col):
    return jnp.where(col < P_KR, col, jnp.where(col < P_KR + 256, col + (P_CKV - P_KR), col - 2176 + P_KR + HEAD))


PLACE_TILE = 384
PLACE_SHARDS = 3
PICK_TILE = 128
PICK_TILES = 4


def _w_in_tables():
    col = np.arange(N_DEV * W_IN_SHARD)
    dest = np.where(col < P_KR, col, np.where(col < P_KR + 256, col + (P_CKV - P_KR), col - 2176 + P_KR + HEAD))
    shard = col // W_IN_SHARD

    def filled(used, universe, n):
        used = sorted(set(int(u) for u in used))
        assert len(used) <= n, used
        return used + [u for u in universe if u not in used][:n - len(used)]

    place = [filled(shard[dest // PLACE_TILE == c], range(N_DEV), PLACE_SHARDS) for c in range(P_COLS // PLACE_TILE)]
    pick = [filled(dest[shard == j] // PICK_TILE, range(P_COLS // PICK_TILE), PICK_TILES) for j in range(N_DEV)]
    return np.asarray(place, np.int32).reshape(-1), np.asarray(pick, np.int32).reshape(-1)


def _place_w_in(g, *, name):
    _, d, sh = g.shape
    tc, ns = PLACE_TILE, PLACE_SHARDS
    table = jnp.asarray(_w_in_tables()[0])

    def body(tab_ref, g_ref, o_ref, acc_ref):
        ct, s = pl.program_id(0), pl.program_id(1)
        j = tab_ref[ct * ns + s]

        @pl.when(s == 0)
        def _():
            acc_ref[...] = jnp.zeros_like(acc_ref)

        src = j * sh + lax.broadcasted_iota(jnp.int32, (sh, tc), 0)
        dst = ct * tc + lax.broadcasted_iota(jnp.int32, (sh, tc), 1)
        place = (_w_in_dest(src) == dst).astype(BF16)
        acc_ref[...] += jnp.dot(g_ref[...], place, preferred_element_type=F32)

        @pl.when(s == ns - 1)
        def _():
            o_ref[...] = acc_ref[...].astype(o_ref.dtype)

    return pl.pallas_call(
        body, name=name,
        grid_spec=pltpu.PrefetchScalarGridSpec(
            num_scalar_prefetch=1, grid=(P_COLS // tc, ns),
            in_specs=[pl.BlockSpec((None, d, sh), lambda ct, s, tab: (tab[ct * ns + s], 0, 0))],
            out_specs=pl.BlockSpec((d, tc), lambda ct, s, tab: (0, ct)),
            scratch_shapes=[pltpu.VMEM((d, tc), F32)]),
        out_shape=jax.ShapeDtypeStruct((d, P_COLS), BF16),
        compiler_params=_cparams(("parallel", "arbitrary")),
    )(table, g)


def _unplace_w_in(dw, *, name):
    d = dw.shape[0]
    sh, tk, nt = W_IN_SHARD, PICK_TILE, PICK_TILES
    table = jnp.asarray(_w_in_tables()[1])

    def body(tab_ref, dw_ref, o_ref):
        j, kk = pl.program_id(0), pl.program_id(1)
        tile = tab_ref[j * nt + kk]
        src = j * sh + lax.broadcasted_iota(jnp.int32, (tk, sh), 1)
        dst = tile * tk + lax.broadcasted_iota(jnp.int32, (tk, sh), 0)
        pick = (_w_in_dest(src) == dst).astype(BF16)
        part = _split_dot(dw_ref[...], pick)

        @pl.when(kk == 0)
        def _():
            o_ref[...] = part

        @pl.when(kk > 0)
        def _():
            o_ref[...] += part

    return pl.pallas_call(
        body, name=name,
        grid_spec=pltpu.PrefetchScalarGridSpec(
            num_scalar_prefetch=1, grid=(N_DEV, nt),
            in_specs=[pl.BlockSpec((d, tk), lambda j, kk, tab: (0, tab[j * nt + kk]))],
            out_specs=pl.BlockSpec((None, d, sh), lambda j, kk, tab: (j, 0, 0))),
        out_shape=jax.ShapeDtypeStruct((N_DEV, d, sh), F32),
        compiler_params=_cparams(("parallel", "arbitrary")),
    )(table, dw)


def _gate_up_swiglu(h1, wgu, *, name):
    t_rows, k = h1.shape
    w = wgu.shape[2]
    tm = _tile(t_rows, 1024)

    def body(a_ref, wg_ref, wu_ref, gu_ref, act_ref):
        a = a_ref[...].astype(BF16)
        gate = jnp.dot(a, wg_ref[...], preferred_element_type=F32)
        up = jnp.dot(a, wu_ref[...], preferred_element_type=F32)
        gu_ref[0] = gate.astype(gu_ref.dtype)
        gu_ref[1] = up.astype(gu_ref.dtype)
        act_ref[...] = (gate * _sigmoid(gate) * up).astype(act_ref.dtype)

    return pl.pallas_call(
        body, name=name, grid=(t_rows // tm, 4),
        in_specs=[pl.BlockSpec((tm, k), lambda i, j: (i, 0)),
                  pl.BlockSpec((None, k, w), lambda i, j: (j, 0, 0)),
                  pl.BlockSpec((None, k, w), lambda i, j: (j + 4, 0, 0))],
        out_specs=[pl.BlockSpec((2, None, tm, w), lambda i, j: (0, j, i, 0)),
                   pl.BlockSpec((None, tm, w), lambda i, j: (j, i, 0))],
        out_shape=[jax.ShapeDtypeStruct((2, 4, t_rows, w), BF16), jax.ShapeDtypeStruct((4, t_rows, w), BF16)],
        compiler_params=_cparams(("parallel", "arbitrary")),
    )(h1, wgu, wgu)


def _down_dx_swiglu(dffn, wdown, gu, *, name):
    t_rows, k = dffn.shape
    w = gu.shape[3]
    tm = _tile(t_rows, 1024)

    def body(d_ref, w_ref, gu_ref, o_ref):
        dact = lax.dot_general(d_ref[...].astype(BF16), w_ref[...], _NT, preferred_element_type=F32)
        gate, up = gu_ref[0].astype(F32), gu_ref[1].astype(F32)
        sg = _sigmoid(gate)
        silu = gate * sg
        o_ref[0] = (dact * up * (sg + silu - silu * sg)).astype(o_ref.dtype)
        o_ref[1] = (dact * silu).astype(o_ref.dtype)

    blk = pl.BlockSpec((2, None, tm, w), lambda i, j: (0, j, i, 0))
    return pl.pallas_call(
        body, name=name, grid=(t_rows // tm, 4),
        in_specs=[pl.BlockSpec((tm, k), lambda i, j: (i, 0)), pl.BlockSpec((w, k), lambda i, j: (j, 0)), blk],
        out_specs=blk, out_shape=jax.ShapeDtypeStruct(gu.shape, BF16),
        compiler_params=_cparams(("parallel", "arbitrary")),
    )(dffn, wdown, gu)


BIG = ("w_in", "mla_w_uq", "mla_w_ukv", "w_out", "w_gate_up", "w_down", "ple_w_gate", "ple_w_proj")
SMALL = ("ln_in_g", "ln_in_b", "hgrn_lb_logits", "hgrn_norm_g", "sgu_ln_g", "sgu_ln_b", "sgu_w_s", "sgu_b_s",
         "mla_q_norm_g", "mla_kv_norm_g", "ln1_g", "ln1_b", "ln2_g", "ln2_b")
ORDER = ("ln_in_g", "ln_in_b", "w_in", "hgrn_lb_logits", "hgrn_norm_g", "sgu_ln_g", "sgu_ln_b", "sgu_w_s", "sgu_b_s",
         "mla_q_norm_g", "mla_w_uq", "mla_kv_norm_g", "mla_w_ukv", "w_out", "ln1_g", "ln1_b", "w_gate_up", "w_down",
         "ple_w_gate", "ple_w_proj", "ln2_g", "ln2_b")


def _slab(a, align):
    s = a.reshape(-1, LANES)
    pad = -s.shape[0] % align
    return jnp.pad(s, ((0, pad), (0, 0))) if pad else s


def _pack(arrays, align=16, total_align=512):
    s = jnp.concatenate([_slab(a, align) for a in arrays], axis=0)
    pad = -s.shape[0] % total_align
    return jnp.pad(s, ((0, pad), (0, 0))) if pad else s


def _unpack(slab, shapes, align=16):
    out, r0 = [], 0
    for s in shapes:
        nr = math.prod(s) // LANES
        out.append(slab[r0:r0 + nr].reshape(s))
        r0 += nr + (-nr % align)
    return out


def _weight_shards(w, li):
    uq_pad = ((0, 0), (0, LANES - ATT_D))
    shards = {k: w[k][li] for k in BIG}
    shards["mla_w_uq"] = jnp.pad(shards["mla_w_uq"], uq_pad)
    return {k: s.astype(BF16) for k, s in shards.items()}


def _usable_weights(g, *, name):
    out = {}
    for k, a in g.items():
        if k == "w_in":
            out[k] = _place_w_in(a, name=name + "_place_w_in")
        elif k in ("w_out", "w_down", "ple_w_gate"):
            out[k] = a.reshape(a.shape[0] * a.shape[1], a.shape[2])
        else:
            out[k] = a
    return out


BY_COLUMNS = ("mla_w_uq", "mla_w_ukv", "ple_w_proj")


def _as_pairs(k, g):
    if k in BY_COLUMNS:
        return g
    if g.ndim == 2:
        return g.reshape((4, 2, g.shape[0] // N_DEV) + g.shape[1:])
    return g.reshape((4, 2) + g.shape[1:])


def _twice(fn):
    return lambda *a: fn(*a) * 2


def _layer_forward(li, h, hb, p_i, wts, sm, lbs, tables, alpha, hgrn_job=None, after_hgrn=None, attn_job=None,
                   after_attn=None, loss_target=None):
    n = f"l{li}_"
    row1 = lambda a: a.reshape(1, -1)
    projp = _mm(hb, wts["w_in"], name=n + "proj")
    ng = row1(sm["hgrn_norm_g"][li])
    res = _hgrn_fwd(projp, lbs[li], ng, name=n + "hgrn_fwd", job=hgrn_job)
    if hgrn_job is not None:
        res, got = res
    o_a, o_pre, states = res
    lg, lbias = row1(sm["sgu_ln_g"][li]), row1(sm["sgu_ln_b"][li])
    w_s = sm["sgu_w_s"][li]
    bias_full = jnp.repeat(sm["sgu_b_s"][li].T, HEAD, axis=1)
    o_b = _sgu_fwd(projp, lg, lbias, w_s, bias_full, name=n + "sgu_fwd",
                   job=None if hgrn_job is None else _forward_job(got))
    if hgrn_job is not None:
        o_b, got = o_b
        wts = dict(wts, **after_hgrn(got))
    qg, kvg = row1(sm["mla_q_norm_g"][li]), row1(sm["mla_kv_norm_g"][li])
    cq_view, ckv_view = (projp, 384, P_CQ // 384), (projp, 256, P_CKV // 256)
    (cqn,) = _rowwise(_fn_rms, [cq_view], [qg], [(384, BF16)], name=n + "q_norm")
    (ckvn,) = _rowwise(_fn_rms, [ckv_view], [kvg], [(256, BF16)], name=n + "kv_norm")
    q = _mm(cqn, wts["mla_w_uq"], name=n + "uq")
    kv = _mm(ckvn, wts["mla_w_ukv"], name=n + "ukv")
    qr, kf, kvb = _mla_prep(q, kv, projp, tables, name=n + "mla_prep")
    res = _attn_fwd(qr, kf, kvb, name=n + "attn_fwd", job=attn_job)
    if attn_job is not None:
        res, got = res
    o_c, lse = res
    cat = jnp.concatenate([o_a, o_b, o_c.astype(BF16)], axis=1)
    mix = _mm(cat, wts["w_out"], name=n + "out_proj", job=None if attn_job is None else _forward_job(got))
    if attn_job is not None:
        mix, got = mix
        wts = dict(wts, **after_attn(got))
    g1, b1 = row1(sm["ln1_g"][li]), row1(sm["ln1_b"][li])
    d = h.shape[1]
    h1, h1b = _rowwise(_twice(_make_post_mix(alpha)), [h, mix], [g1, b1], [(d, F32), (d, BF16)], name=n + "ln1")
    gu, act = _gate_up_swiglu(h1b, wts["w_gate_up"], name=n + "gate_up")
    ffn = _mm_kblocks(act, wts["w_down"], bm="kn", tm=1024, name=n + "down")
    pg = _mm(h1b, wts["ple_w_gate"], name=n + "ple_gate")
    pp = _mm(p_i, wts["ple_w_proj"], name=n + "ple_proj")
    g2, b2 = row1(sm["ln2_g"][li]), row1(sm["ln2_b"][li])
    if loss_target is None:
        out = _rowwise(_twice(_make_ple_ln(alpha)), [h1, ffn, pg, pp], [g2, b2], [(d, F32), (d, BF16)],
                       name=n + "ln2")
    else:
        def ln_and_loss(h1v, ffnv, pgv, ppv, tv, gv, bv):
            err = _make_ple_ln(alpha)(h1v, ffnv, pgv, ppv, gv, bv)[0] - tv
            return err * (1.0 / d), 0.5 * jnp.sum(jnp.mean(err * err, axis=-1, keepdims=True), axis=0, keepdims=True)

        out = _rowwise(ln_and_loss, [h1, ffn, pg, pp, loss_target], [g2, b2], [(d, F32)], accs=[(1, 1)],
                       name=n + "ln2_loss")
    saved = dict(h=h, hb=hb, h1b=h1b, projp=projp, o_pre=o_pre, states=states, cqn=cqn, ckvn=ckvn, qr=qr, kf=kf, kvb=kvb, o_c=o_c,
                 lse=lse, cat=cat, mix=mix, h1=h1, gu=gu, act=act, ffn=ffn, pg=pg, pp=pp, ng=ng, lg=lg, wts=wts,
                 lbias=lbias, w_s=w_s, bias_full=bias_full, qg=qg, kvg=kvg, g1=g1, b1=b1, g2=g2, b2=b2)
    return tuple(out), saved


RS_EARLY = ("ple_w_proj", "ple_w_gate", "w_down", "w_gate_up", "w_out")
RS_LATE = ("mla_w_uq", "mla_w_ukv", "w_in")


def _layer_backward(li, dh2_parts, p_i, sv, lbs, tables, alpha, core, carried=None):
    n = f"l{li}_b_"
    wts = sv["wts"]
    gr = {}
    dh1_a, dffn, dpg, dpp, gr["ln2_g"], gr["ln2_b"] = _rowwise_vjp(
        _make_ple_ln(alpha), [sv["h1"], sv["ffn"], sv["pg"], sv["pp"]], [sv["g2"], sv["b2"]], [dh2_parts],
        groups=[[0], [1], [2], [3]], gdtypes=[F32, BF16, BF16, BF16], name=n + "ln2")
    big = {}
    big["ple_w_proj"] = _mm(p_i, dpp, am="km", tk=2048, name=n + "ple_proj_dw")
    big["ple_w_gate"] = _mm(sv["h1b"], dpg, am="km", name=n + "ple_gate_dw")
    dh1_b = _mm(dpg, wts["ple_w_gate"], bm="nk", name=n + "ple_gate_dx")
    big["w_down"] = _mm(sv["act"], dffn, am="bkm", tk=4096, name=n + "down_dw")
    dgu = _down_dx_swiglu(dffn, wts["w_down"], sv["gu"], name=n + "down_dx")
    dgu = dgu.reshape((N_DEV,) + dgu.shape[2:])
    big["w_gate_up"], carried_got = _mm(sv["h1b"], dgu, am="km", bm="bkn", om="bmn", tk=4096, name=n + "gate_up_dw",
                                        job=carried), None
    if carried is not None:
        big["w_gate_up"], carried_got = big["w_gate_up"]
    early = [_as_pairs(k, big[k]) for k in RS_EARLY[:-1]]
    dh1_c, theirs = _mm_kblocks(dgu, wts["w_gate_up"], bm="bnk", tm=512, name=n + "gate_up_dx",
                                job=_pair_job(early))
    dh_a, dmix, gr["ln1_g"], gr["ln1_b"] = _rowwise_vjp(
        _make_post_mix(alpha), [sv["h"], sv["mix"]], [sv["g1"], sv["b1"]], [[dh1_a, dh1_b, dh1_c]],
        groups=[[0], [1]], gdtypes=[F32, BF16], name=n + "ln1")
    big["w_out"] = _mm(sv["cat"], dmix, am="km", name=n + "out_proj_dw")
    early.append(_as_pairs("w_out", big["w_out"]))
    dcat, their_w_out = _mm(dmix, wts["w_out"], bm="nk", name=n + "out_proj_dx", job=_pair_job(early[-1:]))
    sums = [_pair_add(x, r, core, name=n + "pair_add_" + k)
            for k, x, r in zip(RS_EARLY, early, list(theirs) + list(their_w_out))]

    (dqr, dkv, dkf), early_quads = _attn_bwd(sv["qr"], sv["kf"], sv["kvb"], dcat, sv["o_c"], sv["lse"],
                                             name=n + "attn", job=_quad_job(sums))
    dqpad, dkr = _mla_prep_bwd(dqr, dkf, tables, name=n + "mla_prep")
    big["mla_w_uq"] = _mm(sv["cqn"], dqpad, am="km", tk=2048, name=n + "uq_dw")
    dcqn = _mm(dqpad, wts["mla_w_uq"], bm="nk", name=n + "uq_dx")
    big["mla_w_ukv"] = _mm(sv["ckvn"], dkv, am="km", tk=2048, name=n + "ukv_dw")
    dckvn = _mm(dkv, wts["mla_w_ukv"], bm="nk", name=n + "ukv_dx")
    projp = sv["projp"]
    dcq, gr["mla_q_norm_g"] = _rowwise_vjp(_fn_rms, [(projp, 384, P_CQ // 384)], [sv["qg"]], [[dcqn]],
                                           groups=[[0]], gdtypes=[BF16], name=n + "q_norm")
    dckv, gr["mla_kv_norm_g"] = _rowwise_vjp(_fn_rms, [(projp, 256, P_CKV // 256)], [sv["kvg"]], [[dckvn]],
                                             groups=[[0]], gdtypes=[BF16], name=n + "kv_norm")
    dsgu, gr["sgu_ln_g"], gr["sgu_ln_b"], gr["sgu_w_s"], gr["sgu_b_s"] = _sgu_bwd(
        projp, sv["lg"], sv["lbias"], sv["w_s"], sv["bias_full"], dcat, name=n + "sgu")
    dhg, gr["hgrn_norm_g"], gr["lower_bound"] = _hgrn_bwd(
        projp, lbs[li], sv["ng"], sv["o_pre"], sv["states"], dcat, name=n + "hgrn")
    dprojp = jnp.concatenate([dhg, dsgu, dcq, dkr, dckv], axis=1)
    big["w_in"] = _unplace_w_in(_mm(sv["hb"], dprojp, am="km", tk=4096, name=n + "proj_dw"),
                                name=n + "proj_dw_shards")
    late = [_as_pairs(k, big[k]) for k in RS_LATE]
    dh_b, theirs = _mm(dprojp, wts["w_in"], bm="nk", tk=P_COLS, name=n + "proj_dx", job=_pair_job(late))
    late_sums = [_pair_add(x, r, core, name=n + "pair_add_" + k) for k, x, r in zip(RS_LATE, late, theirs)]
    return [dh_a, dh_b], gr, early_quads, late_sums, carried_got


def kernel(x, p, positions, ln_in_g, ln_in_b, w_in, hgrn_lb_logits, hgrn_norm_g, sgu_ln_g, sgu_ln_b, sgu_w_s, sgu_b_s, mla_q_norm_g, mla_w_uq, mla_kv_norm_g, mla_w_ukv, w_out, ln1_g, ln1_b, w_gate_up, w_down, ple_w_gate, ple_w_proj, ln2_g, ln2_b, loss_target, m_ln_in_g, m_ln_in_b, m_w_in, m_hgrn_lb_logits, m_hgrn_norm_g, m_sgu_ln_g, m_sgu_ln_b, m_sgu_w_s, m_sgu_b_s, m_mla_q_norm_g, m_mla_w_uq, m_mla_kv_norm_g, m_mla_w_ukv, m_w_out, m_ln1_g, m_ln1_b, m_w_gate_up, m_w_down, m_ple_w_gate, m_ple_w_proj, m_ln2_g, m_ln2_b, v_ln_in_g, v_ln_in_b, v_w_in, v_hgrn_lb_logits, v_hgrn_norm_g, v_sgu_ln_g, v_sgu_ln_b, v_sgu_w_s, v_sgu_b_s, v_mla_q_norm_g, v_mla_w_uq, v_mla_kv_norm_g, v_mla_w_ukv, v_w_out, v_ln1_g, v_ln1_b, v_w_gate_up, v_w_down, v_ple_w_gate, v_ple_w_proj, v_ln2_g, v_ln2_b):
    args = dict(locals())
    w = {k: args[k] for k in ORDER}
    m = {k: args["m_" + k] for k in ORDER}
    v = {k: args["v_" + k] for k in ORDER}
    depth = w_in.shape[0]
    assert depth == 2, "the lower-bound kernel is written for two layers"
    alpha = (2 * depth) ** 0.25
    xs, tgt = x[0], loss_target[0]
    d_model = xs.shape[1]

    shards = [_weight_shards(w, li) for li in range(depth)]
    on_hgrn0 = ("mla_w_uq", "mla_w_ukv", "w_out", "ple_w_gate", "ple_w_proj")
    ffn0 = ("w_gate_up", "w_down")
    first1 = ("w_in", "mla_w_uq", "mla_w_ukv", "w_out")
    on_attn1 = ("w_gate_up", "w_down", "ple_w_gate", "ple_w_proj")
    layer1_first = {}

    def after_hgrn0(got):
        return _usable_weights(dict(zip(on_hgrn0, got)), name="l0")

    def after_attn0(got):
        layer1_first.update(_usable_weights(dict(zip(first1, got[len(ffn0):])), name="l1"))
        return _usable_weights(dict(zip(ffn0, got[:len(ffn0)])), name="l0")

    def after_attn1(got):
        return _usable_weights(dict(zip(on_attn1, got)), name="l1")

    tables = _rope_tables(positions[0])
    row1 = lambda a: a.reshape(1, -1)
    l0, l1 = row1(hgrn_lb_logits[0]), row1(hgrn_lb_logits[1])
    lbs = _rowwise(_fn_lower_bounds, [l0, l1], [], [(HG_W, F32), (HG_W, F32)], name="lower_bounds")

    gin, bin_ = row1(ln_in_g), row1(ln_in_b)
    (h, hb), g_in = _rowwise(_twice(_fn_ln), [xs], [gin, bin_], [(d_model, F32), (d_model, BF16)], name="ln_in",
                             job=_gather_job([shards[0]["w_in"]]))
    w_in0 = _usable_weights({"w_in": _gather_forward(g_in, name="gather_l0_w_in_forward")[0]}, name="l0")
    (h, hb), sv0 = _layer_forward(
        0, h, hb, p[0, 0], w_in0, w, lbs, tables, alpha,
        hgrn_job=_gather_job([shards[0][k] for k in on_hgrn0]), after_hgrn=after_hgrn0,
        attn_job=_gather_job([shards[0][k] for k in ffn0] + [shards[1][k] for k in first1]), after_attn=after_attn0)
    (dy, loss_local), sv1 = _layer_forward(
        1, h, hb, p[1, 0], layer1_first, w, lbs, tables, alpha,
        attn_job=_gather_job([shards[1][k] for k in on_attn1]), after_attn=after_attn1, loss_target=tgt)
    saved = [sv0, sv1]
    loss = lax.psum(loss_local[0, 0], ("x", "y", "c"))

    core = lax.axis_index("c").astype(jnp.int32).reshape(1)
    dparts, grads, quads, carried = [dy], [None] * depth, [None] * depth, None
    for li in reversed(range(depth)):
        dparts, grads[li], early_quads, late_sums, late_quads = _layer_backward(
            li, dparts, p[li, 0], saved[li], lbs, tables, alpha, core, carried=carried)
        quads[li] = dict(zip(RS_EARLY, early_quads))
        if carried is not None:
            quads[li + 1].update(zip(RS_LATE, late_quads))
        carried = _quad_job(late_sums)
    (dx, d_gin, d_bin), late_quads = _rowwise_vjp(_fn_ln, [xs], [gin, bin_], [dparts], groups=[[0]], name="ln_in_b",
                                                   job=carried)
    quads[0].update(zip(RS_LATE, late_quads))
    dl0, dl1 = _rowwise_vjp(_fn_lower_bounds, [l0, l1], [], [[grads[0]["lower_bound"]], [grads[1]["lower_bound"]]],
                            groups=[[0], [1]], name="lower_bounds_b")

    prefixes = ("grad_", "delta_", "new_m_", "new_v_")
    uq_pad = ((0, 0), (0, 0), (0, LANES - ATT_D))
    state = {k: ((jnp.pad(w[k], uq_pad), jnp.pad(m[k], uq_pad), jnp.pad(v[k], uq_pad)) if k == "mla_w_uq"
                 else (w[k], m[k], v[k])) for k in BIG}
    out = {}
    for k in BIG:
        res4 = None
        for li in range(depth):
            res4 = _adamw(quads[li][k], *state[k], li, name=f"adamw_l{li}_{k}", into=res4)
        for pre, a in zip(prefixes, res4):
            out[pre + k] = a[:, :, :ATT_D] if k == "mla_w_uq" else a

    small_g = {"ln_in_g": d_gin.reshape(-1), "ln_in_b": d_bin.reshape(-1),
               "hgrn_lb_logits": jnp.stack([dl0.reshape(-1), dl1.reshape(-1)])}
    for k in SMALL[3:]:
        small_g[k] = jnp.stack([grads[li][k].reshape(w[k].shape[1:]) for li in range(depth)])
    (small_parts,) = _all_gather([_pack([small_g[k] for k in SMALL])], name="gather_small_grads", columns=False)
    slabs = _adamw(small_parts, _pack([w[k] for k in SMALL]), _pack([m[k] for k in SMALL]),
                   _pack([v[k] for k in SMALL]), None, name="adamw_small")
    shapes = [w[k].shape for k in SMALL]
    for pre, slab in zip(prefixes, slabs):
        for k, a in zip(SMALL, _unpack(slab, shapes)):
            out[pre + k] = a
    res = [loss, dx[None]]
    for prefix in ("grad_", "delta_", "new_m_", "new_v_"):
        res += [out[prefix + k] for k in ORDER]
    return tuple(res)
```

```python
import functools
import math

import jax
import jax.numpy as jnp
import numpy as np
from jax import lax
from jax.experimental import pallas as pl
from jax.experimental.pallas import tpu as pltpu

F32 = jnp.float32
BF16 = jnp.bfloat16
MESH = pl.DeviceIdType.MESH

LN_EPS = 1e-5
RMS_EPS = 1e-6
ROPE_THETA = 10000.0
ADAM_LR, ADAM_B1, ADAM_B2, ADAM_EPS, ADAM_WD, ADAM_STEP = 0.001, 0.9, 0.999, 1e-08, 0.01, 10

N_DEV = 8
LANES = 128
HG_CHUNK = 16
HG_W = 256
HEAD = 64
SGU_CHUNK = 128
SGU_STEP_CHUNKS = 4
N_ATT_HEADS = 8
ATT_D = 96
VMEM_LIMIT = 56 * 1024 * 1024

HG_TILE = 256
ATT_TQ = 512
ROW_TILE = 512

P_CQ, P_KR, P_CKV, P_COLS = 1536, 1920, 2048, 2304


def _cparams(sem):
    return pltpu.CompilerParams(dimension_semantics=sem, vmem_limit_bytes=VMEM_LIMIT)


_ANY = pl.BlockSpec(memory_space=pl.ANY)


def _call(body, operands, *, name, grid, in_specs, out_specs, out_shape, sem, scratch_shapes=(), job=None):
    if job is None:
        return pl.pallas_call(body, name=name, grid=grid, in_specs=in_specs, out_specs=out_specs, out_shape=out_shape,
                              scratch_shapes=list(scratch_shapes), compiler_params=_cparams(sem))(*operands)
    single = not isinstance(out_shape, (list, tuple))
    shapes = [out_shape] if single else list(out_shape)
    ospecs = [out_specs] if single else list(out_specs)
    ni, no, ns = len(operands), len(shapes), len(scratch_shapes)
    ji, jo = len(job.inputs), len(job.out_shapes)

    def hosted(*refs):
        p = 0
        parts = []
        for cnt in (ni, ji, no, jo, ns):
            parts.append(refs[p:p + cnt])
            p += cnt
        ins, jins, outs, jouts, scr = parts
        jsems = refs[p:]
        ids = [pl.program_id(a) for a in range(len(grid))]
        first = functools.reduce(lambda a, b: a & b, [i == 0 for i in ids])
        last = functools.reduce(lambda a, b: a & b, [i == g - 1 for i, g in zip(ids, grid)])

        @pl.when(first)
        def _():
            job.start(jins, jouts, jsems)

        body(*ins, *outs, *scr)

        @pl.when(last)
        def _():
            job.finish(jins, jouts, jsems)

    res = pl.pallas_call(
        hosted, name=name, grid=grid,
        in_specs=list(in_specs) + [_ANY] * ji, out_specs=ospecs + [_ANY] * jo,
        out_shape=shapes + list(job.out_shapes),
        scratch_shapes=list(scratch_shapes) + [pltpu.SemaphoreType.DMA((c,)) for c in job.sem_counts],
        input_output_aliases=job.aliases(ni, no),
        compiler_params=_cparams(("arbitrary",) * len(grid)),
    )(*operands, *job.inputs)
    own = res[0] if single else res[:no]
    return own, res[no:]


class _Job:
    def __init__(self, inputs, out_shapes, sem_counts, start, finish, in_place=False):
        self.inputs, self.out_shapes, self.sem_counts = list(inputs), list(out_shapes), list(sem_counts)
        self.start, self.finish, self.in_place = start, finish, in_place

    def aliases(self, first_in, first_out):
        return {first_in + i: first_out + i for i in range(len(self.inputs))} if self.in_place else {}


def _copies_job(inputs, out_shapes, n_remote, n_local, make, in_place=False):
    def start(jins, jouts, sems):
        sends, _, local = make(jins, jouts, *sems)
        for cp in local + sends:
            cp.start()

    def finish(jins, jouts, sems):
        sends, recvs, local = make(jins, jouts, *sems)
        for cp in recvs:
            cp.wait_recv()
        for cp in sends:
            cp.wait_send()
        for cp in local:
            cp.wait()

    return _Job(inputs, out_shapes, [n_remote, n_remote, max(n_local, 1)], start, finish, in_place)


def _run_job(job, *, name):
    ji, jo = len(job.inputs), len(job.out_shapes)

    def body(*refs):
        jins, jouts, sems = refs[:ji], refs[ji:ji + jo], refs[ji + jo:]
        job.start(jins, jouts, sems)
        job.finish(jins, jouts, sems)

    return pl.pallas_call(
        body, name=name, out_shape=list(job.out_shapes), in_specs=[_ANY] * ji, out_specs=[_ANY] * jo,
        scratch_shapes=[pltpu.SemaphoreType.DMA((c,)) for c in job.sem_counts],
        input_output_aliases=job.aliases(0, 0),
    )(*job.inputs)


def _tile(n, pref):
    if n % pref == 0:
        return pref
    best = None
    t = LANES
    while t <= min(n, pref):
        if n % t == 0:
            best = t
        t += LANES
    return best if best is not None else n


def _mm(a, b, *, am="mk", bm="kn", om="mn", out_dtype=F32, tm=1024, tn=1024, tk=1024, name, job=None):
    if am == "mk":
        m, k = a.shape
    elif am == "km":
        k, m = a.shape
    elif am == "bmk":
        m, tk = a.shape[1], a.shape[2]
        k = a.shape[0] * tk
    else:
        k, tm = a.shape[1], a.shape[2]
        m = a.shape[0] * tm
    if bm == "kn":
        kb_, n = b.shape
    elif bm == "nk":
        n, kb_ = b.shape
    elif bm == "bkn":
        kb_, tn = b.shape[1], b.shape[2]
        n = b.shape[0] * tn
    else:
        n, tk = b.shape[1], b.shape[2]
        kb_ = b.shape[0] * tk
    assert kb_ == k, (a.shape, b.shape, am, bm)
    tm, tn, tk = _tile(m, tm), _tile(n, tn), _tile(k, tk)
    nk = k // tk
    dims = (((0 if am in ("km", "bkm") else 1,), (1 if bm in ("nk", "bnk") else 0,)), ((), ()))

    a_spec = {"mk": pl.BlockSpec((tm, tk), lambda i, j, kk: (i, kk)),
              "km": pl.BlockSpec((tk, tm), lambda i, j, kk: (kk, i)),
              "bmk": pl.BlockSpec((None, tm, tk), lambda i, j, kk: (kk, i, 0)),
              "bkm": pl.BlockSpec((None, tk, tm), lambda i, j, kk: (i, kk, 0))}[am]
    b_spec = {"kn": pl.BlockSpec((tk, tn), lambda i, j, kk: (kk, j)),
              "nk": pl.BlockSpec((tn, tk), lambda i, j, kk: (j, kk)),
              "bkn": pl.BlockSpec((None, tk, tn), lambda i, j, kk: (j, kk, 0)),
              "bnk": pl.BlockSpec((None, tn, tk), lambda i, j, kk: (kk, j, 0))}[bm]
    if om == "mn":
        o_spec, o_shape = pl.BlockSpec((tm, tn), lambda i, j, kk: (i, j)), (m, n)
    else:
        o_spec, o_shape = pl.BlockSpec((None, tm, tn), lambda i, j, kk: (j, i, 0)), (n // tn, m, tn)

    def body(a_ref, b_ref, o_ref, *acc):
        kk = pl.program_id(2)

        def prod():
            return lax.dot_general(a_ref[...].astype(BF16), b_ref[...].astype(BF16), dims, preferred_element_type=F32)

        if nk == 1:
            o_ref[...] = prod().astype(o_ref.dtype)
            return
        acc_ref, = acc

        @pl.when(kk == 0)
        def _():
            acc_ref[...] = prod()

        if nk > 2:
            @pl.when((kk > 0) & (kk < nk - 1))
            def _():
                acc_ref[...] += prod()

        @pl.when(kk == nk - 1)
        def _():
            o_ref[...] = (acc_ref[...] + prod()).astype(o_ref.dtype)

    return _call(body, (a, b), name=name, grid=(m // tm, n // tn, nk), in_specs=[a_spec, b_spec], out_specs=o_spec,
                 out_shape=jax.ShapeDtypeStruct(o_shape, out_dtype),
                 scratch_shapes=[pltpu.VMEM((tm, tn), F32)] if nk > 1 else [],
                 sem=("parallel", "parallel", "arbitrary"), job=job)


def _mm_kblocks(a, b, *, bm, tm, name, job=None):
    nkb, m, kb = a.shape
    n = b.shape[1]
    tm = _tile(m, tm)

    def body(a_ref, b_ref, o_ref):
        acc = None
        for j in range(nkb):
            if bm == "kn":
                part = jnp.dot(a_ref[j], b_ref[j * kb:(j + 1) * kb, :], preferred_element_type=F32)
            else:
                part = lax.dot_general(a_ref[j], b_ref[j], _NT, preferred_element_type=F32)
            acc = part if acc is None else acc + part
        o_ref[...] = acc

    b_spec = (pl.BlockSpec(b.shape, lambda i: (0, 0)) if bm == "kn" else pl.BlockSpec(b.shape, lambda i: (0, 0, 0)))
    return _call(body, (a, b), name=name, grid=(m // tm,),
                 in_specs=[pl.BlockSpec((nkb, tm, kb), lambda i: (0, i, 0)), b_spec],
                 out_specs=pl.BlockSpec((tm, n), lambda i: (i, 0)), out_shape=jax.ShapeDtypeStruct((m, n), F32),
                 sem=("parallel",), job=job)


def _row_operand(a, tile):
    if isinstance(a, tuple):
        arr, w, j = a
        return arr, pl.BlockSpec((tile, w), lambda i, j=j: (i, j))
    return a, pl.BlockSpec((tile, a.shape[1]), lambda i: (i, 0))


def _const_spec(c):
    nd = c.ndim
    return pl.BlockSpec(c.shape, lambda i, nd=nd: (0,) * nd)


def _rowwise(fn, rows, consts, outs, *, name, accs=(), tile=None, job=None):
    t_rows = (rows[0][0] if isinstance(rows[0], tuple) else rows[0]).shape[0]
    tile = min(tile or ROW_TILE, t_rows)
    arrs, specs = zip(*[_row_operand(a, tile) for a in rows])
    nin, no = len(rows) + len(consts), len(outs)

    def body(*refs):
        res = fn(*[r[...] for r in refs[:nin]])
        for r, v in zip(refs[nin:nin + no], res[:no]):
            r[...] = v.astype(r.dtype)
        if accs:
            a_refs = refs[nin + no:]

            @pl.when(pl.program_id(0) == 0)
            def _():
                for r in a_refs:
                    r[...] = jnp.zeros_like(r)

            for r, v in zip(a_refs, res[no:]):
                r[...] += v

    out_shape = [jax.ShapeDtypeStruct((t_rows, w), dt) for w, dt in outs]
    out_shape += [jax.ShapeDtypeStruct(s, F32) for s in accs]
    out_specs = [pl.BlockSpec((tile, w), lambda i: (i, 0)) for w, _ in outs]
    out_specs += [pl.BlockSpec(s, lambda i, nd=len(s): (0,) * nd) for s in accs]
    return _call(body, (*arrs, *consts), name=name, grid=(t_rows // tile,),
                 in_specs=list(specs) + [_const_spec(c) for c in consts],
                 out_specs=out_specs, out_shape=out_shape, sem=("arbitrary",), job=job)


def _rowwise_vjp(fn, rows, consts, cts, *, name, groups, tile=None, gdtypes=None, job=None):
    t_rows = (rows[0][0] if isinstance(rows[0], tuple) else rows[0]).shape[0]
    tile = min(tile or ROW_TILE, t_rows)
    arrs, specs = zip(*[_row_operand(a, tile) for a in rows])
    flat_cts = [c for group in cts for c in group]
    ct_arrs, ct_specs = zip(*[_row_operand(a, tile) for a in flat_cts])
    nr, nc, nct, ng = len(rows), len(consts), len(flat_cts), len(groups)

    def width(a):
        return a[1] if isinstance(a, tuple) else a.shape[1]

    def body(*refs):
        rv = [r[...].astype(F32) for r in refs[:nr]]
        cv = [r[...] for r in refs[nr:nr + nc]]
        ct_refs = refs[nr + nc:nr + nc + nct]
        ctv, pos = [], 0
        for group in cts:
            s = ct_refs[pos][...].astype(F32)
            for r in ct_refs[pos + 1:pos + len(group)]:
                s = s + r[...].astype(F32)
            ctv.append(s)
            pos += len(group)
        _, pull = jax.vjp(fn, *rv, *cv)
        grads = pull(tuple(ctv))
        g_refs = refs[nr + nc + nct:nr + nc + nct + ng]
        for r, idx in zip(g_refs, groups):
            parts = [grads[i] for i in idx]
            r[...] = (parts[0] if len(parts) == 1 else jnp.concatenate(parts, axis=1)).astype(r.dtype)
        c_refs = refs[nr + nc + nct + ng:]

        @pl.when(pl.program_id(0) == 0)
        def _():
            for r in c_refs:
                r[...] = jnp.zeros_like(r)

        for r, v in zip(c_refs, grads[nr:]):
            r[...] += v

    gw = [sum(width(rows[i]) for i in idx) for idx in groups]
    gdtypes = gdtypes or [F32] * ng
    out_shape = [jax.ShapeDtypeStruct((t_rows, w), dt) for w, dt in zip(gw, gdtypes)]
    out_shape += [jax.ShapeDtypeStruct(c.shape, F32) for c in consts]
    out_specs = [pl.BlockSpec((tile, w), lambda i: (i, 0)) for w in gw]
    out_specs += [_const_spec(c) for c in consts]
    return _call(body, (*arrs, *consts, *ct_arrs), name=name, grid=(t_rows // tile,),
                 in_specs=list(specs) + [_const_spec(c) for c in consts] + list(ct_specs),
                 out_specs=out_specs, out_shape=out_shape, sem=("arbitrary",), job=job)


def _layer_norm(x, g, b):
    mu = jnp.mean(x, axis=-1, keepdims=True)
    xc = x - mu
    var = jnp.mean(xc * xc, axis=-1, keepdims=True)
    return xc * lax.rsqrt(var + LN_EPS) * g + b


def _sigmoid(x):
    return 1.0 / (1.0 + jnp.exp(-x))


def _fn_ln(x, g, b):
    return (_layer_norm(x, g, b),)


def _fn_rms(x, g):
    return (x * lax.rsqrt(jnp.mean(x * x, axis=-1, keepdims=True) + RMS_EPS) * g,)


def _make_post_mix(alpha):
    def fn(h, mix, g, b):
        return (_layer_norm(alpha * h + mix, g, b),)
    return fn


def _make_ple_ln(alpha):
    def fn(h1, ffn, pg, pp, g, b):
        return (_layer_norm(alpha * h1 + ffn + _sigmoid(pg) * pp, g, b),)
    return fn


def _fn_lower_bounds(l0, l1):
    m = jnp.maximum(l0, l1)
    e0, e1 = jnp.exp(l0 - m), jnp.exp(l1 - m)
    s = e0 + e1
    p0, p1 = e0 / s, e1 / s
    return (p0 - p0, (p0 + p1) - p0)


def _split_dot(x, e_bf16):
    hi = x.astype(BF16)
    lo = (x - hi.astype(F32)).astype(BF16)
    return (jnp.dot(hi, e_bf16, preferred_element_type=F32) + jnp.dot(lo, e_bf16, preferred_element_type=F32))


def _hgrn_common(th):
    rm = lax.broadcasted_iota(jnp.int32, (th, HG_W), 0) % HG_CHUNK

    def seg_cumsum(x):
        for s in (1, 2, 4, 8):
            x = x + jnp.where(rm >= s, pltpu.roll(x, s, 0), 0.0)
        return x

    def seg_rcumsum(x):
        for s in (1, 2, 4, 8):
            x = x + jnp.where(rm < HG_CHUNK - s, pltpu.roll(x, th - s, 0), 0.0)
        return x

    ri = lax.broadcasted_iota(jnp.int32, (HG_W, HG_W), 0) // HEAD
    ci = lax.broadcasted_iota(jnp.int32, (HG_W, HG_W), 1) // HEAD
    head_f32 = (ri == ci).astype(F32)
    head_bf16 = head_f32.astype(BF16)

    def headsum(x, pieces=2):
        if pieces == 1:
            return jnp.dot(x.astype(BF16), head_bf16, preferred_element_type=F32)
        return _split_dot(x, head_bf16)

    return rm, seg_cumsum, seg_rcumsum, head_f32, headsum


def _hgrn_gates(qr, fl, lb):
    sg = _sigmoid(fl)
    f = lb + (1.0 - lb) * sg
    sq = _sigmoid(qr)
    return sg, f, jnp.log(f), 1.0 - f, qr * sq, sq


def _shifted(x, d, th):
    return x if d == 0 else pltpu.roll(x, d, 0)


def _unshift(x, d, th):
    return x if d == 0 else pltpu.roll(x, th - d, 0)


def _hgrn_fwd(projp, lb, ng, *, name, job=None):
    t_rows = projp.shape[0]
    th = min(HG_TILE, t_rows)
    nct = th // HG_CHUNK

    def body(q_ref, f_ref, i_ref, g_ref, lb_ref, ng_ref, oa_ref, opre_ref, st_out_ref,
             st_ref, vtm_ref, kv_ref, qe_ref, dec_ref, oint_ref):
        rm, seg_cumsum, seg_rcumsum, head_f32, headsum = _hgrn_common(th)

        @pl.when(pl.program_id(0) == 0)
        def _():
            st_ref[...] = jnp.zeros_like(st_ref)

        qr, fl, v, g = q_ref[...], f_ref[...], i_ref[...], g_ref[...]
        _, f, lf, k, q, _ = _hgrn_gates(qr, fl, lb_ref[...])
        b = seg_cumsum(lf)

        o = jnp.zeros((th, HG_W), F32)
        for d in range(HG_CHUNK):
            kd, bd, vd = _shifted(k, d, th), _shifted(b, d, th), _shifted(v, d, th)
            e = jnp.exp(jnp.where(rm >= d, b - bd, -1e30))
            o = o + headsum(q * kd * e, 1) * vd

        blast = seg_rcumsum(jnp.where(rm == HG_CHUNK - 1, b, 0.0))
        kte = (k * jnp.exp(blast - b)).astype(BF16)
        qe_ref[...] = q * jnp.exp(b)
        dec_ref[...] = jnp.exp(blast)
        vt = v.T
        lane_chunk = lax.broadcasted_iota(jnp.int32, (HG_W, th), 1) // HG_CHUNK
        for c in range(nct):
            vtm_ref[c * HG_W:(c + 1) * HG_W, :] = jnp.where(lane_chunk == c, vt, 0.0).astype(BF16)
        kv_ref[...] = jnp.dot(vtm_ref[...], kte, preferred_element_type=F32)

        s = st_ref[...]
        for c in range(nct):
            rows = slice(c * HG_CHUNK, (c + 1) * HG_CHUNK)
            st_out_ref[c] = s
            oint_ref[rows, :] = lax.dot_general(qe_ref[rows, :].astype(BF16), s.astype(BF16),
                                                (((1,), (1,)), ((), ())), preferred_element_type=F32)
            dec = jnp.max(dec_ref[rows, :], axis=0, keepdims=True)
            s = s * dec + kv_ref[c * HG_W:(c + 1) * HG_W, :] * head_f32
        st_ref[...] = s

        o = o + oint_ref[...]
        opre_ref[...] = o
        r = lax.rsqrt(headsum(o * o) * (1.0 / HEAD) + RMS_EPS)
        oa_ref[...] = (o * r * ng_ref[...] * (g * _sigmoid(g))).astype(oa_ref.dtype)

    col = lambda j: pl.BlockSpec((th, HG_W), lambda i, j=j: (i, j))
    vec = pl.BlockSpec((1, HG_W), lambda i: (0, 0))
    row = pl.BlockSpec((th, HG_W), lambda i: (i, 0))
    n_chunks = t_rows // HG_CHUNK
    return _call(
        body, (projp, projp, projp, projp, lb, ng), name=name, grid=(t_rows // th,),
        in_specs=[col(0), col(1), col(2), col(3), vec, vec],
        out_specs=[row, row, pl.BlockSpec((nct, HG_W, HG_W), lambda i: (i, 0, 0))],
        out_shape=[jax.ShapeDtypeStruct((t_rows, HG_W), BF16), jax.ShapeDtypeStruct((t_rows, HG_W), F32),
                   jax.ShapeDtypeStruct((n_chunks, HG_W, HG_W), F32)],
        scratch_shapes=[pltpu.VMEM((HG_W, HG_W), F32), pltpu.VMEM((nct * HG_W, th), BF16),
                        pltpu.VMEM((nct * HG_W, HG_W), F32), pltpu.VMEM((th, HG_W), F32),
                        pltpu.VMEM((th, HG_W), F32), pltpu.VMEM((th, HG_W), F32)],
        sem=("arbitrary",), job=job)


def _hgrn_bwd(projp, lb, ng, opre, states, dcat, *, name):
    t_rows = projp.shape[0]
    th = min(HG_TILE, t_rows)
    nct = th // HG_CHUNK
    nt = t_rows // th

    def body(q_ref, f_ref, i_ref, g_ref, lb_ref, ng_ref, opre_ref, st_in_ref, do_ref,
             dproj_ref, dng_ref, dlb_ref,
             gst_ref, dotm_ref, qg_ref, v_ref, kte_ref, dop_ref, dec_ref, dkte_ref, dvi_ref, dqe_ref, ddec_ref):
        rm, seg_cumsum, seg_rcumsum, head_f32, headsum = _hgrn_common(th)

        @pl.when(pl.program_id(0) == 0)
        def _():
            gst_ref[...] = jnp.zeros_like(gst_ref)
            dng_ref[...] = jnp.zeros_like(dng_ref)
            dlb_ref[...] = jnp.zeros_like(dlb_ref)

        qr, fl, v, g = q_ref[...], f_ref[...], i_ref[...], g_ref[...]
        lb, ngv = lb_ref[...], ng_ref[...]
        sg, f, lf, k, q, sq = _hgrn_gates(qr, fl, lb)
        b = seg_cumsum(lf)
        blast = seg_rcumsum(jnp.where(rm == HG_CHUNK - 1, b, 0.0))
        eb = jnp.exp(b)
        ekb = jnp.exp(blast - b)
        qe, kte, dec = q * eb, k * ekb, jnp.exp(blast)

        do_out, op = do_ref[...], opre_ref[...]
        sgg = _sigmoid(g)
        sil = g * sgg
        r = lax.rsqrt(headsum(op * op) * (1.0 / HEAD) + RMS_EPS)
        on = op * r
        dng_ref[...] += jnp.sum(do_out * on * sil, axis=0, keepdims=True)
        dg = do_out * on * ngv * (sgg * (1.0 + g * (1.0 - sgg)))
        don = do_out * ngv * sil
        dop = r * (don - on * (headsum(don * on) * (1.0 / HEAD)))

        v_ref[...] = v
        kte_ref[...] = kte
        dop_ref[...] = dop
        dec_ref[...] = dec
        dot_t = dop.T
        lane_chunk = lax.broadcasted_iota(jnp.int32, (HG_W, th), 1) // HG_CHUNK
        for c in range(nct):
            dotm_ref[c * HG_W:(c + 1) * HG_W, :] = jnp.where(lane_chunk == c, dot_t, 0.0).astype(BF16)
        qg_ref[...] = jnp.dot(dotm_ref[...], qe.astype(BF16), preferred_element_type=F32)

        gs = gst_ref[...]
        for c in reversed(range(nct)):
            rows = slice(c * HG_CHUNK, (c + 1) * HG_CHUNK)
            s = st_in_ref[c]
            gm = (gs * head_f32).astype(BF16)
            dkte_ref[rows, :] = jnp.dot(v_ref[rows, :].astype(BF16), gm, preferred_element_type=F32)
            dvi_ref[rows, :] = lax.dot_general(kte_ref[rows, :].astype(BF16), gm, (((1,), (1,)), ((), ())),
                                               preferred_element_type=F32)
            dqe_ref[rows, :] = jnp.dot(dop_ref[rows, :].astype(BF16), s.astype(BF16), preferred_element_type=F32)
            ddec_ref[rows, :] = jnp.broadcast_to(jnp.sum(gs * s, axis=0, keepdims=True), (HG_CHUNK, HG_W))
            dec_c = jnp.max(dec_ref[rows, :], axis=0, keepdims=True)
            gs = gs * dec_c + qg_ref[c * HG_W:(c + 1) * HG_W, :] * head_f32
        gst_ref[...] = gs

        dkte, dqe = dkte_ref[...], dqe_ref[...]
        dq = dqe * eb
        dk = dkte * ekb
        db = dqe * qe - dkte * kte
        dv = dvi_ref[...]
        dblast = dkte * kte + jnp.where(rm == HG_CHUNK - 1, ddec_ref[...] * dec, 0.0)

        for d in range(HG_CHUNK):
            kd, bd, vd = _shifted(k, d, th), _shifted(b, d, th), _shifted(v, d, th)
            e = jnp.exp(jnp.where(rm >= d, b - bd, -1e30))
            p = q * kd * e
            sc = headsum(p, 1)
            dsc = headsum(dop * vd, 1)
            dv = dv + _unshift(sc * dop, d, th)
            dq = dq + dsc * kd * e
            dk = dk + _unshift(dsc * q * e, d, th)
            darg = dsc * p
            db = db + darg - _unshift(darg, d, th)

        db = db + jnp.where(rm == HG_CHUNK - 1, seg_cumsum(dblast), 0.0)
        dlf = seg_rcumsum(db)
        df = dlf / f - dk
        dlb_ref[...] += jnp.sum(df * (1.0 - sg), axis=0, keepdims=True)
        dfl = df * (1.0 - lb) * sg * (1.0 - sg)
        dqr = dq * (sq * (1.0 + qr * (1.0 - sq)))
        dproj_ref[...] = jnp.concatenate([dqr, dfl, dv, dg], axis=1).astype(dproj_ref.dtype)

    rev = lambda i: nt - 1 - i
    col = lambda j: pl.BlockSpec((th, HG_W), lambda i, j=j: (rev(i), j))
    vec = pl.BlockSpec((1, HG_W), lambda i: (0, 0))
    row = pl.BlockSpec((th, HG_W), lambda i: (rev(i), 0))
    tile_f32 = pltpu.VMEM((th, HG_W), F32)
    return pl.pallas_call(
        body, name=name, grid=(nt,),
        in_specs=[col(0), col(1), col(2), col(3), vec, vec, row,
                  pl.BlockSpec((nct, HG_W, HG_W), lambda i: (rev(i), 0, 0)), col(0)],
        out_specs=[pl.BlockSpec((th, 4 * HG_W), lambda i: (rev(i), 0)), vec, vec],
        out_shape=[jax.ShapeDtypeStruct((t_rows, 4 * HG_W), BF16), jax.ShapeDtypeStruct((1, HG_W), F32),
                   jax.ShapeDtypeStruct((1, HG_W), F32)],
        scratch_shapes=[pltpu.VMEM((HG_W, HG_W), F32), pltpu.VMEM((nct * HG_W, th), BF16),
                        pltpu.VMEM((nct * HG_W, HG_W), F32)] + [tile_f32] * 8,
        compiler_params=_cparams(("arbitrary",)),
    )(projp, projp, projp, projp, lb, ng, opre, states, dcat)


_INV_SQRT2 = 1.0 / math.sqrt(2.0)
_INV_SQRT2PI = 1.0 / math.sqrt(2.0 * math.pi)


def _gelu(x):
    return 0.5 * x * (1.0 + lax.erf(x * _INV_SQRT2))


def _gelu_grad(x):
    return 0.5 * (1.0 + lax.erf(x * _INV_SQRT2)) + x * jnp.exp(-0.5 * x * x) * _INV_SQRT2PI


def _sgu_parts(bu, bv, lg, lbias, w_ref, n_groups):
    c = SGU_CHUNK
    tril = (lax.broadcasted_iota(jnp.int32, (c, c), 0) >= lax.broadcasted_iota(jnp.int32, (c, c), 1)).astype(F32)
    gid = lax.broadcasted_iota(jnp.int32, bu.shape, 1) // HEAD
    u = _gelu(bu)
    gv = _gelu(bv)
    mu = jnp.mean(gv, axis=-1, keepdims=True)
    xc = gv - mu
    rstd = lax.rsqrt(jnp.mean(xc * xc, axis=-1, keepdims=True) + LN_EPS)
    xhat = xc * rstd
    vn = xhat * lg + lbias
    ws = [w_ref[gi] * tril for gi in range(n_groups)]
    return tril, gid, u, rstd, xhat, vn, ws


def _sgu_fwd(projp, lg, lbias, w_s, bias_full, *, name, job=None):
    t_rows = projp.shape[0]
    n_groups = w_s.shape[0]
    c = SGU_CHUNK
    rows_per_step = min(SGU_STEP_CHUNKS * c, t_rows)

    def body(u_ref, v_ref, lg_ref, lb_ref, w_ref, bias_ref, o_ref):
        for r0 in range(0, rows_per_step, c):
            rows = slice(r0, r0 + c)
            _, gid, u, _, _, vn, ws = _sgu_parts(u_ref[rows, :], v_ref[rows, :], lg_ref[...], lb_ref[...], w_ref,
                                                 n_groups)
            vnb = vn.astype(BF16)
            z = bias_ref[...]
            for gi in range(n_groups):
                z = z + jnp.where(gid == gi, jnp.dot(ws[gi].astype(BF16), vnb, preferred_element_type=F32), 0.0)
            o_ref[rows, :] = (u * z).astype(o_ref.dtype)

    col = lambda j: pl.BlockSpec((rows_per_step, HG_W), lambda i, j=j: (i, j))
    return _call(
        body, (projp, projp, lg, lbias, w_s, bias_full), name=name, grid=(t_rows // rows_per_step,),
        in_specs=[col(4), col(5), _const_spec(lg), _const_spec(lbias), _const_spec(w_s), _const_spec(bias_full)],
        out_specs=pl.BlockSpec((rows_per_step, HG_W), lambda i: (i, 0)),
        out_shape=jax.ShapeDtypeStruct((t_rows, HG_W), BF16), sem=("arbitrary",), job=job)


def _sgu_bwd(projp, lg, lbias, w_s, bias_full, dcat, *, name):
    t_rows = projp.shape[0]
    n_groups = w_s.shape[0]
    c = SGU_CHUNK
    rows_per_step = min(SGU_STEP_CHUNKS * c, t_rows)
    n = t_rows // rows_per_step

    def body(u_ref, v_ref, lg_ref, lb_ref, w_ref, bias_ref, do_ref,
             dproj_ref, dlg_ref, dlb_ref, dw_ref, dbs_ref, dbias_acc):
        i = pl.program_id(0)

        @pl.when(i == 0)
        def _():
            dlg_ref[...] = jnp.zeros_like(dlg_ref)
            dlb_ref[...] = jnp.zeros_like(dlb_ref)
            dw_ref[...] = jnp.zeros_like(dw_ref)
            dbias_acc[...] = jnp.zeros_like(dbias_acc)

        lg_v = lg_ref[...]
        for r0 in range(0, rows_per_step, c):
            rows = slice(r0, r0 + c)
            bu, bv = u_ref[rows, :], v_ref[rows, :]
            tril, gid, u, rstd, xhat, vn, ws = _sgu_parts(bu, bv, lg_v, lb_ref[...], w_ref, n_groups)
            vnb = vn.astype(BF16)
            z = bias_ref[...]
            for gi in range(n_groups):
                z = z + jnp.where(gid == gi, jnp.dot(ws[gi].astype(BF16), vnb, preferred_element_type=F32), 0.0)
            do = do_ref[rows, :]
            dbu = do * z * _gelu_grad(bu)
            dz = do * u
            dbias_acc[...] += dz
            dvn = jnp.zeros_like(dz)
            for gi in range(n_groups):
                dzg = jnp.where(gid == gi, dz, 0.0).astype(BF16)
                dw_ref[gi] += lax.dot_general(dzg, vnb, (((1,), (1,)), ((), ())), preferred_element_type=F32) * tril
                dvn = dvn + jnp.dot(ws[gi].T.astype(BF16), dzg, preferred_element_type=F32)
            dlg_ref[...] += jnp.sum(dvn * xhat, axis=0, keepdims=True)
            dlb_ref[...] += jnp.sum(dvn, axis=0, keepdims=True)
            dxh = dvn * lg_v
            dgv = rstd * (dxh - jnp.mean(dxh, axis=-1, keepdims=True)
                          - xhat * jnp.mean(dxh * xhat, axis=-1, keepdims=True))
            dproj_ref[rows, :] = jnp.concatenate([dbu, dgv * _gelu_grad(bv)], axis=1).astype(dproj_ref.dtype)

        @pl.when(i == n - 1)
        def _():
            dbs_ref[...] = jnp.sum(dbias_acc[...].T.reshape(n_groups, HEAD, c), axis=1)

    col = lambda j: pl.BlockSpec((rows_per_step, HG_W), lambda i, j=j: (i, j))
    return pl.pallas_call(
        body, name=name, grid=(n,),
        in_specs=[col(4), col(5), _const_spec(lg), _const_spec(lbias), _const_spec(w_s), _const_spec(bias_full),
                  col(1)],
        out_specs=[pl.BlockSpec((rows_per_step, 2 * HG_W), lambda i: (i, 0)), _const_spec(lg), _const_spec(lbias),
                   _const_spec(w_s), pl.BlockSpec((n_groups, c), lambda i: (0, 0))],
        out_shape=[jax.ShapeDtypeStruct((t_rows, 2 * HG_W), BF16), jax.ShapeDtypeStruct(lg.shape, F32),
                   jax.ShapeDtypeStruct(lbias.shape, F32), jax.ShapeDtypeStruct(w_s.shape, F32),
                   jax.ShapeDtypeStruct((n_groups, c), F32)],
        scratch_shapes=[pltpu.VMEM((c, HG_W), F32)],
        compiler_params=_cparams(("arbitrary",)),
    )(projp, projp, lg, lbias, w_s, bias_full, dcat)


def _rope_tables(positions):
    t = positions.shape[0]
    inv_freq = ROPE_THETA ** (-jnp.arange(0, 32, 2, dtype=F32) / 32)
    ang = positions.astype(F32)[:, None] * inv_freq
    cos, sin = jnp.cos(ang), jnp.sin(ang)
    z = lambda w: jnp.zeros((t, w), F32)
    cos_t = jnp.concatenate([jnp.ones((t, 64), F32), cos, cos, z(32)], axis=1)
    sin_up = jnp.concatenate([z(80), sin, z(32)], axis=1)
    sin_dn = jnp.concatenate([z(64), -sin, z(48)], axis=1)
    return cos_t, sin_up, sin_dn


def _rep(x, n):
    return x if n == 1 else jnp.concatenate([x] * n, axis=1)


def _rope(x, cos_t, sin_up, sin_dn):
    w = x.shape[1]
    return x * cos_t + pltpu.roll(x, 16, 1) * sin_up + pltpu.roll(x, w - 16, 1) * sin_dn


def _rope_t(dy, cos_t, sin_up, sin_dn):
    w = dy.shape[1]
    return dy * cos_t + pltpu.roll(dy * sin_up, w - 16, 1) + pltpu.roll(dy * sin_dn, 16, 1)


def _mla_prep(q, kv, projp, tables, *, name):
    nh = N_ATT_HEADS

    def fn(qv, kvv, kr, cos_t, sin_up, sin_dn):
        qr = _rope(qv, _rep(cos_t, nh), _rep(sin_up, nh), _rep(sin_dn, nh))
        krr = _rope(kr, cos_t, sin_up, sin_dn)
        lane = lax.broadcasted_iota(jnp.int32, kvv.shape, 1) % LANES
        return qr, jnp.where(lane < HEAD, kvv, 0.0) + _rep(krr, nh), kvv

    w = q.shape[1]
    return _rowwise(fn, [q, kv, (projp, LANES, P_KR // LANES)] + list(tables), [],
                    [(w, BF16), (w, BF16), (w, BF16)], name=name)


def _mla_prep_bwd(dqr, dkf, tables, *, name):
    nh = N_ATT_HEADS

    def fn(dq, dk, cos_t, sin_up, sin_dn):
        dqp = _rope_t(dq, _rep(cos_t, nh), _rep(sin_up, nh), _rep(sin_dn, nh))
        dkrr = dk[:, 0:LANES]
        for h in range(1, nh):
            dkrr = dkrr + dk[:, LANES * h:LANES * (h + 1)]
        return dqp, _rope_t(dkrr, cos_t, sin_up, sin_dn)

    return _rowwise(fn, [dqr, dkf] + list(tables), [], [(dqr.shape[1], BF16), (LANES, BF16)], name=name)


_LOG2E = 1.0 / math.log(2.0)
_NT = (((1,), (1,)), ((), ()))
_TN = (((0,), (0,)), ((), ()))


def _attn_fwd(qr, kf, kvb, *, name, job=None):
    t_rows = qr.shape[0]
    tq = min(ATT_TQ, t_rows)
    nb = t_rows // tq
    scale = ATT_D ** -0.5

    c2 = scale * _LOG2E

    def body(q_ref, kf_ref, kvb_ref, o_ref, lse_ref):
        qi = pl.program_id(1)
        lane = lax.broadcasted_iota(jnp.int32, (tq, LANES), 1)
        causal_t = (lax.broadcasted_iota(jnp.int32, (tq, tq), 0) <= lax.broadcasted_iota(jnp.int32, (tq, tq), 1))
        heads = [slice(hh * LANES, (hh + 1) * LANES) for hh in range(2)]
        qs = [q_ref[:, cols] for cols in heads]

        def block(first, n_keys, carry, diagonal):
            rows = pl.ds(pl.multiple_of(first * tq, tq), n_keys)
            new = []
            for q, cols, (m_old, l_old, acc_t) in zip(qs, heads, carry):
                s_t = lax.dot_general(kf_ref[rows, cols], q, _NT, preferred_element_type=F32)
                if diagonal:
                    s_t = jnp.where(causal_t, s_t, -1e30)
                m_new = jnp.maximum(m_old, jnp.max(s_t, axis=0, keepdims=True))
                p_t = jnp.exp2((s_t - m_new) * c2)
                a = jnp.exp2((m_old - m_new) * c2)
                pv_t = lax.dot_general(kvb_ref[rows, cols], p_t.astype(BF16), _TN, preferred_element_type=F32)
                new.append((m_new, a * l_old + jnp.sum(p_t, axis=0, keepdims=True), a * acc_t + pv_t))
            return tuple(new)

        init = (jnp.full((1, tq), -1e30, F32), jnp.zeros((1, tq), F32), jnp.zeros((LANES, tq), F32))
        carry = lax.fori_loop(0, qi // 4, lambda g, c: block(4 * g, 4 * tq, c, False), (init, init))
        carry = lax.cond((qi // 2) % 2 == 1, lambda c: block(4 * (qi // 4), 2 * tq, c, False), lambda c: c, carry)
        carry = lax.cond(qi % 2 == 1, lambda c: block(qi - 1, tq, c, False), lambda c: c, carry)
        outs = []
        for hh, (m_fin, l_fin, acc_t) in enumerate(block(qi, tq, carry, True)):
            lse_ref[hh] = m_fin * scale + jnp.log(l_fin)
            outs.append((acc_t / l_fin).T)
        o_ref[...] = jnp.where(lane < HEAD, pltpu.roll(outs[0], HEAD, 1), outs[1])

    pair = pl.BlockSpec((t_rows, 2 * LANES), lambda pr, qi: (0, pr))
    return _call(
        body, (qr, kf, kvb), name=name, grid=(N_ATT_HEADS // 2, nb),
        in_specs=[pl.BlockSpec((tq, 2 * LANES), lambda pr, qi: (qi, pr)), pair, pair],
        out_specs=[pl.BlockSpec((tq, LANES), lambda pr, qi: (qi, pr)),
                   pl.BlockSpec((2, 1, tq), lambda pr, qi: (pr, 0, qi))],
        out_shape=[jax.ShapeDtypeStruct((t_rows, N_ATT_HEADS * HEAD), F32),
                   jax.ShapeDtypeStruct((N_ATT_HEADS, 1, t_rows), F32)],
        sem=("parallel", "arbitrary"), job=job)


def _attn_bwd(qr, kf, kvb, dcat, o, lse, *, name, job=None):
    t_rows = qr.shape[0]
    tq = min(ATT_TQ, t_rows)
    nb = t_rows // tq
    scale = ATT_D ** -0.5
    c2 = scale * _LOG2E
    do_off = 2 * HG_W // LANES

    def body(q_ref, kf_ref, kvb_ref, do_ref, o_ref, lse_ref, dq_ref, dkv_ref, dk_ref):
        ki = pl.program_id(1)

        @pl.when(ki == 0)
        def _():
            dq_ref[...] = jnp.zeros_like(dq_ref)

        lane = lax.broadcasted_iota(jnp.int32, (tq, LANES), 1)
        causal_t = (lax.broadcasted_iota(jnp.int32, (tq, tq), 0) <= lax.broadcasted_iota(jnp.int32, (tq, tq), 1))
        heads = [slice(hh * LANES, (hh + 1) * LANES) for hh in range(2)]
        ks = [kf_ref[:, cols] for cols in heads]
        vs = [kvb_ref[:, cols] for cols in heads]

        def block(qi, n_q, carry, diagonal):
            rows = pl.ds(pl.multiple_of(qi * tq, tq), n_q)
            do_pair, o_pair = do_ref[rows, :], o_ref[rows, :]
            upper = lax.broadcasted_iota(jnp.int32, do_pair.shape, 1) >= HEAD
            new = []
            for hh, (cols, k, v, (dk, dv)) in enumerate(zip(heads, ks, vs, carry)):
                q = q_ref[rows, cols]
                do, ov = (pltpu.roll(do_pair, HEAD, 1), pltpu.roll(o_pair, HEAD, 1)) if hh == 0 else (do_pair, o_pair)
                do = jnp.where(upper, do, 0.0)
                delta = jnp.sum((do * ov).T, axis=0, keepdims=True)
                s_t = lax.dot_general(k, q, _NT, preferred_element_type=F32)
                if diagonal:
                    s_t = jnp.where(causal_t, s_t, -1e30)
                p_t = jnp.exp2(s_t * c2 - lse_ref[hh, :, rows] * _LOG2E)
                dob = do.astype(BF16)
                dv = dv + jnp.dot(p_t.astype(BF16), dob, preferred_element_type=F32)
                dp_t = lax.dot_general(v, dob, _NT, preferred_element_type=F32)
                ds_t = (p_t * (dp_t - delta) * scale).astype(BF16)
                dk = dk + jnp.dot(ds_t, q, preferred_element_type=F32)
                dq_ref[rows, cols] += lax.dot_general(ds_t, k, _TN, preferred_element_type=F32)
                new.append((dk, dv))
            return tuple(new)

        zero = jnp.zeros((tq, LANES), F32)
        carry = block(ki, tq, ((zero, zero), (zero, zero)), True)
        rest = nb - 1 - ki
        carry = lax.fori_loop(0, rest // 2, lambda g, c: block(ki + 1 + 2 * g, 2 * tq, c, False), carry)
        carry = lax.cond(rest % 2 == 1, lambda c: block(nb - 1, tq, c, False), lambda c: c, carry)
        dkv_ref[...] = jnp.concatenate([jnp.where(lane < HEAD, dk, dv) for dk, dv in carry],
                                       axis=1).astype(dkv_ref.dtype)
        dk_ref[...] = jnp.concatenate([dk for dk, _ in carry], axis=1)

    pair_all = pl.BlockSpec((t_rows, 2 * LANES), lambda pr, ki: (0, pr))
    pair_blk = pl.BlockSpec((tq, 2 * LANES), lambda pr, ki: (ki, pr))
    wide = jax.ShapeDtypeStruct((t_rows, N_ATT_HEADS * LANES), F32)
    return _call(
        body, (qr, kf, kvb, dcat, o, lse), name=name, grid=(N_ATT_HEADS // 2, nb),
        in_specs=[pair_all, pair_blk, pair_blk,
                  pl.BlockSpec((t_rows, LANES), lambda pr, ki: (0, do_off + pr)),
                  pl.BlockSpec((t_rows, LANES), lambda pr, ki: (0, pr)),
                  pl.BlockSpec((2, 1, t_rows), lambda pr, ki: (pr, 0, 0))],
        out_specs=[pair_all, pair_blk, pair_blk],
        out_shape=[wide, jax.ShapeDtypeStruct(wide.shape, BF16), wide],
        sem=("parallel", "arbitrary"), job=job)


def _my_pos():
    return lax.axis_index("x"), lax.axis_index("y"), lax.axis_index("c")


def _all_gather(xs, *, name, columns=True):
    return _gather_forward(_run_job(_gather_job(xs, columns), name=name), name=name + "_forward")


def _remote(src, dst, send_sems, recv_sems, k, dev):
    return pltpu.make_async_remote_copy(src_ref=src, dst_ref=dst, send_sem=send_sems.at[k], recv_sem=recv_sems.at[k],
                                        device_id=dev, device_id_type=MESH)


def _block(ref, idx):
    if len(ref.shape) == 2:
        return ref.at[:, pl.ds(pl.multiple_of(idx * LANES, LANES), LANES)]
    return ref.at[idx]


def _gather_job(xs, columns=True):
    n = len(xs)

    def make(x_refs, out_refs, send_sems, recv_sems, local_sems):
        mx, my, mc = _my_pos()
        mine = 4 * mx + 2 * my + mc
        peers = [(mx, my, 1 - mc), (1 - mx, my, mc), (mx, 1 - my, mc), (1 - mx, 1 - my, mc)]
        sends, recvs, local = [], [], []
        for a in range(n):
            local.append(pltpu.make_async_copy(x_refs[a], _block(out_refs[a], mine), local_sems.at[a]))
            for k, dev in enumerate(peers):
                theirs = 4 * dev[0] + 2 * dev[1] + dev[2]
                sends.append(_remote(x_refs[a], _block(out_refs[a], mine), send_sems, recv_sems, 4 * a + k, dev))
                recvs.append(_remote(x_refs[a], _block(out_refs[a], theirs), send_sems, recv_sems, 4 * a + k, dev))
        return sends, recvs, local

    def gathered(x):
        if columns and x.ndim == 2 and x.shape[1] == LANES:
            return jax.ShapeDtypeStruct((x.shape[0], N_DEV * LANES), x.dtype)
        return jax.ShapeDtypeStruct((N_DEV,) + x.shape, x.dtype)

    return _copies_job(xs, [gathered(x) for x in xs], 4 * n, n, make)


def _forward_job(gs):
    n = len(gs)

    def make(in_refs, out_refs, send_sems, recv_sems, local_sems):
        mx, my, mc = _my_pos()
        chips = [(1 - mx, my), (mx, 1 - my), (1 - mx, 1 - my)]
        sends, recvs = [], []
        for a in range(n):
            for j, (cx, cy) in enumerate(chips):
                here = _block(out_refs[a], 4 * cx + 2 * cy + mc)
                there = _block(out_refs[a], 4 * cx + 2 * cy + 1 - mc)
                sends.append(_remote(here, here, send_sems, recv_sems, 3 * a + j, (mx, my, 1 - mc)))
                recvs.append(_remote(here, there, send_sems, recv_sems, 3 * a + j, (mx, my, 1 - mc)))
        return sends, recvs, []

    shapes = [jax.ShapeDtypeStruct(g.shape, g.dtype) for g in gs]
    return _copies_job(gs, shapes, 3 * n, 0, make, in_place=True)


def _gather_forward(gs, *, name):
    return _run_job(_forward_job(gs), name=name)


def _pair_job(xs):
    n = len(xs)

    def make(x_refs, out_refs, send_sems, recv_sems, local_sems):
        mx, my, mc = _my_pos()

        def src(ref, g):
            return _block(ref, 2 * g + 1 - mc) if len(ref.shape) == 2 else ref.at[g, 1 - mc]

        copies = [_remote(src(x_refs[a], g), out_refs[a].at[g], send_sems, recv_sems, 4 * a + g, (mx, my, 1 - mc))
                  for a in range(n) for g in range(4)]
        return copies, copies, []

    shapes = [jax.ShapeDtypeStruct((4, x.shape[0], LANES) if x.ndim == 2 else (4,) + x.shape[2:], x.dtype)
              for x in xs]
    return _copies_job(xs, shapes, 4 * n, 0, make)


def _pair_add(x, r, core, *, name):
    _, a, b = r.shape
    ta = _row_tile(a, 512)

    def body(c_ref, x_ref, r_ref, o_ref):
        o_ref[...] = (x_ref[...] + r_ref[...]).astype(o_ref.dtype)

    blk = pl.BlockSpec((None, ta, b), lambda g, i, c_ref: (g, i, 0))
    own = (pl.BlockSpec((ta, b), lambda g, i, c_ref: (i, 2 * g + c_ref[0])) if x.ndim == 2
           else pl.BlockSpec((None, None, ta, b), lambda g, i, c_ref: (g, c_ref[0], i, 0)))
    return pl.pallas_call(
        body, name=name,
        grid_spec=pltpu.PrefetchScalarGridSpec(
            num_scalar_prefetch=1, grid=(4, a // ta), in_specs=[own, blk], out_specs=blk),
        out_shape=jax.ShapeDtypeStruct((4, a, b), BF16),
        compiler_params=_cparams(("parallel", "parallel")),
    )(core, x, r)


def _quad_job(xs):
    n = len(xs)

    def make(x_refs, out_refs, send_sems, recv_sems, local_sems):
        mx, my, mc = _my_pos()
        mine = 2 * mx + my
        peers = [((1 - mx, my, mc), 2 * (1 - mx) + my), ((mx, 1 - my, mc), 2 * mx + 1 - my),
                 ((1 - mx, 1 - my, mc), 2 * (1 - mx) + 1 - my)]
        sends, recvs, local = [], [], []
        for a in range(n):
            local.append(pltpu.make_async_copy(x_refs[a].at[mine], out_refs[a].at[mine], local_sems.at[a]))
            for k, (dev, g) in enumerate(peers):
                sends.append(_remote(x_refs[a].at[g], out_refs[a].at[mine], send_sems, recv_sems, 3 * a + k, dev))
                recvs.append(_remote(x_refs[a].at[g], out_refs[a].at[g], send_sems, recv_sems, 3 * a + k, dev))
        return sends, recvs, local

    shapes = [jax.ShapeDtypeStruct(x.shape, x.dtype) for x in xs]
    return _copies_job(xs, shapes, 3 * n, n, make)


def _row_tile(r, pref):
    t = min(pref, r)
    while r % t or (t % 8 and t != r):
        t -= 1
    return t


def _adamw(parts, w, m, v, layer, *, name, tile=512, into=None):
    g, a, b = parts.shape
    tile = _row_tile(a, tile)
    c1 = 1.0 / (1.0 - ADAM_B1 ** ADAM_STEP)
    c2 = 1.0 / (1.0 - ADAM_B2 ** ADAM_STEP)
    into = tuple(into or ())

    def body(p_ref, w_ref, m_ref, v_ref, *refs):
        g_ref, d_ref, mo_ref, vo_ref = refs[len(into):]
        grad = p_ref[0].astype(F32)
        for j in range(1, g):
            grad = grad + p_ref[j].astype(F32)
        mn = ADAM_B1 * m_ref[...] + (1.0 - ADAM_B1) * grad
        vn = ADAM_B2 * v_ref[...] + (1.0 - ADAM_B2) * (grad * grad)
        g_ref[...] = grad
        mo_ref[...] = mn
        vo_ref[...] = vn
        d_ref[...] = -ADAM_LR * ((mn * c1) / (jnp.sqrt(vn * c2) + ADAM_EPS) + ADAM_WD * w_ref[...])

    if layer is None:
        src, shape = pl.BlockSpec((tile, b), lambda i: (i, 0)), (a, b)
    else:
        src, shape = pl.BlockSpec((None, tile, b), lambda i: (layer, i, 0)), w.shape
    return pl.pallas_call(
        body, name=name, grid=(a // tile,),
        in_specs=[pl.BlockSpec((g, tile, b), lambda i: (0, i, 0)), src, src, src] + [_ANY] * len(into),
        out_specs=[src] * 4,
        out_shape=[jax.ShapeDtypeStruct(shape, F32)] * 4,
        input_output_aliases={4 + i: i for i in range(len(into))},
        compiler_params=_cparams(("parallel",)),
    )(parts, w, m, v, *into)


W_IN_SHARD = 276


def _w_in_dest(col):
    return jnp.where(col < P_KR, col, jnp.where(col < P_KR + 256, col + (P_CKV - P_KR), col - 2176 + P_KR + HEAD))


PLACE_TILE = 384
PLACE_SHARDS = 3
PICK_TILE = 128
PICK_TILES = 4


def _w_in_tables():
    col = np.arange(N_DEV * W_IN_SHARD)
    dest = np.where(col < P_KR, col, np.where(col < P_KR + 256, col + (P_CKV - P_KR), col - 2176 + P_KR + HEAD))
    shard = col // W_IN_SHARD

    def filled(used, universe, n):
        used = sorted(set(int(u) for u in used))
        assert len(used) <= n, used
        return used + [u for u in universe if u not in used][:n - len(used)]

    place = [filled(shard[dest // PLACE_TILE == c], range(N_DEV), PLACE_SHARDS) for c in range(P_COLS // PLACE_TILE)]
    pick = [filled(dest[shard == j] // PICK_TILE, range(P_COLS // PICK_TILE), PICK_TILES) for j in range(N_DEV)]
    return np.asarray(place, np.int32).reshape(-1), np.asarray(pick, np.int32).reshape(-1)


def _place_w_in(g, *, name):
    _, d, sh = g.shape
    tc, ns = PLACE_TILE, PLACE_SHARDS
    table = jnp.asarray(_w_in_tables()[0])

    def body(tab_ref, g_ref, o_ref, acc_ref):
        ct, s = pl.program_id(0), pl.program_id(1)
        j = tab_ref[ct * ns + s]

        @pl.when(s == 0)
        def _():
            acc_ref[...] = jnp.zeros_like(acc_ref)

        src = j * sh + lax.broadcasted_iota(jnp.int32, (sh, tc), 0)
        dst = ct * tc + lax.broadcasted_iota(jnp.int32, (sh, tc), 1)
        place = (_w_in_dest(src) == dst).astype(BF16)
        acc_ref[...] += jnp.dot(g_ref[...], place, preferred_element_type=F32)

        @pl.when(s == ns - 1)
        def _():
            o_ref[...] = acc_ref[...].astype(o_ref.dtype)

    return pl.pallas_call(
        body, name=name,
        grid_spec=pltpu.PrefetchScalarGridSpec(
            num_scalar_prefetch=1, grid=(P_COLS // tc, ns),
            in_specs=[pl.BlockSpec((None, d, sh), lambda ct, s, tab: (tab[ct * ns + s], 0, 0))],
            out_specs=pl.BlockSpec((d, tc), lambda ct, s, tab: (0, ct)),
            scratch_shapes=[pltpu.VMEM((d, tc), F32)]),
        out_shape=jax.ShapeDtypeStruct((d, P_COLS), BF16),
        compiler_params=_cparams(("parallel", "arbitrary")),
    )(table, g)


def _unplace_w_in(dw, *, name):
    d = dw.shape[0]
    sh, tk, nt = W_IN_SHARD, PICK_TILE, PICK_TILES
    table = jnp.asarray(_w_in_tables()[1])

    def body(tab_ref, dw_ref, o_ref):
        j, kk = pl.program_id(0), pl.program_id(1)
        tile = tab_ref[j * nt + kk]
        src = j * sh + lax.broadcasted_iota(jnp.int32, (tk, sh), 1)
        dst = tile * tk + lax.broadcasted_iota(jnp.int32, (tk, sh), 0)
        pick = (_w_in_dest(src) == dst).astype(BF16)
        part = _split_dot(dw_ref[...], pick)

        @pl.when(kk == 0)
        def _():
            o_ref[...] = part

        @pl.when(kk > 0)
        def _():
            o_ref[...] += part

    return pl.pallas_call(
        body, name=name,
        grid_spec=pltpu.PrefetchScalarGridSpec(
            num_scalar_prefetch=1, grid=(N_DEV, nt),
            in_specs=[pl.BlockSpec((d, tk), lambda j, kk, tab: (0, tab[j * nt + kk]))],
            out_specs=pl.BlockSpec((None, d, sh), lambda j, kk, tab: (j, 0, 0))),
        out_shape=jax.ShapeDtypeStruct((N_DEV, d, sh), F32),
        compiler_params=_cparams(("parallel", "arbitrary")),
    )(table, dw)


def _gate_up_swiglu(h1, wgu, *, name):
    t_rows, k = h1.shape
    w = wgu.shape[2]
    tm = _tile(t_rows, 1024)

    def body(a_ref, wg_ref, wu_ref, gu_ref, act_ref):
        a = a_ref[...].astype(BF16)
        gate = jnp.dot(a, wg_ref[...], preferred_element_type=F32)
        up = jnp.dot(a, wu_ref[...], preferred_element_type=F32)
        gu_ref[0] = gate.astype(gu_ref.dtype)
        gu_ref[1] = up.astype(gu_ref.dtype)
        act_ref[...] = (gate * _sigmoid(gate) * up).astype(act_ref.dtype)

    return pl.pallas_call(
        body, name=name, grid=(t_rows // tm, 4),
        in_specs=[pl.BlockSpec((tm, k), lambda i, j: (i, 0)),
                  pl.BlockSpec((None, k, w), lambda i, j: (j, 0, 0)),
                  pl.BlockSpec((None, k, w), lambda i, j: (j + 4, 0, 0))],
        out_specs=[pl.BlockSpec((2, None, tm, w), lambda i, j: (0, j, i, 0)),
                   pl.BlockSpec((None, tm, w), lambda i, j: (j, i, 0))],
        out_shape=[jax.ShapeDtypeStruct((2, 4, t_rows, w), BF16), jax.ShapeDtypeStruct((4, t_rows, w), BF16)],
        compiler_params=_cparams(("parallel", "arbitrary")),
    )(h1, wgu, wgu)


def _down_dx_swiglu(dffn, wdown, gu, *, name):
    t_rows, k = dffn.shape
    w = gu.shape[3]
    tm = _tile(t_rows, 1024)

    def body(d_ref, w_ref, gu_ref, o_ref):
        dact = lax.dot_general(d_ref[...].astype(BF16), w_ref[...], _NT, preferred_element_type=F32)
        gate, up = gu_ref[0].astype(F32), gu_ref[1].astype(F32)
        sg = _sigmoid(gate)
        silu = gate * sg
        o_ref[0] = (dact * up * (sg + silu - silu * sg)).astype(o_ref.dtype)
        o_ref[1] = (dact * silu).astype(o_ref.dtype)

    blk = pl.BlockSpec((2, None, tm, w), lambda i, j: (0, j, i, 0))
    return pl.pallas_call(
        body, name=name, grid=(t_rows // tm, 4),
        in_specs=[pl.BlockSpec((tm, k), lambda i, j: (i, 0)), pl.BlockSpec((w, k), lambda i, j: (j, 0)), blk],
        out_specs=blk, out_shape=jax.ShapeDtypeStruct(gu.shape, BF16),
        compiler_params=_cparams(("parallel", "arbitrary")),
    )(dffn, wdown, gu)


BIG = ("w_in", "mla_w_uq", "mla_w_ukv", "w_out", "w_gate_up", "w_down", "ple_w_gate", "ple_w_proj")
SMALL = ("ln_in_g", "ln_in_b", "hgrn_lb_logits", "hgrn_norm_g", "sgu_ln_g", "sgu_ln_b", "sgu_w_s", "sgu_b_s",
         "mla_q_norm_g", "mla_kv_norm_g", "ln1_g", "ln1_b", "ln2_g", "ln2_b")
ORDER = ("ln_in_g", "ln_in_b", "w_in", "hgrn_lb_logits", "hgrn_norm_g", "sgu_ln_g", "sgu_ln_b", "sgu_w_s", "sgu_b_s",
         "mla_q_norm_g", "mla_w_uq", "mla_kv_norm_g", "mla_w_ukv", "w_out", "ln1_g", "ln1_b", "w_gate_up", "w_down",
         "ple_w_gate", "ple_w_proj", "ln2_g", "ln2_b")


def _slab(a, align):
    s = a.reshape(-1, LANES)
    pad = -s.shape[0] % align
    return jnp.pad(s, ((0, pad), (0, 0))) if pad else s


def _pack(arrays, align=16, total_align=512):
    s = jnp.concatenate([_slab(a, align) for a in arrays], axis=0)
    pad = -s.shape[0] % total_align
    return jnp.pad(s, ((0, pad), (0, 0))) if pad else s


def _unpack(slab, shapes, align=16):
    out, r0 = [], 0
    for s in shapes:
        nr = math.prod(s) // LANES
        out.append(slab[r0:r0 + nr].reshape(s))
        r0 += nr + (-nr % align)
    return out


def _weight_shards(w, li):
    uq_pad = ((0, 0), (0, LANES - ATT_D))
    shards = {k: w[k][li] for k in BIG}
    shards["mla_w_uq"] = jnp.pad(shards["mla_w_uq"], uq_pad)
    return {k: s.astype(BF16) for k, s in shards.items()}


def _usable_weights(g, *, name):
    out = {}
    for k, a in g.items():
        if k == "w_in":
            out[k] = _place_w_in(a, name=name + "_place_w_in")
        elif k in ("w_out", "w_down", "ple_w_gate"):
            out[k] = a.reshape(a.shape[0] * a.shape[1], a.shape[2])
        else:
            out[k] = a
    return out


BY_COLUMNS = ("mla_w_uq", "mla_w_ukv", "ple_w_proj")


def _as_pairs(k, g):
    if k in BY_COLUMNS:
        return g
    if g.ndim == 2:
        return g.reshape((4, 2, g.shape[0] // N_DEV) + g.shape[1:])
    return g.reshape((4, 2) + g.shape[1:])


def _twice(fn):
    return lambda *a: fn(*a) * 2


def _layer_forward(li, h, hb, p_i, wts, sm, lbs, tables, alpha, hgrn_job=None, after_hgrn=None, attn_job=None,
                   after_attn=None, loss_target=None):
    n = f"l{li}_"
    row1 = lambda a: a.reshape(1, -1)
    projp = _mm(hb, wts["w_in"], name=n + "proj")
    ng = row1(sm["hgrn_norm_g"][li])
    res = _hgrn_fwd(projp, lbs[li], ng, name=n + "hgrn_fwd", job=hgrn_job)
    if hgrn_job is not None:
        res, got = res
    o_a, o_pre, states = res
    lg, lbias = row1(sm["sgu_ln_g"][li]), row1(sm["sgu_ln_b"][li])
    w_s = sm["sgu_w_s"][li]
    bias_full = jnp.repeat(sm["sgu_b_s"][li].T, HEAD, axis=1)
    o_b = _sgu_fwd(projp, lg, lbias, w_s, bias_full, name=n + "sgu_fwd",
                   job=None if hgrn_job is None else _forward_job(got))
    if hgrn_job is not None:
        o_b, got = o_b
        wts = dict(wts, **after_hgrn(got))
    qg, kvg = row1(sm["mla_q_norm_g"][li]), row1(sm["mla_kv_norm_g"][li])
    cq_view, ckv_view = (projp, 384, P_CQ // 384), (projp, 256, P_CKV // 256)
    (cqn,) = _rowwise(_fn_rms, [cq_view], [qg], [(384, BF16)], name=n + "q_norm")
    (ckvn,) = _rowwise(_fn_rms, [ckv_view], [kvg], [(256, BF16)], name=n + "kv_norm")
    q = _mm(cqn, wts["mla_w_uq"], name=n + "uq")
    kv = _mm(ckvn, wts["mla_w_ukv"], name=n + "ukv")
    qr, kf, kvb = _mla_prep(q, kv, projp, tables, name=n + "mla_prep")
    res = _attn_fwd(qr, kf, kvb, name=n + "attn_fwd", job=attn_job)
    if attn_job is not None:
        res, got = res
    o_c, lse = res
    cat = jnp.concatenate([o_a, o_b, o_c.astype(BF16)], axis=1)
    mix = _mm(cat, wts["w_out"], name=n + "out_proj", job=None if attn_job is None else _forward_job(got))
    if attn_job is not None:
        mix, got = mix
        wts = dict(wts, **after_attn(got))
    g1, b1 = row1(sm["ln1_g"][li]), row1(sm["ln1_b"][li])
    d = h.shape[1]
    h1, h1b = _rowwise(_twice(_make_post_mix(alpha)), [h, mix], [g1, b1], [(d, F32), (d, BF16)], name=n + "ln1")
    gu, act = _gate_up_swiglu(h1b, wts["w_gate_up"], name=n + "gate_up")
    ffn = _mm_kblocks(act, wts["w_down"], bm="kn", tm=1024, name=n + "down")
    pg = _mm(h1b, wts["ple_w_gate"], name=n + "ple_gate")
    pp = _mm(p_i, wts["ple_w_proj"], name=n + "ple_proj")
    g2, b2 = row1(sm["ln2_g"][li]), row1(sm["ln2_b"][li])
    if loss_target is None:
        out = _rowwise(_twice(_make_ple_ln(alpha)), [h1, ffn, pg, pp], [g2, b2], [(d, F32), (d, BF16)],
                       name=n + "ln2")
    else:
        def ln_and_loss(h1v, ffnv, pgv, ppv, tv, gv, bv):
            err = _make_ple_ln(alpha)(h1v, ffnv, pgv, ppv, gv, bv)[0] - tv
            return err * (1.0 / d), 0.5 * jnp.sum(jnp.mean(err * err, axis=-1, keepdims=True), axis=0, keepdims=True)

        out = _rowwise(ln_and_loss, [h1, ffn, pg, pp, loss_target], [g2, b2], [(d, F32)], accs=[(1, 1)],
                       name=n + "ln2_loss")
    saved = dict(h=h, hb=hb, h1b=h1b, projp=projp, o_pre=o_pre, states=states, cqn=cqn, ckvn=ckvn, qr=qr, kf=kf, kvb=kvb, o_c=o_c,
                 lse=lse, cat=cat, mix=mix, h1=h1, gu=gu, act=act, ffn=ffn, pg=pg, pp=pp, ng=ng, lg=lg, wts=wts,
                 lbias=lbias, w_s=w_s, bias_full=bias_full, qg=qg, kvg=kvg, g1=g1, b1=b1, g2=g2, b2=b2)
    return tuple(out), saved


RS_EARLY = ("ple_w_proj", "ple_w_gate", "w_down", "w_gate_up", "w_out")
RS_LATE = ("mla_w_uq", "mla_w_ukv", "w_in")


def _layer_backward(li, dh2_parts, p_i, sv, lbs, tables, alpha, core, carried=None):
    n = f"l{li}_b_"
    wts = sv["wts"]
    gr = {}
    dh1_a, dffn, dpg, dpp, gr["ln2_g"], gr["ln2_b"] = _rowwise_vjp(
        _make_ple_ln(alpha), [sv["h1"], sv["ffn"], sv["pg"], sv["pp"]], [sv["g2"], sv["b2"]], [dh2_parts],
        groups=[[0], [1], [2], [3]], gdtypes=[F32, BF16, BF16, BF16], name=n + "ln2")
    big = {}
    big["ple_w_proj"] = _mm(p_i, dpp, am="km", tk=2048, name=n + "ple_proj_dw")
    big["ple_w_gate"] = _mm(sv["h1b"], dpg, am="km", name=n + "ple_gate_dw")
    dh1_b = _mm(dpg, wts["ple_w_gate"], bm="nk", name=n + "ple_gate_dx")
    big["w_down"] = _mm(sv["act"], dffn, am="bkm", tk=4096, name=n + "down_dw")
    dgu = _down_dx_swiglu(dffn, wts["w_down"], sv["gu"], name=n + "down_dx")
    dgu = dgu.reshape((N_DEV,) + dgu.shape[2:])
    big["w_gate_up"], carried_got = _mm(sv["h1b"], dgu, am="km", bm="bkn", om="bmn", tk=4096, name=n + "gate_up_dw",
                                        job=carried), None
    if carried is not None:
        big["w_gate_up"], carried_got = big["w_gate_up"]
    early = [_as_pairs(k, big[k]) for k in RS_EARLY[:-1]]
    dh1_c, theirs = _mm_kblocks(dgu, wts["w_gate_up"], bm="bnk", tm=512, name=n + "gate_up_dx",
                                job=_pair_job(early))
    dh_a, dmix, gr["ln1_g"], gr["ln1_b"] = _rowwise_vjp(
        _make_post_mix(alpha), [sv["h"], sv["mix"]], [sv["g1"], sv["b1"]], [[dh1_a, dh1_b, dh1_c]],
        groups=[[0], [1]], gdtypes=[F32, BF16], name=n + "ln1")
    big["w_out"] = _mm(sv["cat"], dmix, am="km", name=n + "out_proj_dw")
    early.append(_as_pairs("w_out", big["w_out"]))
    dcat, their_w_out = _mm(dmix, wts["w_out"], bm="nk", name=n + "out_proj_dx", job=_pair_job(early[-1:]))
    sums = [_pair_add(x, r, core, name=n + "pair_add_" + k)
            for k, x, r in zip(RS_EARLY, early, list(theirs) + list(their_w_out))]

    (dqr, dkv, dkf), early_quads = _attn_bwd(sv["qr"], sv["kf"], sv["kvb"], dcat, sv["o_c"], sv["lse"],
                                             name=n + "attn", job=_quad_job(sums))
    dqpad, dkr = _mla_prep_bwd(dqr, dkf, tables, name=n + "mla_prep")
    big["mla_w_uq"] = _mm(sv["cqn"], dqpad, am="km", tk=2048, name=n + "uq_dw")
    dcqn = _mm(dqpad, wts["mla_w_uq"], bm="nk", name=n + "uq_dx")
    big["mla_w_ukv"] = _mm(sv["ckvn"], dkv, am="km", tk=2048, name=n + "ukv_dw")
    dckvn = _mm(dkv, wts["mla_w_ukv"], bm="nk", name=n + "ukv_dx")
    projp = sv["projp"]
    dcq, gr["mla_q_norm_g"] = _rowwise_vjp(_fn_rms, [(projp, 384, P_CQ // 384)], [sv["qg"]], [[dcqn]],
                                           groups=[[0]], gdtypes=[BF16], name=n + "q_norm")
    dckv, gr["mla_kv_norm_g"] = _rowwise_vjp(_fn_rms, [(projp, 256, P_CKV // 256)], [sv["kvg"]], [[dckvn]],
                                             groups=[[0]], gdtypes=[BF16], name=n + "kv_norm")
    dsgu, gr["sgu_ln_g"], gr["sgu_ln_b"], gr["sgu_w_s"], gr["sgu_b_s"] = _sgu_bwd(
        projp, sv["lg"], sv["lbias"], sv["w_s"], sv["bias_full"], dcat, name=n + "sgu")
    dhg, gr["hgrn_norm_g"], gr["lower_bound"] = _hgrn_bwd(
        projp, lbs[li], sv["ng"], sv["o_pre"], sv["states"], dcat, name=n + "hgrn")
    dprojp = jnp.concatenate([dhg, dsgu, dcq, dkr, dckv], axis=1)
    big["w_in"] = _unplace_w_in(_mm(sv["hb"], dprojp, am="km", tk=4096, name=n + "proj_dw"),
                                name=n + "proj_dw_shards")
    late = [_as_pairs(k, big[k]) for k in RS_LATE]
    dh_b, theirs = _mm(dprojp, wts["w_in"], bm="nk", tk=P_COLS, name=n + "proj_dx", job=_pair_job(late))
    late_sums = [_pair_add(x, r, core, name=n + "pair_add_" + k) for k, x, r in zip(RS_LATE, late, theirs)]
    return [dh_a, dh_b], gr, early_quads, late_sums, carried_got


def kernel(x, p, positions, ln_in_g, ln_in_b, w_in, hgrn_lb_logits, hgrn_norm_g, sgu_ln_g, sgu_ln_b, sgu_w_s, sgu_b_s, mla_q_norm_g, mla_w_uq, mla_kv_norm_g, mla_w_ukv, w_out, ln1_g, ln1_b, w_gate_up, w_down, ple_w_gate, ple_w_proj, ln2_g, ln2_b, loss_target, m_ln_in_g, m_ln_in_b, m_w_in, m_hgrn_lb_logits, m_hgrn_norm_g, m_sgu_ln_g, m_sgu_ln_b, m_sgu_w_s, m_sgu_b_s, m_mla_q_norm_g, m_mla_w_uq, m_mla_kv_norm_g, m_mla_w_ukv, m_w_out, m_ln1_g, m_ln1_b, m_w_gate_up, m_w_down, m_ple_w_gate, m_ple_w_proj, m_ln2_g, m_ln2_b, v_ln_in_g, v_ln_in_b, v_w_in, v_hgrn_lb_logits, v_hgrn_norm_g, v_sgu_ln_g, v_sgu_ln_b, v_sgu_w_s, v_sgu_b_s, v_mla_q_norm_g, v_mla_w_uq, v_mla_kv_norm_g, v_mla_w_ukv, v_w_out, v_ln1_g, v_ln1_b, v_w_gate_up, v_w_down, v_ple_w_gate, v_ple_w_proj, v_ln2_g, v_ln2_b):
    args = dict(locals())
    w = {k: args[k] for k in ORDER}
    m = {k: args["m_" + k] for k in ORDER}
    v = {k: args["v_" + k] for k in ORDER}
    depth = w_in.shape[0]
    assert depth == 2, "the lower-bound kernel is written for two layers"
    alpha = (2 * depth) ** 0.25
    xs, tgt = x[0], loss_target[0]
    d_model = xs.shape[1]

    shards = [_weight_shards(w, li) for li in range(depth)]
    on_hgrn0 = ("mla_w_uq", "mla_w_ukv", "w_out", "ple_w_gate", "ple_w_proj")
    ffn0 = ("w_gate_up", "w_down")
    first1 = ("w_in", "mla_w_uq", "mla_w_ukv", "w_out")
    on_attn1 = ("w_gate_up", "w_down", "ple_w_gate", "ple_w_proj")
    layer1_first = {}

    def after_hgrn0(got):
        return _usable_weights(dict(zip(on_hgrn0, got)), name="l0")

    def after_attn0(got):
        layer1_first.update(_usable_weights(dict(zip(first1, got[len(ffn0):])), name="l1"))
        return _usable_weights(dict(zip(ffn0, got[:len(ffn0)])), name="l0")

    def after_attn1(got):
        return _usable_weights(dict(zip(on_attn1, got)), name="l1")

    tables = _rope_tables(positions[0])
    row1 = lambda a: a.reshape(1, -1)
    l0, l1 = row1(hgrn_lb_logits[0]), row1(hgrn_lb_logits[1])
    lbs = _rowwise(_fn_lower_bounds, [l0, l1], [], [(HG_W, F32), (HG_W, F32)], name="lower_bounds")

    gin, bin_ = row1(ln_in_g), row1(ln_in_b)
    (h, hb), g_in = _rowwise(_twice(_fn_ln), [xs], [gin, bin_], [(d_model, F32), (d_model, BF16)], name="ln_in",
                             job=_gather_job([shards[0]["w_in"]]))
    w_in0 = _usable_weights({"w_in": _gather_forward(g_in, name="gather_l0_w_in_forward")[0]}, name="l0")
    (h, hb), sv0 = _layer_forward(
        0, h, hb, p[0, 0], w_in0, w, lbs, tables, alpha,
        hgrn_job=_gather_job([shards[0][k] for k in on_hgrn0]), after_hgrn=after_hgrn0,
        attn_job=_gather_job([shards[0][k] for k in ffn0] + [shards[1][k] for k in first1]), after_attn=after_attn0)
    (dy, loss_local), sv1 = _layer_forward(
        1, h, hb, p[1, 0], layer1_first, w, lbs, tables, alpha,
        attn_job=_gather_job([shards[1][k] for k in on_attn1]), after_attn=after_attn1, loss_target=tgt)
    saved = [sv0, sv1]
    loss = lax.psum(loss_local[0, 0], ("x", "y", "c"))

    core = lax.axis_index("c").astype(jnp.int32).reshape(1)
    dparts, grads, quads, carried = [dy], [None] * depth, [None] * depth, None
    for li in reversed(range(depth)):
        dparts, grads[li], early_quads, late_sums, late_quads = _layer_backward(
            li, dparts, p[li, 0], saved[li], lbs, tables, alpha, core, carried=carried)
        quads[li] = dict(zip(RS_EARLY, early_quads))
        if carried is not None:
            quads[li + 1].update(zip(RS_LATE, late_quads))
        carried = _quad_job(late_sums)
    (dx, d_gin, d_bin), late_quads = _rowwise_vjp(_fn_ln, [xs], [gin, bin_], [dparts], groups=[[0]], name="ln_in_b",
                                                   job=carried)
    quads[0].update(zip(RS_LATE, late_quads))
    dl0, dl1 = _rowwise_vjp(_fn_lower_bounds, [l0, l1], [], [[grads[0]["lower_bound"]], [grads[1]["lower_bound"]]],
                            groups=[[0], [1]], name="lower_bounds_b")

    prefixes = ("grad_", "delta_", "new_m_", "new_v_")
    uq_pad = ((0, 0), (0, 0), (0, LANES - ATT_D))
    state = {k: ((jnp.pad(w[k], uq_pad), jnp.pad(m[k], uq_pad), jnp.pad(v[k], uq_pad)) if k == "mla_w_uq"
                 else (w[k], m[k], v[k])) for k in BIG}
    out = {}
    for k in BIG:
        res4 = None
        for li in range(depth):
            res4 = _adamw(quads[li][k], *state[k], li, name=f"adamw_l{li}_{k}", into=res4)
        for pre, a in zip(prefixes, res4):
            out[pre + k] = a[:, :, :ATT_D] if k == "mla_w_uq" else a

    small_g = {"ln_in_g": d_gin.reshape(-1), "ln_in_b": d_bin.reshape(-1),
               "hgrn_lb_logits": jnp.stack([dl0.reshape(-1), dl1.reshape(-1)])}
    for k in SMALL[3:]:
        small_g[k] = jnp.stack([grads[li][k].reshape(w[k].shape[1:]) for li in range(depth)])
    (small_parts,) = _all_gather([_pack([small_g[k] for k in SMALL])], name="gather_small_grads", columns=False)
    slabs = _adamw(small_parts, _pack([w[k] for k in SMALL]), _pack([m[k] for k in SMALL]),
                   _pack([v[k] for k in SMALL]), None, name="adamw_small")
    shapes = [w[k].shape for k in SMALL]
    for pre, slab in zip(prefixes, slabs):
        for k, a in zip(SMALL, _unpack(slab, shapes)):
            out[pre + k] = a
    res = [loss, dx[None]]
    for prefix in ("grad_", "delta_", "new_m_", "new_v_"):
        res += [out[prefix + k] for k in ORDER]
    return tuple(res)
```

```python
import functools
import math

import jax
import jax.numpy as jnp
import numpy as np
from jax import lax
from jax.experimental import pallas as pl
from jax.experimental.pallas import tpu as pltpu

F32 = jnp.float32
BF16 = jnp.bfloat16
MESH = pl.DeviceIdType.MESH

LN_EPS = 1e-5
RMS_EPS = 1e-6
ROPE_THETA = 10000.0
ADAM_LR, ADAM_B1, ADAM_B2, ADAM_EPS, ADAM_WD, ADAM_STEP = 0.001, 0.9, 0.999, 1e-08, 0.01, 10

N_DEV = 8
LANES = 128
HG_CHUNK = 16
HG_W = 256
HEAD = 64
SGU_CHUNK = 128
SGU_STEP_CHUNKS = 4
N_ATT_HEADS = 8
ATT_D = 96
VMEM_LIMIT = 56 * 1024 * 1024

HG_TILE = 256
ATT_TQ = 512
ROW_TILE = 512

P_CQ, P_KR, P_CKV, P_COLS = 1536, 1920, 2048, 2304


def _cparams(sem):
    return pltpu.CompilerParams(dimension_semantics=sem, vmem_limit_bytes=VMEM_LIMIT)


_ANY = pl.BlockSpec(memory_space=pl.ANY)


def _call(body, operands, *, name, grid, in_specs, out_specs, out_shape, sem, scratch_shapes=(), job=None):
    if job is None:
        return pl.pallas_call(body, name=name, grid=grid, in_specs=in_specs, out_specs=out_specs, out_shape=out_shape,
                              scratch_shapes=list(scratch_shapes), compiler_params=_cparams(sem))(*operands)
    single = not isinstance(out_shape, (list, tuple))
    shapes = [out_shape] if single else list(out_shape)
    ospecs = [out_specs] if single else list(out_specs)
    ni, no, ns = len(operands), len(shapes), len(scratch_shapes)
    ji, jo = len(job.inputs), len(job.out_shapes)

    def hosted(*refs):
        p = 0
        parts = []
        for cnt in (ni, ji, no, jo, ns):
            parts.append(refs[p:p + cnt])
            p += cnt
        ins, jins, outs, jouts, scr = parts
        jsems = refs[p:]
        ids = [pl.program_id(a) for a in range(len(grid))]
        first = functools.reduce(lambda a, b: a & b, [i == 0 for i in ids])
        last = functools.reduce(lambda a, b: a & b, [i == g - 1 for i, g in zip(ids, grid)])

        @pl.when(first)
        def _():
            job.start(jins, jouts, jsems)

        body(*ins, *outs, *scr)

        @pl.when(last)
        def _():
            job.finish(jins, jouts, jsems)

    res = pl.pallas_call(
        hosted, name=name, grid=grid,
        in_specs=list(in_specs) + [_ANY] * ji, out_specs=ospecs + [_ANY] * jo,
        out_shape=shapes + list(job.out_shapes),
        scratch_shapes=list(scratch_shapes) + [pltpu.SemaphoreType.DMA((c,)) for c in job.sem_counts],
        input_output_aliases=job.aliases(ni, no),
        compiler_params=_cparams(("arbitrary",) * len(grid)),
    )(*operands, *job.inputs)
    own = res[0] if single else res[:no]
    return own, res[no:]


class _Job:
    def __init__(self, inputs, out_shapes, sem_counts, start, finish, in_place=False):
        self.inputs, self.out_shapes, self.sem_counts = list(inputs), list(out_shapes), list(sem_counts)
        self.start, self.finish, self.in_place = start, finish, in_place

    def aliases(self, first_in, first_out):
        return {first_in + i: first_out + i for i in range(len(self.inputs))} if self.in_place else {}


def _copies_job(inputs, out_shapes, n_remote, n_local, make, in_place=False):
    def start(jins, jouts, sems):
        sends, _, local = make(jins, jouts, *sems)
        for cp in local + sends:
            cp.start()

    def finish(jins, jouts, sems):
        sends, recvs, local = make(jins, jouts, *sems)
        for cp in recvs:
            cp.wait_recv()
        for cp in sends:
            cp.wait_send()
        for cp in local:
            cp.wait()

    return _Job(inputs, out_shapes, [n_remote, n_remote, max(n_local, 1)], start, finish, in_place)


def _run_job(job, *, name):
    ji, jo = len(job.inputs), len(job.out_shapes)

    def body(*refs):
        jins, jouts, sems = refs[:ji], refs[ji:ji + jo], refs[ji + jo:]
        job.start(jins, jouts, sems)
        job.finish(jins, jouts, sems)

    return pl.pallas_call(
        body, name=name, out_shape=list(job.out_shapes), in_specs=[_ANY] * ji, out_specs=[_ANY] * jo,
        scratch_shapes=[pltpu.SemaphoreType.DMA((c,)) for c in job.sem_counts],
        input_output_aliases=job.aliases(0, 0),
    )(*job.inputs)


def _tile(n, pref):
    if n % pref == 0:
        return pref
    best = None
    t = LANES
    while t <= min(n, pref):
        if n % t == 0:
            best = t
        t += LANES
    return best if best is not None else n


def _mm(a, b, *, am="mk", bm="kn", om="mn", out_dtype=F32, tm=1024, tn=1024, tk=1024, name, job=None):
    if am == "mk":
        m, k = a.shape
    elif am == "km":
        k, m = a.shape
    elif am == "bmk":
        m, tk = a.shape[1], a.shape[2]
        k = a.shape[0] * tk
    else:
        k, tm = a.shape[1], a.shape[2]
        m = a.shape[0] * tm
    if bm == "kn":
        kb_, n = b.shape
    elif bm == "nk":
        n, kb_ = b.shape
    elif bm == "bkn":
        kb_, tn = b.shape[1], b.shape[2]
        n = b.shape[0] * tn
    else:
        n, tk = b.shape[1], b.shape[2]
        kb_ = b.shape[0] * tk
    assert kb_ == k, (a.shape, b.shape, am, bm)
    tm, tn, tk = _tile(m, tm), _tile(n, tn), _tile(k, tk)
    nk = k // tk
    dims = (((0 if am in ("km", "bkm") else 1,), (1 if bm in ("nk", "bnk") else 0,)), ((), ()))

    a_spec = {"mk": pl.BlockSpec((tm, tk), lambda i, j, kk: (i, kk)),
              "km": pl.BlockSpec((tk, tm), lambda i, j, kk: (kk, i)),
              "bmk": pl.BlockSpec((None, tm, tk), lambda i, j, kk: (kk, i, 0)),
              "bkm": pl.BlockSpec((None, tk, tm), lambda i, j, kk: (i, kk, 0))}[am]
    b_spec = {"kn": pl.BlockSpec((tk, tn), lambda i, j, kk: (kk, j)),
              "nk": pl.BlockSpec((tn, tk), lambda i, j, kk: (j, kk)),
              "bkn": pl.BlockSpec((None, tk, tn), lambda i, j, kk: (j, kk, 0)),
              "bnk": pl.BlockSpec((None, tn, tk), lambda i, j, kk: (kk, j, 0))}[bm]
    if om == "mn":
        o_spec, o_shape = pl.BlockSpec((tm, tn), lambda i, j, kk: (i, j)), (m, n)
    else:
        o_spec, o_shape = pl.BlockSpec((None, tm, tn), lambda i, j, kk: (j, i, 0)), (n // tn, m, tn)

    def body(a_ref, b_ref, o_ref, *acc):
        kk = pl.program_id(2)

        def prod():
            return lax.dot_general(a_ref[...].astype(BF16), b_ref[...].astype(BF16), dims, preferred_element_type=F32)

        if nk == 1:
            o_ref[...] = prod().astype(o_ref.dtype)
            return
        acc_ref, = acc

        @pl.when(kk == 0)
        def _():
            acc_ref[...] = prod()

        if nk > 2:
            @pl.when((kk > 0) & (kk < nk - 1))
            def _():
                acc_ref[...] += prod()

        @pl.when(kk == nk - 1)
        def _():
            o_ref[...] = (acc_ref[...] + prod()).astype(o_ref.dtype)

    return _call(body, (a, b), name=name, grid=(m // tm, n // tn, nk), in_specs=[a_spec, b_spec], out_specs=o_spec,
                 out_shape=jax.ShapeDtypeStruct(o_shape, out_dtype),
                 scratch_shapes=[pltpu.VMEM((tm, tn), F32)] if nk > 1 else [],
                 sem=("parallel", "parallel", "arbitrary"), job=job)


def _mm_kblocks(a, b, *, bm, tm, name, job=None):
    nkb, m, kb = a.shape
    n = b.shape[1]
    tm = _tile(m, tm)

    def body(a_ref, b_ref, o_ref):
        acc = None
        for j in range(nkb):
            if bm == "kn":
                part = jnp.dot(a_ref[j], b_ref[j * kb:(j + 1) * kb, :], preferred_element_type=F32)
            else:
                part = lax.dot_general(a_ref[j], b_ref[j], _NT, preferred_element_type=F32)
            acc = part if acc is None else acc + part
        o_ref[...] = acc

    b_spec = (pl.BlockSpec(b.shape, lambda i: (0, 0)) if bm == "kn" else pl.BlockSpec(b.shape, lambda i: (0, 0, 0)))
    return _call(body, (a, b), name=name, grid=(m // tm,),
                 in_specs=[pl.BlockSpec((nkb, tm, kb), lambda i: (0, i, 0)), b_spec],
                 out_specs=pl.BlockSpec((tm, n), lambda i: (i, 0)), out_shape=jax.ShapeDtypeStruct((m, n), F32),
                 sem=("parallel",), job=job)


def _row_operand(a, tile):
    if isinstance(a, tuple):
        arr, w, j = a
        return arr, pl.BlockSpec((tile, w), lambda i, j=j: (i, j))
    return a, pl.BlockSpec((tile, a.shape[1]), lambda i: (i, 0))


def _const_spec(c):
    nd = c.ndim
    return pl.BlockSpec(c.shape, lambda i, nd=nd: (0,) * nd)


def _rowwise(fn, rows, consts, outs, *, name, accs=(), tile=None, job=None):
    t_rows = (rows[0][0] if isinstance(rows[0], tuple) else rows[0]).shape[0]
    tile = min(tile or ROW_TILE, t_rows)
    arrs, specs = zip(*[_row_operand(a, tile) for a in rows])
    nin, no = len(rows) + len(consts), len(outs)

    def body(*refs):
        res = fn(*[r[...] for r in refs[:nin]])
        for r, v in zip(refs[nin:nin + no], res[:no]):
            r[...] = v.astype(r.dtype)
        if accs:
            a_refs = refs[nin + no:]

            @pl.when(pl.program_id(0) == 0)
            def _():
                for r in a_refs:
                    r[...] = jnp.zeros_like(r)

            for r, v in zip(a_refs, res[no:]):
                r[...] += v

    out_shape = [jax.ShapeDtypeStruct((t_rows, w), dt) for w, dt in outs]
    out_shape += [jax.ShapeDtypeStruct(s, F32) for s in accs]
    out_specs = [pl.BlockSpec((tile, w), lambda i: (i, 0)) for w, _ in outs]
    out_specs += [pl.BlockSpec(s, lambda i, nd=len(s): (0,) * nd) for s in accs]
    return _call(body, (*arrs, *consts), name=name, grid=(t_rows // tile,),
                 in_specs=list(specs) + [_const_spec(c) for c in consts],
                 out_specs=out_specs, out_shape=out_shape, sem=("arbitrary",), job=job)


def _rowwise_vjp(fn, rows, consts, cts, *, name, groups, tile=None, gdtypes=None, job=None):
    t_rows = (rows[0][0] if isinstance(rows[0], tuple) else rows[0]).shape[0]
    tile = min(tile or ROW_TILE, t_rows)
    arrs, specs = zip(*[_row_operand(a, tile) for a in rows])
    flat_cts = [c for group in cts for c in group]
    ct_arrs, ct_specs = zip(*[_row_operand(a, tile) for a in flat_cts])
    nr, nc, nct, ng = len(rows), len(consts), len(flat_cts), len(groups)

    def width(a):
        return a[1] if isinstance(a, tuple) else a.shape[1]

    def body(*refs):
        rv = [r[...].astype(F32) for r in refs[:nr]]
        cv = [r[...] for r in refs[nr:nr + nc]]
        ct_refs = refs[nr + nc:nr + nc + nct]
        ctv, pos = [], 0
        for group in cts:
            s = ct_refs[pos][...].astype(F32)
            for r in ct_refs[pos + 1:pos + len(group)]:
                s = s + r[...].astype(F32)
            ctv.append(s)
            pos += len(group)
        _, pull = jax.vjp(fn, *rv, *cv)
        grads = pull(tuple(ctv))
        g_refs = refs[nr + nc + nct:nr + nc + nct + ng]
        for r, idx in zip(g_refs, groups):
            parts = [grads[i] for i in idx]
            r[...] = (parts[0] if len(parts) == 1 else jnp.concatenate(parts, axis=1)).astype(r.dtype)
        c_refs = refs[nr + nc + nct + ng:]

        @pl.when(pl.program_id(0) == 0)
        def _():
            for r in c_refs:
                r[...] = jnp.zeros_like(r)

        for r, v in zip(c_refs, grads[nr:]):
            r[...] += v

    gw = [sum(width(rows[i]) for i in idx) for idx in groups]
    gdtypes = gdtypes or [F32] * ng
    out_shape = [jax.ShapeDtypeStruct((t_rows, w), dt) for w, dt in zip(gw, gdtypes)]
    out_shape += [jax.ShapeDtypeStruct(c.shape, F32) for c in consts]
    out_specs = [pl.BlockSpec((tile, w), lambda i: (i, 0)) for w in gw]
    out_specs += [_const_spec(c) for c in consts]
    return _call(body, (*arrs, *consts, *ct_arrs), name=name, grid=(t_rows // tile,),
                 in_specs=list(specs) + [_const_spec(c) for c in consts] + list(ct_specs),
                 out_specs=out_specs, out_shape=out_shape, sem=("arbitrary",), job=job)


def _layer_norm(x, g, b):
    mu = jnp.mean(x, axis=-1, keepdims=True)
    xc = x - mu
    var = jnp.mean(xc * xc, axis=-1, keepdims=True)
    return xc * lax.rsqrt(var + LN_EPS) * g + b


def _sigmoid(x):
    return 1.0 / (1.0 + jnp.exp(-x))


def _fn_ln(x, g, b):
    return (_layer_norm(x, g, b),)


def _fn_rms(x, g):
    return (x * lax.rsqrt(jnp.mean(x * x, axis=-1, keepdims=True) + RMS_EPS) * g,)


def _make_post_mix(alpha):
    def fn(h, mix, g, b):
        return (_layer_norm(alpha * h + mix, g, b),)
    return fn


def _make_ple_ln(alpha):
    def fn(h1, ffn, pg, pp, g, b):
        return (_layer_norm(alpha * h1 + ffn + _sigmoid(pg) * pp, g, b),)
    return fn


def _fn_lower_bounds(l0, l1):
    m = jnp.maximum(l0, l1)
    e0, e1 = jnp.exp(l0 - m), jnp.exp(l1 - m)
    s = e0 + e1
    p0, p1 = e0 / s, e1 / s
    return (p0 - p0, (p0 + p1) - p0)


def _split_dot(x, e_bf16):
    hi = x.astype(BF16)
    lo = (x - hi.astype(F32)).astype(BF16)
    return (jnp.dot(hi, e_bf16, preferred_element_type=F32) + jnp.dot(lo, e_bf16, preferred_element_type=F32))


def _hgrn_common(th):
    rm = lax.broadcasted_iota(jnp.int32, (th, HG_W), 0) % HG_CHUNK

    def seg_cumsum(x):
        for s in (1, 2, 4, 8):
            x = x + jnp.where(rm >= s, pltpu.roll(x, s, 0), 0.0)
        return x

    def seg_rcumsum(x):
        for s in (1, 2, 4, 8):
            x = x + jnp.where(rm < HG_CHUNK - s, pltpu.roll(x, th - s, 0), 0.0)
        return x

    ri = lax.broadcasted_iota(jnp.int32, (HG_W, HG_W), 0) // HEAD
    ci = lax.broadcasted_iota(jnp.int32, (HG_W, HG_W), 1) // HEAD
    head_f32 = (ri == ci).astype(F32)
    head_bf16 = head_f32.astype(BF16)

    def headsum(x, pieces=2):
        if pieces == 1:
            return jnp.dot(x.astype(BF16), head_bf16, preferred_element_type=F32)
        return _split_dot(x, head_bf16)

    return rm, seg_cumsum, seg_rcumsum, head_f32, headsum


def _hgrn_gates(qr, fl, lb):
    sg = _sigmoid(fl)
    f = lb + (1.0 - lb) * sg
    sq = _sigmoid(qr)
    return sg, f, jnp.log(f), 1.0 - f, qr * sq, sq


def _shifted(x, d, th):
    return x if d == 0 else pltpu.roll(x, d, 0)


def _unshift(x, d, th):
    return x if d == 0 else pltpu.roll(x, th - d, 0)


def _hgrn_fwd(projp, lb, ng, *, name, job=None):
    t_rows = projp.shape[0]
    th = min(HG_TILE, t_rows)
    nct = th // HG_CHUNK

    def body(q_ref, f_ref, i_ref, g_ref, lb_ref, ng_ref, oa_ref, opre_ref, st_out_ref,
             st_ref, vtm_ref, kv_ref, qe_ref, dec_ref, oint_ref):
        rm, seg_cumsum, seg_rcumsum, head_f32, headsum = _hgrn_common(th)

        @pl.when(pl.program_id(0) == 0)
        def _():
            st_ref[...] = jnp.zeros_like(st_ref)

        qr, fl, v, g = q_ref[...], f_ref[...], i_ref[...], g_ref[...]
        _, f, lf, k, q, _ = _hgrn_gates(qr, fl, lb_ref[...])
        b = seg_cumsum(lf)

        o = jnp.zeros((th, HG_W), F32)
        for d in range(HG_CHUNK):
            kd, bd, vd = _shifted(k, d, th), _shifted(b, d, th), _shifted(v, d, th)
            e = jnp.exp(jnp.where(rm >= d, b - bd, -1e30))
            o = o + headsum(q * kd * e, 1) * vd

        blast = seg_rcumsum(jnp.where(rm == HG_CHUNK - 1, b, 0.0))
        kte = (k * jnp.exp(blast - b)).astype(BF16)
        qe_ref[...] = q * jnp.exp(b)
        dec_ref[...] = jnp.exp(blast)
        vt = v.T
        lane_chunk = lax.broadcasted_iota(jnp.int32, (HG_W, th), 1) // HG_CHUNK
        for c in range(nct):
            vtm_ref[c * HG_W:(c + 1) * HG_W, :] = jnp.where(lane_chunk == c, vt, 0.0).astype(BF16)
        kv_ref[...] = jnp.dot(vtm_ref[...], kte, preferred_element_type=F32)

        s = st_ref[...]
        for c in range(nct):
            rows = slice(c * HG_CHUNK, (c + 1) * HG_CHUNK)
            st_out_ref[c] = s
            oint_ref[rows, :] = lax.dot_general(qe_ref[rows, :].astype(BF16), s.astype(BF16),
                                                (((1,), (1,)), ((), ())), preferred_element_type=F32)
            dec = jnp.max(dec_ref[rows, :], axis=0, keepdims=True)
            s = s * dec + kv_ref[c * HG_W:(c + 1) * HG_W, :] * head_f32
        st_ref[...] = s

        o = o + oint_ref[...]
        opre_ref[...] = o
        r = lax.rsqrt(headsum(o * o) * (1.0 / HEAD) + RMS_EPS)
        oa_ref[...] = (o * r * ng_ref[...] * (g * _sigmoid(g))).astype(oa_ref.dtype)

    col = lambda j: pl.BlockSpec((th, HG_W), lambda i, j=j: (i, j))
    vec = pl.BlockSpec((1, HG_W), lambda i: (0, 0))
    row = pl.BlockSpec((th, HG_W), lambda i: (i, 0))
    n_chunks = t_rows // HG_CHUNK
    return _call(
        body, (projp, projp, projp, projp, lb, ng), name=name, grid=(t_rows // th,),
        in_specs=[col(0), col(1), col(2), col(3), vec, vec],
        out_specs=[row, row, pl.BlockSpec((nct, HG_W, HG_W), lambda i: (i, 0, 0))],
        out_shape=[jax.ShapeDtypeStruct((t_rows, HG_W), BF16), jax.ShapeDtypeStruct((t_rows, HG_W), F32),
                   jax.ShapeDtypeStruct((n_chunks, HG_W, HG_W), F32)],
        scratch_shapes=[pltpu.VMEM((HG_W, HG_W), F32), pltpu.VMEM((nct * HG_W, th), BF16),
                        pltpu.VMEM((nct * HG_W, HG_W), F32), pltpu.VMEM((th, HG_W), F32),
                        pltpu.VMEM((th, HG_W), F32), pltpu.VMEM((th, HG_W), F32)],
        sem=("arbitrary",), job=job)


def _hgrn_bwd(projp, lb, ng, opre, states, dcat, *, name):
    t_rows = projp.shape[0]
    th = min(HG_TILE, t_rows)
    nct = th // HG_CHUNK
    nt = t_rows // th

    def body(q_ref, f_ref, i_ref, g_ref, lb_ref, ng_ref, opre_ref, st_in_ref, do_ref,
             dproj_ref, dng_ref, dlb_ref,
             gst_ref, dotm_ref, qg_ref, v_ref, kte_ref, dop_ref, dec_ref, dkte_ref, dvi_ref, dqe_ref, ddec_ref):
        rm, seg_cumsum, seg_rcumsum, head_f32, headsum = _hgrn_common(th)

        @pl.when(pl.program_id(0) == 0)
        def _():
            gst_ref[...] = jnp.zeros_like(gst_ref)
            dng_ref[...] = jnp.zeros_like(dng_ref)
            dlb_ref[...] = jnp.zeros_like(dlb_ref)

        qr, fl, v, g = q_ref[...], f_ref[...], i_ref[...], g_ref[...]
        lb, ngv = lb_ref[...], ng_ref[...]
        sg, f, lf, k, q, sq = _hgrn_gates(qr, fl, lb)
        b = seg_cumsum(lf)
        blast = seg_rcumsum(jnp.where(rm == HG_CHUNK - 1, b, 0.0))
        eb = jnp.exp(b)
        ekb = jnp.exp(blast - b)
        qe, kte, dec = q * eb, k * ekb, jnp.exp(blast)

        do_out, op = do_ref[...], opre_ref[...]
        sgg = _sigmoid(g)
        sil = g * sgg
        r = lax.rsqrt(headsum(op * op) * (1.0 / HEAD) + RMS_EPS)
        on = op * r
        dng_ref[...] += jnp.sum(do_out * on * sil, axis=0, keepdims=True)
        dg = do_out * on * ngv * (sgg * (1.0 + g * (1.0 - sgg)))
        don = do_out * ngv * sil
        dop = r * (don - on * (headsum(don * on) * (1.0 / HEAD)))

        v_ref[...] = v
        kte_ref[...] = kte
        dop_ref[...] = dop
        dec_ref[...] = dec
        dot_t = dop.T
        lane_chunk = lax.broadcasted_iota(jnp.int32, (HG_W, th), 1) // HG_CHUNK
        for c in range(nct):
            dotm_ref[c * HG_W:(c + 1) * HG_W, :] = jnp.where(lane_chunk == c, dot_t, 0.0).astype(BF16)
        qg_ref[...] = jnp.dot(dotm_ref[...], qe.astype(BF16), preferred_element_type=F32)

        gs = gst_ref[...]
        for c in reversed(range(nct)):
            rows = slice(c * HG_CHUNK, (c + 1) * HG_CHUNK)
            s = st_in_ref[c]
            gm = (gs * head_f32).astype(BF16)
            dkte_ref[rows, :] = jnp.dot(v_ref[rows, :].astype(BF16), gm, preferred_element_type=F32)
            dvi_ref[rows, :] = lax.dot_general(kte_ref[rows, :].astype(BF16), gm, (((1,), (1,)), ((), ())),
                                               preferred_element_type=F32)
            dqe_ref[rows, :] = jnp.dot(dop_ref[rows, :].astype(BF16), s.astype(BF16), preferred_element_type=F32)
            ddec_ref[rows, :] = jnp.broadcast_to(jnp.sum(gs * s, axis=0, keepdims=True), (HG_CHUNK, HG_W))
            dec_c = jnp.max(dec_ref[rows, :], axis=0, keepdims=True)
            gs = gs * dec_c + qg_ref[c * HG_W:(c + 1) * HG_W, :] * head_f32
        gst_ref[...] = gs

        dkte, dqe = dkte_ref[...], dqe_ref[...]
        dq = dqe * eb
        dk = dkte * ekb
        db = dqe * qe - dkte * kte
        dv = dvi_ref[...]
        dblast = dkte * kte + jnp.where(rm == HG_CHUNK - 1, ddec_ref[...] * dec, 0.0)

        for d in range(HG_CHUNK):
            kd, bd, vd = _shifted(k, d, th), _shifted(b, d, th), _shifted(v, d, th)
            e = jnp.exp(jnp.where(rm >= d, b - bd, -1e30))
            p = q * kd * e
            sc = headsum(p, 1)
            dsc = headsum(dop * vd, 1)
            dv = dv + _unshift(sc * dop, d, th)
            dq = dq + dsc * kd * e
            dk = dk + _unshift(dsc * q * e, d, th)
            darg = dsc * p
            db = db + darg - _unshift(darg, d, th)

        db = db + jnp.where(rm == HG_CHUNK - 1, seg_cumsum(dblast), 0.0)
        dlf = seg_rcumsum(db)
        df = dlf / f - dk
        dlb_ref[...] += jnp.sum(df * (1.0 - sg), axis=0, keepdims=True)
        dfl = df * (1.0 - lb) * sg * (1.0 - sg)
        dqr = dq * (sq * (1.0 + qr * (1.0 - sq)))
        dproj_ref[...] = jnp.concatenate([dqr, dfl, dv, dg], axis=1).astype(dproj_ref.dtype)

    rev = lambda i: nt - 1 - i
    col = lambda j: pl.BlockSpec((th, HG_W), lambda i, j=j: (rev(i), j))
    vec = pl.BlockSpec((1, HG_W), lambda i: (0, 0))
    row = pl.BlockSpec((th, HG_W), lambda i: (rev(i), 0))
    tile_f32 = pltpu.VMEM((th, HG_W), F32)
    return pl.pallas_call(
        body, name=name, grid=(nt,),
        in_specs=[col(0), col(1), col(2), col(3), vec, vec, row,
                  pl.BlockSpec((nct, HG_W, HG_W), lambda i: (rev(i), 0, 0)), col(0)],
        out_specs=[pl.BlockSpec((th, 4 * HG_W), lambda i: (rev(i), 0)), vec, vec],
        out_shape=[jax.ShapeDtypeStruct((t_rows, 4 * HG_W), BF16), jax.ShapeDtypeStruct((1, HG_W), F32),
                   jax.ShapeDtypeStruct((1, HG_W), F32)],
        scratch_shapes=[pltpu.VMEM((HG_W, HG_W), F32), pltpu.VMEM((nct * HG_W, th), BF16),
                        pltpu.VMEM((nct * HG_W, HG_W), F32)] + [tile_f32] * 8,
        compiler_params=_cparams(("arbitrary",)),
    )(projp, projp, projp, projp, lb, ng, opre, states, dcat)


_INV_SQRT2 = 1.0 / math.sqrt(2.0)
_INV_SQRT2PI = 1.0 / math.sqrt(2.0 * math.pi)


def _gelu(x):
    return 0.5 * x * (1.0 + lax.erf(x * _INV_SQRT2))


def _gelu_grad(x):
    return 0.5 * (1.0 + lax.erf(x * _INV_SQRT2)) + x * jnp.exp(-0.5 * x * x) * _INV_SQRT2PI


def _sgu_parts(bu, bv, lg, lbias, w_ref, n_groups):
    c = SGU_CHUNK
    tril = (lax.broadcasted_iota(jnp.int32, (c, c), 0) >= lax.broadcasted_iota(jnp.int32, (c, c), 1)).astype(F32)
    gid = lax.broadcasted_iota(jnp.int32, bu.shape, 1) // HEAD
    u = _gelu(bu)
    gv = _gelu(bv)
    mu = jnp.mean(gv, axis=-1, keepdims=True)
    xc = gv - mu
    rstd = lax.rsqrt(jnp.mean(xc * xc, axis=-1, keepdims=True) + LN_EPS)
    xhat = xc * rstd
    vn = xhat * lg + lbias
    ws = [w_ref[gi] * tril for gi in range(n_groups)]
    return tril, gid, u, rstd, xhat, vn, ws


def _sgu_fwd(projp, lg, lbias, w_s, bias_full, *, name, job=None):
    t_rows = projp.shape[0]
    n_groups = w_s.shape[0]
    c = SGU_CHUNK
    rows_per_step = min(SGU_STEP_CHUNKS * c, t_rows)

    def body(u_ref, v_ref, lg_ref, lb_ref, w_ref, bias_ref, o_ref):
        for r0 in range(0, rows_per_step, c):
            rows = slice(r0, r0 + c)
            _, gid, u, _, _, vn, ws = _sgu_parts(u_ref[rows, :], v_ref[rows, :], lg_ref[...], lb_ref[...], w_ref,
                                                 n_groups)
            vnb = vn.astype(BF16)
            z = bias_ref[...]
            for gi in range(n_groups):
                z = z + jnp.where(gid == gi, jnp.dot(ws[gi].astype(BF16), vnb, preferred_element_type=F32), 0.0)
            o_ref[rows, :] = (u * z).astype(o_ref.dtype)

    col = lambda j: pl.BlockSpec((rows_per_step, HG_W), lambda i, j=j: (i, j))
    return _call(
        body, (projp, projp, lg, lbias, w_s, bias_full), name=name, grid=(t_rows // rows_per_step,),
        in_specs=[col(4), col(5), _const_spec(lg), _const_spec(lbias), _const_spec(w_s), _const_spec(bias_full)],
        out_specs=pl.BlockSpec((rows_per_step, HG_W), lambda i: (i, 0)),
        out_shape=jax.ShapeDtypeStruct((t_rows, HG_W), BF16), sem=("arbitrary",), job=job)


def _sgu_bwd(projp, lg, lbias, w_s, bias_full, dcat, *, name):
    t_rows = projp.shape[0]
    n_groups = w_s.shape[0]
    c = SGU_CHUNK
    rows_per_step = min(SGU_STEP_CHUNKS * c, t_rows)
    n = t_rows // rows_per_step

    def body(u_ref, v_ref, lg_ref, lb_ref, w_ref, bias_ref, do_ref,
             dproj_ref, dlg_ref, dlb_ref, dw_ref, dbs_ref, dbias_acc):
        i = pl.program_id(0)

        @pl.when(i == 0)
        def _():
            dlg_ref[...] = jnp.zeros_like(dlg_ref)
            dlb_ref[...] = jnp.zeros_like(dlb_ref)
            dw_ref[...] = jnp.zeros_like(dw_ref)
            dbias_acc[...] = jnp.zeros_like(dbias_acc)

        lg_v = lg_ref[...]
        for r0 in range(0, rows_per_step, c):
            rows = slice(r0, r0 + c)
            bu, bv = u_ref[rows, :], v_ref[rows, :]
            tril, gid, u, rstd, xhat, vn, ws = _sgu_parts(bu, bv, lg_v, lb_ref[...], w_ref, n_groups)
            vnb = vn.astype(BF16)
            z = bias_ref[...]
            for gi in range(n_groups):
                z = z + jnp.where(gid == gi, jnp.dot(ws[gi].astype(BF16), vnb, preferred_element_type=F32), 0.0)
            do = do_ref[rows, :]
            dbu = do * z * _gelu_grad(bu)
            dz = do * u
            dbias_acc[...] += dz
            dvn = jnp.zeros_like(dz)
            for gi in range(n_groups):
                dzg = jnp.where(gid == gi, dz, 0.0).astype(BF16)
                dw_ref[gi] += lax.dot_general(dzg, vnb, (((1,), (1,)), ((), ())), preferred_element_type=F32) * tril
                dvn = dvn + jnp.dot(ws[gi].T.astype(BF16), dzg, preferred_element_type=F32)
            dlg_ref[...] += jnp.sum(dvn * xhat, axis=0, keepdims=True)
            dlb_ref[...] += jnp.sum(dvn, axis=0, keepdims=True)
            dxh = dvn * lg_v
            dgv = rstd * (dxh - jnp.mean(dxh, axis=-1, keepdims=True)
                          - xhat * jnp.mean(dxh * xhat, axis=-1, keepdims=True))
            dproj_ref[rows, :] = jnp.concatenate([dbu, dgv * _gelu_grad(bv)], axis=1).astype(dproj_ref.dtype)

        @pl.when(i == n - 1)
        def _():
            dbs_ref[...] = jnp.sum(dbias_acc[...].T.reshape(n_groups, HEAD, c), axis=1)

    col = lambda j: pl.BlockSpec((rows_per_step, HG_W), lambda i, j=j: (i, j))
    return pl.pallas_call(
        body, name=name, grid=(n,),
        in_specs=[col(4), col(5), _const_spec(lg), _const_spec(lbias), _const_spec(w_s), _const_spec(bias_full),
                  col(1)],
        out_specs=[pl.BlockSpec((rows_per_step, 2 * HG_W), lambda i: (i, 0)), _const_spec(lg), _const_spec(lbias),
                   _const_spec(w_s), pl.BlockSpec((n_groups, c), lambda i: (0, 0))],
        out_shape=[jax.ShapeDtypeStruct((t_rows, 2 * HG_W), BF16), jax.ShapeDtypeStruct(lg.shape, F32),
                   jax.ShapeDtypeStruct(lbias.shape, F32), jax.ShapeDtypeStruct(w_s.shape, F32),
                   jax.ShapeDtypeStruct((n_groups, c), F32)],
        scratch_shapes=[pltpu.VMEM((c, HG_W), F32)],
        compiler_params=_cparams(("arbitrary",)),
    )(projp, projp, lg, lbias, w_s, bias_full, dcat)


def _rope_tables(positions):
    t = positions.shape[0]
    inv_freq = ROPE_THETA ** (-jnp.arange(0, 32, 2, dtype=F32) / 32)
    ang = positions.astype(F32)[:, None] * inv_freq
    cos, sin = jnp.cos(ang), jnp.sin(ang)
    z = lambda w: jnp.zeros((t, w), F32)
    cos_t = jnp.concatenate([jnp.ones((t, 64), F32), cos, cos, z(32)], axis=1)
    sin_up = jnp.concatenate([z(80), sin, z(32)], axis=1)
    sin_dn = jnp.concatenate([z(64), -sin, z(48)], axis=1)
    return cos_t, sin_up, sin_dn


def _rep(x, n):
    return x if n == 1 else jnp.concatenate([x] * n, axis=1)


def _rope(x, cos_t, sin_up, sin_dn):
    w = x.shape[1]
    return x * cos_t + pltpu.roll(x, 16, 1) * sin_up + pltpu.roll(x, w - 16, 1) * sin_dn


def _rope_t(dy, cos_t, sin_up, sin_dn):
    w = dy.shape[1]
    return dy * cos_t + pltpu.roll(dy * sin_up, w - 16, 1) + pltpu.roll(dy * sin_dn, 16, 1)


def _mla_prep(q, kv, projp, tables, *, name):
    nh = N_ATT_HEADS

    def fn(qv, kvv, kr, cos_t, sin_up, sin_dn):
        qr = _rope(qv, _rep(cos_t, nh), _rep(sin_up, nh), _rep(sin_dn, nh))
        krr = _rope(kr, cos_t, sin_up, sin_dn)
        lane = lax.broadcasted_iota(jnp.int32, kvv.shape, 1) % LANES
        return qr, jnp.where(lane < HEAD, kvv, 0.0) + _rep(krr, nh), kvv

    w = q.shape[1]
    return _rowwise(fn, [q, kv, (projp, LANES, P_KR // LANES)] + list(tables), [],
                    [(w, BF16), (w, BF16), (w, BF16)], name=name)


def _mla_prep_bwd(dqr, dkf, tables, *, name):
    nh = N_ATT_HEADS

    def fn(dq, dk, cos_t, sin_up, sin_dn):
        dqp = _rope_t(dq, _rep(cos_t, nh), _rep(sin_up, nh), _rep(sin_dn, nh))
        dkrr = dk[:, 0:LANES]
        for h in range(1, nh):
            dkrr = dkrr + dk[:, LANES * h:LANES * (h + 1)]
        return dqp, _rope_t(dkrr, cos_t, sin_up, sin_dn)

    return _rowwise(fn, [dqr, dkf] + list(tables), [], [(dqr.shape[1], BF16), (LANES, BF16)], name=name)


_LOG2E = 1.0 / math.log(2.0)
_NT = (((1,), (1,)), ((), ()))
_TN = (((0,), (0,)), ((), ()))


def _attn_fwd(qr, kf, kvb, *, name, job=None):
    t_rows = qr.shape[0]
    tq = min(ATT_TQ, t_rows)
    nb = t_rows // tq
    scale = ATT_D ** -0.5

    c2 = scale * _LOG2E

    def body(q_ref, kf_ref, kvb_ref, o_ref, lse_ref):
        qi = pl.program_id(1)
        lane = lax.broadcasted_iota(jnp.int32, (tq, LANES), 1)
        causal_t = (lax.broadcasted_iota(jnp.int32, (tq, tq), 0) <= lax.broadcasted_iota(jnp.int32, (tq, tq), 1))
        heads = [slice(hh * LANES, (hh + 1) * LANES) for hh in range(2)]
        qs = [q_ref[:, cols] for cols in heads]

        def block(first, n_keys, carry, diagonal):
            rows = pl.ds(pl.multiple_of(first * tq, tq), n_keys)
            new = []
            for q, cols, (m_old, l_old, acc_t) in zip(qs, heads, carry):
                s_t = lax.dot_general(kf_ref[rows, cols], q, _NT, preferred_element_type=F32)
                if diagonal:
                    s_t = jnp.where(causal_t, s_t, -1e30)
                m_new = jnp.maximum(m_old, jnp.max(s_t, axis=0, keepdims=True))
                p_t = jnp.exp2((s_t - m_new) * c2)
                a = jnp.exp2((m_old - m_new) * c2)
                pv_t = lax.dot_general(kvb_ref[rows, cols], p_t.astype(BF16), _TN, preferred_element_type=F32)
                new.append((m_new, a * l_old + jnp.sum(p_t, axis=0, keepdims=True), a * acc_t + pv_t))
            return tuple(new)

        init = (jnp.full((1, tq), -1e30, F32), jnp.zeros((1, tq), F32), jnp.zeros((LANES, tq), F32))
        carry = lax.fori_loop(0, qi // 4, lambda g, c: block(4 * g, 4 * tq, c, False), (init, init))
        carry = lax.cond((qi // 2) % 2 == 1, lambda c: block(4 * (qi // 4), 2 * tq, c, False), lambda c: c, carry)
        carry = lax.cond(qi % 2 == 1, lambda c: block(qi - 1, tq, c, False), lambda c: c, carry)
        outs = []
        for hh, (m_fin, l_fin, acc_t) in enumerate(block(qi, tq, carry, True)):
            lse_ref[hh] = m_fin * scale + jnp.log(l_fin)
            outs.append((acc_t / l_fin).T)
        o_ref[...] = jnp.where(lane < HEAD, pltpu.roll(outs[0], HEAD, 1), outs[1])

    pair = pl.BlockSpec((t_rows, 2 * LANES), lambda pr, qi: (0, pr))
    return _call(
        body, (qr, kf, kvb), name=name, grid=(N_ATT_HEADS // 2, nb),
        in_specs=[pl.BlockSpec((tq, 2 * LANES), lambda pr, qi: (qi, pr)), pair, pair],
        out_specs=[pl.BlockSpec((tq, LANES), lambda pr, qi: (qi, pr)),
                   pl.BlockSpec((2, 1, tq), lambda pr, qi: (pr, 0, qi))],
        out_shape=[jax.ShapeDtypeStruct((t_rows, N_ATT_HEADS * HEAD), F32),
                   jax.ShapeDtypeStruct((N_ATT_HEADS, 1, t_rows), F32)],
        sem=("parallel", "arbitrary"), job=job)


def _attn_bwd(qr, kf, kvb, dcat, o, lse, *, name, job=None):
    t_rows = qr.shape[0]
    tq = min(ATT_TQ, t_rows)
    nb = t_rows // tq
    scale = ATT_D ** -0.5
    c2 = scale * _LOG2E
    do_off = 2 * HG_W // LANES

    def body(q_ref, kf_ref, kvb_ref, do_ref, o_ref, lse_ref, dq_ref, dkv_ref, dk_ref):
        ki = pl.program_id(1)

        @pl.when(ki == 0)
        def _():
            dq_ref[...] = jnp.zeros_like(dq_ref)

        lane = lax.broadcasted_iota(jnp.int32, (tq, LANES), 1)
        causal_t = (lax.broadcasted_iota(jnp.int32, (tq, tq), 0) <= lax.broadcasted_iota(jnp.int32, (tq, tq), 1))
        heads = [slice(hh * LANES, (hh + 1) * LANES) for hh in range(2)]
        ks = [kf_ref[:, cols] for cols in heads]
        vs = [kvb_ref[:, cols] for cols in heads]

        def block(qi, n_q, carry, diagonal):
            rows = pl.ds(pl.multiple_of(qi * tq, tq), n_q)
            do_pair, o_pair = do_ref[rows, :], o_ref[rows, :]
            upper = lax.broadcasted_iota(jnp.int32, do_pair.shape, 1) >= HEAD
            new = []
            for hh, (cols, k, v, (dk, dv)) in enumerate(zip(heads, ks, vs, carry)):
                q = q_ref[rows, cols]
                do, ov = (pltpu.roll(do_pair, HEAD, 1), pltpu.roll(o_pair, HEAD, 1)) if hh == 0 else (do_pair, o_pair)
                do = jnp.where(upper, do, 0.0)
                delta = jnp.sum((do * ov).T, axis=0, keepdims=True)
                s_t = lax.dot_general(k, q, _NT, preferred_element_type=F32)
                if diagonal:
                    s_t = jnp.where(causal_t, s_t, -1e30)
                p_t = jnp.exp2(s_t * c2 - lse_ref[hh, :, rows] * _LOG2E)
                dob = do.astype(BF16)
                dv = dv + jnp.dot(p_t.astype(BF16), dob, preferred_element_type=F32)
                dp_t = lax.dot_general(v, dob, _NT, preferred_element_type=F32)
                ds_t = (p_t * (dp_t - delta) * scale).astype(BF16)
                dk = dk + jnp.dot(ds_t, q, preferred_element_type=F32)
                dq_ref[rows, cols] += lax.dot_general(ds_t, k, _TN, preferred_element_type=F32)
                new.append((dk, dv))
            return tuple(new)

        zero = jnp.zeros((tq, LANES), F32)
        carry = block(ki, tq, ((zero, zero), (zero, zero)), True)
        rest = nb - 1 - ki
        carry = lax.fori_loop(0, rest // 2, lambda g, c: block(ki + 1 + 2 * g, 2 * tq, c, False), carry)
        carry = lax.cond(rest % 2 == 1, lambda c: block(nb - 1, tq, c, False), lambda c: c, carry)
        dkv_ref[...] = jnp.concatenate([jnp.where(lane < HEAD, dk, dv) for dk, dv in carry],
                                       axis=1).astype(dkv_ref.dtype)
        dk_ref[...] = jnp.concatenate([dk for dk, _ in carry], axis=1)

    pair_all = pl.BlockSpec((t_rows, 2 * LANES), lambda pr, ki: (0, pr))
    pair_blk = pl.BlockSpec((tq, 2 * LANES), lambda pr, ki: (ki, pr))
    wide = jax.ShapeDtypeStruct((t_rows, N_ATT_HEADS * LANES), F32)
    return _call(
        body, (qr, kf, kvb, dcat, o, lse), name=name, grid=(N_ATT_HEADS // 2, nb),
        in_specs=[pair_all, pair_blk, pair_blk,
                  pl.BlockSpec((t_rows, LANES), lambda pr, ki: (0, do_off + pr)),
                  pl.BlockSpec((t_rows, LANES), lambda pr, ki: (0, pr)),
                  pl.BlockSpec((2, 1, t_rows), lambda pr, ki: (pr, 0, 0))],
        out_specs=[pair_all, pair_blk, pair_blk],
        out_shape=[wide, jax.ShapeDtypeStruct(wide.shape, BF16), wide],
        sem=("parallel", "arbitrary"), job=job)


def _my_pos():
    return lax.axis_index("x"), lax.axis_index("y"), lax.axis_index("c")


def _all_gather(xs, *, name, columns=True):
    return _gather_forward(_run_job(_gather_job(xs, columns), name=name), name=name + "_forward")


def _remote(src, dst, send_sems, recv_sems, k, dev):
    return pltpu.make_async_remote_copy(src_ref=src, dst_ref=dst, send_sem=send_sems.at[k], recv_sem=recv_sems.at[k],
                                        device_id=dev, device_id_type=MESH)


def _block(ref, idx):
    if len(ref.shape) == 2:
        return ref.at[:, pl.ds(pl.multiple_of(idx * LANES, LANES), LANES)]
    return ref.at[idx]


def _gather_job(xs, columns=True):
    n = len(xs)

    def make(x_refs, out_refs, send_sems, recv_sems, local_sems):
        mx, my, mc = _my_pos()
        mine = 4 * mx + 2 * my + mc
        peers = [(mx, my, 1 - mc), (1 - mx, my, mc), (mx, 1 - my, mc), (1 - mx, 1 - my, mc)]
        sends, recvs, local = [], [], []
        for a in range(n):
            local.append(pltpu.make_async_copy(x_refs[a], _block(out_refs[a], mine), local_sems.at[a]))
            for k, dev in enumerate(peers):
                theirs = 4 * dev[0] + 2 * dev[1] + dev[2]
                sends.append(_remote(x_refs[a], _block(out_refs[a], mine), send_sems, recv_sems, 4 * a + k, dev))
                recvs.append(_remote(x_refs[a], _block(out_refs[a], theirs), send_sems, recv_sems, 4 * a + k, dev))
        return sends, recvs, local

    def gathered(x):
        if columns and x.ndim == 2 and x.shape[1] == LANES:
            return jax.ShapeDtypeStruct((x.shape[0], N_DEV * LANES), x.dtype)
        return jax.ShapeDtypeStruct((N_DEV,) + x.shape, x.dtype)

    return _copies_job(xs, [gathered(x) for x in xs], 4 * n, n, make)


def _forward_job(gs):
    n = len(gs)

    def make(in_refs, out_refs, send_sems, recv_sems, local_sems):
        mx, my, mc = _my_pos()
        chips = [(1 - mx, my), (mx, 1 - my), (1 - mx, 1 - my)]
        sends, recvs = [], []
        for a in range(n):
            for j, (cx, cy) in enumerate(chips):
                here = _block(out_refs[a], 4 * cx + 2 * cy + mc)
                there = _block(out_refs[a], 4 * cx + 2 * cy + 1 - mc)
                sends.append(_remote(here, here, send_sems, recv_sems, 3 * a + j, (mx, my, 1 - mc)))
                recvs.append(_remote(here, there, send_sems, recv_sems, 3 * a + j, (mx, my, 1 - mc)))
        return sends, recvs, []

    shapes = [jax.ShapeDtypeStruct(g.shape, g.dtype) for g in gs]
    return _copies_job(gs, shapes, 3 * n, 0, make, in_place=True)


def _gather_forward(gs, *, name):
    return _run_job(_forward_job(gs), name=name)


def _pair_job(xs):
    n = len(xs)

    def make(x_refs, out_refs, send_sems, recv_sems, local_sems):
        mx, my, mc = _my_pos()

        def src(ref, g):
            return _block(ref, 2 * g + 1 - mc) if len(ref.shape) == 2 else ref.at[g, 1 - mc]

        copies = [_remote(src(x_refs[a], g), out_refs[a].at[g], send_sems, recv_sems, 4 * a + g, (mx, my, 1 - mc))
                  for a in range(n) for g in range(4)]
        return copies, copies, []

    shapes = [jax.ShapeDtypeStruct((4, x.shape[0], LANES) if x.ndim == 2 else (4,) + x.shape[2:], x.dtype)
              for x in xs]
    return _copies_job(xs, shapes, 4 * n, 0, make)


def _pair_add(x, r, core, *, name):
    _, a, b = r.shape
    ta = _row_tile(a, 512)

    def body(c_ref, x_ref, r_ref, o_ref):
        o_ref[...] = (x_ref[...] + r_ref[...]).astype(o_ref.dtype)

    blk = pl.BlockSpec((None, ta, b), lambda g, i, c_ref: (g, i, 0))
    own = (pl.BlockSpec((ta, b), lambda g, i, c_ref: (i, 2 * g + c_ref[0])) if x.ndim == 2
           else pl.BlockSpec((None, None, ta, b), lambda g, i, c_ref: (g, c_ref[0], i, 0)))
    return pl.pallas_call(
        body, name=name,
        grid_spec=pltpu.PrefetchScalarGridSpec(
            num_scalar_prefetch=1, grid=(4, a // ta), in_specs=[own, blk], out_specs=blk),
        out_shape=jax.ShapeDtypeStruct((4, a, b), BF16),
        compiler_params=_cparams(("parallel", "parallel")),
    )(core, x, r)


def _quad_job(xs):
    n = len(xs)

    def make(x_refs, out_refs, send_sems, recv_sems, local_sems):
        mx, my, mc = _my_pos()
        mine = 2 * mx + my
        peers = [((1 - mx, my, mc), 2 * (1 - mx) + my), ((mx, 1 - my, mc), 2 * mx + 1 - my),
                 ((1 - mx, 1 - my, mc), 2 * (1 - mx) + 1 - my)]
        sends, recvs, local = [], [], []
        for a in range(n):
            local.append(pltpu.make_async_copy(x_refs[a].at[mine], out_refs[a].at[mine], local_sems.at[a]))
            for k, (dev, g) in enumerate(peers):
                sends.append(_remote(x_refs[a].at[g], out_refs[a].at[mine], send_sems, recv_sems, 3 * a + k, dev))
                recvs.append(_remote(x_refs[a].at[g], out_refs[a].at[g], send_sems, recv_sems, 3 * a + k, dev))
        return sends, recvs, local

    shapes = [jax.ShapeDtypeStruct(x.shape, x.dtype) for x in xs]
    return _copies_job(xs, shapes, 3 * n, n, make)


def _row_tile(r, pref):
    t = min(pref, r)
    while r % t or (t % 8 and t != r):
        t -= 1
    return t


def _adamw(parts, w, m, v, layer, *, name, tile=512, into=None):
    g, a, b = parts.shape
    tile = _row_tile(a, tile)
    c1 = 1.0 / (1.0 - ADAM_B1 ** ADAM_STEP)
    c2 = 1.0 / (1.0 - ADAM_B2 ** ADAM_STEP)
    into = tuple(into or ())

    def body(p_ref, w_ref, m_ref, v_ref, *refs):
        g_ref, d_ref, mo_ref, vo_ref = refs[len(into):]
        grad = p_ref[0].astype(F32)
        for j in range(1, g):
            grad = grad + p_ref[j].astype(F32)
        mn = ADAM_B1 * m_ref[...] + (1.0 - ADAM_B1) * grad
        vn = ADAM_B2 * v_ref[...] + (1.0 - ADAM_B2) * (grad * grad)
        g_ref[...] = grad
        mo_ref[...] = mn
        vo_ref[...] = vn
        d_ref[...] = -ADAM_LR * ((mn * c1) / (jnp.sqrt(vn * c2) + ADAM_EPS) + ADAM_WD * w_ref[...])

    if layer is None:
        src, shape = pl.BlockSpec((tile, b), lambda i: (i, 0)), (a, b)
    else:
        src, shape = pl.BlockSpec((None, tile, b), lambda i: (layer, i, 0)), w.shape
    return pl.pallas_call(
        body, name=name, grid=(a // tile,),
        in_specs=[pl.BlockSpec((g, tile, b), lambda i: (0, i, 0)), src, src, src] + [_ANY] * len(into),
        out_specs=[src] * 4,
        out_shape=[jax.ShapeDtypeStruct(shape, F32)] * 4,
        input_output_aliases={4 + i: i for i in range(len(into))},
        compiler_params=_cparams(("parallel",)),
    )(parts, w, m, v, *into)


W_IN_SHARD = 276


def _w_in_dest(col):
    return jnp.where(col < P_KR, col, jnp.where(col < P_KR + 256, col + (P_CKV - P_KR), col - 2176 + P_KR + HEAD))


PLACE_TILE = 384
PLACE_SHARDS = 3
PICK_TILE = 128
PICK_TILES = 4


def _w_in_tables():
    col = np.arange(N_DEV * W_IN_SHARD)
    dest = np.where(col < P_KR, col, np.where(col < P_KR + 256, col + (P_CKV - P_KR), col - 2176 + P_KR + HEAD))
    shard = col // W_IN_SHARD

    def filled(used, universe, n):
        used = sorted(set(int(u) for u in used))
        assert len(used) <= n, used
        return used + [u for u in universe if u not in used][:n - len(used)]

    place = [filled(shard[dest // PLACE_TILE == c], range(N_DEV), PLACE_SHARDS) for c in range(P_COLS // PLACE_TILE)]
    pick = [filled(dest[shard == j] // PICK_TILE, range(P_COLS // PICK_TILE), PICK_TILES) for j in range(N_DEV)]
    return np.asarray(place, np.int32).reshape(-1), np.asarray(pick, np.int32).reshape(-1)


def _place_w_in(g, *, name):
    _, d, sh = g.shape
    tc, ns = PLACE_TILE, PLACE_SHARDS
    table = jnp.asarray(_w_in_tables()[0])

    def body(tab_ref, g_ref, o_ref, acc_ref):
        ct, s = pl.program_id(0), pl.program_id(1)
        j = tab_ref[ct * ns + s]

        @pl.when(s == 0)
        def _():
            acc_ref[...] = jnp.zeros_like(acc_ref)

        src = j * sh + lax.broadcasted_iota(jnp.int32, (sh, tc), 0)
        dst = ct * tc + lax.broadcasted_iota(jnp.int32, (sh, tc), 1)
        place = (_w_in_dest(src) == dst).astype(BF16)
        acc_ref[...] += jnp.dot(g_ref[...], place, preferred_element_type=F32)

        @pl.when(s == ns - 1)
        def _():
            o_ref[...] = acc_ref[...].astype(o_ref.dtype)

    return pl.pallas_call(
        body, name=name,
        grid_spec=pltpu.PrefetchScalarGridSpec(
            num_scalar_prefetch=1, grid=(P_COLS // tc, ns),
            in_specs=[pl.BlockSpec((None, d, sh), lambda ct, s, tab: (tab[ct * ns + s], 0, 0))],
            out_specs=pl.BlockSpec((d, tc), lambda ct, s, tab: (0, ct)),
            scratch_shapes=[pltpu.VMEM((d, tc), F32)]),
        out_shape=jax.ShapeDtypeStruct((d, P_COLS), BF16),
        compiler_params=_cparams(("parallel", "arbitrary")),
    )(table, g)


def _unplace_w_in(dw, *, name):
    d = dw.shape[0]
    sh, tk, nt = W_IN_SHARD, PICK_TILE, PICK_TILES
    table = jnp.asarray(_w_in_tables()[1])

    def body(tab_ref, dw_ref, o_ref):
        j, kk = pl.program_id(0), pl.program_id(1)
        tile = tab_ref[j * nt + kk]
        src = j * sh + lax.broadcasted_iota(jnp.int32, (tk, sh), 1)
        dst = tile * tk + lax.broadcasted_iota(jnp.int32, (tk, sh), 0)
        pick = (_w_in_dest(src) == dst).astype(BF16)
        part = _split_dot(dw_ref[...], pick)

        @pl.when(kk == 0)
        def _():
            o_ref[...] = part

        @pl.when(kk > 0)
        def _():
            o_ref[...] += part

    return pl.pallas_call(
        body, name=name,
        grid_spec=pltpu.PrefetchScalarGridSpec(
            num_scalar_prefetch=1, grid=(N_DEV, nt),
            in_specs=[pl.BlockSpec((d, tk), lambda j, kk, tab: (0, tab[j * nt + kk]))],
            out_specs=pl.BlockSpec((None, d, sh), lambda j, kk, tab: (j, 0, 0))),
        out_shape=jax.ShapeDtypeStruct((N_DEV, d, sh), F32),
        compiler_params=_cparams(("parallel", "arbitrary")),
    )(table, dw)


def _gate_up_swiglu(h1, wgu, *, name):
    t_rows, k = h1.shape
    w = wgu.shape[2]
    tm = _tile(t_rows, 1024)

    def body(a_ref, wg_ref, wu_ref, gu_ref, act_ref):
        a = a_ref[...].astype(BF16)
        gate = jnp.dot(a, wg_ref[...], preferred_element_type=F32)
        up = jnp.dot(a, wu_ref[...], preferred_element_type=F32)
        gu_ref[0] = gate.astype(gu_ref.dtype)
        gu_ref[1] = up.astype(gu_ref.dtype)
        act_ref[...] = (gate * _sigmoid(gate) * up).astype(act_ref.dtype)

    return pl.pallas_call(
        body, name=name, grid=(t_rows // tm, 4),
        in_specs=[pl.BlockSpec((tm, k), lambda i, j: (i, 0)),
                  pl.BlockSpec((None, k, w), lambda i, j: (j, 0, 0)),
                  pl.BlockSpec((None, k, w), lambda i, j: (j + 4, 0, 0))],
        out_specs=[pl.BlockSpec((2, None, tm, w), lambda i, j: (0, j, i, 0)),
                   pl.BlockSpec((None, tm, w), lambda i, j: (j, i, 0))],
        out_shape=[jax.ShapeDtypeStruct((2, 4, t_rows, w), BF16), jax.ShapeDtypeStruct((4, t_rows, w), BF16)],
        compiler_params=_cparams(("parallel", "arbitrary")),
    )(h1, wgu, wgu)


def _down_dx_swiglu(dffn, wdown, gu, *, name):
    t_rows, k = dffn.shape
    w = gu.shape[3]
    tm = _tile(t_rows, 1024)

    def body(d_ref, w_ref, gu_ref, o_ref):
        dact = lax.dot_general(d_ref[...].astype(BF16), w_ref[...], _NT, preferred_element_type=F32)
        gate, up = gu_ref[0].astype(F32), gu_ref[1].astype(F32)
        sg = _sigmoid(gate)
        silu = gate * sg
        o_ref[0] = (dact * up * (sg + silu - silu * sg)).astype(o_ref.dtype)
        o_ref[1] = (dact * silu).astype(o_ref.dtype)

    blk = pl.BlockSpec((2, None, tm, w), lambda i, j: (0, j, i, 0))
    return pl.pallas_call(
        body, name=name, grid=(t_rows // tm, 4),
        in_specs=[pl.BlockSpec((tm, k), lambda i, j: (i, 0)), pl.BlockSpec((w, k), lambda i, j: (j, 0)), blk],
        out_specs=blk, out_shape=jax.ShapeDtypeStruct(gu.shape, BF16),
        compiler_params=_cparams(("parallel", "arbitrary")),
    )(dffn, wdown, gu)


BIG = ("w_in", "mla_w_uq", "mla_w_ukv", "w_out", "w_gate_up", "w_down", "ple_w_gate", "ple_w_proj")
SMALL = ("ln_in_g", "ln_in_b", "hgrn_lb_logits", "hgrn_norm_g", "sgu_ln_g", "sgu_ln_b", "sgu_w_s", "sgu_b_s",
         "mla_q_norm_g", "mla_kv_norm_g", "ln1_g", "ln1_b", "ln2_g", "ln2_b")
ORDER = ("ln_in_g", "ln_in_b", "w_in", "hgrn_lb_logits", "hgrn_norm_g", "sgu_ln_g", "sgu_ln_b", "sgu_w_s", "sgu_b_s",
         "mla_q_norm_g", "mla_w_uq", "mla_kv_norm_g", "mla_w_ukv", "w_out", "ln1_g", "ln1_b", "w_gate_up", "w_down",
         "ple_w_gate", "ple_w_proj", "ln2_g", "ln2_b")


def _slab(a, align):
    s = a.reshape(-1, LANES)
    pad = -s.shape[0] % align
    return jnp.pad(s, ((0, pad), (0, 0))) if pad else s


def _pack(arrays, align=16, total_align=512):
    s = jnp.concatenate([_slab(a, align) for a in arrays], axis=0)
    pad = -s.shape[0] % total_align
    return jnp.pad(s, ((0, pad), (0, 0))) if pad else s


def _unpack(slab, shapes, align=16):
    out, r0 = [], 0
    for s in shapes:
        nr = math.prod(s) // LANES
        out.append(slab[r0:r0 + nr].reshape(s))
        r0 += nr + (-nr % align)
    return out


def _weight_shards(w, li):
    uq_pad = ((0, 0), (0, LANES - ATT_D))
    shards = {k: w[k][li] for k in BIG}
    shards["mla_w_uq"] = jnp.pad(shards["mla_w_uq"], uq_pad)
    return {k: s.astype(BF16) for k, s in shards.items()}


def _usable_weights(g, *, name):
    out = {}
    for k, a in g.items():
        if k == "w_in":
            out[k] = _place_w_in(a, name=name + "_place_w_in")
        elif k in ("w_out", "w_down", "ple_w_gate"):
            out[k] = a.reshape(a.shape[0] * a.shape[1], a.shape[2])
        else:
            out[k] = a
    return out


BY_COLUMNS = ("mla_w_uq", "mla_w_ukv", "ple_w_proj")


def _as_pairs(k, g):
    if k in BY_COLUMNS:
        return g
    if g.ndim == 2:
        return g.reshape((4, 2, g.shape[0] // N_DEV) + g.shape[1:])
    return g.reshape((4, 2) + g.shape[1:])


def _twice(fn):
    return lambda *a: fn(*a) * 2


def _layer_forward(li, h, hb, p_i, wts, sm, lbs, tables, alpha, hgrn_job=None, after_hgrn=None, attn_job=None,
                   after_attn=None, loss_target=None):
    n = f"l{li}_"
    row1 = lambda a: a.reshape(1, -1)
    projp = _mm(hb, wts["w_in"], name=n + "proj")
    ng = row1(sm["hgrn_norm_g"][li])
    res = _hgrn_fwd(projp, lbs[li], ng, name=n + "hgrn_fwd", job=hgrn_job)
    if hgrn_job is not None:
        res, got = res
    o_a, o_pre, states = res
    lg, lbias = row1(sm["sgu_ln_g"][li]), row1(sm["sgu_ln_b"][li])
    w_s = sm["sgu_w_s"][li]
    bias_full = jnp.repeat(sm["sgu_b_s"][li].T, HEAD, axis=1)
    o_b = _sgu_fwd(projp, lg, lbias, w_s, bias_full, name=n + "sgu_fwd",
                   job=None if hgrn_job is None else _forward_job(got))
    if hgrn_job is not None:
        o_b, got = o_b
        wts = dict(wts, **after_hgrn(got))
    qg, kvg = row1(sm["mla_q_norm_g"][li]), row1(sm["mla_kv_norm_g"][li])
    cq_view, ckv_view = (projp, 384, P_CQ // 384), (projp, 256, P_CKV // 256)
    (cqn,) = _rowwise(_fn_rms, [cq_view], [qg], [(384, BF16)], tile=2048, name=n + "q_norm")
    (ckvn,) = _rowwise(_fn_rms, [ckv_view], [kvg], [(256, BF16)], tile=2048, name=n + "kv_norm")
    q = _mm(cqn, wts["mla_w_uq"], name=n + "uq")
    kv = _mm(ckvn, wts["mla_w_ukv"], name=n + "ukv")
    qr, kf, kvb = _mla_prep(q, kv, projp, tables, name=n + "mla_prep")
    res = _attn_fwd(qr, kf, kvb, name=n + "attn_fwd", job=attn_job)
    if attn_job is not None:
        res, got = res
    o_c, lse = res
    cat = jnp.concatenate([o_a, o_b, o_c.astype(BF16)], axis=1)
    mix = _mm(cat, wts["w_out"], name=n + "out_proj", job=None if attn_job is None else _forward_job(got))
    if attn_job is not None:
        mix, got = mix
        wts = dict(wts, **after_attn(got))
    g1, b1 = row1(sm["ln1_g"][li]), row1(sm["ln1_b"][li])
    d = h.shape[1]
    h1, h1b = _rowwise(_twice(_make_post_mix(alpha)), [h, mix], [g1, b1], [(d, F32), (d, BF16)], tile=1024,
                       name=n + "ln1")
    gu, act = _gate_up_swiglu(h1b, wts["w_gate_up"], name=n + "gate_up")
    ffn = _mm_kblocks(act, wts["w_down"], bm="kn", tm=1024, name=n + "down")
    pg = _mm(h1b, wts["ple_w_gate"], name=n + "ple_gate")
    pp = _mm(p_i, wts["ple_w_proj"], name=n + "ple_proj")
    g2, b2 = row1(sm["ln2_g"][li]), row1(sm["ln2_b"][li])
    if loss_target is None:
        out = _rowwise(_twice(_make_ple_ln(alpha)), [h1, ffn, pg, pp], [g2, b2], [(d, F32), (d, BF16)],
                       name=n + "ln2")
    else:
        def ln_and_loss(h1v, ffnv, pgv, ppv, tv, gv, bv):
            err = _make_ple_ln(alpha)(h1v, ffnv, pgv, ppv, gv, bv)[0] - tv
            return err * (1.0 / d), 0.5 * jnp.sum(jnp.mean(err * err, axis=-1, keepdims=True), axis=0, keepdims=True)

        out = _rowwise(ln_and_loss, [h1, ffn, pg, pp, loss_target], [g2, b2], [(d, F32)], accs=[(1, 1)],
                       name=n + "ln2_loss")
    saved = dict(h=h, hb=hb, h1b=h1b, projp=projp, o_pre=o_pre, states=states, cqn=cqn, ckvn=ckvn, qr=qr, kf=kf, kvb=kvb, o_c=o_c,
                 lse=lse, cat=cat, mix=mix, h1=h1, gu=gu, act=act, ffn=ffn, pg=pg, pp=pp, ng=ng, lg=lg, wts=wts,
                 lbias=lbias, w_s=w_s, bias_full=bias_full, qg=qg, kvg=kvg, g1=g1, b1=b1, g2=g2, b2=b2)
    return tuple(out), saved


RS_EARLY = ("ple_w_proj", "ple_w_gate", "w_down", "w_gate_up", "w_out")
RS_LATE = ("mla_w_uq", "mla_w_ukv", "w_in")


def _layer_backward(li, dh2_parts, p_i, sv, lbs, tables, alpha, core, carried=None):
    n = f"l{li}_b_"
    wts = sv["wts"]
    gr = {}
    dh1_a, dffn, dpg, dpp, gr["ln2_g"], gr["ln2_b"] = _rowwise_vjp(
        _make_ple_ln(alpha), [sv["h1"], sv["ffn"], sv["pg"], sv["pp"]], [sv["g2"], sv["b2"]], [dh2_parts],
        groups=[[0], [1], [2], [3]], gdtypes=[F32, BF16, BF16, BF16], name=n + "ln2")
    big = {}
    big["ple_w_proj"] = _mm(p_i, dpp, am="km", tk=2048, name=n + "ple_proj_dw")
    big["ple_w_gate"] = _mm(sv["h1b"], dpg, am="km", name=n + "ple_gate_dw")
    dh1_b = _mm(dpg, wts["ple_w_gate"], bm="nk", name=n + "ple_gate_dx")
    big["w_down"] = _mm(sv["act"], dffn, am="bkm", tk=4096, name=n + "down_dw")
    dgu = _down_dx_swiglu(dffn, wts["w_down"], sv["gu"], name=n + "down_dx")
    dgu = dgu.reshape((N_DEV,) + dgu.shape[2:])
    big["w_gate_up"], carried_got = _mm(sv["h1b"], dgu, am="km", bm="bkn", om="bmn", tk=4096, name=n + "gate_up_dw",
                                        job=carried), None
    if carried is not None:
        big["w_gate_up"], carried_got = big["w_gate_up"]
    early = [_as_pairs(k, big[k]) for k in RS_EARLY[:-1]]
    dh1_c, theirs = _mm_kblocks(dgu, wts["w_gate_up"], bm="bnk", tm=512, name=n + "gate_up_dx",
                                job=_pair_job(early))
    dh_a, dmix, gr["ln1_g"], gr["ln1_b"] = _rowwise_vjp(
        _make_post_mix(alpha), [sv["h"], sv["mix"]], [sv["g1"], sv["b1"]], [[dh1_a, dh1_b, dh1_c]],
        groups=[[0], [1]], gdtypes=[F32, BF16], name=n + "ln1")
    big["w_out"] = _mm(sv["cat"], dmix, am="km", name=n + "out_proj_dw")
    early.append(_as_pairs("w_out", big["w_out"]))
    dcat, their_w_out = _mm(dmix, wts["w_out"], bm="nk", name=n + "out_proj_dx", job=_pair_job(early[-1:]))
    sums = [_pair_add(x, r, core, name=n + "pair_add_" + k)
            for k, x, r in zip(RS_EARLY, early, list(theirs) + list(their_w_out))]

    (dqr, dkv, dkf), early_quads = _attn_bwd(sv["qr"], sv["kf"], sv["kvb"], dcat, sv["o_c"], sv["lse"],
                                             name=n + "attn", job=_quad_job(sums))
    dqpad, dkr = _mla_prep_bwd(dqr, dkf, tables, name=n + "mla_prep")
    big["mla_w_uq"] = _mm(sv["cqn"], dqpad, am="km", tk=2048, name=n + "uq_dw")
    dcqn = _mm(dqpad, wts["mla_w_uq"], bm="nk", name=n + "uq_dx")
    big["mla_w_ukv"] = _mm(sv["ckvn"], dkv, am="km", tk=2048, name=n + "ukv_dw")
    dckvn = _mm(dkv, wts["mla_w_ukv"], bm="nk", name=n + "ukv_dx")
    projp = sv["projp"]
    dcq, gr["mla_q_norm_g"] = _rowwise_vjp(_fn_rms, [(projp, 384, P_CQ // 384)], [sv["qg"]], [[dcqn]],
                                           groups=[[0]], gdtypes=[BF16], tile=2048, name=n + "q_norm")
    dckv, gr["mla_kv_norm_g"] = _rowwise_vjp(_fn_rms, [(projp, 256, P_CKV // 256)], [sv["kvg"]], [[dckvn]],
                                             groups=[[0]], gdtypes=[BF16], tile=2048, name=n + "kv_norm")
    dsgu, gr["sgu_ln_g"], gr["sgu_ln_b"], gr["sgu_w_s"], gr["sgu_b_s"] = _sgu_bwd(
        projp, sv["lg"], sv["lbias"], sv["w_s"], sv["bias_full"], dcat, name=n + "sgu")
    dhg, gr["hgrn_norm_g"], gr["lower_bound"] = _hgrn_bwd(
        projp, lbs[li], sv["ng"], sv["o_pre"], sv["states"], dcat, name=n + "hgrn")
    dprojp = jnp.concatenate([dhg, dsgu, dcq, dkr, dckv], axis=1)
    big["w_in"] = _unplace_w_in(_mm(sv["hb"], dprojp, am="km", tk=4096, name=n + "proj_dw"),
                                name=n + "proj_dw_shards")
    late = [_as_pairs(k, big[k]) for k in RS_LATE]
    dh_b, theirs = _mm(dprojp, wts["w_in"], bm="nk", tk=P_COLS, name=n + "proj_dx", job=_pair_job(late))
    late_sums = [_pair_add(x, r, core, name=n + "pair_add_" + k) for k, x, r in zip(RS_LATE, late, theirs)]
    return [dh_a, dh_b], gr, early_quads, late_sums, carried_got


def kernel(x, p, positions, ln_in_g, ln_in_b, w_in, hgrn_lb_logits, hgrn_norm_g, sgu_ln_g, sgu_ln_b, sgu_w_s, sgu_b_s, mla_q_norm_g, mla_w_uq, mla_kv_norm_g, mla_w_ukv, w_out, ln1_g, ln1_b, w_gate_up, w_down, ple_w_gate, ple_w_proj, ln2_g, ln2_b, loss_target, m_ln_in_g, m_ln_in_b, m_w_in, m_hgrn_lb_logits, m_hgrn_norm_g, m_sgu_ln_g, m_sgu_ln_b, m_sgu_w_s, m_sgu_b_s, m_mla_q_norm_g, m_mla_w_uq, m_mla_kv_norm_g, m_mla_w_ukv, m_w_out, m_ln1_g, m_ln1_b, m_w_gate_up, m_w_down, m_ple_w_gate, m_ple_w_proj, m_ln2_g, m_ln2_b, v_ln_in_g, v_ln_in_b, v_w_in, v_hgrn_lb_logits, v_hgrn_norm_g, v_sgu_ln_g, v_sgu_ln_b, v_sgu_w_s, v_sgu_b_s, v_mla_q_norm_g, v_mla_w_uq, v_mla_kv_norm_g, v_mla_w_ukv, v_w_out, v_ln1_g, v_ln1_b, v_w_gate_up, v_w_down, v_ple_w_gate, v_ple_w_proj, v_ln2_g, v_ln2_b):
    args = dict(locals())
    w = {k: args[k] for k in ORDER}
    m = {k: args["m_" + k] for k in ORDER}
    v = {k: args["v_" + k] for k in ORDER}
    depth = w_in.shape[0]
    assert depth == 2, "the lower-bound kernel is written for two layers"
    alpha = (2 * depth) ** 0.25
    xs, tgt = x[0], loss_target[0]
    d_model = xs.shape[1]

    shards = [_weight_shards(w, li) for li in range(depth)]
    on_hgrn0 = ("mla_w_uq", "mla_w_ukv", "w_out", "ple_w_gate", "ple_w_proj")
    ffn0 = ("w_gate_up", "w_down")
    first1 = ("w_in", "mla_w_uq", "mla_w_ukv", "w_out")
    on_attn1 = ("w_gate_up", "w_down", "ple_w_gate", "ple_w_proj")
    layer1_first = {}

    def after_hgrn0(got):
        return _usable_weights(dict(zip(on_hgrn0, got)), name="l0")

    def after_attn0(got):
        layer1_first.update(_usable_weights(dict(zip(first1, got[len(ffn0):])), name="l1"))
        return _usable_weights(dict(zip(ffn0, got[:len(ffn0)])), name="l0")

    def after_attn1(got):
        return _usable_weights(dict(zip(on_attn1, got)), name="l1")

    tables = _rope_tables(positions[0])
    row1 = lambda a: a.reshape(1, -1)
    l0, l1 = row1(hgrn_lb_logits[0]), row1(hgrn_lb_logits[1])
    lbs = _rowwise(_fn_lower_bounds, [l0, l1], [], [(HG_W, F32), (HG_W, F32)], name="lower_bounds")

    gin, bin_ = row1(ln_in_g), row1(ln_in_b)
    (h, hb), g_in = _rowwise(_twice(_fn_ln), [xs], [gin, bin_], [(d_model, F32), (d_model, BF16)], name="ln_in",
                             job=_gather_job([shards[0]["w_in"]]))
    w_in0 = _usable_weights({"w_in": _gather_forward(g_in, name="gather_l0_w_in_forward")[0]}, name="l0")
    (h, hb), sv0 = _layer_forward(
        0, h, hb, p[0, 0], w_in0, w, lbs, tables, alpha,
        hgrn_job=_gather_job([shards[0][k] for k in on_hgrn0]), after_hgrn=after_hgrn0,
        attn_job=_gather_job([shards[0][k] for k in ffn0] + [shards[1][k] for k in first1]), after_attn=after_attn0)
    (dy, loss_local), sv1 = _layer_forward(
        1, h, hb, p[1, 0], layer1_first, w, lbs, tables, alpha,
        attn_job=_gather_job([shards[1][k] for k in on_attn1]), after_attn=after_attn1, loss_target=tgt)
    saved = [sv0, sv1]
    loss = lax.psum(loss_local[0, 0], ("x", "y", "c"))

    core = lax.axis_index("c").astype(jnp.int32).reshape(1)
    dparts, grads, quads, carried = [dy], [None] * depth, [None] * depth, None
    for li in reversed(range(depth)):
        dparts, grads[li], early_quads, late_sums, late_quads = _layer_backward(
            li, dparts, p[li, 0], saved[li], lbs, tables, alpha, core, carried=carried)
        quads[li] = dict(zip(RS_EARLY, early_quads))
        if carried is not None:
            quads[li + 1].update(zip(RS_LATE, late_quads))
        carried = _quad_job(late_sums)
    (dx, d_gin, d_bin), late_quads = _rowwise_vjp(_fn_ln, [xs], [gin, bin_], [dparts], groups=[[0]], name="ln_in_b",
                                                   job=carried)
    quads[0].update(zip(RS_LATE, late_quads))
    dl0, dl1 = _rowwise_vjp(_fn_lower_bounds, [l0, l1], [], [[grads[0]["lower_bound"]], [grads[1]["lower_bound"]]],
                            groups=[[0], [1]], name="lower_bounds_b")

    prefixes = ("grad_", "delta_", "new_m_", "new_v_")
    uq_pad = ((0, 0), (0, 0), (0, LANES - ATT_D))
    state = {k: ((jnp.pad(w[k], uq_pad), jnp.pad(m[k], uq_pad), jnp.pad(v[k], uq_pad)) if k == "mla_w_uq"
                 else (w[k], m[k], v[k])) for k in BIG}
    out = {}
    for k in BIG:
        res4 = None
        for li in range(depth):
            res4 = _adamw(quads[li][k], *state[k], li, name=f"adamw_l{li}_{k}", into=res4)
        for pre, a in zip(prefixes, res4):
            out[pre + k] = a[:, :, :ATT_D] if k == "mla_w_uq" else a

    small_g = {"ln_in_g": d_gin.reshape(-1), "ln_in_b": d_bin.reshape(-1),
               "hgrn_lb_logits": jnp.stack([dl0.reshape(-1), dl1.reshape(-1)])}
    for k in SMALL[3:]:
        small_g[k] = jnp.stack([grads[li][k].reshape(w[k].shape[1:]) for li in range(depth)])
    (small_parts,) = _all_gather([_pack([small_g[k] for k in SMALL])], name="gather_small_grads", columns=False)
    slabs = _adamw(small_parts, _pack([w[k] for k in SMALL]), _pack([m[k] for k in SMALL]),
                   _pack([v[k] for k in SMALL]), None, name="adamw_small")
    shapes = [w[k].shape for k in SMALL]
    for pre, slab in zip(prefixes, slabs):
        for k, a in zip(SMALL, _unpack(slab, shapes)):
            out[pre + k] = a
    res = [loss, dx[None]]
    for prefix in ("grad_", "delta_", "new_m_", "new_v_"):
        res += [out[prefix + k] for k in ORDER]
    return tuple(res)
```

```python
import functools
import math

import jax
import jax.numpy as jnp
import numpy as np
from jax import lax
from jax.experimental import pallas as pl
from jax.experimental.pallas import tpu as pltpu

F32 = jnp.float32
BF16 = jnp.bfloat16
MESH = pl.DeviceIdType.MESH

LN_EPS = 1e-5
RMS_EPS = 1e-6
ROPE_THETA = 10000.0
ADAM_LR, ADAM_B1, ADAM_B2, ADAM_EPS, ADAM_WD, ADAM_STEP = 0.001, 0.9, 0.999, 1e-08, 0.01, 10

N_DEV = 8
LANES = 128
HG_CHUNK = 16
HG_W = 256
HEAD = 64
SGU_CHUNK = 128
SGU_STEP_CHUNKS = 4
N_ATT_HEADS = 8
ATT_D = 96
VMEM_LIMIT = 56 * 1024 * 1024

HG_TILE = 256
ATT_TQ = 512
ROW_TILE = 512

P_CQ, P_KR, P_CKV, P_COLS = 1536, 1920, 2048, 2304


def _cparams(sem):
    return pltpu.CompilerParams(dimension_semantics=sem, vmem_limit_bytes=VMEM_LIMIT)


_ANY = pl.BlockSpec(memory_space=pl.ANY)


def _call(body, operands, *, name, grid, in_specs, out_specs, out_shape, sem, scratch_shapes=(), job=None):
    if job is None:
        return pl.pallas_call(body, name=name, grid=grid, in_specs=in_specs, out_specs=out_specs, out_shape=out_shape,
                              scratch_shapes=list(scratch_shapes), compiler_params=_cparams(sem))(*operands)
    single = not isinstance(out_shape, (list, tuple))
    shapes = [out_shape] if single else list(out_shape)
    ospecs = [out_specs] if single else list(out_specs)
    ni, no, ns = len(operands), len(shapes), len(scratch_shapes)
    ji, jo = len(job.inputs), len(job.out_shapes)

    def hosted(*refs):
        p = 0
        parts = []
        for cnt in (ni, ji, no, jo, ns):
            parts.append(refs[p:p + cnt])
            p += cnt
        ins, jins, outs, jouts, scr = parts
        jsems = refs[p:]
        ids = [pl.program_id(a) for a in range(len(grid))]
        first = functools.reduce(lambda a, b: a & b, [i == 0 for i in ids])
        last = functools.reduce(lambda a, b: a & b, [i == g - 1 for i, g in zip(ids, grid)])

        @pl.when(first)
        def _():
            job.start(jins, jouts, jsems)

        body(*ins, *outs, *scr)

        @pl.when(last)
        def _():
            job.finish(jins, jouts, jsems)

    res = pl.pallas_call(
        hosted, name=name, grid=grid,
        in_specs=list(in_specs) + [_ANY] * ji, out_specs=ospecs + [_ANY] * jo,
        out_shape=shapes + list(job.out_shapes),
        scratch_shapes=list(scratch_shapes) + [pltpu.SemaphoreType.DMA((c,)) for c in job.sem_counts],
        input_output_aliases=job.aliases(ni, no),
        compiler_params=_cparams(("arbitrary",) * len(grid)),
    )(*operands, *job.inputs)
    own = res[0] if single else res[:no]
    return own, res[no:]


class _Job:
    def __init__(self, inputs, out_shapes, sem_counts, start, finish, in_place=False):
        self.inputs, self.out_shapes, self.sem_counts = list(inputs), list(out_shapes), list(sem_counts)
        self.start, self.finish, self.in_place = start, finish, in_place

    def aliases(self, first_in, first_out):
        return {first_in + i: first_out + i for i in range(len(self.inputs))} if self.in_place else {}


def _copies_job(inputs, out_shapes, n_remote, n_local, make, in_place=False):
    def start(jins, jouts, sems):
        sends, _, local = make(jins, jouts, *sems)
        for cp in local + sends:
            cp.start()

    def finish(jins, jouts, sems):
        sends, recvs, local = make(jins, jouts, *sems)
        for cp in recvs:
            cp.wait_recv()
        for cp in sends:
            cp.wait_send()
        for cp in local:
            cp.wait()

    return _Job(inputs, out_shapes, [n_remote, n_remote, max(n_local, 1)], start, finish, in_place)


def _run_job(job, *, name):
    ji, jo = len(job.inputs), len(job.out_shapes)

    def body(*refs):
        jins, jouts, sems = refs[:ji], refs[ji:ji + jo], refs[ji + jo:]
        job.start(jins, jouts, sems)
        job.finish(jins, jouts, sems)

    return pl.pallas_call(
        body, name=name, out_shape=list(job.out_shapes), in_specs=[_ANY] * ji, out_specs=[_ANY] * jo,
        scratch_shapes=[pltpu.SemaphoreType.DMA((c,)) for c in job.sem_counts],
        input_output_aliases=job.aliases(0, 0),
    )(*job.inputs)


def _tile(n, pref):
    if n % pref == 0:
        return pref
    best = None
    t = LANES
    while t <= min(n, pref):
        if n % t == 0:
            best = t
        t += LANES
    return best if best is not None else n


def _mm(a, b, *, am="mk", bm="kn", om="mn", out_dtype=F32, tm=1024, tn=1024, tk=1024, name, job=None):
    if am == "mk":
        m, k = a.shape
    elif am == "km":
        k, m = a.shape
    elif am == "bmk":
        m, tk = a.shape[1], a.shape[2]
        k = a.shape[0] * tk
    else:
        k, tm = a.shape[1], a.shape[2]
        m = a.shape[0] * tm
    if bm == "kn":
        kb_, n = b.shape
    elif bm == "nk":
        n, kb_ = b.shape
    elif bm == "bkn":
        kb_, tn = b.shape[1], b.shape[2]
        n = b.shape[0] * tn
    else:
        n, tk = b.shape[1], b.shape[2]
        kb_ = b.shape[0] * tk
    assert kb_ == k, (a.shape, b.shape, am, bm)
    tm, tn, tk = _tile(m, tm), _tile(n, tn), _tile(k, tk)
    nk = k // tk
    dims = (((0 if am in ("km", "bkm") else 1,), (1 if bm in ("nk", "bnk") else 0,)), ((), ()))

    a_spec = {"mk": pl.BlockSpec((tm, tk), lambda i, j, kk: (i, kk)),
              "km": pl.BlockSpec((tk, tm), lambda i, j, kk: (kk, i)),
              "bmk": pl.BlockSpec((None, tm, tk), lambda i, j, kk: (kk, i, 0)),
              "bkm": pl.BlockSpec((None, tk, tm), lambda i, j, kk: (i, kk, 0))}[am]
    b_spec = {"kn": pl.BlockSpec((tk, tn), lambda i, j, kk: (kk, j)),
              "nk": pl.BlockSpec((tn, tk), lambda i, j, kk: (j, kk)),
              "bkn": pl.BlockSpec((None, tk, tn), lambda i, j, kk: (j, kk, 0)),
              "bnk": pl.BlockSpec((None, tn, tk), lambda i, j, kk: (kk, j, 0))}[bm]
    if om == "mn":
        o_spec, o_shape = pl.BlockSpec((tm, tn), lambda i, j, kk: (i, j)), (m, n)
    else:
        o_spec, o_shape = pl.BlockSpec((None, tm, tn), lambda i, j, kk: (j, i, 0)), (n // tn, m, tn)

    def body(a_ref, b_ref, o_ref, *acc):
        kk = pl.program_id(2)

        def prod():
            return lax.dot_general(a_ref[...].astype(BF16), b_ref[...].astype(BF16), dims, preferred_element_type=F32)

        if nk == 1:
            o_ref[...] = prod().astype(o_ref.dtype)
            return
        acc_ref, = acc

        @pl.when(kk == 0)
        def _():
            acc_ref[...] = prod()

        if nk > 2:
            @pl.when((kk > 0) & (kk < nk - 1))
            def _():
                acc_ref[...] += prod()

        @pl.when(kk == nk - 1)
        def _():
            o_ref[...] = (acc_ref[...] + prod()).astype(o_ref.dtype)

    return _call(body, (a, b), name=name, grid=(m // tm, n // tn, nk), in_specs=[a_spec, b_spec], out_specs=o_spec,
                 out_shape=jax.ShapeDtypeStruct(o_shape, out_dtype),
                 scratch_shapes=[pltpu.VMEM((tm, tn), F32)] if nk > 1 else [],
                 sem=("parallel", "parallel", "arbitrary"), job=job)


def _mm_kblocks(a, b, *, bm, tm, name, job=None):
    nkb, m, kb = a.shape
    n = b.shape[1]
    tm = _tile(m, tm)

    def body(a_ref, b_ref, o_ref):
        acc = None
        for j in range(nkb):
            if bm == "kn":
                part = jnp.dot(a_ref[j], b_ref[j * kb:(j + 1) * kb, :], preferred_element_type=F32)
            else:
                part = lax.dot_general(a_ref[j], b_ref[j], _NT, preferred_element_type=F32)
            acc = part if acc is None else acc + part
        o_ref[...] = acc

    b_spec = (pl.BlockSpec(b.shape, lambda i: (0, 0)) if bm == "kn" else pl.BlockSpec(b.shape, lambda i: (0, 0, 0)))
    return _call(body, (a, b), name=name, grid=(m // tm,),
                 in_specs=[pl.BlockSpec((nkb, tm, kb), lambda i: (0, i, 0)), b_spec],
                 out_specs=pl.BlockSpec((tm, n), lambda i: (i, 0)), out_shape=jax.ShapeDtypeStruct((m, n), F32),
                 sem=("parallel",), job=job)


def _row_operand(a, tile):
    if isinstance(a, tuple):
        arr, w, j = a
        return arr, pl.BlockSpec((tile, w), lambda i, j=j: (i, j))
    return a, pl.BlockSpec((tile, a.shape[1]), lambda i: (i, 0))


def _const_spec(c):
    nd = c.ndim
    return pl.BlockSpec(c.shape, lambda i, nd=nd: (0,) * nd)


def _rowwise(fn, rows, consts, outs, *, name, accs=(), tile=None, job=None):
    t_rows = (rows[0][0] if isinstance(rows[0], tuple) else rows[0]).shape[0]
    tile = min(tile or ROW_TILE, t_rows)
    arrs, specs = zip(*[_row_operand(a, tile) for a in rows])
    nin, no = len(rows) + len(consts), len(outs)

    def body(*refs):
        res = fn(*[r[...] for r in refs[:nin]])
        for r, v in zip(refs[nin:nin + no], res[:no]):
            r[...] = v.astype(r.dtype)
        if accs:
            a_refs = refs[nin + no:]

            @pl.when(pl.program_id(0) == 0)
            def _():
                for r in a_refs:
                    r[...] = jnp.zeros_like(r)

            for r, v in zip(a_refs, res[no:]):
                r[...] += v

    out_shape = [jax.ShapeDtypeStruct((t_rows, w), dt) for w, dt in outs]
    out_shape += [jax.ShapeDtypeStruct(s, F32) for s in accs]
    out_specs = [pl.BlockSpec((tile, w), lambda i: (i, 0)) for w, _ in outs]
    out_specs += [pl.BlockSpec(s, lambda i, nd=len(s): (0,) * nd) for s in accs]
    return _call(body, (*arrs, *consts), name=name, grid=(t_rows // tile,),
                 in_specs=list(specs) + [_const_spec(c) for c in consts],
                 out_specs=out_specs, out_shape=out_shape, sem=("arbitrary",), job=job)


def _rowwise_vjp(fn, rows, consts, cts, *, name, groups, tile=None, gdtypes=None, job=None):
    t_rows = (rows[0][0] if isinstance(rows[0], tuple) else rows[0]).shape[0]
    tile = min(tile or ROW_TILE, t_rows)
    arrs, specs = zip(*[_row_operand(a, tile) for a in rows])
    flat_cts = [c for group in cts for c in group]
    ct_arrs, ct_specs = zip(*[_row_operand(a, tile) for a in flat_cts])
    nr, nc, nct, ng = len(rows), len(consts), len(flat_cts), len(groups)

    def width(a):
        return a[1] if isinstance(a, tuple) else a.shape[1]

    def body(*refs):
        rv = [r[...].astype(F32) for r in refs[:nr]]
        cv = [r[...] for r in refs[nr:nr + nc]]
        ct_refs = refs[nr + nc:nr + nc + nct]
        ctv, pos = [], 0
        for group in cts:
            s = ct_refs[pos][...].astype(F32)
            for r in ct_refs[pos + 1:pos + len(group)]:
                s = s + r[...].astype(F32)
            ctv.append(s)
            pos += len(group)
        _, pull = jax.vjp(fn, *rv, *cv)
        grads = pull(tuple(ctv))
        g_refs = refs[nr + nc + nct:nr + nc + nct + ng]
        for r, idx in zip(g_refs, groups):
            parts = [grads[i] for i in idx]
            r[...] = (parts[0] if len(parts) == 1 else jnp.concatenate(parts, axis=1)).astype(r.dtype)
        c_refs = refs[nr + nc + nct + ng:]

        @pl.when(pl.program_id(0) == 0)
        def _():
            for r in c_refs:
                r[...] = jnp.zeros_like(r)

        for r, v in zip(c_refs, grads[nr:]):
            r[...] += v

    gw = [sum(width(rows[i]) for i in idx) for idx in groups]
    gdtypes = gdtypes or [F32] * ng
    out_shape = [jax.ShapeDtypeStruct((t_rows, w), dt) for w, dt in zip(gw, gdtypes)]
    out_shape += [jax.ShapeDtypeStruct(c.shape, F32) for c in consts]
    out_specs = [pl.BlockSpec((tile, w), lambda i: (i, 0)) for w in gw]
    out_specs += [_const_spec(c) for c in consts]
    return _call(body, (*arrs, *consts, *ct_arrs), name=name, grid=(t_rows // tile,),
                 in_specs=list(specs) + [_const_spec(c) for c in consts] + list(ct_specs),
                 out_specs=out_specs, out_shape=out_shape, sem=("arbitrary",), job=job)


def _layer_norm(x, g, b):
    mu = jnp.mean(x, axis=-1, keepdims=True)
    xc = x - mu
    var = jnp.mean(xc * xc, axis=-1, keepdims=True)
    return xc * lax.rsqrt(var + LN_EPS) * g + b


def _sigmoid(x):
    return 1.0 / (1.0 + jnp.exp(-x))


def _fn_ln(x, g, b):
    return (_layer_norm(x, g, b),)


def _fn_rms(x, g):
    return (x * lax.rsqrt(jnp.mean(x * x, axis=-1, keepdims=True) + RMS_EPS) * g,)


def _make_post_mix(alpha):
    def fn(h, mix, g, b):
        return (_layer_norm(alpha * h + mix, g, b),)
    return fn


def _make_ple_ln(alpha):
    def fn(h1, ffn, pg, pp, g, b):
        return (_layer_norm(alpha * h1 + ffn + _sigmoid(pg) * pp, g, b),)
    return fn


def _fn_lower_bounds(l0, l1):
    m = jnp.maximum(l0, l1)
    e0, e1 = jnp.exp(l0 - m), jnp.exp(l1 - m)
    s = e0 + e1
    p0, p1 = e0 / s, e1 / s
    return (p0 - p0, (p0 + p1) - p0)


def _split_dot(x, e_bf16):
    hi = x.astype(BF16)
    lo = (x - hi.astype(F32)).astype(BF16)
    return (jnp.dot(hi, e_bf16, preferred_element_type=F32) + jnp.dot(lo, e_bf16, preferred_element_type=F32))


def _hgrn_common(th):
    rm = lax.broadcasted_iota(jnp.int32, (th, HG_W), 0) % HG_CHUNK

    def seg_cumsum(x):
        for s in (1, 2, 4, 8):
            x = x + jnp.where(rm >= s, pltpu.roll(x, s, 0), 0.0)
        return x

    def seg_rcumsum(x):
        for s in (1, 2, 4, 8):
            x = x + jnp.where(rm < HG_CHUNK - s, pltpu.roll(x, th - s, 0), 0.0)
        return x

    ri = lax.broadcasted_iota(jnp.int32, (HG_W, HG_W), 0) // HEAD
    ci = lax.broadcasted_iota(jnp.int32, (HG_W, HG_W), 1) // HEAD
    head_f32 = (ri == ci).astype(F32)
    head_bf16 = head_f32.astype(BF16)

    def headsum(x, pieces=2):
        if pieces == 1:
            return jnp.dot(x.astype(BF16), head_bf16, preferred_element_type=F32)
        return _split_dot(x, head_bf16)

    return rm, seg_cumsum, seg_rcumsum, head_f32, headsum


def _hgrn_gates(qr, fl, lb):
    sg = _sigmoid(fl)
    f = lb + (1.0 - lb) * sg
    sq = _sigmoid(qr)
    return sg, f, jnp.log(f), 1.0 - f, qr * sq, sq


def _shifted(x, d, th):
    return x if d == 0 else pltpu.roll(x, d, 0)


def _unshift(x, d, th):
    return x if d == 0 else pltpu.roll(x, th - d, 0)


def _hgrn_fwd(projp, lb, ng, *, name, job=None):
    t_rows = projp.shape[0]
    th = min(HG_TILE, t_rows)
    nct = th // HG_CHUNK

    def body(q_ref, f_ref, i_ref, g_ref, lb_ref, ng_ref, oa_ref, opre_ref, st_out_ref,
             st_ref, vtm_ref, kv_ref, qe_ref, dec_ref, oint_ref):
        rm, seg_cumsum, seg_rcumsum, head_f32, headsum = _hgrn_common(th)

        @pl.when(pl.program_id(0) == 0)
        def _():
            st_ref[...] = jnp.zeros_like(st_ref)

        qr, fl, v, g = q_ref[...], f_ref[...], i_ref[...], g_ref[...]
        _, f, lf, k, q, _ = _hgrn_gates(qr, fl, lb_ref[...])
        b = seg_cumsum(lf)

        o = jnp.zeros((th, HG_W), F32)
        for d in range(HG_CHUNK):
            kd, bd, vd = _shifted(k, d, th), _shifted(b, d, th), _shifted(v, d, th)
            e = jnp.exp(jnp.where(rm >= d, b - bd, -1e30))
            o = o + headsum(q * kd * e, 1) * vd

        blast = seg_rcumsum(jnp.where(rm == HG_CHUNK - 1, b, 0.0))
        kte = (k * jnp.exp(blast - b)).astype(BF16)
        qe_ref[...] = q * jnp.exp(b)
        dec_ref[...] = jnp.exp(blast)
        vt = v.T
        lane_chunk = lax.broadcasted_iota(jnp.int32, (HG_W, th), 1) // HG_CHUNK
        for c in range(nct):
            vtm_ref[c * HG_W:(c + 1) * HG_W, :] = jnp.where(lane_chunk == c, vt, 0.0).astype(BF16)
        kv_ref[...] = jnp.dot(vtm_ref[...], kte, preferred_element_type=F32)

        s = st_ref[...]
        for c in range(nct):
            rows = slice(c * HG_CHUNK, (c + 1) * HG_CHUNK)
            st_out_ref[c] = s
            oint_ref[rows, :] = lax.dot_general(qe_ref[rows, :].astype(BF16), s.astype(BF16),
                                                (((1,), (1,)), ((), ())), preferred_element_type=F32)
            dec = jnp.max(dec_ref[rows, :], axis=0, keepdims=True)
            s = s * dec + kv_ref[c * HG_W:(c + 1) * HG_W, :] * head_f32
        st_ref[...] = s

        o = o + oint_ref[...]
        opre_ref[...] = o
        r = lax.rsqrt(headsum(o * o) * (1.0 / HEAD) + RMS_EPS)
        oa_ref[...] = (o * r * ng_ref[...] * (g * _sigmoid(g))).astype(oa_ref.dtype)

    col = lambda j: pl.BlockSpec((th, HG_W), lambda i, j=j: (i, j))
    vec = pl.BlockSpec((1, HG_W), lambda i: (0, 0))
    row = pl.BlockSpec((th, HG_W), lambda i: (i, 0))
    n_chunks = t_rows // HG_CHUNK
    return _call(
        body, (projp, projp, projp, projp, lb, ng), name=name, grid=(t_rows // th,),
        in_specs=[col(0), col(1), col(2), col(3), vec, vec],
        out_specs=[row, row, pl.BlockSpec((nct, HG_W, HG_W), lambda i: (i, 0, 0))],
        out_shape=[jax.ShapeDtypeStruct((t_rows, HG_W), BF16), jax.ShapeDtypeStruct((t_rows, HG_W), F32),
                   jax.ShapeDtypeStruct((n_chunks, HG_W, HG_W), F32)],
        scratch_shapes=[pltpu.VMEM((HG_W, HG_W), F32), pltpu.VMEM((nct * HG_W, th), BF16),
                        pltpu.VMEM((nct * HG_W, HG_W), F32), pltpu.VMEM((th, HG_W), F32),
                        pltpu.VMEM((th, HG_W), F32), pltpu.VMEM((th, HG_W), F32)],
        sem=("arbitrary",), job=job)


def _hgrn_bwd(projp, lb, ng, opre, states, dcat, *, name):
    t_rows = projp.shape[0]
    th = min(HG_TILE, t_rows)
    nct = th // HG_CHUNK
    nt = t_rows // th

    def body(q_ref, f_ref, i_ref, g_ref, lb_ref, ng_ref, opre_ref, st_in_ref, do_ref,
             dproj_ref, dng_ref, dlb_ref,
             gst_ref, dotm_ref, qg_ref, v_ref, kte_ref, dop_ref, dec_ref, dkte_ref, dvi_ref, dqe_ref, ddec_ref):
        rm, seg_cumsum, seg_rcumsum, head_f32, headsum = _hgrn_common(th)

        @pl.when(pl.program_id(0) == 0)
        def _():
            gst_ref[...] = jnp.zeros_like(gst_ref)
            dng_ref[...] = jnp.zeros_like(dng_ref)
            dlb_ref[...] = jnp.zeros_like(dlb_ref)

        qr, fl, v, g = q_ref[...], f_ref[...], i_ref[...], g_ref[...]
        lb, ngv = lb_ref[...], ng_ref[...]
        sg, f, lf, k, q, sq = _hgrn_gates(qr, fl, lb)
        b = seg_cumsum(lf)
        blast = seg_rcumsum(jnp.where(rm == HG_CHUNK - 1, b, 0.0))
        eb = jnp.exp(b)
        ekb = jnp.exp(blast - b)
        qe, kte, dec = q * eb, k * ekb, jnp.exp(blast)

        do_out, op = do_ref[...], opre_ref[...]
        sgg = _sigmoid(g)
        sil = g * sgg
        r = lax.rsqrt(headsum(op * op) * (1.0 / HEAD) + RMS_EPS)
        on = op * r
        dng_ref[...] += jnp.sum(do_out * on * sil, axis=0, keepdims=True)
        dg = do_out * on * ngv * (sgg * (1.0 + g * (1.0 - sgg)))
        don = do_out * ngv * sil
        dop = r * (don - on * (headsum(don * on) * (1.0 / HEAD)))

        v_ref[...] = v
        kte_ref[...] = kte
        dop_ref[...] = dop
        dec_ref[...] = dec
        dot_t = dop.T
        lane_chunk = lax.broadcasted_iota(jnp.int32, (HG_W, th), 1) // HG_CHUNK
        for c in range(nct):
            dotm_ref[c * HG_W:(c + 1) * HG_W, :] = jnp.where(lane_chunk == c, dot_t, 0.0).astype(BF16)
        qg_ref[...] = jnp.dot(dotm_ref[...], qe.astype(BF16), preferred_element_type=F32)

        gs = gst_ref[...]
        for c in reversed(range(nct)):
            rows = slice(c * HG_CHUNK, (c + 1) * HG_CHUNK)
            s = st_in_ref[c]
            gm = (gs * head_f32).astype(BF16)
            dkte_ref[rows, :] = jnp.dot(v_ref[rows, :].astype(BF16), gm, preferred_element_type=F32)
            dvi_ref[rows, :] = lax.dot_general(kte_ref[rows, :].astype(BF16), gm, (((1,), (1,)), ((), ())),
                                               preferred_element_type=F32)
            dqe_ref[rows, :] = jnp.dot(dop_ref[rows, :].astype(BF16), s.astype(BF16), preferred_element_type=F32)
            ddec_ref[rows, :] = jnp.broadcast_to(jnp.sum(gs * s, axis=0, keepdims=True), (HG_CHUNK, HG_W))
            dec_c = jnp.max(dec_ref[rows, :], axis=0, keepdims=True)
            gs = gs * dec_c + qg_ref[c * HG_W:(c + 1) * HG_W, :] * head_f32
        gst_ref[...] = gs

        dkte, dqe = dkte_ref[...], dqe_ref[...]
        dq = dqe * eb
        dk = dkte * ekb
        db = dqe * qe - dkte * kte
        dv = dvi_ref[...]
        dblast = dkte * kte + jnp.where(rm == HG_CHUNK - 1, ddec_ref[...] * dec, 0.0)

        for d in range(HG_CHUNK):
            kd, bd, vd = _shifted(k, d, th), _shifted(b, d, th), _shifted(v, d, th)
            e = jnp.exp(jnp.where(rm >= d, b - bd, -1e30))
            p = q * kd * e
            sc = headsum(p, 1)
            dsc = headsum(dop * vd, 1)
            dv = dv + _unshift(sc * dop, d, th)
            dq = dq + dsc * kd * e
            dk = dk + _unshift(dsc * q * e, d, th)
            darg = dsc * p
            db = db + darg - _unshift(darg, d, th)

        db = db + jnp.where(rm == HG_CHUNK - 1, seg_cumsum(dblast), 0.0)
        dlf = seg_rcumsum(db)
        df = dlf / f - dk
        dlb_ref[...] += jnp.sum(df * (1.0 - sg), axis=0, keepdims=True)
        dfl = df * (1.0 - lb) * sg * (1.0 - sg)
        dqr = dq * (sq * (1.0 + qr * (1.0 - sq)))
        dproj_ref[...] = jnp.concatenate([dqr, dfl, dv, dg], axis=1).astype(dproj_ref.dtype)

    rev = lambda i: nt - 1 - i
    col = lambda j: pl.BlockSpec((th, HG_W), lambda i, j=j: (rev(i), j))
    vec = pl.BlockSpec((1, HG_W), lambda i: (0, 0))
    row = pl.BlockSpec((th, HG_W), lambda i: (rev(i), 0))
    tile_f32 = pltpu.VMEM((th, HG_W), F32)
    return pl.pallas_call(
        body, name=name, grid=(nt,),
        in_specs=[col(0), col(1), col(2), col(3), vec, vec, row,
                  pl.BlockSpec((nct, HG_W, HG_W), lambda i: (rev(i), 0, 0)), col(0)],
        out_specs=[pl.BlockSpec((th, 4 * HG_W), lambda i: (rev(i), 0)), vec, vec],
        out_shape=[jax.ShapeDtypeStruct((t_rows, 4 * HG_W), BF16), jax.ShapeDtypeStruct((1, HG_W), F32),
                   jax.ShapeDtypeStruct((1, HG_W), F32)],
        scratch_shapes=[pltpu.VMEM((HG_W, HG_W), F32), pltpu.VMEM((nct * HG_W, th), BF16),
                        pltpu.VMEM((nct * HG_W, HG_W), F32)] + [tile_f32] * 8,
        compiler_params=_cparams(("arbitrary",)),
    )(projp, projp, projp, projp, lb, ng, opre, states, dcat)


_INV_SQRT2 = 1.0 / math.sqrt(2.0)
_INV_SQRT2PI = 1.0 / math.sqrt(2.0 * math.pi)


def _gelu(x):
    return 0.5 * x * (1.0 + lax.erf(x * _INV_SQRT2))


def _gelu_grad(x):
    return 0.5 * (1.0 + lax.erf(x * _INV_SQRT2)) + x * jnp.exp(-0.5 * x * x) * _INV_SQRT2PI


def _sgu_parts(bu, bv, lg, lbias, w_ref, n_groups):
    c = SGU_CHUNK
    tril = (lax.broadcasted_iota(jnp.int32, (c, c), 0) >= lax.broadcasted_iota(jnp.int32, (c, c), 1)).astype(F32)
    gid = lax.broadcasted_iota(jnp.int32, bu.shape, 1) // HEAD
    u = _gelu(bu)
    gv = _gelu(bv)
    mu = jnp.mean(gv, axis=-1, keepdims=True)
    xc = gv - mu
    rstd = lax.rsqrt(jnp.mean(xc * xc, axis=-1, keepdims=True) + LN_EPS)
    xhat = xc * rstd
    vn = xhat * lg + lbias
    ws = [w_ref[gi] * tril for gi in range(n_groups)]
    return tril, gid, u, rstd, xhat, vn, ws


def _sgu_fwd(projp, lg, lbias, w_s, bias_full, *, name, job=None):
    t_rows = projp.shape[0]
    n_groups = w_s.shape[0]
    c = SGU_CHUNK
    rows_per_step = min(SGU_STEP_CHUNKS * c, t_rows)

    def body(u_ref, v_ref, lg_ref, lb_ref, w_ref, bias_ref, o_ref):
        for r0 in range(0, rows_per_step, c):
            rows = slice(r0, r0 + c)
            _, gid, u, _, _, vn, ws = _sgu_parts(u_ref[rows, :], v_ref[rows, :], lg_ref[...], lb_ref[...], w_ref,
                                                 n_groups)
            vnb = vn.astype(BF16)
            z = bias_ref[...]
            for gi in range(n_groups):
                z = z + jnp.where(gid == gi, jnp.dot(ws[gi].astype(BF16), vnb, preferred_element_type=F32), 0.0)
            o_ref[rows, :] = (u * z).astype(o_ref.dtype)

    col = lambda j: pl.BlockSpec((rows_per_step, HG_W), lambda i, j=j: (i, j))
    return _call(
        body, (projp, projp, lg, lbias, w_s, bias_full), name=name, grid=(t_rows // rows_per_step,),
        in_specs=[col(4), col(5), _const_spec(lg), _const_spec(lbias), _const_spec(w_s), _const_spec(bias_full)],
        out_specs=pl.BlockSpec((rows_per_step, HG_W), lambda i: (i, 0)),
        out_shape=jax.ShapeDtypeStruct((t_rows, HG_W), BF16), sem=("arbitrary",), job=job)


def _sgu_bwd(projp, lg, lbias, w_s, bias_full, dcat, *, name):
    t_rows = projp.shape[0]
    n_groups = w_s.shape[0]
    c = SGU_CHUNK
    rows_per_step = min(SGU_STEP_CHUNKS * c, t_rows)
    n = t_rows // rows_per_step

    def body(u_ref, v_ref, lg_ref, lb_ref, w_ref, bias_ref, do_ref,
             dproj_ref, dlg_ref, dlb_ref, dw_ref, dbs_ref, dbias_acc):
        i = pl.program_id(0)

        @pl.when(i == 0)
        def _():
            dlg_ref[...] = jnp.zeros_like(dlg_ref)
            dlb_ref[...] = jnp.zeros_like(dlb_ref)
            dw_ref[...] = jnp.zeros_like(dw_ref)
            dbias_acc[...] = jnp.zeros_like(dbias_acc)

        lg_v = lg_ref[...]
        for r0 in range(0, rows_per_step, c):
            rows = slice(r0, r0 + c)
            bu, bv = u_ref[rows, :], v_ref[rows, :]
            tril, gid, u, rstd, xhat, vn, ws = _sgu_parts(bu, bv, lg_v, lb_ref[...], w_ref, n_groups)
            vnb = vn.astype(BF16)
            z = bias_ref[...]
            for gi in range(n_groups):
                z = z + jnp.where(gid == gi, jnp.dot(ws[gi].astype(BF16), vnb, preferred_element_type=F32), 0.0)
            do = do_ref[rows, :]
            dbu = do * z * _gelu_grad(bu)
            dz = do * u
            dbias_acc[...] += dz
            dvn = jnp.zeros_like(dz)
            for gi in range(n_groups):
                dzg = jnp.where(gid == gi, dz, 0.0).astype(BF16)
                dw_ref[gi] += lax.dot_general(dzg, vnb, (((1,), (1,)), ((), ())), preferred_element_type=F32) * tril
                dvn = dvn + jnp.dot(ws[gi].T.astype(BF16), dzg, preferred_element_type=F32)
            dlg_ref[...] += jnp.sum(dvn * xhat, axis=0, keepdims=True)
            dlb_ref[...] += jnp.sum(dvn, axis=0, keepdims=True)
            dxh = dvn * lg_v
            dgv = rstd * (dxh - jnp.mean(dxh, axis=-1, keepdims=True)
                          - xhat * jnp.mean(dxh * xhat, axis=-1, keepdims=True))
            dproj_ref[rows, :] = jnp.concatenate([dbu, dgv * _gelu_grad(bv)], axis=1).astype(dproj_ref.dtype)

        @pl.when(i == n - 1)
        def _():
            dbs_ref[...] = jnp.sum(dbias_acc[...].T.reshape(n_groups, HEAD, c), axis=1)

    col = lambda j: pl.BlockSpec((rows_per_step, HG_W), lambda i, j=j: (i, j))
    return pl.pallas_call(
        body, name=name, grid=(n,),
        in_specs=[col(4), col(5), _const_spec(lg), _const_spec(lbias), _const_spec(w_s), _const_spec(bias_full),
                  col(1)],
        out_specs=[pl.BlockSpec((rows_per_step, 2 * HG_W), lambda i: (i, 0)), _const_spec(lg), _const_spec(lbias),
                   _const_spec(w_s), pl.BlockSpec((n_groups, c), lambda i: (0, 0))],
        out_shape=[jax.ShapeDtypeStruct((t_rows, 2 * HG_W), BF16), jax.ShapeDtypeStruct(lg.shape, F32),
                   jax.ShapeDtypeStruct(lbias.shape, F32), jax.ShapeDtypeStruct(w_s.shape, F32),
                   jax.ShapeDtypeStruct((n_groups, c), F32)],
        scratch_shapes=[pltpu.VMEM((c, HG_W), F32)],
        compiler_params=_cparams(("arbitrary",)),
    )(projp, projp, lg, lbias, w_s, bias_full, dcat)


def _rope_tables(positions):
    t = positions.shape[0]
    inv_freq = ROPE_THETA ** (-jnp.arange(0, 32, 2, dtype=F32) / 32)
    ang = positions.astype(F32)[:, None] * inv_freq
    cos, sin = jnp.cos(ang), jnp.sin(ang)
    z = lambda w: jnp.zeros((t, w), F32)
    cos_t = jnp.concatenate([jnp.ones((t, 64), F32), cos, cos, z(32)], axis=1)
    sin_up = jnp.concatenate([z(80), sin, z(32)], axis=1)
    sin_dn = jnp.concatenate([z(64), -sin, z(48)], axis=1)
    return cos_t, sin_up, sin_dn


def _rep(x, n):
    return x if n == 1 else jnp.concatenate([x] * n, axis=1)


def _rope(x, cos_t, sin_up, sin_dn):
    w = x.shape[1]
    return x * cos_t + pltpu.roll(x, 16, 1) * sin_up + pltpu.roll(x, w - 16, 1) * sin_dn


def _rope_t(dy, cos_t, sin_up, sin_dn):
    w = dy.shape[1]
    return dy * cos_t + pltpu.roll(dy * sin_up, w - 16, 1) + pltpu.roll(dy * sin_dn, 16, 1)


def _mla_prep(q, kv, projp, tables, *, name):
    nh = N_ATT_HEADS

    def fn(qv, kvv, kr, cos_t, sin_up, sin_dn):
        qr = _rope(qv, _rep(cos_t, nh), _rep(sin_up, nh), _rep(sin_dn, nh))
        krr = _rope(kr, cos_t, sin_up, sin_dn)
        lane = lax.broadcasted_iota(jnp.int32, kvv.shape, 1) % LANES
        return qr, jnp.where(lane < HEAD, kvv, 0.0) + _rep(krr, nh), kvv

    w = q.shape[1]
    return _rowwise(fn, [q, kv, (projp, LANES, P_KR // LANES)] + list(tables), [],
                    [(w, BF16), (w, BF16), (w, BF16)], tile=1024, name=name)


def _mla_prep_bwd(dqr, dkf, tables, *, name):
    nh = N_ATT_HEADS

    def fn(dq, dk, cos_t, sin_up, sin_dn):
        dqp = _rope_t(dq, _rep(cos_t, nh), _rep(sin_up, nh), _rep(sin_dn, nh))
        dkrr = dk[:, 0:LANES]
        for h in range(1, nh):
            dkrr = dkrr + dk[:, LANES * h:LANES * (h + 1)]
        return dqp, _rope_t(dkrr, cos_t, sin_up, sin_dn)

    return _rowwise(fn, [dqr, dkf] + list(tables), [], [(dqr.shape[1], BF16), (LANES, BF16)], tile=1024, name=name)


_LOG2E = 1.0 / math.log(2.0)
_NT = (((1,), (1,)), ((), ()))
_TN = (((0,), (0,)), ((), ()))


def _attn_fwd(qr, kf, kvb, *, name, job=None):
    t_rows = qr.shape[0]
    tq = min(ATT_TQ, t_rows)
    nb = t_rows // tq
    scale = ATT_D ** -0.5

    c2 = scale * _LOG2E

    def body(q_ref, kf_ref, kvb_ref, o_ref, lse_ref):
        qi = pl.program_id(1)
        lane = lax.broadcasted_iota(jnp.int32, (tq, LANES), 1)
        causal_t = (lax.broadcasted_iota(jnp.int32, (tq, tq), 0) <= lax.broadcasted_iota(jnp.int32, (tq, tq), 1))
        heads = [slice(hh * LANES, (hh + 1) * LANES) for hh in range(2)]
        qs = [q_ref[:, cols] for cols in heads]

        def block(first, n_keys, carry, diagonal):
            rows = pl.ds(pl.multiple_of(first * tq, tq), n_keys)
            new = []
            for q, cols, (m_old, l_old, acc_t) in zip(qs, heads, carry):
                s_t = lax.dot_general(kf_ref[rows, cols], q, _NT, preferred_element_type=F32)
                if diagonal:
                    s_t = jnp.where(causal_t, s_t, -1e30)
                m_new = jnp.maximum(m_old, jnp.max(s_t, axis=0, keepdims=True))
                p_t = jnp.exp2((s_t - m_new) * c2)
                a = jnp.exp2((m_old - m_new) * c2)
                pv_t = lax.dot_general(kvb_ref[rows, cols], p_t.astype(BF16), _TN, preferred_element_type=F32)
                new.append((m_new, a * l_old + jnp.sum(p_t, axis=0, keepdims=True), a * acc_t + pv_t))
            return tuple(new)

        init = (jnp.full((1, tq), -1e30, F32), jnp.zeros((1, tq), F32), jnp.zeros((LANES, tq), F32))
        carry = lax.fori_loop(0, qi // 4, lambda g, c: block(4 * g, 4 * tq, c, False), (init, init))
        carry = lax.cond((qi // 2) % 2 == 1, lambda c: block(4 * (qi // 4), 2 * tq, c, False), lambda c: c, carry)
        carry = lax.cond(qi % 2 == 1, lambda c: block(qi - 1, tq, c, False), lambda c: c, carry)
        outs = []
        for hh, (m_fin, l_fin, acc_t) in enumerate(block(qi, tq, carry, True)):
            lse_ref[hh] = m_fin * scale + jnp.log(l_fin)
            outs.append((acc_t / l_fin).T)
        o_ref[...] = jnp.where(lane < HEAD, pltpu.roll(outs[0], HEAD, 1), outs[1])

    pair = pl.BlockSpec((t_rows, 2 * LANES), lambda pr, qi: (0, pr))
    return _call(
        body, (qr, kf, kvb), name=name, grid=(N_ATT_HEADS // 2, nb),
        in_specs=[pl.BlockSpec((tq, 2 * LANES), lambda pr, qi: (qi, pr)), pair, pair],
        out_specs=[pl.BlockSpec((tq, LANES), lambda pr, qi: (qi, pr)),
                   pl.BlockSpec((2, 1, tq), lambda pr, qi: (pr, 0, qi))],
        out_shape=[jax.ShapeDtypeStruct((t_rows, N_ATT_HEADS * HEAD), F32),
                   jax.ShapeDtypeStruct((N_ATT_HEADS, 1, t_rows), F32)],
        sem=("parallel", "arbitrary"), job=job)


def _attn_bwd(qr, kf, kvb, dcat, o, lse, *, name, job=None):
    t_rows = qr.shape[0]
    tq = min(ATT_TQ, t_rows)
    nb = t_rows // tq
    scale = ATT_D ** -0.5
    c2 = scale * _LOG2E
    do_off = 2 * HG_W // LANES

    def body(q_ref, kf_ref, kvb_ref, do_ref, o_ref, lse_ref, dq_ref, dkv_ref, dk_ref):
        ki = pl.program_id(1)

        @pl.when(ki == 0)
        def _():
            dq_ref[...] = jnp.zeros_like(dq_ref)

        lane = lax.broadcasted_iota(jnp.int32, (tq, LANES), 1)
        causal_t = (lax.broadcasted_iota(jnp.int32, (tq, tq), 0) <= lax.broadcasted_iota(jnp.int32, (tq, tq), 1))
        heads = [slice(hh * LANES, (hh + 1) * LANES) for hh in range(2)]
        ks = [kf_ref[:, cols] for cols in heads]
        vs = [kvb_ref[:, cols] for cols in heads]

        def block(qi, n_q, carry, diagonal):
            rows = pl.ds(pl.multiple_of(qi * tq, tq), n_q)
            do_pair, o_pair = do_ref[rows, :], o_ref[rows, :]
            upper = lax.broadcasted_iota(jnp.int32, do_pair.shape, 1) >= HEAD
            new = []
            for hh, (cols, k, v, (dk, dv)) in enumerate(zip(heads, ks, vs, carry)):
                q = q_ref[rows, cols]
                do, ov = (pltpu.roll(do_pair, HEAD, 1), pltpu.roll(o_pair, HEAD, 1)) if hh == 0 else (do_pair, o_pair)
                do = jnp.where(upper, do, 0.0)
                delta = jnp.sum((do * ov).T, axis=0, keepdims=True)
                s_t = lax.dot_general(k, q, _NT, preferred_element_type=F32)
                if diagonal:
                    s_t = jnp.where(causal_t, s_t, -1e30)
                p_t = jnp.exp2(s_t * c2 - lse_ref[hh, :, rows] * _LOG2E)
                dob = do.astype(BF16)
                dv = dv + jnp.dot(p_t.astype(BF16), dob, preferred_element_type=F32)
                dp_t = lax.dot_general(v, dob, _NT, preferred_element_type=F32)
                ds_t = (p_t * (dp_t - delta) * scale).astype(BF16)
                dk = dk + jnp.dot(ds_t, q, preferred_element_type=F32)
                dq_ref[rows, cols] += lax.dot_general(ds_t, k, _TN, preferred_element_type=F32)
                new.append((dk, dv))
            return tuple(new)

        zero = jnp.zeros((tq, LANES), F32)
        carry = block(ki, tq, ((zero, zero), (zero, zero)), True)
        rest = nb - 1 - ki
        carry = lax.fori_loop(0, rest // 2, lambda g, c: block(ki + 1 + 2 * g, 2 * tq, c, False), carry)
        carry = lax.cond(rest % 2 == 1, lambda c: block(nb - 1, tq, c, False), lambda c: c, carry)
        dkv_ref[...] = jnp.concatenate([jnp.where(lane < HEAD, dk, dv) for dk, dv in carry],
                                       axis=1).astype(dkv_ref.dtype)
        dk_ref[...] = jnp.concatenate([dk for dk, _ in carry], axis=1)

    pair_all = pl.BlockSpec((t_rows, 2 * LANES), lambda pr, ki: (0, pr))
    pair_blk = pl.BlockSpec((tq, 2 * LANES), lambda pr, ki: (ki, pr))
    wide = jax.ShapeDtypeStruct((t_rows, N_ATT_HEADS * LANES), F32)
    return _call(
        body, (qr, kf, kvb, dcat, o, lse), name=name, grid=(N_ATT_HEADS // 2, nb),
        in_specs=[pair_all, pair_blk, pair_blk,
                  pl.BlockSpec((t_rows, LANES), lambda pr, ki: (0, do_off + pr)),
                  pl.BlockSpec((t_rows, LANES), lambda pr, ki: (0, pr)),
                  pl.BlockSpec((2, 1, t_rows), lambda pr, ki: (pr, 0, 0))],
        out_specs=[pair_all, pair_blk, pair_blk],
        out_shape=[wide, jax.ShapeDtypeStruct(wide.shape, BF16), wide],
        sem=("parallel", "arbitrary"), job=job)


def _my_pos():
    return lax.axis_index("x"), lax.axis_index("y"), lax.axis_index("c")


def _all_gather(xs, *, name, columns=True):
    return _gather_forward(_run_job(_gather_job(xs, columns), name=name), name=name + "_forward")


def _remote(src, dst, send_sems, recv_sems, k, dev):
    return pltpu.make_async_remote_copy(src_ref=src, dst_ref=dst, send_sem=send_sems.at[k], recv_sem=recv_sems.at[k],
                                        device_id=dev, device_id_type=MESH)


def _block(ref, idx):
    if len(ref.shape) == 2:
        return ref.at[:, pl.ds(pl.multiple_of(idx * LANES, LANES), LANES)]
    return ref.at[idx]


def _gather_job(xs, columns=True):
    n = len(xs)

    def make(x_refs, out_refs, send_sems, recv_sems, local_sems):
        mx, my, mc = _my_pos()
        mine = 4 * mx + 2 * my + mc
        peers = [(mx, my, 1 - mc), (1 - mx, my, mc), (mx, 1 - my, mc), (1 - mx, 1 - my, mc)]
        sends, recvs, local = [], [], []
        for a in range(n):
            local.append(pltpu.make_async_copy(x_refs[a], _block(out_refs[a], mine), local_sems.at[a]))
            for k, dev in enumerate(peers):
                theirs = 4 * dev[0] + 2 * dev[1] + dev[2]
                sends.append(_remote(x_refs[a], _block(out_refs[a], mine), send_sems, recv_sems, 4 * a + k, dev))
                recvs.append(_remote(x_refs[a], _block(out_refs[a], theirs), send_sems, recv_sems, 4 * a + k, dev))
        return sends, recvs, local

    def gathered(x):
        if columns and x.ndim == 2 and x.shape[1] == LANES:
            return jax.ShapeDtypeStruct((x.shape[0], N_DEV * LANES), x.dtype)
        return jax.ShapeDtypeStruct((N_DEV,) + x.shape, x.dtype)

    return _copies_job(xs, [gathered(x) for x in xs], 4 * n, n, make)


def _forward_job(gs):
    n = len(gs)

    def make(in_refs, out_refs, send_sems, recv_sems, local_sems):
        mx, my, mc = _my_pos()
        chips = [(1 - mx, my), (mx, 1 - my), (1 - mx, 1 - my)]
        sends, recvs = [], []
        for a in range(n):
            for j, (cx, cy) in enumerate(chips):
                here = _block(out_refs[a], 4 * cx + 2 * cy + mc)
                there = _block(out_refs[a], 4 * cx + 2 * cy + 1 - mc)
                sends.append(_remote(here, here, send_sems, recv_sems, 3 * a + j, (mx, my, 1 - mc)))
                recvs.append(_remote(here, there, send_sems, recv_sems, 3 * a + j, (mx, my, 1 - mc)))
        return sends, recvs, []

    shapes = [jax.ShapeDtypeStruct(g.shape, g.dtype) for g in gs]
    return _copies_job(gs, shapes, 3 * n, 0, make, in_place=True)


def _gather_forward(gs, *, name):
    return _run_job(_forward_job(gs), name=name)


def _pair_job(xs):
    n = len(xs)

    def make(x_refs, out_refs, send_sems, recv_sems, local_sems):
        mx, my, mc = _my_pos()

        def src(ref, g):
            return _block(ref, 2 * g + 1 - mc) if len(ref.shape) == 2 else ref.at[g, 1 - mc]

        copies = [_remote(src(x_refs[a], g), out_refs[a].at[g], send_sems, recv_sems, 4 * a + g, (mx, my, 1 - mc))
                  for a in range(n) for g in range(4)]
        return copies, copies, []

    shapes = [jax.ShapeDtypeStruct((4, x.shape[0], LANES) if x.ndim == 2 else (4,) + x.shape[2:], x.dtype)
              for x in xs]
    return _copies_job(xs, shapes, 4 * n, 0, make)


def _pair_add(x, r, core, *, name):
    _, a, b = r.shape
    ta = _row_tile(a, 512)

    def body(c_ref, x_ref, r_ref, o_ref):
        o_ref[...] = (x_ref[...] + r_ref[...]).astype(o_ref.dtype)

    blk = pl.BlockSpec((None, ta, b), lambda g, i, c_ref: (g, i, 0))
    own = (pl.BlockSpec((ta, b), lambda g, i, c_ref: (i, 2 * g + c_ref[0])) if x.ndim == 2
           else pl.BlockSpec((None, None, ta, b), lambda g, i, c_ref: (g, c_ref[0], i, 0)))
    return pl.pallas_call(
        body, name=name,
        grid_spec=pltpu.PrefetchScalarGridSpec(
            num_scalar_prefetch=1, grid=(4, a // ta), in_specs=[own, blk], out_specs=blk),
        out_shape=jax.ShapeDtypeStruct((4, a, b), BF16),
        compiler_params=_cparams(("parallel", "parallel")),
    )(core, x, r)


def _quad_job(xs):
    n = len(xs)

    def make(x_refs, out_refs, send_sems, recv_sems, local_sems):
        mx, my, mc = _my_pos()
        mine = 2 * mx + my
        peers = [((1 - mx, my, mc), 2 * (1 - mx) + my), ((mx, 1 - my, mc), 2 * mx + 1 - my),
                 ((1 - mx, 1 - my, mc), 2 * (1 - mx) + 1 - my)]
        sends, recvs, local = [], [], []
        for a in range(n):
            local.append(pltpu.make_async_copy(x_refs[a].at[mine], out_refs[a].at[mine], local_sems.at[a]))
            for k, (dev, g) in enumerate(peers):
                sends.append(_remote(x_refs[a].at[g], out_refs[a].at[mine], send_sems, recv_sems, 3 * a + k, dev))
                recvs.append(_remote(x_refs[a].at[g], out_refs[a].at[g], send_sems, recv_sems, 3 * a + k, dev))
        return sends, recvs, local

    shapes = [jax.ShapeDtypeStruct(x.shape, x.dtype) for x in xs]
    return _copies_job(xs, shapes, 3 * n, n, make)


def _row_tile(r, pref):
    t = min(pref, r)
    while r % t or (t % 8 and t != r):
        t -= 1
    return t


def _adamw(parts, w, m, v, layer, *, name, tile=512, into=None):
    g, a, b = parts.shape
    tile = _row_tile(a, tile)
    c1 = 1.0 / (1.0 - ADAM_B1 ** ADAM_STEP)
    c2 = 1.0 / (1.0 - ADAM_B2 ** ADAM_STEP)
    into = tuple(into or ())

    def body(p_ref, w_ref, m_ref, v_ref, *refs):
        g_ref, d_ref, mo_ref, vo_ref = refs[len(into):]
        grad = p_ref[0].astype(F32)
        for j in range(1, g):
            grad = grad + p_ref[j].astype(F32)
        mn = ADAM_B1 * m_ref[...] + (1.0 - ADAM_B1) * grad
        vn = ADAM_B2 * v_ref[...] + (1.0 - ADAM_B2) * (grad * grad)
        g_ref[...] = grad
        mo_ref[...] = mn
        vo_ref[...] = vn
        d_ref[...] = -ADAM_LR * ((mn * c1) / (jnp.sqrt(vn * c2) + ADAM_EPS) + ADAM_WD * w_ref[...])

    if layer is None:
        src, shape = pl.BlockSpec((tile, b), lambda i: (i, 0)), (a, b)
    else:
        src, shape = pl.BlockSpec((None, tile, b), lambda i: (layer, i, 0)), w.shape
    return pl.pallas_call(
        body, name=name, grid=(a // tile,),
        in_specs=[pl.BlockSpec((g, tile, b), lambda i: (0, i, 0)), src, src, src] + [_ANY] * len(into),
        out_specs=[src] * 4,
        out_shape=[jax.ShapeDtypeStruct(shape, F32)] * 4,
        input_output_aliases={4 + i: i for i in range(len(into))},
        compiler_params=_cparams(("parallel",)),
    )(parts, w, m, v, *into)


W_IN_SHARD = 276


def _w_in_dest(col):
    return jnp.where(col < P_KR, col, jnp.where(col < P_KR + 256, col + (P_CKV - P_KR), col - 2176 + P_KR + HEAD))


PLACE_TILE = 384
PLACE_SHARDS = 3
PICK_TILE = 128
PICK_TILES = 4


def _w_in_tables():
    col = np.arange(N_DEV * W_IN_SHARD)
    dest = np.where(col < P_KR, col, np.where(col < P_KR + 256, col + (P_CKV - P_KR), col - 2176 + P_KR + HEAD))
    shard = col // W_IN_SHARD

    def filled(used, universe, n):
        used = sorted(set(int(u) for u in used))
        assert len(used) <= n, used
        return used + [u for u in universe if u not in used][:n - len(used)]

    place = [filled(shard[dest // PLACE_TILE == c], range(N_DEV), PLACE_SHARDS) for c in range(P_COLS // PLACE_TILE)]
    pick = [filled(dest[shard == j] // PICK_TILE, range(P_COLS // PICK_TILE), PICK_TILES) for j in range(N_DEV)]
    return np.asarray(place, np.int32).reshape(-1), np.asarray(pick, np.int32).reshape(-1)


def _place_w_in(g, *, name):
    _, d, sh = g.shape
    tc, ns = PLACE_TILE, PLACE_SHARDS
    table = jnp.asarray(_w_in_tables()[0])

    def body(tab_ref, g_ref, o_ref, acc_ref):
        ct, s = pl.program_id(0), pl.program_id(1)
        j = tab_ref[ct * ns + s]

        @pl.when(s == 0)
        def _():
            acc_ref[...] = jnp.zeros_like(acc_ref)

        src = j * sh + lax.broadcasted_iota(jnp.int32, (sh, tc), 0)
        dst = ct * tc + lax.broadcasted_iota(jnp.int32, (sh, tc), 1)
        place = (_w_in_dest(src) == dst).astype(BF16)
        acc_ref[...] += jnp.dot(g_ref[...], place, preferred_element_type=F32)

        @pl.when(s == ns - 1)
        def _():
            o_ref[...] = acc_ref[...].astype(o_ref.dtype)

    return pl.pallas_call(
        body, name=name,
        grid_spec=pltpu.PrefetchScalarGridSpec(
            num_scalar_prefetch=1, grid=(P_COLS // tc, ns),
            in_specs=[pl.BlockSpec((None, d, sh), lambda ct, s, tab: (tab[ct * ns + s], 0, 0))],
            out_specs=pl.BlockSpec((d, tc), lambda ct, s, tab: (0, ct)),
            scratch_shapes=[pltpu.VMEM((d, tc), F32)]),
        out_shape=jax.ShapeDtypeStruct((d, P_COLS), BF16),
        compiler_params=_cparams(("parallel", "arbitrary")),
    )(table, g)


def _unplace_w_in(dw, *, name):
    d = dw.shape[0]
    sh, tk, nt = W_IN_SHARD, PICK_TILE, PICK_TILES
    table = jnp.asarray(_w_in_tables()[1])

    def body(tab_ref, dw_ref, o_ref):
        j, kk = pl.program_id(0), pl.program_id(1)
        tile = tab_ref[j * nt + kk]
        src = j * sh + lax.broadcasted_iota(jnp.int32, (tk, sh), 1)
        dst = tile * tk + lax.broadcasted_iota(jnp.int32, (tk, sh), 0)
        pick = (_w_in_dest(src) == dst).astype(BF16)
        part = _split_dot(dw_ref[...], pick)

        @pl.when(kk == 0)
        def _():
            o_ref[...] = part

        @pl.when(kk > 0)
        def _():
            o_ref[...] += part

    return pl.pallas_call(
        body, name=name,
        grid_spec=pltpu.PrefetchScalarGridSpec(
            num_scalar_prefetch=1, grid=(N_DEV, nt),
            in_specs=[pl.BlockSpec((d, tk), lambda j, kk, tab: (0, tab[j * nt + kk]))],
            out_specs=pl.BlockSpec((None, d, sh), lambda j, kk, tab: (j, 0, 0))),
        out_shape=jax.ShapeDtypeStruct((N_DEV, d, sh), F32),
        compiler_params=_cparams(("parallel", "arbitrary")),
    )(table, dw)


def _gate_up_swiglu(h1, wgu, *, name):
    t_rows, k = h1.shape
    w = wgu.shape[2]
    tm = _tile(t_rows, 1024)

    def body(a_ref, wg_ref, wu_ref, gu_ref, act_ref):
        a = a_ref[...].astype(BF16)
        gate = jnp.dot(a, wg_ref[...], preferred_element_type=F32)
        up = jnp.dot(a, wu_ref[...], preferred_element_type=F32)
        gu_ref[0] = gate.astype(gu_ref.dtype)
        gu_ref[1] = up.astype(gu_ref.dtype)
        act_ref[...] = (gate * _sigmoid(gate) * up).astype(act_ref.dtype)

    return pl.pallas_call(
        body, name=name, grid=(t_rows // tm, 4),
        in_specs=[pl.BlockSpec((tm, k), lambda i, j: (i, 0)),
                  pl.BlockSpec((None, k, w), lambda i, j: (j, 0, 0)),
                  pl.BlockSpec((None, k, w), lambda i, j: (j + 4, 0, 0))],
        out_specs=[pl.BlockSpec((2, None, tm, w), lambda i, j: (0, j, i, 0)),
                   pl.BlockSpec((None, tm, w), lambda i, j: (j, i, 0))],
        out_shape=[jax.ShapeDtypeStruct((2, 4, t_rows, w), BF16), jax.ShapeDtypeStruct((4, t_rows, w), BF16)],
        compiler_params=_cparams(("parallel", "arbitrary")),
    )(h1, wgu, wgu)


def _down_dx_swiglu(dffn, wdown, gu, *, name):
    t_rows, k = dffn.shape
    w = gu.shape[3]
    tm = _tile(t_rows, 1024)

    def body(d_ref, w_ref, gu_ref, o_ref):
        dact = lax.dot_general(d_ref[...].astype(BF16), w_ref[...], _NT, preferred_element_type=F32)
        gate, up = gu_ref[0].astype(F32), gu_ref[1].astype(F32)
        sg = _sigmoid(gate)
        silu = gate * sg
        o_ref[0] = (dact * up * (sg + silu - silu * sg)).astype(o_ref.dtype)
        o_ref[1] = (dact * silu).astype(o_ref.dtype)

    blk = pl.BlockSpec((2, None, tm, w), lambda i, j: (0, j, i, 0))
    return pl.pallas_call(
        body, name=name, grid=(t_rows // tm, 4),
        in_specs=[pl.BlockSpec((tm, k), lambda i, j: (i, 0)), pl.BlockSpec((w, k), lambda i, j: (j, 0)), blk],
        out_specs=blk, out_shape=jax.ShapeDtypeStruct(gu.shape, BF16),
        compiler_params=_cparams(("parallel", "arbitrary")),
    )(dffn, wdown, gu)


BIG = ("w_in", "mla_w_uq", "mla_w_ukv", "w_out", "w_gate_up", "w_down", "ple_w_gate", "ple_w_proj")
SMALL = ("ln_in_g", "ln_in_b", "hgrn_lb_logits", "hgrn_norm_g", "sgu_ln_g", "sgu_ln_b", "sgu_w_s", "sgu_b_s",
         "mla_q_norm_g", "mla_kv_norm_g", "ln1_g", "ln1_b", "ln2_g", "ln2_b")
ORDER = ("ln_in_g", "ln_in_b", "w_in", "hgrn_lb_logits", "hgrn_norm_g", "sgu_ln_g", "sgu_ln_b", "sgu_w_s", "sgu_b_s",
         "mla_q_norm_g", "mla_w_uq", "mla_kv_norm_g", "mla_w_ukv", "w_out", "ln1_g", "ln1_b", "w_gate_up", "w_down",
         "ple_w_gate", "ple_w_proj", "ln2_g", "ln2_b")


def _slab(a, align):
    s = a.reshape(-1, LANES)
    pad = -s.shape[0] % align
    return jnp.pad(s, ((0, pad), (0, 0))) if pad else s


def _pack(arrays, align=16, total_align=512):
    s = jnp.concatenate([_slab(a, align) for a in arrays], axis=0)
    pad = -s.shape[0] % total_align
    return jnp.pad(s, ((0, pad), (0, 0))) if pad else s


def _unpack(slab, shapes, align=16):
    out, r0 = [], 0
    for s in shapes:
        nr = math.prod(s) // LANES
        out.append(slab[r0:r0 + nr].reshape(s))
        r0 += nr + (-nr % align)
    return out


def _weight_shards(w, li):
    uq_pad = ((0, 0), (0, LANES - ATT_D))
    shards = {k: w[k][li] for k in BIG}
    shards["mla_w_uq"] = jnp.pad(shards["mla_w_uq"], uq_pad)
    return {k: s.astype(BF16) for k, s in shards.items()}


def _usable_weights(g, *, name):
    out = {}
    for k, a in g.items():
        if k == "w_in":
            out[k] = _place_w_in(a, name=name + "_place_w_in")
        elif k in ("w_out", "w_down", "ple_w_gate"):
            out[k] = a.reshape(a.shape[0] * a.shape[1], a.shape[2])
        else:
            out[k] = a
    return out


BY_COLUMNS = ("mla_w_uq", "mla_w_ukv", "ple_w_proj")


def _as_pairs(k, g):
    if k in BY_COLUMNS:
        return g
    if g.ndim == 2:
        return g.reshape((4, 2, g.shape[0] // N_DEV) + g.shape[1:])
    return g.reshape((4, 2) + g.shape[1:])


def _twice(fn):
    return lambda *a: fn(*a) * 2


def _layer_forward(li, h, hb, p_i, wts, sm, lbs, tables, alpha, hgrn_job=None, after_hgrn=None, attn_job=None,
                   after_attn=None, loss_target=None):
    n = f"l{li}_"
    row1 = lambda a: a.reshape(1, -1)
    projp = _mm(hb, wts["w_in"], name=n + "proj")
    ng = row1(sm["hgrn_norm_g"][li])
    res = _hgrn_fwd(projp, lbs[li], ng, name=n + "hgrn_fwd", job=hgrn_job)
    if hgrn_job is not None:
        res, got = res
    o_a, o_pre, states = res
    lg, lbias = row1(sm["sgu_ln_g"][li]), row1(sm["sgu_ln_b"][li])
    w_s = sm["sgu_w_s"][li]
    bias_full = jnp.repeat(sm["sgu_b_s"][li].T, HEAD, axis=1)
    o_b = _sgu_fwd(projp, lg, lbias, w_s, bias_full, name=n + "sgu_fwd",
                   job=None if hgrn_job is None else _forward_job(got))
    if hgrn_job is not None:
        o_b, got = o_b
        wts = dict(wts, **after_hgrn(got))
    qg, kvg = row1(sm["mla_q_norm_g"][li]), row1(sm["mla_kv_norm_g"][li])
    cq_view, ckv_view = (projp, 384, P_CQ // 384), (projp, 256, P_CKV // 256)
    (cqn,) = _rowwise(_fn_rms, [cq_view], [qg], [(384, BF16)], tile=2048, name=n + "q_norm")
    (ckvn,) = _rowwise(_fn_rms, [ckv_view], [kvg], [(256, BF16)], tile=2048, name=n + "kv_norm")
    q = _mm(cqn, wts["mla_w_uq"], name=n + "uq")
    kv = _mm(ckvn, wts["mla_w_ukv"], name=n + "ukv")
    qr, kf, kvb = _mla_prep(q, kv, projp, tables, name=n + "mla_prep")
    res = _attn_fwd(qr, kf, kvb, name=n + "attn_fwd", job=attn_job)
    if attn_job is not None:
        res, got = res
    o_c, lse = res
    cat = jnp.concatenate([o_a, o_b, o_c.astype(BF16)], axis=1)
    mix = _mm(cat, wts["w_out"], name=n + "out_proj", job=None if attn_job is None else _forward_job(got))
    if attn_job is not None:
        mix, got = mix
        wts = dict(wts, **after_attn(got))
    g1, b1 = row1(sm["ln1_g"][li]), row1(sm["ln1_b"][li])
    d = h.shape[1]
    h1, h1b = _rowwise(_twice(_make_post_mix(alpha)), [h, mix], [g1, b1], [(d, F32), (d, BF16)], tile=1024,
                       name=n + "ln1")
    gu, act = _gate_up_swiglu(h1b, wts["w_gate_up"], name=n + "gate_up")
    ffn = _mm_kblocks(act, wts["w_down"], bm="kn", tm=1024, name=n + "down")
    pg = _mm(h1b, wts["ple_w_gate"], name=n + "ple_gate")
    pp = _mm(p_i, wts["ple_w_proj"], name=n + "ple_proj")
    g2, b2 = row1(sm["ln2_g"][li]), row1(sm["ln2_b"][li])
    if loss_target is None:
        out = _rowwise(_twice(_make_ple_ln(alpha)), [h1, ffn, pg, pp], [g2, b2], [(d, F32), (d, BF16)],
                       name=n + "ln2")
    else:
        def ln_and_loss(h1v, ffnv, pgv, ppv, tv, gv, bv):
            err = _make_ple_ln(alpha)(h1v, ffnv, pgv, ppv, gv, bv)[0] - tv
            return err * (1.0 / d), 0.5 * jnp.sum(jnp.mean(err * err, axis=-1, keepdims=True), axis=0, keepdims=True)

        out = _rowwise(ln_and_loss, [h1, ffn, pg, pp, loss_target], [g2, b2], [(d, F32)], accs=[(1, 1)],
                       name=n + "ln2_loss")
    saved = dict(h=h, hb=hb, h1b=h1b, projp=projp, o_pre=o_pre, states=states, cqn=cqn, ckvn=ckvn, qr=qr, kf=kf, kvb=kvb, o_c=o_c,
                 lse=lse, cat=cat, mix=mix, h1=h1, gu=gu, act=act, ffn=ffn, pg=pg, pp=pp, ng=ng, lg=lg, wts=wts,
                 lbias=lbias, w_s=w_s, bias_full=bias_full, qg=qg, kvg=kvg, g1=g1, b1=b1, g2=g2, b2=b2)
    return tuple(out), saved


RS_EARLY = ("ple_w_proj", "ple_w_gate", "w_down", "w_gate_up", "w_out")
RS_LATE = ("mla_w_uq", "mla_w_ukv", "w_in")


def _layer_backward(li, dh2_parts, p_i, sv, lbs, tables, alpha, core, carried=None):
    n = f"l{li}_b_"
    wts = sv["wts"]
    gr = {}
    dh1_a, dffn, dpg, dpp, gr["ln2_g"], gr["ln2_b"] = _rowwise_vjp(
        _make_ple_ln(alpha), [sv["h1"], sv["ffn"], sv["pg"], sv["pp"]], [sv["g2"], sv["b2"]], [dh2_parts],
        groups=[[0], [1], [2], [3]], gdtypes=[F32, BF16, BF16, BF16], name=n + "ln2")
    big = {}
    big["ple_w_proj"] = _mm(p_i, dpp, am="km", tk=2048, name=n + "ple_proj_dw")
    big["ple_w_gate"] = _mm(sv["h1b"], dpg, am="km", name=n + "ple_gate_dw")
    dh1_b = _mm(dpg, wts["ple_w_gate"], bm="nk", name=n + "ple_gate_dx")
    big["w_down"] = _mm(sv["act"], dffn, am="bkm", tk=4096, name=n + "down_dw")
    dgu = _down_dx_swiglu(dffn, wts["w_down"], sv["gu"], name=n + "down_dx")
    dgu = dgu.reshape((N_DEV,) + dgu.shape[2:])
    big["w_gate_up"], carried_got = _mm(sv["h1b"], dgu, am="km", bm="bkn", om="bmn", tk=4096, name=n + "gate_up_dw",
                                        job=carried), None
    if carried is not None:
        big["w_gate_up"], carried_got = big["w_gate_up"]
    early = [_as_pairs(k, big[k]) for k in RS_EARLY[:-1]]
    dh1_c, theirs = _mm_kblocks(dgu, wts["w_gate_up"], bm="bnk", tm=512, name=n + "gate_up_dx",
                                job=_pair_job(early))
    dh_a, dmix, gr["ln1_g"], gr["ln1_b"] = _rowwise_vjp(
        _make_post_mix(alpha), [sv["h"], sv["mix"]], [sv["g1"], sv["b1"]], [[dh1_a, dh1_b, dh1_c]],
        groups=[[0], [1]], gdtypes=[F32, BF16], name=n + "ln1")
    big["w_out"] = _mm(sv["cat"], dmix, am="km", name=n + "out_proj_dw")
    early.append(_as_pairs("w_out", big["w_out"]))
    dcat, their_w_out = _mm(dmix, wts["w_out"], bm="nk", name=n + "out_proj_dx", job=_pair_job(early[-1:]))
    sums = [_pair_add(x, r, core, name=n + "pair_add_" + k)
            for k, x, r in zip(RS_EARLY, early, list(theirs) + list(their_w_out))]

    (dqr, dkv, dkf), early_quads = _attn_bwd(sv["qr"], sv["kf"], sv["kvb"], dcat, sv["o_c"], sv["lse"],
                                             name=n + "attn", job=_quad_job(sums))
    dqpad, dkr = _mla_prep_bwd(dqr, dkf, tables, name=n + "mla_prep")
    big["mla_w_uq"] = _mm(sv["cqn"], dqpad, am="km", tk=2048, name=n + "uq_dw")
    dcqn = _mm(dqpad, wts["mla_w_uq"], bm="nk", name=n + "uq_dx")
    big["mla_w_ukv"] = _mm(sv["ckvn"], dkv, am="km", tk=2048, name=n + "ukv_dw")
    dckvn = _mm(dkv, wts["mla_w_ukv"], bm="nk", name=n + "ukv_dx")
    projp = sv["projp"]
    dcq, gr["mla_q_norm_g"] = _rowwise_vjp(_fn_rms, [(projp, 384, P_CQ // 384)], [sv["qg"]], [[dcqn]],
                                           groups=[[0]], gdtypes=[BF16], tile=2048, name=n + "q_norm")
    dckv, gr["mla_kv_norm_g"] = _rowwise_vjp(_fn_rms, [(projp, 256, P_CKV // 256)], [sv["kvg"]], [[dckvn]],
                                             groups=[[0]], gdtypes=[BF16], tile=2048, name=n + "kv_norm")
    dsgu, gr["sgu_ln_g"], gr["sgu_ln_b"], gr["sgu_w_s"], gr["sgu_b_s"] = _sgu_bwd(
        projp, sv["lg"], sv["lbias"], sv["w_s"], sv["bias_full"], dcat, name=n + "sgu")
    dhg, gr["hgrn_norm_g"], gr["lower_bound"] = _hgrn_bwd(
        projp, lbs[li], sv["ng"], sv["o_pre"], sv["states"], dcat, name=n + "hgrn")
    dprojp = jnp.concatenate([dhg, dsgu, dcq, dkr, dckv], axis=1)
    big["w_in"] = _unplace_w_in(_mm(sv["hb"], dprojp, am="km", tk=4096, name=n + "proj_dw"),
                                name=n + "proj_dw_shards")
    late = [_as_pairs(k, big[k]) for k in RS_LATE]
    dh_b, theirs = _mm(dprojp, wts["w_in"], bm="nk", tk=P_COLS, name=n + "proj_dx", job=_pair_job(late))
    late_sums = [_pair_add(x, r, core, name=n + "pair_add_" + k) for k, x, r in zip(RS_LATE, late, theirs)]
    return [dh_a, dh_b], gr, early_quads, late_sums, carried_got


def kernel(x, p, positions, ln_in_g, ln_in_b, w_in, hgrn_lb_logits, hgrn_norm_g, sgu_ln_g, sgu_ln_b, sgu_w_s, sgu_b_s, mla_q_norm_g, mla_w_uq, mla_kv_norm_g, mla_w_ukv, w_out, ln1_g, ln1_b, w_gate_up, w_down, ple_w_gate, ple_w_proj, ln2_g, ln2_b, loss_target, m_ln_in_g, m_ln_in_b, m_w_in, m_hgrn_lb_logits, m_hgrn_norm_g, m_sgu_ln_g, m_sgu_ln_b, m_sgu_w_s, m_sgu_b_s, m_mla_q_norm_g, m_mla_w_uq, m_mla_kv_norm_g, m_mla_w_ukv, m_w_out, m_ln1_g, m_ln1_b, m_w_gate_up, m_w_down, m_ple_w_gate, m_ple_w_proj, m_ln2_g, m_ln2_b, v_ln_in_g, v_ln_in_b, v_w_in, v_hgrn_lb_logits, v_hgrn_norm_g, v_sgu_ln_g, v_sgu_ln_b, v_sgu_w_s, v_sgu_b_s, v_mla_q_norm_g, v_mla_w_uq, v_mla_kv_norm_g, v_mla_w_ukv, v_w_out, v_ln1_g, v_ln1_b, v_w_gate_up, v_w_down, v_ple_w_gate, v_ple_w_proj, v_ln2_g, v_ln2_b):
    args = dict(locals())
    w = {k: args[k] for k in ORDER}
    m = {k: args["m_" + k] for k in ORDER}
    v = {k: args["v_" + k] for k in ORDER}
    depth = w_in.shape[0]
    assert depth == 2, "the lower-bound kernel is written for two layers"
    alpha = (2 * depth) ** 0.25
    xs, tgt = x[0], loss_target[0]
    d_model = xs.shape[1]

    shards = [_weight_shards(w, li) for li in range(depth)]
    on_hgrn0 = ("mla_w_uq", "mla_w_ukv", "w_out", "ple_w_gate", "ple_w_proj")
    ffn0 = ("w_gate_up", "w_down")
    first1 = ("w_in", "mla_w_uq", "mla_w_ukv", "w_out")
    on_attn1 = ("w_gate_up", "w_down", "ple_w_gate", "ple_w_proj")
    layer1_first = {}

    def after_hgrn0(got):
        return _usable_weights(dict(zip(on_hgrn0, got)), name="l0")

    def after_attn0(got):
        layer1_first.update(_usable_weights(dict(zip(first1, got[len(ffn0):])), name="l1"))
        return _usable_weights(dict(zip(ffn0, got[:len(ffn0)])), name="l0")

    def after_attn1(got):
        return _usable_weights(dict(zip(on_attn1, got)), name="l1")

    tables = _rope_tables(positions[0])
    row1 = lambda a: a.reshape(1, -1)
    l0, l1 = row1(hgrn_lb_logits[0]), row1(hgrn_lb_logits[1])
    lbs = _rowwise(_fn_lower_bounds, [l0, l1], [], [(HG_W, F32), (HG_W, F32)], name="lower_bounds")

    gin, bin_ = row1(ln_in_g), row1(ln_in_b)
    (h, hb), g_in = _rowwise(_twice(_fn_ln), [xs], [gin, bin_], [(d_model, F32), (d_model, BF16)], name="ln_in",
                             job=_gather_job([shards[0]["w_in"]]))
    w_in0 = _usable_weights({"w_in": _gather_forward(g_in, name="gather_l0_w_in_forward")[0]}, name="l0")
    (h, hb), sv0 = _layer_forward(
        0, h, hb, p[0, 0], w_in0, w, lbs, tables, alpha,
        hgrn_job=_gather_job([shards[0][k] for k in on_hgrn0]), after_hgrn=after_hgrn0,
        attn_job=_gather_job([shards[0][k] for k in ffn0] + [shards[1][k] for k in first1]), after_attn=after_attn0)
    (dy, loss_local), sv1 = _layer_forward(
        1, h, hb, p[1, 0], layer1_first, w, lbs, tables, alpha,
        attn_job=_gather_job([shards[1][k] for k in on_attn1]), after_attn=after_attn1, loss_target=tgt)
    saved = [sv0, sv1]
    loss = lax.psum(loss_local[0, 0], ("x", "y", "c"))

    core = lax.axis_index("c").astype(jnp.int32).reshape(1)
    dparts, grads, quads, carried = [dy], [None] * depth, [None] * depth, None
    for li in reversed(range(depth)):
        dparts, grads[li], early_quads, late_sums, late_quads = _layer_backward(
            li, dparts, p[li, 0], saved[li], lbs, tables, alpha, core, carried=carried)
        quads[li] = dict(zip(RS_EARLY, early_quads))
        if carried is not None:
            quads[li + 1].update(zip(RS_LATE, late_quads))
        carried = _quad_job(late_sums)
    (dx, d_gin, d_bin), late_quads = _rowwise_vjp(_fn_ln, [xs], [gin, bin_], [dparts], groups=[[0]], name="ln_in_b",
                                                   job=carried)
    quads[0].update(zip(RS_LATE, late_quads))
    dl0, dl1 = _rowwise_vjp(_fn_lower_bounds, [l0, l1], [], [[grads[0]["lower_bound"]], [grads[1]["lower_bound"]]],
                            groups=[[0], [1]], name="lower_bounds_b")

    prefixes = ("grad_", "delta_", "new_m_", "new_v_")
    uq_pad = ((0, 0), (0, 0), (0, LANES - ATT_D))
    state = {k: ((jnp.pad(w[k], uq_pad), jnp.pad(m[k], uq_pad), jnp.pad(v[k], uq_pad)) if k == "mla_w_uq"
                 else (w[k], m[k], v[k])) for k in BIG}
    out = {}
    for k in BIG:
        res4 = None
        for li in range(depth):
            res4 = _adamw(quads[li][k], *state[k], li, name=f"adamw_l{li}_{k}", into=res4)
        for pre, a in zip(prefixes, res4):
            out[pre + k] = a[:, :, :ATT_D] if k == "mla_w_uq" else a

    small_g = {"ln_in_g": d_gin.reshape(-1), "ln_in_b": d_bin.reshape(-1),
               "hgrn_lb_logits": jnp.stack([dl0.reshape(-1), dl1.reshape(-1)])}
    for k in SMALL[3:]:
        small_g[k] = jnp.stack([grads[li][k].reshape(w[k].shape[1:]) for li in range(depth)])
    (small_parts,) = _all_gather([_pack([small_g[k] for k in SMALL])], name="gather_small_grads", columns=False)
    slabs = _adamw(small_parts, _pack([w[k] for k in SMALL]), _pack([m[k] for k in SMALL]),
                   _pack([v[k] for k in SMALL]), None, name="adamw_small")
    shapes = [w[k].shape for k in SMALL]
    for pre, slab in zip(prefixes, slabs):
        for k, a in zip(SMALL, _unpack(slab, shapes)):
            out[pre + k] = a
    res = [loss, dx[None]]
    for prefix in ("grad_", "delta_", "new_m_", "new_v_"):
        res += [out[prefix + k] for k in ORDER]
    return tuple(res)
```

```python
import functools
import math

import jax
import jax.numpy as jnp
import numpy as np
from jax import lax
from jax.experimental import pallas as pl
from jax.experimental.pallas import tpu as pltpu

F32 = jnp.float32
BF16 = jnp.bfloat16
MESH = pl.DeviceIdType.MESH

LN_EPS = 1e-5
RMS_EPS = 1e-6
ROPE_THETA = 10000.0
ADAM_LR, ADAM_B1, ADAM_B2, ADAM_EPS, ADAM_WD, ADAM_STEP = 0.001, 0.9, 0.999, 1e-08, 0.01, 10

N_DEV = 8
LANES = 128
HG_CHUNK = 16
HG_W = 256
HEAD = 64
SGU_CHUNK = 128
SGU_STEP_CHUNKS = 4
N_ATT_HEADS = 8
ATT_D = 96
VMEM_LIMIT = 56 * 1024 * 1024

HG_TILE = 256
ATT_TQ = 512
ROW_TILE = 512

P_CQ, P_KR, P_CKV, P_COLS = 1536, 1920, 2048, 2304


def _cparams(sem):
    return pltpu.CompilerParams(dimension_semantics=sem, vmem_limit_bytes=VMEM_LIMIT)


_ANY = pl.BlockSpec(memory_space=pl.ANY)


def _call(body, operands, *, name, grid, in_specs, out_specs, out_shape, sem, scratch_shapes=(), job=None):
    if job is None:
        return pl.pallas_call(body, name=name, grid=grid, in_specs=in_specs, out_specs=out_specs, out_shape=out_shape,
                              scratch_shapes=list(scratch_shapes), compiler_params=_cparams(sem))(*operands)
    single = not isinstance(out_shape, (list, tuple))
    shapes = [out_shape] if single else list(out_shape)
    ospecs = [out_specs] if single else list(out_specs)
    ni, no, ns = len(operands), len(shapes), len(scratch_shapes)
    ji, jo = len(job.inputs), len(job.out_shapes)

    def hosted(*refs):
        p = 0
        parts = []
        for cnt in (ni, ji, no, jo, ns):
            parts.append(refs[p:p + cnt])
            p += cnt
        ins, jins, outs, jouts, scr = parts
        jsems = refs[p:]
        ids = [pl.program_id(a) for a in range(len(grid))]
        first = functools.reduce(lambda a, b: a & b, [i == 0 for i in ids])
        last = functools.reduce(lambda a, b: a & b, [i == g - 1 for i, g in zip(ids, grid)])

        @pl.when(first)
        def _():
            job.start(jins, jouts, jsems)

        body(*ins, *outs, *scr)

        @pl.when(last)
        def _():
            job.finish(jins, jouts, jsems)

    res = pl.pallas_call(
        hosted, name=name, grid=grid,
        in_specs=list(in_specs) + [_ANY] * ji, out_specs=ospecs + [_ANY] * jo,
        out_shape=shapes + list(job.out_shapes),
        scratch_shapes=list(scratch_shapes) + [pltpu.SemaphoreType.DMA((c,)) for c in job.sem_counts],
        input_output_aliases=job.aliases(ni, no),
        compiler_params=_cparams(("arbitrary",) * len(grid)),
    )(*operands, *job.inputs)
    own = res[0] if single else res[:no]
    return own, res[no:]


class _Job:
    def __init__(self, inputs, out_shapes, sem_counts, start, finish, in_place=False):
        self.inputs, self.out_shapes, self.sem_counts = list(inputs), list(out_shapes), list(sem_counts)
        self.start, self.finish, self.in_place = start, finish, in_place

    def aliases(self, first_in, first_out):
        return {first_in + i: first_out + i for i in range(len(self.inputs))} if self.in_place else {}


def _copies_job(inputs, out_shapes, n_remote, n_local, make, in_place=False):
    def start(jins, jouts, sems):
        sends, _, local = make(jins, jouts, *sems)
        for cp in local + sends:
            cp.start()

    def finish(jins, jouts, sems):
        sends, recvs, local = make(jins, jouts, *sems)
        for cp in recvs:
            cp.wait_recv()
        for cp in sends:
            cp.wait_send()
        for cp in local:
            cp.wait()

    return _Job(inputs, out_shapes, [n_remote, n_remote, max(n_local, 1)], start, finish, in_place)


def _run_job(job, *, name):
    ji, jo = len(job.inputs), len(job.out_shapes)

    def body(*refs):
        jins, jouts, sems = refs[:ji], refs[ji:ji + jo], refs[ji + jo:]
        job.start(jins, jouts, sems)
        job.finish(jins, jouts, sems)

    return pl.pallas_call(
        body, name=name, out_shape=list(job.out_shapes), in_specs=[_ANY] * ji, out_specs=[_ANY] * jo,
        scratch_shapes=[pltpu.SemaphoreType.DMA((c,)) for c in job.sem_counts],
        input_output_aliases=job.aliases(0, 0),
    )(*job.inputs)


def _tile(n, pref):
    if n % pref == 0:
        return pref
    best = None
    t = LANES
    while t <= min(n, pref):
        if n % t == 0:
            best = t
        t += LANES
    return best if best is not None else n


def _mm(a, b, *, am="mk", bm="kn", om="mn", out_dtype=F32, tm=2048, tn=1024, tk=1024, name, job=None):
    if am == "mk":
        m, k = a.shape
    elif am == "km":
        k, m = a.shape
    elif am == "bmk":
        m, tk = a.shape[1], a.shape[2]
        k = a.shape[0] * tk
    else:
        k, tm = a.shape[1], a.shape[2]
        m = a.shape[0] * tm
    if bm == "kn":
        kb_, n = b.shape
    elif bm == "nk":
        n, kb_ = b.shape
    elif bm == "bkn":
        kb_, tn = b.shape[1], b.shape[2]
        n = b.shape[0] * tn
    else:
        n, tk = b.shape[1], b.shape[2]
        kb_ = b.shape[0] * tk
    assert kb_ == k, (a.shape, b.shape, am, bm)
    tm, tn, tk = _tile(m, tm), _tile(n, tn), _tile(k, tk)
    nk = k // tk
    dims = (((0 if am in ("km", "bkm") else 1,), (1 if bm in ("nk", "bnk") else 0,)), ((), ()))

    a_spec = {"mk": pl.BlockSpec((tm, tk), lambda i, j, kk: (i, kk)),
              "km": pl.BlockSpec((tk, tm), lambda i, j, kk: (kk, i)),
              "bmk": pl.BlockSpec((None, tm, tk), lambda i, j, kk: (kk, i, 0)),
              "bkm": pl.BlockSpec((None, tk, tm), lambda i, j, kk: (i, kk, 0))}[am]
    b_spec = {"kn": pl.BlockSpec((tk, tn), lambda i, j, kk: (kk, j)),
              "nk": pl.BlockSpec((tn, tk), lambda i, j, kk: (j, kk)),
              "bkn": pl.BlockSpec((None, tk, tn), lambda i, j, kk: (j, kk, 0)),
              "bnk": pl.BlockSpec((None, tn, tk), lambda i, j, kk: (kk, j, 0))}[bm]
    if om == "mn":
        o_spec, o_shape = pl.BlockSpec((tm, tn), lambda i, j, kk: (i, j)), (m, n)
    else:
        o_spec, o_shape = pl.BlockSpec((None, tm, tn), lambda i, j, kk: (j, i, 0)), (n // tn, m, tn)

    def body(a_ref, b_ref, o_ref, *acc):
        kk = pl.program_id(2)

        def prod():
            return lax.dot_general(a_ref[...].astype(BF16), b_ref[...].astype(BF16), dims, preferred_element_type=F32)

        if nk == 1:
            o_ref[...] = prod().astype(o_ref.dtype)
            return
        acc_ref, = acc

        @pl.when(kk == 0)
        def _():
            acc_ref[...] = prod()

        if nk > 2:
            @pl.when((kk > 0) & (kk < nk - 1))
            def _():
                acc_ref[...] += prod()

        @pl.when(kk == nk - 1)
        def _():
            o_ref[...] = (acc_ref[...] + prod()).astype(o_ref.dtype)

    return _call(body, (a, b), name=name, grid=(m // tm, n // tn, nk), in_specs=[a_spec, b_spec], out_specs=o_spec,
                 out_shape=jax.ShapeDtypeStruct(o_shape, out_dtype),
                 scratch_shapes=[pltpu.VMEM((tm, tn), F32)] if nk > 1 else [],
                 sem=("parallel", "parallel", "arbitrary"), job=job)


def _mm_kblocks(a, b, *, bm, tm, name, job=None):
    nkb, m, kb = a.shape
    n = b.shape[1]
    tm = _tile(m, tm)

    def body(a_ref, b_ref, o_ref):
        acc = None
        for j in range(nkb):
            if bm == "kn":
                part = jnp.dot(a_ref[j], b_ref[j * kb:(j + 1) * kb, :], preferred_element_type=F32)
            else:
                part = lax.dot_general(a_ref[j], b_ref[j], _NT, preferred_element_type=F32)
            acc = part if acc is None else acc + part
        o_ref[...] = acc

    b_spec = (pl.BlockSpec(b.shape, lambda i: (0, 0)) if bm == "kn" else pl.BlockSpec(b.shape, lambda i: (0, 0, 0)))
    return _call(body, (a, b), name=name, grid=(m // tm,),
                 in_specs=[pl.BlockSpec((nkb, tm, kb), lambda i: (0, i, 0)), b_spec],
                 out_specs=pl.BlockSpec((tm, n), lambda i: (i, 0)), out_shape=jax.ShapeDtypeStruct((m, n), F32),
                 sem=("parallel",), job=job)


def _row_operand(a, tile):
    if isinstance(a, tuple):
        arr, w, j = a
        return arr, pl.BlockSpec((tile, w), lambda i, j=j: (i, j))
    return a, pl.BlockSpec((tile, a.shape[1]), lambda i: (i, 0))


def _const_spec(c):
    nd = c.ndim
    return pl.BlockSpec(c.shape, lambda i, nd=nd: (0,) * nd)


def _rowwise(fn, rows, consts, outs, *, name, accs=(), tile=None, job=None):
    t_rows = (rows[0][0] if isinstance(rows[0], tuple) else rows[0]).shape[0]
    tile = min(tile or ROW_TILE, t_rows)
    arrs, specs = zip(*[_row_operand(a, tile) for a in rows])
    nin, no = len(rows) + len(consts), len(outs)

    def body(*refs):
        res = fn(*[r[...] for r in refs[:nin]])
        for r, v in zip(refs[nin:nin + no], res[:no]):
            r[...] = v.astype(r.dtype)
        if accs:
            a_refs = refs[nin + no:]

            @pl.when(pl.program_id(0) == 0)
            def _():
                for r in a_refs:
                    r[...] = jnp.zeros_like(r)

            for r, v in zip(a_refs, res[no:]):
                r[...] += v

    out_shape = [jax.ShapeDtypeStruct((t_rows, w), dt) for w, dt in outs]
    out_shape += [jax.ShapeDtypeStruct(s, F32) for s in accs]
    out_specs = [pl.BlockSpec((tile, w), lambda i: (i, 0)) for w, _ in outs]
    out_specs += [pl.BlockSpec(s, lambda i, nd=len(s): (0,) * nd) for s in accs]
    return _call(body, (*arrs, *consts), name=name, grid=(t_rows // tile,),
                 in_specs=list(specs) + [_const_spec(c) for c in consts],
                 out_specs=out_specs, out_shape=out_shape, sem=("arbitrary",), job=job)


def _rowwise_vjp(fn, rows, consts, cts, *, name, groups, tile=None, gdtypes=None, job=None):
    t_rows = (rows[0][0] if isinstance(rows[0], tuple) else rows[0]).shape[0]
    tile = min(tile or ROW_TILE, t_rows)
    arrs, specs = zip(*[_row_operand(a, tile) for a in rows])
    flat_cts = [c for group in cts for c in group]
    ct_arrs, ct_specs = zip(*[_row_operand(a, tile) for a in flat_cts])
    nr, nc, nct, ng = len(rows), len(consts), len(flat_cts), len(groups)

    def width(a):
        return a[1] if isinstance(a, tuple) else a.shape[1]

    def body(*refs):
        rv = [r[...].astype(F32) for r in refs[:nr]]
        cv = [r[...] for r in refs[nr:nr + nc]]
        ct_refs = refs[nr + nc:nr + nc + nct]
        ctv, pos = [], 0
        for group in cts:
            s = ct_refs[pos][...].astype(F32)
            for r in ct_refs[pos + 1:pos + len(group)]:
                s = s + r[...].astype(F32)
            ctv.append(s)
            pos += len(group)
        _, pull = jax.vjp(fn, *rv, *cv)
        grads = pull(tuple(ctv))
        g_refs = refs[nr + nc + nct:nr + nc + nct + ng]
        for r, idx in zip(g_refs, groups):
            parts = [grads[i] for i in idx]
            r[...] = (parts[0] if len(parts) == 1 else jnp.concatenate(parts, axis=1)).astype(r.dtype)
        c_refs = refs[nr + nc + nct + ng:]

        @pl.when(pl.program_id(0) == 0)
        def _():
            for r in c_refs:
                r[...] = jnp.zeros_like(r)

        for r, v in zip(c_refs, grads[nr:]):
            r[...] += v

    gw = [sum(width(rows[i]) for i in idx) for idx in groups]
    gdtypes = gdtypes or [F32] * ng
    out_shape = [jax.ShapeDtypeStruct((t_rows, w), dt) for w, dt in zip(gw, gdtypes)]
    out_shape += [jax.ShapeDtypeStruct(c.shape, F32) for c in consts]
    out_specs = [pl.BlockSpec((tile, w), lambda i: (i, 0)) for w in gw]
    out_specs += [_const_spec(c) for c in consts]
    return _call(body, (*arrs, *consts, *ct_arrs), name=name, grid=(t_rows // tile,),
                 in_specs=list(specs) + [_const_spec(c) for c in consts] + list(ct_specs),
                 out_specs=out_specs, out_shape=out_shape, sem=("arbitrary",), job=job)


def _layer_norm(x, g, b):
    mu = jnp.mean(x, axis=-1, keepdims=True)
    xc = x - mu
    var = jnp.mean(xc * xc, axis=-1, keepdims=True)
    return xc * lax.rsqrt(var + LN_EPS) * g + b


def _sigmoid(x):
    return 1.0 / (1.0 + jnp.exp(-x))


def _fn_ln(x, g, b):
    return (_layer_norm(x, g, b),)


def _fn_rms(x, g):
    return (x * lax.rsqrt(jnp.mean(x * x, axis=-1, keepdims=True) + RMS_EPS) * g,)


def _make_post_mix(alpha):
    def fn(h, mix, g, b):
        return (_layer_norm(alpha * h + mix, g, b),)
    return fn


def _make_ple_ln(alpha):
    def fn(h1, ffn, pg, pp, g, b):
        return (_layer_norm(alpha * h1 + ffn + _sigmoid(pg) * pp, g, b),)
    return fn


def _fn_lower_bounds(l0, l1):
    m = jnp.maximum(l0, l1)
    e0, e1 = jnp.exp(l0 - m), jnp.exp(l1 - m)
    s = e0 + e1
    p0, p1 = e0 / s, e1 / s
    return (p0 - p0, (p0 + p1) - p0)


def _split_dot(x, e_bf16):
    hi = x.astype(BF16)
    lo = (x - hi.astype(F32)).astype(BF16)
    return (jnp.dot(hi, e_bf16, preferred_element_type=F32) + jnp.dot(lo, e_bf16, preferred_element_type=F32))


def _hgrn_common(th):
    rm = lax.broadcasted_iota(jnp.int32, (th, HG_W), 0) % HG_CHUNK

    def seg_cumsum(x):
        for s in (1, 2, 4, 8):
            x = x + jnp.where(rm >= s, pltpu.roll(x, s, 0), 0.0)
        return x

    def seg_rcumsum(x):
        for s in (1, 2, 4, 8):
            x = x + jnp.where(rm < HG_CHUNK - s, pltpu.roll(x, th - s, 0), 0.0)
        return x

    ri = lax.broadcasted_iota(jnp.int32, (HG_W, HG_W), 0) // HEAD
    ci = lax.broadcasted_iota(jnp.int32, (HG_W, HG_W), 1) // HEAD
    head_f32 = (ri == ci).astype(F32)
    head_bf16 = head_f32.astype(BF16)

    def headsum(x, pieces=2):
        if pieces == 1:
            return jnp.dot(x.astype(BF16), head_bf16, preferred_element_type=F32)
        return _split_dot(x, head_bf16)

    return rm, seg_cumsum, seg_rcumsum, head_f32, headsum


def _hgrn_gates(qr, fl, lb):
    sg = _sigmoid(fl)
    f = lb + (1.0 - lb) * sg
    sq = _sigmoid(qr)
    return sg, f, jnp.log(f), 1.0 - f, qr * sq, sq


def _shifted(x, d, th):
    return x if d == 0 else pltpu.roll(x, d, 0)


def _unshift(x, d, th):
    return x if d == 0 else pltpu.roll(x, th - d, 0)


def _hgrn_fwd(projp, lb, ng, *, name, job=None):
    t_rows = projp.shape[0]
    th = min(HG_TILE, t_rows)
    nct = th // HG_CHUNK

    def body(q_ref, f_ref, i_ref, g_ref, lb_ref, ng_ref, oa_ref, opre_ref, st_out_ref,
             st_ref, vtm_ref, kv_ref, qe_ref, dec_ref, oint_ref):
        rm, seg_cumsum, seg_rcumsum, head_f32, headsum = _hgrn_common(th)

        @pl.when(pl.program_id(0) == 0)
        def _():
            st_ref[...] = jnp.zeros_like(st_ref)

        qr, fl, v, g = q_ref[...], f_ref[...], i_ref[...], g_ref[...]
        _, f, lf, k, q, _ = _hgrn_gates(qr, fl, lb_ref[...])
        b = seg_cumsum(lf)

        o = jnp.zeros((th, HG_W), F32)
        for d in range(HG_CHUNK):
            kd, bd, vd = _shifted(k, d, th), _shifted(b, d, th), _shifted(v, d, th)
            e = jnp.exp(jnp.where(rm >= d, b - bd, -1e30))
            o = o + headsum(q * kd * e, 1) * vd

        blast = seg_rcumsum(jnp.where(rm == HG_CHUNK - 1, b, 0.0))
        kte = (k * jnp.exp(blast - b)).astype(BF16)
        qe_ref[...] = q * jnp.exp(b)
        dec_ref[...] = jnp.exp(blast)
        vt = v.T
        lane_chunk = lax.broadcasted_iota(jnp.int32, (HG_W, th), 1) // HG_CHUNK
        for c in range(nct):
            vtm_ref[c * HG_W:(c + 1) * HG_W, :] = jnp.where(lane_chunk == c, vt, 0.0).astype(BF16)
        kv_ref[...] = jnp.dot(vtm_ref[...], kte, preferred_element_type=F32)

        s = st_ref[...]
        for c in range(nct):
            rows = slice(c * HG_CHUNK, (c + 1) * HG_CHUNK)
            st_out_ref[c] = s
            oint_ref[rows, :] = lax.dot_general(qe_ref[rows, :].astype(BF16), s.astype(BF16),
                                                (((1,), (1,)), ((), ())), preferred_element_type=F32)
            dec = jnp.max(dec_ref[rows, :], axis=0, keepdims=True)
            s = s * dec + kv_ref[c * HG_W:(c + 1) * HG_W, :] * head_f32
        st_ref[...] = s

        o = o + oint_ref[...]
        opre_ref[...] = o
        r = lax.rsqrt(headsum(o * o) * (1.0 / HEAD) + RMS_EPS)
        oa_ref[...] = (o * r * ng_ref[...] * (g * _sigmoid(g))).astype(oa_ref.dtype)

    col = lambda j: pl.BlockSpec((th, HG_W), lambda i, j=j: (i, j))
    vec = pl.BlockSpec((1, HG_W), lambda i: (0, 0))
    row = pl.BlockSpec((th, HG_W), lambda i: (i, 0))
    n_chunks = t_rows // HG_CHUNK
    return _call(
        body, (projp, projp, projp, projp, lb, ng), name=name, grid=(t_rows // th,),
        in_specs=[col(0), col(1), col(2), col(3), vec, vec],
        out_specs=[row, row, pl.BlockSpec((nct, HG_W, HG_W), lambda i: (i, 0, 0))],
        out_shape=[jax.ShapeDtypeStruct((t_rows, HG_W), BF16), jax.ShapeDtypeStruct((t_rows, HG_W), F32),
                   jax.ShapeDtypeStruct((n_chunks, HG_W, HG_W), F32)],
        scratch_shapes=[pltpu.VMEM((HG_W, HG_W), F32), pltpu.VMEM((nct * HG_W, th), BF16),
                        pltpu.VMEM((nct * HG_W, HG_W), F32), pltpu.VMEM((th, HG_W), F32),
                        pltpu.VMEM((th, HG_W), F32), pltpu.VMEM((th, HG_W), F32)],
        sem=("arbitrary",), job=job)


def _hgrn_bwd(projp, lb, ng, opre, states, dcat, *, name):
    t_rows = projp.shape[0]
    th = min(HG_TILE, t_rows)
    nct = th // HG_CHUNK
    nt = t_rows // th

    def body(q_ref, f_ref, i_ref, g_ref, lb_ref, ng_ref, opre_ref, st_in_ref, do_ref,
             dproj_ref, dng_ref, dlb_ref,
             gst_ref, dotm_ref, qg_ref, v_ref, kte_ref, dop_ref, dec_ref, dkte_ref, dvi_ref, dqe_ref, ddec_ref):
        rm, seg_cumsum, seg_rcumsum, head_f32, headsum = _hgrn_common(th)

        @pl.when(pl.program_id(0) == 0)
        def _():
            gst_ref[...] = jnp.zeros_like(gst_ref)
            dng_ref[...] = jnp.zeros_like(dng_ref)
            dlb_ref[...] = jnp.zeros_like(dlb_ref)

        qr, fl, v, g = q_ref[...], f_ref[...], i_ref[...], g_ref[...]
        lb, ngv = lb_ref[...], ng_ref[...]
        sg, f, lf, k, q, sq = _hgrn_gates(qr, fl, lb)
        b = seg_cumsum(lf)
        blast = seg_rcumsum(jnp.where(rm == HG_CHUNK - 1, b, 0.0))
        eb = jnp.exp(b)
        ekb = jnp.exp(blast - b)
        qe, kte, dec = q * eb, k * ekb, jnp.exp(blast)

        do_out, op = do_ref[...], opre_ref[...]
        sgg = _sigmoid(g)
        sil = g * sgg
        r = lax.rsqrt(headsum(op * op) * (1.0 / HEAD) + RMS_EPS)
        on = op * r
        dng_ref[...] += jnp.sum(do_out * on * sil, axis=0, keepdims=True)
        dg = do_out * on * ngv * (sgg * (1.0 + g * (1.0 - sgg)))
        don = do_out * ngv * sil
        dop = r * (don - on * (headsum(don * on) * (1.0 / HEAD)))

        v_ref[...] = v
        kte_ref[...] = kte
        dop_ref[...] = dop
        dec_ref[...] = dec
        dot_t = dop.T
        lane_chunk = lax.broadcasted_iota(jnp.int32, (HG_W, th), 1) // HG_CHUNK
        for c in range(nct):
            dotm_ref[c * HG_W:(c + 1) * HG_W, :] = jnp.where(lane_chunk == c, dot_t, 0.0).astype(BF16)
        qg_ref[...] = jnp.dot(dotm_ref[...], qe.astype(BF16), preferred_element_type=F32)

        gs = gst_ref[...]
        for c in reversed(range(nct)):
            rows = slice(c * HG_CHUNK, (c + 1) * HG_CHUNK)
            s = st_in_ref[c]
            gm = (gs * head_f32).astype(BF16)
            dkte_ref[rows, :] = jnp.dot(v_ref[rows, :].astype(BF16), gm, preferred_element_type=F32)
            dvi_ref[rows, :] = lax.dot_general(kte_ref[rows, :].astype(BF16), gm, (((1,), (1,)), ((), ())),
                                               preferred_element_type=F32)
            dqe_ref[rows, :] = jnp.dot(dop_ref[rows, :].astype(BF16), s.astype(BF16), preferred_element_type=F32)
            ddec_ref[rows, :] = jnp.broadcast_to(jnp.sum(gs * s, axis=0, keepdims=True), (HG_CHUNK, HG_W))
            dec_c = jnp.max(dec_ref[rows, :], axis=0, keepdims=True)
            gs = gs * dec_c + qg_ref[c * HG_W:(c + 1) * HG_W, :] * head_f32
        gst_ref[...] = gs

        dkte, dqe = dkte_ref[...], dqe_ref[...]
        dq = dqe * eb
        dk = dkte * ekb
        db = dqe * qe - dkte * kte
        dv = dvi_ref[...]
        dblast = dkte * kte + jnp.where(rm == HG_CHUNK - 1, ddec_ref[...] * dec, 0.0)

        for d in range(HG_CHUNK):
            kd, bd, vd = _shifted(k, d, th), _shifted(b, d, th), _shifted(v, d, th)
            e = jnp.exp(jnp.where(rm >= d, b - bd, -1e30))
            p = q * kd * e
            sc = headsum(p, 1)
            dsc = headsum(dop * vd, 1)
            dv = dv + _unshift(sc * dop, d, th)
            dq = dq + dsc * kd * e
            dk = dk + _unshift(dsc * q * e, d, th)
            darg = dsc * p
            db = db + darg - _unshift(darg, d, th)

        db = db + jnp.where(rm == HG_CHUNK - 1, seg_cumsum(dblast), 0.0)
        dlf = seg_rcumsum(db)
        df = dlf / f - dk
        dlb_ref[...] += jnp.sum(df * (1.0 - sg), axis=0, keepdims=True)
        dfl = df * (1.0 - lb) * sg * (1.0 - sg)
        dqr = dq * (sq * (1.0 + qr * (1.0 - sq)))
        dproj_ref[...] = jnp.concatenate([dqr, dfl, dv, dg], axis=1).astype(dproj_ref.dtype)

    rev = lambda i: nt - 1 - i
    col = lambda j: pl.BlockSpec((th, HG_W), lambda i, j=j: (rev(i), j))
    vec = pl.BlockSpec((1, HG_W), lambda i: (0, 0))
    row = pl.BlockSpec((th, HG_W), lambda i: (rev(i), 0))
    tile_f32 = pltpu.VMEM((th, HG_W), F32)
    return pl.pallas_call(
        body, name=name, grid=(nt,),
        in_specs=[col(0), col(1), col(2), col(3), vec, vec, row,
                  pl.BlockSpec((nct, HG_W, HG_W), lambda i: (rev(i), 0, 0)), col(0)],
        out_specs=[pl.BlockSpec((th, 4 * HG_W), lambda i: (rev(i), 0)), vec, vec],
        out_shape=[jax.ShapeDtypeStruct((t_rows, 4 * HG_W), BF16), jax.ShapeDtypeStruct((1, HG_W), F32),
                   jax.ShapeDtypeStruct((1, HG_W), F32)],
        scratch_shapes=[pltpu.VMEM((HG_W, HG_W), F32), pltpu.VMEM((nct * HG_W, th), BF16),
                        pltpu.VMEM((nct * HG_W, HG_W), F32)] + [tile_f32] * 8,
        compiler_params=_cparams(("arbitrary",)),
    )(projp, projp, projp, projp, lb, ng, opre, states, dcat)


_INV_SQRT2 = 1.0 / math.sqrt(2.0)
_INV_SQRT2PI = 1.0 / math.sqrt(2.0 * math.pi)


def _gelu(x):
    return 0.5 * x * (1.0 + lax.erf(x * _INV_SQRT2))


def _gelu_grad(x):
    return 0.5 * (1.0 + lax.erf(x * _INV_SQRT2)) + x * jnp.exp(-0.5 * x * x) * _INV_SQRT2PI


def _sgu_parts(bu, bv, lg, lbias, w_ref, n_groups):
    c = SGU_CHUNK
    tril = (lax.broadcasted_iota(jnp.int32, (c, c), 0) >= lax.broadcasted_iota(jnp.int32, (c, c), 1)).astype(F32)
    gid = lax.broadcasted_iota(jnp.int32, bu.shape, 1) // HEAD
    u = _gelu(bu)
    gv = _gelu(bv)
    mu = jnp.mean(gv, axis=-1, keepdims=True)
    xc = gv - mu
    rstd = lax.rsqrt(jnp.mean(xc * xc, axis=-1, keepdims=True) + LN_EPS)
    xhat = xc * rstd
    vn = xhat * lg + lbias
    ws = [w_ref[gi] * tril for gi in range(n_groups)]
    return tril, gid, u, rstd, xhat, vn, ws


def _sgu_fwd(projp, lg, lbias, w_s, bias_full, *, name, job=None):
    t_rows = projp.shape[0]
    n_groups = w_s.shape[0]
    c = SGU_CHUNK
    rows_per_step = min(SGU_STEP_CHUNKS * c, t_rows)

    def body(u_ref, v_ref, lg_ref, lb_ref, w_ref, bias_ref, o_ref):
        for r0 in range(0, rows_per_step, c):
            rows = slice(r0, r0 + c)
            _, gid, u, _, _, vn, ws = _sgu_parts(u_ref[rows, :], v_ref[rows, :], lg_ref[...], lb_ref[...], w_ref,
                                                 n_groups)
            vnb = vn.astype(BF16)
            z = bias_ref[...]
            for gi in range(n_groups):
                z = z + jnp.where(gid == gi, jnp.dot(ws[gi].astype(BF16), vnb, preferred_element_type=F32), 0.0)
            o_ref[rows, :] = (u * z).astype(o_ref.dtype)

    col = lambda j: pl.BlockSpec((rows_per_step, HG_W), lambda i, j=j: (i, j))
    return _call(
        body, (projp, projp, lg, lbias, w_s, bias_full), name=name, grid=(t_rows // rows_per_step,),
        in_specs=[col(4), col(5), _const_spec(lg), _const_spec(lbias), _const_spec(w_s), _const_spec(bias_full)],
        out_specs=pl.BlockSpec((rows_per_step, HG_W), lambda i: (i, 0)),
        out_shape=jax.ShapeDtypeStruct((t_rows, HG_W), BF16), sem=("arbitrary",), job=job)


def _sgu_bwd(projp, lg, lbias, w_s, bias_full, dcat, *, name):
    t_rows = projp.shape[0]
    n_groups = w_s.shape[0]
    c = SGU_CHUNK
    rows_per_step = min(SGU_STEP_CHUNKS * c, t_rows)
    n = t_rows // rows_per_step

    def body(u_ref, v_ref, lg_ref, lb_ref, w_ref, bias_ref, do_ref,
             dproj_ref, dlg_ref, dlb_ref, dw_ref, dbs_ref, dbias_acc):
        i = pl.program_id(0)

        @pl.when(i == 0)
        def _():
            dlg_ref[...] = jnp.zeros_like(dlg_ref)
            dlb_ref[...] = jnp.zeros_like(dlb_ref)
            dw_ref[...] = jnp.zeros_like(dw_ref)
            dbias_acc[...] = jnp.zeros_like(dbias_acc)

        lg_v = lg_ref[...]
        for r0 in range(0, rows_per_step, c):
            rows = slice(r0, r0 + c)
            bu, bv = u_ref[rows, :], v_ref[rows, :]
            tril, gid, u, rstd, xhat, vn, ws = _sgu_parts(bu, bv, lg_v, lb_ref[...], w_ref, n_groups)
            vnb = vn.astype(BF16)
            z = bias_ref[...]
            for gi in range(n_groups):
                z = z + jnp.where(gid == gi, jnp.dot(ws[gi].astype(BF16), vnb, preferred_element_type=F32), 0.0)
            do = do_ref[rows, :]
            dbu = do * z * _gelu_grad(bu)
            dz = do * u
            dbias_acc[...] += dz
            dvn = jnp.zeros_like(dz)
            for gi in range(n_groups):
                dzg = jnp.where(gid == gi, dz, 0.0).astype(BF16)
                dw_ref[gi] += lax.dot_general(dzg, vnb, (((1,), (1,)), ((), ())), preferred_element_type=F32) * tril
                dvn = dvn + jnp.dot(ws[gi].T.astype(BF16), dzg, preferred_element_type=F32)
            dlg_ref[...] += jnp.sum(dvn * xhat, axis=0, keepdims=True)
            dlb_ref[...] += jnp.sum(dvn, axis=0, keepdims=True)
            dxh = dvn * lg_v
            dgv = rstd * (dxh - jnp.mean(dxh, axis=-1, keepdims=True)
                          - xhat * jnp.mean(dxh * xhat, axis=-1, keepdims=True))
            dproj_ref[rows, :] = jnp.concatenate([dbu, dgv * _gelu_grad(bv)], axis=1).astype(dproj_ref.dtype)

        @pl.when(i == n - 1)
        def _():
            dbs_ref[...] = jnp.sum(dbias_acc[...].T.reshape(n_groups, HEAD, c), axis=1)

    col = lambda j: pl.BlockSpec((rows_per_step, HG_W), lambda i, j=j: (i, j))
    return pl.pallas_call(
        body, name=name, grid=(n,),
        in_specs=[col(4), col(5), _const_spec(lg), _const_spec(lbias), _const_spec(w_s), _const_spec(bias_full),
                  col(1)],
        out_specs=[pl.BlockSpec((rows_per_step, 2 * HG_W), lambda i: (i, 0)), _const_spec(lg), _const_spec(lbias),
                   _const_spec(w_s), pl.BlockSpec((n_groups, c), lambda i: (0, 0))],
        out_shape=[jax.ShapeDtypeStruct((t_rows, 2 * HG_W), BF16), jax.ShapeDtypeStruct(lg.shape, F32),
                   jax.ShapeDtypeStruct(lbias.shape, F32), jax.ShapeDtypeStruct(w_s.shape, F32),
                   jax.ShapeDtypeStruct((n_groups, c), F32)],
        scratch_shapes=[pltpu.VMEM((c, HG_W), F32)],
        compiler_params=_cparams(("arbitrary",)),
    )(projp, projp, lg, lbias, w_s, bias_full, dcat)


def _rope_tables(positions):
    t = positions.shape[0]
    inv_freq = ROPE_THETA ** (-jnp.arange(0, 32, 2, dtype=F32) / 32)
    ang = positions.astype(F32)[:, None] * inv_freq
    cos, sin = jnp.cos(ang), jnp.sin(ang)
    z = lambda w: jnp.zeros((t, w), F32)
    cos_t = jnp.concatenate([jnp.ones((t, 64), F32), cos, cos, z(32)], axis=1)
    sin_up = jnp.concatenate([z(80), sin, z(32)], axis=1)
    sin_dn = jnp.concatenate([z(64), -sin, z(48)], axis=1)
    return cos_t, sin_up, sin_dn


def _rep(x, n):
    return x if n == 1 else jnp.concatenate([x] * n, axis=1)


def _rope(x, cos_t, sin_up, sin_dn):
    w = x.shape[1]
    return x * cos_t + pltpu.roll(x, 16, 1) * sin_up + pltpu.roll(x, w - 16, 1) * sin_dn


def _rope_t(dy, cos_t, sin_up, sin_dn):
    w = dy.shape[1]
    return dy * cos_t + pltpu.roll(dy * sin_up, w - 16, 1) + pltpu.roll(dy * sin_dn, 16, 1)


def _mla_prep(q, kv, projp, tables, *, name):
    nh = N_ATT_HEADS

    def fn(qv, kvv, kr, cos_t, sin_up, sin_dn):
        qr = _rope(qv, _rep(cos_t, nh), _rep(sin_up, nh), _rep(sin_dn, nh))
        krr = _rope(kr, cos_t, sin_up, sin_dn)
        lane = lax.broadcasted_iota(jnp.int32, kvv.shape, 1) % LANES
        return qr, jnp.where(lane < HEAD, kvv, 0.0) + _rep(krr, nh), kvv

    w = q.shape[1]
    return _rowwise(fn, [q, kv, (projp, LANES, P_KR // LANES)] + list(tables), [],
                    [(w, BF16), (w, BF16), (w, BF16)], tile=1024, name=name)


def _mla_prep_bwd(dqr, dkf, tables, *, name):
    nh = N_ATT_HEADS

    def fn(dq, dk, cos_t, sin_up, sin_dn):
        dqp = _rope_t(dq, _rep(cos_t, nh), _rep(sin_up, nh), _rep(sin_dn, nh))
        dkrr = dk[:, 0:LANES]
        for h in range(1, nh):
            dkrr = dkrr + dk[:, LANES * h:LANES * (h + 1)]
        return dqp, _rope_t(dkrr, cos_t, sin_up, sin_dn)

    return _rowwise(fn, [dqr, dkf] + list(tables), [], [(dqr.shape[1], BF16), (LANES, BF16)], tile=1024, name=name)


_LOG2E = 1.0 / math.log(2.0)
_NT = (((1,), (1,)), ((), ()))
_TN = (((0,), (0,)), ((), ()))


def _attn_fwd(qr, kf, kvb, *, name, job=None):
    t_rows = qr.shape[0]
    tq = min(ATT_TQ, t_rows)
    nb = t_rows // tq
    scale = ATT_D ** -0.5

    c2 = scale * _LOG2E

    def body(q_ref, kf_ref, kvb_ref, o_ref, lse_ref):
        qi = pl.program_id(1)
        lane = lax.broadcasted_iota(jnp.int32, (tq, LANES), 1)
        causal_t = (lax.broadcasted_iota(jnp.int32, (tq, tq), 0) <= lax.broadcasted_iota(jnp.int32, (tq, tq), 1))
        heads = [slice(hh * LANES, (hh + 1) * LANES) for hh in range(2)]
        qs = [q_ref[:, cols] for cols in heads]

        def block(first, n_keys, carry, diagonal):
            rows = pl.ds(pl.multiple_of(first * tq, tq), n_keys)
            new = []
            for q, cols, (m_old, l_old, acc_t) in zip(qs, heads, carry):
                s_t = lax.dot_general(kf_ref[rows, cols], q, _NT, preferred_element_type=F32)
                if diagonal:
                    s_t = jnp.where(causal_t, s_t, -1e30)
                m_new = jnp.maximum(m_old, jnp.max(s_t, axis=0, keepdims=True))
                p_t = jnp.exp2((s_t - m_new) * c2)
                a = jnp.exp2((m_old - m_new) * c2)
                pv_t = lax.dot_general(kvb_ref[rows, cols], p_t.astype(BF16), _TN, preferred_element_type=F32)
                new.append((m_new, a * l_old + jnp.sum(p_t, axis=0, keepdims=True), a * acc_t + pv_t))
            return tuple(new)

        init = (jnp.full((1, tq), -1e30, F32), jnp.zeros((1, tq), F32), jnp.zeros((LANES, tq), F32))
        carry = lax.fori_loop(0, qi // 4, lambda g, c: block(4 * g, 4 * tq, c, False), (init, init))
        carry = lax.cond((qi // 2) % 2 == 1, lambda c: block(4 * (qi // 4), 2 * tq, c, False), lambda c: c, carry)
        carry = lax.cond(qi % 2 == 1, lambda c: block(qi - 1, tq, c, False), lambda c: c, carry)
        outs = []
        for hh, (m_fin, l_fin, acc_t) in enumerate(block(qi, tq, carry, True)):
            lse_ref[hh] = m_fin * scale + jnp.log(l_fin)
            outs.append((acc_t / l_fin).T)
        o_ref[...] = jnp.where(lane < HEAD, pltpu.roll(outs[0], HEAD, 1), outs[1])

    pair = pl.BlockSpec((t_rows, 2 * LANES), lambda pr, qi: (0, pr))
    return _call(
        body, (qr, kf, kvb), name=name, grid=(N_ATT_HEADS // 2, nb),
        in_specs=[pl.BlockSpec((tq, 2 * LANES), lambda pr, qi: (qi, pr)), pair, pair],
        out_specs=[pl.BlockSpec((tq, LANES), lambda pr, qi: (qi, pr)),
                   pl.BlockSpec((2, 1, tq), lambda pr, qi: (pr, 0, qi))],
        out_shape=[jax.ShapeDtypeStruct((t_rows, N_ATT_HEADS * HEAD), F32),
                   jax.ShapeDtypeStruct((N_ATT_HEADS, 1, t_rows), F32)],
        sem=("parallel", "arbitrary"), job=job)


def _attn_bwd(qr, kf, kvb, dcat, o, lse, *, name, job=None):
    t_rows = qr.shape[0]
    tq = min(ATT_TQ, t_rows)
    nb = t_rows // tq
    scale = ATT_D ** -0.5
    c2 = scale * _LOG2E
    do_off = 2 * HG_W // LANES

    def body(q_ref, kf_ref, kvb_ref, do_ref, o_ref, lse_ref, dq_ref, dkv_ref, dk_ref):
        ki = pl.program_id(1)

        @pl.when(ki == 0)
        def _():
            dq_ref[...] = jnp.zeros_like(dq_ref)

        lane = lax.broadcasted_iota(jnp.int32, (tq, LANES), 1)
        causal_t = (lax.broadcasted_iota(jnp.int32, (tq, tq), 0) <= lax.broadcasted_iota(jnp.int32, (tq, tq), 1))
        heads = [slice(hh * LANES, (hh + 1) * LANES) for hh in range(2)]
        ks = [kf_ref[:, cols] for cols in heads]
        vs = [kvb_ref[:, cols] for cols in heads]

        def block(qi, n_q, carry, diagonal):
            rows = pl.ds(pl.multiple_of(qi * tq, tq), n_q)
            do_pair, o_pair = do_ref[rows, :], o_ref[rows, :]
            upper = lax.broadcasted_iota(jnp.int32, do_pair.shape, 1) >= HEAD
            new = []
            for hh, (cols, k, v, (dk, dv)) in enumerate(zip(heads, ks, vs, carry)):
                q = q_ref[rows, cols]
                do, ov = (pltpu.roll(do_pair, HEAD, 1), pltpu.roll(o_pair, HEAD, 1)) if hh == 0 else (do_pair, o_pair)
                do = jnp.where(upper, do, 0.0)
                delta = jnp.sum((do * ov).T, axis=0, keepdims=True)
                s_t = lax.dot_general(k, q, _NT, preferred_element_type=F32)
                if diagonal:
                    s_t = jnp.where(causal_t, s_t, -1e30)
                p_t = jnp.exp2(s_t * c2 - lse_ref[hh, :, rows] * _LOG2E)
                dob = do.astype(BF16)
                dv = dv + jnp.dot(p_t.astype(BF16), dob, preferred_element_type=F32)
                dp_t = lax.dot_general(v, dob, _NT, preferred_element_type=F32)
                ds_t = (p_t * (dp_t - delta) * scale).astype(BF16)
                dk = dk + jnp.dot(ds_t, q, preferred_element_type=F32)
                dq_ref[rows, cols] += lax.dot_general(ds_t, k, _TN, preferred_element_type=F32)
                new.append((dk, dv))
            return tuple(new)

        zero = jnp.zeros((tq, LANES), F32)
        carry = block(ki, tq, ((zero, zero), (zero, zero)), True)
        rest = nb - 1 - ki
        carry = lax.fori_loop(0, rest // 2, lambda g, c: block(ki + 1 + 2 * g, 2 * tq, c, False), carry)
        carry = lax.cond(rest % 2 == 1, lambda c: block(nb - 1, tq, c, False), lambda c: c, carry)
        dkv_ref[...] = jnp.concatenate([jnp.where(lane < HEAD, dk, dv) for dk, dv in carry],
                                       axis=1).astype(dkv_ref.dtype)
        dk_ref[...] = jnp.concatenate([dk for dk, _ in carry], axis=1)

    pair_all = pl.BlockSpec((t_rows, 2 * LANES), lambda pr, ki: (0, pr))
    pair_blk = pl.BlockSpec((tq, 2 * LANES), lambda pr, ki: (ki, pr))
    wide = jax.ShapeDtypeStruct((t_rows, N_ATT_HEADS * LANES), F32)
    return _call(
        body, (qr, kf, kvb, dcat, o, lse), name=name, grid=(N_ATT_HEADS // 2, nb),
        in_specs=[pair_all, pair_blk, pair_blk,
                  pl.BlockSpec((t_rows, LANES), lambda pr, ki: (0, do_off + pr)),
                  pl.BlockSpec((t_rows, LANES), lambda pr, ki: (0, pr)),
                  pl.BlockSpec((2, 1, t_rows), lambda pr, ki: (pr, 0, 0))],
        out_specs=[pair_all, pair_blk, pair_blk],
        out_shape=[wide, jax.ShapeDtypeStruct(wide.shape, BF16), wide],
        sem=("parallel", "arbitrary"), job=job)


def _my_pos():
    return lax.axis_index("x"), lax.axis_index("y"), lax.axis_index("c")


def _all_gather(xs, *, name, columns=True):
    return _gather_forward(_run_job(_gather_job(xs, columns), name=name), name=name + "_forward")


def _remote(src, dst, send_sems, recv_sems, k, dev):
    return pltpu.make_async_remote_copy(src_ref=src, dst_ref=dst, send_sem=send_sems.at[k], recv_sem=recv_sems.at[k],
                                        device_id=dev, device_id_type=MESH)


def _block(ref, idx):
    if len(ref.shape) == 2:
        return ref.at[:, pl.ds(pl.multiple_of(idx * LANES, LANES), LANES)]
    return ref.at[idx]


def _gather_job(xs, columns=True):
    n = len(xs)

    def make(x_refs, out_refs, send_sems, recv_sems, local_sems):
        mx, my, mc = _my_pos()
        mine = 4 * mx + 2 * my + mc
        peers = [(mx, my, 1 - mc), (1 - mx, my, mc), (mx, 1 - my, mc), (1 - mx, 1 - my, mc)]
        sends, recvs, local = [], [], []
        for a in range(n):
            local.append(pltpu.make_async_copy(x_refs[a], _block(out_refs[a], mine), local_sems.at[a]))
            for k, dev in enumerate(peers):
                theirs = 4 * dev[0] + 2 * dev[1] + dev[2]
                sends.append(_remote(x_refs[a], _block(out_refs[a], mine), send_sems, recv_sems, 4 * a + k, dev))
                recvs.append(_remote(x_refs[a], _block(out_refs[a], theirs), send_sems, recv_sems, 4 * a + k, dev))
        return sends, recvs, local

    def gathered(x):
        if columns and x.ndim == 2 and x.shape[1] == LANES:
            return jax.ShapeDtypeStruct((x.shape[0], N_DEV * LANES), x.dtype)
        return jax.ShapeDtypeStruct((N_DEV,) + x.shape, x.dtype)

    return _copies_job(xs, [gathered(x) for x in xs], 4 * n, n, make)


def _forward_job(gs):
    n = len(gs)

    def make(in_refs, out_refs, send_sems, recv_sems, local_sems):
        mx, my, mc = _my_pos()
        chips = [(1 - mx, my), (mx, 1 - my), (1 - mx, 1 - my)]
        sends, recvs = [], []
        for a in range(n):
            for j, (cx, cy) in enumerate(chips):
                here = _block(out_refs[a], 4 * cx + 2 * cy + mc)
                there = _block(out_refs[a], 4 * cx + 2 * cy + 1 - mc)
                sends.append(_remote(here, here, send_sems, recv_sems, 3 * a + j, (mx, my, 1 - mc)))
                recvs.append(_remote(here, there, send_sems, recv_sems, 3 * a + j, (mx, my, 1 - mc)))
        return sends, recvs, []

    shapes = [jax.ShapeDtypeStruct(g.shape, g.dtype) for g in gs]
    return _copies_job(gs, shapes, 3 * n, 0, make, in_place=True)


def _gather_forward(gs, *, name):
    return _run_job(_forward_job(gs), name=name)


def _pair_job(xs):
    n = len(xs)

    def make(x_refs, out_refs, send_sems, recv_sems, local_sems):
        mx, my, mc = _my_pos()

        def src(ref, g):
            return _block(ref, 2 * g + 1 - mc) if len(ref.shape) == 2 else ref.at[g, 1 - mc]

        copies = [_remote(src(x_refs[a], g), out_refs[a].at[g], send_sems, recv_sems, 4 * a + g, (mx, my, 1 - mc))
                  for a in range(n) for g in range(4)]
        return copies, copies, []

    shapes = [jax.ShapeDtypeStruct((4, x.shape[0], LANES) if x.ndim == 2 else (4,) + x.shape[2:], x.dtype)
              for x in xs]
    return _copies_job(xs, shapes, 4 * n, 0, make)


def _pair_add(x, r, core, *, name):
    _, a, b = r.shape
    ta = _row_tile(a, 512)

    def body(c_ref, x_ref, r_ref, o_ref):
        o_ref[...] = (x_ref[...] + r_ref[...]).astype(o_ref.dtype)

    blk = pl.BlockSpec((None, ta, b), lambda g, i, c_ref: (g, i, 0))
    own = (pl.BlockSpec((ta, b), lambda g, i, c_ref: (i, 2 * g + c_ref[0])) if x.ndim == 2
           else pl.BlockSpec((None, None, ta, b), lambda g, i, c_ref: (g, c_ref[0], i, 0)))
    return pl.pallas_call(
        body, name=name,
        grid_spec=pltpu.PrefetchScalarGridSpec(
            num_scalar_prefetch=1, grid=(4, a // ta), in_specs=[own, blk], out_specs=blk),
        out_shape=jax.ShapeDtypeStruct((4, a, b), BF16),
        compiler_params=_cparams(("parallel", "parallel")),
    )(core, x, r)


def _quad_job(xs):
    n = len(xs)

    def make(x_refs, out_refs, send_sems, recv_sems, local_sems):
        mx, my, mc = _my_pos()
        mine = 2 * mx + my
        peers = [((1 - mx, my, mc), 2 * (1 - mx) + my), ((mx, 1 - my, mc), 2 * mx + 1 - my),
                 ((1 - mx, 1 - my, mc), 2 * (1 - mx) + 1 - my)]
        sends, recvs, local = [], [], []
        for a in range(n):
            local.append(pltpu.make_async_copy(x_refs[a].at[mine], out_refs[a].at[mine], local_sems.at[a]))
            for k, (dev, g) in enumerate(peers):
                sends.append(_remote(x_refs[a].at[g], out_refs[a].at[mine], send_sems, recv_sems, 3 * a + k, dev))
                recvs.append(_remote(x_refs[a].at[g], out_refs[a].at[g], send_sems, recv_sems, 3 * a + k, dev))
        return sends, recvs, local

    shapes = [jax.ShapeDtypeStruct(x.shape, x.dtype) for x in xs]
    return _copies_job(xs, shapes, 3 * n, n, make)


def _row_tile(r, pref):
    t = min(pref, r)
    while r % t or (t % 8 and t != r):
        t -= 1
    return t


def _adamw(parts, w, m, v, layer, *, name, tile=512, into=None):
    g, a, b = parts.shape
    tile = _row_tile(a, tile)
    c1 = 1.0 / (1.0 - ADAM_B1 ** ADAM_STEP)
    c2 = 1.0 / (1.0 - ADAM_B2 ** ADAM_STEP)
    into = tuple(into or ())

    def body(p_ref, w_ref, m_ref, v_ref, *refs):
        g_ref, d_ref, mo_ref, vo_ref = refs[len(into):]
        grad = p_ref[0].astype(F32)
        for j in range(1, g):
            grad = grad + p_ref[j].astype(F32)
        mn = ADAM_B1 * m_ref[...] + (1.0 - ADAM_B1) * grad
        vn = ADAM_B2 * v_ref[...] + (1.0 - ADAM_B2) * (grad * grad)
        g_ref[...] = grad
        mo_ref[...] = mn
        vo_ref[...] = vn
        d_ref[...] = -ADAM_LR * ((mn * c1) / (jnp.sqrt(vn * c2) + ADAM_EPS) + ADAM_WD * w_ref[...])

    if layer is None:
        src, shape = pl.BlockSpec((tile, b), lambda i: (i, 0)), (a, b)
    else:
        src, shape = pl.BlockSpec((None, tile, b), lambda i: (layer, i, 0)), w.shape
    return pl.pallas_call(
        body, name=name, grid=(a // tile,),
        in_specs=[pl.BlockSpec((g, tile, b), lambda i: (0, i, 0)), src, src, src] + [_ANY] * len(into),
        out_specs=[src] * 4,
        out_shape=[jax.ShapeDtypeStruct(shape, F32)] * 4,
        input_output_aliases={4 + i: i for i in range(len(into))},
        compiler_params=_cparams(("parallel",)),
    )(parts, w, m, v, *into)


W_IN_SHARD = 276


def _w_in_dest(col):
    return jnp.where(col < P_KR, col, jnp.where(col < P_KR + 256, col + (P_CKV - P_KR), col - 2176 + P_KR + HEAD))


PLACE_TILE = 384
PLACE_SHARDS = 3
PICK_TILE = 128
PICK_TILES = 4


def _w_in_tables():
    col = np.arange(N_DEV * W_IN_SHARD)
    dest = np.where(col < P_KR, col, np.where(col < P_KR + 256, col + (P_CKV - P_KR), col - 2176 + P_KR + HEAD))
    shard = col // W_IN_SHARD

    def filled(used, universe, n):
        used = sorted(set(int(u) for u in used))
        assert len(used) <= n, used
        return used + [u for u in universe if u not in used][:n - len(used)]

    place = [filled(shard[dest // PLACE_TILE == c], range(N_DEV), PLACE_SHARDS) for c in range(P_COLS // PLACE_TILE)]
    pick = [filled(dest[shard == j] // PICK_TILE, range(P_COLS // PICK_TILE), PICK_TILES) for j in range(N_DEV)]
    return np.asarray(place, np.int32).reshape(-1), np.asarray(pick, np.int32).reshape(-1)


def _place_w_in(g, *, name):
    _, d, sh = g.shape
    tc, ns = PLACE_TILE, PLACE_SHARDS
    table = jnp.asarray(_w_in_tables()[0])

    def body(tab_ref, g_ref, o_ref, acc_ref):
        ct, s = pl.program_id(0), pl.program_id(1)
        j = tab_ref[ct * ns + s]

        @pl.when(s == 0)
        def _():
            acc_ref[...] = jnp.zeros_like(acc_ref)

        src = j * sh + lax.broadcasted_iota(jnp.int32, (sh, tc), 0)
        dst = ct * tc + lax.broadcasted_iota(jnp.int32, (sh, tc), 1)
        place = (_w_in_dest(src) == dst).astype(BF16)
        acc_ref[...] += jnp.dot(g_ref[...], place, preferred_element_type=F32)

        @pl.when(s == ns - 1)
        def _():
            o_ref[...] = acc_ref[...].astype(o_ref.dtype)

    return pl.pallas_call(
        body, name=name,
        grid_spec=pltpu.PrefetchScalarGridSpec(
            num_scalar_prefetch=1, grid=(P_COLS // tc, ns),
            in_specs=[pl.BlockSpec((None, d, sh), lambda ct, s, tab: (tab[ct * ns + s], 0, 0))],
            out_specs=pl.BlockSpec((d, tc), lambda ct, s, tab: (0, ct)),
            scratch_shapes=[pltpu.VMEM((d, tc), F32)]),
        out_shape=jax.ShapeDtypeStruct((d, P_COLS), BF16),
        compiler_params=_cparams(("parallel", "arbitrary")),
    )(table, g)


def _unplace_w_in(dw, *, name):
    d = dw.shape[0]
    sh, tk, nt = W_IN_SHARD, PICK_TILE, PICK_TILES
    table = jnp.asarray(_w_in_tables()[1])

    def body(tab_ref, dw_ref, o_ref):
        j, kk = pl.program_id(0), pl.program_id(1)
        tile = tab_ref[j * nt + kk]
        src = j * sh + lax.broadcasted_iota(jnp.int32, (tk, sh), 1)
        dst = tile * tk + lax.broadcasted_iota(jnp.int32, (tk, sh), 0)
        pick = (_w_in_dest(src) == dst).astype(BF16)
        part = _split_dot(dw_ref[...], pick)

        @pl.when(kk == 0)
        def _():
            o_ref[...] = part

        @pl.when(kk > 0)
        def _():
            o_ref[...] += part

    return pl.pallas_call(
        body, name=name,
        grid_spec=pltpu.PrefetchScalarGridSpec(
            num_scalar_prefetch=1, grid=(N_DEV, nt),
            in_specs=[pl.BlockSpec((d, tk), lambda j, kk, tab: (0, tab[j * nt + kk]))],
            out_specs=pl.BlockSpec((None, d, sh), lambda j, kk, tab: (j, 0, 0))),
        out_shape=jax.ShapeDtypeStruct((N_DEV, d, sh), F32),
        compiler_params=_cparams(("parallel", "arbitrary")),
    )(table, dw)


def _gate_up_swiglu(h1, wgu, *, name):
    t_rows, k = h1.shape
    w = wgu.shape[2]
    tm = _tile(t_rows, 1024)

    def body(a_ref, wg_ref, wu_ref, gu_ref, act_ref):
        a = a_ref[...].astype(BF16)
        gate = jnp.dot(a, wg_ref[...], preferred_element_type=F32)
        up = jnp.dot(a, wu_ref[...], preferred_element_type=F32)
        gu_ref[0] = gate.astype(gu_ref.dtype)
        gu_ref[1] = up.astype(gu_ref.dtype)
        act_ref[...] = (gate * _sigmoid(gate) * up).astype(act_ref.dtype)

    return pl.pallas_call(
        body, name=name, grid=(t_rows // tm, 4),
        in_specs=[pl.BlockSpec((tm, k), lambda i, j: (i, 0)),
                  pl.BlockSpec((None, k, w), lambda i, j: (j, 0, 0)),
                  pl.BlockSpec((None, k, w), lambda i, j: (j + 4, 0, 0))],
        out_specs=[pl.BlockSpec((2, None, tm, w), lambda i, j: (0, j, i, 0)),
                   pl.BlockSpec((None, tm, w), lambda i, j: (j, i, 0))],
        out_shape=[jax.ShapeDtypeStruct((2, 4, t_rows, w), BF16), jax.ShapeDtypeStruct((4, t_rows, w), BF16)],
        compiler_params=_cparams(("parallel", "arbitrary")),
    )(h1, wgu, wgu)


def _down_dx_swiglu(dffn, wdown, gu, *, name):
    t_rows, k = dffn.shape
    w = gu.shape[3]
    tm = _tile(t_rows, 1024)

    def body(d_ref, w_ref, gu_ref, o_ref):
        dact = lax.dot_general(d_ref[...].astype(BF16), w_ref[...], _NT, preferred_element_type=F32)
        gate, up = gu_ref[0].astype(F32), gu_ref[1].astype(F32)
        sg = _sigmoid(gate)
        silu = gate * sg
        o_ref[0] = (dact * up * (sg + silu - silu * sg)).astype(o_ref.dtype)
        o_ref[1] = (dact * silu).astype(o_ref.dtype)

    blk = pl.BlockSpec((2, None, tm, w), lambda i, j: (0, j, i, 0))
    return pl.pallas_call(
        body, name=name, grid=(t_rows // tm, 4),
        in_specs=[pl.BlockSpec((tm, k), lambda i, j: (i, 0)), pl.BlockSpec((w, k), lambda i, j: (j, 0)), blk],
        out_specs=blk, out_shape=jax.ShapeDtypeStruct(gu.shape, BF16),
        compiler_params=_cparams(("parallel", "arbitrary")),
    )(dffn, wdown, gu)


BIG = ("w_in", "mla_w_uq", "mla_w_ukv", "w_out", "w_gate_up", "w_down", "ple_w_gate", "ple_w_proj")
SMALL = ("ln_in_g", "ln_in_b", "hgrn_lb_logits", "hgrn_norm_g", "sgu_ln_g", "sgu_ln_b", "sgu_w_s", "sgu_b_s",
         "mla_q_norm_g", "mla_kv_norm_g", "ln1_g", "ln1_b", "ln2_g", "ln2_b")
ORDER = ("ln_in_g", "ln_in_b", "w_in", "hgrn_lb_logits", "hgrn_norm_g", "sgu_ln_g", "sgu_ln_b", "sgu_w_s", "sgu_b_s",
         "mla_q_norm_g", "mla_w_uq", "mla_kv_norm_g", "mla_w_ukv", "w_out", "ln1_g", "ln1_b", "w_gate_up", "w_down",
         "ple_w_gate", "ple_w_proj", "ln2_g", "ln2_b")


def _slab(a, align):
    s = a.reshape(-1, LANES)
    pad = -s.shape[0] % align
    return jnp.pad(s, ((0, pad), (0, 0))) if pad else s


def _pack(arrays, align=16, total_align=512):
    s = jnp.concatenate([_slab(a, align) for a in arrays], axis=0)
    pad = -s.shape[0] % total_align
    return jnp.pad(s, ((0, pad), (0, 0))) if pad else s


def _unpack(slab, shapes, align=16):
    out, r0 = [], 0
    for s in shapes:
        nr = math.prod(s) // LANES
        out.append(slab[r0:r0 + nr].reshape(s))
        r0 += nr + (-nr % align)
    return out


def _weight_shards(w, li):
    uq_pad = ((0, 0), (0, LANES - ATT_D))
    shards = {k: w[k][li] for k in BIG}
    shards["mla_w_uq"] = jnp.pad(shards["mla_w_uq"], uq_pad)
    return {k: s.astype(BF16) for k, s in shards.items()}


def _usable_weights(g, *, name):
    out = {}
    for k, a in g.items():
        if k == "w_in":
            out[k] = _place_w_in(a, name=name + "_place_w_in")
        elif k in ("w_out", "w_down", "ple_w_gate"):
            out[k] = a.reshape(a.shape[0] * a.shape[1], a.shape[2])
        else:
            out[k] = a
    return out


BY_COLUMNS = ("mla_w_uq", "mla_w_ukv", "ple_w_proj")


def _as_pairs(k, g):
    if k in BY_COLUMNS:
        return g
    if g.ndim == 2:
        return g.reshape((4, 2, g.shape[0] // N_DEV) + g.shape[1:])
    return g.reshape((4, 2) + g.shape[1:])


def _twice(fn):
    return lambda *a: fn(*a) * 2


def _layer_forward(li, h, hb, p_i, wts, sm, lbs, tables, alpha, hgrn_job=None, after_hgrn=None, attn_job=None,
                   after_attn=None, loss_target=None):
    n = f"l{li}_"
    row1 = lambda a: a.reshape(1, -1)
    projp = _mm(hb, wts["w_in"], name=n + "proj")
    ng = row1(sm["hgrn_norm_g"][li])
    res = _hgrn_fwd(projp, lbs[li], ng, name=n + "hgrn_fwd", job=hgrn_job)
    if hgrn_job is not None:
        res, got = res
    o_a, o_pre, states = res
    lg, lbias = row1(sm["sgu_ln_g"][li]), row1(sm["sgu_ln_b"][li])
    w_s = sm["sgu_w_s"][li]
    bias_full = jnp.repeat(sm["sgu_b_s"][li].T, HEAD, axis=1)
    o_b = _sgu_fwd(projp, lg, lbias, w_s, bias_full, name=n + "sgu_fwd",
                   job=None if hgrn_job is None else _forward_job(got))
    if hgrn_job is not None:
        o_b, got = o_b
        wts = dict(wts, **after_hgrn(got))
    qg, kvg = row1(sm["mla_q_norm_g"][li]), row1(sm["mla_kv_norm_g"][li])
    cq_view, ckv_view = (projp, 384, P_CQ // 384), (projp, 256, P_CKV // 256)
    (cqn,) = _rowwise(_fn_rms, [cq_view], [qg], [(384, BF16)], tile=2048, name=n + "q_norm")
    (ckvn,) = _rowwise(_fn_rms, [ckv_view], [kvg], [(256, BF16)], tile=2048, name=n + "kv_norm")
    q = _mm(cqn, wts["mla_w_uq"], name=n + "uq")
    kv = _mm(ckvn, wts["mla_w_ukv"], name=n + "ukv")
    qr, kf, kvb = _mla_prep(q, kv, projp, tables, name=n + "mla_prep")
    res = _attn_fwd(qr, kf, kvb, name=n + "attn_fwd", job=attn_job)
    if attn_job is not None:
        res, got = res
    o_c, lse = res
    cat = jnp.concatenate([o_a, o_b, o_c.astype(BF16)], axis=1)
    mix = _mm(cat, wts["w_out"], name=n + "out_proj", job=None if attn_job is None else _forward_job(got))
    if attn_job is not None:
        mix, got = mix
        wts = dict(wts, **after_attn(got))
    g1, b1 = row1(sm["ln1_g"][li]), row1(sm["ln1_b"][li])
    d = h.shape[1]
    h1, h1b = _rowwise(_twice(_make_post_mix(alpha)), [h, mix], [g1, b1], [(d, F32), (d, BF16)], tile=1024,
                       name=n + "ln1")
    gu, act = _gate_up_swiglu(h1b, wts["w_gate_up"], name=n + "gate_up")
    ffn = _mm_kblocks(act, wts["w_down"], bm="kn", tm=1024, name=n + "down")
    pg = _mm(h1b, wts["ple_w_gate"], name=n + "ple_gate")
    pp = _mm(p_i, wts["ple_w_proj"], name=n + "ple_proj")
    g2, b2 = row1(sm["ln2_g"][li]), row1(sm["ln2_b"][li])
    if loss_target is None:
        out = _rowwise(_twice(_make_ple_ln(alpha)), [h1, ffn, pg, pp], [g2, b2], [(d, F32), (d, BF16)],
                       name=n + "ln2")
    else:
        def ln_and_loss(h1v, ffnv, pgv, ppv, tv, gv, bv):
            err = _make_ple_ln(alpha)(h1v, ffnv, pgv, ppv, gv, bv)[0] - tv
            return err * (1.0 / d), 0.5 * jnp.sum(jnp.mean(err * err, axis=-1, keepdims=True), axis=0, keepdims=True)

        out = _rowwise(ln_and_loss, [h1, ffn, pg, pp, loss_target], [g2, b2], [(d, F32)], accs=[(1, 1)],
                       name=n + "ln2_loss")
    saved = dict(h=h, hb=hb, h1b=h1b, projp=projp, o_pre=o_pre, states=states, cqn=cqn, ckvn=ckvn, qr=qr, kf=kf, kvb=kvb, o_c=o_c,
                 lse=lse, cat=cat, mix=mix, h1=h1, gu=gu, act=act, ffn=ffn, pg=pg, pp=pp, ng=ng, lg=lg, wts=wts,
                 lbias=lbias, w_s=w_s, bias_full=bias_full, qg=qg, kvg=kvg, g1=g1, b1=b1, g2=g2, b2=b2)
    return tuple(out), saved


RS_EARLY = ("ple_w_proj", "ple_w_gate", "w_down", "w_gate_up", "w_out")
RS_LATE = ("mla_w_uq", "mla_w_ukv", "w_in")


def _layer_backward(li, dh2_parts, p_i, sv, lbs, tables, alpha, core, carried=None):
    n = f"l{li}_b_"
    wts = sv["wts"]
    gr = {}
    dh1_a, dffn, dpg, dpp, gr["ln2_g"], gr["ln2_b"] = _rowwise_vjp(
        _make_ple_ln(alpha), [sv["h1"], sv["ffn"], sv["pg"], sv["pp"]], [sv["g2"], sv["b2"]], [dh2_parts],
        groups=[[0], [1], [2], [3]], gdtypes=[F32, BF16, BF16, BF16], name=n + "ln2")
    big = {}
    big["ple_w_proj"] = _mm(p_i, dpp, am="km", tk=2048, name=n + "ple_proj_dw")
    big["ple_w_gate"] = _mm(sv["h1b"], dpg, am="km", name=n + "ple_gate_dw")
    dh1_b = _mm(dpg, wts["ple_w_gate"], bm="nk", name=n + "ple_gate_dx")
    big["w_down"] = _mm(sv["act"], dffn, am="bkm", tk=4096, name=n + "down_dw")
    dgu = _down_dx_swiglu(dffn, wts["w_down"], sv["gu"], name=n + "down_dx")
    dgu = dgu.reshape((N_DEV,) + dgu.shape[2:])
    big["w_gate_up"], carried_got = _mm(sv["h1b"], dgu, am="km", bm="bkn", om="bmn", tk=4096, name=n + "gate_up_dw",
                                        job=carried), None
    if carried is not None:
        big["w_gate_up"], carried_got = big["w_gate_up"]
    early = [_as_pairs(k, big[k]) for k in RS_EARLY[:-1]]
    dh1_c, theirs = _mm_kblocks(dgu, wts["w_gate_up"], bm="bnk", tm=512, name=n + "gate_up_dx",
                                job=_pair_job(early))
    dh_a, dmix, gr["ln1_g"], gr["ln1_b"] = _rowwise_vjp(
        _make_post_mix(alpha), [sv["h"], sv["mix"]], [sv["g1"], sv["b1"]], [[dh1_a, dh1_b, dh1_c]],
        groups=[[0], [1]], gdtypes=[F32, BF16], name=n + "ln1")
    big["w_out"] = _mm(sv["cat"], dmix, am="km", name=n + "out_proj_dw")
    early.append(_as_pairs("w_out", big["w_out"]))
    dcat, their_w_out = _mm(dmix, wts["w_out"], bm="nk", name=n + "out_proj_dx", job=_pair_job(early[-1:]))
    sums = [_pair_add(x, r, core, name=n + "pair_add_" + k)
            for k, x, r in zip(RS_EARLY, early, list(theirs) + list(their_w_out))]

    (dqr, dkv, dkf), early_quads = _attn_bwd(sv["qr"], sv["kf"], sv["kvb"], dcat, sv["o_c"], sv["lse"],
                                             name=n + "attn", job=_quad_job(sums))
    dqpad, dkr = _mla_prep_bwd(dqr, dkf, tables, name=n + "mla_prep")
    big["mla_w_uq"] = _mm(sv["cqn"], dqpad, am="km", tk=2048, name=n + "uq_dw")
    dcqn = _mm(dqpad, wts["mla_w_uq"], bm="nk", name=n + "uq_dx")
    big["mla_w_ukv"] = _mm(sv["ckvn"], dkv, am="km", tk=2048, name=n + "ukv_dw")
    dckvn = _mm(dkv, wts["mla_w_ukv"], bm="nk", name=n + "ukv_dx")
    projp = sv["projp"]
    dcq, gr["mla_q_norm_g"] = _rowwise_vjp(_fn_rms, [(projp, 384, P_CQ // 384)], [sv["qg"]], [[dcqn]],
                                           groups=[[0]], gdtypes=[BF16], tile=2048, name=n + "q_norm")
    dckv, gr["mla_kv_norm_g"] = _rowwise_vjp(_fn_rms, [(projp, 256, P_CKV // 256)], [sv["kvg"]], [[dckvn]],
                                             groups=[[0]], gdtypes=[BF16], tile=2048, name=n + "kv_norm")
    dsgu, gr["sgu_ln_g"], gr["sgu_ln_b"], gr["sgu_w_s"], gr["sgu_b_s"] = _sgu_bwd(
        projp, sv["lg"], sv["lbias"], sv["w_s"], sv["bias_full"], dcat, name=n + "sgu")
    dhg, gr["hgrn_norm_g"], gr["lower_bound"] = _hgrn_bwd(
        projp, lbs[li], sv["ng"], sv["o_pre"], sv["states"], dcat, name=n + "hgrn")
    dprojp = jnp.concatenate([dhg, dsgu, dcq, dkr, dckv], axis=1)
    big["w_in"] = _unplace_w_in(_mm(sv["hb"], dprojp, am="km", tk=4096, name=n + "proj_dw"),
                                name=n + "proj_dw_shards")
    late = [_as_pairs(k, big[k]) for k in RS_LATE]
    dh_b, theirs = _mm(dprojp, wts["w_in"], bm="nk", tm=1024, tk=P_COLS, name=n + "proj_dx", job=_pair_job(late))
    late_sums = [_pair_add(x, r, core, name=n + "pair_add_" + k) for k, x, r in zip(RS_LATE, late, theirs)]
    return [dh_a, dh_b], gr, early_quads, late_sums, carried_got


def kernel(x, p, positions, ln_in_g, ln_in_b, w_in, hgrn_lb_logits, hgrn_norm_g, sgu_ln_g, sgu_ln_b, sgu_w_s, sgu_b_s, mla_q_norm_g, mla_w_uq, mla_kv_norm_g, mla_w_ukv, w_out, ln1_g, ln1_b, w_gate_up, w_down, ple_w_gate, ple_w_proj, ln2_g, ln2_b, loss_target, m_ln_in_g, m_ln_in_b, m_w_in, m_hgrn_lb_logits, m_hgrn_norm_g, m_sgu_ln_g, m_sgu_ln_b, m_sgu_w_s, m_sgu_b_s, m_mla_q_norm_g, m_mla_w_uq, m_mla_kv_norm_g, m_mla_w_ukv, m_w_out, m_ln1_g, m_ln1_b, m_w_gate_up, m_w_down, m_ple_w_gate, m_ple_w_proj, m_ln2_g, m_ln2_b, v_ln_in_g, v_ln_in_b, v_w_in, v_hgrn_lb_logits, v_hgrn_norm_g, v_sgu_ln_g, v_sgu_ln_b, v_sgu_w_s, v_sgu_b_s, v_mla_q_norm_g, v_mla_w_uq, v_mla_kv_norm_g, v_mla_w_ukv, v_w_out, v_ln1_g, v_ln1_b, v_w_gate_up, v_w_down, v_ple_w_gate, v_ple_w_proj, v_ln2_g, v_ln2_b):
    args = dict(locals())
    w = {k: args[k] for k in ORDER}
    m = {k: args["m_" + k] for k in ORDER}
    v = {k: args["v_" + k] for k in ORDER}
    depth = w_in.shape[0]
    assert depth == 2, "the lower-bound kernel is written for two layers"
    alpha = (2 * depth) ** 0.25
    xs, tgt = x[0], loss_target[0]
    d_model = xs.shape[1]

    shards = [_weight_shards(w, li) for li in range(depth)]
    on_hgrn0 = ("mla_w_uq", "mla_w_ukv", "w_out", "ple_w_gate", "ple_w_proj")
    ffn0 = ("w_gate_up", "w_down")
    first1 = ("w_in", "mla_w_uq", "mla_w_ukv", "w_out")
    on_attn1 = ("w_gate_up", "w_down", "ple_w_gate", "ple_w_proj")
    layer1_first = {}

    def after_hgrn0(got):
        return _usable_weights(dict(zip(on_hgrn0, got)), name="l0")

    def after_attn0(got):
        layer1_first.update(_usable_weights(dict(zip(first1, got[len(ffn0):])), name="l1"))
        return _usable_weights(dict(zip(ffn0, got[:len(ffn0)])), name="l0")

    def after_attn1(got):
        return _usable_weights(dict(zip(on_attn1, got)), name="l1")

    tables = _rope_tables(positions[0])
    row1 = lambda a: a.reshape(1, -1)
    l0, l1 = row1(hgrn_lb_logits[0]), row1(hgrn_lb_logits[1])
    lbs = _rowwise(_fn_lower_bounds, [l0, l1], [], [(HG_W, F32), (HG_W, F32)], name="lower_bounds")

    gin, bin_ = row1(ln_in_g), row1(ln_in_b)
    (h, hb), g_in = _rowwise(_twice(_fn_ln), [xs], [gin, bin_], [(d_model, F32), (d_model, BF16)], name="ln_in",
                             job=_gather_job([shards[0]["w_in"]]))
    w_in0 = _usable_weights({"w_in": _gather_forward(g_in, name="gather_l0_w_in_forward")[0]}, name="l0")
    (h, hb), sv0 = _layer_forward(
        0, h, hb, p[0, 0], w_in0, w, lbs, tables, alpha,
        hgrn_job=_gather_job([shards[0][k] for k in on_hgrn0]), after_hgrn=after_hgrn0,
        attn_job=_gather_job([shards[0][k] for k in ffn0] + [shards[1][k] for k in first1]), after_attn=after_attn0)
    (dy, loss_local), sv1 = _layer_forward(
        1, h, hb, p[1, 0], layer1_first, w, lbs, tables, alpha,
        attn_job=_gather_job([shards[1][k] for k in on_attn1]), after_attn=after_attn1, loss_target=tgt)
    saved = [sv0, sv1]
    loss = lax.psum(loss_local[0, 0], ("x", "y", "c"))

    core = lax.axis_index("c").astype(jnp.int32).reshape(1)
    dparts, grads, quads, carried = [dy], [None] * depth, [None] * depth, None
    for li in reversed(range(depth)):
        dparts, grads[li], early_quads, late_sums, late_quads = _layer_backward(
            li, dparts, p[li, 0], saved[li], lbs, tables, alpha, core, carried=carried)
        quads[li] = dict(zip(RS_EARLY, early_quads))
        if carried is not None:
            quads[li + 1].update(zip(RS_LATE, late_quads))
        carried = _quad_job(late_sums)
    (dx, d_gin, d_bin), late_quads = _rowwise_vjp(_fn_ln, [xs], [gin, bin_], [dparts], groups=[[0]], name="ln_in_b",
                                                   job=carried)
    quads[0].update(zip(RS_LATE, late_quads))
    dl0, dl1 = _rowwise_vjp(_fn_lower_bounds, [l0, l1], [], [[grads[0]["lower_bound"]], [grads[1]["lower_bound"]]],
                            groups=[[0], [1]], name="lower_bounds_b")

    prefixes = ("grad_", "delta_", "new_m_", "new_v_")
    uq_pad = ((0, 0), (0, 0), (0, LANES - ATT_D))
    state = {k: ((jnp.pad(w[k], uq_pad), jnp.pad(m[k], uq_pad), jnp.pad(v[k], uq_pad)) if k == "mla_w_uq"
                 else (w[k], m[k], v[k])) for k in BIG}
    out = {}
    for k in BIG:
        res4 = None
        for li in range(depth):
            res4 = _adamw(quads[li][k], *state[k], li, name=f"adamw_l{li}_{k}", into=res4)
        for pre, a in zip(prefixes, res4):
            out[pre + k] = a[:, :, :ATT_D] if k == "mla_w_uq" else a

    small_g = {"ln_in_g": d_gin.reshape(-1), "ln_in_b": d_bin.reshape(-1),
               "hgrn_lb_logits": jnp.stack([dl0.reshape(-1), dl1.reshape(-1)])}
    for k in SMALL[3:]:
        small_g[k] = jnp.stack([grads[li][k].reshape(w[k].shape[1:]) for li in range(depth)])
    (small_parts,) = _all_gather([_pack([small_g[k] for k in SMALL])], name="gather_small_grads", columns=False)
    slabs = _adamw(small_parts, _pack([w[k] for k in SMALL]), _pack([m[k] for k in SMALL]),
                   _pack([v[k] for k in SMALL]), None, name="adamw_small")
    shapes = [w[k].shape for k in SMALL]
    for pre, slab in zip(prefixes, slabs):
        for k, a in zip(SMALL, _unpack(slab, shapes)):
            out[pre + k] = a
    res = [loss, dx[None]]
    for prefix in ("grad_", "delta_", "new_m_", "new_v_"):
        res += [out[prefix + k] for k in ORDER]
    return tuple(res)
```

```python
import functools
import math

import jax
import jax.numpy as jnp
import numpy as np
from jax import lax
from jax.experimental import pallas as pl
from jax.experimental.pallas import tpu as pltpu

F32 = jnp.float32
BF16 = jnp.bfloat16
MESH = pl.DeviceIdType.MESH

LN_EPS = 1e-5
RMS_EPS = 1e-6
ROPE_THETA = 10000.0
ADAM_LR, ADAM_B1, ADAM_B2, ADAM_EPS, ADAM_WD, ADAM_STEP = 0.001, 0.9, 0.999, 1e-08, 0.01, 10

N_DEV = 8
LANES = 128
HG_CHUNK = 16
HG_W = 256
HEAD = 64
SGU_CHUNK = 128
SGU_STEP_CHUNKS = 4
N_ATT_HEADS = 8
ATT_D = 96
VMEM_LIMIT = 56 * 1024 * 1024

HG_TILE = 256
ATT_TQ = 512
ROW_TILE = 512

P_CQ, P_KR, P_CKV, P_COLS = 1536, 1920, 2048, 2304


def _cparams(sem):
    return pltpu.CompilerParams(dimension_semantics=sem, vmem_limit_bytes=VMEM_LIMIT)


_ANY = pl.BlockSpec(memory_space=pl.ANY)


def _call(body, operands, *, name, grid, in_specs, out_specs, out_shape, sem, scratch_shapes=(), job=None):
    if job is None:
        return pl.pallas_call(body, name=name, grid=grid, in_specs=in_specs, out_specs=out_specs, out_shape=out_shape,
                              scratch_shapes=list(scratch_shapes), compiler_params=_cparams(sem))(*operands)
    single = not isinstance(out_shape, (list, tuple))
    shapes = [out_shape] if single else list(out_shape)
    ospecs = [out_specs] if single else list(out_specs)
    ni, no, ns = len(operands), len(shapes), len(scratch_shapes)
    ji, jo = len(job.inputs), len(job.out_shapes)

    def hosted(*refs):
        p = 0
        parts = []
        for cnt in (ni, ji, no, jo, ns):
            parts.append(refs[p:p + cnt])
            p += cnt
        ins, jins, outs, jouts, scr = parts
        jsems = refs[p:]
        ids = [pl.program_id(a) for a in range(len(grid))]
        first = functools.reduce(lambda a, b: a & b, [i == 0 for i in ids])
        last = functools.reduce(lambda a, b: a & b, [i == g - 1 for i, g in zip(ids, grid)])

        @pl.when(first)
        def _():
            job.start(jins, jouts, jsems)

        body(*ins, *outs, *scr)

        @pl.when(last)
        def _():
            job.finish(jins, jouts, jsems)

    res = pl.pallas_call(
        hosted, name=name, grid=grid,
        in_specs=list(in_specs) + [_ANY] * ji, out_specs=ospecs + [_ANY] * jo,
        out_shape=shapes + list(job.out_shapes),
        scratch_shapes=list(scratch_shapes) + [pltpu.SemaphoreType.DMA((c,)) for c in job.sem_counts],
        input_output_aliases=job.aliases(ni, no),
        compiler_params=_cparams(("arbitrary",) * len(grid)),
    )(*operands, *job.inputs)
    own = res[0] if single else res[:no]
    return own, res[no:]


class _Job:
    def __init__(self, inputs, out_shapes, sem_counts, start, finish, in_place=False):
        self.inputs, self.out_shapes, self.sem_counts = list(inputs), list(out_shapes), list(sem_counts)
        self.start, self.finish, self.in_place = start, finish, in_place

    def aliases(self, first_in, first_out):
        return {first_in + i: first_out + i for i in range(len(self.inputs))} if self.in_place else {}


def _copies_job(inputs, out_shapes, n_remote, n_local, make, in_place=False):
    def start(jins, jouts, sems):
        sends, _, local = make(jins, jouts, *sems)
        for cp in local + sends:
            cp.start()

    def finish(jins, jouts, sems):
        sends, recvs, local = make(jins, jouts, *sems)
        for cp in recvs:
            cp.wait_recv()
        for cp in sends:
            cp.wait_send()
        for cp in local:
            cp.wait()

    return _Job(inputs, out_shapes, [n_remote, n_remote, max(n_local, 1)], start, finish, in_place)


def _run_job(job, *, name):
    ji, jo = len(job.inputs), len(job.out_shapes)

    def body(*refs):
        jins, jouts, sems = refs[:ji], refs[ji:ji + jo], refs[ji + jo:]
        job.start(jins, jouts, sems)
        job.finish(jins, jouts, sems)

    return pl.pallas_call(
        body, name=name, out_shape=list(job.out_shapes), in_specs=[_ANY] * ji, out_specs=[_ANY] * jo,
        scratch_shapes=[pltpu.SemaphoreType.DMA((c,)) for c in job.sem_counts],
        input_output_aliases=job.aliases(0, 0),
    )(*job.inputs)


def _tile(n, pref):
    if n % pref == 0:
        return pref
    best = None
    t = LANES
    while t <= min(n, pref):
        if n % t == 0:
            best = t
        t += LANES
    return best if best is not None else n


def _mm(a, b, *, am="mk", bm="kn", om="mn", out_dtype=F32, tm=1024, tn=1024, tk=1024, name, job=None):
    if am == "mk":
        m, k = a.shape
    elif am == "km":
        k, m = a.shape
    elif am == "bmk":
        m, tk = a.shape[1], a.shape[2]
        k = a.shape[0] * tk
    else:
        k, tm = a.shape[1], a.shape[2]
        m = a.shape[0] * tm
    if bm == "kn":
        kb_, n = b.shape
    elif bm == "nk":
        n, kb_ = b.shape
    elif bm == "bkn":
        kb_, tn = b.shape[1], b.shape[2]
        n = b.shape[0] * tn
    else:
        n, tk = b.shape[1], b.shape[2]
        kb_ = b.shape[0] * tk
    assert kb_ == k, (a.shape, b.shape, am, bm)
    tm, tn, tk = _tile(m, tm), _tile(n, tn), _tile(k, tk)
    nk = k // tk
    dims = (((0 if am in ("km", "bkm") else 1,), (1 if bm in ("nk", "bnk") else 0,)), ((), ()))

    a_spec = {"mk": pl.BlockSpec((tm, tk), lambda i, j, kk: (i, kk)),
              "km": pl.BlockSpec((tk, tm), lambda i, j, kk: (kk, i)),
              "bmk": pl.BlockSpec((None, tm, tk), lambda i, j, kk: (kk, i, 0)),
              "bkm": pl.BlockSpec((None, tk, tm), lambda i, j, kk: (i, kk, 0))}[am]
    b_spec = {"kn": pl.BlockSpec((tk, tn), lambda i, j, kk: (kk, j)),
              "nk": pl.BlockSpec((tn, tk), lambda i, j, kk: (j, kk)),
              "bkn": pl.BlockSpec((None, tk, tn), lambda i, j, kk: (j, kk, 0)),
              "bnk": pl.BlockSpec((None, tn, tk), lambda i, j, kk: (kk, j, 0))}[bm]
    if om == "mn":
        o_spec, o_shape = pl.BlockSpec((tm, tn), lambda i, j, kk: (i, j)), (m, n)
    else:
        o_spec, o_shape = pl.BlockSpec((None, tm, tn), lambda i, j, kk: (j, i, 0)), (n // tn, m, tn)

    def body(a_ref, b_ref, o_ref, *acc):
        kk = pl.program_id(2)

        def prod():
            return lax.dot_general(a_ref[...].astype(BF16), b_ref[...].astype(BF16), dims, preferred_element_type=F32)

        if nk == 1:
            o_ref[...] = prod().astype(o_ref.dtype)
            return
        acc_ref, = acc

        @pl.when(kk == 0)
        def _():
            acc_ref[...] = prod()

        if nk > 2:
            @pl.when((kk > 0) & (kk < nk - 1))
            def _():
                acc_ref[...] += prod()

        @pl.when(kk == nk - 1)
        def _():
            o_ref[...] = (acc_ref[...] + prod()).astype(o_ref.dtype)

    return _call(body, (a, b), name=name, grid=(m // tm, n // tn, nk), in_specs=[a_spec, b_spec], out_specs=o_spec,
                 out_shape=jax.ShapeDtypeStruct(o_shape, out_dtype),
                 scratch_shapes=[pltpu.VMEM((tm, tn), F32)] if nk > 1 else [],
                 sem=("parallel", "parallel", "arbitrary"), job=job)


def _mm_kblocks(a, b, *, bm, tm, name, job=None):
    nkb, m, kb = a.shape
    n = b.shape[1]
    tm = _tile(m, tm)

    def body(a_ref, b_ref, o_ref):
        acc = None
        for j in range(nkb):
            if bm == "kn":
                part = jnp.dot(a_ref[j], b_ref[j * kb:(j + 1) * kb, :], preferred_element_type=F32)
            else:
                part = lax.dot_general(a_ref[j], b_ref[j], _NT, preferred_element_type=F32)
            acc = part if acc is None else acc + part
        o_ref[...] = acc

    b_spec = (pl.BlockSpec(b.shape, lambda i: (0, 0)) if bm == "kn" else pl.BlockSpec(b.shape, lambda i: (0, 0, 0)))
    return _call(body, (a, b), name=name, grid=(m // tm,),
                 in_specs=[pl.BlockSpec((nkb, tm, kb), lambda i: (0, i, 0)), b_spec],
                 out_specs=pl.BlockSpec((tm, n), lambda i: (i, 0)), out_shape=jax.ShapeDtypeStruct((m, n), F32),
                 sem=("parallel",), job=job)


def _row_operand(a, tile):
    if isinstance(a, tuple):
        arr, w, j = a
        return arr, pl.BlockSpec((tile, w), lambda i, j=j: (i, j))
    return a, pl.BlockSpec((tile, a.shape[1]), lambda i: (i, 0))


def _const_spec(c):
    nd = c.ndim
    return pl.BlockSpec(c.shape, lambda i, nd=nd: (0,) * nd)


def _rowwise(fn, rows, consts, outs, *, name, accs=(), tile=None, job=None):
    t_rows = (rows[0][0] if isinstance(rows[0], tuple) else rows[0]).shape[0]
    tile = min(tile or ROW_TILE, t_rows)
    arrs, specs = zip(*[_row_operand(a, tile) for a in rows])
    nin, no = len(rows) + len(consts), len(outs)

    def body(*refs):
        res = fn(*[r[...] for r in refs[:nin]])
        for r, v in zip(refs[nin:nin + no], res[:no]):
            r[...] = v.astype(r.dtype)
        if accs:
            a_refs = refs[nin + no:]

            @pl.when(pl.program_id(0) == 0)
            def _():
                for r in a_refs:
                    r[...] = jnp.zeros_like(r)

            for r, v in zip(a_refs, res[no:]):
                r[...] += v

    out_shape = [jax.ShapeDtypeStruct((t_rows, w), dt) for w, dt in outs]
    out_shape += [jax.ShapeDtypeStruct(s, F32) for s in accs]
    out_specs = [pl.BlockSpec((tile, w), lambda i: (i, 0)) for w, _ in outs]
    out_specs += [pl.BlockSpec(s, lambda i, nd=len(s): (0,) * nd) for s in accs]
    return _call(body, (*arrs, *consts), name=name, grid=(t_rows // tile,),
                 in_specs=list(specs) + [_const_spec(c) for c in consts],
                 out_specs=out_specs, out_shape=out_shape, sem=("arbitrary",), job=job)


def _rowwise_vjp(fn, rows, consts, cts, *, name, groups, tile=None, gdtypes=None, job=None):
    t_rows = (rows[0][0] if isinstance(rows[0], tuple) else rows[0]).shape[0]
    tile = min(tile or ROW_TILE, t_rows)
    arrs, specs = zip(*[_row_operand(a, tile) for a in rows])
    flat_cts = [c for group in cts for c in group]
    ct_arrs, ct_specs = zip(*[_row_operand(a, tile) for a in flat_cts])
    nr, nc, nct, ng = len(rows), len(consts), len(flat_cts), len(groups)

    def width(a):
        return a[1] if isinstance(a, tuple) else a.shape[1]

    def body(*refs):
        rv = [r[...].astype(F32) for r in refs[:nr]]
        cv = [r[...] for r in refs[nr:nr + nc]]
        ct_refs = refs[nr + nc:nr + nc + nct]
        ctv, pos = [], 0
        for group in cts:
            s = ct_refs[pos][...].astype(F32)
            for r in ct_refs[pos + 1:pos + len(group)]:
                s = s + r[...].astype(F32)
            ctv.append(s)
            pos += len(group)
        _, pull = jax.vjp(fn, *rv, *cv)
        grads = pull(tuple(ctv))
        g_refs = refs[nr + nc + nct:nr + nc + nct + ng]
        for r, idx in zip(g_refs, groups):
            parts = [grads[i] for i in idx]
            r[...] = (parts[0] if len(parts) == 1 else jnp.concatenate(parts, axis=1)).astype(r.dtype)
        c_refs = refs[nr + nc + nct + ng:]

        @pl.when(pl.program_id(0) == 0)
        def _():
            for r in c_refs:
                r[...] = jnp.zeros_like(r)

        for r, v in zip(c_refs, grads[nr:]):
            r[...] += v

    gw = [sum(width(rows[i]) for i in idx) for idx in groups]
    gdtypes = gdtypes or [F32] * ng
    out_shape = [jax.ShapeDtypeStruct((t_rows, w), dt) for w, dt in zip(gw, gdtypes)]
    out_shape += [jax.ShapeDtypeStruct(c.shape, F32) for c in consts]
    out_specs = [pl.BlockSpec((tile, w), lambda i: (i, 0)) for w in gw]
    out_specs += [_const_spec(c) for c in consts]
    return _call(body, (*arrs, *consts, *ct_arrs), name=name, grid=(t_rows // tile,),
                 in_specs=list(specs) + [_const_spec(c) for c in consts] + list(ct_specs),
                 out_specs=out_specs, out_shape=out_shape, sem=("arbitrary",), job=job)


def _layer_norm(x, g, b):
    mu = jnp.mean(x, axis=-1, keepdims=True)
    xc = x - mu
    var = jnp.mean(xc * xc, axis=-1, keepdims=True)
    return xc * lax.rsqrt(var + LN_EPS) * g + b


def _sigmoid(x):
    return 1.0 / (1.0 + jnp.exp(-x))


def _fn_ln(x, g, b):
    return (_layer_norm(x, g, b),)


def _fn_rms(x, g):
    return (x * lax.rsqrt(jnp.mean(x * x, axis=-1, keepdims=True) + RMS_EPS) * g,)


def _make_post_mix(alpha):
    def fn(h, mix, g, b):
        return (_layer_norm(alpha * h + mix, g, b),)
    return fn


def _make_ple_ln(alpha):
    def fn(h1, ffn, pg, pp, g, b):
        return (_layer_norm(alpha * h1 + ffn + _sigmoid(pg) * pp, g, b),)
    return fn


def _fn_lower_bounds(l0, l1):
    m = jnp.maximum(l0, l1)
    e0, e1 = jnp.exp(l0 - m), jnp.exp(l1 - m)
    s = e0 + e1
    p0, p1 = e0 / s, e1 / s
    return (p0 - p0, (p0 + p1) - p0)


def _split_dot(x, e_bf16):
    hi = x.astype(BF16)
    lo = (x - hi.astype(F32)).astype(BF16)
    return (jnp.dot(hi, e_bf16, preferred_element_type=F32) + jnp.dot(lo, e_bf16, preferred_element_type=F32))


def _hgrn_common(th):
    rm = lax.broadcasted_iota(jnp.int32, (th, HG_W), 0) % HG_CHUNK

    def seg_cumsum(x):
        for s in (1, 2, 4, 8):
            x = x + jnp.where(rm >= s, pltpu.roll(x, s, 0), 0.0)
        return x

    def seg_rcumsum(x):
        for s in (1, 2, 4, 8):
            x = x + jnp.where(rm < HG_CHUNK - s, pltpu.roll(x, th - s, 0), 0.0)
        return x

    ri = lax.broadcasted_iota(jnp.int32, (HG_W, HG_W), 0) // HEAD
    ci = lax.broadcasted_iota(jnp.int32, (HG_W, HG_W), 1) // HEAD
    head_f32 = (ri == ci).astype(F32)
    head_bf16 = head_f32.astype(BF16)

    def headsum(x, pieces=2):
        if pieces == 1:
            return jnp.dot(x.astype(BF16), head_bf16, preferred_element_type=F32)
        return _split_dot(x, head_bf16)

    return rm, seg_cumsum, seg_rcumsum, head_f32, headsum


def _hgrn_gates(qr, fl, lb):
    sg = _sigmoid(fl)
    f = lb + (1.0 - lb) * sg
    sq = _sigmoid(qr)
    return sg, f, jnp.log(f), 1.0 - f, qr * sq, sq


def _shifted(x, d, th):
    return x if d == 0 else pltpu.roll(x, d, 0)


def _unshift(x, d, th):
    return x if d == 0 else pltpu.roll(x, th - d, 0)


def _hgrn_fwd(projp, lb, ng, *, name, job=None):
    t_rows = projp.shape[0]
    th = min(HG_TILE, t_rows)
    nct = th // HG_CHUNK

    def body(q_ref, f_ref, i_ref, g_ref, lb_ref, ng_ref, oa_ref, opre_ref, st_out_ref,
             st_ref, vtm_ref, kv_ref, qe_ref, dec_ref, oint_ref):
        rm, seg_cumsum, seg_rcumsum, head_f32, headsum = _hgrn_common(th)

        @pl.when(pl.program_id(0) == 0)
        def _():
            st_ref[...] = jnp.zeros_like(st_ref)

        qr, fl, v, g = q_ref[...], f_ref[...], i_ref[...], g_ref[...]
        _, f, lf, k, q, _ = _hgrn_gates(qr, fl, lb_ref[...])
        b = seg_cumsum(lf)

        o = jnp.zeros((th, HG_W), F32)
        for d in range(HG_CHUNK):
            kd, bd, vd = _shifted(k, d, th), _shifted(b, d, th), _shifted(v, d, th)
            e = jnp.exp(jnp.where(rm >= d, b - bd, -1e30))
            o = o + headsum(q * kd * e, 1) * vd

        blast = seg_rcumsum(jnp.where(rm == HG_CHUNK - 1, b, 0.0))
        kte = (k * jnp.exp(blast - b)).astype(BF16)
        qe_ref[...] = q * jnp.exp(b)
        dec_ref[...] = jnp.exp(blast)
        vt = v.T
        lane_chunk = lax.broadcasted_iota(jnp.int32, (HG_W, th), 1) // HG_CHUNK
        for c in range(nct):
            vtm_ref[c * HG_W:(c + 1) * HG_W, :] = jnp.where(lane_chunk == c, vt, 0.0).astype(BF16)
        kv_ref[...] = jnp.dot(vtm_ref[...], kte, preferred_element_type=F32)

        s = st_ref[...]
        for c in range(nct):
            rows = slice(c * HG_CHUNK, (c + 1) * HG_CHUNK)
            st_out_ref[c] = s
            oint_ref[rows, :] = lax.dot_general(qe_ref[rows, :].astype(BF16), s.astype(BF16),
                                                (((1,), (1,)), ((), ())), preferred_element_type=F32)
            dec = jnp.max(dec_ref[rows, :], axis=0, keepdims=True)
            s = s * dec + kv_ref[c * HG_W:(c + 1) * HG_W, :] * head_f32
        st_ref[...] = s

        o = o + oint_ref[...]
        opre_ref[...] = o
        r = lax.rsqrt(headsum(o * o) * (1.0 / HEAD) + RMS_EPS)
        oa_ref[...] = (o * r * ng_ref[...] * (g * _sigmoid(g))).astype(oa_ref.dtype)

    col = lambda j: pl.BlockSpec((th, HG_W), lambda i, j=j: (i, j))
    vec = pl.BlockSpec((1, HG_W), lambda i: (0, 0))
    row = pl.BlockSpec((th, HG_W), lambda i: (i, 0))
    n_chunks = t_rows // HG_CHUNK
    return _call(
        body, (projp, projp, projp, projp, lb, ng), name=name, grid=(t_rows // th,),
        in_specs=[col(0), col(1), col(2), col(3), vec, vec],
        out_specs=[row, row, pl.BlockSpec((nct, HG_W, HG_W), lambda i: (i, 0, 0))],
        out_shape=[jax.ShapeDtypeStruct((t_rows, HG_W), BF16), jax.ShapeDtypeStruct((t_rows, HG_W), F32),
                   jax.ShapeDtypeStruct((n_chunks, HG_W, HG_W), F32)],
        scratch_shapes=[pltpu.VMEM((HG_W, HG_W), F32), pltpu.VMEM((nct * HG_W, th), BF16),
                        pltpu.VMEM((nct * HG_W, HG_W), F32), pltpu.VMEM((th, HG_W), F32),
                        pltpu.VMEM((th, HG_W), F32), pltpu.VMEM((th, HG_W), F32)],
        sem=("arbitrary",), job=job)


def _hgrn_bwd(projp, lb, ng, opre, states, dcat, *, name):
    t_rows = projp.shape[0]
    th = min(HG_TILE, t_rows)
    nct = th // HG_CHUNK
    nt = t_rows // th

    def body(q_ref, f_ref, i_ref, g_ref, lb_ref, ng_ref, opre_ref, st_in_ref, do_ref,
             dproj_ref, dng_ref, dlb_ref,
             gst_ref, dotm_ref, qg_ref, v_ref, kte_ref, dop_ref, dec_ref, dkte_ref, dvi_ref, dqe_ref, ddec_ref):
        rm, seg_cumsum, seg_rcumsum, head_f32, headsum = _hgrn_common(th)

        @pl.when(pl.program_id(0) == 0)
        def _():
            gst_ref[...] = jnp.zeros_like(gst_ref)
            dng_ref[...] = jnp.zeros_like(dng_ref)
            dlb_ref[...] = jnp.zeros_like(dlb_ref)

        qr, fl, v, g = q_ref[...], f_ref[...], i_ref[...], g_ref[...]
        lb, ngv = lb_ref[...], ng_ref[...]
        sg, f, lf, k, q, sq = _hgrn_gates(qr, fl, lb)
        b = seg_cumsum(lf)
        blast = seg_rcumsum(jnp.where(rm == HG_CHUNK - 1, b, 0.0))
        eb = jnp.exp(b)
        ekb = jnp.exp(blast - b)
        qe, kte, dec = q * eb, k * ekb, jnp.exp(blast)

        do_out, op = do_ref[...], opre_ref[...]
        sgg = _sigmoid(g)
        sil = g * sgg
        r = lax.rsqrt(headsum(op * op) * (1.0 / HEAD) + RMS_EPS)
        on = op * r
        dng_ref[...] += jnp.sum(do_out * on * sil, axis=0, keepdims=True)
        dg = do_out * on * ngv * (sgg * (1.0 + g * (1.0 - sgg)))
        don = do_out * ngv * sil
        dop = r * (don - on * (headsum(don * on) * (1.0 / HEAD)))

        v_ref[...] = v
        kte_ref[...] = kte
        dop_ref[...] = dop
        dec_ref[...] = dec
        dot_t = dop.T
        lane_chunk = lax.broadcasted_iota(jnp.int32, (HG_W, th), 1) // HG_CHUNK
        for c in range(nct):
            dotm_ref[c * HG_W:(c + 1) * HG_W, :] = jnp.where(lane_chunk == c, dot_t, 0.0).astype(BF16)
        qg_ref[...] = jnp.dot(dotm_ref[...], qe.astype(BF16), preferred_element_type=F32)

        gs = gst_ref[...]
        for c in reversed(range(nct)):
            rows = slice(c * HG_CHUNK, (c + 1) * HG_CHUNK)
            s = st_in_ref[c]
            gm = (gs * head_f32).astype(BF16)
            dkte_ref[rows, :] = jnp.dot(v_ref[rows, :].astype(BF16), gm, preferred_element_type=F32)
            dvi_ref[rows, :] = lax.dot_general(kte_ref[rows, :].astype(BF16), gm, (((1,), (1,)), ((), ())),
                                               preferred_element_type=F32)
            dqe_ref[rows, :] = jnp.dot(dop_ref[rows, :].astype(BF16), s.astype(BF16), preferred_element_type=F32)
            ddec_ref[rows, :] = jnp.broadcast_to(jnp.sum(gs * s, axis=0, keepdims=True), (HG_CHUNK, HG_W))
            dec_c = jnp.max(dec_ref[rows, :], axis=0, keepdims=True)
            gs = gs * dec_c + qg_ref[c * HG_W:(c + 1) * HG_W, :] * head_f32
        gst_ref[...] = gs

        dkte, dqe = dkte_ref[...], dqe_ref[...]
        dq = dqe * eb
        dk = dkte * ekb
        db = dqe * qe - dkte * kte
        dv = dvi_ref[...]
        dblast = dkte * kte + jnp.where(rm == HG_CHUNK - 1, ddec_ref[...] * dec, 0.0)

        for d in range(HG_CHUNK):
            kd, bd, vd = _shifted(k, d, th), _shifted(b, d, th), _shifted(v, d, th)
            e = jnp.exp(jnp.where(rm >= d, b - bd, -1e30))
            ke = kd * e
            p = q * ke
            sc = headsum(p, 1)
            dsc = headsum(dop * vd, 1)
            dv = dv + _unshift(sc * dop, d, th)
            dq = dq + dsc * ke
            dk = dk + _unshift(dsc * e * q, d, th)
            darg = dsc * p
            db = db + darg - _unshift(darg, d, th)

        db = db + jnp.where(rm == HG_CHUNK - 1, seg_cumsum(dblast), 0.0)
        dlf = seg_rcumsum(db)
        df = dlf / f - dk
        dlb_ref[...] += jnp.sum(df * (1.0 - sg), axis=0, keepdims=True)
        dfl = df * (1.0 - lb) * sg * (1.0 - sg)
        dqr = dq * (sq * (1.0 + qr * (1.0 - sq)))
        dproj_ref[...] = jnp.concatenate([dqr, dfl, dv, dg], axis=1).astype(dproj_ref.dtype)

    rev = lambda i: nt - 1 - i
    col = lambda j: pl.BlockSpec((th, HG_W), lambda i, j=j: (rev(i), j))
    vec = pl.BlockSpec((1, HG_W), lambda i: (0, 0))
    row = pl.BlockSpec((th, HG_W), lambda i: (rev(i), 0))
    tile_f32 = pltpu.VMEM((th, HG_W), F32)
    return pl.pallas_call(
        body, name=name, grid=(nt,),
        in_specs=[col(0), col(1), col(2), col(3), vec, vec, row,
                  pl.BlockSpec((nct, HG_W, HG_W), lambda i: (rev(i), 0, 0)), col(0)],
        out_specs=[pl.BlockSpec((th, 4 * HG_W), lambda i: (rev(i), 0)), vec, vec],
        out_shape=[jax.ShapeDtypeStruct((t_rows, 4 * HG_W), BF16), jax.ShapeDtypeStruct((1, HG_W), F32),
                   jax.ShapeDtypeStruct((1, HG_W), F32)],
        scratch_shapes=[pltpu.VMEM((HG_W, HG_W), F32), pltpu.VMEM((nct * HG_W, th), BF16),
                        pltpu.VMEM((nct * HG_W, HG_W), F32)] + [tile_f32] * 8,
        compiler_params=_cparams(("arbitrary",)),
    )(projp, projp, projp, projp, lb, ng, opre, states, dcat)


_INV_SQRT2 = 1.0 / math.sqrt(2.0)
_INV_SQRT2PI = 1.0 / math.sqrt(2.0 * math.pi)


def _gelu(x):
    return 0.5 * x * (1.0 + lax.erf(x * _INV_SQRT2))


def _gelu_grad(x):
    return 0.5 * (1.0 + lax.erf(x * _INV_SQRT2)) + x * jnp.exp(-0.5 * x * x) * _INV_SQRT2PI


def _sgu_parts(bu, bv, lg, lbias, w_ref, n_groups):
    c = SGU_CHUNK
    tril = (lax.broadcasted_iota(jnp.int32, (c, c), 0) >= lax.broadcasted_iota(jnp.int32, (c, c), 1)).astype(F32)
    gid = lax.broadcasted_iota(jnp.int32, bu.shape, 1) // HEAD
    u = _gelu(bu)
    gv = _gelu(bv)
    mu = jnp.mean(gv, axis=-1, keepdims=True)
    xc = gv - mu
    rstd = lax.rsqrt(jnp.mean(xc * xc, axis=-1, keepdims=True) + LN_EPS)
    xhat = xc * rstd
    vn = xhat * lg + lbias
    ws = [w_ref[gi] * tril for gi in range(n_groups)]
    return tril, gid, u, rstd, xhat, vn, ws


def _sgu_fwd(projp, lg, lbias, w_s, bias_full, *, name, job=None):
    t_rows = projp.shape[0]
    n_groups = w_s.shape[0]
    c = SGU_CHUNK
    rows_per_step = min(SGU_STEP_CHUNKS * c, t_rows)

    def body(u_ref, v_ref, lg_ref, lb_ref, w_ref, bias_ref, o_ref):
        for r0 in range(0, rows_per_step, c):
            rows = slice(r0, r0 + c)
            _, gid, u, _, _, vn, ws = _sgu_parts(u_ref[rows, :], v_ref[rows, :], lg_ref[...], lb_ref[...], w_ref,
                                                 n_groups)
            vnb = vn.astype(BF16)
            z = bias_ref[...]
            for gi in range(n_groups):
                z = z + jnp.where(gid == gi, jnp.dot(ws[gi].astype(BF16), vnb, preferred_element_type=F32), 0.0)
            o_ref[rows, :] = (u * z).astype(o_ref.dtype)

    col = lambda j: pl.BlockSpec((rows_per_step, HG_W), lambda i, j=j: (i, j))
    return _call(
        body, (projp, projp, lg, lbias, w_s, bias_full), name=name, grid=(t_rows // rows_per_step,),
        in_specs=[col(4), col(5), _const_spec(lg), _const_spec(lbias), _const_spec(w_s), _const_spec(bias_full)],
        out_specs=pl.BlockSpec((rows_per_step, HG_W), lambda i: (i, 0)),
        out_shape=jax.ShapeDtypeStruct((t_rows, HG_W), BF16), sem=("arbitrary",), job=job)


def _sgu_bwd(projp, lg, lbias, w_s, bias_full, dcat, *, name):
    t_rows = projp.shape[0]
    n_groups = w_s.shape[0]
    c = SGU_CHUNK
    rows_per_step = min(SGU_STEP_CHUNKS * c, t_rows)
    n = t_rows // rows_per_step

    def body(u_ref, v_ref, lg_ref, lb_ref, w_ref, bias_ref, do_ref,
             dproj_ref, dlg_ref, dlb_ref, dw_ref, dbs_ref, dbias_acc):
        i = pl.program_id(0)

        @pl.when(i == 0)
        def _():
            dlg_ref[...] = jnp.zeros_like(dlg_ref)
            dlb_ref[...] = jnp.zeros_like(dlb_ref)
            dw_ref[...] = jnp.zeros_like(dw_ref)
            dbias_acc[...] = jnp.zeros_like(dbias_acc)

        lg_v = lg_ref[...]
        for r0 in range(0, rows_per_step, c):
            rows = slice(r0, r0 + c)
            bu, bv = u_ref[rows, :], v_ref[rows, :]
            tril, gid, u, rstd, xhat, vn, ws = _sgu_parts(bu, bv, lg_v, lb_ref[...], w_ref, n_groups)
            vnb = vn.astype(BF16)
            z = bias_ref[...]
            for gi in range(n_groups):
                z = z + jnp.where(gid == gi, jnp.dot(ws[gi].astype(BF16), vnb, preferred_element_type=F32), 0.0)
            do = do_ref[rows, :]
            dbu = do * z * _gelu_grad(bu)
            dz = do * u
            dbias_acc[...] += dz
            dvn = jnp.zeros_like(dz)
            for gi in range(n_groups):
                dzg = jnp.where(gid == gi, dz, 0.0).astype(BF16)
                dw_ref[gi] += lax.dot_general(dzg, vnb, (((1,), (1,)), ((), ())), preferred_element_type=F32) * tril
                dvn = dvn + jnp.dot(ws[gi].T.astype(BF16), dzg, preferred_element_type=F32)
            dlg_ref[...] += jnp.sum(dvn * xhat, axis=0, keepdims=True)
            dlb_ref[...] += jnp.sum(dvn, axis=0, keepdims=True)
            dxh = dvn * lg_v
            dgv = rstd * (dxh - jnp.mean(dxh, axis=-1, keepdims=True)
                          - xhat * jnp.mean(dxh * xhat, axis=-1, keepdims=True))
            dproj_ref[rows, :] = jnp.concatenate([dbu, dgv * _gelu_grad(bv)], axis=1).astype(dproj_ref.dtype)

        @pl.when(i == n - 1)
        def _():
            dbs_ref[...] = jnp.sum(dbias_acc[...].T.reshape(n_groups, HEAD, c), axis=1)

    col = lambda j: pl.BlockSpec((rows_per_step, HG_W), lambda i, j=j: (i, j))
    return pl.pallas_call(
        body, name=name, grid=(n,),
        in_specs=[col(4), col(5), _const_spec(lg), _const_spec(lbias), _const_spec(w_s), _const_spec(bias_full),
                  col(1)],
        out_specs=[pl.BlockSpec((rows_per_step, 2 * HG_W), lambda i: (i, 0)), _const_spec(lg), _const_spec(lbias),
                   _const_spec(w_s), pl.BlockSpec((n_groups, c), lambda i: (0, 0))],
        out_shape=[jax.ShapeDtypeStruct((t_rows, 2 * HG_W), BF16), jax.ShapeDtypeStruct(lg.shape, F32),
                   jax.ShapeDtypeStruct(lbias.shape, F32), jax.ShapeDtypeStruct(w_s.shape, F32),
                   jax.ShapeDtypeStruct((n_groups, c), F32)],
        scratch_shapes=[pltpu.VMEM((c, HG_W), F32)],
        compiler_params=_cparams(("arbitrary",)),
    )(projp, projp, lg, lbias, w_s, bias_full, dcat)


def _rope_tables(positions):
    t = positions.shape[0]
    inv_freq = ROPE_THETA ** (-jnp.arange(0, 32, 2, dtype=F32) / 32)
    ang = positions.astype(F32)[:, None] * inv_freq
    cos, sin = jnp.cos(ang), jnp.sin(ang)
    z = lambda w: jnp.zeros((t, w), F32)
    cos_t = jnp.concatenate([jnp.ones((t, 64), F32), cos, cos, z(32)], axis=1)
    sin_up = jnp.concatenate([z(80), sin, z(32)], axis=1)
    sin_dn = jnp.concatenate([z(64), -sin, z(48)], axis=1)
    return cos_t, sin_up, sin_dn


def _rep(x, n):
    return x if n == 1 else jnp.concatenate([x] * n, axis=1)


def _rope(x, cos_t, sin_up, sin_dn):
    w = x.shape[1]
    return x * cos_t + pltpu.roll(x, 16, 1) * sin_up + pltpu.roll(x, w - 16, 1) * sin_dn


def _rope_t(dy, cos_t, sin_up, sin_dn):
    w = dy.shape[1]
    return dy * cos_t + pltpu.roll(dy * sin_up, w - 16, 1) + pltpu.roll(dy * sin_dn, 16, 1)


def _mla_prep(q, kv, projp, tables, *, name):
    nh = N_ATT_HEADS

    def fn(qv, kvv, kr, cos_t, sin_up, sin_dn):
        qr = _rope(qv, _rep(cos_t, nh), _rep(sin_up, nh), _rep(sin_dn, nh))
        krr = _rope(kr, cos_t, sin_up, sin_dn)
        lane = lax.broadcasted_iota(jnp.int32, kvv.shape, 1) % LANES
        return qr, jnp.where(lane < HEAD, kvv, 0.0) + _rep(krr, nh), kvv

    w = q.shape[1]
    return _rowwise(fn, [q, kv, (projp, LANES, P_KR // LANES)] + list(tables), [],
                    [(w, BF16), (w, BF16), (w, BF16)], tile=1024, name=name)


def _mla_prep_bwd(dqr, dkf, tables, *, name):
    nh = N_ATT_HEADS

    def fn(dq, dk, cos_t, sin_up, sin_dn):
        dqp = _rope_t(dq, _rep(cos_t, nh), _rep(sin_up, nh), _rep(sin_dn, nh))
        dkrr = dk[:, 0:LANES]
        for h in range(1, nh):
            dkrr = dkrr + dk[:, LANES * h:LANES * (h + 1)]
        return dqp, _rope_t(dkrr, cos_t, sin_up, sin_dn)

    return _rowwise(fn, [dqr, dkf] + list(tables), [], [(dqr.shape[1], BF16), (LANES, BF16)], tile=1024, name=name)


_LOG2E = 1.0 / math.log(2.0)
_NT = (((1,), (1,)), ((), ()))
_TN = (((0,), (0,)), ((), ()))


def _attn_fwd(qr, kf, kvb, *, name, job=None):
    t_rows = qr.shape[0]
    tq = min(ATT_TQ, t_rows)
    nb = t_rows // tq
    scale = ATT_D ** -0.5

    c2 = scale * _LOG2E

    def body(q_ref, kf_ref, kvb_ref, o_ref, lse_ref):
        qi = pl.program_id(1)
        lane = lax.broadcasted_iota(jnp.int32, (tq, LANES), 1)
        causal_t = (lax.broadcasted_iota(jnp.int32, (tq, tq), 0) <= lax.broadcasted_iota(jnp.int32, (tq, tq), 1))
        heads = [slice(hh * LANES, (hh + 1) * LANES) for hh in range(2)]
        qs = [q_ref[:, cols] for cols in heads]

        def block(first, n_keys, carry, diagonal):
            rows = pl.ds(pl.multiple_of(first * tq, tq), n_keys)
            new = []
            for q, cols, (m_old, l_old, acc_t) in zip(qs, heads, carry):
                s_t = lax.dot_general(kf_ref[rows, cols], q, _NT, preferred_element_type=F32)
                if diagonal:
                    s_t = jnp.where(causal_t, s_t, -1e30)
                m_new = jnp.maximum(m_old, jnp.max(s_t, axis=0, keepdims=True))
                p_t = jnp.exp2((s_t - m_new) * c2)
                a = jnp.exp2((m_old - m_new) * c2)
                pv_t = lax.dot_general(kvb_ref[rows, cols], p_t.astype(BF16), _TN, preferred_element_type=F32)
                new.append((m_new, a * l_old + jnp.sum(p_t, axis=0, keepdims=True), a * acc_t + pv_t))
            return tuple(new)

        init = (jnp.full((1, tq), -1e30, F32), jnp.zeros((1, tq), F32), jnp.zeros((LANES, tq), F32))
        carry = lax.fori_loop(0, qi // 4, lambda g, c: block(4 * g, 4 * tq, c, False), (init, init))
        carry = lax.cond((qi // 2) % 2 == 1, lambda c: block(4 * (qi // 4), 2 * tq, c, False), lambda c: c, carry)
        carry = lax.cond(qi % 2 == 1, lambda c: block(qi - 1, tq, c, False), lambda c: c, carry)
        outs = []
        for hh, (m_fin, l_fin, acc_t) in enumerate(block(qi, tq, carry, True)):
            lse_ref[hh] = m_fin * scale + jnp.log(l_fin)
            outs.append((acc_t / l_fin).T)
        o_ref[...] = jnp.where(lane < HEAD, pltpu.roll(outs[0], HEAD, 1), outs[1])

    pair = pl.BlockSpec((t_rows, 2 * LANES), lambda pr, qi: (0, pr))
    return _call(
        body, (qr, kf, kvb), name=name, grid=(N_ATT_HEADS // 2, nb),
        in_specs=[pl.BlockSpec((tq, 2 * LANES), lambda pr, qi: (qi, pr)), pair, pair],
        out_specs=[pl.BlockSpec((tq, LANES), lambda pr, qi: (qi, pr)),
                   pl.BlockSpec((2, 1, tq), lambda pr, qi: (pr, 0, qi))],
        out_shape=[jax.ShapeDtypeStruct((t_rows, N_ATT_HEADS * HEAD), F32),
                   jax.ShapeDtypeStruct((N_ATT_HEADS, 1, t_rows), F32)],
        sem=("parallel", "arbitrary"), job=job)


def _attn_bwd(qr, kf, kvb, dcat, o, lse, *, name, job=None):
    t_rows = qr.shape[0]
    tq = min(ATT_TQ, t_rows)
    nb = t_rows // tq
    scale = ATT_D ** -0.5
    c2 = scale * _LOG2E
    do_off = 2 * HG_W // LANES

    def body(q_ref, kf_ref, kvb_ref, do_ref, o_ref, lse_ref, dq_ref, dkv_ref, dk_ref):
        ki = pl.program_id(1)

        @pl.when(ki == 0)
        def _():
            dq_ref[...] = jnp.zeros_like(dq_ref)

        lane = lax.broadcasted_iota(jnp.int32, (tq, LANES), 1)
        causal_t = (lax.broadcasted_iota(jnp.int32, (tq, tq), 0) <= lax.broadcasted_iota(jnp.int32, (tq, tq), 1))
        heads = [slice(hh * LANES, (hh + 1) * LANES) for hh in range(2)]
        ks = [kf_ref[:, cols] for cols in heads]
        vs = [kvb_ref[:, cols] for cols in heads]

        def block(qi, n_q, carry, diagonal):
            rows = pl.ds(pl.multiple_of(qi * tq, tq), n_q)
            do_pair, o_pair = do_ref[rows, :], o_ref[rows, :]
            upper = lax.broadcasted_iota(jnp.int32, do_pair.shape, 1) >= HEAD
            new = []
            for hh, (cols, k, v, (dk, dv)) in enumerate(zip(heads, ks, vs, carry)):
                q = q_ref[rows, cols]
                do, ov = (pltpu.roll(do_pair, HEAD, 1), pltpu.roll(o_pair, HEAD, 1)) if hh == 0 else (do_pair, o_pair)
                do = jnp.where(upper, do, 0.0)
                delta = jnp.sum((do * ov).T, axis=0, keepdims=True)
                s_t = lax.dot_general(k, q, _NT, preferred_element_type=F32)
                if diagonal:
                    s_t = jnp.where(causal_t, s_t, -1e30)
                p_t = jnp.exp2(s_t * c2 - lse_ref[hh, :, rows] * _LOG2E)
                dob = do.astype(BF16)
                dv = dv + jnp.dot(p_t.astype(BF16), dob, preferred_element_type=F32)
                dp_t = lax.dot_general(v, dob, _NT, preferred_element_type=F32)
                ds_t = (p_t * (dp_t - delta) * scale).astype(BF16)
                dk = dk + jnp.dot(ds_t, q, preferred_element_type=F32)
                dq_ref[rows, cols] += lax.dot_general(ds_t, k, _TN, preferred_element_type=F32)
                new.append((dk, dv))
            return tuple(new)

        zero = jnp.zeros((tq, LANES), F32)
        carry = block(ki, tq, ((zero, zero), (zero, zero)), True)
        rest = nb - 1 - ki
        carry = lax.fori_loop(0, rest // 2, lambda g, c: block(ki + 1 + 2 * g, 2 * tq, c, False), carry)
        carry = lax.cond(rest % 2 == 1, lambda c: block(nb - 1, tq, c, False), lambda c: c, carry)
        dkv_ref[...] = jnp.concatenate([jnp.where(lane < HEAD, dk, dv) for dk, dv in carry],
                                       axis=1).astype(dkv_ref.dtype)
        dk_ref[...] = jnp.concatenate([dk for dk, _ in carry], axis=1)

    pair_all = pl.BlockSpec((t_rows, 2 * LANES), lambda pr, ki: (0, pr))
    pair_blk = pl.BlockSpec((tq, 2 * LANES), lambda pr, ki: (ki, pr))
    wide = jax.ShapeDtypeStruct((t_rows, N_ATT_HEADS * LANES), F32)
    return _call(
        body, (qr, kf, kvb, dcat, o, lse), name=name, grid=(N_ATT_HEADS // 2, nb),
        in_specs=[pair_all, pair_blk, pair_blk,
                  pl.BlockSpec((t_rows, LANES), lambda pr, ki: (0, do_off + pr)),
                  pl.BlockSpec((t_rows, LANES), lambda pr, ki: (0, pr)),
                  pl.BlockSpec((2, 1, t_rows), lambda pr, ki: (pr, 0, 0))],
        out_specs=[pair_all, pair_blk, pair_blk],
        out_shape=[wide, jax.ShapeDtypeStruct(wide.shape, BF16), wide],
        sem=("parallel", "arbitrary"), job=job)


def _my_pos():
    return lax.axis_index("x"), lax.axis_index("y"), lax.axis_index("c")


def _all_gather(xs, *, name, columns=True):
    return _gather_forward(_run_job(_gather_job(xs, columns), name=name), name=name + "_forward")


def _remote(src, dst, send_sems, recv_sems, k, dev):
    return pltpu.make_async_remote_copy(src_ref=src, dst_ref=dst, send_sem=send_sems.at[k], recv_sem=recv_sems.at[k],
                                        device_id=dev, device_id_type=MESH)


def _block(ref, idx):
    if len(ref.shape) == 2:
        return ref.at[:, pl.ds(pl.multiple_of(idx * LANES, LANES), LANES)]
    return ref.at[idx]


def _gather_job(xs, columns=True):
    n = len(xs)

    def make(x_refs, out_refs, send_sems, recv_sems, local_sems):
        mx, my, mc = _my_pos()
        mine = 4 * mx + 2 * my + mc
        peers = [(mx, my, 1 - mc), (1 - mx, my, mc), (mx, 1 - my, mc), (1 - mx, 1 - my, mc)]
        sends, recvs, local = [], [], []
        for a in range(n):
            local.append(pltpu.make_async_copy(x_refs[a], _block(out_refs[a], mine), local_sems.at[a]))
            for k, dev in enumerate(peers):
                theirs = 4 * dev[0] + 2 * dev[1] + dev[2]
                sends.append(_remote(x_refs[a], _block(out_refs[a], mine), send_sems, recv_sems, 4 * a + k, dev))
                recvs.append(_remote(x_refs[a], _block(out_refs[a], theirs), send_sems, recv_sems, 4 * a + k, dev))
        return sends, recvs, local

    def gathered(x):
        if columns and x.ndim == 2 and x.shape[1] == LANES:
            return jax.ShapeDtypeStruct((x.shape[0], N_DEV * LANES), x.dtype)
        return jax.ShapeDtypeStruct((N_DEV,) + x.shape, x.dtype)

    return _copies_job(xs, [gathered(x) for x in xs], 4 * n, n, make)


def _forward_job(gs):
    n = len(gs)

    def make(in_refs, out_refs, send_sems, recv_sems, local_sems):
        mx, my, mc = _my_pos()
        chips = [(1 - mx, my), (mx, 1 - my), (1 - mx, 1 - my)]
        sends, recvs = [], []
        for a in range(n):
            for j, (cx, cy) in enumerate(chips):
                here = _block(out_refs[a], 4 * cx + 2 * cy + mc)
                there = _block(out_refs[a], 4 * cx + 2 * cy + 1 - mc)
                sends.append(_remote(here, here, send_sems, recv_sems, 3 * a + j, (mx, my, 1 - mc)))
                recvs.append(_remote(here, there, send_sems, recv_sems, 3 * a + j, (mx, my, 1 - mc)))
        return sends, recvs, []

    shapes = [jax.ShapeDtypeStruct(g.shape, g.dtype) for g in gs]
    return _copies_job(gs, shapes, 3 * n, 0, make, in_place=True)


def _gather_forward(gs, *, name):
    return _run_job(_forward_job(gs), name=name)


def _pair_job(xs):
    n = len(xs)

    def make(x_refs, out_refs, send_sems, recv_sems, local_sems):
        mx, my, mc = _my_pos()

        def src(ref, g):
            return _block(ref, 2 * g + 1 - mc) if len(ref.shape) == 2 else ref.at[g, 1 - mc]

        copies = [_remote(src(x_refs[a], g), out_refs[a].at[g], send_sems, recv_sems, 4 * a + g, (mx, my, 1 - mc))
                  for a in range(n) for g in range(4)]
        return copies, copies, []

    shapes = [jax.ShapeDtypeStruct((4, x.shape[0], LANES) if x.ndim == 2 else (4,) + x.shape[2:], x.dtype)
              for x in xs]
    return _copies_job(xs, shapes, 4 * n, 0, make)


def _pair_add(x, r, core, *, name):
    _, a, b = r.shape
    ta = _row_tile(a, 512)

    def body(c_ref, x_ref, r_ref, o_ref):
        o_ref[...] = (x_ref[...] + r_ref[...]).astype(o_ref.dtype)

    blk = pl.BlockSpec((None, ta, b), lambda g, i, c_ref: (g, i, 0))
    own = (pl.BlockSpec((ta, b), lambda g, i, c_ref: (i, 2 * g + c_ref[0])) if x.ndim == 2
           else pl.BlockSpec((None, None, ta, b), lambda g, i, c_ref: (g, c_ref[0], i, 0)))
    return pl.pallas_call(
        body, name=name,
        grid_spec=pltpu.PrefetchScalarGridSpec(
            num_scalar_prefetch=1, grid=(4, a // ta), in_specs=[own, blk], out_specs=blk),
        out_shape=jax.ShapeDtypeStruct((4, a, b), BF16),
        compiler_params=_cparams(("parallel", "parallel")),
    )(core, x, r)


def _quad_job(xs):
    n = len(xs)

    def make(x_refs, out_refs, send_sems, recv_sems, local_sems):
        mx, my, mc = _my_pos()
        mine = 2 * mx + my
        peers = [((1 - mx, my, mc), 2 * (1 - mx) + my), ((mx, 1 - my, mc), 2 * mx + 1 - my),
                 ((1 - mx, 1 - my, mc), 2 * (1 - mx) + 1 - my)]
        sends, recvs, local = [], [], []
        for a in range(n):
            local.append(pltpu.make_async_copy(x_refs[a].at[mine], out_refs[a].at[mine], local_sems.at[a]))
            for k, (dev, g) in enumerate(peers):
                sends.append(_remote(x_refs[a].at[g], out_refs[a].at[mine], send_sems, recv_sems, 3 * a + k, dev))
                recvs.append(_remote(x_refs[a].at[g], out_refs[a].at[g], send_sems, recv_sems, 3 * a + k, dev))
        return sends, recvs, local

    shapes = [jax.ShapeDtypeStruct(x.shape, x.dtype) for x in xs]
    return _copies_job(xs, shapes, 3 * n, n, make)


def _row_tile(r, pref):
    t = min(pref, r)
    while r % t or (t % 8 and t != r):
        t -= 1
    return t


def _adamw(parts, w, m, v, layer, *, name, tile=512, into=None):
    g, a, b = parts.shape
    tile = _row_tile(a, tile)
    c1 = 1.0 / (1.0 - ADAM_B1 ** ADAM_STEP)
    c2 = 1.0 / (1.0 - ADAM_B2 ** ADAM_STEP)
    into = tuple(into or ())

    def body(p_ref, w_ref, m_ref, v_ref, *refs):
        g_ref, d_ref, mo_ref, vo_ref = refs[len(into):]
        grad = p_ref[0].astype(F32)
        for j in range(1, g):
            grad = grad + p_ref[j].astype(F32)
        mn = ADAM_B1 * m_ref[...] + (1.0 - ADAM_B1) * grad
        vn = ADAM_B2 * v_ref[...] + (1.0 - ADAM_B2) * (grad * grad)
        g_ref[...] = grad
        mo_ref[...] = mn
        vo_ref[...] = vn
        d_ref[...] = -ADAM_LR * ((mn * c1) / (jnp.sqrt(vn * c2) + ADAM_EPS) + ADAM_WD * w_ref[...])

    if layer is None:
        src, shape = pl.BlockSpec((tile, b), lambda i: (i, 0)), (a, b)
    else:
        src, shape = pl.BlockSpec((None, tile, b), lambda i: (layer, i, 0)), w.shape
    return pl.pallas_call(
        body, name=name, grid=(a // tile,),
        in_specs=[pl.BlockSpec((g, tile, b), lambda i: (0, i, 0)), src, src, src] + [_ANY] * len(into),
        out_specs=[src] * 4,
        out_shape=[jax.ShapeDtypeStruct(shape, F32)] * 4,
        input_output_aliases={4 + i: i for i in range(len(into))},
        compiler_params=_cparams(("parallel",)),
    )(parts, w, m, v, *into)


W_IN_SHARD = 276


def _w_in_dest(col):
    return jnp.where(col < P_KR, col, jnp.where(col < P_KR + 256, col + (P_CKV - P_KR), col - 2176 + P_KR + HEAD))


PLACE_TILE = 384
PLACE_SHARDS = 3
PICK_TILE = 128
PICK_TILES = 4


def _w_in_tables():
    col = np.arange(N_DEV * W_IN_SHARD)
    dest = np.where(col < P_KR, col, np.where(col < P_KR + 256, col + (P_CKV - P_KR), col - 2176 + P_KR + HEAD))
    shard = col // W_IN_SHARD

    def filled(used, universe, n):
        used = sorted(set(int(u) for u in used))
        assert len(used) <= n, used
        return used + [u for u in universe if u not in used][:n - len(used)]

    place = [filled(shard[dest // PLACE_TILE == c], range(N_DEV), PLACE_SHARDS) for c in range(P_COLS // PLACE_TILE)]
    pick = [filled(dest[shard == j] // PICK_TILE, range(P_COLS // PICK_TILE), PICK_TILES) for j in range(N_DEV)]
    return np.asarray(place, np.int32).reshape(-1), np.asarray(pick, np.int32).reshape(-1)


def _place_w_in(g, *, name):
    _, d, sh = g.shape
    tc, ns = PLACE_TILE, PLACE_SHARDS
    table = jnp.asarray(_w_in_tables()[0])

    def body(tab_ref, g_ref, o_ref, acc_ref):
        ct, s = pl.program_id(0), pl.program_id(1)
        j = tab_ref[ct * ns + s]

        @pl.when(s == 0)
        def _():
            acc_ref[...] = jnp.zeros_like(acc_ref)

        src = j * sh + lax.broadcasted_iota(jnp.int32, (sh, tc), 0)
        dst = ct * tc + lax.broadcasted_iota(jnp.int32, (sh, tc), 1)
        place = (_w_in_dest(src) == dst).astype(BF16)
        acc_ref[...] += jnp.dot(g_ref[...], place, preferred_element_type=F32)

        @pl.when(s == ns - 1)
        def _():
            o_ref[...] = acc_ref[...].astype(o_ref.dtype)

    return pl.pallas_call(
        body, name=name,
        grid_spec=pltpu.PrefetchScalarGridSpec(
            num_scalar_prefetch=1, grid=(P_COLS // tc, ns),
            in_specs=[pl.BlockSpec((None, d, sh), lambda ct, s, tab: (tab[ct * ns + s], 0, 0))],
            out_specs=pl.BlockSpec((d, tc), lambda ct, s, tab: (0, ct)),
            scratch_shapes=[pltpu.VMEM((d, tc), F32)]),
        out_shape=jax.ShapeDtypeStruct((d, P_COLS), BF16),
        compiler_params=_cparams(("parallel", "arbitrary")),
    )(table, g)


def _unplace_w_in(dw, *, name):
    d = dw.shape[0]
    sh, tk, nt = W_IN_SHARD, PICK_TILE, PICK_TILES
    table = jnp.asarray(_w_in_tables()[1])

    def body(tab_ref, dw_ref, o_ref):
        j, kk = pl.program_id(0), pl.program_id(1)
        tile = tab_ref[j * nt + kk]
        src = j * sh + lax.broadcasted_iota(jnp.int32, (tk, sh), 1)
        dst = tile * tk + lax.broadcasted_iota(jnp.int32, (tk, sh), 0)
        pick = (_w_in_dest(src) == dst).astype(BF16)
        part = _split_dot(dw_ref[...], pick)

        @pl.when(kk == 0)
        def _():
            o_ref[...] = part

        @pl.when(kk > 0)
        def _():
            o_ref[...] += part

    return pl.pallas_call(
        body, name=name,
        grid_spec=pltpu.PrefetchScalarGridSpec(
            num_scalar_prefetch=1, grid=(N_DEV, nt),
            in_specs=[pl.BlockSpec((d, tk), lambda j, kk, tab: (0, tab[j * nt + kk]))],
            out_specs=pl.BlockSpec((None, d, sh), lambda j, kk, tab: (j, 0, 0))),
        out_shape=jax.ShapeDtypeStruct((N_DEV, d, sh), F32),
        compiler_params=_cparams(("parallel", "arbitrary")),
    )(table, dw)


def _gate_up_swiglu(h1, wgu, *, name):
    t_rows, k = h1.shape
    w = wgu.shape[2]
    tm = _tile(t_rows, 1024)

    def body(a_ref, wg_ref, wu_ref, gu_ref, act_ref):
        a = a_ref[...].astype(BF16)
        gate = jnp.dot(a, wg_ref[...], preferred_element_type=F32)
        up = jnp.dot(a, wu_ref[...], preferred_element_type=F32)
        gu_ref[0] = gate.astype(gu_ref.dtype)
        gu_ref[1] = up.astype(gu_ref.dtype)
        act_ref[...] = (gate * _sigmoid(gate) * up).astype(act_ref.dtype)

    return pl.pallas_call(
        body, name=name, grid=(t_rows // tm, 4),
        in_specs=[pl.BlockSpec((tm, k), lambda i, j: (i, 0)),
                  pl.BlockSpec((None, k, w), lambda i, j: (j, 0, 0)),
                  pl.BlockSpec((None, k, w), lambda i, j: (j + 4, 0, 0))],
        out_specs=[pl.BlockSpec((2, None, tm, w), lambda i, j: (0, j, i, 0)),
                   pl.BlockSpec((None, tm, w), lambda i, j: (j, i, 0))],
        out_shape=[jax.ShapeDtypeStruct((2, 4, t_rows, w), BF16), jax.ShapeDtypeStruct((4, t_rows, w), BF16)],
        compiler_params=_cparams(("parallel", "arbitrary")),
    )(h1, wgu, wgu)


def _down_dx_swiglu(dffn, wdown, gu, *, name):
    t_rows, k = dffn.shape
    w = gu.shape[3]
    tm = _tile(t_rows, 1024)

    def body(d_ref, w_ref, gu_ref, o_ref):
        dact = lax.dot_general(d_ref[...].astype(BF16), w_ref[...], _NT, preferred_element_type=F32)
        gate, up = gu_ref[0].astype(F32), gu_ref[1].astype(F32)
        sg = _sigmoid(gate)
        silu = gate * sg
        o_ref[0] = (dact * up * (sg + silu - silu * sg)).astype(o_ref.dtype)
        o_ref[1] = (dact * silu).astype(o_ref.dtype)

    blk = pl.BlockSpec((2, None, tm, w), lambda i, j: (0, j, i, 0))
    return pl.pallas_call(
        body, name=name, grid=(t_rows // tm, 4),
        in_specs=[pl.BlockSpec((tm, k), lambda i, j: (i, 0)), pl.BlockSpec((w, k), lambda i, j: (j, 0)), blk],
        out_specs=blk, out_shape=jax.ShapeDtypeStruct(gu.shape, BF16),
        compiler_params=_cparams(("parallel", "arbitrary")),
    )(dffn, wdown, gu)


BIG = ("w_in", "mla_w_uq", "mla_w_ukv", "w_out", "w_gate_up", "w_down", "ple_w_gate", "ple_w_proj")
SMALL = ("ln_in_g", "ln_in_b", "hgrn_lb_logits", "hgrn_norm_g", "sgu_ln_g", "sgu_ln_b", "sgu_w_s", "sgu_b_s",
         "mla_q_norm_g", "mla_kv_norm_g", "ln1_g", "ln1_b", "ln2_g", "ln2_b")
ORDER = ("ln_in_g", "ln_in_b", "w_in", "hgrn_lb_logits", "hgrn_norm_g", "sgu_ln_g", "sgu_ln_b", "sgu_w_s", "sgu_b_s",
         "mla_q_norm_g", "mla_w_uq", "mla_kv_norm_g", "mla_w_ukv", "w_out", "ln1_g", "ln1_b", "w_gate_up", "w_down",
         "ple_w_gate", "ple_w_proj", "ln2_g", "ln2_b")


def _slab(a, align):
    s = a.reshape(-1, LANES)
    pad = -s.shape[0] % align
    return jnp.pad(s, ((0, pad), (0, 0))) if pad else s


def _pack(arrays, align=16, total_align=512):
    s = jnp.concatenate([_slab(a, align) for a in arrays], axis=0)
    pad = -s.shape[0] % total_align
    return jnp.pad(s, ((0, pad), (0, 0))) if pad else s


def _unpack(slab, shapes, align=16):
    out, r0 = [], 0
    for s in shapes:
        nr = math.prod(s) // LANES
        out.append(slab[r0:r0 + nr].reshape(s))
        r0 += nr + (-nr % align)
    return out


def _weight_shards(w, li):
    uq_pad = ((0, 0), (0, LANES - ATT_D))
    shards = {k: w[k][li] for k in BIG}
    shards["mla_w_uq"] = jnp.pad(shards["mla_w_uq"], uq_pad)
    return {k: s.astype(BF16) for k, s in shards.items()}


def _usable_weights(g, *, name):
    out = {}
    for k, a in g.items():
        if k == "w_in":
            out[k] = _place_w_in(a, name=name + "_place_w_in")
        elif k in ("w_out", "w_down", "ple_w_gate"):
            out[k] = a.reshape(a.shape[0] * a.shape[1], a.shape[2])
        else:
            out[k] = a
    return out


BY_COLUMNS = ("mla_w_uq", "mla_w_ukv", "ple_w_proj")


def _as_pairs(k, g):
    if k in BY_COLUMNS:
        return g
    if g.ndim == 2:
        return g.reshape((4, 2, g.shape[0] // N_DEV) + g.shape[1:])
    return g.reshape((4, 2) + g.shape[1:])


def _twice(fn):
    return lambda *a: fn(*a) * 2


def _layer_forward(li, h, hb, p_i, wts, sm, lbs, tables, alpha, hgrn_job=None, after_hgrn=None, attn_job=None,
                   after_attn=None, loss_target=None):
    n = f"l{li}_"
    row1 = lambda a: a.reshape(1, -1)
    projp = _mm(hb, wts["w_in"], name=n + "proj")
    ng = row1(sm["hgrn_norm_g"][li])
    res = _hgrn_fwd(projp, lbs[li], ng, name=n + "hgrn_fwd", job=hgrn_job)
    if hgrn_job is not None:
        res, got = res
    o_a, o_pre, states = res
    lg, lbias = row1(sm["sgu_ln_g"][li]), row1(sm["sgu_ln_b"][li])
    w_s = sm["sgu_w_s"][li]
    bias_full = jnp.repeat(sm["sgu_b_s"][li].T, HEAD, axis=1)
    o_b = _sgu_fwd(projp, lg, lbias, w_s, bias_full, name=n + "sgu_fwd",
                   job=None if hgrn_job is None else _forward_job(got))
    if hgrn_job is not None:
        o_b, got = o_b
        wts = dict(wts, **after_hgrn(got))
    qg, kvg = row1(sm["mla_q_norm_g"][li]), row1(sm["mla_kv_norm_g"][li])
    cq_view, ckv_view = (projp, 384, P_CQ // 384), (projp, 256, P_CKV // 256)
    (cqn,) = _rowwise(_fn_rms, [cq_view], [qg], [(384, BF16)], tile=2048, name=n + "q_norm")
    (ckvn,) = _rowwise(_fn_rms, [ckv_view], [kvg], [(256, BF16)], tile=2048, name=n + "kv_norm")
    q = _mm(cqn, wts["mla_w_uq"], name=n + "uq")
    kv = _mm(ckvn, wts["mla_w_ukv"], name=n + "ukv")
    qr, kf, kvb = _mla_prep(q, kv, projp, tables, name=n + "mla_prep")
    res = _attn_fwd(qr, kf, kvb, name=n + "attn_fwd", job=attn_job)
    if attn_job is not None:
        res, got = res
    o_c, lse = res
    cat = jnp.concatenate([o_a, o_b, o_c.astype(BF16)], axis=1)
    mix = _mm(cat, wts["w_out"], name=n + "out_proj", job=None if attn_job is None else _forward_job(got))
    if attn_job is not None:
        mix, got = mix
        wts = dict(wts, **after_attn(got))
    g1, b1 = row1(sm["ln1_g"][li]), row1(sm["ln1_b"][li])
    d = h.shape[1]
    h1, h1b = _rowwise(_twice(_make_post_mix(alpha)), [h, mix], [g1, b1], [(d, F32), (d, BF16)], tile=1024,
                       name=n + "ln1")
    gu, act = _gate_up_swiglu(h1b, wts["w_gate_up"], name=n + "gate_up")
    ffn = _mm_kblocks(act, wts["w_down"], bm="kn", tm=1024, name=n + "down")
    pg = _mm(h1b, wts["ple_w_gate"], name=n + "ple_gate")
    pp = _mm(p_i, wts["ple_w_proj"], name=n + "ple_proj")
    g2, b2 = row1(sm["ln2_g"][li]), row1(sm["ln2_b"][li])
    if loss_target is None:
        out = _rowwise(_twice(_make_ple_ln(alpha)), [h1, ffn, pg, pp], [g2, b2], [(d, F32), (d, BF16)],
                       name=n + "ln2")
    else:
        def ln_and_loss(h1v, ffnv, pgv, ppv, tv, gv, bv):
            err = _make_ple_ln(alpha)(h1v, ffnv, pgv, ppv, gv, bv)[0] - tv
            return err * (1.0 / d), 0.5 * jnp.sum(jnp.mean(err * err, axis=-1, keepdims=True), axis=0, keepdims=True)

        out = _rowwise(ln_and_loss, [h1, ffn, pg, pp, loss_target], [g2, b2], [(d, F32)], accs=[(1, 1)],
                       name=n + "ln2_loss")
    saved = dict(h=h, hb=hb, h1b=h1b, projp=projp, o_pre=o_pre, states=states, cqn=cqn, ckvn=ckvn, qr=qr, kf=kf, kvb=kvb, o_c=o_c,
                 lse=lse, cat=cat, mix=mix, h1=h1, gu=gu, act=act, ffn=ffn, pg=pg, pp=pp, ng=ng, lg=lg, wts=wts,
                 lbias=lbias, w_s=w_s, bias_full=bias_full, qg=qg, kvg=kvg, g1=g1, b1=b1, g2=g2, b2=b2)
    return tuple(out), saved


RS_EARLY = ("ple_w_proj", "ple_w_gate", "w_down", "w_gate_up", "w_out")
RS_LATE = ("mla_w_uq", "mla_w_ukv", "w_in")


def _layer_backward(li, dh2_parts, p_i, sv, lbs, tables, alpha, core, carried=None):
    n = f"l{li}_b_"
    wts = sv["wts"]
    gr = {}
    dh1_a, dffn, dpg, dpp, gr["ln2_g"], gr["ln2_b"] = _rowwise_vjp(
        _make_ple_ln(alpha), [sv["h1"], sv["ffn"], sv["pg"], sv["pp"]], [sv["g2"], sv["b2"]], [dh2_parts],
        groups=[[0], [1], [2], [3]], gdtypes=[F32, BF16, BF16, BF16], name=n + "ln2")
    big = {}
    big["ple_w_proj"] = _mm(p_i, dpp, am="km", tk=2048, name=n + "ple_proj_dw")
    big["ple_w_gate"] = _mm(sv["h1b"], dpg, am="km", name=n + "ple_gate_dw")
    dh1_b = _mm(dpg, wts["ple_w_gate"], bm="nk", name=n + "ple_gate_dx")
    big["w_down"] = _mm(sv["act"], dffn, am="bkm", tk=4096, name=n + "down_dw")
    dgu = _down_dx_swiglu(dffn, wts["w_down"], sv["gu"], name=n + "down_dx")
    dgu = dgu.reshape((N_DEV,) + dgu.shape[2:])
    big["w_gate_up"], carried_got = _mm(sv["h1b"], dgu, am="km", bm="bkn", om="bmn", tk=4096, name=n + "gate_up_dw",
                                        job=carried), None
    if carried is not None:
        big["w_gate_up"], carried_got = big["w_gate_up"]
    early = [_as_pairs(k, big[k]) for k in RS_EARLY[:-1]]
    dh1_c, theirs = _mm_kblocks(dgu, wts["w_gate_up"], bm="bnk", tm=512, name=n + "gate_up_dx",
                                job=_pair_job(early))
    dh_a, dmix, gr["ln1_g"], gr["ln1_b"] = _rowwise_vjp(
        _make_post_mix(alpha), [sv["h"], sv["mix"]], [sv["g1"], sv["b1"]], [[dh1_a, dh1_b, dh1_c]],
        groups=[[0], [1]], gdtypes=[F32, BF16], name=n + "ln1")
    big["w_out"] = _mm(sv["cat"], dmix, am="km", name=n + "out_proj_dw")
    early.append(_as_pairs("w_out", big["w_out"]))
    dcat, their_w_out = _mm(dmix, wts["w_out"], bm="nk", name=n + "out_proj_dx", job=_pair_job(early[-1:]))
    sums = [_pair_add(x, r, core, name=n + "pair_add_" + k)
            for k, x, r in zip(RS_EARLY, early, list(theirs) + list(their_w_out))]

    (dqr, dkv, dkf), early_quads = _attn_bwd(sv["qr"], sv["kf"], sv["kvb"], dcat, sv["o_c"], sv["lse"],
                                             name=n + "attn", job=_quad_job(sums))
    dqpad, dkr = _mla_prep_bwd(dqr, dkf, tables, name=n + "mla_prep")
    big["mla_w_uq"] = _mm(sv["cqn"], dqpad, am="km", tk=2048, name=n + "uq_dw")
    dcqn = _mm(dqpad, wts["mla_w_uq"], bm="nk", name=n + "uq_dx")
    big["mla_w_ukv"] = _mm(sv["ckvn"], dkv, am="km", tk=2048, name=n + "ukv_dw")
    dckvn = _mm(dkv, wts["mla_w_ukv"], bm="nk", name=n + "ukv_dx")
    projp = sv["projp"]
    dcq, gr["mla_q_norm_g"] = _rowwise_vjp(_fn_rms, [(projp, 384, P_CQ // 384)], [sv["qg"]], [[dcqn]],
                                           groups=[[0]], gdtypes=[BF16], tile=2048, name=n + "q_norm")
    dckv, gr["mla_kv_norm_g"] = _rowwise_vjp(_fn_rms, [(projp, 256, P_CKV // 256)], [sv["kvg"]], [[dckvn]],
                                             groups=[[0]], gdtypes=[BF16], tile=2048, name=n + "kv_norm")
    dsgu, gr["sgu_ln_g"], gr["sgu_ln_b"], gr["sgu_w_s"], gr["sgu_b_s"] = _sgu_bwd(
        projp, sv["lg"], sv["lbias"], sv["w_s"], sv["bias_full"], dcat, name=n + "sgu")
    dhg, gr["hgrn_norm_g"], gr["lower_bound"] = _hgrn_bwd(
        projp, lbs[li], sv["ng"], sv["o_pre"], sv["states"], dcat, name=n + "hgrn")
    dprojp = jnp.concatenate([dhg, dsgu, dcq, dkr, dckv], axis=1)
    big["w_in"] = _unplace_w_in(_mm(sv["hb"], dprojp, am="km", tk=4096, name=n + "proj_dw"),
                                name=n + "proj_dw_shards")
    late = [_as_pairs(k, big[k]) for k in RS_LATE]
    dh_b, theirs = _mm(dprojp, wts["w_in"], bm="nk", tk=P_COLS, name=n + "proj_dx", job=_pair_job(late))
    late_sums = [_pair_add(x, r, core, name=n + "pair_add_" + k) for k, x, r in zip(RS_LATE, late, theirs)]
    return [dh_a, dh_b], gr, early_quads, late_sums, carried_got


def kernel(x, p, positions, ln_in_g, ln_in_b, w_in, hgrn_lb_logits, hgrn_norm_g, sgu_ln_g, sgu_ln_b, sgu_w_s, sgu_b_s, mla_q_norm_g, mla_w_uq, mla_kv_norm_g, mla_w_ukv, w_out, ln1_g, ln1_b, w_gate_up, w_down, ple_w_gate, ple_w_proj, ln2_g, ln2_b, loss_target, m_ln_in_g, m_ln_in_b, m_w_in, m_hgrn_lb_logits, m_hgrn_norm_g, m_sgu_ln_g, m_sgu_ln_b, m_sgu_w_s, m_sgu_b_s, m_mla_q_norm_g, m_mla_w_uq, m_mla_kv_norm_g, m_mla_w_ukv, m_w_out, m_ln1_g, m_ln1_b, m_w_gate_up, m_w_down, m_ple_w_gate, m_ple_w_proj, m_ln2_g, m_ln2_b, v_ln_in_g, v_ln_in_b, v_w_in, v_hgrn_lb_logits, v_hgrn_norm_g, v_sgu_ln_g, v_sgu_ln_b, v_sgu_w_s, v_sgu_b_s, v_mla_q_norm_g, v_mla_w_uq, v_mla_kv_norm_g, v_mla_w_ukv, v_w_out, v_ln1_g, v_ln1_b, v_w_gate_up, v_w_down, v_ple_w_gate, v_ple_w_proj, v_ln2_g, v_ln2_b):
    args = dict(locals())
    w = {k: args[k] for k in ORDER}
    m = {k: args["m_" + k] for k in ORDER}
    v = {k: args["v_" + k] for k in ORDER}
    depth = w_in.shape[0]
    assert depth == 2, "the lower-bound kernel is written for two layers"
    alpha = (2 * depth) ** 0.25
    xs, tgt = x[0], loss_target[0]
    d_model = xs.shape[1]

    shards = [_weight_shards(w, li) for li in range(depth)]
    on_hgrn0 = ("mla_w_uq", "mla_w_ukv", "w_out", "ple_w_gate", "ple_w_proj")
    ffn0 = ("w_gate_up", "w_down")
    first1 = ("w_in", "mla_w_uq", "mla_w_ukv", "w_out")
    on_attn1 = ("w_gate_up", "w_down", "ple_w_gate", "ple_w_proj")
    layer1_first = {}

    def after_hgrn0(got):
        return _usable_weights(dict(zip(on_hgrn0, got)), name="l0")

    def after_attn0(got):
        layer1_first.update(_usable_weights(dict(zip(first1, got[len(ffn0):])), name="l1"))
        return _usable_weights(dict(zip(ffn0, got[:len(ffn0)])), name="l0")

    def after_attn1(got):
        return _usable_weights(dict(zip(on_attn1, got)), name="l1")

    tables = _rope_tables(positions[0])
    row1 = lambda a: a.reshape(1, -1)
    l0, l1 = row1(hgrn_lb_logits[0]), row1(hgrn_lb_logits[1])
    lbs = _rowwise(_fn_lower_bounds, [l0, l1], [], [(HG_W, F32), (HG_W, F32)], name="lower_bounds")

    gin, bin_ = row1(ln_in_g), row1(ln_in_b)
    (h, hb), g_in = _rowwise(_twice(_fn_ln), [xs], [gin, bin_], [(d_model, F32), (d_model, BF16)], name="ln_in",
                             job=_gather_job([shards[0]["w_in"]]))
    w_in0 = _usable_weights({"w_in": _gather_forward(g_in, name="gather_l0_w_in_forward")[0]}, name="l0")
    (h, hb), sv0 = _layer_forward(
        0, h, hb, p[0, 0], w_in0, w, lbs, tables, alpha,
        hgrn_job=_gather_job([shards[0][k] for k in on_hgrn0]), after_hgrn=after_hgrn0,
        attn_job=_gather_job([shards[0][k] for k in ffn0] + [shards[1][k] for k in first1]), after_attn=after_attn0)
    (dy, loss_local), sv1 = _layer_forward(
        1, h, hb, p[1, 0], layer1_first, w, lbs, tables, alpha,
        attn_job=_gather_job([shards[1][k] for k in on_attn1]), after_attn=after_attn1, loss_target=tgt)
    saved = [sv0, sv1]
    loss = lax.psum(loss_local[0, 0], ("x", "y", "c"))

    core = lax.axis_index("c").astype(jnp.int32).reshape(1)
    dparts, grads, quads, carried = [dy], [None] * depth, [None] * depth, None
    for li in reversed(range(depth)):
        dparts, grads[li], early_quads, late_sums, late_quads = _layer_backward(
            li, dparts, p[li, 0], saved[li], lbs, tables, alpha, core, carried=carried)
        quads[li] = dict(zip(RS_EARLY, early_quads))
        if carried is not None:
            quads[li + 1].update(zip(RS_LATE, late_quads))
        carried = _quad_job(late_sums)
    (dx, d_gin, d_bin), late_quads = _rowwise_vjp(_fn_ln, [xs], [gin, bin_], [dparts], groups=[[0]], name="ln_in_b",
                                                   job=carried)
    quads[0].update(zip(RS_LATE, late_quads))
    dl0, dl1 = _rowwise_vjp(_fn_lower_bounds, [l0, l1], [], [[grads[0]["lower_bound"]], [grads[1]["lower_bound"]]],
                            groups=[[0], [1]], name="lower_bounds_b")

    prefixes = ("grad_", "delta_", "new_m_", "new_v_")
    uq_pad = ((0, 0), (0, 0), (0, LANES - ATT_D))
    state = {k: ((jnp.pad(w[k], uq_pad), jnp.pad(m[k], uq_pad), jnp.pad(v[k], uq_pad)) if k == "mla_w_uq"
                 else (w[k], m[k], v[k])) for k in BIG}
    out = {}
    for k in BIG:
        res4 = None
        for li in range(depth):
            res4 = _adamw(quads[li][k], *state[k], li, name=f"adamw_l{li}_{k}", into=res4)
        for pre, a in zip(prefixes, res4):
            out[pre + k] = a[:, :, :ATT_D] if k == "mla_w_uq" else a

    small_g = {"ln_in_g": d_gin.reshape(-1), "ln_in_b": d_bin.reshape(-1),
               "hgrn_lb_logits": jnp.stack([dl0.reshape(-1), dl1.reshape(-1)])}
    for k in SMALL[3:]:
        small_g[k] = jnp.stack([grads[li][k].reshape(w[k].shape[1:]) for li in range(depth)])
    (small_parts,) = _all_gather([_pack([small_g[k] for k in SMALL])], name="gather_small_grads", columns=False)
    slabs = _adamw(small_parts, _pack([w[k] for k in SMALL]), _pack([m[k] for k in SMALL]),
                   _pack([v[k] for k in SMALL]), None, name="adamw_small")
    shapes = [w[k].shape for k in SMALL]
    for pre, slab in zip(prefixes, slabs):
        for k, a in zip(SMALL, _unpack(slab, shapes)):
            out[pre + k] = a
    res = [loss, dx[None]]
    for prefix in ("grad_", "delta_", "new_m_", "new_v_"):
        res += [out[prefix + k] for k in ORDER]
    return tuple(res)
```

```python
import functools
import math

import jax
import jax.numpy as jnp
import numpy as np
from jax import lax
from jax.experimental import pallas as pl
from jax.experimental.pallas import tpu as pltpu

F32 = jnp.float32
BF16 = jnp.bfloat16
MESH = pl.DeviceIdType.MESH

LN_EPS = 1e-5
RMS_EPS = 1e-6
ROPE_THETA = 10000.0
ADAM_LR, ADAM_B1, ADAM_B2, ADAM_EPS, ADAM_WD, ADAM_STEP = 0.001, 0.9, 0.999, 1e-08, 0.01, 10

N_DEV = 8
LANES = 128
HG_CHUNK = 16
HG_W = 256
HEAD = 64
SGU_CHUNK = 128
SGU_STEP_CHUNKS = 4
N_ATT_HEADS = 8
ATT_D = 96
VMEM_LIMIT = 56 * 1024 * 1024

HG_TILE = 256
ATT_TQ = 512
ROW_TILE = 512

P_CQ, P_KR, P_CKV, P_COLS = 1536, 1920, 2048, 2304


def _cparams(sem):
    return pltpu.CompilerParams(dimension_semantics=sem, vmem_limit_bytes=VMEM_LIMIT)


_ANY = pl.BlockSpec(memory_space=pl.ANY)


def _call(body, operands, *, name, grid, in_specs, out_specs, out_shape, sem, scratch_shapes=(), job=None):
    if job is None:
        return pl.pallas_call(body, name=name, grid=grid, in_specs=in_specs, out_specs=out_specs, out_shape=out_shape,
                              scratch_shapes=list(scratch_shapes), compiler_params=_cparams(sem))(*operands)
    single = not isinstance(out_shape, (list, tuple))
    shapes = [out_shape] if single else list(out_shape)
    ospecs = [out_specs] if single else list(out_specs)
    ni, no, ns = len(operands), len(shapes), len(scratch_shapes)
    ji, jo = len(job.inputs), len(job.out_shapes)

    def hosted(*refs):
        p = 0
        parts = []
        for cnt in (ni, ji, no, jo, ns):
            parts.append(refs[p:p + cnt])
            p += cnt
        ins, jins, outs, jouts, scr = parts
        jsems = refs[p:]
        ids = [pl.program_id(a) for a in range(len(grid))]
        first = functools.reduce(lambda a, b: a & b, [i == 0 for i in ids])
        last = functools.reduce(lambda a, b: a & b, [i == g - 1 for i, g in zip(ids, grid)])

        @pl.when(first)
        def _():
            job.start(jins, jouts, jsems)

        body(*ins, *outs, *scr)

        @pl.when(last)
        def _():
            job.finish(jins, jouts, jsems)

    res = pl.pallas_call(
        hosted, name=name, grid=grid,
        in_specs=list(in_specs) + [_ANY] * ji, out_specs=ospecs + [_ANY] * jo,
        out_shape=shapes + list(job.out_shapes),
        scratch_shapes=list(scratch_shapes) + [pltpu.SemaphoreType.DMA((c,)) for c in job.sem_counts],
        input_output_aliases=job.aliases(ni, no),
        compiler_params=_cparams(("arbitrary",) * len(grid)),
    )(*operands, *job.inputs)
    own = res[0] if single else res[:no]
    return own, res[no:]


class _Job:
    def __init__(self, inputs, out_shapes, sem_counts, start, finish, in_place=False):
        self.inputs, self.out_shapes, self.sem_counts = list(inputs), list(out_shapes), list(sem_counts)
        self.start, self.finish, self.in_place = start, finish, in_place

    def aliases(self, first_in, first_out):
        return {first_in + i: first_out + i for i in range(len(self.inputs))} if self.in_place else {}


def _copies_job(inputs, out_shapes, n_remote, n_local, make, in_place=False):
    def start(jins, jouts, sems):
        sends, _, local = make(jins, jouts, *sems)
        for cp in local + sends:
            cp.start()

    def finish(jins, jouts, sems):
        sends, recvs, local = make(jins, jouts, *sems)
        for cp in recvs:
            cp.wait_recv()
        for cp in sends:
            cp.wait_send()
        for cp in local:
            cp.wait()

    return _Job(inputs, out_shapes, [n_remote, n_remote, max(n_local, 1)], start, finish, in_place)


def _run_job(job, *, name):
    ji, jo = len(job.inputs), len(job.out_shapes)

    def body(*refs):
        jins, jouts, sems = refs[:ji], refs[ji:ji + jo], refs[ji + jo:]
        job.start(jins, jouts, sems)
        job.finish(jins, jouts, sems)

    return pl.pallas_call(
        body, name=name, out_shape=list(job.out_shapes), in_specs=[_ANY] * ji, out_specs=[_ANY] * jo,
        scratch_shapes=[pltpu.SemaphoreType.DMA((c,)) for c in job.sem_counts],
        input_output_aliases=job.aliases(0, 0),
    )(*job.inputs)


def _tile(n, pref):
    if n % pref == 0:
        return pref
    best = None
    t = LANES
    while t <= min(n, pref):
        if n % t == 0:
            best = t
        t += LANES
    return best if best is not None else n


def _mm(a, b, *, am="mk", bm="kn", om="mn", out_dtype=F32, tm=1024, tn=1024, tk=1024, name, job=None):
    if am == "mk":
        m, k = a.shape
    elif am == "km":
        k, m = a.shape
    elif am == "bmk":
        m, tk = a.shape[1], a.shape[2]
        k = a.shape[0] * tk
    else:
        k, tm = a.shape[1], a.shape[2]
        m = a.shape[0] * tm
    if bm == "kn":
        kb_, n = b.shape
    elif bm == "nk":
        n, kb_ = b.shape
    elif bm == "bkn":
        kb_, tn = b.shape[1], b.shape[2]
        n = b.shape[0] * tn
    else:
        n, tk = b.shape[1], b.shape[2]
        kb_ = b.shape[0] * tk
    assert kb_ == k, (a.shape, b.shape, am, bm)
    tm, tn, tk = _tile(m, tm), _tile(n, tn), _tile(k, tk)
    nk = k // tk
    dims = (((0 if am in ("km", "bkm") else 1,), (1 if bm in ("nk", "bnk") else 0,)), ((), ()))

    a_spec = {"mk": pl.BlockSpec((tm, tk), lambda i, j, kk: (i, kk)),
              "km": pl.BlockSpec((tk, tm), lambda i, j, kk: (kk, i)),
              "bmk": pl.BlockSpec((None, tm, tk), lambda i, j, kk: (kk, i, 0)),
              "bkm": pl.BlockSpec((None, tk, tm), lambda i, j, kk: (i, kk, 0))}[am]
    b_spec = {"kn": pl.BlockSpec((tk, tn), lambda i, j, kk: (kk, j)),
              "nk": pl.BlockSpec((tn, tk), lambda i, j, kk: (j, kk)),
              "bkn": pl.BlockSpec((None, tk, tn), lambda i, j, kk: (j, kk, 0)),
              "bnk": pl.BlockSpec((None, tn, tk), lambda i, j, kk: (kk, j, 0))}[bm]
    if om == "mn":
        o_spec, o_shape = pl.BlockSpec((tm, tn), lambda i, j, kk: (i, j)), (m, n)
    else:
        o_spec, o_shape = pl.BlockSpec((None, tm, tn), lambda i, j, kk: (j, i, 0)), (n // tn, m, tn)

    def body(a_ref, b_ref, o_ref, *acc):
        kk = pl.program_id(2)

        def prod():
            return lax.dot_general(a_ref[...].astype(BF16), b_ref[...].astype(BF16), dims, preferred_element_type=F32)

        if nk == 1:
            o_ref[...] = prod().astype(o_ref.dtype)
            return
        acc_ref, = acc

        @pl.when(kk == 0)
        def _():
            acc_ref[...] = prod()

        if nk > 2:
            @pl.when((kk > 0) & (kk < nk - 1))
            def _():
                acc_ref[...] += prod()

        @pl.when(kk == nk - 1)
        def _():
            o_ref[...] = (acc_ref[...] + prod()).astype(o_ref.dtype)

    return _call(body, (a, b), name=name, grid=(m // tm, n // tn, nk), in_specs=[a_spec, b_spec], out_specs=o_spec,
                 out_shape=jax.ShapeDtypeStruct(o_shape, out_dtype),
                 scratch_shapes=[pltpu.VMEM((tm, tn), F32)] if nk > 1 else [],
                 sem=("parallel", "parallel", "arbitrary"), job=job)


def _mm_kblocks(a, b, *, bm, tm, name, job=None):
    nkb, m, kb = a.shape
    n = b.shape[1]
    tm = _tile(m, tm)

    def body(a_ref, b_ref, o_ref):
        acc = None
        for j in range(nkb):
            if bm == "kn":
                part = jnp.dot(a_ref[j], b_ref[j * kb:(j + 1) * kb, :], preferred_element_type=F32)
            else:
                part = lax.dot_general(a_ref[j], b_ref[j], _NT, preferred_element_type=F32)
            acc = part if acc is None else acc + part
        o_ref[...] = acc

    b_spec = (pl.BlockSpec(b.shape, lambda i: (0, 0)) if bm == "kn" else pl.BlockSpec(b.shape, lambda i: (0, 0, 0)))
    return _call(body, (a, b), name=name, grid=(m // tm,),
                 in_specs=[pl.BlockSpec((nkb, tm, kb), lambda i: (0, i, 0)), b_spec],
                 out_specs=pl.BlockSpec((tm, n), lambda i: (i, 0)), out_shape=jax.ShapeDtypeStruct((m, n), F32),
                 sem=("parallel",), job=job)


def _row_operand(a, tile):
    if isinstance(a, tuple):
        arr, w, j = a
        return arr, pl.BlockSpec((tile, w), lambda i, j=j: (i, j))
    return a, pl.BlockSpec((tile, a.shape[1]), lambda i: (i, 0))


def _const_spec(c):
    nd = c.ndim
    return pl.BlockSpec(c.shape, lambda i, nd=nd: (0,) * nd)


def _rowwise(fn, rows, consts, outs, *, name, accs=(), tile=None, job=None):
    t_rows = (rows[0][0] if isinstance(rows[0], tuple) else rows[0]).shape[0]
    tile = min(tile or ROW_TILE, t_rows)
    arrs, specs = zip(*[_row_operand(a, tile) for a in rows])
    nin, no = len(rows) + len(consts), len(outs)

    def body(*refs):
        res = fn(*[r[...] for r in refs[:nin]])
        for r, v in zip(refs[nin:nin + no], res[:no]):
            r[...] = v.astype(r.dtype)
        if accs:
            a_refs = refs[nin + no:]

            @pl.when(pl.program_id(0) == 0)
            def _():
                for r in a_refs:
                    r[...] = jnp.zeros_like(r)

            for r, v in zip(a_refs, res[no:]):
                r[...] += v

    out_shape = [jax.ShapeDtypeStruct((t_rows, w), dt) for w, dt in outs]
    out_shape += [jax.ShapeDtypeStruct(s, F32) for s in accs]
    out_specs = [pl.BlockSpec((tile, w), lambda i: (i, 0)) for w, _ in outs]
    out_specs += [pl.BlockSpec(s, lambda i, nd=len(s): (0,) * nd) for s in accs]
    return _call(body, (*arrs, *consts), name=name, grid=(t_rows // tile,),
                 in_specs=list(specs) + [_const_spec(c) for c in consts],
                 out_specs=out_specs, out_shape=out_shape, sem=("arbitrary",), job=job)


def _rowwise_vjp(fn, rows, consts, cts, *, name, groups, tile=None, gdtypes=None, job=None):
    t_rows = (rows[0][0] if isinstance(rows[0], tuple) else rows[0]).shape[0]
    tile = min(tile or ROW_TILE, t_rows)
    arrs, specs = zip(*[_row_operand(a, tile) for a in rows])
    flat_cts = [c for group in cts for c in group]
    ct_arrs, ct_specs = zip(*[_row_operand(a, tile) for a in flat_cts])
    nr, nc, nct, ng = len(rows), len(consts), len(flat_cts), len(groups)

    def width(a):
        return a[1] if isinstance(a, tuple) else a.shape[1]

    def body(*refs):
        rv = [r[...].astype(F32) for r in refs[:nr]]
        cv = [r[...] for r in refs[nr:nr + nc]]
        ct_refs = refs[nr + nc:nr + nc + nct]
        ctv, pos = [], 0
        for group in cts:
            s = ct_refs[pos][...].astype(F32)
            for r in ct_refs[pos + 1:pos + len(group)]:
                s = s + r[...].astype(F32)
            ctv.append(s)
            pos += len(group)
        _, pull = jax.vjp(fn, *rv, *cv)
        grads = pull(tuple(ctv))
        g_refs = refs[nr + nc + nct:nr + nc + nct + ng]
        for r, idx in zip(g_refs, groups):
            parts = [grads[i] for i in idx]
            r[...] = (parts[0] if len(parts) == 1 else jnp.concatenate(parts, axis=1)).astype(r.dtype)
        c_refs = refs[nr + nc + nct + ng:]

        @pl.when(pl.program_id(0) == 0)
        def _():
            for r in c_refs:
                r[...] = jnp.zeros_like(r)

        for r, v in zip(c_refs, grads[nr:]):
            r[...] += v

    gw = [sum(width(rows[i]) for i in idx) for idx in groups]
    gdtypes = gdtypes or [F32] * ng
    out_shape = [jax.ShapeDtypeStruct((t_rows, w), dt) for w, dt in zip(gw, gdtypes)]
    out_shape += [jax.ShapeDtypeStruct(c.shape, F32) for c in consts]
    out_specs = [pl.BlockSpec((tile, w), lambda i: (i, 0)) for w in gw]
    out_specs += [_const_spec(c) for c in consts]
    return _call(body, (*arrs, *consts, *ct_arrs), name=name, grid=(t_rows // tile,),
                 in_specs=list(specs) + [_const_spec(c) for c in consts] + list(ct_specs),
                 out_specs=out_specs, out_shape=out_shape, sem=("arbitrary",), job=job)


def _layer_norm(x, g, b):
    mu = jnp.mean(x, axis=-1, keepdims=True)
    xc = x - mu
    var = jnp.mean(xc * xc, axis=-1, keepdims=True)
    return xc * lax.rsqrt(var + LN_EPS) * g + b


def _sigmoid(x):
    return 1.0 / (1.0 + jnp.exp(-x))


def _fn_ln(x, g, b):
    return (_layer_norm(x, g, b),)


def _fn_rms(x, g):
    return (x * lax.rsqrt(jnp.mean(x * x, axis=-1, keepdims=True) + RMS_EPS) * g,)


def _make_post_mix(alpha):
    def fn(h, mix, g, b):
        return (_layer_norm(alpha * h + mix, g, b),)
    return fn


def _make_ple_ln(alpha):
    def fn(h1, ffn, pg, pp, g, b):
        return (_layer_norm(alpha * h1 + ffn + _sigmoid(pg) * pp, g, b),)
    return fn


def _fn_lower_bounds(l0, l1):
    m = jnp.maximum(l0, l1)
    e0, e1 = jnp.exp(l0 - m), jnp.exp(l1 - m)
    s = e0 + e1
    p0, p1 = e0 / s, e1 / s
    return (p0 - p0, (p0 + p1) - p0)


def _split_dot(x, e_bf16):
    hi = x.astype(BF16)
    lo = (x - hi.astype(F32)).astype(BF16)
    return (jnp.dot(hi, e_bf16, preferred_element_type=F32) + jnp.dot(lo, e_bf16, preferred_element_type=F32))


def _hgrn_common(th):
    rm = lax.broadcasted_iota(jnp.int32, (th, HG_W), 0) % HG_CHUNK

    def seg_cumsum(x):
        for s in (1, 2, 4, 8):
            x = x + jnp.where(rm >= s, pltpu.roll(x, s, 0), 0.0)
        return x

    def seg_rcumsum(x):
        for s in (1, 2, 4, 8):
            x = x + jnp.where(rm < HG_CHUNK - s, pltpu.roll(x, th - s, 0), 0.0)
        return x

    ri = lax.broadcasted_iota(jnp.int32, (HG_W, HG_W), 0) // HEAD
    ci = lax.broadcasted_iota(jnp.int32, (HG_W, HG_W), 1) // HEAD
    head_f32 = (ri == ci).astype(F32)
    head_bf16 = head_f32.astype(BF16)

    def headsum(x, pieces=2):
        if pieces == 1:
            return jnp.dot(x.astype(BF16), head_bf16, preferred_element_type=F32)
        return _split_dot(x, head_bf16)

    return rm, seg_cumsum, seg_rcumsum, head_f32, headsum


def _hgrn_gates(qr, fl, lb):
    sg = _sigmoid(fl)
    f = lb + (1.0 - lb) * sg
    sq = _sigmoid(qr)
    return sg, f, jnp.log(f), 1.0 - f, qr * sq, sq


def _shifted(x, d, th):
    return x if d == 0 else pltpu.roll(x, d, 0)


def _unshift(x, d, th):
    return x if d == 0 else pltpu.roll(x, th - d, 0)


def _hgrn_fwd(projp, lb, ng, *, name, job=None):
    t_rows = projp.shape[0]
    th = min(HG_TILE, t_rows)
    nct = th // HG_CHUNK

    def body(q_ref, f_ref, i_ref, g_ref, lb_ref, ng_ref, oa_ref, opre_ref, st_out_ref,
             st_ref, vtm_ref, kv_ref, qe_ref, dec_ref, oint_ref):
        rm, seg_cumsum, seg_rcumsum, head_f32, headsum = _hgrn_common(th)

        @pl.when(pl.program_id(0) == 0)
        def _():
            st_ref[...] = jnp.zeros_like(st_ref)

        qr, fl, v, g = q_ref[...], f_ref[...], i_ref[...], g_ref[...]
        _, f, lf, k, q, _ = _hgrn_gates(qr, fl, lb_ref[...])
        b = seg_cumsum(lf)

        o = jnp.zeros((th, HG_W), F32)
        for d in range(HG_CHUNK):
            kd, bd, vd = _shifted(k, d, th), _shifted(b, d, th), _shifted(v, d, th)
            e = jnp.exp(jnp.where(rm >= d, b - bd, -1e30))
            o = o + headsum(q * kd * e, 1) * vd

        blast = seg_rcumsum(jnp.where(rm == HG_CHUNK - 1, b, 0.0))
        kte = (k * jnp.exp(blast - b)).astype(BF16)
        qe_ref[...] = q * jnp.exp(b)
        dec_ref[...] = jnp.exp(blast)
        vt = v.T
        lane_chunk = lax.broadcasted_iota(jnp.int32, (HG_W, th), 1) // HG_CHUNK
        for c in range(nct):
            vtm_ref[c * HG_W:(c + 1) * HG_W, :] = jnp.where(lane_chunk == c, vt, 0.0).astype(BF16)
        kv_ref[...] = jnp.dot(vtm_ref[...], kte, preferred_element_type=F32)

        s = st_ref[...]
        for c in range(nct):
            rows = slice(c * HG_CHUNK, (c + 1) * HG_CHUNK)
            st_out_ref[c] = s
            oint_ref[rows, :] = lax.dot_general(qe_ref[rows, :].astype(BF16), s.astype(BF16),
                                                (((1,), (1,)), ((), ())), preferred_element_type=F32)
            dec = jnp.max(dec_ref[rows, :], axis=0, keepdims=True)
            s = s * dec + kv_ref[c * HG_W:(c + 1) * HG_W, :] * head_f32
        st_ref[...] = s

        o = o + oint_ref[...]
        opre_ref[...] = o
        r = lax.rsqrt(headsum(o * o) * (1.0 / HEAD) + RMS_EPS)
        oa_ref[...] = (o * r * ng_ref[...] * (g * _sigmoid(g))).astype(oa_ref.dtype)

    col = lambda j: pl.BlockSpec((th, HG_W), lambda i, j=j: (i, j))
    vec = pl.BlockSpec((1, HG_W), lambda i: (0, 0))
    row = pl.BlockSpec((th, HG_W), lambda i: (i, 0))
    n_chunks = t_rows // HG_CHUNK
    return _call(
        body, (projp, projp, projp, projp, lb, ng), name=name, grid=(t_rows // th,),
        in_specs=[col(0), col(1), col(2), col(3), vec, vec],
        out_specs=[row, row, pl.BlockSpec((nct, HG_W, HG_W), lambda i: (i, 0, 0))],
        out_shape=[jax.ShapeDtypeStruct((t_rows, HG_W), BF16), jax.ShapeDtypeStruct((t_rows, HG_W), F32),
                   jax.ShapeDtypeStruct((n_chunks, HG_W, HG_W), F32)],
        scratch_shapes=[pltpu.VMEM((HG_W, HG_W), F32), pltpu.VMEM((nct * HG_W, th), BF16),
                        pltpu.VMEM((nct * HG_W, HG_W), F32), pltpu.VMEM((th, HG_W), F32),
                        pltpu.VMEM((th, HG_W), F32), pltpu.VMEM((th, HG_W), F32)],
        sem=("arbitrary",), job=job)


def _hgrn_bwd(projp, lb, ng, opre, states, dcat, *, name):
    t_rows = projp.shape[0]
    th = min(HG_TILE, t_rows)
    nct = th // HG_CHUNK
    nt = t_rows // th

    def body(q_ref, f_ref, i_ref, g_ref, lb_ref, ng_ref, opre_ref, st_in_ref, do_ref,
             dproj_ref, dng_ref, dlb_ref,
             gst_ref, dotm_ref, qg_ref, v_ref, kte_ref, dop_ref, dec_ref, dkte_ref, dvi_ref, dqe_ref, ddec_ref):
        rm, seg_cumsum, seg_rcumsum, head_f32, headsum = _hgrn_common(th)

        @pl.when(pl.program_id(0) == 0)
        def _():
            gst_ref[...] = jnp.zeros_like(gst_ref)
            dng_ref[...] = jnp.zeros_like(dng_ref)
            dlb_ref[...] = jnp.zeros_like(dlb_ref)

        qr, fl, v, g = q_ref[...], f_ref[...], i_ref[...], g_ref[...]
        lb, ngv = lb_ref[...], ng_ref[...]
        sg, f, lf, k, q, sq = _hgrn_gates(qr, fl, lb)
        b = seg_cumsum(lf)
        blast = seg_rcumsum(jnp.where(rm == HG_CHUNK - 1, b, 0.0))
        eb = jnp.exp(b)
        ekb = jnp.exp(blast - b)
        qe, kte, dec = q * eb, k * ekb, jnp.exp(blast)

        do_out, op = do_ref[...], opre_ref[...]
        sgg = _sigmoid(g)
        sil = g * sgg
        r = lax.rsqrt(headsum(op * op) * (1.0 / HEAD) + RMS_EPS)
        on = op * r
        dng_ref[...] += jnp.sum(do_out * on * sil, axis=0, keepdims=True)
        dg = do_out * on * ngv * (sgg * (1.0 + g * (1.0 - sgg)))
        don = do_out * ngv * sil
        dop = r * (don - on * (headsum(don * on) * (1.0 / HEAD)))

        v_ref[...] = v
        kte_ref[...] = kte
        dop_ref[...] = dop
        dec_ref[...] = dec
        dot_t = dop.T
        lane_chunk = lax.broadcasted_iota(jnp.int32, (HG_W, th), 1) // HG_CHUNK
        for c in range(nct):
            dotm_ref[c * HG_W:(c + 1) * HG_W, :] = jnp.where(lane_chunk == c, dot_t, 0.0).astype(BF16)
        qg_ref[...] = jnp.dot(dotm_ref[...], qe.astype(BF16), preferred_element_type=F32)

        gs = gst_ref[...]
        for c in reversed(range(nct)):
            rows = slice(c * HG_CHUNK, (c + 1) * HG_CHUNK)
            s = st_in_ref[c]
            gm = (gs * head_f32).astype(BF16)
            dkte_ref[rows, :] = jnp.dot(v_ref[rows, :].astype(BF16), gm, preferred_element_type=F32)
            dvi_ref[rows, :] = lax.dot_general(kte_ref[rows, :].astype(BF16), gm, (((1,), (1,)), ((), ())),
                                               preferred_element_type=F32)
            dqe_ref[rows, :] = jnp.dot(dop_ref[rows, :].astype(BF16), s.astype(BF16), preferred_element_type=F32)
            ddec_ref[rows, :] = jnp.broadcast_to(jnp.sum(gs * s, axis=0, keepdims=True), (HG_CHUNK, HG_W))
            dec_c = jnp.max(dec_ref[rows, :], axis=0, keepdims=True)
            gs = gs * dec_c + qg_ref[c * HG_W:(c + 1) * HG_W, :] * head_f32
        gst_ref[...] = gs

        dkte, dqe = dkte_ref[...], dqe_ref[...]
        dblast = dkte * kte + jnp.where(rm == HG_CHUNK - 1, ddec_ref[...] * dec, 0.0)
        dq_ref, dk_ref, db_ref = dqe_ref, dkte_ref, ddec_ref
        db_ref[...] = dqe * qe - dkte * kte
        dq_ref[...] = dqe * eb
        dk_ref[...] = dkte * ekb
        kte_ref[...] = dblast

        for d in range(HG_CHUNK):
            kd, bd, vd = _shifted(k, d, th), _shifted(b, d, th), _shifted(v_ref[...], d, th)
            e = jnp.exp(jnp.where(rm >= d, b - bd, -1e30))
            p = q * kd * e
            sc = headsum(p, 1)
            dsc = headsum(dop_ref[...] * vd, 1)
            dvi_ref[...] += _unshift(sc * dop_ref[...], d, th)
            dq_ref[...] += dsc * kd * e
            dk_ref[...] += _unshift(dsc * q * e, d, th)
            darg = dsc * p
            db_ref[...] += darg - _unshift(darg, d, th)

        dq, dk, dv = dq_ref[...], dk_ref[...], dvi_ref[...]
        db = db_ref[...] + jnp.where(rm == HG_CHUNK - 1, seg_cumsum(kte_ref[...]), 0.0)
        dlf = seg_rcumsum(db)
        df = dlf / f - dk
        dlb_ref[...] += jnp.sum(df * (1.0 - sg), axis=0, keepdims=True)
        dfl = df * (1.0 - lb) * sg * (1.0 - sg)
        dqr = dq * (sq * (1.0 + qr * (1.0 - sq)))
        dproj_ref[...] = jnp.concatenate([dqr, dfl, dv, dg], axis=1).astype(dproj_ref.dtype)

    rev = lambda i: nt - 1 - i
    col = lambda j: pl.BlockSpec((th, HG_W), lambda i, j=j: (rev(i), j))
    vec = pl.BlockSpec((1, HG_W), lambda i: (0, 0))
    row = pl.BlockSpec((th, HG_W), lambda i: (rev(i), 0))
    tile_f32 = pltpu.VMEM((th, HG_W), F32)
    return pl.pallas_call(
        body, name=name, grid=(nt,),
        in_specs=[col(0), col(1), col(2), col(3), vec, vec, row,
                  pl.BlockSpec((nct, HG_W, HG_W), lambda i: (rev(i), 0, 0)), col(0)],
        out_specs=[pl.BlockSpec((th, 4 * HG_W), lambda i: (rev(i), 0)), vec, vec],
        out_shape=[jax.ShapeDtypeStruct((t_rows, 4 * HG_W), BF16), jax.ShapeDtypeStruct((1, HG_W), F32),
                   jax.ShapeDtypeStruct((1, HG_W), F32)],
        scratch_shapes=[pltpu.VMEM((HG_W, HG_W), F32), pltpu.VMEM((nct * HG_W, th), BF16),
                        pltpu.VMEM((nct * HG_W, HG_W), F32)] + [tile_f32] * 8,
        compiler_params=_cparams(("arbitrary",)),
    )(projp, projp, projp, projp, lb, ng, opre, states, dcat)


_INV_SQRT2 = 1.0 / math.sqrt(2.0)
_INV_SQRT2PI = 1.0 / math.sqrt(2.0 * math.pi)


def _gelu(x):
    return 0.5 * x * (1.0 + lax.erf(x * _INV_SQRT2))


def _gelu_grad(x):
    return 0.5 * (1.0 + lax.erf(x * _INV_SQRT2)) + x * jnp.exp(-0.5 * x * x) * _INV_SQRT2PI


def _sgu_parts(bu, bv, lg, lbias, w_ref, n_groups):
    c = SGU_CHUNK
    tril = (lax.broadcasted_iota(jnp.int32, (c, c), 0) >= lax.broadcasted_iota(jnp.int32, (c, c), 1)).astype(F32)
    gid = lax.broadcasted_iota(jnp.int32, bu.shape, 1) // HEAD
    u = _gelu(bu)
    gv = _gelu(bv)
    mu = jnp.mean(gv, axis=-1, keepdims=True)
    xc = gv - mu
    rstd = lax.rsqrt(jnp.mean(xc * xc, axis=-1, keepdims=True) + LN_EPS)
    xhat = xc * rstd
    vn = xhat * lg + lbias
    ws = [w_ref[gi] * tril for gi in range(n_groups)]
    return tril, gid, u, rstd, xhat, vn, ws


def _sgu_fwd(projp, lg, lbias, w_s, bias_full, *, name, job=None):
    t_rows = projp.shape[0]
    n_groups = w_s.shape[0]
    c = SGU_CHUNK
    rows_per_step = min(SGU_STEP_CHUNKS * c, t_rows)

    def body(u_ref, v_ref, lg_ref, lb_ref, w_ref, bias_ref, o_ref):
        for r0 in range(0, rows_per_step, c):
            rows = slice(r0, r0 + c)
            _, gid, u, _, _, vn, ws = _sgu_parts(u_ref[rows, :], v_ref[rows, :], lg_ref[...], lb_ref[...], w_ref,
                                                 n_groups)
            vnb = vn.astype(BF16)
            z = bias_ref[...]
            for gi in range(n_groups):
                z = z + jnp.where(gid == gi, jnp.dot(ws[gi].astype(BF16), vnb, preferred_element_type=F32), 0.0)
            o_ref[rows, :] = (u * z).astype(o_ref.dtype)

    col = lambda j: pl.BlockSpec((rows_per_step, HG_W), lambda i, j=j: (i, j))
    return _call(
        body, (projp, projp, lg, lbias, w_s, bias_full), name=name, grid=(t_rows // rows_per_step,),
        in_specs=[col(4), col(5), _const_spec(lg), _const_spec(lbias), _const_spec(w_s), _const_spec(bias_full)],
        out_specs=pl.BlockSpec((rows_per_step, HG_W), lambda i: (i, 0)),
        out_shape=jax.ShapeDtypeStruct((t_rows, HG_W), BF16), sem=("arbitrary",), job=job)


def _sgu_bwd(projp, lg, lbias, w_s, bias_full, dcat, *, name):
    t_rows = projp.shape[0]
    n_groups = w_s.shape[0]
    c = SGU_CHUNK
    rows_per_step = min(SGU_STEP_CHUNKS * c, t_rows)
    n = t_rows // rows_per_step

    def body(u_ref, v_ref, lg_ref, lb_ref, w_ref, bias_ref, do_ref,
             dproj_ref, dlg_ref, dlb_ref, dw_ref, dbs_ref, dbias_acc):
        i = pl.program_id(0)

        @pl.when(i == 0)
        def _():
            dlg_ref[...] = jnp.zeros_like(dlg_ref)
            dlb_ref[...] = jnp.zeros_like(dlb_ref)
            dw_ref[...] = jnp.zeros_like(dw_ref)
            dbias_acc[...] = jnp.zeros_like(dbias_acc)

        lg_v = lg_ref[...]
        for r0 in range(0, rows_per_step, c):
            rows = slice(r0, r0 + c)
            bu, bv = u_ref[rows, :], v_ref[rows, :]
            tril, gid, u, rstd, xhat, vn, ws = _sgu_parts(bu, bv, lg_v, lb_ref[...], w_ref, n_groups)
            vnb = vn.astype(BF16)
            z = bias_ref[...]
            for gi in range(n_groups):
                z = z + jnp.where(gid == gi, jnp.dot(ws[gi].astype(BF16), vnb, preferred_element_type=F32), 0.0)
            do = do_ref[rows, :]
            dbu = do * z * _gelu_grad(bu)
            dz = do * u
            dbias_acc[...] += dz
            dvn = jnp.zeros_like(dz)
            for gi in range(n_groups):
                dzg = jnp.where(gid == gi, dz, 0.0).astype(BF16)
                dw_ref[gi] += lax.dot_general(dzg, vnb, (((1,), (1,)), ((), ())), preferred_element_type=F32) * tril
                dvn = dvn + jnp.dot(ws[gi].T.astype(BF16), dzg, preferred_element_type=F32)
            dlg_ref[...] += jnp.sum(dvn * xhat, axis=0, keepdims=True)
            dlb_ref[...] += jnp.sum(dvn, axis=0, keepdims=True)
            dxh = dvn * lg_v
            dgv = rstd * (dxh - jnp.mean(dxh, axis=-1, keepdims=True)
                          - xhat * jnp.mean(dxh * xhat, axis=-1, keepdims=True))
            dproj_ref[rows, :] = jnp.concatenate([dbu, dgv * _gelu_grad(bv)], axis=1).astype(dproj_ref.dtype)

        @pl.when(i == n - 1)
        def _():
            dbs_ref[...] = jnp.sum(dbias_acc[...].T.reshape(n_groups, HEAD, c), axis=1)

    col = lambda j: pl.BlockSpec((rows_per_step, HG_W), lambda i, j=j: (i, j))
    return pl.pallas_call(
        body, name=name, grid=(n,),
        in_specs=[col(4), col(5), _const_spec(lg), _const_spec(lbias), _const_spec(w_s), _const_spec(bias_full),
                  col(1)],
        out_specs=[pl.BlockSpec((rows_per_step, 2 * HG_W), lambda i: (i, 0)), _const_spec(lg), _const_spec(lbias),
                   _const_spec(w_s), pl.BlockSpec((n_groups, c), lambda i: (0, 0))],
        out_shape=[jax.ShapeDtypeStruct((t_rows, 2 * HG_W), BF16), jax.ShapeDtypeStruct(lg.shape, F32),
                   jax.ShapeDtypeStruct(lbias.shape, F32), jax.ShapeDtypeStruct(w_s.shape, F32),
                   jax.ShapeDtypeStruct((n_groups, c), F32)],
        scratch_shapes=[pltpu.VMEM((c, HG_W), F32)],
        compiler_params=_cparams(("arbitrary",)),
    )(projp, projp, lg, lbias, w_s, bias_full, dcat)


def _rope_tables(positions):
    t = positions.shape[0]
    inv_freq = ROPE_THETA ** (-jnp.arange(0, 32, 2, dtype=F32) / 32)
    ang = positions.astype(F32)[:, None] * inv_freq
    cos, sin = jnp.cos(ang), jnp.sin(ang)
    z = lambda w: jnp.zeros((t, w), F32)
    cos_t = jnp.concatenate([jnp.ones((t, 64), F32), cos, cos, z(32)], axis=1)
    sin_up = jnp.concatenate([z(80), sin, z(32)], axis=1)
    sin_dn = jnp.concatenate([z(64), -sin, z(48)], axis=1)
    return cos_t, sin_up, sin_dn


def _rep(x, n):
    return x if n == 1 else jnp.concatenate([x] * n, axis=1)


def _rope(x, cos_t, sin_up, sin_dn):
    w = x.shape[1]
    return x * cos_t + pltpu.roll(x, 16, 1) * sin_up + pltpu.roll(x, w - 16, 1) * sin_dn


def _rope_t(dy, cos_t, sin_up, sin_dn):
    w = dy.shape[1]
    return dy * cos_t + pltpu.roll(dy * sin_up, w - 16, 1) + pltpu.roll(dy * sin_dn, 16, 1)


def _mla_prep(q, kv, projp, tables, *, name):
    nh = N_ATT_HEADS

    def fn(qv, kvv, kr, cos_t, sin_up, sin_dn):
        qr = _rope(qv, _rep(cos_t, nh), _rep(sin_up, nh), _rep(sin_dn, nh))
        krr = _rope(kr, cos_t, sin_up, sin_dn)
        lane = lax.broadcasted_iota(jnp.int32, kvv.shape, 1) % LANES
        return qr, jnp.where(lane < HEAD, kvv, 0.0) + _rep(krr, nh), kvv

    w = q.shape[1]
    return _rowwise(fn, [q, kv, (projp, LANES, P_KR // LANES)] + list(tables), [],
                    [(w, BF16), (w, BF16), (w, BF16)], tile=1024, name=name)


def _mla_prep_bwd(dqr, dkf, tables, *, name):
    nh = N_ATT_HEADS

    def fn(dq, dk, cos_t, sin_up, sin_dn):
        dqp = _rope_t(dq, _rep(cos_t, nh), _rep(sin_up, nh), _rep(sin_dn, nh))
        dkrr = dk[:, 0:LANES]
        for h in range(1, nh):
            dkrr = dkrr + dk[:, LANES * h:LANES * (h + 1)]
        return dqp, _rope_t(dkrr, cos_t, sin_up, sin_dn)

    return _rowwise(fn, [dqr, dkf] + list(tables), [], [(dqr.shape[1], BF16), (LANES, BF16)], tile=1024, name=name)


_LOG2E = 1.0 / math.log(2.0)
_NT = (((1,), (1,)), ((), ()))
_TN = (((0,), (0,)), ((), ()))


def _attn_fwd(qr, kf, kvb, *, name, job=None):
    t_rows = qr.shape[0]
    tq = min(ATT_TQ, t_rows)
    nb = t_rows // tq
    scale = ATT_D ** -0.5

    c2 = scale * _LOG2E

    def body(q_ref, kf_ref, kvb_ref, o_ref, lse_ref):
        qi = pl.program_id(1)
        lane = lax.broadcasted_iota(jnp.int32, (tq, LANES), 1)
        causal_t = (lax.broadcasted_iota(jnp.int32, (tq, tq), 0) <= lax.broadcasted_iota(jnp.int32, (tq, tq), 1))
        heads = [slice(hh * LANES, (hh + 1) * LANES) for hh in range(2)]
        qs = [q_ref[:, cols] for cols in heads]

        def block(first, n_keys, carry, diagonal):
            rows = pl.ds(pl.multiple_of(first * tq, tq), n_keys)
            new = []
            for q, cols, (m_old, l_old, acc_t) in zip(qs, heads, carry):
                s_t = lax.dot_general(kf_ref[rows, cols], q, _NT, preferred_element_type=F32)
                if diagonal:
                    s_t = jnp.where(causal_t, s_t, -1e30)
                m_new = jnp.maximum(m_old, jnp.max(s_t, axis=0, keepdims=True))
                p_t = jnp.exp2((s_t - m_new) * c2)
                a = jnp.exp2((m_old - m_new) * c2)
                pv_t = lax.dot_general(kvb_ref[rows, cols], p_t.astype(BF16), _TN, preferred_element_type=F32)
                new.append((m_new, a * l_old + jnp.sum(p_t, axis=0, keepdims=True), a * acc_t + pv_t))
            return tuple(new)

        init = (jnp.full((1, tq), -1e30, F32), jnp.zeros((1, tq), F32), jnp.zeros((LANES, tq), F32))
        carry = lax.fori_loop(0, qi // 4, lambda g, c: block(4 * g, 4 * tq, c, False), (init, init))
        carry = lax.cond((qi // 2) % 2 == 1, lambda c: block(4 * (qi // 4), 2 * tq, c, False), lambda c: c, carry)
        carry = lax.cond(qi % 2 == 1, lambda c: block(qi - 1, tq, c, False), lambda c: c, carry)
        outs = []
        for hh, (m_fin, l_fin, acc_t) in enumerate(block(qi, tq, carry, True)):
            lse_ref[hh] = m_fin * scale + jnp.log(l_fin)
            outs.append((acc_t / l_fin).T)
        o_ref[...] = jnp.where(lane < HEAD, pltpu.roll(outs[0], HEAD, 1), outs[1])

    pair = pl.BlockSpec((t_rows, 2 * LANES), lambda pr, qi: (0, pr))
    return _call(
        body, (qr, kf, kvb), name=name, grid=(N_ATT_HEADS // 2, nb),
        in_specs=[pl.BlockSpec((tq, 2 * LANES), lambda pr, qi: (qi, pr)), pair, pair],
        out_specs=[pl.BlockSpec((tq, LANES), lambda pr, qi: (qi, pr)),
                   pl.BlockSpec((2, 1, tq), lambda pr, qi: (pr, 0, qi))],
        out_shape=[jax.ShapeDtypeStruct((t_rows, N_ATT_HEADS * HEAD), F32),
                   jax.ShapeDtypeStruct((N_ATT_HEADS, 1, t_rows), F32)],
        sem=("parallel", "arbitrary"), job=job)


def _attn_bwd(qr, kf, kvb, dcat, o, lse, *, name, job=None):
    t_rows = qr.shape[0]
    tq = min(ATT_TQ, t_rows)
    nb = t_rows // tq
    scale = ATT_D ** -0.5
    c2 = scale * _LOG2E
    do_off = 2 * HG_W // LANES

    def body(q_ref, kf_ref, kvb_ref, do_ref, o_ref, lse_ref, dq_ref, dkv_ref, dk_ref):
        ki = pl.program_id(1)

        @pl.when(ki == 0)
        def _():
            dq_ref[...] = jnp.zeros_like(dq_ref)

        lane = lax.broadcasted_iota(jnp.int32, (tq, LANES), 1)
        causal_t = (lax.broadcasted_iota(jnp.int32, (tq, tq), 0) <= lax.broadcasted_iota(jnp.int32, (tq, tq), 1))
        heads = [slice(hh * LANES, (hh + 1) * LANES) for hh in range(2)]
        ks = [kf_ref[:, cols] for cols in heads]
        vs = [kvb_ref[:, cols] for cols in heads]

        def block(qi, n_q, carry, diagonal):
            rows = pl.ds(pl.multiple_of(qi * tq, tq), n_q)
            do_pair, o_pair = do_ref[rows, :], o_ref[rows, :]
            upper = lax.broadcasted_iota(jnp.int32, do_pair.shape, 1) >= HEAD
            new = []
            for hh, (cols, k, v, (dk, dv)) in enumerate(zip(heads, ks, vs, carry)):
                q = q_ref[rows, cols]
                do, ov = (pltpu.roll(do_pair, HEAD, 1), pltpu.roll(o_pair, HEAD, 1)) if hh == 0 else (do_pair, o_pair)
                do = jnp.where(upper, do, 0.0)
                delta = jnp.sum((do * ov).T, axis=0, keepdims=True)
                s_t = lax.dot_general(k, q, _NT, preferred_element_type=F32)
                if diagonal:
                    s_t = jnp.where(causal_t, s_t, -1e30)
                p_t = jnp.exp2(s_t * c2 - lse_ref[hh, :, rows] * _LOG2E)
                dob = do.astype(BF16)
                dv = dv + jnp.dot(p_t.astype(BF16), dob, preferred_element_type=F32)
                dp_t = lax.dot_general(v, dob, _NT, preferred_element_type=F32)
                ds_t = (p_t * (dp_t - delta) * scale).astype(BF16)
                dk = dk + jnp.dot(ds_t, q, preferred_element_type=F32)
                dq_ref[rows, cols] += lax.dot_general(ds_t, k, _TN, preferred_element_type=F32)
                new.append((dk, dv))
            return tuple(new)

        zero = jnp.zeros((tq, LANES), F32)
        carry = block(ki, tq, ((zero, zero), (zero, zero)), True)
        rest = nb - 1 - ki
        carry = lax.fori_loop(0, rest // 2, lambda g, c: block(ki + 1 + 2 * g, 2 * tq, c, False), carry)
        carry = lax.cond(rest % 2 == 1, lambda c: block(nb - 1, tq, c, False), lambda c: c, carry)
        dkv_ref[...] = jnp.concatenate([jnp.where(lane < HEAD, dk, dv) for dk, dv in carry],
                                       axis=1).astype(dkv_ref.dtype)
        dk_ref[...] = jnp.concatenate([dk for dk, _ in carry], axis=1)

    pair_all = pl.BlockSpec((t_rows, 2 * LANES), lambda pr, ki: (0, pr))
    pair_blk = pl.BlockSpec((tq, 2 * LANES), lambda pr, ki: (ki, pr))
    wide = jax.ShapeDtypeStruct((t_rows, N_ATT_HEADS * LANES), F32)
    return _call(
        body, (qr, kf, kvb, dcat, o, lse), name=name, grid=(N_ATT_HEADS // 2, nb),
        in_specs=[pair_all, pair_blk, pair_blk,
                  pl.BlockSpec((t_rows, LANES), lambda pr, ki: (0, do_off + pr)),
                  pl.BlockSpec((t_rows, LANES), lambda pr, ki: (0, pr)),
                  pl.BlockSpec((2, 1, t_rows), lambda pr, ki: (pr, 0, 0))],
        out_specs=[pair_all, pair_blk, pair_blk],
        out_shape=[wide, jax.ShapeDtypeStruct(wide.shape, BF16), wide],
        sem=("parallel", "arbitrary"), job=job)


def _my_pos():
    return lax.axis_index("x"), lax.axis_index("y"), lax.axis_index("c")


def _all_gather(xs, *, name, columns=True):
    return _gather_forward(_run_job(_gather_job(xs, columns), name=name), name=name + "_forward")


def _remote(src, dst, send_sems, recv_sems, k, dev):
    return pltpu.make_async_remote_copy(src_ref=src, dst_ref=dst, send_sem=send_sems.at[k], recv_sem=recv_sems.at[k],
                                        device_id=dev, device_id_type=MESH)


def _block(ref, idx):
    if len(ref.shape) == 2:
        return ref.at[:, pl.ds(pl.multiple_of(idx * LANES, LANES), LANES)]
    return ref.at[idx]


def _gather_job(xs, columns=True):
    n = len(xs)

    def make(x_refs, out_refs, send_sems, recv_sems, local_sems):
        mx, my, mc = _my_pos()
        mine = 4 * mx + 2 * my + mc
        peers = [(mx, my, 1 - mc), (1 - mx, my, mc), (mx, 1 - my, mc), (1 - mx, 1 - my, mc)]
        sends, recvs, local = [], [], []
        for a in range(n):
            local.append(pltpu.make_async_copy(x_refs[a], _block(out_refs[a], mine), local_sems.at[a]))
            for k, dev in enumerate(peers):
                theirs = 4 * dev[0] + 2 * dev[1] + dev[2]
                sends.append(_remote(x_refs[a], _block(out_refs[a], mine), send_sems, recv_sems, 4 * a + k, dev))
                recvs.append(_remote(x_refs[a], _block(out_refs[a], theirs), send_sems, recv_sems, 4 * a + k, dev))
        return sends, recvs, local

    def gathered(x):
        if columns and x.ndim == 2 and x.shape[1] == LANES:
            return jax.ShapeDtypeStruct((x.shape[0], N_DEV * LANES), x.dtype)
        return jax.ShapeDtypeStruct((N_DEV,) + x.shape, x.dtype)

    return _copies_job(xs, [gathered(x) for x in xs], 4 * n, n, make)


def _forward_job(gs):
    n = len(gs)

    def make(in_refs, out_refs, send_sems, recv_sems, local_sems):
        mx, my, mc = _my_pos()
        chips = [(1 - mx, my), (mx, 1 - my), (1 - mx, 1 - my)]
        sends, recvs = [], []
        for a in range(n):
            for j, (cx, cy) in enumerate(chips):
                here = _block(out_refs[a], 4 * cx + 2 * cy + mc)
                there = _block(out_refs[a], 4 * cx + 2 * cy + 1 - mc)
                sends.append(_remote(here, here, send_sems, recv_sems, 3 * a + j, (mx, my, 1 - mc)))
                recvs.append(_remote(here, there, send_sems, recv_sems, 3 * a + j, (mx, my, 1 - mc)))
        return sends, recvs, []

    shapes = [jax.ShapeDtypeStruct(g.shape, g.dtype) for g in gs]
    return _copies_job(gs, shapes, 3 * n, 0, make, in_place=True)


def _gather_forward(gs, *, name):
    return _run_job(_forward_job(gs), name=name)


def _pair_job(xs):
    n = len(xs)

    def make(x_refs, out_refs, send_sems, recv_sems, local_sems):
        mx, my, mc = _my_pos()

        def src(ref, g):
            return _block(ref, 2 * g + 1 - mc) if len(ref.shape) == 2 else ref.at[g, 1 - mc]

        copies = [_remote(src(x_refs[a], g), out_refs[a].at[g], send_sems, recv_sems, 4 * a + g, (mx, my, 1 - mc))
                  for a in range(n) for g in range(4)]
        return copies, copies, []

    shapes = [jax.ShapeDtypeStruct((4, x.shape[0], LANES) if x.ndim == 2 else (4,) + x.shape[2:], x.dtype)
              for x in xs]
    return _copies_job(xs, shapes, 4 * n, 0, make)


def _pair_add(x, r, core, *, name):
    _, a, b = r.shape
    ta = _row_tile(a, 512)

    def body(c_ref, x_ref, r_ref, o_ref):
        o_ref[...] = (x_ref[...] + r_ref[...]).astype(o_ref.dtype)

    blk = pl.BlockSpec((None, ta, b), lambda g, i, c_ref: (g, i, 0))
    own = (pl.BlockSpec((ta, b), lambda g, i, c_ref: (i, 2 * g + c_ref[0])) if x.ndim == 2
           else pl.BlockSpec((None, None, ta, b), lambda g, i, c_ref: (g, c_ref[0], i, 0)))
    return pl.pallas_call(
        body, name=name,
        grid_spec=pltpu.PrefetchScalarGridSpec(
            num_scalar_prefetch=1, grid=(4, a // ta), in_specs=[own, blk], out_specs=blk),
        out_shape=jax.ShapeDtypeStruct((4, a, b), BF16),
        compiler_params=_cparams(("parallel", "parallel")),
    )(core, x, r)


def _quad_job(xs):
    n = len(xs)

    def make(x_refs, out_refs, send_sems, recv_sems, local_sems):
        mx, my, mc = _my_pos()
        mine = 2 * mx + my
        peers = [((1 - mx, my, mc), 2 * (1 - mx) + my), ((mx, 1 - my, mc), 2 * mx + 1 - my),
                 ((1 - mx, 1 - my, mc), 2 * (1 - mx) + 1 - my)]
        sends, recvs, local = [], [], []
        for a in range(n):
            local.append(pltpu.make_async_copy(x_refs[a].at[mine], out_refs[a].at[mine], local_sems.at[a]))
            for k, (dev, g) in enumerate(peers):
                sends.append(_remote(x_refs[a].at[g], out_refs[a].at[mine], send_sems, recv_sems, 3 * a + k, dev))
                recvs.append(_remote(x_refs[a].at[g], out_refs[a].at[g], send_sems, recv_sems, 3 * a + k, dev))
        return sends, recvs, local

    shapes = [jax.ShapeDtypeStruct(x.shape, x.dtype) for x in xs]
    return _copies_job(xs, shapes, 3 * n, n, make)


def _row_tile(r, pref):
    t = min(pref, r)
    while r % t or (t % 8 and t != r):
        t -= 1
    return t


def _adamw(parts, w, m, v, layer, *, name, tile=512, into=None):
    g, a, b = parts.shape
    tile = _row_tile(a, tile)
    c1 = 1.0 / (1.0 - ADAM_B1 ** ADAM_STEP)
    c2 = 1.0 / (1.0 - ADAM_B2 ** ADAM_STEP)
    into = tuple(into or ())

    def body(p_ref, w_ref, m_ref, v_ref, *refs):
        g_ref, d_ref, mo_ref, vo_ref = refs[len(into):]
        grad = p_ref[0].astype(F32)
        for j in range(1, g):
            grad = grad + p_ref[j].astype(F32)
        mn = ADAM_B1 * m_ref[...] + (1.0 - ADAM_B1) * grad
        vn = ADAM_B2 * v_ref[...] + (1.0 - ADAM_B2) * (grad * grad)
        g_ref[...] = grad
        mo_ref[...] = mn
        vo_ref[...] = vn
        d_ref[...] = -ADAM_LR * ((mn * c1) / (jnp.sqrt(vn * c2) + ADAM_EPS) + ADAM_WD * w_ref[...])

    if layer is None:
        src, shape = pl.BlockSpec((tile, b), lambda i: (i, 0)), (a, b)
    else:
        src, shape = pl.BlockSpec((None, tile, b), lambda i: (layer, i, 0)), w.shape
    return pl.pallas_call(
        body, name=name, grid=(a // tile,),
        in_specs=[pl.BlockSpec((g, tile, b), lambda i: (0, i, 0)), src, src, src] + [_ANY] * len(into),
        out_specs=[src] * 4,
        out_shape=[jax.ShapeDtypeStruct(shape, F32)] * 4,
        input_output_aliases={4 + i: i for i in range(len(into))},
        compiler_params=_cparams(("parallel",)),
    )(parts, w, m, v, *into)


W_IN_SHARD = 276


def _w_in_dest(col):
    return jnp.where(col < P_KR, col, jnp.where(col < P_KR + 256, col + (P_CKV - P_KR), col - 2176 + P_KR + HEAD))


PLACE_TILE = 384
PLACE_SHARDS = 3
PICK_TILE = 128
PICK_TILES = 4


def _w_in_tables():
    col = np.arange(N_DEV * W_IN_SHARD)
    dest = np.where(col < P_KR, col, np.where(col < P_KR + 256, col + (P_CKV - P_KR), col - 2176 + P_KR + HEAD))
    shard = col // W_IN_SHARD

    def filled(used, universe, n):
        used = sorted(set(int(u) for u in used))
        assert len(used) <= n, used
        return used + [u for u in universe if u not in used][:n - len(used)]

    place = [filled(shard[dest // PLACE_TILE == c], range(N_DEV), PLACE_SHARDS) for c in range(P_COLS // PLACE_TILE)]
    pick = [filled(dest[shard == j] // PICK_TILE, range(P_COLS // PICK_TILE), PICK_TILES) for j in range(N_DEV)]
    return np.asarray(place, np.int32).reshape(-1), np.asarray(pick, np.int32).reshape(-1)


def _place_w_in(g, *, name):
    _, d, sh = g.shape
    tc, ns = PLACE_TILE, PLACE_SHARDS
    table = jnp.asarray(_w_in_tables()[0])

    def body(tab_ref, g_ref, o_ref, acc_ref):
        ct, s = pl.program_id(0), pl.program_id(1)
        j = tab_ref[ct * ns + s]

        @pl.when(s == 0)
        def _():
            acc_ref[...] = jnp.zeros_like(acc_ref)

        src = j * sh + lax.broadcasted_iota(jnp.int32, (sh, tc), 0)
        dst = ct * tc + lax.broadcasted_iota(jnp.int32, (sh, tc), 1)
        place = (_w_in_dest(src) == dst).astype(BF16)
        acc_ref[...] += jnp.dot(g_ref[...], place, preferred_element_type=F32)

        @pl.when(s == ns - 1)
        def _():
            o_ref[...] = acc_ref[...].astype(o_ref.dtype)

    return pl.pallas_call(
        body, name=name,
        grid_spec=pltpu.PrefetchScalarGridSpec(
            num_scalar_prefetch=1, grid=(P_COLS // tc, ns),
            in_specs=[pl.BlockSpec((None, d, sh), lambda ct, s, tab: (tab[ct * ns + s], 0, 0))],
            out_specs=pl.BlockSpec((d, tc), lambda ct, s, tab: (0, ct)),
            scratch_shapes=[pltpu.VMEM((d, tc), F32)]),
        out_shape=jax.ShapeDtypeStruct((d, P_COLS), BF16),
        compiler_params=_cparams(("parallel", "arbitrary")),
    )(table, g)


def _unplace_w_in(dw, *, name):
    d = dw.shape[0]
    sh, tk, nt = W_IN_SHARD, PICK_TILE, PICK_TILES
    table = jnp.asarray(_w_in_tables()[1])

    def body(tab_ref, dw_ref, o_ref):
        j, kk = pl.program_id(0), pl.program_id(1)
        tile = tab_ref[j * nt + kk]
        src = j * sh + lax.broadcasted_iota(jnp.int32, (tk, sh), 1)
        dst = tile * tk + lax.broadcasted_iota(jnp.int32, (tk, sh), 0)
        pick = (_w_in_dest(src) == dst).astype(BF16)
        part = _split_dot(dw_ref[...], pick)

        @pl.when(kk == 0)
        def _():
            o_ref[...] = part

        @pl.when(kk > 0)
        def _():
            o_ref[...] += part

    return pl.pallas_call(
        body, name=name,
        grid_spec=pltpu.PrefetchScalarGridSpec(
            num_scalar_prefetch=1, grid=(N_DEV, nt),
            in_specs=[pl.BlockSpec((d, tk), lambda j, kk, tab: (0, tab[j * nt + kk]))],
            out_specs=pl.BlockSpec((None, d, sh), lambda j, kk, tab: (j, 0, 0))),
        out_shape=jax.ShapeDtypeStruct((N_DEV, d, sh), F32),
        compiler_params=_cparams(("parallel", "arbitrary")),
    )(table, dw)


def _gate_up_swiglu(h1, wgu, *, name):
    t_rows, k = h1.shape
    w = wgu.shape[2]
    tm = _tile(t_rows, 1024)

    def body(a_ref, wg_ref, wu_ref, gu_ref, act_ref):
        a = a_ref[...].astype(BF16)
        gate = jnp.dot(a, wg_ref[...], preferred_element_type=F32)
        up = jnp.dot(a, wu_ref[...], preferred_element_type=F32)
        gu_ref[0] = gate.astype(gu_ref.dtype)
        gu_ref[1] = up.astype(gu_ref.dtype)
        act_ref[...] = (gate * _sigmoid(gate) * up).astype(act_ref.dtype)

    return pl.pallas_call(
        body, name=name, grid=(t_rows // tm, 4),
        in_specs=[pl.BlockSpec((tm, k), lambda i, j: (i, 0)),
                  pl.BlockSpec((None, k, w), lambda i, j: (j, 0, 0)),
                  pl.BlockSpec((None, k, w), lambda i, j: (j + 4, 0, 0))],
        out_specs=[pl.BlockSpec((2, None, tm, w), lambda i, j: (0, j, i, 0)),
                   pl.BlockSpec((None, tm, w), lambda i, j: (j, i, 0))],
        out_shape=[jax.ShapeDtypeStruct((2, 4, t_rows, w), BF16), jax.ShapeDtypeStruct((4, t_rows, w), BF16)],
        compiler_params=_cparams(("parallel", "arbitrary")),
    )(h1, wgu, wgu)


def _down_dx_swiglu(dffn, wdown, gu, *, name):
    t_rows, k = dffn.shape
    w = gu.shape[3]
    tm = _tile(t_rows, 1024)

    def body(d_ref, w_ref, gu_ref, o_ref):
        dact = lax.dot_general(d_ref[...].astype(BF16), w_ref[...], _NT, preferred_element_type=F32)
        gate, up = gu_ref[0].astype(F32), gu_ref[1].astype(F32)
        sg = _sigmoid(gate)
        silu = gate * sg
        o_ref[0] = (dact * up * (sg + silu - silu * sg)).astype(o_ref.dtype)
        o_ref[1] = (dact * silu).astype(o_ref.dtype)

    blk = pl.BlockSpec((2, None, tm, w), lambda i, j: (0, j, i, 0))
    return pl.pallas_call(
        body, name=name, grid=(t_rows // tm, 4),
        in_specs=[pl.BlockSpec((tm, k), lambda i, j: (i, 0)), pl.BlockSpec((w, k), lambda i, j: (j, 0)), blk],
        out_specs=blk, out_shape=jax.ShapeDtypeStruct(gu.shape, BF16),
        compiler_params=_cparams(("parallel", "arbitrary")),
    )(dffn, wdown, gu)


BIG = ("w_in", "mla_w_uq", "mla_w_ukv", "w_out", "w_gate_up", "w_down", "ple_w_gate", "ple_w_proj")
SMALL = ("ln_in_g", "ln_in_b", "hgrn_lb_logits", "hgrn_norm_g", "sgu_ln_g", "sgu_ln_b", "sgu_w_s", "sgu_b_s",
         "mla_q_norm_g", "mla_kv_norm_g", "ln1_g", "ln1_b", "ln2_g", "ln2_b")
ORDER = ("ln_in_g", "ln_in_b", "w_in", "hgrn_lb_logits", "hgrn_norm_g", "sgu_ln_g", "sgu_ln_b", "sgu_w_s", "sgu_b_s",
         "mla_q_norm_g", "mla_w_uq", "mla_kv_norm_g", "mla_w_ukv", "w_out", "ln1_g", "ln1_b", "w_gate_up", "w_down",
         "ple_w_gate", "ple_w_proj", "ln2_g", "ln2_b")


def _slab(a, align):
    s = a.reshape(-1, LANES)
    pad = -s.shape[0] % align
    return jnp.pad(s, ((0, pad), (0, 0))) if pad else s


def _pack(arrays, align=16, total_align=512):
    s = jnp.concatenate([_slab(a, align) for a in arrays], axis=0)
    pad = -s.shape[0] % total_align
    return jnp.pad(s, ((0, pad), (0, 0))) if pad else s


def _unpack(slab, shapes, align=16):
    out, r0 = [], 0
    for s in shapes:
        nr = math.prod(s) // LANES
        out.append(slab[r0:r0 + nr].reshape(s))
        r0 += nr + (-nr % align)
    return out


def _weight_shards(w, li):
    uq_pad = ((0, 0), (0, LANES - ATT_D))
    shards = {k: w[k][li] for k in BIG}
    shards["mla_w_uq"] = jnp.pad(shards["mla_w_uq"], uq_pad)
    return {k: s.astype(BF16) for k, s in shards.items()}


def _usable_weights(g, *, name):
    out = {}
    for k, a in g.items():
        if k == "w_in":
            out[k] = _place_w_in(a, name=name + "_place_w_in")
        elif k in ("w_out", "w_down", "ple_w_gate"):
            out[k] = a.reshape(a.shape[0] * a.shape[1], a.shape[2])
        else:
            out[k] = a
    return out


BY_COLUMNS = ("mla_w_uq", "mla_w_ukv", "ple_w_proj")


def _as_pairs(k, g):
    if k in BY_COLUMNS:
        return g
    if g.ndim == 2:
        return g.reshape((4, 2, g.shape[0] // N_DEV) + g.shape[1:])
    return g.reshape((4, 2) + g.shape[1:])


def _twice(fn):
    return lambda *a: fn(*a) * 2


def _layer_forward(li, h, hb, p_i, wts, sm, lbs, tables, alpha, hgrn_job=None, after_hgrn=None, attn_job=None,
                   after_attn=None, loss_target=None):
    n = f"l{li}_"
    row1 = lambda a: a.reshape(1, -1)
    projp = _mm(hb, wts["w_in"], name=n + "proj")
    ng = row1(sm["hgrn_norm_g"][li])
    res = _hgrn_fwd(projp, lbs[li], ng, name=n + "hgrn_fwd", job=hgrn_job)
    if hgrn_job is not None:
        res, got = res
    o_a, o_pre, states = res
    lg, lbias = row1(sm["sgu_ln_g"][li]), row1(sm["sgu_ln_b"][li])
    w_s = sm["sgu_w_s"][li]
    bias_full = jnp.repeat(sm["sgu_b_s"][li].T, HEAD, axis=1)
    o_b = _sgu_fwd(projp, lg, lbias, w_s, bias_full, name=n + "sgu_fwd",
                   job=None if hgrn_job is None else _forward_job(got))
    if hgrn_job is not None:
        o_b, got = o_b
        wts = dict(wts, **after_hgrn(got))
    qg, kvg = row1(sm["mla_q_norm_g"][li]), row1(sm["mla_kv_norm_g"][li])
    cq_view, ckv_view = (projp, 384, P_CQ // 384), (projp, 256, P_CKV // 256)
    (cqn,) = _rowwise(_fn_rms, [cq_view], [qg], [(384, BF16)], tile=2048, name=n + "q_norm")
    (ckvn,) = _rowwise(_fn_rms, [ckv_view], [kvg], [(256, BF16)], tile=2048, name=n + "kv_norm")
    q = _mm(cqn, wts["mla_w_uq"], name=n + "uq")
    kv = _mm(ckvn, wts["mla_w_ukv"], name=n + "ukv")
    qr, kf, kvb = _mla_prep(q, kv, projp, tables, name=n + "mla_prep")
    res = _attn_fwd(qr, kf, kvb, name=n + "attn_fwd", job=attn_job)
    if attn_job is not None:
        res, got = res
    o_c, lse = res
    cat = jnp.concatenate([o_a, o_b, o_c.astype(BF16)], axis=1)
    mix = _mm(cat, wts["w_out"], name=n + "out_proj", job=None if attn_job is None else _forward_job(got))
    if attn_job is not None:
        mix, got = mix
        wts = dict(wts, **after_attn(got))
    g1, b1 = row1(sm["ln1_g"][li]), row1(sm["ln1_b"][li])
    d = h.shape[1]
    h1, h1b = _rowwise(_twice(_make_post_mix(alpha)), [h, mix], [g1, b1], [(d, F32), (d, BF16)], tile=1024,
                       name=n + "ln1")
    gu, act = _gate_up_swiglu(h1b, wts["w_gate_up"], name=n + "gate_up")
    ffn = _mm_kblocks(act, wts["w_down"], bm="kn", tm=1024, name=n + "down")
    pg = _mm(h1b, wts["ple_w_gate"], name=n + "ple_gate")
    pp = _mm(p_i, wts["ple_w_proj"], name=n + "ple_proj")
    g2, b2 = row1(sm["ln2_g"][li]), row1(sm["ln2_b"][li])
    if loss_target is None:
        out = _rowwise(_twice(_make_ple_ln(alpha)), [h1, ffn, pg, pp], [g2, b2], [(d, F32), (d, BF16)],
                       name=n + "ln2")
    else:
        def ln_and_loss(h1v, ffnv, pgv, ppv, tv, gv, bv):
            err = _make_ple_ln(alpha)(h1v, ffnv, pgv, ppv, gv, bv)[0] - tv
            return err * (1.0 / d), 0.5 * jnp.sum(jnp.mean(err * err, axis=-1, keepdims=True), axis=0, keepdims=True)

        out = _rowwise(ln_and_loss, [h1, ffn, pg, pp, loss_target], [g2, b2], [(d, F32)], accs=[(1, 1)],
                       name=n + "ln2_loss")
    saved = dict(h=h, hb=hb, h1b=h1b, projp=projp, o_pre=o_pre, states=states, cqn=cqn, ckvn=ckvn, qr=qr, kf=kf, kvb=kvb, o_c=o_c,
                 lse=lse, cat=cat, mix=mix, h1=h1, gu=gu, act=act, ffn=ffn, pg=pg, pp=pp, ng=ng, lg=lg, wts=wts,
                 lbias=lbias, w_s=w_s, bias_full=bias_full, qg=qg, kvg=kvg, g1=g1, b1=b1, g2=g2, b2=b2)
    return tuple(out), saved


RS_EARLY = ("ple_w_proj", "ple_w_gate", "w_down", "w_gate_up", "w_out")
RS_LATE = ("mla_w_uq", "mla_w_ukv", "w_in")


def _layer_backward(li, dh2_parts, p_i, sv, lbs, tables, alpha, core, carried=None):
    n = f"l{li}_b_"
    wts = sv["wts"]
    gr = {}
    dh1_a, dffn, dpg, dpp, gr["ln2_g"], gr["ln2_b"] = _rowwise_vjp(
        _make_ple_ln(alpha), [sv["h1"], sv["ffn"], sv["pg"], sv["pp"]], [sv["g2"], sv["b2"]], [dh2_parts],
        groups=[[0], [1], [2], [3]], gdtypes=[F32, BF16, BF16, BF16], name=n + "ln2")
    big = {}
    big["ple_w_proj"] = _mm(p_i, dpp, am="km", tk=2048, name=n + "ple_proj_dw")
    big["ple_w_gate"] = _mm(sv["h1b"], dpg, am="km", name=n + "ple_gate_dw")
    dh1_b = _mm(dpg, wts["ple_w_gate"], bm="nk", name=n + "ple_gate_dx")
    big["w_down"] = _mm(sv["act"], dffn, am="bkm", tk=4096, name=n + "down_dw")
    dgu = _down_dx_swiglu(dffn, wts["w_down"], sv["gu"], name=n + "down_dx")
    dgu = dgu.reshape((N_DEV,) + dgu.shape[2:])
    big["w_gate_up"], carried_got = _mm(sv["h1b"], dgu, am="km", bm="bkn", om="bmn", tk=4096, name=n + "gate_up_dw",
                                        job=carried), None
    if carried is not None:
        big["w_gate_up"], carried_got = big["w_gate_up"]
    early = [_as_pairs(k, big[k]) for k in RS_EARLY[:-1]]
    dh1_c, theirs = _mm_kblocks(dgu, wts["w_gate_up"], bm="bnk", tm=512, name=n + "gate_up_dx",
                                job=_pair_job(early))
    dh_a, dmix, gr["ln1_g"], gr["ln1_b"] = _rowwise_vjp(
        _make_post_mix(alpha), [sv["h"], sv["mix"]], [sv["g1"], sv["b1"]], [[dh1_a, dh1_b, dh1_c]],
        groups=[[0], [1]], gdtypes=[F32, BF16], name=n + "ln1")
    big["w_out"] = _mm(sv["cat"], dmix, am="km", name=n + "out_proj_dw")
    early.append(_as_pairs("w_out", big["w_out"]))
    dcat, their_w_out = _mm(dmix, wts["w_out"], bm="nk", name=n + "out_proj_dx", job=_pair_job(early[-1:]))
    sums = [_pair_add(x, r, core, name=n + "pair_add_" + k)
            for k, x, r in zip(RS_EARLY, early, list(theirs) + list(their_w_out))]

    (dqr, dkv, dkf), early_quads = _attn_bwd(sv["qr"], sv["kf"], sv["kvb"], dcat, sv["o_c"], sv["lse"],
                                             name=n + "attn", job=_quad_job(sums))
    dqpad, dkr = _mla_prep_bwd(dqr, dkf, tables, name=n + "mla_prep")
    big["mla_w_uq"] = _mm(sv["cqn"], dqpad, am="km", tk=2048, name=n + "uq_dw")
    dcqn = _mm(dqpad, wts["mla_w_uq"], bm="nk", name=n + "uq_dx")
    big["mla_w_ukv"] = _mm(sv["ckvn"], dkv, am="km", tk=2048, name=n + "ukv_dw")
    dckvn = _mm(dkv, wts["mla_w_ukv"], bm="nk", name=n + "ukv_dx")
    projp = sv["projp"]
    dcq, gr["mla_q_norm_g"] = _rowwise_vjp(_fn_rms, [(projp, 384, P_CQ // 384)], [sv["qg"]], [[dcqn]],
                                           groups=[[0]], gdtypes=[BF16], tile=2048, name=n + "q_norm")
    dckv, gr["mla_kv_norm_g"] = _rowwise_vjp(_fn_rms, [(projp, 256, P_CKV // 256)], [sv["kvg"]], [[dckvn]],
                                             groups=[[0]], gdtypes=[BF16], tile=2048, name=n + "kv_norm")
    dsgu, gr["sgu_ln_g"], gr["sgu_ln_b"], gr["sgu_w_s"], gr["sgu_b_s"] = _sgu_bwd(
        projp, sv["lg"], sv["lbias"], sv["w_s"], sv["bias_full"], dcat, name=n + "sgu")
    dhg, gr["hgrn_norm_g"], gr["lower_bound"] = _hgrn_bwd(
        projp, lbs[li], sv["ng"], sv["o_pre"], sv["states"], dcat, name=n + "hgrn")
    dprojp = jnp.concatenate([dhg, dsgu, dcq, dkr, dckv], axis=1)
    big["w_in"] = _unplace_w_in(_mm(sv["hb"], dprojp, am="km", tk=4096, name=n + "proj_dw"),
                                name=n + "proj_dw_shards")
    late = [_as_pairs(k, big[k]) for k in RS_LATE]
    dh_b, theirs = _mm(dprojp, wts["w_in"], bm="nk", tk=P_COLS, name=n + "proj_dx", job=_pair_job(late))
    late_sums = [_pair_add(x, r, core, name=n + "pair_add_" + k) for k, x, r in zip(RS_LATE, late, theirs)]
    return [dh_a, dh_b], gr, early_quads, late_sums, carried_got


def kernel(x, p, positions, ln_in_g, ln_in_b, w_in, hgrn_lb_logits, hgrn_norm_g, sgu_ln_g, sgu_ln_b, sgu_w_s, sgu_b_s, mla_q_norm_g, mla_w_uq, mla_kv_norm_g, mla_w_ukv, w_out, ln1_g, ln1_b, w_gate_up, w_down, ple_w_gate, ple_w_proj, ln2_g, ln2_b, loss_target, m_ln_in_g, m_ln_in_b, m_w_in, m_hgrn_lb_logits, m_hgrn_norm_g, m_sgu_ln_g, m_sgu_ln_b, m_sgu_w_s, m_sgu_b_s, m_mla_q_norm_g, m_mla_w_uq, m_mla_kv_norm_g, m_mla_w_ukv, m_w_out, m_ln1_g, m_ln1_b, m_w_gate_up, m_w_down, m_ple_w_gate, m_ple_w_proj, m_ln2_g, m_ln2_b, v_ln_in_g, v_ln_in_b, v_w_in, v_hgrn_lb_logits, v_hgrn_norm_g, v_sgu_ln_g, v_sgu_ln_b, v_sgu_w_s, v_sgu_b_s, v_mla_q_norm_g, v_mla_w_uq, v_mla_kv_norm_g, v_mla_w_ukv, v_w_out, v_ln1_g, v_ln1_b, v_w_gate_up, v_w_down, v_ple_w_gate, v_ple_w_proj, v_ln2_g, v_ln2_b):
    args = dict(locals())
    w = {k: args[k] for k in ORDER}
    m = {k: args["m_" + k] for k in ORDER}
    v = {k: args["v_" + k] for k in ORDER}
    depth = w_in.shape[0]
    assert depth == 2, "the lower-bound kernel is written for two layers"
    alpha = (2 * depth) ** 0.25
    xs, tgt = x[0], loss_target[0]
    d_model = xs.shape[1]

    shards = [_weight_shards(w, li) for li in range(depth)]
    on_hgrn0 = ("mla_w_uq", "mla_w_ukv", "w_out", "ple_w_gate", "ple_w_proj")
    ffn0 = ("w_gate_up", "w_down")
    first1 = ("w_in", "mla_w_uq", "mla_w_ukv", "w_out")
    on_attn1 = ("w_gate_up", "w_down", "ple_w_gate", "ple_w_proj")
    layer1_first = {}

    def after_hgrn0(got):
        return _usable_weights(dict(zip(on_hgrn0, got)), name="l0")

    def after_attn0(got):
        layer1_first.update(_usable_weights(dict(zip(first1, got[len(ffn0):])), name="l1"))
        return _usable_weights(dict(zip(ffn0, got[:len(ffn0)])), name="l0")

    def after_attn1(got):
        return _usable_weights(dict(zip(on_attn1, got)), name="l1")

    tables = _rope_tables(positions[0])
    row1 = lambda a: a.reshape(1, -1)
    l0, l1 = row1(hgrn_lb_logits[0]), row1(hgrn_lb_logits[1])
    lbs = _rowwise(_fn_lower_bounds, [l0, l1], [], [(HG_W, F32), (HG_W, F32)], name="lower_bounds")

    gin, bin_ = row1(ln_in_g), row1(ln_in_b)
    (h, hb), g_in = _rowwise(_twice(_fn_ln), [xs], [gin, bin_], [(d_model, F32), (d_model, BF16)], name="ln_in",
                             job=_gather_job([shards[0]["w_in"]]))
    w_in0 = _usable_weights({"w_in": _gather_forward(g_in, name="gather_l0_w_in_forward")[0]}, name="l0")
    (h, hb), sv0 = _layer_forward(
        0, h, hb, p[0, 0], w_in0, w, lbs, tables, alpha,
        hgrn_job=_gather_job([shards[0][k] for k in on_hgrn0]), after_hgrn=after_hgrn0,
        attn_job=_gather_job([shards[0][k] for k in ffn0] + [shards[1][k] for k in first1]), after_attn=after_attn0)
    (dy, loss_local), sv1 = _layer_forward(
        1, h, hb, p[1, 0], layer1_first, w, lbs, tables, alpha,
        attn_job=_gather_job([shards[1][k] for k in on_attn1]), after_attn=after_attn1, loss_target=tgt)
    saved = [sv0, sv1]
    loss = lax.psum(loss_local[0, 0], ("x", "y", "c"))

    core = lax.axis_index("c").astype(jnp.int32).reshape(1)
    dparts, grads, quads, carried = [dy], [None] * depth, [None] * depth, None
    for li in reversed(range(depth)):
        dparts, grads[li], early_quads, late_sums, late_quads = _layer_backward(
            li, dparts, p[li, 0], saved[li], lbs, tables, alpha, core, carried=carried)
        quads[li] = dict(zip(RS_EARLY, early_quads))
        if carried is not None:
            quads[li + 1].update(zip(RS_LATE, late_quads))
        carried = _quad_job(late_sums)
    (dx, d_gin, d_bin), late_quads = _rowwise_vjp(_fn_ln, [xs], [gin, bin_], [dparts], groups=[[0]], name="ln_in_b",
                                                   job=carried)
    quads[0].update(zip(RS_LATE, late_quads))
    dl0, dl1 = _rowwise_vjp(_fn_lower_bounds, [l0, l1], [], [[grads[0]["lower_bound"]], [grads[1]["lower_bound"]]],
                            groups=[[0], [1]], name="lower_bounds_b")

    prefixes = ("grad_", "delta_", "new_m_", "new_v_")
    uq_pad = ((0, 0), (0, 0), (0, LANES - ATT_D))
    state = {k: ((jnp.pad(w[k], uq_pad), jnp.pad(m[k], uq_pad), jnp.pad(v[k], uq_pad)) if k == "mla_w_uq"
                 else (w[k], m[k], v[k])) for k in BIG}
    out = {}
    for k in BIG:
        res4 = None
        for li in range(depth):
            res4 = _adamw(quads[li][k], *state[k], li, name=f"adamw_l{li}_{k}", into=res4)
        for pre, a in zip(prefixes, res4):
            out[pre + k] = a[:, :, :ATT_D] if k == "mla_w_uq" else a

    small_g = {"ln_in_g": d_gin.reshape(-1), "ln_in_b": d_bin.reshape(-1),
               "hgrn_lb_logits": jnp.stack([dl0.reshape(-1), dl1.reshape(-1)])}
    for k in SMALL[3:]:
        small_g[k] = jnp.stack([grads[li][k].reshape(w[k].shape[1:]) for li in range(depth)])
    (small_parts,) = _all_gather([_pack([small_g[k] for k in SMALL])], name="gather_small_grads", columns=False)
    slabs = _adamw(small_parts, _pack([w[k] for k in SMALL]), _pack([m[k] for k in SMALL]),
                   _pack([v[k] for k in SMALL]), None, name="adamw_small")
    shapes = [w[k].shape for k in SMALL]
    for pre, slab in zip(prefixes, slabs):
        for k, a in zip(SMALL, _unpack(slab, shapes)):
            out[pre + k] = a
    res = [loss, dx[None]]
    for prefix in ("grad_", "delta_", "new_m_", "new_v_"):
        res += [out[prefix + k] for k in ORDER]
    return tuple(res)
```
